```python
import jax, jax.numpy as jnp
from jax import lax
import numpy as np

D_MODEL = 1024
BATCH = 16
SEQ = 2048
DEPTH = 1

N_META = 16
D_MIX = D_MODEL
D_CONV = D_MIX // 2
CONV_WIDTH = 31
GLA_HEADS = 4
GLA_DV = (D_MIX - D_CONV) // GLA_HEADS
GLA_DK = GLA_DV // 2
GLA_GATE_RANK = 16
GLA_TAU = 16.0
CHUNK = 64
D_FF = 4 * D_MODEL
LN_EPS = 1e-5
DEEPNORM_ALPHA = (2.0 * DEPTH) ** 0.25
DEEPNORM_BETA = (8.0 * DEPTH) ** -0.25

SPLIT_SIZES = (D_CONV, D_CONV,
               GLA_HEADS * GLA_DK, GLA_HEADS * GLA_DK,
               GLA_HEADS * GLA_DV, GLA_HEADS * GLA_DV,
               GLA_GATE_RANK)
D_IN = sum(SPLIT_SIZES)
SPLIT_IDX = tuple(int(i) for i in np.cumsum(SPLIT_SIZES)[:-1])

kernel_name = "hymba_conformer_gla_deepnorm"


def layer_norm(x, g, b):
    xf = x.astype(jnp.float32)
    mu = jnp.mean(xf, axis=-1, keepdims=True)
    var = jnp.mean(jnp.square(xf - mu), axis=-1, keepdims=True)
    y = (xf - mu) * lax.rsqrt(var + LN_EPS)
    return (y * g.astype(jnp.float32) + b.astype(jnp.float32)).astype(x.dtype)


def rms_norm(x, g):
    xf = x.astype(jnp.float32)
    y = xf * lax.rsqrt(jnp.mean(jnp.square(xf), axis=-1, keepdims=True) + LN_EPS)
    return y * g.astype(jnp.float32)


def conformer_conv(a, gate, conv_w, conv_b, ln_g, ln_b):
    h = a * jax.nn.sigmoid(gate)
    h = lax.conv_general_dilated(
        h, conv_w[:, None, :].astype(h.dtype),
        window_strides=(1,), padding=[(CONV_WIDTH - 1, 0)],
        dimension_numbers=("NWC", "WIO", "NWC"),
        feature_group_count=D_CONV) + conv_b
    return jax.nn.silu(layer_norm(h, ln_g, ln_b))


def gla_chunked(q, k, v, log_g):
    B, T = q.shape[0], q.shape[1]
    pad = (-N_META) % CHUNK
    padw = ((0, 0), (pad, 0), (0, 0), (0, 0))
    q, k, v, log_g = [jnp.pad(t.astype(jnp.float32), padw) for t in (q, k, v, log_g)]
    L = T + pad
    N = L // CHUNK

    def to_chunks(t):
        return t.reshape(B, N, CHUNK, GLA_HEADS, t.shape[-1]).transpose(0, 3, 1, 2, 4)

    q, k, v, log_g = map(to_chunks, (q, k, v, log_g))
    q = q * (GLA_DK ** -0.5)
    b = jnp.cumsum(log_g, axis=3)
    b_last = b[:, :, :, -1:, :]
    qe = q * jnp.exp(b)
    ke = k * jnp.exp(-b)
    kd = k * jnp.exp(b_last - b)

    mask = jnp.tril(jnp.ones((CHUNK, CHUNK), dtype=bool))
    A = jnp.einsum("bhncd,bhnsd->bhncs", qe, ke)
    A = jnp.where(mask, A, 0.0)
    o_intra = jnp.einsum("bhncs,bhnse->bhnce", A, v)

    dS = jnp.einsum("bhncd,bhnce->bhnde", kd, v)
    decay = jnp.exp(b_last[:, :, :, 0, :])

    def step(S, xs):
        dec, upd = xs
        return dec[..., None] * S + upd, S

    S0 = jnp.zeros((B, GLA_HEADS, GLA_DK, GLA_DV), jnp.float32)
    _, S_before = lax.scan(step, S0, (jnp.moveaxis(decay, 2, 0), jnp.moveaxis(dS, 2, 0)))
    S_before = jnp.moveaxis(S_before, 0, 2)
    o_inter = jnp.einsum("bhncd,bhnde->bhnce", qe, S_before)

    o = (o_intra + o_inter).transpose(0, 2, 3, 1, 4).reshape(B, L, GLA_HEADS, GLA_DV)
    return o[:, pad:]


def _fwd_setup_inputs(seed: int = 0) -> dict:
    key = jax.random.key(seed)
    ks = jax.random.split(key, 20)
    f32 = jnp.float32

    def nrm(k, shape, scale):
        return jax.random.normal(k, shape, f32) * scale

    def gain(k, shape):
        return 1.0 + 0.02 * jax.random.normal(k, shape, f32)

    return {
        "x": jax.random.normal(ks[0], (BATCH, SEQ, D_MODEL), f32),
        "meta_tokens": nrm(ks[1], (N_META, D_MODEL), 1.0),
        "ln_in_g": gain(ks[2], (D_MODEL,)),
        "ln_in_b": nrm(ks[3], (D_MODEL,), 0.02),
        "w_in": nrm(ks[4], (DEPTH, D_MODEL, D_IN), D_MODEL ** -0.5),
        "conv_w": nrm(ks[5], (DEPTH, CONV_WIDTH, D_CONV), CONV_WIDTH ** -0.5),
        "conv_b": nrm(ks[6], (DEPTH, D_CONV), 0.02),
        "conv_ln_g": gain(ks[7], (DEPTH, D_CONV)),
        "conv_ln_b": nrm(ks[8], (DEPTH, D_CONV), 0.02),
        "gate_up": nrm(ks[9], (DEPTH, GLA_GATE_RANK, GLA_HEADS * GLA_DK), GLA_GATE_RANK ** -0.5),
        "gate_bias": nrm(ks[10], (DEPTH, GLA_HEADS * GLA_DK), 0.02),
        "gla_norm_g": gain(ks[11], (DEPTH, GLA_DV)),
        "w_out": nrm(ks[12], (DEPTH, D_MIX, D_MODEL), DEEPNORM_BETA * D_MIX ** -0.5),
        "ln1_g": gain(ks[13], (DEPTH, D_MODEL)),
        "ln1_b": nrm(ks[14], (DEPTH, D_MODEL), 0.02),
        "w_ff1": nrm(ks[15], (DEPTH, D_MODEL, D_FF), D_MODEL ** -0.5),
        "w_ff2": nrm(ks[16], (DEPTH, D_FF, D_MODEL), DEEPNORM_BETA * D_FF ** -0.5),
        "ln2_g": gain(ks[17], (DEPTH, D_MODEL)),
        "ln2_b": nrm(ks[18], (DEPTH, D_MODEL), 0.02),
    }


def _fwd_reference(x, meta_tokens, ln_in_g, ln_in_b, w_in, conv_w, conv_b, conv_ln_g, conv_ln_b,
              gate_up, gate_bias, gla_norm_g, w_out, ln1_g, ln1_b, w_ff1, w_ff2, ln2_g, ln2_b):
    B = x.shape[0]
    meta = jnp.broadcast_to(meta_tokens.astype(x.dtype)[None], (B, N_META, D_MODEL))
    s = jnp.concatenate([meta, x], axis=1)
    s = layer_norm(s, ln_in_g, ln_in_b)
    T = s.shape[1]

    for l in range(DEPTH):
        u = s @ w_in[l]
        c_val, c_gate, q, k, v, r, g_down = jnp.split(u, SPLIT_IDX, axis=-1)

        conv_out = conformer_conv(c_val, c_gate, conv_w[l], conv_b[l], conv_ln_g[l], conv_ln_b[l])

        z = (g_down @ gate_up[l] + gate_bias[l]).astype(jnp.float32)
        log_g = jax.nn.log_sigmoid(z) / GLA_TAU
        hd = lambda t, d: t.reshape(B, T, GLA_HEADS, d)
        o = gla_chunked(hd(q, GLA_DK), hd(k, GLA_DK), hd(v, GLA_DV), hd(log_g, GLA_DK))
        o = rms_norm(o, gla_norm_g[l]) * jax.nn.silu(hd(r, GLA_DV).astype(jnp.float32))
        gla_out = o.reshape(B, T, GLA_HEADS * GLA_DV).astype(s.dtype)

        mix = jnp.concatenate([conv_out, gla_out], axis=-1) @ w_out[l]
        s = layer_norm(DEEPNORM_ALPHA * s + mix, ln1_g[l], ln1_b[l])

        f = jnp.square(jax.nn.relu(s @ w_ff1[l])) @ w_ff2[l]
        s = layer_norm(DEEPNORM_ALPHA * s + f, ln2_g[l], ln2_b[l])

    return s[:, N_META:]


import jax as _jax
import jax.numpy as _jnp

TWIN_FORMAT = 'train_step'
FWD_PARAMS = ['x', 'meta_tokens', 'ln_in_g', 'ln_in_b', 'w_in', 'conv_w', 'conv_b', 'conv_ln_g', 'conv_ln_b', 'gate_up', 'gate_bias', 'gla_norm_g', 'w_out', 'ln1_g', 'ln1_b', 'w_ff1', 'w_ff2', 'ln2_g', 'ln2_b']
TWIN_WEIGHTS = ['meta_tokens', 'ln_in_g', 'ln_in_b', 'w_in', 'conv_w', 'conv_b', 'conv_ln_g', 'conv_ln_b', 'gate_up', 'gate_bias', 'gla_norm_g', 'w_out', 'ln1_g', 'ln1_b', 'w_ff1', 'w_ff2', 'ln2_g', 'ln2_b']
TWIN_DIFF_INPUT = 'x'
TWIN_INPUTS = ['x', 'meta_tokens', 'ln_in_g', 'ln_in_b', 'w_in', 'conv_w', 'conv_b', 'conv_ln_g', 'conv_ln_b', 'gate_up', 'gate_bias', 'gla_norm_g', 'w_out', 'ln1_g', 'ln1_b', 'w_ff1', 'w_ff2', 'ln2_g', 'ln2_b', 'loss_target', 'm_meta_tokens', 'm_ln_in_g', 'm_ln_in_b', 'm_w_in', 'm_conv_w', 'm_conv_b', 'm_conv_ln_g', 'm_conv_ln_b', 'm_gate_up', 'm_gate_bias', 'm_gla_norm_g', 'm_w_out', 'm_ln1_g', 'm_ln1_b', 'm_w_ff1', 'm_w_ff2', 'm_ln2_g', 'm_ln2_b', 'v_meta_tokens', 'v_ln_in_g', 'v_ln_in_b', 'v_w_in', 'v_conv_w', 'v_conv_b', 'v_conv_ln_g', 'v_conv_ln_b', 'v_gate_up', 'v_gate_bias', 'v_gla_norm_g', 'v_w_out', 'v_ln1_g', 'v_ln1_b', 'v_w_ff1', 'v_w_ff2', 'v_ln2_g', 'v_ln2_b']
TWIN_OUTPUTS = ['loss', 'grad_x', 'grad_meta_tokens', 'grad_ln_in_g', 'grad_ln_in_b', 'grad_w_in', 'grad_conv_w', 'grad_conv_b', 'grad_conv_ln_g', 'grad_conv_ln_b', 'grad_gate_up', 'grad_gate_bias', 'grad_gla_norm_g', 'grad_w_out', 'grad_ln1_g', 'grad_ln1_b', 'grad_w_ff1', 'grad_w_ff2', 'grad_ln2_g', 'grad_ln2_b', 'delta_meta_tokens', 'delta_ln_in_g', 'delta_ln_in_b', 'delta_w_in', 'delta_conv_w', 'delta_conv_b', 'delta_conv_ln_g', 'delta_conv_ln_b', 'delta_gate_up', 'delta_gate_bias', 'delta_gla_norm_g', 'delta_w_out', 'delta_ln1_g', 'delta_ln1_b', 'delta_w_ff1', 'delta_w_ff2', 'delta_ln2_g', 'delta_ln2_b', 'new_m_meta_tokens', 'new_m_ln_in_g', 'new_m_ln_in_b', 'new_m_w_in', 'new_m_conv_w', 'new_m_conv_b', 'new_m_conv_ln_g', 'new_m_conv_ln_b', 'new_m_gate_up', 'new_m_gate_bias', 'new_m_gla_norm_g', 'new_m_w_out', 'new_m_ln1_g', 'new_m_ln1_b', 'new_m_w_ff1', 'new_m_w_ff2', 'new_m_ln2_g', 'new_m_ln2_b', 'new_v_meta_tokens', 'new_v_ln_in_g', 'new_v_ln_in_b', 'new_v_w_in', 'new_v_conv_w', 'new_v_conv_b', 'new_v_conv_ln_g', 'new_v_conv_ln_b', 'new_v_gate_up', 'new_v_gate_bias', 'new_v_gla_norm_g', 'new_v_w_out', 'new_v_ln1_g', 'new_v_ln1_b', 'new_v_w_ff1', 'new_v_w_ff2', 'new_v_ln2_g', 'new_v_ln2_b']
TWIN_LEAF_KINDS = {'loss': 'loss', 'grad_x': 'grad_x', 'grad_meta_tokens': 'grad_w', 'grad_ln_in_g': 'grad_w', 'grad_ln_in_b': 'grad_w', 'grad_w_in': 'grad_w', 'grad_conv_w': 'grad_w', 'grad_conv_b': 'grad_w', 'grad_conv_ln_g': 'grad_w', 'grad_conv_ln_b': 'grad_w', 'grad_gate_up': 'grad_w', 'grad_gate_bias': 'grad_w', 'grad_gla_norm_g': 'grad_w', 'grad_w_out': 'grad_w', 'grad_ln1_g': 'grad_w', 'grad_ln1_b': 'grad_w', 'grad_w_ff1': 'grad_w', 'grad_w_ff2': 'grad_w', 'grad_ln2_g': 'grad_w', 'grad_ln2_b': 'grad_w', 'delta_meta_tokens': 'delta_w', 'delta_ln_in_g': 'delta_w', 'delta_ln_in_b': 'delta_w', 'delta_w_in': 'delta_w', 'delta_conv_w': 'delta_w', 'delta_conv_b': 'delta_w', 'delta_conv_ln_g': 'delta_w', 'delta_conv_ln_b': 'delta_w', 'delta_gate_up': 'delta_w', 'delta_gate_bias': 'delta_w', 'delta_gla_norm_g': 'delta_w', 'delta_w_out': 'delta_w', 'delta_ln1_g': 'delta_w', 'delta_ln1_b': 'delta_w', 'delta_w_ff1': 'delta_w', 'delta_w_ff2': 'delta_w', 'delta_ln2_g': 'delta_w', 'delta_ln2_b': 'delta_w', 'new_m_meta_tokens': 'new_m', 'new_m_ln_in_g': 'new_m', 'new_m_ln_in_b': 'new_m', 'new_m_w_in': 'new_m', 'new_m_conv_w': 'new_m', 'new_m_conv_b': 'new_m', 'new_m_conv_ln_g': 'new_m', 'new_m_conv_ln_b': 'new_m', 'new_m_gate_up': 'new_m', 'new_m_gate_bias': 'new_m', 'new_m_gla_norm_g': 'new_m', 'new_m_w_out': 'new_m', 'new_m_ln1_g': 'new_m', 'new_m_ln1_b': 'new_m', 'new_m_w_ff1': 'new_m', 'new_m_w_ff2': 'new_m', 'new_m_ln2_g': 'new_m', 'new_m_ln2_b': 'new_m', 'new_v_meta_tokens': 'new_v', 'new_v_ln_in_g': 'new_v', 'new_v_ln_in_b': 'new_v', 'new_v_w_in': 'new_v', 'new_v_conv_w': 'new_v', 'new_v_conv_b': 'new_v', 'new_v_conv_ln_g': 'new_v', 'new_v_conv_ln_b': 'new_v', 'new_v_gate_up': 'new_v', 'new_v_gate_bias': 'new_v', 'new_v_gla_norm_g': 'new_v', 'new_v_w_out': 'new_v', 'new_v_ln1_g': 'new_v', 'new_v_ln1_b': 'new_v', 'new_v_w_ff1': 'new_v', 'new_v_w_ff2': 'new_v', 'new_v_ln2_g': 'new_v', 'new_v_ln2_b': 'new_v'}


def _forward(args):
    return _fwd_reference(*[args[k] for k in FWD_PARAMS])


def _output_shape():
    out = _jax.eval_shape(lambda: _forward(_fwd_setup_inputs(0)))
    return out.shape, out.dtype

N_MICROBATCH = 1
ADAM_LR = 0.001
ADAM_B1 = 0.9
ADAM_B2 = 0.999
ADAM_EPS = 1e-08
ADAM_WD = 0.01
ADAM_STEP = 10
PER_EXAMPLE_BATCH_AXIS = {'x': 0, 'loss_target': 0}
SHARED_INPUTS = []
_WEIGHT_DTYPES = {'meta_tokens': _jnp.float32, 'ln_in_g': _jnp.float32, 'ln_in_b': _jnp.float32, 'w_in': _jnp.float32, 'conv_w': _jnp.float32, 'conv_b': _jnp.float32, 'conv_ln_g': _jnp.float32, 'conv_ln_b': _jnp.float32, 'gate_up': _jnp.float32, 'gate_bias': _jnp.float32, 'gla_norm_g': _jnp.float32, 'w_out': _jnp.float32, 'ln1_g': _jnp.float32, 'ln1_b': _jnp.float32, 'w_ff1': _jnp.float32, 'w_ff2': _jnp.float32, 'ln2_g': _jnp.float32, 'ln2_b': _jnp.float32}
MOMENT_SCALE = {'meta_tokens': 3.527671e-03, 'ln_in_g': 6.340512e-01, 'ln_in_b': 4.326502e-01, 'w_in': 5.509862e-02, 'conv_w': 5.467505e-02, 'conv_b': 2.150425e-01, 'conv_ln_g': 8.590343e-02, 'conv_ln_b': 1.171183e-01, 'gate_up': 8.214866e-03, 'gate_bias': 3.878043e-02, 'gla_norm_g': 1.105911e-01, 'w_out': 9.683120e-02, 'ln1_g': 6.953166e-01, 'ln1_b': 4.524544e-01, 'w_ff1': 5.363777e-02, 'w_ff2': 2.091304e-01, 'ln2_g': 3.205855e+01, 'ln2_b': 7.172733e+00}


def _to_microbatches(a, axis):
    t = _jnp.moveaxis(a, axis, 0)
    t = t.reshape((N_MICROBATCH, t.shape[0] // N_MICROBATCH) + t.shape[1:])
    return _jnp.moveaxis(t, 1, axis + 1)


def setup_inputs(seed: int = 0) -> dict:
    inp = _fwd_setup_inputs(seed)
    key = _jax.random.fold_in(_jax.random.key(seed), 7919)
    shape, _ = _output_shape()
    out = dict(inp)
    out["loss_target"] = _jax.random.normal(_jax.random.fold_in(key, 0), shape, _jnp.float32)
    for i, name in enumerate(TWIN_WEIGHTS):
        w = inp[name].astype(_jnp.float32)
        if MOMENT_SCALE is None:
            s = _jnp.sqrt(_jnp.mean(_jnp.square(w)) + 1e-30)
        else:
            s = MOMENT_SCALE[name]
        km, kv = _jax.random.split(_jax.random.fold_in(key, i + 1))
        out[name] = w
        out["m_" + name] = s * _jax.random.normal(km, w.shape, _jnp.float32)
        out["v_" + name] = (s * s) * _jax.random.uniform(kv, w.shape, _jnp.float32, 0.5, 1.5)
    if N_MICROBATCH > 1:
        for name, axis in PER_EXAMPLE_BATCH_AXIS.items():
            out[name] = _to_microbatches(out[name], axis)
    return {'x': out['x'], 'meta_tokens': out['meta_tokens'], 'ln_in_g': out['ln_in_g'], 'ln_in_b': out['ln_in_b'], 'w_in': out['w_in'], 'conv_w': out['conv_w'], 'conv_b': out['conv_b'], 'conv_ln_g': out['conv_ln_g'], 'conv_ln_b': out['conv_ln_b'], 'gate_up': out['gate_up'], 'gate_bias': out['gate_bias'], 'gla_norm_g': out['gla_norm_g'], 'w_out': out['w_out'], 'ln1_g': out['ln1_g'], 'ln1_b': out['ln1_b'], 'w_ff1': out['w_ff1'], 'w_ff2': out['w_ff2'], 'ln2_g': out['ln2_g'], 'ln2_b': out['ln2_b'], 'loss_target': out['loss_target'], 'm_meta_tokens': out['m_meta_tokens'], 'm_ln_in_g': out['m_ln_in_g'], 'm_ln_in_b': out['m_ln_in_b'], 'm_w_in': out['m_w_in'], 'm_conv_w': out['m_conv_w'], 'm_conv_b': out['m_conv_b'], 'm_conv_ln_g': out['m_conv_ln_g'], 'm_conv_ln_b': out['m_conv_ln_b'], 'm_gate_up': out['m_gate_up'], 'm_gate_bias': out['m_gate_bias'], 'm_gla_norm_g': out['m_gla_norm_g'], 'm_w_out': out['m_w_out'], 'm_ln1_g': out['m_ln1_g'], 'm_ln1_b': out['m_ln1_b'], 'm_w_ff1': out['m_w_ff1'], 'm_w_ff2': out['m_w_ff2'], 'm_ln2_g': out['m_ln2_g'], 'm_ln2_b': out['m_ln2_b'], 'v_meta_tokens': out['v_meta_tokens'], 'v_ln_in_g': out['v_ln_in_g'], 'v_ln_in_b': out['v_ln_in_b'], 'v_w_in': out['v_w_in'], 'v_conv_w': out['v_conv_w'], 'v_conv_b': out['v_conv_b'], 'v_conv_ln_g': out['v_conv_ln_g'], 'v_conv_ln_b': out['v_conv_ln_b'], 'v_gate_up': out['v_gate_up'], 'v_gate_bias': out['v_gate_bias'], 'v_gla_norm_g': out['v_gla_norm_g'], 'v_w_out': out['v_w_out'], 'v_ln1_g': out['v_ln1_g'], 'v_ln1_b': out['v_ln1_b'], 'v_w_ff1': out['v_w_ff1'], 'v_w_ff2': out['v_w_ff2'], 'v_ln2_g': out['v_ln2_g'], 'v_ln2_b': out['v_ln2_b']}


def _loss(weights, diff, rest, loss_target):
    with _jax.named_scope("forward"):
        args = {**rest, TWIN_DIFF_INPUT: diff, **{k: w.astype(_WEIGHT_DTYPES[k]) for k, w in weights.items()}}
        y = _forward(args)
    with _jax.named_scope("loss_head"):
        err = _jnp.square(y.astype(_jnp.float32) - loss_target)
        return 0.5 * _jnp.sum(_jnp.mean(err, axis=-1)) if err.ndim else 0.5 * err


def _adamw(w, g, m, v):
    m = ADAM_B1 * m + (1.0 - ADAM_B1) * g
    v = ADAM_B2 * v + (1.0 - ADAM_B2) * _jnp.square(g)
    m_hat = m / (1.0 - ADAM_B1 ** ADAM_STEP)
    v_hat = v / (1.0 - ADAM_B2 ** ADAM_STEP)
    delta = -ADAM_LR * (m_hat / (_jnp.sqrt(v_hat) + ADAM_EPS) + ADAM_WD * w)
    return delta, m, v


def reference(x, meta_tokens, ln_in_g, ln_in_b, w_in, conv_w, conv_b, conv_ln_g, conv_ln_b, gate_up, gate_bias, gla_norm_g, w_out, ln1_g, ln1_b, w_ff1, w_ff2, ln2_g, ln2_b, loss_target, m_meta_tokens, m_ln_in_g, m_ln_in_b, m_w_in, m_conv_w, m_conv_b, m_conv_ln_g, m_conv_ln_b, m_gate_up, m_gate_bias, m_gla_norm_g, m_w_out, m_ln1_g, m_ln1_b, m_w_ff1, m_w_ff2, m_ln2_g, m_ln2_b, v_meta_tokens, v_ln_in_g, v_ln_in_b, v_w_in, v_conv_w, v_conv_b, v_conv_ln_g, v_conv_ln_b, v_gate_up, v_gate_bias, v_gla_norm_g, v_w_out, v_ln1_g, v_ln1_b, v_w_ff1, v_w_ff2, v_ln2_g, v_ln2_b):
    given = dict(x=x, meta_tokens=meta_tokens, ln_in_g=ln_in_g, ln_in_b=ln_in_b, w_in=w_in, conv_w=conv_w, conv_b=conv_b, conv_ln_g=conv_ln_g, conv_ln_b=conv_ln_b, gate_up=gate_up, gate_bias=gate_bias, gla_norm_g=gla_norm_g, w_out=w_out, ln1_g=ln1_g, ln1_b=ln1_b, w_ff1=w_ff1, w_ff2=w_ff2, ln2_g=ln2_g, ln2_b=ln2_b, loss_target=loss_target, m_meta_tokens=m_meta_tokens, m_ln_in_g=m_ln_in_g, m_ln_in_b=m_ln_in_b, m_w_in=m_w_in, m_conv_w=m_conv_w, m_conv_b=m_conv_b, m_conv_ln_g=m_conv_ln_g, m_conv_ln_b=m_conv_ln_b, m_gate_up=m_gate_up, m_gate_bias=m_gate_bias, m_gla_norm_g=m_gla_norm_g, m_w_out=m_w_out, m_ln1_g=m_ln1_g, m_ln1_b=m_ln1_b, m_w_ff1=m_w_ff1, m_w_ff2=m_w_ff2, m_ln2_g=m_ln2_g, m_ln2_b=m_ln2_b, v_meta_tokens=v_meta_tokens, v_ln_in_g=v_ln_in_g, v_ln_in_b=v_ln_in_b, v_w_in=v_w_in, v_conv_w=v_conv_w, v_conv_b=v_conv_b, v_conv_ln_g=v_conv_ln_g, v_conv_ln_b=v_conv_ln_b, v_gate_up=v_gate_up, v_gate_bias=v_gate_bias, v_gla_norm_g=v_gla_norm_g, v_w_out=v_w_out, v_ln1_g=v_ln1_g, v_ln1_b=v_ln1_b, v_w_ff1=v_w_ff1, v_w_ff2=v_w_ff2, v_ln2_g=v_ln2_g, v_ln2_b=v_ln2_b)
    weights = {n: given[n] for n in TWIN_WEIGHTS}
    shared = {n: given[n] for n in SHARED_INPUTS}
    per_example = {n: given[n] for n in ['x']}
    grad_fn = _jax.value_and_grad(_loss, argnums=(0, 1))

    def one_microbatch(ex, loss_target):
        ex = dict(ex)
        diff = ex.pop(TWIN_DIFF_INPUT)
        return grad_fn(weights, diff, {**shared, **ex}, loss_target)

    if N_MICROBATCH == 1:
        loss, (grad_w, grad_x) = one_microbatch(per_example, given["loss_target"])
    else:
        def body(carry, xs):
            loss_sum, grad_sum = carry
            l_k, (gw_k, gx_k) = one_microbatch(xs[0], xs[1])
            with _jax.named_scope("update"):
                return (loss_sum + l_k, _jax.tree.map(_jnp.add, grad_sum, gw_k)), gx_k

        init = (_jnp.zeros((), _jnp.float32), _jax.tree.map(_jnp.zeros_like, weights))
        (loss, grad_w), grad_x = _jax.lax.scan(body, init, (per_example, given["loss_target"]))
    with _jax.named_scope("update"):
        delta_w, new_m, new_v = {}, {}, {}
        for n in TWIN_WEIGHTS:
            delta_w[n], new_m[n], new_v[n] = _adamw(weights[n], grad_w[n], given["m_" + n], given["v_" + n])
    return (loss, grad_x, *[grad_w[n] for n in TWIN_WEIGHTS], *[delta_w[n] for n in TWIN_WEIGHTS],
            *[new_m[n] for n in TWIN_WEIGHTS], *[new_v[n] for n in TWIN_WEIGHTS])
```

```python
import functools

import jax
import jax.numpy as jnp
from jax import lax
from jax.experimental import pallas as pl
from jax.experimental.pallas import tpu as pltpu

F32 = jnp.float32
BF16 = jnp.bfloat16

D_MODEL = 1024
N_META = 16
D_CONV = 512
CONV_WIDTH = 31
GLA_HEADS = 4
GLA_DV = 128
GLA_DK = 64
GLA_RANK = 16
GLA_TAU = 16.0
CHUNK = 64
D_FF = 4096
LN_EPS = 1e-5
ALPHA = 2.0 ** 0.25
D_IN = 2576
D_IN_PAD = 2688
PAD = CHUNK - N_META
HEAD = PAD + N_META
Q_SCALE = GLA_DK ** -0.5
ADAM_LR, ADAM_B1, ADAM_B2, ADAM_EPS, ADAM_WD, ADAM_STEP = 0.001, 0.9, 0.999, 1e-08, 0.01, 10
HALF = D_MODEL // 2
VMEM_LIMIT = 56 * 1024 * 1024
MESH = pl.DeviceIdType.MESH

C_VAL, C_GATE, C_Q, C_K, C_V, C_R, C_GD = 0, 512, 1024, 1280, 1536, 2048, 2560

_SMALL_FIELDS = (("loss", 8), ("ln_in_g", 8), ("ln_in_b", 8), ("conv_b", 4), ("conv_ln_g", 4), ("conv_ln_b", 4),
                 ("gate_bias", 2), ("gla_norm_g", 1), ("ln1_g", 8), ("ln1_b", 8), ("ln2_g", 8), ("ln2_b", 8),
                 ("conv_w", 124), ("gate_up", 32), ("meta_tokens", 128))
SMALL_ROWS = 360


def _params(sem=None, **kw):
    return pltpu.CompilerParams(dimension_semantics=sem, vmem_limit_bytes=VMEM_LIMIT, **kw)


def _row_tile(tp):
    for t in (704, 352, 192, 64):
        if tp % t == 0:
            return t
    raise ValueError(tp)


def _dot(a, b, dims, precision=None):
    return lax.dot_general(a, b, (dims, ((), ())), preferred_element_type=F32, precision=precision)


def _nn(a, b, **kw):
    return _dot(a, b, ((1,), (0,)), **kw)


def _nt(a, b, **kw):
    return _dot(a, b, ((1,), (1,)), **kw)


def _tn(a, b, **kw):
    return _dot(a, b, ((0,), (0,)), **kw)


def _sigmoid(x):
    return 1.0 / (1.0 + jnp.exp(-x))


def _log_sigmoid(z):
    return jnp.minimum(z, 0.0) - jnp.log(1.0 + jnp.exp(-jnp.abs(z)))


def _ln_stats(t):
    mu = jnp.mean(t, axis=-1, keepdims=True)
    d = t - mu
    var = jnp.mean(d * d, axis=-1, keepdims=True)
    rstd = lax.rsqrt(var + LN_EPS)
    return d * rstd, rstd


def _ln_bwd(dxhat, xhat, rstd):
    m1 = jnp.mean(dxhat, axis=-1, keepdims=True)
    m2 = jnp.mean(dxhat * xhat, axis=-1, keepdims=True)
    return rstd * (dxhat - m1 - xhat * m2)


def _mesh_pos():
    return lax.axis_index("x"), lax.axis_index("y"), lax.axis_index("c")


ANY = pl.BlockSpec(memory_space=pl.ANY)


def _gather_weights(shards, small):
    n = len(shards)

    def body(*refs):
        ins, small_in = refs[:n], refs[n]
        outs, small_out = refs[n + 1:2 * n + 1], refs[2 * n + 1]
        send, recv, fsend, frecv, ssend, srecv, local = refs[2 * n + 2:]
        x, y, c = _mesh_pos()
        me = 2 * x + y
        sibling = (x, y, 1 - c)
        chips = [(1 - x, y), (x, 1 - y), (1 - x, 1 - y)]
        chip_idx = [2 * px + py for px, py in chips]
        mine = pl.ds(pl.multiple_of(c * HALF, 128), HALF)
        other = pl.ds(pl.multiple_of((1 - c) * HALF, 128), HALF)

        locals_ = [pltpu.make_async_copy(ins[a], outs[a].at[me], local.at[a]) for a in range(n)]
        locals_.append(pltpu.make_async_copy(small_in, small_out.at[me], local.at[n]))
        for cp in locals_:
            cp.start()

        def ici(a, k):
            return pltpu.make_async_remote_copy(
                src_ref=ins[a].at[:, mine], dst_ref=outs[a].at[me, :, mine],
                send_sem=send.at[a, k], recv_sem=recv.at[a, k], device_id=(*chips[k], c), device_id_type=MESH)

        def landed(a, k):
            return pltpu.make_async_remote_copy(
                src_ref=ins[a].at[:, mine], dst_ref=outs[a].at[chip_idx[k], :, mine],
                send_sem=send.at[a, k], recv_sem=recv.at[a, k], device_id=(*chips[k], c), device_id_type=MESH)

        def forward(a, k, cols):
            blk = outs[a].at[chip_idx[k], :, cols]
            return pltpu.make_async_remote_copy(
                src_ref=blk, dst_ref=blk, send_sem=fsend.at[a, k], recv_sem=frecv.at[a, k],
                device_id=sibling, device_id_type=MESH)

        def small_copy(k, slot):
            return pltpu.make_async_remote_copy(
                src_ref=small_in, dst_ref=small_out.at[slot], send_sem=ssend.at[k], recv_sem=srecv.at[k],
                device_id=(*chips[k], c), device_id_type=MESH)

        for a in range(n):
            for k in range(3):
                ici(a, k).start()
        for k in range(3):
            small_copy(k, me).start()
        for a in range(n):
            for k in range(3):
                landed(a, k).wait_recv()
                forward(a, k, mine).start()
        for a in range(n):
            for k in range(3):
                forward(a, k, other).wait_recv()
        for k in range(3):
            small_copy(k, chip_idx[k]).wait_recv()
        for a in range(n):
            for k in range(3):
                ici(a, k).wait_send()
                forward(a, k, mine).wait_send()
        for k in range(3):
            small_copy(k, me).wait_send()
        for cp in locals_:
            cp.wait()

    out_shape = [jax.ShapeDtypeStruct((4,) + s.shape, s.dtype) for s in shards]
    out_shape.append(jax.ShapeDtypeStruct((4,) + small.shape, small.dtype))
    res = pl.pallas_call(
        body, name="gather_weights", out_shape=out_shape,
        in_specs=[ANY] * (n + 1), out_specs=[ANY] * (n + 1),
        scratch_shapes=[pltpu.SemaphoreType.DMA((n, 3)), pltpu.SemaphoreType.DMA((n, 3)),
                        pltpu.SemaphoreType.DMA((n, 3)), pltpu.SemaphoreType.DMA((n, 3)),
                        pltpu.SemaphoreType.DMA((3,)), pltpu.SemaphoreType.DMA((3,)),
                        pltpu.SemaphoreType.DMA((n + 1,))],
    )(*shards, small)
    return res[:n], res[n]


def _pair_exchange(grads):
    n = len(grads)

    def body(*refs):
        ins, outs = refs[:n], refs[n:2 * n]
        send, recv = refs[2 * n:]
        x, y, c = _mesh_pos()
        other = pl.ds(pl.multiple_of((1 - c) * HALF, 128), HALF)
        cps = [pltpu.make_async_remote_copy(
            src_ref=ins[a].at[:, :, other], dst_ref=outs[a], send_sem=send.at[a], recv_sem=recv.at[a],
            device_id=(x, y, 1 - c), device_id_type=MESH) for a in range(n)]
        for cp in cps:
            cp.start()
        for cp in cps:
            cp.wait()

    return pl.pallas_call(
        body, name="grad_pair_exchange",
        out_shape=[jax.ShapeDtypeStruct(g.shape[:2] + (HALF,), g.dtype) for g in grads],
        in_specs=[ANY] * n, out_specs=[ANY] * n,
        scratch_shapes=[pltpu.SemaphoreType.DMA((n,)), pltpu.SemaphoreType.DMA((n,))],
    )(*grads)


def _chip_exchange(parts):
    n = len(parts)

    def body(*refs):
        ins, outs = refs[:n], refs[n:2 * n]
        send, recv = refs[2 * n:]
        x, y, c = _mesh_pos()
        chips = [(1 - x, y), (x, 1 - y), (1 - x, 1 - y)]
        cps = [pltpu.make_async_remote_copy(
            src_ref=ins[a].at[2 * px + py], dst_ref=outs[a].at[k], send_sem=send.at[a, k], recv_sem=recv.at[a, k],
            device_id=(px, py, c), device_id_type=MESH) for a in range(n) for k, (px, py) in enumerate(chips)]
        for cp in cps:
            cp.start()
        for cp in cps:
            cp.wait()

    return pl.pallas_call(
        body, name="grad_chip_exchange",
        out_shape=[jax.ShapeDtypeStruct((3,) + p.shape[1:], p.dtype) for p in parts],
        in_specs=[ANY] * n, out_specs=[ANY] * n,
        scratch_shapes=[pltpu.SemaphoreType.DMA((n, 3)), pltpu.SemaphoreType.DMA((n, 3))],
    )(*parts)


def _pair_allgather(halves):
    n = len(halves)

    def body(*refs):
        ins, outs = refs[:n], refs[n:2 * n]
        send, recv, local = refs[2 * n:]
        x, y, c = _mesh_pos()
        mine = pl.ds(pl.multiple_of(c * HALF, 128), HALF)
        loc = [pltpu.make_async_copy(ins[a], outs[a].at[:, mine], local.at[a]) for a in range(n)]
        cps = [pltpu.make_async_remote_copy(
            src_ref=ins[a], dst_ref=outs[a].at[:, mine], send_sem=send.at[a], recv_sem=recv.at[a],
            device_id=(x, y, 1 - c), device_id_type=MESH) for a in range(n)]
        for cp in loc + cps:
            cp.start()
        for cp in cps + loc:
            cp.wait()

    return pl.pallas_call(
        body, name="grad_pair_allgather",
        out_shape=[jax.ShapeDtypeStruct((h.shape[0], 2 * HALF), h.dtype) for h in halves],
        in_specs=[ANY] * n, out_specs=[ANY] * n,
        scratch_shapes=[pltpu.SemaphoreType.DMA((n,)), pltpu.SemaphoreType.DMA((n,)), pltpu.SemaphoreType.DMA((n,))],
    )(*halves)


def _allgather_small(pack):
    m_per, ncol = pack.shape

    def body(x_ref, out_ref, send_sems, recv_sems, local_sem):
        x, y, c = _mesh_pos()
        me, sibling = (x, y, c), (x, y, 1 - c)
        chips = [(1 - x, y), (x, 1 - y), (1 - x, 1 - y)]

        def rows(px, py, pc):
            return out_ref.at[pl.ds(pl.multiple_of((4 * px + 2 * py + pc) * m_per, 8), m_per), :]

        def copy(k, block, to, src=None):
            return pltpu.make_async_remote_copy(
                src_ref=rows(*block) if src is None else src, dst_ref=rows(*block),
                send_sem=send_sems.at[k], recv_sem=recv_sems.at[k], device_id=to, device_id_type=MESH)

        mine = pltpu.make_async_copy(x_ref, rows(*me), local_sem)
        mine.start()
        first = [copy(0, me, sibling, src=x_ref)]
        first += [copy(1 + j, me, (*chip, c), src=x_ref) for j, chip in enumerate(chips)]
        for cp in first:
            cp.start()
        passed = [copy(4 + j, (*chip, c), sibling) for j, chip in enumerate(chips)]
        for j, chip in enumerate(chips):
            copy(1 + j, (*chip, c), me).wait_recv()
            passed[j].start()
        copy(0, sibling, me).wait_recv()
        for j, chip in enumerate(chips):
            copy(4 + j, (*chip, 1 - c), me).wait_recv()
        for cp in first + passed:
            cp.wait_send()
        mine.wait()

    return pl.pallas_call(
        body, name="allgather_small",
        out_shape=jax.ShapeDtypeStruct((8 * m_per, ncol), pack.dtype),
        in_specs=[pl.BlockSpec(memory_space=pltpu.VMEM)],
        out_specs=pl.BlockSpec(memory_space=pltpu.VMEM),
        scratch_shapes=[pltpu.SemaphoreType.DMA((7,)), pltpu.SemaphoreType.DMA((7,)), pltpu.SemaphoreType.DMA],
    )(pack)


def _add_pair(g, got, c_arr):
    _, rows, _ = g.shape

    def body(c_ref, g_ref, r_ref, o_ref):
        o_ref[...] = g_ref[...] + r_ref[...]

    return pl.pallas_call(
        body, name="grad_add_pair", out_shape=jax.ShapeDtypeStruct((4, rows, HALF), F32),
        grid_spec=pltpu.PrefetchScalarGridSpec(
            num_scalar_prefetch=1, grid=(4,),
            in_specs=[pl.BlockSpec((1, rows, HALF), lambda j, c: (j, 0, c[0])),
                      pl.BlockSpec((1, rows, HALF), lambda j, c: (j, 0, 0))],
            out_specs=pl.BlockSpec((1, rows, HALF), lambda j, c: (j, 0, 0))),
        compiler_params=_params(("arbitrary",)),
    )(c_arr, g, got)


def _add_chips(part, got, chip_arr):
    _, rows, _ = part.shape

    def body(j_ref, p_ref, r_ref, o_ref):
        o_ref[...] = ((p_ref[0] + r_ref[0]) + r_ref[1]) + r_ref[2]

    return pl.pallas_call(
        body, name="grad_add_chips", out_shape=jax.ShapeDtypeStruct((rows, HALF), F32),
        grid_spec=pltpu.PrefetchScalarGridSpec(
            num_scalar_prefetch=1, grid=(1,),
            in_specs=[pl.BlockSpec((1, rows, HALF), lambda i, j: (j[0], 0, 0)),
                      pl.BlockSpec((3, rows, HALF), lambda i, j: (0, 0, 0))],
            out_specs=pl.BlockSpec((rows, HALF), lambda i, j: (0, 0))),
        compiler_params=_params(("arbitrary",)),
    )(chip_arr, part, got)


def _sum_small(gathered):
    def body(g_ref, o_ref):
        acc = g_ref[0:SMALL_ROWS, :]
        for d in range(1, 8):
            acc = acc + g_ref[d * SMALL_ROWS:(d + 1) * SMALL_ROWS, :]
        o_ref[...] = acc

    return pl.pallas_call(
        body, name="sum_small", out_shape=jax.ShapeDtypeStruct((SMALL_ROWS, 128), F32),
    )(gathered)


def _adamw(w, g, m, v):
    rows, cols = w.shape
    tr = 256 if rows % 256 == 0 else rows
    c1 = 1.0 - ADAM_B1 ** ADAM_STEP
    c2 = 1.0 - ADAM_B2 ** ADAM_STEP

    def body(w_ref, g_ref, m_ref, v_ref, d_ref, mo_ref, vo_ref):
        gg = g_ref[...]
        mn = ADAM_B1 * m_ref[...] + (1.0 - ADAM_B1) * gg
        vn = ADAM_B2 * v_ref[...] + (1.0 - ADAM_B2) * (gg * gg)
        d_ref[...] = -ADAM_LR * ((mn / c1) / (jnp.sqrt(vn / c2) + ADAM_EPS) + ADAM_WD * w_ref[...])
        mo_ref[...] = mn
        vo_ref[...] = vn

    spec = pl.BlockSpec((tr, cols), lambda i: (i, 0))
    return pl.pallas_call(
        body, name="adamw", out_shape=[jax.ShapeDtypeStruct(w.shape, F32)] * 3,
        grid=(rows // tr,), in_specs=[spec] * 4, out_specs=[spec] * 3,
        compiler_params=_params(("parallel",)),
    )(w, g, m, v)


def _ln_in_fwd(x, meta, g, b):
    bsz, s, d = x.shape
    tp = s + HEAD
    nh = 2
    sh = s // nh
    rc = min(256, sh)

    def body(x_ref, meta_ref, g_ref, b_ref, s0_ref, s0b_ref):
        h = pl.program_id(1)
        gg, bb = g_ref[...], b_ref[...]

        @pl.when(h == 0)
        def _():
            s0_ref[0, 0:PAD, :] = jnp.zeros((PAD, d), F32)
            s0b_ref[0, 0:PAD, :] = jnp.zeros((PAD, d), BF16)
            mh, _ = _ln_stats(meta_ref[...])
            mv = mh * gg + bb
            s0_ref[0, PAD:HEAD, :] = mv
            s0b_ref[0, PAD:HEAD, :] = mv.astype(BF16)

        def step(i, carry):
            src = pl.ds(pl.multiple_of(i * rc, rc), rc)
            dst = pl.ds(pl.multiple_of(HEAD + h * sh + i * rc, 64), rc)
            xh, _ = _ln_stats(x_ref[0, src, :])
            val = xh * gg + bb
            s0_ref[0, dst, :] = val
            s0b_ref[0, dst, :] = val.astype(BF16)
            return carry

        lax.fori_loop(0, sh // rc, step, 0)

    full = lambda bi, hi: (bi, 0, 0)
    return pl.pallas_call(
        body, name="ln_in_fwd",
        out_shape=[jax.ShapeDtypeStruct((bsz, tp, d), F32), jax.ShapeDtypeStruct((bsz, tp, d), BF16)],
        grid=(bsz, nh),
        in_specs=[pl.BlockSpec((1, sh, d), lambda bi, hi: (bi, hi, 0)),
                  pl.BlockSpec((N_META, d), lambda bi, hi: (0, 0)),
                  pl.BlockSpec((1, d), lambda bi, hi: (0, 0)),
                  pl.BlockSpec((1, d), lambda bi, hi: (0, 0))],
        out_specs=[pl.BlockSpec((1, tp, d), full)] * 2,
        compiler_params=_params(("parallel", "arbitrary")),
    )(x, meta, g, b)


def _in_proj(s0b, w_int):
    r, d = s0b.shape
    tm = _row_tile(r)

    def body(a_ref, w_ref, o_ref):
        o_ref[...] = _nt(a_ref[...], w_ref[...])

    return pl.pallas_call(
        body, name="in_proj", out_shape=jax.ShapeDtypeStruct((r, D_IN_PAD), F32),
        grid=(r // tm,),
        in_specs=[pl.BlockSpec((tm, d), lambda i: (i, 0)), pl.BlockSpec((D_IN_PAD, d), lambda i: (0, 0))],
        out_specs=pl.BlockSpec((tm, D_IN_PAD), lambda i: (i, 0)),
        compiler_params=_params(("parallel",)),
    )(s0b, w_int)


def _conv_fwd(u, conv_w, conv_b):
    bsz, tp, _ = u.shape
    nchunk = tp // CHUNK
    win = CHUNK + 32
    nct = D_CONV // 128

    def body(cv_ref, cg_ref, w_ref, cb_ref, hc_ref, h_scr, win_scr):
        h_scr[0:32, :] = jnp.zeros((32, 128), F32)
        h_scr[32:32 + tp, :] = cv_ref[0] * _sigmoid(cg_ref[0])
        cb = cb_ref[...]

        def step(n, carry):
            r0 = pl.multiple_of(n * CHUNK, CHUNK)
            win_scr[...] = h_scr[pl.ds(r0, win), :]
            acc = jnp.zeros((CHUNK, 128), F32)
            for j in range(CONV_WIDTH):
                acc = acc + w_ref[j:j + 1, :] * win_scr[2 + j:2 + j + CHUNK, :]
            hc_ref[0, pl.ds(r0, CHUNK), :] = acc + cb
            return carry

        lax.fori_loop(0, nchunk, step, 0)

    return pl.pallas_call(
        body, name="conv_fwd", out_shape=jax.ShapeDtypeStruct((bsz, tp, D_CONV), F32),
        grid=(bsz, nct),
        in_specs=[pl.BlockSpec((1, tp, 128), lambda bi, t: (bi, 0, C_VAL // 128 + t)),
                  pl.BlockSpec((1, tp, 128), lambda bi, t: (bi, 0, C_GATE // 128 + t)),
                  pl.BlockSpec((32, 128), lambda bi, t: (0, t)),
                  pl.BlockSpec((1, 128), lambda bi, t: (0, t))],
        out_specs=pl.BlockSpec((1, tp, 128), lambda bi, t: (bi, 0, t)),
        scratch_shapes=[pltpu.VMEM((tp + 32, 128), F32), pltpu.VMEM((win, 128), F32)],
        compiler_params=_params(("parallel", "parallel")),
    )(u, u, conv_w, conv_b)


def _gla_consts():
    row = lax.broadcasted_iota(jnp.int32, (CHUNK, CHUNK), 0)
    col = lax.broadcasted_iota(jnp.int32, (CHUNK, CHUNK), 1)
    lane = lax.broadcasted_iota(jnp.int32, (1, 128), 1)
    return row >= col, [lane < GLA_DK, lane >= GLA_DK]


def _gla_triu():
    row = lax.broadcasted_iota(jnp.int32, (CHUNK, CHUNK), 0)
    col = lax.broadcasted_iota(jnp.int32, (CHUNK, CHUNK), 1)
    return (row <= col).astype(F32)


def _gla_chunk_terms(n, q_ref, k_ref, gd_ref, gup_ref, gb_ref, tri):
    r0 = pl.multiple_of(n * CHUNK, CHUNK)
    rows = pl.ds(r0, CHUNK)
    z = _nn(gd_ref[0, rows, :].astype(BF16), gup_ref[...]) + gb_ref[...]
    rowid = n * CHUNK + lax.broadcasted_iota(jnp.int32, (CHUNK, 1), 0)
    valid = rowid >= PAD
    lg = jnp.where(valid, _log_sigmoid(z) * (1.0 / GLA_TAU), 0.0)
    bcum = _nn(tri.astype(F32), lg, precision=lax.Precision.HIGHEST)
    blast = bcum[CHUNK - 1:CHUNK, :]
    eb = jnp.exp(bcum)
    enb = jnp.exp(-bcum)
    erest = jnp.exp(blast - bcum)
    q = q_ref[0, rows, :] * Q_SCALE
    k = k_ref[0, rows, :]
    return rows, valid, z, eb, enb, erest, jnp.exp(blast), q * eb, k * enb, k * erest


def _gla_fwd(u, gup, gbias, gnorm):
    bsz, tp, _ = u.shape
    nchunk = tp // CHUNK

    def body(q_ref, k_ref, v_ref, r_ref, gd_ref, gup_ref, gb_ref, gn_ref, out_ref, o_ref, st_ref, s_scr):
        tri, hmask = _gla_consts()
        s_scr[...] = jnp.zeros_like(s_scr)
        gn = gn_ref[...]

        def step(n, carry):
            rows, _, _, _, _, _, dec, qe, ke, kd = _gla_chunk_terms(n, q_ref, k_ref, gd_ref, gup_ref, gb_ref, tri)
            keb, kdb = ke.astype(BF16), kd.astype(BF16)
            for h in range(2):
                cols = slice(h * GLA_DV, (h + 1) * GLA_DV)
                qh = jnp.where(hmask[h], qe, 0.0).astype(BF16)
                vh = v_ref[0, rows, cols].astype(BF16)
                a = jnp.where(tri, _nt(qh, keb), 0.0)
                st = s_scr[h]
                st_ref[0, h, n] = st
                o = _nn(a.astype(BF16), vh) + _nt(qh, st.astype(BF16))
                s_scr[h] = dec * st + _tn(vh, kdb)
                o_ref[0, rows, cols] = o
                rms = lax.rsqrt(jnp.mean(o * o, axis=-1, keepdims=True) + LN_EPS)
                rh = r_ref[0, rows, cols]
                out_ref[0, rows, cols] = (o * rms * gn * (rh * _sigmoid(rh))).astype(BF16)
            return carry

        lax.fori_loop(0, nchunk, step, 0)

    return pl.pallas_call(
        body, name="gla_fwd",
        out_shape=[jax.ShapeDtypeStruct((bsz, tp, 512), BF16), jax.ShapeDtypeStruct((bsz, tp, 512), F32),
                   jax.ShapeDtypeStruct((bsz, GLA_HEADS, nchunk, GLA_DV, 128), F32)],
        grid=(bsz, 2),
        in_specs=[pl.BlockSpec((1, tp, 128), lambda bi, p: (bi, 0, C_Q // 128 + p)),
                  pl.BlockSpec((1, tp, 128), lambda bi, p: (bi, 0, C_K // 128 + p)),
                  pl.BlockSpec((1, tp, 256), lambda bi, p: (bi, 0, C_V // 256 + p)),
                  pl.BlockSpec((1, tp, 256), lambda bi, p: (bi, 0, C_R // 256 + p)),
                  pl.BlockSpec((1, tp, 128), lambda bi, p: (bi, 0, C_GD // 128)),
                  pl.BlockSpec((128, 128), lambda bi, p: (0, p)),
                  pl.BlockSpec((1, 128), lambda bi, p: (0, p)),
                  pl.BlockSpec((1, 128), lambda bi, p: (0, 0))],
        out_specs=[pl.BlockSpec((1, tp, 256), lambda bi, p: (bi, 0, p)),
                   pl.BlockSpec((1, tp, 256), lambda bi, p: (bi, 0, p)),
                   pl.BlockSpec((1, 2, nchunk, GLA_DV, 128), lambda bi, p: (bi, p, 0, 0, 0))],
        scratch_shapes=[pltpu.VMEM((2, GLA_DV, 128), F32)],
        compiler_params=_params(("parallel", "parallel")),
    )(u, u, u, u, u, gup, gbias, gnorm)


def _out_proj_ln1(hc, gla_out, w_out, s0, cg, cb, g1, b1):
    r, d = s0.shape
    tm = _row_tile(r)

    def body(hc_ref, a_ref, w_ref, s0_ref, cg_ref, cb_ref, g_ref, b_ref, co_ref, xh_ref, rstd_ref, s1b_ref):
        xc, _ = _ln_stats(hc_ref[...])
        nv = xc * cg_ref[...] + cb_ref[...]
        co = (nv * _sigmoid(nv)).astype(BF16)
        co_ref[...] = co
        mix = _nn(co, w_ref[0:D_CONV, :]) + _nn(a_ref[...], w_ref[D_CONV:, :])
        xh, rstd = _ln_stats(ALPHA * s0_ref[...] + mix)
        xh_ref[...] = xh
        rstd_ref[...] = rstd
        s1b_ref[...] = (xh * g_ref[...] + b_ref[...]).astype(BF16)

    row = lambda n: pl.BlockSpec((tm, n), lambda i: (i, 0))
    vec = lambda n: pl.BlockSpec((1, n), lambda i: (0, 0))
    return pl.pallas_call(
        body, name="out_proj_ln1",
        out_shape=[jax.ShapeDtypeStruct((r, D_CONV), BF16), jax.ShapeDtypeStruct((r, d), F32),
                   jax.ShapeDtypeStruct((r, 1), F32), jax.ShapeDtypeStruct((r, d), BF16)],
        grid=(r // tm,),
        in_specs=[row(D_CONV), row(512), pl.BlockSpec((d, d), lambda i: (0, 0)), row(d),
                  vec(D_CONV), vec(D_CONV), vec(d), vec(d)],
        out_specs=[row(D_CONV), row(d), row(1), row(d)],
        compiler_params=_params(("parallel",)),
    )(hc, gla_out, w_out, s0, cg, cb, g1, b1)


def _ffn1(s1b, w1t):
    r, d = s1b.shape
    tm = _row_tile(r)

    def body(a_ref, w_ref, o_ref):
        o_ref[...] = jnp.maximum(_nt(a_ref[...], w_ref[...]), 0.0).astype(BF16)

    return pl.pallas_call(
        body, name="ffn1", out_shape=jax.ShapeDtypeStruct((r, D_FF), BF16),
        grid=(r // tm,),
        in_specs=[pl.BlockSpec((tm, d), lambda i: (i, 0)), pl.BlockSpec((D_FF, d), lambda i: (0, 0))],
        out_specs=pl.BlockSpec((tm, D_FF), lambda i: (i, 0)),
        compiler_params=_params(("parallel",)),
    )(s1b, w1t)


def _ffn2_ln2_loss(ra, w2, xhat1, g1, b1, g2, b2, tgt, tp):
    r, d = xhat1.shape
    tm = _row_tile(tp)
    per = tp // tm

    def body(ra_ref, w_ref, xh1_ref, g1_ref, b1_ref, g2_ref, b2_ref, t_ref, dt_ref, dtb_ref, acc_ref):
        i = pl.program_id(0)

        @pl.when(i == 0)
        def _():
            acc_ref[...] = jnp.zeros_like(acc_ref)

        rf = ra_ref[...].astype(F32)
        f = _nn((rf * rf).astype(BF16), w_ref[...])
        s1 = xh1_ref[...] * g1_ref[...] + b1_ref[...]
        xh2, rstd2 = _ln_stats(ALPHA * s1 + f)
        y = xh2 * g2_ref[...] + b2_ref[...]
        rowid = (i % per) * tm + lax.broadcasted_iota(jnp.int32, (tm, 1), 0)
        e = jnp.where(rowid >= HEAD, y - t_ref[...], 0.0)
        dy = e * (1.0 / d)
        dt2 = _ln_bwd(dy * g2_ref[...], xh2, rstd2)
        dt_ref[...] = dt2
        dtb_ref[...] = dt2.astype(BF16)
        acc_ref[0:1, :] += (0.5 / d) * jnp.sum(e * e, axis=0, keepdims=True)
        acc_ref[1:2, :] += jnp.sum(dy * xh2, axis=0, keepdims=True)
        acc_ref[2:3, :] += jnp.sum(dy, axis=0, keepdims=True)

    row = lambda n: pl.BlockSpec((tm, n), lambda i: (i, 0))
    vec = pl.BlockSpec((1, d), lambda i: (0, 0))
    return pl.pallas_call(
        body, name="ffn2_ln2_loss",
        out_shape=[jax.ShapeDtypeStruct((r, d), F32), jax.ShapeDtypeStruct((r, d), BF16),
                   jax.ShapeDtypeStruct((8, d), F32)],
        grid=(r // tm,),
        in_specs=[row(D_FF), pl.BlockSpec((D_FF, d), lambda i: (0, 0)), row(d), vec, vec, vec, vec, row(d)],
        out_specs=[row(d), row(d), pl.BlockSpec((8, d), lambda i: (0, 0))],
        compiler_params=_params(("arbitrary",)),
    )(ra, w2, xhat1, g1, b1, g2, b2, tgt)


def _ffn_bwd_da(dt2b, w2, ra):
    r, d = dt2b.shape
    tm = _row_tile(r)

    def body(g_ref, w_ref, ra_ref, o_ref):
        o_ref[...] = (_nt(g_ref[...], w_ref[...]) * (2.0 * ra_ref[...].astype(F32))).astype(BF16)

    return pl.pallas_call(
        body, name="ffn_bwd_da", out_shape=jax.ShapeDtypeStruct((r, D_FF), BF16),
        grid=(r // tm,),
        in_specs=[pl.BlockSpec((tm, d), lambda i: (i, 0)), pl.BlockSpec((D_FF, d), lambda i: (0, 0)),
                  pl.BlockSpec((tm, D_FF), lambda i: (i, 0))],
        out_specs=pl.BlockSpec((tm, D_FF), lambda i: (i, 0)),
        compiler_params=_params(("parallel",)),
    )(dt2b, w2, ra)


def _ffn_bwd_ln1(da, w1t, dt2, xhat1, rstd1, g1):
    r, d = dt2.shape
    tm = _row_tile(r)

    def body(da_ref, w_ref, dt2_ref, xh_ref, rstd_ref, g_ref, dt_ref, dtb_ref, acc_ref):
        @pl.when(pl.program_id(0) == 0)
        def _():
            acc_ref[...] = jnp.zeros_like(acc_ref)

        ds1 = ALPHA * dt2_ref[...] + _nn(da_ref[...], w_ref[...])
        xh = xh_ref[...]
        dt1 = _ln_bwd(ds1 * g_ref[...], xh, rstd_ref[...])
        dt_ref[...] = dt1
        dtb_ref[...] = dt1.astype(BF16)
        acc_ref[0:1, :] += jnp.sum(ds1 * xh, axis=0, keepdims=True)
        acc_ref[1:2, :] += jnp.sum(ds1, axis=0, keepdims=True)

    row = lambda n: pl.BlockSpec((tm, n), lambda i: (i, 0))
    return pl.pallas_call(
        body, name="ffn_bwd_ln1",
        out_shape=[jax.ShapeDtypeStruct((r, d), F32), jax.ShapeDtypeStruct((r, d), BF16),
                   jax.ShapeDtypeStruct((8, d), F32)],
        grid=(r // tm,),
        in_specs=[row(D_FF), pl.BlockSpec((D_FF, d), lambda i: (0, 0)), row(d), row(d), row(1),
                  pl.BlockSpec((1, d), lambda i: (0, 0))],
        out_specs=[row(d), row(d), pl.BlockSpec((8, d), lambda i: (0, 0))],
        compiler_params=_params(("arbitrary",)),
    )(da, w1t, dt2, xhat1, rstd1, g1)


def _matmul_tn(lhs, rhs, bm, square_lhs=False, name="matmul_tn"):
    r, m = lhs.shape
    n = rhs.shape[1]
    tk = _row_tile(r)

    def body(a_ref, b_ref, o_ref):
        @pl.when(pl.program_id(1) == 0)
        def _():
            o_ref[...] = jnp.zeros_like(o_ref)

        a = a_ref[...]
        if square_lhs:
            af = a.astype(F32)
            a = (af * af).astype(BF16)
        o_ref[...] += _tn(a, b_ref[...])

    return pl.pallas_call(
        body, name=name, out_shape=jax.ShapeDtypeStruct((m, n), F32),
        grid=(m // bm, r // tk),
        in_specs=[pl.BlockSpec((tk, bm), lambda i, k: (k, i)), pl.BlockSpec((tk, n), lambda i, k: (k, 0))],
        out_specs=pl.BlockSpec((bm, n), lambda i, k: (i, 0)),
        compiler_params=_params(("parallel", "arbitrary")),
    )(lhs, rhs)


def _out_proj_bwd(dt1b, w_out, hc, cg, cb):
    r, d = dt1b.shape
    tm = _row_tile(r)

    def body(g_ref, w_ref, hc_ref, cg_ref, cb_ref, dhc_ref, dgla_ref, acc_ref):
        @pl.when(pl.program_id(0) == 0)
        def _():
            acc_ref[...] = jnp.zeros_like(acc_ref)

        dmix = _nt(g_ref[...], w_ref[...])
        dgla_ref[...] = dmix[:, D_CONV:]
        gg = cg_ref[...]
        xh, rstd = _ln_stats(hc_ref[...])
        nv = xh * gg + cb_ref[...]
        sig = _sigmoid(nv)
        dn = dmix[:, :D_CONV] * (sig * (1.0 + nv * (1.0 - sig)))
        dhc = _ln_bwd(dn * gg, xh, rstd)
        dhc_ref[...] = dhc
        acc_ref[0:1, :] += jnp.sum(dhc, axis=0, keepdims=True)
        acc_ref[1:2, :] += jnp.sum(dn * xh, axis=0, keepdims=True)
        acc_ref[2:3, :] += jnp.sum(dn, axis=0, keepdims=True)

    row = lambda n: pl.BlockSpec((tm, n), lambda i: (i, 0))
    vec = pl.BlockSpec((1, D_CONV), lambda i: (0, 0))
    return pl.pallas_call(
        body, name="out_proj_bwd",
        out_shape=[jax.ShapeDtypeStruct((r, D_CONV), F32), jax.ShapeDtypeStruct((r, 512), F32),
                   jax.ShapeDtypeStruct((8, D_CONV), F32)],
        grid=(r // tm,),
        in_specs=[row(d), pl.BlockSpec((d, d), lambda i: (0, 0)), row(D_CONV), vec, vec],
        out_specs=[row(D_CONV), row(512), pl.BlockSpec((8, D_CONV), lambda i: (0, 0))],
        compiler_params=_params(("arbitrary",)),
    )(dt1b, w_out, hc, cg, cb)


def _conv_bwd(dhc, u, conv_w):
    bsz, tp, _ = u.shape
    nchunk = tp // CHUNK
    win = CHUNK + 32
    nct = D_CONV // 128

    def body(dhc_ref, cv_ref, cg_ref, w_ref, dv_ref, dg_ref, dw_ref, h_scr, dhc_scr, hwin, dwin, dw_scr):
        h_scr[0:32, :] = jnp.zeros((32, 128), F32)
        h_scr[32:32 + tp, :] = cv_ref[0] * _sigmoid(cg_ref[0])
        dhc_scr[0:tp, :] = dhc_ref[0]
        dhc_scr[tp:tp + 32, :] = jnp.zeros((32, 128), F32)
        dw_scr[...] = jnp.zeros_like(dw_scr)

        def step(n, carry):
            r0 = pl.multiple_of(n * CHUNK, CHUNK)
            rows = pl.ds(r0, CHUNK)
            hwin[...] = h_scr[pl.ds(r0, win), :]
            dwin[...] = dhc_scr[pl.ds(r0, win), :]
            dcur = dwin[0:CHUNK, :]
            acc = jnp.zeros((CHUNK, 128), F32)
            for j in range(CONV_WIDTH):
                acc = acc + w_ref[j:j + 1, :] * dwin[30 - j:30 - j + CHUNK, :]
                prod = dcur * hwin[2 + j:2 + j + CHUNK, :]
                dw_scr[j * 8:(j + 1) * 8, :] += jnp.sum(prod.reshape(CHUNK // 8, 8, 128), axis=0)
            cg = cg_ref[0, rows, :]
            sig = _sigmoid(cg)
            rowid = n * CHUNK + lax.broadcasted_iota(jnp.int32, (CHUNK, 1), 0)
            dh = jnp.where(rowid >= PAD, acc, 0.0)
            dv_ref[0, rows, :] = (dh * sig).astype(BF16)
            dg_ref[0, rows, :] = (dh * cv_ref[0, rows, :] * sig * (1.0 - sig)).astype(BF16)
            return carry

        lax.fori_loop(0, nchunk, step, 0)
        dw_ref[0] = jnp.zeros((32, 128), F32)
        for j in range(CONV_WIDTH):
            dw_ref[0, j:j + 1, :] = jnp.sum(dw_scr[j * 8:(j + 1) * 8, :], axis=0, keepdims=True)

    blk = lambda off: pl.BlockSpec((1, tp, 128), lambda bi, t: (bi, 0, off // 128 + t))
    return pl.pallas_call(
        body, name="conv_bwd",
        out_shape=[jax.ShapeDtypeStruct((bsz, tp, D_CONV), BF16), jax.ShapeDtypeStruct((bsz, tp, D_CONV), BF16),
                   jax.ShapeDtypeStruct((bsz, 32, D_CONV), F32)],
        grid=(bsz, nct),
        in_specs=[blk(0), blk(C_VAL), blk(C_GATE), pl.BlockSpec((32, 128), lambda bi, t: (0, t))],
        out_specs=[blk(0), blk(0), pl.BlockSpec((1, 32, 128), lambda bi, t: (bi, 0, t))],
        scratch_shapes=[pltpu.VMEM((tp + 32, 128), F32), pltpu.VMEM((tp + 32, 128), F32),
                        pltpu.VMEM((win, 128), F32), pltpu.VMEM((win, 128), F32),
                        pltpu.VMEM((CONV_WIDTH * 8, 128), F32)],
        compiler_params=_params(("parallel", "parallel")),
    )(dhc, u, u, conv_w)


def _gla_bwd(dgla, u, o_pre, states, gup, gbias, gnorm):
    bsz, tp, _ = u.shape
    nchunk = tp // CHUNK

    def body(dy_ref, q_ref, k_ref, v_ref, r_ref, gd_ref, o_ref, st_ref, gup_ref, gb_ref, gn_ref,
             dq_ref, dk_ref, dv_ref, dr_ref, dgd_ref, dgup_ref, vec_ref, h_scr, gup_acc):
        tri, hmask = _gla_consts()
        triu = _gla_triu()
        h_scr[...] = jnp.zeros_like(h_scr)
        gup_acc[...] = jnp.zeros_like(gup_acc)
        gn = gn_ref[...]
        gupb = gup_ref[...]

        def step(i, carry):
            dbias, dgn = carry
            n = nchunk - 1 - i
            rows, valid, z, eb, enb, erest, dec, qe, ke, kd = _gla_chunk_terms(
                n, q_ref, k_ref, gd_ref, gup_ref, gb_ref, tri)
            keb, kdb = ke.astype(BF16), kd.astype(BF16)
            dqe = jnp.zeros((CHUNK, 128), F32)
            dke = jnp.zeros((CHUNK, 128), F32)
            dkd = jnp.zeros((CHUNK, 128), F32)
            ddec = jnp.zeros((1, 128), F32)
            for h in range(2):
                cols = slice(h * GLA_DV, (h + 1) * GLA_DV)
                o = o_ref[0, rows, cols]
                rh = r_ref[0, rows, cols]
                dy = dy_ref[0, rows, cols]
                rms = lax.rsqrt(jnp.mean(o * o, axis=-1, keepdims=True) + LN_EPS)
                nrm = o * rms
                sig = _sigmoid(rh)
                sw = rh * sig
                dr_ref[0, rows, cols] = (dy * nrm * gn * (sig * (1.0 + rh * (1.0 - sig)))).astype(BF16)
                dgn = dgn + jnp.sum(dy * nrm * sw, axis=0, keepdims=True)
                dn = dy * gn * sw
                do = rms * (dn - nrm * jnp.mean(dn * nrm, axis=-1, keepdims=True))
                dob = do.astype(BF16)
                qh = jnp.where(hmask[h], qe, 0.0).astype(BF16)
                vh = v_ref[0, rows, cols].astype(BF16)
                a = jnp.where(tri, _nt(qh, keb), 0.0).astype(BF16)
                st = st_ref[0, h, n]
                ht = h_scr[h]
                da = jnp.where(tri, _nt(dob, vh), 0.0).astype(BF16)
                dqe = dqe + jnp.where(hmask[h], _nn(da, keb) + _nn(dob, st.astype(BF16)), 0.0)
                dke = dke + _tn(da, qh)
                dv_ref[0, rows, cols] = (_tn(a, dob) + _nt(kdb, ht.astype(BF16))).astype(BF16)
                dkd = dkd + jnp.where(hmask[h], _nn(vh, ht.astype(BF16)), 0.0)
                ddec = ddec + jnp.where(hmask[h], jnp.sum(ht * st, axis=0, keepdims=True), 0.0)
                h_scr[h] = dec * ht + _tn(dob, qh)
            dq_ref[0, rows, :] = (dqe * eb * Q_SCALE).astype(BF16)
            dk_ref[0, rows, :] = (dke * enb + dkd * erest).astype(BF16)
            db = dqe * qe - dke * ke - dkd * kd
            dblast = jnp.sum(dkd * kd, axis=0, keepdims=True) + ddec * dec
            lastrow = lax.broadcasted_iota(jnp.int32, (CHUNK, 1), 0) == CHUNK - 1
            db = db + jnp.where(lastrow, dblast, 0.0)
            dlg = jnp.where(valid, _nn(triu, db, precision=lax.Precision.HIGHEST), 0.0)
            dz = dlg * (1.0 / GLA_TAU) * (1.0 - _sigmoid(z))
            dzb = dz.astype(BF16)
            dgd_ref[0, 0, rows, :] = _nt(dzb, gupb).astype(BF16)
            gup_acc[...] += _tn(gd_ref[0, rows, :].astype(BF16), dzb)
            return dbias + jnp.sum(dz, axis=0, keepdims=True), dgn

        zero = jnp.zeros((1, 128), F32)
        dbias, dgn = lax.fori_loop(0, nchunk, step, (zero, zero))
        dgup_ref[0] = gup_acc[...]
        vec_ref[0] = jnp.zeros((8, 128), F32)
        vec_ref[0, 0:1, :] = dbias
        vec_ref[0, 1:2, :] = dgn

    pair = lambda w, off: pl.BlockSpec((1, tp, w), lambda bi, p: (bi, 0, off // w + p))
    return pl.pallas_call(
        body, name="gla_bwd",
        out_shape=[jax.ShapeDtypeStruct((bsz, tp, 256), BF16), jax.ShapeDtypeStruct((bsz, tp, 256), BF16),
                   jax.ShapeDtypeStruct((bsz, tp, 512), BF16), jax.ShapeDtypeStruct((bsz, tp, 512), BF16),
                   jax.ShapeDtypeStruct((bsz, 2, tp, 128), BF16), jax.ShapeDtypeStruct((bsz, 128, 256), F32),
                   jax.ShapeDtypeStruct((bsz, 8, 256), F32)],
        grid=(bsz, 2),
        in_specs=[pair(256, 0), pair(128, C_Q), pair(128, C_K), pair(256, C_V), pair(256, C_R),
                  pl.BlockSpec((1, tp, 128), lambda bi, p: (bi, 0, C_GD // 128)),
                  pair(256, 0),
                  pl.BlockSpec((1, 2, nchunk, GLA_DV, 128), lambda bi, p: (bi, p, 0, 0, 0)),
                  pl.BlockSpec((128, 128), lambda bi, p: (0, p)),
                  pl.BlockSpec((1, 128), lambda bi, p: (0, p)),
                  pl.BlockSpec((1, 128), lambda bi, p: (0, 0))],
        out_specs=[pair(128, 0), pair(128, 0), pair(256, 0), pair(256, 0),
                   pl.BlockSpec((1, 1, tp, 128), lambda bi, p: (bi, p, 0, 0)),
                   pl.BlockSpec((1, 128, 128), lambda bi, p: (bi, 0, p)),
                   pl.BlockSpec((1, 8, 128), lambda bi, p: (bi, 0, p))],
        scratch_shapes=[pltpu.VMEM((2, GLA_DV, 128), F32), pltpu.VMEM((128, 128), F32)],
        compiler_params=_params(("parallel", "parallel")),
    )(dgla, u, u, u, u, u, o_pre, states, gup, gbias, gnorm)


def _in_proj_bwd(du, w_int, dt1):
    r, d = dt1.shape
    tm = _row_tile(r)

    def body(g_ref, w_ref, dt_ref, o_ref):
        o_ref[...] = ALPHA * dt_ref[...] + _nn(g_ref[...], w_ref[...])

    return pl.pallas_call(
        body, name="in_proj_bwd", out_shape=jax.ShapeDtypeStruct((r, d), F32),
        grid=(r // tm,),
        in_specs=[pl.BlockSpec((tm, D_IN_PAD), lambda i: (i, 0)), pl.BlockSpec((D_IN_PAD, d), lambda i: (0, 0)),
                  pl.BlockSpec((tm, d), lambda i: (i, 0))],
        out_specs=pl.BlockSpec((tm, d), lambda i: (i, 0)),
        compiler_params=_params(("parallel",)),
    )(du, w_int, dt1)


def _ln_in_bwd(ds0, x, meta, g):
    bsz, s, d = x.shape
    tp = s + HEAD
    nh = 2
    sh = s // nh
    rc = min(256, sh)

    def body(ds_ref, x_ref, meta_ref, g_ref, gx_ref, dm_ref, vec_ref):
        h = pl.program_id(1)
        gg = g_ref[...]

        @pl.when(h == 0)
        def _():
            mh, mr = _ln_stats(meta_ref[...])
            dsm = ds_ref[0, PAD:HEAD, :]
            dm_ref[0] = _ln_bwd(dsm * gg, mh, mr)
            vec_ref[0] = jnp.zeros((8, d), F32)
            vec_ref[0, 0:1, :] = jnp.sum(dsm * mh, axis=0, keepdims=True)
            vec_ref[0, 1:2, :] = jnp.sum(dsm, axis=0, keepdims=True)

        def step(i, carry):
            sg, sb = carry
            dst = pl.ds(pl.multiple_of(i * rc, rc), rc)
            src = pl.ds(pl.multiple_of(HEAD + h * sh + i * rc, 64), rc)
            xh, rstd = _ln_stats(x_ref[0, dst, :])
            dsv = ds_ref[0, src, :]
            gx_ref[0, dst, :] = _ln_bwd(dsv * gg, xh, rstd)
            return sg + jnp.sum(dsv * xh, axis=0, keepdims=True), sb + jnp.sum(dsv, axis=0, keepdims=True)

        zero = jnp.zeros((1, d), F32)
        sg, sb = lax.fori_loop(0, sh // rc, step, (zero, zero))
        vec_ref[0, 0:1, :] += sg
        vec_ref[0, 1:2, :] += sb

    return pl.pallas_call(
        body, name="ln_in_bwd",
        out_shape=[jax.ShapeDtypeStruct((bsz, s, d), F32), jax.ShapeDtypeStruct((bsz, N_META, d), F32),
                   jax.ShapeDtypeStruct((bsz, 8, d), F32)],
        grid=(bsz, nh),
        in_specs=[pl.BlockSpec((1, tp, d), lambda bi, hi: (bi, 0, 0)),
                  pl.BlockSpec((1, sh, d), lambda bi, hi: (bi, hi, 0)),
                  pl.BlockSpec((N_META, d), lambda bi, hi: (0, 0)),
                  pl.BlockSpec((1, d), lambda bi, hi: (0, 0))],
        out_specs=[pl.BlockSpec((1, sh, d), lambda bi, hi: (bi, hi, 0)),
                   pl.BlockSpec((1, N_META, d), lambda bi, hi: (bi, 0, 0)),
                   pl.BlockSpec((1, 8, d), lambda bi, hi: (bi, 0, 0))],
        compiler_params=_params(("parallel", "arbitrary")),
    )(ds0, x, meta, g)


def _rows128(a):
    return a.reshape(-1, 128)


def kernel(x, meta_tokens, ln_in_g, ln_in_b, w_in, conv_w, conv_b, conv_ln_g, conv_ln_b, gate_up, gate_bias, gla_norm_g, w_out, ln1_g, ln1_b, w_ff1, w_ff2, ln2_g, ln2_b, loss_target, m_meta_tokens, m_ln_in_g, m_ln_in_b, m_w_in, m_conv_w, m_conv_b, m_conv_ln_g, m_conv_ln_b, m_gate_up, m_gate_bias, m_gla_norm_g, m_w_out, m_ln1_g, m_ln1_b, m_w_ff1, m_w_ff2, m_ln2_g, m_ln2_b, v_meta_tokens, v_ln_in_g, v_ln_in_b, v_w_in, v_conv_w, v_conv_b, v_conv_ln_g, v_conv_ln_b, v_gate_up, v_gate_bias, v_gla_norm_g, v_w_out, v_ln1_g, v_ln1_b, v_w_ff1, v_w_ff2, v_ln2_g, v_ln2_b):
    bsz, seq, d = x.shape
    tp = seq + HEAD
    r = bsz * tp
    xi, yi, ci = _mesh_pos()
    chip = 2 * xi + yi
    c_arr = jnp.reshape(ci, (1,)).astype(jnp.int32)
    chip_arr = jnp.reshape(chip, (1,)).astype(jnp.int32)

    shards = [w_in[0].T.astype(BF16), w_ff1[0].T.astype(BF16), w_out[0].astype(BF16), w_ff2[0].astype(BF16)]
    small_w = jnp.concatenate([_rows128(meta_tokens), _rows128(conv_w[0]), _rows128(gate_up[0])], axis=0)
    (g_int, g_w1t, g_wout, g_w2), g_small = _gather_weights(shards, small_w)
    w_int = jnp.pad(g_int.reshape(D_IN, d), ((0, D_IN_PAD - D_IN), (0, 0)))
    w1t = g_w1t.reshape(D_FF, d)
    wout = g_wout.reshape(d, d)
    w2 = g_w2.reshape(D_FF, d)
    n_meta_rows, n_cw_rows = N_META * 256 // 128, CONV_WIDTH * 128 // 128
    meta_full = jnp.concatenate([g_small[j, :n_meta_rows].reshape(N_META, 256) for j in range(4)], axis=1)
    convw_full = jnp.concatenate(
        [g_small[j, n_meta_rows:n_meta_rows + n_cw_rows].reshape(CONV_WIDTH, 128) for j in range(4)], axis=1)
    gup_full = jnp.concatenate(
        [g_small[j, n_meta_rows + n_cw_rows:].reshape(GLA_RANK, 64) for j in range(4)], axis=1)
    convw_p = jnp.pad(convw_full, ((0, 1), (0, 0)))
    gup_p = jnp.pad(gup_full, ((0, 128 - GLA_RANK), (0, 0))).astype(BF16)
    ln_in_g2, ln_in_b2 = ln_in_g.reshape(1, d), ln_in_b.reshape(1, d)

    s0, s0b = _ln_in_fwd(x, meta_full, ln_in_g2, ln_in_b2)
    tgt_p = jnp.pad(loss_target, ((0, 0), (HEAD, 0), (0, 0)))
    s0f, s0bf = s0.reshape(r, d), s0b.reshape(r, d)
    u = _in_proj(s0bf, w_int)
    u3 = u.reshape(bsz, tp, D_IN_PAD)
    hc = _conv_fwd(u3, convw_p, conv_b).reshape(r, D_CONV)
    gla_out, o_pre, states = _gla_fwd(u3, gup_p, gate_bias, gla_norm_g)
    gla_of = gla_out.reshape(r, 512)
    conv_of, xhat1, rstd1, s1b = _out_proj_ln1(hc, gla_of, wout, s0f, conv_ln_g, conv_ln_b, ln1_g, ln1_b)
    ra = _ffn1(s1b, w1t)
    dt2, dt2b, acc2 = _ffn2_ln2_loss(ra, w2, xhat1, ln1_g, ln1_b, ln2_g, ln2_b, tgt_p.reshape(r, d), tp)

    da = _ffn_bwd_da(dt2b, w2, ra)
    dt1, dt1b, acc1 = _ffn_bwd_ln1(da, w1t, dt2, xhat1, rstd1, ln1_g)
    g_w2 = _matmul_tn(ra, dt2b, 1024, square_lhs=True, name="grad_w_ff2")
    g_w1t = _matmul_tn(da, s1b, 1024, name="grad_w_ff1")
    dhc, dgla, cacc = _out_proj_bwd(dt1b, wout, hc, conv_ln_g, conv_ln_b)
    g_wout = jnp.concatenate([_matmul_tn(conv_of, dt1b, D_CONV, name="grad_w_out_conv"),
                              _matmul_tn(gla_of, dt1b, 512, name="grad_w_out_gla")], axis=0)
    dcv, dcg, dcw = _conv_bwd(dhc.reshape(bsz, tp, D_CONV), u3, convw_p)
    dq, dk, dv, dr, dgd, dgup, gvec = _gla_bwd(dgla.reshape(bsz, tp, 512), u3, o_pre, states, gup_p, gate_bias,
                                               gla_norm_g)
    dgd_sum = (dgd[:, 0].astype(F32) + dgd[:, 1].astype(F32)).astype(BF16)
    du = jnp.concatenate([dcv, dcg, dq, dk, dv, dr, dgd_sum], axis=-1).reshape(r, D_IN_PAD)
    ds0 = _in_proj_bwd(du, w_int, dt1)
    g_wint = _matmul_tn(du, s0bf, D_IN_PAD // 3, name="grad_w_in")
    grad_x, dmeta, lvec = _ln_in_bwd(ds0.reshape(bsz, tp, d), x, meta_full, ln_in_g2)

    small = {
        "loss": acc2[0:1], "ln_in_g": jnp.sum(lvec[:, 0], axis=0), "ln_in_b": jnp.sum(lvec[:, 1], axis=0),
        "conv_b": cacc[0], "conv_ln_g": cacc[1], "conv_ln_b": cacc[2], "gate_bias": jnp.sum(gvec[:, 0], axis=0),
        "gla_norm_g": jnp.sum(gvec[:, 1].reshape(bsz * 2, 128), axis=0),
        "ln1_g": acc1[0], "ln1_b": acc1[1], "ln2_g": acc2[1], "ln2_b": acc2[2],
        "conv_w": jnp.sum(dcw, axis=0)[:CONV_WIDTH], "gate_up": jnp.sum(dgup, axis=0)[:GLA_RANK],
        "meta_tokens": jnp.sum(dmeta, axis=0),
    }
    pack = jnp.concatenate([_rows128(small[k]) for k, _ in _SMALL_FIELDS], axis=0)
    pack = jnp.pad(pack, ((0, SMALL_ROWS - pack.shape[0]), (0, 0)))
    red = _sum_small(_allgather_small(pack))
    off, tot = {}, 0
    for k, nrows in _SMALL_FIELDS:
        off[k] = (tot, nrows)
        tot += nrows

    def field(k, shape):
        o, nrows = off[k]
        return red[o:o + nrows].reshape(shape)

    loss = jnp.sum(field("loss", (d,)))

    big = [g_wint[:D_IN].reshape(4, D_IN // 4, d), g_w1t.reshape(4, D_FF // 4, d),
           g_wout.reshape(4, d // 4, d), g_w2.reshape(4, D_FF // 4, d)]
    got = _pair_exchange(big)
    parts = [_add_pair(g, t, c_arr) for g, t in zip(big, got)]
    got = _chip_exchange(parts)
    halves = [_add_chips(p, t, chip_arr) for p, t in zip(parts, got)]
    f_wint, f_w1t, f_wout, f_w2 = _pair_allgather(halves)

    def cols(a, width):
        return lax.dynamic_slice_in_dim(a, chip * width, width, axis=a.ndim - 1)

    grads = {
        "meta_tokens": cols(field("meta_tokens", (N_META, d)), 256),
        "ln_in_g": field("ln_in_g", (d,)), "ln_in_b": field("ln_in_b", (d,)),
        "w_in": f_wint.T[None], "conv_w": cols(field("conv_w", (CONV_WIDTH, D_CONV)), 128)[None],
        "conv_b": field("conv_b", (1, D_CONV)), "conv_ln_g": field("conv_ln_g", (1, D_CONV)),
        "conv_ln_b": field("conv_ln_b", (1, D_CONV)),
        "gate_up": cols(field("gate_up", (GLA_RANK, 256)), 64)[None],
        "gate_bias": field("gate_bias", (1, 256)), "gla_norm_g": field("gla_norm_g", (1, 128)),
        "w_out": f_wout[None], "ln1_g": field("ln1_g", (1, d)), "ln1_b": field("ln1_b", (1, d)),
        "w_ff1": f_w1t.T[None], "w_ff2": f_w2[None],
        "ln2_g": field("ln2_g", (1, d)), "ln2_b": field("ln2_b", (1, d)),
    }
    weights = dict(meta_tokens=meta_tokens, ln_in_g=ln_in_g, ln_in_b=ln_in_b, w_in=w_in, conv_w=conv_w, conv_b=conv_b,
                   conv_ln_g=conv_ln_g, conv_ln_b=conv_ln_b, gate_up=gate_up, gate_bias=gate_bias,
                   gla_norm_g=gla_norm_g, w_out=w_out, ln1_g=ln1_g, ln1_b=ln1_b, w_ff1=w_ff1, w_ff2=w_ff2,
                   ln2_g=ln2_g, ln2_b=ln2_b)
    moms = dict(meta_tokens=(m_meta_tokens, v_meta_tokens), ln_in_g=(m_ln_in_g, v_ln_in_g),
                ln_in_b=(m_ln_in_b, v_ln_in_b), w_in=(m_w_in, v_w_in), conv_w=(m_conv_w, v_conv_w),
                conv_b=(m_conv_b, v_conv_b), conv_ln_g=(m_conv_ln_g, v_conv_ln_g),
                conv_ln_b=(m_conv_ln_b, v_conv_ln_b), gate_up=(m_gate_up, v_gate_up),
                gate_bias=(m_gate_bias, v_gate_bias), gla_norm_g=(m_gla_norm_g, v_gla_norm_g),
                w_out=(m_w_out, v_w_out), ln1_g=(m_ln1_g, v_ln1_g), ln1_b=(m_ln1_b, v_ln1_b),
                w_ff1=(m_w_ff1, v_w_ff1), w_ff2=(m_w_ff2, v_w_ff2), ln2_g=(m_ln2_g, v_ln2_g),
                ln2_b=(m_ln2_b, v_ln2_b))
    names = list(weights)
    big_names = ("w_in", "w_out", "w_ff1", "w_ff2")
    delta, new_m, new_v = {}, {}, {}
    for k in big_names:
        shp = weights[k].shape
        two = lambda a: a.reshape(shp[-2], shp[-1])
        dl, mn, vn = _adamw(two(weights[k]), two(grads[k]), two(moms[k][0]), two(moms[k][1]))
        delta[k], new_m[k], new_v[k] = dl.reshape(shp), mn.reshape(shp), vn.reshape(shp)
    small_names = [k for k in names if k not in big_names]
    sizes = [weights[k].size for k in small_names]
    total = sum(sizes)
    padded = -(-total // 1024) * 1024

    def flat(get):
        v = jnp.concatenate([get(k).reshape(-1) for k in small_names])
        return jnp.pad(v, (0, padded - total)).reshape(padded // 128, 128)

    ones = jnp.ones((padded - total,), F32)
    vflat = jnp.concatenate([jnp.concatenate([moms[k][1].reshape(-1) for k in small_names]), ones])
    dl, mn, vn = _adamw(flat(lambda k: weights[k]), flat(lambda k: grads[k]), flat(lambda k: moms[k][0]),
                        vflat.reshape(padded // 128, 128))
    pos = 0
    for k, sz in zip(small_names, sizes):
        shp = weights[k].shape
        delta[k] = dl.reshape(-1)[pos:pos + sz].reshape(shp)
        new_m[k] = mn.reshape(-1)[pos:pos + sz].reshape(shp)
        new_v[k] = vn.reshape(-1)[pos:pos + sz].reshape(shp)
        pos += sz

    return (loss, grad_x, *[grads[k] for k in names], *[delta[k] for k in names],
            *[new_m[k] for k in names], *[new_v[k] for k in names])
```

```python
import functools

import jax
import jax.numpy as jnp
from jax import lax
from jax.experimental import pallas as pl
from jax.experimental.pallas import tpu as pltpu

F32 = jnp.float32
BF16 = jnp.bfloat16

D_MODEL = 1024
N_META = 16
D_CONV = 512
CONV_WIDTH = 31
GLA_HEADS = 4
GLA_DV = 128
GLA_DK = 64
GLA_RANK = 16
GLA_TAU = 16.0
CHUNK = 64
D_FF = 4096
LN_EPS = 1e-5
ALPHA = 2.0 ** 0.25
D_IN = 2576
D_IN_PAD = 2688
PAD = CHUNK - N_META
HEAD = PAD + N_META
Q_SCALE = GLA_DK ** -0.5
ADAM_LR, ADAM_B1, ADAM_B2, ADAM_EPS, ADAM_WD, ADAM_STEP = 0.001, 0.9, 0.999, 1e-08, 0.01, 10
HALF = D_MODEL // 2
VMEM_LIMIT = 56 * 1024 * 1024
MESH = pl.DeviceIdType.MESH

C_VAL, C_GATE, C_Q, C_K, C_V, C_R, C_GD = 0, 512, 1024, 1280, 1536, 2048, 2560

_SMALL_FIELDS = (("loss", 8), ("ln_in_g", 8), ("ln_in_b", 8), ("conv_b", 4), ("conv_ln_g", 4), ("conv_ln_b", 4),
                 ("gate_bias", 2), ("gla_norm_g", 1), ("ln1_g", 8), ("ln1_b", 8), ("ln2_g", 8), ("ln2_b", 8),
                 ("conv_w", 124), ("gate_up", 32), ("meta_tokens", 128))
SMALL_ROWS = 360


def _params(sem=None, **kw):
    return pltpu.CompilerParams(dimension_semantics=sem, vmem_limit_bytes=VMEM_LIMIT, **kw)


def _row_tile(tp):
    for t in (704, 352, 192, 64):
        if tp % t == 0:
            return t
    raise ValueError(tp)


def _dot(a, b, dims, precision=None):
    return lax.dot_general(a, b, (dims, ((), ())), preferred_element_type=F32, precision=precision)


def _nn(a, b, **kw):
    return _dot(a, b, ((1,), (0,)), **kw)


def _nt(a, b, **kw):
    return _dot(a, b, ((1,), (1,)), **kw)


def _tn(a, b, **kw):
    return _dot(a, b, ((0,), (0,)), **kw)


def _sigmoid(x):
    return 1.0 / (1.0 + jnp.exp(-x))


def _log_sigmoid(z):
    return jnp.minimum(z, 0.0) - jnp.log(1.0 + jnp.exp(-jnp.abs(z)))


def _ln_stats(t):
    mu = jnp.mean(t, axis=-1, keepdims=True)
    d = t - mu
    var = jnp.mean(d * d, axis=-1, keepdims=True)
    rstd = lax.rsqrt(var + LN_EPS)
    return d * rstd, rstd


def _ln_bwd(dxhat, xhat, rstd):
    m1 = jnp.mean(dxhat, axis=-1, keepdims=True)
    m2 = jnp.mean(dxhat * xhat, axis=-1, keepdims=True)
    return rstd * (dxhat - m1 - xhat * m2)


def _mesh_pos():
    return lax.axis_index("x"), lax.axis_index("y"), lax.axis_index("c")


ANY = pl.BlockSpec(memory_space=pl.ANY)


def _gather_weights(shards, small):
    n = len(shards)
    half = [s.shape[0] // 2 for s in shards]

    def body(*refs):
        ins, small_in = refs[:n], refs[n]
        outs, small_out = refs[n + 1:2 * n + 1], refs[2 * n + 1]
        send, recv, fsend, frecv, ssend, srecv, local = refs[2 * n + 2:]
        x, y, c = _mesh_pos()
        me = 2 * x + y
        sibling = (x, y, 1 - c)
        chips = [(1 - x, y), (x, 1 - y), (1 - x, 1 - y)]
        chip_idx = [2 * px + py for px, py in chips]
        mine = [pl.ds(pl.multiple_of(c * h, 16), h) for h in half]
        other = [pl.ds(pl.multiple_of((1 - c) * h, 16), h) for h in half]

        locals_ = [pltpu.make_async_copy(ins[a], outs[a].at[me], local.at[a]) for a in range(n)]
        locals_.append(pltpu.make_async_copy(small_in, small_out.at[me], local.at[n]))
        for cp in locals_:
            cp.start()

        def ici(a, k):
            return pltpu.make_async_remote_copy(
                src_ref=ins[a].at[mine[a], :], dst_ref=outs[a].at[me, mine[a], :],
                send_sem=send.at[a, k], recv_sem=recv.at[a, k], device_id=(*chips[k], c), device_id_type=MESH)

        def landed(a, k):
            return pltpu.make_async_remote_copy(
                src_ref=ins[a].at[mine[a], :], dst_ref=outs[a].at[chip_idx[k], mine[a], :],
                send_sem=send.at[a, k], recv_sem=recv.at[a, k], device_id=(*chips[k], c), device_id_type=MESH)

        def forward(a, k, rows):
            blk = outs[a].at[chip_idx[k], rows[a], :]
            return pltpu.make_async_remote_copy(
                src_ref=blk, dst_ref=blk, send_sem=fsend.at[a, k], recv_sem=frecv.at[a, k],
                device_id=sibling, device_id_type=MESH)

        def small_copy(k, slot):
            return pltpu.make_async_remote_copy(
                src_ref=small_in, dst_ref=small_out.at[slot], send_sem=ssend.at[k], recv_sem=srecv.at[k],
                device_id=(*chips[k], c), device_id_type=MESH)

        for a in range(n):
            for k in range(3):
                ici(a, k).start()
        for k in range(3):
            small_copy(k, me).start()
        for a in range(n):
            for k in range(3):
                landed(a, k).wait_recv()
                forward(a, k, mine).start()
        for a in range(n):
            for k in range(3):
                forward(a, k, other).wait_recv()
        for k in range(3):
            small_copy(k, chip_idx[k]).wait_recv()
        for a in range(n):
            for k in range(3):
                ici(a, k).wait_send()
                forward(a, k, mine).wait_send()
        for k in range(3):
            small_copy(k, me).wait_send()
        for cp in locals_:
            cp.wait()

    out_shape = [jax.ShapeDtypeStruct((4,) + s.shape, s.dtype) for s in shards]
    out_shape.append(jax.ShapeDtypeStruct((4,) + small.shape, small.dtype))
    res = pl.pallas_call(
        body, name="gather_weights", out_shape=out_shape,
        in_specs=[ANY] * (n + 1), out_specs=[ANY] * (n + 1),
        scratch_shapes=[pltpu.SemaphoreType.DMA((n, 3)), pltpu.SemaphoreType.DMA((n, 3)),
                        pltpu.SemaphoreType.DMA((n, 3)), pltpu.SemaphoreType.DMA((n, 3)),
                        pltpu.SemaphoreType.DMA((3,)), pltpu.SemaphoreType.DMA((3,)),
                        pltpu.SemaphoreType.DMA((n + 1,))],
    )(*shards, small)
    return res[:n], res[n]


def _pair_exchange(grads):
    n = len(grads)

    def body(*refs):
        ins, outs = refs[:n], refs[n:2 * n]
        send, recv = refs[2 * n:]
        x, y, c = _mesh_pos()
        other = pl.ds(pl.multiple_of((1 - c) * HALF, 128), HALF)
        cps = [pltpu.make_async_remote_copy(
            src_ref=ins[a].at[:, :, other], dst_ref=outs[a], send_sem=send.at[a], recv_sem=recv.at[a],
            device_id=(x, y, 1 - c), device_id_type=MESH) for a in range(n)]
        for cp in cps:
            cp.start()
        for cp in cps:
            cp.wait()

    return pl.pallas_call(
        body, name="grad_pair_exchange",
        out_shape=[jax.ShapeDtypeStruct(g.shape[:2] + (HALF,), g.dtype) for g in grads],
        in_specs=[ANY] * n, out_specs=[ANY] * n,
        scratch_shapes=[pltpu.SemaphoreType.DMA((n,)), pltpu.SemaphoreType.DMA((n,))],
    )(*grads)


def _chip_exchange(parts):
    n = len(parts)

    def body(*refs):
        ins, outs = refs[:n], refs[n:2 * n]
        send, recv = refs[2 * n:]
        x, y, c = _mesh_pos()
        chips = [(1 - x, y), (x, 1 - y), (1 - x, 1 - y)]
        cps = [pltpu.make_async_remote_copy(
            src_ref=ins[a].at[2 * px + py], dst_ref=outs[a].at[k], send_sem=send.at[a, k], recv_sem=recv.at[a, k],
            device_id=(px, py, c), device_id_type=MESH) for a in range(n) for k, (px, py) in enumerate(chips)]
        for cp in cps:
            cp.start()
        for cp in cps:
            cp.wait()

    return pl.pallas_call(
        body, name="grad_chip_exchange",
        out_shape=[jax.ShapeDtypeStruct((3,) + p.shape[1:], p.dtype) for p in parts],
        in_specs=[ANY] * n, out_specs=[ANY] * n,
        scratch_shapes=[pltpu.SemaphoreType.DMA((n, 3)), pltpu.SemaphoreType.DMA((n, 3))],
    )(*parts)


def _pair_allgather(halves):
    n = len(halves)

    def body(*refs):
        ins, outs = refs[:n], refs[n:2 * n]
        send, recv, local = refs[2 * n:]
        x, y, c = _mesh_pos()
        loc = [pltpu.make_async_copy(ins[a], outs[a].at[c], local.at[a]) for a in range(n)]
        cps = [pltpu.make_async_remote_copy(
            src_ref=ins[a], dst_ref=outs[a].at[c], send_sem=send.at[a], recv_sem=recv.at[a],
            device_id=(x, y, 1 - c), device_id_type=MESH) for a in range(n)]
        for cp in loc + cps:
            cp.start()
        for cp in cps + loc:
            cp.wait()

    return pl.pallas_call(
        body, name="grad_pair_allgather",
        out_shape=[jax.ShapeDtypeStruct((2,) + h.shape, h.dtype) for h in halves],
        in_specs=[ANY] * n, out_specs=[ANY] * n,
        scratch_shapes=[pltpu.SemaphoreType.DMA((n,)), pltpu.SemaphoreType.DMA((n,)), pltpu.SemaphoreType.DMA((n,))],
    )(*halves)


def _allgather_small(pack):
    m_per, ncol = pack.shape

    def body(x_ref, out_ref, send_sems, recv_sems, local_sem):
        x, y, c = _mesh_pos()
        me, sibling = (x, y, c), (x, y, 1 - c)
        chips = [(1 - x, y), (x, 1 - y), (1 - x, 1 - y)]

        def rows(px, py, pc):
            return out_ref.at[pl.ds(pl.multiple_of((4 * px + 2 * py + pc) * m_per, 8), m_per), :]

        def copy(k, block, to, src=None):
            return pltpu.make_async_remote_copy(
                src_ref=rows(*block) if src is None else src, dst_ref=rows(*block),
                send_sem=send_sems.at[k], recv_sem=recv_sems.at[k], device_id=to, device_id_type=MESH)

        mine = pltpu.make_async_copy(x_ref, rows(*me), local_sem)
        mine.start()
        first = [copy(0, me, sibling, src=x_ref)]
        first += [copy(1 + j, me, (*chip, c), src=x_ref) for j, chip in enumerate(chips)]
        for cp in first:
            cp.start()
        passed = [copy(4 + j, (*chip, c), sibling) for j, chip in enumerate(chips)]
        for j, chip in enumerate(chips):
            copy(1 + j, (*chip, c), me).wait_recv()
            passed[j].start()
        copy(0, sibling, me).wait_recv()
        for j, chip in enumerate(chips):
            copy(4 + j, (*chip, 1 - c), me).wait_recv()
        for cp in first + passed:
            cp.wait_send()
        mine.wait()

    return pl.pallas_call(
        body, name="allgather_small",
        out_shape=jax.ShapeDtypeStruct((8 * m_per, ncol), pack.dtype),
        in_specs=[pl.BlockSpec(memory_space=pltpu.VMEM)],
        out_specs=pl.BlockSpec(memory_space=pltpu.VMEM),
        scratch_shapes=[pltpu.SemaphoreType.DMA((7,)), pltpu.SemaphoreType.DMA((7,)), pltpu.SemaphoreType.DMA],
    )(pack)


def _add_pair(g, got, c_arr):
    _, rows, _ = g.shape

    def body(c_ref, g_ref, r_ref, o_ref):
        o_ref[...] = (g_ref[...] + r_ref[...]).astype(BF16)

    return pl.pallas_call(
        body, name="grad_add_pair", out_shape=jax.ShapeDtypeStruct((4, rows, HALF), BF16),
        grid_spec=pltpu.PrefetchScalarGridSpec(
            num_scalar_prefetch=1, grid=(4,),
            in_specs=[pl.BlockSpec((1, rows, HALF), lambda j, c: (j, 0, c[0])),
                      pl.BlockSpec((1, rows, HALF), lambda j, c: (j, 0, 0))],
            out_specs=pl.BlockSpec((1, rows, HALF), lambda j, c: (j, 0, 0))),
        compiler_params=_params(("arbitrary",)),
    )(c_arr, g, got)


def _add_chips(g, pair_got, chip_got, pos_arr):
    _, rows, _ = g.shape

    def body(pos_ref, g_ref, p_ref, r_ref, o_ref):
        own = g_ref[0] + p_ref[0]
        o_ref[...] = ((own + r_ref[0].astype(F32)) + r_ref[1].astype(F32)) + r_ref[2].astype(F32)

    return pl.pallas_call(
        body, name="grad_add_chips", out_shape=jax.ShapeDtypeStruct((rows, HALF), F32),
        grid_spec=pltpu.PrefetchScalarGridSpec(
            num_scalar_prefetch=1, grid=(1,),
            in_specs=[pl.BlockSpec((1, rows, HALF), lambda i, p: (p[0], 0, p[1])),
                      pl.BlockSpec((1, rows, HALF), lambda i, p: (p[0], 0, 0)),
                      pl.BlockSpec((3, rows, HALF), lambda i, p: (0, 0, 0))],
            out_specs=pl.BlockSpec((rows, HALF), lambda i, p: (0, 0))),
        compiler_params=_params(("arbitrary",)),
    )(pos_arr, g, pair_got, chip_got)


def _sum_small(gathered):
    def body(g_ref, o_ref):
        acc = g_ref[0:SMALL_ROWS, :]
        for d in range(1, 8):
            acc = acc + g_ref[d * SMALL_ROWS:(d + 1) * SMALL_ROWS, :]
        o_ref[...] = acc

    return pl.pallas_call(
        body, name="sum_small", out_shape=jax.ShapeDtypeStruct((SMALL_ROWS, 128), F32),
    )(gathered)


def _adamw(w, g, m, v):
    rows, cols = w.shape
    tr = 256 if rows % 256 == 0 else rows
    c1 = 1.0 - ADAM_B1 ** ADAM_STEP
    c2 = 1.0 - ADAM_B2 ** ADAM_STEP

    def body(w_ref, g_ref, m_ref, v_ref, d_ref, mo_ref, vo_ref):
        gg = g_ref[...]
        mn = ADAM_B1 * m_ref[...] + (1.0 - ADAM_B1) * gg
        vn = ADAM_B2 * v_ref[...] + (1.0 - ADAM_B2) * (gg * gg)
        d_ref[...] = -ADAM_LR * ((mn / c1) / (jnp.sqrt(vn / c2) + ADAM_EPS) + ADAM_WD * w_ref[...])
        mo_ref[...] = mn
        vo_ref[...] = vn

    spec = pl.BlockSpec((tr, cols), lambda i: (i, 0))
    return pl.pallas_call(
        body, name="adamw", out_shape=[jax.ShapeDtypeStruct(w.shape, F32)] * 3,
        grid=(rows // tr,), in_specs=[spec] * 4, out_specs=[spec] * 3,
        compiler_params=_params(("parallel",)),
    )(w, g, m, v)


def _ln_in_fwd(x, meta, g, b):
    bsz, s, d = x.shape
    tp = s + HEAD
    nh = 2
    sh = s // nh
    rc = min(256, sh)

    def body(x_ref, meta_ref, g_ref, b_ref, s0_ref, s0b_ref):
        h = pl.program_id(1)
        gg, bb = g_ref[...], b_ref[...]

        @pl.when(h == 0)
        def _():
            s0_ref[0, 0:PAD, :] = jnp.zeros((PAD, d), F32)
            s0b_ref[0, 0:PAD, :] = jnp.zeros((PAD, d), BF16)
            mh, _ = _ln_stats(meta_ref[...])
            mv = mh * gg + bb
            s0_ref[0, PAD:HEAD, :] = mv
            s0b_ref[0, PAD:HEAD, :] = mv.astype(BF16)

        def step(i, carry):
            src = pl.ds(pl.multiple_of(i * rc, rc), rc)
            dst = pl.ds(pl.multiple_of(HEAD + h * sh + i * rc, 64), rc)
            xh, _ = _ln_stats(x_ref[0, src, :])
            val = xh * gg + bb
            s0_ref[0, dst, :] = val
            s0b_ref[0, dst, :] = val.astype(BF16)
            return carry

        lax.fori_loop(0, sh // rc, step, 0)

    full = lambda bi, hi: (bi, 0, 0)
    return pl.pallas_call(
        body, name="ln_in_fwd",
        out_shape=[jax.ShapeDtypeStruct((bsz, tp, d), F32), jax.ShapeDtypeStruct((bsz, tp, d), BF16)],
        grid=(bsz, nh),
        in_specs=[pl.BlockSpec((1, sh, d), lambda bi, hi: (bi, hi, 0)),
                  pl.BlockSpec((N_META, d), lambda bi, hi: (0, 0)),
                  pl.BlockSpec((1, d), lambda bi, hi: (0, 0)),
                  pl.BlockSpec((1, d), lambda bi, hi: (0, 0))],
        out_specs=[pl.BlockSpec((1, tp, d), full)] * 2,
        compiler_params=_params(("parallel", "arbitrary")),
    )(x, meta, g, b)


def _in_proj(s0b, w_int):
    r, d = s0b.shape
    tm = _row_tile(r)

    def body(a_ref, w_ref, o_ref):
        o_ref[...] = _nt(a_ref[...], w_ref[...])

    return pl.pallas_call(
        body, name="in_proj", out_shape=jax.ShapeDtypeStruct((r, D_IN_PAD), F32),
        grid=(r // tm,),
        in_specs=[pl.BlockSpec((tm, d), lambda i: (i, 0)), pl.BlockSpec((D_IN_PAD, d), lambda i: (0, 0))],
        out_specs=pl.BlockSpec((tm, D_IN_PAD), lambda i: (i, 0)),
        compiler_params=_params(("parallel",)),
    )(s0b, w_int)


def _conv_fwd(u, conv_w, conv_b):
    bsz, tp, _ = u.shape
    nchunk = tp // CHUNK
    win = CHUNK + 32
    nct = D_CONV // 128

    def body(cv_ref, cg_ref, w_ref, cb_ref, hc_ref, h_scr, win_scr):
        h_scr[0:32, :] = jnp.zeros((32, 128), F32)
        h_scr[32:32 + tp, :] = cv_ref[0] * _sigmoid(cg_ref[0])
        cb = cb_ref[...]

        def step(n, carry):
            r0 = pl.multiple_of(n * CHUNK, CHUNK)
            win_scr[...] = h_scr[pl.ds(r0, win), :]
            acc = jnp.zeros((CHUNK, 128), F32)
            for j in range(CONV_WIDTH):
                acc = acc + w_ref[j:j + 1, :] * win_scr[2 + j:2 + j + CHUNK, :]
            hc_ref[0, pl.ds(r0, CHUNK), :] = acc + cb
            return carry

        lax.fori_loop(0, nchunk, step, 0)

    return pl.pallas_call(
        body, name="conv_fwd", out_shape=jax.ShapeDtypeStruct((bsz, tp, D_CONV), F32),
        grid=(bsz, nct),
        in_specs=[pl.BlockSpec((1, tp, 128), lambda bi, t: (bi, 0, C_VAL // 128 + t)),
                  pl.BlockSpec((1, tp, 128), lambda bi, t: (bi, 0, C_GATE // 128 + t)),
                  pl.BlockSpec((32, 128), lambda bi, t: (0, t)),
                  pl.BlockSpec((1, 128), lambda bi, t: (0, t))],
        out_specs=pl.BlockSpec((1, tp, 128), lambda bi, t: (bi, 0, t)),
        scratch_shapes=[pltpu.VMEM((tp + 32, 128), F32), pltpu.VMEM((win, 128), F32)],
        compiler_params=_params(("parallel", "parallel")),
    )(u, u, conv_w, conv_b)


def _gla_consts():
    row = lax.broadcasted_iota(jnp.int32, (CHUNK, CHUNK), 0)
    col = lax.broadcasted_iota(jnp.int32, (CHUNK, CHUNK), 1)
    lane = lax.broadcasted_iota(jnp.int32, (1, 128), 1)
    return row >= col, [lane < GLA_DK, lane >= GLA_DK]


def _gla_triu():
    row = lax.broadcasted_iota(jnp.int32, (CHUNK, CHUNK), 0)
    col = lax.broadcasted_iota(jnp.int32, (CHUNK, CHUNK), 1)
    return (row <= col).astype(F32)


def _gla_chunk_terms(n, q_ref, k_ref, gd_ref, gup_ref, gb_ref, tri):
    r0 = pl.multiple_of(n * CHUNK, CHUNK)
    rows = pl.ds(r0, CHUNK)
    z = _nn(gd_ref[0, rows, :].astype(BF16), gup_ref[...]) + gb_ref[...]
    rowid = n * CHUNK + lax.broadcasted_iota(jnp.int32, (CHUNK, 1), 0)
    valid = rowid >= PAD
    lg = jnp.where(valid, _log_sigmoid(z) * (1.0 / GLA_TAU), 0.0)
    bcum = _nn(tri.astype(F32), lg, precision=lax.Precision.HIGHEST)
    blast = bcum[CHUNK - 1:CHUNK, :]
    eb = jnp.exp(bcum)
    enb = jnp.exp(-bcum)
    erest = jnp.exp(blast - bcum)
    q = q_ref[0, rows, :] * Q_SCALE
    k = k_ref[0, rows, :]
    return rows, valid, z, eb, enb, erest, jnp.exp(blast), q * eb, k * enb, k * erest


def _gla_fwd(u, gup, gbias, gnorm):
    bsz, tp, _ = u.shape
    nchunk = tp // CHUNK

    def body(q_ref, k_ref, v_ref, r_ref, gd_ref, gup_ref, gb_ref, gn_ref, out_ref, o_ref, st_ref, s_scr):
        tri, hmask = _gla_consts()
        s_scr[...] = jnp.zeros_like(s_scr)
        gn = gn_ref[...]

        def step(n, carry):
            rows, _, _, _, _, _, dec, qe, ke, kd = _gla_chunk_terms(n, q_ref, k_ref, gd_ref, gup_ref, gb_ref, tri)
            keb, kdb = ke.astype(BF16), kd.astype(BF16)
            for h in range(2):
                cols = slice(h * GLA_DV, (h + 1) * GLA_DV)
                qh = jnp.where(hmask[h], qe, 0.0).astype(BF16)
                vh = v_ref[0, rows, cols].astype(BF16)
                a = jnp.where(tri, _nt(qh, keb), 0.0)
                st = s_scr[h]
                st_ref[0, h, n] = st
                o = _nn(a.astype(BF16), vh) + _nt(qh, st.astype(BF16))
                s_scr[h] = dec * st + _tn(vh, kdb)
                o_ref[0, rows, cols] = o
                rms = lax.rsqrt(jnp.mean(o * o, axis=-1, keepdims=True) + LN_EPS)
                rh = r_ref[0, rows, cols]
                out_ref[0, rows, cols] = (o * rms * gn * (rh * _sigmoid(rh))).astype(BF16)
            return carry

        lax.fori_loop(0, nchunk, step, 0)

    return pl.pallas_call(
        body, name="gla_fwd",
        out_shape=[jax.ShapeDtypeStruct((bsz, tp, 512), BF16), jax.ShapeDtypeStruct((bsz, tp, 512), F32),
                   jax.ShapeDtypeStruct((bsz, GLA_HEADS, nchunk, GLA_DV, 128), F32)],
        grid=(bsz, 2),
        in_specs=[pl.BlockSpec((1, tp, 128), lambda bi, p: (bi, 0, C_Q // 128 + p)),
                  pl.BlockSpec((1, tp, 128), lambda bi, p: (bi, 0, C_K // 128 + p)),
                  pl.BlockSpec((1, tp, 256), lambda bi, p: (bi, 0, C_V // 256 + p)),
                  pl.BlockSpec((1, tp, 256), lambda bi, p: (bi, 0, C_R // 256 + p)),
                  pl.BlockSpec((1, tp, 128), lambda bi, p: (bi, 0, C_GD // 128)),
                  pl.BlockSpec((128, 128), lambda bi, p: (0, p)),
                  pl.BlockSpec((1, 128), lambda bi, p: (0, p)),
                  pl.BlockSpec((1, 128), lambda bi, p: (0, 0))],
        out_specs=[pl.BlockSpec((1, tp, 256), lambda bi, p: (bi, 0, p)),
                   pl.BlockSpec((1, tp, 256), lambda bi, p: (bi, 0, p)),
                   pl.BlockSpec((1, 2, nchunk, GLA_DV, 128), lambda bi, p: (bi, p, 0, 0, 0))],
        scratch_shapes=[pltpu.VMEM((2, GLA_DV, 128), F32)],
        compiler_params=_params(("parallel", "parallel")),
    )(u, u, u, u, u, gup, gbias, gnorm)


def _out_proj_ln1(hc, gla_out, w_out, s0, cg, cb, g1, b1):
    r, d = s0.shape
    tm = _row_tile(r)

    def body(hc_ref, a_ref, w_ref, s0_ref, cg_ref, cb_ref, g_ref, b_ref, co_ref, xh_ref, rstd_ref, s1b_ref):
        xc, _ = _ln_stats(hc_ref[...])
        nv = xc * cg_ref[...] + cb_ref[...]
        co = (nv * _sigmoid(nv)).astype(BF16)
        co_ref[...] = co
        mix = _nn(co, w_ref[0:D_CONV, :]) + _nn(a_ref[...], w_ref[D_CONV:, :])
        xh, rstd = _ln_stats(ALPHA * s0_ref[...] + mix)
        xh_ref[...] = xh
        rstd_ref[...] = rstd
        s1b_ref[...] = (xh * g_ref[...] + b_ref[...]).astype(BF16)

    row = lambda n: pl.BlockSpec((tm, n), lambda i: (i, 0))
    vec = lambda n: pl.BlockSpec((1, n), lambda i: (0, 0))
    return pl.pallas_call(
        body, name="out_proj_ln1",
        out_shape=[jax.ShapeDtypeStruct((r, D_CONV), BF16), jax.ShapeDtypeStruct((r, d), F32),
                   jax.ShapeDtypeStruct((r, 1), F32), jax.ShapeDtypeStruct((r, d), BF16)],
        grid=(r // tm,),
        in_specs=[row(D_CONV), row(512), pl.BlockSpec((d, d), lambda i: (0, 0)), row(d),
                  vec(D_CONV), vec(D_CONV), vec(d), vec(d)],
        out_specs=[row(D_CONV), row(d), row(1), row(d)],
        compiler_params=_params(("parallel",)),
    )(hc, gla_out, w_out, s0, cg, cb, g1, b1)


def _ffn1(s1b, w1t):
    r, d = s1b.shape
    tm = _row_tile(r)

    def body(a_ref, w_ref, o_ref):
        o_ref[...] = jnp.maximum(_nt(a_ref[...], w_ref[...]), 0.0).astype(BF16)

    return pl.pallas_call(
        body, name="ffn1", out_shape=jax.ShapeDtypeStruct((r, D_FF), BF16),
        grid=(r // tm,),
        in_specs=[pl.BlockSpec((tm, d), lambda i: (i, 0)), pl.BlockSpec((D_FF, d), lambda i: (0, 0))],
        out_specs=pl.BlockSpec((tm, D_FF), lambda i: (i, 0)),
        compiler_params=_params(("parallel",)),
    )(s1b, w1t)


def _ffn2_ln2_loss(ra, w2, xhat1, g1, b1, g2, b2, tgt, tp):
    r, d = xhat1.shape
    tm = _row_tile(tp)
    per = tp // tm

    def body(ra_ref, w_ref, xh1_ref, g1_ref, b1_ref, g2_ref, b2_ref, t_ref, dt_ref, dtb_ref, acc_ref):
        i = pl.program_id(0)

        @pl.when(i == 0)
        def _():
            acc_ref[...] = jnp.zeros_like(acc_ref)

        rf = ra_ref[...].astype(F32)
        f = _nn((rf * rf).astype(BF16), w_ref[...])
        s1 = xh1_ref[...] * g1_ref[...] + b1_ref[...]
        xh2, rstd2 = _ln_stats(ALPHA * s1 + f)
        y = xh2 * g2_ref[...] + b2_ref[...]
        rowid = (i % per) * tm + lax.broadcasted_iota(jnp.int32, (tm, 1), 0)
        e = jnp.where(rowid >= HEAD, y - t_ref[...], 0.0)
        dy = e * (1.0 / d)
        dt2 = _ln_bwd(dy * g2_ref[...], xh2, rstd2)
        dt_ref[...] = dt2
        dtb_ref[...] = dt2.astype(BF16)
        acc_ref[0:1, :] += (0.5 / d) * jnp.sum(e * e, axis=0, keepdims=True)
        acc_ref[1:2, :] += jnp.sum(dy * xh2, axis=0, keepdims=True)
        acc_ref[2:3, :] += jnp.sum(dy, axis=0, keepdims=True)

    row = lambda n: pl.BlockSpec((tm, n), lambda i: (i, 0))
    vec = pl.BlockSpec((1, d), lambda i: (0, 0))
    return pl.pallas_call(
        body, name="ffn2_ln2_loss",
        out_shape=[jax.ShapeDtypeStruct((r, d), F32), jax.ShapeDtypeStruct((r, d), BF16),
                   jax.ShapeDtypeStruct((8, d), F32)],
        grid=(r // tm,),
        in_specs=[row(D_FF), pl.BlockSpec((D_FF, d), lambda i: (0, 0)), row(d), vec, vec, vec, vec, row(d)],
        out_specs=[row(d), row(d), pl.BlockSpec((8, d), lambda i: (0, 0))],
        compiler_params=_params(("arbitrary",)),
    )(ra, w2, xhat1, g1, b1, g2, b2, tgt)


def _ffn_bwd_da(dt2b, w2, ra):
    r, d = dt2b.shape
    tm = _row_tile(r)

    def body(g_ref, w_ref, ra_ref, o_ref):
        o_ref[...] = (_nt(g_ref[...], w_ref[...]) * (2.0 * ra_ref[...].astype(F32))).astype(BF16)

    return pl.pallas_call(
        body, name="ffn_bwd_da", out_shape=jax.ShapeDtypeStruct((r, D_FF), BF16),
        grid=(r // tm,),
        in_specs=[pl.BlockSpec((tm, d), lambda i: (i, 0)), pl.BlockSpec((D_FF, d), lambda i: (0, 0)),
                  pl.BlockSpec((tm, D_FF), lambda i: (i, 0))],
        out_specs=pl.BlockSpec((tm, D_FF), lambda i: (i, 0)),
        compiler_params=_params(("parallel",)),
    )(dt2b, w2, ra)


def _ffn_bwd_ln1(da, w1t, dt2, xhat1, rstd1, g1):
    r, d = dt2.shape
    tm = _row_tile(r)

    def body(da_ref, w_ref, dt2_ref, xh_ref, rstd_ref, g_ref, dt_ref, dtb_ref, acc_ref):
        @pl.when(pl.program_id(0) == 0)
        def _():
            acc_ref[...] = jnp.zeros_like(acc_ref)

        ds1 = ALPHA * dt2_ref[...] + _nn(da_ref[...], w_ref[...])
        xh = xh_ref[...]
        dt1 = _ln_bwd(ds1 * g_ref[...], xh, rstd_ref[...])
        dt_ref[...] = dt1
        dtb_ref[...] = dt1.astype(BF16)
        acc_ref[0:1, :] += jnp.sum(ds1 * xh, axis=0, keepdims=True)
        acc_ref[1:2, :] += jnp.sum(ds1, axis=0, keepdims=True)

    row = lambda n: pl.BlockSpec((tm, n), lambda i: (i, 0))
    return pl.pallas_call(
        body, name="ffn_bwd_ln1",
        out_shape=[jax.ShapeDtypeStruct((r, d), F32), jax.ShapeDtypeStruct((r, d), BF16),
                   jax.ShapeDtypeStruct((8, d), F32)],
        grid=(r // tm,),
        in_specs=[row(D_FF), pl.BlockSpec((D_FF, d), lambda i: (0, 0)), row(d), row(d), row(1),
                  pl.BlockSpec((1, d), lambda i: (0, 0))],
        out_specs=[row(d), row(d), pl.BlockSpec((8, d), lambda i: (0, 0))],
        compiler_params=_params(("arbitrary",)),
    )(da, w1t, dt2, xhat1, rstd1, g1)


def _matmul_tn(lhs, rhs, bm, square_lhs=False, name="matmul_tn"):
    r, m = lhs.shape
    n = rhs.shape[1]
    tk = _row_tile(r)

    def body(a_ref, b_ref, o_ref):
        @pl.when(pl.program_id(1) == 0)
        def _():
            o_ref[...] = jnp.zeros_like(o_ref)

        a = a_ref[...]
        if square_lhs:
            af = a.astype(F32)
            a = (af * af).astype(BF16)
        o_ref[...] += _tn(a, b_ref[...])

    return pl.pallas_call(
        body, name=name, out_shape=jax.ShapeDtypeStruct((m, n), F32),
        grid=(m // bm, r // tk),
        in_specs=[pl.BlockSpec((tk, bm), lambda i, k: (k, i)), pl.BlockSpec((tk, n), lambda i, k: (k, 0))],
        out_specs=pl.BlockSpec((bm, n), lambda i, k: (i, 0)),
        compiler_params=_params(("parallel", "arbitrary")),
    )(lhs, rhs)


def _out_proj_bwd(dt1b, w_out, hc, cg, cb):
    r, d = dt1b.shape
    tm = _row_tile(r)

    def body(g_ref, w_ref, hc_ref, cg_ref, cb_ref, dhc_ref, dgla_ref, acc_ref):
        @pl.when(pl.program_id(0) == 0)
        def _():
            acc_ref[...] = jnp.zeros_like(acc_ref)

        dmix = _nt(g_ref[...], w_ref[...])
        dgla_ref[...] = dmix[:, D_CONV:]
        gg = cg_ref[...]
        xh, rstd = _ln_stats(hc_ref[...])
        nv = xh * gg + cb_ref[...]
        sig = _sigmoid(nv)
        dn = dmix[:, :D_CONV] * (sig * (1.0 + nv * (1.0 - sig)))
        dhc = _ln_bwd(dn * gg, xh, rstd)
        dhc_ref[...] = dhc
        acc_ref[0:1, :] += jnp.sum(dhc, axis=0, keepdims=True)
        acc_ref[1:2, :] += jnp.sum(dn * xh, axis=0, keepdims=True)
        acc_ref[2:3, :] += jnp.sum(dn, axis=0, keepdims=True)

    row = lambda n: pl.BlockSpec((tm, n), lambda i: (i, 0))
    vec = pl.BlockSpec((1, D_CONV), lambda i: (0, 0))
    return pl.pallas_call(
        body, name="out_proj_bwd",
        out_shape=[jax.ShapeDtypeStruct((r, D_CONV), F32), jax.ShapeDtypeStruct((r, 512), F32),
                   jax.ShapeDtypeStruct((8, D_CONV), F32)],
        grid=(r // tm,),
        in_specs=[row(d), pl.BlockSpec((d, d), lambda i: (0, 0)), row(D_CONV), vec, vec],
        out_specs=[row(D_CONV), row(512), pl.BlockSpec((8, D_CONV), lambda i: (0, 0))],
        compiler_params=_params(("arbitrary",)),
    )(dt1b, w_out, hc, cg, cb)


def _conv_bwd(dhc, u, conv_w):
    bsz, tp, _ = u.shape
    nchunk = tp // CHUNK
    win = CHUNK + 32
    nct = D_CONV // 128

    def body(dhc_ref, cv_ref, cg_ref, w_ref, dv_ref, dg_ref, dw_ref, h_scr, dhc_scr, hwin, dwin, dw_scr):
        h_scr[0:32, :] = jnp.zeros((32, 128), F32)
        h_scr[32:32 + tp, :] = cv_ref[0] * _sigmoid(cg_ref[0])
        dhc_scr[0:tp, :] = dhc_ref[0]
        dhc_scr[tp:tp + 32, :] = jnp.zeros((32, 128), F32)
        dw_scr[...] = jnp.zeros_like(dw_scr)

        def step(n, carry):
            r0 = pl.multiple_of(n * CHUNK, CHUNK)
            rows = pl.ds(r0, CHUNK)
            hwin[...] = h_scr[pl.ds(r0, win), :]
            dwin[...] = dhc_scr[pl.ds(r0, win), :]
            dcur = dwin[0:CHUNK, :]
            acc = jnp.zeros((CHUNK, 128), F32)
            for j in range(CONV_WIDTH):
                acc = acc + w_ref[j:j + 1, :] * dwin[30 - j:30 - j + CHUNK, :]
                prod = dcur * hwin[2 + j:2 + j + CHUNK, :]
                dw_scr[j * 8:(j + 1) * 8, :] += jnp.sum(prod.reshape(CHUNK // 8, 8, 128), axis=0)
            cg = cg_ref[0, rows, :]
            sig = _sigmoid(cg)
            rowid = n * CHUNK + lax.broadcasted_iota(jnp.int32, (CHUNK, 1), 0)
            dh = jnp.where(rowid >= PAD, acc, 0.0)
            dv_ref[0, rows, :] = (dh * sig).astype(BF16)
            dg_ref[0, rows, :] = (dh * cv_ref[0, rows, :] * sig * (1.0 - sig)).astype(BF16)
            return carry

        lax.fori_loop(0, nchunk, step, 0)
        dw_ref[0] = jnp.zeros((32, 128), F32)
        for j in range(CONV_WIDTH):
            dw_ref[0, j:j + 1, :] = jnp.sum(dw_scr[j * 8:(j + 1) * 8, :], axis=0, keepdims=True)

    blk = lambda off: pl.BlockSpec((1, tp, 128), lambda bi, t: (bi, 0, off // 128 + t))
    return pl.pallas_call(
        body, name="conv_bwd",
        out_shape=[jax.ShapeDtypeStruct((bsz, tp, D_CONV), BF16), jax.ShapeDtypeStruct((bsz, tp, D_CONV), BF16),
                   jax.ShapeDtypeStruct((bsz, 32, D_CONV), F32)],
        grid=(bsz, nct),
        in_specs=[blk(0), blk(C_VAL), blk(C_GATE), pl.BlockSpec((32, 128), lambda bi, t: (0, t))],
        out_specs=[blk(0), blk(0), pl.BlockSpec((1, 32, 128), lambda bi, t: (bi, 0, t))],
        scratch_shapes=[pltpu.VMEM((tp + 32, 128), F32), pltpu.VMEM((tp + 32, 128), F32),
                        pltpu.VMEM((win, 128), F32), pltpu.VMEM((win, 128), F32),
                        pltpu.VMEM((CONV_WIDTH * 8, 128), F32)],
        compiler_params=_params(("parallel", "parallel")),
    )(dhc, u, u, conv_w)


def _gla_bwd(dgla, u, o_pre, states, gup, gbias, gnorm):
    bsz, tp, _ = u.shape
    nchunk = tp // CHUNK

    def body(dy_ref, q_ref, k_ref, v_ref, r_ref, gd_ref, o_ref, st_ref, gup_ref, gb_ref, gn_ref,
             dq_ref, dk_ref, dv_ref, dr_ref, dgd_ref, dgup_ref, vec_ref, h_scr, gup_acc):
        tri, hmask = _gla_consts()
        triu = _gla_triu()
        h_scr[...] = jnp.zeros_like(h_scr)
        gup_acc[...] = jnp.zeros_like(gup_acc)
        gn = gn_ref[...]
        gupb = gup_ref[...]

        def step(i, carry):
            dbias, dgn = carry
            n = nchunk - 1 - i
            rows, valid, z, eb, enb, erest, dec, qe, ke, kd = _gla_chunk_terms(
                n, q_ref, k_ref, gd_ref, gup_ref, gb_ref, tri)
            keb, kdb = ke.astype(BF16), kd.astype(BF16)
            dqe = jnp.zeros((CHUNK, 128), F32)
            dke = jnp.zeros((CHUNK, 128), F32)
            dkd = jnp.zeros((CHUNK, 128), F32)
            ddec = jnp.zeros((1, 128), F32)
            for h in range(2):
                cols = slice(h * GLA_DV, (h + 1) * GLA_DV)
                o = o_ref[0, rows, cols]
                rh = r_ref[0, rows, cols]
                dy = dy_ref[0, rows, cols]
                rms = lax.rsqrt(jnp.mean(o * o, axis=-1, keepdims=True) + LN_EPS)
                nrm = o * rms
                sig = _sigmoid(rh)
                sw = rh * sig
                dr_ref[0, rows, cols] = (dy * nrm * gn * (sig * (1.0 + rh * (1.0 - sig)))).astype(BF16)
                dgn = dgn + jnp.sum(dy * nrm * sw, axis=0, keepdims=True)
                dn = dy * gn * sw
                do = rms * (dn - nrm * jnp.mean(dn * nrm, axis=-1, keepdims=True))
                dob = do.astype(BF16)
                qh = jnp.where(hmask[h], qe, 0.0).astype(BF16)
                vh = v_ref[0, rows, cols].astype(BF16)
                a = jnp.where(tri, _nt(qh, keb), 0.0).astype(BF16)
                st = st_ref[0, h, n]
                ht = h_scr[h]
                da = jnp.where(tri, _nt(dob, vh), 0.0).astype(BF16)
                dqe = dqe + jnp.where(hmask[h], _nn(da, keb) + _nn(dob, st.astype(BF16)), 0.0)
                dke = dke + _tn(da, qh)
                dv_ref[0, rows, cols] = (_tn(a, dob) + _nt(kdb, ht.astype(BF16))).astype(BF16)
                dkd = dkd + jnp.where(hmask[h], _nn(vh, ht.astype(BF16)), 0.0)
                ddec = ddec + jnp.where(hmask[h], jnp.sum(ht * st, axis=0, keepdims=True), 0.0)
                h_scr[h] = dec * ht + _tn(dob, qh)
            dq_ref[0, rows, :] = (dqe * eb * Q_SCALE).astype(BF16)
            dk_ref[0, rows, :] = (dke * enb + dkd * erest).astype(BF16)
            db = dqe * qe - dke * ke - dkd * kd
            dblast = jnp.sum(dkd * kd, axis=0, keepdims=True) + ddec * dec
            lastrow = lax.broadcasted_iota(jnp.int32, (CHUNK, 1), 0) == CHUNK - 1
            db = db + jnp.where(lastrow, dblast, 0.0)
            dlg = jnp.where(valid, _nn(triu, db, precision=lax.Precision.HIGHEST), 0.0)
            dz = dlg * (1.0 / GLA_TAU) * (1.0 - _sigmoid(z))
            dzb = dz.astype(BF16)
            dgd_ref[0, 0, rows, :] = _nt(dzb, gupb).astype(BF16)
            gup_acc[...] += _tn(gd_ref[0, rows, :].astype(BF16), dzb)
            return dbias + jnp.sum(dz, axis=0, keepdims=True), dgn

        zero = jnp.zeros((1, 128), F32)
        dbias, dgn = lax.fori_loop(0, nchunk, step, (zero, zero))
        dgup_ref[0] = gup_acc[...]
        vec_ref[0] = jnp.zeros((8, 128), F32)
        vec_ref[0, 0:1, :] = dbias
        vec_ref[0, 1:2, :] = dgn

    pair = lambda w, off: pl.BlockSpec((1, tp, w), lambda bi, p: (bi, 0, off // w + p))
    return pl.pallas_call(
        body, name="gla_bwd",
        out_shape=[jax.ShapeDtypeStruct((bsz, tp, 256), BF16), jax.ShapeDtypeStruct((bsz, tp, 256), BF16),
                   jax.ShapeDtypeStruct((bsz, tp, 512), BF16), jax.ShapeDtypeStruct((bsz, tp, 512), BF16),
                   jax.ShapeDtypeStruct((bsz, 2, tp, 128), BF16), jax.ShapeDtypeStruct((bsz, 128, 256), F32),
                   jax.ShapeDtypeStruct((bsz, 8, 256), F32)],
        grid=(bsz, 2),
        in_specs=[pair(256, 0), pair(128, C_Q), pair(128, C_K), pair(256, C_V), pair(256, C_R),
                  pl.BlockSpec((1, tp, 128), lambda bi, p: (bi, 0, C_GD // 128)),
                  pair(256, 0),
                  pl.BlockSpec((1, 2, nchunk, GLA_DV, 128), lambda bi, p: (bi, p, 0, 0, 0)),
                  pl.BlockSpec((128, 128), lambda bi, p: (0, p)),
                  pl.BlockSpec((1, 128), lambda bi, p: (0, p)),
                  pl.BlockSpec((1, 128), lambda bi, p: (0, 0))],
        out_specs=[pair(128, 0), pair(128, 0), pair(256, 0), pair(256, 0),
                   pl.BlockSpec((1, 1, tp, 128), lambda bi, p: (bi, p, 0, 0)),
                   pl.BlockSpec((1, 128, 128), lambda bi, p: (bi, 0, p)),
                   pl.BlockSpec((1, 8, 128), lambda bi, p: (bi, 0, p))],
        scratch_shapes=[pltpu.VMEM((2, GLA_DV, 128), F32), pltpu.VMEM((128, 128), F32)],
        compiler_params=_params(("parallel", "parallel")),
    )(dgla, u, u, u, u, u, o_pre, states, gup, gbias, gnorm)


def _in_proj_bwd(du, w_int, dt1):
    r, d = dt1.shape
    tm = _row_tile(r)

    def body(g_ref, w_ref, dt_ref, o_ref):
        o_ref[...] = ALPHA * dt_ref[...] + _nn(g_ref[...], w_ref[...])

    return pl.pallas_call(
        body, name="in_proj_bwd", out_shape=jax.ShapeDtypeStruct((r, d), F32),
        grid=(r // tm,),
        in_specs=[pl.BlockSpec((tm, D_IN_PAD), lambda i: (i, 0)), pl.BlockSpec((D_IN_PAD, d), lambda i: (0, 0)),
                  pl.BlockSpec((tm, d), lambda i: (i, 0))],
        out_specs=pl.BlockSpec((tm, d), lambda i: (i, 0)),
        compiler_params=_params(("parallel",)),
    )(du, w_int, dt1)


def _ln_in_bwd(ds0, x, meta, g):
    bsz, s, d = x.shape
    tp = s + HEAD
    nh = 2
    sh = s // nh
    rc = min(256, sh)

    def body(ds_ref, x_ref, meta_ref, g_ref, gx_ref, dm_ref, vec_ref):
        h = pl.program_id(1)
        gg = g_ref[...]

        @pl.when(h == 0)
        def _():
            mh, mr = _ln_stats(meta_ref[...])
            dsm = ds_ref[0, PAD:HEAD, :]
            dm_ref[0] = _ln_bwd(dsm * gg, mh, mr)
            vec_ref[0] = jnp.zeros((8, d), F32)
            vec_ref[0, 0:1, :] = jnp.sum(dsm * mh, axis=0, keepdims=True)
            vec_ref[0, 1:2, :] = jnp.sum(dsm, axis=0, keepdims=True)

        def step(i, carry):
            sg, sb = carry
            dst = pl.ds(pl.multiple_of(i * rc, rc), rc)
            src = pl.ds(pl.multiple_of(HEAD + h * sh + i * rc, 64), rc)
            xh, rstd = _ln_stats(x_ref[0, dst, :])
            dsv = ds_ref[0, src, :]
            gx_ref[0, dst, :] = _ln_bwd(dsv * gg, xh, rstd)
            return sg + jnp.sum(dsv * xh, axis=0, keepdims=True), sb + jnp.sum(dsv, axis=0, keepdims=True)

        zero = jnp.zeros((1, d), F32)
        sg, sb = lax.fori_loop(0, sh // rc, step, (zero, zero))
        vec_ref[0, 0:1, :] += sg
        vec_ref[0, 1:2, :] += sb

    return pl.pallas_call(
        body, name="ln_in_bwd",
        out_shape=[jax.ShapeDtypeStruct((bsz, s, d), F32), jax.ShapeDtypeStruct((bsz, N_META, d), F32),
                   jax.ShapeDtypeStruct((bsz, 8, d), F32)],
        grid=(bsz, nh),
        in_specs=[pl.BlockSpec((1, tp, d), lambda bi, hi: (bi, 0, 0)),
                  pl.BlockSpec((1, sh, d), lambda bi, hi: (bi, hi, 0)),
                  pl.BlockSpec((N_META, d), lambda bi, hi: (0, 0)),
                  pl.BlockSpec((1, d), lambda bi, hi: (0, 0))],
        out_specs=[pl.BlockSpec((1, sh, d), lambda bi, hi: (bi, hi, 0)),
                   pl.BlockSpec((1, N_META, d), lambda bi, hi: (bi, 0, 0)),
                   pl.BlockSpec((1, 8, d), lambda bi, hi: (bi, 0, 0))],
        compiler_params=_params(("parallel", "arbitrary")),
    )(ds0, x, meta, g)


def _rows128(a):
    return a.reshape(-1, 128)


def kernel(x, meta_tokens, ln_in_g, ln_in_b, w_in, conv_w, conv_b, conv_ln_g, conv_ln_b, gate_up, gate_bias, gla_norm_g, w_out, ln1_g, ln1_b, w_ff1, w_ff2, ln2_g, ln2_b, loss_target, m_meta_tokens, m_ln_in_g, m_ln_in_b, m_w_in, m_conv_w, m_conv_b, m_conv_ln_g, m_conv_ln_b, m_gate_up, m_gate_bias, m_gla_norm_g, m_w_out, m_ln1_g, m_ln1_b, m_w_ff1, m_w_ff2, m_ln2_g, m_ln2_b, v_meta_tokens, v_ln_in_g, v_ln_in_b, v_w_in, v_conv_w, v_conv_b, v_conv_ln_g, v_conv_ln_b, v_gate_up, v_gate_bias, v_gla_norm_g, v_w_out, v_ln1_g, v_ln1_b, v_w_ff1, v_w_ff2, v_ln2_g, v_ln2_b):
    bsz, seq, d = x.shape
    tp = seq + HEAD
    r = bsz * tp
    xi, yi, ci = _mesh_pos()
    chip = 2 * xi + yi
    c_arr = jnp.reshape(ci, (1,)).astype(jnp.int32)
    pos_arr = jnp.stack([chip, ci]).astype(jnp.int32)

    sh_in = D_IN // 4
    shards = [jnp.pad(w_in[0].T.astype(BF16), ((0, D_IN_PAD // 4 - sh_in), (0, 0))), w_ff1[0].T.astype(BF16),
              w_out[0].astype(BF16), w_ff2[0].astype(BF16)]
    small_w = jnp.concatenate([_rows128(meta_tokens), _rows128(conv_w[0]), _rows128(gate_up[0])], axis=0)
    (g_int, g_w1t, g_wout, g_w2), g_small = _gather_weights(shards, small_w)
    w_int = jnp.pad(g_int[:, :sh_in].reshape(D_IN, d), ((0, D_IN_PAD - D_IN), (0, 0)))
    w1t = g_w1t.reshape(D_FF, d)
    wout = g_wout.reshape(d, d)
    w2 = g_w2.reshape(D_FF, d)
    n_meta_rows, n_cw_rows = N_META * 256 // 128, CONV_WIDTH * 128 // 128
    meta_full = jnp.concatenate([g_small[j, :n_meta_rows].reshape(N_META, 256) for j in range(4)], axis=1)
    convw_full = jnp.concatenate(
        [g_small[j, n_meta_rows:n_meta_rows + n_cw_rows].reshape(CONV_WIDTH, 128) for j in range(4)], axis=1)
    gup_full = jnp.concatenate(
        [g_small[j, n_meta_rows + n_cw_rows:].reshape(GLA_RANK, 64) for j in range(4)], axis=1)
    convw_p = jnp.pad(convw_full, ((0, 1), (0, 0)))
    gup_p = jnp.pad(gup_full, ((0, 128 - GLA_RANK), (0, 0))).astype(BF16)
    ln_in_g2, ln_in_b2 = ln_in_g.reshape(1, d), ln_in_b.reshape(1, d)

    s0, s0b = _ln_in_fwd(x, meta_full, ln_in_g2, ln_in_b2)
    tgt_p = jnp.pad(loss_target, ((0, 0), (HEAD, 0), (0, 0)))
    s0f, s0bf = s0.reshape(r, d), s0b.reshape(r, d)
    u = _in_proj(s0bf, w_int)
    u3 = u.reshape(bsz, tp, D_IN_PAD)
    hc = _conv_fwd(u3, convw_p, conv_b).reshape(r, D_CONV)
    gla_out, o_pre, states = _gla_fwd(u3, gup_p, gate_bias, gla_norm_g)
    gla_of = gla_out.reshape(r, 512)
    conv_of, xhat1, rstd1, s1b = _out_proj_ln1(hc, gla_of, wout, s0f, conv_ln_g, conv_ln_b, ln1_g, ln1_b)
    ra = _ffn1(s1b, w1t)
    dt2, dt2b, acc2 = _ffn2_ln2_loss(ra, w2, xhat1, ln1_g, ln1_b, ln2_g, ln2_b, tgt_p.reshape(r, d), tp)

    da = _ffn_bwd_da(dt2b, w2, ra)
    dt1, dt1b, acc1 = _ffn_bwd_ln1(da, w1t, dt2, xhat1, rstd1, ln1_g)
    g_w2 = _matmul_tn(ra, dt2b, 1024, square_lhs=True, name="grad_w_ff2")
    g_w1t = _matmul_tn(da, s1b, 1024, name="grad_w_ff1")
    dhc, dgla, cacc = _out_proj_bwd(dt1b, wout, hc, conv_ln_g, conv_ln_b)
    g_wout = jnp.concatenate([_matmul_tn(conv_of, dt1b, D_CONV, name="grad_w_out_conv"),
                              _matmul_tn(gla_of, dt1b, 512, name="grad_w_out_gla")], axis=0)
    dcv, dcg, dcw = _conv_bwd(dhc.reshape(bsz, tp, D_CONV), u3, convw_p)
    dq, dk, dv, dr, dgd, dgup, gvec = _gla_bwd(dgla.reshape(bsz, tp, 512), u3, o_pre, states, gup_p, gate_bias,
                                               gla_norm_g)
    dgd_sum = (dgd[:, 0].astype(F32) + dgd[:, 1].astype(F32)).astype(BF16)
    du = jnp.concatenate([dcv, dcg, dq, dk, dv, dr, dgd_sum], axis=-1).reshape(r, D_IN_PAD)
    ds0 = _in_proj_bwd(du, w_int, dt1)
    g_wint = _matmul_tn(du, s0bf, D_IN_PAD // 3, name="grad_w_in")
    grad_x, dmeta, lvec = _ln_in_bwd(ds0.reshape(bsz, tp, d), x, meta_full, ln_in_g2)

    small = {
        "loss": acc2[0:1], "ln_in_g": jnp.sum(lvec[:, 0], axis=0), "ln_in_b": jnp.sum(lvec[:, 1], axis=0),
        "conv_b": cacc[0], "conv_ln_g": cacc[1], "conv_ln_b": cacc[2], "gate_bias": jnp.sum(gvec[:, 0], axis=0),
        "gla_norm_g": jnp.sum(gvec[:, 1].reshape(bsz * 2, 128), axis=0),
        "ln1_g": acc1[0], "ln1_b": acc1[1], "ln2_g": acc2[1], "ln2_b": acc2[2],
        "conv_w": jnp.sum(dcw, axis=0)[:CONV_WIDTH], "gate_up": jnp.sum(dgup, axis=0)[:GLA_RANK],
        "meta_tokens": jnp.sum(dmeta, axis=0),
    }
    pack = jnp.concatenate([_rows128(small[k]) for k, _ in _SMALL_FIELDS], axis=0)
    pack = jnp.pad(pack, ((0, SMALL_ROWS - pack.shape[0]), (0, 0)))
    red = _sum_small(_allgather_small(pack))
    off, tot = {}, 0
    for k, nrows in _SMALL_FIELDS:
        off[k] = (tot, nrows)
        tot += nrows

    def field(k, shape):
        o, nrows = off[k]
        return red[o:o + nrows].reshape(shape)

    loss = jnp.sum(field("loss", (d,)))

    big = [g_wint[:D_IN].reshape(4, D_IN // 4, d), g_w1t.reshape(4, D_FF // 4, d),
           g_wout.reshape(4, d // 4, d), g_w2.reshape(4, D_FF // 4, d)]
    pair_got = _pair_exchange(big)
    parts = [_add_pair(g, t, c_arr) for g, t in zip(big, pair_got)]
    chip_got = _chip_exchange(parts)
    halves = [_add_chips(g, t, s, pos_arr) for g, t, s in zip(big, pair_got, chip_got)]
    f_wint, f_w1t, f_wout, f_w2 = [jnp.concatenate([f[0], f[1]], axis=1) for f in _pair_allgather(halves)]

    def cols(a, width):
        return lax.dynamic_slice_in_dim(a, chip * width, width, axis=a.ndim - 1)

    grads = {
        "meta_tokens": cols(field("meta_tokens", (N_META, d)), 256),
        "ln_in_g": field("ln_in_g", (d,)), "ln_in_b": field("ln_in_b", (d,)),
        "w_in": f_wint.T[None], "conv_w": cols(field("conv_w", (CONV_WIDTH, D_CONV)), 128)[None],
        "conv_b": field("conv_b", (1, D_CONV)), "conv_ln_g": field("conv_ln_g", (1, D_CONV)),
        "conv_ln_b": field("conv_ln_b", (1, D_CONV)),
        "gate_up": cols(field("gate_up", (GLA_RANK, 256)), 64)[None],
        "gate_bias": field("gate_bias", (1, 256)), "gla_norm_g": field("gla_norm_g", (1, 128)),
        "w_out": f_wout[None], "ln1_g": field("ln1_g", (1, d)), "ln1_b": field("ln1_b", (1, d)),
        "w_ff1": f_w1t.T[None], "w_ff2": f_w2[None],
        "ln2_g": field("ln2_g", (1, d)), "ln2_b": field("ln2_b", (1, d)),
    }
    weights = dict(meta_tokens=meta_tokens, ln_in_g=ln_in_g, ln_in_b=ln_in_b, w_in=w_in, conv_w=conv_w, conv_b=conv_b,
                   conv_ln_g=conv_ln_g, conv_ln_b=conv_ln_b, gate_up=gate_up, gate_bias=gate_bias,
                   gla_norm_g=gla_norm_g, w_out=w_out, ln1_g=ln1_g, ln1_b=ln1_b, w_ff1=w_ff1, w_ff2=w_ff2,
                   ln2_g=ln2_g, ln2_b=ln2_b)
    moms = dict(meta_tokens=(m_meta_tokens, v_meta_tokens), ln_in_g=(m_ln_in_g, v_ln_in_g),
                ln_in_b=(m_ln_in_b, v_ln_in_b), w_in=(m_w_in, v_w_in), conv_w=(m_conv_w, v_conv_w),
                conv_b=(m_conv_b, v_conv_b), conv_ln_g=(m_conv_ln_g, v_conv_ln_g),
                conv_ln_b=(m_conv_ln_b, v_conv_ln_b), gate_up=(m_gate_up, v_gate_up),
                gate_bias=(m_gate_bias, v_gate_bias), gla_norm_g=(m_gla_norm_g, v_gla_norm_g),
                w_out=(m_w_out, v_w_out), ln1_g=(m_ln1_g, v_ln1_g), ln1_b=(m_ln1_b, v_ln1_b),
                w_ff1=(m_w_ff1, v_w_ff1), w_ff2=(m_w_ff2, v_w_ff2), ln2_g=(m_ln2_g, v_ln2_g),
                ln2_b=(m_ln2_b, v_ln2_b))
    names = list(weights)
    big_names = ("w_in", "w_out", "w_ff1", "w_ff2")
    delta, new_m, new_v = {}, {}, {}
    for k in big_names:
        shp = weights[k].shape
        two = lambda a: a.reshape(shp[-2], shp[-1])
        dl, mn, vn = _adamw(two(weights[k]), two(grads[k]), two(moms[k][0]), two(moms[k][1]))
        delta[k], new_m[k], new_v[k] = dl.reshape(shp), mn.reshape(shp), vn.reshape(shp)
    small_names = [k for k in names if k not in big_names]
    sizes = [weights[k].size for k in small_names]
    total = sum(sizes)
    padded = -(-total // 1024) * 1024

    def flat(get):
        v = jnp.concatenate([get(k).reshape(-1) for k in small_names])
        return jnp.pad(v, (0, padded - total)).reshape(padded // 128, 128)

    ones = jnp.ones((padded - total,), F32)
    vflat = jnp.concatenate([jnp.concatenate([moms[k][1].reshape(-1) for k in small_names]), ones])
    dl, mn, vn = _adamw(flat(lambda k: weights[k]), flat(lambda k: grads[k]), flat(lambda k: moms[k][0]),
                        vflat.reshape(padded // 128, 128))
    pos = 0
    for k, sz in zip(small_names, sizes):
        shp = weights[k].shape
        delta[k] = dl.reshape(-1)[pos:pos + sz].reshape(shp)
        new_m[k] = mn.reshape(-1)[pos:pos + sz].reshape(shp)
        new_v[k] = vn.reshape(-1)[pos:pos + sz].reshape(shp)
        pos += sz

    return (loss, grad_x, *[grads[k] for k in names], *[delta[k] for k in names],
            *[new_m[k] for k in names], *[new_v[k] for k in names])
```

```python
import functools

import jax
import jax.numpy as jnp
from jax import lax
from jax.experimental import pallas as pl
from jax.experimental.pallas import tpu as pltpu

F32 = jnp.float32
BF16 = jnp.bfloat16

D_MODEL = 1024
N_META = 16
D_CONV = 512
CONV_WIDTH = 31
GLA_HEADS = 4
GLA_DV = 128
GLA_DK = 64
GLA_RANK = 16
GLA_TAU = 16.0
CHUNK = 64
D_FF = 4096
LN_EPS = 1e-5
ALPHA = 2.0 ** 0.25
D_IN = 2576
D_IN_PAD = 2688
PAD = CHUNK - N_META
HEAD = PAD + N_META
Q_SCALE = GLA_DK ** -0.5
ADAM_LR, ADAM_B1, ADAM_B2, ADAM_EPS, ADAM_WD, ADAM_STEP = 0.001, 0.9, 0.999, 1e-08, 0.01, 10
HALF = D_MODEL // 2
VMEM_LIMIT = 56 * 1024 * 1024
MESH = pl.DeviceIdType.MESH

C_VAL, C_GATE, C_Q, C_K, C_V, C_R, C_GD = 0, 512, 1024, 1280, 1536, 2048, 2560

_SMALL_FIELDS = (("loss", 8), ("ln_in_g", 8), ("ln_in_b", 8), ("conv_b", 4), ("conv_ln_g", 4), ("conv_ln_b", 4),
                 ("gate_bias", 2), ("gla_norm_g", 1), ("ln1_g", 8), ("ln1_b", 8), ("ln2_g", 8), ("ln2_b", 8),
                 ("conv_w", 124), ("gate_up", 32), ("meta_tokens", 128))
SMALL_ROWS = 360


def _params(sem=None, **kw):
    return pltpu.CompilerParams(dimension_semantics=sem, vmem_limit_bytes=VMEM_LIMIT, **kw)


def _row_tile(tp):
    for t in (704, 352, 192, 64):
        if tp % t == 0:
            return t
    raise ValueError(tp)


def _dot(a, b, dims, precision=None):
    return lax.dot_general(a, b, (dims, ((), ())), preferred_element_type=F32, precision=precision)


def _nn(a, b, **kw):
    return _dot(a, b, ((1,), (0,)), **kw)


def _nt(a, b, **kw):
    return _dot(a, b, ((1,), (1,)), **kw)


def _tn(a, b, **kw):
    return _dot(a, b, ((0,), (0,)), **kw)


def _sigmoid(x):
    return 1.0 / (1.0 + jnp.exp(-x))


def _log_sigmoid(z):
    return jnp.minimum(z, 0.0) - jnp.log(1.0 + jnp.exp(-jnp.abs(z)))


def _ln_stats(t):
    mu = jnp.mean(t, axis=-1, keepdims=True)
    d = t - mu
    var = jnp.mean(d * d, axis=-1, keepdims=True)
    rstd = lax.rsqrt(var + LN_EPS)
    return d * rstd, rstd


def _ln_bwd(dxhat, xhat, rstd):
    m1 = jnp.mean(dxhat, axis=-1, keepdims=True)
    m2 = jnp.mean(dxhat * xhat, axis=-1, keepdims=True)
    return rstd * (dxhat - m1 - xhat * m2)


def _mesh_pos():
    return lax.axis_index("x"), lax.axis_index("y"), lax.axis_index("c")


ANY = pl.BlockSpec(memory_space=pl.ANY)


def _gather_weights(shards, small):
    n = len(shards)
    half = [s.shape[0] // 2 for s in shards]

    def body(*refs):
        ins, small_in = refs[:n], refs[n]
        outs, small_out = refs[n + 1:2 * n + 1], refs[2 * n + 1]
        send, recv, fsend, frecv, ssend, srecv = refs[2 * n + 2:]
        x, y, c = _mesh_pos()
        me = 2 * x + y
        sibling = (x, y, 1 - c)
        chips = [(1 - x, y), (x, 1 - y), (1 - x, 1 - y)]
        chip_idx = [2 * px + py for px, py in chips]
        mine = [pl.ds(pl.multiple_of(c * h, 16), h) for h in half]
        other = [pl.ds(pl.multiple_of((1 - c) * h, 16), h) for h in half]

        def ici(a, k):
            return pltpu.make_async_remote_copy(
                src_ref=ins[a].at[mine[a], :], dst_ref=outs[a].at[me, mine[a], :],
                send_sem=send.at[a, k], recv_sem=recv.at[a, k], device_id=(*chips[k], c), device_id_type=MESH)

        def landed(a, k):
            return pltpu.make_async_remote_copy(
                src_ref=ins[a].at[mine[a], :], dst_ref=outs[a].at[chip_idx[k], mine[a], :],
                send_sem=send.at[a, k], recv_sem=recv.at[a, k], device_id=(*chips[k], c), device_id_type=MESH)

        def forward(a, k, rows):
            blk = outs[a].at[chip_idx[k], rows[a], :]
            return pltpu.make_async_remote_copy(
                src_ref=blk, dst_ref=blk, send_sem=fsend.at[a, k], recv_sem=frecv.at[a, k],
                device_id=sibling, device_id_type=MESH)

        def small_copy(k, slot):
            return pltpu.make_async_remote_copy(
                src_ref=small_in, dst_ref=small_out.at[slot], send_sem=ssend.at[k], recv_sem=srecv.at[k],
                device_id=(*chips[k], c), device_id_type=MESH)

        for a in range(n):
            for k in range(3):
                ici(a, k).start()
        for k in range(3):
            small_copy(k, me).start()
        for a in range(n):
            for k in range(3):
                landed(a, k).wait_recv()
                forward(a, k, mine).start()
        for a in range(n):
            for k in range(3):
                forward(a, k, other).wait_recv()
        for k in range(3):
            small_copy(k, chip_idx[k]).wait_recv()
        for a in range(n):
            for k in range(3):
                ici(a, k).wait_send()
                forward(a, k, mine).wait_send()
        for k in range(3):
            small_copy(k, me).wait_send()

    out_shape = [jax.ShapeDtypeStruct((4,) + s.shape, s.dtype) for s in shards]
    out_shape.append(jax.ShapeDtypeStruct((4,) + small.shape, small.dtype))
    res = pl.pallas_call(
        body, name="gather_weights", out_shape=out_shape,
        in_specs=[ANY] * (n + 1), out_specs=[ANY] * (n + 1),
        scratch_shapes=[pltpu.SemaphoreType.DMA((n, 3)), pltpu.SemaphoreType.DMA((n, 3)),
                        pltpu.SemaphoreType.DMA((n, 3)), pltpu.SemaphoreType.DMA((n, 3)),
                        pltpu.SemaphoreType.DMA((3,)), pltpu.SemaphoreType.DMA((3,))],
    )(*shards, small)
    chip = 2 * lax.axis_index("x") + lax.axis_index("y")
    res = [lax.dynamic_update_slice(g, s[None], (chip, 0, 0)) for g, s in zip(res, list(shards) + [small])]
    return res[:n], res[n]


def _pair_exchange(grads):
    n = len(grads)

    def body(*refs):
        ins, outs = refs[:n], refs[n:2 * n]
        send, recv = refs[2 * n:]
        x, y, c = _mesh_pos()
        other = pl.ds(pl.multiple_of((1 - c) * HALF, 128), HALF)
        cps = [pltpu.make_async_remote_copy(
            src_ref=ins[a].at[:, :, other], dst_ref=outs[a], send_sem=send.at[a], recv_sem=recv.at[a],
            device_id=(x, y, 1 - c), device_id_type=MESH) for a in range(n)]
        for cp in cps:
            cp.start()
        for cp in cps:
            cp.wait()

    return pl.pallas_call(
        body, name="grad_pair_exchange",
        out_shape=[jax.ShapeDtypeStruct(g.shape[:2] + (HALF,), g.dtype) for g in grads],
        in_specs=[ANY] * n, out_specs=[ANY] * n,
        scratch_shapes=[pltpu.SemaphoreType.DMA((n,)), pltpu.SemaphoreType.DMA((n,))],
    )(*grads)


def _chip_exchange(parts):
    n = len(parts)

    def body(*refs):
        ins, outs = refs[:n], refs[n:2 * n]
        send, recv = refs[2 * n:]
        x, y, c = _mesh_pos()
        chips = [(1 - x, y), (x, 1 - y), (1 - x, 1 - y)]
        cps = [pltpu.make_async_remote_copy(
            src_ref=ins[a].at[2 * px + py], dst_ref=outs[a].at[k], send_sem=send.at[a, k], recv_sem=recv.at[a, k],
            device_id=(px, py, c), device_id_type=MESH) for a in range(n) for k, (px, py) in enumerate(chips)]
        for cp in cps:
            cp.start()
        for cp in cps:
            cp.wait()

    return pl.pallas_call(
        body, name="grad_chip_exchange",
        out_shape=[jax.ShapeDtypeStruct((3,) + p.shape[1:], p.dtype) for p in parts],
        in_specs=[ANY] * n, out_specs=[ANY] * n,
        scratch_shapes=[pltpu.SemaphoreType.DMA((n, 3)), pltpu.SemaphoreType.DMA((n, 3))],
    )(*parts)


def _pair_swap(halves):
    n = len(halves)

    def body(*refs):
        ins, outs = refs[:n], refs[n:2 * n]
        send, recv = refs[2 * n:]
        x, y, c = _mesh_pos()
        cps = [pltpu.make_async_remote_copy(
            src_ref=ins[a], dst_ref=outs[a], send_sem=send.at[a], recv_sem=recv.at[a],
            device_id=(x, y, 1 - c), device_id_type=MESH) for a in range(n)]
        for cp in cps:
            cp.start()
        for cp in cps:
            cp.wait()

    return pl.pallas_call(
        body, name="grad_pair_swap",
        out_shape=[jax.ShapeDtypeStruct(h.shape, h.dtype) for h in halves],
        in_specs=[ANY] * n, out_specs=[ANY] * n,
        scratch_shapes=[pltpu.SemaphoreType.DMA((n,)), pltpu.SemaphoreType.DMA((n,))],
    )(*halves)


def _allgather_small(pack):
    m_per, ncol = pack.shape

    def body(x_ref, out_ref, send_sems, recv_sems, local_sem):
        x, y, c = _mesh_pos()
        me, sibling = (x, y, c), (x, y, 1 - c)
        chips = [(1 - x, y), (x, 1 - y), (1 - x, 1 - y)]

        def rows(px, py, pc):
            return out_ref.at[pl.ds(pl.multiple_of((4 * px + 2 * py + pc) * m_per, 8), m_per), :]

        def copy(k, block, to, src=None):
            return pltpu.make_async_remote_copy(
                src_ref=rows(*block) if src is None else src, dst_ref=rows(*block),
                send_sem=send_sems.at[k], recv_sem=recv_sems.at[k], device_id=to, device_id_type=MESH)

        mine = pltpu.make_async_copy(x_ref, rows(*me), local_sem)
        mine.start()
        first = [copy(0, me, sibling, src=x_ref)]
        first += [copy(1 + j, me, (*chip, c), src=x_ref) for j, chip in enumerate(chips)]
        for cp in first:
            cp.start()
        passed = [copy(4 + j, (*chip, c), sibling) for j, chip in enumerate(chips)]
        for j, chip in enumerate(chips):
            copy(1 + j, (*chip, c), me).wait_recv()
            passed[j].start()
        copy(0, sibling, me).wait_recv()
        for j, chip in enumerate(chips):
            copy(4 + j, (*chip, 1 - c), me).wait_recv()
        for cp in first + passed:
            cp.wait_send()
        mine.wait()

    return pl.pallas_call(
        body, name="allgather_small",
        out_shape=jax.ShapeDtypeStruct((8 * m_per, ncol), pack.dtype),
        in_specs=[pl.BlockSpec(memory_space=pltpu.VMEM)],
        out_specs=pl.BlockSpec(memory_space=pltpu.VMEM),
        scratch_shapes=[pltpu.SemaphoreType.DMA((7,)), pltpu.SemaphoreType.DMA((7,)), pltpu.SemaphoreType.DMA],
    )(pack)


def _add_pair(g, got, c_arr):
    _, rows, _ = g.shape

    def body(c_ref, g_ref, r_ref, o_ref):
        o_ref[...] = (g_ref[...] + r_ref[...]).astype(BF16)

    return pl.pallas_call(
        body, name="grad_add_pair", out_shape=jax.ShapeDtypeStruct((4, rows, HALF), BF16),
        grid_spec=pltpu.PrefetchScalarGridSpec(
            num_scalar_prefetch=1, grid=(4,),
            in_specs=[pl.BlockSpec((1, rows, HALF), lambda j, c: (j, 0, c[0])),
                      pl.BlockSpec((1, rows, HALF), lambda j, c: (j, 0, 0))],
            out_specs=pl.BlockSpec((1, rows, HALF), lambda j, c: (j, 0, 0))),
        compiler_params=_params(("arbitrary",)),
    )(c_arr, g, got)


def _add_chips(g, pair_got, chip_got, pos_arr):
    _, rows, _ = g.shape

    def body(pos_ref, g_ref, p_ref, r_ref, o_ref):
        own = g_ref[0] + p_ref[0]
        o_ref[...] = ((own + r_ref[0].astype(F32)) + r_ref[1].astype(F32)) + r_ref[2].astype(F32)

    return pl.pallas_call(
        body, name="grad_add_chips", out_shape=jax.ShapeDtypeStruct((rows, HALF), F32),
        grid_spec=pltpu.PrefetchScalarGridSpec(
            num_scalar_prefetch=1, grid=(1,),
            in_specs=[pl.BlockSpec((1, rows, HALF), lambda i, p: (p[0], 0, p[1])),
                      pl.BlockSpec((1, rows, HALF), lambda i, p: (p[0], 0, 0)),
                      pl.BlockSpec((3, rows, HALF), lambda i, p: (0, 0, 0))],
            out_specs=pl.BlockSpec((rows, HALF), lambda i, p: (0, 0))),
        compiler_params=_params(("arbitrary",)),
    )(pos_arr, g, pair_got, chip_got)


def _sum_small(gathered):
    def body(g_ref, o_ref):
        acc = g_ref[0:SMALL_ROWS, :]
        for d in range(1, 8):
            acc = acc + g_ref[d * SMALL_ROWS:(d + 1) * SMALL_ROWS, :]
        o_ref[...] = acc

    return pl.pallas_call(
        body, name="sum_small", out_shape=jax.ShapeDtypeStruct((SMALL_ROWS, 128), F32),
    )(gathered)


def _adamw(w, g, m, v):
    rows, cols = w.shape
    tr = 256 if rows % 256 == 0 else rows
    c1 = 1.0 - ADAM_B1 ** ADAM_STEP
    c2 = 1.0 - ADAM_B2 ** ADAM_STEP

    def body(w_ref, g_ref, m_ref, v_ref, d_ref, mo_ref, vo_ref):
        gg = g_ref[...]
        mn = ADAM_B1 * m_ref[...] + (1.0 - ADAM_B1) * gg
        vn = ADAM_B2 * v_ref[...] + (1.0 - ADAM_B2) * (gg * gg)
        d_ref[...] = -ADAM_LR * ((mn / c1) / (jnp.sqrt(vn / c2) + ADAM_EPS) + ADAM_WD * w_ref[...])
        mo_ref[...] = mn
        vo_ref[...] = vn

    spec = pl.BlockSpec((tr, cols), lambda i: (i, 0))
    return pl.pallas_call(
        body, name="adamw", out_shape=[jax.ShapeDtypeStruct(w.shape, F32)] * 3,
        grid=(rows // tr,), in_specs=[spec] * 4, out_specs=[spec] * 3,
        compiler_params=_params(("parallel",)),
    )(w, g, m, v)


def _ln_in_fwd(x, meta, g, b):
    bsz, s, d = x.shape
    tp = s + HEAD
    nh = 2
    sh = s // nh
    rc = min(256, sh)

    def body(x_ref, meta_ref, g_ref, b_ref, s0_ref, s0b_ref):
        h = pl.program_id(1)
        gg, bb = g_ref[...], b_ref[...]

        @pl.when(h == 0)
        def _():
            s0_ref[0, 0:PAD, :] = jnp.zeros((PAD, d), F32)
            s0b_ref[0, 0:PAD, :] = jnp.zeros((PAD, d), BF16)
            mh, _ = _ln_stats(meta_ref[...])
            mv = mh * gg + bb
            s0_ref[0, PAD:HEAD, :] = mv
            s0b_ref[0, PAD:HEAD, :] = mv.astype(BF16)

        def step(i, carry):
            src = pl.ds(pl.multiple_of(i * rc, rc), rc)
            dst = pl.ds(pl.multiple_of(HEAD + h * sh + i * rc, 64), rc)
            xh, _ = _ln_stats(x_ref[0, src, :])
            val = xh * gg + bb
            s0_ref[0, dst, :] = val
            s0b_ref[0, dst, :] = val.astype(BF16)
            return carry

        lax.fori_loop(0, sh // rc, step, 0)

    full = lambda bi, hi: (bi, 0, 0)
    return pl.pallas_call(
        body, name="ln_in_fwd",
        out_shape=[jax.ShapeDtypeStruct((bsz, tp, d), F32), jax.ShapeDtypeStruct((bsz, tp, d), BF16)],
        grid=(bsz, nh),
        in_specs=[pl.BlockSpec((1, sh, d), lambda bi, hi: (bi, hi, 0)),
                  pl.BlockSpec((N_META, d), lambda bi, hi: (0, 0)),
                  pl.BlockSpec((1, d), lambda bi, hi: (0, 0)),
                  pl.BlockSpec((1, d), lambda bi, hi: (0, 0))],
        out_specs=[pl.BlockSpec((1, tp, d), full)] * 2,
        compiler_params=_params(("parallel", "arbitrary")),
    )(x, meta, g, b)


def _in_proj(s0b, w_int):
    r, d = s0b.shape
    tm = _row_tile(r)

    def body(a_ref, w_ref, o_ref):
        o_ref[...] = _nt(a_ref[...], w_ref[...])

    return pl.pallas_call(
        body, name="in_proj", out_shape=jax.ShapeDtypeStruct((r, D_IN_PAD), F32),
        grid=(r // tm,),
        in_specs=[pl.BlockSpec((tm, d), lambda i: (i, 0)), pl.BlockSpec((D_IN_PAD, d), lambda i: (0, 0))],
        out_specs=pl.BlockSpec((tm, D_IN_PAD), lambda i: (i, 0)),
        compiler_params=_params(("parallel",)),
    )(s0b, w_int)


def _conv_fwd(u, conv_w, conv_b):
    bsz, tp, _ = u.shape
    nchunk = tp // CHUNK
    win = CHUNK + 32
    nct = D_CONV // 128

    def body(cv_ref, cg_ref, w_ref, cb_ref, hc_ref, h_scr, win_scr):
        h_scr[0:32, :] = jnp.zeros((32, 128), F32)
        h_scr[32:32 + tp, :] = cv_ref[0] * _sigmoid(cg_ref[0])
        cb = cb_ref[...]

        def step(n, carry):
            r0 = pl.multiple_of(n * CHUNK, CHUNK)
            win_scr[...] = h_scr[pl.ds(r0, win), :]
            acc = jnp.zeros((CHUNK, 128), F32)
            for j in range(CONV_WIDTH):
                acc = acc + w_ref[j:j + 1, :] * win_scr[2 + j:2 + j + CHUNK, :]
            hc_ref[0, pl.ds(r0, CHUNK), :] = acc + cb
            return carry

        lax.fori_loop(0, nchunk, step, 0)

    return pl.pallas_call(
        body, name="conv_fwd", out_shape=jax.ShapeDtypeStruct((bsz, tp, D_CONV), F32),
        grid=(bsz, nct),
        in_specs=[pl.BlockSpec((1, tp, 128), lambda bi, t: (bi, 0, C_VAL // 128 + t)),
                  pl.BlockSpec((1, tp, 128), lambda bi, t: (bi, 0, C_GATE // 128 + t)),
                  pl.BlockSpec((32, 128), lambda bi, t: (0, t)),
                  pl.BlockSpec((1, 128), lambda bi, t: (0, t))],
        out_specs=pl.BlockSpec((1, tp, 128), lambda bi, t: (bi, 0, t)),
        scratch_shapes=[pltpu.VMEM((tp + 32, 128), F32), pltpu.VMEM((win, 128), F32)],
        compiler_params=_params(("parallel", "parallel")),
    )(u, u, conv_w, conv_b)


def _gla_consts():
    row = lax.broadcasted_iota(jnp.int32, (CHUNK, CHUNK), 0)
    col = lax.broadcasted_iota(jnp.int32, (CHUNK, CHUNK), 1)
    lane = lax.broadcasted_iota(jnp.int32, (1, 128), 1)
    return row >= col, [lane < GLA_DK, lane >= GLA_DK]


def _gla_triu():
    row = lax.broadcasted_iota(jnp.int32, (CHUNK, CHUNK), 0)
    col = lax.broadcasted_iota(jnp.int32, (CHUNK, CHUNK), 1)
    return (row <= col).astype(F32)


def _gla_chunk_terms(n, q_ref, k_ref, gd_ref, gup_ref, gb_ref, tri):
    r0 = pl.multiple_of(n * CHUNK, CHUNK)
    rows = pl.ds(r0, CHUNK)
    z = _nn(gd_ref[0, rows, :].astype(BF16), gup_ref[...]) + gb_ref[...]
    rowid = n * CHUNK + lax.broadcasted_iota(jnp.int32, (CHUNK, 1), 0)
    valid = rowid >= PAD
    lg = jnp.where(valid, _log_sigmoid(z) * (1.0 / GLA_TAU), 0.0)
    bcum = _nn(tri.astype(F32), lg, precision=lax.Precision.HIGHEST)
    blast = bcum[CHUNK - 1:CHUNK, :]
    eb = jnp.exp(bcum)
    enb = jnp.exp(-bcum)
    erest = jnp.exp(blast - bcum)
    q = q_ref[0, rows, :] * Q_SCALE
    k = k_ref[0, rows, :]
    return rows, valid, z, eb, enb, erest, jnp.exp(blast), q * eb, k * enb, k * erest


def _gla_fwd(u, gup, gbias, gnorm):
    bsz, tp, _ = u.shape
    nchunk = tp // CHUNK

    def body(q_ref, k_ref, v_ref, r_ref, gd_ref, gup_ref, gb_ref, gn_ref, out_ref, o_ref, st_ref, s_scr):
        tri, hmask = _gla_consts()
        s_scr[...] = jnp.zeros_like(s_scr)
        gn = gn_ref[...]

        def step(n, carry):
            rows, _, _, _, _, _, dec, qe, ke, kd = _gla_chunk_terms(n, q_ref, k_ref, gd_ref, gup_ref, gb_ref, tri)
            keb, kdb = ke.astype(BF16), kd.astype(BF16)
            for h in range(2):
                cols = slice(h * GLA_DV, (h + 1) * GLA_DV)
                qh = jnp.where(hmask[h], qe, 0.0).astype(BF16)
                vh = v_ref[0, rows, cols].astype(BF16)
                a = jnp.where(tri, _nt(qh, keb), 0.0)
                st = s_scr[h]
                st_ref[0, h, n] = st
                o = _nn(a.astype(BF16), vh) + _nt(qh, st.astype(BF16))
                s_scr[h] = dec * st + _tn(vh, kdb)
                o_ref[0, rows, cols] = o
                rms = lax.rsqrt(jnp.mean(o * o, axis=-1, keepdims=True) + LN_EPS)
                rh = r_ref[0, rows, cols]
                out_ref[0, rows, cols] = (o * rms * gn * (rh * _sigmoid(rh))).astype(BF16)
            return carry

        lax.fori_loop(0, nchunk, step, 0)

    return pl.pallas_call(
        body, name="gla_fwd",
        out_shape=[jax.ShapeDtypeStruct((bsz, tp, 512), BF16), jax.ShapeDtypeStruct((bsz, tp, 512), F32),
                   jax.ShapeDtypeStruct((bsz, GLA_HEADS, nchunk, GLA_DV, 128), F32)],
        grid=(bsz, 2),
        in_specs=[pl.BlockSpec((1, tp, 128), lambda bi, p: (bi, 0, C_Q // 128 + p)),
                  pl.BlockSpec((1, tp, 128), lambda bi, p: (bi, 0, C_K // 128 + p)),
                  pl.BlockSpec((1, tp, 256), lambda bi, p: (bi, 0, C_V // 256 + p)),
                  pl.BlockSpec((1, tp, 256), lambda bi, p: (bi, 0, C_R // 256 + p)),
                  pl.BlockSpec((1, tp, 128), lambda bi, p: (bi, 0, C_GD // 128)),
                  pl.BlockSpec((128, 128), lambda bi, p: (0, p)),
                  pl.BlockSpec((1, 128), lambda bi, p: (0, p)),
                  pl.BlockSpec((1, 128), lambda bi, p: (0, 0))],
        out_specs=[pl.BlockSpec((1, tp, 256), lambda bi, p: (bi, 0, p)),
                   pl.BlockSpec((1, tp, 256), lambda bi, p: (bi, 0, p)),
                   pl.BlockSpec((1, 2, nchunk, GLA_DV, 128), lambda bi, p: (bi, p, 0, 0, 0))],
        scratch_shapes=[pltpu.VMEM((2, GLA_DV, 128), F32)],
        compiler_params=_params(("parallel", "parallel")),
    )(u, u, u, u, u, gup, gbias, gnorm)


def _out_proj_ln1(hc, gla_out, w_out, s0, cg, cb, g1, b1):
    r, d = s0.shape
    tm = _row_tile(r)

    def body(hc_ref, a_ref, w_ref, s0_ref, cg_ref, cb_ref, g_ref, b_ref, co_ref, xh_ref, rstd_ref, s1b_ref):
        xc, _ = _ln_stats(hc_ref[...])
        nv = xc * cg_ref[...] + cb_ref[...]
        co = (nv * _sigmoid(nv)).astype(BF16)
        co_ref[...] = co
        mix = _nn(co, w_ref[0:D_CONV, :]) + _nn(a_ref[...], w_ref[D_CONV:, :])
        xh, rstd = _ln_stats(ALPHA * s0_ref[...] + mix)
        xh_ref[...] = xh
        rstd_ref[...] = rstd
        s1b_ref[...] = (xh * g_ref[...] + b_ref[...]).astype(BF16)

    row = lambda n: pl.BlockSpec((tm, n), lambda i: (i, 0))
    vec = lambda n: pl.BlockSpec((1, n), lambda i: (0, 0))
    return pl.pallas_call(
        body, name="out_proj_ln1",
        out_shape=[jax.ShapeDtypeStruct((r, D_CONV), BF16), jax.ShapeDtypeStruct((r, d), F32),
                   jax.ShapeDtypeStruct((r, 1), F32), jax.ShapeDtypeStruct((r, d), BF16)],
        grid=(r // tm,),
        in_specs=[row(D_CONV), row(512), pl.BlockSpec((d, d), lambda i: (0, 0)), row(d),
                  vec(D_CONV), vec(D_CONV), vec(d), vec(d)],
        out_specs=[row(D_CONV), row(d), row(1), row(d)],
        compiler_params=_params(("parallel",)),
    )(hc, gla_out, w_out, s0, cg, cb, g1, b1)


def _ffn1(s1b, w1t):
    r, d = s1b.shape
    tm = _row_tile(r)

    def body(a_ref, w_ref, o_ref):
        o_ref[...] = jnp.maximum(_nt(a_ref[...], w_ref[...]), 0.0).astype(BF16)

    return pl.pallas_call(
        body, name="ffn1", out_shape=jax.ShapeDtypeStruct((r, D_FF), BF16),
        grid=(r // tm,),
        in_specs=[pl.BlockSpec((tm, d), lambda i: (i, 0)), pl.BlockSpec((D_FF, d), lambda i: (0, 0))],
        out_specs=pl.BlockSpec((tm, D_FF), lambda i: (i, 0)),
        compiler_params=_params(("parallel",)),
    )(s1b, w1t)


def _ffn2_ln2_loss(ra, w2, xhat1, g1, b1, g2, b2, tgt, tp):
    r, d = xhat1.shape
    tm = _row_tile(tp)
    per = tp // tm

    def body(ra_ref, w_ref, xh1_ref, g1_ref, b1_ref, g2_ref, b2_ref, t_ref, dt_ref, dtb_ref, acc_ref):
        i = pl.program_id(0)

        @pl.when(i == 0)
        def _():
            acc_ref[...] = jnp.zeros_like(acc_ref)

        rf = ra_ref[...].astype(F32)
        f = _nn((rf * rf).astype(BF16), w_ref[...])
        s1 = xh1_ref[...] * g1_ref[...] + b1_ref[...]
        xh2, rstd2 = _ln_stats(ALPHA * s1 + f)
        y = xh2 * g2_ref[...] + b2_ref[...]
        rowid = (i % per) * tm + lax.broadcasted_iota(jnp.int32, (tm, 1), 0)
        e = jnp.where(rowid >= HEAD, y - t_ref[...], 0.0)
        dy = e * (1.0 / d)
        dt2 = _ln_bwd(dy * g2_ref[...], xh2, rstd2)
        dt_ref[...] = dt2
        dtb_ref[...] = dt2.astype(BF16)
        acc_ref[0:1, :] += (0.5 / d) * jnp.sum(e * e, axis=0, keepdims=True)
        acc_ref[1:2, :] += jnp.sum(dy * xh2, axis=0, keepdims=True)
        acc_ref[2:3, :] += jnp.sum(dy, axis=0, keepdims=True)

    row = lambda n: pl.BlockSpec((tm, n), lambda i: (i, 0))
    vec = pl.BlockSpec((1, d), lambda i: (0, 0))
    return pl.pallas_call(
        body, name="ffn2_ln2_loss",
        out_shape=[jax.ShapeDtypeStruct((r, d), F32), jax.ShapeDtypeStruct((r, d), BF16),
                   jax.ShapeDtypeStruct((8, d), F32)],
        grid=(r // tm,),
        in_specs=[row(D_FF), pl.BlockSpec((D_FF, d), lambda i: (0, 0)), row(d), vec, vec, vec, vec, row(d)],
        out_specs=[row(d), row(d), pl.BlockSpec((8, d), lambda i: (0, 0))],
        compiler_params=_params(("arbitrary",)),
    )(ra, w2, xhat1, g1, b1, g2, b2, tgt)


def _ffn_bwd_da(dt2b, w2, ra):
    r, d = dt2b.shape
    tm = _row_tile(r)

    def body(g_ref, w_ref, ra_ref, o_ref):
        o_ref[...] = (_nt(g_ref[...], w_ref[...]) * (2.0 * ra_ref[...].astype(F32))).astype(BF16)

    return pl.pallas_call(
        body, name="ffn_bwd_da", out_shape=jax.ShapeDtypeStruct((r, D_FF), BF16),
        grid=(r // tm,),
        in_specs=[pl.BlockSpec((tm, d), lambda i: (i, 0)), pl.BlockSpec((D_FF, d), lambda i: (0, 0)),
                  pl.BlockSpec((tm, D_FF), lambda i: (i, 0))],
        out_specs=pl.BlockSpec((tm, D_FF), lambda i: (i, 0)),
        compiler_params=_params(("parallel",)),
    )(dt2b, w2, ra)


def _ffn_bwd_ln1(da, w1t, dt2, xhat1, rstd1, g1):
    r, d = dt2.shape
    tm = _row_tile(r)

    def body(da_ref, w_ref, dt2_ref, xh_ref, rstd_ref, g_ref, dt_ref, dtb_ref, acc_ref):
        @pl.when(pl.program_id(0) == 0)
        def _():
            acc_ref[...] = jnp.zeros_like(acc_ref)

        ds1 = ALPHA * dt2_ref[...] + _nn(da_ref[...], w_ref[...])
        xh = xh_ref[...]
        dt1 = _ln_bwd(ds1 * g_ref[...], xh, rstd_ref[...])
        dt_ref[...] = dt1
        dtb_ref[...] = dt1.astype(BF16)
        acc_ref[0:1, :] += jnp.sum(ds1 * xh, axis=0, keepdims=True)
        acc_ref[1:2, :] += jnp.sum(ds1, axis=0, keepdims=True)

    row = lambda n: pl.BlockSpec((tm, n), lambda i: (i, 0))
    return pl.pallas_call(
        body, name="ffn_bwd_ln1",
        out_shape=[jax.ShapeDtypeStruct((r, d), F32), jax.ShapeDtypeStruct((r, d), BF16),
                   jax.ShapeDtypeStruct((8, d), F32)],
        grid=(r // tm,),
        in_specs=[row(D_FF), pl.BlockSpec((D_FF, d), lambda i: (0, 0)), row(d), row(d), row(1),
                  pl.BlockSpec((1, d), lambda i: (0, 0))],
        out_specs=[row(d), row(d), pl.BlockSpec((8, d), lambda i: (0, 0))],
        compiler_params=_params(("arbitrary",)),
    )(da, w1t, dt2, xhat1, rstd1, g1)


def _matmul_tn(lhs, rhs, bm, square_lhs=False, name="matmul_tn"):
    r, m = lhs.shape
    n = rhs.shape[1]
    tk = _row_tile(r)

    def body(a_ref, b_ref, o_ref):
        @pl.when(pl.program_id(1) == 0)
        def _():
            o_ref[...] = jnp.zeros_like(o_ref)

        a = a_ref[...]
        if square_lhs:
            af = a.astype(F32)
            a = (af * af).astype(BF16)
        o_ref[...] += _tn(a, b_ref[...])

    return pl.pallas_call(
        body, name=name, out_shape=jax.ShapeDtypeStruct((m, n), F32),
        grid=(m // bm, r // tk),
        in_specs=[pl.BlockSpec((tk, bm), lambda i, k: (k, i)), pl.BlockSpec((tk, n), lambda i, k: (k, 0))],
        out_specs=pl.BlockSpec((bm, n), lambda i, k: (i, 0)),
        compiler_params=_params(("parallel", "arbitrary")),
    )(lhs, rhs)


def _out_proj_bwd(dt1b, w_out, hc, cg, cb):
    r, d = dt1b.shape
    tm = _row_tile(r)

    def body(g_ref, w_ref, hc_ref, cg_ref, cb_ref, dhc_ref, dgla_ref, acc_ref):
        @pl.when(pl.program_id(0) == 0)
        def _():
            acc_ref[...] = jnp.zeros_like(acc_ref)

        dmix = _nt(g_ref[...], w_ref[...])
        dgla_ref[...] = dmix[:, D_CONV:]
        gg = cg_ref[...]
        xh, rstd = _ln_stats(hc_ref[...])
        nv = xh * gg + cb_ref[...]
        sig = _sigmoid(nv)
        dn = dmix[:, :D_CONV] * (sig * (1.0 + nv * (1.0 - sig)))
        dhc = _ln_bwd(dn * gg, xh, rstd)
        dhc_ref[...] = dhc
        acc_ref[0:1, :] += jnp.sum(dhc, axis=0, keepdims=True)
        acc_ref[1:2, :] += jnp.sum(dn * xh, axis=0, keepdims=True)
        acc_ref[2:3, :] += jnp.sum(dn, axis=0, keepdims=True)

    row = lambda n: pl.BlockSpec((tm, n), lambda i: (i, 0))
    vec = pl.BlockSpec((1, D_CONV), lambda i: (0, 0))
    return pl.pallas_call(
        body, name="out_proj_bwd",
        out_shape=[jax.ShapeDtypeStruct((r, D_CONV), F32), jax.ShapeDtypeStruct((r, 512), F32),
                   jax.ShapeDtypeStruct((8, D_CONV), F32)],
        grid=(r // tm,),
        in_specs=[row(d), pl.BlockSpec((d, d), lambda i: (0, 0)), row(D_CONV), vec, vec],
        out_specs=[row(D_CONV), row(512), pl.BlockSpec((8, D_CONV), lambda i: (0, 0))],
        compiler_params=_params(("arbitrary",)),
    )(dt1b, w_out, hc, cg, cb)


def _conv_bwd(dhc, u, conv_w):
    bsz, tp, _ = u.shape
    nchunk = tp // CHUNK
    win = CHUNK + 32
    nct = D_CONV // 128

    def body(dhc_ref, cv_ref, cg_ref, w_ref, dv_ref, dg_ref, dw_ref, h_scr, dhc_scr, hwin, dwin, dw_scr):
        h_scr[0:32, :] = jnp.zeros((32, 128), F32)
        h_scr[32:32 + tp, :] = cv_ref[0] * _sigmoid(cg_ref[0])
        dhc_scr[0:tp, :] = dhc_ref[0]
        dhc_scr[tp:tp + 32, :] = jnp.zeros((32, 128), F32)
        dw_scr[...] = jnp.zeros_like(dw_scr)

        def step(n, carry):
            r0 = pl.multiple_of(n * CHUNK, CHUNK)
            rows = pl.ds(r0, CHUNK)
            hwin[...] = h_scr[pl.ds(r0, win), :]
            dwin[...] = dhc_scr[pl.ds(r0, win), :]
            dcur = dwin[0:CHUNK, :]
            acc = jnp.zeros((CHUNK, 128), F32)
            for j in range(CONV_WIDTH):
                acc = acc + w_ref[j:j + 1, :] * dwin[30 - j:30 - j + CHUNK, :]
                prod = dcur * hwin[2 + j:2 + j + CHUNK, :]
                dw_scr[j * 8:(j + 1) * 8, :] += jnp.sum(prod.reshape(CHUNK // 8, 8, 128), axis=0)
            cg = cg_ref[0, rows, :]
            sig = _sigmoid(cg)
            rowid = n * CHUNK + lax.broadcasted_iota(jnp.int32, (CHUNK, 1), 0)
            dh = jnp.where(rowid >= PAD, acc, 0.0)
            dv_ref[0, rows, :] = (dh * sig).astype(BF16)
            dg_ref[0, rows, :] = (dh * cv_ref[0, rows, :] * sig * (1.0 - sig)).astype(BF16)
            return carry

        lax.fori_loop(0, nchunk, step, 0)
        dw_ref[0] = jnp.zeros((32, 128), F32)
        for j in range(CONV_WIDTH):
            dw_ref[0, j:j + 1, :] = jnp.sum(dw_scr[j * 8:(j + 1) * 8, :], axis=0, keepdims=True)

    blk = lambda off: pl.BlockSpec((1, tp, 128), lambda bi, t: (bi, 0, off // 128 + t))
    return pl.pallas_call(
        body, name="conv_bwd",
        out_shape=[jax.ShapeDtypeStruct((bsz, tp, D_CONV), BF16), jax.ShapeDtypeStruct((bsz, tp, D_CONV), BF16),
                   jax.ShapeDtypeStruct((bsz, 32, D_CONV), F32)],
        grid=(bsz, nct),
        in_specs=[blk(0), blk(C_VAL), blk(C_GATE), pl.BlockSpec((32, 128), lambda bi, t: (0, t))],
        out_specs=[blk(0), blk(0), pl.BlockSpec((1, 32, 128), lambda bi, t: (bi, 0, t))],
        scratch_shapes=[pltpu.VMEM((tp + 32, 128), F32), pltpu.VMEM((tp + 32, 128), F32),
                        pltpu.VMEM((win, 128), F32), pltpu.VMEM((win, 128), F32),
                        pltpu.VMEM((CONV_WIDTH * 8, 128), F32)],
        compiler_params=_params(("parallel", "parallel")),
    )(dhc, u, u, conv_w)


def _gla_bwd(dgla, u, o_pre, states, gup, gbias, gnorm):
    bsz, tp, _ = u.shape
    nchunk = tp // CHUNK

    def body(dy_ref, q_ref, k_ref, v_ref, r_ref, gd_ref, o_ref, st_ref, gup_ref, gb_ref, gn_ref,
             dq_ref, dk_ref, dv_ref, dr_ref, dgd_ref, dgup_ref, vec_ref, h_scr, gup_acc):
        tri, hmask = _gla_consts()
        triu = _gla_triu()
        h_scr[...] = jnp.zeros_like(h_scr)
        gup_acc[...] = jnp.zeros_like(gup_acc)
        gn = gn_ref[...]
        gupb = gup_ref[...]

        def step(i, carry):
            dbias, dgn = carry
            n = nchunk - 1 - i
            rows, valid, z, eb, enb, erest, dec, qe, ke, kd = _gla_chunk_terms(
                n, q_ref, k_ref, gd_ref, gup_ref, gb_ref, tri)
            keb, kdb = ke.astype(BF16), kd.astype(BF16)
            dqe = jnp.zeros((CHUNK, 128), F32)
            dke = jnp.zeros((CHUNK, 128), F32)
            dkd = jnp.zeros((CHUNK, 128), F32)
            ddec = jnp.zeros((1, 128), F32)
            for h in range(2):
                cols = slice(h * GLA_DV, (h + 1) * GLA_DV)
                o = o_ref[0, rows, cols]
                rh = r_ref[0, rows, cols]
                dy = dy_ref[0, rows, cols]
                rms = lax.rsqrt(jnp.mean(o * o, axis=-1, keepdims=True) + LN_EPS)
                nrm = o * rms
                sig = _sigmoid(rh)
                sw = rh * sig
                dr_ref[0, rows, cols] = (dy * nrm * gn * (sig * (1.0 + rh * (1.0 - sig)))).astype(BF16)
                dgn = dgn + jnp.sum(dy * nrm * sw, axis=0, keepdims=True)
                dn = dy * gn * sw
                do = rms * (dn - nrm * jnp.mean(dn * nrm, axis=-1, keepdims=True))
                dob = do.astype(BF16)
                qh = jnp.where(hmask[h], qe, 0.0).astype(BF16)
                vh = v_ref[0, rows, cols].astype(BF16)
                a = jnp.where(tri, _nt(qh, keb), 0.0).astype(BF16)
                st = st_ref[0, h, n]
                ht = h_scr[h]
                da = jnp.where(tri, _nt(dob, vh), 0.0).astype(BF16)
                dqe = dqe + jnp.where(hmask[h], _nn(da, keb) + _nn(dob, st.astype(BF16)), 0.0)
                dke = dke + _tn(da, qh)
                dv_ref[0, rows, cols] = (_tn(a, dob) + _nt(kdb, ht.astype(BF16))).astype(BF16)
                dkd = dkd + jnp.where(hmask[h], _nn(vh, ht.astype(BF16)), 0.0)
                ddec = ddec + jnp.where(hmask[h], jnp.sum(ht * st, axis=0, keepdims=True), 0.0)
                h_scr[h] = dec * ht + _tn(dob, qh)
            dq_ref[0, rows, :] = (dqe * eb * Q_SCALE).astype(BF16)
            dk_ref[0, rows, :] = (dke * enb + dkd * erest).astype(BF16)
            db = dqe * qe - dke * ke - dkd * kd
            dblast = jnp.sum(dkd * kd, axis=0, keepdims=True) + ddec * dec
            lastrow = lax.broadcasted_iota(jnp.int32, (CHUNK, 1), 0) == CHUNK - 1
            db = db + jnp.where(lastrow, dblast, 0.0)
            dlg = jnp.where(valid, _nn(triu, db, precision=lax.Precision.HIGHEST), 0.0)
            dz = dlg * (1.0 / GLA_TAU) * (1.0 - _sigmoid(z))
            dzb = dz.astype(BF16)
            dgd_ref[0, 0, rows, :] = _nt(dzb, gupb).astype(BF16)
            gup_acc[...] += _tn(gd_ref[0, rows, :].astype(BF16), dzb)
            return dbias + jnp.sum(dz, axis=0, keepdims=True), dgn

        zero = jnp.zeros((1, 128), F32)
        dbias, dgn = lax.fori_loop(0, nchunk, step, (zero, zero))
        dgup_ref[0] = gup_acc[...]
        vec_ref[0] = jnp.zeros((8, 128), F32)
        vec_ref[0, 0:1, :] = dbias
        vec_ref[0, 1:2, :] = dgn

    pair = lambda w, off: pl.BlockSpec((1, tp, w), lambda bi, p: (bi, 0, off // w + p))
    return pl.pallas_call(
        body, name="gla_bwd",
        out_shape=[jax.ShapeDtypeStruct((bsz, tp, 256), BF16), jax.ShapeDtypeStruct((bsz, tp, 256), BF16),
                   jax.ShapeDtypeStruct((bsz, tp, 512), BF16), jax.ShapeDtypeStruct((bsz, tp, 512), BF16),
                   jax.ShapeDtypeStruct((bsz, 2, tp, 128), BF16), jax.ShapeDtypeStruct((bsz, 128, 256), F32),
                   jax.ShapeDtypeStruct((bsz, 8, 256), F32)],
        grid=(bsz, 2),
        in_specs=[pair(256, 0), pair(128, C_Q), pair(128, C_K), pair(256, C_V), pair(256, C_R),
                  pl.BlockSpec((1, tp, 128), lambda bi, p: (bi, 0, C_GD // 128)),
                  pair(256, 0),
                  pl.BlockSpec((1, 2, nchunk, GLA_DV, 128), lambda bi, p: (bi, p, 0, 0, 0)),
                  pl.BlockSpec((128, 128), lambda bi, p: (0, p)),
                  pl.BlockSpec((1, 128), lambda bi, p: (0, p)),
                  pl.BlockSpec((1, 128), lambda bi, p: (0, 0))],
        out_specs=[pair(128, 0), pair(128, 0), pair(256, 0), pair(256, 0),
                   pl.BlockSpec((1, 1, tp, 128), lambda bi, p: (bi, p, 0, 0)),
                   pl.BlockSpec((1, 128, 128), lambda bi, p: (bi, 0, p)),
                   pl.BlockSpec((1, 8, 128), lambda bi, p: (bi, 0, p))],
        scratch_shapes=[pltpu.VMEM((2, GLA_DV, 128), F32), pltpu.VMEM((128, 128), F32)],
        compiler_params=_params(("parallel", "parallel")),
    )(dgla, u, u, u, u, u, o_pre, states, gup, gbias, gnorm)


_DU_OFFSETS = (C_VAL, C_GATE, C_Q, C_K, C_V, C_R)
_DU_WIDTHS = (512, 512, 256, 256, 512, 512)


def _du_specs(tm, per, row_map):
    specs = [pl.BlockSpec((tm, w), row_map) for w in _DU_WIDTHS]
    for p in range(2):
        specs.append(pl.BlockSpec((1, 1, tm, 128), lambda *ix, p=p: (row_map(*ix)[0] // per, p, row_map(*ix)[0] % per, 0)))
    return specs


def _du_pieces(refs):
    out = [(off, ref[...]) for off, ref in zip(_DU_OFFSETS, refs[:6])]
    dgd = (refs[6][0, 0].astype(F32) + refs[7][0, 0].astype(F32)).astype(BF16)
    out.append((C_GD, dgd))
    return out


def _in_proj_bwd(pieces, dgd, w_int, dt1, tp):
    r, d = dt1.shape
    tm = _row_tile(tp)
    per = tp // tm

    def body(*refs):
        w_ref, dt_ref, o_ref = refs[8:]
        acc = ALPHA * dt_ref[...]
        for off, val in _du_pieces(refs[:8]):
            acc = acc + _nn(val, w_ref[off:off + val.shape[1], :])
        o_ref[...] = acc

    row = lambda i: (i, 0)
    return pl.pallas_call(
        body, name="in_proj_bwd", out_shape=jax.ShapeDtypeStruct((r, d), F32),
        grid=(r // tm,),
        in_specs=_du_specs(tm, per, row) + [pl.BlockSpec((D_IN_PAD, d), lambda i: (0, 0)), pl.BlockSpec((tm, d), row)],
        out_specs=pl.BlockSpec((tm, d), row),
        compiler_params=_params(("parallel",)),
    )(*pieces, dgd, dgd, w_int, dt1)


def _grad_w_in(pieces, dgd, s0b, tp):
    r, d = s0b.shape
    tk = _row_tile(tp)
    per = tp // tk

    def body(*refs):
        s_ref, o_ref = refs[8:]

        @pl.when(pl.program_id(0) == 0)
        def _():
            o_ref[...] = jnp.zeros_like(o_ref)

        s = s_ref[...]
        for off, val in _du_pieces(refs[:8]):
            o_ref[off:off + val.shape[1], :] += _tn(val, s)

    row = lambda k: (k, 0)
    return pl.pallas_call(
        body, name="grad_w_in", out_shape=jax.ShapeDtypeStruct((D_IN_PAD, d), F32),
        grid=(r // tk,),
        in_specs=_du_specs(tk, per, row) + [pl.BlockSpec((tk, d), row)],
        out_specs=pl.BlockSpec((D_IN_PAD, d), lambda k: (0, 0)),
        compiler_params=_params(("arbitrary",)),
    )(*pieces, dgd, dgd, s0b)


def _ln_in_bwd(ds0, x, meta, g):
    bsz, s, d = x.shape
    tp = s + HEAD
    nh = 2
    sh = s // nh
    rc = min(256, sh)

    def body(ds_ref, x_ref, meta_ref, g_ref, gx_ref, dm_ref, vec_ref):
        h = pl.program_id(1)
        gg = g_ref[...]

        @pl.when(h == 0)
        def _():
            mh, mr = _ln_stats(meta_ref[...])
            dsm = ds_ref[0, PAD:HEAD, :]
            dm_ref[0] = _ln_bwd(dsm * gg, mh, mr)
            vec_ref[0] = jnp.zeros((8, d), F32)
            vec_ref[0, 0:1, :] = jnp.sum(dsm * mh, axis=0, keepdims=True)
            vec_ref[0, 1:2, :] = jnp.sum(dsm, axis=0, keepdims=True)

        def step(i, carry):
            sg, sb = carry
            dst = pl.ds(pl.multiple_of(i * rc, rc), rc)
            src = pl.ds(pl.multiple_of(HEAD + h * sh + i * rc, 64), rc)
            xh, rstd = _ln_stats(x_ref[0, dst, :])
            dsv = ds_ref[0, src, :]
            gx_ref[0, dst, :] = _ln_bwd(dsv * gg, xh, rstd)
            return sg + jnp.sum(dsv * xh, axis=0, keepdims=True), sb + jnp.sum(dsv, axis=0, keepdims=True)

        zero = jnp.zeros((1, d), F32)
        sg, sb = lax.fori_loop(0, sh // rc, step, (zero, zero))
        vec_ref[0, 0:1, :] += sg
        vec_ref[0, 1:2, :] += sb

    return pl.pallas_call(
        body, name="ln_in_bwd",
        out_shape=[jax.ShapeDtypeStruct((bsz, s, d), F32), jax.ShapeDtypeStruct((bsz, N_META, d), F32),
                   jax.ShapeDtypeStruct((bsz, 8, d), F32)],
        grid=(bsz, nh),
        in_specs=[pl.BlockSpec((1, tp, d), lambda bi, hi: (bi, 0, 0)),
                  pl.BlockSpec((1, sh, d), lambda bi, hi: (bi, hi, 0)),
                  pl.BlockSpec((N_META, d), lambda bi, hi: (0, 0)),
                  pl.BlockSpec((1, d), lambda bi, hi: (0, 0))],
        out_specs=[pl.BlockSpec((1, sh, d), lambda bi, hi: (bi, hi, 0)),
                   pl.BlockSpec((1, N_META, d), lambda bi, hi: (bi, 0, 0)),
                   pl.BlockSpec((1, 8, d), lambda bi, hi: (bi, 0, 0))],
        compiler_params=_params(("parallel", "arbitrary")),
    )(ds0, x, meta, g)


def _rows128(a):
    return a.reshape(-1, 128)


def kernel(x, meta_tokens, ln_in_g, ln_in_b, w_in, conv_w, conv_b, conv_ln_g, conv_ln_b, gate_up, gate_bias, gla_norm_g, w_out, ln1_g, ln1_b, w_ff1, w_ff2, ln2_g, ln2_b, loss_target, m_meta_tokens, m_ln_in_g, m_ln_in_b, m_w_in, m_conv_w, m_conv_b, m_conv_ln_g, m_conv_ln_b, m_gate_up, m_gate_bias, m_gla_norm_g, m_w_out, m_ln1_g, m_ln1_b, m_w_ff1, m_w_ff2, m_ln2_g, m_ln2_b, v_meta_tokens, v_ln_in_g, v_ln_in_b, v_w_in, v_conv_w, v_conv_b, v_conv_ln_g, v_conv_ln_b, v_gate_up, v_gate_bias, v_gla_norm_g, v_w_out, v_ln1_g, v_ln1_b, v_w_ff1, v_w_ff2, v_ln2_g, v_ln2_b):
    bsz, seq, d = x.shape
    tp = seq + HEAD
    r = bsz * tp
    xi, yi, ci = _mesh_pos()
    chip = 2 * xi + yi
    c_arr = jnp.reshape(ci, (1,)).astype(jnp.int32)
    pos_arr = jnp.stack([chip, ci]).astype(jnp.int32)

    sh_in = D_IN // 4
    shards = [jnp.pad(w_in[0].T.astype(BF16), ((0, D_IN_PAD // 4 - sh_in), (0, 0))), w_ff1[0].T.astype(BF16),
              w_out[0].astype(BF16), w_ff2[0].astype(BF16)]
    small_w = jnp.concatenate([_rows128(meta_tokens), _rows128(conv_w[0]), _rows128(gate_up[0])], axis=0)
    (g_int, g_w1t, g_wout, g_w2), g_small = _gather_weights(shards, small_w)
    w_int = jnp.pad(g_int[:, :sh_in].reshape(D_IN, d), ((0, D_IN_PAD - D_IN), (0, 0)))
    w1t = g_w1t.reshape(D_FF, d)
    wout = g_wout.reshape(d, d)
    w2 = g_w2.reshape(D_FF, d)
    n_meta_rows, n_cw_rows = N_META * 256 // 128, CONV_WIDTH * 128 // 128
    meta_full = jnp.concatenate([g_small[j, :n_meta_rows].reshape(N_META, 256) for j in range(4)], axis=1)
    convw_full = jnp.concatenate(
        [g_small[j, n_meta_rows:n_meta_rows + n_cw_rows].reshape(CONV_WIDTH, 128) for j in range(4)], axis=1)
    gup_full = jnp.concatenate(
        [g_small[j, n_meta_rows + n_cw_rows:].reshape(GLA_RANK, 64) for j in range(4)], axis=1)
    convw_p = jnp.pad(convw_full, ((0, 1), (0, 0)))
    gup_p = jnp.pad(gup_full, ((0, 128 - GLA_RANK), (0, 0))).astype(BF16)
    ln_in_g2, ln_in_b2 = ln_in_g.reshape(1, d), ln_in_b.reshape(1, d)

    s0, s0b = _ln_in_fwd(x, meta_full, ln_in_g2, ln_in_b2)
    tgt_p = jnp.pad(loss_target, ((0, 0), (HEAD, 0), (0, 0)))
    s0f, s0bf = s0.reshape(r, d), s0b.reshape(r, d)
    u = _in_proj(s0bf, w_int)
    u3 = u.reshape(bsz, tp, D_IN_PAD)
    hc = _conv_fwd(u3, convw_p, conv_b).reshape(r, D_CONV)
    gla_out, o_pre, states = _gla_fwd(u3, gup_p, gate_bias, gla_norm_g)
    gla_of = gla_out.reshape(r, 512)
    conv_of, xhat1, rstd1, s1b = _out_proj_ln1(hc, gla_of, wout, s0f, conv_ln_g, conv_ln_b, ln1_g, ln1_b)
    ra = _ffn1(s1b, w1t)
    dt2, dt2b, acc2 = _ffn2_ln2_loss(ra, w2, xhat1, ln1_g, ln1_b, ln2_g, ln2_b, tgt_p.reshape(r, d), tp)

    da = _ffn_bwd_da(dt2b, w2, ra)
    dt1, dt1b, acc1 = _ffn_bwd_ln1(da, w1t, dt2, xhat1, rstd1, ln1_g)
    g_w2 = _matmul_tn(ra, dt2b, 1024, square_lhs=True, name="grad_w_ff2")
    g_w1t = _matmul_tn(da, s1b, 1024, name="grad_w_ff1")
    dhc, dgla, cacc = _out_proj_bwd(dt1b, wout, hc, conv_ln_g, conv_ln_b)
    g_wout = jnp.concatenate([_matmul_tn(conv_of, dt1b, D_CONV, name="grad_w_out_conv"),
                              _matmul_tn(gla_of, dt1b, 512, name="grad_w_out_gla")], axis=0)
    dcv, dcg, dcw = _conv_bwd(dhc.reshape(bsz, tp, D_CONV), u3, convw_p)
    dq, dk, dv, dr, dgd, dgup, gvec = _gla_bwd(dgla.reshape(bsz, tp, 512), u3, o_pre, states, gup_p, gate_bias,
                                               gla_norm_g)
    pieces = [a.reshape(r, a.shape[-1]) for a in (dcv, dcg, dq, dk, dv, dr)]
    ds0 = _in_proj_bwd(pieces, dgd, w_int, dt1, tp)
    g_wint = _grad_w_in(pieces, dgd, s0bf, tp)
    grad_x, dmeta, lvec = _ln_in_bwd(ds0.reshape(bsz, tp, d), x, meta_full, ln_in_g2)

    small = {
        "loss": acc2[0:1], "ln_in_g": jnp.sum(lvec[:, 0], axis=0), "ln_in_b": jnp.sum(lvec[:, 1], axis=0),
        "conv_b": cacc[0], "conv_ln_g": cacc[1], "conv_ln_b": cacc[2], "gate_bias": jnp.sum(gvec[:, 0], axis=0),
        "gla_norm_g": jnp.sum(gvec[:, 1].reshape(bsz * 2, 128), axis=0),
        "ln1_g": acc1[0], "ln1_b": acc1[1], "ln2_g": acc2[1], "ln2_b": acc2[2],
        "conv_w": jnp.sum(dcw, axis=0)[:CONV_WIDTH], "gate_up": jnp.sum(dgup, axis=0)[:GLA_RANK],
        "meta_tokens": jnp.sum(dmeta, axis=0),
    }
    pack = jnp.concatenate([_rows128(small[k]) for k, _ in _SMALL_FIELDS], axis=0)
    pack = jnp.pad(pack, ((0, SMALL_ROWS - pack.shape[0]), (0, 0)))
    red = _sum_small(_allgather_small(pack))
    off, tot = {}, 0
    for k, nrows in _SMALL_FIELDS:
        off[k] = (tot, nrows)
        tot += nrows

    def field(k, shape):
        o, nrows = off[k]
        return red[o:o + nrows].reshape(shape)

    loss = jnp.sum(field("loss", (d,)))

    big = [g_wint[:D_IN].reshape(4, D_IN // 4, d), g_w1t.reshape(4, D_FF // 4, d),
           g_wout.reshape(4, d // 4, d), g_w2.reshape(4, D_FF // 4, d)]
    pair_got = _pair_exchange(big)
    parts = [_add_pair(g, t, c_arr) for g, t in zip(big, pair_got)]
    chip_got = _chip_exchange(parts)
    halves = [_add_chips(g, t, s, pos_arr) for g, t, s in zip(big, pair_got, chip_got)]
    f_wint, f_w1t, f_wout, f_w2 = [
        jnp.concatenate([jnp.where(ci == 0, own, sib), jnp.where(ci == 0, sib, own)], axis=1)
        for own, sib in zip(halves, _pair_swap(halves))]

    def cols(a, width):
        return lax.dynamic_slice_in_dim(a, chip * width, width, axis=a.ndim - 1)

    grads = {
        "meta_tokens": cols(field("meta_tokens", (N_META, d)), 256),
        "ln_in_g": field("ln_in_g", (d,)), "ln_in_b": field("ln_in_b", (d,)),
        "w_in": f_wint.T[None], "conv_w": cols(field("conv_w", (CONV_WIDTH, D_CONV)), 128)[None],
        "conv_b": field("conv_b", (1, D_CONV)), "conv_ln_g": field("conv_ln_g", (1, D_CONV)),
        "conv_ln_b": field("conv_ln_b", (1, D_CONV)),
        "gate_up": cols(field("gate_up", (GLA_RANK, 256)), 64)[None],
        "gate_bias": field("gate_bias", (1, 256)), "gla_norm_g": field("gla_norm_g", (1, 128)),
        "w_out": f_wout[None], "ln1_g": field("ln1_g", (1, d)), "ln1_b": field("ln1_b", (1, d)),
        "w_ff1": f_w1t.T[None], "w_ff2": f_w2[None],
        "ln2_g": field("ln2_g", (1, d)), "ln2_b": field("ln2_b", (1, d)),
    }
    weights = dict(meta_tokens=meta_tokens, ln_in_g=ln_in_g, ln_in_b=ln_in_b, w_in=w_in, conv_w=conv_w, conv_b=conv_b,
                   conv_ln_g=conv_ln_g, conv_ln_b=conv_ln_b, gate_up=gate_up, gate_bias=gate_bias,
                   gla_norm_g=gla_norm_g, w_out=w_out, ln1_g=ln1_g, ln1_b=ln1_b, w_ff1=w_ff1, w_ff2=w_ff2,
                   ln2_g=ln2_g, ln2_b=ln2_b)
    moms = dict(meta_tokens=(m_meta_tokens, v_meta_tokens), ln_in_g=(m_ln_in_g, v_ln_in_g),
                ln_in_b=(m_ln_in_b, v_ln_in_b), w_in=(m_w_in, v_w_in), conv_w=(m_conv_w, v_conv_w),
                conv_b=(m_conv_b, v_conv_b), conv_ln_g=(m_conv_ln_g, v_conv_ln_g),
                conv_ln_b=(m_conv_ln_b, v_conv_ln_b), gate_up=(m_gate_up, v_gate_up),
                gate_bias=(m_gate_bias, v_gate_bias), gla_norm_g=(m_gla_norm_g, v_gla_norm_g),
                w_out=(m_w_out, v_w_out), ln1_g=(m_ln1_g, v_ln1_g), ln1_b=(m_ln1_b, v_ln1_b),
                w_ff1=(m_w_ff1, v_w_ff1), w_ff2=(m_w_ff2, v_w_ff2), ln2_g=(m_ln2_g, v_ln2_g),
                ln2_b=(m_ln2_b, v_ln2_b))
    names = list(weights)
    big_names = ("w_in", "w_out", "w_ff1", "w_ff2")
    delta, new_m, new_v = {}, {}, {}
    dl, mn, vn = _adamw(w_in[0].T, f_wint, m_w_in[0].T, v_w_in[0].T)
    delta["w_in"], new_m["w_in"], new_v["w_in"] = dl.T[None], mn.T[None], vn.T[None]
    for k in big_names[1:]:
        shp = weights[k].shape
        two = lambda a: a.reshape(shp[-2], shp[-1])
        dl, mn, vn = _adamw(two(weights[k]), two(grads[k]), two(moms[k][0]), two(moms[k][1]))
        delta[k], new_m[k], new_v[k] = dl.reshape(shp), mn.reshape(shp), vn.reshape(shp)
    small_names = [k for k in names if k not in big_names]
    sizes = [weights[k].size for k in small_names]
    total = sum(sizes)
    padded = -(-total // 1024) * 1024

    def flat(get):
        v = jnp.concatenate([get(k).reshape(-1) for k in small_names])
        return jnp.pad(v, (0, padded - total)).reshape(padded // 128, 128)

    ones = jnp.ones((padded - total,), F32)
    vflat = jnp.concatenate([jnp.concatenate([moms[k][1].reshape(-1) for k in small_names]), ones])
    dl, mn, vn = _adamw(flat(lambda k: weights[k]), flat(lambda k: grads[k]), flat(lambda k: moms[k][0]),
                        vflat.reshape(padded // 128, 128))
    pos = 0
    for k, sz in zip(small_names, sizes):
        shp = weights[k].shape
        delta[k] = dl.reshape(-1)[pos:pos + sz].reshape(shp)
        new_m[k] = mn.reshape(-1)[pos:pos + sz].reshape(shp)
        new_v[k] = vn.reshape(-1)[pos:pos + sz].reshape(shp)
        pos += sz

    return (loss, grad_x, *[grads[k] for k in names], *[delta[k] for k in names],
            *[new_m[k] for k in names], *[new_v[k] for k in names])
```

```python
import functools

import jax
import jax.numpy as jnp
from jax import lax
from jax.experimental import pallas as pl
from jax.experimental.pallas import tpu as pltpu

F32 = jnp.float32
BF16 = jnp.bfloat16

D_MODEL = 1024
N_META = 16
D_CONV = 512
CONV_WIDTH = 31
GLA_HEADS = 4
GLA_DV = 128
GLA_DK = 64
GLA_RANK = 16
GLA_TAU = 16.0
CHUNK = 64
D_FF = 4096
LN_EPS = 1e-5
ALPHA = 2.0 ** 0.25
D_IN = 2576
D_IN_PAD = 2688
PAD = CHUNK - N_META
HEAD = PAD + N_META
Q_SCALE = GLA_DK ** -0.5
ADAM_LR, ADAM_B1, ADAM_B2, ADAM_EPS, ADAM_WD, ADAM_STEP = 0.001, 0.9, 0.999, 1e-08, 0.01, 10
HALF = D_MODEL // 2
VMEM_LIMIT = 56 * 1024 * 1024
MESH = pl.DeviceIdType.MESH

C_VAL, C_GATE, C_Q, C_K, C_V, C_R, C_GD = 0, 512, 1024, 1280, 1536, 2048, 2560

_SMALL_FIELDS = (("loss", 8), ("ln_in_g", 8), ("ln_in_b", 8), ("conv_b", 4), ("conv_ln_g", 4), ("conv_ln_b", 4),
                 ("gate_bias", 2), ("gla_norm_g", 1), ("ln1_g", 8), ("ln1_b", 8), ("ln2_g", 8), ("ln2_b", 8),
                 ("conv_w", 124), ("gate_up", 32), ("meta_tokens", 128))
SMALL_ROWS = 360


def _params(sem=None, **kw):
    return pltpu.CompilerParams(dimension_semantics=sem, vmem_limit_bytes=VMEM_LIMIT, **kw)


def _row_tile(tp):
    for t in (704, 352, 192, 64):
        if tp % t == 0:
            return t
    raise ValueError(tp)


def _dot(a, b, dims, precision=None):
    return lax.dot_general(a, b, (dims, ((), ())), preferred_element_type=F32, precision=precision)


def _nn(a, b, **kw):
    return _dot(a, b, ((1,), (0,)), **kw)


def _nt(a, b, **kw):
    return _dot(a, b, ((1,), (1,)), **kw)


def _tn(a, b, **kw):
    return _dot(a, b, ((0,), (0,)), **kw)


def _sigmoid(x):
    return 1.0 / (1.0 + jnp.exp(-x))


def _log_sigmoid(z):
    return jnp.minimum(z, 0.0) - jnp.log(1.0 + jnp.exp(-jnp.abs(z)))


def _ln_stats(t):
    mu = jnp.mean(t, axis=-1, keepdims=True)
    d = t - mu
    var = jnp.mean(d * d, axis=-1, keepdims=True)
    rstd = lax.rsqrt(var + LN_EPS)
    return d * rstd, rstd


def _ln_bwd(dxhat, xhat, rstd):
    m1 = jnp.mean(dxhat, axis=-1, keepdims=True)
    m2 = jnp.mean(dxhat * xhat, axis=-1, keepdims=True)
    return rstd * (dxhat - m1 - xhat * m2)


def _mesh_pos():
    return lax.axis_index("x"), lax.axis_index("y"), lax.axis_index("c")


ANY = pl.BlockSpec(memory_space=pl.ANY)


def _gather_weights(shards, small):
    n = len(shards)
    half = [s.shape[0] // 2 for s in shards]

    def body(*refs):
        ins, small_in = refs[:n], refs[n]
        outs, small_out = refs[n + 1:2 * n + 1], refs[2 * n + 1]
        send, recv, fsend, frecv, ssend, srecv = refs[2 * n + 2:]
        x, y, c = _mesh_pos()
        me = 2 * x + y
        sibling = (x, y, 1 - c)
        chips = [(1 - x, y), (x, 1 - y), (1 - x, 1 - y)]
        chip_idx = [2 * px + py for px, py in chips]
        mine = [pl.ds(pl.multiple_of(c * h, 16), h) for h in half]
        other = [pl.ds(pl.multiple_of((1 - c) * h, 16), h) for h in half]

        def ici(a, k):
            return pltpu.make_async_remote_copy(
                src_ref=ins[a].at[mine[a], :], dst_ref=outs[a].at[me, mine[a], :],
                send_sem=send.at[a, k], recv_sem=recv.at[a, k], device_id=(*chips[k], c), device_id_type=MESH)

        def landed(a, k):
            return pltpu.make_async_remote_copy(
                src_ref=ins[a].at[mine[a], :], dst_ref=outs[a].at[chip_idx[k], mine[a], :],
                send_sem=send.at[a, k], recv_sem=recv.at[a, k], device_id=(*chips[k], c), device_id_type=MESH)

        def forward(a, k, rows):
            blk = outs[a].at[chip_idx[k], rows[a], :]
            return pltpu.make_async_remote_copy(
                src_ref=blk, dst_ref=blk, send_sem=fsend.at[a, k], recv_sem=frecv.at[a, k],
                device_id=sibling, device_id_type=MESH)

        def small_copy(k, slot):
            return pltpu.make_async_remote_copy(
                src_ref=small_in, dst_ref=small_out.at[slot], send_sem=ssend.at[k], recv_sem=srecv.at[k],
                device_id=(*chips[k], c), device_id_type=MESH)

        for a in range(n):
            for k in range(3):
                ici(a, k).start()
        for k in range(3):
            small_copy(k, me).start()
        for a in range(n):
            for k in range(3):
                landed(a, k).wait_recv()
                forward(a, k, mine).start()
        for a in range(n):
            for k in range(3):
                forward(a, k, other).wait_recv()
        for k in range(3):
            small_copy(k, chip_idx[k]).wait_recv()
        for a in range(n):
            for k in range(3):
                ici(a, k).wait_send()
                forward(a, k, mine).wait_send()
        for k in range(3):
            small_copy(k, me).wait_send()

    out_shape = [jax.ShapeDtypeStruct((4,) + s.shape, s.dtype) for s in shards]
    out_shape.append(jax.ShapeDtypeStruct((4,) + small.shape, small.dtype))
    res = pl.pallas_call(
        body, name="gather_weights", out_shape=out_shape,
        in_specs=[ANY] * (n + 1), out_specs=[ANY] * (n + 1),
        scratch_shapes=[pltpu.SemaphoreType.DMA((n, 3)), pltpu.SemaphoreType.DMA((n, 3)),
                        pltpu.SemaphoreType.DMA((n, 3)), pltpu.SemaphoreType.DMA((n, 3)),
                        pltpu.SemaphoreType.DMA((3,)), pltpu.SemaphoreType.DMA((3,))],
    )(*shards, small)
    chip = 2 * lax.axis_index("x") + lax.axis_index("y")
    res = [lax.dynamic_update_slice(g, s[None], (chip, 0, 0)) for g, s in zip(res, list(shards) + [small])]
    return res[:n], res[n]


def _pair_exchange(grads):
    n = len(grads)

    def body(*refs):
        ins, outs = refs[:n], refs[n:2 * n]
        send, recv = refs[2 * n:]
        x, y, c = _mesh_pos()
        other = pl.ds(pl.multiple_of((1 - c) * HALF, 128), HALF)
        cps = [pltpu.make_async_remote_copy(
            src_ref=ins[a].at[:, :, other], dst_ref=outs[a], send_sem=send.at[a], recv_sem=recv.at[a],
            device_id=(x, y, 1 - c), device_id_type=MESH) for a in range(n)]
        for cp in cps:
            cp.start()
        for cp in cps:
            cp.wait()

    return pl.pallas_call(
        body, name="grad_pair_exchange",
        out_shape=[jax.ShapeDtypeStruct(g.shape[:2] + (HALF,), g.dtype) for g in grads],
        in_specs=[ANY] * n, out_specs=[ANY] * n,
        scratch_shapes=[pltpu.SemaphoreType.DMA((n,)), pltpu.SemaphoreType.DMA((n,))],
    )(*grads)


def _chip_exchange(parts):
    n = len(parts)

    def body(*refs):
        ins, outs = refs[:n], refs[n:2 * n]
        send, recv = refs[2 * n:]
        x, y, c = _mesh_pos()
        chips = [(1 - x, y), (x, 1 - y), (1 - x, 1 - y)]
        cps = [pltpu.make_async_remote_copy(
            src_ref=ins[a].at[2 * px + py], dst_ref=outs[a].at[k], send_sem=send.at[a, k], recv_sem=recv.at[a, k],
            device_id=(px, py, c), device_id_type=MESH) for a in range(n) for k, (px, py) in enumerate(chips)]
        for cp in cps:
            cp.start()
        for cp in cps:
            cp.wait()

    return pl.pallas_call(
        body, name="grad_chip_exchange",
        out_shape=[jax.ShapeDtypeStruct((3,) + p.shape[1:], p.dtype) for p in parts],
        in_specs=[ANY] * n, out_specs=[ANY] * n,
        scratch_shapes=[pltpu.SemaphoreType.DMA((n, 3)), pltpu.SemaphoreType.DMA((n, 3))],
    )(*parts)


def _pair_swap(halves):
    n = len(halves)

    def body(*refs):
        ins, outs = refs[:n], refs[n:2 * n]
        send, recv = refs[2 * n:]
        x, y, c = _mesh_pos()
        cps = [pltpu.make_async_remote_copy(
            src_ref=ins[a], dst_ref=outs[a], send_sem=send.at[a], recv_sem=recv.at[a],
            device_id=(x, y, 1 - c), device_id_type=MESH) for a in range(n)]
        for cp in cps:
            cp.start()
        for cp in cps:
            cp.wait()

    return pl.pallas_call(
        body, name="grad_pair_swap",
        out_shape=[jax.ShapeDtypeStruct(h.shape, h.dtype) for h in halves],
        in_specs=[ANY] * n, out_specs=[ANY] * n,
        scratch_shapes=[pltpu.SemaphoreType.DMA((n,)), pltpu.SemaphoreType.DMA((n,))],
    )(*halves)


def _allgather_small(pack):
    m_per, ncol = pack.shape

    def body(x_ref, out_ref, send_sems, recv_sems, local_sem):
        x, y, c = _mesh_pos()
        me, sibling = (x, y, c), (x, y, 1 - c)
        chips = [(1 - x, y), (x, 1 - y), (1 - x, 1 - y)]

        def rows(px, py, pc):
            return out_ref.at[pl.ds(pl.multiple_of((4 * px + 2 * py + pc) * m_per, 8), m_per), :]

        def copy(k, block, to, src=None):
            return pltpu.make_async_remote_copy(
                src_ref=rows(*block) if src is None else src, dst_ref=rows(*block),
                send_sem=send_sems.at[k], recv_sem=recv_sems.at[k], device_id=to, device_id_type=MESH)

        mine = pltpu.make_async_copy(x_ref, rows(*me), local_sem)
        mine.start()
        first = [copy(0, me, sibling, src=x_ref)]
        first += [copy(1 + j, me, (*chip, c), src=x_ref) for j, chip in enumerate(chips)]
        for cp in first:
            cp.start()
        passed = [copy(4 + j, (*chip, c), sibling) for j, chip in enumerate(chips)]
        for j, chip in enumerate(chips):
            copy(1 + j, (*chip, c), me).wait_recv()
            passed[j].start()
        copy(0, sibling, me).wait_recv()
        for j, chip in enumerate(chips):
            copy(4 + j, (*chip, 1 - c), me).wait_recv()
        for cp in first + passed:
            cp.wait_send()
        mine.wait()

    return pl.pallas_call(
        body, name="allgather_small",
        out_shape=jax.ShapeDtypeStruct((8 * m_per, ncol), pack.dtype),
        in_specs=[pl.BlockSpec(memory_space=pltpu.VMEM)],
        out_specs=pl.BlockSpec(memory_space=pltpu.VMEM),
        scratch_shapes=[pltpu.SemaphoreType.DMA((7,)), pltpu.SemaphoreType.DMA((7,)), pltpu.SemaphoreType.DMA],
    )(pack)


def _add_pair(g, got, c_arr):
    _, rows, _ = g.shape

    def body(c_ref, g_ref, r_ref, o_ref):
        o_ref[...] = (g_ref[...] + r_ref[...]).astype(BF16)

    return pl.pallas_call(
        body, name="grad_add_pair", out_shape=jax.ShapeDtypeStruct((4, rows, HALF), BF16),
        grid_spec=pltpu.PrefetchScalarGridSpec(
            num_scalar_prefetch=1, grid=(4,),
            in_specs=[pl.BlockSpec((1, rows, HALF), lambda j, c: (j, 0, c[0])),
                      pl.BlockSpec((1, rows, HALF), lambda j, c: (j, 0, 0))],
            out_specs=pl.BlockSpec((1, rows, HALF), lambda j, c: (j, 0, 0))),
        compiler_params=_params(("arbitrary",)),
    )(c_arr, g, got)


def _add_chips(g, pair_got, chip_got, pos_arr):
    _, rows, _ = g.shape

    def body(pos_ref, g_ref, p_ref, r_ref, o_ref):
        own = g_ref[0] + p_ref[0]
        o_ref[...] = ((own + r_ref[0].astype(F32)) + r_ref[1].astype(F32)) + r_ref[2].astype(F32)

    return pl.pallas_call(
        body, name="grad_add_chips", out_shape=jax.ShapeDtypeStruct((rows, HALF), F32),
        grid_spec=pltpu.PrefetchScalarGridSpec(
            num_scalar_prefetch=1, grid=(1,),
            in_specs=[pl.BlockSpec((1, rows, HALF), lambda i, p: (p[0], 0, p[1])),
                      pl.BlockSpec((1, rows, HALF), lambda i, p: (p[0], 0, 0)),
                      pl.BlockSpec((3, rows, HALF), lambda i, p: (0, 0, 0))],
            out_specs=pl.BlockSpec((rows, HALF), lambda i, p: (0, 0))),
        compiler_params=_params(("arbitrary",)),
    )(pos_arr, g, pair_got, chip_got)


def _sum_small(gathered):
    def body(g_ref, o_ref):
        acc = g_ref[0:SMALL_ROWS, :]
        for d in range(1, 8):
            acc = acc + g_ref[d * SMALL_ROWS:(d + 1) * SMALL_ROWS, :]
        o_ref[...] = acc

    return pl.pallas_call(
        body, name="sum_small", out_shape=jax.ShapeDtypeStruct((SMALL_ROWS, 128), F32),
    )(gathered)


def _adamw(w, g, m, v):
    rows, cols = w.shape
    tr = 256 if rows % 256 == 0 else rows
    c1 = 1.0 - ADAM_B1 ** ADAM_STEP
    c2 = 1.0 - ADAM_B2 ** ADAM_STEP

    def body(w_ref, g_ref, m_ref, v_ref, d_ref, mo_ref, vo_ref):
        gg = g_ref[...]
        mn = ADAM_B1 * m_ref[...] + (1.0 - ADAM_B1) * gg
        vn = ADAM_B2 * v_ref[...] + (1.0 - ADAM_B2) * (gg * gg)
        d_ref[...] = -ADAM_LR * ((mn / c1) / (jnp.sqrt(vn / c2) + ADAM_EPS) + ADAM_WD * w_ref[...])
        mo_ref[...] = mn
        vo_ref[...] = vn

    spec = pl.BlockSpec((tr, cols), lambda i: (i, 0))
    return pl.pallas_call(
        body, name="adamw", out_shape=[jax.ShapeDtypeStruct(w.shape, F32)] * 3,
        grid=(rows // tr,), in_specs=[spec] * 4, out_specs=[spec] * 3,
        compiler_params=_params(("parallel",)),
    )(w, g, m, v)


def _ln_in_fwd(x, meta, g, b):
    bsz, s, d = x.shape
    tp = s + HEAD
    nh = 2
    sh = s // nh
    rc = min(256, sh)

    def body(x_ref, meta_ref, g_ref, b_ref, s0_ref, s0b_ref):
        h = pl.program_id(1)
        gg, bb = g_ref[...], b_ref[...]

        @pl.when(h == 0)
        def _():
            s0_ref[0, 0:PAD, :] = jnp.zeros((PAD, d), F32)
            s0b_ref[0, 0:PAD, :] = jnp.zeros((PAD, d), BF16)
            mh, _ = _ln_stats(meta_ref[...])
            mv = mh * gg + bb
            s0_ref[0, PAD:HEAD, :] = mv
            s0b_ref[0, PAD:HEAD, :] = mv.astype(BF16)

        def step(i, carry):
            src = pl.ds(pl.multiple_of(i * rc, rc), rc)
            dst = pl.ds(pl.multiple_of(HEAD + h * sh + i * rc, 64), rc)
            xh, _ = _ln_stats(x_ref[0, src, :])
            val = xh * gg + bb
            s0_ref[0, dst, :] = val
            s0b_ref[0, dst, :] = val.astype(BF16)
            return carry

        lax.fori_loop(0, sh // rc, step, 0)

    full = lambda bi, hi: (bi, 0, 0)
    return pl.pallas_call(
        body, name="ln_in_fwd",
        out_shape=[jax.ShapeDtypeStruct((bsz, tp, d), F32), jax.ShapeDtypeStruct((bsz, tp, d), BF16)],
        grid=(bsz, nh),
        in_specs=[pl.BlockSpec((1, sh, d), lambda bi, hi: (bi, hi, 0)),
                  pl.BlockSpec((N_META, d), lambda bi, hi: (0, 0)),
                  pl.BlockSpec((1, d), lambda bi, hi: (0, 0)),
                  pl.BlockSpec((1, d), lambda bi, hi: (0, 0))],
        out_specs=[pl.BlockSpec((1, tp, d), full)] * 2,
        compiler_params=_params(("parallel", "arbitrary")),
    )(x, meta, g, b)


def _in_proj(s0b, w_int):
    r, d = s0b.shape
    tm = _row_tile(r)

    def body(a_ref, w_ref, o_ref):
        o_ref[...] = _nt(a_ref[...], w_ref[...])

    return pl.pallas_call(
        body, name="in_proj", out_shape=jax.ShapeDtypeStruct((r, D_IN_PAD), F32),
        grid=(r // tm,),
        in_specs=[pl.BlockSpec((tm, d), lambda i: (i, 0)), pl.BlockSpec((D_IN_PAD, d), lambda i: (0, 0))],
        out_specs=pl.BlockSpec((tm, D_IN_PAD), lambda i: (i, 0)),
        compiler_params=_params(("parallel",)),
    )(s0b, w_int)


def _conv_fwd(u, conv_w, conv_b):
    bsz, tp, _ = u.shape
    nchunk = tp // CHUNK
    win = CHUNK + 32
    nct = D_CONV // 128

    def body(cv_ref, cg_ref, w_ref, cb_ref, hc_ref, h_scr, win_scr):
        h_scr[0:32, :] = jnp.zeros((32, 128), F32)
        h_scr[32:32 + tp, :] = cv_ref[0] * _sigmoid(cg_ref[0])
        cb = cb_ref[...]

        def step(n, carry):
            r0 = pl.multiple_of(n * CHUNK, CHUNK)
            win_scr[...] = h_scr[pl.ds(r0, win), :]
            acc = jnp.zeros((CHUNK, 128), F32)
            for j in range(CONV_WIDTH):
                acc = acc + w_ref[j:j + 1, :] * win_scr[2 + j:2 + j + CHUNK, :]
            hc_ref[0, pl.ds(r0, CHUNK), :] = acc + cb
            return carry

        lax.fori_loop(0, nchunk, step, 0)

    return pl.pallas_call(
        body, name="conv_fwd", out_shape=jax.ShapeDtypeStruct((bsz, tp, D_CONV), F32),
        grid=(bsz, nct),
        in_specs=[pl.BlockSpec((1, tp, 128), lambda bi, t: (bi, 0, C_VAL // 128 + t)),
                  pl.BlockSpec((1, tp, 128), lambda bi, t: (bi, 0, C_GATE // 128 + t)),
                  pl.BlockSpec((32, 128), lambda bi, t: (0, t)),
                  pl.BlockSpec((1, 128), lambda bi, t: (0, t))],
        out_specs=pl.BlockSpec((1, tp, 128), lambda bi, t: (bi, 0, t)),
        scratch_shapes=[pltpu.VMEM((tp + 32, 128), F32), pltpu.VMEM((win, 128), F32)],
        compiler_params=_params(("parallel", "parallel")),
    )(u, u, conv_w, conv_b)


def _chunk_loop(nchunk, step, init):
    unroll = 3 if nchunk % 3 == 0 else 1

    def trip(t, carry):
        for j in range(unroll):
            carry = step(t * unroll + j, carry)
        return carry

    return lax.fori_loop(0, nchunk // unroll, trip, init)


def _gla_consts():
    row = lax.broadcasted_iota(jnp.int32, (CHUNK, CHUNK), 0)
    col = lax.broadcasted_iota(jnp.int32, (CHUNK, CHUNK), 1)
    lane = lax.broadcasted_iota(jnp.int32, (1, 128), 1)
    return row >= col, [lane < GLA_DK, lane >= GLA_DK]


def _gla_triu():
    row = lax.broadcasted_iota(jnp.int32, (CHUNK, CHUNK), 0)
    col = lax.broadcasted_iota(jnp.int32, (CHUNK, CHUNK), 1)
    return (row <= col).astype(F32)


def _gla_chunk_terms(n, q_ref, k_ref, gd_ref, gup_ref, gb_ref, tri):
    r0 = pl.multiple_of(n * CHUNK, CHUNK)
    rows = pl.ds(r0, CHUNK)
    z = _nn(gd_ref[0, rows, :].astype(BF16), gup_ref[...]) + gb_ref[...]
    rowid = n * CHUNK + lax.broadcasted_iota(jnp.int32, (CHUNK, 1), 0)
    valid = rowid >= PAD
    lg = jnp.where(valid, _log_sigmoid(z) * (1.0 / GLA_TAU), 0.0)
    bcum = _nn(tri.astype(F32), lg, precision=lax.Precision.HIGHEST)
    blast = bcum[CHUNK - 1:CHUNK, :]
    eb = jnp.exp(bcum)
    enb = jnp.exp(-bcum)
    erest = jnp.exp(blast - bcum)
    q = q_ref[0, rows, :] * Q_SCALE
    k = k_ref[0, rows, :]
    return rows, valid, z, eb, enb, erest, jnp.exp(blast), q * eb, k * enb, k * erest


def _gla_fwd(u, gup, gbias, gnorm):
    bsz, tp, _ = u.shape
    nchunk = tp // CHUNK

    def body(q_ref, k_ref, v_ref, r_ref, gd_ref, gup_ref, gb_ref, gn_ref, out_ref, o_ref, st_ref, s_scr):
        tri, hmask = _gla_consts()
        s_scr[...] = jnp.zeros_like(s_scr)
        gn = gn_ref[...]

        def step(n, carry):
            rows, _, _, _, _, _, dec, qe, ke, kd = _gla_chunk_terms(n, q_ref, k_ref, gd_ref, gup_ref, gb_ref, tri)
            keb, kdb = ke.astype(BF16), kd.astype(BF16)
            for h in range(2):
                cols = slice(h * GLA_DV, (h + 1) * GLA_DV)
                qh = jnp.where(hmask[h], qe, 0.0).astype(BF16)
                vh = v_ref[0, rows, cols].astype(BF16)
                a = jnp.where(tri, _nt(qh, keb), 0.0)
                st = s_scr[h]
                st_ref[0, h, n] = st
                o = _nn(a.astype(BF16), vh) + _nt(qh, st.astype(BF16))
                s_scr[h] = dec * st + _tn(vh, kdb)
                o_ref[0, rows, cols] = o
                rms = lax.rsqrt(jnp.mean(o * o, axis=-1, keepdims=True) + LN_EPS)
                rh = r_ref[0, rows, cols]
                out_ref[0, rows, cols] = (o * rms * gn * (rh * _sigmoid(rh))).astype(BF16)
            return carry

        _chunk_loop(nchunk, step, 0)

    return pl.pallas_call(
        body, name="gla_fwd",
        out_shape=[jax.ShapeDtypeStruct((bsz, tp, 512), BF16), jax.ShapeDtypeStruct((bsz, tp, 512), F32),
                   jax.ShapeDtypeStruct((bsz, GLA_HEADS, nchunk, GLA_DV, 128), F32)],
        grid=(bsz, 2),
        in_specs=[pl.BlockSpec((1, tp, 128), lambda bi, p: (bi, 0, C_Q // 128 + p)),
                  pl.BlockSpec((1, tp, 128), lambda bi, p: (bi, 0, C_K // 128 + p)),
                  pl.BlockSpec((1, tp, 256), lambda bi, p: (bi, 0, C_V // 256 + p)),
                  pl.BlockSpec((1, tp, 256), lambda bi, p: (bi, 0, C_R // 256 + p)),
                  pl.BlockSpec((1, tp, 128), lambda bi, p: (bi, 0, C_GD // 128)),
                  pl.BlockSpec((128, 128), lambda bi, p: (0, p)),
                  pl.BlockSpec((1, 128), lambda bi, p: (0, p)),
                  pl.BlockSpec((1, 128), lambda bi, p: (0, 0))],
        out_specs=[pl.BlockSpec((1, tp, 256), lambda bi, p: (bi, 0, p)),
                   pl.BlockSpec((1, tp, 256), lambda bi, p: (bi, 0, p)),
                   pl.BlockSpec((1, 2, nchunk, GLA_DV, 128), lambda bi, p: (bi, p, 0, 0, 0))],
        scratch_shapes=[pltpu.VMEM((2, GLA_DV, 128), F32)],
        compiler_params=_params(("parallel", "parallel")),
    )(u, u, u, u, u, gup, gbias, gnorm)


def _out_proj_ln1(hc, gla_out, w_out, s0, cg, cb, g1, b1):
    r, d = s0.shape
    tm = _row_tile(r)

    def body(hc_ref, a_ref, w_ref, s0_ref, cg_ref, cb_ref, g_ref, b_ref, co_ref, xh_ref, rstd_ref, s1b_ref):
        xc, _ = _ln_stats(hc_ref[...])
        nv = xc * cg_ref[...] + cb_ref[...]
        co = (nv * _sigmoid(nv)).astype(BF16)
        co_ref[...] = co
        mix = _nn(co, w_ref[0:D_CONV, :]) + _nn(a_ref[...], w_ref[D_CONV:, :])
        xh, rstd = _ln_stats(ALPHA * s0_ref[...] + mix)
        xh_ref[...] = xh
        rstd_ref[...] = rstd
        s1b_ref[...] = (xh * g_ref[...] + b_ref[...]).astype(BF16)

    row = lambda n: pl.BlockSpec((tm, n), lambda i: (i, 0))
    vec = lambda n: pl.BlockSpec((1, n), lambda i: (0, 0))
    return pl.pallas_call(
        body, name="out_proj_ln1",
        out_shape=[jax.ShapeDtypeStruct((r, D_CONV), BF16), jax.ShapeDtypeStruct((r, d), F32),
                   jax.ShapeDtypeStruct((r, 1), F32), jax.ShapeDtypeStruct((r, d), BF16)],
        grid=(r // tm,),
        in_specs=[row(D_CONV), row(512), pl.BlockSpec((d, d), lambda i: (0, 0)), row(d),
                  vec(D_CONV), vec(D_CONV), vec(d), vec(d)],
        out_specs=[row(D_CONV), row(d), row(1), row(d)],
        compiler_params=_params(("parallel",)),
    )(hc, gla_out, w_out, s0, cg, cb, g1, b1)


def _ffn1(s1b, w1t):
    r, d = s1b.shape
    tm = _row_tile(r)

    def body(a_ref, w_ref, o_ref):
        o_ref[...] = jnp.maximum(_nt(a_ref[...], w_ref[...]), 0.0).astype(BF16)

    return pl.pallas_call(
        body, name="ffn1", out_shape=jax.ShapeDtypeStruct((r, D_FF), BF16),
        grid=(r // tm,),
        in_specs=[pl.BlockSpec((tm, d), lambda i: (i, 0)), pl.BlockSpec((D_FF, d), lambda i: (0, 0))],
        out_specs=pl.BlockSpec((tm, D_FF), lambda i: (i, 0)),
        compiler_params=_params(("parallel",)),
    )(s1b, w1t)


def _ffn2_ln2_loss(ra, w2, xhat1, g1, b1, g2, b2, tgt, tp):
    r, d = xhat1.shape
    tm = _row_tile(tp)
    per = tp // tm

    def body(ra_ref, w_ref, xh1_ref, g1_ref, b1_ref, g2_ref, b2_ref, t_ref, dt_ref, dtb_ref, acc_ref):
        i = pl.program_id(0)

        @pl.when(i == 0)
        def _():
            acc_ref[...] = jnp.zeros_like(acc_ref)

        rf = ra_ref[...].astype(F32)
        f = _nn((rf * rf).astype(BF16), w_ref[...])
        s1 = xh1_ref[...] * g1_ref[...] + b1_ref[...]
        xh2, rstd2 = _ln_stats(ALPHA * s1 + f)
        y = xh2 * g2_ref[...] + b2_ref[...]
        rowid = (i % per) * tm + lax.broadcasted_iota(jnp.int32, (tm, 1), 0)
        e = jnp.where(rowid >= HEAD, y - t_ref[...], 0.0)
        dy = e * (1.0 / d)
        dt2 = _ln_bwd(dy * g2_ref[...], xh2, rstd2)
        dt_ref[...] = dt2
        dtb_ref[...] = dt2.astype(BF16)
        acc_ref[0:1, :] += (0.5 / d) * jnp.sum(e * e, axis=0, keepdims=True)
        acc_ref[1:2, :] += jnp.sum(dy * xh2, axis=0, keepdims=True)
        acc_ref[2:3, :] += jnp.sum(dy, axis=0, keepdims=True)

    row = lambda n: pl.BlockSpec((tm, n), lambda i: (i, 0))
    vec = pl.BlockSpec((1, d), lambda i: (0, 0))
    return pl.pallas_call(
        body, name="ffn2_ln2_loss",
        out_shape=[jax.ShapeDtypeStruct((r, d), F32), jax.ShapeDtypeStruct((r, d), BF16),
                   jax.ShapeDtypeStruct((8, d), F32)],
        grid=(r // tm,),
        in_specs=[row(D_FF), pl.BlockSpec((D_FF, d), lambda i: (0, 0)), row(d), vec, vec, vec, vec, row(d)],
        out_specs=[row(d), row(d), pl.BlockSpec((8, d), lambda i: (0, 0))],
        compiler_params=_params(("arbitrary",)),
    )(ra, w2, xhat1, g1, b1, g2, b2, tgt)


def _ffn_bwd_da(dt2b, w2, ra):
    r, d = dt2b.shape
    tm = _row_tile(r)

    def body(g_ref, w_ref, ra_ref, o_ref):
        o_ref[...] = (_nt(g_ref[...], w_ref[...]) * (2.0 * ra_ref[...].astype(F32))).astype(BF16)

    return pl.pallas_call(
        body, name="ffn_bwd_da", out_shape=jax.ShapeDtypeStruct((r, D_FF), BF16),
        grid=(r // tm,),
        in_specs=[pl.BlockSpec((tm, d), lambda i: (i, 0)), pl.BlockSpec((D_FF, d), lambda i: (0, 0)),
                  pl.BlockSpec((tm, D_FF), lambda i: (i, 0))],
        out_specs=pl.BlockSpec((tm, D_FF), lambda i: (i, 0)),
        compiler_params=_params(("parallel",)),
    )(dt2b, w2, ra)


def _ffn_bwd_ln1(da, w1t, dt2, xhat1, rstd1, g1):
    r, d = dt2.shape
    tm = _row_tile(r)

    def body(da_ref, w_ref, dt2_ref, xh_ref, rstd_ref, g_ref, dt_ref, dtb_ref, acc_ref):
        @pl.when(pl.program_id(0) == 0)
        def _():
            acc_ref[...] = jnp.zeros_like(acc_ref)

        ds1 = ALPHA * dt2_ref[...] + _nn(da_ref[...], w_ref[...])
        xh = xh_ref[...]
        dt1 = _ln_bwd(ds1 * g_ref[...], xh, rstd_ref[...])
        dt_ref[...] = dt1
        dtb_ref[...] = dt1.astype(BF16)
        acc_ref[0:1, :] += jnp.sum(ds1 * xh, axis=0, keepdims=True)
        acc_ref[1:2, :] += jnp.sum(ds1, axis=0, keepdims=True)

    row = lambda n: pl.BlockSpec((tm, n), lambda i: (i, 0))
    return pl.pallas_call(
        body, name="ffn_bwd_ln1",
        out_shape=[jax.ShapeDtypeStruct((r, d), F32), jax.ShapeDtypeStruct((r, d), BF16),
                   jax.ShapeDtypeStruct((8, d), F32)],
        grid=(r // tm,),
        in_specs=[row(D_FF), pl.BlockSpec((D_FF, d), lambda i: (0, 0)), row(d), row(d), row(1),
                  pl.BlockSpec((1, d), lambda i: (0, 0))],
        out_specs=[row(d), row(d), pl.BlockSpec((8, d), lambda i: (0, 0))],
        compiler_params=_params(("arbitrary",)),
    )(da, w1t, dt2, xhat1, rstd1, g1)


def _matmul_tn(lhs, rhs, bm, square_lhs=False, name="matmul_tn"):
    r, m = lhs.shape
    n = rhs.shape[1]
    tk = _row_tile(r)

    def body(a_ref, b_ref, o_ref):
        @pl.when(pl.program_id(1) == 0)
        def _():
            o_ref[...] = jnp.zeros_like(o_ref)

        a = a_ref[...]
        if square_lhs:
            af = a.astype(F32)
            a = (af * af).astype(BF16)
        o_ref[...] += _tn(a, b_ref[...])

    return pl.pallas_call(
        body, name=name, out_shape=jax.ShapeDtypeStruct((m, n), F32),
        grid=(m // bm, r // tk),
        in_specs=[pl.BlockSpec((tk, bm), lambda i, k: (k, i)), pl.BlockSpec((tk, n), lambda i, k: (k, 0))],
        out_specs=pl.BlockSpec((bm, n), lambda i, k: (i, 0)),
        compiler_params=_params(("parallel", "arbitrary")),
    )(lhs, rhs)


def _out_proj_bwd(dt1b, w_out, hc, cg, cb):
    r, d = dt1b.shape
    tm = _row_tile(r)

    def body(g_ref, w_ref, hc_ref, cg_ref, cb_ref, dhc_ref, dgla_ref, acc_ref):
        @pl.when(pl.program_id(0) == 0)
        def _():
            acc_ref[...] = jnp.zeros_like(acc_ref)

        dmix = _nt(g_ref[...], w_ref[...])
        dgla_ref[...] = dmix[:, D_CONV:]
        gg = cg_ref[...]
        xh, rstd = _ln_stats(hc_ref[...])
        nv = xh * gg + cb_ref[...]
        sig = _sigmoid(nv)
        dn = dmix[:, :D_CONV] * (sig * (1.0 + nv * (1.0 - sig)))
        dhc = _ln_bwd(dn * gg, xh, rstd)
        dhc_ref[...] = dhc
        acc_ref[0:1, :] += jnp.sum(dhc, axis=0, keepdims=True)
        acc_ref[1:2, :] += jnp.sum(dn * xh, axis=0, keepdims=True)
        acc_ref[2:3, :] += jnp.sum(dn, axis=0, keepdims=True)

    row = lambda n: pl.BlockSpec((tm, n), lambda i: (i, 0))
    vec = pl.BlockSpec((1, D_CONV), lambda i: (0, 0))
    return pl.pallas_call(
        body, name="out_proj_bwd",
        out_shape=[jax.ShapeDtypeStruct((r, D_CONV), F32), jax.ShapeDtypeStruct((r, 512), F32),
                   jax.ShapeDtypeStruct((8, D_CONV), F32)],
        grid=(r // tm,),
        in_specs=[row(d), pl.BlockSpec((d, d), lambda i: (0, 0)), row(D_CONV), vec, vec],
        out_specs=[row(D_CONV), row(512), pl.BlockSpec((8, D_CONV), lambda i: (0, 0))],
        compiler_params=_params(("arbitrary",)),
    )(dt1b, w_out, hc, cg, cb)


def _conv_bwd(dhc, u, conv_w):
    bsz, tp, _ = u.shape
    nchunk = tp // CHUNK
    win = CHUNK + 32
    nct = D_CONV // 128

    def body(dhc_ref, cv_ref, cg_ref, w_ref, dv_ref, dg_ref, dw_ref, h_scr, dhc_scr, hwin, dwin, dw_scr):
        h_scr[0:32, :] = jnp.zeros((32, 128), F32)
        h_scr[32:32 + tp, :] = cv_ref[0] * _sigmoid(cg_ref[0])
        dhc_scr[0:tp, :] = dhc_ref[0]
        dhc_scr[tp:tp + 32, :] = jnp.zeros((32, 128), F32)
        dw_scr[...] = jnp.zeros_like(dw_scr)

        def step(n, carry):
            r0 = pl.multiple_of(n * CHUNK, CHUNK)
            rows = pl.ds(r0, CHUNK)
            hwin[...] = h_scr[pl.ds(r0, win), :]
            dwin[...] = dhc_scr[pl.ds(r0, win), :]
            dcur = dwin[0:CHUNK, :]
            acc = jnp.zeros((CHUNK, 128), F32)
            for j in range(CONV_WIDTH):
                acc = acc + w_ref[j:j + 1, :] * dwin[30 - j:30 - j + CHUNK, :]
                prod = dcur * hwin[2 + j:2 + j + CHUNK, :]
                dw_scr[j * 8:(j + 1) * 8, :] += jnp.sum(prod.reshape(CHUNK // 8, 8, 128), axis=0)
            cg = cg_ref[0, rows, :]
            sig = _sigmoid(cg)
            rowid = n * CHUNK + lax.broadcasted_iota(jnp.int32, (CHUNK, 1), 0)
            dh = jnp.where(rowid >= PAD, acc, 0.0)
            dv_ref[0, rows, :] = (dh * sig).astype(BF16)
            dg_ref[0, rows, :] = (dh * cv_ref[0, rows, :] * sig * (1.0 - sig)).astype(BF16)
            return carry

        lax.fori_loop(0, nchunk, step, 0)
        dw_ref[0] = jnp.zeros((32, 128), F32)
        for j in range(CONV_WIDTH):
            dw_ref[0, j:j + 1, :] = jnp.sum(dw_scr[j * 8:(j + 1) * 8, :], axis=0, keepdims=True)

    blk = lambda off: pl.BlockSpec((1, tp, 128), lambda bi, t: (bi, 0, off // 128 + t))
    return pl.pallas_call(
        body, name="conv_bwd",
        out_shape=[jax.ShapeDtypeStruct((bsz, tp, D_CONV), BF16), jax.ShapeDtypeStruct((bsz, tp, D_CONV), BF16),
                   jax.ShapeDtypeStruct((bsz, 32, D_CONV), F32)],
        grid=(bsz, nct),
        in_specs=[blk(0), blk(C_VAL), blk(C_GATE), pl.BlockSpec((32, 128), lambda bi, t: (0, t))],
        out_specs=[blk(0), blk(0), pl.BlockSpec((1, 32, 128), lambda bi, t: (bi, 0, t))],
        scratch_shapes=[pltpu.VMEM((tp + 32, 128), F32), pltpu.VMEM((tp + 32, 128), F32),
                        pltpu.VMEM((win, 128), F32), pltpu.VMEM((win, 128), F32),
                        pltpu.VMEM((CONV_WIDTH * 8, 128), F32)],
        compiler_params=_params(("parallel", "parallel")),
    )(dhc, u, u, conv_w)


def _gla_bwd(dgla, u, o_pre, states, gup, gbias, gnorm):
    bsz, tp, _ = u.shape
    nchunk = tp // CHUNK

    def body(dy_ref, q_ref, k_ref, v_ref, r_ref, gd_ref, o_ref, st_ref, gup_ref, gb_ref, gn_ref,
             dq_ref, dk_ref, dv_ref, dr_ref, dgd_ref, dgup_ref, vec_ref, h_scr, gup_acc):
        tri, hmask = _gla_consts()
        triu = _gla_triu()
        h_scr[...] = jnp.zeros_like(h_scr)
        gup_acc[...] = jnp.zeros_like(gup_acc)
        gn = gn_ref[...]
        gupb = gup_ref[...]

        def step(i, carry):
            dbias, dgn = carry
            n = nchunk - 1 - i
            rows, valid, z, eb, enb, erest, dec, qe, ke, kd = _gla_chunk_terms(
                n, q_ref, k_ref, gd_ref, gup_ref, gb_ref, tri)
            keb, kdb = ke.astype(BF16), kd.astype(BF16)
            dqe = jnp.zeros((CHUNK, 128), F32)
            dke = jnp.zeros((CHUNK, 128), F32)
            dkd = jnp.zeros((CHUNK, 128), F32)
            ddec = jnp.zeros((1, 128), F32)
            for h in range(2):
                cols = slice(h * GLA_DV, (h + 1) * GLA_DV)
                o = o_ref[0, rows, cols]
                rh = r_ref[0, rows, cols]
                dy = dy_ref[0, rows, cols]
                rms = lax.rsqrt(jnp.mean(o * o, axis=-1, keepdims=True) + LN_EPS)
                nrm = o * rms
                sig = _sigmoid(rh)
                sw = rh * sig
                dr_ref[0, rows, cols] = (dy * nrm * gn * (sig * (1.0 + rh * (1.0 - sig)))).astype(BF16)
                dgn = dgn + jnp.sum(dy * nrm * sw, axis=0, keepdims=True)
                dn = dy * gn * sw
                do = rms * (dn - nrm * jnp.mean(dn * nrm, axis=-1, keepdims=True))
                dob = do.astype(BF16)
                qh = jnp.where(hmask[h], qe, 0.0).astype(BF16)
                vh = v_ref[0, rows, cols].astype(BF16)
                a = jnp.where(tri, _nt(qh, keb), 0.0).astype(BF16)
                st = st_ref[0, h, n]
                ht = h_scr[h]
                da = jnp.where(tri, _nt(dob, vh), 0.0).astype(BF16)
                dqe = dqe + jnp.where(hmask[h], _nn(da, keb) + _nn(dob, st.astype(BF16)), 0.0)
                dke = dke + _tn(da, qh)
                dv_ref[0, rows, cols] = (_tn(a, dob) + _nt(kdb, ht.astype(BF16))).astype(BF16)
                dkd = dkd + jnp.where(hmask[h], _nn(vh, ht.astype(BF16)), 0.0)
                ddec = ddec + jnp.where(hmask[h], jnp.sum(ht * st, axis=0, keepdims=True), 0.0)
                h_scr[h] = dec * ht + _tn(dob, qh)
            dq_ref[0, rows, :] = (dqe * eb * Q_SCALE).astype(BF16)
            dk_ref[0, rows, :] = (dke * enb + dkd * erest).astype(BF16)
            db = dqe * qe - dke * ke - dkd * kd
            dblast = jnp.sum(dkd * kd, axis=0, keepdims=True) + ddec * dec
            lastrow = lax.broadcasted_iota(jnp.int32, (CHUNK, 1), 0) == CHUNK - 1
            db = db + jnp.where(lastrow, dblast, 0.0)
            dlg = jnp.where(valid, _nn(triu, db, precision=lax.Precision.HIGHEST), 0.0)
            dz = dlg * (1.0 / GLA_TAU) * (1.0 - _sigmoid(z))
            dzb = dz.astype(BF16)
            dgd_ref[0, 0, rows, :] = _nt(dzb, gupb).astype(BF16)
            gup_acc[...] += _tn(gd_ref[0, rows, :].astype(BF16), dzb)
            return dbias + jnp.sum(dz, axis=0, keepdims=True), dgn

        zero = jnp.zeros((1, 128), F32)
        dbias, dgn = _chunk_loop(nchunk, step, (zero, zero))
        dgup_ref[0] = gup_acc[...]
        vec_ref[0] = jnp.zeros((8, 128), F32)
        vec_ref[0, 0:1, :] = dbias
        vec_ref[0, 1:2, :] = dgn

    pair = lambda w, off: pl.BlockSpec((1, tp, w), lambda bi, p: (bi, 0, off // w + p))
    return pl.pallas_call(
        body, name="gla_bwd",
        out_shape=[jax.ShapeDtypeStruct((bsz, tp, 256), BF16), jax.ShapeDtypeStruct((bsz, tp, 256), BF16),
                   jax.ShapeDtypeStruct((bsz, tp, 512), BF16), jax.ShapeDtypeStruct((bsz, tp, 512), BF16),
                   jax.ShapeDtypeStruct((bsz, 2, tp, 128), BF16), jax.ShapeDtypeStruct((bsz, 128, 256), F32),
                   jax.ShapeDtypeStruct((bsz, 8, 256), F32)],
        grid=(bsz, 2),
        in_specs=[pair(256, 0), pair(128, C_Q), pair(128, C_K), pair(256, C_V), pair(256, C_R),
                  pl.BlockSpec((1, tp, 128), lambda bi, p: (bi, 0, C_GD // 128)),
                  pair(256, 0),
                  pl.BlockSpec((1, 2, nchunk, GLA_DV, 128), lambda bi, p: (bi, p, 0, 0, 0)),
                  pl.BlockSpec((128, 128), lambda bi, p: (0, p)),
                  pl.BlockSpec((1, 128), lambda bi, p: (0, p)),
                  pl.BlockSpec((1, 128), lambda bi, p: (0, 0))],
        out_specs=[pair(128, 0), pair(128, 0), pair(256, 0), pair(256, 0),
                   pl.BlockSpec((1, 1, tp, 128), lambda bi, p: (bi, p, 0, 0)),
                   pl.BlockSpec((1, 128, 128), lambda bi, p: (bi, 0, p)),
                   pl.BlockSpec((1, 8, 128), lambda bi, p: (bi, 0, p))],
        scratch_shapes=[pltpu.VMEM((2, GLA_DV, 128), F32), pltpu.VMEM((128, 128), F32)],
        compiler_params=_params(("parallel", "parallel")),
    )(dgla, u, u, u, u, u, o_pre, states, gup, gbias, gnorm)


_DU_OFFSETS = (C_VAL, C_GATE, C_Q, C_K, C_V, C_R)
_DU_WIDTHS = (512, 512, 256, 256, 512, 512)


def _du_specs(tm, per, row_map):
    specs = [pl.BlockSpec((tm, w), row_map) for w in _DU_WIDTHS]
    for p in range(2):
        specs.append(pl.BlockSpec((1, 1, tm, 128), lambda *ix, p=p: (row_map(*ix)[0] // per, p, row_map(*ix)[0] % per, 0)))
    return specs


def _du_pieces(refs):
    out = [(off, ref[...]) for off, ref in zip(_DU_OFFSETS, refs[:6])]
    dgd = (refs[6][0, 0].astype(F32) + refs[7][0, 0].astype(F32)).astype(BF16)
    out.append((C_GD, dgd))
    return out


def _in_proj_bwd(pieces, dgd, w_int, dt1, tp):
    r, d = dt1.shape
    tm = _row_tile(tp)
    per = tp // tm

    def body(*refs):
        w_ref, dt_ref, o_ref = refs[8:]
        acc = ALPHA * dt_ref[...]
        for off, val in _du_pieces(refs[:8]):
            acc = acc + _nn(val, w_ref[off:off + val.shape[1], :])
        o_ref[...] = acc

    row = lambda i: (i, 0)
    return pl.pallas_call(
        body, name="in_proj_bwd", out_shape=jax.ShapeDtypeStruct((r, d), F32),
        grid=(r // tm,),
        in_specs=_du_specs(tm, per, row) + [pl.BlockSpec((D_IN_PAD, d), lambda i: (0, 0)), pl.BlockSpec((tm, d), row)],
        out_specs=pl.BlockSpec((tm, d), row),
        compiler_params=_params(("parallel",)),
    )(*pieces, dgd, dgd, w_int, dt1)


def _grad_w_in(pieces, dgd, s0b, tp):
    r, d = s0b.shape
    tk = _row_tile(tp)
    per = tp // tk

    def body(*refs):
        s_ref, o_ref = refs[8:]

        @pl.when(pl.program_id(0) == 0)
        def _():
            o_ref[...] = jnp.zeros_like(o_ref)

        s = s_ref[...]
        for off, val in _du_pieces(refs[:8]):
            o_ref[off:off + val.shape[1], :] += _tn(val, s)

    row = lambda k: (k, 0)
    return pl.pallas_call(
        body, name="grad_w_in", out_shape=jax.ShapeDtypeStruct((D_IN_PAD, d), F32),
        grid=(r // tk,),
        in_specs=_du_specs(tk, per, row) + [pl.BlockSpec((tk, d), row)],
        out_specs=pl.BlockSpec((D_IN_PAD, d), lambda k: (0, 0)),
        compiler_params=_params(("arbitrary",)),
    )(*pieces, dgd, dgd, s0b)


def _ln_in_bwd(ds0, x, meta, g):
    bsz, s, d = x.shape
    tp = s + HEAD
    nh = 2
    sh = s // nh
    rc = min(256, sh)

    def body(ds_ref, x_ref, meta_ref, g_ref, gx_ref, dm_ref, vec_ref):
        h = pl.program_id(1)
        gg = g_ref[...]

        @pl.when(h == 0)
        def _():
            mh, mr = _ln_stats(meta_ref[...])
            dsm = ds_ref[0, PAD:HEAD, :]
            dm_ref[0] = _ln_bwd(dsm * gg, mh, mr)
            vec_ref[0] = jnp.zeros((8, d), F32)
            vec_ref[0, 0:1, :] = jnp.sum(dsm * mh, axis=0, keepdims=True)
            vec_ref[0, 1:2, :] = jnp.sum(dsm, axis=0, keepdims=True)

        def step(i, carry):
            sg, sb = carry
            dst = pl.ds(pl.multiple_of(i * rc, rc), rc)
            src = pl.ds(pl.multiple_of(HEAD + h * sh + i * rc, 64), rc)
            xh, rstd = _ln_stats(x_ref[0, dst, :])
            dsv = ds_ref[0, src, :]
            gx_ref[0, dst, :] = _ln_bwd(dsv * gg, xh, rstd)
            return sg + jnp.sum(dsv * xh, axis=0, keepdims=True), sb + jnp.sum(dsv, axis=0, keepdims=True)

        zero = jnp.zeros((1, d), F32)
        sg, sb = lax.fori_loop(0, sh // rc, step, (zero, zero))
        vec_ref[0, 0:1, :] += sg
        vec_ref[0, 1:2, :] += sb

    return pl.pallas_call(
        body, name="ln_in_bwd",
        out_shape=[jax.ShapeDtypeStruct((bsz, s, d), F32), jax.ShapeDtypeStruct((bsz, N_META, d), F32),
                   jax.ShapeDtypeStruct((bsz, 8, d), F32)],
        grid=(bsz, nh),
        in_specs=[pl.BlockSpec((1, tp, d), lambda bi, hi: (bi, 0, 0)),
                  pl.BlockSpec((1, sh, d), lambda bi, hi: (bi, hi, 0)),
                  pl.BlockSpec((N_META, d), lambda bi, hi: (0, 0)),
                  pl.BlockSpec((1, d), lambda bi, hi: (0, 0))],
        out_specs=[pl.BlockSpec((1, sh, d), lambda bi, hi: (bi, hi, 0)),
                   pl.BlockSpec((1, N_META, d), lambda bi, hi: (bi, 0, 0)),
                   pl.BlockSpec((1, 8, d), lambda bi, hi: (bi, 0, 0))],
        compiler_params=_params(("parallel", "arbitrary")),
    )(ds0, x, meta, g)


def _rows128(a):
    return a.reshape(-1, 128)


def kernel(x, meta_tokens, ln_in_g, ln_in_b, w_in, conv_w, conv_b, conv_ln_g, conv_ln_b, gate_up, gate_bias, gla_norm_g, w_out, ln1_g, ln1_b, w_ff1, w_ff2, ln2_g, ln2_b, loss_target, m_meta_tokens, m_ln_in_g, m_ln_in_b, m_w_in, m_conv_w, m_conv_b, m_conv_ln_g, m_conv_ln_b, m_gate_up, m_gate_bias, m_gla_norm_g, m_w_out, m_ln1_g, m_ln1_b, m_w_ff1, m_w_ff2, m_ln2_g, m_ln2_b, v_meta_tokens, v_ln_in_g, v_ln_in_b, v_w_in, v_conv_w, v_conv_b, v_conv_ln_g, v_conv_ln_b, v_gate_up, v_gate_bias, v_gla_norm_g, v_w_out, v_ln1_g, v_ln1_b, v_w_ff1, v_w_ff2, v_ln2_g, v_ln2_b):
    bsz, seq, d = x.shape
    tp = seq + HEAD
    r = bsz * tp
    xi, yi, ci = _mesh_pos()
    chip = 2 * xi + yi
    c_arr = jnp.reshape(ci, (1,)).astype(jnp.int32)
    pos_arr = jnp.stack([chip, ci]).astype(jnp.int32)

    sh_in = D_IN // 4
    shards = [jnp.pad(w_in[0].T.astype(BF16), ((0, D_IN_PAD // 4 - sh_in), (0, 0))), w_ff1[0].T.astype(BF16),
              w_out[0].astype(BF16), w_ff2[0].astype(BF16)]
    small_w = jnp.concatenate([_rows128(meta_tokens), _rows128(conv_w[0]), _rows128(gate_up[0])], axis=0)
    (g_int, g_w1t, g_wout, g_w2), g_small = _gather_weights(shards, small_w)
    w_int = jnp.pad(g_int[:, :sh_in].reshape(D_IN, d), ((0, D_IN_PAD - D_IN), (0, 0)))
    w1t = g_w1t.reshape(D_FF, d)
    wout = g_wout.reshape(d, d)
    w2 = g_w2.reshape(D_FF, d)
    n_meta_rows, n_cw_rows = N_META * 256 // 128, CONV_WIDTH * 128 // 128
    meta_full = jnp.concatenate([g_small[j, :n_meta_rows].reshape(N_META, 256) for j in range(4)], axis=1)
    convw_full = jnp.concatenate(
        [g_small[j, n_meta_rows:n_meta_rows + n_cw_rows].reshape(CONV_WIDTH, 128) for j in range(4)], axis=1)
    gup_full = jnp.concatenate(
        [g_small[j, n_meta_rows + n_cw_rows:].reshape(GLA_RANK, 64) for j in range(4)], axis=1)
    convw_p = jnp.pad(convw_full, ((0, 1), (0, 0)))
    gup_p = jnp.pad(gup_full, ((0, 128 - GLA_RANK), (0, 0))).astype(BF16)
    ln_in_g2, ln_in_b2 = ln_in_g.reshape(1, d), ln_in_b.reshape(1, d)

    s0, s0b = _ln_in_fwd(x, meta_full, ln_in_g2, ln_in_b2)
    tgt_p = jnp.pad(loss_target, ((0, 0), (HEAD, 0), (0, 0)))
    s0f, s0bf = s0.reshape(r, d), s0b.reshape(r, d)
    u = _in_proj(s0bf, w_int)
    u3 = u.reshape(bsz, tp, D_IN_PAD)
    hc = _conv_fwd(u3, convw_p, conv_b).reshape(r, D_CONV)
    gla_out, o_pre, states = _gla_fwd(u3, gup_p, gate_bias, gla_norm_g)
    gla_of = gla_out.reshape(r, 512)
    conv_of, xhat1, rstd1, s1b = _out_proj_ln1(hc, gla_of, wout, s0f, conv_ln_g, conv_ln_b, ln1_g, ln1_b)
    ra = _ffn1(s1b, w1t)
    dt2, dt2b, acc2 = _ffn2_ln2_loss(ra, w2, xhat1, ln1_g, ln1_b, ln2_g, ln2_b, tgt_p.reshape(r, d), tp)

    da = _ffn_bwd_da(dt2b, w2, ra)
    dt1, dt1b, acc1 = _ffn_bwd_ln1(da, w1t, dt2, xhat1, rstd1, ln1_g)
    g_w2 = _matmul_tn(ra, dt2b, 1024, square_lhs=True, name="grad_w_ff2")
    g_w1t = _matmul_tn(da, s1b, 1024, name="grad_w_ff1")
    dhc, dgla, cacc = _out_proj_bwd(dt1b, wout, hc, conv_ln_g, conv_ln_b)
    g_wout = jnp.concatenate([_matmul_tn(conv_of, dt1b, D_CONV, name="grad_w_out_conv"),
                              _matmul_tn(gla_of, dt1b, 512, name="grad_w_out_gla")], axis=0)
    dcv, dcg, dcw = _conv_bwd(dhc.reshape(bsz, tp, D_CONV), u3, convw_p)
    dq, dk, dv, dr, dgd, dgup, gvec = _gla_bwd(dgla.reshape(bsz, tp, 512), u3, o_pre, states, gup_p, gate_bias,
                                               gla_norm_g)
    pieces = [a.reshape(r, a.shape[-1]) for a in (dcv, dcg, dq, dk, dv, dr)]
    ds0 = _in_proj_bwd(pieces, dgd, w_int, dt1, tp)
    g_wint = _grad_w_in(pieces, dgd, s0bf, tp)
    grad_x, dmeta, lvec = _ln_in_bwd(ds0.reshape(bsz, tp, d), x, meta_full, ln_in_g2)

    small = {
        "loss": acc2[0:1], "ln_in_g": jnp.sum(lvec[:, 0], axis=0), "ln_in_b": jnp.sum(lvec[:, 1], axis=0),
        "conv_b": cacc[0], "conv_ln_g": cacc[1], "conv_ln_b": cacc[2], "gate_bias": jnp.sum(gvec[:, 0], axis=0),
        "gla_norm_g": jnp.sum(gvec[:, 1].reshape(bsz * 2, 128), axis=0),
        "ln1_g": acc1[0], "ln1_b": acc1[1], "ln2_g": acc2[1], "ln2_b": acc2[2],
        "conv_w": jnp.sum(dcw, axis=0)[:CONV_WIDTH], "gate_up": jnp.sum(dgup, axis=0)[:GLA_RANK],
        "meta_tokens": jnp.sum(dmeta, axis=0),
    }
    pack = jnp.concatenate([_rows128(small[k]) for k, _ in _SMALL_FIELDS], axis=0)
    pack = jnp.pad(pack, ((0, SMALL_ROWS - pack.shape[0]), (0, 0)))
    red = _sum_small(_allgather_small(pack))
    off, tot = {}, 0
    for k, nrows in _SMALL_FIELDS:
        off[k] = (tot, nrows)
        tot += nrows

    def field(k, shape):
        o, nrows = off[k]
        return red[o:o + nrows].reshape(shape)

    loss = jnp.sum(field("loss", (d,)))

    big = [g_wint[:D_IN].reshape(4, D_IN // 4, d), g_w1t.reshape(4, D_FF // 4, d),
           g_wout.reshape(4, d // 4, d), g_w2.reshape(4, D_FF // 4, d)]
    pair_got = _pair_exchange(big)
    parts = [_add_pair(g, t, c_arr) for g, t in zip(big, pair_got)]
    chip_got = _chip_exchange(parts)
    halves = [_add_chips(g, t, s, pos_arr) for g, t, s in zip(big, pair_got, chip_got)]
    f_wint, f_w1t, f_wout, f_w2 = [
        jnp.concatenate([jnp.where(ci == 0, own, sib), jnp.where(ci == 0, sib, own)], axis=1)
        for own, sib in zip(halves, _pair_swap(halves))]

    def cols(a, width):
        return lax.dynamic_slice_in_dim(a, chip * width, width, axis=a.ndim - 1)

    grads = {
        "meta_tokens": cols(field("meta_tokens", (N_META, d)), 256),
        "ln_in_g": field("ln_in_g", (d,)), "ln_in_b": field("ln_in_b", (d,)),
        "w_in": f_wint.T[None], "conv_w": cols(field("conv_w", (CONV_WIDTH, D_CONV)), 128)[None],
        "conv_b": field("conv_b", (1, D_CONV)), "conv_ln_g": field("conv_ln_g", (1, D_CONV)),
        "conv_ln_b": field("conv_ln_b", (1, D_CONV)),
        "gate_up": cols(field("gate_up", (GLA_RANK, 256)), 64)[None],
        "gate_bias": field("gate_bias", (1, 256)), "gla_norm_g": field("gla_norm_g", (1, 128)),
        "w_out": f_wout[None], "ln1_g": field("ln1_g", (1, d)), "ln1_b": field("ln1_b", (1, d)),
        "w_ff1": f_w1t.T[None], "w_ff2": f_w2[None],
        "ln2_g": field("ln2_g", (1, d)), "ln2_b": field("ln2_b", (1, d)),
    }
    weights = dict(meta_tokens=meta_tokens, ln_in_g=ln_in_g, ln_in_b=ln_in_b, w_in=w_in, conv_w=conv_w, conv_b=conv_b,
                   conv_ln_g=conv_ln_g, conv_ln_b=conv_ln_b, gate_up=gate_up, gate_bias=gate_bias,
                   gla_norm_g=gla_norm_g, w_out=w_out, ln1_g=ln1_g, ln1_b=ln1_b, w_ff1=w_ff1, w_ff2=w_ff2,
                   ln2_g=ln2_g, ln2_b=ln2_b)
    moms = dict(meta_tokens=(m_meta_tokens, v_meta_tokens), ln_in_g=(m_ln_in_g, v_ln_in_g),
                ln_in_b=(m_ln_in_b, v_ln_in_b), w_in=(m_w_in, v_w_in), conv_w=(m_conv_w, v_conv_w),
                conv_b=(m_conv_b, v_conv_b), conv_ln_g=(m_conv_ln_g, v_conv_ln_g),
                conv_ln_b=(m_conv_ln_b, v_conv_ln_b), gate_up=(m_gate_up, v_gate_up),
                gate_bias=(m_gate_bias, v_gate_bias), gla_norm_g=(m_gla_norm_g, v_gla_norm_g),
                w_out=(m_w_out, v_w_out), ln1_g=(m_ln1_g, v_ln1_g), ln1_b=(m_ln1_b, v_ln1_b),
                w_ff1=(m_w_ff1, v_w_ff1), w_ff2=(m_w_ff2, v_w_ff2), ln2_g=(m_ln2_g, v_ln2_g),
                ln2_b=(m_ln2_b, v_ln2_b))
    names = list(weights)
    big_names = ("w_in", "w_out", "w_ff1", "w_ff2")
    delta, new_m, new_v = {}, {}, {}
    dl, mn, vn = _adamw(w_in[0].T, f_wint, m_w_in[0].T, v_w_in[0].T)
    delta["w_in"], new_m["w_in"], new_v["w_in"] = dl.T[None], mn.T[None], vn.T[None]
    for k in big_names[1:]:
        shp = weights[k].shape
        two = lambda a: a.reshape(shp[-2], shp[-1])
        dl, mn, vn = _adamw(two(weights[k]), two(grads[k]), two(moms[k][0]), two(moms[k][1]))
        delta[k], new_m[k], new_v[k] = dl.reshape(shp), mn.reshape(shp), vn.reshape(shp)
    small_names = [k for k in names if k not in big_names]
    sizes = [weights[k].size for k in small_names]
    total = sum(sizes)
    padded = -(-total // 1024) * 1024

    def flat(get):
        v = jnp.concatenate([get(k).reshape(-1) for k in small_names])
        return jnp.pad(v, (0, padded - total)).reshape(padded // 128, 128)

    ones = jnp.ones((padded - total,), F32)
    vflat = jnp.concatenate([jnp.concatenate([moms[k][1].reshape(-1) for k in small_names]), ones])
    dl, mn, vn = _adamw(flat(lambda k: weights[k]), flat(lambda k: grads[k]), flat(lambda k: moms[k][0]),
                        vflat.reshape(padded // 128, 128))
    pos = 0
    for k, sz in zip(small_names, sizes):
        shp = weights[k].shape
        delta[k] = dl.reshape(-1)[pos:pos + sz].reshape(shp)
        new_m[k] = mn.reshape(-1)[pos:pos + sz].reshape(shp)
        new_v[k] = vn.reshape(-1)[pos:pos + sz].reshape(shp)
        pos += sz

    return (loss, grad_x, *[grads[k] for k in names], *[delta[k] for k in names],
            *[new_m[k] for k in names], *[new_v[k] for k in names])
```

```python
import functools

import jax
import jax.numpy as jnp
from jax import lax
from jax.experimental import pallas as pl
from jax.experimental.pallas import tpu as pltpu

F32 = jnp.float32
BF16 = jnp.bfloat16

D_MODEL = 1024
N_META = 16
D_CONV = 512
CONV_WIDTH = 31
GLA_HEADS = 4
GLA_DV = 128
GLA_DK = 64
GLA_RANK = 16
GLA_TAU = 16.0
CHUNK = 64
D_FF = 4096
LN_EPS = 1e-5
ALPHA = 2.0 ** 0.25
D_IN = 2576
D_IN_PAD = 2688
PAD = CHUNK - N_META
HEAD = PAD + N_META
Q_SCALE = GLA_DK ** -0.5
ADAM_LR, ADAM_B1, ADAM_B2, ADAM_EPS, ADAM_WD, ADAM_STEP = 0.001, 0.9, 0.999, 1e-08, 0.01, 10
HALF = D_MODEL // 2
VMEM_LIMIT = 56 * 1024 * 1024
MESH = pl.DeviceIdType.MESH

C_VAL, C_GATE, C_Q, C_K, C_V, C_R, C_GD = 0, 512, 1024, 1280, 1536, 2048, 2560

_SMALL_FIELDS = (("loss", 8), ("ln_in_g", 8), ("ln_in_b", 8), ("conv_b", 4), ("conv_ln_g", 4), ("conv_ln_b", 4),
                 ("gate_bias", 2), ("gla_norm_g", 1), ("ln1_g", 8), ("ln1_b", 8), ("ln2_g", 8), ("ln2_b", 8),
                 ("conv_w", 124), ("gate_up", 32), ("meta_tokens", 128))
SMALL_ROWS = 360


def _params(sem=None, **kw):
    return pltpu.CompilerParams(dimension_semantics=sem, vmem_limit_bytes=VMEM_LIMIT, **kw)


def _row_tile(tp):
    for t in (704, 352, 192, 64):
        if tp % t == 0:
            return t
    raise ValueError(tp)


def _dot(a, b, dims, precision=None):
    return lax.dot_general(a, b, (dims, ((), ())), preferred_element_type=F32, precision=precision)


def _nn(a, b, **kw):
    return _dot(a, b, ((1,), (0,)), **kw)


def _nt(a, b, **kw):
    return _dot(a, b, ((1,), (1,)), **kw)


def _tn(a, b, **kw):
    return _dot(a, b, ((0,), (0,)), **kw)


def _sigmoid(x):
    return 1.0 / (1.0 + jnp.exp(-x))


def _log_sigmoid(z):
    return jnp.minimum(z, 0.0) - jnp.log(1.0 + jnp.exp(-jnp.abs(z)))


def _ln_stats(t):
    mu = jnp.mean(t, axis=-1, keepdims=True)
    d = t - mu
    var = jnp.mean(d * d, axis=-1, keepdims=True)
    rstd = lax.rsqrt(var + LN_EPS)
    return d * rstd, rstd


def _ln_bwd(dxhat, xhat, rstd):
    m1 = jnp.mean(dxhat, axis=-1, keepdims=True)
    m2 = jnp.mean(dxhat * xhat, axis=-1, keepdims=True)
    return rstd * (dxhat - m1 - xhat * m2)


def _mesh_pos():
    return lax.axis_index("x"), lax.axis_index("y"), lax.axis_index("c")


ANY = pl.BlockSpec(memory_space=pl.ANY)


def _gather_sems(n):
    return [pltpu.SemaphoreType.DMA((n, 3))] * 4


def _gather_steps(ins, outs, sems):
    n = len(ins)
    send, recv, fsend, frecv = sems
    half = [ref.shape[0] // 2 for ref in ins]
    x, y, c = _mesh_pos()
    me = 2 * x + y
    sibling = (x, y, 1 - c)
    chips = [(1 - x, y), (x, 1 - y), (1 - x, 1 - y)]
    chip_idx = [2 * px + py for px, py in chips]
    mine = [pl.ds(pl.multiple_of(c * h, 16), h) for h in half]
    other = [pl.ds(pl.multiple_of((1 - c) * h, 16), h) for h in half]
    pairs = [(a, k) for a in range(n) for k in range(3)]

    def ici(a, k, slab):
        return pltpu.make_async_remote_copy(
            src_ref=ins[a].at[mine[a], :], dst_ref=outs[a].at[slab, mine[a], :],
            send_sem=send.at[a, k], recv_sem=recv.at[a, k], device_id=(*chips[k], c), device_id_type=MESH)

    def forward(a, k, rows):
        blk = outs[a].at[chip_idx[k], rows[a], :]
        return pltpu.make_async_remote_copy(
            src_ref=blk, dst_ref=blk, send_sem=fsend.at[a, k], recv_sem=frecv.at[a, k],
            device_id=sibling, device_id_type=MESH)

    def start():
        for a, k in pairs:
            ici(a, k, me).start()

    def relay():
        for a, k in pairs:
            ici(a, k, chip_idx[k]).wait_recv()
            forward(a, k, mine).start()

    def finish():
        for a, k in pairs:
            forward(a, k, other).wait_recv()
        for a, k in pairs:
            ici(a, k, me).wait_send()
            forward(a, k, mine).wait_send()

    return start, relay, finish


def _place_own(gathered, shards):
    chip = 2 * lax.axis_index("x") + lax.axis_index("y")
    return [lax.dynamic_update_slice(g, s[None], (chip, 0, 0)) for g, s in zip(gathered, shards)]


def _gather_weights(shards, small):
    n = len(shards)

    def body(*refs):
        ins, small_in = refs[:n], refs[n]
        outs, small_out = refs[n + 1:2 * n + 1], refs[2 * n + 1]
        ssend, srecv = refs[2 * n + 6:]
        start, relay, finish = _gather_steps(ins, outs, refs[2 * n + 2:2 * n + 6])
        x, y, c = _mesh_pos()
        chips = [(1 - x, y), (x, 1 - y), (1 - x, 1 - y)]

        def small_copy(k, slot):
            return pltpu.make_async_remote_copy(
                src_ref=small_in, dst_ref=small_out.at[slot], send_sem=ssend.at[k], recv_sem=srecv.at[k],
                device_id=(*chips[k], c), device_id_type=MESH)

        start()
        for k in range(3):
            small_copy(k, 2 * x + y).start()
        relay()
        finish()
        for k, (px, py) in enumerate(chips):
            small_copy(k, 2 * px + py).wait_recv()
        for k in range(3):
            small_copy(k, 2 * x + y).wait_send()

    out_shape = [jax.ShapeDtypeStruct((4,) + s.shape, s.dtype) for s in shards]
    out_shape.append(jax.ShapeDtypeStruct((4,) + small.shape, small.dtype))
    res = pl.pallas_call(
        body, name="gather_weights", out_shape=out_shape,
        in_specs=[ANY] * (n + 1), out_specs=[ANY] * (n + 1),
        scratch_shapes=_gather_sems(n) + [pltpu.SemaphoreType.DMA((3,)), pltpu.SemaphoreType.DMA((3,))],
    )(*shards, small)
    res = _place_own(res, list(shards) + [small])
    return res[:n], res[n]


def _pair_exchange(grads):
    n = len(grads)

    def body(*refs):
        ins, outs = refs[:n], refs[n:2 * n]
        send, recv = refs[2 * n:]
        x, y, c = _mesh_pos()
        other = pl.ds(pl.multiple_of((1 - c) * HALF, 128), HALF)
        cps = [pltpu.make_async_remote_copy(
            src_ref=ins[a].at[:, :, other], dst_ref=outs[a], send_sem=send.at[a], recv_sem=recv.at[a],
            device_id=(x, y, 1 - c), device_id_type=MESH) for a in range(n)]
        for cp in cps:
            cp.start()
        for cp in cps:
            cp.wait()

    return pl.pallas_call(
        body, name="grad_pair_exchange",
        out_shape=[jax.ShapeDtypeStruct(g.shape[:2] + (HALF,), g.dtype) for g in grads],
        in_specs=[ANY] * n, out_specs=[ANY] * n,
        scratch_shapes=[pltpu.SemaphoreType.DMA((n,)), pltpu.SemaphoreType.DMA((n,))],
    )(*grads)


def _chip_exchange_sems(n):
    return [pltpu.SemaphoreType.DMA((n, 3))] * 2


def _chip_exchange_shapes(parts):
    return [jax.ShapeDtypeStruct((3,) + p.shape[1:], p.dtype) for p in parts]


def _chip_exchange_steps(ins, outs, sems):
    send, recv = sems
    x, y, c = _mesh_pos()
    chips = [(1 - x, y), (x, 1 - y), (1 - x, 1 - y)]
    cps = [pltpu.make_async_remote_copy(
        src_ref=ins[a].at[2 * px + py], dst_ref=outs[a].at[k], send_sem=send.at[a, k], recv_sem=recv.at[a, k],
        device_id=(px, py, c), device_id_type=MESH) for a in range(len(ins)) for k, (px, py) in enumerate(chips)]

    def start():
        for cp in cps:
            cp.start()

    def finish():
        for cp in cps:
            cp.wait()

    return start, finish


def _chip_exchange(parts):
    n = len(parts)

    def body(*refs):
        start, finish = _chip_exchange_steps(refs[:n], refs[n:2 * n], refs[2 * n:])
        start()
        finish()

    return pl.pallas_call(
        body, name="grad_chip_exchange", out_shape=_chip_exchange_shapes(parts),
        in_specs=[ANY] * n, out_specs=[ANY] * n, scratch_shapes=_chip_exchange_sems(n),
    )(*parts)


def _pair_swap(halves):
    n = len(halves)

    def body(*refs):
        ins, outs = refs[:n], refs[n:2 * n]
        send, recv = refs[2 * n:]
        x, y, c = _mesh_pos()
        cps = [pltpu.make_async_remote_copy(
            src_ref=ins[a], dst_ref=outs[a], send_sem=send.at[a], recv_sem=recv.at[a],
            device_id=(x, y, 1 - c), device_id_type=MESH) for a in range(n)]
        for cp in cps:
            cp.start()
        for cp in cps:
            cp.wait()

    return pl.pallas_call(
        body, name="grad_pair_swap",
        out_shape=[jax.ShapeDtypeStruct(h.shape, h.dtype) for h in halves],
        in_specs=[ANY] * n, out_specs=[ANY] * n,
        scratch_shapes=[pltpu.SemaphoreType.DMA((n,)), pltpu.SemaphoreType.DMA((n,))],
    )(*halves)


def _allgather_small(pack):
    m_per, ncol = pack.shape

    def body(x_ref, out_ref, send_sems, recv_sems, local_sem):
        x, y, c = _mesh_pos()
        me, sibling = (x, y, c), (x, y, 1 - c)
        chips = [(1 - x, y), (x, 1 - y), (1 - x, 1 - y)]

        def rows(px, py, pc):
            return out_ref.at[pl.ds(pl.multiple_of((4 * px + 2 * py + pc) * m_per, 8), m_per), :]

        def copy(k, block, to, src=None):
            return pltpu.make_async_remote_copy(
                src_ref=rows(*block) if src is None else src, dst_ref=rows(*block),
                send_sem=send_sems.at[k], recv_sem=recv_sems.at[k], device_id=to, device_id_type=MESH)

        mine = pltpu.make_async_copy(x_ref, rows(*me), local_sem)
        mine.start()
        first = [copy(0, me, sibling, src=x_ref)]
        first += [copy(1 + j, me, (*chip, c), src=x_ref) for j, chip in enumerate(chips)]
        for cp in first:
            cp.start()
        passed = [copy(4 + j, (*chip, c), sibling) for j, chip in enumerate(chips)]
        for j, chip in enumerate(chips):
            copy(1 + j, (*chip, c), me).wait_recv()
            passed[j].start()
        copy(0, sibling, me).wait_recv()
        for j, chip in enumerate(chips):
            copy(4 + j, (*chip, 1 - c), me).wait_recv()
        for cp in first + passed:
            cp.wait_send()
        mine.wait()

    return pl.pallas_call(
        body, name="allgather_small",
        out_shape=jax.ShapeDtypeStruct((8 * m_per, ncol), pack.dtype),
        in_specs=[pl.BlockSpec(memory_space=pltpu.VMEM)],
        out_specs=pl.BlockSpec(memory_space=pltpu.VMEM),
        scratch_shapes=[pltpu.SemaphoreType.DMA((7,)), pltpu.SemaphoreType.DMA((7,)), pltpu.SemaphoreType.DMA],
    )(pack)


def _add_pair(g, got, c_arr):
    _, rows, _ = g.shape

    def body(c_ref, g_ref, r_ref, o_ref):
        o_ref[...] = (g_ref[...] + r_ref[...]).astype(BF16)

    return pl.pallas_call(
        body, name="grad_add_pair", out_shape=jax.ShapeDtypeStruct((4, rows, HALF), BF16),
        grid_spec=pltpu.PrefetchScalarGridSpec(
            num_scalar_prefetch=1, grid=(4,),
            in_specs=[pl.BlockSpec((1, rows, HALF), lambda j, c: (j, 0, c[0])),
                      pl.BlockSpec((1, rows, HALF), lambda j, c: (j, 0, 0))],
            out_specs=pl.BlockSpec((1, rows, HALF), lambda j, c: (j, 0, 0))),
        compiler_params=_params(("arbitrary",)),
    )(c_arr, g, got)


def _add_chips(g, pair_got, chip_got, pos_arr):
    _, rows, _ = g.shape

    def body(pos_ref, g_ref, p_ref, r_ref, o_ref):
        own = g_ref[0] + p_ref[0]
        o_ref[...] = ((own + r_ref[0].astype(F32)) + r_ref[1].astype(F32)) + r_ref[2].astype(F32)

    return pl.pallas_call(
        body, name="grad_add_chips", out_shape=jax.ShapeDtypeStruct((rows, HALF), F32),
        grid_spec=pltpu.PrefetchScalarGridSpec(
            num_scalar_prefetch=1, grid=(1,),
            in_specs=[pl.BlockSpec((1, rows, HALF), lambda i, p: (p[0], 0, p[1])),
                      pl.BlockSpec((1, rows, HALF), lambda i, p: (p[0], 0, 0)),
                      pl.BlockSpec((3, rows, HALF), lambda i, p: (0, 0, 0))],
            out_specs=pl.BlockSpec((rows, HALF), lambda i, p: (0, 0))),
        compiler_params=_params(("arbitrary",)),
    )(pos_arr, g, pair_got, chip_got)


def _sum_small(gathered):
    def body(g_ref, o_ref):
        acc = g_ref[0:SMALL_ROWS, :]
        for d in range(1, 8):
            acc = acc + g_ref[d * SMALL_ROWS:(d + 1) * SMALL_ROWS, :]
        o_ref[...] = acc

    return pl.pallas_call(
        body, name="sum_small", out_shape=jax.ShapeDtypeStruct((SMALL_ROWS, 128), F32),
    )(gathered)


def _adamw(w, g, m, v):
    rows, cols = w.shape
    tr = 256 if rows % 256 == 0 else rows
    c1 = 1.0 - ADAM_B1 ** ADAM_STEP
    c2 = 1.0 - ADAM_B2 ** ADAM_STEP

    def body(w_ref, g_ref, m_ref, v_ref, d_ref, mo_ref, vo_ref):
        gg = g_ref[...]
        mn = ADAM_B1 * m_ref[...] + (1.0 - ADAM_B1) * gg
        vn = ADAM_B2 * v_ref[...] + (1.0 - ADAM_B2) * (gg * gg)
        d_ref[...] = -ADAM_LR * ((mn / c1) / (jnp.sqrt(vn / c2) + ADAM_EPS) + ADAM_WD * w_ref[...])
        mo_ref[...] = mn
        vo_ref[...] = vn

    spec = pl.BlockSpec((tr, cols), lambda i: (i, 0))
    return pl.pallas_call(
        body, name="adamw", out_shape=[jax.ShapeDtypeStruct(w.shape, F32)] * 3,
        grid=(rows // tr,), in_specs=[spec] * 4, out_specs=[spec] * 3,
        compiler_params=_params(("parallel",)),
    )(w, g, m, v)


def _ln_in_fwd(x, meta, g, b):
    bsz, s, d = x.shape
    tp = s + HEAD
    nh = 2
    sh = s // nh
    rc = min(256, sh)

    def body(x_ref, meta_ref, g_ref, b_ref, s0_ref, s0b_ref):
        h = pl.program_id(1)
        gg, bb = g_ref[...], b_ref[...]

        @pl.when(h == 0)
        def _():
            s0_ref[0, 0:PAD, :] = jnp.zeros((PAD, d), F32)
            s0b_ref[0, 0:PAD, :] = jnp.zeros((PAD, d), BF16)
            mh, _ = _ln_stats(meta_ref[...])
            mv = mh * gg + bb
            s0_ref[0, PAD:HEAD, :] = mv
            s0b_ref[0, PAD:HEAD, :] = mv.astype(BF16)

        def step(i, carry):
            src = pl.ds(pl.multiple_of(i * rc, rc), rc)
            dst = pl.ds(pl.multiple_of(HEAD + h * sh + i * rc, 64), rc)
            xh, _ = _ln_stats(x_ref[0, src, :])
            val = xh * gg + bb
            s0_ref[0, dst, :] = val
            s0b_ref[0, dst, :] = val.astype(BF16)
            return carry

        lax.fori_loop(0, sh // rc, step, 0)

    full = lambda bi, hi: (bi, 0, 0)
    return pl.pallas_call(
        body, name="ln_in_fwd",
        out_shape=[jax.ShapeDtypeStruct((bsz, tp, d), F32), jax.ShapeDtypeStruct((bsz, tp, d), BF16)],
        grid=(bsz, nh),
        in_specs=[pl.BlockSpec((1, sh, d), lambda bi, hi: (bi, hi, 0)),
                  pl.BlockSpec((N_META, d), lambda bi, hi: (0, 0)),
                  pl.BlockSpec((1, d), lambda bi, hi: (0, 0)),
                  pl.BlockSpec((1, d), lambda bi, hi: (0, 0))],
        out_specs=[pl.BlockSpec((1, tp, d), full)] * 2,
        compiler_params=_params(("parallel", "arbitrary")),
    )(x, meta, g, b)


def _in_proj(s0b, w_int):
    r, d = s0b.shape
    tm = _row_tile(r)

    def body(a_ref, w_ref, o_ref):
        o_ref[...] = _nt(a_ref[...], w_ref[...])

    return pl.pallas_call(
        body, name="in_proj", out_shape=jax.ShapeDtypeStruct((r, D_IN_PAD), F32),
        grid=(r // tm,),
        in_specs=[pl.BlockSpec((tm, d), lambda i: (i, 0)), pl.BlockSpec((D_IN_PAD, d), lambda i: (0, 0))],
        out_specs=pl.BlockSpec((tm, D_IN_PAD), lambda i: (i, 0)),
        compiler_params=_params(("parallel",)),
    )(s0b, w_int)


def _conv_fwd(u, conv_w, conv_b):
    bsz, tp, _ = u.shape
    nchunk = tp // CHUNK
    win = CHUNK + 32
    nct = D_CONV // 128

    def body(cv_ref, cg_ref, w_ref, cb_ref, hc_ref, h_scr, win_scr):
        h_scr[0:32, :] = jnp.zeros((32, 128), F32)
        h_scr[32:32 + tp, :] = cv_ref[0] * _sigmoid(cg_ref[0])
        cb = cb_ref[...]

        def step(n, carry):
            r0 = pl.multiple_of(n * CHUNK, CHUNK)
            win_scr[...] = h_scr[pl.ds(r0, win), :]
            acc = jnp.zeros((CHUNK, 128), F32)
            for j in range(CONV_WIDTH):
                acc = acc + w_ref[j:j + 1, :] * win_scr[2 + j:2 + j + CHUNK, :]
            hc_ref[0, pl.ds(r0, CHUNK), :] = acc + cb
            return carry

        lax.fori_loop(0, nchunk, step, 0)

    return pl.pallas_call(
        body, name="conv_fwd", out_shape=jax.ShapeDtypeStruct((bsz, tp, D_CONV), F32),
        grid=(bsz, nct),
        in_specs=[pl.BlockSpec((1, tp, 128), lambda bi, t: (bi, 0, C_VAL // 128 + t)),
                  pl.BlockSpec((1, tp, 128), lambda bi, t: (bi, 0, C_GATE // 128 + t)),
                  pl.BlockSpec((32, 128), lambda bi, t: (0, t)),
                  pl.BlockSpec((1, 128), lambda bi, t: (0, t))],
        out_specs=pl.BlockSpec((1, tp, 128), lambda bi, t: (bi, 0, t)),
        scratch_shapes=[pltpu.VMEM((tp + 32, 128), F32), pltpu.VMEM((win, 128), F32)],
        compiler_params=_params(("parallel", "parallel")),
    )(u, u, conv_w, conv_b)


def _chunk_loop(nchunk, step, init):
    unroll = 3 if nchunk % 3 == 0 else 1

    def trip(t, carry):
        for j in range(unroll):
            carry = step(t * unroll + j, carry)
        return carry

    return lax.fori_loop(0, nchunk // unroll, trip, init)


def _gla_consts():
    row = lax.broadcasted_iota(jnp.int32, (CHUNK, CHUNK), 0)
    col = lax.broadcasted_iota(jnp.int32, (CHUNK, CHUNK), 1)
    lane = lax.broadcasted_iota(jnp.int32, (1, 128), 1)
    return row >= col, [lane < GLA_DK, lane >= GLA_DK]


def _gla_triu():
    row = lax.broadcasted_iota(jnp.int32, (CHUNK, CHUNK), 0)
    col = lax.broadcasted_iota(jnp.int32, (CHUNK, CHUNK), 1)
    return (row <= col).astype(F32)


def _gla_chunk_terms(n, q_ref, k_ref, gd_ref, gup_ref, gb_ref, tri):
    r0 = pl.multiple_of(n * CHUNK, CHUNK)
    rows = pl.ds(r0, CHUNK)
    z = _nn(gd_ref[0, rows, :].astype(BF16), gup_ref[...]) + gb_ref[...]
    rowid = n * CHUNK + lax.broadcasted_iota(jnp.int32, (CHUNK, 1), 0)
    valid = rowid >= PAD
    lg = jnp.where(valid, _log_sigmoid(z) * (1.0 / GLA_TAU), 0.0)
    bcum = _nn(tri.astype(F32), lg, precision=lax.Precision.HIGHEST)
    blast = bcum[CHUNK - 1:CHUNK, :]
    eb = jnp.exp(bcum)
    enb = jnp.exp(-bcum)
    erest = jnp.exp(blast - bcum)
    q = q_ref[0, rows, :] * Q_SCALE
    k = k_ref[0, rows, :]
    return rows, valid, z, eb, enb, erest, jnp.exp(blast), q * eb, k * enb, k * erest


def _grid_ends(grid):
    ids = [pl.program_id(i) for i in range(len(grid))]
    first = functools.reduce(jnp.logical_and, [i == 0 for i in ids])
    last = functools.reduce(jnp.logical_and, [i == g - 1 for i, g in zip(ids, grid)])
    return first, last


def _gla_fwd(u, gup, gbias, gnorm, shards):
    bsz, tp, _ = u.shape
    nchunk = tp // CHUNK
    ns = len(shards)
    grid = (bsz, 2)

    def body(*refs):
        q_ref, k_ref, v_ref, r_ref, gd_ref, gup_ref, gb_ref, gn_ref = refs[:8]
        out_ref, o_ref, st_ref = refs[8 + ns:11 + ns]
        s_scr = refs[11 + 2 * ns]
        start, relay, finish = _gather_steps(refs[8:8 + ns], refs[11 + ns:11 + 2 * ns], refs[12 + 2 * ns:])
        first, last = _grid_ends(grid)
        pl.when(first)(start)
        pl.when(last)(relay)
        tri, hmask = _gla_consts()
        s_scr[...] = jnp.zeros_like(s_scr)
        gn = gn_ref[...]

        def step(n, carry):
            rows, _, _, _, _, _, dec, qe, ke, kd = _gla_chunk_terms(n, q_ref, k_ref, gd_ref, gup_ref, gb_ref, tri)
            keb, kdb = ke.astype(BF16), kd.astype(BF16)
            for h in range(2):
                cols = slice(h * GLA_DV, (h + 1) * GLA_DV)
                qh = jnp.where(hmask[h], qe, 0.0).astype(BF16)
                vh = v_ref[0, rows, cols].astype(BF16)
                a = jnp.where(tri, _nt(qh, keb), 0.0)
                st = s_scr[h]
                st_ref[0, h, n] = st
                o = _nn(a.astype(BF16), vh) + _nt(qh, st.astype(BF16))
                s_scr[h] = dec * st + _tn(vh, kdb)
                o_ref[0, rows, cols] = o
                rms = lax.rsqrt(jnp.mean(o * o, axis=-1, keepdims=True) + LN_EPS)
                rh = r_ref[0, rows, cols]
                out_ref[0, rows, cols] = (o * rms * gn * (rh * _sigmoid(rh))).astype(BF16)
            return carry

        _chunk_loop(nchunk, step, 0)
        pl.when(last)(finish)

    res = pl.pallas_call(
        body, name="gla_fwd",
        out_shape=[jax.ShapeDtypeStruct((bsz, tp, 512), BF16), jax.ShapeDtypeStruct((bsz, tp, 512), F32),
                   jax.ShapeDtypeStruct((bsz, GLA_HEADS, nchunk, GLA_DV, 128), F32)]
        + [jax.ShapeDtypeStruct((4,) + s.shape, s.dtype) for s in shards],
        grid=grid,
        in_specs=[pl.BlockSpec((1, tp, 128), lambda bi, p: (bi, 0, C_Q // 128 + p)),
                  pl.BlockSpec((1, tp, 128), lambda bi, p: (bi, 0, C_K // 128 + p)),
                  pl.BlockSpec((1, tp, 256), lambda bi, p: (bi, 0, C_V // 256 + p)),
                  pl.BlockSpec((1, tp, 256), lambda bi, p: (bi, 0, C_R // 256 + p)),
                  pl.BlockSpec((1, tp, 128), lambda bi, p: (bi, 0, C_GD // 128)),
                  pl.BlockSpec((128, 128), lambda bi, p: (0, p)),
                  pl.BlockSpec((1, 128), lambda bi, p: (0, p)),
                  pl.BlockSpec((1, 128), lambda bi, p: (0, 0))] + [ANY] * ns,
        out_specs=[pl.BlockSpec((1, tp, 256), lambda bi, p: (bi, 0, p)),
                   pl.BlockSpec((1, tp, 256), lambda bi, p: (bi, 0, p)),
                   pl.BlockSpec((1, 2, nchunk, GLA_DV, 128), lambda bi, p: (bi, p, 0, 0, 0))] + [ANY] * ns,
        scratch_shapes=[pltpu.VMEM((2, GLA_DV, 128), F32)] + _gather_sems(ns),
        compiler_params=_params(("arbitrary", "arbitrary")),
    )(u, u, u, u, u, gup, gbias, gnorm, *shards)
    return res[0], res[1], res[2], _place_own(res[3:], shards)


def _out_proj_ln1(hc, gla_out, w_out, s0, cg, cb, g1, b1):
    r, d = s0.shape
    tm = _row_tile(r)

    def body(hc_ref, a_ref, w_ref, s0_ref, cg_ref, cb_ref, g_ref, b_ref, co_ref, xh_ref, rstd_ref, s1b_ref):
        xc, _ = _ln_stats(hc_ref[...])
        nv = xc * cg_ref[...] + cb_ref[...]
        co = (nv * _sigmoid(nv)).astype(BF16)
        co_ref[...] = co
        mix = _nn(co, w_ref[0:D_CONV, :]) + _nn(a_ref[...], w_ref[D_CONV:, :])
        xh, rstd = _ln_stats(ALPHA * s0_ref[...] + mix)
        xh_ref[...] = xh
        rstd_ref[...] = rstd
        s1b_ref[...] = (xh * g_ref[...] + b_ref[...]).astype(BF16)

    row = lambda n: pl.BlockSpec((tm, n), lambda i: (i, 0))
    vec = lambda n: pl.BlockSpec((1, n), lambda i: (0, 0))
    return pl.pallas_call(
        body, name="out_proj_ln1",
        out_shape=[jax.ShapeDtypeStruct((r, D_CONV), BF16), jax.ShapeDtypeStruct((r, d), F32),
                   jax.ShapeDtypeStruct((r, 1), F32), jax.ShapeDtypeStruct((r, d), BF16)],
        grid=(r // tm,),
        in_specs=[row(D_CONV), row(512), pl.BlockSpec((d, d), lambda i: (0, 0)), row(d),
                  vec(D_CONV), vec(D_CONV), vec(d), vec(d)],
        out_specs=[row(D_CONV), row(d), row(1), row(d)],
        compiler_params=_params(("parallel",)),
    )(hc, gla_out, w_out, s0, cg, cb, g1, b1)


def _ffn1(s1b, w1t, shards):
    r, d = s1b.shape
    tm = _row_tile(r)
    ns = len(shards)
    grid = (r // tm,)

    def body(*refs):
        a_ref, w_ref = refs[:2]
        o_ref = refs[2 + ns]
        start, relay, finish = _gather_steps(refs[2:2 + ns], refs[3 + ns:3 + 2 * ns], refs[3 + 2 * ns:])
        first, last = _grid_ends(grid)
        pl.when(first)(start)
        pl.when(last)(relay)
        o_ref[...] = jnp.maximum(_nt(a_ref[...], w_ref[...]), 0.0).astype(BF16)
        pl.when(last)(finish)

    res = pl.pallas_call(
        body, name="ffn1",
        out_shape=[jax.ShapeDtypeStruct((r, D_FF), BF16)] + [jax.ShapeDtypeStruct((4,) + s.shape, s.dtype) for s in shards],
        grid=grid,
        in_specs=[pl.BlockSpec((tm, d), lambda i: (i, 0)), pl.BlockSpec((D_FF, d), lambda i: (0, 0))] + [ANY] * ns,
        out_specs=[pl.BlockSpec((tm, D_FF), lambda i: (i, 0))] + [ANY] * ns,
        scratch_shapes=_gather_sems(ns),
        compiler_params=_params(("arbitrary",)),
    )(s1b, w1t, *shards)
    return res[0], _place_own(res[1:], shards)


def _ffn2_ln2_loss(ra, w2, xhat1, g1, b1, g2, b2, tgt, tp):
    r, d = xhat1.shape
    tm = _row_tile(tp)
    per = tp // tm

    def body(ra_ref, w_ref, xh1_ref, g1_ref, b1_ref, g2_ref, b2_ref, t_ref, dt_ref, dtb_ref, acc_ref):
        i = pl.program_id(0)

        @pl.when(i == 0)
        def _():
            acc_ref[...] = jnp.zeros_like(acc_ref)

        rf = ra_ref[...].astype(F32)
        f = _nn((rf * rf).astype(BF16), w_ref[...])
        s1 = xh1_ref[...] * g1_ref[...] + b1_ref[...]
        xh2, rstd2 = _ln_stats(ALPHA * s1 + f)
        y = xh2 * g2_ref[...] + b2_ref[...]
        rowid = (i % per) * tm + lax.broadcasted_iota(jnp.int32, (tm, 1), 0)
        e = jnp.where(rowid >= HEAD, y - t_ref[...], 0.0)
        dy = e * (1.0 / d)
        dt2 = _ln_bwd(dy * g2_ref[...], xh2, rstd2)
        dt_ref[...] = dt2
        dtb_ref[...] = dt2.astype(BF16)
        acc_ref[0:1, :] += (0.5 / d) * jnp.sum(e * e, axis=0, keepdims=True)
        acc_ref[1:2, :] += jnp.sum(dy * xh2, axis=0, keepdims=True)
        acc_ref[2:3, :] += jnp.sum(dy, axis=0, keepdims=True)

    row = lambda n: pl.BlockSpec((tm, n), lambda i: (i, 0))
    vec = pl.BlockSpec((1, d), lambda i: (0, 0))
    return pl.pallas_call(
        body, name="ffn2_ln2_loss",
        out_shape=[jax.ShapeDtypeStruct((r, d), F32), jax.ShapeDtypeStruct((r, d), BF16),
                   jax.ShapeDtypeStruct((8, d), F32)],
        grid=(r // tm,),
        in_specs=[row(D_FF), pl.BlockSpec((D_FF, d), lambda i: (0, 0)), row(d), vec, vec, vec, vec, row(d)],
        out_specs=[row(d), row(d), pl.BlockSpec((8, d), lambda i: (0, 0))],
        compiler_params=_params(("arbitrary",)),
    )(ra, w2, xhat1, g1, b1, g2, b2, tgt)


def _ffn_bwd_da(dt2b, w2, ra):
    r, d = dt2b.shape
    tm = _row_tile(r)

    def body(g_ref, w_ref, ra_ref, o_ref):
        o_ref[...] = (_nt(g_ref[...], w_ref[...]) * (2.0 * ra_ref[...].astype(F32))).astype(BF16)

    return pl.pallas_call(
        body, name="ffn_bwd_da", out_shape=jax.ShapeDtypeStruct((r, D_FF), BF16),
        grid=(r // tm,),
        in_specs=[pl.BlockSpec((tm, d), lambda i: (i, 0)), pl.BlockSpec((D_FF, d), lambda i: (0, 0)),
                  pl.BlockSpec((tm, D_FF), lambda i: (i, 0))],
        out_specs=pl.BlockSpec((tm, D_FF), lambda i: (i, 0)),
        compiler_params=_params(("parallel",)),
    )(dt2b, w2, ra)


def _ffn_bwd_ln1(da, w1t, dt2, xhat1, rstd1, g1):
    r, d = dt2.shape
    tm = _row_tile(r)

    def body(da_ref, w_ref, dt2_ref, xh_ref, rstd_ref, g_ref, dt_ref, dtb_ref, acc_ref):
        @pl.when(pl.program_id(0) == 0)
        def _():
            acc_ref[...] = jnp.zeros_like(acc_ref)

        ds1 = ALPHA * dt2_ref[...] + _nn(da_ref[...], w_ref[...])
        xh = xh_ref[...]
        dt1 = _ln_bwd(ds1 * g_ref[...], xh, rstd_ref[...])
        dt_ref[...] = dt1
        dtb_ref[...] = dt1.astype(BF16)
        acc_ref[0:1, :] += jnp.sum(ds1 * xh, axis=0, keepdims=True)
        acc_ref[1:2, :] += jnp.sum(ds1, axis=0, keepdims=True)

    row = lambda n: pl.BlockSpec((tm, n), lambda i: (i, 0))
    return pl.pallas_call(
        body, name="ffn_bwd_ln1",
        out_shape=[jax.ShapeDtypeStruct((r, d), F32), jax.ShapeDtypeStruct((r, d), BF16),
                   jax.ShapeDtypeStruct((8, d), F32)],
        grid=(r // tm,),
        in_specs=[row(D_FF), pl.BlockSpec((D_FF, d), lambda i: (0, 0)), row(d), row(d), row(1),
                  pl.BlockSpec((1, d), lambda i: (0, 0))],
        out_specs=[row(d), row(d), pl.BlockSpec((8, d), lambda i: (0, 0))],
        compiler_params=_params(("arbitrary",)),
    )(da, w1t, dt2, xhat1, rstd1, g1)


def _matmul_tn(lhs, rhs, bm, square_lhs=False, name="matmul_tn"):
    r, m = lhs.shape
    n = rhs.shape[1]
    tk = _row_tile(r)

    def body(a_ref, b_ref, o_ref):
        @pl.when(pl.program_id(1) == 0)
        def _():
            o_ref[...] = jnp.zeros_like(o_ref)

        a = a_ref[...]
        if square_lhs:
            af = a.astype(F32)
            a = (af * af).astype(BF16)
        o_ref[...] += _tn(a, b_ref[...])

    return pl.pallas_call(
        body, name=name, out_shape=jax.ShapeDtypeStruct((m, n), F32),
        grid=(m // bm, r // tk),
        in_specs=[pl.BlockSpec((tk, bm), lambda i, k: (k, i)), pl.BlockSpec((tk, n), lambda i, k: (k, 0))],
        out_specs=pl.BlockSpec((bm, n), lambda i, k: (i, 0)),
        compiler_params=_params(("parallel", "arbitrary")),
    )(lhs, rhs)


def _out_proj_bwd(dt1b, w_out, hc, cg, cb):
    r, d = dt1b.shape
    tm = _row_tile(r)

    def body(g_ref, w_ref, hc_ref, cg_ref, cb_ref, dhc_ref, dgla_ref, acc_ref):
        @pl.when(pl.program_id(0) == 0)
        def _():
            acc_ref[...] = jnp.zeros_like(acc_ref)

        dmix = _nt(g_ref[...], w_ref[...])
        dgla_ref[...] = dmix[:, D_CONV:]
        gg = cg_ref[...]
        xh, rstd = _ln_stats(hc_ref[...])
        nv = xh * gg + cb_ref[...]
        sig = _sigmoid(nv)
        dn = dmix[:, :D_CONV] * (sig * (1.0 + nv * (1.0 - sig)))
        dhc = _ln_bwd(dn * gg, xh, rstd)
        dhc_ref[...] = dhc
        acc_ref[0:1, :] += jnp.sum(dhc, axis=0, keepdims=True)
        acc_ref[1:2, :] += jnp.sum(dn * xh, axis=0, keepdims=True)
        acc_ref[2:3, :] += jnp.sum(dn, axis=0, keepdims=True)

    row = lambda n: pl.BlockSpec((tm, n), lambda i: (i, 0))
    vec = pl.BlockSpec((1, D_CONV), lambda i: (0, 0))
    return pl.pallas_call(
        body, name="out_proj_bwd",
        out_shape=[jax.ShapeDtypeStruct((r, D_CONV), F32), jax.ShapeDtypeStruct((r, 512), F32),
                   jax.ShapeDtypeStruct((8, D_CONV), F32)],
        grid=(r // tm,),
        in_specs=[row(d), pl.BlockSpec((d, d), lambda i: (0, 0)), row(D_CONV), vec, vec],
        out_specs=[row(D_CONV), row(512), pl.BlockSpec((8, D_CONV), lambda i: (0, 0))],
        compiler_params=_params(("arbitrary",)),
    )(dt1b, w_out, hc, cg, cb)


def _conv_bwd(dhc, u, conv_w):
    bsz, tp, _ = u.shape
    nchunk = tp // CHUNK
    win = CHUNK + 32
    nct = D_CONV // 128

    def body(dhc_ref, cv_ref, cg_ref, w_ref, dv_ref, dg_ref, dw_ref, h_scr, dhc_scr, hwin, dwin, dw_scr):
        h_scr[0:32, :] = jnp.zeros((32, 128), F32)
        h_scr[32:32 + tp, :] = cv_ref[0] * _sigmoid(cg_ref[0])
        dhc_scr[0:tp, :] = dhc_ref[0]
        dhc_scr[tp:tp + 32, :] = jnp.zeros((32, 128), F32)
        dw_scr[...] = jnp.zeros_like(dw_scr)

        def step(n, carry):
            r0 = pl.multiple_of(n * CHUNK, CHUNK)
            rows = pl.ds(r0, CHUNK)
            hwin[...] = h_scr[pl.ds(r0, win), :]
            dwin[...] = dhc_scr[pl.ds(r0, win), :]
            dcur = dwin[0:CHUNK, :]
            acc = jnp.zeros((CHUNK, 128), F32)
            for j in range(CONV_WIDTH):
                acc = acc + w_ref[j:j + 1, :] * dwin[30 - j:30 - j + CHUNK, :]
                prod = dcur * hwin[2 + j:2 + j + CHUNK, :]
                dw_scr[j * 8:(j + 1) * 8, :] += jnp.sum(prod.reshape(CHUNK // 8, 8, 128), axis=0)
            cg = cg_ref[0, rows, :]
            sig = _sigmoid(cg)
            rowid = n * CHUNK + lax.broadcasted_iota(jnp.int32, (CHUNK, 1), 0)
            dh = jnp.where(rowid >= PAD, acc, 0.0)
            dv_ref[0, rows, :] = (dh * sig).astype(BF16)
            dg_ref[0, rows, :] = (dh * cv_ref[0, rows, :] * sig * (1.0 - sig)).astype(BF16)
            return carry

        lax.fori_loop(0, nchunk, step, 0)
        dw_ref[0] = jnp.zeros((32, 128), F32)
        for j in range(CONV_WIDTH):
            dw_ref[0, j:j + 1, :] = jnp.sum(dw_scr[j * 8:(j + 1) * 8, :], axis=0, keepdims=True)

    blk = lambda off: pl.BlockSpec((1, tp, 128), lambda bi, t: (bi, 0, off // 128 + t))
    return pl.pallas_call(
        body, name="conv_bwd",
        out_shape=[jax.ShapeDtypeStruct((bsz, tp, D_CONV), BF16), jax.ShapeDtypeStruct((bsz, tp, D_CONV), BF16),
                   jax.ShapeDtypeStruct((bsz, 32, D_CONV), F32)],
        grid=(bsz, nct),
        in_specs=[blk(0), blk(C_VAL), blk(C_GATE), pl.BlockSpec((32, 128), lambda bi, t: (0, t))],
        out_specs=[blk(0), blk(0), pl.BlockSpec((1, 32, 128), lambda bi, t: (bi, 0, t))],
        scratch_shapes=[pltpu.VMEM((tp + 32, 128), F32), pltpu.VMEM((tp + 32, 128), F32),
                        pltpu.VMEM((win, 128), F32), pltpu.VMEM((win, 128), F32),
                        pltpu.VMEM((CONV_WIDTH * 8, 128), F32)],
        compiler_params=_params(("parallel", "parallel")),
    )(dhc, u, u, conv_w)


def _gla_bwd(dgla, u, o_pre, states, gup, gbias, gnorm, parts):
    bsz, tp, _ = u.shape
    nchunk = tp // CHUNK
    ns = len(parts)
    grid = (bsz, 2)

    def body(*refs):
        dy_ref, q_ref, k_ref, v_ref, r_ref, gd_ref, o_ref, st_ref, gup_ref, gb_ref, gn_ref = refs[:11]
        dq_ref, dk_ref, dv_ref, dr_ref, dgd_ref, dgup_ref, vec_ref = refs[11 + ns:18 + ns]
        h_scr, gup_acc = refs[18 + 2 * ns:20 + 2 * ns]
        start, finish = _chip_exchange_steps(refs[11:11 + ns], refs[18 + ns:18 + 2 * ns], refs[20 + 2 * ns:])
        first, last = _grid_ends(grid)
        pl.when(first)(start)
        tri, hmask = _gla_consts()
        triu = _gla_triu()
        h_scr[...] = jnp.zeros_like(h_scr)
        gup_acc[...] = jnp.zeros_like(gup_acc)
        gn = gn_ref[...]
        gupb = gup_ref[...]

        def step(i, carry):
            dbias, dgn = carry
            n = nchunk - 1 - i
            rows, valid, z, eb, enb, erest, dec, qe, ke, kd = _gla_chunk_terms(
                n, q_ref, k_ref, gd_ref, gup_ref, gb_ref, tri)
            keb, kdb = ke.astype(BF16), kd.astype(BF16)
            dqe = jnp.zeros((CHUNK, 128), F32)
            dke = jnp.zeros((CHUNK, 128), F32)
            dkd = jnp.zeros((CHUNK, 128), F32)
            ddec = jnp.zeros((1, 128), F32)
            for h in range(2):
                cols = slice(h * GLA_DV, (h + 1) * GLA_DV)
                o = o_ref[0, rows, cols]
                rh = r_ref[0, rows, cols]
                dy = dy_ref[0, rows, cols]
                rms = lax.rsqrt(jnp.mean(o * o, axis=-1, keepdims=True) + LN_EPS)
                nrm = o * rms
                sig = _sigmoid(rh)
                sw = rh * sig
                dr_ref[0, rows, cols] = (dy * nrm * gn * (sig * (1.0 + rh * (1.0 - sig)))).astype(BF16)
                dgn = dgn + jnp.sum(dy * nrm * sw, axis=0, keepdims=True)
                dn = dy * gn * sw
                do = rms * (dn - nrm * jnp.mean(dn * nrm, axis=-1, keepdims=True))
                dob = do.astype(BF16)
                qh = jnp.where(hmask[h], qe, 0.0).astype(BF16)
                vh = v_ref[0, rows, cols].astype(BF16)
                a = jnp.where(tri, _nt(qh, keb), 0.0).astype(BF16)
                st = st_ref[0, h, n]
                ht = h_scr[h]
                da = jnp.where(tri, _nt(dob, vh), 0.0).astype(BF16)
                dqe = dqe + jnp.where(hmask[h], _nn(da, keb) + _nn(dob, st.astype(BF16)), 0.0)
                dke = dke + _tn(da, qh)
                dv_ref[0, rows, cols] = (_tn(a, dob) + _nt(kdb, ht.astype(BF16))).astype(BF16)
                dkd = dkd + jnp.where(hmask[h], _nn(vh, ht.astype(BF16)), 0.0)
                ddec = ddec + jnp.where(hmask[h], jnp.sum(ht * st, axis=0, keepdims=True), 0.0)
                h_scr[h] = dec * ht + _tn(dob, qh)
            dq_ref[0, rows, :] = (dqe * eb * Q_SCALE).astype(BF16)
            dk_ref[0, rows, :] = (dke * enb + dkd * erest).astype(BF16)
            db = dqe * qe - dke * ke - dkd * kd
            dblast = jnp.sum(dkd * kd, axis=0, keepdims=True) + ddec * dec
            lastrow = lax.broadcasted_iota(jnp.int32, (CHUNK, 1), 0) == CHUNK - 1
            db = db + jnp.where(lastrow, dblast, 0.0)
            dlg = jnp.where(valid, _nn(triu, db, precision=lax.Precision.HIGHEST), 0.0)
            dz = dlg * (1.0 / GLA_TAU) * (1.0 - _sigmoid(z))
            dzb = dz.astype(BF16)
            dgd_ref[0, 0, rows, :] = _nt(dzb, gupb).astype(BF16)
            gup_acc[...] += _tn(gd_ref[0, rows, :].astype(BF16), dzb)
            return dbias + jnp.sum(dz, axis=0, keepdims=True), dgn

        zero = jnp.zeros((1, 128), F32)
        dbias, dgn = _chunk_loop(nchunk, step, (zero, zero))
        dgup_ref[0] = gup_acc[...]
        vec_ref[0] = jnp.zeros((8, 128), F32)
        vec_ref[0, 0:1, :] = dbias
        vec_ref[0, 1:2, :] = dgn
        pl.when(last)(finish)

    pair = lambda w, off: pl.BlockSpec((1, tp, w), lambda bi, p: (bi, 0, off // w + p))
    res = pl.pallas_call(
        body, name="gla_bwd",
        out_shape=[jax.ShapeDtypeStruct((bsz, tp, 256), BF16), jax.ShapeDtypeStruct((bsz, tp, 256), BF16),
                   jax.ShapeDtypeStruct((bsz, tp, 512), BF16), jax.ShapeDtypeStruct((bsz, tp, 512), BF16),
                   jax.ShapeDtypeStruct((bsz, 2, tp, 128), BF16), jax.ShapeDtypeStruct((bsz, 128, 256), F32),
                   jax.ShapeDtypeStruct((bsz, 8, 256), F32)] + _chip_exchange_shapes(parts),
        grid=grid,
        in_specs=[pair(256, 0), pair(128, C_Q), pair(128, C_K), pair(256, C_V), pair(256, C_R),
                  pl.BlockSpec((1, tp, 128), lambda bi, p: (bi, 0, C_GD // 128)),
                  pair(256, 0),
                  pl.BlockSpec((1, 2, nchunk, GLA_DV, 128), lambda bi, p: (bi, p, 0, 0, 0)),
                  pl.BlockSpec((128, 128), lambda bi, p: (0, p)),
                  pl.BlockSpec((1, 128), lambda bi, p: (0, p)),
                  pl.BlockSpec((1, 128), lambda bi, p: (0, 0))] + [ANY] * ns,
        out_specs=[pair(128, 0), pair(128, 0), pair(256, 0), pair(256, 0),
                   pl.BlockSpec((1, 1, tp, 128), lambda bi, p: (bi, p, 0, 0)),
                   pl.BlockSpec((1, 128, 128), lambda bi, p: (bi, 0, p)),
                   pl.BlockSpec((1, 8, 128), lambda bi, p: (bi, 0, p))] + [ANY] * ns,
        scratch_shapes=[pltpu.VMEM((2, GLA_DV, 128), F32), pltpu.VMEM((128, 128), F32)] + _chip_exchange_sems(ns),
        compiler_params=_params(("arbitrary", "arbitrary")),
    )(dgla, u, u, u, u, u, o_pre, states, gup, gbias, gnorm, *parts)
    return res[:7], res[7:]


_DU_OFFSETS = (C_VAL, C_GATE, C_Q, C_K, C_V, C_R)
_DU_WIDTHS = (512, 512, 256, 256, 512, 512)


def _du_specs(tm, per, row_map):
    specs = [pl.BlockSpec((tm, w), row_map) for w in _DU_WIDTHS]
    for p in range(2):
        specs.append(pl.BlockSpec((1, 1, tm, 128), lambda *ix, p=p: (row_map(*ix)[0] // per, p, row_map(*ix)[0] % per, 0)))
    return specs


def _du_pieces(refs):
    out = [(off, ref[...]) for off, ref in zip(_DU_OFFSETS, refs[:6])]
    dgd = (refs[6][0, 0].astype(F32) + refs[7][0, 0].astype(F32)).astype(BF16)
    out.append((C_GD, dgd))
    return out


def _in_proj_bwd(pieces, dgd, w_int, dt1, tp):
    r, d = dt1.shape
    tm = _row_tile(tp)
    per = tp // tm

    def body(*refs):
        w_ref, dt_ref, o_ref = refs[8:]
        acc = ALPHA * dt_ref[...]
        for off, val in _du_pieces(refs[:8]):
            acc = acc + _nn(val, w_ref[off:off + val.shape[1], :])
        o_ref[...] = acc

    row = lambda i: (i, 0)
    return pl.pallas_call(
        body, name="in_proj_bwd", out_shape=jax.ShapeDtypeStruct((r, d), F32),
        grid=(r // tm,),
        in_specs=_du_specs(tm, per, row) + [pl.BlockSpec((D_IN_PAD, d), lambda i: (0, 0)), pl.BlockSpec((tm, d), row)],
        out_specs=pl.BlockSpec((tm, d), row),
        compiler_params=_params(("parallel",)),
    )(*pieces, dgd, dgd, w_int, dt1)


def _grad_w_in(pieces, dgd, s0b, tp):
    r, d = s0b.shape
    tk = _row_tile(tp)
    per = tp // tk

    def body(*refs):
        s_ref, o_ref = refs[8:]

        @pl.when(pl.program_id(0) == 0)
        def _():
            o_ref[...] = jnp.zeros_like(o_ref)

        s = s_ref[...]
        for off, val in _du_pieces(refs[:8]):
            o_ref[off:off + val.shape[1], :] += _tn(val, s)

    row = lambda k: (k, 0)
    return pl.pallas_call(
        body, name="grad_w_in", out_shape=jax.ShapeDtypeStruct((D_IN_PAD, d), F32),
        grid=(r // tk,),
        in_specs=_du_specs(tk, per, row) + [pl.BlockSpec((tk, d), row)],
        out_specs=pl.BlockSpec((D_IN_PAD, d), lambda k: (0, 0)),
        compiler_params=_params(("arbitrary",)),
    )(*pieces, dgd, dgd, s0b)


def _ln_in_bwd(ds0, x, meta, g, parts):
    bsz, s, d = x.shape
    tp = s + HEAD
    nh = 2
    sh = s // nh
    rc = min(256, sh)
    ns = len(parts)
    grid = (bsz, nh)

    def body(*refs):
        ds_ref, x_ref, meta_ref, g_ref = refs[:4]
        gx_ref, dm_ref, vec_ref = refs[4 + ns:7 + ns]
        start, finish = _chip_exchange_steps(refs[4:4 + ns], refs[7 + ns:7 + 2 * ns], refs[7 + 2 * ns:])
        first, last = _grid_ends(grid)
        pl.when(first)(start)
        h = pl.program_id(1)
        gg = g_ref[...]

        @pl.when(h == 0)
        def _():
            mh, mr = _ln_stats(meta_ref[...])
            dsm = ds_ref[0, PAD:HEAD, :]
            dm_ref[0] = _ln_bwd(dsm * gg, mh, mr)
            vec_ref[0] = jnp.zeros((8, d), F32)
            vec_ref[0, 0:1, :] = jnp.sum(dsm * mh, axis=0, keepdims=True)
            vec_ref[0, 1:2, :] = jnp.sum(dsm, axis=0, keepdims=True)

        def step(i, carry):
            sg, sb = carry
            dst = pl.ds(pl.multiple_of(i * rc, rc), rc)
            src = pl.ds(pl.multiple_of(HEAD + h * sh + i * rc, 64), rc)
            xh, rstd = _ln_stats(x_ref[0, dst, :])
            dsv = ds_ref[0, src, :]
            gx_ref[0, dst, :] = _ln_bwd(dsv * gg, xh, rstd)
            return sg + jnp.sum(dsv * xh, axis=0, keepdims=True), sb + jnp.sum(dsv, axis=0, keepdims=True)

        zero = jnp.zeros((1, d), F32)
        sg, sb = lax.fori_loop(0, sh // rc, step, (zero, zero))
        vec_ref[0, 0:1, :] += sg
        vec_ref[0, 1:2, :] += sb
        pl.when(last)(finish)

    res = pl.pallas_call(
        body, name="ln_in_bwd",
        out_shape=[jax.ShapeDtypeStruct((bsz, s, d), F32), jax.ShapeDtypeStruct((bsz, N_META, d), F32),
                   jax.ShapeDtypeStruct((bsz, 8, d), F32)] + _chip_exchange_shapes(parts),
        grid=grid,
        in_specs=[pl.BlockSpec((1, tp, d), lambda bi, hi: (bi, 0, 0)),
                  pl.BlockSpec((1, sh, d), lambda bi, hi: (bi, hi, 0)),
                  pl.BlockSpec((N_META, d), lambda bi, hi: (0, 0)),
                  pl.BlockSpec((1, d), lambda bi, hi: (0, 0))] + [ANY] * ns,
        out_specs=[pl.BlockSpec((1, sh, d), lambda bi, hi: (bi, hi, 0)),
                   pl.BlockSpec((1, N_META, d), lambda bi, hi: (bi, 0, 0)),
                   pl.BlockSpec((1, 8, d), lambda bi, hi: (bi, 0, 0))] + [ANY] * ns,
        scratch_shapes=_chip_exchange_sems(ns),
        compiler_params=_params(("arbitrary", "arbitrary")),
    )(ds0, x, meta, g, *parts)
    return res[0], res[1], res[2], res[3:]


def _rows128(a):
    return a.reshape(-1, 128)


def kernel(x, meta_tokens, ln_in_g, ln_in_b, w_in, conv_w, conv_b, conv_ln_g, conv_ln_b, gate_up, gate_bias, gla_norm_g, w_out, ln1_g, ln1_b, w_ff1, w_ff2, ln2_g, ln2_b, loss_target, m_meta_tokens, m_ln_in_g, m_ln_in_b, m_w_in, m_conv_w, m_conv_b, m_conv_ln_g, m_conv_ln_b, m_gate_up, m_gate_bias, m_gla_norm_g, m_w_out, m_ln1_g, m_ln1_b, m_w_ff1, m_w_ff2, m_ln2_g, m_ln2_b, v_meta_tokens, v_ln_in_g, v_ln_in_b, v_w_in, v_conv_w, v_conv_b, v_conv_ln_g, v_conv_ln_b, v_gate_up, v_gate_bias, v_gla_norm_g, v_w_out, v_ln1_g, v_ln1_b, v_w_ff1, v_w_ff2, v_ln2_g, v_ln2_b):
    bsz, seq, d = x.shape
    tp = seq + HEAD
    r = bsz * tp
    xi, yi, ci = _mesh_pos()
    chip = 2 * xi + yi
    c_arr = jnp.reshape(ci, (1,)).astype(jnp.int32)
    pos_arr = jnp.stack([chip, ci]).astype(jnp.int32)

    sh_in = D_IN // 4
    shard_in = jnp.pad(w_in[0].T.astype(BF16), ((0, D_IN_PAD // 4 - sh_in), (0, 0)))
    shard_w1t, shard_wout, shard_w2 = w_ff1[0].T.astype(BF16), w_out[0].astype(BF16), w_ff2[0].astype(BF16)
    small_w = jnp.concatenate([_rows128(meta_tokens), _rows128(conv_w[0]), _rows128(gate_up[0])], axis=0)
    (g_int, g_wout), g_small = _gather_weights([shard_in, shard_wout], small_w)
    w_int = jnp.pad(g_int[:, :sh_in].reshape(D_IN, d), ((0, D_IN_PAD - D_IN), (0, 0)))
    wout = g_wout.reshape(d, d)
    n_meta_rows, n_cw_rows = N_META * 256 // 128, CONV_WIDTH * 128 // 128
    meta_full = jnp.concatenate([g_small[j, :n_meta_rows].reshape(N_META, 256) for j in range(4)], axis=1)
    convw_full = jnp.concatenate(
        [g_small[j, n_meta_rows:n_meta_rows + n_cw_rows].reshape(CONV_WIDTH, 128) for j in range(4)], axis=1)
    gup_full = jnp.concatenate(
        [g_small[j, n_meta_rows + n_cw_rows:].reshape(GLA_RANK, 64) for j in range(4)], axis=1)
    convw_p = jnp.pad(convw_full, ((0, 1), (0, 0)))
    gup_p = jnp.pad(gup_full, ((0, 128 - GLA_RANK), (0, 0))).astype(BF16)
    ln_in_g2, ln_in_b2 = ln_in_g.reshape(1, d), ln_in_b.reshape(1, d)

    s0, s0b = _ln_in_fwd(x, meta_full, ln_in_g2, ln_in_b2)
    tgt_p = jnp.pad(loss_target, ((0, 0), (HEAD, 0), (0, 0)))
    s0f, s0bf = s0.reshape(r, d), s0b.reshape(r, d)
    u = _in_proj(s0bf, w_int)
    u3 = u.reshape(bsz, tp, D_IN_PAD)
    hc = _conv_fwd(u3, convw_p, conv_b).reshape(r, D_CONV)
    gla_out, o_pre, states, (g_w1t,) = _gla_fwd(u3, gup_p, gate_bias, gla_norm_g, [shard_w1t])
    w1t = g_w1t.reshape(D_FF, d)
    gla_of = gla_out.reshape(r, 512)
    conv_of, xhat1, rstd1, s1b = _out_proj_ln1(hc, gla_of, wout, s0f, conv_ln_g, conv_ln_b, ln1_g, ln1_b)
    ra, (g_w2,) = _ffn1(s1b, w1t, [shard_w2])
    w2 = g_w2.reshape(D_FF, d)
    dt2, dt2b, acc2 = _ffn2_ln2_loss(ra, w2, xhat1, ln1_g, ln1_b, ln2_g, ln2_b, tgt_p.reshape(r, d), tp)

    def pair_reduce(gs):
        got = _pair_exchange(gs)
        return got, [_add_pair(g, t, c_arr) for g, t in zip(gs, got)]

    da = _ffn_bwd_da(dt2b, w2, ra)
    dt1, dt1b, acc1 = _ffn_bwd_ln1(da, w1t, dt2, xhat1, rstd1, ln1_g)
    g_w2 = _matmul_tn(ra, dt2b, 1024, square_lhs=True, name="grad_w_ff2")
    g_w1t = _matmul_tn(da, s1b, 1024, name="grad_w_ff1")
    big_ffn = [g_w1t.reshape(4, D_FF // 4, d), g_w2.reshape(4, D_FF // 4, d)]
    pair_got_ffn, parts_ffn = pair_reduce(big_ffn)
    dhc, dgla, cacc = _out_proj_bwd(dt1b, wout, hc, conv_ln_g, conv_ln_b)
    g_wout = jnp.concatenate([_matmul_tn(conv_of, dt1b, D_CONV, name="grad_w_out_conv"),
                              _matmul_tn(gla_of, dt1b, 512, name="grad_w_out_gla")], axis=0)
    dcv, dcg, dcw = _conv_bwd(dhc.reshape(bsz, tp, D_CONV), u3, convw_p)
    (dq, dk, dv, dr, dgd, dgup, gvec), chip_got_ffn = _gla_bwd(
        dgla.reshape(bsz, tp, 512), u3, o_pre, states, gup_p, gate_bias, gla_norm_g, parts_ffn)
    pieces = [a.reshape(r, a.shape[-1]) for a in (dcv, dcg, dq, dk, dv, dr)]
    ds0 = _in_proj_bwd(pieces, dgd, w_int, dt1, tp)
    g_wint = _grad_w_in(pieces, dgd, s0bf, tp)
    big_io = [g_wint[:D_IN].reshape(4, D_IN // 4, d), g_wout.reshape(4, d // 4, d)]
    pair_got_io, parts_io = pair_reduce(big_io)
    grad_x, dmeta, lvec, chip_got_io = _ln_in_bwd(ds0.reshape(bsz, tp, d), x, meta_full, ln_in_g2, parts_io)

    small = {
        "loss": acc2[0:1], "ln_in_g": jnp.sum(lvec[:, 0], axis=0), "ln_in_b": jnp.sum(lvec[:, 1], axis=0),
        "conv_b": cacc[0], "conv_ln_g": cacc[1], "conv_ln_b": cacc[2], "gate_bias": jnp.sum(gvec[:, 0], axis=0),
        "gla_norm_g": jnp.sum(gvec[:, 1].reshape(bsz * 2, 128), axis=0),
        "ln1_g": acc1[0], "ln1_b": acc1[1], "ln2_g": acc2[1], "ln2_b": acc2[2],
        "conv_w": jnp.sum(dcw, axis=0)[:CONV_WIDTH], "gate_up": jnp.sum(dgup, axis=0)[:GLA_RANK],
        "meta_tokens": jnp.sum(dmeta, axis=0),
    }
    pack = jnp.concatenate([_rows128(small[k]) for k, _ in _SMALL_FIELDS], axis=0)
    pack = jnp.pad(pack, ((0, SMALL_ROWS - pack.shape[0]), (0, 0)))
    red = _sum_small(_allgather_small(pack))
    off, tot = {}, 0
    for k, nrows in _SMALL_FIELDS:
        off[k] = (tot, nrows)
        tot += nrows

    def field(k, shape):
        o, nrows = off[k]
        return red[o:o + nrows].reshape(shape)

    loss = jnp.sum(field("loss", (d,)))

    halves = [_add_chips(g, t, s, pos_arr) for g, t, s in zip(
        big_io + big_ffn, list(pair_got_io) + list(pair_got_ffn), list(chip_got_io) + list(chip_got_ffn))]
    f_wint, f_wout, f_w1t, f_w2 = [
        jnp.concatenate([jnp.where(ci == 0, own, sib), jnp.where(ci == 0, sib, own)], axis=1)
        for own, sib in zip(halves, _pair_swap(halves))]

    def cols(a, width):
        return lax.dynamic_slice_in_dim(a, chip * width, width, axis=a.ndim - 1)

    grads = {
        "meta_tokens": cols(field("meta_tokens", (N_META, d)), 256),
        "ln_in_g": field("ln_in_g", (d,)), "ln_in_b": field("ln_in_b", (d,)),
        "w_in": f_wint.T[None], "conv_w": cols(field("conv_w", (CONV_WIDTH, D_CONV)), 128)[None],
        "conv_b": field("conv_b", (1, D_CONV)), "conv_ln_g": field("conv_ln_g", (1, D_CONV)),
        "conv_ln_b": field("conv_ln_b", (1, D_CONV)),
        "gate_up": cols(field("gate_up", (GLA_RANK, 256)), 64)[None],
        "gate_bias": field("gate_bias", (1, 256)), "gla_norm_g": field("gla_norm_g", (1, 128)),
        "w_out": f_wout[None], "ln1_g": field("ln1_g", (1, d)), "ln1_b": field("ln1_b", (1, d)),
        "w_ff1": f_w1t.T[None], "w_ff2": f_w2[None],
        "ln2_g": field("ln2_g", (1, d)), "ln2_b": field("ln2_b", (1, d)),
    }
    weights = dict(meta_tokens=meta_tokens, ln_in_g=ln_in_g, ln_in_b=ln_in_b, w_in=w_in, conv_w=conv_w, conv_b=conv_b,
                   conv_ln_g=conv_ln_g, conv_ln_b=conv_ln_b, gate_up=gate_up, gate_bias=gate_bias,
                   gla_norm_g=gla_norm_g, w_out=w_out, ln1_g=ln1_g, ln1_b=ln1_b, w_ff1=w_ff1, w_ff2=w_ff2,
                   ln2_g=ln2_g, ln2_b=ln2_b)
    moms = dict(meta_tokens=(m_meta_tokens, v_meta_tokens), ln_in_g=(m_ln_in_g, v_ln_in_g),
                ln_in_b=(m_ln_in_b, v_ln_in_b), w_in=(m_w_in, v_w_in), conv_w=(m_conv_w, v_conv_w),
                conv_b=(m_conv_b, v_conv_b), conv_ln_g=(m_conv_ln_g, v_conv_ln_g),
                conv_ln_b=(m_conv_ln_b, v_conv_ln_b), gate_up=(m_gate_up, v_gate_up),
                gate_bias=(m_gate_bias, v_gate_bias), gla_norm_g=(m_gla_norm_g, v_gla_norm_g),
                w_out=(m_w_out, v_w_out), ln1_g=(m_ln1_g, v_ln1_g), ln1_b=(m_ln1_b, v_ln1_b),
                w_ff1=(m_w_ff1, v_w_ff1), w_ff2=(m_w_ff2, v_w_ff2), ln2_g=(m_ln2_g, v_ln2_g),
                ln2_b=(m_ln2_b, v_ln2_b))
    names = list(weights)
    big_names = ("w_in", "w_out", "w_ff1", "w_ff2")
    delta, new_m, new_v = {}, {}, {}
    dl, mn, vn = _adamw(w_in[0].T, f_wint, m_w_in[0].T, v_w_in[0].T)
    delta["w_in"], new_m["w_in"], new_v["w_in"] = dl.T[None], mn.T[None], vn.T[None]
    for k in big_names[1:]:
        shp = weights[k].shape
        two = lambda a: a.reshape(shp[-2], shp[-1])
        dl, mn, vn = _adamw(two(weights[k]), two(grads[k]), two(moms[k][0]), two(moms[k][1]))
        delta[k], new_m[k], new_v[k] = dl.reshape(shp), mn.reshape(shp), vn.reshape(shp)
    small_names = [k for k in names if k not in big_names]
    sizes = [weights[k].size for k in small_names]
    total = sum(sizes)
    padded = -(-total // 1024) * 1024

    def flat(get):
        v = jnp.concatenate([get(k).reshape(-1) for k in small_names])
        return jnp.pad(v, (0, padded - total)).reshape(padded // 128, 128)

    ones = jnp.ones((padded - total,), F32)
    vflat = jnp.concatenate([jnp.concatenate([moms[k][1].reshape(-1) for k in small_names]), ones])
    dl, mn, vn = _adamw(flat(lambda k: weights[k]), flat(lambda k: grads[k]), flat(lambda k: moms[k][0]),
                        vflat.reshape(padded // 128, 128))
    pos = 0
    for k, sz in zip(small_names, sizes):
        shp = weights[k].shape
        delta[k] = dl.reshape(-1)[pos:pos + sz].reshape(shp)
        new_m[k] = mn.reshape(-1)[pos:pos + sz].reshape(shp)
        new_v[k] = vn.reshape(-1)[pos:pos + sz].reshape(shp)
        pos += sz

    return (loss, grad_x, *[grads[k] for k in names], *[delta[k] for k in names],
            *[new_m[k] for k in names], *[new_v[k] for k in names])
```

```python
import functools

import jax
import jax.numpy as jnp
from jax import lax
from jax.experimental import pallas as pl
from jax.experimental.pallas import tpu as pltpu

F32 = jnp.float32
BF16 = jnp.bfloat16

D_MODEL = 1024
N_META = 16
D_CONV = 512
CONV_WIDTH = 31
GLA_HEADS = 4
GLA_DV = 128
GLA_DK = 64
GLA_RANK = 16
GLA_TAU = 16.0
CHUNK = 64
D_FF = 4096
LN_EPS = 1e-5
ALPHA = 2.0 ** 0.25
D_IN = 2576
D_IN_PAD = 2688
PAD = CHUNK - N_META
HEAD = PAD + N_META
Q_SCALE = GLA_DK ** -0.5
ADAM_LR, ADAM_B1, ADAM_B2, ADAM_EPS, ADAM_WD, ADAM_STEP = 0.001, 0.9, 0.999, 1e-08, 0.01, 10
HALF = D_MODEL // 2
VMEM_LIMIT = 56 * 1024 * 1024
MESH = pl.DeviceIdType.MESH

C_VAL, C_GATE, C_Q, C_K, C_V, C_R, C_GD = 0, 512, 1024, 1280, 1536, 2048, 2560

_SMALL_FIELDS = (("loss", 8), ("ln_in_g", 8), ("ln_in_b", 8), ("conv_b", 4), ("conv_ln_g", 4), ("conv_ln_b", 4),
                 ("gate_bias", 2), ("gla_norm_g", 1), ("ln1_g", 8), ("ln1_b", 8), ("ln2_g", 8), ("ln2_b", 8),
                 ("conv_w", 124), ("gate_up", 32), ("meta_tokens", 128))
SMALL_ROWS = 360


def _params(sem=None, **kw):
    return pltpu.CompilerParams(dimension_semantics=sem, vmem_limit_bytes=VMEM_LIMIT, **kw)


def _row_tile(tp):
    for t in (704, 352, 192, 64):
        if tp % t == 0:
            return t
    raise ValueError(tp)


def _sub_rows(tm):
    ts = tm // 4 if tm % 64 == 0 else tm
    return [slice(r0, r0 + ts) for r0 in range(0, tm, ts)]


def _dot(a, b, dims, precision=None):
    return lax.dot_general(a, b, (dims, ((), ())), preferred_element_type=F32, precision=precision)


def _nn(a, b, **kw):
    return _dot(a, b, ((1,), (0,)), **kw)


def _nt(a, b, **kw):
    return _dot(a, b, ((1,), (1,)), **kw)


def _tn(a, b, **kw):
    return _dot(a, b, ((0,), (0,)), **kw)


def _sigmoid(x):
    return 1.0 / (1.0 + jnp.exp(-x))


def _log_sigmoid(z):
    return jnp.minimum(z, 0.0) - jnp.log(1.0 + jnp.exp(-jnp.abs(z)))


def _ln_stats(t):
    mu = jnp.mean(t, axis=-1, keepdims=True)
    d = t - mu
    var = jnp.mean(d * d, axis=-1, keepdims=True)
    rstd = lax.rsqrt(var + LN_EPS)
    return d * rstd, rstd


def _ln_bwd(dxhat, xhat, rstd):
    m1 = jnp.mean(dxhat, axis=-1, keepdims=True)
    m2 = jnp.mean(dxhat * xhat, axis=-1, keepdims=True)
    return rstd * (dxhat - m1 - xhat * m2)


def _mesh_pos():
    return lax.axis_index("x"), lax.axis_index("y"), lax.axis_index("c")


ANY = pl.BlockSpec(memory_space=pl.ANY)


def _gather_sems(n):
    return [pltpu.SemaphoreType.DMA((n, 3))] * 4


def _gather_steps(ins, outs, sems):
    n = len(ins)
    send, recv, fsend, frecv = sems
    half = [ref.shape[0] // 2 for ref in ins]
    x, y, c = _mesh_pos()
    me = 2 * x + y
    sibling = (x, y, 1 - c)
    chips = [(1 - x, y), (x, 1 - y), (1 - x, 1 - y)]
    chip_idx = [2 * px + py for px, py in chips]
    mine = [pl.ds(pl.multiple_of(c * h, 16), h) for h in half]
    other = [pl.ds(pl.multiple_of((1 - c) * h, 16), h) for h in half]
    pairs = [(a, k) for a in range(n) for k in range(3)]

    def ici(a, k, slab):
        return pltpu.make_async_remote_copy(
            src_ref=ins[a].at[mine[a], :], dst_ref=outs[a].at[slab, mine[a], :],
            send_sem=send.at[a, k], recv_sem=recv.at[a, k], device_id=(*chips[k], c), device_id_type=MESH)

    def forward(a, k, rows):
        blk = outs[a].at[chip_idx[k], rows[a], :]
        return pltpu.make_async_remote_copy(
            src_ref=blk, dst_ref=blk, send_sem=fsend.at[a, k], recv_sem=frecv.at[a, k],
            device_id=sibling, device_id_type=MESH)

    def start():
        for a, k in pairs:
            ici(a, k, me).start()

    def relay():
        for a, k in pairs:
            ici(a, k, chip_idx[k]).wait_recv()
            forward(a, k, mine).start()

    def finish():
        for a, k in pairs:
            forward(a, k, other).wait_recv()
        for a, k in pairs:
            ici(a, k, me).wait_send()
            forward(a, k, mine).wait_send()

    return start, relay, finish


def _place_own(gathered, shards):
    chip = 2 * lax.axis_index("x") + lax.axis_index("y")
    return [lax.dynamic_update_slice(g, s[None], (chip, 0, 0)) for g, s in zip(gathered, shards)]


def _gather_weights(shards, small):
    n = len(shards)

    def body(*refs):
        ins, small_in = refs[:n], refs[n]
        outs, small_out = refs[n + 1:2 * n + 1], refs[2 * n + 1]
        ssend, srecv = refs[2 * n + 6:]
        start, relay, finish = _gather_steps(ins, outs, refs[2 * n + 2:2 * n + 6])
        x, y, c = _mesh_pos()
        chips = [(1 - x, y), (x, 1 - y), (1 - x, 1 - y)]

        def small_copy(k, slot):
            return pltpu.make_async_remote_copy(
                src_ref=small_in, dst_ref=small_out.at[slot], send_sem=ssend.at[k], recv_sem=srecv.at[k],
                device_id=(*chips[k], c), device_id_type=MESH)

        start()
        for k in range(3):
            small_copy(k, 2 * x + y).start()
        relay()
        finish()
        for k, (px, py) in enumerate(chips):
            small_copy(k, 2 * px + py).wait_recv()
        for k in range(3):
            small_copy(k, 2 * x + y).wait_send()

    out_shape = [jax.ShapeDtypeStruct((4,) + s.shape, s.dtype) for s in shards]
    out_shape.append(jax.ShapeDtypeStruct((4,) + small.shape, small.dtype))
    res = pl.pallas_call(
        body, name="gather_weights", out_shape=out_shape,
        in_specs=[ANY] * (n + 1), out_specs=[ANY] * (n + 1),
        scratch_shapes=_gather_sems(n) + [pltpu.SemaphoreType.DMA((3,)), pltpu.SemaphoreType.DMA((3,))],
    )(*shards, small)
    res = _place_own(res, list(shards) + [small])
    return res[:n], res[n]


def _pair_exchange(grads):
    n = len(grads)

    def body(*refs):
        ins, outs = refs[:n], refs[n:2 * n]
        send, recv = refs[2 * n:]
        x, y, c = _mesh_pos()
        other = pl.ds(pl.multiple_of((1 - c) * HALF, 128), HALF)
        cps = [pltpu.make_async_remote_copy(
            src_ref=ins[a].at[:, :, other], dst_ref=outs[a], send_sem=send.at[a], recv_sem=recv.at[a],
            device_id=(x, y, 1 - c), device_id_type=MESH) for a in range(n)]
        for cp in cps:
            cp.start()
        for cp in cps:
            cp.wait()

    return pl.pallas_call(
        body, name="grad_pair_exchange",
        out_shape=[jax.ShapeDtypeStruct(g.shape[:2] + (HALF,), g.dtype) for g in grads],
        in_specs=[ANY] * n, out_specs=[ANY] * n,
        scratch_shapes=[pltpu.SemaphoreType.DMA((n,)), pltpu.SemaphoreType.DMA((n,))],
    )(*grads)


def _chip_exchange_sems(n):
    return [pltpu.SemaphoreType.DMA((n, 3))] * 2


def _chip_exchange_shapes(parts):
    return [jax.ShapeDtypeStruct((3,) + p.shape[1:], p.dtype) for p in parts]


def _chip_exchange_steps(ins, outs, sems):
    send, recv = sems
    x, y, c = _mesh_pos()
    chips = [(1 - x, y), (x, 1 - y), (1 - x, 1 - y)]
    cps = [pltpu.make_async_remote_copy(
        src_ref=ins[a].at[2 * px + py], dst_ref=outs[a].at[k], send_sem=send.at[a, k], recv_sem=recv.at[a, k],
        device_id=(px, py, c), device_id_type=MESH) for a in range(len(ins)) for k, (px, py) in enumerate(chips)]

    def start():
        for cp in cps:
            cp.start()

    def finish():
        for cp in cps:
            cp.wait()

    return start, finish


def _chip_exchange(parts):
    n = len(parts)

    def body(*refs):
        start, finish = _chip_exchange_steps(refs[:n], refs[n:2 * n], refs[2 * n:])
        start()
        finish()

    return pl.pallas_call(
        body, name="grad_chip_exchange", out_shape=_chip_exchange_shapes(parts),
        in_specs=[ANY] * n, out_specs=[ANY] * n, scratch_shapes=_chip_exchange_sems(n),
    )(*parts)


def _pair_swap(halves):
    n = len(halves)

    def body(*refs):
        ins, outs = refs[:n], refs[n:2 * n]
        send, recv = refs[2 * n:]
        x, y, c = _mesh_pos()
        cps = [pltpu.make_async_remote_copy(
            src_ref=ins[a], dst_ref=outs[a], send_sem=send.at[a], recv_sem=recv.at[a],
            device_id=(x, y, 1 - c), device_id_type=MESH) for a in range(n)]
        for cp in cps:
            cp.start()
        for cp in cps:
            cp.wait()

    return pl.pallas_call(
        body, name="grad_pair_swap",
        out_shape=[jax.ShapeDtypeStruct(h.shape, h.dtype) for h in halves],
        in_specs=[ANY] * n, out_specs=[ANY] * n,
        scratch_shapes=[pltpu.SemaphoreType.DMA((n,)), pltpu.SemaphoreType.DMA((n,))],
    )(*halves)


def _allgather_small(pack):
    m_per, ncol = pack.shape

    def body(x_ref, out_ref, send_sems, recv_sems, local_sem):
        x, y, c = _mesh_pos()
        me, sibling = (x, y, c), (x, y, 1 - c)
        chips = [(1 - x, y), (x, 1 - y), (1 - x, 1 - y)]

        def rows(px, py, pc):
            return out_ref.at[pl.ds(pl.multiple_of((4 * px + 2 * py + pc) * m_per, 8), m_per), :]

        def copy(k, block, to, src=None):
            return pltpu.make_async_remote_copy(
                src_ref=rows(*block) if src is None else src, dst_ref=rows(*block),
                send_sem=send_sems.at[k], recv_sem=recv_sems.at[k], device_id=to, device_id_type=MESH)

        mine = pltpu.make_async_copy(x_ref, rows(*me), local_sem)
        mine.start()
        first = [copy(0, me, sibling, src=x_ref)]
        first += [copy(1 + j, me, (*chip, c), src=x_ref) for j, chip in enumerate(chips)]
        for cp in first:
            cp.start()
        passed = [copy(4 + j, (*chip, c), sibling) for j, chip in enumerate(chips)]
        for j, chip in enumerate(chips):
            copy(1 + j, (*chip, c), me).wait_recv()
            passed[j].start()
        copy(0, sibling, me).wait_recv()
        for j, chip in enumerate(chips):
            copy(4 + j, (*chip, 1 - c), me).wait_recv()
        for cp in first + passed:
            cp.wait_send()
        mine.wait()

    return pl.pallas_call(
        body, name="allgather_small",
        out_shape=jax.ShapeDtypeStruct((8 * m_per, ncol), pack.dtype),
        in_specs=[pl.BlockSpec(memory_space=pltpu.VMEM)],
        out_specs=pl.BlockSpec(memory_space=pltpu.VMEM),
        scratch_shapes=[pltpu.SemaphoreType.DMA((7,)), pltpu.SemaphoreType.DMA((7,)), pltpu.SemaphoreType.DMA],
    )(pack)


def _add_pair(g, got, c_arr):
    _, rows, _ = g.shape

    def body(c_ref, g_ref, r_ref, o_ref):
        o_ref[...] = (g_ref[...] + r_ref[...]).astype(BF16)

    return pl.pallas_call(
        body, name="grad_add_pair", out_shape=jax.ShapeDtypeStruct((4, rows, HALF), BF16),
        grid_spec=pltpu.PrefetchScalarGridSpec(
            num_scalar_prefetch=1, grid=(4,),
            in_specs=[pl.BlockSpec((1, rows, HALF), lambda j, c: (j, 0, c[0])),
                      pl.BlockSpec((1, rows, HALF), lambda j, c: (j, 0, 0))],
            out_specs=pl.BlockSpec((1, rows, HALF), lambda j, c: (j, 0, 0))),
        compiler_params=_params(("arbitrary",)),
    )(c_arr, g, got)


def _add_chips(g, pair_got, chip_got, pos_arr):
    _, rows, _ = g.shape

    def body(pos_ref, g_ref, p_ref, r_ref, o_ref):
        own = g_ref[0] + p_ref[0]
        o_ref[...] = ((own + r_ref[0].astype(F32)) + r_ref[1].astype(F32)) + r_ref[2].astype(F32)

    return pl.pallas_call(
        body, name="grad_add_chips", out_shape=jax.ShapeDtypeStruct((rows, HALF), F32),
        grid_spec=pltpu.PrefetchScalarGridSpec(
            num_scalar_prefetch=1, grid=(1,),
            in_specs=[pl.BlockSpec((1, rows, HALF), lambda i, p: (p[0], 0, p[1])),
                      pl.BlockSpec((1, rows, HALF), lambda i, p: (p[0], 0, 0)),
                      pl.BlockSpec((3, rows, HALF), lambda i, p: (0, 0, 0))],
            out_specs=pl.BlockSpec((rows, HALF), lambda i, p: (0, 0))),
        compiler_params=_params(("arbitrary",)),
    )(pos_arr, g, pair_got, chip_got)


def _sum_small(gathered):
    def body(g_ref, o_ref):
        acc = g_ref[0:SMALL_ROWS, :]
        for d in range(1, 8):
            acc = acc + g_ref[d * SMALL_ROWS:(d + 1) * SMALL_ROWS, :]
        o_ref[...] = acc

    return pl.pallas_call(
        body, name="sum_small", out_shape=jax.ShapeDtypeStruct((SMALL_ROWS, 128), F32),
    )(gathered)


def _adamw(w, g, m, v):
    rows, cols = w.shape
    tr = 256 if rows % 256 == 0 else rows
    c1 = 1.0 - ADAM_B1 ** ADAM_STEP
    c2 = 1.0 - ADAM_B2 ** ADAM_STEP

    def body(w_ref, g_ref, m_ref, v_ref, d_ref, mo_ref, vo_ref):
        gg = g_ref[...]
        mn = ADAM_B1 * m_ref[...] + (1.0 - ADAM_B1) * gg
        vn = ADAM_B2 * v_ref[...] + (1.0 - ADAM_B2) * (gg * gg)
        d_ref[...] = -ADAM_LR * ((mn / c1) / (jnp.sqrt(vn / c2) + ADAM_EPS) + ADAM_WD * w_ref[...])
        mo_ref[...] = mn
        vo_ref[...] = vn

    spec = pl.BlockSpec((tr, cols), lambda i: (i, 0))
    return pl.pallas_call(
        body, name="adamw", out_shape=[jax.ShapeDtypeStruct(w.shape, F32)] * 3,
        grid=(rows // tr,), in_specs=[spec] * 4, out_specs=[spec] * 3,
        compiler_params=_params(("parallel",)),
    )(w, g, m, v)


def _ln_in_fwd(x, meta, g, b):
    bsz, s, d = x.shape
    tp = s + HEAD
    nh = 2
    sh = s // nh
    rc = min(256, sh)

    def body(x_ref, meta_ref, g_ref, b_ref, s0_ref, s0b_ref):
        h = pl.program_id(1)
        gg, bb = g_ref[...], b_ref[...]

        @pl.when(h == 0)
        def _():
            s0_ref[0, 0:PAD, :] = jnp.zeros((PAD, d), F32)
            s0b_ref[0, 0:PAD, :] = jnp.zeros((PAD, d), BF16)
            mh, _ = _ln_stats(meta_ref[...])
            mv = mh * gg + bb
            s0_ref[0, PAD:HEAD, :] = mv
            s0b_ref[0, PAD:HEAD, :] = mv.astype(BF16)

        def step(i, carry):
            src = pl.ds(pl.multiple_of(i * rc, rc), rc)
            dst = pl.ds(pl.multiple_of(HEAD + h * sh + i * rc, 64), rc)
            xh, _ = _ln_stats(x_ref[0, src, :])
            val = xh * gg + bb
            s0_ref[0, dst, :] = val
            s0b_ref[0, dst, :] = val.astype(BF16)
            return carry

        lax.fori_loop(0, sh // rc, step, 0)

    full = lambda bi, hi: (bi, 0, 0)
    return pl.pallas_call(
        body, name="ln_in_fwd",
        out_shape=[jax.ShapeDtypeStruct((bsz, tp, d), F32), jax.ShapeDtypeStruct((bsz, tp, d), BF16)],
        grid=(bsz, nh),
        in_specs=[pl.BlockSpec((1, sh, d), lambda bi, hi: (bi, hi, 0)),
                  pl.BlockSpec((N_META, d), lambda bi, hi: (0, 0)),
                  pl.BlockSpec((1, d), lambda bi, hi: (0, 0)),
                  pl.BlockSpec((1, d), lambda bi, hi: (0, 0))],
        out_specs=[pl.BlockSpec((1, tp, d), full)] * 2,
        compiler_params=_params(("parallel", "arbitrary")),
    )(x, meta, g, b)


def _in_proj(s0b, w_int):
    r, d = s0b.shape
    tm = _row_tile(r)

    def body(a_ref, w_ref, o_ref):
        o_ref[...] = _nt(a_ref[...], w_ref[...])

    return pl.pallas_call(
        body, name="in_proj", out_shape=jax.ShapeDtypeStruct((r, D_IN_PAD), F32),
        grid=(r // tm,),
        in_specs=[pl.BlockSpec((tm, d), lambda i: (i, 0)), pl.BlockSpec((D_IN_PAD, d), lambda i: (0, 0))],
        out_specs=pl.BlockSpec((tm, D_IN_PAD), lambda i: (i, 0)),
        compiler_params=_params(("parallel",)),
    )(s0b, w_int)


def _conv_fwd(u, conv_w, conv_b):
    bsz, tp, _ = u.shape
    nchunk = tp // CHUNK
    win = CHUNK + 32
    nct = D_CONV // 128

    def body(cv_ref, cg_ref, w_ref, cb_ref, hc_ref, h_scr, win_scr):
        h_scr[0:32, :] = jnp.zeros((32, 128), F32)
        h_scr[32:32 + tp, :] = cv_ref[0] * _sigmoid(cg_ref[0])
        cb = cb_ref[...]

        def step(n, carry):
            r0 = pl.multiple_of(n * CHUNK, CHUNK)
            win_scr[...] = h_scr[pl.ds(r0, win), :]
            acc = jnp.zeros((CHUNK, 128), F32)
            for j in range(CONV_WIDTH):
                acc = acc + w_ref[j:j + 1, :] * win_scr[2 + j:2 + j + CHUNK, :]
            hc_ref[0, pl.ds(r0, CHUNK), :] = acc + cb
            return carry

        lax.fori_loop(0, nchunk, step, 0)

    return pl.pallas_call(
        body, name="conv_fwd", out_shape=jax.ShapeDtypeStruct((bsz, tp, D_CONV), F32),
        grid=(bsz, nct),
        in_specs=[pl.BlockSpec((1, tp, 128), lambda bi, t: (bi, 0, C_VAL // 128 + t)),
                  pl.BlockSpec((1, tp, 128), lambda bi, t: (bi, 0, C_GATE // 128 + t)),
                  pl.BlockSpec((32, 128), lambda bi, t: (0, t)),
                  pl.BlockSpec((1, 128), lambda bi, t: (0, t))],
        out_specs=pl.BlockSpec((1, tp, 128), lambda bi, t: (bi, 0, t)),
        scratch_shapes=[pltpu.VMEM((tp + 32, 128), F32), pltpu.VMEM((win, 128), F32)],
        compiler_params=_params(("parallel", "parallel")),
    )(u, u, conv_w, conv_b)


def _chunk_loop(nchunk, step, init):
    unroll = 3 if nchunk % 3 == 0 else 1

    def trip(t, carry):
        for j in range(unroll):
            carry = step(t * unroll + j, carry)
        return carry

    return lax.fori_loop(0, nchunk // unroll, trip, init)


def _gla_consts():
    row = lax.broadcasted_iota(jnp.int32, (CHUNK, CHUNK), 0)
    col = lax.broadcasted_iota(jnp.int32, (CHUNK, CHUNK), 1)
    lane = lax.broadcasted_iota(jnp.int32, (1, 128), 1)
    return row >= col, [lane < GLA_DK, lane >= GLA_DK]


def _gla_triu():
    row = lax.broadcasted_iota(jnp.int32, (CHUNK, CHUNK), 0)
    col = lax.broadcasted_iota(jnp.int32, (CHUNK, CHUNK), 1)
    return (row <= col).astype(F32)


def _gla_chunk_terms(n, q_ref, k_ref, gd_ref, gup_ref, gb_ref, tri):
    r0 = pl.multiple_of(n * CHUNK, CHUNK)
    rows = pl.ds(r0, CHUNK)
    z = _nn(gd_ref[0, rows, :].astype(BF16), gup_ref[...]) + gb_ref[...]
    rowid = n * CHUNK + lax.broadcasted_iota(jnp.int32, (CHUNK, 1), 0)
    valid = rowid >= PAD
    lg = jnp.where(valid, _log_sigmoid(z) * (1.0 / GLA_TAU), 0.0)
    bcum = _nn(tri.astype(F32), lg, precision=lax.Precision.HIGHEST)
    blast = bcum[CHUNK - 1:CHUNK, :]
    eb = jnp.exp(bcum)
    enb = jnp.exp(-bcum)
    erest = jnp.exp(blast - bcum)
    q = q_ref[0, rows, :] * Q_SCALE
    k = k_ref[0, rows, :]
    return rows, valid, z, eb, enb, erest, jnp.exp(blast), q * eb, k * enb, k * erest


def _grid_ends(grid):
    ids = [pl.program_id(i) for i in range(len(grid))]
    first = functools.reduce(jnp.logical_and, [i == 0 for i in ids])
    last = functools.reduce(jnp.logical_and, [i == g - 1 for i, g in zip(ids, grid)])
    return first, last


def _gla_fwd(u, gup, gbias, gnorm, shards):
    bsz, tp, _ = u.shape
    nchunk = tp // CHUNK
    ns = len(shards)
    grid = (bsz, 2)

    def body(*refs):
        q_ref, k_ref, v_ref, r_ref, gd_ref, gup_ref, gb_ref, gn_ref = refs[:8]
        out_ref, o_ref, st_ref = refs[8 + ns:11 + ns]
        s_scr = refs[11 + 2 * ns]
        start, relay, finish = _gather_steps(refs[8:8 + ns], refs[11 + ns:11 + 2 * ns], refs[12 + 2 * ns:])
        first, last = _grid_ends(grid)
        pl.when(first)(start)
        pl.when(last)(relay)
        tri, hmask = _gla_consts()
        s_scr[...] = jnp.zeros_like(s_scr)
        gn = gn_ref[...]

        def step(n, carry):
            rows, _, _, _, _, _, dec, qe, ke, kd = _gla_chunk_terms(n, q_ref, k_ref, gd_ref, gup_ref, gb_ref, tri)
            keb, kdb = ke.astype(BF16), kd.astype(BF16)
            for h in range(2):
                cols = slice(h * GLA_DV, (h + 1) * GLA_DV)
                qh = jnp.where(hmask[h], qe, 0.0).astype(BF16)
                vh = v_ref[0, rows, cols].astype(BF16)
                a = jnp.where(tri, _nt(qh, keb), 0.0)
                st = s_scr[h]
                st_ref[0, h, n] = st
                o = _nn(a.astype(BF16), vh) + _nt(qh, st.astype(BF16))
                s_scr[h] = dec * st + _tn(vh, kdb)
                o_ref[0, rows, cols] = o
                rms = lax.rsqrt(jnp.mean(o * o, axis=-1, keepdims=True) + LN_EPS)
                rh = r_ref[0, rows, cols]
                out_ref[0, rows, cols] = (o * rms * gn * (rh * _sigmoid(rh))).astype(BF16)
            return carry

        _chunk_loop(nchunk, step, 0)
        pl.when(last)(finish)

    res = pl.pallas_call(
        body, name="gla_fwd",
        out_shape=[jax.ShapeDtypeStruct((bsz, tp, 512), BF16), jax.ShapeDtypeStruct((bsz, tp, 512), F32),
                   jax.ShapeDtypeStruct((bsz, GLA_HEADS, nchunk, GLA_DV, 128), F32)]
        + [jax.ShapeDtypeStruct((4,) + s.shape, s.dtype) for s in shards],
        grid=grid,
        in_specs=[pl.BlockSpec((1, tp, 128), lambda bi, p: (bi, 0, C_Q // 128 + p)),
                  pl.BlockSpec((1, tp, 128), lambda bi, p: (bi, 0, C_K // 128 + p)),
                  pl.BlockSpec((1, tp, 256), lambda bi, p: (bi, 0, C_V // 256 + p)),
                  pl.BlockSpec((1, tp, 256), lambda bi, p: (bi, 0, C_R // 256 + p)),
                  pl.BlockSpec((1, tp, 128), lambda bi, p: (bi, 0, C_GD // 128)),
                  pl.BlockSpec((128, 128), lambda bi, p: (0, p)),
                  pl.BlockSpec((1, 128), lambda bi, p: (0, p)),
                  pl.BlockSpec((1, 128), lambda bi, p: (0, 0))] + [ANY] * ns,
        out_specs=[pl.BlockSpec((1, tp, 256), lambda bi, p: (bi, 0, p)),
                   pl.BlockSpec((1, tp, 256), lambda bi, p: (bi, 0, p)),
                   pl.BlockSpec((1, 2, nchunk, GLA_DV, 128), lambda bi, p: (bi, p, 0, 0, 0))] + [ANY] * ns,
        scratch_shapes=[pltpu.VMEM((2, GLA_DV, 128), F32)] + _gather_sems(ns),
        compiler_params=_params(("arbitrary", "arbitrary")),
    )(u, u, u, u, u, gup, gbias, gnorm, *shards)
    return res[0], res[1], res[2], _place_own(res[3:], shards)


def _out_proj_ln1(hc, gla_out, w_out, s0, cg, cb, g1, b1):
    r, d = s0.shape
    tm = _row_tile(r)

    def body(hc_ref, a_ref, w_ref, s0_ref, cg_ref, cb_ref, g_ref, b_ref, co_ref, xh_ref, rstd_ref, s1b_ref):
        for rs in _sub_rows(tm):
            xc, _ = _ln_stats(hc_ref[rs, :])
            nv = xc * cg_ref[...] + cb_ref[...]
            co = (nv * _sigmoid(nv)).astype(BF16)
            co_ref[rs, :] = co
            mix = _nn(co, w_ref[0:D_CONV, :]) + _nn(a_ref[rs, :], w_ref[D_CONV:, :])
            xh, rstd = _ln_stats(ALPHA * s0_ref[rs, :] + mix)
            xh_ref[rs, :] = xh
            rstd_ref[rs, :] = rstd
            s1b_ref[rs, :] = (xh * g_ref[...] + b_ref[...]).astype(BF16)

    row = lambda n: pl.BlockSpec((tm, n), lambda i: (i, 0))
    vec = lambda n: pl.BlockSpec((1, n), lambda i: (0, 0))
    return pl.pallas_call(
        body, name="out_proj_ln1",
        out_shape=[jax.ShapeDtypeStruct((r, D_CONV), BF16), jax.ShapeDtypeStruct((r, d), F32),
                   jax.ShapeDtypeStruct((r, 1), F32), jax.ShapeDtypeStruct((r, d), BF16)],
        grid=(r // tm,),
        in_specs=[row(D_CONV), row(512), pl.BlockSpec((d, d), lambda i: (0, 0)), row(d),
                  vec(D_CONV), vec(D_CONV), vec(d), vec(d)],
        out_specs=[row(D_CONV), row(d), row(1), row(d)],
        compiler_params=_params(("parallel",)),
    )(hc, gla_out, w_out, s0, cg, cb, g1, b1)


def _ffn1(s1b, w1t, shards):
    r, d = s1b.shape
    tm = _row_tile(r)
    ns = len(shards)
    grid = (r // tm,)

    def body(*refs):
        a_ref, w_ref = refs[:2]
        o_ref = refs[2 + ns]
        start, relay, finish = _gather_steps(refs[2:2 + ns], refs[3 + ns:3 + 2 * ns], refs[3 + 2 * ns:])
        first, last = _grid_ends(grid)
        pl.when(first)(start)
        pl.when(last)(relay)
        o_ref[...] = jnp.maximum(_nt(a_ref[...], w_ref[...]), 0.0).astype(BF16)
        pl.when(last)(finish)

    res = pl.pallas_call(
        body, name="ffn1",
        out_shape=[jax.ShapeDtypeStruct((r, D_FF), BF16)] + [jax.ShapeDtypeStruct((4,) + s.shape, s.dtype) for s in shards],
        grid=grid,
        in_specs=[pl.BlockSpec((tm, d), lambda i: (i, 0)), pl.BlockSpec((D_FF, d), lambda i: (0, 0))] + [ANY] * ns,
        out_specs=[pl.BlockSpec((tm, D_FF), lambda i: (i, 0))] + [ANY] * ns,
        scratch_shapes=_gather_sems(ns),
        compiler_params=_params(("arbitrary",)),
    )(s1b, w1t, *shards)
    return res[0], _place_own(res[1:], shards)


def _ffn2_ln2_loss(ra, w2, xhat1, g1, b1, g2, b2, tgt, tp):
    r, d = xhat1.shape
    tm = _row_tile(tp)
    per = tp // tm

    def body(ra_ref, w_ref, xh1_ref, g1_ref, b1_ref, g2_ref, b2_ref, tgt_ref, dt_ref, dtb_ref, acc_ref, t_ref, sem):
        i = pl.program_id(0)
        b, j = i // per, i % per

        @pl.when(i == 0)
        def _():
            acc_ref[...] = jnp.zeros_like(acc_ref)

        head_copy = pltpu.make_async_copy(tgt_ref.at[b, pl.ds(0, tm - HEAD), :], t_ref.at[pl.ds(HEAD, tm - HEAD), :], sem)
        body_copy = pltpu.make_async_copy(
            tgt_ref.at[b, pl.ds(pl.multiple_of(jnp.maximum(j * tm - HEAD, 0), 64), tm), :], t_ref, sem)

        @pl.when(j == 0)
        def _():
            t_ref[0:HEAD, :] = jnp.zeros((HEAD, d), F32)
            head_copy.start()

        pl.when(j > 0)(body_copy.start)

        sums = [jnp.zeros((1, d), F32)] * 3
        for rs in _sub_rows(tm):
            rb = ra_ref[rs, :]
            f = _nn(rb * rb, w_ref[...])
            if rs.start == 0:
                pl.when(j == 0)(head_copy.wait)
                pl.when(j > 0)(body_copy.wait)
            s1 = xh1_ref[rs, :] * g1_ref[...] + b1_ref[...]
            xh2, rstd2 = _ln_stats(ALPHA * s1 + f)
            y = xh2 * g2_ref[...] + b2_ref[...]
            rowid = (i % per) * tm + rs.start + lax.broadcasted_iota(jnp.int32, (rs.stop - rs.start, 1), 0)
            e = jnp.where(rowid >= HEAD, y - t_ref[rs, :], 0.0)
            dy = e * (1.0 / d)
            dt2 = _ln_bwd(dy * g2_ref[...], xh2, rstd2)
            dt_ref[rs, :] = dt2
            dtb_ref[rs, :] = dt2.astype(BF16)
            sums = [sums[0] + (0.5 / d) * jnp.sum(e * e, axis=0, keepdims=True),
                    sums[1] + jnp.sum(dy * xh2, axis=0, keepdims=True), sums[2] + jnp.sum(dy, axis=0, keepdims=True)]
        for k in range(3):
            acc_ref[k:k + 1, :] += sums[k]

    row = lambda n: pl.BlockSpec((tm, n), lambda i: (i, 0))
    vec = pl.BlockSpec((1, d), lambda i: (0, 0))
    return pl.pallas_call(
        body, name="ffn2_ln2_loss",
        out_shape=[jax.ShapeDtypeStruct((r, d), F32), jax.ShapeDtypeStruct((r, d), BF16),
                   jax.ShapeDtypeStruct((8, d), F32)],
        grid=(r // tm,),
        in_specs=[row(D_FF), pl.BlockSpec((D_FF, d), lambda i: (0, 0)), row(d), vec, vec, vec, vec, ANY],
        out_specs=[row(d), row(d), pl.BlockSpec((8, d), lambda i: (0, 0))],
        scratch_shapes=[pltpu.VMEM((tm, d), F32), pltpu.SemaphoreType.DMA],
        compiler_params=_params(("arbitrary",)),
    )(ra, w2, xhat1, g1, b1, g2, b2, tgt)


def _ffn_bwd_da(dt2b, w2, ra):
    r, d = dt2b.shape
    tm = _row_tile(r)

    def body(g_ref, w_ref, ra_ref, o_ref):
        o_ref[...] = (_nt(g_ref[...], w_ref[...]) * (2.0 * ra_ref[...].astype(F32))).astype(BF16)

    return pl.pallas_call(
        body, name="ffn_bwd_da", out_shape=jax.ShapeDtypeStruct((r, D_FF), BF16),
        grid=(r // tm,),
        in_specs=[pl.BlockSpec((tm, d), lambda i: (i, 0)), pl.BlockSpec((D_FF, d), lambda i: (0, 0)),
                  pl.BlockSpec((tm, D_FF), lambda i: (i, 0))],
        out_specs=pl.BlockSpec((tm, D_FF), lambda i: (i, 0)),
        compiler_params=_params(("parallel",)),
    )(dt2b, w2, ra)


def _ffn_bwd_ln1(da, w1t, dt2, xhat1, rstd1, g1):
    r, d = dt2.shape
    tm = _row_tile(r)

    def body(da_ref, w_ref, dt2_ref, xh_ref, rstd_ref, g_ref, dt_ref, dtb_ref, acc_ref):
        @pl.when(pl.program_id(0) == 0)
        def _():
            acc_ref[...] = jnp.zeros_like(acc_ref)

        sums = [jnp.zeros((1, d), F32)] * 2
        for rs in _sub_rows(tm):
            ds1 = ALPHA * dt2_ref[rs, :] + _nn(da_ref[rs, :], w_ref[...])
            xh = xh_ref[rs, :]
            dt1 = _ln_bwd(ds1 * g_ref[...], xh, rstd_ref[rs, :])
            dt_ref[rs, :] = dt1
            dtb_ref[rs, :] = dt1.astype(BF16)
            sums = [sums[0] + jnp.sum(ds1 * xh, axis=0, keepdims=True), sums[1] + jnp.sum(ds1, axis=0, keepdims=True)]
        for k in range(2):
            acc_ref[k:k + 1, :] += sums[k]

    row = lambda n: pl.BlockSpec((tm, n), lambda i: (i, 0))
    return pl.pallas_call(
        body, name="ffn_bwd_ln1",
        out_shape=[jax.ShapeDtypeStruct((r, d), F32), jax.ShapeDtypeStruct((r, d), BF16),
                   jax.ShapeDtypeStruct((8, d), F32)],
        grid=(r // tm,),
        in_specs=[row(D_FF), pl.BlockSpec((D_FF, d), lambda i: (0, 0)), row(d), row(d), row(1),
                  pl.BlockSpec((1, d), lambda i: (0, 0))],
        out_specs=[row(d), row(d), pl.BlockSpec((8, d), lambda i: (0, 0))],
        compiler_params=_params(("arbitrary",)),
    )(da, w1t, dt2, xhat1, rstd1, g1)


def _matmul_tn(lhs, rhs, bm, square_lhs=False, name="matmul_tn"):
    r, m = lhs.shape
    n = rhs.shape[1]
    tk = _row_tile(r)

    def body(a_ref, b_ref, o_ref):
        @pl.when(pl.program_id(1) == 0)
        def _():
            o_ref[...] = jnp.zeros_like(o_ref)

        a = a_ref[...]
        if square_lhs:
            a = a * a
        o_ref[...] += _tn(a, b_ref[...])

    return pl.pallas_call(
        body, name=name, out_shape=jax.ShapeDtypeStruct((m, n), F32),
        grid=(m // bm, r // tk),
        in_specs=[pl.BlockSpec((tk, bm), lambda i, k: (k, i)), pl.BlockSpec((tk, n), lambda i, k: (k, 0))],
        out_specs=pl.BlockSpec((bm, n), lambda i, k: (i, 0)),
        compiler_params=_params(("parallel", "arbitrary")),
    )(lhs, rhs)


def _out_proj_bwd(dt1b, w_out, hc, cg, cb):
    r, d = dt1b.shape
    tm = _row_tile(r)

    def body(g_ref, w_ref, hc_ref, cg_ref, cb_ref, dhc_ref, dgla_ref, acc_ref):
        @pl.when(pl.program_id(0) == 0)
        def _():
            acc_ref[...] = jnp.zeros_like(acc_ref)

        gg = cg_ref[...]
        sums = [jnp.zeros((1, D_CONV), F32)] * 3
        for rs in _sub_rows(tm):
            dmix = _nt(g_ref[rs, :], w_ref[...])
            dgla_ref[rs, :] = dmix[:, D_CONV:]
            xh, rstd = _ln_stats(hc_ref[rs, :])
            nv = xh * gg + cb_ref[...]
            sig = _sigmoid(nv)
            dn = dmix[:, :D_CONV] * (sig * (1.0 + nv * (1.0 - sig)))
            dhc = _ln_bwd(dn * gg, xh, rstd)
            dhc_ref[rs, :] = dhc
            sums = [sums[0] + jnp.sum(dhc, axis=0, keepdims=True), sums[1] + jnp.sum(dn * xh, axis=0, keepdims=True),
                    sums[2] + jnp.sum(dn, axis=0, keepdims=True)]
        for k in range(3):
            acc_ref[k:k + 1, :] += sums[k]

    row = lambda n: pl.BlockSpec((tm, n), lambda i: (i, 0))
    vec = pl.BlockSpec((1, D_CONV), lambda i: (0, 0))
    return pl.pallas_call(
        body, name="out_proj_bwd",
        out_shape=[jax.ShapeDtypeStruct((r, D_CONV), F32), jax.ShapeDtypeStruct((r, 512), F32),
                   jax.ShapeDtypeStruct((8, D_CONV), F32)],
        grid=(r // tm,),
        in_specs=[row(d), pl.BlockSpec((d, d), lambda i: (0, 0)), row(D_CONV), vec, vec],
        out_specs=[row(D_CONV), row(512), pl.BlockSpec((8, D_CONV), lambda i: (0, 0))],
        compiler_params=_params(("arbitrary",)),
    )(dt1b, w_out, hc, cg, cb)


def _conv_bwd(dhc, u, conv_w):
    bsz, tp, _ = u.shape
    nchunk = tp // CHUNK
    win = CHUNK + 32
    nct = D_CONV // 128

    def body(dhc_ref, cv_ref, cg_ref, w_ref, dv_ref, dg_ref, dw_ref, h_scr, dhc_scr, hwin, dwin, dw_scr):
        h_scr[0:32, :] = jnp.zeros((32, 128), F32)
        h_scr[32:32 + tp, :] = cv_ref[0] * _sigmoid(cg_ref[0])
        dhc_scr[0:tp, :] = dhc_ref[0]
        dhc_scr[tp:tp + 32, :] = jnp.zeros((32, 128), F32)
        dw_scr[...] = jnp.zeros_like(dw_scr)

        def step(n, carry):
            r0 = pl.multiple_of(n * CHUNK, CHUNK)
            rows = pl.ds(r0, CHUNK)
            hwin[...] = h_scr[pl.ds(r0, win), :]
            dwin[...] = dhc_scr[pl.ds(r0, win), :]
            dcur = dwin[0:CHUNK, :]
            acc = jnp.zeros((CHUNK, 128), F32)
            for j in range(CONV_WIDTH):
                acc = acc + w_ref[j:j + 1, :] * dwin[30 - j:30 - j + CHUNK, :]
                prod = dcur * hwin[2 + j:2 + j + CHUNK, :]
                dw_scr[j * 8:(j + 1) * 8, :] += jnp.sum(prod.reshape(CHUNK // 8, 8, 128), axis=0)
            cg = cg_ref[0, rows, :]
            sig = _sigmoid(cg)
            rowid = n * CHUNK + lax.broadcasted_iota(jnp.int32, (CHUNK, 1), 0)
            dh = jnp.where(rowid >= PAD, acc, 0.0)
            dv_ref[0, rows, :] = (dh * sig).astype(BF16)
            dg_ref[0, rows, :] = (dh * cv_ref[0, rows, :] * sig * (1.0 - sig)).astype(BF16)
            return carry

        lax.fori_loop(0, nchunk, step, 0)
        dw_ref[0] = jnp.zeros((32, 128), F32)
        for j in range(CONV_WIDTH):
            dw_ref[0, j:j + 1, :] = jnp.sum(dw_scr[j * 8:(j + 1) * 8, :], axis=0, keepdims=True)

    blk = lambda off: pl.BlockSpec((1, tp, 128), lambda bi, t: (bi, 0, off // 128 + t))
    return pl.pallas_call(
        body, name="conv_bwd",
        out_shape=[jax.ShapeDtypeStruct((bsz, tp, D_CONV), BF16), jax.ShapeDtypeStruct((bsz, tp, D_CONV), BF16),
                   jax.ShapeDtypeStruct((bsz, 32, D_CONV), F32)],
        grid=(bsz, nct),
        in_specs=[blk(0), blk(C_VAL), blk(C_GATE), pl.BlockSpec((32, 128), lambda bi, t: (0, t))],
        out_specs=[blk(0), blk(0), pl.BlockSpec((1, 32, 128), lambda bi, t: (bi, 0, t))],
        scratch_shapes=[pltpu.VMEM((tp + 32, 128), F32), pltpu.VMEM((tp + 32, 128), F32),
                        pltpu.VMEM((win, 128), F32), pltpu.VMEM((win, 128), F32),
                        pltpu.VMEM((CONV_WIDTH * 8, 128), F32)],
        compiler_params=_params(("parallel", "parallel")),
    )(dhc, u, u, conv_w)


def _gla_bwd(dgla, u, o_pre, states, gup, gbias, gnorm, parts):
    bsz, tp, _ = u.shape
    nchunk = tp // CHUNK
    ns = len(parts)
    grid = (bsz, 2)

    def body(*refs):
        dy_ref, q_ref, k_ref, v_ref, r_ref, gd_ref, o_ref, st_ref, gup_ref, gb_ref, gn_ref = refs[:11]
        dq_ref, dk_ref, dv_ref, dr_ref, dgd_ref, dgup_ref, vec_ref = refs[11 + ns:18 + ns]
        h_scr, gup_acc = refs[18 + 2 * ns:20 + 2 * ns]
        start, finish = _chip_exchange_steps(refs[11:11 + ns], refs[18 + ns:18 + 2 * ns], refs[20 + 2 * ns:])
        first, last = _grid_ends(grid)
        pl.when(first)(start)
        tri, hmask = _gla_consts()
        triu = _gla_triu()
        h_scr[...] = jnp.zeros_like(h_scr)
        gup_acc[...] = jnp.zeros_like(gup_acc)
        gn = gn_ref[...]
        gupb = gup_ref[...]

        def step(i, carry):
            dbias, dgn = carry
            n = nchunk - 1 - i
            rows, valid, z, eb, enb, erest, dec, qe, ke, kd = _gla_chunk_terms(
                n, q_ref, k_ref, gd_ref, gup_ref, gb_ref, tri)
            keb, kdb = ke.astype(BF16), kd.astype(BF16)
            dqe = jnp.zeros((CHUNK, 128), F32)
            dke = jnp.zeros((CHUNK, 128), F32)
            dkd = jnp.zeros((CHUNK, 128), F32)
            ddec = jnp.zeros((1, 128), F32)
            for h in range(2):
                cols = slice(h * GLA_DV, (h + 1) * GLA_DV)
                o = o_ref[0, rows, cols]
                rh = r_ref[0, rows, cols]
                dy = dy_ref[0, rows, cols]
                rms = lax.rsqrt(jnp.mean(o * o, axis=-1, keepdims=True) + LN_EPS)
                nrm = o * rms
                sig = _sigmoid(rh)
                sw = rh * sig
                dr_ref[0, rows, cols] = (dy * nrm * gn * (sig * (1.0 + rh * (1.0 - sig)))).astype(BF16)
                dgn = dgn + jnp.sum(dy * nrm * sw, axis=0, keepdims=True)
                dn = dy * gn * sw
                do = rms * (dn - nrm * jnp.mean(dn * nrm, axis=-1, keepdims=True))
                dob = do.astype(BF16)
                qh = jnp.where(hmask[h], qe, 0.0).astype(BF16)
                vh = v_ref[0, rows, cols].astype(BF16)
                a = jnp.where(tri, _nt(qh, keb), 0.0).astype(BF16)
                st = st_ref[0, h, n]
                ht = h_scr[h]
                da = jnp.where(tri, _nt(dob, vh), 0.0).astype(BF16)
                dqe = dqe + jnp.where(hmask[h], _nn(da, keb) + _nn(dob, st.astype(BF16)), 0.0)
                dke = dke + _tn(da, qh)
                dv_ref[0, rows, cols] = (_tn(a, dob) + _nt(kdb, ht.astype(BF16))).astype(BF16)
                dkd = dkd + jnp.where(hmask[h], _nn(vh, ht.astype(BF16)), 0.0)
                ddec = ddec + jnp.where(hmask[h], jnp.sum(ht * st, axis=0, keepdims=True), 0.0)
                h_scr[h] = dec * ht + _tn(dob, qh)
            dq_ref[0, rows, :] = (dqe * eb * Q_SCALE).astype(BF16)
            dk_ref[0, rows, :] = (dke * enb + dkd * erest).astype(BF16)
            db = dqe * qe - dke * ke - dkd * kd
            dblast = jnp.sum(dkd * kd, axis=0, keepdims=True) + ddec * dec
            lastrow = lax.broadcasted_iota(jnp.int32, (CHUNK, 1), 0) == CHUNK - 1
            db = db + jnp.where(lastrow, dblast, 0.0)
            dlg = jnp.where(valid, _nn(triu, db, precision=lax.Precision.HIGHEST), 0.0)
            dz = dlg * (1.0 / GLA_TAU) * (1.0 - _sigmoid(z))
            dzb = dz.astype(BF16)
            dgd_ref[0, 0, rows, :] = _nt(dzb, gupb).astype(BF16)
            gup_acc[...] += _tn(gd_ref[0, rows, :].astype(BF16), dzb)
            return dbias + jnp.sum(dz, axis=0, keepdims=True), dgn

        zero = jnp.zeros((1, 128), F32)
        dbias, dgn = _chunk_loop(nchunk, step, (zero, zero))
        dgup_ref[0] = gup_acc[...]
        vec_ref[0] = jnp.zeros((8, 128), F32)
        vec_ref[0, 0:1, :] = dbias
        vec_ref[0, 1:2, :] = dgn
        pl.when(last)(finish)

    pair = lambda w, off: pl.BlockSpec((1, tp, w), lambda bi, p: (bi, 0, off // w + p))
    res = pl.pallas_call(
        body, name="gla_bwd",
        out_shape=[jax.ShapeDtypeStruct((bsz, tp, 256), BF16), jax.ShapeDtypeStruct((bsz, tp, 256), BF16),
                   jax.ShapeDtypeStruct((bsz, tp, 512), BF16), jax.ShapeDtypeStruct((bsz, tp, 512), BF16),
                   jax.ShapeDtypeStruct((bsz, 2, tp, 128), BF16), jax.ShapeDtypeStruct((bsz, 128, 256), F32),
                   jax.ShapeDtypeStruct((bsz, 8, 256), F32)] + _chip_exchange_shapes(parts),
        grid=grid,
        in_specs=[pair(256, 0), pair(128, C_Q), pair(128, C_K), pair(256, C_V), pair(256, C_R),
                  pl.BlockSpec((1, tp, 128), lambda bi, p: (bi, 0, C_GD // 128)),
                  pair(256, 0),
                  pl.BlockSpec((1, 2, nchunk, GLA_DV, 128), lambda bi, p: (bi, p, 0, 0, 0)),
                  pl.BlockSpec((128, 128), lambda bi, p: (0, p)),
                  pl.BlockSpec((1, 128), lambda bi, p: (0, p)),
                  pl.BlockSpec((1, 128), lambda bi, p: (0, 0))] + [ANY] * ns,
        out_specs=[pair(128, 0), pair(128, 0), pair(256, 0), pair(256, 0),
                   pl.BlockSpec((1, 1, tp, 128), lambda bi, p: (bi, p, 0, 0)),
                   pl.BlockSpec((1, 128, 128), lambda bi, p: (bi, 0, p)),
                   pl.BlockSpec((1, 8, 128), lambda bi, p: (bi, 0, p))] + [ANY] * ns,
        scratch_shapes=[pltpu.VMEM((2, GLA_DV, 128), F32), pltpu.VMEM((128, 128), F32)] + _chip_exchange_sems(ns),
        compiler_params=_params(("arbitrary", "arbitrary")),
    )(dgla, u, u, u, u, u, o_pre, states, gup, gbias, gnorm, *parts)
    return res[:7], res[7:]


_DU_OFFSETS = (C_VAL, C_GATE, C_Q, C_K, C_V, C_R)
_DU_WIDTHS = (512, 512, 256, 256, 512, 512)


def _du_specs(tm, per, row_map):
    specs = [pl.BlockSpec((tm, w), row_map) for w in _DU_WIDTHS]
    for p in range(2):
        specs.append(pl.BlockSpec((1, 1, tm, 128), lambda *ix, p=p: (row_map(*ix)[0] // per, p, row_map(*ix)[0] % per, 0)))
    return specs


def _du_pieces(refs):
    out = [(off, ref[...]) for off, ref in zip(_DU_OFFSETS, refs[:6])]
    dgd = (refs[6][0, 0].astype(F32) + refs[7][0, 0].astype(F32)).astype(BF16)
    out.append((C_GD, dgd))
    return out


def _in_proj_bwd(pieces, dgd, w_int, dt1, tp, parts):
    r, d = dt1.shape
    tm = _row_tile(tp)
    per = tp // tm
    ns = len(parts)
    grid = (r // tm,)

    def body(*refs):
        w_ref, dt_ref = refs[8:10]
        o_ref = refs[10 + ns]
        start, finish = _chip_exchange_steps(refs[10:10 + ns], refs[11 + ns:11 + 2 * ns], refs[11 + 2 * ns:])
        first, last = _grid_ends(grid)
        pl.when(first)(start)
        acc = ALPHA * dt_ref[...]
        for off, val in _du_pieces(refs[:8]):
            acc = acc + _nn(val, w_ref[off:off + val.shape[1], :])
        o_ref[...] = acc
        pl.when(last)(finish)

    row = lambda i: (i, 0)
    res = pl.pallas_call(
        body, name="in_proj_bwd", out_shape=[jax.ShapeDtypeStruct((r, d), F32)] + _chip_exchange_shapes(parts),
        grid=grid,
        in_specs=_du_specs(tm, per, row) + [pl.BlockSpec((D_IN_PAD, d), lambda i: (0, 0)), pl.BlockSpec((tm, d), row)]
        + [ANY] * ns,
        out_specs=[pl.BlockSpec((tm, d), row)] + [ANY] * ns,
        scratch_shapes=_chip_exchange_sems(ns),
        compiler_params=_params(("arbitrary",)),
    )(*pieces, dgd, dgd, w_int, dt1, *parts)
    return res[0], res[1:]


def _grad_w_in(pieces, dgd, s0b, tp):
    r, d = s0b.shape
    tk = _row_tile(tp)
    per = tp // tk

    def body(*refs):
        s_ref, o_ref = refs[8:]

        @pl.when(pl.program_id(0) == 0)
        def _():
            o_ref[...] = jnp.zeros_like(o_ref)

        s = s_ref[...]
        for off, val in _du_pieces(refs[:8]):
            o_ref[off:off + val.shape[1], :] += _tn(val, s)

    row = lambda k: (k, 0)
    return pl.pallas_call(
        body, name="grad_w_in", out_shape=jax.ShapeDtypeStruct((D_IN_PAD, d), F32),
        grid=(r // tk,),
        in_specs=_du_specs(tk, per, row) + [pl.BlockSpec((tk, d), row)],
        out_specs=pl.BlockSpec((D_IN_PAD, d), lambda k: (0, 0)),
        compiler_params=_params(("arbitrary",)),
    )(*pieces, dgd, dgd, s0b)


def _ln_in_bwd(ds0, x, meta, g):
    bsz, s, d = x.shape
    tp = s + HEAD
    nh = 2
    sh = s // nh
    rc = min(256, sh)

    def body(ds_ref, x_ref, meta_ref, g_ref, gx_ref, dm_ref, vec_ref):
        h = pl.program_id(1)
        gg = g_ref[...]

        @pl.when(h == 0)
        def _():
            mh, mr = _ln_stats(meta_ref[...])
            dsm = ds_ref[0, PAD:HEAD, :]
            dm_ref[0] = _ln_bwd(dsm * gg, mh, mr)
            vec_ref[0] = jnp.zeros((8, d), F32)
            vec_ref[0, 0:1, :] = jnp.sum(dsm * mh, axis=0, keepdims=True)
            vec_ref[0, 1:2, :] = jnp.sum(dsm, axis=0, keepdims=True)

        def step(i, carry):
            sg, sb = carry
            dst = pl.ds(pl.multiple_of(i * rc, rc), rc)
            src = pl.ds(pl.multiple_of(HEAD + h * sh + i * rc, 64), rc)
            xh, rstd = _ln_stats(x_ref[0, dst, :])
            dsv = ds_ref[0, src, :]
            gx_ref[0, dst, :] = _ln_bwd(dsv * gg, xh, rstd)
            return sg + jnp.sum(dsv * xh, axis=0, keepdims=True), sb + jnp.sum(dsv, axis=0, keepdims=True)

        zero = jnp.zeros((1, d), F32)
        sg, sb = lax.fori_loop(0, sh // rc, step, (zero, zero))
        vec_ref[0, 0:1, :] += sg
        vec_ref[0, 1:2, :] += sb

    return pl.pallas_call(
        body, name="ln_in_bwd",
        out_shape=[jax.ShapeDtypeStruct((bsz, s, d), F32), jax.ShapeDtypeStruct((bsz, N_META, d), F32),
                   jax.ShapeDtypeStruct((bsz, 8, d), F32)],
        grid=(bsz, nh),
        in_specs=[pl.BlockSpec((1, tp, d), lambda bi, hi: (bi, 0, 0)),
                  pl.BlockSpec((1, sh, d), lambda bi, hi: (bi, hi, 0)),
                  pl.BlockSpec((N_META, d), lambda bi, hi: (0, 0)),
                  pl.BlockSpec((1, d), lambda bi, hi: (0, 0))],
        out_specs=[pl.BlockSpec((1, sh, d), lambda bi, hi: (bi, hi, 0)),
                   pl.BlockSpec((1, N_META, d), lambda bi, hi: (bi, 0, 0)),
                   pl.BlockSpec((1, 8, d), lambda bi, hi: (bi, 0, 0))],
        compiler_params=_params(("parallel", "arbitrary")),
    )(ds0, x, meta, g)


def _rows128(a):
    return a.reshape(-1, 128)


def kernel(x, meta_tokens, ln_in_g, ln_in_b, w_in, conv_w, conv_b, conv_ln_g, conv_ln_b, gate_up, gate_bias, gla_norm_g, w_out, ln1_g, ln1_b, w_ff1, w_ff2, ln2_g, ln2_b, loss_target, m_meta_tokens, m_ln_in_g, m_ln_in_b, m_w_in, m_conv_w, m_conv_b, m_conv_ln_g, m_conv_ln_b, m_gate_up, m_gate_bias, m_gla_norm_g, m_w_out, m_ln1_g, m_ln1_b, m_w_ff1, m_w_ff2, m_ln2_g, m_ln2_b, v_meta_tokens, v_ln_in_g, v_ln_in_b, v_w_in, v_conv_w, v_conv_b, v_conv_ln_g, v_conv_ln_b, v_gate_up, v_gate_bias, v_gla_norm_g, v_w_out, v_ln1_g, v_ln1_b, v_w_ff1, v_w_ff2, v_ln2_g, v_ln2_b):
    bsz, seq, d = x.shape
    tp = seq + HEAD
    r = bsz * tp
    xi, yi, ci = _mesh_pos()
    chip = 2 * xi + yi
    c_arr = jnp.reshape(ci, (1,)).astype(jnp.int32)
    pos_arr = jnp.stack([chip, ci]).astype(jnp.int32)

    sh_in = D_IN // 4
    shard_in = jnp.pad(w_in[0].T.astype(BF16), ((0, D_IN_PAD // 4 - sh_in), (0, 0)))
    shard_w1t, shard_wout, shard_w2 = w_ff1[0].T.astype(BF16), w_out[0].astype(BF16), w_ff2[0].astype(BF16)
    small_w = jnp.concatenate([_rows128(meta_tokens), _rows128(conv_w[0]), _rows128(gate_up[0])], axis=0)
    (g_int, g_wout), g_small = _gather_weights([shard_in, shard_wout], small_w)
    w_int = jnp.pad(g_int[:, :sh_in].reshape(D_IN, d), ((0, D_IN_PAD - D_IN), (0, 0)))
    wout = g_wout.reshape(d, d)
    n_meta_rows, n_cw_rows = N_META * 256 // 128, CONV_WIDTH * 128 // 128
    meta_full = jnp.concatenate([g_small[j, :n_meta_rows].reshape(N_META, 256) for j in range(4)], axis=1)
    convw_full = jnp.concatenate(
        [g_small[j, n_meta_rows:n_meta_rows + n_cw_rows].reshape(CONV_WIDTH, 128) for j in range(4)], axis=1)
    gup_full = jnp.concatenate(
        [g_small[j, n_meta_rows + n_cw_rows:].reshape(GLA_RANK, 64) for j in range(4)], axis=1)
    convw_p = jnp.pad(convw_full, ((0, 1), (0, 0)))
    gup_p = jnp.pad(gup_full, ((0, 128 - GLA_RANK), (0, 0))).astype(BF16)
    ln_in_g2, ln_in_b2 = ln_in_g.reshape(1, d), ln_in_b.reshape(1, d)

    s0, s0b = _ln_in_fwd(x, meta_full, ln_in_g2, ln_in_b2)
    s0f, s0bf = s0.reshape(r, d), s0b.reshape(r, d)
    u = _in_proj(s0bf, w_int)
    u3 = u.reshape(bsz, tp, D_IN_PAD)
    hc = _conv_fwd(u3, convw_p, conv_b).reshape(r, D_CONV)
    gla_out, o_pre, states, (g_w1t,) = _gla_fwd(u3, gup_p, gate_bias, gla_norm_g, [shard_w1t])
    w1t = g_w1t.reshape(D_FF, d)
    gla_of = gla_out.reshape(r, 512)
    conv_of, xhat1, rstd1, s1b = _out_proj_ln1(hc, gla_of, wout, s0f, conv_ln_g, conv_ln_b, ln1_g, ln1_b)
    ra, (g_w2,) = _ffn1(s1b, w1t, [shard_w2])
    w2 = g_w2.reshape(D_FF, d)
    dt2, dt2b, acc2 = _ffn2_ln2_loss(ra, w2, xhat1, ln1_g, ln1_b, ln2_g, ln2_b, loss_target, tp)

    def pair_reduce(gs):
        got = _pair_exchange(gs)
        return got, [_add_pair(g, t, c_arr) for g, t in zip(gs, got)]

    da = _ffn_bwd_da(dt2b, w2, ra)
    dt1, dt1b, acc1 = _ffn_bwd_ln1(da, w1t, dt2, xhat1, rstd1, ln1_g)
    g_w2 = _matmul_tn(ra, dt2b, 1024, square_lhs=True, name="grad_w_ff2")
    g_w1t = _matmul_tn(da, s1b, 1024, name="grad_w_ff1")
    big_ffn = [g_w1t.reshape(4, D_FF // 4, d), g_w2.reshape(4, D_FF // 4, d)]
    pair_got_ffn, parts_ffn = pair_reduce(big_ffn)
    dhc, dgla, cacc = _out_proj_bwd(dt1b, wout, hc, conv_ln_g, conv_ln_b)
    g_wout = jnp.concatenate([_matmul_tn(conv_of, dt1b, D_CONV, name="grad_w_out_conv"),
                              _matmul_tn(gla_of, dt1b, 512, name="grad_w_out_gla")], axis=0)
    dcv, dcg, dcw = _conv_bwd(dhc.reshape(bsz, tp, D_CONV), u3, convw_p)
    (dq, dk, dv, dr, dgd, dgup, gvec), chip_got_ffn = _gla_bwd(
        dgla.reshape(bsz, tp, 512), u3, o_pre, states, gup_p, gate_bias, gla_norm_g, parts_ffn)
    pieces = [a.reshape(r, a.shape[-1]) for a in (dcv, dcg, dq, dk, dv, dr)]
    g_wint = _grad_w_in(pieces, dgd, s0bf, tp)
    big_io = [jnp.stack([g_wint[j * sh_in:(j + 1) * sh_in] for j in range(4)]), g_wout.reshape(4, d // 4, d)]
    pair_got_io, parts_io = pair_reduce(big_io)
    ds0, chip_got_io = _in_proj_bwd(pieces, dgd, w_int, dt1, tp, parts_io)
    grad_x, dmeta, lvec = _ln_in_bwd(ds0.reshape(bsz, tp, d), x, meta_full, ln_in_g2)

    small = {
        "loss": acc2[0:1], "ln_in_g": jnp.sum(lvec[:, 0], axis=0), "ln_in_b": jnp.sum(lvec[:, 1], axis=0),
        "conv_b": cacc[0], "conv_ln_g": cacc[1], "conv_ln_b": cacc[2], "gate_bias": jnp.sum(gvec[:, 0], axis=0),
        "gla_norm_g": jnp.sum(gvec[:, 1].reshape(bsz * 2, 128), axis=0),
        "ln1_g": acc1[0], "ln1_b": acc1[1], "ln2_g": acc2[1], "ln2_b": acc2[2],
        "conv_w": jnp.sum(dcw, axis=0)[:CONV_WIDTH], "gate_up": jnp.sum(dgup, axis=0)[:GLA_RANK],
        "meta_tokens": jnp.sum(dmeta, axis=0),
    }
    pack = jnp.concatenate([_rows128(small[k]) for k, _ in _SMALL_FIELDS], axis=0)
    pack = jnp.pad(pack, ((0, SMALL_ROWS - pack.shape[0]), (0, 0)))
    red = _sum_small(_allgather_small(pack))
    off, tot = {}, 0
    for k, nrows in _SMALL_FIELDS:
        off[k] = (tot, nrows)
        tot += nrows

    def field(k, shape):
        o, nrows = off[k]
        return red[o:o + nrows].reshape(shape)

    loss = jnp.sum(field("loss", (d,)))

    halves = [_add_chips(g, t, s, pos_arr) for g, t, s in zip(
        big_io + big_ffn, list(pair_got_io) + list(pair_got_ffn), list(chip_got_io) + list(chip_got_ffn))]
    f_wint, f_wout, f_w1t, f_w2 = [
        jnp.concatenate([jnp.where(ci == 0, own, sib), jnp.where(ci == 0, sib, own)], axis=1)
        for own, sib in zip(halves, _pair_swap(halves))]

    def cols(a, width):
        return lax.dynamic_slice_in_dim(a, chip * width, width, axis=a.ndim - 1)

    grads = {
        "meta_tokens": cols(field("meta_tokens", (N_META, d)), 256),
        "ln_in_g": field("ln_in_g", (d,)), "ln_in_b": field("ln_in_b", (d,)),
        "w_in": f_wint.T[None], "conv_w": cols(field("conv_w", (CONV_WIDTH, D_CONV)), 128)[None],
        "conv_b": field("conv_b", (1, D_CONV)), "conv_ln_g": field("conv_ln_g", (1, D_CONV)),
        "conv_ln_b": field("conv_ln_b", (1, D_CONV)),
        "gate_up": cols(field("gate_up", (GLA_RANK, 256)), 64)[None],
        "gate_bias": field("gate_bias", (1, 256)), "gla_norm_g": field("gla_norm_g", (1, 128)),
        "w_out": f_wout[None], "ln1_g": field("ln1_g", (1, d)), "ln1_b": field("ln1_b", (1, d)),
        "w_ff1": f_w1t.T[None], "w_ff2": f_w2[None],
        "ln2_g": field("ln2_g", (1, d)), "ln2_b": field("ln2_b", (1, d)),
    }
    weights = dict(meta_tokens=meta_tokens, ln_in_g=ln_in_g, ln_in_b=ln_in_b, w_in=w_in, conv_w=conv_w, conv_b=conv_b,
                   conv_ln_g=conv_ln_g, conv_ln_b=conv_ln_b, gate_up=gate_up, gate_bias=gate_bias,
                   gla_norm_g=gla_norm_g, w_out=w_out, ln1_g=ln1_g, ln1_b=ln1_b, w_ff1=w_ff1, w_ff2=w_ff2,
                   ln2_g=ln2_g, ln2_b=ln2_b)
    moms = dict(meta_tokens=(m_meta_tokens, v_meta_tokens), ln_in_g=(m_ln_in_g, v_ln_in_g),
                ln_in_b=(m_ln_in_b, v_ln_in_b), w_in=(m_w_in, v_w_in), conv_w=(m_conv_w, v_conv_w),
                conv_b=(m_conv_b, v_conv_b), conv_ln_g=(m_conv_ln_g, v_conv_ln_g),
                conv_ln_b=(m_conv_ln_b, v_conv_ln_b), gate_up=(m_gate_up, v_gate_up),
                gate_bias=(m_gate_bias, v_gate_bias), gla_norm_g=(m_gla_norm_g, v_gla_norm_g),
                w_out=(m_w_out, v_w_out), ln1_g=(m_ln1_g, v_ln1_g), ln1_b=(m_ln1_b, v_ln1_b),
                w_ff1=(m_w_ff1, v_w_ff1), w_ff2=(m_w_ff2, v_w_ff2), ln2_g=(m_ln2_g, v_ln2_g),
                ln2_b=(m_ln2_b, v_ln2_b))
    names = list(weights)
    big_names = ("w_in", "w_out", "w_ff1", "w_ff2")
    delta, new_m, new_v = {}, {}, {}
    dl, mn, vn = _adamw(w_in[0].T, f_wint, m_w_in[0].T, v_w_in[0].T)
    delta["w_in"], new_m["w_in"], new_v["w_in"] = dl.T[None], mn.T[None], vn.T[None]
    for k in big_names[1:]:
        shp = weights[k].shape
        two = lambda a: a.reshape(shp[-2], shp[-1])
        dl, mn, vn = _adamw(two(weights[k]), two(grads[k]), two(moms[k][0]), two(moms[k][1]))
        delta[k], new_m[k], new_v[k] = dl.reshape(shp), mn.reshape(shp), vn.reshape(shp)
    small_names = [k for k in names if k not in big_names]
    sizes = [weights[k].size for k in small_names]
    total = sum(sizes)
    padded = -(-total // 1024) * 1024

    def flat(get):
        v = jnp.concatenate([get(k).reshape(-1) for k in small_names])
        return jnp.pad(v, (0, padded - total)).reshape(padded // 128, 128)

    ones = jnp.ones((padded - total,), F32)
    vflat = jnp.concatenate([jnp.concatenate([moms[k][1].reshape(-1) for k in small_names]), ones])
    dl, mn, vn = _adamw(flat(lambda k: weights[k]), flat(lambda k: grads[k]), flat(lambda k: moms[k][0]),
                        vflat.reshape(padded // 128, 128))
    pos = 0
    for k, sz in zip(small_names, sizes):
        shp = weights[k].shape
        delta[k] = dl.reshape(-1)[pos:pos + sz].reshape(shp)
        new_m[k] = mn.reshape(-1)[pos:pos + sz].reshape(shp)
        new_v[k] = vn.reshape(-1)[pos:pos + sz].reshape(shp)
        pos += sz

    return (loss, grad_x, *[grads[k] for k in names], *[delta[k] for k in names],
            *[new_m[k] for k in names], *[new_v[k] for k in names])
```

```python
import functools

import jax
import jax.numpy as jnp
from jax import lax
from jax.experimental import pallas as pl
from jax.experimental.pallas import tpu as pltpu

F32 = jnp.float32
BF16 = jnp.bfloat16

D_MODEL = 1024
N_META = 16
D_CONV = 512
CONV_WIDTH = 31
GLA_HEADS = 4
GLA_DV = 128
GLA_DK = 64
GLA_RANK = 16
GLA_TAU = 16.0
CHUNK = 64
D_FF = 4096
LN_EPS = 1e-5
ALPHA = 2.0 ** 0.25
D_IN = 2576
D_IN_PAD = 2688
PAD = CHUNK - N_META
HEAD = PAD + N_META
Q_SCALE = GLA_DK ** -0.5
ADAM_LR, ADAM_B1, ADAM_B2, ADAM_EPS, ADAM_WD, ADAM_STEP = 0.001, 0.9, 0.999, 1e-08, 0.01, 10
HALF = D_MODEL // 2
VMEM_LIMIT = 56 * 1024 * 1024
MESH = pl.DeviceIdType.MESH

C_VAL, C_GATE, C_Q, C_K, C_V, C_R, C_GD = 0, 512, 1024, 1280, 1536, 2048, 2560

_SMALL_FIELDS = (("loss", 8), ("ln_in_g", 8), ("ln_in_b", 8), ("conv_b", 4), ("conv_ln_g", 4), ("conv_ln_b", 4),
                 ("gate_bias", 2), ("gla_norm_g", 1), ("ln1_g", 8), ("ln1_b", 8), ("ln2_g", 8), ("ln2_b", 8),
                 ("conv_w", 124), ("gate_up", 32), ("meta_tokens", 128))
SMALL_ROWS = 360


def _params(sem=None, **kw):
    return pltpu.CompilerParams(dimension_semantics=sem, vmem_limit_bytes=VMEM_LIMIT, **kw)


def _row_tile(tp):
    for t in (704, 352, 192, 64):
        if tp % t == 0:
            return t
    raise ValueError(tp)


def _sub_rows(tm):
    return [slice(0, tm)]


def _dot(a, b, dims, precision=None):
    return lax.dot_general(a, b, (dims, ((), ())), preferred_element_type=F32, precision=precision)


def _nn(a, b, **kw):
    return _dot(a, b, ((1,), (0,)), **kw)


def _nt(a, b, **kw):
    return _dot(a, b, ((1,), (1,)), **kw)


def _tn(a, b, **kw):
    return _dot(a, b, ((0,), (0,)), **kw)


def _sigmoid(x):
    return 1.0 / (1.0 + jnp.exp(-x))


def _log_sigmoid(z):
    return jnp.minimum(z, 0.0) - jnp.log(1.0 + jnp.exp(-jnp.abs(z)))


def _ln_stats(t):
    mu = jnp.mean(t, axis=-1, keepdims=True)
    d = t - mu
    var = jnp.mean(d * d, axis=-1, keepdims=True)
    rstd = lax.rsqrt(var + LN_EPS)
    return d * rstd, rstd


def _ln_bwd(dxhat, xhat, rstd):
    m1 = jnp.mean(dxhat, axis=-1, keepdims=True)
    m2 = jnp.mean(dxhat * xhat, axis=-1, keepdims=True)
    return rstd * (dxhat - m1 - xhat * m2)


def _mesh_pos():
    return lax.axis_index("x"), lax.axis_index("y"), lax.axis_index("c")


ANY = pl.BlockSpec(memory_space=pl.ANY)


def _gather_sems(n):
    return [pltpu.SemaphoreType.DMA((n, 3))] * 4


def _gather_steps(ins, outs, sems):
    n = len(ins)
    send, recv, fsend, frecv = sems
    half = [ref.shape[0] // 2 for ref in ins]
    x, y, c = _mesh_pos()
    me = 2 * x + y
    sibling = (x, y, 1 - c)
    chips = [(1 - x, y), (x, 1 - y), (1 - x, 1 - y)]
    chip_idx = [2 * px + py for px, py in chips]
    mine = [pl.ds(pl.multiple_of(c * h, 16), h) for h in half]
    other = [pl.ds(pl.multiple_of((1 - c) * h, 16), h) for h in half]
    pairs = [(a, k) for a in range(n) for k in range(3)]

    def ici(a, k, slab):
        return pltpu.make_async_remote_copy(
            src_ref=ins[a].at[mine[a], :], dst_ref=outs[a].at[slab, mine[a], :],
            send_sem=send.at[a, k], recv_sem=recv.at[a, k], device_id=(*chips[k], c), device_id_type=MESH)

    def forward(a, k, rows):
        blk = outs[a].at[chip_idx[k], rows[a], :]
        return pltpu.make_async_remote_copy(
            src_ref=blk, dst_ref=blk, send_sem=fsend.at[a, k], recv_sem=frecv.at[a, k],
            device_id=sibling, device_id_type=MESH)

    def start():
        for a, k in pairs:
            ici(a, k, me).start()

    def relay():
        for a, k in pairs:
            ici(a, k, chip_idx[k]).wait_recv()
            forward(a, k, mine).start()

    def finish():
        for a, k in pairs:
            forward(a, k, other).wait_recv()
        for a, k in pairs:
            ici(a, k, me).wait_send()
            forward(a, k, mine).wait_send()

    return start, relay, finish


def _place_own(gathered, shards):
    chip = 2 * lax.axis_index("x") + lax.axis_index("y")
    return [lax.dynamic_update_slice(g, s[None], (chip, 0, 0)) for g, s in zip(gathered, shards)]


def _gather_weights(shards, small):
    n = len(shards)

    def body(*refs):
        ins, small_in = refs[:n], refs[n]
        outs, small_out = refs[n + 1:2 * n + 1], refs[2 * n + 1]
        ssend, srecv = refs[2 * n + 6:]
        start, relay, finish = _gather_steps(ins, outs, refs[2 * n + 2:2 * n + 6])
        x, y, c = _mesh_pos()
        chips = [(1 - x, y), (x, 1 - y), (1 - x, 1 - y)]

        def small_copy(k, slot):
            return pltpu.make_async_remote_copy(
                src_ref=small_in, dst_ref=small_out.at[slot], send_sem=ssend.at[k], recv_sem=srecv.at[k],
                device_id=(*chips[k], c), device_id_type=MESH)

        start()
        for k in range(3):
            small_copy(k, 2 * x + y).start()
        relay()
        finish()
        for k, (px, py) in enumerate(chips):
            small_copy(k, 2 * px + py).wait_recv()
        for k in range(3):
            small_copy(k, 2 * x + y).wait_send()

    out_shape = [jax.ShapeDtypeStruct((4,) + s.shape, s.dtype) for s in shards]
    out_shape.append(jax.ShapeDtypeStruct((4,) + small.shape, small.dtype))
    res = pl.pallas_call(
        body, name="gather_weights", out_shape=out_shape,
        in_specs=[ANY] * (n + 1), out_specs=[ANY] * (n + 1),
        scratch_shapes=_gather_sems(n) + [pltpu.SemaphoreType.DMA((3,)), pltpu.SemaphoreType.DMA((3,))],
    )(*shards, small)
    res = _place_own(res, list(shards) + [small])
    return res[:n], res[n]


def _pair_exchange(grads):
    n = len(grads)

    def body(*refs):
        ins, outs = refs[:n], refs[n:2 * n]
        send, recv = refs[2 * n:]
        x, y, c = _mesh_pos()
        other = pl.ds(pl.multiple_of((1 - c) * HALF, 128), HALF)
        cps = [pltpu.make_async_remote_copy(
            src_ref=ins[a].at[:, :, other], dst_ref=outs[a], send_sem=send.at[a], recv_sem=recv.at[a],
            device_id=(x, y, 1 - c), device_id_type=MESH) for a in range(n)]
        for cp in cps:
            cp.start()
        for cp in cps:
            cp.wait()

    return pl.pallas_call(
        body, name="grad_pair_exchange",
        out_shape=[jax.ShapeDtypeStruct(g.shape[:2] + (HALF,), g.dtype) for g in grads],
        in_specs=[ANY] * n, out_specs=[ANY] * n,
        scratch_shapes=[pltpu.SemaphoreType.DMA((n,)), pltpu.SemaphoreType.DMA((n,))],
    )(*grads)


def _chip_exchange_sems(n):
    return [pltpu.SemaphoreType.DMA((n, 3))] * 2


def _chip_exchange_shapes(parts):
    return [jax.ShapeDtypeStruct((3,) + p.shape[1:], p.dtype) for p in parts]


def _chip_exchange_steps(ins, outs, sems):
    send, recv = sems
    x, y, c = _mesh_pos()
    chips = [(1 - x, y), (x, 1 - y), (1 - x, 1 - y)]
    cps = [pltpu.make_async_remote_copy(
        src_ref=ins[a].at[2 * px + py], dst_ref=outs[a].at[k], send_sem=send.at[a, k], recv_sem=recv.at[a, k],
        device_id=(px, py, c), device_id_type=MESH) for a in range(len(ins)) for k, (px, py) in enumerate(chips)]

    def start():
        for cp in cps:
            cp.start()

    def finish():
        for cp in cps:
            cp.wait()

    return start, finish


def _chip_exchange(parts):
    n = len(parts)

    def body(*refs):
        start, finish = _chip_exchange_steps(refs[:n], refs[n:2 * n], refs[2 * n:])
        start()
        finish()

    return pl.pallas_call(
        body, name="grad_chip_exchange", out_shape=_chip_exchange_shapes(parts),
        in_specs=[ANY] * n, out_specs=[ANY] * n, scratch_shapes=_chip_exchange_sems(n),
    )(*parts)


def _pair_swap(halves):
    n = len(halves)

    def body(*refs):
        ins, outs = refs[:n], refs[n:2 * n]
        send, recv = refs[2 * n:]
        x, y, c = _mesh_pos()
        cps = [pltpu.make_async_remote_copy(
            src_ref=ins[a], dst_ref=outs[a], send_sem=send.at[a], recv_sem=recv.at[a],
            device_id=(x, y, 1 - c), device_id_type=MESH) for a in range(n)]
        for cp in cps:
            cp.start()
        for cp in cps:
            cp.wait()

    return pl.pallas_call(
        body, name="grad_pair_swap",
        out_shape=[jax.ShapeDtypeStruct(h.shape, h.dtype) for h in halves],
        in_specs=[ANY] * n, out_specs=[ANY] * n,
        scratch_shapes=[pltpu.SemaphoreType.DMA((n,)), pltpu.SemaphoreType.DMA((n,))],
    )(*halves)


def _allgather_small(pack):
    m_per, ncol = pack.shape

    def body(x_ref, out_ref, send_sems, recv_sems, local_sem):
        x, y, c = _mesh_pos()
        me, sibling = (x, y, c), (x, y, 1 - c)
        chips = [(1 - x, y), (x, 1 - y), (1 - x, 1 - y)]

        def rows(px, py, pc):
            return out_ref.at[pl.ds(pl.multiple_of((4 * px + 2 * py + pc) * m_per, 8), m_per), :]

        def copy(k, block, to, src=None):
            return pltpu.make_async_remote_copy(
                src_ref=rows(*block) if src is None else src, dst_ref=rows(*block),
                send_sem=send_sems.at[k], recv_sem=recv_sems.at[k], device_id=to, device_id_type=MESH)

        mine = pltpu.make_async_copy(x_ref, rows(*me), local_sem)
        mine.start()
        first = [copy(0, me, sibling, src=x_ref)]
        first += [copy(1 + j, me, (*chip, c), src=x_ref) for j, chip in enumerate(chips)]
        for cp in first:
            cp.start()
        passed = [copy(4 + j, (*chip, c), sibling) for j, chip in enumerate(chips)]
        for j, chip in enumerate(chips):
            copy(1 + j, (*chip, c), me).wait_recv()
            passed[j].start()
        copy(0, sibling, me).wait_recv()
        for j, chip in enumerate(chips):
            copy(4 + j, (*chip, 1 - c), me).wait_recv()
        for cp in first + passed:
            cp.wait_send()
        mine.wait()

    return pl.pallas_call(
        body, name="allgather_small",
        out_shape=jax.ShapeDtypeStruct((8 * m_per, ncol), pack.dtype),
        in_specs=[pl.BlockSpec(memory_space=pltpu.VMEM)],
        out_specs=pl.BlockSpec(memory_space=pltpu.VMEM),
        scratch_shapes=[pltpu.SemaphoreType.DMA((7,)), pltpu.SemaphoreType.DMA((7,)), pltpu.SemaphoreType.DMA],
    )(pack)


def _add_pair(g, got, c_arr):
    _, rows, _ = g.shape

    def body(c_ref, g_ref, r_ref, o_ref):
        o_ref[...] = (g_ref[...] + r_ref[...]).astype(BF16)

    return pl.pallas_call(
        body, name="grad_add_pair", out_shape=jax.ShapeDtypeStruct((4, rows, HALF), BF16),
        grid_spec=pltpu.PrefetchScalarGridSpec(
            num_scalar_prefetch=1, grid=(4,),
            in_specs=[pl.BlockSpec((1, rows, HALF), lambda j, c: (j, 0, c[0])),
                      pl.BlockSpec((1, rows, HALF), lambda j, c: (j, 0, 0))],
            out_specs=pl.BlockSpec((1, rows, HALF), lambda j, c: (j, 0, 0))),
        compiler_params=_params(("arbitrary",)),
    )(c_arr, g, got)


def _add_chips(g, pair_got, chip_got, pos_arr):
    _, rows, _ = g.shape

    def body(pos_ref, g_ref, p_ref, r_ref, o_ref):
        own = g_ref[0] + p_ref[0]
        o_ref[...] = ((own + r_ref[0].astype(F32)) + r_ref[1].astype(F32)) + r_ref[2].astype(F32)

    return pl.pallas_call(
        body, name="grad_add_chips", out_shape=jax.ShapeDtypeStruct((rows, HALF), F32),
        grid_spec=pltpu.PrefetchScalarGridSpec(
            num_scalar_prefetch=1, grid=(1,),
            in_specs=[pl.BlockSpec((1, rows, HALF), lambda i, p: (p[0], 0, p[1])),
                      pl.BlockSpec((1, rows, HALF), lambda i, p: (p[0], 0, 0)),
                      pl.BlockSpec((3, rows, HALF), lambda i, p: (0, 0, 0))],
            out_specs=pl.BlockSpec((rows, HALF), lambda i, p: (0, 0))),
        compiler_params=_params(("arbitrary",)),
    )(pos_arr, g, pair_got, chip_got)


def _sum_small(gathered):
    def body(g_ref, o_ref):
        acc = g_ref[0:SMALL_ROWS, :]
        for d in range(1, 8):
            acc = acc + g_ref[d * SMALL_ROWS:(d + 1) * SMALL_ROWS, :]
        o_ref[...] = acc

    return pl.pallas_call(
        body, name="sum_small", out_shape=jax.ShapeDtypeStruct((SMALL_ROWS, 128), F32),
    )(gathered)


def _adamw(w, g, m, v):
    rows, cols = w.shape
    tr = 256 if rows % 256 == 0 else rows
    c1 = 1.0 - ADAM_B1 ** ADAM_STEP
    c2 = 1.0 - ADAM_B2 ** ADAM_STEP

    def body(w_ref, g_ref, m_ref, v_ref, d_ref, mo_ref, vo_ref):
        gg = g_ref[...]
        mn = ADAM_B1 * m_ref[...] + (1.0 - ADAM_B1) * gg
        vn = ADAM_B2 * v_ref[...] + (1.0 - ADAM_B2) * (gg * gg)
        d_ref[...] = -ADAM_LR * ((mn / c1) / (jnp.sqrt(vn / c2) + ADAM_EPS) + ADAM_WD * w_ref[...])
        mo_ref[...] = mn
        vo_ref[...] = vn

    spec = pl.BlockSpec((tr, cols), lambda i: (i, 0))
    return pl.pallas_call(
        body, name="adamw", out_shape=[jax.ShapeDtypeStruct(w.shape, F32)] * 3,
        grid=(rows // tr,), in_specs=[spec] * 4, out_specs=[spec] * 3,
        compiler_params=_params(("parallel",)),
    )(w, g, m, v)


def _ln_in_fwd(x, meta, g, b):
    bsz, s, d = x.shape
    tp = s + HEAD
    nh = 2
    sh = s // nh
    rc = min(256, sh)

    def body(x_ref, meta_ref, g_ref, b_ref, s0_ref, s0b_ref):
        h = pl.program_id(1)
        gg, bb = g_ref[...], b_ref[...]

        @pl.when(h == 0)
        def _():
            s0_ref[0, 0:PAD, :] = jnp.zeros((PAD, d), F32)
            s0b_ref[0, 0:PAD, :] = jnp.zeros((PAD, d), BF16)
            mh, _ = _ln_stats(meta_ref[...])
            mv = mh * gg + bb
            s0_ref[0, PAD:HEAD, :] = mv
            s0b_ref[0, PAD:HEAD, :] = mv.astype(BF16)

        def step(i, carry):
            src = pl.ds(pl.multiple_of(i * rc, rc), rc)
            dst = pl.ds(pl.multiple_of(HEAD + h * sh + i * rc, 64), rc)
            xh, _ = _ln_stats(x_ref[0, src, :])
            val = xh * gg + bb
            s0_ref[0, dst, :] = val
            s0b_ref[0, dst, :] = val.astype(BF16)
            return carry

        lax.fori_loop(0, sh // rc, step, 0)

    full = lambda bi, hi: (bi, 0, 0)
    return pl.pallas_call(
        body, name="ln_in_fwd",
        out_shape=[jax.ShapeDtypeStruct((bsz, tp, d), F32), jax.ShapeDtypeStruct((bsz, tp, d), BF16)],
        grid=(bsz, nh),
        in_specs=[pl.BlockSpec((1, sh, d), lambda bi, hi: (bi, hi, 0)),
                  pl.BlockSpec((N_META, d), lambda bi, hi: (0, 0)),
                  pl.BlockSpec((1, d), lambda bi, hi: (0, 0)),
                  pl.BlockSpec((1, d), lambda bi, hi: (0, 0))],
        out_specs=[pl.BlockSpec((1, tp, d), full)] * 2,
        compiler_params=_params(("parallel", "arbitrary")),
    )(x, meta, g, b)


def _in_proj(s0b, w_int):
    r, d = s0b.shape
    tm = _row_tile(r)

    def body(a_ref, w_ref, o_ref):
        o_ref[...] = _nt(a_ref[...], w_ref[...])

    return pl.pallas_call(
        body, name="in_proj", out_shape=jax.ShapeDtypeStruct((r, D_IN_PAD), F32),
        grid=(r // tm,),
        in_specs=[pl.BlockSpec((tm, d), lambda i: (i, 0)), pl.BlockSpec((D_IN_PAD, d), lambda i: (0, 0))],
        out_specs=pl.BlockSpec((tm, D_IN_PAD), lambda i: (i, 0)),
        compiler_params=_params(("parallel",)),
    )(s0b, w_int)


def _conv_fwd(u, conv_w, conv_b):
    bsz, tp, _ = u.shape
    nchunk = tp // CHUNK
    win = CHUNK + 32
    nct = D_CONV // 128

    def body(cv_ref, cg_ref, w_ref, cb_ref, hc_ref, h_scr, win_scr):
        h_scr[0:32, :] = jnp.zeros((32, 128), F32)
        h_scr[32:32 + tp, :] = cv_ref[0] * _sigmoid(cg_ref[0])
        cb = cb_ref[...]

        def step(n, carry):
            r0 = pl.multiple_of(n * CHUNK, CHUNK)
            win_scr[...] = h_scr[pl.ds(r0, win), :]
            acc = jnp.zeros((CHUNK, 128), F32)
            for j in range(CONV_WIDTH):
                acc = acc + w_ref[j:j + 1, :] * win_scr[2 + j:2 + j + CHUNK, :]
            hc_ref[0, pl.ds(r0, CHUNK), :] = acc + cb
            return carry

        lax.fori_loop(0, nchunk, step, 0)

    return pl.pallas_call(
        body, name="conv_fwd", out_shape=jax.ShapeDtypeStruct((bsz, tp, D_CONV), F32),
        grid=(bsz, nct),
        in_specs=[pl.BlockSpec((1, tp, 128), lambda bi, t: (bi, 0, C_VAL // 128 + t)),
                  pl.BlockSpec((1, tp, 128), lambda bi, t: (bi, 0, C_GATE // 128 + t)),
                  pl.BlockSpec((32, 128), lambda bi, t: (0, t)),
                  pl.BlockSpec((1, 128), lambda bi, t: (0, t))],
        out_specs=pl.BlockSpec((1, tp, 128), lambda bi, t: (bi, 0, t)),
        scratch_shapes=[pltpu.VMEM((tp + 32, 128), F32), pltpu.VMEM((win, 128), F32)],
        compiler_params=_params(("parallel", "parallel")),
    )(u, u, conv_w, conv_b)


def _gla_group(nchunk):
    return 11 if nchunk % 11 == 0 else nchunk


def _bdot(a, b, ca, cb, precision=None):
    return lax.dot_general(a, b, (((ca,), (cb,)), ((0,), (0,))), preferred_element_type=F32, precision=precision)


def _bnn(a, b, **kw):
    return _bdot(a, b, 2, 1, **kw)


def _bnt(a, b, **kw):
    return _bdot(a, b, 2, 2, **kw)


def _gla_consts(nb):
    row = lax.broadcasted_iota(jnp.int32, (nb, CHUNK, CHUNK), 1)
    col = lax.broadcasted_iota(jnp.int32, (nb, CHUNK, CHUNK), 2)
    lane = lax.broadcasted_iota(jnp.int32, (1, 1, 128), 2)
    return row >= col, row <= col, [lane < GLA_DK, lane >= GLA_DK]


def _gla_group_terms(g, nb, q_ref, k_ref, gd_ref, gup_ref, gb_ref, tril):
    m = nb * CHUNK
    rows = pl.ds(pl.multiple_of(g * m, CHUNK), m)
    z = _nn(gd_ref[0, rows, :].astype(BF16), gup_ref[...]) + gb_ref[...]
    valid = g * m + lax.broadcasted_iota(jnp.int32, (m, 1), 0) >= PAD
    lg = jnp.where(valid, _log_sigmoid(z) * (1.0 / GLA_TAU), 0.0)
    bcum = _bnn(tril.astype(F32), lg.reshape(nb, CHUNK, 128), precision=lax.Precision.HIGHEST)
    blast = bcum[:, CHUNK - 1:CHUNK, :]
    eb = jnp.exp(bcum)
    enb = jnp.exp(-bcum)
    erest = jnp.exp(blast - bcum)
    q = (q_ref[0, rows, :] * Q_SCALE).reshape(nb, CHUNK, 128)
    k = k_ref[0, rows, :].reshape(nb, CHUNK, 128)
    return rows, valid, z, eb, enb, erest, jnp.exp(blast), q * eb, k * enb, k * erest


def _grid_ends(grid):
    ids = [pl.program_id(i) for i in range(len(grid))]
    first = functools.reduce(jnp.logical_and, [i == 0 for i in ids])
    last = functools.reduce(jnp.logical_and, [i == g - 1 for i, g in zip(ids, grid)])
    return first, last


def _gla_fwd(u, gup, gbias, gnorm, shards):
    bsz, tp, _ = u.shape
    nchunk = tp // CHUNK
    nb = _gla_group(nchunk)
    ns = len(shards)
    grid = (bsz, 2)

    def body(*refs):
        q_ref, k_ref, v_ref, r_ref, gd_ref, gup_ref, gb_ref, gn_ref = refs[:8]
        out_ref, o_ref, st_ref = refs[8 + ns:11 + ns]
        s_scr = refs[11 + 2 * ns]
        start, relay, finish = _gather_steps(refs[8:8 + ns], refs[11 + ns:11 + 2 * ns], refs[12 + 2 * ns:])
        first, last = _grid_ends(grid)
        pl.when(first)(start)
        pl.when(last)(relay)
        tril, _, hmask = _gla_consts(nb)
        s_scr[...] = jnp.zeros_like(s_scr)
        gn = gn_ref[...]

        def group(g, carry):
            rows, _, _, _, _, _, dec, qe, ke, kd = _gla_group_terms(g, nb, q_ref, k_ref, gd_ref, gup_ref, gb_ref, tril)
            keb, kdb = ke.astype(BF16), kd.astype(BF16)
            for h in range(2):
                cols = slice(h * GLA_DV, (h + 1) * GLA_DV)
                qh = jnp.where(hmask[h], qe, 0.0).astype(BF16)
                vh = v_ref[0, rows, cols].astype(BF16).reshape(nb, CHUNK, GLA_DV)
                a = jnp.where(tril, _bnt(qh, keb), 0.0).astype(BF16)
                st = s_scr[h]
                sts = []
                for n in range(nb):
                    st_ref[0, h, g * nb + n] = st
                    sts.append(st.astype(BF16))
                    st = dec[n] * st + _tn(vh[n], kdb[n])
                s_scr[h] = st
                o = (_bnn(a, vh) + _bnt(qh, jnp.stack(sts))).reshape(nb * CHUNK, GLA_DV)
                o_ref[0, rows, cols] = o
                rms = lax.rsqrt(jnp.mean(o * o, axis=-1, keepdims=True) + LN_EPS)
                rh = r_ref[0, rows, cols]
                out_ref[0, rows, cols] = (o * rms * gn * (rh * _sigmoid(rh))).astype(BF16)
            return carry

        lax.fori_loop(0, nchunk // nb, group, 0)
        pl.when(last)(finish)

    res = pl.pallas_call(
        body, name="gla_fwd",
        out_shape=[jax.ShapeDtypeStruct((bsz, tp, 512), BF16), jax.ShapeDtypeStruct((bsz, tp, 512), F32),
                   jax.ShapeDtypeStruct((bsz, GLA_HEADS, nchunk, GLA_DV, 128), F32)]
        + [jax.ShapeDtypeStruct((4,) + s.shape, s.dtype) for s in shards],
        grid=grid,
        in_specs=[pl.BlockSpec((1, tp, 128), lambda bi, p: (bi, 0, C_Q // 128 + p)),
                  pl.BlockSpec((1, tp, 128), lambda bi, p: (bi, 0, C_K // 128 + p)),
                  pl.BlockSpec((1, tp, 256), lambda bi, p: (bi, 0, C_V // 256 + p)),
                  pl.BlockSpec((1, tp, 256), lambda bi, p: (bi, 0, C_R // 256 + p)),
                  pl.BlockSpec((1, tp, 128), lambda bi, p: (bi, 0, C_GD // 128)),
                  pl.BlockSpec((128, 128), lambda bi, p: (0, p)),
                  pl.BlockSpec((1, 128), lambda bi, p: (0, p)),
                  pl.BlockSpec((1, 128), lambda bi, p: (0, 0))] + [ANY] * ns,
        out_specs=[pl.BlockSpec((1, tp, 256), lambda bi, p: (bi, 0, p)),
                   pl.BlockSpec((1, tp, 256), lambda bi, p: (bi, 0, p)),
                   pl.BlockSpec((1, 2, nchunk, GLA_DV, 128), lambda bi, p: (bi, p, 0, 0, 0))] + [ANY] * ns,
        scratch_shapes=[pltpu.VMEM((2, GLA_DV, 128), F32)] + _gather_sems(ns),
        compiler_params=_params(("arbitrary", "arbitrary")),
    )(u, u, u, u, u, gup, gbias, gnorm, *shards)
    return res[0], res[1], res[2], _place_own(res[3:], shards)


def _out_proj_ln1(hc, gla_out, w_out, s0, cg, cb, g1, b1):
    r, d = s0.shape
    tm = _row_tile(r)

    def body(hc_ref, a_ref, w_ref, s0_ref, cg_ref, cb_ref, g_ref, b_ref, co_ref, xh_ref, rstd_ref, s1b_ref):
        for rs in _sub_rows(tm):
            xc, _ = _ln_stats(hc_ref[rs, :])
            nv = xc * cg_ref[...] + cb_ref[...]
            co = (nv * _sigmoid(nv)).astype(BF16)
            co_ref[rs, :] = co
            mix = _nn(co, w_ref[0:D_CONV, :]) + _nn(a_ref[rs, :], w_ref[D_CONV:, :])
            xh, rstd = _ln_stats(ALPHA * s0_ref[rs, :] + mix)
            xh_ref[rs, :] = xh
            rstd_ref[rs, :] = rstd
            s1b_ref[rs, :] = (xh * g_ref[...] + b_ref[...]).astype(BF16)

    row = lambda n: pl.BlockSpec((tm, n), lambda i: (i, 0))
    vec = lambda n: pl.BlockSpec((1, n), lambda i: (0, 0))
    return pl.pallas_call(
        body, name="out_proj_ln1",
        out_shape=[jax.ShapeDtypeStruct((r, D_CONV), BF16), jax.ShapeDtypeStruct((r, d), F32),
                   jax.ShapeDtypeStruct((r, 1), F32), jax.ShapeDtypeStruct((r, d), BF16)],
        grid=(r // tm,),
        in_specs=[row(D_CONV), row(512), pl.BlockSpec((d, d), lambda i: (0, 0)), row(d),
                  vec(D_CONV), vec(D_CONV), vec(d), vec(d)],
        out_specs=[row(D_CONV), row(d), row(1), row(d)],
        compiler_params=_params(("parallel",)),
    )(hc, gla_out, w_out, s0, cg, cb, g1, b1)


def _ffn1(s1b, w1t, shards):
    r, d = s1b.shape
    tm = _row_tile(r)
    ns = len(shards)
    grid = (r // tm,)

    def body(*refs):
        a_ref, w_ref = refs[:2]
        o_ref = refs[2 + ns]
        start, relay, finish = _gather_steps(refs[2:2 + ns], refs[3 + ns:3 + 2 * ns], refs[3 + 2 * ns:])
        first, last = _grid_ends(grid)
        pl.when(first)(start)
        pl.when(last)(relay)
        o_ref[...] = jnp.maximum(_nt(a_ref[...], w_ref[...]), 0.0).astype(BF16)
        pl.when(last)(finish)

    res = pl.pallas_call(
        body, name="ffn1",
        out_shape=[jax.ShapeDtypeStruct((r, D_FF), BF16)] + [jax.ShapeDtypeStruct((4,) + s.shape, s.dtype) for s in shards],
        grid=grid,
        in_specs=[pl.BlockSpec((tm, d), lambda i: (i, 0)), pl.BlockSpec((D_FF, d), lambda i: (0, 0))] + [ANY] * ns,
        out_specs=[pl.BlockSpec((tm, D_FF), lambda i: (i, 0))] + [ANY] * ns,
        scratch_shapes=_gather_sems(ns),
        compiler_params=_params(("arbitrary",)),
    )(s1b, w1t, *shards)
    return res[0], _place_own(res[1:], shards)


def _ffn2_ln2_loss(ra, w2, xhat1, g1, b1, g2, b2, tgt, tp):
    r, d = xhat1.shape
    tm = _row_tile(tp)
    per = tp // tm

    def body(ra_ref, w_ref, xh1_ref, g1_ref, b1_ref, g2_ref, b2_ref, tgt_ref, dt_ref, dtb_ref, acc_ref, t_ref, sem):
        i = pl.program_id(0)
        b, j = i // per, i % per

        @pl.when(i == 0)
        def _():
            acc_ref[...] = jnp.zeros_like(acc_ref)

        head_copy = pltpu.make_async_copy(tgt_ref.at[b, pl.ds(0, tm - HEAD), :], t_ref.at[pl.ds(HEAD, tm - HEAD), :], sem)
        body_copy = pltpu.make_async_copy(
            tgt_ref.at[b, pl.ds(pl.multiple_of(jnp.maximum(j * tm - HEAD, 0), 64), tm), :], t_ref, sem)

        @pl.when(j == 0)
        def _():
            t_ref[0:HEAD, :] = jnp.zeros((HEAD, d), F32)
            head_copy.start()

        pl.when(j > 0)(body_copy.start)

        sums = [jnp.zeros((1, d), F32)] * 3
        for rs in _sub_rows(tm):
            rb = ra_ref[rs, :]
            f = _nn(rb * rb, w_ref[...])
            if rs.start == 0:
                pl.when(j == 0)(head_copy.wait)
                pl.when(j > 0)(body_copy.wait)
            s1 = xh1_ref[rs, :] * g1_ref[...] + b1_ref[...]
            xh2, rstd2 = _ln_stats(ALPHA * s1 + f)
            y = xh2 * g2_ref[...] + b2_ref[...]
            rowid = (i % per) * tm + rs.start + lax.broadcasted_iota(jnp.int32, (rs.stop - rs.start, 1), 0)
            e = jnp.where(rowid >= HEAD, y - t_ref[rs, :], 0.0)
            dy = e * (1.0 / d)
            dt2 = _ln_bwd(dy * g2_ref[...], xh2, rstd2)
            dt_ref[rs, :] = dt2
            dtb_ref[rs, :] = dt2.astype(BF16)
            sums = [sums[0] + (0.5 / d) * jnp.sum(e * e, axis=0, keepdims=True),
                    sums[1] + jnp.sum(dy * xh2, axis=0, keepdims=True), sums[2] + jnp.sum(dy, axis=0, keepdims=True)]
        for k in range(3):
            acc_ref[k:k + 1, :] += sums[k]

    row = lambda n: pl.BlockSpec((tm, n), lambda i: (i, 0))
    vec = pl.BlockSpec((1, d), lambda i: (0, 0))
    return pl.pallas_call(
        body, name="ffn2_ln2_loss",
        out_shape=[jax.ShapeDtypeStruct((r, d), F32), jax.ShapeDtypeStruct((r, d), BF16),
                   jax.ShapeDtypeStruct((8, d), F32)],
        grid=(r // tm,),
        in_specs=[row(D_FF), pl.BlockSpec((D_FF, d), lambda i: (0, 0)), row(d), vec, vec, vec, vec, ANY],
        out_specs=[row(d), row(d), pl.BlockSpec((8, d), lambda i: (0, 0))],
        scratch_shapes=[pltpu.VMEM((tm, d), F32), pltpu.SemaphoreType.DMA],
        compiler_params=_params(("arbitrary",)),
    )(ra, w2, xhat1, g1, b1, g2, b2, tgt)


def _ffn_bwd_da(dt2b, w2, ra):
    r, d = dt2b.shape
    tm = _row_tile(r)

    def body(g_ref, w_ref, ra_ref, o_ref):
        o_ref[...] = (_nt(g_ref[...], w_ref[...]) * (2.0 * ra_ref[...].astype(F32))).astype(BF16)

    return pl.pallas_call(
        body, name="ffn_bwd_da", out_shape=jax.ShapeDtypeStruct((r, D_FF), BF16),
        grid=(r // tm,),
        in_specs=[pl.BlockSpec((tm, d), lambda i: (i, 0)), pl.BlockSpec((D_FF, d), lambda i: (0, 0)),
                  pl.BlockSpec((tm, D_FF), lambda i: (i, 0))],
        out_specs=pl.BlockSpec((tm, D_FF), lambda i: (i, 0)),
        compiler_params=_params(("parallel",)),
    )(dt2b, w2, ra)


def _ffn_bwd_ln1(da, w1t, dt2, xhat1, rstd1, g1):
    r, d = dt2.shape
    tm = _row_tile(r)

    def body(da_ref, w_ref, dt2_ref, xh_ref, rstd_ref, g_ref, dt_ref, dtb_ref, acc_ref):
        @pl.when(pl.program_id(0) == 0)
        def _():
            acc_ref[...] = jnp.zeros_like(acc_ref)

        sums = [jnp.zeros((1, d), F32)] * 2
        for rs in _sub_rows(tm):
            ds1 = ALPHA * dt2_ref[rs, :] + _nn(da_ref[rs, :], w_ref[...])
            xh = xh_ref[rs, :]
            dt1 = _ln_bwd(ds1 * g_ref[...], xh, rstd_ref[rs, :])
            dt_ref[rs, :] = dt1
            dtb_ref[rs, :] = dt1.astype(BF16)
            sums = [sums[0] + jnp.sum(ds1 * xh, axis=0, keepdims=True), sums[1] + jnp.sum(ds1, axis=0, keepdims=True)]
        for k in range(2):
            acc_ref[k:k + 1, :] += sums[k]

    row = lambda n: pl.BlockSpec((tm, n), lambda i: (i, 0))
    return pl.pallas_call(
        body, name="ffn_bwd_ln1",
        out_shape=[jax.ShapeDtypeStruct((r, d), F32), jax.ShapeDtypeStruct((r, d), BF16),
                   jax.ShapeDtypeStruct((8, d), F32)],
        grid=(r // tm,),
        in_specs=[row(D_FF), pl.BlockSpec((D_FF, d), lambda i: (0, 0)), row(d), row(d), row(1),
                  pl.BlockSpec((1, d), lambda i: (0, 0))],
        out_specs=[row(d), row(d), pl.BlockSpec((8, d), lambda i: (0, 0))],
        compiler_params=_params(("arbitrary",)),
    )(da, w1t, dt2, xhat1, rstd1, g1)


def _matmul_tn(lhs, rhs, bm, square_lhs=False, name="matmul_tn"):
    r, m = lhs.shape
    n = rhs.shape[1]
    tk = _row_tile(r)

    def body(a_ref, b_ref, o_ref):
        @pl.when(pl.program_id(1) == 0)
        def _():
            o_ref[...] = jnp.zeros_like(o_ref)

        a = a_ref[...]
        if square_lhs:
            a = a * a
        o_ref[...] += _tn(a, b_ref[...])

    return pl.pallas_call(
        body, name=name, out_shape=jax.ShapeDtypeStruct((m, n), F32),
        grid=(m // bm, r // tk),
        in_specs=[pl.BlockSpec((tk, bm), lambda i, k: (k, i)), pl.BlockSpec((tk, n), lambda i, k: (k, 0))],
        out_specs=pl.BlockSpec((bm, n), lambda i, k: (i, 0)),
        compiler_params=_params(("parallel", "arbitrary")),
    )(lhs, rhs)


def _out_proj_bwd(dt1b, w_out, hc, cg, cb):
    r, d = dt1b.shape
    tm = _row_tile(r)

    def body(g_ref, w_ref, hc_ref, cg_ref, cb_ref, dhc_ref, dgla_ref, acc_ref):
        @pl.when(pl.program_id(0) == 0)
        def _():
            acc_ref[...] = jnp.zeros_like(acc_ref)

        gg = cg_ref[...]
        sums = [jnp.zeros((1, D_CONV), F32)] * 3
        for rs in _sub_rows(tm):
            dmix = _nt(g_ref[rs, :], w_ref[...])
            dgla_ref[rs, :] = dmix[:, D_CONV:]
            xh, rstd = _ln_stats(hc_ref[rs, :])
            nv = xh * gg + cb_ref[...]
            sig = _sigmoid(nv)
            dn = dmix[:, :D_CONV] * (sig * (1.0 + nv * (1.0 - sig)))
            dhc = _ln_bwd(dn * gg, xh, rstd)
            dhc_ref[rs, :] = dhc
            sums = [sums[0] + jnp.sum(dhc, axis=0, keepdims=True), sums[1] + jnp.sum(dn * xh, axis=0, keepdims=True),
                    sums[2] + jnp.sum(dn, axis=0, keepdims=True)]
        for k in range(3):
            acc_ref[k:k + 1, :] += sums[k]

    row = lambda n: pl.BlockSpec((tm, n), lambda i: (i, 0))
    vec = pl.BlockSpec((1, D_CONV), lambda i: (0, 0))
    return pl.pallas_call(
        body, name="out_proj_bwd",
        out_shape=[jax.ShapeDtypeStruct((r, D_CONV), F32), jax.ShapeDtypeStruct((r, 512), F32),
                   jax.ShapeDtypeStruct((8, D_CONV), F32)],
        grid=(r // tm,),
        in_specs=[row(d), pl.BlockSpec((d, d), lambda i: (0, 0)), row(D_CONV), vec, vec],
        out_specs=[row(D_CONV), row(512), pl.BlockSpec((8, D_CONV), lambda i: (0, 0))],
        compiler_params=_params(("arbitrary",)),
    )(dt1b, w_out, hc, cg, cb)


def _conv_bwd(dhc, u, conv_w):
    bsz, tp, _ = u.shape
    nchunk = tp // CHUNK
    win = CHUNK + 32
    nct = D_CONV // 128

    def body(dhc_ref, cv_ref, cg_ref, w_ref, dv_ref, dg_ref, dw_ref, h_scr, dhc_scr, hwin, dwin, dw_scr):
        h_scr[0:32, :] = jnp.zeros((32, 128), F32)
        h_scr[32:32 + tp, :] = cv_ref[0] * _sigmoid(cg_ref[0])
        dhc_scr[0:tp, :] = dhc_ref[0]
        dhc_scr[tp:tp + 32, :] = jnp.zeros((32, 128), F32)
        dw_scr[...] = jnp.zeros_like(dw_scr)

        def step(n, carry):
            r0 = pl.multiple_of(n * CHUNK, CHUNK)
            rows = pl.ds(r0, CHUNK)
            hwin[...] = h_scr[pl.ds(r0, win), :]
            dwin[...] = dhc_scr[pl.ds(r0, win), :]
            dcur = dwin[0:CHUNK, :]
            acc = jnp.zeros((CHUNK, 128), F32)
            for j in range(CONV_WIDTH):
                acc = acc + w_ref[j:j + 1, :] * dwin[30 - j:30 - j + CHUNK, :]
                prod = dcur * hwin[2 + j:2 + j + CHUNK, :]
                dw_scr[j * 8:(j + 1) * 8, :] += jnp.sum(prod.reshape(CHUNK // 8, 8, 128), axis=0)
            cg = cg_ref[0, rows, :]
            sig = _sigmoid(cg)
            rowid = n * CHUNK + lax.broadcasted_iota(jnp.int32, (CHUNK, 1), 0)
            dh = jnp.where(rowid >= PAD, acc, 0.0)
            dv_ref[0, rows, :] = (dh * sig).astype(BF16)
            dg_ref[0, rows, :] = (dh * cv_ref[0, rows, :] * sig * (1.0 - sig)).astype(BF16)
            return carry

        lax.fori_loop(0, nchunk, step, 0)
        dw_ref[0] = jnp.zeros((32, 128), F32)
        for j in range(CONV_WIDTH):
            dw_ref[0, j:j + 1, :] = jnp.sum(dw_scr[j * 8:(j + 1) * 8, :], axis=0, keepdims=True)

    blk = lambda off: pl.BlockSpec((1, tp, 128), lambda bi, t: (bi, 0, off // 128 + t))
    return pl.pallas_call(
        body, name="conv_bwd",
        out_shape=[jax.ShapeDtypeStruct((bsz, tp, D_CONV), BF16), jax.ShapeDtypeStruct((bsz, tp, D_CONV), BF16),
                   jax.ShapeDtypeStruct((bsz, 32, D_CONV), F32)],
        grid=(bsz, nct),
        in_specs=[blk(0), blk(C_VAL), blk(C_GATE), pl.BlockSpec((32, 128), lambda bi, t: (0, t))],
        out_specs=[blk(0), blk(0), pl.BlockSpec((1, 32, 128), lambda bi, t: (bi, 0, t))],
        scratch_shapes=[pltpu.VMEM((tp + 32, 128), F32), pltpu.VMEM((tp + 32, 128), F32),
                        pltpu.VMEM((win, 128), F32), pltpu.VMEM((win, 128), F32),
                        pltpu.VMEM((CONV_WIDTH * 8, 128), F32)],
        compiler_params=_params(("parallel", "parallel")),
    )(dhc, u, u, conv_w)


def _gla_bwd(dgla, u, o_pre, states, gup, gbias, gnorm, parts):
    bsz, tp, _ = u.shape
    nchunk = tp // CHUNK
    nb = _gla_group(nchunk)
    ns = len(parts)
    grid = (bsz, 2)

    def body(*refs):
        dy_ref, q_ref, k_ref, v_ref, r_ref, gd_ref, o_ref, st_ref, gup_ref, gb_ref, gn_ref = refs[:11]
        dq_ref, dk_ref, dv_ref, dr_ref, dgd_ref, dgup_ref, vec_ref = refs[11 + ns:18 + ns]
        h_scr, gup_acc = refs[18 + 2 * ns:20 + 2 * ns]
        start, finish = _chip_exchange_steps(refs[11:11 + ns], refs[18 + ns:18 + 2 * ns], refs[20 + 2 * ns:])
        first, last = _grid_ends(grid)
        pl.when(first)(start)
        tril, triu, hmask = _gla_consts(nb)
        h_scr[...] = jnp.zeros_like(h_scr)
        gup_acc[...] = jnp.zeros_like(gup_acc)
        gn = gn_ref[...]
        gupb = gup_ref[...]
        m = nb * CHUNK
        ngroup = nchunk // nb

        def group(i, carry):
            dbias, dgn = carry
            g = ngroup - 1 - i
            rows, valid, z, eb, enb, erest, dec, qe, ke, kd = _gla_group_terms(
                g, nb, q_ref, k_ref, gd_ref, gup_ref, gb_ref, tril)
            keb, kdb = ke.astype(BF16), kd.astype(BF16)
            dqe = jnp.zeros((nb, CHUNK, 128), F32)
            dke = jnp.zeros((nb, CHUNK, 128), F32)
            dkd = jnp.zeros((nb, CHUNK, 128), F32)
            ddec = jnp.zeros((nb, 1, 128), F32)
            for h in range(2):
                cols = slice(h * GLA_DV, (h + 1) * GLA_DV)
                o = o_ref[0, rows, cols]
                rh = r_ref[0, rows, cols]
                dy = dy_ref[0, rows, cols]
                rms = lax.rsqrt(jnp.mean(o * o, axis=-1, keepdims=True) + LN_EPS)
                nrm = o * rms
                sig = _sigmoid(rh)
                sw = rh * sig
                dr_ref[0, rows, cols] = (dy * nrm * gn * (sig * (1.0 + rh * (1.0 - sig)))).astype(BF16)
                dgn = dgn + jnp.sum(dy * nrm * sw, axis=0, keepdims=True)
                dn = dy * gn * sw
                do = rms * (dn - nrm * jnp.mean(dn * nrm, axis=-1, keepdims=True))
                dob = do.astype(BF16).reshape(nb, CHUNK, GLA_DV)
                qh = jnp.where(hmask[h], qe, 0.0).astype(BF16)
                vh = v_ref[0, rows, cols].astype(BF16).reshape(nb, CHUNK, GLA_DV)
                ht = h_scr[h]
                hts = [None] * nb
                for n in reversed(range(nb)):
                    hts[n] = ht
                    ht = dec[n] * ht + _tn(dob[n], qh[n])
                h_scr[h] = ht
                htf = jnp.stack(hts)
                htb = htf.astype(BF16)
                st = st_ref[0, h, pl.ds(g * nb, nb)]
                at = jnp.where(triu, _bnt(keb, qh), 0.0).astype(BF16)
                da = jnp.where(tril, _bnt(dob, vh), 0.0).astype(BF16)
                dat = jnp.where(triu, _bnt(vh, dob), 0.0).astype(BF16)
                dqe = dqe + jnp.where(hmask[h], _bnn(da, keb) + _bnn(dob, st.astype(BF16)), 0.0)
                dke = dke + _bnn(dat, qh)
                dv_ref[0, rows, cols] = (_bnn(at, dob) + _bnt(kdb, htb)).reshape(m, GLA_DV).astype(BF16)
                dkd = dkd + jnp.where(hmask[h], _bnn(vh, htb), 0.0)
                ddec = ddec + jnp.where(hmask[h], jnp.sum(htf * st, axis=1, keepdims=True), 0.0)
            dq_ref[0, rows, :] = (dqe * eb * Q_SCALE).reshape(m, 128).astype(BF16)
            dk_ref[0, rows, :] = (dke * enb + dkd * erest).reshape(m, 128).astype(BF16)
            db = dqe * qe - dke * ke - dkd * kd
            dblast = jnp.sum(dkd * kd, axis=1, keepdims=True) + ddec * dec
            lastrow = lax.broadcasted_iota(jnp.int32, (1, CHUNK, 1), 1) == CHUNK - 1
            db = db + jnp.where(lastrow, dblast, 0.0)
            dlg = _bnn(triu.astype(F32), db, precision=lax.Precision.HIGHEST).reshape(m, 128)
            dz = jnp.where(valid, dlg, 0.0) * (1.0 / GLA_TAU) * (1.0 - _sigmoid(z))
            dzb = dz.astype(BF16)
            dgd_ref[0, 0, rows, :] = _nt(dzb, gupb).astype(BF16)
            gup_acc[...] += _tn(gd_ref[0, rows, :].astype(BF16), dzb)
            return dbias + jnp.sum(dz, axis=0, keepdims=True), dgn

        zero = jnp.zeros((1, 128), F32)
        dbias, dgn = lax.fori_loop(0, ngroup, group, (zero, zero))
        dgup_ref[0] = gup_acc[...]
        vec_ref[0] = jnp.zeros((8, 128), F32)
        vec_ref[0, 0:1, :] = dbias
        vec_ref[0, 1:2, :] = dgn
        pl.when(last)(finish)

    pair = lambda w, off: pl.BlockSpec((1, tp, w), lambda bi, p: (bi, 0, off // w + p))
    res = pl.pallas_call(
        body, name="gla_bwd",
        out_shape=[jax.ShapeDtypeStruct((bsz, tp, 256), BF16), jax.ShapeDtypeStruct((bsz, tp, 256), BF16),
                   jax.ShapeDtypeStruct((bsz, tp, 512), BF16), jax.ShapeDtypeStruct((bsz, tp, 512), BF16),
                   jax.ShapeDtypeStruct((bsz, 2, tp, 128), BF16), jax.ShapeDtypeStruct((bsz, 128, 256), F32),
                   jax.ShapeDtypeStruct((bsz, 8, 256), F32)] + _chip_exchange_shapes(parts),
        grid=grid,
        in_specs=[pair(256, 0), pair(128, C_Q), pair(128, C_K), pair(256, C_V), pair(256, C_R),
                  pl.BlockSpec((1, tp, 128), lambda bi, p: (bi, 0, C_GD // 128)),
                  pair(256, 0),
                  pl.BlockSpec((1, 2, nchunk, GLA_DV, 128), lambda bi, p: (bi, p, 0, 0, 0)),
                  pl.BlockSpec((128, 128), lambda bi, p: (0, p)),
                  pl.BlockSpec((1, 128), lambda bi, p: (0, p)),
                  pl.BlockSpec((1, 128), lambda bi, p: (0, 0))] + [ANY] * ns,
        out_specs=[pair(128, 0), pair(128, 0), pair(256, 0), pair(256, 0),
                   pl.BlockSpec((1, 1, tp, 128), lambda bi, p: (bi, p, 0, 0)),
                   pl.BlockSpec((1, 128, 128), lambda bi, p: (bi, 0, p)),
                   pl.BlockSpec((1, 8, 128), lambda bi, p: (bi, 0, p))] + [ANY] * ns,
        scratch_shapes=[pltpu.VMEM((2, GLA_DV, 128), F32), pltpu.VMEM((128, 128), F32)] + _chip_exchange_sems(ns),
        compiler_params=_params(("arbitrary", "arbitrary")),
    )(dgla, u, u, u, u, u, o_pre, states, gup, gbias, gnorm, *parts)
    return res[:7], res[7:]


_DU_OFFSETS = (C_VAL, C_GATE, C_Q, C_K, C_V, C_R)
_DU_WIDTHS = (512, 512, 256, 256, 512, 512)


def _du_specs(tm, per, row_map):
    specs = [pl.BlockSpec((tm, w), row_map) for w in _DU_WIDTHS]
    for p in range(2):
        specs.append(pl.BlockSpec((1, 1, tm, 128), lambda *ix, p=p: (row_map(*ix)[0] // per, p, row_map(*ix)[0] % per, 0)))
    return specs


def _du_pieces(refs):
    out = [(off, ref[...]) for off, ref in zip(_DU_OFFSETS, refs[:6])]
    dgd = (refs[6][0, 0].astype(F32) + refs[7][0, 0].astype(F32)).astype(BF16)
    out.append((C_GD, dgd))
    return out


def _in_proj_bwd(pieces, dgd, w_int, dt1, tp, parts):
    r, d = dt1.shape
    tm = _row_tile(tp)
    per = tp // tm
    ns = len(parts)
    grid = (r // tm,)

    def body(*refs):
        w_ref, dt_ref = refs[8:10]
        o_ref = refs[10 + ns]
        start, finish = _chip_exchange_steps(refs[10:10 + ns], refs[11 + ns:11 + 2 * ns], refs[11 + 2 * ns:])
        first, last = _grid_ends(grid)
        pl.when(first)(start)
        acc = ALPHA * dt_ref[...]
        for off, val in _du_pieces(refs[:8]):
            acc = acc + _nn(val, w_ref[off:off + val.shape[1], :])
        o_ref[...] = acc
        pl.when(last)(finish)

    row = lambda i: (i, 0)
    res = pl.pallas_call(
        body, name="in_proj_bwd", out_shape=[jax.ShapeDtypeStruct((r, d), F32)] + _chip_exchange_shapes(parts),
        grid=grid,
        in_specs=_du_specs(tm, per, row) + [pl.BlockSpec((D_IN_PAD, d), lambda i: (0, 0)), pl.BlockSpec((tm, d), row)]
        + [ANY] * ns,
        out_specs=[pl.BlockSpec((tm, d), row)] + [ANY] * ns,
        scratch_shapes=_chip_exchange_sems(ns),
        compiler_params=_params(("arbitrary",)),
    )(*pieces, dgd, dgd, w_int, dt1, *parts)
    return res[0], res[1:]


def _grad_w_in(pieces, dgd, s0b, tp):
    r, d = s0b.shape
    tk = _row_tile(tp)
    per = tp // tk

    def body(*refs):
        s_ref, o_ref = refs[8:]

        @pl.when(pl.program_id(0) == 0)
        def _():
            o_ref[...] = jnp.zeros_like(o_ref)

        s = s_ref[...]
        for off, val in _du_pieces(refs[:8]):
            o_ref[off:off + val.shape[1], :] += _tn(val, s)

    row = lambda k: (k, 0)
    return pl.pallas_call(
        body, name="grad_w_in", out_shape=jax.ShapeDtypeStruct((D_IN_PAD, d), F32),
        grid=(r // tk,),
        in_specs=_du_specs(tk, per, row) + [pl.BlockSpec((tk, d), row)],
        out_specs=pl.BlockSpec((D_IN_PAD, d), lambda k: (0, 0)),
        compiler_params=_params(("arbitrary",)),
    )(*pieces, dgd, dgd, s0b)


def _ln_in_bwd(ds0, x, meta, g):
    bsz, s, d = x.shape
    tp = s + HEAD
    nh = 2
    sh = s // nh
    rc = min(256, sh)

    def body(ds_ref, x_ref, meta_ref, g_ref, gx_ref, dm_ref, vec_ref):
        h = pl.program_id(1)
        gg = g_ref[...]

        @pl.when(h == 0)
        def _():
            mh, mr = _ln_stats(meta_ref[...])
            dsm = ds_ref[0, PAD:HEAD, :]
            dm_ref[0] = _ln_bwd(dsm * gg, mh, mr)
            vec_ref[0] = jnp.zeros((8, d), F32)
            vec_ref[0, 0:1, :] = jnp.sum(dsm * mh, axis=0, keepdims=True)
            vec_ref[0, 1:2, :] = jnp.sum(dsm, axis=0, keepdims=True)

        def step(i, carry):
            sg, sb = carry
            dst = pl.ds(pl.multiple_of(i * rc, rc), rc)
            src = pl.ds(pl.multiple_of(HEAD + h * sh + i * rc, 64), rc)
            xh, rstd = _ln_stats(x_ref[0, dst, :])
            dsv = ds_ref[0, src, :]
            gx_ref[0, dst, :] = _ln_bwd(dsv * gg, xh, rstd)
            return sg + jnp.sum(dsv * xh, axis=0, keepdims=True), sb + jnp.sum(dsv, axis=0, keepdims=True)

        zero = jnp.zeros((1, d), F32)
        sg, sb = lax.fori_loop(0, sh // rc, step, (zero, zero))
        vec_ref[0, 0:1, :] += sg
        vec_ref[0, 1:2, :] += sb

    return pl.pallas_call(
        body, name="ln_in_bwd",
        out_shape=[jax.ShapeDtypeStruct((bsz, s, d), F32), jax.ShapeDtypeStruct((bsz, N_META, d), F32),
                   jax.ShapeDtypeStruct((bsz, 8, d), F32)],
        grid=(bsz, nh),
        in_specs=[pl.BlockSpec((1, tp, d), lambda bi, hi: (bi, 0, 0)),
                  pl.BlockSpec((1, sh, d), lambda bi, hi: (bi, hi, 0)),
                  pl.BlockSpec((N_META, d), lambda bi, hi: (0, 0)),
                  pl.BlockSpec((1, d), lambda bi, hi: (0, 0))],
        out_specs=[pl.BlockSpec((1, sh, d), lambda bi, hi: (bi, hi, 0)),
                   pl.BlockSpec((1, N_META, d), lambda bi, hi: (bi, 0, 0)),
                   pl.BlockSpec((1, 8, d), lambda bi, hi: (bi, 0, 0))],
        compiler_params=_params(("parallel", "arbitrary")),
    )(ds0, x, meta, g)


def _rows128(a):
    return a.reshape(-1, 128)


def kernel(x, meta_tokens, ln_in_g, ln_in_b, w_in, conv_w, conv_b, conv_ln_g, conv_ln_b, gate_up, gate_bias, gla_norm_g, w_out, ln1_g, ln1_b, w_ff1, w_ff2, ln2_g, ln2_b, loss_target, m_meta_tokens, m_ln_in_g, m_ln_in_b, m_w_in, m_conv_w, m_conv_b, m_conv_ln_g, m_conv_ln_b, m_gate_up, m_gate_bias, m_gla_norm_g, m_w_out, m_ln1_g, m_ln1_b, m_w_ff1, m_w_ff2, m_ln2_g, m_ln2_b, v_meta_tokens, v_ln_in_g, v_ln_in_b, v_w_in, v_conv_w, v_conv_b, v_conv_ln_g, v_conv_ln_b, v_gate_up, v_gate_bias, v_gla_norm_g, v_w_out, v_ln1_g, v_ln1_b, v_w_ff1, v_w_ff2, v_ln2_g, v_ln2_b):
    bsz, seq, d = x.shape
    tp = seq + HEAD
    r = bsz * tp
    xi, yi, ci = _mesh_pos()
    chip = 2 * xi + yi
    c_arr = jnp.reshape(ci, (1,)).astype(jnp.int32)
    pos_arr = jnp.stack([chip, ci]).astype(jnp.int32)

    sh_in = D_IN // 4
    shard_in = jnp.pad(w_in[0].T.astype(BF16), ((0, D_IN_PAD // 4 - sh_in), (0, 0)))
    shard_w1t, shard_wout, shard_w2 = w_ff1[0].T.astype(BF16), w_out[0].astype(BF16), w_ff2[0].astype(BF16)
    small_w = jnp.concatenate([_rows128(meta_tokens), _rows128(conv_w[0]), _rows128(gate_up[0])], axis=0)
    (g_int, g_wout), g_small = _gather_weights([shard_in, shard_wout], small_w)
    w_int = jnp.pad(g_int[:, :sh_in].reshape(D_IN, d), ((0, D_IN_PAD - D_IN), (0, 0)))
    wout = g_wout.reshape(d, d)
    n_meta_rows, n_cw_rows = N_META * 256 // 128, CONV_WIDTH * 128 // 128
    meta_full = jnp.concatenate([g_small[j, :n_meta_rows].reshape(N_META, 256) for j in range(4)], axis=1)
    convw_full = jnp.concatenate(
        [g_small[j, n_meta_rows:n_meta_rows + n_cw_rows].reshape(CONV_WIDTH, 128) for j in range(4)], axis=1)
    gup_full = jnp.concatenate(
        [g_small[j, n_meta_rows + n_cw_rows:].reshape(GLA_RANK, 64) for j in range(4)], axis=1)
    convw_p = jnp.pad(convw_full, ((0, 1), (0, 0)))
    gup_p = jnp.pad(gup_full, ((0, 128 - GLA_RANK), (0, 0))).astype(BF16)
    ln_in_g2, ln_in_b2 = ln_in_g.reshape(1, d), ln_in_b.reshape(1, d)

    s0, s0b = _ln_in_fwd(x, meta_full, ln_in_g2, ln_in_b2)
    s0f, s0bf = s0.reshape(r, d), s0b.reshape(r, d)
    u = _in_proj(s0bf, w_int)
    u3 = u.reshape(bsz, tp, D_IN_PAD)
    hc = _conv_fwd(u3, convw_p, conv_b).reshape(r, D_CONV)
    gla_out, o_pre, states, (g_w1t,) = _gla_fwd(u3, gup_p, gate_bias, gla_norm_g, [shard_w1t])
    w1t = g_w1t.reshape(D_FF, d)
    gla_of = gla_out.reshape(r, 512)
    conv_of, xhat1, rstd1, s1b = _out_proj_ln1(hc, gla_of, wout, s0f, conv_ln_g, conv_ln_b, ln1_g, ln1_b)
    ra, (g_w2,) = _ffn1(s1b, w1t, [shard_w2])
    w2 = g_w2.reshape(D_FF, d)
    dt2, dt2b, acc2 = _ffn2_ln2_loss(ra, w2, xhat1, ln1_g, ln1_b, ln2_g, ln2_b, loss_target, tp)

    def pair_reduce(gs):
        got = _pair_exchange(gs)
        return got, [_add_pair(g, t, c_arr) for g, t in zip(gs, got)]

    da = _ffn_bwd_da(dt2b, w2, ra)
    dt1, dt1b, acc1 = _ffn_bwd_ln1(da, w1t, dt2, xhat1, rstd1, ln1_g)
    g_w2 = _matmul_tn(ra, dt2b, 1024, square_lhs=True, name="grad_w_ff2")
    g_w1t = _matmul_tn(da, s1b, 1024, name="grad_w_ff1")
    big_ffn = [g_w1t.reshape(4, D_FF // 4, d), g_w2.reshape(4, D_FF // 4, d)]
    pair_got_ffn, parts_ffn = pair_reduce(big_ffn)
    dhc, dgla, cacc = _out_proj_bwd(dt1b, wout, hc, conv_ln_g, conv_ln_b)
    g_wout = jnp.concatenate([_matmul_tn(conv_of, dt1b, D_CONV, name="grad_w_out_conv"),
                              _matmul_tn(gla_of, dt1b, 512, name="grad_w_out_gla")], axis=0)
    dcv, dcg, dcw = _conv_bwd(dhc.reshape(bsz, tp, D_CONV), u3, convw_p)
    (dq, dk, dv, dr, dgd, dgup, gvec), chip_got_ffn = _gla_bwd(
        dgla.reshape(bsz, tp, 512), u3, o_pre, states, gup_p, gate_bias, gla_norm_g, parts_ffn)
    pieces = [a.reshape(r, a.shape[-1]) for a in (dcv, dcg, dq, dk, dv, dr)]
    g_wint = _grad_w_in(pieces, dgd, s0bf, tp)
    big_io = [jnp.stack([g_wint[j * sh_in:(j + 1) * sh_in] for j in range(4)]), g_wout.reshape(4, d // 4, d)]
    pair_got_io, parts_io = pair_reduce(big_io)
    ds0, chip_got_io = _in_proj_bwd(pieces, dgd, w_int, dt1, tp, parts_io)
    grad_x, dmeta, lvec = _ln_in_bwd(ds0.reshape(bsz, tp, d), x, meta_full, ln_in_g2)

    small = {
        "loss": acc2[0:1], "ln_in_g": jnp.sum(lvec[:, 0], axis=0), "ln_in_b": jnp.sum(lvec[:, 1], axis=0),
        "conv_b": cacc[0], "conv_ln_g": cacc[1], "conv_ln_b": cacc[2], "gate_bias": jnp.sum(gvec[:, 0], axis=0),
        "gla_norm_g": jnp.sum(gvec[:, 1].reshape(bsz * 2, 128), axis=0),
        "ln1_g": acc1[0], "ln1_b": acc1[1], "ln2_g": acc2[1], "ln2_b": acc2[2],
        "conv_w": jnp.sum(dcw, axis=0)[:CONV_WIDTH], "gate_up": jnp.sum(dgup, axis=0)[:GLA_RANK],
        "meta_tokens": jnp.sum(dmeta, axis=0),
    }
    pack = jnp.concatenate([_rows128(small[k]) for k, _ in _SMALL_FIELDS], axis=0)
    pack = jnp.pad(pack, ((0, SMALL_ROWS - pack.shape[0]), (0, 0)))
    red = _sum_small(_allgather_small(pack))
    off, tot = {}, 0
    for k, nrows in _SMALL_FIELDS:
        off[k] = (tot, nrows)
        tot += nrows

    def field(k, shape):
        o, nrows = off[k]
        return red[o:o + nrows].reshape(shape)

    loss = jnp.sum(field("loss", (d,)))

    halves = [_add_chips(g, t, s, pos_arr) for g, t, s in zip(
        big_io + big_ffn, list(pair_got_io) + list(pair_got_ffn), list(chip_got_io) + list(chip_got_ffn))]
    f_wint, f_wout, f_w1t, f_w2 = [
        jnp.concatenate([jnp.where(ci == 0, own, sib), jnp.where(ci == 0, sib, own)], axis=1)
        for own, sib in zip(halves, _pair_swap(halves))]

    def cols(a, width):
        return lax.dynamic_slice_in_dim(a, chip * width, width, axis=a.ndim - 1)

    grads = {
        "meta_tokens": cols(field("meta_tokens", (N_META, d)), 256),
        "ln_in_g": field("ln_in_g", (d,)), "ln_in_b": field("ln_in_b", (d,)),
        "w_in": f_wint.T[None], "conv_w": cols(field("conv_w", (CONV_WIDTH, D_CONV)), 128)[None],
        "conv_b": field("conv_b", (1, D_CONV)), "conv_ln_g": field("conv_ln_g", (1, D_CONV)),
        "conv_ln_b": field("conv_ln_b", (1, D_CONV)),
        "gate_up": cols(field("gate_up", (GLA_RANK, 256)), 64)[None],
        "gate_bias": field("gate_bias", (1, 256)), "gla_norm_g": field("gla_norm_g", (1, 128)),
        "w_out": f_wout[None], "ln1_g": field("ln1_g", (1, d)), "ln1_b": field("ln1_b", (1, d)),
        "w_ff1": f_w1t.T[None], "w_ff2": f_w2[None],
        "ln2_g": field("ln2_g", (1, d)), "ln2_b": field("ln2_b", (1, d)),
    }
    weights = dict(meta_tokens=meta_tokens, ln_in_g=ln_in_g, ln_in_b=ln_in_b, w_in=w_in, conv_w=conv_w, conv_b=conv_b,
                   conv_ln_g=conv_ln_g, conv_ln_b=conv_ln_b, gate_up=gate_up, gate_bias=gate_bias,
                   gla_norm_g=gla_norm_g, w_out=w_out, ln1_g=ln1_g, ln1_b=ln1_b, w_ff1=w_ff1, w_ff2=w_ff2,
                   ln2_g=ln2_g, ln2_b=ln2_b)
    moms = dict(meta_tokens=(m_meta_tokens, v_meta_tokens), ln_in_g=(m_ln_in_g, v_ln_in_g),
                ln_in_b=(m_ln_in_b, v_ln_in_b), w_in=(m_w_in, v_w_in), conv_w=(m_conv_w, v_conv_w),
                conv_b=(m_conv_b, v_conv_b), conv_ln_g=(m_conv_ln_g, v_conv_ln_g),
                conv_ln_b=(m_conv_ln_b, v_conv_ln_b), gate_up=(m_gate_up, v_gate_up),
                gate_bias=(m_gate_bias, v_gate_bias), gla_norm_g=(m_gla_norm_g, v_gla_norm_g),
                w_out=(m_w_out, v_w_out), ln1_g=(m_ln1_g, v_ln1_g), ln1_b=(m_ln1_b, v_ln1_b),
                w_ff1=(m_w_ff1, v_w_ff1), w_ff2=(m_w_ff2, v_w_ff2), ln2_g=(m_ln2_g, v_ln2_g),
                ln2_b=(m_ln2_b, v_ln2_b))
    names = list(weights)
    big_names = ("w_in", "w_out", "w_ff1", "w_ff2")
    delta, new_m, new_v = {}, {}, {}
    dl, mn, vn = _adamw(w_in[0].T, f_wint, m_w_in[0].T, v_w_in[0].T)
    delta["w_in"], new_m["w_in"], new_v["w_in"] = dl.T[None], mn.T[None], vn.T[None]
    for k in big_names[1:]:
        shp = weights[k].shape
        two = lambda a: a.reshape(shp[-2], shp[-1])
        dl, mn, vn = _adamw(two(weights[k]), two(grads[k]), two(moms[k][0]), two(moms[k][1]))
        delta[k], new_m[k], new_v[k] = dl.reshape(shp), mn.reshape(shp), vn.reshape(shp)
    small_names = [k for k in names if k not in big_names]
    sizes = [weights[k].size for k in small_names]
    total = sum(sizes)
    padded = -(-total // 1024) * 1024

    def flat(get):
        v = jnp.concatenate([get(k).reshape(-1) for k in small_names])
        return jnp.pad(v, (0, padded - total)).reshape(padded // 128, 128)

    ones = jnp.ones((padded - total,), F32)
    vflat = jnp.concatenate([jnp.concatenate([moms[k][1].reshape(-1) for k in small_names]), ones])
    dl, mn, vn = _adamw(flat(lambda k: weights[k]), flat(lambda k: grads[k]), flat(lambda k: moms[k][0]),
                        vflat.reshape(padded // 128, 128))
    pos = 0
    for k, sz in zip(small_names, sizes):
        shp = weights[k].shape
        delta[k] = dl.reshape(-1)[pos:pos + sz].reshape(shp)
        new_m[k] = mn.reshape(-1)[pos:pos + sz].reshape(shp)
        new_v[k] = vn.reshape(-1)[pos:pos + sz].reshape(shp)
        pos += sz

    return (loss, grad_x, *[grads[k] for k in names], *[delta[k] for k in names],
            *[new_m[k] for k in names], *[new_v[k] for k in names])
```

```python
import functools

import jax
import jax.numpy as jnp
from jax import lax
from jax.experimental import pallas as pl
from jax.experimental.pallas import tpu as pltpu

F32 = jnp.float32
BF16 = jnp.bfloat16

D_MODEL = 1024
N_META = 16
D_CONV = 512
CONV_WIDTH = 31
GLA_HEADS = 4
GLA_DV = 128
GLA_DK = 64
GLA_RANK = 16
GLA_TAU = 16.0
CHUNK = 64
D_FF = 4096
LN_EPS = 1e-5
ALPHA = 2.0 ** 0.25
D_IN = 2576
D_IN_PAD = 2688
PAD = CHUNK - N_META
HEAD = PAD + N_META
Q_SCALE = GLA_DK ** -0.5
ADAM_LR, ADAM_B1, ADAM_B2, ADAM_EPS, ADAM_WD, ADAM_STEP = 0.001, 0.9, 0.999, 1e-08, 0.01, 10
HALF = D_MODEL // 2
VMEM_LIMIT = 56 * 1024 * 1024
MESH = pl.DeviceIdType.MESH

C_VAL, C_GATE, C_Q, C_K, C_V, C_R, C_GD = 0, 512, 1024, 1280, 1536, 2048, 2560

_SMALL_FIELDS = (("loss", 8), ("ln_in_g", 8), ("ln_in_b", 8), ("conv_b", 4), ("conv_ln_g", 4), ("conv_ln_b", 4),
                 ("gate_bias", 2), ("gla_norm_g", 1), ("ln1_g", 8), ("ln1_b", 8), ("ln2_g", 8), ("ln2_b", 8),
                 ("conv_w", 124), ("gate_up", 32), ("meta_tokens", 128))
SMALL_ROWS = 360


def _params(sem=None, **kw):
    return pltpu.CompilerParams(dimension_semantics=sem, vmem_limit_bytes=VMEM_LIMIT, **kw)


def _row_tile(tp):
    for t in (704, 352, 192, 64):
        if tp % t == 0:
            return t
    raise ValueError(tp)


def _sub_rows(tm):
    return [slice(0, tm)]


def _dot(a, b, dims, precision=None):
    return lax.dot_general(a, b, (dims, ((), ())), preferred_element_type=F32, precision=precision)


def _nn(a, b, **kw):
    return _dot(a, b, ((1,), (0,)), **kw)


def _nt(a, b, **kw):
    return _dot(a, b, ((1,), (1,)), **kw)


def _tn(a, b, **kw):
    return _dot(a, b, ((0,), (0,)), **kw)


def _sigmoid(x):
    return 1.0 / (1.0 + jnp.exp(-x))


def _log_sigmoid(z):
    return jnp.minimum(z, 0.0) - jnp.log(1.0 + jnp.exp(-jnp.abs(z)))


def _ln_stats(t):
    mu = jnp.mean(t, axis=-1, keepdims=True)
    d = t - mu
    var = jnp.mean(d * d, axis=-1, keepdims=True)
    rstd = lax.rsqrt(var + LN_EPS)
    return d * rstd, rstd


def _ln_bwd(dxhat, xhat, rstd):
    m1 = jnp.mean(dxhat, axis=-1, keepdims=True)
    m2 = jnp.mean(dxhat * xhat, axis=-1, keepdims=True)
    return rstd * (dxhat - m1 - xhat * m2)


def _mesh_pos():
    return lax.axis_index("x"), lax.axis_index("y"), lax.axis_index("c")


ANY = pl.BlockSpec(memory_space=pl.ANY)


def _gather_sems(n):
    return [pltpu.SemaphoreType.DMA((n, 3))] * 4


def _gather_steps(ins, outs, sems):
    n = len(ins)
    send, recv, fsend, frecv = sems
    half = [ref.shape[0] // 2 for ref in ins]
    x, y, c = _mesh_pos()
    me = 2 * x + y
    sibling = (x, y, 1 - c)
    chips = [(1 - x, y), (x, 1 - y), (1 - x, 1 - y)]
    chip_idx = [2 * px + py for px, py in chips]
    mine = [pl.ds(pl.multiple_of(c * h, 16), h) for h in half]
    other = [pl.ds(pl.multiple_of((1 - c) * h, 16), h) for h in half]
    pairs = [(a, k) for a in range(n) for k in range(3)]

    def ici(a, k, slab):
        return pltpu.make_async_remote_copy(
            src_ref=ins[a].at[mine[a], :], dst_ref=outs[a].at[slab, mine[a], :],
            send_sem=send.at[a, k], recv_sem=recv.at[a, k], device_id=(*chips[k], c), device_id_type=MESH)

    def forward(a, k, rows):
        blk = outs[a].at[chip_idx[k], rows[a], :]
        return pltpu.make_async_remote_copy(
            src_ref=blk, dst_ref=blk, send_sem=fsend.at[a, k], recv_sem=frecv.at[a, k],
            device_id=sibling, device_id_type=MESH)

    def start():
        for a, k in pairs:
            ici(a, k, me).start()

    def relay():
        for a, k in pairs:
            ici(a, k, chip_idx[k]).wait_recv()
            forward(a, k, mine).start()

    def finish():
        for a, k in pairs:
            forward(a, k, other).wait_recv()
        for a, k in pairs:
            ici(a, k, me).wait_send()
            forward(a, k, mine).wait_send()

    return start, relay, finish


def _place_own(gathered, shards):
    chip = 2 * lax.axis_index("x") + lax.axis_index("y")
    return [lax.dynamic_update_slice(g, s[None], (chip, 0, 0)) for g, s in zip(gathered, shards)]


def _gather_weights(shards, small):
    n = len(shards)

    def body(*refs):
        ins, small_in = refs[:n], refs[n]
        outs, small_out = refs[n + 1:2 * n + 1], refs[2 * n + 1]
        ssend, srecv = refs[2 * n + 6:]
        start, relay, finish = _gather_steps(ins, outs, refs[2 * n + 2:2 * n + 6])
        x, y, c = _mesh_pos()
        chips = [(1 - x, y), (x, 1 - y), (1 - x, 1 - y)]

        def small_copy(k, slot):
            return pltpu.make_async_remote_copy(
                src_ref=small_in, dst_ref=small_out.at[slot], send_sem=ssend.at[k], recv_sem=srecv.at[k],
                device_id=(*chips[k], c), device_id_type=MESH)

        start()
        for k in range(3):
            small_copy(k, 2 * x + y).start()
        relay()
        finish()
        for k, (px, py) in enumerate(chips):
            small_copy(k, 2 * px + py).wait_recv()
        for k in range(3):
            small_copy(k, 2 * x + y).wait_send()

    out_shape = [jax.ShapeDtypeStruct((4,) + s.shape, s.dtype) for s in shards]
    out_shape.append(jax.ShapeDtypeStruct((4,) + small.shape, small.dtype))
    res = pl.pallas_call(
        body, name="gather_weights", out_shape=out_shape,
        in_specs=[ANY] * (n + 1), out_specs=[ANY] * (n + 1),
        scratch_shapes=_gather_sems(n) + [pltpu.SemaphoreType.DMA((3,)), pltpu.SemaphoreType.DMA((3,))],
    )(*shards, small)
    res = _place_own(res, list(shards) + [small])
    return res[:n], res[n]


def _pair_exchange_steps(ins, outs, sems):
    send, recv = sems
    x, y, c = _mesh_pos()
    other = pl.ds(pl.multiple_of((1 - c) * HALF, 128), HALF)
    cps = [pltpu.make_async_remote_copy(
        src_ref=ins[a].at[:, :, other], dst_ref=outs[a], send_sem=send.at[a], recv_sem=recv.at[a],
        device_id=(x, y, 1 - c), device_id_type=MESH) for a in range(len(ins))]

    def start():
        for cp in cps:
            cp.start()

    def finish():
        for cp in cps:
            cp.wait()

    return start, finish


def _pair_exchange_shapes(grads):
    return [jax.ShapeDtypeStruct(g.shape[:2] + (HALF,), g.dtype) for g in grads]


def _pair_exchange(grads):
    n = len(grads)

    def body(*refs):
        start, finish = _pair_exchange_steps(refs[:n], refs[n:2 * n], refs[2 * n:])
        start()
        finish()

    return pl.pallas_call(
        body, name="grad_pair_exchange", out_shape=_pair_exchange_shapes(grads),
        in_specs=[ANY] * n, out_specs=[ANY] * n,
        scratch_shapes=[pltpu.SemaphoreType.DMA((n,)), pltpu.SemaphoreType.DMA((n,))],
    )(*grads)


NO_FUSE = ("none", ())


def _fused_call(body, fuse, *, name, grid, in_specs, out_specs, out_shape, scratch_shapes, operands):
    kind, comm = fuse
    comm = list(comm)
    nc, n_in, n_out, n_s = len(comm), len(in_specs), len(out_shape), len(scratch_shapes)
    if nc == 0:
        res = pl.pallas_call(
            body, name=name, grid=grid, in_specs=in_specs, out_specs=out_specs, out_shape=out_shape,
            scratch_shapes=scratch_shapes, compiler_params=_params(("arbitrary",) * len(grid)))(*operands)
        return list(res), []
    steps_fn, shapes, sems = {
        "gather": (_gather_steps, [jax.ShapeDtypeStruct((4,) + s.shape, s.dtype) for s in comm], _gather_sems(nc)),
        "exchange": (_chip_exchange_steps, _chip_exchange_shapes(comm), _chip_exchange_sems(nc)),
        "pair": (_pair_exchange_steps, _pair_exchange_shapes(comm),
                 [pltpu.SemaphoreType.DMA((nc,)), pltpu.SemaphoreType.DMA((nc,))]),
    }[kind]

    def wrapped(*refs):
        o0 = n_in + nc
        s0 = o0 + n_out + nc
        steps = steps_fn(refs[n_in:o0], refs[o0 + n_out:s0], refs[s0 + n_s:])
        first, last = _grid_ends(grid)
        pl.when(first)(steps[0])
        for mid in steps[1:-1]:
            pl.when(last)(mid)
        body(*refs[:n_in], *refs[o0:o0 + n_out], *refs[s0:s0 + n_s])
        pl.when(last)(steps[-1])

    res = pl.pallas_call(
        wrapped, name=name, grid=grid, in_specs=list(in_specs) + [ANY] * nc, out_specs=list(out_specs) + [ANY] * nc,
        out_shape=list(out_shape) + shapes, scratch_shapes=list(scratch_shapes) + sems,
        compiler_params=_params(("arbitrary",) * len(grid)))(*operands, *comm)
    outs, got = list(res[:n_out]), list(res[n_out:])
    return outs, (_place_own(got, comm) if kind == "gather" else got)


def _chip_exchange_sems(n):
    return [pltpu.SemaphoreType.DMA((n, 3))] * 2


def _chip_exchange_shapes(parts):
    return [jax.ShapeDtypeStruct((3,) + p.shape[1:], p.dtype) for p in parts]


def _chip_exchange_steps(ins, outs, sems):
    send, recv = sems
    x, y, c = _mesh_pos()
    chips = [(1 - x, y), (x, 1 - y), (1 - x, 1 - y)]
    cps = [pltpu.make_async_remote_copy(
        src_ref=ins[a].at[2 * px + py], dst_ref=outs[a].at[k], send_sem=send.at[a, k], recv_sem=recv.at[a, k],
        device_id=(px, py, c), device_id_type=MESH) for a in range(len(ins)) for k, (px, py) in enumerate(chips)]

    def start():
        for cp in cps:
            cp.start()

    def finish():
        for cp in cps:
            cp.wait()

    return start, finish


def _chip_exchange(parts):
    n = len(parts)

    def body(*refs):
        start, finish = _chip_exchange_steps(refs[:n], refs[n:2 * n], refs[2 * n:])
        start()
        finish()

    return pl.pallas_call(
        body, name="grad_chip_exchange", out_shape=_chip_exchange_shapes(parts),
        in_specs=[ANY] * n, out_specs=[ANY] * n, scratch_shapes=_chip_exchange_sems(n),
    )(*parts)


def _pair_swap(halves):
    n = len(halves)

    def body(*refs):
        ins, outs = refs[:n], refs[n:2 * n]
        send, recv = refs[2 * n:]
        x, y, c = _mesh_pos()
        cps = [pltpu.make_async_remote_copy(
            src_ref=ins[a], dst_ref=outs[a], send_sem=send.at[a], recv_sem=recv.at[a],
            device_id=(x, y, 1 - c), device_id_type=MESH) for a in range(n)]
        for cp in cps:
            cp.start()
        for cp in cps:
            cp.wait()

    return pl.pallas_call(
        body, name="grad_pair_swap",
        out_shape=[jax.ShapeDtypeStruct(h.shape, h.dtype) for h in halves],
        in_specs=[ANY] * n, out_specs=[ANY] * n,
        scratch_shapes=[pltpu.SemaphoreType.DMA((n,)), pltpu.SemaphoreType.DMA((n,))],
    )(*halves)


def _allgather_small(pack):
    m_per, ncol = pack.shape

    def body(x_ref, out_ref, send_sems, recv_sems, local_sem):
        x, y, c = _mesh_pos()
        me, sibling = (x, y, c), (x, y, 1 - c)
        chips = [(1 - x, y), (x, 1 - y), (1 - x, 1 - y)]

        def rows(px, py, pc):
            return out_ref.at[pl.ds(pl.multiple_of((4 * px + 2 * py + pc) * m_per, 8), m_per), :]

        def copy(k, block, to, src=None):
            return pltpu.make_async_remote_copy(
                src_ref=rows(*block) if src is None else src, dst_ref=rows(*block),
                send_sem=send_sems.at[k], recv_sem=recv_sems.at[k], device_id=to, device_id_type=MESH)

        mine = pltpu.make_async_copy(x_ref, rows(*me), local_sem)
        mine.start()
        first = [copy(0, me, sibling, src=x_ref)]
        first += [copy(1 + j, me, (*chip, c), src=x_ref) for j, chip in enumerate(chips)]
        for cp in first:
            cp.start()
        passed = [copy(4 + j, (*chip, c), sibling) for j, chip in enumerate(chips)]
        for j, chip in enumerate(chips):
            copy(1 + j, (*chip, c), me).wait_recv()
            passed[j].start()
        copy(0, sibling, me).wait_recv()
        for j, chip in enumerate(chips):
            copy(4 + j, (*chip, 1 - c), me).wait_recv()
        for cp in first + passed:
            cp.wait_send()
        mine.wait()

    return pl.pallas_call(
        body, name="allgather_small",
        out_shape=jax.ShapeDtypeStruct((8 * m_per, ncol), pack.dtype),
        in_specs=[pl.BlockSpec(memory_space=pltpu.VMEM)],
        out_specs=pl.BlockSpec(memory_space=pltpu.VMEM),
        scratch_shapes=[pltpu.SemaphoreType.DMA((7,)), pltpu.SemaphoreType.DMA((7,)), pltpu.SemaphoreType.DMA],
    )(pack)


def _add_pair(g, got, c_arr):
    _, rows, _ = g.shape

    def body(c_ref, g_ref, r_ref, o_ref):
        o_ref[...] = (g_ref[...] + r_ref[...]).astype(BF16)

    return pl.pallas_call(
        body, name="grad_add_pair", out_shape=jax.ShapeDtypeStruct((4, rows, HALF), BF16),
        grid_spec=pltpu.PrefetchScalarGridSpec(
            num_scalar_prefetch=1, grid=(4,),
            in_specs=[pl.BlockSpec((1, rows, HALF), lambda j, c: (j, 0, c[0])),
                      pl.BlockSpec((1, rows, HALF), lambda j, c: (j, 0, 0))],
            out_specs=pl.BlockSpec((1, rows, HALF), lambda j, c: (j, 0, 0))),
        compiler_params=_params(("arbitrary",)),
    )(c_arr, g, got)


def _add_chips(g, pair_got, chip_got, pos_arr):
    _, rows, _ = g.shape

    def body(pos_ref, g_ref, p_ref, r_ref, o_ref):
        own = g_ref[0] + p_ref[0]
        o_ref[...] = ((own + r_ref[0].astype(F32)) + r_ref[1].astype(F32)) + r_ref[2].astype(F32)

    return pl.pallas_call(
        body, name="grad_add_chips", out_shape=jax.ShapeDtypeStruct((rows, HALF), F32),
        grid_spec=pltpu.PrefetchScalarGridSpec(
            num_scalar_prefetch=1, grid=(1,),
            in_specs=[pl.BlockSpec((1, rows, HALF), lambda i, p: (p[0], 0, p[1])),
                      pl.BlockSpec((1, rows, HALF), lambda i, p: (p[0], 0, 0)),
                      pl.BlockSpec((3, rows, HALF), lambda i, p: (0, 0, 0))],
            out_specs=pl.BlockSpec((rows, HALF), lambda i, p: (0, 0))),
        compiler_params=_params(("arbitrary",)),
    )(pos_arr, g, pair_got, chip_got)


def _sum_small(gathered):
    def body(g_ref, o_ref):
        acc = g_ref[0:SMALL_ROWS, :]
        for d in range(1, 8):
            acc = acc + g_ref[d * SMALL_ROWS:(d + 1) * SMALL_ROWS, :]
        o_ref[...] = acc

    return pl.pallas_call(
        body, name="sum_small", out_shape=jax.ShapeDtypeStruct((SMALL_ROWS, 128), F32),
    )(gathered)


def _adamw(w, g, m, v):
    rows, cols = w.shape
    tr = 256 if rows % 256 == 0 else rows
    c1 = 1.0 - ADAM_B1 ** ADAM_STEP
    c2 = 1.0 - ADAM_B2 ** ADAM_STEP

    def body(w_ref, g_ref, m_ref, v_ref, d_ref, mo_ref, vo_ref):
        gg = g_ref[...]
        mn = ADAM_B1 * m_ref[...] + (1.0 - ADAM_B1) * gg
        vn = ADAM_B2 * v_ref[...] + (1.0 - ADAM_B2) * (gg * gg)
        d_ref[...] = -ADAM_LR * ((mn / c1) / (jnp.sqrt(vn / c2) + ADAM_EPS) + ADAM_WD * w_ref[...])
        mo_ref[...] = mn
        vo_ref[...] = vn

    spec = pl.BlockSpec((tr, cols), lambda i: (i, 0))
    return pl.pallas_call(
        body, name="adamw", out_shape=[jax.ShapeDtypeStruct(w.shape, F32)] * 3,
        grid=(rows // tr,), in_specs=[spec] * 4, out_specs=[spec] * 3,
        compiler_params=_params(("parallel",)),
    )(w, g, m, v)


def _ln_in_fwd(x, meta, g, b):
    bsz, s, d = x.shape
    tp = s + HEAD
    nh = 2
    sh = s // nh
    rc = min(256, sh)

    def body(x_ref, meta_ref, g_ref, b_ref, s0_ref, s0b_ref):
        h = pl.program_id(1)
        gg, bb = g_ref[...], b_ref[...]

        @pl.when(h == 0)
        def _():
            s0_ref[0, 0:PAD, :] = jnp.zeros((PAD, d), F32)
            s0b_ref[0, 0:PAD, :] = jnp.zeros((PAD, d), BF16)
            mh, _ = _ln_stats(meta_ref[...])
            mv = mh * gg + bb
            s0_ref[0, PAD:HEAD, :] = mv
            s0b_ref[0, PAD:HEAD, :] = mv.astype(BF16)

        def step(i, carry):
            src = pl.ds(pl.multiple_of(i * rc, rc), rc)
            dst = pl.ds(pl.multiple_of(HEAD + h * sh + i * rc, 64), rc)
            xh, _ = _ln_stats(x_ref[0, src, :])
            val = xh * gg + bb
            s0_ref[0, dst, :] = val
            s0b_ref[0, dst, :] = val.astype(BF16)
            return carry

        lax.fori_loop(0, sh // rc, step, 0)

    full = lambda bi, hi: (bi, 0, 0)
    return pl.pallas_call(
        body, name="ln_in_fwd",
        out_shape=[jax.ShapeDtypeStruct((bsz, tp, d), F32), jax.ShapeDtypeStruct((bsz, tp, d), BF16)],
        grid=(bsz, nh),
        in_specs=[pl.BlockSpec((1, sh, d), lambda bi, hi: (bi, hi, 0)),
                  pl.BlockSpec((N_META, d), lambda bi, hi: (0, 0)),
                  pl.BlockSpec((1, d), lambda bi, hi: (0, 0)),
                  pl.BlockSpec((1, d), lambda bi, hi: (0, 0))],
        out_specs=[pl.BlockSpec((1, tp, d), full)] * 2,
        compiler_params=_params(("parallel", "arbitrary")),
    )(x, meta, g, b)


def _in_proj(s0b, w_int, fuse=NO_FUSE):
    r, d = s0b.shape
    tm = _row_tile(r)

    def body(a_ref, w_ref, o_ref):
        o_ref[...] = _nt(a_ref[...], w_ref[...])

    (u,), got = _fused_call(
        body, fuse, name="in_proj", out_shape=[jax.ShapeDtypeStruct((r, D_IN_PAD), F32)],
        grid=(r // tm,),
        in_specs=[pl.BlockSpec((tm, d), lambda i: (i, 0)), pl.BlockSpec((D_IN_PAD, d), lambda i: (0, 0))],
        out_specs=[pl.BlockSpec((tm, D_IN_PAD), lambda i: (i, 0))], scratch_shapes=[], operands=(s0b, w_int))
    return u, got


def _conv_fwd(u, conv_w, conv_b, fuse=NO_FUSE):
    bsz, tp, _ = u.shape
    nchunk = tp // CHUNK
    win = CHUNK + 32
    nct = D_CONV // 128

    def body(cv_ref, cg_ref, w_ref, cb_ref, hc_ref, h_scr, win_scr):
        h_scr[0:32, :] = jnp.zeros((32, 128), F32)
        h_scr[32:32 + tp, :] = cv_ref[0] * _sigmoid(cg_ref[0])
        cb = cb_ref[...]

        def step(n, carry):
            r0 = pl.multiple_of(n * CHUNK, CHUNK)
            win_scr[...] = h_scr[pl.ds(r0, win), :]
            acc = jnp.zeros((CHUNK, 128), F32)
            for j in range(CONV_WIDTH):
                acc = acc + w_ref[j:j + 1, :] * win_scr[2 + j:2 + j + CHUNK, :]
            hc_ref[0, pl.ds(r0, CHUNK), :] = acc + cb
            return carry

        lax.fori_loop(0, nchunk, step, 0)

    (hc,), got = _fused_call(
        body, fuse, name="conv_fwd", out_shape=[jax.ShapeDtypeStruct((bsz, tp, D_CONV), F32)],
        grid=(bsz, nct),
        in_specs=[pl.BlockSpec((1, tp, 128), lambda bi, t: (bi, 0, C_VAL // 128 + t)),
                  pl.BlockSpec((1, tp, 128), lambda bi, t: (bi, 0, C_GATE // 128 + t)),
                  pl.BlockSpec((32, 128), lambda bi, t: (0, t)),
                  pl.BlockSpec((1, 128), lambda bi, t: (0, t))],
        out_specs=[pl.BlockSpec((1, tp, 128), lambda bi, t: (bi, 0, t))],
        scratch_shapes=[pltpu.VMEM((tp + 32, 128), F32), pltpu.VMEM((win, 128), F32)],
        operands=(u, u, conv_w, conv_b))
    return hc, got


def _gla_group(nchunk):
    return 11 if nchunk % 11 == 0 else nchunk


def _bdot(a, b, ca, cb, precision=None):
    return lax.dot_general(a, b, (((ca,), (cb,)), ((0,), (0,))), preferred_element_type=F32, precision=precision)


def _bnn(a, b, **kw):
    return _bdot(a, b, 2, 1, **kw)


def _bnt(a, b, **kw):
    return _bdot(a, b, 2, 2, **kw)


def _gla_consts(nb):
    row = lax.broadcasted_iota(jnp.int32, (nb, CHUNK, CHUNK), 1)
    col = lax.broadcasted_iota(jnp.int32, (nb, CHUNK, CHUNK), 2)
    lane = lax.broadcasted_iota(jnp.int32, (1, 1, 128), 2)
    return row >= col, row <= col, [lane < GLA_DK, lane >= GLA_DK]


def _gla_group_terms(g, nb, q_ref, k_ref, gd_ref, gup_ref, gb_ref, tril):
    m = nb * CHUNK
    rows = pl.ds(pl.multiple_of(g * m, CHUNK), m)
    z = _nn(gd_ref[0, rows, :].astype(BF16), gup_ref[...]) + gb_ref[...]
    valid = g * m + lax.broadcasted_iota(jnp.int32, (m, 1), 0) >= PAD
    lg = jnp.where(valid, _log_sigmoid(z) * (1.0 / GLA_TAU), 0.0)
    bcum = _bnn(tril.astype(F32), lg.reshape(nb, CHUNK, 128), precision=lax.Precision.HIGHEST)
    blast = bcum[:, CHUNK - 1:CHUNK, :]
    eb = jnp.exp(bcum)
    enb = jnp.exp(-bcum)
    erest = jnp.exp(blast - bcum)
    q = (q_ref[0, rows, :] * Q_SCALE).reshape(nb, CHUNK, 128)
    k = k_ref[0, rows, :].reshape(nb, CHUNK, 128)
    return rows, valid, z, eb, enb, erest, jnp.exp(blast), q * eb, k * enb, k * erest


def _grid_ends(grid):
    ids = [pl.program_id(i) for i in range(len(grid))]
    first = functools.reduce(jnp.logical_and, [i == 0 for i in ids])
    last = functools.reduce(jnp.logical_and, [i == g - 1 for i, g in zip(ids, grid)])
    return first, last


def _gla_fwd(u, gup, gbias, gnorm, fuse=NO_FUSE):
    bsz, tp, _ = u.shape
    nchunk = tp // CHUNK
    nb = _gla_group(nchunk)

    def body(q_ref, k_ref, v_ref, r_ref, gd_ref, gup_ref, gb_ref, gn_ref, out_ref, o_ref, st_ref, s_scr):
        tril, _, hmask = _gla_consts(nb)
        s_scr[...] = jnp.zeros_like(s_scr)
        gn = gn_ref[...]

        def group(g, carry):
            rows, _, _, _, _, _, dec, qe, ke, kd = _gla_group_terms(g, nb, q_ref, k_ref, gd_ref, gup_ref, gb_ref, tril)
            keb, kdb = ke.astype(BF16), kd.astype(BF16)
            for h in range(2):
                cols = slice(h * GLA_DV, (h + 1) * GLA_DV)
                qh = jnp.where(hmask[h], qe, 0.0).astype(BF16)
                vh = v_ref[0, rows, cols].astype(BF16).reshape(nb, CHUNK, GLA_DV)
                a = jnp.where(tril, _bnt(qh, keb), 0.0).astype(BF16)
                st = s_scr[h]
                sts = []
                for n in range(nb):
                    st_ref[0, h, g * nb + n] = st
                    sts.append(st.astype(BF16))
                    st = dec[n] * st + _tn(vh[n], kdb[n])
                s_scr[h] = st
                o = (_bnn(a, vh) + _bnt(qh, jnp.stack(sts))).reshape(nb * CHUNK, GLA_DV)
                o_ref[0, rows, cols] = o
                rms = lax.rsqrt(jnp.mean(o * o, axis=-1, keepdims=True) + LN_EPS)
                rh = r_ref[0, rows, cols]
                out_ref[0, rows, cols] = (o * rms * gn * (rh * _sigmoid(rh))).astype(BF16)
            return carry

        lax.fori_loop(0, nchunk // nb, group, 0)

    res, got = _fused_call(
        body, fuse, name="gla_fwd",
        out_shape=[jax.ShapeDtypeStruct((bsz, tp, 512), BF16), jax.ShapeDtypeStruct((bsz, tp, 512), F32),
                   jax.ShapeDtypeStruct((bsz, GLA_HEADS, nchunk, GLA_DV, 128), F32)],
        grid=(bsz, 2),
        in_specs=[pl.BlockSpec((1, tp, 128), lambda bi, p: (bi, 0, C_Q // 128 + p)),
                  pl.BlockSpec((1, tp, 128), lambda bi, p: (bi, 0, C_K // 128 + p)),
                  pl.BlockSpec((1, tp, 256), lambda bi, p: (bi, 0, C_V // 256 + p)),
                  pl.BlockSpec((1, tp, 256), lambda bi, p: (bi, 0, C_R // 256 + p)),
                  pl.BlockSpec((1, tp, 128), lambda bi, p: (bi, 0, C_GD // 128)),
                  pl.BlockSpec((128, 128), lambda bi, p: (0, p)),
                  pl.BlockSpec((1, 128), lambda bi, p: (0, p)),
                  pl.BlockSpec((1, 128), lambda bi, p: (0, 0))],
        out_specs=[pl.BlockSpec((1, tp, 256), lambda bi, p: (bi, 0, p)),
                   pl.BlockSpec((1, tp, 256), lambda bi, p: (bi, 0, p)),
                   pl.BlockSpec((1, 2, nchunk, GLA_DV, 128), lambda bi, p: (bi, p, 0, 0, 0))],
        scratch_shapes=[pltpu.VMEM((2, GLA_DV, 128), F32)],
        operands=(u, u, u, u, u, gup, gbias, gnorm))
    return res[0], res[1], res[2], got


def _out_proj_ln1(hc, gla_out, w_out, s0, cg, cb, g1, b1, fuse=NO_FUSE):
    r, d = s0.shape
    tm = _row_tile(r)

    def body(hc_ref, a_ref, w_ref, s0_ref, cg_ref, cb_ref, g_ref, b_ref, co_ref, xh_ref, rstd_ref, s1b_ref):
        for rs in _sub_rows(tm):
            xc, _ = _ln_stats(hc_ref[rs, :])
            nv = xc * cg_ref[...] + cb_ref[...]
            co = (nv * _sigmoid(nv)).astype(BF16)
            co_ref[rs, :] = co
            mix = _nn(co, w_ref[0:D_CONV, :]) + _nn(a_ref[rs, :], w_ref[D_CONV:, :])
            xh, rstd = _ln_stats(ALPHA * s0_ref[rs, :] + mix)
            xh_ref[rs, :] = xh
            rstd_ref[rs, :] = rstd
            s1b_ref[rs, :] = (xh * g_ref[...] + b_ref[...]).astype(BF16)

    row = lambda n: pl.BlockSpec((tm, n), lambda i: (i, 0))
    vec = lambda n: pl.BlockSpec((1, n), lambda i: (0, 0))
    res, got = _fused_call(
        body, fuse, name="out_proj_ln1",
        out_shape=[jax.ShapeDtypeStruct((r, D_CONV), BF16), jax.ShapeDtypeStruct((r, d), F32),
                   jax.ShapeDtypeStruct((r, 1), F32), jax.ShapeDtypeStruct((r, d), BF16)],
        grid=(r // tm,),
        in_specs=[row(D_CONV), row(512), pl.BlockSpec((d, d), lambda i: (0, 0)), row(d),
                  vec(D_CONV), vec(D_CONV), vec(d), vec(d)],
        out_specs=[row(D_CONV), row(d), row(1), row(d)], scratch_shapes=[],
        operands=(hc, gla_out, w_out, s0, cg, cb, g1, b1))
    return res[0], res[1], res[2], res[3], got


def _ffn1(s1b, w1t, fuse=NO_FUSE):
    r, d = s1b.shape
    tm = _row_tile(r)

    def body(a_ref, w_ref, o_ref):
        o_ref[...] = jnp.maximum(_nt(a_ref[...], w_ref[...]), 0.0).astype(BF16)

    (ra,), got = _fused_call(
        body, fuse, name="ffn1", out_shape=[jax.ShapeDtypeStruct((r, D_FF), BF16)], grid=(r // tm,),
        in_specs=[pl.BlockSpec((tm, d), lambda i: (i, 0)), pl.BlockSpec((D_FF, d), lambda i: (0, 0))],
        out_specs=[pl.BlockSpec((tm, D_FF), lambda i: (i, 0))], scratch_shapes=[], operands=(s1b, w1t))
    return ra, got


def _ffn2_ln2_loss(ra, w2, xhat1, g1, b1, g2, b2, tgt, tp):
    r, d = xhat1.shape
    tm = _row_tile(tp)
    per = tp // tm

    def body(ra_ref, w_ref, xh1_ref, g1_ref, b1_ref, g2_ref, b2_ref, tgt_ref, dt_ref, dtb_ref, acc_ref, t_ref, sem):
        i = pl.program_id(0)
        b, j = i // per, i % per

        @pl.when(i == 0)
        def _():
            acc_ref[...] = jnp.zeros_like(acc_ref)

        head_copy = pltpu.make_async_copy(tgt_ref.at[b, pl.ds(0, tm - HEAD), :], t_ref.at[pl.ds(HEAD, tm - HEAD), :], sem)
        body_copy = pltpu.make_async_copy(
            tgt_ref.at[b, pl.ds(pl.multiple_of(jnp.maximum(j * tm - HEAD, 0), 64), tm), :], t_ref, sem)

        @pl.when(j == 0)
        def _():
            t_ref[0:HEAD, :] = jnp.zeros((HEAD, d), F32)
            head_copy.start()

        pl.when(j > 0)(body_copy.start)

        sums = [jnp.zeros((1, d), F32)] * 3
        for rs in _sub_rows(tm):
            rb = ra_ref[rs, :]
            f = _nn(rb * rb, w_ref[...])
            if rs.start == 0:
                pl.when(j == 0)(head_copy.wait)
                pl.when(j > 0)(body_copy.wait)
            s1 = xh1_ref[rs, :] * g1_ref[...] + b1_ref[...]
            xh2, rstd2 = _ln_stats(ALPHA * s1 + f)
            y = xh2 * g2_ref[...] + b2_ref[...]
            rowid = (i % per) * tm + rs.start + lax.broadcasted_iota(jnp.int32, (rs.stop - rs.start, 1), 0)
            e = jnp.where(rowid >= HEAD, y - t_ref[rs, :], 0.0)
            dy = e * (1.0 / d)
            dt2 = _ln_bwd(dy * g2_ref[...], xh2, rstd2)
            dt_ref[rs, :] = dt2
            dtb_ref[rs, :] = dt2.astype(BF16)
            sums = [sums[0] + (0.5 / d) * jnp.sum(e * e, axis=0, keepdims=True),
                    sums[1] + jnp.sum(dy * xh2, axis=0, keepdims=True), sums[2] + jnp.sum(dy, axis=0, keepdims=True)]
        for k in range(3):
            acc_ref[k:k + 1, :] += sums[k]

    row = lambda n: pl.BlockSpec((tm, n), lambda i: (i, 0))
    vec = pl.BlockSpec((1, d), lambda i: (0, 0))
    return pl.pallas_call(
        body, name="ffn2_ln2_loss",
        out_shape=[jax.ShapeDtypeStruct((r, d), F32), jax.ShapeDtypeStruct((r, d), BF16),
                   jax.ShapeDtypeStruct((8, d), F32)],
        grid=(r // tm,),
        in_specs=[row(D_FF), pl.BlockSpec((D_FF, d), lambda i: (0, 0)), row(d), vec, vec, vec, vec, ANY],
        out_specs=[row(d), row(d), pl.BlockSpec((8, d), lambda i: (0, 0))],
        scratch_shapes=[pltpu.VMEM((tm, d), F32), pltpu.SemaphoreType.DMA],
        compiler_params=_params(("arbitrary",)),
    )(ra, w2, xhat1, g1, b1, g2, b2, tgt)


def _ffn_bwd_da(dt2b, w2, ra):
    r, d = dt2b.shape
    tm = _row_tile(r)

    def body(g_ref, w_ref, ra_ref, o_ref):
        o_ref[...] = (_nt(g_ref[...], w_ref[...]) * (2.0 * ra_ref[...].astype(F32))).astype(BF16)

    return pl.pallas_call(
        body, name="ffn_bwd_da", out_shape=jax.ShapeDtypeStruct((r, D_FF), BF16),
        grid=(r // tm,),
        in_specs=[pl.BlockSpec((tm, d), lambda i: (i, 0)), pl.BlockSpec((D_FF, d), lambda i: (0, 0)),
                  pl.BlockSpec((tm, D_FF), lambda i: (i, 0))],
        out_specs=pl.BlockSpec((tm, D_FF), lambda i: (i, 0)),
        compiler_params=_params(("parallel",)),
    )(dt2b, w2, ra)


def _ffn_bwd_ln1(da, w1t, dt2, xhat1, rstd1, g1):
    r, d = dt2.shape
    tm = _row_tile(r)

    def body(da_ref, w_ref, dt2_ref, xh_ref, rstd_ref, g_ref, dt_ref, dtb_ref, acc_ref):
        @pl.when(pl.program_id(0) == 0)
        def _():
            acc_ref[...] = jnp.zeros_like(acc_ref)

        sums = [jnp.zeros((1, d), F32)] * 2
        for rs in _sub_rows(tm):
            ds1 = ALPHA * dt2_ref[rs, :] + _nn(da_ref[rs, :], w_ref[...])
            xh = xh_ref[rs, :]
            dt1 = _ln_bwd(ds1 * g_ref[...], xh, rstd_ref[rs, :])
            dt_ref[rs, :] = dt1
            dtb_ref[rs, :] = dt1.astype(BF16)
            sums = [sums[0] + jnp.sum(ds1 * xh, axis=0, keepdims=True), sums[1] + jnp.sum(ds1, axis=0, keepdims=True)]
        for k in range(2):
            acc_ref[k:k + 1, :] += sums[k]

    row = lambda n: pl.BlockSpec((tm, n), lambda i: (i, 0))
    return pl.pallas_call(
        body, name="ffn_bwd_ln1",
        out_shape=[jax.ShapeDtypeStruct((r, d), F32), jax.ShapeDtypeStruct((r, d), BF16),
                   jax.ShapeDtypeStruct((8, d), F32)],
        grid=(r // tm,),
        in_specs=[row(D_FF), pl.BlockSpec((D_FF, d), lambda i: (0, 0)), row(d), row(d), row(1),
                  pl.BlockSpec((1, d), lambda i: (0, 0))],
        out_specs=[row(d), row(d), pl.BlockSpec((8, d), lambda i: (0, 0))],
        compiler_params=_params(("arbitrary",)),
    )(da, w1t, dt2, xhat1, rstd1, g1)


def _matmul_tn(lhs, rhs, bm, square_lhs=False, name="matmul_tn"):
    r, m = lhs.shape
    n = rhs.shape[1]
    tk = _row_tile(r)

    def body(a_ref, b_ref, o_ref):
        @pl.when(pl.program_id(1) == 0)
        def _():
            o_ref[...] = jnp.zeros_like(o_ref)

        a = a_ref[...]
        if square_lhs:
            a = a * a
        o_ref[...] += _tn(a, b_ref[...])

    return pl.pallas_call(
        body, name=name, out_shape=jax.ShapeDtypeStruct((m, n), F32),
        grid=(m // bm, r // tk),
        in_specs=[pl.BlockSpec((tk, bm), lambda i, k: (k, i)), pl.BlockSpec((tk, n), lambda i, k: (k, 0))],
        out_specs=pl.BlockSpec((bm, n), lambda i, k: (i, 0)),
        compiler_params=_params(("parallel", "arbitrary")),
    )(lhs, rhs)


def _out_proj_bwd(dt1b, w_out, hc, cg, cb, fuse=NO_FUSE):
    r, d = dt1b.shape
    tm = _row_tile(r)

    def body(g_ref, w_ref, hc_ref, cg_ref, cb_ref, dhc_ref, dgla_ref, acc_ref):
        @pl.when(pl.program_id(0) == 0)
        def _():
            acc_ref[...] = jnp.zeros_like(acc_ref)

        gg = cg_ref[...]
        sums = [jnp.zeros((1, D_CONV), F32)] * 3
        for rs in _sub_rows(tm):
            dmix = _nt(g_ref[rs, :], w_ref[...])
            dgla_ref[rs, :] = dmix[:, D_CONV:]
            xh, rstd = _ln_stats(hc_ref[rs, :])
            nv = xh * gg + cb_ref[...]
            sig = _sigmoid(nv)
            dn = dmix[:, :D_CONV] * (sig * (1.0 + nv * (1.0 - sig)))
            dhc = _ln_bwd(dn * gg, xh, rstd)
            dhc_ref[rs, :] = dhc
            sums = [sums[0] + jnp.sum(dhc, axis=0, keepdims=True), sums[1] + jnp.sum(dn * xh, axis=0, keepdims=True),
                    sums[2] + jnp.sum(dn, axis=0, keepdims=True)]
        for k in range(3):
            acc_ref[k:k + 1, :] += sums[k]

    row = lambda n: pl.BlockSpec((tm, n), lambda i: (i, 0))
    vec = pl.BlockSpec((1, D_CONV), lambda i: (0, 0))
    res, got = _fused_call(
        body, fuse, name="out_proj_bwd",
        out_shape=[jax.ShapeDtypeStruct((r, D_CONV), F32), jax.ShapeDtypeStruct((r, 512), F32),
                   jax.ShapeDtypeStruct((8, D_CONV), F32)],
        grid=(r // tm,),
        in_specs=[row(d), pl.BlockSpec((d, d), lambda i: (0, 0)), row(D_CONV), vec, vec],
        out_specs=[row(D_CONV), row(512), pl.BlockSpec((8, D_CONV), lambda i: (0, 0))], scratch_shapes=[],
        operands=(dt1b, w_out, hc, cg, cb))
    return res[0], res[1], res[2], got


def _conv_bwd(dhc, u, conv_w, fuse=NO_FUSE):
    bsz, tp, _ = u.shape
    nchunk = tp // CHUNK
    win = CHUNK + 32
    nct = D_CONV // 128

    def body(dhc_ref, cv_ref, cg_ref, w_ref, dv_ref, dg_ref, dw_ref, h_scr, dhc_scr, hwin, dwin, dw_scr):
        h_scr[0:32, :] = jnp.zeros((32, 128), F32)
        h_scr[32:32 + tp, :] = cv_ref[0] * _sigmoid(cg_ref[0])
        dhc_scr[0:tp, :] = dhc_ref[0]
        dhc_scr[tp:tp + 32, :] = jnp.zeros((32, 128), F32)
        dw_scr[...] = jnp.zeros_like(dw_scr)

        def step(n, carry):
            r0 = pl.multiple_of(n * CHUNK, CHUNK)
            rows = pl.ds(r0, CHUNK)
            hwin[...] = h_scr[pl.ds(r0, win), :]
            dwin[...] = dhc_scr[pl.ds(r0, win), :]
            dcur = dwin[0:CHUNK, :]
            acc = jnp.zeros((CHUNK, 128), F32)
            for j in range(CONV_WIDTH):
                acc = acc + w_ref[j:j + 1, :] * dwin[30 - j:30 - j + CHUNK, :]
                prod = dcur * hwin[2 + j:2 + j + CHUNK, :]
                dw_scr[j * 8:(j + 1) * 8, :] += jnp.sum(prod.reshape(CHUNK // 8, 8, 128), axis=0)
            cg = cg_ref[0, rows, :]
            sig = _sigmoid(cg)
            rowid = n * CHUNK + lax.broadcasted_iota(jnp.int32, (CHUNK, 1), 0)
            dh = jnp.where(rowid >= PAD, acc, 0.0)
            dv_ref[0, rows, :] = (dh * sig).astype(BF16)
            dg_ref[0, rows, :] = (dh * cv_ref[0, rows, :] * sig * (1.0 - sig)).astype(BF16)
            return carry

        lax.fori_loop(0, nchunk, step, 0)
        dw_ref[0] = jnp.zeros((32, 128), F32)
        for j in range(CONV_WIDTH):
            dw_ref[0, j:j + 1, :] = jnp.sum(dw_scr[j * 8:(j + 1) * 8, :], axis=0, keepdims=True)

    blk = lambda off: pl.BlockSpec((1, tp, 128), lambda bi, t: (bi, 0, off // 128 + t))
    res, got = _fused_call(
        body, fuse, name="conv_bwd",
        out_shape=[jax.ShapeDtypeStruct((bsz, tp, D_CONV), BF16), jax.ShapeDtypeStruct((bsz, tp, D_CONV), BF16),
                   jax.ShapeDtypeStruct((bsz, 32, D_CONV), F32)],
        grid=(bsz, nct),
        in_specs=[blk(0), blk(C_VAL), blk(C_GATE), pl.BlockSpec((32, 128), lambda bi, t: (0, t))],
        out_specs=[blk(0), blk(0), pl.BlockSpec((1, 32, 128), lambda bi, t: (bi, 0, t))],
        scratch_shapes=[pltpu.VMEM((tp + 32, 128), F32), pltpu.VMEM((tp + 32, 128), F32),
                        pltpu.VMEM((win, 128), F32), pltpu.VMEM((win, 128), F32),
                        pltpu.VMEM((CONV_WIDTH * 8, 128), F32)],
        operands=(dhc, u, u, conv_w))
    return res[0], res[1], res[2], got


def _gla_bwd(dgla, u, o_pre, states, gup, gbias, gnorm, fuse=NO_FUSE):
    bsz, tp, _ = u.shape
    nchunk = tp // CHUNK
    nb = _gla_group(nchunk)

    def body(dy_ref, q_ref, k_ref, v_ref, r_ref, gd_ref, o_ref, st_ref, gup_ref, gb_ref, gn_ref,
             dq_ref, dk_ref, dv_ref, dr_ref, dgd_ref, dgup_ref, vec_ref, h_scr, gup_acc):
        tril, triu, hmask = _gla_consts(nb)
        h_scr[...] = jnp.zeros_like(h_scr)
        gup_acc[...] = jnp.zeros_like(gup_acc)
        gn = gn_ref[...]
        gupb = gup_ref[...]
        m = nb * CHUNK
        ngroup = nchunk // nb

        def group(i, carry):
            dbias, dgn = carry
            g = ngroup - 1 - i
            rows, valid, z, eb, enb, erest, dec, qe, ke, kd = _gla_group_terms(
                g, nb, q_ref, k_ref, gd_ref, gup_ref, gb_ref, tril)
            keb, kdb = ke.astype(BF16), kd.astype(BF16)
            dqe = jnp.zeros((nb, CHUNK, 128), F32)
            dke = jnp.zeros((nb, CHUNK, 128), F32)
            dkd = jnp.zeros((nb, CHUNK, 128), F32)
            ddec = jnp.zeros((nb, 1, 128), F32)
            for h in range(2):
                cols = slice(h * GLA_DV, (h + 1) * GLA_DV)
                o = o_ref[0, rows, cols]
                rh = r_ref[0, rows, cols]
                dy = dy_ref[0, rows, cols]
                rms = lax.rsqrt(jnp.mean(o * o, axis=-1, keepdims=True) + LN_EPS)
                nrm = o * rms
                sig = _sigmoid(rh)
                sw = rh * sig
                dr_ref[0, rows, cols] = (dy * nrm * gn * (sig * (1.0 + rh * (1.0 - sig)))).astype(BF16)
                dgn = dgn + jnp.sum(dy * nrm * sw, axis=0, keepdims=True)
                dn = dy * gn * sw
                do = rms * (dn - nrm * jnp.mean(dn * nrm, axis=-1, keepdims=True))
                dob = do.astype(BF16).reshape(nb, CHUNK, GLA_DV)
                qh = jnp.where(hmask[h], qe, 0.0).astype(BF16)
                vh = v_ref[0, rows, cols].astype(BF16).reshape(nb, CHUNK, GLA_DV)
                ht = h_scr[h]
                hts = [None] * nb
                for n in reversed(range(nb)):
                    hts[n] = ht
                    ht = dec[n] * ht + _tn(dob[n], qh[n])
                h_scr[h] = ht
                htf = jnp.stack(hts)
                htb = htf.astype(BF16)
                st = st_ref[0, h, pl.ds(g * nb, nb)]
                at = jnp.where(triu, _bnt(keb, qh), 0.0).astype(BF16)
                da = jnp.where(tril, _bnt(dob, vh), 0.0).astype(BF16)
                dat = jnp.where(triu, _bnt(vh, dob), 0.0).astype(BF16)
                dqe = dqe + jnp.where(hmask[h], _bnn(da, keb) + _bnn(dob, st.astype(BF16)), 0.0)
                dke = dke + _bnn(dat, qh)
                dv_ref[0, rows, cols] = (_bnn(at, dob) + _bnt(kdb, htb)).reshape(m, GLA_DV).astype(BF16)
                dkd = dkd + jnp.where(hmask[h], _bnn(vh, htb), 0.0)
                ddec = ddec + jnp.where(hmask[h], jnp.sum(htf * st, axis=1, keepdims=True), 0.0)
            dq_ref[0, rows, :] = (dqe * eb * Q_SCALE).reshape(m, 128).astype(BF16)
            dk_ref[0, rows, :] = (dke * enb + dkd * erest).reshape(m, 128).astype(BF16)
            db = dqe * qe - dke * ke - dkd * kd
            dblast = jnp.sum(dkd * kd, axis=1, keepdims=True) + ddec * dec
            lastrow = lax.broadcasted_iota(jnp.int32, (1, CHUNK, 1), 1) == CHUNK - 1
            db = db + jnp.where(lastrow, dblast, 0.0)
            dlg = _bnn(triu.astype(F32), db, precision=lax.Precision.HIGHEST).reshape(m, 128)
            dz = jnp.where(valid, dlg, 0.0) * (1.0 / GLA_TAU) * (1.0 - _sigmoid(z))
            dzb = dz.astype(BF16)
            dgd_ref[0, 0, rows, :] = _nt(dzb, gupb).astype(BF16)
            gup_acc[...] += _tn(gd_ref[0, rows, :].astype(BF16), dzb)
            return dbias + jnp.sum(dz, axis=0, keepdims=True), dgn

        zero = jnp.zeros((1, 128), F32)
        dbias, dgn = lax.fori_loop(0, ngroup, group, (zero, zero))
        dgup_ref[0] = gup_acc[...]
        vec_ref[0] = jnp.zeros((8, 128), F32)
        vec_ref[0, 0:1, :] = dbias
        vec_ref[0, 1:2, :] = dgn

    pair = lambda w, off: pl.BlockSpec((1, tp, w), lambda bi, p: (bi, 0, off // w + p))
    return _fused_call(
        body, fuse, name="gla_bwd",
        out_shape=[jax.ShapeDtypeStruct((bsz, tp, 256), BF16), jax.ShapeDtypeStruct((bsz, tp, 256), BF16),
                   jax.ShapeDtypeStruct((bsz, tp, 512), BF16), jax.ShapeDtypeStruct((bsz, tp, 512), BF16),
                   jax.ShapeDtypeStruct((bsz, 2, tp, 128), BF16), jax.ShapeDtypeStruct((bsz, 128, 256), F32),
                   jax.ShapeDtypeStruct((bsz, 8, 256), F32)],
        grid=(bsz, 2),
        in_specs=[pair(256, 0), pair(128, C_Q), pair(128, C_K), pair(256, C_V), pair(256, C_R),
                  pl.BlockSpec((1, tp, 128), lambda bi, p: (bi, 0, C_GD // 128)),
                  pair(256, 0),
                  pl.BlockSpec((1, 2, nchunk, GLA_DV, 128), lambda bi, p: (bi, p, 0, 0, 0)),
                  pl.BlockSpec((128, 128), lambda bi, p: (0, p)),
                  pl.BlockSpec((1, 128), lambda bi, p: (0, p)),
                  pl.BlockSpec((1, 128), lambda bi, p: (0, 0))],
        out_specs=[pair(128, 0), pair(128, 0), pair(256, 0), pair(256, 0),
                   pl.BlockSpec((1, 1, tp, 128), lambda bi, p: (bi, p, 0, 0)),
                   pl.BlockSpec((1, 128, 128), lambda bi, p: (bi, 0, p)),
                   pl.BlockSpec((1, 8, 128), lambda bi, p: (bi, 0, p))],
        scratch_shapes=[pltpu.VMEM((2, GLA_DV, 128), F32), pltpu.VMEM((128, 128), F32)],
        operands=(dgla, u, u, u, u, u, o_pre, states, gup, gbias, gnorm))


_DU_OFFSETS = (C_VAL, C_GATE, C_Q, C_K, C_V, C_R)
_DU_WIDTHS = (512, 512, 256, 256, 512, 512)


def _du_specs(tm, per, row_map):
    specs = [pl.BlockSpec((tm, w), row_map) for w in _DU_WIDTHS]
    for p in range(2):
        specs.append(pl.BlockSpec((1, 1, tm, 128), lambda *ix, p=p: (row_map(*ix)[0] // per, p, row_map(*ix)[0] % per, 0)))
    return specs


def _du_pieces(refs):
    out = [(off, ref[...]) for off, ref in zip(_DU_OFFSETS, refs[:6])]
    dgd = (refs[6][0, 0].astype(F32) + refs[7][0, 0].astype(F32)).astype(BF16)
    out.append((C_GD, dgd))
    return out


def _in_proj_bwd(pieces, dgd, w_int, dt1, tp, fuse=NO_FUSE):
    r, d = dt1.shape
    tm = _row_tile(tp)
    per = tp // tm

    def body(*refs):
        w_ref, dt_ref, o_ref = refs[8:]
        acc = ALPHA * dt_ref[...]
        for off, val in _du_pieces(refs[:8]):
            acc = acc + _nn(val, w_ref[off:off + val.shape[1], :])
        o_ref[...] = acc

    row = lambda i: (i, 0)
    (ds0,), got = _fused_call(
        body, fuse, name="in_proj_bwd", out_shape=[jax.ShapeDtypeStruct((r, d), F32)], grid=(r // tm,),
        in_specs=_du_specs(tm, per, row) + [pl.BlockSpec((D_IN_PAD, d), lambda i: (0, 0)), pl.BlockSpec((tm, d), row)],
        out_specs=[pl.BlockSpec((tm, d), row)], scratch_shapes=[], operands=(*pieces, dgd, dgd, w_int, dt1))
    return ds0, got


def _grad_w_in(pieces, dgd, s0b, tp):
    r, d = s0b.shape
    tk = _row_tile(tp)
    per = tp // tk

    def body(*refs):
        s_ref, o_ref = refs[8:]

        @pl.when(pl.program_id(0) == 0)
        def _():
            o_ref[...] = jnp.zeros_like(o_ref)

        s = s_ref[...]
        for off, val in _du_pieces(refs[:8]):
            o_ref[off:off + val.shape[1], :] += _tn(val, s)

    row = lambda k: (k, 0)
    return pl.pallas_call(
        body, name="grad_w_in", out_shape=jax.ShapeDtypeStruct((D_IN_PAD, d), F32),
        grid=(r // tk,),
        in_specs=_du_specs(tk, per, row) + [pl.BlockSpec((tk, d), row)],
        out_specs=pl.BlockSpec((D_IN_PAD, d), lambda k: (0, 0)),
        compiler_params=_params(("arbitrary",)),
    )(*pieces, dgd, dgd, s0b)


def _ln_in_bwd(ds0, x, meta, g):
    bsz, s, d = x.shape
    tp = s + HEAD
    nh = 2
    sh = s // nh
    rc = min(256, sh)

    def body(ds_ref, x_ref, meta_ref, g_ref, gx_ref, dm_ref, vec_ref):
        h = pl.program_id(1)
        gg = g_ref[...]

        @pl.when(h == 0)
        def _():
            mh, mr = _ln_stats(meta_ref[...])
            dsm = ds_ref[0, PAD:HEAD, :]
            dm_ref[0] = _ln_bwd(dsm * gg, mh, mr)
            vec_ref[0] = jnp.zeros((8, d), F32)
            vec_ref[0, 0:1, :] = jnp.sum(dsm * mh, axis=0, keepdims=True)
            vec_ref[0, 1:2, :] = jnp.sum(dsm, axis=0, keepdims=True)

        def step(i, carry):
            sg, sb = carry
            dst = pl.ds(pl.multiple_of(i * rc, rc), rc)
            src = pl.ds(pl.multiple_of(HEAD + h * sh + i * rc, 64), rc)
            xh, rstd = _ln_stats(x_ref[0, dst, :])
            dsv = ds_ref[0, src, :]
            gx_ref[0, dst, :] = _ln_bwd(dsv * gg, xh, rstd)
            return sg + jnp.sum(dsv * xh, axis=0, keepdims=True), sb + jnp.sum(dsv, axis=0, keepdims=True)

        zero = jnp.zeros((1, d), F32)
        sg, sb = lax.fori_loop(0, sh // rc, step, (zero, zero))
        vec_ref[0, 0:1, :] += sg
        vec_ref[0, 1:2, :] += sb

    return pl.pallas_call(
        body, name="ln_in_bwd",
        out_shape=[jax.ShapeDtypeStruct((bsz, s, d), F32), jax.ShapeDtypeStruct((bsz, N_META, d), F32),
                   jax.ShapeDtypeStruct((bsz, 8, d), F32)],
        grid=(bsz, nh),
        in_specs=[pl.BlockSpec((1, tp, d), lambda bi, hi: (bi, 0, 0)),
                  pl.BlockSpec((1, sh, d), lambda bi, hi: (bi, hi, 0)),
                  pl.BlockSpec((N_META, d), lambda bi, hi: (0, 0)),
                  pl.BlockSpec((1, d), lambda bi, hi: (0, 0))],
        out_specs=[pl.BlockSpec((1, sh, d), lambda bi, hi: (bi, hi, 0)),
                   pl.BlockSpec((1, N_META, d), lambda bi, hi: (bi, 0, 0)),
                   pl.BlockSpec((1, 8, d), lambda bi, hi: (bi, 0, 0))],
        compiler_params=_params(("parallel", "arbitrary")),
    )(ds0, x, meta, g)


def _rows128(a):
    return a.reshape(-1, 128)


def kernel(x, meta_tokens, ln_in_g, ln_in_b, w_in, conv_w, conv_b, conv_ln_g, conv_ln_b, gate_up, gate_bias, gla_norm_g, w_out, ln1_g, ln1_b, w_ff1, w_ff2, ln2_g, ln2_b, loss_target, m_meta_tokens, m_ln_in_g, m_ln_in_b, m_w_in, m_conv_w, m_conv_b, m_conv_ln_g, m_conv_ln_b, m_gate_up, m_gate_bias, m_gla_norm_g, m_w_out, m_ln1_g, m_ln1_b, m_w_ff1, m_w_ff2, m_ln2_g, m_ln2_b, v_meta_tokens, v_ln_in_g, v_ln_in_b, v_w_in, v_conv_w, v_conv_b, v_conv_ln_g, v_conv_ln_b, v_gate_up, v_gate_bias, v_gla_norm_g, v_w_out, v_ln1_g, v_ln1_b, v_w_ff1, v_w_ff2, v_ln2_g, v_ln2_b):
    bsz, seq, d = x.shape
    tp = seq + HEAD
    r = bsz * tp
    xi, yi, ci = _mesh_pos()
    chip = 2 * xi + yi
    c_arr = jnp.reshape(ci, (1,)).astype(jnp.int32)
    pos_arr = jnp.stack([chip, ci]).astype(jnp.int32)

    sh_in = D_IN // 4
    shard_in = jnp.pad(w_in[0].T.astype(BF16), ((0, D_IN_PAD // 4 - sh_in), (0, 0)))
    shard_w1t, shard_wout, shard_w2 = w_ff1[0].T.astype(BF16), w_out[0].astype(BF16), w_ff2[0].astype(BF16)
    small_w = jnp.concatenate([_rows128(meta_tokens), _rows128(conv_w[0]), _rows128(gate_up[0])], axis=0)
    (g_int,), g_small = _gather_weights([shard_in], small_w)
    w_int = jnp.pad(g_int[:, :sh_in].reshape(D_IN, d), ((0, D_IN_PAD - D_IN), (0, 0)))
    n_meta_rows, n_cw_rows = N_META * 256 // 128, CONV_WIDTH * 128 // 128
    meta_full = jnp.concatenate([g_small[j, :n_meta_rows].reshape(N_META, 256) for j in range(4)], axis=1)
    convw_full = jnp.concatenate(
        [g_small[j, n_meta_rows:n_meta_rows + n_cw_rows].reshape(CONV_WIDTH, 128) for j in range(4)], axis=1)
    gup_full = jnp.concatenate(
        [g_small[j, n_meta_rows + n_cw_rows:].reshape(GLA_RANK, 64) for j in range(4)], axis=1)
    convw_p = jnp.pad(convw_full, ((0, 1), (0, 0)))
    gup_p = jnp.pad(gup_full, ((0, 128 - GLA_RANK), (0, 0))).astype(BF16)
    ln_in_g2, ln_in_b2 = ln_in_g.reshape(1, d), ln_in_b.reshape(1, d)

    s0, s0b = _ln_in_fwd(x, meta_full, ln_in_g2, ln_in_b2)
    s0f, s0bf = s0.reshape(r, d), s0b.reshape(r, d)
    u, (g_w1t,) = _in_proj(s0bf, w_int, ("gather", [shard_w1t]))
    w1t = g_w1t.reshape(D_FF, d)
    u3 = u.reshape(bsz, tp, D_IN_PAD)
    hc, (g_wout,) = _conv_fwd(u3, convw_p, conv_b, ("gather", [shard_wout]))
    hc = hc.reshape(r, D_CONV)
    wout = g_wout.reshape(d, d)
    gla_out, o_pre, states, _ = _gla_fwd(u3, gup_p, gate_bias, gla_norm_g)
    gla_of = gla_out.reshape(r, 512)
    conv_of, xhat1, rstd1, s1b, _ = _out_proj_ln1(hc, gla_of, wout, s0f, conv_ln_g, conv_ln_b, ln1_g, ln1_b)
    ra, (g_w2,) = _ffn1(s1b, w1t, ("gather", [shard_w2]))
    w2 = g_w2.reshape(D_FF, d)
    dt2, dt2b, acc2 = _ffn2_ln2_loss(ra, w2, xhat1, ln1_g, ln1_b, ln2_g, ln2_b, loss_target, tp)

    def add_pairs(gs, got):
        return [_add_pair(g, t, c_arr) for g, t in zip(gs, got)]

    da = _ffn_bwd_da(dt2b, w2, ra)
    dt1, dt1b, acc1 = _ffn_bwd_ln1(da, w1t, dt2, xhat1, rstd1, ln1_g)
    g_w2 = _matmul_tn(ra, dt2b, 1024, square_lhs=True, name="grad_w_ff2")
    g_w1t = _matmul_tn(da, s1b, 1024, name="grad_w_ff1")
    big_ffn = [g_w1t.reshape(4, D_FF // 4, d), g_w2.reshape(4, D_FF // 4, d)]
    dhc, dgla, cacc, pair_got_ffn = _out_proj_bwd(dt1b, wout, hc, conv_ln_g, conv_ln_b, ("pair", big_ffn))
    parts_ffn = add_pairs(big_ffn, pair_got_ffn)
    g_wout = jnp.concatenate([_matmul_tn(conv_of, dt1b, D_CONV, name="grad_w_out_conv"),
                              _matmul_tn(gla_of, dt1b, 512, name="grad_w_out_gla")], axis=0)
    dcv, dcg, dcw, got_w1 = _conv_bwd(dhc.reshape(bsz, tp, D_CONV), u3, convw_p, ("exchange", parts_ffn[:1]))
    (dq, dk, dv, dr, dgd, dgup, gvec), got_w2 = _gla_bwd(
        dgla.reshape(bsz, tp, 512), u3, o_pre, states, gup_p, gate_bias, gla_norm_g, ("exchange", parts_ffn[1:]))
    chip_got_ffn = got_w1 + got_w2
    pieces = [a.reshape(r, a.shape[-1]) for a in (dcv, dcg, dq, dk, dv, dr)]
    g_wint = _grad_w_in(pieces, dgd, s0bf, tp)
    big_io = [jnp.stack([g_wint[j * sh_in:(j + 1) * sh_in] for j in range(4)]), g_wout.reshape(4, d // 4, d)]
    pair_got_io = _pair_exchange(big_io)
    ds0, chip_got_io = _in_proj_bwd(pieces, dgd, w_int, dt1, tp, ("exchange", add_pairs(big_io, pair_got_io)))
    grad_x, dmeta, lvec = _ln_in_bwd(ds0.reshape(bsz, tp, d), x, meta_full, ln_in_g2)

    small = {
        "loss": acc2[0:1], "ln_in_g": jnp.sum(lvec[:, 0], axis=0), "ln_in_b": jnp.sum(lvec[:, 1], axis=0),
        "conv_b": cacc[0], "conv_ln_g": cacc[1], "conv_ln_b": cacc[2], "gate_bias": jnp.sum(gvec[:, 0], axis=0),
        "gla_norm_g": jnp.sum(gvec[:, 1].reshape(bsz * 2, 128), axis=0),
        "ln1_g": acc1[0], "ln1_b": acc1[1], "ln2_g": acc2[1], "ln2_b": acc2[2],
        "conv_w": jnp.sum(dcw, axis=0)[:CONV_WIDTH], "gate_up": jnp.sum(dgup, axis=0)[:GLA_RANK],
        "meta_tokens": jnp.sum(dmeta, axis=0),
    }
    pack = jnp.concatenate([_rows128(small[k]) for k, _ in _SMALL_FIELDS], axis=0)
    pack = jnp.pad(pack, ((0, SMALL_ROWS - pack.shape[0]), (0, 0)))
    red = _sum_small(_allgather_small(pack))
    off, tot = {}, 0
    for k, nrows in _SMALL_FIELDS:
        off[k] = (tot, nrows)
        tot += nrows

    def field(k, shape):
        o, nrows = off[k]
        return red[o:o + nrows].reshape(shape)

    loss = jnp.sum(field("loss", (d,)))

    halves = [_add_chips(g, t, s, pos_arr) for g, t, s in zip(
        big_io + big_ffn, list(pair_got_io) + list(pair_got_ffn), list(chip_got_io) + list(chip_got_ffn))]
    f_wint, f_wout, f_w1t, f_w2 = [
        jnp.concatenate([jnp.where(ci == 0, own, sib), jnp.where(ci == 0, sib, own)], axis=1)
        for own, sib in zip(halves, _pair_swap(halves))]

    def cols(a, width):
        return lax.dynamic_slice_in_dim(a, chip * width, width, axis=a.ndim - 1)

    grads = {
        "meta_tokens": cols(field("meta_tokens", (N_META, d)), 256),
        "ln_in_g": field("ln_in_g", (d,)), "ln_in_b": field("ln_in_b", (d,)),
        "w_in": f_wint.T[None], "conv_w": cols(field("conv_w", (CONV_WIDTH, D_CONV)), 128)[None],
        "conv_b": field("conv_b", (1, D_CONV)), "conv_ln_g": field("conv_ln_g", (1, D_CONV)),
        "conv_ln_b": field("conv_ln_b", (1, D_CONV)),
        "gate_up": cols(field("gate_up", (GLA_RANK, 256)), 64)[None],
        "gate_bias": field("gate_bias", (1, 256)), "gla_norm_g": field("gla_norm_g", (1, 128)),
        "w_out": f_wout[None], "ln1_g": field("ln1_g", (1, d)), "ln1_b": field("ln1_b", (1, d)),
        "w_ff1": f_w1t.T[None], "w_ff2": f_w2[None],
        "ln2_g": field("ln2_g", (1, d)), "ln2_b": field("ln2_b", (1, d)),
    }
    weights = dict(meta_tokens=meta_tokens, ln_in_g=ln_in_g, ln_in_b=ln_in_b, w_in=w_in, conv_w=conv_w, conv_b=conv_b,
                   conv_ln_g=conv_ln_g, conv_ln_b=conv_ln_b, gate_up=gate_up, gate_bias=gate_bias,
                   gla_norm_g=gla_norm_g, w_out=w_out, ln1_g=ln1_g, ln1_b=ln1_b, w_ff1=w_ff1, w_ff2=w_ff2,
                   ln2_g=ln2_g, ln2_b=ln2_b)
    moms = dict(meta_tokens=(m_meta_tokens, v_meta_tokens), ln_in_g=(m_ln_in_g, v_ln_in_g),
                ln_in_b=(m_ln_in_b, v_ln_in_b), w_in=(m_w_in, v_w_in), conv_w=(m_conv_w, v_conv_w),
                conv_b=(m_conv_b, v_conv_b), conv_ln_g=(m_conv_ln_g, v_conv_ln_g),
                conv_ln_b=(m_conv_ln_b, v_conv_ln_b), gate_up=(m_gate_up, v_gate_up),
                gate_bias=(m_gate_bias, v_gate_bias), gla_norm_g=(m_gla_norm_g, v_gla_norm_g),
                w_out=(m_w_out, v_w_out), ln1_g=(m_ln1_g, v_ln1_g), ln1_b=(m_ln1_b, v_ln1_b),
                w_ff1=(m_w_ff1, v_w_ff1), w_ff2=(m_w_ff2, v_w_ff2), ln2_g=(m_ln2_g, v_ln2_g),
                ln2_b=(m_ln2_b, v_ln2_b))
    names = list(weights)
    big_names = ("w_in", "w_out", "w_ff1", "w_ff2")
    delta, new_m, new_v = {}, {}, {}
    dl, mn, vn = _adamw(w_in[0].T, f_wint, m_w_in[0].T, v_w_in[0].T)
    delta["w_in"], new_m["w_in"], new_v["w_in"] = dl.T[None], mn.T[None], vn.T[None]
    for k in big_names[1:]:
        shp = weights[k].shape
        two = lambda a: a.reshape(shp[-2], shp[-1])
        dl, mn, vn = _adamw(two(weights[k]), two(grads[k]), two(moms[k][0]), two(moms[k][1]))
        delta[k], new_m[k], new_v[k] = dl.reshape(shp), mn.reshape(shp), vn.reshape(shp)
    small_names = [k for k in names if k not in big_names]
    sizes = [weights[k].size for k in small_names]
    total = sum(sizes)
    padded = -(-total // 1024) * 1024

    def flat(get):
        v = jnp.concatenate([get(k).reshape(-1) for k in small_names])
        return jnp.pad(v, (0, padded - total)).reshape(padded // 128, 128)

    ones = jnp.ones((padded - total,), F32)
    vflat = jnp.concatenate([jnp.concatenate([moms[k][1].reshape(-1) for k in small_names]), ones])
    dl, mn, vn = _adamw(flat(lambda k: weights[k]), flat(lambda k: grads[k]), flat(lambda k: moms[k][0]),
                        vflat.reshape(padded // 128, 128))
    pos = 0
    for k, sz in zip(small_names, sizes):
        shp = weights[k].shape
        delta[k] = dl.reshape(-1)[pos:pos + sz].reshape(shp)
        new_m[k] = mn.reshape(-1)[pos:pos + sz].reshape(shp)
        new_v[k] = vn.reshape(-1)[pos:pos + sz].reshape(shp)
        pos += sz

    return (loss, grad_x, *[grads[k] for k in names], *[delta[k] for k in names],
            *[new_m[k] for k in names], *[new_v[k] for k in names])
```

```python
import functools

import jax
import jax.numpy as jnp
from jax import lax
from jax.experimental import pallas as pl
from jax.experimental.pallas import tpu as pltpu

F32 = jnp.float32
BF16 = jnp.bfloat16

D_MODEL = 1024
N_META = 16
D_CONV = 512
CONV_WIDTH = 31
GLA_HEADS = 4
GLA_DV = 128
GLA_DK = 64
GLA_RANK = 16
GLA_TAU = 16.0
CHUNK = 64
D_FF = 4096
LN_EPS = 1e-5
ALPHA = 2.0 ** 0.25
D_IN = 2576
D_IN_PAD = 2688
PAD = CHUNK - N_META
HEAD = PAD + N_META
Q_SCALE = GLA_DK ** -0.5
ADAM_LR, ADAM_B1, ADAM_B2, ADAM_EPS, ADAM_WD, ADAM_STEP = 0.001, 0.9, 0.999, 1e-08, 0.01, 10
HALF = D_MODEL // 2
VMEM_LIMIT = 56 * 1024 * 1024
MESH = pl.DeviceIdType.MESH

C_VAL, C_GATE, C_Q, C_K, C_V, C_R, C_GD = 0, 512, 1024, 1280, 1536, 2048, 2560

_SMALL_FIELDS = (("loss", 8), ("ln_in_g", 8), ("ln_in_b", 8), ("conv_b", 4), ("conv_ln_g", 4), ("conv_ln_b", 4),
                 ("gate_bias", 2), ("gla_norm_g", 1), ("ln1_g", 8), ("ln1_b", 8), ("ln2_g", 8), ("ln2_b", 8),
                 ("conv_w", 124), ("gate_up", 32), ("meta_tokens", 128))
SMALL_ROWS = 360


def _params(sem=None, **kw):
    return pltpu.CompilerParams(dimension_semantics=sem, vmem_limit_bytes=VMEM_LIMIT, **kw)


def _row_tile(tp):
    for t in (704, 352, 192, 64):
        if tp % t == 0:
            return t
    raise ValueError(tp)


def _sub_rows(tm):
    return [slice(0, tm)]


def _dot(a, b, dims, precision=None):
    return lax.dot_general(a, b, (dims, ((), ())), preferred_element_type=F32, precision=precision)


def _nn(a, b, **kw):
    return _dot(a, b, ((1,), (0,)), **kw)


def _nt(a, b, **kw):
    return _dot(a, b, ((1,), (1,)), **kw)


def _tn(a, b, **kw):
    return _dot(a, b, ((0,), (0,)), **kw)


def _sigmoid(x):
    return 1.0 / (1.0 + jnp.exp(-x))


def _log_sigmoid(z):
    return jnp.minimum(z, 0.0) - jnp.log(1.0 + jnp.exp(-jnp.abs(z)))


def _ln_stats(t):
    mu = jnp.mean(t, axis=-1, keepdims=True)
    d = t - mu
    var = jnp.mean(d * d, axis=-1, keepdims=True)
    rstd = lax.rsqrt(var + LN_EPS)
    return d * rstd, rstd


def _ln_bwd(dxhat, xhat, rstd):
    m1 = jnp.mean(dxhat, axis=-1, keepdims=True)
    m2 = jnp.mean(dxhat * xhat, axis=-1, keepdims=True)
    return rstd * (dxhat - m1 - xhat * m2)


def _mesh_pos():
    return lax.axis_index("x"), lax.axis_index("y"), lax.axis_index("c")


ANY = pl.BlockSpec(memory_space=pl.ANY)


def _gather_sems(n):
    return [pltpu.SemaphoreType.DMA((n, 3))] * 4


def _gather_steps(ins, outs, sems):
    n = len(ins)
    send, recv, fsend, frecv = sems
    half = [ref.shape[0] // 2 for ref in ins]
    x, y, c = _mesh_pos()
    me = 2 * x + y
    sibling = (x, y, 1 - c)
    chips = [(1 - x, y), (x, 1 - y), (1 - x, 1 - y)]
    chip_idx = [2 * px + py for px, py in chips]
    mine = [pl.ds(pl.multiple_of(c * h, 16), h) for h in half]
    other = [pl.ds(pl.multiple_of((1 - c) * h, 16), h) for h in half]
    pairs = [(a, k) for a in range(n) for k in range(3)]

    def ici(a, k, slab):
        return pltpu.make_async_remote_copy(
            src_ref=ins[a].at[mine[a], :], dst_ref=outs[a].at[slab, mine[a], :],
            send_sem=send.at[a, k], recv_sem=recv.at[a, k], device_id=(*chips[k], c), device_id_type=MESH)

    def forward(a, k, rows):
        blk = outs[a].at[chip_idx[k], rows[a], :]
        return pltpu.make_async_remote_copy(
            src_ref=blk, dst_ref=blk, send_sem=fsend.at[a, k], recv_sem=frecv.at[a, k],
            device_id=sibling, device_id_type=MESH)

    def start():
        for a, k in pairs:
            ici(a, k, me).start()

    def relay():
        for a, k in pairs:
            ici(a, k, chip_idx[k]).wait_recv()
            forward(a, k, mine).start()

    def finish():
        for a, k in pairs:
            forward(a, k, other).wait_recv()
        for a, k in pairs:
            ici(a, k, me).wait_send()
            forward(a, k, mine).wait_send()

    return start, relay, finish


def _place_own(gathered, shards):
    chip = 2 * lax.axis_index("x") + lax.axis_index("y")
    return [lax.dynamic_update_slice(g, s[None], (chip, 0, 0)) for g, s in zip(gathered, shards)]


def _gather_weights(shards, small):
    n = len(shards)

    def body(*refs):
        ins, small_in = refs[:n], refs[n]
        outs, small_out = refs[n + 1:2 * n + 1], refs[2 * n + 1]
        ssend, srecv = refs[2 * n + 6:]
        start, relay, finish = _gather_steps(ins, outs, refs[2 * n + 2:2 * n + 6])
        x, y, c = _mesh_pos()
        chips = [(1 - x, y), (x, 1 - y), (1 - x, 1 - y)]

        def small_copy(k, slot):
            return pltpu.make_async_remote_copy(
                src_ref=small_in, dst_ref=small_out.at[slot], send_sem=ssend.at[k], recv_sem=srecv.at[k],
                device_id=(*chips[k], c), device_id_type=MESH)

        start()
        for k in range(3):
            small_copy(k, 2 * x + y).start()
        relay()
        finish()
        for k, (px, py) in enumerate(chips):
            small_copy(k, 2 * px + py).wait_recv()
        for k in range(3):
            small_copy(k, 2 * x + y).wait_send()

    out_shape = [jax.ShapeDtypeStruct((4,) + s.shape, s.dtype) for s in shards]
    out_shape.append(jax.ShapeDtypeStruct((4,) + small.shape, small.dtype))
    res = pl.pallas_call(
        body, name="gather_weights", out_shape=out_shape,
        in_specs=[ANY] * (n + 1), out_specs=[ANY] * (n + 1),
        scratch_shapes=_gather_sems(n) + [pltpu.SemaphoreType.DMA((3,)), pltpu.SemaphoreType.DMA((3,))],
    )(*shards, small)
    res = _place_own(res, list(shards) + [small])
    return res[:n], res[n]


def _pair_exchange_steps(ins, outs, sems):
    send, recv = sems
    x, y, c = _mesh_pos()
    other = pl.ds(pl.multiple_of((1 - c) * HALF, 128), HALF)
    cps = [pltpu.make_async_remote_copy(
        src_ref=ins[a].at[:, :, other], dst_ref=outs[a], send_sem=send.at[a], recv_sem=recv.at[a],
        device_id=(x, y, 1 - c), device_id_type=MESH) for a in range(len(ins))]

    def start():
        for cp in cps:
            cp.start()

    def finish():
        for cp in cps:
            cp.wait()

    return start, finish


def _pair_exchange_shapes(grads):
    return [jax.ShapeDtypeStruct(g.shape[:2] + (HALF,), g.dtype) for g in grads]


def _pair_exchange(grads):
    n = len(grads)

    def body(*refs):
        start, finish = _pair_exchange_steps(refs[:n], refs[n:2 * n], refs[2 * n:])
        start()
        finish()

    return pl.pallas_call(
        body, name="grad_pair_exchange", out_shape=_pair_exchange_shapes(grads),
        in_specs=[ANY] * n, out_specs=[ANY] * n,
        scratch_shapes=[pltpu.SemaphoreType.DMA((n,)), pltpu.SemaphoreType.DMA((n,))],
    )(*grads)


NO_FUSE = ()


def _fused_call(body, fuse, *, name, grid, in_specs, out_specs, out_shape, scratch_shapes, operands):
    fuse = [(kind, list(arrays)) for kind, arrays in fuse if len(arrays)]
    n_in, n_out, n_s = len(in_specs), len(out_shape), len(scratch_shapes)
    table = {
        "gather": (_gather_steps, lambda a: [jax.ShapeDtypeStruct((4,) + s.shape, s.dtype) for s in a], _gather_sems),
        "exchange": (_chip_exchange_steps, _chip_exchange_shapes, _chip_exchange_sems),
        "pair": (_pair_exchange_steps, _pair_exchange_shapes,
                 lambda n: [pltpu.SemaphoreType.DMA((n,)), pltpu.SemaphoreType.DMA((n,))]),
    }
    comm = [a for _, arrays in fuse for a in arrays]
    shapes = [s for kind, arrays in fuse for s in table[kind][1](arrays)]
    sems = [table[kind][2](len(arrays)) for kind, arrays in fuse]
    nc = len(comm)

    def wrapped(*refs):
        o0 = n_in + nc
        s0 = o0 + n_out + nc
        at, sem_at, steps = 0, s0 + n_s, []
        for (kind, arrays), sm in zip(fuse, sems):
            k = len(arrays)
            steps.append(table[kind][0](refs[n_in + at:n_in + at + k], refs[o0 + n_out + at:o0 + n_out + at + k],
                                        refs[sem_at:sem_at + len(sm)]))
            at, sem_at = at + k, sem_at + len(sm)
        first, last = _grid_ends(grid)
        for st in steps:
            pl.when(first)(st[0])
        for st in steps:
            for mid in st[1:-1]:
                pl.when(last)(mid)
        body(*refs[:n_in], *refs[o0:o0 + n_out], *refs[s0:s0 + n_s])
        for st in steps:
            pl.when(last)(st[-1])

    res = pl.pallas_call(
        wrapped if nc else body, name=name, grid=grid, in_specs=list(in_specs) + [ANY] * nc,
        out_specs=list(out_specs) + [ANY] * nc, out_shape=list(out_shape) + shapes,
        scratch_shapes=list(scratch_shapes) + [s for sm in sems for s in sm],
        compiler_params=_params(("arbitrary",) * len(grid)))(*operands, *comm)
    outs, got, results = list(res[:n_out]), list(res[n_out:]), []
    for kind, arrays in fuse:
        mine, got = got[:len(arrays)], got[len(arrays):]
        results.append(_place_own(mine, arrays) if kind == "gather" else mine)
    return outs, results


def _chip_exchange_sems(n):
    return [pltpu.SemaphoreType.DMA((n, 3))] * 2


def _chip_exchange_shapes(parts):
    return [jax.ShapeDtypeStruct((3,) + p.shape[1:], p.dtype) for p in parts]


def _chip_exchange_steps(ins, outs, sems):
    send, recv = sems
    x, y, c = _mesh_pos()
    chips = [(1 - x, y), (x, 1 - y), (1 - x, 1 - y)]
    cps = [pltpu.make_async_remote_copy(
        src_ref=ins[a].at[2 * px + py], dst_ref=outs[a].at[k], send_sem=send.at[a, k], recv_sem=recv.at[a, k],
        device_id=(px, py, c), device_id_type=MESH) for a in range(len(ins)) for k, (px, py) in enumerate(chips)]

    def start():
        for cp in cps:
            cp.start()

    def finish():
        for cp in cps:
            cp.wait()

    return start, finish


def _chip_exchange(parts):
    n = len(parts)

    def body(*refs):
        start, finish = _chip_exchange_steps(refs[:n], refs[n:2 * n], refs[2 * n:])
        start()
        finish()

    return pl.pallas_call(
        body, name="grad_chip_exchange", out_shape=_chip_exchange_shapes(parts),
        in_specs=[ANY] * n, out_specs=[ANY] * n, scratch_shapes=_chip_exchange_sems(n),
    )(*parts)


def _pair_swap(halves):
    n = len(halves)

    def body(*refs):
        ins, outs = refs[:n], refs[n:2 * n]
        send, recv = refs[2 * n:]
        x, y, c = _mesh_pos()
        cps = [pltpu.make_async_remote_copy(
            src_ref=ins[a], dst_ref=outs[a], send_sem=send.at[a], recv_sem=recv.at[a],
            device_id=(x, y, 1 - c), device_id_type=MESH) for a in range(n)]
        for cp in cps:
            cp.start()
        for cp in cps:
            cp.wait()

    return pl.pallas_call(
        body, name="grad_pair_swap",
        out_shape=[jax.ShapeDtypeStruct(h.shape, h.dtype) for h in halves],
        in_specs=[ANY] * n, out_specs=[ANY] * n,
        scratch_shapes=[pltpu.SemaphoreType.DMA((n,)), pltpu.SemaphoreType.DMA((n,))],
    )(*halves)


def _allgather_small(pack):
    m_per, ncol = pack.shape

    def body(x_ref, out_ref, send_sems, recv_sems, local_sem):
        x, y, c = _mesh_pos()
        me, sibling = (x, y, c), (x, y, 1 - c)
        chips = [(1 - x, y), (x, 1 - y), (1 - x, 1 - y)]

        def rows(px, py, pc):
            return out_ref.at[pl.ds(pl.multiple_of((4 * px + 2 * py + pc) * m_per, 8), m_per), :]

        def copy(k, block, to, src=None):
            return pltpu.make_async_remote_copy(
                src_ref=rows(*block) if src is None else src, dst_ref=rows(*block),
                send_sem=send_sems.at[k], recv_sem=recv_sems.at[k], device_id=to, device_id_type=MESH)

        mine = pltpu.make_async_copy(x_ref, rows(*me), local_sem)
        mine.start()
        first = [copy(0, me, sibling, src=x_ref)]
        first += [copy(1 + j, me, (*chip, c), src=x_ref) for j, chip in enumerate(chips)]
        for cp in first:
            cp.start()
        passed = [copy(4 + j, (*chip, c), sibling) for j, chip in enumerate(chips)]
        for j, chip in enumerate(chips):
            copy(1 + j, (*chip, c), me).wait_recv()
            passed[j].start()
        copy(0, sibling, me).wait_recv()
        for j, chip in enumerate(chips):
            copy(4 + j, (*chip, 1 - c), me).wait_recv()
        for cp in first + passed:
            cp.wait_send()
        mine.wait()

    return pl.pallas_call(
        body, name="allgather_small",
        out_shape=jax.ShapeDtypeStruct((8 * m_per, ncol), pack.dtype),
        in_specs=[pl.BlockSpec(memory_space=pltpu.VMEM)],
        out_specs=pl.BlockSpec(memory_space=pltpu.VMEM),
        scratch_shapes=[pltpu.SemaphoreType.DMA((7,)), pltpu.SemaphoreType.DMA((7,)), pltpu.SemaphoreType.DMA],
    )(pack)


def _add_pair(g, got, c_arr):
    _, rows, _ = g.shape

    def body(c_ref, g_ref, r_ref, o_ref):
        o_ref[...] = (g_ref[...] + r_ref[...]).astype(BF16)

    return pl.pallas_call(
        body, name="grad_add_pair", out_shape=jax.ShapeDtypeStruct((4, rows, HALF), BF16),
        grid_spec=pltpu.PrefetchScalarGridSpec(
            num_scalar_prefetch=1, grid=(4,),
            in_specs=[pl.BlockSpec((1, rows, HALF), lambda j, c: (j, 0, c[0])),
                      pl.BlockSpec((1, rows, HALF), lambda j, c: (j, 0, 0))],
            out_specs=pl.BlockSpec((1, rows, HALF), lambda j, c: (j, 0, 0))),
        compiler_params=_params(("arbitrary",)),
    )(c_arr, g, got)


def _add_chips(g, pair_got, chip_got, pos_arr):
    _, rows, _ = g.shape

    def body(pos_ref, g_ref, p_ref, r_ref, o_ref):
        own = g_ref[0] + p_ref[0]
        o_ref[...] = ((own + r_ref[0].astype(F32)) + r_ref[1].astype(F32)) + r_ref[2].astype(F32)

    return pl.pallas_call(
        body, name="grad_add_chips", out_shape=jax.ShapeDtypeStruct((rows, HALF), F32),
        grid_spec=pltpu.PrefetchScalarGridSpec(
            num_scalar_prefetch=1, grid=(1,),
            in_specs=[pl.BlockSpec((1, rows, HALF), lambda i, p: (p[0], 0, p[1])),
                      pl.BlockSpec((1, rows, HALF), lambda i, p: (p[0], 0, 0)),
                      pl.BlockSpec((3, rows, HALF), lambda i, p: (0, 0, 0))],
            out_specs=pl.BlockSpec((rows, HALF), lambda i, p: (0, 0))),
        compiler_params=_params(("arbitrary",)),
    )(pos_arr, g, pair_got, chip_got)


def _sum_small(gathered):
    def body(g_ref, o_ref):
        acc = g_ref[0:SMALL_ROWS, :]
        for d in range(1, 8):
            acc = acc + g_ref[d * SMALL_ROWS:(d + 1) * SMALL_ROWS, :]
        o_ref[...] = acc

    return pl.pallas_call(
        body, name="sum_small", out_shape=jax.ShapeDtypeStruct((SMALL_ROWS, 128), F32),
    )(gathered)


def _adamw(w, g, m, v):
    rows, cols = w.shape
    tr = 256 if rows % 256 == 0 else rows
    c1 = 1.0 - ADAM_B1 ** ADAM_STEP
    c2 = 1.0 - ADAM_B2 ** ADAM_STEP

    def body(w_ref, g_ref, m_ref, v_ref, d_ref, mo_ref, vo_ref):
        gg = g_ref[...]
        mn = ADAM_B1 * m_ref[...] + (1.0 - ADAM_B1) * gg
        vn = ADAM_B2 * v_ref[...] + (1.0 - ADAM_B2) * (gg * gg)
        d_ref[...] = -ADAM_LR * ((mn / c1) / (jnp.sqrt(vn / c2) + ADAM_EPS) + ADAM_WD * w_ref[...])
        mo_ref[...] = mn
        vo_ref[...] = vn

    spec = pl.BlockSpec((tr, cols), lambda i: (i, 0))
    return pl.pallas_call(
        body, name="adamw", out_shape=[jax.ShapeDtypeStruct(w.shape, F32)] * 3,
        grid=(rows // tr,), in_specs=[spec] * 4, out_specs=[spec] * 3,
        compiler_params=_params(("parallel",)),
    )(w, g, m, v)


def _ln_in_fwd(x, meta, g, b):
    bsz, s, d = x.shape
    tp = s + HEAD
    nh = 2
    sh = s // nh
    rc = min(256, sh)

    def body(x_ref, meta_ref, g_ref, b_ref, s0_ref, s0b_ref):
        h = pl.program_id(1)
        gg, bb = g_ref[...], b_ref[...]

        @pl.when(h == 0)
        def _():
            s0_ref[0, 0:PAD, :] = jnp.zeros((PAD, d), F32)
            s0b_ref[0, 0:PAD, :] = jnp.zeros((PAD, d), BF16)
            mh, _ = _ln_stats(meta_ref[...])
            mv = mh * gg + bb
            s0_ref[0, PAD:HEAD, :] = mv
            s0b_ref[0, PAD:HEAD, :] = mv.astype(BF16)

        def step(i, carry):
            src = pl.ds(pl.multiple_of(i * rc, rc), rc)
            dst = pl.ds(pl.multiple_of(HEAD + h * sh + i * rc, 64), rc)
            xh, _ = _ln_stats(x_ref[0, src, :])
            val = xh * gg + bb
            s0_ref[0, dst, :] = val
            s0b_ref[0, dst, :] = val.astype(BF16)
            return carry

        lax.fori_loop(0, sh // rc, step, 0)

    full = lambda bi, hi: (bi, 0, 0)
    return pl.pallas_call(
        body, name="ln_in_fwd",
        out_shape=[jax.ShapeDtypeStruct((bsz, tp, d), F32), jax.ShapeDtypeStruct((bsz, tp, d), BF16)],
        grid=(bsz, nh),
        in_specs=[pl.BlockSpec((1, sh, d), lambda bi, hi: (bi, hi, 0)),
                  pl.BlockSpec((N_META, d), lambda bi, hi: (0, 0)),
                  pl.BlockSpec((1, d), lambda bi, hi: (0, 0)),
                  pl.BlockSpec((1, d), lambda bi, hi: (0, 0))],
        out_specs=[pl.BlockSpec((1, tp, d), full)] * 2,
        compiler_params=_params(("parallel", "arbitrary")),
    )(x, meta, g, b)


def _in_proj(s0b, w_int, fuse=NO_FUSE):
    r, d = s0b.shape
    tm = _row_tile(r)

    def body(a_ref, w_ref, o_ref):
        o_ref[...] = _nt(a_ref[...], w_ref[...])

    (u,), got = _fused_call(
        body, fuse, name="in_proj", out_shape=[jax.ShapeDtypeStruct((r, D_IN_PAD), F32)],
        grid=(r // tm,),
        in_specs=[pl.BlockSpec((tm, d), lambda i: (i, 0)), pl.BlockSpec((D_IN_PAD, d), lambda i: (0, 0))],
        out_specs=[pl.BlockSpec((tm, D_IN_PAD), lambda i: (i, 0))], scratch_shapes=[], operands=(s0b, w_int))
    return u, got


def _conv_fwd(u, conv_w, conv_b, fuse=NO_FUSE):
    bsz, tp, _ = u.shape
    nchunk = tp // CHUNK
    win = CHUNK + 32
    nct = D_CONV // 128

    def body(cv_ref, cg_ref, w_ref, cb_ref, hc_ref, h_scr, win_scr):
        h_scr[0:32, :] = jnp.zeros((32, 128), F32)
        h_scr[32:32 + tp, :] = cv_ref[0] * _sigmoid(cg_ref[0])
        cb = cb_ref[...]

        def step(n, carry):
            r0 = pl.multiple_of(n * CHUNK, CHUNK)
            win_scr[...] = h_scr[pl.ds(r0, win), :]
            acc = jnp.zeros((CHUNK, 128), F32)
            for j in range(CONV_WIDTH):
                acc = acc + w_ref[j:j + 1, :] * win_scr[2 + j:2 + j + CHUNK, :]
            hc_ref[0, pl.ds(r0, CHUNK), :] = acc + cb
            return carry

        lax.fori_loop(0, nchunk, step, 0)

    (hc,), got = _fused_call(
        body, fuse, name="conv_fwd", out_shape=[jax.ShapeDtypeStruct((bsz, tp, D_CONV), F32)],
        grid=(bsz, nct),
        in_specs=[pl.BlockSpec((1, tp, 128), lambda bi, t: (bi, 0, C_VAL // 128 + t)),
                  pl.BlockSpec((1, tp, 128), lambda bi, t: (bi, 0, C_GATE // 128 + t)),
                  pl.BlockSpec((32, 128), lambda bi, t: (0, t)),
                  pl.BlockSpec((1, 128), lambda bi, t: (0, t))],
        out_specs=[pl.BlockSpec((1, tp, 128), lambda bi, t: (bi, 0, t))],
        scratch_shapes=[pltpu.VMEM((tp + 32, 128), F32), pltpu.VMEM((win, 128), F32)],
        operands=(u, u, conv_w, conv_b))
    return hc, got


def _gla_group(nchunk):
    return 11 if nchunk % 11 == 0 else nchunk


def _bdot(a, b, ca, cb, precision=None):
    return lax.dot_general(a, b, (((ca,), (cb,)), ((0,), (0,))), preferred_element_type=F32, precision=precision)


def _bnn(a, b, **kw):
    return _bdot(a, b, 2, 1, **kw)


def _bnt(a, b, **kw):
    return _bdot(a, b, 2, 2, **kw)


def _gla_consts(nb):
    row = lax.broadcasted_iota(jnp.int32, (nb, CHUNK, CHUNK), 1)
    col = lax.broadcasted_iota(jnp.int32, (nb, CHUNK, CHUNK), 2)
    lane = lax.broadcasted_iota(jnp.int32, (1, 1, 128), 2)
    return row >= col, row <= col, [lane < GLA_DK, lane >= GLA_DK]


def _gla_group_terms(g, nb, q_ref, k_ref, gd_ref, gup_ref, gb_ref, tril):
    m = nb * CHUNK
    rows = pl.ds(pl.multiple_of(g * m, CHUNK), m)
    z = _nn(gd_ref[0, rows, :].astype(BF16), gup_ref[...]) + gb_ref[...]
    valid = g * m + lax.broadcasted_iota(jnp.int32, (m, 1), 0) >= PAD
    lg = jnp.where(valid, _log_sigmoid(z) * (1.0 / GLA_TAU), 0.0)
    bcum = _bnn(tril.astype(F32), lg.reshape(nb, CHUNK, 128), precision=lax.Precision.HIGHEST)
    blast = bcum[:, CHUNK - 1:CHUNK, :]
    eb = jnp.exp(bcum)
    enb = jnp.exp(-bcum)
    erest = jnp.exp(blast - bcum)
    q = (q_ref[0, rows, :] * Q_SCALE).reshape(nb, CHUNK, 128)
    k = k_ref[0, rows, :].reshape(nb, CHUNK, 128)
    return rows, valid, z, eb, enb, erest, jnp.exp(blast), q * eb, k * enb, k * erest


def _grid_ends(grid):
    ids = [pl.program_id(i) for i in range(len(grid))]
    first = functools.reduce(jnp.logical_and, [i == 0 for i in ids])
    last = functools.reduce(jnp.logical_and, [i == g - 1 for i, g in zip(ids, grid)])
    return first, last


def _gla_fwd(u, gup, gbias, gnorm, fuse=NO_FUSE):
    bsz, tp, _ = u.shape
    nchunk = tp // CHUNK
    nb = _gla_group(nchunk)

    def body(q_ref, k_ref, v_ref, r_ref, gd_ref, gup_ref, gb_ref, gn_ref, out_ref, o_ref, st_ref, s_scr):
        tril, _, hmask = _gla_consts(nb)
        s_scr[...] = jnp.zeros_like(s_scr)
        gn = gn_ref[...]

        def group(g, carry):
            rows, _, _, _, _, _, dec, qe, ke, kd = _gla_group_terms(g, nb, q_ref, k_ref, gd_ref, gup_ref, gb_ref, tril)
            keb, kdb = ke.astype(BF16), kd.astype(BF16)
            for h in range(2):
                cols = slice(h * GLA_DV, (h + 1) * GLA_DV)
                qh = jnp.where(hmask[h], qe, 0.0).astype(BF16)
                vh = v_ref[0, rows, cols].astype(BF16).reshape(nb, CHUNK, GLA_DV)
                a = jnp.where(tril, _bnt(qh, keb), 0.0).astype(BF16)
                st = s_scr[h]
                sts = []
                for n in range(nb):
                    st_ref[0, h, g * nb + n] = st
                    sts.append(st.astype(BF16))
                    st = dec[n] * st + _tn(vh[n], kdb[n])
                s_scr[h] = st
                o = (_bnn(a, vh) + _bnt(qh, jnp.stack(sts))).reshape(nb * CHUNK, GLA_DV)
                o_ref[0, rows, cols] = o
                rms = lax.rsqrt(jnp.mean(o * o, axis=-1, keepdims=True) + LN_EPS)
                rh = r_ref[0, rows, cols]
                out_ref[0, rows, cols] = (o * rms * gn * (rh * _sigmoid(rh))).astype(BF16)
            return carry

        lax.fori_loop(0, nchunk // nb, group, 0)

    res, got = _fused_call(
        body, fuse, name="gla_fwd",
        out_shape=[jax.ShapeDtypeStruct((bsz, tp, 512), BF16), jax.ShapeDtypeStruct((bsz, tp, 512), F32),
                   jax.ShapeDtypeStruct((bsz, GLA_HEADS, nchunk, GLA_DV, 128), F32)],
        grid=(bsz, 2),
        in_specs=[pl.BlockSpec((1, tp, 128), lambda bi, p: (bi, 0, C_Q // 128 + p)),
                  pl.BlockSpec((1, tp, 128), lambda bi, p: (bi, 0, C_K // 128 + p)),
                  pl.BlockSpec((1, tp, 256), lambda bi, p: (bi, 0, C_V // 256 + p)),
                  pl.BlockSpec((1, tp, 256), lambda bi, p: (bi, 0, C_R // 256 + p)),
                  pl.BlockSpec((1, tp, 128), lambda bi, p: (bi, 0, C_GD // 128)),
                  pl.BlockSpec((128, 128), lambda bi, p: (0, p)),
                  pl.BlockSpec((1, 128), lambda bi, p: (0, p)),
                  pl.BlockSpec((1, 128), lambda bi, p: (0, 0))],
        out_specs=[pl.BlockSpec((1, tp, 256), lambda bi, p: (bi, 0, p)),
                   pl.BlockSpec((1, tp, 256), lambda bi, p: (bi, 0, p)),
                   pl.BlockSpec((1, 2, nchunk, GLA_DV, 128), lambda bi, p: (bi, p, 0, 0, 0))],
        scratch_shapes=[pltpu.VMEM((2, GLA_DV, 128), F32)],
        operands=(u, u, u, u, u, gup, gbias, gnorm))
    return res[0], res[1], res[2], got


def _out_proj_ln1(hc, gla_out, w_out, s0, cg, cb, g1, b1, fuse=NO_FUSE):
    r, d = s0.shape
    tm = _row_tile(r)

    def body(hc_ref, a_ref, w_ref, s0_ref, cg_ref, cb_ref, g_ref, b_ref, co_ref, xh_ref, rstd_ref, s1b_ref):
        for rs in _sub_rows(tm):
            xc, _ = _ln_stats(hc_ref[rs, :])
            nv = xc * cg_ref[...] + cb_ref[...]
            co = (nv * _sigmoid(nv)).astype(BF16)
            co_ref[rs, :] = co
            mix = _nn(co, w_ref[0:D_CONV, :]) + _nn(a_ref[rs, :], w_ref[D_CONV:, :])
            xh, rstd = _ln_stats(ALPHA * s0_ref[rs, :] + mix)
            xh_ref[rs, :] = xh
            rstd_ref[rs, :] = rstd
            s1b_ref[rs, :] = (xh * g_ref[...] + b_ref[...]).astype(BF16)

    row = lambda n: pl.BlockSpec((tm, n), lambda i: (i, 0))
    vec = lambda n: pl.BlockSpec((1, n), lambda i: (0, 0))
    res, got = _fused_call(
        body, fuse, name="out_proj_ln1",
        out_shape=[jax.ShapeDtypeStruct((r, D_CONV), BF16), jax.ShapeDtypeStruct((r, d), F32),
                   jax.ShapeDtypeStruct((r, 1), F32), jax.ShapeDtypeStruct((r, d), BF16)],
        grid=(r // tm,),
        in_specs=[row(D_CONV), row(512), pl.BlockSpec((d, d), lambda i: (0, 0)), row(d),
                  vec(D_CONV), vec(D_CONV), vec(d), vec(d)],
        out_specs=[row(D_CONV), row(d), row(1), row(d)], scratch_shapes=[],
        operands=(hc, gla_out, w_out, s0, cg, cb, g1, b1))
    return res[0], res[1], res[2], res[3], got


def _ffn1(s1b, w1t, fuse=NO_FUSE):
    r, d = s1b.shape
    tm = _row_tile(r)

    def body(a_ref, w_ref, o_ref):
        o_ref[...] = jnp.maximum(_nt(a_ref[...], w_ref[...]), 0.0).astype(BF16)

    (ra,), got = _fused_call(
        body, fuse, name="ffn1", out_shape=[jax.ShapeDtypeStruct((r, D_FF), BF16)], grid=(r // tm,),
        in_specs=[pl.BlockSpec((tm, d), lambda i: (i, 0)), pl.BlockSpec((D_FF, d), lambda i: (0, 0))],
        out_specs=[pl.BlockSpec((tm, D_FF), lambda i: (i, 0))], scratch_shapes=[], operands=(s1b, w1t))
    return ra, got


def _ffn2_ln2_loss(ra, w2, xhat1, g1, b1, g2, b2, tgt, tp):
    r, d = xhat1.shape
    tm = _row_tile(tp)
    per = tp // tm

    def body(ra_ref, w_ref, xh1_ref, g1_ref, b1_ref, g2_ref, b2_ref, tgt_ref, dt_ref, dtb_ref, acc_ref, t_ref, sem):
        i = pl.program_id(0)
        b, j = i // per, i % per

        @pl.when(i == 0)
        def _():
            acc_ref[...] = jnp.zeros_like(acc_ref)

        head_copy = pltpu.make_async_copy(tgt_ref.at[b, pl.ds(0, tm - HEAD), :], t_ref.at[pl.ds(HEAD, tm - HEAD), :], sem)
        body_copy = pltpu.make_async_copy(
            tgt_ref.at[b, pl.ds(pl.multiple_of(jnp.maximum(j * tm - HEAD, 0), 64), tm), :], t_ref, sem)

        @pl.when(j == 0)
        def _():
            t_ref[0:HEAD, :] = jnp.zeros((HEAD, d), F32)
            head_copy.start()

        pl.when(j > 0)(body_copy.start)

        sums = [jnp.zeros((1, d), F32)] * 3
        for rs in _sub_rows(tm):
            rb = ra_ref[rs, :]
            f = _nn(rb * rb, w_ref[...])
            if rs.start == 0:
                pl.when(j == 0)(head_copy.wait)
                pl.when(j > 0)(body_copy.wait)
            s1 = xh1_ref[rs, :] * g1_ref[...] + b1_ref[...]
            xh2, rstd2 = _ln_stats(ALPHA * s1 + f)
            y = xh2 * g2_ref[...] + b2_ref[...]
            rowid = (i % per) * tm + rs.start + lax.broadcasted_iota(jnp.int32, (rs.stop - rs.start, 1), 0)
            e = jnp.where(rowid >= HEAD, y - t_ref[rs, :], 0.0)
            dy = e * (1.0 / d)
            dt2 = _ln_bwd(dy * g2_ref[...], xh2, rstd2)
            dt_ref[rs, :] = dt2
            dtb_ref[rs, :] = dt2.astype(BF16)
            sums = [sums[0] + (0.5 / d) * jnp.sum(e * e, axis=0, keepdims=True),
                    sums[1] + jnp.sum(dy * xh2, axis=0, keepdims=True), sums[2] + jnp.sum(dy, axis=0, keepdims=True)]
        for k in range(3):
            acc_ref[k:k + 1, :] += sums[k]

    row = lambda n: pl.BlockSpec((tm, n), lambda i: (i, 0))
    vec = pl.BlockSpec((1, d), lambda i: (0, 0))
    return pl.pallas_call(
        body, name="ffn2_ln2_loss",
        out_shape=[jax.ShapeDtypeStruct((r, d), F32), jax.ShapeDtypeStruct((r, d), BF16),
                   jax.ShapeDtypeStruct((8, d), F32)],
        grid=(r // tm,),
        in_specs=[row(D_FF), pl.BlockSpec((D_FF, d), lambda i: (0, 0)), row(d), vec, vec, vec, vec, ANY],
        out_specs=[row(d), row(d), pl.BlockSpec((8, d), lambda i: (0, 0))],
        scratch_shapes=[pltpu.VMEM((tm, d), F32), pltpu.SemaphoreType.DMA],
        compiler_params=_params(("arbitrary",)),
    )(ra, w2, xhat1, g1, b1, g2, b2, tgt)


def _ffn_bwd_da(dt2b, w2, ra):
    r, d = dt2b.shape
    tm = _row_tile(r)

    def body(g_ref, w_ref, ra_ref, o_ref):
        o_ref[...] = (_nt(g_ref[...], w_ref[...]) * (2.0 * ra_ref[...].astype(F32))).astype(BF16)

    return pl.pallas_call(
        body, name="ffn_bwd_da", out_shape=jax.ShapeDtypeStruct((r, D_FF), BF16),
        grid=(r // tm,),
        in_specs=[pl.BlockSpec((tm, d), lambda i: (i, 0)), pl.BlockSpec((D_FF, d), lambda i: (0, 0)),
                  pl.BlockSpec((tm, D_FF), lambda i: (i, 0))],
        out_specs=pl.BlockSpec((tm, D_FF), lambda i: (i, 0)),
        compiler_params=_params(("parallel",)),
    )(dt2b, w2, ra)


def _ffn_bwd_ln1(da, w1t, dt2, xhat1, rstd1, g1):
    r, d = dt2.shape
    tm = _row_tile(r)

    def body(da_ref, w_ref, dt2_ref, xh_ref, rstd_ref, g_ref, dt_ref, dtb_ref, acc_ref):
        @pl.when(pl.program_id(0) == 0)
        def _():
            acc_ref[...] = jnp.zeros_like(acc_ref)

        sums = [jnp.zeros((1, d), F32)] * 2
        for rs in _sub_rows(tm):
            ds1 = ALPHA * dt2_ref[rs, :] + _nn(da_ref[rs, :], w_ref[...])
            xh = xh_ref[rs, :]
            dt1 = _ln_bwd(ds1 * g_ref[...], xh, rstd_ref[rs, :])
            dt_ref[rs, :] = dt1
            dtb_ref[rs, :] = dt1.astype(BF16)
            sums = [sums[0] + jnp.sum(ds1 * xh, axis=0, keepdims=True), sums[1] + jnp.sum(ds1, axis=0, keepdims=True)]
        for k in range(2):
            acc_ref[k:k + 1, :] += sums[k]

    row = lambda n: pl.BlockSpec((tm, n), lambda i: (i, 0))
    return pl.pallas_call(
        body, name="ffn_bwd_ln1",
        out_shape=[jax.ShapeDtypeStruct((r, d), F32), jax.ShapeDtypeStruct((r, d), BF16),
                   jax.ShapeDtypeStruct((8, d), F32)],
        grid=(r // tm,),
        in_specs=[row(D_FF), pl.BlockSpec((D_FF, d), lambda i: (0, 0)), row(d), row(d), row(1),
                  pl.BlockSpec((1, d), lambda i: (0, 0))],
        out_specs=[row(d), row(d), pl.BlockSpec((8, d), lambda i: (0, 0))],
        compiler_params=_params(("arbitrary",)),
    )(da, w1t, dt2, xhat1, rstd1, g1)


def _matmul_tn(lhs, rhs, bm, square_lhs=False, name="matmul_tn", fuse=NO_FUSE):
    r, m = lhs.shape
    n = rhs.shape[1]
    tk = _row_tile(r)

    def body(a_ref, b_ref, o_ref):
        @pl.when(pl.program_id(1) == 0)
        def _():
            o_ref[...] = jnp.zeros_like(o_ref)

        a = a_ref[...]
        if square_lhs:
            a = a * a
        o_ref[...] += _tn(a, b_ref[...])

    (out,), got = _fused_call(
        body, fuse, name=name, out_shape=[jax.ShapeDtypeStruct((m, n), F32)], grid=(m // bm, r // tk),
        in_specs=[pl.BlockSpec((tk, bm), lambda i, k: (k, i)), pl.BlockSpec((tk, n), lambda i, k: (k, 0))],
        out_specs=[pl.BlockSpec((bm, n), lambda i, k: (i, 0))], scratch_shapes=[], operands=(lhs, rhs))
    return out, got


def _out_proj_bwd(dt1b, w_out, hc, cg, cb, fuse=NO_FUSE):
    r, d = dt1b.shape
    tm = _row_tile(r)

    def body(g_ref, w_ref, hc_ref, cg_ref, cb_ref, dhc_ref, dgla_ref, acc_ref):
        @pl.when(pl.program_id(0) == 0)
        def _():
            acc_ref[...] = jnp.zeros_like(acc_ref)

        gg = cg_ref[...]
        sums = [jnp.zeros((1, D_CONV), F32)] * 3
        for rs in _sub_rows(tm):
            dmix = _nt(g_ref[rs, :], w_ref[...])
            dgla_ref[rs, :] = dmix[:, D_CONV:]
            xh, rstd = _ln_stats(hc_ref[rs, :])
            nv = xh * gg + cb_ref[...]
            sig = _sigmoid(nv)
            dn = dmix[:, :D_CONV] * (sig * (1.0 + nv * (1.0 - sig)))
            dhc = _ln_bwd(dn * gg, xh, rstd)
            dhc_ref[rs, :] = dhc
            sums = [sums[0] + jnp.sum(dhc, axis=0, keepdims=True), sums[1] + jnp.sum(dn * xh, axis=0, keepdims=True),
                    sums[2] + jnp.sum(dn, axis=0, keepdims=True)]
        for k in range(3):
            acc_ref[k:k + 1, :] += sums[k]

    row = lambda n: pl.BlockSpec((tm, n), lambda i: (i, 0))
    vec = pl.BlockSpec((1, D_CONV), lambda i: (0, 0))
    res, got = _fused_call(
        body, fuse, name="out_proj_bwd",
        out_shape=[jax.ShapeDtypeStruct((r, D_CONV), F32), jax.ShapeDtypeStruct((r, 512), F32),
                   jax.ShapeDtypeStruct((8, D_CONV), F32)],
        grid=(r // tm,),
        in_specs=[row(d), pl.BlockSpec((d, d), lambda i: (0, 0)), row(D_CONV), vec, vec],
        out_specs=[row(D_CONV), row(512), pl.BlockSpec((8, D_CONV), lambda i: (0, 0))], scratch_shapes=[],
        operands=(dt1b, w_out, hc, cg, cb))
    return res[0], res[1], res[2], got


def _conv_bwd(dhc, u, conv_w, fuse=NO_FUSE):
    bsz, tp, _ = u.shape
    nchunk = tp // CHUNK
    win = CHUNK + 32
    nct = D_CONV // 128

    def body(dhc_ref, cv_ref, cg_ref, w_ref, dv_ref, dg_ref, dw_ref, h_scr, dhc_scr, hwin, dwin, dw_scr):
        h_scr[0:32, :] = jnp.zeros((32, 128), F32)
        h_scr[32:32 + tp, :] = cv_ref[0] * _sigmoid(cg_ref[0])
        dhc_scr[0:tp, :] = dhc_ref[0]
        dhc_scr[tp:tp + 32, :] = jnp.zeros((32, 128), F32)
        dw_scr[...] = jnp.zeros_like(dw_scr)

        def step(n, carry):
            r0 = pl.multiple_of(n * CHUNK, CHUNK)
            rows = pl.ds(r0, CHUNK)
            hwin[...] = h_scr[pl.ds(r0, win), :]
            dwin[...] = dhc_scr[pl.ds(r0, win), :]
            dcur = dwin[0:CHUNK, :]
            acc = jnp.zeros((CHUNK, 128), F32)
            for j in range(CONV_WIDTH):
                acc = acc + w_ref[j:j + 1, :] * dwin[30 - j:30 - j + CHUNK, :]
                prod = dcur * hwin[2 + j:2 + j + CHUNK, :]
                dw_scr[j * 8:(j + 1) * 8, :] += jnp.sum(prod.reshape(CHUNK // 8, 8, 128), axis=0)
            cg = cg_ref[0, rows, :]
            sig = _sigmoid(cg)
            rowid = n * CHUNK + lax.broadcasted_iota(jnp.int32, (CHUNK, 1), 0)
            dh = jnp.where(rowid >= PAD, acc, 0.0)
            dv_ref[0, rows, :] = (dh * sig).astype(BF16)
            dg_ref[0, rows, :] = (dh * cv_ref[0, rows, :] * sig * (1.0 - sig)).astype(BF16)
            return carry

        lax.fori_loop(0, nchunk, step, 0)
        dw_ref[0] = jnp.zeros((32, 128), F32)
        for j in range(CONV_WIDTH):
            dw_ref[0, j:j + 1, :] = jnp.sum(dw_scr[j * 8:(j + 1) * 8, :], axis=0, keepdims=True)

    blk = lambda off: pl.BlockSpec((1, tp, 128), lambda bi, t: (bi, 0, off // 128 + t))
    res, got = _fused_call(
        body, fuse, name="conv_bwd",
        out_shape=[jax.ShapeDtypeStruct((bsz, tp, D_CONV), BF16), jax.ShapeDtypeStruct((bsz, tp, D_CONV), BF16),
                   jax.ShapeDtypeStruct((bsz, 32, D_CONV), F32)],
        grid=(bsz, nct),
        in_specs=[blk(0), blk(C_VAL), blk(C_GATE), pl.BlockSpec((32, 128), lambda bi, t: (0, t))],
        out_specs=[blk(0), blk(0), pl.BlockSpec((1, 32, 128), lambda bi, t: (bi, 0, t))],
        scratch_shapes=[pltpu.VMEM((tp + 32, 128), F32), pltpu.VMEM((tp + 32, 128), F32),
                        pltpu.VMEM((win, 128), F32), pltpu.VMEM((win, 128), F32),
                        pltpu.VMEM((CONV_WIDTH * 8, 128), F32)],
        operands=(dhc, u, u, conv_w))
    return res[0], res[1], res[2], got


def _gla_bwd(dgla, u, o_pre, states, gup, gbias, gnorm, fuse=NO_FUSE):
    bsz, tp, _ = u.shape
    nchunk = tp // CHUNK
    nb = _gla_group(nchunk)

    def body(dy_ref, q_ref, k_ref, v_ref, r_ref, gd_ref, o_ref, st_ref, gup_ref, gb_ref, gn_ref,
             dq_ref, dk_ref, dv_ref, dr_ref, dgd_ref, dgup_ref, vec_ref, h_scr, gup_acc):
        tril, triu, hmask = _gla_consts(nb)
        h_scr[...] = jnp.zeros_like(h_scr)
        gup_acc[...] = jnp.zeros_like(gup_acc)
        gn = gn_ref[...]
        gupb = gup_ref[...]
        m = nb * CHUNK
        ngroup = nchunk // nb

        def group(i, carry):
            dbias, dgn = carry
            g = ngroup - 1 - i
            rows, valid, z, eb, enb, erest, dec, qe, ke, kd = _gla_group_terms(
                g, nb, q_ref, k_ref, gd_ref, gup_ref, gb_ref, tril)
            keb, kdb = ke.astype(BF16), kd.astype(BF16)
            dqe = jnp.zeros((nb, CHUNK, 128), F32)
            dke = jnp.zeros((nb, CHUNK, 128), F32)
            dkd = jnp.zeros((nb, CHUNK, 128), F32)
            ddec = jnp.zeros((nb, 1, 128), F32)
            for h in range(2):
                cols = slice(h * GLA_DV, (h + 1) * GLA_DV)
                o = o_ref[0, rows, cols]
                rh = r_ref[0, rows, cols]
                dy = dy_ref[0, rows, cols]
                rms = lax.rsqrt(jnp.mean(o * o, axis=-1, keepdims=True) + LN_EPS)
                nrm = o * rms
                sig = _sigmoid(rh)
                sw = rh * sig
                dr_ref[0, rows, cols] = (dy * nrm * gn * (sig * (1.0 + rh * (1.0 - sig)))).astype(BF16)
                dgn = dgn + jnp.sum(dy * nrm * sw, axis=0, keepdims=True)
                dn = dy * gn * sw
                do = rms * (dn - nrm * jnp.mean(dn * nrm, axis=-1, keepdims=True))
                dob = do.astype(BF16).reshape(nb, CHUNK, GLA_DV)
                qh = jnp.where(hmask[h], qe, 0.0).astype(BF16)
                vh = v_ref[0, rows, cols].astype(BF16).reshape(nb, CHUNK, GLA_DV)
                ht = h_scr[h]
                hts = [None] * nb
                for n in reversed(range(nb)):
                    hts[n] = ht
                    ht = dec[n] * ht + _tn(dob[n], qh[n])
                h_scr[h] = ht
                htf = jnp.stack(hts)
                htb = htf.astype(BF16)
                st = st_ref[0, h, pl.ds(g * nb, nb)]
                at = jnp.where(triu, _bnt(keb, qh), 0.0).astype(BF16)
                da = jnp.where(tril, _bnt(dob, vh), 0.0).astype(BF16)
                dat = jnp.where(triu, _bnt(vh, dob), 0.0).astype(BF16)
                dqe = dqe + jnp.where(hmask[h], _bnn(da, keb) + _bnn(dob, st.astype(BF16)), 0.0)
                dke = dke + _bnn(dat, qh)
                dv_ref[0, rows, cols] = (_bnn(at, dob) + _bnt(kdb, htb)).reshape(m, GLA_DV).astype(BF16)
                dkd = dkd + jnp.where(hmask[h], _bnn(vh, htb), 0.0)
                ddec = ddec + jnp.where(hmask[h], jnp.sum(htf * st, axis=1, keepdims=True), 0.0)
            dq_ref[0, rows, :] = (dqe * eb * Q_SCALE).reshape(m, 128).astype(BF16)
            dk_ref[0, rows, :] = (dke * enb + dkd * erest).reshape(m, 128).astype(BF16)
            db = dqe * qe - dke * ke - dkd * kd
            dblast = jnp.sum(dkd * kd, axis=1, keepdims=True) + ddec * dec
            lastrow = lax.broadcasted_iota(jnp.int32, (1, CHUNK, 1), 1) == CHUNK - 1
            db = db + jnp.where(lastrow, dblast, 0.0)
            dlg = _bnn(triu.astype(F32), db, precision=lax.Precision.HIGHEST).reshape(m, 128)
            dz = jnp.where(valid, dlg, 0.0) * (1.0 / GLA_TAU) * (1.0 - _sigmoid(z))
            dzb = dz.astype(BF16)
            dgd_ref[0, 0, rows, :] = _nt(dzb, gupb).astype(BF16)
            gup_acc[...] += _tn(gd_ref[0, rows, :].astype(BF16), dzb)
            return dbias + jnp.sum(dz, axis=0, keepdims=True), dgn

        zero = jnp.zeros((1, 128), F32)
        dbias, dgn = lax.fori_loop(0, ngroup, group, (zero, zero))
        dgup_ref[0] = gup_acc[...]
        vec_ref[0] = jnp.zeros((8, 128), F32)
        vec_ref[0, 0:1, :] = dbias
        vec_ref[0, 1:2, :] = dgn

    pair = lambda w, off: pl.BlockSpec((1, tp, w), lambda bi, p: (bi, 0, off // w + p))
    return _fused_call(
        body, fuse, name="gla_bwd",
        out_shape=[jax.ShapeDtypeStruct((bsz, tp, 256), BF16), jax.ShapeDtypeStruct((bsz, tp, 256), BF16),
                   jax.ShapeDtypeStruct((bsz, tp, 512), BF16), jax.ShapeDtypeStruct((bsz, tp, 512), BF16),
                   jax.ShapeDtypeStruct((bsz, 2, tp, 128), BF16), jax.ShapeDtypeStruct((bsz, 128, 256), F32),
                   jax.ShapeDtypeStruct((bsz, 8, 256), F32)],
        grid=(bsz, 2),
        in_specs=[pair(256, 0), pair(128, C_Q), pair(128, C_K), pair(256, C_V), pair(256, C_R),
                  pl.BlockSpec((1, tp, 128), lambda bi, p: (bi, 0, C_GD // 128)),
                  pair(256, 0),
                  pl.BlockSpec((1, 2, nchunk, GLA_DV, 128), lambda bi, p: (bi, p, 0, 0, 0)),
                  pl.BlockSpec((128, 128), lambda bi, p: (0, p)),
                  pl.BlockSpec((1, 128), lambda bi, p: (0, p)),
                  pl.BlockSpec((1, 128), lambda bi, p: (0, 0))],
        out_specs=[pair(128, 0), pair(128, 0), pair(256, 0), pair(256, 0),
                   pl.BlockSpec((1, 1, tp, 128), lambda bi, p: (bi, p, 0, 0)),
                   pl.BlockSpec((1, 128, 128), lambda bi, p: (bi, 0, p)),
                   pl.BlockSpec((1, 8, 128), lambda bi, p: (bi, 0, p))],
        scratch_shapes=[pltpu.VMEM((2, GLA_DV, 128), F32), pltpu.VMEM((128, 128), F32)],
        operands=(dgla, u, u, u, u, u, o_pre, states, gup, gbias, gnorm))


_DU_OFFSETS = (C_VAL, C_GATE, C_Q, C_K, C_V, C_R)
_DU_WIDTHS = (512, 512, 256, 256, 512, 512)


def _du_specs(tm, per, row_map):
    specs = [pl.BlockSpec((tm, w), row_map) for w in _DU_WIDTHS]
    for p in range(2):
        specs.append(pl.BlockSpec((1, 1, tm, 128), lambda *ix, p=p: (row_map(*ix)[0] // per, p, row_map(*ix)[0] % per, 0)))
    return specs


def _du_pieces(refs):
    out = [(off, ref[...]) for off, ref in zip(_DU_OFFSETS, refs[:6])]
    dgd = (refs[6][0, 0].astype(F32) + refs[7][0, 0].astype(F32)).astype(BF16)
    out.append((C_GD, dgd))
    return out


def _in_proj_bwd(pieces, dgd, w_int, dt1, tp, fuse=NO_FUSE):
    r, d = dt1.shape
    tm = _row_tile(tp)
    per = tp // tm

    def body(*refs):
        w_ref, dt_ref, o_ref = refs[8:]
        acc = ALPHA * dt_ref[...]
        for off, val in _du_pieces(refs[:8]):
            acc = acc + _nn(val, w_ref[off:off + val.shape[1], :])
        o_ref[...] = acc

    row = lambda i: (i, 0)
    (ds0,), got = _fused_call(
        body, fuse, name="in_proj_bwd", out_shape=[jax.ShapeDtypeStruct((r, d), F32)], grid=(r // tm,),
        in_specs=_du_specs(tm, per, row) + [pl.BlockSpec((D_IN_PAD, d), lambda i: (0, 0)), pl.BlockSpec((tm, d), row)],
        out_specs=[pl.BlockSpec((tm, d), row)], scratch_shapes=[], operands=(*pieces, dgd, dgd, w_int, dt1))
    return ds0, got


def _grad_w_in(pieces, dgd, s0b, tp, fuse=NO_FUSE):
    r, d = s0b.shape
    tk = _row_tile(tp)
    per = tp // tk

    def body(*refs):
        s_ref, o_ref = refs[8:]

        @pl.when(pl.program_id(0) == 0)
        def _():
            o_ref[...] = jnp.zeros_like(o_ref)

        s = s_ref[...]
        for off, val in _du_pieces(refs[:8]):
            o_ref[off:off + val.shape[1], :] += _tn(val, s)

    row = lambda k: (k, 0)
    (out,), got = _fused_call(
        body, fuse, name="grad_w_in", out_shape=[jax.ShapeDtypeStruct((D_IN_PAD, d), F32)], grid=(r // tk,),
        in_specs=_du_specs(tk, per, row) + [pl.BlockSpec((tk, d), row)],
        out_specs=[pl.BlockSpec((D_IN_PAD, d), lambda k: (0, 0))], scratch_shapes=[],
        operands=(*pieces, dgd, dgd, s0b))
    return out, got


def _ln_in_bwd(ds0, x, meta, g):
    bsz, s, d = x.shape
    tp = s + HEAD
    nh = 2
    sh = s // nh
    rc = min(256, sh)

    def body(ds_ref, x_ref, meta_ref, g_ref, gx_ref, dm_ref, vec_ref):
        h = pl.program_id(1)
        gg = g_ref[...]

        @pl.when(h == 0)
        def _():
            mh, mr = _ln_stats(meta_ref[...])
            dsm = ds_ref[0, PAD:HEAD, :]
            dm_ref[0] = _ln_bwd(dsm * gg, mh, mr)
            vec_ref[0] = jnp.zeros((8, d), F32)
            vec_ref[0, 0:1, :] = jnp.sum(dsm * mh, axis=0, keepdims=True)
            vec_ref[0, 1:2, :] = jnp.sum(dsm, axis=0, keepdims=True)

        def step(i, carry):
            sg, sb = carry
            dst = pl.ds(pl.multiple_of(i * rc, rc), rc)
            src = pl.ds(pl.multiple_of(HEAD + h * sh + i * rc, 64), rc)
            xh, rstd = _ln_stats(x_ref[0, dst, :])
            dsv = ds_ref[0, src, :]
            gx_ref[0, dst, :] = _ln_bwd(dsv * gg, xh, rstd)
            return sg + jnp.sum(dsv * xh, axis=0, keepdims=True), sb + jnp.sum(dsv, axis=0, keepdims=True)

        zero = jnp.zeros((1, d), F32)
        sg, sb = lax.fori_loop(0, sh // rc, step, (zero, zero))
        vec_ref[0, 0:1, :] += sg
        vec_ref[0, 1:2, :] += sb

    return pl.pallas_call(
        body, name="ln_in_bwd",
        out_shape=[jax.ShapeDtypeStruct((bsz, s, d), F32), jax.ShapeDtypeStruct((bsz, N_META, d), F32),
                   jax.ShapeDtypeStruct((bsz, 8, d), F32)],
        grid=(bsz, nh),
        in_specs=[pl.BlockSpec((1, tp, d), lambda bi, hi: (bi, 0, 0)),
                  pl.BlockSpec((1, sh, d), lambda bi, hi: (bi, hi, 0)),
                  pl.BlockSpec((N_META, d), lambda bi, hi: (0, 0)),
                  pl.BlockSpec((1, d), lambda bi, hi: (0, 0))],
        out_specs=[pl.BlockSpec((1, sh, d), lambda bi, hi: (bi, hi, 0)),
                   pl.BlockSpec((1, N_META, d), lambda bi, hi: (bi, 0, 0)),
                   pl.BlockSpec((1, 8, d), lambda bi, hi: (bi, 0, 0))],
        compiler_params=_params(("parallel", "arbitrary")),
    )(ds0, x, meta, g)


def _rows128(a):
    return a.reshape(-1, 128)


def kernel(x, meta_tokens, ln_in_g, ln_in_b, w_in, conv_w, conv_b, conv_ln_g, conv_ln_b, gate_up, gate_bias, gla_norm_g, w_out, ln1_g, ln1_b, w_ff1, w_ff2, ln2_g, ln2_b, loss_target, m_meta_tokens, m_ln_in_g, m_ln_in_b, m_w_in, m_conv_w, m_conv_b, m_conv_ln_g, m_conv_ln_b, m_gate_up, m_gate_bias, m_gla_norm_g, m_w_out, m_ln1_g, m_ln1_b, m_w_ff1, m_w_ff2, m_ln2_g, m_ln2_b, v_meta_tokens, v_ln_in_g, v_ln_in_b, v_w_in, v_conv_w, v_conv_b, v_conv_ln_g, v_conv_ln_b, v_gate_up, v_gate_bias, v_gla_norm_g, v_w_out, v_ln1_g, v_ln1_b, v_w_ff1, v_w_ff2, v_ln2_g, v_ln2_b):
    bsz, seq, d = x.shape
    tp = seq + HEAD
    r = bsz * tp
    xi, yi, ci = _mesh_pos()
    chip = 2 * xi + yi
    c_arr = jnp.reshape(ci, (1,)).astype(jnp.int32)
    pos_arr = jnp.stack([chip, ci]).astype(jnp.int32)

    sh_in = D_IN // 4
    shard_in = jnp.pad(w_in[0].T.astype(BF16), ((0, D_IN_PAD // 4 - sh_in), (0, 0)))
    shard_w1t, shard_wout, shard_w2 = w_ff1[0].T.astype(BF16), w_out[0].astype(BF16), w_ff2[0].astype(BF16)
    small_w = jnp.concatenate([_rows128(meta_tokens), _rows128(conv_w[0]), _rows128(gate_up[0])], axis=0)
    (g_int,), g_small = _gather_weights([shard_in], small_w)
    w_int = jnp.pad(g_int[:, :sh_in].reshape(D_IN, d), ((0, D_IN_PAD - D_IN), (0, 0)))
    n_meta_rows, n_cw_rows = N_META * 256 // 128, CONV_WIDTH * 128 // 128
    meta_full = jnp.concatenate([g_small[j, :n_meta_rows].reshape(N_META, 256) for j in range(4)], axis=1)
    convw_full = jnp.concatenate(
        [g_small[j, n_meta_rows:n_meta_rows + n_cw_rows].reshape(CONV_WIDTH, 128) for j in range(4)], axis=1)
    gup_full = jnp.concatenate(
        [g_small[j, n_meta_rows + n_cw_rows:].reshape(GLA_RANK, 64) for j in range(4)], axis=1)
    convw_p = jnp.pad(convw_full, ((0, 1), (0, 0)))
    gup_p = jnp.pad(gup_full, ((0, 128 - GLA_RANK), (0, 0))).astype(BF16)
    ln_in_g2, ln_in_b2 = ln_in_g.reshape(1, d), ln_in_b.reshape(1, d)

    s0, s0b = _ln_in_fwd(x, meta_full, ln_in_g2, ln_in_b2)
    s0f, s0bf = s0.reshape(r, d), s0b.reshape(r, d)
    u, ((g_w1a,),) = _in_proj(s0bf, w_int, [("gather", [shard_w1t[:640]])])
    u3 = u.reshape(bsz, tp, D_IN_PAD)
    hc, ((g_w1b,),) = _conv_fwd(u3, convw_p, conv_b, [("gather", [shard_w1t[640:]])])
    hc = hc.reshape(r, D_CONV)
    gla_out, o_pre, states, ((g_wout, g_w2a),) = _gla_fwd(
        u3, gup_p, gate_bias, gla_norm_g, [("gather", [shard_wout, shard_w2[:256]])])
    wout = g_wout.reshape(d, d)
    gla_of = gla_out.reshape(r, 512)
    conv_of, xhat1, rstd1, s1b, ((g_w2b,),) = _out_proj_ln1(
        hc, gla_of, wout, s0f, conv_ln_g, conv_ln_b, ln1_g, ln1_b, [("gather", [shard_w2[256:640]])])
    w1t = jnp.concatenate([g_w1a, g_w1b], axis=1).reshape(D_FF, d)
    ra, ((g_w2c,),) = _ffn1(s1b, w1t, [("gather", [shard_w2[640:]])])
    w2 = jnp.concatenate([g_w2a, g_w2b, g_w2c], axis=1).reshape(D_FF, d)
    dt2, dt2b, acc2 = _ffn2_ln2_loss(ra, w2, xhat1, ln1_g, ln1_b, ln2_g, ln2_b, loss_target, tp)

    def add_pair(g, got):
        return _add_pair(g, got, c_arr)

    da = _ffn_bwd_da(dt2b, w2, ra)
    dt1, dt1b, acc1 = _ffn_bwd_ln1(da, w1t, dt2, xhat1, rstd1, ln1_g)
    g_w2, _ = _matmul_tn(ra, dt2b, 1024, square_lhs=True, name="grad_w_ff2")
    big_w2 = g_w2.reshape(4, D_FF // 4, d)
    g_w1t, ((pair_w2,),) = _matmul_tn(da, s1b, 1024, name="grad_w_ff1", fuse=[("pair", [big_w2])])
    big_w1 = g_w1t.reshape(4, D_FF // 4, d)
    dhc, dgla, cacc, ((pair_w1,),) = _out_proj_bwd(dt1b, wout, hc, conv_ln_g, conv_ln_b, [("pair", [big_w1])])
    part_w1, part_w2 = add_pair(big_w1, pair_w1), add_pair(big_w2, pair_w2)
    g_wout = jnp.concatenate([_matmul_tn(conv_of, dt1b, D_CONV, name="grad_w_out_conv")[0],
                              _matmul_tn(gla_of, dt1b, 512, name="grad_w_out_gla")[0]], axis=0)
    big_wout = g_wout.reshape(4, d // 4, d)
    dcv, dcg, dcw, ((chip_w1,), (pair_wout,)) = _conv_bwd(
        dhc.reshape(bsz, tp, D_CONV), u3, convw_p, [("exchange", [part_w1]), ("pair", [big_wout])])
    part_wout = add_pair(big_wout, pair_wout)
    (dq, dk, dv, dr, dgd, dgup, gvec), ((chip_w2a,),) = _gla_bwd(
        dgla.reshape(bsz, tp, 512), u3, o_pre, states, gup_p, gate_bias, gla_norm_g,
        [("exchange", [part_w2[:, :512]])])
    pieces = [a.reshape(r, a.shape[-1]) for a in (dcv, dcg, dq, dk, dv, dr)]
    g_wint, ((chip_w2b, chip_wout),) = _grad_w_in(
        pieces, dgd, s0bf, tp, [("exchange", [part_w2[:, 512:], part_wout])])
    chip_w2 = jnp.concatenate([chip_w2a, chip_w2b], axis=1)
    big_win = jnp.stack([g_wint[j * sh_in:(j + 1) * sh_in] for j in range(4)])
    (pair_win,) = _pair_exchange([big_win])
    ds0, ((chip_win,),) = _in_proj_bwd(pieces, dgd, w_int, dt1, tp, [("exchange", [add_pair(big_win, pair_win)])])
    grad_x, dmeta, lvec = _ln_in_bwd(ds0.reshape(bsz, tp, d), x, meta_full, ln_in_g2)

    small = {
        "loss": acc2[0:1], "ln_in_g": jnp.sum(lvec[:, 0], axis=0), "ln_in_b": jnp.sum(lvec[:, 1], axis=0),
        "conv_b": cacc[0], "conv_ln_g": cacc[1], "conv_ln_b": cacc[2], "gate_bias": jnp.sum(gvec[:, 0], axis=0),
        "gla_norm_g": jnp.sum(gvec[:, 1].reshape(bsz * 2, 128), axis=0),
        "ln1_g": acc1[0], "ln1_b": acc1[1], "ln2_g": acc2[1], "ln2_b": acc2[2],
        "conv_w": jnp.sum(dcw, axis=0)[:CONV_WIDTH], "gate_up": jnp.sum(dgup, axis=0)[:GLA_RANK],
        "meta_tokens": jnp.sum(dmeta, axis=0),
    }
    pack = jnp.concatenate([_rows128(small[k]) for k, _ in _SMALL_FIELDS], axis=0)
    pack = jnp.pad(pack, ((0, SMALL_ROWS - pack.shape[0]), (0, 0)))
    red = _sum_small(_allgather_small(pack))
    off, tot = {}, 0
    for k, nrows in _SMALL_FIELDS:
        off[k] = (tot, nrows)
        tot += nrows

    def field(k, shape):
        o, nrows = off[k]
        return red[o:o + nrows].reshape(shape)

    loss = jnp.sum(field("loss", (d,)))

    halves = [_add_chips(g, t, s, pos_arr) for g, t, s in (
        (big_win, pair_win, chip_win), (big_wout, pair_wout, chip_wout), (big_w1, pair_w1, chip_w1),
        (big_w2, pair_w2, chip_w2))]
    f_wint, f_wout, f_w1t, f_w2 = [
        jnp.concatenate([jnp.where(ci == 0, own, sib), jnp.where(ci == 0, sib, own)], axis=1)
        for own, sib in zip(halves, _pair_swap(halves))]

    def cols(a, width):
        return lax.dynamic_slice_in_dim(a, chip * width, width, axis=a.ndim - 1)

    grads = {
        "meta_tokens": cols(field("meta_tokens", (N_META, d)), 256),
        "ln_in_g": field("ln_in_g", (d,)), "ln_in_b": field("ln_in_b", (d,)),
        "w_in": f_wint.T[None], "conv_w": cols(field("conv_w", (CONV_WIDTH, D_CONV)), 128)[None],
        "conv_b": field("conv_b", (1, D_CONV)), "conv_ln_g": field("conv_ln_g", (1, D_CONV)),
        "conv_ln_b": field("conv_ln_b", (1, D_CONV)),
        "gate_up": cols(field("gate_up", (GLA_RANK, 256)), 64)[None],
        "gate_bias": field("gate_bias", (1, 256)), "gla_norm_g": field("gla_norm_g", (1, 128)),
        "w_out": f_wout[None], "ln1_g": field("ln1_g", (1, d)), "ln1_b": field("ln1_b", (1, d)),
        "w_ff1": f_w1t.T[None], "w_ff2": f_w2[None],
        "ln2_g": field("ln2_g", (1, d)), "ln2_b": field("ln2_b", (1, d)),
    }
    weights = dict(meta_tokens=meta_tokens, ln_in_g=ln_in_g, ln_in_b=ln_in_b, w_in=w_in, conv_w=conv_w, conv_b=conv_b,
                   conv_ln_g=conv_ln_g, conv_ln_b=conv_ln_b, gate_up=gate_up, gate_bias=gate_bias,
                   gla_norm_g=gla_norm_g, w_out=w_out, ln1_g=ln1_g, ln1_b=ln1_b, w_ff1=w_ff1, w_ff2=w_ff2,
                   ln2_g=ln2_g, ln2_b=ln2_b)
    moms = dict(meta_tokens=(m_meta_tokens, v_meta_tokens), ln_in_g=(m_ln_in_g, v_ln_in_g),
                ln_in_b=(m_ln_in_b, v_ln_in_b), w_in=(m_w_in, v_w_in), conv_w=(m_conv_w, v_conv_w),
                conv_b=(m_conv_b, v_conv_b), conv_ln_g=(m_conv_ln_g, v_conv_ln_g),
                conv_ln_b=(m_conv_ln_b, v_conv_ln_b), gate_up=(m_gate_up, v_gate_up),
                gate_bias=(m_gate_bias, v_gate_bias), gla_norm_g=(m_gla_norm_g, v_gla_norm_g),
                w_out=(m_w_out, v_w_out), ln1_g=(m_ln1_g, v_ln1_g), ln1_b=(m_ln1_b, v_ln1_b),
                w_ff1=(m_w_ff1, v_w_ff1), w_ff2=(m_w_ff2, v_w_ff2), ln2_g=(m_ln2_g, v_ln2_g),
                ln2_b=(m_ln2_b, v_ln2_b))
    names = list(weights)
    big_names = ("w_in", "w_out", "w_ff1", "w_ff2")
    delta, new_m, new_v = {}, {}, {}
    dl, mn, vn = _adamw(w_in[0].T, f_wint, m_w_in[0].T, v_w_in[0].T)
    delta["w_in"], new_m["w_in"], new_v["w_in"] = dl.T[None], mn.T[None], vn.T[None]
    for k in big_names[1:]:
        shp = weights[k].shape
        two = lambda a: a.reshape(shp[-2], shp[-1])
        dl, mn, vn = _adamw(two(weights[k]), two(grads[k]), two(moms[k][0]), two(moms[k][1]))
        delta[k], new_m[k], new_v[k] = dl.reshape(shp), mn.reshape(shp), vn.reshape(shp)
    small_names = [k for k in names if k not in big_names]
    sizes = [weights[k].size for k in small_names]
    total = sum(sizes)
    padded = -(-total // 1024) * 1024

    def flat(get):
        v = jnp.concatenate([get(k).reshape(-1) for k in small_names])
        return jnp.pad(v, (0, padded - total)).reshape(padded // 128, 128)

    ones = jnp.ones((padded - total,), F32)
    vflat = jnp.concatenate([jnp.concatenate([moms[k][1].reshape(-1) for k in small_names]), ones])
    dl, mn, vn = _adamw(flat(lambda k: weights[k]), flat(lambda k: grads[k]), flat(lambda k: moms[k][0]),
                        vflat.reshape(padded // 128, 128))
    pos = 0
    for k, sz in zip(small_names, sizes):
        shp = weights[k].shape
        delta[k] = dl.reshape(-1)[pos:pos + sz].reshape(shp)
        new_m[k] = mn.reshape(-1)[pos:pos + sz].reshape(shp)
        new_v[k] = vn.reshape(-1)[pos:pos + sz].reshape(shp)
        pos += sz

    return (loss, grad_x, *[grads[k] for k in names], *[delta[k] for k in names],
            *[new_m[k] for k in names], *[new_v[k] for k in names])
```

```python
import functools

import jax
import jax.numpy as jnp
from jax import lax
from jax.experimental import pallas as pl
from jax.experimental.pallas import tpu as pltpu

F32 = jnp.float32
BF16 = jnp.bfloat16

D_MODEL = 1024
N_META = 16
D_CONV = 512
CONV_WIDTH = 31
GLA_HEADS = 4
GLA_DV = 128
GLA_DK = 64
GLA_RANK = 16
GLA_TAU = 16.0
CHUNK = 64
D_FF = 4096
LN_EPS = 1e-5
ALPHA = 2.0 ** 0.25
D_IN = 2576
D_IN_PAD = 2688
PAD = CHUNK - N_META
HEAD = PAD + N_META
Q_SCALE = GLA_DK ** -0.5
ADAM_LR, ADAM_B1, ADAM_B2, ADAM_EPS, ADAM_WD, ADAM_STEP = 0.001, 0.9, 0.999, 1e-08, 0.01, 10
HALF = D_MODEL // 2
VMEM_LIMIT = 56 * 1024 * 1024
MESH = pl.DeviceIdType.MESH

C_VAL, C_GATE, C_Q, C_K, C_V, C_R, C_GD = 0, 512, 1024, 1280, 1536, 2048, 2560

_SMALL_FIELDS = (("loss", 8), ("ln_in_g", 8), ("ln_in_b", 8), ("conv_b", 4), ("conv_ln_g", 4), ("conv_ln_b", 4),
                 ("gate_bias", 2), ("gla_norm_g", 1), ("ln1_g", 8), ("ln1_b", 8), ("ln2_g", 8), ("ln2_b", 8),
                 ("conv_w", 124), ("gate_up", 32), ("meta_tokens", 128))
SMALL_ROWS = 360


def _params(sem=None, **kw):
    return pltpu.CompilerParams(dimension_semantics=sem, vmem_limit_bytes=VMEM_LIMIT, **kw)


def _row_tile(tp):
    for t in (704, 352, 192, 64):
        if tp % t == 0:
            return t
    raise ValueError(tp)


def _sub_rows(tm):
    return [slice(0, tm)]


def _dot(a, b, dims, precision=None):
    return lax.dot_general(a, b, (dims, ((), ())), preferred_element_type=F32, precision=precision)


def _nn(a, b, **kw):
    return _dot(a, b, ((1,), (0,)), **kw)


def _nt(a, b, **kw):
    return _dot(a, b, ((1,), (1,)), **kw)


def _tn(a, b, **kw):
    return _dot(a, b, ((0,), (0,)), **kw)


def _sigmoid(x):
    return 1.0 / (1.0 + jnp.exp(-x))


def _log_sigmoid(z):
    return jnp.minimum(z, 0.0) - jnp.log(1.0 + jnp.exp(-jnp.abs(z)))


def _ln_stats(t):
    mu = jnp.mean(t, axis=-1, keepdims=True)
    d = t - mu
    var = jnp.mean(d * d, axis=-1, keepdims=True)
    rstd = lax.rsqrt(var + LN_EPS)
    return d * rstd, rstd


def _ln_bwd(dxhat, xhat, rstd):
    m1 = jnp.mean(dxhat, axis=-1, keepdims=True)
    m2 = jnp.mean(dxhat * xhat, axis=-1, keepdims=True)
    return rstd * (dxhat - m1 - xhat * m2)


def _mesh_pos():
    return lax.axis_index("x"), lax.axis_index("y"), lax.axis_index("c")


ANY = pl.BlockSpec(memory_space=pl.ANY)


def _gather_sems(n):
    return [pltpu.SemaphoreType.DMA((n, 3))] * 4


def _gather_steps(ins, outs, sems, ranges=None):
    n = len(ins)
    send, recv, fsend, frecv = sems
    ranges = ranges or [(0, ref.shape[0]) for ref in ins]
    x, y, c = _mesh_pos()
    me = 2 * x + y
    sibling = (x, y, 1 - c)
    chips = [(1 - x, y), (x, 1 - y), (1 - x, 1 - y)]
    chip_idx = [2 * px + py for px, py in chips]
    mine = [pl.ds(pl.multiple_of(r0 + c * (nr // 2), 16), nr // 2) for r0, nr in ranges]
    other = [pl.ds(pl.multiple_of(r0 + (1 - c) * (nr // 2), 16), nr // 2) for r0, nr in ranges]
    pairs = [(a, k) for a in range(n) for k in range(3)]

    def ici(a, k, slab):
        return pltpu.make_async_remote_copy(
            src_ref=ins[a].at[mine[a], :], dst_ref=outs[a].at[slab, mine[a], :],
            send_sem=send.at[a, k], recv_sem=recv.at[a, k], device_id=(*chips[k], c), device_id_type=MESH)

    def forward(a, k, rows):
        blk = outs[a].at[chip_idx[k], rows[a], :]
        return pltpu.make_async_remote_copy(
            src_ref=blk, dst_ref=blk, send_sem=fsend.at[a, k], recv_sem=frecv.at[a, k],
            device_id=sibling, device_id_type=MESH)

    def start():
        for a, k in pairs:
            ici(a, k, me).start()

    def relay():
        for a, k in pairs:
            ici(a, k, chip_idx[k]).wait_recv()
            forward(a, k, mine).start()

    def finish():
        for a, k in pairs:
            forward(a, k, other).wait_recv()
        for a, k in pairs:
            ici(a, k, me).wait_send()
            forward(a, k, mine).wait_send()

    return start, relay, finish


def _place_own(gathered, shards):
    chip = 2 * lax.axis_index("x") + lax.axis_index("y")
    return [lax.dynamic_update_slice(g, s[None], (chip, 0, 0)) for g, s in zip(gathered, shards)]


def _gather_weights(shards, small):
    n = len(shards)

    def body(*refs):
        ins, small_in = refs[:n], refs[n]
        outs, small_out = refs[n + 1:2 * n + 1], refs[2 * n + 1]
        ssend, srecv = refs[2 * n + 6:]
        start, relay, finish = _gather_steps(ins, outs, refs[2 * n + 2:2 * n + 6])
        x, y, c = _mesh_pos()
        chips = [(1 - x, y), (x, 1 - y), (1 - x, 1 - y)]

        def small_copy(k, slot):
            return pltpu.make_async_remote_copy(
                src_ref=small_in, dst_ref=small_out.at[slot], send_sem=ssend.at[k], recv_sem=srecv.at[k],
                device_id=(*chips[k], c), device_id_type=MESH)

        start()
        for k in range(3):
            small_copy(k, 2 * x + y).start()
        relay()
        finish()
        for k, (px, py) in enumerate(chips):
            small_copy(k, 2 * px + py).wait_recv()
        for k in range(3):
            small_copy(k, 2 * x + y).wait_send()

    out_shape = [jax.ShapeDtypeStruct((4,) + s.shape, s.dtype) for s in shards]
    out_shape.append(jax.ShapeDtypeStruct((4,) + small.shape, small.dtype))
    res = pl.pallas_call(
        body, name="gather_weights", out_shape=out_shape,
        in_specs=[ANY] * (n + 1), out_specs=[ANY] * (n + 1),
        scratch_shapes=_gather_sems(n) + [pltpu.SemaphoreType.DMA((3,)), pltpu.SemaphoreType.DMA((3,))],
    )(*shards, small)
    res = _place_own(res, list(shards) + [small])
    return res[:n], res[n]


def _pair_exchange_steps(ins, outs, sems):
    send, recv = sems
    x, y, c = _mesh_pos()
    other = pl.ds(pl.multiple_of((1 - c) * HALF, 128), HALF)
    cps = [pltpu.make_async_remote_copy(
        src_ref=ins[a].at[:, :, other], dst_ref=outs[a], send_sem=send.at[a], recv_sem=recv.at[a],
        device_id=(x, y, 1 - c), device_id_type=MESH) for a in range(len(ins))]

    def start():
        for cp in cps:
            cp.start()

    def finish():
        for cp in cps:
            cp.wait()

    return start, finish


def _pair_exchange_shapes(grads):
    return [jax.ShapeDtypeStruct(g.shape[:2] + (HALF,), g.dtype) for g in grads]


def _pair_exchange(grads):
    n = len(grads)

    def body(*refs):
        start, finish = _pair_exchange_steps(refs[:n], refs[n:2 * n], refs[2 * n:])
        start()
        finish()

    return pl.pallas_call(
        body, name="grad_pair_exchange", out_shape=_pair_exchange_shapes(grads),
        in_specs=[ANY] * n, out_specs=[ANY] * n,
        scratch_shapes=[pltpu.SemaphoreType.DMA((n,)), pltpu.SemaphoreType.DMA((n,))],
    )(*grads)


NO_FUSE = ()


def _fuse_plan(kind, items):
    n = len(items)
    if kind == "gather":
        shards = [it[0] for it in items]
        bufs = [it[3] for it in items if it[3] is not None]
        alias, b = {}, 0
        for a, it in enumerate(items):
            if it[3] is not None:
                alias[n + b] = a
                b += 1
        ranges = [(it[1], it[2]) for it in items]
        return (shards + bufs, [jax.ShapeDtypeStruct((4,) + s.shape, s.dtype) for s in shards], alias, _gather_sems(n),
                lambda i, o, s: _gather_steps(i[:n], o, s, ranges))
    if kind == "exchange":
        return list(items), _chip_exchange_shapes(items), {}, _chip_exchange_sems(n), _chip_exchange_steps
    return (list(items), _pair_exchange_shapes(items), {},
            [pltpu.SemaphoreType.DMA((n,)), pltpu.SemaphoreType.DMA((n,))], _pair_exchange_steps)


def _fused_call(body, fuse, *, name, grid, in_specs, out_specs, out_shape, scratch_shapes, operands):
    plans = [_fuse_plan(kind, list(items)) for kind, items in fuse if len(items)]
    n_in, n_out, n_s = len(in_specs), len(out_shape), len(scratch_shapes)
    comm = [a for p in plans for a in p[0]]
    shapes = [s for p in plans for s in p[1]]
    nc, no = len(comm), len(shapes)
    aliases, i_at, o_at = {}, n_in, n_out
    for p in plans:
        for i, o in p[2].items():
            aliases[i_at + i] = o_at + o
        i_at, o_at = i_at + len(p[0]), o_at + len(p[1])

    def wrapped(*refs):
        o0 = n_in + nc
        s0 = o0 + n_out + no
        i_at, o_at, sem_at, steps = n_in, o0 + n_out, s0 + n_s, []
        for p in plans:
            steps.append(p[4](refs[i_at:i_at + len(p[0])], refs[o_at:o_at + len(p[1])], refs[sem_at:sem_at + len(p[3])]))
            i_at, o_at, sem_at = i_at + len(p[0]), o_at + len(p[1]), sem_at + len(p[3])
        first, last = _grid_ends(grid)
        for st in steps:
            pl.when(first)(st[0])
        for st in steps:
            for mid in st[1:-1]:
                pl.when(last)(mid)
        body(*refs[:n_in], *refs[o0:o0 + n_out], *refs[s0:s0 + n_s])
        for st in steps:
            pl.when(last)(st[-1])

    res = pl.pallas_call(
        wrapped if plans else body, name=name, grid=grid, in_specs=list(in_specs) + [ANY] * nc,
        out_specs=list(out_specs) + [ANY] * no, out_shape=list(out_shape) + shapes,
        scratch_shapes=list(scratch_shapes) + [s for p in plans for s in p[3]], input_output_aliases=aliases,
        compiler_params=_params(("arbitrary",) * len(grid)))(*operands, *comm)
    outs, got, results = list(res[:n_out]), list(res[n_out:]), []
    for p in plans:
        results.append(got[:len(p[1])])
        got = got[len(p[1]):]
    return outs, results


def _chip_exchange_sems(n):
    return [pltpu.SemaphoreType.DMA((n, 3))] * 2


def _chip_exchange_shapes(parts):
    return [jax.ShapeDtypeStruct((3,) + p.shape[1:], p.dtype) for p in parts]


def _chip_exchange_steps(ins, outs, sems):
    send, recv = sems
    x, y, c = _mesh_pos()
    chips = [(1 - x, y), (x, 1 - y), (1 - x, 1 - y)]
    cps = [pltpu.make_async_remote_copy(
        src_ref=ins[a].at[2 * px + py], dst_ref=outs[a].at[k], send_sem=send.at[a, k], recv_sem=recv.at[a, k],
        device_id=(px, py, c), device_id_type=MESH) for a in range(len(ins)) for k, (px, py) in enumerate(chips)]

    def start():
        for cp in cps:
            cp.start()

    def finish():
        for cp in cps:
            cp.wait()

    return start, finish


def _chip_exchange(parts):
    n = len(parts)

    def body(*refs):
        start, finish = _chip_exchange_steps(refs[:n], refs[n:2 * n], refs[2 * n:])
        start()
        finish()

    return pl.pallas_call(
        body, name="grad_chip_exchange", out_shape=_chip_exchange_shapes(parts),
        in_specs=[ANY] * n, out_specs=[ANY] * n, scratch_shapes=_chip_exchange_sems(n),
    )(*parts)


def _pair_swap(halves):
    n = len(halves)

    def body(*refs):
        ins, outs = refs[:n], refs[n:2 * n]
        send, recv = refs[2 * n:]
        x, y, c = _mesh_pos()
        cps = [pltpu.make_async_remote_copy(
            src_ref=ins[a], dst_ref=outs[a], send_sem=send.at[a], recv_sem=recv.at[a],
            device_id=(x, y, 1 - c), device_id_type=MESH) for a in range(n)]
        for cp in cps:
            cp.start()
        for cp in cps:
            cp.wait()

    return pl.pallas_call(
        body, name="grad_pair_swap",
        out_shape=[jax.ShapeDtypeStruct(h.shape, h.dtype) for h in halves],
        in_specs=[ANY] * n, out_specs=[ANY] * n,
        scratch_shapes=[pltpu.SemaphoreType.DMA((n,)), pltpu.SemaphoreType.DMA((n,))],
    )(*halves)


def _allgather_small(pack):
    m_per, ncol = pack.shape

    def body(x_ref, out_ref, send_sems, recv_sems, local_sem):
        x, y, c = _mesh_pos()
        me, sibling = (x, y, c), (x, y, 1 - c)
        chips = [(1 - x, y), (x, 1 - y), (1 - x, 1 - y)]

        def rows(px, py, pc):
            return out_ref.at[pl.ds(pl.multiple_of((4 * px + 2 * py + pc) * m_per, 8), m_per), :]

        def copy(k, block, to, src=None):
            return pltpu.make_async_remote_copy(
                src_ref=rows(*block) if src is None else src, dst_ref=rows(*block),
                send_sem=send_sems.at[k], recv_sem=recv_sems.at[k], device_id=to, device_id_type=MESH)

        mine = pltpu.make_async_copy(x_ref, rows(*me), local_sem)
        mine.start()
        first = [copy(0, me, sibling, src=x_ref)]
        first += [copy(1 + j, me, (*chip, c), src=x_ref) for j, chip in enumerate(chips)]
        for cp in first:
            cp.start()
        passed = [copy(4 + j, (*chip, c), sibling) for j, chip in enumerate(chips)]
        for j, chip in enumerate(chips):
            copy(1 + j, (*chip, c), me).wait_recv()
            passed[j].start()
        copy(0, sibling, me).wait_recv()
        for j, chip in enumerate(chips):
            copy(4 + j, (*chip, 1 - c), me).wait_recv()
        for cp in first + passed:
            cp.wait_send()
        mine.wait()

    return pl.pallas_call(
        body, name="allgather_small",
        out_shape=jax.ShapeDtypeStruct((8 * m_per, ncol), pack.dtype),
        in_specs=[pl.BlockSpec(memory_space=pltpu.VMEM)],
        out_specs=pl.BlockSpec(memory_space=pltpu.VMEM),
        scratch_shapes=[pltpu.SemaphoreType.DMA((7,)), pltpu.SemaphoreType.DMA((7,)), pltpu.SemaphoreType.DMA],
    )(pack)


def _add_pair(g, got, c_arr, splits=None):
    _, rows, _ = g.shape
    splits = splits or [rows]

    def body(c_ref, g_ref, r_ref, *o_refs):
        at = 0
        for o_ref, n in zip(o_refs, splits):
            o_ref[...] = (g_ref[:, at:at + n, :] + r_ref[:, at:at + n, :]).astype(BF16)
            at += n

    return pl.pallas_call(
        body, name="grad_add_pair", out_shape=[jax.ShapeDtypeStruct((4, n, HALF), BF16) for n in splits],
        grid_spec=pltpu.PrefetchScalarGridSpec(
            num_scalar_prefetch=1, grid=(4,),
            in_specs=[pl.BlockSpec((1, rows, HALF), lambda j, c: (j, 0, c[0])),
                      pl.BlockSpec((1, rows, HALF), lambda j, c: (j, 0, 0))],
            out_specs=[pl.BlockSpec((1, n, HALF), lambda j, c: (j, 0, 0)) for n in splits]),
        compiler_params=_params(("arbitrary",)),
    )(c_arr, g, got)


def _add_chips(g, pair_got, chip_gots, pos_arr):
    _, rows, _ = g.shape

    def body(pos_ref, g_ref, p_ref, *refs):
        o_ref, at = refs[-1], 0
        for r_ref in refs[:-1]:
            n = r_ref.shape[1]
            own = g_ref[0, at:at + n, :] + p_ref[0, at:at + n, :]
            o_ref[at:at + n, :] = ((own + r_ref[0].astype(F32)) + r_ref[1].astype(F32)) + r_ref[2].astype(F32)
            at += n

    return pl.pallas_call(
        body, name="grad_add_chips", out_shape=jax.ShapeDtypeStruct((rows, HALF), F32),
        grid_spec=pltpu.PrefetchScalarGridSpec(
            num_scalar_prefetch=1, grid=(1,),
            in_specs=[pl.BlockSpec((1, rows, HALF), lambda i, p: (p[0], 0, p[1])),
                      pl.BlockSpec((1, rows, HALF), lambda i, p: (p[0], 0, 0))]
            + [pl.BlockSpec(t.shape, lambda i, p: (0, 0, 0)) for t in chip_gots],
            out_specs=pl.BlockSpec((rows, HALF), lambda i, p: (0, 0))),
        compiler_params=_params(("arbitrary",)),
    )(pos_arr, g, pair_got, *chip_gots)


def _sum_small(gathered):
    def body(g_ref, o_ref):
        acc = g_ref[0:SMALL_ROWS, :]
        for d in range(1, 8):
            acc = acc + g_ref[d * SMALL_ROWS:(d + 1) * SMALL_ROWS, :]
        o_ref[...] = acc

    return pl.pallas_call(
        body, name="sum_small", out_shape=jax.ShapeDtypeStruct((SMALL_ROWS, 128), F32),
    )(gathered)


def _adamw(w, g, m, v):
    rows, cols = w.shape
    tr = 256 if rows % 256 == 0 else rows
    c1 = 1.0 - ADAM_B1 ** ADAM_STEP
    c2 = 1.0 - ADAM_B2 ** ADAM_STEP

    def body(w_ref, g_ref, m_ref, v_ref, d_ref, mo_ref, vo_ref):
        gg = g_ref[...]
        mn = ADAM_B1 * m_ref[...] + (1.0 - ADAM_B1) * gg
        vn = ADAM_B2 * v_ref[...] + (1.0 - ADAM_B2) * (gg * gg)
        d_ref[...] = -ADAM_LR * ((mn / c1) / (jnp.sqrt(vn / c2) + ADAM_EPS) + ADAM_WD * w_ref[...])
        mo_ref[...] = mn
        vo_ref[...] = vn

    spec = pl.BlockSpec((tr, cols), lambda i: (i, 0))
    return pl.pallas_call(
        body, name="adamw", out_shape=[jax.ShapeDtypeStruct(w.shape, F32)] * 3,
        grid=(rows // tr,), in_specs=[spec] * 4, out_specs=[spec] * 3,
        compiler_params=_params(("parallel",)),
    )(w, g, m, v)


def _ln_in_fwd(x, meta, g, b):
    bsz, s, d = x.shape
    tp = s + HEAD
    nh = 2
    sh = s // nh
    rc = min(256, sh)

    def body(x_ref, meta_ref, g_ref, b_ref, s0_ref, s0b_ref):
        h = pl.program_id(1)
        gg, bb = g_ref[...], b_ref[...]

        @pl.when(h == 0)
        def _():
            s0_ref[0, 0:PAD, :] = jnp.zeros((PAD, d), F32)
            s0b_ref[0, 0:PAD, :] = jnp.zeros((PAD, d), BF16)
            mh, _ = _ln_stats(meta_ref[...])
            mv = mh * gg + bb
            s0_ref[0, PAD:HEAD, :] = mv
            s0b_ref[0, PAD:HEAD, :] = mv.astype(BF16)

        def step(i, carry):
            src = pl.ds(pl.multiple_of(i * rc, rc), rc)
            dst = pl.ds(pl.multiple_of(HEAD + h * sh + i * rc, 64), rc)
            xh, _ = _ln_stats(x_ref[0, src, :])
            val = xh * gg + bb
            s0_ref[0, dst, :] = val
            s0b_ref[0, dst, :] = val.astype(BF16)
            return carry

        lax.fori_loop(0, sh // rc, step, 0)

    full = lambda bi, hi: (bi, 0, 0)
    return pl.pallas_call(
        body, name="ln_in_fwd",
        out_shape=[jax.ShapeDtypeStruct((bsz, tp, d), F32), jax.ShapeDtypeStruct((bsz, tp, d), BF16)],
        grid=(bsz, nh),
        in_specs=[pl.BlockSpec((1, sh, d), lambda bi, hi: (bi, hi, 0)),
                  pl.BlockSpec((N_META, d), lambda bi, hi: (0, 0)),
                  pl.BlockSpec((1, d), lambda bi, hi: (0, 0)),
                  pl.BlockSpec((1, d), lambda bi, hi: (0, 0))],
        out_specs=[pl.BlockSpec((1, tp, d), full)] * 2,
        compiler_params=_params(("parallel", "arbitrary")),
    )(x, meta, g, b)


def _in_proj(s0b, w_int, fuse=NO_FUSE):
    r, d = s0b.shape
    tm = _row_tile(r)

    def body(a_ref, w_ref, o_ref):
        o_ref[...] = _nt(a_ref[...], w_ref[...])

    (u,), got = _fused_call(
        body, fuse, name="in_proj", out_shape=[jax.ShapeDtypeStruct((r, D_IN_PAD), F32)],
        grid=(r // tm,),
        in_specs=[pl.BlockSpec((tm, d), lambda i: (i, 0)), pl.BlockSpec((D_IN_PAD, d), lambda i: (0, 0))],
        out_specs=[pl.BlockSpec((tm, D_IN_PAD), lambda i: (i, 0))], scratch_shapes=[], operands=(s0b, w_int))
    return u, got


def _conv_fwd(u, conv_w, conv_b, fuse=NO_FUSE):
    bsz, tp, _ = u.shape
    nchunk = tp // CHUNK
    win = CHUNK + 32
    nct = D_CONV // 128

    def body(cv_ref, cg_ref, w_ref, cb_ref, hc_ref, h_scr, win_scr):
        h_scr[0:32, :] = jnp.zeros((32, 128), F32)
        h_scr[32:32 + tp, :] = cv_ref[0] * _sigmoid(cg_ref[0])
        cb = cb_ref[...]

        def step(n, carry):
            r0 = pl.multiple_of(n * CHUNK, CHUNK)
            win_scr[...] = h_scr[pl.ds(r0, win), :]
            acc = jnp.zeros((CHUNK, 128), F32)
            for j in range(CONV_WIDTH):
                acc = acc + w_ref[j:j + 1, :] * win_scr[2 + j:2 + j + CHUNK, :]
            hc_ref[0, pl.ds(r0, CHUNK), :] = acc + cb
            return carry

        lax.fori_loop(0, nchunk, step, 0)

    (hc,), got = _fused_call(
        body, fuse, name="conv_fwd", out_shape=[jax.ShapeDtypeStruct((bsz, tp, D_CONV), F32)],
        grid=(bsz, nct),
        in_specs=[pl.BlockSpec((1, tp, 128), lambda bi, t: (bi, 0, C_VAL // 128 + t)),
                  pl.BlockSpec((1, tp, 128), lambda bi, t: (bi, 0, C_GATE // 128 + t)),
                  pl.BlockSpec((32, 128), lambda bi, t: (0, t)),
                  pl.BlockSpec((1, 128), lambda bi, t: (0, t))],
        out_specs=[pl.BlockSpec((1, tp, 128), lambda bi, t: (bi, 0, t))],
        scratch_shapes=[pltpu.VMEM((tp + 32, 128), F32), pltpu.VMEM((win, 128), F32)],
        operands=(u, u, conv_w, conv_b))
    return hc, got


def _gla_group(nchunk):
    return 11 if nchunk % 11 == 0 else nchunk


def _bdot(a, b, ca, cb, precision=None):
    return lax.dot_general(a, b, (((ca,), (cb,)), ((0,), (0,))), preferred_element_type=F32, precision=precision)


def _bnn(a, b, **kw):
    return _bdot(a, b, 2, 1, **kw)


def _bnt(a, b, **kw):
    return _bdot(a, b, 2, 2, **kw)


def _gla_consts(nb):
    row = lax.broadcasted_iota(jnp.int32, (nb, CHUNK, CHUNK), 1)
    col = lax.broadcasted_iota(jnp.int32, (nb, CHUNK, CHUNK), 2)
    lane = lax.broadcasted_iota(jnp.int32, (1, 1, 128), 2)
    return row >= col, row <= col, [lane < GLA_DK, lane >= GLA_DK]


def _gla_group_terms(g, nb, q_ref, k_ref, gd_ref, gup_ref, gb_ref, tril):
    m = nb * CHUNK
    rows = pl.ds(pl.multiple_of(g * m, CHUNK), m)
    z = _nn(gd_ref[0, rows, :].astype(BF16), gup_ref[...]) + gb_ref[...]
    valid = g * m + lax.broadcasted_iota(jnp.int32, (m, 1), 0) >= PAD
    lg = jnp.where(valid, _log_sigmoid(z) * (1.0 / GLA_TAU), 0.0)
    bcum = _bnn(tril.astype(F32), lg.reshape(nb, CHUNK, 128), precision=lax.Precision.HIGHEST)
    blast = bcum[:, CHUNK - 1:CHUNK, :]
    eb = jnp.exp(bcum)
    enb = jnp.exp(-bcum)
    erest = jnp.exp(blast - bcum)
    q = (q_ref[0, rows, :] * Q_SCALE).reshape(nb, CHUNK, 128)
    k = k_ref[0, rows, :].reshape(nb, CHUNK, 128)
    return rows, valid, z, eb, enb, erest, jnp.exp(blast), q * eb, k * enb, k * erest


def _grid_ends(grid):
    ids = [pl.program_id(i) for i in range(len(grid))]
    first = functools.reduce(jnp.logical_and, [i == 0 for i in ids])
    last = functools.reduce(jnp.logical_and, [i == g - 1 for i, g in zip(ids, grid)])
    return first, last


def _gla_fwd(u, gup, gbias, gnorm, fuse=NO_FUSE):
    bsz, tp, _ = u.shape
    nchunk = tp // CHUNK
    nb = _gla_group(nchunk)

    def body(q_ref, k_ref, v_ref, r_ref, gd_ref, gup_ref, gb_ref, gn_ref, out_ref, o_ref, st_ref, s_scr):
        tril, _, hmask = _gla_consts(nb)
        s_scr[...] = jnp.zeros_like(s_scr)
        gn = gn_ref[...]

        def group(g, carry):
            rows, _, _, _, _, _, dec, qe, ke, kd = _gla_group_terms(g, nb, q_ref, k_ref, gd_ref, gup_ref, gb_ref, tril)
            keb, kdb = ke.astype(BF16), kd.astype(BF16)
            for h in range(2):
                cols = slice(h * GLA_DV, (h + 1) * GLA_DV)
                qh = jnp.where(hmask[h], qe, 0.0).astype(BF16)
                vh = v_ref[0, rows, cols].astype(BF16).reshape(nb, CHUNK, GLA_DV)
                a = jnp.where(tril, _bnt(qh, keb), 0.0).astype(BF16)
                st = s_scr[h]
                sts = []
                for n in range(nb):
                    st_ref[0, h, g * nb + n] = st
                    sts.append(st.astype(BF16))
                    st = dec[n] * st + _tn(vh[n], kdb[n])
                s_scr[h] = st
                o = (_bnn(a, vh) + _bnt(qh, jnp.stack(sts))).reshape(nb * CHUNK, GLA_DV)
                o_ref[0, rows, cols] = o
                rms = lax.rsqrt(jnp.mean(o * o, axis=-1, keepdims=True) + LN_EPS)
                rh = r_ref[0, rows, cols]
                out_ref[0, rows, cols] = (o * rms * gn * (rh * _sigmoid(rh))).astype(BF16)
            return carry

        lax.fori_loop(0, nchunk // nb, group, 0)

    res, got = _fused_call(
        body, fuse, name="gla_fwd",
        out_shape=[jax.ShapeDtypeStruct((bsz, tp, 512), BF16), jax.ShapeDtypeStruct((bsz, tp, 512), F32),
                   jax.ShapeDtypeStruct((bsz, GLA_HEADS, nchunk, GLA_DV, 128), F32)],
        grid=(bsz, 2),
        in_specs=[pl.BlockSpec((1, tp, 128), lambda bi, p: (bi, 0, C_Q // 128 + p)),
                  pl.BlockSpec((1, tp, 128), lambda bi, p: (bi, 0, C_K // 128 + p)),
                  pl.BlockSpec((1, tp, 256), lambda bi, p: (bi, 0, C_V // 256 + p)),
                  pl.BlockSpec((1, tp, 256), lambda bi, p: (bi, 0, C_R // 256 + p)),
                  pl.BlockSpec((1, tp, 128), lambda bi, p: (bi, 0, C_GD // 128)),
                  pl.BlockSpec((128, 128), lambda bi, p: (0, p)),
                  pl.BlockSpec((1, 128), lambda bi, p: (0, p)),
                  pl.BlockSpec((1, 128), lambda bi, p: (0, 0))],
        out_specs=[pl.BlockSpec((1, tp, 256), lambda bi, p: (bi, 0, p)),
                   pl.BlockSpec((1, tp, 256), lambda bi, p: (bi, 0, p)),
                   pl.BlockSpec((1, 2, nchunk, GLA_DV, 128), lambda bi, p: (bi, p, 0, 0, 0))],
        scratch_shapes=[pltpu.VMEM((2, GLA_DV, 128), F32)],
        operands=(u, u, u, u, u, gup, gbias, gnorm))
    return res[0], res[1], res[2], got


def _out_proj_ln1(hc, gla_out, w_out, s0, cg, cb, g1, b1, fuse=NO_FUSE):
    r, d = s0.shape
    tm = _row_tile(r)

    def body(hc_ref, a_ref, w_ref, s0_ref, cg_ref, cb_ref, g_ref, b_ref, co_ref, xh_ref, rstd_ref, s1b_ref):
        for rs in _sub_rows(tm):
            xc, _ = _ln_stats(hc_ref[rs, :])
            nv = xc * cg_ref[...] + cb_ref[...]
            co = (nv * _sigmoid(nv)).astype(BF16)
            co_ref[rs, :] = co
            mix = _nn(co, w_ref[0:D_CONV, :]) + _nn(a_ref[rs, :], w_ref[D_CONV:, :])
            xh, rstd = _ln_stats(ALPHA * s0_ref[rs, :] + mix)
            xh_ref[rs, :] = xh
            rstd_ref[rs, :] = rstd
            s1b_ref[rs, :] = (xh * g_ref[...] + b_ref[...]).astype(BF16)

    row = lambda n: pl.BlockSpec((tm, n), lambda i: (i, 0))
    vec = lambda n: pl.BlockSpec((1, n), lambda i: (0, 0))
    res, got = _fused_call(
        body, fuse, name="out_proj_ln1",
        out_shape=[jax.ShapeDtypeStruct((r, D_CONV), BF16), jax.ShapeDtypeStruct((r, d), F32),
                   jax.ShapeDtypeStruct((r, 1), F32), jax.ShapeDtypeStruct((r, d), BF16)],
        grid=(r // tm,),
        in_specs=[row(D_CONV), row(512), pl.BlockSpec((d, d), lambda i: (0, 0)), row(d),
                  vec(D_CONV), vec(D_CONV), vec(d), vec(d)],
        out_specs=[row(D_CONV), row(d), row(1), row(d)], scratch_shapes=[],
        operands=(hc, gla_out, w_out, s0, cg, cb, g1, b1))
    return res[0], res[1], res[2], res[3], got


def _ffn1(s1b, w1t, fuse=NO_FUSE):
    r, d = s1b.shape
    tm = _row_tile(r)

    def body(a_ref, w_ref, o_ref):
        o_ref[...] = jnp.maximum(_nt(a_ref[...], w_ref[...]), 0.0).astype(BF16)

    (ra,), got = _fused_call(
        body, fuse, name="ffn1", out_shape=[jax.ShapeDtypeStruct((r, D_FF), BF16)], grid=(r // tm,),
        in_specs=[pl.BlockSpec((tm, d), lambda i: (i, 0)), pl.BlockSpec((D_FF, d), lambda i: (0, 0))],
        out_specs=[pl.BlockSpec((tm, D_FF), lambda i: (i, 0))], scratch_shapes=[], operands=(s1b, w1t))
    return ra, got


def _ffn2_ln2_loss(ra, w2, xhat1, g1, b1, g2, b2, tgt, tp):
    r, d = xhat1.shape
    tm = _row_tile(tp)
    per = tp // tm

    def body(ra_ref, w_ref, xh1_ref, g1_ref, b1_ref, g2_ref, b2_ref, tgt_ref, dt_ref, dtb_ref, acc_ref, t_ref, sem):
        i = pl.program_id(0)
        b, j = i // per, i % per

        @pl.when(i == 0)
        def _():
            acc_ref[...] = jnp.zeros_like(acc_ref)

        head_copy = pltpu.make_async_copy(tgt_ref.at[b, pl.ds(0, tm - HEAD), :], t_ref.at[pl.ds(HEAD, tm - HEAD), :], sem)
        body_copy = pltpu.make_async_copy(
            tgt_ref.at[b, pl.ds(pl.multiple_of(jnp.maximum(j * tm - HEAD, 0), 64), tm), :], t_ref, sem)

        @pl.when(j == 0)
        def _():
            t_ref[0:HEAD, :] = jnp.zeros((HEAD, d), F32)
            head_copy.start()

        pl.when(j > 0)(body_copy.start)

        sums = [jnp.zeros((1, d), F32)] * 3
        for rs in _sub_rows(tm):
            rb = ra_ref[rs, :]
            f = _nn(rb * rb, w_ref[...])
            if rs.start == 0:
                pl.when(j == 0)(head_copy.wait)
                pl.when(j > 0)(body_copy.wait)
            s1 = xh1_ref[rs, :] * g1_ref[...] + b1_ref[...]
            xh2, rstd2 = _ln_stats(ALPHA * s1 + f)
            y = xh2 * g2_ref[...] + b2_ref[...]
            rowid = (i % per) * tm + rs.start + lax.broadcasted_iota(jnp.int32, (rs.stop - rs.start, 1), 0)
            e = jnp.where(rowid >= HEAD, y - t_ref[rs, :], 0.0)
            dy = e * (1.0 / d)
            dt2 = _ln_bwd(dy * g2_ref[...], xh2, rstd2)
            dt_ref[rs, :] = dt2
            dtb_ref[rs, :] = dt2.astype(BF16)
            sums = [sums[0] + (0.5 / d) * jnp.sum(e * e, axis=0, keepdims=True),
                    sums[1] + jnp.sum(dy * xh2, axis=0, keepdims=True), sums[2] + jnp.sum(dy, axis=0, keepdims=True)]
        for k in range(3):
            acc_ref[k:k + 1, :] += sums[k]

    row = lambda n: pl.BlockSpec((tm, n), lambda i: (i, 0))
    vec = pl.BlockSpec((1, d), lambda i: (0, 0))
    return pl.pallas_call(
        body, name="ffn2_ln2_loss",
        out_shape=[jax.ShapeDtypeStruct((r, d), F32), jax.ShapeDtypeStruct((r, d), BF16),
                   jax.ShapeDtypeStruct((8, d), F32)],
        grid=(r // tm,),
        in_specs=[row(D_FF), pl.BlockSpec((D_FF, d), lambda i: (0, 0)), row(d), vec, vec, vec, vec, ANY],
        out_specs=[row(d), row(d), pl.BlockSpec((8, d), lambda i: (0, 0))],
        scratch_shapes=[pltpu.VMEM((tm, d), F32), pltpu.SemaphoreType.DMA],
        compiler_params=_params(("arbitrary",)),
    )(ra, w2, xhat1, g1, b1, g2, b2, tgt)


def _ffn_bwd_da(dt2b, w2, ra):
    r, d = dt2b.shape
    tm = _row_tile(r)

    def body(g_ref, w_ref, ra_ref, o_ref):
        o_ref[...] = (_nt(g_ref[...], w_ref[...]) * (2.0 * ra_ref[...].astype(F32))).astype(BF16)

    return pl.pallas_call(
        body, name="ffn_bwd_da", out_shape=jax.ShapeDtypeStruct((r, D_FF), BF16),
        grid=(r // tm,),
        in_specs=[pl.BlockSpec((tm, d), lambda i: (i, 0)), pl.BlockSpec((D_FF, d), lambda i: (0, 0)),
                  pl.BlockSpec((tm, D_FF), lambda i: (i, 0))],
        out_specs=pl.BlockSpec((tm, D_FF), lambda i: (i, 0)),
        compiler_params=_params(("parallel",)),
    )(dt2b, w2, ra)


def _ffn_bwd_ln1(da, w1t, dt2, xhat1, rstd1, g1):
    r, d = dt2.shape
    tm = _row_tile(r)

    def body(da_ref, w_ref, dt2_ref, xh_ref, rstd_ref, g_ref, dt_ref, dtb_ref, acc_ref):
        @pl.when(pl.program_id(0) == 0)
        def _():
            acc_ref[...] = jnp.zeros_like(acc_ref)

        sums = [jnp.zeros((1, d), F32)] * 2
        for rs in _sub_rows(tm):
            ds1 = ALPHA * dt2_ref[rs, :] + _nn(da_ref[rs, :], w_ref[...])
            xh = xh_ref[rs, :]
            dt1 = _ln_bwd(ds1 * g_ref[...], xh, rstd_ref[rs, :])
            dt_ref[rs, :] = dt1
            dtb_ref[rs, :] = dt1.astype(BF16)
            sums = [sums[0] + jnp.sum(ds1 * xh, axis=0, keepdims=True), sums[1] + jnp.sum(ds1, axis=0, keepdims=True)]
        for k in range(2):
            acc_ref[k:k + 1, :] += sums[k]

    row = lambda n: pl.BlockSpec((tm, n), lambda i: (i, 0))
    return pl.pallas_call(
        body, name="ffn_bwd_ln1",
        out_shape=[jax.ShapeDtypeStruct((r, d), F32), jax.ShapeDtypeStruct((r, d), BF16),
                   jax.ShapeDtypeStruct((8, d), F32)],
        grid=(r // tm,),
        in_specs=[row(D_FF), pl.BlockSpec((D_FF, d), lambda i: (0, 0)), row(d), row(d), row(1),
                  pl.BlockSpec((1, d), lambda i: (0, 0))],
        out_specs=[row(d), row(d), pl.BlockSpec((8, d), lambda i: (0, 0))],
        compiler_params=_params(("arbitrary",)),
    )(da, w1t, dt2, xhat1, rstd1, g1)


def _matmul_tn(lhs, rhs, bm, square_lhs=False, name="matmul_tn", fuse=NO_FUSE):
    r, m = lhs.shape
    n = rhs.shape[1]
    tk = _row_tile(r)

    def body(a_ref, b_ref, o_ref):
        @pl.when(pl.program_id(1) == 0)
        def _():
            o_ref[...] = jnp.zeros_like(o_ref)

        a = a_ref[...]
        if square_lhs:
            a = a * a
        o_ref[...] += _tn(a, b_ref[...])

    (out,), got = _fused_call(
        body, fuse, name=name, out_shape=[jax.ShapeDtypeStruct((m, n), F32)], grid=(m // bm, r // tk),
        in_specs=[pl.BlockSpec((tk, bm), lambda i, k: (k, i)), pl.BlockSpec((tk, n), lambda i, k: (k, 0))],
        out_specs=[pl.BlockSpec((bm, n), lambda i, k: (i, 0))], scratch_shapes=[], operands=(lhs, rhs))
    return out, got


def _out_proj_bwd(dt1b, w_out, hc, cg, cb, fuse=NO_FUSE):
    r, d = dt1b.shape
    tm = _row_tile(r)

    def body(g_ref, w_ref, hc_ref, cg_ref, cb_ref, dhc_ref, dgla_ref, acc_ref):
        @pl.when(pl.program_id(0) == 0)
        def _():
            acc_ref[...] = jnp.zeros_like(acc_ref)

        gg = cg_ref[...]
        sums = [jnp.zeros((1, D_CONV), F32)] * 3
        for rs in _sub_rows(tm):
            dmix = _nt(g_ref[rs, :], w_ref[...])
            dgla_ref[rs, :] = dmix[:, D_CONV:]
            xh, rstd = _ln_stats(hc_ref[rs, :])
            nv = xh * gg + cb_ref[...]
            sig = _sigmoid(nv)
            dn = dmix[:, :D_CONV] * (sig * (1.0 + nv * (1.0 - sig)))
            dhc = _ln_bwd(dn * gg, xh, rstd)
            dhc_ref[rs, :] = dhc
            sums = [sums[0] + jnp.sum(dhc, axis=0, keepdims=True), sums[1] + jnp.sum(dn * xh, axis=0, keepdims=True),
                    sums[2] + jnp.sum(dn, axis=0, keepdims=True)]
        for k in range(3):
            acc_ref[k:k + 1, :] += sums[k]

    row = lambda n: pl.BlockSpec((tm, n), lambda i: (i, 0))
    vec = pl.BlockSpec((1, D_CONV), lambda i: (0, 0))
    res, got = _fused_call(
        body, fuse, name="out_proj_bwd",
        out_shape=[jax.ShapeDtypeStruct((r, D_CONV), F32), jax.ShapeDtypeStruct((r, 512), F32),
                   jax.ShapeDtypeStruct((8, D_CONV), F32)],
        grid=(r // tm,),
        in_specs=[row(d), pl.BlockSpec((d, d), lambda i: (0, 0)), row(D_CONV), vec, vec],
        out_specs=[row(D_CONV), row(512), pl.BlockSpec((8, D_CONV), lambda i: (0, 0))], scratch_shapes=[],
        operands=(dt1b, w_out, hc, cg, cb))
    return res[0], res[1], res[2], got


def _conv_bwd(dhc, u, conv_w, fuse=NO_FUSE):
    bsz, tp, _ = u.shape
    nchunk = tp // CHUNK
    win = CHUNK + 32
    nct = D_CONV // 128

    def body(dhc_ref, cv_ref, cg_ref, w_ref, dv_ref, dg_ref, dw_ref, h_scr, dhc_scr, hwin, dwin, dw_scr):
        h_scr[0:32, :] = jnp.zeros((32, 128), F32)
        h_scr[32:32 + tp, :] = cv_ref[0] * _sigmoid(cg_ref[0])
        dhc_scr[0:tp, :] = dhc_ref[0]
        dhc_scr[tp:tp + 32, :] = jnp.zeros((32, 128), F32)
        dw_scr[...] = jnp.zeros_like(dw_scr)

        def step(n, carry):
            r0 = pl.multiple_of(n * CHUNK, CHUNK)
            rows = pl.ds(r0, CHUNK)
            hwin[...] = h_scr[pl.ds(r0, win), :]
            dwin[...] = dhc_scr[pl.ds(r0, win), :]
            dcur = dwin[0:CHUNK, :]
            acc = jnp.zeros((CHUNK, 128), F32)
            for j in range(CONV_WIDTH):
                acc = acc + w_ref[j:j + 1, :] * dwin[30 - j:30 - j + CHUNK, :]
                prod = dcur * hwin[2 + j:2 + j + CHUNK, :]
                dw_scr[j * 8:(j + 1) * 8, :] += jnp.sum(prod.reshape(CHUNK // 8, 8, 128), axis=0)
            cg = cg_ref[0, rows, :]
            sig = _sigmoid(cg)
            rowid = n * CHUNK + lax.broadcasted_iota(jnp.int32, (CHUNK, 1), 0)
            dh = jnp.where(rowid >= PAD, acc, 0.0)
            dv_ref[0, rows, :] = (dh * sig).astype(BF16)
            dg_ref[0, rows, :] = (dh * cv_ref[0, rows, :] * sig * (1.0 - sig)).astype(BF16)
            return carry

        lax.fori_loop(0, nchunk, step, 0)
        dw_ref[0] = jnp.zeros((32, 128), F32)
        for j in range(CONV_WIDTH):
            dw_ref[0, j:j + 1, :] = jnp.sum(dw_scr[j * 8:(j + 1) * 8, :], axis=0, keepdims=True)

    blk = lambda off: pl.BlockSpec((1, tp, 128), lambda bi, t: (bi, 0, off // 128 + t))
    res, got = _fused_call(
        body, fuse, name="conv_bwd",
        out_shape=[jax.ShapeDtypeStruct((bsz, tp, D_CONV), BF16), jax.ShapeDtypeStruct((bsz, tp, D_CONV), BF16),
                   jax.ShapeDtypeStruct((bsz, 32, D_CONV), F32)],
        grid=(bsz, nct),
        in_specs=[blk(0), blk(C_VAL), blk(C_GATE), pl.BlockSpec((32, 128), lambda bi, t: (0, t))],
        out_specs=[blk(0), blk(0), pl.BlockSpec((1, 32, 128), lambda bi, t: (bi, 0, t))],
        scratch_shapes=[pltpu.VMEM((tp + 32, 128), F32), pltpu.VMEM((tp + 32, 128), F32),
                        pltpu.VMEM((win, 128), F32), pltpu.VMEM((win, 128), F32),
                        pltpu.VMEM((CONV_WIDTH * 8, 128), F32)],
        operands=(dhc, u, u, conv_w))
    return res[0], res[1], res[2], got


def _gla_bwd(dgla, u, o_pre, states, gup, gbias, gnorm, fuse=NO_FUSE):
    bsz, tp, _ = u.shape
    nchunk = tp // CHUNK
    nb = _gla_group(nchunk)

    def body(dy_ref, q_ref, k_ref, v_ref, r_ref, gd_ref, o_ref, st_ref, gup_ref, gb_ref, gn_ref,
             dq_ref, dk_ref, dv_ref, dr_ref, dgd_ref, dgup_ref, vec_ref, h_scr, gup_acc):
        tril, triu, hmask = _gla_consts(nb)
        h_scr[...] = jnp.zeros_like(h_scr)
        gup_acc[...] = jnp.zeros_like(gup_acc)
        gn = gn_ref[...]
        gupb = gup_ref[...]
        m = nb * CHUNK
        ngroup = nchunk // nb

        def group(i, carry):
            dbias, dgn = carry
            g = ngroup - 1 - i
            rows, valid, z, eb, enb, erest, dec, qe, ke, kd = _gla_group_terms(
                g, nb, q_ref, k_ref, gd_ref, gup_ref, gb_ref, tril)
            keb, kdb = ke.astype(BF16), kd.astype(BF16)
            dqe = jnp.zeros((nb, CHUNK, 128), F32)
            dke = jnp.zeros((nb, CHUNK, 128), F32)
            dkd = jnp.zeros((nb, CHUNK, 128), F32)
            ddec = jnp.zeros((nb, 1, 128), F32)
            for h in range(2):
                cols = slice(h * GLA_DV, (h + 1) * GLA_DV)
                o = o_ref[0, rows, cols]
                rh = r_ref[0, rows, cols]
                dy = dy_ref[0, rows, cols]
                rms = lax.rsqrt(jnp.mean(o * o, axis=-1, keepdims=True) + LN_EPS)
                nrm = o * rms
                sig = _sigmoid(rh)
                sw = rh * sig
                dr_ref[0, rows, cols] = (dy * nrm * gn * (sig * (1.0 + rh * (1.0 - sig)))).astype(BF16)
                dgn = dgn + jnp.sum(dy * nrm * sw, axis=0, keepdims=True)
                dn = dy * gn * sw
                do = rms * (dn - nrm * jnp.mean(dn * nrm, axis=-1, keepdims=True))
                dob = do.astype(BF16).reshape(nb, CHUNK, GLA_DV)
                qh = jnp.where(hmask[h], qe, 0.0).astype(BF16)
                vh = v_ref[0, rows, cols].astype(BF16).reshape(nb, CHUNK, GLA_DV)
                ht = h_scr[h]
                hts = [None] * nb
                for n in reversed(range(nb)):
                    hts[n] = ht
                    ht = dec[n] * ht + _tn(dob[n], qh[n])
                h_scr[h] = ht
                htf = jnp.stack(hts)
                htb = htf.astype(BF16)
                st = st_ref[0, h, pl.ds(g * nb, nb)]
                at = jnp.where(triu, _bnt(keb, qh), 0.0).astype(BF16)
                da = jnp.where(tril, _bnt(dob, vh), 0.0).astype(BF16)
                dat = jnp.where(triu, _bnt(vh, dob), 0.0).astype(BF16)
                dqe = dqe + jnp.where(hmask[h], _bnn(da, keb) + _bnn(dob, st.astype(BF16)), 0.0)
                dke = dke + _bnn(dat, qh)
                dv_ref[0, rows, cols] = (_bnn(at, dob) + _bnt(kdb, htb)).reshape(m, GLA_DV).astype(BF16)
                dkd = dkd + jnp.where(hmask[h], _bnn(vh, htb), 0.0)
                ddec = ddec + jnp.where(hmask[h], jnp.sum(htf * st, axis=1, keepdims=True), 0.0)
            dq_ref[0, rows, :] = (dqe * eb * Q_SCALE).reshape(m, 128).astype(BF16)
            dk_ref[0, rows, :] = (dke * enb + dkd * erest).reshape(m, 128).astype(BF16)
            db = dqe * qe - dke * ke - dkd * kd
            dblast = jnp.sum(dkd * kd, axis=1, keepdims=True) + ddec * dec
            lastrow = lax.broadcasted_iota(jnp.int32, (1, CHUNK, 1), 1) == CHUNK - 1
            db = db + jnp.where(lastrow, dblast, 0.0)
            dlg = _bnn(triu.astype(F32), db, precision=lax.Precision.HIGHEST).reshape(m, 128)
            dz = jnp.where(valid, dlg, 0.0) * (1.0 / GLA_TAU) * (1.0 - _sigmoid(z))
            dzb = dz.astype(BF16)
            dgd_ref[0, 0, rows, :] = _nt(dzb, gupb).astype(BF16)
            gup_acc[...] += _tn(gd_ref[0, rows, :].astype(BF16), dzb)
            return dbias + jnp.sum(dz, axis=0, keepdims=True), dgn

        zero = jnp.zeros((1, 128), F32)
        dbias, dgn = lax.fori_loop(0, ngroup, group, (zero, zero))
        dgup_ref[0] = gup_acc[...]
        vec_ref[0] = jnp.zeros((8, 128), F32)
        vec_ref[0, 0:1, :] = dbias
        vec_ref[0, 1:2, :] = dgn

    pair = lambda w, off: pl.BlockSpec((1, tp, w), lambda bi, p: (bi, 0, off // w + p))
    return _fused_call(
        body, fuse, name="gla_bwd",
        out_shape=[jax.ShapeDtypeStruct((bsz, tp, 256), BF16), jax.ShapeDtypeStruct((bsz, tp, 256), BF16),
                   jax.ShapeDtypeStruct((bsz, tp, 512), BF16), jax.ShapeDtypeStruct((bsz, tp, 512), BF16),
                   jax.ShapeDtypeStruct((bsz, 2, tp, 128), BF16), jax.ShapeDtypeStruct((bsz, 128, 256), F32),
                   jax.ShapeDtypeStruct((bsz, 8, 256), F32)],
        grid=(bsz, 2),
        in_specs=[pair(256, 0), pair(128, C_Q), pair(128, C_K), pair(256, C_V), pair(256, C_R),
                  pl.BlockSpec((1, tp, 128), lambda bi, p: (bi, 0, C_GD // 128)),
                  pair(256, 0),
                  pl.BlockSpec((1, 2, nchunk, GLA_DV, 128), lambda bi, p: (bi, p, 0, 0, 0)),
                  pl.BlockSpec((128, 128), lambda bi, p: (0, p)),
                  pl.BlockSpec((1, 128), lambda bi, p: (0, p)),
                  pl.BlockSpec((1, 128), lambda bi, p: (0, 0))],
        out_specs=[pair(128, 0), pair(128, 0), pair(256, 0), pair(256, 0),
                   pl.BlockSpec((1, 1, tp, 128), lambda bi, p: (bi, p, 0, 0)),
                   pl.BlockSpec((1, 128, 128), lambda bi, p: (bi, 0, p)),
                   pl.BlockSpec((1, 8, 128), lambda bi, p: (bi, 0, p))],
        scratch_shapes=[pltpu.VMEM((2, GLA_DV, 128), F32), pltpu.VMEM((128, 128), F32)],
        operands=(dgla, u, u, u, u, u, o_pre, states, gup, gbias, gnorm))


_DU_OFFSETS = (C_VAL, C_GATE, C_Q, C_K, C_V, C_R)
_DU_WIDTHS = (512, 512, 256, 256, 512, 512)


def _du_specs(tm, per, row_map):
    specs = [pl.BlockSpec((tm, w), row_map) for w in _DU_WIDTHS]
    for p in range(2):
        specs.append(pl.BlockSpec((1, 1, tm, 128), lambda *ix, p=p: (row_map(*ix)[0] // per, p, row_map(*ix)[0] % per, 0)))
    return specs


def _du_pieces(refs):
    out = [(off, ref[...]) for off, ref in zip(_DU_OFFSETS, refs[:6])]
    dgd = (refs[6][0, 0].astype(F32) + refs[7][0, 0].astype(F32)).astype(BF16)
    out.append((C_GD, dgd))
    return out


def _in_proj_bwd(pieces, dgd, w_int, dt1, tp, fuse=NO_FUSE):
    r, d = dt1.shape
    tm = _row_tile(tp)
    per = tp // tm

    def body(*refs):
        w_ref, dt_ref, o_ref = refs[8:]
        acc = ALPHA * dt_ref[...]
        for off, val in _du_pieces(refs[:8]):
            acc = acc + _nn(val, w_ref[off:off + val.shape[1], :])
        o_ref[...] = acc

    row = lambda i: (i, 0)
    (ds0,), got = _fused_call(
        body, fuse, name="in_proj_bwd", out_shape=[jax.ShapeDtypeStruct((r, d), F32)], grid=(r // tm,),
        in_specs=_du_specs(tm, per, row) + [pl.BlockSpec((D_IN_PAD, d), lambda i: (0, 0)), pl.BlockSpec((tm, d), row)],
        out_specs=[pl.BlockSpec((tm, d), row)], scratch_shapes=[], operands=(*pieces, dgd, dgd, w_int, dt1))
    return ds0, got


def _grad_w_in(pieces, dgd, s0b, tp, fuse=NO_FUSE):
    r, d = s0b.shape
    tk = _row_tile(tp)
    per = tp // tk

    def body(*refs):
        s_ref, o_ref = refs[8:]

        @pl.when(pl.program_id(0) == 0)
        def _():
            o_ref[...] = jnp.zeros_like(o_ref)

        s = s_ref[...]
        for off, val in _du_pieces(refs[:8]):
            o_ref[off:off + val.shape[1], :] += _tn(val, s)

    row = lambda k: (k, 0)
    (out,), got = _fused_call(
        body, fuse, name="grad_w_in", out_shape=[jax.ShapeDtypeStruct((D_IN_PAD, d), F32)], grid=(r // tk,),
        in_specs=_du_specs(tk, per, row) + [pl.BlockSpec((tk, d), row)],
        out_specs=[pl.BlockSpec((D_IN_PAD, d), lambda k: (0, 0))], scratch_shapes=[],
        operands=(*pieces, dgd, dgd, s0b))
    return out, got


def _ln_in_bwd(ds0, x, meta, g):
    bsz, s, d = x.shape
    tp = s + HEAD
    nh = 2
    sh = s // nh
    rc = min(256, sh)

    def body(ds_ref, x_ref, meta_ref, g_ref, gx_ref, dm_ref, vec_ref):
        h = pl.program_id(1)
        gg = g_ref[...]

        @pl.when(h == 0)
        def _():
            mh, mr = _ln_stats(meta_ref[...])
            dsm = ds_ref[0, PAD:HEAD, :]
            dm_ref[0] = _ln_bwd(dsm * gg, mh, mr)
            vec_ref[0] = jnp.zeros((8, d), F32)
            vec_ref[0, 0:1, :] = jnp.sum(dsm * mh, axis=0, keepdims=True)
            vec_ref[0, 1:2, :] = jnp.sum(dsm, axis=0, keepdims=True)

        def step(i, carry):
            sg, sb = carry
            dst = pl.ds(pl.multiple_of(i * rc, rc), rc)
            src = pl.ds(pl.multiple_of(HEAD + h * sh + i * rc, 64), rc)
            xh, rstd = _ln_stats(x_ref[0, dst, :])
            dsv = ds_ref[0, src, :]
            gx_ref[0, dst, :] = _ln_bwd(dsv * gg, xh, rstd)
            return sg + jnp.sum(dsv * xh, axis=0, keepdims=True), sb + jnp.sum(dsv, axis=0, keepdims=True)

        zero = jnp.zeros((1, d), F32)
        sg, sb = lax.fori_loop(0, sh // rc, step, (zero, zero))
        vec_ref[0, 0:1, :] += sg
        vec_ref[0, 1:2, :] += sb

    return pl.pallas_call(
        body, name="ln_in_bwd",
        out_shape=[jax.ShapeDtypeStruct((bsz, s, d), F32), jax.ShapeDtypeStruct((bsz, N_META, d), F32),
                   jax.ShapeDtypeStruct((bsz, 8, d), F32)],
        grid=(bsz, nh),
        in_specs=[pl.BlockSpec((1, tp, d), lambda bi, hi: (bi, 0, 0)),
                  pl.BlockSpec((1, sh, d), lambda bi, hi: (bi, hi, 0)),
                  pl.BlockSpec((N_META, d), lambda bi, hi: (0, 0)),
                  pl.BlockSpec((1, d), lambda bi, hi: (0, 0))],
        out_specs=[pl.BlockSpec((1, sh, d), lambda bi, hi: (bi, hi, 0)),
                   pl.BlockSpec((1, N_META, d), lambda bi, hi: (bi, 0, 0)),
                   pl.BlockSpec((1, 8, d), lambda bi, hi: (bi, 0, 0))],
        compiler_params=_params(("parallel", "arbitrary")),
    )(ds0, x, meta, g)


def _rows128(a):
    return a.reshape(-1, 128)


def kernel(x, meta_tokens, ln_in_g, ln_in_b, w_in, conv_w, conv_b, conv_ln_g, conv_ln_b, gate_up, gate_bias, gla_norm_g, w_out, ln1_g, ln1_b, w_ff1, w_ff2, ln2_g, ln2_b, loss_target, m_meta_tokens, m_ln_in_g, m_ln_in_b, m_w_in, m_conv_w, m_conv_b, m_conv_ln_g, m_conv_ln_b, m_gate_up, m_gate_bias, m_gla_norm_g, m_w_out, m_ln1_g, m_ln1_b, m_w_ff1, m_w_ff2, m_ln2_g, m_ln2_b, v_meta_tokens, v_ln_in_g, v_ln_in_b, v_w_in, v_conv_w, v_conv_b, v_conv_ln_g, v_conv_ln_b, v_gate_up, v_gate_bias, v_gla_norm_g, v_w_out, v_ln1_g, v_ln1_b, v_w_ff1, v_w_ff2, v_ln2_g, v_ln2_b):
    bsz, seq, d = x.shape
    tp = seq + HEAD
    r = bsz * tp
    xi, yi, ci = _mesh_pos()
    chip = 2 * xi + yi
    c_arr = jnp.reshape(ci, (1,)).astype(jnp.int32)
    pos_arr = jnp.stack([chip, ci]).astype(jnp.int32)

    sh_in = D_IN // 4
    shard_in = jnp.pad(w_in[0].T.astype(BF16), ((0, D_IN_PAD // 4 - sh_in), (0, 0)))
    shard_w1t, shard_wout, shard_w2 = w_ff1[0].T.astype(BF16), w_out[0].astype(BF16), w_ff2[0].astype(BF16)
    small_w = jnp.concatenate([_rows128(meta_tokens), _rows128(conv_w[0]), _rows128(gate_up[0])], axis=0)
    (g_int,), g_small = _gather_weights([shard_in], small_w)
    w_int = jnp.pad(g_int[:, :sh_in].reshape(D_IN, d), ((0, D_IN_PAD - D_IN), (0, 0)))
    n_meta_rows, n_cw_rows = N_META * 256 // 128, CONV_WIDTH * 128 // 128
    meta_full = jnp.concatenate([g_small[j, :n_meta_rows].reshape(N_META, 256) for j in range(4)], axis=1)
    convw_full = jnp.concatenate(
        [g_small[j, n_meta_rows:n_meta_rows + n_cw_rows].reshape(CONV_WIDTH, 128) for j in range(4)], axis=1)
    gup_full = jnp.concatenate(
        [g_small[j, n_meta_rows + n_cw_rows:].reshape(GLA_RANK, 64) for j in range(4)], axis=1)
    convw_p = jnp.pad(convw_full, ((0, 1), (0, 0)))
    gup_p = jnp.pad(gup_full, ((0, 128 - GLA_RANK), (0, 0))).astype(BF16)
    ln_in_g2, ln_in_b2 = ln_in_g.reshape(1, d), ln_in_b.reshape(1, d)

    s0, s0b = _ln_in_fwd(x, meta_full, ln_in_g2, ln_in_b2)
    s0f, s0bf = s0.reshape(r, d), s0b.reshape(r, d)
    u, ((w1_buf,),) = _in_proj(s0bf, w_int, [("gather", [(shard_w1t, 0, 640, None)])])
    (w1_buf,) = _place_own([w1_buf], [shard_w1t])
    u3 = u.reshape(bsz, tp, D_IN_PAD)
    hc, ((w1_buf,),) = _conv_fwd(u3, convw_p, conv_b, [("gather", [(shard_w1t, 640, 384, w1_buf)])])
    hc = hc.reshape(r, D_CONV)
    gla_out, o_pre, states, ((g_wout, w2_buf),) = _gla_fwd(
        u3, gup_p, gate_bias, gla_norm_g, [("gather", [(shard_wout, 0, 256, None), (shard_w2, 0, 256, None)])])
    g_wout, w2_buf = _place_own([g_wout, w2_buf], [shard_wout, shard_w2])
    wout = g_wout.reshape(d, d)
    gla_of = gla_out.reshape(r, 512)
    conv_of, xhat1, rstd1, s1b, ((w2_buf,),) = _out_proj_ln1(
        hc, gla_of, wout, s0f, conv_ln_g, conv_ln_b, ln1_g, ln1_b, [("gather", [(shard_w2, 256, 384, w2_buf)])])
    w1t = w1_buf.reshape(D_FF, d)
    ra, ((w2_buf,),) = _ffn1(s1b, w1t, [("gather", [(shard_w2, 640, 384, w2_buf)])])
    w2 = w2_buf.reshape(D_FF, d)
    dt2, dt2b, acc2 = _ffn2_ln2_loss(ra, w2, xhat1, ln1_g, ln1_b, ln2_g, ln2_b, loss_target, tp)

    def add_pair(g, got, splits=None):
        return _add_pair(g, got, c_arr, splits)

    da = _ffn_bwd_da(dt2b, w2, ra)
    dt1, dt1b, acc1 = _ffn_bwd_ln1(da, w1t, dt2, xhat1, rstd1, ln1_g)
    g_w2, _ = _matmul_tn(ra, dt2b, 1024, square_lhs=True, name="grad_w_ff2")
    big_w2 = g_w2.reshape(4, D_FF // 4, d)
    g_w1t, ((pair_w2,),) = _matmul_tn(da, s1b, 1024, name="grad_w_ff1", fuse=[("pair", [big_w2])])
    big_w1 = g_w1t.reshape(4, D_FF // 4, d)
    dhc, dgla, cacc, ((pair_w1,),) = _out_proj_bwd(dt1b, wout, hc, conv_ln_g, conv_ln_b, [("pair", [big_w1])])
    (part_w1,), (part_w2a, part_w2b) = add_pair(big_w1, pair_w1), add_pair(big_w2, pair_w2, [512, 512])
    g_wout = jnp.concatenate([_matmul_tn(conv_of, dt1b, D_CONV, name="grad_w_out_conv")[0],
                              _matmul_tn(gla_of, dt1b, 512, name="grad_w_out_gla")[0]], axis=0)
    big_wout = g_wout.reshape(4, d // 4, d)
    dcv, dcg, dcw, ((chip_w1,), (pair_wout,)) = _conv_bwd(
        dhc.reshape(bsz, tp, D_CONV), u3, convw_p, [("exchange", [part_w1]), ("pair", [big_wout])])
    (part_wout,) = add_pair(big_wout, pair_wout)
    (dq, dk, dv, dr, dgd, dgup, gvec), ((chip_w2a,),) = _gla_bwd(
        dgla.reshape(bsz, tp, 512), u3, o_pre, states, gup_p, gate_bias, gla_norm_g, [("exchange", [part_w2a])])
    pieces = [a.reshape(r, a.shape[-1]) for a in (dcv, dcg, dq, dk, dv, dr)]
    g_wint, ((chip_w2b, chip_wout),) = _grad_w_in(
        pieces, dgd, s0bf, tp, [("exchange", [part_w2b, part_wout])])
    big_win = jnp.stack([g_wint[j * sh_in:(j + 1) * sh_in] for j in range(4)])
    (pair_win,) = _pair_exchange([big_win])
    ds0, ((chip_win,),) = _in_proj_bwd(pieces, dgd, w_int, dt1, tp, [("exchange", add_pair(big_win, pair_win))])
    grad_x, dmeta, lvec = _ln_in_bwd(ds0.reshape(bsz, tp, d), x, meta_full, ln_in_g2)

    small = {
        "loss": acc2[0:1], "ln_in_g": jnp.sum(lvec[:, 0], axis=0), "ln_in_b": jnp.sum(lvec[:, 1], axis=0),
        "conv_b": cacc[0], "conv_ln_g": cacc[1], "conv_ln_b": cacc[2], "gate_bias": jnp.sum(gvec[:, 0], axis=0),
        "gla_norm_g": jnp.sum(gvec[:, 1].reshape(bsz * 2, 128), axis=0),
        "ln1_g": acc1[0], "ln1_b": acc1[1], "ln2_g": acc2[1], "ln2_b": acc2[2],
        "conv_w": jnp.sum(dcw, axis=0)[:CONV_WIDTH], "gate_up": jnp.sum(dgup, axis=0)[:GLA_RANK],
        "meta_tokens": jnp.sum(dmeta, axis=0),
    }
    pack = jnp.concatenate([_rows128(small[k]) for k, _ in _SMALL_FIELDS], axis=0)
    pack = jnp.pad(pack, ((0, SMALL_ROWS - pack.shape[0]), (0, 0)))
    red = _sum_small(_allgather_small(pack))
    off, tot = {}, 0
    for k, nrows in _SMALL_FIELDS:
        off[k] = (tot, nrows)
        tot += nrows

    def field(k, shape):
        o, nrows = off[k]
        return red[o:o + nrows].reshape(shape)

    loss = jnp.sum(field("loss", (d,)))

    halves = [_add_chips(g, t, s, pos_arr) for g, t, s in (
        (big_win, pair_win, [chip_win]), (big_wout, pair_wout, [chip_wout]), (big_w1, pair_w1, [chip_w1]),
        (big_w2, pair_w2, [chip_w2a, chip_w2b]))]
    f_wint, f_wout, f_w1t, f_w2 = [
        jnp.concatenate([jnp.where(ci == 0, own, sib), jnp.where(ci == 0, sib, own)], axis=1)
        for own, sib in zip(halves, _pair_swap(halves))]

    def cols(a, width):
        return lax.dynamic_slice_in_dim(a, chip * width, width, axis=a.ndim - 1)

    grads = {
        "meta_tokens": cols(field("meta_tokens", (N_META, d)), 256),
        "ln_in_g": field("ln_in_g", (d,)), "ln_in_b": field("ln_in_b", (d,)),
        "w_in": f_wint.T[None], "conv_w": cols(field("conv_w", (CONV_WIDTH, D_CONV)), 128)[None],
        "conv_b": field("conv_b", (1, D_CONV)), "conv_ln_g": field("conv_ln_g", (1, D_CONV)),
        "conv_ln_b": field("conv_ln_b", (1, D_CONV)),
        "gate_up": cols(field("gate_up", (GLA_RANK, 256)), 64)[None],
        "gate_bias": field("gate_bias", (1, 256)), "gla_norm_g": field("gla_norm_g", (1, 128)),
        "w_out": f_wout[None], "ln1_g": field("ln1_g", (1, d)), "ln1_b": field("ln1_b", (1, d)),
        "w_ff1": f_w1t.T[None], "w_ff2": f_w2[None],
        "ln2_g": field("ln2_g", (1, d)), "ln2_b": field("ln2_b", (1, d)),
    }
    weights = dict(meta_tokens=meta_tokens, ln_in_g=ln_in_g, ln_in_b=ln_in_b, w_in=w_in, conv_w=conv_w, conv_b=conv_b,
                   conv_ln_g=conv_ln_g, conv_ln_b=conv_ln_b, gate_up=gate_up, gate_bias=gate_bias,
                   gla_norm_g=gla_norm_g, w_out=w_out, ln1_g=ln1_g, ln1_b=ln1_b, w_ff1=w_ff1, w_ff2=w_ff2,
                   ln2_g=ln2_g, ln2_b=ln2_b)
    moms = dict(meta_tokens=(m_meta_tokens, v_meta_tokens), ln_in_g=(m_ln_in_g, v_ln_in_g),
                ln_in_b=(m_ln_in_b, v_ln_in_b), w_in=(m_w_in, v_w_in), conv_w=(m_conv_w, v_conv_w),
                conv_b=(m_conv_b, v_conv_b), conv_ln_g=(m_conv_ln_g, v_conv_ln_g),
                conv_ln_b=(m_conv_ln_b, v_conv_ln_b), gate_up=(m_gate_up, v_gate_up),
                gate_bias=(m_gate_bias, v_gate_bias), gla_norm_g=(m_gla_norm_g, v_gla_norm_g),
                w_out=(m_w_out, v_w_out), ln1_g=(m_ln1_g, v_ln1_g), ln1_b=(m_ln1_b, v_ln1_b),
                w_ff1=(m_w_ff1, v_w_ff1), w_ff2=(m_w_ff2, v_w_ff2), ln2_g=(m_ln2_g, v_ln2_g),
                ln2_b=(m_ln2_b, v_ln2_b))
    names = list(weights)
    big_names = ("w_in", "w_out", "w_ff1", "w_ff2")
    delta, new_m, new_v = {}, {}, {}
    lin = lambda a: a.reshape(-1, 128)
    unlin = lambda a: a.reshape(sh_in, d).T[None]
    dl, mn, vn = _adamw(lin(w_in[0].T), lin(f_wint), lin(m_w_in[0].T), lin(v_w_in[0].T))
    delta["w_in"], new_m["w_in"], new_v["w_in"] = unlin(dl), unlin(mn), unlin(vn)
    for k in big_names[1:]:
        shp = weights[k].shape
        two = lambda a: a.reshape(shp[-2], shp[-1])
        dl, mn, vn = _adamw(two(weights[k]), two(grads[k]), two(moms[k][0]), two(moms[k][1]))
        delta[k], new_m[k], new_v[k] = dl.reshape(shp), mn.reshape(shp), vn.reshape(shp)
    small_names = [k for k in names if k not in big_names]
    sizes = [weights[k].size for k in small_names]
    total = sum(sizes)
    padded = -(-total // 1024) * 1024

    def flat(get):
        v = jnp.concatenate([get(k).reshape(-1) for k in small_names])
        return jnp.pad(v, (0, padded - total)).reshape(padded // 128, 128)

    ones = jnp.ones((padded - total,), F32)
    vflat = jnp.concatenate([jnp.concatenate([moms[k][1].reshape(-1) for k in small_names]), ones])
    dl, mn, vn = _adamw(flat(lambda k: weights[k]), flat(lambda k: grads[k]), flat(lambda k: moms[k][0]),
                        vflat.reshape(padded // 128, 128))
    pos = 0
    for k, sz in zip(small_names, sizes):
        shp = weights[k].shape
        delta[k] = dl.reshape(-1)[pos:pos + sz].reshape(shp)
        new_m[k] = mn.reshape(-1)[pos:pos + sz].reshape(shp)
        new_v[k] = vn.reshape(-1)[pos:pos + sz].reshape(shp)
        pos += sz

    return (loss, grad_x, *[grads[k] for k in names], *[delta[k] for k in names],
            *[new_m[k] for k in names], *[new_v[k] for k in names])
```

```python
import functools

import jax
import jax.numpy as jnp
from jax import lax
from jax.experimental import pallas as pl
from jax.experimental.pallas import tpu as pltpu

F32 = jnp.float32
BF16 = jnp.bfloat16

D_MODEL = 1024
N_META = 16
D_CONV = 512
CONV_WIDTH = 31
GLA_HEADS = 4
GLA_DV = 128
GLA_DK = 64
GLA_RANK = 16
GLA_TAU = 16.0
CHUNK = 64
D_FF = 4096
LN_EPS = 1e-5
ALPHA = 2.0 ** 0.25
D_IN = 2576
D_IN_PAD = 2688
PAD = CHUNK - N_META
HEAD = PAD + N_META
Q_SCALE = GLA_DK ** -0.5
ADAM_LR, ADAM_B1, ADAM_B2, ADAM_EPS, ADAM_WD, ADAM_STEP = 0.001, 0.9, 0.999, 1e-08, 0.01, 10
HALF = D_MODEL // 2
VMEM_LIMIT = 56 * 1024 * 1024
MESH = pl.DeviceIdType.MESH

C_VAL, C_GATE, C_Q, C_K, C_V, C_R, C_GD = 0, 512, 1024, 1280, 1536, 2048, 2560

_SMALL_FIELDS = (("loss", 8), ("ln_in_g", 8), ("ln_in_b", 8), ("conv_b", 4), ("conv_ln_g", 4), ("conv_ln_b", 4),
                 ("gate_bias", 2), ("gla_norm_g", 1), ("ln1_g", 8), ("ln1_b", 8), ("ln2_g", 8), ("ln2_b", 8),
                 ("conv_w", 124), ("gate_up", 32), ("meta_tokens", 128))
SMALL_ROWS = 360


def _params(sem=None, **kw):
    return pltpu.CompilerParams(dimension_semantics=sem, vmem_limit_bytes=VMEM_LIMIT, **kw)


def _row_tile(tp):
    for t in (704, 352, 192, 64):
        if tp % t == 0:
            return t
    raise ValueError(tp)


def _reduce_tile(tp, big):
    for t in ((2112, 1056, 704) if big else (1056, 704)) + (352, 192, 64):
        if tp % t == 0:
            return t
    raise ValueError(tp)


def _sub_rows(tm):
    return [slice(0, tm)]


def _dot(a, b, dims, precision=None):
    return lax.dot_general(a, b, (dims, ((), ())), preferred_element_type=F32, precision=precision)


def _nn(a, b, **kw):
    return _dot(a, b, ((1,), (0,)), **kw)


def _nt(a, b, **kw):
    return _dot(a, b, ((1,), (1,)), **kw)


def _tn(a, b, **kw):
    return _dot(a, b, ((0,), (0,)), **kw)


def _sigmoid(x):
    return 1.0 / (1.0 + jnp.exp(-x))


def _log_sigmoid(z):
    return jnp.minimum(z, 0.0) - jnp.log(1.0 + jnp.exp(-jnp.abs(z)))


def _ln_stats(t):
    mu = jnp.mean(t, axis=-1, keepdims=True)
    d = t - mu
    var = jnp.mean(d * d, axis=-1, keepdims=True)
    rstd = lax.rsqrt(var + LN_EPS)
    return d * rstd, rstd


def _ln_bwd(dxhat, xhat, rstd):
    m1 = jnp.mean(dxhat, axis=-1, keepdims=True)
    m2 = jnp.mean(dxhat * xhat, axis=-1, keepdims=True)
    return rstd * (dxhat - m1 - xhat * m2)


def _mesh_pos():
    return lax.axis_index("x"), lax.axis_index("y"), lax.axis_index("c")


ANY = pl.BlockSpec(memory_space=pl.ANY)


def _gather_sems(n):
    return [pltpu.SemaphoreType.DMA((n, 3))] * 4


def _gather_steps(ins, outs, sems, ranges=None):
    n = len(ins)
    send, recv, fsend, frecv = sems
    ranges = ranges or [(0, ref.shape[0]) for ref in ins]
    x, y, c = _mesh_pos()
    me = 2 * x + y
    sibling = (x, y, 1 - c)
    chips = [(1 - x, y), (x, 1 - y), (1 - x, 1 - y)]
    chip_idx = [2 * px + py for px, py in chips]
    mine = [pl.ds(pl.multiple_of(r0 + c * (nr // 2), 16), nr // 2) for r0, nr in ranges]
    other = [pl.ds(pl.multiple_of(r0 + (1 - c) * (nr // 2), 16), nr // 2) for r0, nr in ranges]
    pairs = [(a, k) for a in range(n) for k in range(3)]

    def ici(a, k, slab):
        return pltpu.make_async_remote_copy(
            src_ref=ins[a].at[mine[a], :], dst_ref=outs[a].at[slab, mine[a], :],
            send_sem=send.at[a, k], recv_sem=recv.at[a, k], device_id=(*chips[k], c), device_id_type=MESH)

    def forward(a, k, rows):
        blk = outs[a].at[chip_idx[k], rows[a], :]
        return pltpu.make_async_remote_copy(
            src_ref=blk, dst_ref=blk, send_sem=fsend.at[a, k], recv_sem=frecv.at[a, k],
            device_id=sibling, device_id_type=MESH)

    def start():
        for a, k in pairs:
            ici(a, k, me).start()

    def relay():
        for a, k in pairs:
            ici(a, k, chip_idx[k]).wait_recv()
            forward(a, k, mine).start()

    def finish():
        for a, k in pairs:
            forward(a, k, other).wait_recv()
        for a, k in pairs:
            ici(a, k, me).wait_send()
            forward(a, k, mine).wait_send()

    return start, relay, finish


def _place_own(gathered, shards):
    chip = 2 * lax.axis_index("x") + lax.axis_index("y")
    return [lax.dynamic_update_slice(g, s[None], (chip, 0, 0)) for g, s in zip(gathered, shards)]


def _gather_weights(shards, small):
    n = len(shards)

    def body(*refs):
        ins, small_in = refs[:n], refs[n]
        outs, small_out = refs[n + 1:2 * n + 1], refs[2 * n + 1]
        ssend, srecv = refs[2 * n + 6:]
        start, relay, finish = _gather_steps(ins, outs, refs[2 * n + 2:2 * n + 6])
        x, y, c = _mesh_pos()
        chips = [(1 - x, y), (x, 1 - y), (1 - x, 1 - y)]

        def small_copy(k, slot):
            return pltpu.make_async_remote_copy(
                src_ref=small_in, dst_ref=small_out.at[slot], send_sem=ssend.at[k], recv_sem=srecv.at[k],
                device_id=(*chips[k], c), device_id_type=MESH)

        start()
        for k in range(3):
            small_copy(k, 2 * x + y).start()
        relay()
        finish()
        for k, (px, py) in enumerate(chips):
            small_copy(k, 2 * px + py).wait_recv()
        for k in range(3):
            small_copy(k, 2 * x + y).wait_send()

    out_shape = [jax.ShapeDtypeStruct((4,) + s.shape, s.dtype) for s in shards]
    out_shape.append(jax.ShapeDtypeStruct((4,) + small.shape, small.dtype))
    res = pl.pallas_call(
        body, name="gather_weights", out_shape=out_shape,
        in_specs=[ANY] * (n + 1), out_specs=[ANY] * (n + 1),
        scratch_shapes=_gather_sems(n) + [pltpu.SemaphoreType.DMA((3,)), pltpu.SemaphoreType.DMA((3,))],
    )(*shards, small)
    res = _place_own(res, list(shards) + [small])
    return res[:n], res[n]


def _pair_exchange_steps(ins, outs, sems):
    send, recv = sems
    x, y, c = _mesh_pos()
    other = pl.ds(pl.multiple_of((1 - c) * HALF, 128), HALF)
    cps = [pltpu.make_async_remote_copy(
        src_ref=ins[a].at[:, :, other], dst_ref=outs[a], send_sem=send.at[a], recv_sem=recv.at[a],
        device_id=(x, y, 1 - c), device_id_type=MESH) for a in range(len(ins))]

    def start():
        for cp in cps:
            cp.start()

    def finish():
        for cp in cps:
            cp.wait()

    return start, finish


def _pair_exchange_shapes(grads):
    return [jax.ShapeDtypeStruct(g.shape[:2] + (HALF,), g.dtype) for g in grads]


def _pair_exchange(grads):
    n = len(grads)

    def body(*refs):
        start, finish = _pair_exchange_steps(refs[:n], refs[n:2 * n], refs[2 * n:])
        start()
        finish()

    return pl.pallas_call(
        body, name="grad_pair_exchange", out_shape=_pair_exchange_shapes(grads),
        in_specs=[ANY] * n, out_specs=[ANY] * n,
        scratch_shapes=[pltpu.SemaphoreType.DMA((n,)), pltpu.SemaphoreType.DMA((n,))],
    )(*grads)


NO_FUSE = ()


def _fuse_plan(kind, items):
    n = len(items)
    if kind == "gather":
        shards = [it[0] for it in items]
        bufs = [it[3] for it in items if it[3] is not None]
        alias, b = {}, 0
        for a, it in enumerate(items):
            if it[3] is not None:
                alias[n + b] = a
                b += 1
        ranges = [(it[1], it[2]) for it in items]
        return (shards + bufs, [jax.ShapeDtypeStruct((4,) + s.shape, s.dtype) for s in shards], alias, _gather_sems(n),
                lambda i, o, s: _gather_steps(i[:n], o, s, ranges))
    if kind == "exchange":
        return list(items), _chip_exchange_shapes(items), {}, _chip_exchange_sems(n), _chip_exchange_steps
    return (list(items), _pair_exchange_shapes(items), {},
            [pltpu.SemaphoreType.DMA((n,)), pltpu.SemaphoreType.DMA((n,))], _pair_exchange_steps)


def _fused_call(body, fuse, *, name, grid, in_specs, out_specs, out_shape, scratch_shapes, operands):
    plans = [_fuse_plan(kind, list(items)) for kind, items in fuse if len(items)]
    n_in, n_out, n_s = len(in_specs), len(out_shape), len(scratch_shapes)
    comm = [a for p in plans for a in p[0]]
    shapes = [s for p in plans for s in p[1]]
    nc, no = len(comm), len(shapes)
    aliases, i_at, o_at = {}, n_in, n_out
    for p in plans:
        for i, o in p[2].items():
            aliases[i_at + i] = o_at + o
        i_at, o_at = i_at + len(p[0]), o_at + len(p[1])

    def wrapped(*refs):
        o0 = n_in + nc
        s0 = o0 + n_out + no
        i_at, o_at, sem_at, steps = n_in, o0 + n_out, s0 + n_s, []
        for p in plans:
            steps.append(p[4](refs[i_at:i_at + len(p[0])], refs[o_at:o_at + len(p[1])], refs[sem_at:sem_at + len(p[3])]))
            i_at, o_at, sem_at = i_at + len(p[0]), o_at + len(p[1]), sem_at + len(p[3])
        first, last = _grid_ends(grid)
        for st in steps:
            pl.when(first)(st[0])
        for st in steps:
            for mid in st[1:-1]:
                pl.when(last)(mid)
        body(*refs[:n_in], *refs[o0:o0 + n_out], *refs[s0:s0 + n_s])
        for st in steps:
            pl.when(last)(st[-1])

    res = pl.pallas_call(
        wrapped if plans else body, name=name, grid=grid, in_specs=list(in_specs) + [ANY] * nc,
        out_specs=list(out_specs) + [ANY] * no, out_shape=list(out_shape) + shapes,
        scratch_shapes=list(scratch_shapes) + [s for p in plans for s in p[3]], input_output_aliases=aliases,
        compiler_params=_params(("arbitrary",) * len(grid)))(*operands, *comm)
    outs, got, results = list(res[:n_out]), list(res[n_out:]), []
    for p in plans:
        results.append(got[:len(p[1])])
        got = got[len(p[1]):]
    return outs, results


def _chip_exchange_sems(n):
    return [pltpu.SemaphoreType.DMA((n, 3))] * 2


def _chip_exchange_shapes(parts):
    return [jax.ShapeDtypeStruct((3,) + p.shape[1:], p.dtype) for p in parts]


def _chip_exchange_steps(ins, outs, sems):
    send, recv = sems
    x, y, c = _mesh_pos()
    chips = [(1 - x, y), (x, 1 - y), (1 - x, 1 - y)]
    cps = [pltpu.make_async_remote_copy(
        src_ref=ins[a].at[2 * px + py], dst_ref=outs[a].at[k], send_sem=send.at[a, k], recv_sem=recv.at[a, k],
        device_id=(px, py, c), device_id_type=MESH) for a in range(len(ins)) for k, (px, py) in enumerate(chips)]

    def start():
        for cp in cps:
            cp.start()

    def finish():
        for cp in cps:
            cp.wait()

    return start, finish


def _chip_exchange(parts):
    n = len(parts)

    def body(*refs):
        start, finish = _chip_exchange_steps(refs[:n], refs[n:2 * n], refs[2 * n:])
        start()
        finish()

    return pl.pallas_call(
        body, name="grad_chip_exchange", out_shape=_chip_exchange_shapes(parts),
        in_specs=[ANY] * n, out_specs=[ANY] * n, scratch_shapes=_chip_exchange_sems(n),
    )(*parts)


def _pair_swap(halves):
    n = len(halves)

    def body(*refs):
        ins, outs = refs[:n], refs[n:2 * n]
        send, recv = refs[2 * n:]
        x, y, c = _mesh_pos()
        cps = [pltpu.make_async_remote_copy(
            src_ref=ins[a], dst_ref=outs[a], send_sem=send.at[a], recv_sem=recv.at[a],
            device_id=(x, y, 1 - c), device_id_type=MESH) for a in range(n)]
        for cp in cps:
            cp.start()
        for cp in cps:
            cp.wait()

    return pl.pallas_call(
        body, name="grad_pair_swap",
        out_shape=[jax.ShapeDtypeStruct(h.shape, h.dtype) for h in halves],
        in_specs=[ANY] * n, out_specs=[ANY] * n,
        scratch_shapes=[pltpu.SemaphoreType.DMA((n,)), pltpu.SemaphoreType.DMA((n,))],
    )(*halves)


def _allgather_small(pack):
    m_per, ncol = pack.shape

    def body(x_ref, out_ref, send_sems, recv_sems, local_sem):
        x, y, c = _mesh_pos()
        me, sibling = (x, y, c), (x, y, 1 - c)
        chips = [(1 - x, y), (x, 1 - y), (1 - x, 1 - y)]

        def rows(px, py, pc):
            return out_ref.at[pl.ds(pl.multiple_of((4 * px + 2 * py + pc) * m_per, 8), m_per), :]

        def copy(k, block, to, src=None):
            return pltpu.make_async_remote_copy(
                src_ref=rows(*block) if src is None else src, dst_ref=rows(*block),
                send_sem=send_sems.at[k], recv_sem=recv_sems.at[k], device_id=to, device_id_type=MESH)

        mine = pltpu.make_async_copy(x_ref, rows(*me), local_sem)
        mine.start()
        first = [copy(0, me, sibling, src=x_ref)]
        first += [copy(1 + j, me, (*chip, c), src=x_ref) for j, chip in enumerate(chips)]
        for cp in first:
            cp.start()
        passed = [copy(4 + j, (*chip, c), sibling) for j, chip in enumerate(chips)]
        for j, chip in enumerate(chips):
            copy(1 + j, (*chip, c), me).wait_recv()
            passed[j].start()
        copy(0, sibling, me).wait_recv()
        for j, chip in enumerate(chips):
            copy(4 + j, (*chip, 1 - c), me).wait_recv()
        for cp in first + passed:
            cp.wait_send()
        mine.wait()

    return pl.pallas_call(
        body, name="allgather_small",
        out_shape=jax.ShapeDtypeStruct((8 * m_per, ncol), pack.dtype),
        in_specs=[pl.BlockSpec(memory_space=pltpu.VMEM)],
        out_specs=pl.BlockSpec(memory_space=pltpu.VMEM),
        scratch_shapes=[pltpu.SemaphoreType.DMA((7,)), pltpu.SemaphoreType.DMA((7,)), pltpu.SemaphoreType.DMA],
    )(pack)


def _add_pair(g, got, c_arr, splits=None):
    _, rows, _ = g.shape
    splits = splits or [rows]

    def body(c_ref, g_ref, r_ref, *o_refs):
        at = 0
        for o_ref, n in zip(o_refs, splits):
            o_ref[...] = (g_ref[:, at:at + n, :] + r_ref[:, at:at + n, :]).astype(BF16)
            at += n

    return pl.pallas_call(
        body, name="grad_add_pair", out_shape=[jax.ShapeDtypeStruct((4, n, HALF), BF16) for n in splits],
        grid_spec=pltpu.PrefetchScalarGridSpec(
            num_scalar_prefetch=1, grid=(4,),
            in_specs=[pl.BlockSpec((1, rows, HALF), lambda j, c: (j, 0, c[0])),
                      pl.BlockSpec((1, rows, HALF), lambda j, c: (j, 0, 0))],
            out_specs=[pl.BlockSpec((1, n, HALF), lambda j, c: (j, 0, 0)) for n in splits]),
        compiler_params=_params(("arbitrary",)),
    )(c_arr, g, got)


def _add_chips(g, pair_got, chip_gots, pos_arr):
    _, rows, _ = g.shape

    def body(pos_ref, g_ref, p_ref, *refs):
        o_ref, at = refs[-1], 0
        for r_ref in refs[:-1]:
            n = r_ref.shape[1]
            own = g_ref[0, at:at + n, :] + p_ref[0, at:at + n, :]
            o_ref[at:at + n, :] = ((own + r_ref[0].astype(F32)) + r_ref[1].astype(F32)) + r_ref[2].astype(F32)
            at += n

    return pl.pallas_call(
        body, name="grad_add_chips", out_shape=jax.ShapeDtypeStruct((rows, HALF), F32),
        grid_spec=pltpu.PrefetchScalarGridSpec(
            num_scalar_prefetch=1, grid=(1,),
            in_specs=[pl.BlockSpec((1, rows, HALF), lambda i, p: (p[0], 0, p[1])),
                      pl.BlockSpec((1, rows, HALF), lambda i, p: (p[0], 0, 0))]
            + [pl.BlockSpec(t.shape, lambda i, p: (0, 0, 0)) for t in chip_gots],
            out_specs=pl.BlockSpec((rows, HALF), lambda i, p: (0, 0))),
        compiler_params=_params(("arbitrary",)),
    )(pos_arr, g, pair_got, *chip_gots)


def _sum_small(gathered):
    def body(g_ref, o_ref):
        acc = g_ref[0:SMALL_ROWS, :]
        for d in range(1, 8):
            acc = acc + g_ref[d * SMALL_ROWS:(d + 1) * SMALL_ROWS, :]
        o_ref[...] = acc

    return pl.pallas_call(
        body, name="sum_small", out_shape=jax.ShapeDtypeStruct((SMALL_ROWS, 128), F32),
    )(gathered)


def _adamw(w, g, m, v):
    rows, cols = w.shape
    tr = 256 if rows % 256 == 0 else rows
    c1 = 1.0 - ADAM_B1 ** ADAM_STEP
    c2 = 1.0 - ADAM_B2 ** ADAM_STEP

    def body(w_ref, g_ref, m_ref, v_ref, d_ref, mo_ref, vo_ref):
        gg = g_ref[...]
        mn = ADAM_B1 * m_ref[...] + (1.0 - ADAM_B1) * gg
        vn = ADAM_B2 * v_ref[...] + (1.0 - ADAM_B2) * (gg * gg)
        d_ref[...] = -ADAM_LR * ((mn / c1) / (jnp.sqrt(vn / c2) + ADAM_EPS) + ADAM_WD * w_ref[...])
        mo_ref[...] = mn
        vo_ref[...] = vn

    spec = pl.BlockSpec((tr, cols), lambda i: (i, 0))
    return pl.pallas_call(
        body, name="adamw", out_shape=[jax.ShapeDtypeStruct(w.shape, F32)] * 3,
        grid=(rows // tr,), in_specs=[spec] * 4, out_specs=[spec] * 3,
        compiler_params=_params(("parallel",)),
    )(w, g, m, v)


def _ln_in_fwd(x, meta, g, b):
    bsz, s, d = x.shape
    tp = s + HEAD
    nh = 2
    sh = s // nh
    rc = min(256, sh)

    def body(x_ref, meta_ref, g_ref, b_ref, s0_ref, s0b_ref):
        h = pl.program_id(1)
        gg, bb = g_ref[...], b_ref[...]

        @pl.when(h == 0)
        def _():
            s0_ref[0, 0:PAD, :] = jnp.zeros((PAD, d), F32)
            s0b_ref[0, 0:PAD, :] = jnp.zeros((PAD, d), BF16)
            mh, _ = _ln_stats(meta_ref[...])
            mv = mh * gg + bb
            s0_ref[0, PAD:HEAD, :] = mv
            s0b_ref[0, PAD:HEAD, :] = mv.astype(BF16)

        def step(i, carry):
            src = pl.ds(pl.multiple_of(i * rc, rc), rc)
            dst = pl.ds(pl.multiple_of(HEAD + h * sh + i * rc, 64), rc)
            xh, _ = _ln_stats(x_ref[0, src, :])
            val = xh * gg + bb
            s0_ref[0, dst, :] = val
            s0b_ref[0, dst, :] = val.astype(BF16)
            return carry

        lax.fori_loop(0, sh // rc, step, 0)

    full = lambda bi, hi: (bi, 0, 0)
    return pl.pallas_call(
        body, name="ln_in_fwd",
        out_shape=[jax.ShapeDtypeStruct((bsz, tp, d), F32), jax.ShapeDtypeStruct((bsz, tp, d), BF16)],
        grid=(bsz, nh),
        in_specs=[pl.BlockSpec((1, sh, d), lambda bi, hi: (bi, hi, 0)),
                  pl.BlockSpec((N_META, d), lambda bi, hi: (0, 0)),
                  pl.BlockSpec((1, d), lambda bi, hi: (0, 0)),
                  pl.BlockSpec((1, d), lambda bi, hi: (0, 0))],
        out_specs=[pl.BlockSpec((1, tp, d), full)] * 2,
        compiler_params=_params(("parallel", "arbitrary")),
    )(x, meta, g, b)


def _in_proj(s0b, w_int, fuse=NO_FUSE):
    r, d = s0b.shape
    tm = _row_tile(r)

    def body(a_ref, w_ref, o_ref):
        o_ref[...] = _nt(a_ref[...], w_ref[...])

    (u,), got = _fused_call(
        body, fuse, name="in_proj", out_shape=[jax.ShapeDtypeStruct((r, D_IN_PAD), F32)],
        grid=(r // tm,),
        in_specs=[pl.BlockSpec((tm, d), lambda i: (i, 0)), pl.BlockSpec((D_IN_PAD, d), lambda i: (0, 0))],
        out_specs=[pl.BlockSpec((tm, D_IN_PAD), lambda i: (i, 0))], scratch_shapes=[], operands=(s0b, w_int))
    return u, got


def _conv_fwd(u, conv_w, conv_b, fuse=NO_FUSE):
    bsz, tp, _ = u.shape
    nchunk = tp // CHUNK
    win = CHUNK + 32
    nct = D_CONV // 128

    def body(cv_ref, cg_ref, w_ref, cb_ref, hc_ref, h_scr, win_scr):
        h_scr[0:32, :] = jnp.zeros((32, 128), F32)
        h_scr[32:32 + tp, :] = cv_ref[0] * _sigmoid(cg_ref[0])
        cb = cb_ref[...]

        def step(n, carry):
            r0 = pl.multiple_of(n * CHUNK, CHUNK)
            win_scr[...] = h_scr[pl.ds(r0, win), :]
            acc = jnp.zeros((CHUNK, 128), F32)
            for j in range(CONV_WIDTH):
                acc = acc + w_ref[j:j + 1, :] * win_scr[2 + j:2 + j + CHUNK, :]
            hc_ref[0, pl.ds(r0, CHUNK), :] = acc + cb
            return carry

        lax.fori_loop(0, nchunk, step, 0)

    (hc,), got = _fused_call(
        body, fuse, name="conv_fwd", out_shape=[jax.ShapeDtypeStruct((bsz, tp, D_CONV), F32)],
        grid=(bsz, nct),
        in_specs=[pl.BlockSpec((1, tp, 128), lambda bi, t: (bi, 0, C_VAL // 128 + t)),
                  pl.BlockSpec((1, tp, 128), lambda bi, t: (bi, 0, C_GATE // 128 + t)),
                  pl.BlockSpec((32, 128), lambda bi, t: (0, t)),
                  pl.BlockSpec((1, 128), lambda bi, t: (0, t))],
        out_specs=[pl.BlockSpec((1, tp, 128), lambda bi, t: (bi, 0, t))],
        scratch_shapes=[pltpu.VMEM((tp + 32, 128), F32), pltpu.VMEM((win, 128), F32)],
        operands=(u, u, conv_w, conv_b))
    return hc, got


def _gla_group(nchunk):
    return 11 if nchunk % 11 == 0 else nchunk


def _bdot(a, b, ca, cb, precision=None):
    return lax.dot_general(a, b, (((ca,), (cb,)), ((0,), (0,))), preferred_element_type=F32, precision=precision)


def _bnn(a, b, **kw):
    return _bdot(a, b, 2, 1, **kw)


def _bnt(a, b, **kw):
    return _bdot(a, b, 2, 2, **kw)


def _gla_consts(nb):
    row = lax.broadcasted_iota(jnp.int32, (nb, CHUNK, CHUNK), 1)
    col = lax.broadcasted_iota(jnp.int32, (nb, CHUNK, CHUNK), 2)
    lane = lax.broadcasted_iota(jnp.int32, (1, 1, 128), 2)
    return row >= col, row <= col, [lane < GLA_DK, lane >= GLA_DK]


def _gla_group_terms(g, nb, q_ref, k_ref, gd_ref, gup_ref, gb_ref, tril):
    m = nb * CHUNK
    rows = pl.ds(pl.multiple_of(g * m, CHUNK), m)
    z = _nn(gd_ref[0, rows, :].astype(BF16), gup_ref[...]) + gb_ref[...]
    valid = g * m + lax.broadcasted_iota(jnp.int32, (m, 1), 0) >= PAD
    lg = jnp.where(valid, _log_sigmoid(z) * (1.0 / GLA_TAU), 0.0)
    bcum = _bnn(tril.astype(F32), lg.reshape(nb, CHUNK, 128), precision=lax.Precision.HIGHEST)
    blast = bcum[:, CHUNK - 1:CHUNK, :]
    eb = jnp.exp(bcum)
    enb = jnp.exp(-bcum)
    erest = jnp.exp(blast - bcum)
    q = (q_ref[0, rows, :] * Q_SCALE).reshape(nb, CHUNK, 128)
    k = k_ref[0, rows, :].reshape(nb, CHUNK, 128)
    return rows, valid, z, eb, enb, erest, jnp.exp(blast), q * eb, k * enb, k * erest


def _grid_ends(grid):
    ids = [pl.program_id(i) for i in range(len(grid))]
    first = functools.reduce(jnp.logical_and, [i == 0 for i in ids])
    last = functools.reduce(jnp.logical_and, [i == g - 1 for i, g in zip(ids, grid)])
    return first, last


def _gla_fwd(u, gup, gbias, gnorm, fuse=NO_FUSE):
    bsz, tp, _ = u.shape
    nchunk = tp // CHUNK
    nb = _gla_group(nchunk)

    def body(q_ref, k_ref, v_ref, r_ref, gd_ref, gup_ref, gb_ref, gn_ref, out_ref, o_ref, st_ref, s_scr):
        tril, _, hmask = _gla_consts(nb)
        s_scr[...] = jnp.zeros_like(s_scr)
        gn = gn_ref[...]

        def group(g, carry):
            rows, _, _, _, _, _, dec, qe, ke, kd = _gla_group_terms(g, nb, q_ref, k_ref, gd_ref, gup_ref, gb_ref, tril)
            keb, kdb = ke.astype(BF16), kd.astype(BF16)
            for h in range(2):
                cols = slice(h * GLA_DV, (h + 1) * GLA_DV)
                qh = jnp.where(hmask[h], qe, 0.0).astype(BF16)
                vh = v_ref[0, rows, cols].astype(BF16).reshape(nb, CHUNK, GLA_DV)
                a = jnp.where(tril, _bnt(qh, keb), 0.0).astype(BF16)
                st = s_scr[h]
                sts = []
                for n in range(nb):
                    st_ref[0, h, g * nb + n] = st
                    sts.append(st.astype(BF16))
                    st = dec[n] * st + _tn(vh[n], kdb[n])
                s_scr[h] = st
                o = (_bnn(a, vh) + _bnt(qh, jnp.stack(sts))).reshape(nb * CHUNK, GLA_DV)
                o_ref[0, rows, cols] = o
                rms = lax.rsqrt(jnp.mean(o * o, axis=-1, keepdims=True) + LN_EPS)
                rh = r_ref[0, rows, cols]
                out_ref[0, rows, cols] = (o * rms * gn * (rh * _sigmoid(rh))).astype(BF16)
            return carry

        lax.fori_loop(0, nchunk // nb, group, 0)

    res, got = _fused_call(
        body, fuse, name="gla_fwd",
        out_shape=[jax.ShapeDtypeStruct((bsz, tp, 512), BF16), jax.ShapeDtypeStruct((bsz, tp, 512), F32),
                   jax.ShapeDtypeStruct((bsz, GLA_HEADS, nchunk, GLA_DV, 128), F32)],
        grid=(bsz, 2),
        in_specs=[pl.BlockSpec((1, tp, 128), lambda bi, p: (bi, 0, C_Q // 128 + p)),
                  pl.BlockSpec((1, tp, 128), lambda bi, p: (bi, 0, C_K // 128 + p)),
                  pl.BlockSpec((1, tp, 256), lambda bi, p: (bi, 0, C_V // 256 + p)),
                  pl.BlockSpec((1, tp, 256), lambda bi, p: (bi, 0, C_R // 256 + p)),
                  pl.BlockSpec((1, tp, 128), lambda bi, p: (bi, 0, C_GD // 128)),
                  pl.BlockSpec((128, 128), lambda bi, p: (0, p)),
                  pl.BlockSpec((1, 128), lambda bi, p: (0, p)),
                  pl.BlockSpec((1, 128), lambda bi, p: (0, 0))],
        out_specs=[pl.BlockSpec((1, tp, 256), lambda bi, p: (bi, 0, p)),
                   pl.BlockSpec((1, tp, 256), lambda bi, p: (bi, 0, p)),
                   pl.BlockSpec((1, 2, nchunk, GLA_DV, 128), lambda bi, p: (bi, p, 0, 0, 0))],
        scratch_shapes=[pltpu.VMEM((2, GLA_DV, 128), F32)],
        operands=(u, u, u, u, u, gup, gbias, gnorm))
    return res[0], res[1], res[2], got


def _out_proj_ln1(hc, gla_out, w_out, s0, cg, cb, g1, b1, fuse=NO_FUSE):
    r, d = s0.shape
    tm = _row_tile(r)

    def body(hc_ref, a_ref, w_ref, s0_ref, cg_ref, cb_ref, g_ref, b_ref, co_ref, xh_ref, rstd_ref, s1b_ref):
        for rs in _sub_rows(tm):
            xc, _ = _ln_stats(hc_ref[rs, :])
            nv = xc * cg_ref[...] + cb_ref[...]
            co = (nv * _sigmoid(nv)).astype(BF16)
            co_ref[rs, :] = co
            mix = _nn(co, w_ref[0:D_CONV, :]) + _nn(a_ref[rs, :], w_ref[D_CONV:, :])
            xh, rstd = _ln_stats(ALPHA * s0_ref[rs, :] + mix)
            xh_ref[rs, :] = xh
            rstd_ref[rs, :] = rstd
            s1b_ref[rs, :] = (xh * g_ref[...] + b_ref[...]).astype(BF16)

    row = lambda n: pl.BlockSpec((tm, n), lambda i: (i, 0))
    vec = lambda n: pl.BlockSpec((1, n), lambda i: (0, 0))
    res, got = _fused_call(
        body, fuse, name="out_proj_ln1",
        out_shape=[jax.ShapeDtypeStruct((r, D_CONV), BF16), jax.ShapeDtypeStruct((r, d), F32),
                   jax.ShapeDtypeStruct((r, 1), F32), jax.ShapeDtypeStruct((r, d), BF16)],
        grid=(r // tm,),
        in_specs=[row(D_CONV), row(512), pl.BlockSpec((d, d), lambda i: (0, 0)), row(d),
                  vec(D_CONV), vec(D_CONV), vec(d), vec(d)],
        out_specs=[row(D_CONV), row(d), row(1), row(d)], scratch_shapes=[],
        operands=(hc, gla_out, w_out, s0, cg, cb, g1, b1))
    return res[0], res[1], res[2], res[3], got


def _ffn1(s1b, w1t, fuse=NO_FUSE):
    r, d = s1b.shape
    tm = _row_tile(r)

    def body(a_ref, w_ref, o_ref):
        o_ref[...] = jnp.maximum(_nt(a_ref[...], w_ref[...]), 0.0).astype(BF16)

    (ra,), got = _fused_call(
        body, fuse, name="ffn1", out_shape=[jax.ShapeDtypeStruct((r, D_FF), BF16)], grid=(r // tm,),
        in_specs=[pl.BlockSpec((tm, d), lambda i: (i, 0)), pl.BlockSpec((D_FF, d), lambda i: (0, 0))],
        out_specs=[pl.BlockSpec((tm, D_FF), lambda i: (i, 0))], scratch_shapes=[], operands=(s1b, w1t))
    return ra, got


def _ffn2_ln2_loss(ra, w2, xhat1, g1, b1, g2, b2, tgt, tp):
    r, d = xhat1.shape
    tm = _row_tile(tp)
    per = tp // tm

    def body(ra_ref, w_ref, xh1_ref, g1_ref, b1_ref, g2_ref, b2_ref, tgt_ref, dt_ref, dtb_ref, acc_ref, t_ref, sem):
        i = pl.program_id(0)
        b, j = i // per, i % per

        @pl.when(i == 0)
        def _():
            acc_ref[...] = jnp.zeros_like(acc_ref)

        head_copy = pltpu.make_async_copy(tgt_ref.at[b, pl.ds(0, tm - HEAD), :], t_ref.at[pl.ds(HEAD, tm - HEAD), :], sem)
        body_copy = pltpu.make_async_copy(
            tgt_ref.at[b, pl.ds(pl.multiple_of(jnp.maximum(j * tm - HEAD, 0), 64), tm), :], t_ref, sem)

        @pl.when(j == 0)
        def _():
            t_ref[0:HEAD, :] = jnp.zeros((HEAD, d), F32)
            head_copy.start()

        pl.when(j > 0)(body_copy.start)

        sums = [jnp.zeros((1, d), F32)] * 3
        for rs in _sub_rows(tm):
            rb = ra_ref[rs, :]
            f = _nn(rb * rb, w_ref[...])
            if rs.start == 0:
                pl.when(j == 0)(head_copy.wait)
                pl.when(j > 0)(body_copy.wait)
            s1 = xh1_ref[rs, :] * g1_ref[...] + b1_ref[...]
            xh2, rstd2 = _ln_stats(ALPHA * s1 + f)
            y = xh2 * g2_ref[...] + b2_ref[...]
            rowid = (i % per) * tm + rs.start + lax.broadcasted_iota(jnp.int32, (rs.stop - rs.start, 1), 0)
            e = jnp.where(rowid >= HEAD, y - t_ref[rs, :], 0.0)
            dy = e * (1.0 / d)
            dt2 = _ln_bwd(dy * g2_ref[...], xh2, rstd2)
            dt_ref[rs, :] = dt2
            dtb_ref[rs, :] = dt2.astype(BF16)
            sums = [sums[0] + (0.5 / d) * jnp.sum(e * e, axis=0, keepdims=True),
                    sums[1] + jnp.sum(dy * xh2, axis=0, keepdims=True), sums[2] + jnp.sum(dy, axis=0, keepdims=True)]
        for k in range(3):
            acc_ref[k:k + 1, :] += sums[k]

    row = lambda n: pl.BlockSpec((tm, n), lambda i: (i, 0))
    vec = pl.BlockSpec((1, d), lambda i: (0, 0))
    return pl.pallas_call(
        body, name="ffn2_ln2_loss",
        out_shape=[jax.ShapeDtypeStruct((r, d), F32), jax.ShapeDtypeStruct((r, d), BF16),
                   jax.ShapeDtypeStruct((8, d), F32)],
        grid=(r // tm,),
        in_specs=[row(D_FF), pl.BlockSpec((D_FF, d), lambda i: (0, 0)), row(d), vec, vec, vec, vec, ANY],
        out_specs=[row(d), row(d), pl.BlockSpec((8, d), lambda i: (0, 0))],
        scratch_shapes=[pltpu.VMEM((tm, d), F32), pltpu.SemaphoreType.DMA],
        compiler_params=_params(("arbitrary",)),
    )(ra, w2, xhat1, g1, b1, g2, b2, tgt)


def _ffn_bwd_da(dt2b, w2, ra):
    r, d = dt2b.shape
    tm = _row_tile(r)

    def body(g_ref, w_ref, ra_ref, o_ref):
        o_ref[...] = (_nt(g_ref[...], w_ref[...]) * (2.0 * ra_ref[...].astype(F32))).astype(BF16)

    return pl.pallas_call(
        body, name="ffn_bwd_da", out_shape=jax.ShapeDtypeStruct((r, D_FF), BF16),
        grid=(r // tm,),
        in_specs=[pl.BlockSpec((tm, d), lambda i: (i, 0)), pl.BlockSpec((D_FF, d), lambda i: (0, 0)),
                  pl.BlockSpec((tm, D_FF), lambda i: (i, 0))],
        out_specs=pl.BlockSpec((tm, D_FF), lambda i: (i, 0)),
        compiler_params=_params(("parallel",)),
    )(dt2b, w2, ra)


def _ffn_bwd_ln1(da, w1t, dt2, xhat1, rstd1, g1):
    r, d = dt2.shape
    tm = _row_tile(r)

    def body(da_ref, w_ref, dt2_ref, xh_ref, rstd_ref, g_ref, dt_ref, dtb_ref, acc_ref):
        @pl.when(pl.program_id(0) == 0)
        def _():
            acc_ref[...] = jnp.zeros_like(acc_ref)

        sums = [jnp.zeros((1, d), F32)] * 2
        for rs in _sub_rows(tm):
            ds1 = ALPHA * dt2_ref[rs, :] + _nn(da_ref[rs, :], w_ref[...])
            xh = xh_ref[rs, :]
            dt1 = _ln_bwd(ds1 * g_ref[...], xh, rstd_ref[rs, :])
            dt_ref[rs, :] = dt1
            dtb_ref[rs, :] = dt1.astype(BF16)
            sums = [sums[0] + jnp.sum(ds1 * xh, axis=0, keepdims=True), sums[1] + jnp.sum(ds1, axis=0, keepdims=True)]
        for k in range(2):
            acc_ref[k:k + 1, :] += sums[k]

    row = lambda n: pl.BlockSpec((tm, n), lambda i: (i, 0))
    return pl.pallas_call(
        body, name="ffn_bwd_ln1",
        out_shape=[jax.ShapeDtypeStruct((r, d), F32), jax.ShapeDtypeStruct((r, d), BF16),
                   jax.ShapeDtypeStruct((8, d), F32)],
        grid=(r // tm,),
        in_specs=[row(D_FF), pl.BlockSpec((D_FF, d), lambda i: (0, 0)), row(d), row(d), row(1),
                  pl.BlockSpec((1, d), lambda i: (0, 0))],
        out_specs=[row(d), row(d), pl.BlockSpec((8, d), lambda i: (0, 0))],
        compiler_params=_params(("arbitrary",)),
    )(da, w1t, dt2, xhat1, rstd1, g1)


def _matmul_tn(lhs, rhs, bm, square_lhs=False, name="matmul_tn", fuse=NO_FUSE):
    r, m = lhs.shape
    n = rhs.shape[1]
    tk = _reduce_tile(r, True)

    def body(a_ref, b_ref, o_ref):
        @pl.when(pl.program_id(1) == 0)
        def _():
            o_ref[...] = jnp.zeros_like(o_ref)

        a = a_ref[...]
        if square_lhs:
            a = a * a
        o_ref[...] += _tn(a, b_ref[...])

    (out,), got = _fused_call(
        body, fuse, name=name, out_shape=[jax.ShapeDtypeStruct((m, n), F32)], grid=(m // bm, r // tk),
        in_specs=[pl.BlockSpec((tk, bm), lambda i, k: (k, i)), pl.BlockSpec((tk, n), lambda i, k: (k, 0))],
        out_specs=[pl.BlockSpec((bm, n), lambda i, k: (i, 0))], scratch_shapes=[], operands=(lhs, rhs))
    return out, got


def _grad_w_out(conv_of, gla_of, dt1b):
    r, n = dt1b.shape
    tk = _reduce_tile(r, True)

    def body(a_ref, b_ref, g_ref, o_ref):
        @pl.when(pl.program_id(1) == 0)
        def _():
            o_ref[...] = jnp.zeros_like(o_ref)

        @pl.when(pl.program_id(0) == 0)
        def _():
            o_ref[...] += _tn(a_ref[...], g_ref[...])

        @pl.when(pl.program_id(0) == 1)
        def _():
            o_ref[...] += _tn(b_ref[...], g_ref[...])

    lhs = pl.BlockSpec((tk, 512), lambda i, k: (k, 0))
    return pl.pallas_call(
        body, name="grad_w_out", out_shape=jax.ShapeDtypeStruct((2 * 512, n), F32), grid=(2, r // tk),
        in_specs=[lhs, lhs, pl.BlockSpec((tk, n), lambda i, k: (k, 0))],
        out_specs=pl.BlockSpec((512, n), lambda i, k: (i, 0)),
        compiler_params=_params(("parallel", "arbitrary")),
    )(conv_of, gla_of, dt1b)


def _out_proj_bwd(dt1b, w_out, hc, cg, cb, fuse=NO_FUSE):
    r, d = dt1b.shape
    tm = _row_tile(r)

    def body(g_ref, w_ref, hc_ref, cg_ref, cb_ref, dhc_ref, dgla_ref, acc_ref):
        @pl.when(pl.program_id(0) == 0)
        def _():
            acc_ref[...] = jnp.zeros_like(acc_ref)

        gg = cg_ref[...]
        sums = [jnp.zeros((1, D_CONV), F32)] * 3
        for rs in _sub_rows(tm):
            dmix = _nt(g_ref[rs, :], w_ref[...])
            dgla_ref[rs, :] = dmix[:, D_CONV:]
            xh, rstd = _ln_stats(hc_ref[rs, :])
            nv = xh * gg + cb_ref[...]
            sig = _sigmoid(nv)
            dn = dmix[:, :D_CONV] * (sig * (1.0 + nv * (1.0 - sig)))
            dhc = _ln_bwd(dn * gg, xh, rstd)
            dhc_ref[rs, :] = dhc
            sums = [sums[0] + jnp.sum(dhc, axis=0, keepdims=True), sums[1] + jnp.sum(dn * xh, axis=0, keepdims=True),
                    sums[2] + jnp.sum(dn, axis=0, keepdims=True)]
        for k in range(3):
            acc_ref[k:k + 1, :] += sums[k]

    row = lambda n: pl.BlockSpec((tm, n), lambda i: (i, 0))
    vec = pl.BlockSpec((1, D_CONV), lambda i: (0, 0))
    res, got = _fused_call(
        body, fuse, name="out_proj_bwd",
        out_shape=[jax.ShapeDtypeStruct((r, D_CONV), F32), jax.ShapeDtypeStruct((r, 512), F32),
                   jax.ShapeDtypeStruct((8, D_CONV), F32)],
        grid=(r // tm,),
        in_specs=[row(d), pl.BlockSpec((d, d), lambda i: (0, 0)), row(D_CONV), vec, vec],
        out_specs=[row(D_CONV), row(512), pl.BlockSpec((8, D_CONV), lambda i: (0, 0))], scratch_shapes=[],
        operands=(dt1b, w_out, hc, cg, cb))
    return res[0], res[1], res[2], got


def _conv_bwd(dhc, u, conv_w, fuse=NO_FUSE):
    bsz, tp, _ = u.shape
    nchunk = tp // CHUNK
    win = CHUNK + 32
    nct = D_CONV // 128

    def body(dhc_ref, cv_ref, cg_ref, w_ref, dv_ref, dg_ref, dw_ref, h_scr, dhc_scr, hwin, dwin, dw_scr):
        h_scr[0:32, :] = jnp.zeros((32, 128), F32)
        h_scr[32:32 + tp, :] = cv_ref[0] * _sigmoid(cg_ref[0])
        dhc_scr[0:tp, :] = dhc_ref[0]
        dhc_scr[tp:tp + 32, :] = jnp.zeros((32, 128), F32)
        dw_scr[...] = jnp.zeros_like(dw_scr)

        def step(n, carry):
            r0 = pl.multiple_of(n * CHUNK, CHUNK)
            rows = pl.ds(r0, CHUNK)
            hwin[...] = h_scr[pl.ds(r0, win), :]
            dwin[...] = dhc_scr[pl.ds(r0, win), :]
            dcur = dwin[0:CHUNK, :]
            acc = jnp.zeros((CHUNK, 128), F32)
            for j in range(CONV_WIDTH):
                acc = acc + w_ref[j:j + 1, :] * dwin[30 - j:30 - j + CHUNK, :]
                prod = dcur * hwin[2 + j:2 + j + CHUNK, :]
                dw_scr[j * 8:(j + 1) * 8, :] += jnp.sum(prod.reshape(CHUNK // 8, 8, 128), axis=0)
            cg = cg_ref[0, rows, :]
            sig = _sigmoid(cg)
            rowid = n * CHUNK + lax.broadcasted_iota(jnp.int32, (CHUNK, 1), 0)
            dh = jnp.where(rowid >= PAD, acc, 0.0)
            dv_ref[0, rows, :] = (dh * sig).astype(BF16)
            dg_ref[0, rows, :] = (dh * cv_ref[0, rows, :] * sig * (1.0 - sig)).astype(BF16)
            return carry

        lax.fori_loop(0, nchunk, step, 0)
        dw_ref[0] = jnp.zeros((32, 128), F32)
        for j in range(CONV_WIDTH):
            dw_ref[0, j:j + 1, :] = jnp.sum(dw_scr[j * 8:(j + 1) * 8, :], axis=0, keepdims=True)

    blk = lambda off: pl.BlockSpec((1, tp, 128), lambda bi, t: (bi, 0, off // 128 + t))
    res, got = _fused_call(
        body, fuse, name="conv_bwd",
        out_shape=[jax.ShapeDtypeStruct((bsz, tp, D_CONV), BF16), jax.ShapeDtypeStruct((bsz, tp, D_CONV), BF16),
                   jax.ShapeDtypeStruct((bsz, 32, D_CONV), F32)],
        grid=(bsz, nct),
        in_specs=[blk(0), blk(C_VAL), blk(C_GATE), pl.BlockSpec((32, 128), lambda bi, t: (0, t))],
        out_specs=[blk(0), blk(0), pl.BlockSpec((1, 32, 128), lambda bi, t: (bi, 0, t))],
        scratch_shapes=[pltpu.VMEM((tp + 32, 128), F32), pltpu.VMEM((tp + 32, 128), F32),
                        pltpu.VMEM((win, 128), F32), pltpu.VMEM((win, 128), F32),
                        pltpu.VMEM((CONV_WIDTH * 8, 128), F32)],
        operands=(dhc, u, u, conv_w))
    return res[0], res[1], res[2], got


def _gla_bwd(dgla, u, o_pre, states, gup, gbias, gnorm, fuse=NO_FUSE):
    bsz, tp, _ = u.shape
    nchunk = tp // CHUNK
    nb = _gla_group(nchunk)

    def body(dy_ref, q_ref, k_ref, v_ref, r_ref, gd_ref, o_ref, st_ref, gup_ref, gb_ref, gn_ref,
             dq_ref, dk_ref, dv_ref, dr_ref, dgd_ref, dgup_ref, vec_ref, h_scr, gup_acc):
        tril, triu, hmask = _gla_consts(nb)
        h_scr[...] = jnp.zeros_like(h_scr)
        gup_acc[...] = jnp.zeros_like(gup_acc)
        gn = gn_ref[...]
        gupb = gup_ref[...]
        m = nb * CHUNK
        ngroup = nchunk // nb

        def group(i, carry):
            dbias, dgn = carry
            g = ngroup - 1 - i
            rows, valid, z, eb, enb, erest, dec, qe, ke, kd = _gla_group_terms(
                g, nb, q_ref, k_ref, gd_ref, gup_ref, gb_ref, tril)
            keb, kdb = ke.astype(BF16), kd.astype(BF16)
            dqe = jnp.zeros((nb, CHUNK, 128), F32)
            dke = jnp.zeros((nb, CHUNK, 128), F32)
            dkd = jnp.zeros((nb, CHUNK, 128), F32)
            ddec = jnp.zeros((nb, 1, 128), F32)
            for h in range(2):
                cols = slice(h * GLA_DV, (h + 1) * GLA_DV)
                o = o_ref[0, rows, cols]
                rh = r_ref[0, rows, cols]
                dy = dy_ref[0, rows, cols]
                rms = lax.rsqrt(jnp.mean(o * o, axis=-1, keepdims=True) + LN_EPS)
                nrm = o * rms
                sig = _sigmoid(rh)
                sw = rh * sig
                dr_ref[0, rows, cols] = (dy * nrm * gn * (sig * (1.0 + rh * (1.0 - sig)))).astype(BF16)
                dgn = dgn + jnp.sum(dy * nrm * sw, axis=0, keepdims=True)
                dn = dy * gn * sw
                do = rms * (dn - nrm * jnp.mean(dn * nrm, axis=-1, keepdims=True))
                dob = do.astype(BF16).reshape(nb, CHUNK, GLA_DV)
                qh = jnp.where(hmask[h], qe, 0.0).astype(BF16)
                vh = v_ref[0, rows, cols].astype(BF16).reshape(nb, CHUNK, GLA_DV)
                ht = h_scr[h]
                hts = [None] * nb
                for n in reversed(range(nb)):
                    hts[n] = ht
                    ht = dec[n] * ht + _tn(dob[n], qh[n])
                h_scr[h] = ht
                htf = jnp.stack(hts)
                htb = htf.astype(BF16)
                st = st_ref[0, h, pl.ds(g * nb, nb)]
                at = jnp.where(triu, _bnt(keb, qh), 0.0).astype(BF16)
                da = jnp.where(tril, _bnt(dob, vh), 0.0).astype(BF16)
                dat = jnp.where(triu, _bnt(vh, dob), 0.0).astype(BF16)
                dqe = dqe + jnp.where(hmask[h], _bnn(da, keb) + _bnn(dob, st.astype(BF16)), 0.0)
                dke = dke + _bnn(dat, qh)
                dv_ref[0, rows, cols] = (_bnn(at, dob) + _bnt(kdb, htb)).reshape(m, GLA_DV).astype(BF16)
                dkd = dkd + jnp.where(hmask[h], _bnn(vh, htb), 0.0)
                ddec = ddec + jnp.where(hmask[h], jnp.sum(htf * st, axis=1, keepdims=True), 0.0)
            dq_ref[0, rows, :] = (dqe * eb * Q_SCALE).reshape(m, 128).astype(BF16)
            dk_ref[0, rows, :] = (dke * enb + dkd * erest).reshape(m, 128).astype(BF16)
            db = dqe * qe - dke * ke - dkd * kd
            dblast = jnp.sum(dkd * kd, axis=1, keepdims=True) + ddec * dec
            lastrow = lax.broadcasted_iota(jnp.int32, (1, CHUNK, 1), 1) == CHUNK - 1
            db = db + jnp.where(lastrow, dblast, 0.0)
            dlg = _bnn(triu.astype(F32), db, precision=lax.Precision.HIGHEST).reshape(m, 128)
            dz = jnp.where(valid, dlg, 0.0) * (1.0 / GLA_TAU) * (1.0 - _sigmoid(z))
            dzb = dz.astype(BF16)
            dgd_ref[0, 0, rows, :] = _nt(dzb, gupb).astype(BF16)
            gup_acc[...] += _tn(gd_ref[0, rows, :].astype(BF16), dzb)
            return dbias + jnp.sum(dz, axis=0, keepdims=True), dgn

        zero = jnp.zeros((1, 128), F32)
        dbias, dgn = lax.fori_loop(0, ngroup, group, (zero, zero))
        dgup_ref[0] = gup_acc[...]
        vec_ref[0] = jnp.zeros((8, 128), F32)
        vec_ref[0, 0:1, :] = dbias
        vec_ref[0, 1:2, :] = dgn

    pair = lambda w, off: pl.BlockSpec((1, tp, w), lambda bi, p: (bi, 0, off // w + p))
    return _fused_call(
        body, fuse, name="gla_bwd",
        out_shape=[jax.ShapeDtypeStruct((bsz, tp, 256), BF16), jax.ShapeDtypeStruct((bsz, tp, 256), BF16),
                   jax.ShapeDtypeStruct((bsz, tp, 512), BF16), jax.ShapeDtypeStruct((bsz, tp, 512), BF16),
                   jax.ShapeDtypeStruct((bsz, 2, tp, 128), BF16), jax.ShapeDtypeStruct((bsz, 128, 256), F32),
                   jax.ShapeDtypeStruct((bsz, 8, 256), F32)],
        grid=(bsz, 2),
        in_specs=[pair(256, 0), pair(128, C_Q), pair(128, C_K), pair(256, C_V), pair(256, C_R),
                  pl.BlockSpec((1, tp, 128), lambda bi, p: (bi, 0, C_GD // 128)),
                  pair(256, 0),
                  pl.BlockSpec((1, 2, nchunk, GLA_DV, 128), lambda bi, p: (bi, p, 0, 0, 0)),
                  pl.BlockSpec((128, 128), lambda bi, p: (0, p)),
                  pl.BlockSpec((1, 128), lambda bi, p: (0, p)),
                  pl.BlockSpec((1, 128), lambda bi, p: (0, 0))],
        out_specs=[pair(128, 0), pair(128, 0), pair(256, 0), pair(256, 0),
                   pl.BlockSpec((1, 1, tp, 128), lambda bi, p: (bi, p, 0, 0)),
                   pl.BlockSpec((1, 128, 128), lambda bi, p: (bi, 0, p)),
                   pl.BlockSpec((1, 8, 128), lambda bi, p: (bi, 0, p))],
        scratch_shapes=[pltpu.VMEM((2, GLA_DV, 128), F32), pltpu.VMEM((128, 128), F32)],
        operands=(dgla, u, u, u, u, u, o_pre, states, gup, gbias, gnorm))


_DU_OFFSETS = (C_VAL, C_GATE, C_Q, C_K, C_V, C_R)
_DU_WIDTHS = (512, 512, 256, 256, 512, 512)


def _du_specs(tm, per, row_map):
    specs = [pl.BlockSpec((tm, w), row_map) for w in _DU_WIDTHS]
    for p in range(2):
        specs.append(pl.BlockSpec((1, 1, tm, 128), lambda *ix, p=p: (row_map(*ix)[0] // per, p, row_map(*ix)[0] % per, 0)))
    return specs


def _du_pieces(refs):
    out = [(off, ref[...]) for off, ref in zip(_DU_OFFSETS, refs[:6])]
    dgd = (refs[6][0, 0].astype(F32) + refs[7][0, 0].astype(F32)).astype(BF16)
    out.append((C_GD, dgd))
    return out


def _in_proj_bwd(pieces, dgd, w_int, dt1, tp, fuse=NO_FUSE):
    r, d = dt1.shape
    tm = _row_tile(tp)
    per = tp // tm

    def body(*refs):
        w_ref, dt_ref, o_ref = refs[8:]
        acc = ALPHA * dt_ref[...]
        for off, val in _du_pieces(refs[:8]):
            acc = acc + _nn(val, w_ref[off:off + val.shape[1], :])
        o_ref[...] = acc

    row = lambda i: (i, 0)
    (ds0,), got = _fused_call(
        body, fuse, name="in_proj_bwd", out_shape=[jax.ShapeDtypeStruct((r, d), F32)], grid=(r // tm,),
        in_specs=_du_specs(tm, per, row) + [pl.BlockSpec((D_IN_PAD, d), lambda i: (0, 0)), pl.BlockSpec((tm, d), row)],
        out_specs=[pl.BlockSpec((tm, d), row)], scratch_shapes=[], operands=(*pieces, dgd, dgd, w_int, dt1))
    return ds0, got


def _grad_w_in(pieces, dgd, s0b, tp, fuse=NO_FUSE):
    r, d = s0b.shape
    tk = _reduce_tile(tp, False)
    per = tp // tk

    def body(*refs):
        s_ref, o_ref = refs[8:]

        @pl.when(pl.program_id(0) == 0)
        def _():
            o_ref[...] = jnp.zeros_like(o_ref)

        s = s_ref[...]
        for off, val in _du_pieces(refs[:8]):
            o_ref[off:off + val.shape[1], :] += _tn(val, s)

    row = lambda k: (k, 0)
    (out,), got = _fused_call(
        body, fuse, name="grad_w_in", out_shape=[jax.ShapeDtypeStruct((D_IN_PAD, d), F32)], grid=(r // tk,),
        in_specs=_du_specs(tk, per, row) + [pl.BlockSpec((tk, d), row)],
        out_specs=[pl.BlockSpec((D_IN_PAD, d), lambda k: (0, 0))], scratch_shapes=[],
        operands=(*pieces, dgd, dgd, s0b))
    return out, got


def _ln_in_bwd(ds0, x, meta, g):
    bsz, s, d = x.shape
    tp = s + HEAD
    nh = 2
    sh = s // nh
    rc = min(256, sh)

    def body(ds_ref, x_ref, meta_ref, g_ref, gx_ref, dm_ref, vec_ref):
        h = pl.program_id(1)
        gg = g_ref[...]

        @pl.when(h == 0)
        def _():
            mh, mr = _ln_stats(meta_ref[...])
            dsm = ds_ref[0, PAD:HEAD, :]
            dm_ref[0] = _ln_bwd(dsm * gg, mh, mr)
            vec_ref[0] = jnp.zeros((8, d), F32)
            vec_ref[0, 0:1, :] = jnp.sum(dsm * mh, axis=0, keepdims=True)
            vec_ref[0, 1:2, :] = jnp.sum(dsm, axis=0, keepdims=True)

        def step(i, carry):
            sg, sb = carry
            dst = pl.ds(pl.multiple_of(i * rc, rc), rc)
            src = pl.ds(pl.multiple_of(HEAD + h * sh + i * rc, 64), rc)
            xh, rstd = _ln_stats(x_ref[0, dst, :])
            dsv = ds_ref[0, src, :]
            gx_ref[0, dst, :] = _ln_bwd(dsv * gg, xh, rstd)
            return sg + jnp.sum(dsv * xh, axis=0, keepdims=True), sb + jnp.sum(dsv, axis=0, keepdims=True)

        zero = jnp.zeros((1, d), F32)
        sg, sb = lax.fori_loop(0, sh // rc, step, (zero, zero))
        vec_ref[0, 0:1, :] += sg
        vec_ref[0, 1:2, :] += sb

    return pl.pallas_call(
        body, name="ln_in_bwd",
        out_shape=[jax.ShapeDtypeStruct((bsz, s, d), F32), jax.ShapeDtypeStruct((bsz, N_META, d), F32),
                   jax.ShapeDtypeStruct((bsz, 8, d), F32)],
        grid=(bsz, nh),
        in_specs=[pl.BlockSpec((1, tp, d), lambda bi, hi: (bi, 0, 0)),
                  pl.BlockSpec((1, sh, d), lambda bi, hi: (bi, hi, 0)),
                  pl.BlockSpec((N_META, d), lambda bi, hi: (0, 0)),
                  pl.BlockSpec((1, d), lambda bi, hi: (0, 0))],
        out_specs=[pl.BlockSpec((1, sh, d), lambda bi, hi: (bi, hi, 0)),
                   pl.BlockSpec((1, N_META, d), lambda bi, hi: (bi, 0, 0)),
                   pl.BlockSpec((1, 8, d), lambda bi, hi: (bi, 0, 0))],
        compiler_params=_params(("parallel", "arbitrary")),
    )(ds0, x, meta, g)


def _rows128(a):
    return a.reshape(-1, 128)


def kernel(x, meta_tokens, ln_in_g, ln_in_b, w_in, conv_w, conv_b, conv_ln_g, conv_ln_b, gate_up, gate_bias, gla_norm_g, w_out, ln1_g, ln1_b, w_ff1, w_ff2, ln2_g, ln2_b, loss_target, m_meta_tokens, m_ln_in_g, m_ln_in_b, m_w_in, m_conv_w, m_conv_b, m_conv_ln_g, m_conv_ln_b, m_gate_up, m_gate_bias, m_gla_norm_g, m_w_out, m_ln1_g, m_ln1_b, m_w_ff1, m_w_ff2, m_ln2_g, m_ln2_b, v_meta_tokens, v_ln_in_g, v_ln_in_b, v_w_in, v_conv_w, v_conv_b, v_conv_ln_g, v_conv_ln_b, v_gate_up, v_gate_bias, v_gla_norm_g, v_w_out, v_ln1_g, v_ln1_b, v_w_ff1, v_w_ff2, v_ln2_g, v_ln2_b):
    bsz, seq, d = x.shape
    tp = seq + HEAD
    r = bsz * tp
    xi, yi, ci = _mesh_pos()
    chip = 2 * xi + yi
    c_arr = jnp.reshape(ci, (1,)).astype(jnp.int32)
    pos_arr = jnp.stack([chip, ci]).astype(jnp.int32)

    sh_in = D_IN // 4
    shard_in = jnp.pad(w_in[0].T.astype(BF16), ((0, D_IN_PAD // 4 - sh_in), (0, 0)))
    shard_w1t, shard_wout, shard_w2 = w_ff1[0].T.astype(BF16), w_out[0].astype(BF16), w_ff2[0].astype(BF16)
    small_w = jnp.concatenate([_rows128(meta_tokens), _rows128(conv_w[0]), _rows128(gate_up[0])], axis=0)
    (g_int,), g_small = _gather_weights([shard_in], small_w)
    w_int = jnp.concatenate([g_int[j, :sh_in] for j in range(4)] + [jnp.zeros((D_IN_PAD - D_IN, d), BF16)], axis=0)
    n_meta_rows, n_cw_rows = N_META * 256 // 128, CONV_WIDTH * 128 // 128
    meta_full = jnp.concatenate([g_small[j, :n_meta_rows].reshape(N_META, 256) for j in range(4)], axis=1)
    convw_full = jnp.concatenate(
        [g_small[j, n_meta_rows:n_meta_rows + n_cw_rows].reshape(CONV_WIDTH, 128) for j in range(4)], axis=1)
    gup_full = jnp.concatenate(
        [g_small[j, n_meta_rows + n_cw_rows:].reshape(GLA_RANK, 64) for j in range(4)], axis=1)
    convw_p = jnp.pad(convw_full, ((0, 1), (0, 0)))
    gup_p = jnp.pad(gup_full, ((0, 128 - GLA_RANK), (0, 0))).astype(BF16)
    ln_in_g2, ln_in_b2 = ln_in_g.reshape(1, d), ln_in_b.reshape(1, d)

    s0, s0b = _ln_in_fwd(x, meta_full, ln_in_g2, ln_in_b2)
    s0f, s0bf = s0.reshape(r, d), s0b.reshape(r, d)
    u, ((w1_buf,),) = _in_proj(s0bf, w_int, [("gather", [(shard_w1t, 0, 640, None)])])
    (w1_buf,) = _place_own([w1_buf], [shard_w1t])
    u3 = u.reshape(bsz, tp, D_IN_PAD)
    hc, ((w1_buf,),) = _conv_fwd(u3, convw_p, conv_b, [("gather", [(shard_w1t, 640, 384, w1_buf)])])
    hc = hc.reshape(r, D_CONV)
    gla_out, o_pre, states, ((g_wout, w2_buf),) = _gla_fwd(
        u3, gup_p, gate_bias, gla_norm_g, [("gather", [(shard_wout, 0, 256, None), (shard_w2, 0, 256, None)])])
    g_wout, w2_buf = _place_own([g_wout, w2_buf], [shard_wout, shard_w2])
    wout = g_wout.reshape(d, d)
    gla_of = gla_out.reshape(r, 512)
    conv_of, xhat1, rstd1, s1b, ((w2_buf,),) = _out_proj_ln1(
        hc, gla_of, wout, s0f, conv_ln_g, conv_ln_b, ln1_g, ln1_b, [("gather", [(shard_w2, 256, 384, w2_buf)])])
    w1t = w1_buf.reshape(D_FF, d)
    ra, ((w2_buf,),) = _ffn1(s1b, w1t, [("gather", [(shard_w2, 640, 384, w2_buf)])])
    w2 = w2_buf.reshape(D_FF, d)
    dt2, dt2b, acc2 = _ffn2_ln2_loss(ra, w2, xhat1, ln1_g, ln1_b, ln2_g, ln2_b, loss_target, tp)

    def add_pair(g, got, splits=None):
        return _add_pair(g, got, c_arr, splits)

    da = _ffn_bwd_da(dt2b, w2, ra)
    dt1, dt1b, acc1 = _ffn_bwd_ln1(da, w1t, dt2, xhat1, rstd1, ln1_g)
    g_w2, _ = _matmul_tn(ra, dt2b, 1024, square_lhs=True, name="grad_w_ff2")
    big_w2 = g_w2.reshape(4, D_FF // 4, d)
    g_w1t, ((pair_w2,),) = _matmul_tn(da, s1b, 1024, name="grad_w_ff1", fuse=[("pair", [big_w2])])
    big_w1 = g_w1t.reshape(4, D_FF // 4, d)
    dhc, dgla, cacc, ((pair_w1,),) = _out_proj_bwd(dt1b, wout, hc, conv_ln_g, conv_ln_b, [("pair", [big_w1])])
    (part_w1,), (part_w2a, part_w2b) = add_pair(big_w1, pair_w1), add_pair(big_w2, pair_w2, [512, 512])
    big_wout = _grad_w_out(conv_of, gla_of, dt1b).reshape(4, d // 4, d)
    dcv, dcg, dcw, ((chip_w1,), (pair_wout,)) = _conv_bwd(
        dhc.reshape(bsz, tp, D_CONV), u3, convw_p, [("exchange", [part_w1]), ("pair", [big_wout])])
    (part_wout,) = add_pair(big_wout, pair_wout)
    (dq, dk, dv, dr, dgd, dgup, gvec), ((chip_w2a,),) = _gla_bwd(
        dgla.reshape(bsz, tp, 512), u3, o_pre, states, gup_p, gate_bias, gla_norm_g, [("exchange", [part_w2a])])
    pieces = [a.reshape(r, a.shape[-1]) for a in (dcv, dcg, dq, dk, dv, dr)]
    g_wint, ((chip_w2b, chip_wout),) = _grad_w_in(
        pieces, dgd, s0bf, tp, [("exchange", [part_w2b, part_wout])])
    big_win = jnp.stack([g_wint[j * sh_in:(j + 1) * sh_in] for j in range(4)])
    (pair_win,) = _pair_exchange([big_win])
    ds0, ((chip_win,),) = _in_proj_bwd(pieces, dgd, w_int, dt1, tp, [("exchange", add_pair(big_win, pair_win))])
    grad_x, dmeta, lvec = _ln_in_bwd(ds0.reshape(bsz, tp, d), x, meta_full, ln_in_g2)

    small = {
        "loss": acc2[0:1], "ln_in_g": jnp.sum(lvec[:, 0], axis=0), "ln_in_b": jnp.sum(lvec[:, 1], axis=0),
        "conv_b": cacc[0], "conv_ln_g": cacc[1], "conv_ln_b": cacc[2], "gate_bias": jnp.sum(gvec[:, 0], axis=0),
        "gla_norm_g": jnp.sum(gvec[:, 1].reshape(bsz * 2, 128), axis=0),
        "ln1_g": acc1[0], "ln1_b": acc1[1], "ln2_g": acc2[1], "ln2_b": acc2[2],
        "conv_w": jnp.sum(dcw, axis=0)[:CONV_WIDTH], "gate_up": jnp.sum(dgup, axis=0)[:GLA_RANK],
        "meta_tokens": jnp.sum(dmeta, axis=0),
    }
    pack = jnp.concatenate([_rows128(small[k]) for k, _ in _SMALL_FIELDS], axis=0)
    pack = jnp.pad(pack, ((0, SMALL_ROWS - pack.shape[0]), (0, 0)))
    red = _sum_small(_allgather_small(pack))
    off, tot = {}, 0
    for k, nrows in _SMALL_FIELDS:
        off[k] = (tot, nrows)
        tot += nrows

    def field(k, shape):
        o, nrows = off[k]
        return red[o:o + nrows].reshape(shape)

    loss = jnp.sum(field("loss", (d,)))

    halves = [_add_chips(g, t, s, pos_arr) for g, t, s in (
        (big_win, pair_win, [chip_win]), (big_wout, pair_wout, [chip_wout]), (big_w1, pair_w1, [chip_w1]),
        (big_w2, pair_w2, [chip_w2a, chip_w2b]))]
    f_wint, f_wout, f_w1t, f_w2 = [
        jnp.concatenate([jnp.where(ci == 0, own, sib), jnp.where(ci == 0, sib, own)], axis=1)
        for own, sib in zip(halves, _pair_swap(halves))]

    def cols(a, width):
        return lax.dynamic_slice_in_dim(a, chip * width, width, axis=a.ndim - 1)

    grads = {
        "meta_tokens": cols(field("meta_tokens", (N_META, d)), 256),
        "ln_in_g": field("ln_in_g", (d,)), "ln_in_b": field("ln_in_b", (d,)),
        "w_in": f_wint.T[None], "conv_w": cols(field("conv_w", (CONV_WIDTH, D_CONV)), 128)[None],
        "conv_b": field("conv_b", (1, D_CONV)), "conv_ln_g": field("conv_ln_g", (1, D_CONV)),
        "conv_ln_b": field("conv_ln_b", (1, D_CONV)),
        "gate_up": cols(field("gate_up", (GLA_RANK, 256)), 64)[None],
        "gate_bias": field("gate_bias", (1, 256)), "gla_norm_g": field("gla_norm_g", (1, 128)),
        "w_out": f_wout[None], "ln1_g": field("ln1_g", (1, d)), "ln1_b": field("ln1_b", (1, d)),
        "w_ff1": f_w1t.T[None], "w_ff2": f_w2[None],
        "ln2_g": field("ln2_g", (1, d)), "ln2_b": field("ln2_b", (1, d)),
    }
    weights = dict(meta_tokens=meta_tokens, ln_in_g=ln_in_g, ln_in_b=ln_in_b, w_in=w_in, conv_w=conv_w, conv_b=conv_b,
                   conv_ln_g=conv_ln_g, conv_ln_b=conv_ln_b, gate_up=gate_up, gate_bias=gate_bias,
                   gla_norm_g=gla_norm_g, w_out=w_out, ln1_g=ln1_g, ln1_b=ln1_b, w_ff1=w_ff1, w_ff2=w_ff2,
                   ln2_g=ln2_g, ln2_b=ln2_b)
    moms = dict(meta_tokens=(m_meta_tokens, v_meta_tokens), ln_in_g=(m_ln_in_g, v_ln_in_g),
                ln_in_b=(m_ln_in_b, v_ln_in_b), w_in=(m_w_in, v_w_in), conv_w=(m_conv_w, v_conv_w),
                conv_b=(m_conv_b, v_conv_b), conv_ln_g=(m_conv_ln_g, v_conv_ln_g),
                conv_ln_b=(m_conv_ln_b, v_conv_ln_b), gate_up=(m_gate_up, v_gate_up),
                gate_bias=(m_gate_bias, v_gate_bias), gla_norm_g=(m_gla_norm_g, v_gla_norm_g),
                w_out=(m_w_out, v_w_out), ln1_g=(m_ln1_g, v_ln1_g), ln1_b=(m_ln1_b, v_ln1_b),
                w_ff1=(m_w_ff1, v_w_ff1), w_ff2=(m_w_ff2, v_w_ff2), ln2_g=(m_ln2_g, v_ln2_g),
                ln2_b=(m_ln2_b, v_ln2_b))
    names = list(weights)
    big_names = ("w_in", "w_out", "w_ff1", "w_ff2")
    delta, new_m, new_v = {}, {}, {}
    lin = lambda a: a.reshape(-1, 128)
    unlin = lambda a: a.reshape(sh_in, d).T[None]
    dl, mn, vn = _adamw(lin(w_in[0].T), lin(f_wint), lin(m_w_in[0].T), lin(v_w_in[0].T))
    delta["w_in"], new_m["w_in"], new_v["w_in"] = unlin(dl), unlin(mn), unlin(vn)
    for k in big_names[1:]:
        shp = weights[k].shape
        two = lambda a: a.reshape(shp[-2], shp[-1])
        dl, mn, vn = _adamw(two(weights[k]), two(grads[k]), two(moms[k][0]), two(moms[k][1]))
        delta[k], new_m[k], new_v[k] = dl.reshape(shp), mn.reshape(shp), vn.reshape(shp)
    small_names = [k for k in names if k not in big_names]
    sizes = [weights[k].size for k in small_names]
    total = sum(sizes)
    padded = -(-total // 1024) * 1024

    def flat(get):
        v = jnp.concatenate([get(k).reshape(-1) for k in small_names])
        return jnp.pad(v, (0, padded - total)).reshape(padded // 128, 128)

    ones = jnp.ones((padded - total,), F32)
    vflat = jnp.concatenate([jnp.concatenate([moms[k][1].reshape(-1) for k in small_names]), ones])
    dl, mn, vn = _adamw(flat(lambda k: weights[k]), flat(lambda k: grads[k]), flat(lambda k: moms[k][0]),
                        vflat.reshape(padded // 128, 128))
    pos = 0
    for k, sz in zip(small_names, sizes):
        shp = weights[k].shape
        delta[k] = dl.reshape(-1)[pos:pos + sz].reshape(shp)
        new_m[k] = mn.reshape(-1)[pos:pos + sz].reshape(shp)
        new_v[k] = vn.reshape(-1)[pos:pos + sz].reshape(shp)
        pos += sz

    return (loss, grad_x, *[grads[k] for k in names], *[delta[k] for k in names],
            *[new_m[k] for k in names], *[new_v[k] for k in names])
```

```python
import functools

import jax
import jax.numpy as jnp
from jax import lax
from jax.experimental import pallas as pl
from jax.experimental.pallas import tpu as pltpu

F32 = jnp.float32
BF16 = jnp.bfloat16

D_MODEL = 1024
N_META = 16
D_CONV = 512
CONV_WIDTH = 31
GLA_HEADS = 4
GLA_DV = 128
GLA_DK = 64
GLA_RANK = 16
GLA_TAU = 16.0
CHUNK = 64
D_FF = 4096
LN_EPS = 1e-5
ALPHA = 2.0 ** 0.25
D_IN = 2576
D_IN_PAD = 2688
PAD = CHUNK - N_META
HEAD = PAD + N_META
Q_SCALE = GLA_DK ** -0.5
ADAM_LR, ADAM_B1, ADAM_B2, ADAM_EPS, ADAM_WD, ADAM_STEP = 0.001, 0.9, 0.999, 1e-08, 0.01, 10
HALF = D_MODEL // 2
VMEM_LIMIT = 56 * 1024 * 1024
MESH = pl.DeviceIdType.MESH

C_VAL, C_GATE, C_Q, C_K, C_V, C_R, C_GD = 0, 512, 1024, 1280, 1536, 2048, 2560

_SMALL_FIELDS = (("loss", 8), ("ln_in_g", 8), ("ln_in_b", 8), ("conv_b", 4), ("conv_ln_g", 4), ("conv_ln_b", 4),
                 ("gate_bias", 2), ("gla_norm_g", 1), ("ln1_g", 8), ("ln1_b", 8), ("ln2_g", 8), ("ln2_b", 8),
                 ("conv_w", 124), ("gate_up", 32), ("meta_tokens", 128))
SMALL_ROWS = 360


def _params(sem=None, **kw):
    return pltpu.CompilerParams(dimension_semantics=sem, vmem_limit_bytes=VMEM_LIMIT, **kw)


def _row_tile(tp):
    for t in (704, 352, 192, 64):
        if tp % t == 0:
            return t
    raise ValueError(tp)


def _reduce_tile(tp, big):
    for t in ((2112, 1056, 704) if big else (1056, 704)) + (352, 192, 64):
        if tp % t == 0:
            return t
    raise ValueError(tp)


def _sub_rows(tm):
    return [slice(0, tm)]


def _dot(a, b, dims, precision=None):
    return lax.dot_general(a, b, (dims, ((), ())), preferred_element_type=F32, precision=precision)


def _nn(a, b, **kw):
    return _dot(a, b, ((1,), (0,)), **kw)


def _nt(a, b, **kw):
    return _dot(a, b, ((1,), (1,)), **kw)


def _tn(a, b, **kw):
    return _dot(a, b, ((0,), (0,)), **kw)


def _sigmoid(x):
    return 1.0 / (1.0 + jnp.exp(-x))


def _log_sigmoid(z):
    return jnp.minimum(z, 0.0) - jnp.log(1.0 + jnp.exp(-jnp.abs(z)))


def _ln_stats(t):
    mu = jnp.mean(t, axis=-1, keepdims=True)
    d = t - mu
    var = jnp.mean(d * d, axis=-1, keepdims=True)
    rstd = lax.rsqrt(var + LN_EPS)
    return d * rstd, rstd


def _ln_bwd(dxhat, xhat, rstd):
    m1 = jnp.mean(dxhat, axis=-1, keepdims=True)
    m2 = jnp.mean(dxhat * xhat, axis=-1, keepdims=True)
    return rstd * (dxhat - m1 - xhat * m2)


def _mesh_pos():
    return lax.axis_index("x"), lax.axis_index("y"), lax.axis_index("c")


ANY = pl.BlockSpec(memory_space=pl.ANY)


def _gather_sems(n):
    return [pltpu.SemaphoreType.DMA((n, 3))] * 4


def _gather_steps(ins, outs, sems, ranges=None):
    n = len(ins)
    send, recv, fsend, frecv = sems
    ranges = ranges or [(0, ref.shape[0]) for ref in ins]
    x, y, c = _mesh_pos()
    me = 2 * x + y
    sibling = (x, y, 1 - c)
    chips = [(1 - x, y), (x, 1 - y), (1 - x, 1 - y)]
    chip_idx = [2 * px + py for px, py in chips]
    mine = [pl.ds(pl.multiple_of(r0 + c * (nr // 2), 16), nr // 2) for r0, nr in ranges]
    other = [pl.ds(pl.multiple_of(r0 + (1 - c) * (nr // 2), 16), nr // 2) for r0, nr in ranges]
    pairs = [(a, k) for a in range(n) for k in range(3)]

    def ici(a, k, slab):
        return pltpu.make_async_remote_copy(
            src_ref=ins[a].at[mine[a], :], dst_ref=outs[a].at[slab, mine[a], :],
            send_sem=send.at[a, k], recv_sem=recv.at[a, k], device_id=(*chips[k], c), device_id_type=MESH)

    def forward(a, k, rows):
        blk = outs[a].at[chip_idx[k], rows[a], :]
        return pltpu.make_async_remote_copy(
            src_ref=blk, dst_ref=blk, send_sem=fsend.at[a, k], recv_sem=frecv.at[a, k],
            device_id=sibling, device_id_type=MESH)

    def start():
        for a, k in pairs:
            ici(a, k, me).start()

    def relay():
        for a, k in pairs:
            ici(a, k, chip_idx[k]).wait_recv()
            forward(a, k, mine).start()

    def finish():
        for a, k in pairs:
            forward(a, k, other).wait_recv()
        for a, k in pairs:
            ici(a, k, me).wait_send()
            forward(a, k, mine).wait_send()

    return start, relay, finish


def _place_own(gathered, shards):
    chip = 2 * lax.axis_index("x") + lax.axis_index("y")
    return [lax.dynamic_update_slice(g, s[None], (chip, 0, 0)) for g, s in zip(gathered, shards)]


def _gather_small(small):
    def body(small_in, small_out, ssend, srecv):
        x, y, c = _mesh_pos()
        chips = [(1 - x, y), (x, 1 - y), (1 - x, 1 - y)]

        def small_copy(k, slot):
            return pltpu.make_async_remote_copy(
                src_ref=small_in, dst_ref=small_out.at[slot], send_sem=ssend.at[k], recv_sem=srecv.at[k],
                device_id=(*chips[k], c), device_id_type=MESH)

        for k in range(3):
            small_copy(k, 2 * x + y).start()
        for k, (px, py) in enumerate(chips):
            small_copy(k, 2 * px + py).wait_recv()
        for k in range(3):
            small_copy(k, 2 * x + y).wait_send()

    res = pl.pallas_call(
        body, name="gather_small", out_shape=jax.ShapeDtypeStruct((4,) + small.shape, small.dtype),
        in_specs=[ANY], out_specs=ANY,
        scratch_shapes=[pltpu.SemaphoreType.DMA((3,)), pltpu.SemaphoreType.DMA((3,))],
    )(small)
    return _place_own([res], [small])[0]


def _pair_swap_steps(ins, outs, sems):
    send, recv = sems
    x, y, c = _mesh_pos()
    cps = [pltpu.make_async_remote_copy(
        src_ref=ins[a], dst_ref=outs[a], send_sem=send.at[a], recv_sem=recv.at[a],
        device_id=(x, y, 1 - c), device_id_type=MESH) for a in range(len(ins))]

    def start():
        for cp in cps:
            cp.start()

    def finish():
        for cp in cps:
            cp.wait()

    return start, finish


def _pair_exchange_steps(ins, outs, sems):
    send, recv = sems
    x, y, c = _mesh_pos()
    other = pl.ds(pl.multiple_of((1 - c) * HALF, 128), HALF)
    cps = [pltpu.make_async_remote_copy(
        src_ref=ins[a].at[:, :, other], dst_ref=outs[a], send_sem=send.at[a], recv_sem=recv.at[a],
        device_id=(x, y, 1 - c), device_id_type=MESH) for a in range(len(ins))]

    def start():
        for cp in cps:
            cp.start()

    def finish():
        for cp in cps:
            cp.wait()

    return start, finish


def _pair_exchange_shapes(grads):
    return [jax.ShapeDtypeStruct(g.shape[:2] + (HALF,), g.dtype) for g in grads]


def _pair_exchange(grads):
    n = len(grads)

    def body(*refs):
        start, finish = _pair_exchange_steps(refs[:n], refs[n:2 * n], refs[2 * n:])
        start()
        finish()

    return pl.pallas_call(
        body, name="grad_pair_exchange", out_shape=_pair_exchange_shapes(grads),
        in_specs=[ANY] * n, out_specs=[ANY] * n,
        scratch_shapes=[pltpu.SemaphoreType.DMA((n,)), pltpu.SemaphoreType.DMA((n,))],
    )(*grads)


NO_FUSE = ()


def _fuse_plan(kind, items):
    n = len(items)
    if kind == "gather":
        shards = [it[0] for it in items]
        bufs = [it[3] for it in items if it[3] is not None]
        alias, b = {}, 0
        for a, it in enumerate(items):
            if it[3] is not None:
                alias[n + b] = a
                b += 1
        ranges = [(it[1], it[2]) for it in items]
        return (shards + bufs, [jax.ShapeDtypeStruct((4,) + s.shape, s.dtype) for s in shards], alias, _gather_sems(n),
                lambda i, o, s: _gather_steps(i[:n], o, s, ranges))
    if kind == "exchange":
        return list(items), _chip_exchange_shapes(items), {}, _chip_exchange_sems(n), _chip_exchange_steps
    pair_sems = [pltpu.SemaphoreType.DMA((n,)), pltpu.SemaphoreType.DMA((n,))]
    if kind == "swap":
        return (list(items), [jax.ShapeDtypeStruct(h.shape, h.dtype) for h in items], {}, pair_sems, _pair_swap_steps)
    return list(items), _pair_exchange_shapes(items), {}, pair_sems, _pair_exchange_steps


def _fused_call(body, fuse, *, name, grid, in_specs, out_specs, out_shape, scratch_shapes, operands):
    plans = [_fuse_plan(kind, list(items)) for kind, items in fuse if len(items)]
    n_in, n_out, n_s = len(in_specs), len(out_shape), len(scratch_shapes)
    comm = [a for p in plans for a in p[0]]
    shapes = [s for p in plans for s in p[1]]
    nc, no = len(comm), len(shapes)
    aliases, i_at, o_at = {}, n_in, n_out
    for p in plans:
        for i, o in p[2].items():
            aliases[i_at + i] = o_at + o
        i_at, o_at = i_at + len(p[0]), o_at + len(p[1])

    def wrapped(*refs):
        o0 = n_in + nc
        s0 = o0 + n_out + no
        i_at, o_at, sem_at, steps = n_in, o0 + n_out, s0 + n_s, []
        for p in plans:
            steps.append(p[4](refs[i_at:i_at + len(p[0])], refs[o_at:o_at + len(p[1])], refs[sem_at:sem_at + len(p[3])]))
            i_at, o_at, sem_at = i_at + len(p[0]), o_at + len(p[1]), sem_at + len(p[3])
        first, last = _grid_ends(grid)
        for st in steps:
            pl.when(first)(st[0])
        for st in steps:
            for mid in st[1:-1]:
                pl.when(last)(mid)
        body(*refs[:n_in], *refs[o0:o0 + n_out], *refs[s0:s0 + n_s])
        for st in steps:
            pl.when(last)(st[-1])

    res = pl.pallas_call(
        wrapped if plans else body, name=name, grid=grid, in_specs=list(in_specs) + [ANY] * nc,
        out_specs=list(out_specs) + [ANY] * no, out_shape=list(out_shape) + shapes,
        scratch_shapes=list(scratch_shapes) + [s for p in plans for s in p[3]], input_output_aliases=aliases,
        compiler_params=_params(("arbitrary",) * len(grid)))(*operands, *comm)
    outs, got, results = list(res[:n_out]), list(res[n_out:]), []
    for p in plans:
        results.append(got[:len(p[1])])
        got = got[len(p[1]):]
    return outs, results


def _chip_exchange_sems(n):
    return [pltpu.SemaphoreType.DMA((n, 3))] * 2


def _chip_exchange_shapes(parts):
    return [jax.ShapeDtypeStruct((3,) + p.shape[1:], p.dtype) for p in parts]


def _chip_exchange_steps(ins, outs, sems):
    send, recv = sems
    x, y, c = _mesh_pos()
    chips = [(1 - x, y), (x, 1 - y), (1 - x, 1 - y)]
    cps = [pltpu.make_async_remote_copy(
        src_ref=ins[a].at[2 * px + py], dst_ref=outs[a].at[k], send_sem=send.at[a, k], recv_sem=recv.at[a, k],
        device_id=(px, py, c), device_id_type=MESH) for a in range(len(ins)) for k, (px, py) in enumerate(chips)]

    def start():
        for cp in cps:
            cp.start()

    def finish():
        for cp in cps:
            cp.wait()

    return start, finish


def _chip_exchange(parts):
    n = len(parts)

    def body(*refs):
        start, finish = _chip_exchange_steps(refs[:n], refs[n:2 * n], refs[2 * n:])
        start()
        finish()

    return pl.pallas_call(
        body, name="grad_chip_exchange", out_shape=_chip_exchange_shapes(parts),
        in_specs=[ANY] * n, out_specs=[ANY] * n, scratch_shapes=_chip_exchange_sems(n),
    )(*parts)


def _pair_swap(halves):
    n = len(halves)

    def body(*refs):
        start, finish = _pair_swap_steps(refs[:n], refs[n:2 * n], refs[2 * n:])
        start()
        finish()

    return pl.pallas_call(
        body, name="grad_pair_swap",
        out_shape=[jax.ShapeDtypeStruct(h.shape, h.dtype) for h in halves],
        in_specs=[ANY] * n, out_specs=[ANY] * n,
        scratch_shapes=[pltpu.SemaphoreType.DMA((n,)), pltpu.SemaphoreType.DMA((n,))],
    )(*halves)


def _allgather_small(pack):
    m_per, ncol = pack.shape

    def body(x_ref, out_ref, send_sems, recv_sems, local_sem):
        x, y, c = _mesh_pos()
        me, sibling = (x, y, c), (x, y, 1 - c)
        chips = [(1 - x, y), (x, 1 - y), (1 - x, 1 - y)]

        def rows(px, py, pc):
            return out_ref.at[pl.ds(pl.multiple_of((4 * px + 2 * py + pc) * m_per, 8), m_per), :]

        def copy(k, block, to, src=None):
            return pltpu.make_async_remote_copy(
                src_ref=rows(*block) if src is None else src, dst_ref=rows(*block),
                send_sem=send_sems.at[k], recv_sem=recv_sems.at[k], device_id=to, device_id_type=MESH)

        mine = pltpu.make_async_copy(x_ref, rows(*me), local_sem)
        mine.start()
        first = [copy(0, me, sibling, src=x_ref)]
        first += [copy(1 + j, me, (*chip, c), src=x_ref) for j, chip in enumerate(chips)]
        for cp in first:
            cp.start()
        passed = [copy(4 + j, (*chip, c), sibling) for j, chip in enumerate(chips)]
        for j, chip in enumerate(chips):
            copy(1 + j, (*chip, c), me).wait_recv()
            passed[j].start()
        copy(0, sibling, me).wait_recv()
        for j, chip in enumerate(chips):
            copy(4 + j, (*chip, 1 - c), me).wait_recv()
        for cp in first + passed:
            cp.wait_send()
        mine.wait()

    return pl.pallas_call(
        body, name="allgather_small",
        out_shape=jax.ShapeDtypeStruct((8 * m_per, ncol), pack.dtype),
        in_specs=[pl.BlockSpec(memory_space=pltpu.VMEM)],
        out_specs=pl.BlockSpec(memory_space=pltpu.VMEM),
        scratch_shapes=[pltpu.SemaphoreType.DMA((7,)), pltpu.SemaphoreType.DMA((7,)), pltpu.SemaphoreType.DMA],
    )(pack)


def _add_pair(g, got, c_arr, splits=None):
    _, rows, _ = g.shape
    splits = splits or [rows]

    def body(c_ref, g_ref, r_ref, *o_refs):
        at = 0
        for o_ref, n in zip(o_refs, splits):
            o_ref[...] = (g_ref[:, at:at + n, :] + r_ref[:, at:at + n, :]).astype(BF16)
            at += n

    return pl.pallas_call(
        body, name="grad_add_pair", out_shape=[jax.ShapeDtypeStruct((4, n, HALF), BF16) for n in splits],
        grid_spec=pltpu.PrefetchScalarGridSpec(
            num_scalar_prefetch=1, grid=(4,),
            in_specs=[pl.BlockSpec((1, rows, HALF), lambda j, c: (j, 0, c[0])),
                      pl.BlockSpec((1, rows, HALF), lambda j, c: (j, 0, 0))],
            out_specs=[pl.BlockSpec((1, n, HALF), lambda j, c: (j, 0, 0)) for n in splits]),
        compiler_params=_params(("arbitrary",)),
    )(c_arr, g, got)


def _add_chips(g, pair_got, chip_gots, pos_arr):
    _, rows, _ = g.shape

    def body(pos_ref, g_ref, p_ref, *refs):
        o_ref, at = refs[-1], 0
        for r_ref in refs[:-1]:
            n = r_ref.shape[1]
            own = g_ref[0, at:at + n, :] + p_ref[0, at:at + n, :]
            o_ref[at:at + n, :] = ((own + r_ref[0].astype(F32)) + r_ref[1].astype(F32)) + r_ref[2].astype(F32)
            at += n

    return pl.pallas_call(
        body, name="grad_add_chips", out_shape=jax.ShapeDtypeStruct((rows, HALF), F32),
        grid_spec=pltpu.PrefetchScalarGridSpec(
            num_scalar_prefetch=1, grid=(1,),
            in_specs=[pl.BlockSpec((1, rows, HALF), lambda i, p: (p[0], 0, p[1])),
                      pl.BlockSpec((1, rows, HALF), lambda i, p: (p[0], 0, 0))]
            + [pl.BlockSpec(t.shape, lambda i, p: (0, 0, 0)) for t in chip_gots],
            out_specs=pl.BlockSpec((rows, HALF), lambda i, p: (0, 0))),
        compiler_params=_params(("arbitrary",)),
    )(pos_arr, g, pair_got, *chip_gots)


def _sum_small(gathered):
    def body(g_ref, o_ref):
        acc = g_ref[0:SMALL_ROWS, :]
        for d in range(1, 8):
            acc = acc + g_ref[d * SMALL_ROWS:(d + 1) * SMALL_ROWS, :]
        o_ref[...] = acc

    return pl.pallas_call(
        body, name="sum_small", out_shape=jax.ShapeDtypeStruct((SMALL_ROWS, 128), F32),
    )(gathered)


def _adamw(w, g, m, v):
    rows, cols = w.shape
    tr = 256 if rows % 256 == 0 else rows
    c1 = 1.0 - ADAM_B1 ** ADAM_STEP
    c2 = 1.0 - ADAM_B2 ** ADAM_STEP

    def body(w_ref, g_ref, m_ref, v_ref, d_ref, mo_ref, vo_ref):
        gg = g_ref[...]
        mn = ADAM_B1 * m_ref[...] + (1.0 - ADAM_B1) * gg
        vn = ADAM_B2 * v_ref[...] + (1.0 - ADAM_B2) * (gg * gg)
        d_ref[...] = -ADAM_LR * ((mn / c1) / (jnp.sqrt(vn / c2) + ADAM_EPS) + ADAM_WD * w_ref[...])
        mo_ref[...] = mn
        vo_ref[...] = vn

    spec = pl.BlockSpec((tr, cols), lambda i: (i, 0))
    return pl.pallas_call(
        body, name="adamw", out_shape=[jax.ShapeDtypeStruct(w.shape, F32)] * 3,
        grid=(rows // tr,), in_specs=[spec] * 4, out_specs=[spec] * 3,
        compiler_params=_params(("parallel",)),
    )(w, g, m, v)


def _ln_in_fwd(x, meta, g, b, fuse=NO_FUSE):
    bsz, s, d = x.shape
    tp = s + HEAD
    nh = 2
    sh = s // nh
    rc = min(256, sh)

    def body(x_ref, meta_ref, g_ref, b_ref, s0_ref, s0b_ref):
        h = pl.program_id(1)
        gg, bb = g_ref[...], b_ref[...]

        @pl.when(h == 0)
        def _():
            s0_ref[0, 0:PAD, :] = jnp.zeros((PAD, d), F32)
            s0b_ref[0, 0:PAD, :] = jnp.zeros((PAD, d), BF16)
            mh, _ = _ln_stats(meta_ref[...])
            mv = mh * gg + bb
            s0_ref[0, PAD:HEAD, :] = mv
            s0b_ref[0, PAD:HEAD, :] = mv.astype(BF16)

        def step(i, carry):
            src = pl.ds(pl.multiple_of(i * rc, rc), rc)
            dst = pl.ds(pl.multiple_of(HEAD + h * sh + i * rc, 64), rc)
            xh, _ = _ln_stats(x_ref[0, src, :])
            val = xh * gg + bb
            s0_ref[0, dst, :] = val
            s0b_ref[0, dst, :] = val.astype(BF16)
            return carry

        lax.fori_loop(0, sh // rc, step, 0)

    full = lambda bi, hi: (bi, 0, 0)
    (s0, s0b), got = _fused_call(
        body, fuse, name="ln_in_fwd",
        out_shape=[jax.ShapeDtypeStruct((bsz, tp, d), F32), jax.ShapeDtypeStruct((bsz, tp, d), BF16)],
        grid=(bsz, nh),
        in_specs=[pl.BlockSpec((1, sh, d), lambda bi, hi: (bi, hi, 0)),
                  pl.BlockSpec((N_META, d), lambda bi, hi: (0, 0)),
                  pl.BlockSpec((1, d), lambda bi, hi: (0, 0)),
                  pl.BlockSpec((1, d), lambda bi, hi: (0, 0))],
        out_specs=[pl.BlockSpec((1, tp, d), full)] * 2, scratch_shapes=[], operands=(x, meta, g, b))
    return s0, s0b, got


def _in_proj(s0b, w_int, fuse=NO_FUSE):
    r, d = s0b.shape
    tm = _row_tile(r)

    def body(a_ref, w_ref, o_ref):
        o_ref[...] = _nt(a_ref[...], w_ref[...])

    (u,), got = _fused_call(
        body, fuse, name="in_proj", out_shape=[jax.ShapeDtypeStruct((r, D_IN_PAD), F32)],
        grid=(r // tm,),
        in_specs=[pl.BlockSpec((tm, d), lambda i: (i, 0)), pl.BlockSpec((D_IN_PAD, d), lambda i: (0, 0))],
        out_specs=[pl.BlockSpec((tm, D_IN_PAD), lambda i: (i, 0))], scratch_shapes=[], operands=(s0b, w_int))
    return u, got


def _conv_fwd(u, conv_w, conv_b, fuse=NO_FUSE):
    bsz, tp, _ = u.shape
    nchunk = tp // CHUNK
    win = CHUNK + 32
    nct = D_CONV // 128

    def body(cv_ref, cg_ref, w_ref, cb_ref, hc_ref, h_scr, win_scr):
        h_scr[0:32, :] = jnp.zeros((32, 128), F32)
        h_scr[32:32 + tp, :] = cv_ref[0] * _sigmoid(cg_ref[0])
        cb = cb_ref[...]

        def step(n, carry):
            r0 = pl.multiple_of(n * CHUNK, CHUNK)
            win_scr[...] = h_scr[pl.ds(r0, win), :]
            acc = jnp.zeros((CHUNK, 128), F32)
            for j in range(CONV_WIDTH):
                acc = acc + w_ref[j:j + 1, :] * win_scr[2 + j:2 + j + CHUNK, :]
            hc_ref[0, pl.ds(r0, CHUNK), :] = acc + cb
            return carry

        lax.fori_loop(0, nchunk, step, 0)

    (hc,), got = _fused_call(
        body, fuse, name="conv_fwd", out_shape=[jax.ShapeDtypeStruct((bsz, tp, D_CONV), F32)],
        grid=(bsz, nct),
        in_specs=[pl.BlockSpec((1, tp, 128), lambda bi, t: (bi, 0, C_VAL // 128 + t)),
                  pl.BlockSpec((1, tp, 128), lambda bi, t: (bi, 0, C_GATE // 128 + t)),
                  pl.BlockSpec((32, 128), lambda bi, t: (0, t)),
                  pl.BlockSpec((1, 128), lambda bi, t: (0, t))],
        out_specs=[pl.BlockSpec((1, tp, 128), lambda bi, t: (bi, 0, t))],
        scratch_shapes=[pltpu.VMEM((tp + 32, 128), F32), pltpu.VMEM((win, 128), F32)],
        operands=(u, u, conv_w, conv_b))
    return hc, got


def _gla_group(nchunk):
    return 11 if nchunk % 11 == 0 else nchunk


def _bdot(a, b, ca, cb, precision=None):
    return lax.dot_general(a, b, (((ca,), (cb,)), ((0,), (0,))), preferred_element_type=F32, precision=precision)


def _bnn(a, b, **kw):
    return _bdot(a, b, 2, 1, **kw)


def _bnt(a, b, **kw):
    return _bdot(a, b, 2, 2, **kw)


def _gla_consts(nb):
    row = lax.broadcasted_iota(jnp.int32, (nb, CHUNK, CHUNK), 1)
    col = lax.broadcasted_iota(jnp.int32, (nb, CHUNK, CHUNK), 2)
    lane = lax.broadcasted_iota(jnp.int32, (1, 1, 128), 2)
    return row >= col, row <= col, [lane < GLA_DK, lane >= GLA_DK]


def _gla_group_terms(g, nb, q_ref, k_ref, gd_ref, gup_ref, gb_ref, tril):
    m = nb * CHUNK
    rows = pl.ds(pl.multiple_of(g * m, CHUNK), m)
    z = _nn(gd_ref[0, rows, :].astype(BF16), gup_ref[...]) + gb_ref[...]
    valid = g * m + lax.broadcasted_iota(jnp.int32, (m, 1), 0) >= PAD
    lg = jnp.where(valid, _log_sigmoid(z) * (1.0 / GLA_TAU), 0.0)
    bcum = _bnn(tril.astype(F32), lg.reshape(nb, CHUNK, 128), precision=lax.Precision.HIGHEST)
    blast = bcum[:, CHUNK - 1:CHUNK, :]
    eb = jnp.exp(bcum)
    enb = jnp.exp(-bcum)
    erest = jnp.exp(blast - bcum)
    q = (q_ref[0, rows, :] * Q_SCALE).reshape(nb, CHUNK, 128)
    k = k_ref[0, rows, :].reshape(nb, CHUNK, 128)
    return rows, valid, z, eb, enb, erest, jnp.exp(blast), q * eb, k * enb, k * erest


def _grid_ends(grid):
    ids = [pl.program_id(i) for i in range(len(grid))]
    first = functools.reduce(jnp.logical_and, [i == 0 for i in ids])
    last = functools.reduce(jnp.logical_and, [i == g - 1 for i, g in zip(ids, grid)])
    return first, last


def _gla_fwd(u, gup, gbias, gnorm, fuse=NO_FUSE):
    bsz, tp, _ = u.shape
    nchunk = tp // CHUNK
    nb = _gla_group(nchunk)

    def body(q_ref, k_ref, v_ref, r_ref, gd_ref, gup_ref, gb_ref, gn_ref, out_ref, o_ref, st_ref, s_scr):
        tril, _, hmask = _gla_consts(nb)
        s_scr[...] = jnp.zeros_like(s_scr)
        gn = gn_ref[...]

        def group(g, carry):
            rows, _, _, _, _, _, dec, qe, ke, kd = _gla_group_terms(g, nb, q_ref, k_ref, gd_ref, gup_ref, gb_ref, tril)
            keb, kdb = ke.astype(BF16), kd.astype(BF16)
            for h in range(2):
                cols = slice(h * GLA_DV, (h + 1) * GLA_DV)
                qh = jnp.where(hmask[h], qe, 0.0).astype(BF16)
                vh = v_ref[0, rows, cols].astype(BF16).reshape(nb, CHUNK, GLA_DV)
                a = jnp.where(tril, _bnt(qh, keb), 0.0).astype(BF16)
                st = s_scr[h]
                sts = []
                for n in range(nb):
                    st_ref[0, h, g * nb + n] = st
                    sts.append(st.astype(BF16))
                    st = dec[n] * st + _tn(vh[n], kdb[n])
                s_scr[h] = st
                o = (_bnn(a, vh) + _bnt(qh, jnp.stack(sts))).reshape(nb * CHUNK, GLA_DV)
                o_ref[0, rows, cols] = o
                rms = lax.rsqrt(jnp.mean(o * o, axis=-1, keepdims=True) + LN_EPS)
                rh = r_ref[0, rows, cols]
                out_ref[0, rows, cols] = (o * rms * gn * (rh * _sigmoid(rh))).astype(BF16)
            return carry

        lax.fori_loop(0, nchunk // nb, group, 0)

    res, got = _fused_call(
        body, fuse, name="gla_fwd",
        out_shape=[jax.ShapeDtypeStruct((bsz, tp, 512), BF16), jax.ShapeDtypeStruct((bsz, tp, 512), F32),
                   jax.ShapeDtypeStruct((bsz, GLA_HEADS, nchunk, GLA_DV, 128), F32)],
        grid=(bsz, 2),
        in_specs=[pl.BlockSpec((1, tp, 128), lambda bi, p: (bi, 0, C_Q // 128 + p)),
                  pl.BlockSpec((1, tp, 128), lambda bi, p: (bi, 0, C_K // 128 + p)),
                  pl.BlockSpec((1, tp, 256), lambda bi, p: (bi, 0, C_V // 256 + p)),
                  pl.BlockSpec((1, tp, 256), lambda bi, p: (bi, 0, C_R // 256 + p)),
                  pl.BlockSpec((1, tp, 128), lambda bi, p: (bi, 0, C_GD // 128)),
                  pl.BlockSpec((128, 128), lambda bi, p: (0, p)),
                  pl.BlockSpec((1, 128), lambda bi, p: (0, p)),
                  pl.BlockSpec((1, 128), lambda bi, p: (0, 0))],
        out_specs=[pl.BlockSpec((1, tp, 256), lambda bi, p: (bi, 0, p)),
                   pl.BlockSpec((1, tp, 256), lambda bi, p: (bi, 0, p)),
                   pl.BlockSpec((1, 2, nchunk, GLA_DV, 128), lambda bi, p: (bi, p, 0, 0, 0))],
        scratch_shapes=[pltpu.VMEM((2, GLA_DV, 128), F32)],
        operands=(u, u, u, u, u, gup, gbias, gnorm))
    return res[0], res[1], res[2], got


def _out_proj_ln1(hc, gla_out, w_out, s0, cg, cb, g1, b1, fuse=NO_FUSE):
    r, d = s0.shape
    tm = _row_tile(r)

    def body(hc_ref, a_ref, w_ref, s0_ref, cg_ref, cb_ref, g_ref, b_ref, co_ref, xh_ref, rstd_ref, s1b_ref):
        for rs in _sub_rows(tm):
            xc, _ = _ln_stats(hc_ref[rs, :])
            nv = xc * cg_ref[...] + cb_ref[...]
            co = (nv * _sigmoid(nv)).astype(BF16)
            co_ref[rs, :] = co
            mix = _nn(co, w_ref[0:D_CONV, :]) + _nn(a_ref[rs, :], w_ref[D_CONV:, :])
            xh, rstd = _ln_stats(ALPHA * s0_ref[rs, :] + mix)
            xh_ref[rs, :] = xh
            rstd_ref[rs, :] = rstd
            s1b_ref[rs, :] = (xh * g_ref[...] + b_ref[...]).astype(BF16)

    row = lambda n: pl.BlockSpec((tm, n), lambda i: (i, 0))
    vec = lambda n: pl.BlockSpec((1, n), lambda i: (0, 0))
    res, got = _fused_call(
        body, fuse, name="out_proj_ln1",
        out_shape=[jax.ShapeDtypeStruct((r, D_CONV), BF16), jax.ShapeDtypeStruct((r, d), F32),
                   jax.ShapeDtypeStruct((r, 1), F32), jax.ShapeDtypeStruct((r, d), BF16)],
        grid=(r // tm,),
        in_specs=[row(D_CONV), row(512), pl.BlockSpec((d, d), lambda i: (0, 0)), row(d),
                  vec(D_CONV), vec(D_CONV), vec(d), vec(d)],
        out_specs=[row(D_CONV), row(d), row(1), row(d)], scratch_shapes=[],
        operands=(hc, gla_out, w_out, s0, cg, cb, g1, b1))
    return res[0], res[1], res[2], res[3], got


def _ffn1(s1b, w1t, fuse=NO_FUSE):
    r, d = s1b.shape
    tm = _row_tile(r)

    def body(a_ref, w_ref, o_ref):
        o_ref[...] = jnp.maximum(_nt(a_ref[...], w_ref[...]), 0.0).astype(BF16)

    (ra,), got = _fused_call(
        body, fuse, name="ffn1", out_shape=[jax.ShapeDtypeStruct((r, D_FF), BF16)], grid=(r // tm,),
        in_specs=[pl.BlockSpec((tm, d), lambda i: (i, 0)), pl.BlockSpec((D_FF, d), lambda i: (0, 0))],
        out_specs=[pl.BlockSpec((tm, D_FF), lambda i: (i, 0))], scratch_shapes=[], operands=(s1b, w1t))
    return ra, got


def _ffn2_ln2_loss(ra, w2, xhat1, g1, b1, g2, b2, tgt, tp):
    r, d = xhat1.shape
    tm = _row_tile(tp)
    per = tp // tm

    def body(ra_ref, w_ref, xh1_ref, g1_ref, b1_ref, g2_ref, b2_ref, tgt_ref, dt_ref, dtb_ref, acc_ref, t_ref, sem):
        i = pl.program_id(0)
        b, j = i // per, i % per

        @pl.when(i == 0)
        def _():
            acc_ref[...] = jnp.zeros_like(acc_ref)

        head_copy = pltpu.make_async_copy(tgt_ref.at[b, pl.ds(0, tm - HEAD), :], t_ref.at[pl.ds(HEAD, tm - HEAD), :], sem)
        body_copy = pltpu.make_async_copy(
            tgt_ref.at[b, pl.ds(pl.multiple_of(jnp.maximum(j * tm - HEAD, 0), 64), tm), :], t_ref, sem)

        @pl.when(j == 0)
        def _():
            t_ref[0:HEAD, :] = jnp.zeros((HEAD, d), F32)
            head_copy.start()

        pl.when(j > 0)(body_copy.start)

        sums = [jnp.zeros((1, d), F32)] * 3
        for rs in _sub_rows(tm):
            rb = ra_ref[rs, :]
            f = _nn(rb * rb, w_ref[...])
            if rs.start == 0:
                pl.when(j == 0)(head_copy.wait)
                pl.when(j > 0)(body_copy.wait)
            s1 = xh1_ref[rs, :] * g1_ref[...] + b1_ref[...]
            xh2, rstd2 = _ln_stats(ALPHA * s1 + f)
            y = xh2 * g2_ref[...] + b2_ref[...]
            rowid = (i % per) * tm + rs.start + lax.broadcasted_iota(jnp.int32, (rs.stop - rs.start, 1), 0)
            e = jnp.where(rowid >= HEAD, y - t_ref[rs, :], 0.0)
            dy = e * (1.0 / d)
            dt2 = _ln_bwd(dy * g2_ref[...], xh2, rstd2)
            dt_ref[rs, :] = dt2
            dtb_ref[rs, :] = dt2.astype(BF16)
            sums = [sums[0] + (0.5 / d) * jnp.sum(e * e, axis=0, keepdims=True),
                    sums[1] + jnp.sum(dy * xh2, axis=0, keepdims=True), sums[2] + jnp.sum(dy, axis=0, keepdims=True)]
        for k in range(3):
            acc_ref[k:k + 1, :] += sums[k]

    row = lambda n: pl.BlockSpec((tm, n), lambda i: (i, 0))
    vec = pl.BlockSpec((1, d), lambda i: (0, 0))
    return pl.pallas_call(
        body, name="ffn2_ln2_loss",
        out_shape=[jax.ShapeDtypeStruct((r, d), F32), jax.ShapeDtypeStruct((r, d), BF16),
                   jax.ShapeDtypeStruct((8, d), F32)],
        grid=(r // tm,),
        in_specs=[row(D_FF), pl.BlockSpec((D_FF, d), lambda i: (0, 0)), row(d), vec, vec, vec, vec, ANY],
        out_specs=[row(d), row(d), pl.BlockSpec((8, d), lambda i: (0, 0))],
        scratch_shapes=[pltpu.VMEM((tm, d), F32), pltpu.SemaphoreType.DMA],
        compiler_params=_params(("arbitrary",)),
    )(ra, w2, xhat1, g1, b1, g2, b2, tgt)


def _ffn_bwd_da(dt2b, w2, ra):
    r, d = dt2b.shape
    tm = _row_tile(r)

    def body(g_ref, w_ref, ra_ref, o_ref):
        o_ref[...] = (_nt(g_ref[...], w_ref[...]) * (2.0 * ra_ref[...].astype(F32))).astype(BF16)

    return pl.pallas_call(
        body, name="ffn_bwd_da", out_shape=jax.ShapeDtypeStruct((r, D_FF), BF16),
        grid=(r // tm,),
        in_specs=[pl.BlockSpec((tm, d), lambda i: (i, 0)), pl.BlockSpec((D_FF, d), lambda i: (0, 0)),
                  pl.BlockSpec((tm, D_FF), lambda i: (i, 0))],
        out_specs=pl.BlockSpec((tm, D_FF), lambda i: (i, 0)),
        compiler_params=_params(("parallel",)),
    )(dt2b, w2, ra)


def _ffn_bwd_ln1(da, w1t, dt2, xhat1, rstd1, g1):
    r, d = dt2.shape
    tm = _row_tile(r)

    def body(da_ref, w_ref, dt2_ref, xh_ref, rstd_ref, g_ref, dt_ref, dtb_ref, acc_ref):
        @pl.when(pl.program_id(0) == 0)
        def _():
            acc_ref[...] = jnp.zeros_like(acc_ref)

        sums = [jnp.zeros((1, d), F32)] * 2
        for rs in _sub_rows(tm):
            ds1 = ALPHA * dt2_ref[rs, :] + _nn(da_ref[rs, :], w_ref[...])
            xh = xh_ref[rs, :]
            dt1 = _ln_bwd(ds1 * g_ref[...], xh, rstd_ref[rs, :])
            dt_ref[rs, :] = dt1
            dtb_ref[rs, :] = dt1.astype(BF16)
            sums = [sums[0] + jnp.sum(ds1 * xh, axis=0, keepdims=True), sums[1] + jnp.sum(ds1, axis=0, keepdims=True)]
        for k in range(2):
            acc_ref[k:k + 1, :] += sums[k]

    row = lambda n: pl.BlockSpec((tm, n), lambda i: (i, 0))
    return pl.pallas_call(
        body, name="ffn_bwd_ln1",
        out_shape=[jax.ShapeDtypeStruct((r, d), F32), jax.ShapeDtypeStruct((r, d), BF16),
                   jax.ShapeDtypeStruct((8, d), F32)],
        grid=(r // tm,),
        in_specs=[row(D_FF), pl.BlockSpec((D_FF, d), lambda i: (0, 0)), row(d), row(d), row(1),
                  pl.BlockSpec((1, d), lambda i: (0, 0))],
        out_specs=[row(d), row(d), pl.BlockSpec((8, d), lambda i: (0, 0))],
        compiler_params=_params(("arbitrary",)),
    )(da, w1t, dt2, xhat1, rstd1, g1)


def _matmul_tn(lhs, rhs, bm, square_lhs=False, name="matmul_tn", fuse=NO_FUSE):
    r, m = lhs.shape
    n = rhs.shape[1]
    tk = _reduce_tile(r, True)

    def body(a_ref, b_ref, o_ref):
        @pl.when(pl.program_id(1) == 0)
        def _():
            o_ref[...] = jnp.zeros_like(o_ref)

        a = a_ref[...]
        if square_lhs:
            a = a * a
        o_ref[...] += _tn(a, b_ref[...])

    (out,), got = _fused_call(
        body, fuse, name=name, out_shape=[jax.ShapeDtypeStruct((m, n), F32)], grid=(m // bm, r // tk),
        in_specs=[pl.BlockSpec((tk, bm), lambda i, k: (k, i)), pl.BlockSpec((tk, n), lambda i, k: (k, 0))],
        out_specs=[pl.BlockSpec((bm, n), lambda i, k: (i, 0))], scratch_shapes=[], operands=(lhs, rhs))
    return out, got


def _grad_w_out(conv_of, gla_of, dt1b):
    r, n = dt1b.shape
    tk = _reduce_tile(r, True)

    def body(a_ref, b_ref, g_ref, o_ref):
        @pl.when(pl.program_id(1) == 0)
        def _():
            o_ref[...] = jnp.zeros_like(o_ref)

        @pl.when(pl.program_id(0) == 0)
        def _():
            o_ref[...] += _tn(a_ref[...], g_ref[...])

        @pl.when(pl.program_id(0) == 1)
        def _():
            o_ref[...] += _tn(b_ref[...], g_ref[...])

    lhs = pl.BlockSpec((tk, 512), lambda i, k: (k, 0))
    return pl.pallas_call(
        body, name="grad_w_out", out_shape=jax.ShapeDtypeStruct((2 * 512, n), F32), grid=(2, r // tk),
        in_specs=[lhs, lhs, pl.BlockSpec((tk, n), lambda i, k: (k, 0))],
        out_specs=pl.BlockSpec((512, n), lambda i, k: (i, 0)),
        compiler_params=_params(("parallel", "arbitrary")),
    )(conv_of, gla_of, dt1b)


def _out_proj_bwd(dt1b, w_out, hc, cg, cb, fuse=NO_FUSE):
    r, d = dt1b.shape
    tm = _row_tile(r)

    def body(g_ref, w_ref, hc_ref, cg_ref, cb_ref, dhc_ref, dgla_ref, acc_ref):
        @pl.when(pl.program_id(0) == 0)
        def _():
            acc_ref[...] = jnp.zeros_like(acc_ref)

        gg = cg_ref[...]
        sums = [jnp.zeros((1, D_CONV), F32)] * 3
        for rs in _sub_rows(tm):
            dmix = _nt(g_ref[rs, :], w_ref[...])
            dgla_ref[rs, :] = dmix[:, D_CONV:]
            xh, rstd = _ln_stats(hc_ref[rs, :])
            nv = xh * gg + cb_ref[...]
            sig = _sigmoid(nv)
            dn = dmix[:, :D_CONV] * (sig * (1.0 + nv * (1.0 - sig)))
            dhc = _ln_bwd(dn * gg, xh, rstd)
            dhc_ref[rs, :] = dhc
            sums = [sums[0] + jnp.sum(dhc, axis=0, keepdims=True), sums[1] + jnp.sum(dn * xh, axis=0, keepdims=True),
                    sums[2] + jnp.sum(dn, axis=0, keepdims=True)]
        for k in range(3):
            acc_ref[k:k + 1, :] += sums[k]

    row = lambda n: pl.BlockSpec((tm, n), lambda i: (i, 0))
    vec = pl.BlockSpec((1, D_CONV), lambda i: (0, 0))
    res, got = _fused_call(
        body, fuse, name="out_proj_bwd",
        out_shape=[jax.ShapeDtypeStruct((r, D_CONV), F32), jax.ShapeDtypeStruct((r, 512), F32),
                   jax.ShapeDtypeStruct((8, D_CONV), F32)],
        grid=(r // tm,),
        in_specs=[row(d), pl.BlockSpec((d, d), lambda i: (0, 0)), row(D_CONV), vec, vec],
        out_specs=[row(D_CONV), row(512), pl.BlockSpec((8, D_CONV), lambda i: (0, 0))], scratch_shapes=[],
        operands=(dt1b, w_out, hc, cg, cb))
    return res[0], res[1], res[2], got


def _conv_bwd(dhc, u, conv_w, fuse=NO_FUSE):
    bsz, tp, _ = u.shape
    nchunk = tp // CHUNK
    win = CHUNK + 32
    nct = D_CONV // 128

    def body(dhc_ref, cv_ref, cg_ref, w_ref, dv_ref, dg_ref, dw_ref, h_scr, dhc_scr, hwin, dwin, dw_scr):
        h_scr[0:32, :] = jnp.zeros((32, 128), F32)
        h_scr[32:32 + tp, :] = cv_ref[0] * _sigmoid(cg_ref[0])
        dhc_scr[0:tp, :] = dhc_ref[0]
        dhc_scr[tp:tp + 32, :] = jnp.zeros((32, 128), F32)
        dw_scr[...] = jnp.zeros_like(dw_scr)

        def step(n, carry):
            r0 = pl.multiple_of(n * CHUNK, CHUNK)
            rows = pl.ds(r0, CHUNK)
            hwin[...] = h_scr[pl.ds(r0, win), :]
            dwin[...] = dhc_scr[pl.ds(r0, win), :]
            dcur = dwin[0:CHUNK, :]
            acc = jnp.zeros((CHUNK, 128), F32)
            for j in range(CONV_WIDTH):
                acc = acc + w_ref[j:j + 1, :] * dwin[30 - j:30 - j + CHUNK, :]
                prod = dcur * hwin[2 + j:2 + j + CHUNK, :]
                dw_scr[j * 8:(j + 1) * 8, :] += jnp.sum(prod.reshape(CHUNK // 8, 8, 128), axis=0)
            cg = cg_ref[0, rows, :]
            sig = _sigmoid(cg)
            rowid = n * CHUNK + lax.broadcasted_iota(jnp.int32, (CHUNK, 1), 0)
            dh = jnp.where(rowid >= PAD, acc, 0.0)
            dv_ref[0, rows, :] = (dh * sig).astype(BF16)
            dg_ref[0, rows, :] = (dh * cv_ref[0, rows, :] * sig * (1.0 - sig)).astype(BF16)
            return carry

        lax.fori_loop(0, nchunk, step, 0)
        dw_ref[0] = jnp.zeros((32, 128), F32)
        for j in range(CONV_WIDTH):
            dw_ref[0, j:j + 1, :] = jnp.sum(dw_scr[j * 8:(j + 1) * 8, :], axis=0, keepdims=True)

    blk = lambda off: pl.BlockSpec((1, tp, 128), lambda bi, t: (bi, 0, off // 128 + t))
    res, got = _fused_call(
        body, fuse, name="conv_bwd",
        out_shape=[jax.ShapeDtypeStruct((bsz, tp, D_CONV), BF16), jax.ShapeDtypeStruct((bsz, tp, D_CONV), BF16),
                   jax.ShapeDtypeStruct((bsz, 32, D_CONV), F32)],
        grid=(bsz, nct),
        in_specs=[blk(0), blk(C_VAL), blk(C_GATE), pl.BlockSpec((32, 128), lambda bi, t: (0, t))],
        out_specs=[blk(0), blk(0), pl.BlockSpec((1, 32, 128), lambda bi, t: (bi, 0, t))],
        scratch_shapes=[pltpu.VMEM((tp + 32, 128), F32), pltpu.VMEM((tp + 32, 128), F32),
                        pltpu.VMEM((win, 128), F32), pltpu.VMEM((win, 128), F32),
                        pltpu.VMEM((CONV_WIDTH * 8, 128), F32)],
        operands=(dhc, u, u, conv_w))
    return res[0], res[1], res[2], got


def _gla_bwd(dgla, u, o_pre, states, gup, gbias, gnorm, fuse=NO_FUSE):
    bsz, tp, _ = u.shape
    nchunk = tp // CHUNK
    nb = _gla_group(nchunk)

    def body(dy_ref, q_ref, k_ref, v_ref, r_ref, gd_ref, o_ref, st_ref, gup_ref, gb_ref, gn_ref,
             dq_ref, dk_ref, dv_ref, dr_ref, dgd_ref, dgup_ref, vec_ref, h_scr, gup_acc):
        tril, triu, hmask = _gla_consts(nb)
        h_scr[...] = jnp.zeros_like(h_scr)
        gup_acc[...] = jnp.zeros_like(gup_acc)
        gn = gn_ref[...]
        gupb = gup_ref[...]
        m = nb * CHUNK
        ngroup = nchunk // nb

        def group(i, carry):
            dbias, dgn = carry
            g = ngroup - 1 - i
            rows, valid, z, eb, enb, erest, dec, qe, ke, kd = _gla_group_terms(
                g, nb, q_ref, k_ref, gd_ref, gup_ref, gb_ref, tril)
            keb, kdb = ke.astype(BF16), kd.astype(BF16)
            dqe = jnp.zeros((nb, CHUNK, 128), F32)
            dke = jnp.zeros((nb, CHUNK, 128), F32)
            dkd = jnp.zeros((nb, CHUNK, 128), F32)
            ddec = jnp.zeros((nb, 1, 128), F32)
            for h in range(2):
                cols = slice(h * GLA_DV, (h + 1) * GLA_DV)
                o = o_ref[0, rows, cols]
                rh = r_ref[0, rows, cols]
                dy = dy_ref[0, rows, cols]
                rms = lax.rsqrt(jnp.mean(o * o, axis=-1, keepdims=True) + LN_EPS)
                nrm = o * rms
                sig = _sigmoid(rh)
                sw = rh * sig
                dr_ref[0, rows, cols] = (dy * nrm * gn * (sig * (1.0 + rh * (1.0 - sig)))).astype(BF16)
                dgn = dgn + jnp.sum(dy * nrm * sw, axis=0, keepdims=True)
                dn = dy * gn * sw
                do = rms * (dn - nrm * jnp.mean(dn * nrm, axis=-1, keepdims=True))
                dob = do.astype(BF16).reshape(nb, CHUNK, GLA_DV)
                qh = jnp.where(hmask[h], qe, 0.0).astype(BF16)
                vh = v_ref[0, rows, cols].astype(BF16).reshape(nb, CHUNK, GLA_DV)
                ht = h_scr[h]
                hts = [None] * nb
                for n in reversed(range(nb)):
                    hts[n] = ht
                    ht = dec[n] * ht + _tn(dob[n], qh[n])
                h_scr[h] = ht
                htf = jnp.stack(hts)
                htb = htf.astype(BF16)
                st = st_ref[0, h, pl.ds(g * nb, nb)]
                at = jnp.where(triu, _bnt(keb, qh), 0.0).astype(BF16)
                da = jnp.where(tril, _bnt(dob, vh), 0.0).astype(BF16)
                dat = jnp.where(triu, _bnt(vh, dob), 0.0).astype(BF16)
                dqe = dqe + jnp.where(hmask[h], _bnn(da, keb) + _bnn(dob, st.astype(BF16)), 0.0)
                dke = dke + _bnn(dat, qh)
                dv_ref[0, rows, cols] = (_bnn(at, dob) + _bnt(kdb, htb)).reshape(m, GLA_DV).astype(BF16)
                dkd = dkd + jnp.where(hmask[h], _bnn(vh, htb), 0.0)
                ddec = ddec + jnp.where(hmask[h], jnp.sum(htf * st, axis=1, keepdims=True), 0.0)
            dq_ref[0, rows, :] = (dqe * eb * Q_SCALE).reshape(m, 128).astype(BF16)
            dk_ref[0, rows, :] = (dke * enb + dkd * erest).reshape(m, 128).astype(BF16)
            db = dqe * qe - dke * ke - dkd * kd
            dblast = jnp.sum(dkd * kd, axis=1, keepdims=True) + ddec * dec
            lastrow = lax.broadcasted_iota(jnp.int32, (1, CHUNK, 1), 1) == CHUNK - 1
            db = db + jnp.where(lastrow, dblast, 0.0)
            dlg = _bnn(triu.astype(F32), db, precision=lax.Precision.HIGHEST).reshape(m, 128)
            dz = jnp.where(valid, dlg, 0.0) * (1.0 / GLA_TAU) * (1.0 - _sigmoid(z))
            dzb = dz.astype(BF16)
            dgd_ref[0, 0, rows, :] = _nt(dzb, gupb).astype(BF16)
            gup_acc[...] += _tn(gd_ref[0, rows, :].astype(BF16), dzb)
            return dbias + jnp.sum(dz, axis=0, keepdims=True), dgn

        zero = jnp.zeros((1, 128), F32)
        dbias, dgn = lax.fori_loop(0, ngroup, group, (zero, zero))
        dgup_ref[0] = gup_acc[...]
        vec_ref[0] = jnp.zeros((8, 128), F32)
        vec_ref[0, 0:1, :] = dbias
        vec_ref[0, 1:2, :] = dgn

    pair = lambda w, off: pl.BlockSpec((1, tp, w), lambda bi, p: (bi, 0, off // w + p))
    return _fused_call(
        body, fuse, name="gla_bwd",
        out_shape=[jax.ShapeDtypeStruct((bsz, tp, 256), BF16), jax.ShapeDtypeStruct((bsz, tp, 256), BF16),
                   jax.ShapeDtypeStruct((bsz, tp, 512), BF16), jax.ShapeDtypeStruct((bsz, tp, 512), BF16),
                   jax.ShapeDtypeStruct((bsz, 2, tp, 128), BF16), jax.ShapeDtypeStruct((bsz, 128, 256), F32),
                   jax.ShapeDtypeStruct((bsz, 8, 256), F32)],
        grid=(bsz, 2),
        in_specs=[pair(256, 0), pair(128, C_Q), pair(128, C_K), pair(256, C_V), pair(256, C_R),
                  pl.BlockSpec((1, tp, 128), lambda bi, p: (bi, 0, C_GD // 128)),
                  pair(256, 0),
                  pl.BlockSpec((1, 2, nchunk, GLA_DV, 128), lambda bi, p: (bi, p, 0, 0, 0)),
                  pl.BlockSpec((128, 128), lambda bi, p: (0, p)),
                  pl.BlockSpec((1, 128), lambda bi, p: (0, p)),
                  pl.BlockSpec((1, 128), lambda bi, p: (0, 0))],
        out_specs=[pair(128, 0), pair(128, 0), pair(256, 0), pair(256, 0),
                   pl.BlockSpec((1, 1, tp, 128), lambda bi, p: (bi, p, 0, 0)),
                   pl.BlockSpec((1, 128, 128), lambda bi, p: (bi, 0, p)),
                   pl.BlockSpec((1, 8, 128), lambda bi, p: (bi, 0, p))],
        scratch_shapes=[pltpu.VMEM((2, GLA_DV, 128), F32), pltpu.VMEM((128, 128), F32)],
        operands=(dgla, u, u, u, u, u, o_pre, states, gup, gbias, gnorm))


_DU_OFFSETS = (C_VAL, C_GATE, C_Q, C_K, C_V, C_R)
_DU_WIDTHS = (512, 512, 256, 256, 512, 512)


def _du_specs(tm, per, row_map):
    specs = [pl.BlockSpec((tm, w), row_map) for w in _DU_WIDTHS]
    for p in range(2):
        specs.append(pl.BlockSpec((1, 1, tm, 128), lambda *ix, p=p: (row_map(*ix)[0] // per, p, row_map(*ix)[0] % per, 0)))
    return specs


def _du_pieces(refs):
    out = [(off, ref[...]) for off, ref in zip(_DU_OFFSETS, refs[:6])]
    dgd = (refs[6][0, 0].astype(F32) + refs[7][0, 0].astype(F32)).astype(BF16)
    out.append((C_GD, dgd))
    return out


def _in_proj_bwd(pieces, dgd, w_int, dt1, tp, fuse=NO_FUSE):
    r, d = dt1.shape
    tm = _row_tile(tp)
    per = tp // tm

    def body(*refs):
        w_ref, dt_ref, o_ref = refs[8:]
        acc = ALPHA * dt_ref[...]
        for off, val in _du_pieces(refs[:8]):
            acc = acc + _nn(val, w_ref[off:off + val.shape[1], :])
        o_ref[...] = acc

    row = lambda i: (i, 0)
    (ds0,), got = _fused_call(
        body, fuse, name="in_proj_bwd", out_shape=[jax.ShapeDtypeStruct((r, d), F32)], grid=(r // tm,),
        in_specs=_du_specs(tm, per, row) + [pl.BlockSpec((D_IN_PAD, d), lambda i: (0, 0)), pl.BlockSpec((tm, d), row)],
        out_specs=[pl.BlockSpec((tm, d), row)], scratch_shapes=[], operands=(*pieces, dgd, dgd, w_int, dt1))
    return ds0, got


def _grad_w_in(pieces, dgd, s0b, tp, fuse=NO_FUSE):
    r, d = s0b.shape
    tk = _reduce_tile(tp, False)
    per = tp // tk

    def body(*refs):
        s_ref, o_ref = refs[8:]

        @pl.when(pl.program_id(0) == 0)
        def _():
            o_ref[...] = jnp.zeros_like(o_ref)

        s = s_ref[...]
        for off, val in _du_pieces(refs[:8]):
            o_ref[off:off + val.shape[1], :] += _tn(val, s)

    row = lambda k: (k, 0)
    (out,), got = _fused_call(
        body, fuse, name="grad_w_in", out_shape=[jax.ShapeDtypeStruct((D_IN_PAD, d), F32)], grid=(r // tk,),
        in_specs=_du_specs(tk, per, row) + [pl.BlockSpec((tk, d), row)],
        out_specs=[pl.BlockSpec((D_IN_PAD, d), lambda k: (0, 0))], scratch_shapes=[],
        operands=(*pieces, dgd, dgd, s0b))
    return out, got


def _ln_in_bwd(ds0, x, meta, g):
    bsz, s, d = x.shape
    tp = s + HEAD
    nh = 2
    sh = s // nh
    rc = min(256, sh)

    def body(ds_ref, x_ref, meta_ref, g_ref, gx_ref, dm_ref, vec_ref):
        h = pl.program_id(1)
        gg = g_ref[...]

        @pl.when(h == 0)
        def _():
            mh, mr = _ln_stats(meta_ref[...])
            dsm = ds_ref[0, PAD:HEAD, :]
            dm_ref[0] = _ln_bwd(dsm * gg, mh, mr)
            vec_ref[0] = jnp.zeros((8, d), F32)
            vec_ref[0, 0:1, :] = jnp.sum(dsm * mh, axis=0, keepdims=True)
            vec_ref[0, 1:2, :] = jnp.sum(dsm, axis=0, keepdims=True)

        def step(i, carry):
            sg, sb = carry
            dst = pl.ds(pl.multiple_of(i * rc, rc), rc)
            src = pl.ds(pl.multiple_of(HEAD + h * sh + i * rc, 64), rc)
            xh, rstd = _ln_stats(x_ref[0, dst, :])
            dsv = ds_ref[0, src, :]
            gx_ref[0, dst, :] = _ln_bwd(dsv * gg, xh, rstd)
            return sg + jnp.sum(dsv * xh, axis=0, keepdims=True), sb + jnp.sum(dsv, axis=0, keepdims=True)

        zero = jnp.zeros((1, d), F32)
        sg, sb = lax.fori_loop(0, sh // rc, step, (zero, zero))
        vec_ref[0, 0:1, :] += sg
        vec_ref[0, 1:2, :] += sb

    return pl.pallas_call(
        body, name="ln_in_bwd",
        out_shape=[jax.ShapeDtypeStruct((bsz, s, d), F32), jax.ShapeDtypeStruct((bsz, N_META, d), F32),
                   jax.ShapeDtypeStruct((bsz, 8, d), F32)],
        grid=(bsz, nh),
        in_specs=[pl.BlockSpec((1, tp, d), lambda bi, hi: (bi, 0, 0)),
                  pl.BlockSpec((1, sh, d), lambda bi, hi: (bi, hi, 0)),
                  pl.BlockSpec((N_META, d), lambda bi, hi: (0, 0)),
                  pl.BlockSpec((1, d), lambda bi, hi: (0, 0))],
        out_specs=[pl.BlockSpec((1, sh, d), lambda bi, hi: (bi, hi, 0)),
                   pl.BlockSpec((1, N_META, d), lambda bi, hi: (bi, 0, 0)),
                   pl.BlockSpec((1, 8, d), lambda bi, hi: (bi, 0, 0))],
        compiler_params=_params(("parallel", "arbitrary")),
    )(ds0, x, meta, g)


def _rows128(a):
    return a.reshape(-1, 128)


def kernel(x, meta_tokens, ln_in_g, ln_in_b, w_in, conv_w, conv_b, conv_ln_g, conv_ln_b, gate_up, gate_bias, gla_norm_g, w_out, ln1_g, ln1_b, w_ff1, w_ff2, ln2_g, ln2_b, loss_target, m_meta_tokens, m_ln_in_g, m_ln_in_b, m_w_in, m_conv_w, m_conv_b, m_conv_ln_g, m_conv_ln_b, m_gate_up, m_gate_bias, m_gla_norm_g, m_w_out, m_ln1_g, m_ln1_b, m_w_ff1, m_w_ff2, m_ln2_g, m_ln2_b, v_meta_tokens, v_ln_in_g, v_ln_in_b, v_w_in, v_conv_w, v_conv_b, v_conv_ln_g, v_conv_ln_b, v_gate_up, v_gate_bias, v_gla_norm_g, v_w_out, v_ln1_g, v_ln1_b, v_w_ff1, v_w_ff2, v_ln2_g, v_ln2_b):
    bsz, seq, d = x.shape
    tp = seq + HEAD
    r = bsz * tp
    xi, yi, ci = _mesh_pos()
    chip = 2 * xi + yi
    c_arr = jnp.reshape(ci, (1,)).astype(jnp.int32)
    pos_arr = jnp.stack([chip, ci]).astype(jnp.int32)

    sh_in = D_IN // 4
    shard_in = jnp.pad(w_in[0].T.astype(BF16), ((0, D_IN_PAD // 4 - sh_in), (0, 0)))
    shard_w1t, shard_wout, shard_w2 = w_ff1[0].T.astype(BF16), w_out[0].astype(BF16), w_ff2[0].astype(BF16)
    small_w = jnp.concatenate([_rows128(meta_tokens), _rows128(conv_w[0]), _rows128(gate_up[0])], axis=0)
    g_small = _gather_small(small_w)
    n_meta_rows, n_cw_rows = N_META * 256 // 128, CONV_WIDTH * 128 // 128
    meta_full = jnp.concatenate([g_small[j, :n_meta_rows].reshape(N_META, 256) for j in range(4)], axis=1)
    convw_full = jnp.concatenate(
        [g_small[j, n_meta_rows:n_meta_rows + n_cw_rows].reshape(CONV_WIDTH, 128) for j in range(4)], axis=1)
    gup_full = jnp.concatenate(
        [g_small[j, n_meta_rows + n_cw_rows:].reshape(GLA_RANK, 64) for j in range(4)], axis=1)
    convw_p = jnp.pad(convw_full, ((0, 1), (0, 0)))
    gup_p = jnp.pad(gup_full, ((0, 128 - GLA_RANK), (0, 0))).astype(BF16)
    ln_in_g2, ln_in_b2 = ln_in_g.reshape(1, d), ln_in_b.reshape(1, d)

    s0, s0b, ((g_int,),) = _ln_in_fwd(x, meta_full, ln_in_g2, ln_in_b2,
                                      [("gather", [(shard_in, 0, D_IN_PAD // 4, None)])])
    (g_int,) = _place_own([g_int], [shard_in])
    w_int = jnp.pad(g_int[:, :sh_in].reshape(D_IN, d), ((0, D_IN_PAD - D_IN), (0, 0)))
    s0f, s0bf = s0.reshape(r, d), s0b.reshape(r, d)
    u, ((w1_buf,),) = _in_proj(s0bf, w_int, [("gather", [(shard_w1t, 0, 640, None)])])
    (w1_buf,) = _place_own([w1_buf], [shard_w1t])
    u3 = u.reshape(bsz, tp, D_IN_PAD)
    hc, ((w1_buf,),) = _conv_fwd(u3, convw_p, conv_b, [("gather", [(shard_w1t, 640, 384, w1_buf)])])
    hc = hc.reshape(r, D_CONV)
    gla_out, o_pre, states, ((g_wout, w2_buf),) = _gla_fwd(
        u3, gup_p, gate_bias, gla_norm_g, [("gather", [(shard_wout, 0, 256, None), (shard_w2, 0, 256, None)])])
    g_wout, w2_buf = _place_own([g_wout, w2_buf], [shard_wout, shard_w2])
    wout = g_wout.reshape(d, d)
    gla_of = gla_out.reshape(r, 512)
    conv_of, xhat1, rstd1, s1b, ((w2_buf,),) = _out_proj_ln1(
        hc, gla_of, wout, s0f, conv_ln_g, conv_ln_b, ln1_g, ln1_b, [("gather", [(shard_w2, 256, 384, w2_buf)])])
    w1t = w1_buf.reshape(D_FF, d)
    ra, ((w2_buf,),) = _ffn1(s1b, w1t, [("gather", [(shard_w2, 640, 384, w2_buf)])])
    w2 = w2_buf.reshape(D_FF, d)
    dt2, dt2b, acc2 = _ffn2_ln2_loss(ra, w2, xhat1, ln1_g, ln1_b, ln2_g, ln2_b, loss_target, tp)

    def add_pair(g, got, splits=None):
        return _add_pair(g, got, c_arr, splits)

    da = _ffn_bwd_da(dt2b, w2, ra)
    dt1, dt1b, acc1 = _ffn_bwd_ln1(da, w1t, dt2, xhat1, rstd1, ln1_g)
    g_w2, _ = _matmul_tn(ra, dt2b, 1024, square_lhs=True, name="grad_w_ff2")
    big_w2 = g_w2.reshape(4, D_FF // 4, d)
    g_w1t, ((pair_w2,),) = _matmul_tn(da, s1b, 1024, name="grad_w_ff1", fuse=[("pair", [big_w2])])
    big_w1 = g_w1t.reshape(4, D_FF // 4, d)
    dhc, dgla, cacc, ((pair_w1,),) = _out_proj_bwd(dt1b, wout, hc, conv_ln_g, conv_ln_b, [("pair", [big_w1])])
    (part_w1,), (part_w2a, part_w2b) = add_pair(big_w1, pair_w1), add_pair(big_w2, pair_w2, [512, 512])
    big_wout = _grad_w_out(conv_of, gla_of, dt1b).reshape(4, d // 4, d)
    dcv, dcg, dcw, ((chip_w1,), (pair_wout,)) = _conv_bwd(
        dhc.reshape(bsz, tp, D_CONV), u3, convw_p, [("exchange", [part_w1]), ("pair", [big_wout])])
    (part_wout,) = add_pair(big_wout, pair_wout)
    (dq, dk, dv, dr, dgd, dgup, gvec), ((chip_w2a,),) = _gla_bwd(
        dgla.reshape(bsz, tp, 512), u3, o_pre, states, gup_p, gate_bias, gla_norm_g, [("exchange", [part_w2a])])
    pieces = [a.reshape(r, a.shape[-1]) for a in (dcv, dcg, dq, dk, dv, dr)]
    g_wint, ((chip_w2b, chip_wout),) = _grad_w_in(
        pieces, dgd, s0bf, tp, [("exchange", [part_w2b, part_wout])])
    big_win = jnp.stack([g_wint[j * sh_in:(j + 1) * sh_in] for j in range(4)])
    (pair_win,) = _pair_exchange([big_win])
    half_w1 = _add_chips(big_w1, pair_w1, [chip_w1], pos_arr)
    half_w2 = _add_chips(big_w2, pair_w2, [chip_w2a, chip_w2b], pos_arr)
    ds0, ((chip_win,), (sib_w1, sib_w2)) = _in_proj_bwd(
        pieces, dgd, w_int, dt1, tp, [("exchange", add_pair(big_win, pair_win)), ("swap", [half_w1, half_w2])])
    grad_x, dmeta, lvec = _ln_in_bwd(ds0.reshape(bsz, tp, d), x, meta_full, ln_in_g2)

    small = {
        "loss": acc2[0:1], "ln_in_g": jnp.sum(lvec[:, 0], axis=0), "ln_in_b": jnp.sum(lvec[:, 1], axis=0),
        "conv_b": cacc[0], "conv_ln_g": cacc[1], "conv_ln_b": cacc[2], "gate_bias": jnp.sum(gvec[:, 0], axis=0),
        "gla_norm_g": jnp.sum(gvec[:, 1].reshape(bsz * 2, 128), axis=0),
        "ln1_g": acc1[0], "ln1_b": acc1[1], "ln2_g": acc2[1], "ln2_b": acc2[2],
        "conv_w": jnp.sum(dcw, axis=0)[:CONV_WIDTH], "gate_up": jnp.sum(dgup, axis=0)[:GLA_RANK],
        "meta_tokens": jnp.sum(dmeta, axis=0),
    }
    pack = jnp.concatenate([_rows128(small[k]) for k, _ in _SMALL_FIELDS], axis=0)
    pack = jnp.pad(pack, ((0, SMALL_ROWS - pack.shape[0]), (0, 0)))
    red = _sum_small(_allgather_small(pack))
    off, tot = {}, 0
    for k, nrows in _SMALL_FIELDS:
        off[k] = (tot, nrows)
        tot += nrows

    def field(k, shape):
        o, nrows = off[k]
        return red[o:o + nrows].reshape(shape)

    loss = jnp.sum(field("loss", (d,)))

    half_win = _add_chips(big_win, pair_win, [chip_win], pos_arr)
    half_wout = _add_chips(big_wout, pair_wout, [chip_wout], pos_arr)
    sib_win, sib_wout = _pair_swap([half_win, half_wout])
    f_wint, f_wout, f_w1t, f_w2 = [
        jnp.concatenate([jnp.where(ci == 0, own, sib), jnp.where(ci == 0, sib, own)], axis=1)
        for own, sib in ((half_win, sib_win), (half_wout, sib_wout), (half_w1, sib_w1), (half_w2, sib_w2))]

    def cols(a, width):
        return lax.dynamic_slice_in_dim(a, chip * width, width, axis=a.ndim - 1)

    grads = {
        "meta_tokens": cols(field("meta_tokens", (N_META, d)), 256),
        "ln_in_g": field("ln_in_g", (d,)), "ln_in_b": field("ln_in_b", (d,)),
        "w_in": f_wint.T[None], "conv_w": cols(field("conv_w", (CONV_WIDTH, D_CONV)), 128)[None],
        "conv_b": field("conv_b", (1, D_CONV)), "conv_ln_g": field("conv_ln_g", (1, D_CONV)),
        "conv_ln_b": field("conv_ln_b", (1, D_CONV)),
        "gate_up": cols(field("gate_up", (GLA_RANK, 256)), 64)[None],
        "gate_bias": field("gate_bias", (1, 256)), "gla_norm_g": field("gla_norm_g", (1, 128)),
        "w_out": f_wout[None], "ln1_g": field("ln1_g", (1, d)), "ln1_b": field("ln1_b", (1, d)),
        "w_ff1": f_w1t.T[None], "w_ff2": f_w2[None],
        "ln2_g": field("ln2_g", (1, d)), "ln2_b": field("ln2_b", (1, d)),
    }
    weights = dict(meta_tokens=meta_tokens, ln_in_g=ln_in_g, ln_in_b=ln_in_b, w_in=w_in, conv_w=conv_w, conv_b=conv_b,
                   conv_ln_g=conv_ln_g, conv_ln_b=conv_ln_b, gate_up=gate_up, gate_bias=gate_bias,
                   gla_norm_g=gla_norm_g, w_out=w_out, ln1_g=ln1_g, ln1_b=ln1_b, w_ff1=w_ff1, w_ff2=w_ff2,
                   ln2_g=ln2_g, ln2_b=ln2_b)
    moms = dict(meta_tokens=(m_meta_tokens, v_meta_tokens), ln_in_g=(m_ln_in_g, v_ln_in_g),
                ln_in_b=(m_ln_in_b, v_ln_in_b), w_in=(m_w_in, v_w_in), conv_w=(m_conv_w, v_conv_w),
                conv_b=(m_conv_b, v_conv_b), conv_ln_g=(m_conv_ln_g, v_conv_ln_g),
                conv_ln_b=(m_conv_ln_b, v_conv_ln_b), gate_up=(m_gate_up, v_gate_up),
                gate_bias=(m_gate_bias, v_gate_bias), gla_norm_g=(m_gla_norm_g, v_gla_norm_g),
                w_out=(m_w_out, v_w_out), ln1_g=(m_ln1_g, v_ln1_g), ln1_b=(m_ln1_b, v_ln1_b),
                w_ff1=(m_w_ff1, v_w_ff1), w_ff2=(m_w_ff2, v_w_ff2), ln2_g=(m_ln2_g, v_ln2_g),
                ln2_b=(m_ln2_b, v_ln2_b))
    names = list(weights)
    big_names = ("w_in", "w_out", "w_ff1", "w_ff2")
    delta, new_m, new_v = {}, {}, {}
    lin = lambda a: a.reshape(-1, 128)
    unlin = lambda a: a.reshape(sh_in, d).T[None]
    dl, mn, vn = _adamw(lin(w_in[0].T), lin(f_wint), lin(m_w_in[0].T), lin(v_w_in[0].T))
    delta["w_in"], new_m["w_in"], new_v["w_in"] = unlin(dl), unlin(mn), unlin(vn)
    for k in big_names[1:]:
        shp = weights[k].shape
        two = lambda a: a.reshape(shp[-2], shp[-1])
        dl, mn, vn = _adamw(two(weights[k]), two(grads[k]), two(moms[k][0]), two(moms[k][1]))
        delta[k], new_m[k], new_v[k] = dl.reshape(shp), mn.reshape(shp), vn.reshape(shp)
    small_names = [k for k in names if k not in big_names]
    sizes = [weights[k].size for k in small_names]
    total = sum(sizes)
    padded = -(-total // 1024) * 1024

    def flat(get):
        v = jnp.concatenate([get(k).reshape(-1) for k in small_names])
        return jnp.pad(v, (0, padded - total)).reshape(padded // 128, 128)

    ones = jnp.ones((padded - total,), F32)
    vflat = jnp.concatenate([jnp.concatenate([moms[k][1].reshape(-1) for k in small_names]), ones])
    dl, mn, vn = _adamw(flat(lambda k: weights[k]), flat(lambda k: grads[k]), flat(lambda k: moms[k][0]),
                        vflat.reshape(padded // 128, 128))
    pos = 0
    for k, sz in zip(small_names, sizes):
        shp = weights[k].shape
        delta[k] = dl.reshape(-1)[pos:pos + sz].reshape(shp)
        new_m[k] = mn.reshape(-1)[pos:pos + sz].reshape(shp)
        new_v[k] = vn.reshape(-1)[pos:pos + sz].reshape(shp)
        pos += sz

    return (loss, grad_x, *[grads[k] for k in names], *[delta[k] for k in names],
            *[new_m[k] for k in names], *[new_v[k] for k in names])
```

```python
import functools

import jax
import jax.numpy as jnp
from jax import lax
from jax.experimental import pallas as pl
from jax.experimental.pallas import tpu as pltpu

F32 = jnp.float32
BF16 = jnp.bfloat16

D_MODEL = 1024
N_META = 16
D_CONV = 512
CONV_WIDTH = 31
GLA_HEADS = 4
GLA_DV = 128
GLA_DK = 64
GLA_RANK = 16
GLA_TAU = 16.0
CHUNK = 64
D_FF = 4096
LN_EPS = 1e-5
ALPHA = 2.0 ** 0.25
D_IN = 2576
D_IN_PAD = 2688
PAD = CHUNK - N_META
HEAD = PAD + N_META
Q_SCALE = GLA_DK ** -0.5
ADAM_LR, ADAM_B1, ADAM_B2, ADAM_EPS, ADAM_WD, ADAM_STEP = 0.001, 0.9, 0.999, 1e-08, 0.01, 10
HALF = D_MODEL // 2
VMEM_LIMIT = 56 * 1024 * 1024
MESH = pl.DeviceIdType.MESH

C_VAL, C_GATE, C_Q, C_K, C_V, C_R, C_GD = 0, 512, 1024, 1280, 1536, 2048, 2560

SMALL_AT = {"loss": (0, 1, 1024), "ln_in_g": (1, 1, 1024), "ln_in_b": (2, 1, 1024), "ln1_g": (3, 1, 1024),
            "ln1_b": (4, 1, 1024), "ln2_g": (5, 1, 1024), "ln2_b": (6, 1, 1024), "conv_b": (7, 1, 512),
            "conv_ln_g": (8, 1, 512), "conv_ln_b": (9, 1, 512), "gate_bias": (10, 1, 256), "gla_norm_g": (11, 1, 128),
            "conv_w": (16, 32, 512), "gate_up": (48, 16, 256), "meta_tokens": (64, 16, 1024)}
SMALL_ROWS = 80


def _params(sem=None, **kw):
    return pltpu.CompilerParams(dimension_semantics=sem, vmem_limit_bytes=VMEM_LIMIT, **kw)


def _row_tile(tp):
    for t in (704, 352, 192, 64):
        if tp % t == 0:
            return t
    raise ValueError(tp)


def _reduce_tile(tp, big):
    for t in ((2112, 1056, 704) if big else (1056, 704)) + (352, 192, 64):
        if tp % t == 0:
            return t
    raise ValueError(tp)


def _sub_rows(tm):
    return [slice(0, tm)]


def _dot(a, b, dims, precision=None):
    return lax.dot_general(a, b, (dims, ((), ())), preferred_element_type=F32, precision=precision)


def _nn(a, b, **kw):
    return _dot(a, b, ((1,), (0,)), **kw)


def _nt(a, b, **kw):
    return _dot(a, b, ((1,), (1,)), **kw)


def _tn(a, b, **kw):
    return _dot(a, b, ((0,), (0,)), **kw)


def _sigmoid(x):
    return 1.0 / (1.0 + jnp.exp(-x))


def _log_sigmoid(z):
    return jnp.minimum(z, 0.0) - jnp.log(1.0 + jnp.exp(-jnp.abs(z)))


def _ln_stats(t):
    mu = jnp.mean(t, axis=-1, keepdims=True)
    d = t - mu
    var = jnp.mean(d * d, axis=-1, keepdims=True)
    rstd = lax.rsqrt(var + LN_EPS)
    return d * rstd, rstd


def _ln_bwd(dxhat, xhat, rstd):
    m1 = jnp.mean(dxhat, axis=-1, keepdims=True)
    m2 = jnp.mean(dxhat * xhat, axis=-1, keepdims=True)
    return rstd * (dxhat - m1 - xhat * m2)


def _mesh_pos():
    return lax.axis_index("x"), lax.axis_index("y"), lax.axis_index("c")


ANY = pl.BlockSpec(memory_space=pl.ANY)


def _gather_sems(n):
    return [pltpu.SemaphoreType.DMA((n, 3))] * 4


def _gather_steps(ins, outs, sems, ranges=None):
    n = len(ins)
    send, recv, fsend, frecv = sems
    ranges = ranges or [(0, ref.shape[0]) for ref in ins]
    x, y, c = _mesh_pos()
    me = 2 * x + y
    sibling = (x, y, 1 - c)
    chips = [(1 - x, y), (x, 1 - y), (1 - x, 1 - y)]
    chip_idx = [2 * px + py for px, py in chips]
    mine = [pl.ds(pl.multiple_of(r0 + c * (nr // 2), 16), nr // 2) for r0, nr in ranges]
    other = [pl.ds(pl.multiple_of(r0 + (1 - c) * (nr // 2), 16), nr // 2) for r0, nr in ranges]
    pairs = [(a, k) for a in range(n) for k in range(3)]

    def ici(a, k, slab):
        return pltpu.make_async_remote_copy(
            src_ref=ins[a].at[mine[a], :], dst_ref=outs[a].at[slab, mine[a], :],
            send_sem=send.at[a, k], recv_sem=recv.at[a, k], device_id=(*chips[k], c), device_id_type=MESH)

    def forward(a, k, rows):
        blk = outs[a].at[chip_idx[k], rows[a], :]
        return pltpu.make_async_remote_copy(
            src_ref=blk, dst_ref=blk, send_sem=fsend.at[a, k], recv_sem=frecv.at[a, k],
            device_id=sibling, device_id_type=MESH)

    def start():
        for a, k in pairs:
            ici(a, k, me).start()

    def relay():
        for a, k in pairs:
            ici(a, k, chip_idx[k]).wait_recv()
            forward(a, k, mine).start()

    def finish():
        for a, k in pairs:
            forward(a, k, other).wait_recv()
        for a, k in pairs:
            ici(a, k, me).wait_send()
            forward(a, k, mine).wait_send()

    return start, relay, finish


def _place_own(gathered, shards):
    chip = 2 * lax.axis_index("x") + lax.axis_index("y")
    return [lax.dynamic_update_slice(g, s[None], (chip, 0, 0)) for g, s in zip(gathered, shards)]


def _gather_small(small):
    def body(small_in, small_out, ssend, srecv):
        x, y, c = _mesh_pos()
        chips = [(1 - x, y), (x, 1 - y), (1 - x, 1 - y)]

        def small_copy(k, slot):
            return pltpu.make_async_remote_copy(
                src_ref=small_in, dst_ref=small_out.at[slot], send_sem=ssend.at[k], recv_sem=srecv.at[k],
                device_id=(*chips[k], c), device_id_type=MESH)

        for k in range(3):
            small_copy(k, 2 * x + y).start()
        for k, (px, py) in enumerate(chips):
            small_copy(k, 2 * px + py).wait_recv()
        for k in range(3):
            small_copy(k, 2 * x + y).wait_send()

    res = pl.pallas_call(
        body, name="gather_small", out_shape=jax.ShapeDtypeStruct((4,) + small.shape, small.dtype),
        in_specs=[ANY], out_specs=ANY,
        scratch_shapes=[pltpu.SemaphoreType.DMA((3,)), pltpu.SemaphoreType.DMA((3,))],
    )(small)
    return _place_own([res], [small])[0]


def _pair_swap_steps(ins, outs, sems):
    send, recv = sems
    x, y, c = _mesh_pos()
    cps = [pltpu.make_async_remote_copy(
        src_ref=ins[a], dst_ref=outs[a], send_sem=send.at[a], recv_sem=recv.at[a],
        device_id=(x, y, 1 - c), device_id_type=MESH) for a in range(len(ins))]

    def start():
        for cp in cps:
            cp.start()

    def finish():
        for cp in cps:
            cp.wait()

    return start, finish


def _pair_exchange_steps(ins, outs, sems):
    send, recv = sems
    x, y, c = _mesh_pos()
    other = pl.ds(pl.multiple_of((1 - c) * HALF, 128), HALF)
    cps = [pltpu.make_async_remote_copy(
        src_ref=ins[a].at[:, :, other], dst_ref=outs[a], send_sem=send.at[a], recv_sem=recv.at[a],
        device_id=(x, y, 1 - c), device_id_type=MESH) for a in range(len(ins))]

    def start():
        for cp in cps:
            cp.start()

    def finish():
        for cp in cps:
            cp.wait()

    return start, finish


def _pair_exchange_shapes(grads):
    return [jax.ShapeDtypeStruct(g.shape[:2] + (HALF,), g.dtype) for g in grads]


def _pair_exchange(grads):
    n = len(grads)

    def body(*refs):
        start, finish = _pair_exchange_steps(refs[:n], refs[n:2 * n], refs[2 * n:])
        start()
        finish()

    return pl.pallas_call(
        body, name="grad_pair_exchange", out_shape=_pair_exchange_shapes(grads),
        in_specs=[ANY] * n, out_specs=[ANY] * n,
        scratch_shapes=[pltpu.SemaphoreType.DMA((n,)), pltpu.SemaphoreType.DMA((n,))],
    )(*grads)


NO_FUSE = ()


def _fuse_plan(kind, items):
    n = len(items)
    if kind == "gather":
        shards = [it[0] for it in items]
        bufs = [it[3] for it in items if it[3] is not None]
        alias, b = {}, 0
        for a, it in enumerate(items):
            if it[3] is not None:
                alias[n + b] = a
                b += 1
        ranges = [(it[1], it[2]) for it in items]
        return (shards + bufs, [jax.ShapeDtypeStruct((4,) + s.shape, s.dtype) for s in shards], alias, _gather_sems(n),
                lambda i, o, s: _gather_steps(i[:n], o, s, ranges))
    if kind == "exchange":
        return list(items), _chip_exchange_shapes(items), {}, _chip_exchange_sems(n), _chip_exchange_steps
    pair_sems = [pltpu.SemaphoreType.DMA((n,)), pltpu.SemaphoreType.DMA((n,))]
    if kind == "swap":
        return (list(items), [jax.ShapeDtypeStruct(h.shape, h.dtype) for h in items], {}, pair_sems, _pair_swap_steps)
    return list(items), _pair_exchange_shapes(items), {}, pair_sems, _pair_exchange_steps


def _fused_call(body, fuse, *, name, grid, in_specs, out_specs, out_shape, scratch_shapes, operands):
    plans = [_fuse_plan(kind, list(items)) for kind, items in fuse if len(items)]
    n_in, n_out, n_s = len(in_specs), len(out_shape), len(scratch_shapes)
    comm = [a for p in plans for a in p[0]]
    shapes = [s for p in plans for s in p[1]]
    nc, no = len(comm), len(shapes)
    aliases, i_at, o_at = {}, n_in, n_out
    for p in plans:
        for i, o in p[2].items():
            aliases[i_at + i] = o_at + o
        i_at, o_at = i_at + len(p[0]), o_at + len(p[1])

    def wrapped(*refs):
        o0 = n_in + nc
        s0 = o0 + n_out + no
        i_at, o_at, sem_at, steps = n_in, o0 + n_out, s0 + n_s, []
        for p in plans:
            steps.append(p[4](refs[i_at:i_at + len(p[0])], refs[o_at:o_at + len(p[1])], refs[sem_at:sem_at + len(p[3])]))
            i_at, o_at, sem_at = i_at + len(p[0]), o_at + len(p[1]), sem_at + len(p[3])
        first, last = _grid_ends(grid)
        for st in steps:
            pl.when(first)(st[0])
        for st in steps:
            for mid in st[1:-1]:
                pl.when(last)(mid)
        body(*refs[:n_in], *refs[o0:o0 + n_out], *refs[s0:s0 + n_s])
        for st in steps:
            pl.when(last)(st[-1])

    res = pl.pallas_call(
        wrapped if plans else body, name=name, grid=grid, in_specs=list(in_specs) + [ANY] * nc,
        out_specs=list(out_specs) + [ANY] * no, out_shape=list(out_shape) + shapes,
        scratch_shapes=list(scratch_shapes) + [s for p in plans for s in p[3]], input_output_aliases=aliases,
        compiler_params=_params(("arbitrary",) * len(grid)))(*operands, *comm)
    outs, got, results = list(res[:n_out]), list(res[n_out:]), []
    for p in plans:
        results.append(got[:len(p[1])])
        got = got[len(p[1]):]
    return outs, results


def _chip_exchange_sems(n):
    return [pltpu.SemaphoreType.DMA((n, 3))] * 2


def _chip_exchange_shapes(parts):
    return [jax.ShapeDtypeStruct((3,) + p.shape[1:], p.dtype) for p in parts]


def _chip_exchange_steps(ins, outs, sems):
    send, recv = sems
    x, y, c = _mesh_pos()
    chips = [(1 - x, y), (x, 1 - y), (1 - x, 1 - y)]
    cps = [pltpu.make_async_remote_copy(
        src_ref=ins[a].at[2 * px + py], dst_ref=outs[a].at[k], send_sem=send.at[a, k], recv_sem=recv.at[a, k],
        device_id=(px, py, c), device_id_type=MESH) for a in range(len(ins)) for k, (px, py) in enumerate(chips)]

    def start():
        for cp in cps:
            cp.start()

    def finish():
        for cp in cps:
            cp.wait()

    return start, finish


def _chip_exchange(parts):
    n = len(parts)

    def body(*refs):
        start, finish = _chip_exchange_steps(refs[:n], refs[n:2 * n], refs[2 * n:])
        start()
        finish()

    return pl.pallas_call(
        body, name="grad_chip_exchange", out_shape=_chip_exchange_shapes(parts),
        in_specs=[ANY] * n, out_specs=[ANY] * n, scratch_shapes=_chip_exchange_sems(n),
    )(*parts)


def _pair_swap(halves):
    n = len(halves)

    def body(*refs):
        start, finish = _pair_swap_steps(refs[:n], refs[n:2 * n], refs[2 * n:])
        start()
        finish()

    return pl.pallas_call(
        body, name="grad_pair_swap",
        out_shape=[jax.ShapeDtypeStruct(h.shape, h.dtype) for h in halves],
        in_specs=[ANY] * n, out_specs=[ANY] * n,
        scratch_shapes=[pltpu.SemaphoreType.DMA((n,)), pltpu.SemaphoreType.DMA((n,))],
    )(*halves)


def _allgather_small(pack):
    m_per, ncol = pack.shape

    def body(x_ref, out_ref, send_sems, recv_sems, local_sem):
        x, y, c = _mesh_pos()
        me, sibling = (x, y, c), (x, y, 1 - c)
        chips = [(1 - x, y), (x, 1 - y), (1 - x, 1 - y)]

        def rows(px, py, pc):
            return out_ref.at[pl.ds(pl.multiple_of((4 * px + 2 * py + pc) * m_per, 8), m_per), :]

        def copy(k, block, to, src=None):
            return pltpu.make_async_remote_copy(
                src_ref=rows(*block) if src is None else src, dst_ref=rows(*block),
                send_sem=send_sems.at[k], recv_sem=recv_sems.at[k], device_id=to, device_id_type=MESH)

        mine = pltpu.make_async_copy(x_ref, rows(*me), local_sem)
        mine.start()
        first = [copy(0, me, sibling, src=x_ref)]
        first += [copy(1 + j, me, (*chip, c), src=x_ref) for j, chip in enumerate(chips)]
        for cp in first:
            cp.start()
        passed = [copy(4 + j, (*chip, c), sibling) for j, chip in enumerate(chips)]
        for j, chip in enumerate(chips):
            copy(1 + j, (*chip, c), me).wait_recv()
            passed[j].start()
        copy(0, sibling, me).wait_recv()
        for j, chip in enumerate(chips):
            copy(4 + j, (*chip, 1 - c), me).wait_recv()
        for cp in first + passed:
            cp.wait_send()
        mine.wait()

    return pl.pallas_call(
        body, name="allgather_small",
        out_shape=jax.ShapeDtypeStruct((8 * m_per, ncol), pack.dtype),
        in_specs=[pl.BlockSpec(memory_space=pltpu.VMEM)],
        out_specs=pl.BlockSpec(memory_space=pltpu.VMEM),
        scratch_shapes=[pltpu.SemaphoreType.DMA((7,)), pltpu.SemaphoreType.DMA((7,)), pltpu.SemaphoreType.DMA],
    )(pack)


def _add_pair(g, got, c_arr, splits=None):
    _, rows, _ = g.shape
    splits = splits or [rows]

    def body(c_ref, g_ref, r_ref, *o_refs):
        at = 0
        for o_ref, n in zip(o_refs, splits):
            o_ref[...] = (g_ref[:, at:at + n, :] + r_ref[:, at:at + n, :]).astype(BF16)
            at += n

    return pl.pallas_call(
        body, name="grad_add_pair", out_shape=[jax.ShapeDtypeStruct((4, n, HALF), BF16) for n in splits],
        grid_spec=pltpu.PrefetchScalarGridSpec(
            num_scalar_prefetch=1, grid=(4,),
            in_specs=[pl.BlockSpec((1, rows, HALF), lambda j, c: (j, 0, c[0])),
                      pl.BlockSpec((1, rows, HALF), lambda j, c: (j, 0, 0))],
            out_specs=[pl.BlockSpec((1, n, HALF), lambda j, c: (j, 0, 0)) for n in splits]),
        compiler_params=_params(("arbitrary",)),
    )(c_arr, g, got)


def _add_chips(g, pair_got, chip_gots, pos_arr):
    _, rows, _ = g.shape

    def body(pos_ref, g_ref, p_ref, *refs):
        o_ref, at = refs[-1], 0
        for r_ref in refs[:-1]:
            n = r_ref.shape[1]
            own = g_ref[0, at:at + n, :] + p_ref[0, at:at + n, :]
            o_ref[at:at + n, :] = ((own + r_ref[0].astype(F32)) + r_ref[1].astype(F32)) + r_ref[2].astype(F32)
            at += n

    return pl.pallas_call(
        body, name="grad_add_chips", out_shape=jax.ShapeDtypeStruct((rows, HALF), F32),
        grid_spec=pltpu.PrefetchScalarGridSpec(
            num_scalar_prefetch=1, grid=(1,),
            in_specs=[pl.BlockSpec((1, rows, HALF), lambda i, p: (p[0], 0, p[1])),
                      pl.BlockSpec((1, rows, HALF), lambda i, p: (p[0], 0, 0))]
            + [pl.BlockSpec(t.shape, lambda i, p: (0, 0, 0)) for t in chip_gots],
            out_specs=pl.BlockSpec((rows, HALF), lambda i, p: (0, 0))),
        compiler_params=_params(("arbitrary",)),
    )(pos_arr, g, pair_got, *chip_gots)


def _pack_small(acc2, acc1, lvec, cacc, gvec, dcw, dgup, dmeta):
    bsz = lvec.shape[0]

    def body(a2_ref, a1_ref, lv_ref, ca_ref, gv_ref, cw_ref, gu_ref, dm_ref, o_ref):
        def put(name, val):
            r0, nr, nl = SMALL_AT[name]
            o_ref[r0:r0 + nr, 0:nl] = val

        over_b = lambda f: functools.reduce(lambda a, b: a + b, [f(b) for b in range(bsz)])
        o_ref[...] = jnp.zeros_like(o_ref)
        put("loss", a2_ref[0:1, :])
        put("ln2_g", a2_ref[1:2, :])
        put("ln2_b", a2_ref[2:3, :])
        put("ln1_g", a1_ref[0:1, :])
        put("ln1_b", a1_ref[1:2, :])
        put("ln_in_g", over_b(lambda b: lv_ref[b, 0:1, :]))
        put("ln_in_b", over_b(lambda b: lv_ref[b, 1:2, :]))
        put("conv_b", ca_ref[0:1, :])
        put("conv_ln_g", ca_ref[1:2, :])
        put("conv_ln_b", ca_ref[2:3, :])
        put("gate_bias", over_b(lambda b: gv_ref[b, 0:1, :]))
        put("gla_norm_g", over_b(lambda b: gv_ref[b, 1:2, 0:128] + gv_ref[b, 1:2, 128:256]))
        put("conv_w", over_b(lambda b: cw_ref[b]))
        put("gate_up", over_b(lambda b: gu_ref[b, 0:GLA_RANK, :]))
        put("meta_tokens", over_b(lambda b: dm_ref[b]))

    return pl.pallas_call(
        body, name="pack_small", out_shape=jax.ShapeDtypeStruct((SMALL_ROWS, D_MODEL), F32),
    )(acc2, acc1, lvec, cacc, gvec, dcw, dgup, dmeta)


def _sum_small(gathered):
    def body(g_ref, o_ref, loss_ref):
        acc = g_ref[0:SMALL_ROWS, :]
        for d in range(1, 8):
            acc = acc + g_ref[d * SMALL_ROWS:(d + 1) * SMALL_ROWS, :]
        o_ref[...] = acc
        loss_ref[...] = jnp.sum(acc[0:1, :], axis=1, keepdims=True)

    return pl.pallas_call(
        body, name="sum_small",
        out_shape=[jax.ShapeDtypeStruct((SMALL_ROWS, D_MODEL), F32), jax.ShapeDtypeStruct((1, 1), F32)],
    )(gathered)


def _adamw_math(w, g, m, v):
    c1 = 1.0 - ADAM_B1 ** ADAM_STEP
    c2 = 1.0 - ADAM_B2 ** ADAM_STEP
    mn = ADAM_B1 * m + (1.0 - ADAM_B1) * g
    vn = ADAM_B2 * v + (1.0 - ADAM_B2) * (g * g)
    return -ADAM_LR * ((mn / c1) / (jnp.sqrt(vn / c2) + ADAM_EPS) + ADAM_WD * w), mn, vn


def _adamw_small(red, sharded_grads, params):
    names = list(params)
    sharded = [k for k in names if k in sharded_grads]
    n, ns = len(names), len(sharded)

    def body(*refs):
        red_ref, sg_refs, p_refs, o_refs = refs[0], refs[1:1 + ns], refs[1 + ns:1 + ns + 3 * n], refs[1 + ns + 3 * n:]
        for i, k in enumerate(names):
            w_ref, m_ref, v_ref = p_refs[3 * i:3 * i + 3]
            if k in sharded:
                g = sg_refs[sharded.index(k)][...]
            else:
                r0, nr, nl = SMALL_AT[k]
                g = red_ref[r0:r0 + nr, 0:nl]
            dl, mn, vn = _adamw_math(w_ref[...], g, m_ref[...], v_ref[...])
            for o_ref, val in zip(o_refs[4 * i:4 * i + 4], (g, dl, mn, vn)):
                o_ref[...] = val

    flat = [a for k in names for a in params[k]]
    res = pl.pallas_call(
        body, name="adamw_small",
        out_shape=[jax.ShapeDtypeStruct(params[k][0].shape, F32) for k in names for _ in range(4)],
    )(red, *[sharded_grads[k] for k in sharded], *flat)
    return {k: tuple(res[4 * i:4 * i + 4]) for i, k in enumerate(names)}


def _adamw(w, g, m, v):
    rows, cols = w.shape
    tr = 256 if rows % 256 == 0 else rows

    def body(w_ref, g_ref, m_ref, v_ref, d_ref, mo_ref, vo_ref):
        d_ref[...], mo_ref[...], vo_ref[...] = _adamw_math(w_ref[...], g_ref[...], m_ref[...], v_ref[...])

    spec = pl.BlockSpec((tr, cols), lambda i: (i, 0))
    return pl.pallas_call(
        body, name="adamw", out_shape=[jax.ShapeDtypeStruct(w.shape, F32)] * 3,
        grid=(rows // tr,), in_specs=[spec] * 4, out_specs=[spec] * 3,
        compiler_params=_params(("parallel",)),
    )(w, g, m, v)


def _ln_in_fwd(x, meta, g, b, fuse=NO_FUSE):
    bsz, s, d = x.shape
    tp = s + HEAD
    nh = 2
    sh = s // nh
    rc = min(256, sh)

    def body(x_ref, meta_ref, g_ref, b_ref, s0_ref, s0b_ref):
        h = pl.program_id(1)
        gg, bb = g_ref[...], b_ref[...]

        @pl.when(h == 0)
        def _():
            s0_ref[0, 0:PAD, :] = jnp.zeros((PAD, d), F32)
            s0b_ref[0, 0:PAD, :] = jnp.zeros((PAD, d), BF16)
            mh, _ = _ln_stats(meta_ref[...])
            mv = mh * gg + bb
            s0_ref[0, PAD:HEAD, :] = mv
            s0b_ref[0, PAD:HEAD, :] = mv.astype(BF16)

        def step(i, carry):
            src = pl.ds(pl.multiple_of(i * rc, rc), rc)
            dst = pl.ds(pl.multiple_of(HEAD + h * sh + i * rc, 64), rc)
            xh, _ = _ln_stats(x_ref[0, src, :])
            val = xh * gg + bb
            s0_ref[0, dst, :] = val
            s0b_ref[0, dst, :] = val.astype(BF16)
            return carry

        lax.fori_loop(0, sh // rc, step, 0)

    full = lambda bi, hi: (bi, 0, 0)
    (s0, s0b), got = _fused_call(
        body, fuse, name="ln_in_fwd",
        out_shape=[jax.ShapeDtypeStruct((bsz, tp, d), F32), jax.ShapeDtypeStruct((bsz, tp, d), BF16)],
        grid=(bsz, nh),
        in_specs=[pl.BlockSpec((1, sh, d), lambda bi, hi: (bi, hi, 0)),
                  pl.BlockSpec((N_META, d), lambda bi, hi: (0, 0)),
                  pl.BlockSpec((1, d), lambda bi, hi: (0, 0)),
                  pl.BlockSpec((1, d), lambda bi, hi: (0, 0))],
        out_specs=[pl.BlockSpec((1, tp, d), full)] * 2, scratch_shapes=[], operands=(x, meta, g, b))
    return s0, s0b, got


def _in_proj(s0b, w_int, fuse=NO_FUSE):
    r, d = s0b.shape
    tm = _row_tile(r)

    def body(a_ref, w_ref, o_ref):
        o_ref[...] = _nt(a_ref[...], w_ref[...])

    (u,), got = _fused_call(
        body, fuse, name="in_proj", out_shape=[jax.ShapeDtypeStruct((r, D_IN_PAD), F32)],
        grid=(r // tm,),
        in_specs=[pl.BlockSpec((tm, d), lambda i: (i, 0)), pl.BlockSpec((D_IN_PAD, d), lambda i: (0, 0))],
        out_specs=[pl.BlockSpec((tm, D_IN_PAD), lambda i: (i, 0))], scratch_shapes=[], operands=(s0b, w_int))
    return u, got


def _conv_fwd(u, conv_w, conv_b, fuse=NO_FUSE):
    bsz, tp, _ = u.shape
    nchunk = tp // CHUNK
    win = CHUNK + 32
    nct = D_CONV // 128

    def body(cv_ref, cg_ref, w_ref, cb_ref, hc_ref, h_scr, win_scr):
        h_scr[0:32, :] = jnp.zeros((32, 128), F32)
        h_scr[32:32 + tp, :] = cv_ref[0] * _sigmoid(cg_ref[0])
        cb = cb_ref[...]

        def step(n, carry):
            r0 = pl.multiple_of(n * CHUNK, CHUNK)
            win_scr[...] = h_scr[pl.ds(r0, win), :]
            acc = jnp.zeros((CHUNK, 128), F32)
            for j in range(CONV_WIDTH):
                acc = acc + w_ref[j:j + 1, :] * win_scr[2 + j:2 + j + CHUNK, :]
            hc_ref[0, pl.ds(r0, CHUNK), :] = acc + cb
            return carry

        lax.fori_loop(0, nchunk, step, 0)

    (hc,), got = _fused_call(
        body, fuse, name="conv_fwd", out_shape=[jax.ShapeDtypeStruct((bsz, tp, D_CONV), F32)],
        grid=(bsz, nct),
        in_specs=[pl.BlockSpec((1, tp, 128), lambda bi, t: (bi, 0, C_VAL // 128 + t)),
                  pl.BlockSpec((1, tp, 128), lambda bi, t: (bi, 0, C_GATE // 128 + t)),
                  pl.BlockSpec((32, 128), lambda bi, t: (0, t)),
                  pl.BlockSpec((1, 128), lambda bi, t: (0, t))],
        out_specs=[pl.BlockSpec((1, tp, 128), lambda bi, t: (bi, 0, t))],
        scratch_shapes=[pltpu.VMEM((tp + 32, 128), F32), pltpu.VMEM((win, 128), F32)],
        operands=(u, u, conv_w, conv_b))
    return hc, got


def _gla_group(nchunk):
    return 11 if nchunk % 11 == 0 else nchunk


def _bdot(a, b, ca, cb, precision=None):
    return lax.dot_general(a, b, (((ca,), (cb,)), ((0,), (0,))), preferred_element_type=F32, precision=precision)


def _bnn(a, b, **kw):
    return _bdot(a, b, 2, 1, **kw)


def _bnt(a, b, **kw):
    return _bdot(a, b, 2, 2, **kw)


def _gla_consts(nb):
    row = lax.broadcasted_iota(jnp.int32, (nb, CHUNK, CHUNK), 1)
    col = lax.broadcasted_iota(jnp.int32, (nb, CHUNK, CHUNK), 2)
    lane = lax.broadcasted_iota(jnp.int32, (1, 1, 128), 2)
    return row >= col, row <= col, [lane < GLA_DK, lane >= GLA_DK]


def _gla_group_terms(g, nb, q_ref, k_ref, gd_ref, gup_ref, gb_ref, tril):
    m = nb * CHUNK
    rows = pl.ds(pl.multiple_of(g * m, CHUNK), m)
    z = _nn(gd_ref[0, rows, :].astype(BF16), gup_ref[...]) + gb_ref[...]
    valid = g * m + lax.broadcasted_iota(jnp.int32, (m, 1), 0) >= PAD
    lg = jnp.where(valid, _log_sigmoid(z) * (1.0 / GLA_TAU), 0.0)
    bcum = _bnn(tril.astype(F32), lg.reshape(nb, CHUNK, 128), precision=lax.Precision.HIGHEST)
    blast = bcum[:, CHUNK - 1:CHUNK, :]
    eb = jnp.exp(bcum)
    enb = jnp.exp(-bcum)
    erest = jnp.exp(blast - bcum)
    q = (q_ref[0, rows, :] * Q_SCALE).reshape(nb, CHUNK, 128)
    k = k_ref[0, rows, :].reshape(nb, CHUNK, 128)
    return rows, valid, z, eb, enb, erest, jnp.exp(blast), q * eb, k * enb, k * erest


def _grid_ends(grid):
    ids = [pl.program_id(i) for i in range(len(grid))]
    first = functools.reduce(jnp.logical_and, [i == 0 for i in ids])
    last = functools.reduce(jnp.logical_and, [i == g - 1 for i, g in zip(ids, grid)])
    return first, last


def _gla_fwd(u, gup, gbias, gnorm, fuse=NO_FUSE):
    bsz, tp, _ = u.shape
    nchunk = tp // CHUNK
    nb = _gla_group(nchunk)

    def body(q_ref, k_ref, v_ref, r_ref, gd_ref, gup_ref, gb_ref, gn_ref, out_ref, o_ref, st_ref, s_scr):
        tril, _, hmask = _gla_consts(nb)
        s_scr[...] = jnp.zeros_like(s_scr)
        gn = gn_ref[...]

        def group(g, carry):
            rows, _, _, _, _, _, dec, qe, ke, kd = _gla_group_terms(g, nb, q_ref, k_ref, gd_ref, gup_ref, gb_ref, tril)
            keb, kdb = ke.astype(BF16), kd.astype(BF16)
            for h in range(2):
                cols = slice(h * GLA_DV, (h + 1) * GLA_DV)
                qh = jnp.where(hmask[h], qe, 0.0).astype(BF16)
                vh = v_ref[0, rows, cols].astype(BF16).reshape(nb, CHUNK, GLA_DV)
                a = jnp.where(tril, _bnt(qh, keb), 0.0).astype(BF16)
                st = s_scr[h]
                sts = []
                for n in range(nb):
                    st_ref[0, h, g * nb + n] = st
                    sts.append(st.astype(BF16))
                    st = dec[n] * st + _tn(vh[n], kdb[n])
                s_scr[h] = st
                o = (_bnn(a, vh) + _bnt(qh, jnp.stack(sts))).reshape(nb * CHUNK, GLA_DV)
                o_ref[0, rows, cols] = o
                rms = lax.rsqrt(jnp.mean(o * o, axis=-1, keepdims=True) + LN_EPS)
                rh = r_ref[0, rows, cols]
                out_ref[0, rows, cols] = (o * rms * gn * (rh * _sigmoid(rh))).astype(BF16)
            return carry

        lax.fori_loop(0, nchunk // nb, group, 0)

    res, got = _fused_call(
        body, fuse, name="gla_fwd",
        out_shape=[jax.ShapeDtypeStruct((bsz, tp, 512), BF16), jax.ShapeDtypeStruct((bsz, tp, 512), F32),
                   jax.ShapeDtypeStruct((bsz, GLA_HEADS, nchunk, GLA_DV, 128), F32)],
        grid=(bsz, 2),
        in_specs=[pl.BlockSpec((1, tp, 128), lambda bi, p: (bi, 0, C_Q // 128 + p)),
                  pl.BlockSpec((1, tp, 128), lambda bi, p: (bi, 0, C_K // 128 + p)),
                  pl.BlockSpec((1, tp, 256), lambda bi, p: (bi, 0, C_V // 256 + p)),
                  pl.BlockSpec((1, tp, 256), lambda bi, p: (bi, 0, C_R // 256 + p)),
                  pl.BlockSpec((1, tp, 128), lambda bi, p: (bi, 0, C_GD // 128)),
                  pl.BlockSpec((128, 128), lambda bi, p: (0, p)),
                  pl.BlockSpec((1, 128), lambda bi, p: (0, p)),
                  pl.BlockSpec((1, 128), lambda bi, p: (0, 0))],
        out_specs=[pl.BlockSpec((1, tp, 256), lambda bi, p: (bi, 0, p)),
                   pl.BlockSpec((1, tp, 256), lambda bi, p: (bi, 0, p)),
                   pl.BlockSpec((1, 2, nchunk, GLA_DV, 128), lambda bi, p: (bi, p, 0, 0, 0))],
        scratch_shapes=[pltpu.VMEM((2, GLA_DV, 128), F32)],
        operands=(u, u, u, u, u, gup, gbias, gnorm))
    return res[0], res[1], res[2], got


def _out_proj_ln1(hc, gla_out, w_out, s0, cg, cb, g1, b1, fuse=NO_FUSE):
    r, d = s0.shape
    tm = _row_tile(r)

    def body(hc_ref, a_ref, w_ref, s0_ref, cg_ref, cb_ref, g_ref, b_ref, co_ref, xh_ref, rstd_ref, s1b_ref):
        for rs in _sub_rows(tm):
            xc, _ = _ln_stats(hc_ref[rs, :])
            nv = xc * cg_ref[...] + cb_ref[...]
            co = (nv * _sigmoid(nv)).astype(BF16)
            co_ref[rs, :] = co
            mix = _nn(co, w_ref[0:D_CONV, :]) + _nn(a_ref[rs, :], w_ref[D_CONV:, :])
            xh, rstd = _ln_stats(ALPHA * s0_ref[rs, :] + mix)
            xh_ref[rs, :] = xh
            rstd_ref[rs, :] = rstd
            s1b_ref[rs, :] = (xh * g_ref[...] + b_ref[...]).astype(BF16)

    row = lambda n: pl.BlockSpec((tm, n), lambda i: (i, 0))
    vec = lambda n: pl.BlockSpec((1, n), lambda i: (0, 0))
    res, got = _fused_call(
        body, fuse, name="out_proj_ln1",
        out_shape=[jax.ShapeDtypeStruct((r, D_CONV), BF16), jax.ShapeDtypeStruct((r, d), F32),
                   jax.ShapeDtypeStruct((r, 1), F32), jax.ShapeDtypeStruct((r, d), BF16)],
        grid=(r // tm,),
        in_specs=[row(D_CONV), row(512), pl.BlockSpec((d, d), lambda i: (0, 0)), row(d),
                  vec(D_CONV), vec(D_CONV), vec(d), vec(d)],
        out_specs=[row(D_CONV), row(d), row(1), row(d)], scratch_shapes=[],
        operands=(hc, gla_out, w_out, s0, cg, cb, g1, b1))
    return res[0], res[1], res[2], res[3], got


def _ffn1(s1b, w1t, fuse=NO_FUSE):
    r, d = s1b.shape
    tm = _row_tile(r)

    def body(a_ref, w_ref, o_ref):
        o_ref[...] = jnp.maximum(_nt(a_ref[...], w_ref[...]), 0.0).astype(BF16)

    (ra,), got = _fused_call(
        body, fuse, name="ffn1", out_shape=[jax.ShapeDtypeStruct((r, D_FF), BF16)], grid=(r // tm,),
        in_specs=[pl.BlockSpec((tm, d), lambda i: (i, 0)), pl.BlockSpec((D_FF, d), lambda i: (0, 0))],
        out_specs=[pl.BlockSpec((tm, D_FF), lambda i: (i, 0))], scratch_shapes=[], operands=(s1b, w1t))
    return ra, got


def _ffn2_ln2_loss(ra, w2, xhat1, g1, b1, g2, b2, tgt, tp):
    r, d = xhat1.shape
    tm = _row_tile(tp)
    per = tp // tm

    def body(ra_ref, w_ref, xh1_ref, g1_ref, b1_ref, g2_ref, b2_ref, tgt_ref, dt_ref, dtb_ref, acc_ref, t_ref, sem):
        i = pl.program_id(0)
        b, j = i // per, i % per

        @pl.when(i == 0)
        def _():
            acc_ref[...] = jnp.zeros_like(acc_ref)

        head_copy = pltpu.make_async_copy(tgt_ref.at[b, pl.ds(0, tm - HEAD), :], t_ref.at[pl.ds(HEAD, tm - HEAD), :], sem)
        body_copy = pltpu.make_async_copy(
            tgt_ref.at[b, pl.ds(pl.multiple_of(jnp.maximum(j * tm - HEAD, 0), 64), tm), :], t_ref, sem)

        @pl.when(j == 0)
        def _():
            t_ref[0:HEAD, :] = jnp.zeros((HEAD, d), F32)
            head_copy.start()

        pl.when(j > 0)(body_copy.start)

        sums = [jnp.zeros((1, d), F32)] * 3
        for rs in _sub_rows(tm):
            rb = ra_ref[rs, :]
            f = _nn(rb * rb, w_ref[...])
            if rs.start == 0:
                pl.when(j == 0)(head_copy.wait)
                pl.when(j > 0)(body_copy.wait)
            s1 = xh1_ref[rs, :] * g1_ref[...] + b1_ref[...]
            xh2, rstd2 = _ln_stats(ALPHA * s1 + f)
            y = xh2 * g2_ref[...] + b2_ref[...]
            rowid = (i % per) * tm + rs.start + lax.broadcasted_iota(jnp.int32, (rs.stop - rs.start, 1), 0)
            e = jnp.where(rowid >= HEAD, y - t_ref[rs, :], 0.0)
            dy = e * (1.0 / d)
            dt2 = _ln_bwd(dy * g2_ref[...], xh2, rstd2)
            dt_ref[rs, :] = dt2
            dtb_ref[rs, :] = dt2.astype(BF16)
            sums = [sums[0] + (0.5 / d) * jnp.sum(e * e, axis=0, keepdims=True),
                    sums[1] + jnp.sum(dy * xh2, axis=0, keepdims=True), sums[2] + jnp.sum(dy, axis=0, keepdims=True)]
        for k in range(3):
            acc_ref[k:k + 1, :] += sums[k]

    row = lambda n: pl.BlockSpec((tm, n), lambda i: (i, 0))
    vec = pl.BlockSpec((1, d), lambda i: (0, 0))
    return pl.pallas_call(
        body, name="ffn2_ln2_loss",
        out_shape=[jax.ShapeDtypeStruct((r, d), F32), jax.ShapeDtypeStruct((r, d), BF16),
                   jax.ShapeDtypeStruct((8, d), F32)],
        grid=(r // tm,),
        in_specs=[row(D_FF), pl.BlockSpec((D_FF, d), lambda i: (0, 0)), row(d), vec, vec, vec, vec, ANY],
        out_specs=[row(d), row(d), pl.BlockSpec((8, d), lambda i: (0, 0))],
        scratch_shapes=[pltpu.VMEM((tm, d), F32), pltpu.SemaphoreType.DMA],
        compiler_params=_params(("arbitrary",)),
    )(ra, w2, xhat1, g1, b1, g2, b2, tgt)


def _ffn_bwd_da(dt2b, w2, ra):
    r, d = dt2b.shape
    tm = _row_tile(r)

    def body(g_ref, w_ref, ra_ref, o_ref):
        o_ref[...] = (_nt(g_ref[...], w_ref[...]) * (2.0 * ra_ref[...].astype(F32))).astype(BF16)

    return pl.pallas_call(
        body, name="ffn_bwd_da", out_shape=jax.ShapeDtypeStruct((r, D_FF), BF16),
        grid=(r // tm,),
        in_specs=[pl.BlockSpec((tm, d), lambda i: (i, 0)), pl.BlockSpec((D_FF, d), lambda i: (0, 0)),
                  pl.BlockSpec((tm, D_FF), lambda i: (i, 0))],
        out_specs=pl.BlockSpec((tm, D_FF), lambda i: (i, 0)),
        compiler_params=_params(("parallel",)),
    )(dt2b, w2, ra)


def _ffn_bwd_ln1(da, w1t, dt2, xhat1, rstd1, g1):
    r, d = dt2.shape
    tm = _row_tile(r)

    def body(da_ref, w_ref, dt2_ref, xh_ref, rstd_ref, g_ref, dt_ref, dtb_ref, acc_ref):
        @pl.when(pl.program_id(0) == 0)
        def _():
            acc_ref[...] = jnp.zeros_like(acc_ref)

        sums = [jnp.zeros((1, d), F32)] * 2
        for rs in _sub_rows(tm):
            ds1 = ALPHA * dt2_ref[rs, :] + _nn(da_ref[rs, :], w_ref[...])
            xh = xh_ref[rs, :]
            dt1 = _ln_bwd(ds1 * g_ref[...], xh, rstd_ref[rs, :])
            dt_ref[rs, :] = dt1
            dtb_ref[rs, :] = dt1.astype(BF16)
            sums = [sums[0] + jnp.sum(ds1 * xh, axis=0, keepdims=True), sums[1] + jnp.sum(ds1, axis=0, keepdims=True)]
        for k in range(2):
            acc_ref[k:k + 1, :] += sums[k]

    row = lambda n: pl.BlockSpec((tm, n), lambda i: (i, 0))
    return pl.pallas_call(
        body, name="ffn_bwd_ln1",
        out_shape=[jax.ShapeDtypeStruct((r, d), F32), jax.ShapeDtypeStruct((r, d), BF16),
                   jax.ShapeDtypeStruct((8, d), F32)],
        grid=(r // tm,),
        in_specs=[row(D_FF), pl.BlockSpec((D_FF, d), lambda i: (0, 0)), row(d), row(d), row(1),
                  pl.BlockSpec((1, d), lambda i: (0, 0))],
        out_specs=[row(d), row(d), pl.BlockSpec((8, d), lambda i: (0, 0))],
        compiler_params=_params(("arbitrary",)),
    )(da, w1t, dt2, xhat1, rstd1, g1)


def _matmul_tn(lhs, rhs, bm, square_lhs=False, name="matmul_tn", fuse=NO_FUSE):
    r, m = lhs.shape
    n = rhs.shape[1]
    tk = _reduce_tile(r, True)

    def body(a_ref, b_ref, o_ref):
        @pl.when(pl.program_id(1) == 0)
        def _():
            o_ref[...] = jnp.zeros_like(o_ref)

        a = a_ref[...]
        if square_lhs:
            a = a * a
        o_ref[...] += _tn(a, b_ref[...])

    (out,), got = _fused_call(
        body, fuse, name=name, out_shape=[jax.ShapeDtypeStruct((m, n), F32)], grid=(m // bm, r // tk),
        in_specs=[pl.BlockSpec((tk, bm), lambda i, k: (k, i)), pl.BlockSpec((tk, n), lambda i, k: (k, 0))],
        out_specs=[pl.BlockSpec((bm, n), lambda i, k: (i, 0))], scratch_shapes=[], operands=(lhs, rhs))
    return out, got


def _grad_w_out(conv_of, gla_of, dt1b):
    r, n = dt1b.shape
    tk = _reduce_tile(r, True)

    def body(a_ref, b_ref, g_ref, o_ref):
        @pl.when(pl.program_id(1) == 0)
        def _():
            o_ref[...] = jnp.zeros_like(o_ref)

        @pl.when(pl.program_id(0) == 0)
        def _():
            o_ref[...] += _tn(a_ref[...], g_ref[...])

        @pl.when(pl.program_id(0) == 1)
        def _():
            o_ref[...] += _tn(b_ref[...], g_ref[...])

    lhs = pl.BlockSpec((tk, 512), lambda i, k: (k, 0))
    return pl.pallas_call(
        body, name="grad_w_out", out_shape=jax.ShapeDtypeStruct((2 * 512, n), F32), grid=(2, r // tk),
        in_specs=[lhs, lhs, pl.BlockSpec((tk, n), lambda i, k: (k, 0))],
        out_specs=pl.BlockSpec((512, n), lambda i, k: (i, 0)),
        compiler_params=_params(("parallel", "arbitrary")),
    )(conv_of, gla_of, dt1b)


def _out_proj_bwd(dt1b, w_out, hc, cg, cb, fuse=NO_FUSE):
    r, d = dt1b.shape
    tm = _row_tile(r)

    def body(g_ref, w_ref, hc_ref, cg_ref, cb_ref, dhc_ref, dgla_ref, acc_ref):
        @pl.when(pl.program_id(0) == 0)
        def _():
            acc_ref[...] = jnp.zeros_like(acc_ref)

        gg = cg_ref[...]
        sums = [jnp.zeros((1, D_CONV), F32)] * 3
        for rs in _sub_rows(tm):
            dmix = _nt(g_ref[rs, :], w_ref[...])
            dgla_ref[rs, :] = dmix[:, D_CONV:]
            xh, rstd = _ln_stats(hc_ref[rs, :])
            nv = xh * gg + cb_ref[...]
            sig = _sigmoid(nv)
            dn = dmix[:, :D_CONV] * (sig * (1.0 + nv * (1.0 - sig)))
            dhc = _ln_bwd(dn * gg, xh, rstd)
            dhc_ref[rs, :] = dhc
            sums = [sums[0] + jnp.sum(dhc, axis=0, keepdims=True), sums[1] + jnp.sum(dn * xh, axis=0, keepdims=True),
                    sums[2] + jnp.sum(dn, axis=0, keepdims=True)]
        for k in range(3):
            acc_ref[k:k + 1, :] += sums[k]

    row = lambda n: pl.BlockSpec((tm, n), lambda i: (i, 0))
    vec = pl.BlockSpec((1, D_CONV), lambda i: (0, 0))
    res, got = _fused_call(
        body, fuse, name="out_proj_bwd",
        out_shape=[jax.ShapeDtypeStruct((r, D_CONV), F32), jax.ShapeDtypeStruct((r, 512), F32),
                   jax.ShapeDtypeStruct((8, D_CONV), F32)],
        grid=(r // tm,),
        in_specs=[row(d), pl.BlockSpec((d, d), lambda i: (0, 0)), row(D_CONV), vec, vec],
        out_specs=[row(D_CONV), row(512), pl.BlockSpec((8, D_CONV), lambda i: (0, 0))], scratch_shapes=[],
        operands=(dt1b, w_out, hc, cg, cb))
    return res[0], res[1], res[2], got


def _conv_bwd(dhc, u, conv_w, fuse=NO_FUSE):
    bsz, tp, _ = u.shape
    nchunk = tp // CHUNK
    win = CHUNK + 32
    nct = D_CONV // 128

    def body(dhc_ref, cv_ref, cg_ref, w_ref, dv_ref, dg_ref, dw_ref, h_scr, dhc_scr, hwin, dwin, dw_scr):
        h_scr[0:32, :] = jnp.zeros((32, 128), F32)
        h_scr[32:32 + tp, :] = cv_ref[0] * _sigmoid(cg_ref[0])
        dhc_scr[0:tp, :] = dhc_ref[0]
        dhc_scr[tp:tp + 32, :] = jnp.zeros((32, 128), F32)
        dw_scr[...] = jnp.zeros_like(dw_scr)

        def step(n, carry):
            r0 = pl.multiple_of(n * CHUNK, CHUNK)
            rows = pl.ds(r0, CHUNK)
            hwin[...] = h_scr[pl.ds(r0, win), :]
            dwin[...] = dhc_scr[pl.ds(r0, win), :]
            dcur = dwin[0:CHUNK, :]
            acc = jnp.zeros((CHUNK, 128), F32)
            for j in range(CONV_WIDTH):
                acc = acc + w_ref[j:j + 1, :] * dwin[30 - j:30 - j + CHUNK, :]
                prod = dcur * hwin[2 + j:2 + j + CHUNK, :]
                dw_scr[j * 8:(j + 1) * 8, :] += jnp.sum(prod.reshape(CHUNK // 8, 8, 128), axis=0)
            cg = cg_ref[0, rows, :]
            sig = _sigmoid(cg)
            rowid = n * CHUNK + lax.broadcasted_iota(jnp.int32, (CHUNK, 1), 0)
            dh = jnp.where(rowid >= PAD, acc, 0.0)
            dv_ref[0, rows, :] = (dh * sig).astype(BF16)
            dg_ref[0, rows, :] = (dh * cv_ref[0, rows, :] * sig * (1.0 - sig)).astype(BF16)
            return carry

        lax.fori_loop(0, nchunk, step, 0)
        dw_ref[0] = jnp.zeros((32, 128), F32)
        for j in range(CONV_WIDTH):
            dw_ref[0, j:j + 1, :] = jnp.sum(dw_scr[j * 8:(j + 1) * 8, :], axis=0, keepdims=True)

    blk = lambda off: pl.BlockSpec((1, tp, 128), lambda bi, t: (bi, 0, off // 128 + t))
    res, got = _fused_call(
        body, fuse, name="conv_bwd",
        out_shape=[jax.ShapeDtypeStruct((bsz, tp, D_CONV), BF16), jax.ShapeDtypeStruct((bsz, tp, D_CONV), BF16),
                   jax.ShapeDtypeStruct((bsz, 32, D_CONV), F32)],
        grid=(bsz, nct),
        in_specs=[blk(0), blk(C_VAL), blk(C_GATE), pl.BlockSpec((32, 128), lambda bi, t: (0, t))],
        out_specs=[blk(0), blk(0), pl.BlockSpec((1, 32, 128), lambda bi, t: (bi, 0, t))],
        scratch_shapes=[pltpu.VMEM((tp + 32, 128), F32), pltpu.VMEM((tp + 32, 128), F32),
                        pltpu.VMEM((win, 128), F32), pltpu.VMEM((win, 128), F32),
                        pltpu.VMEM((CONV_WIDTH * 8, 128), F32)],
        operands=(dhc, u, u, conv_w))
    return res[0], res[1], res[2], got


def _gla_bwd(dgla, u, o_pre, states, gup, gbias, gnorm, fuse=NO_FUSE):
    bsz, tp, _ = u.shape
    nchunk = tp // CHUNK
    nb = _gla_group(nchunk)

    def body(dy_ref, q_ref, k_ref, v_ref, r_ref, gd_ref, o_ref, st_ref, gup_ref, gb_ref, gn_ref,
             dq_ref, dk_ref, dv_ref, dr_ref, dgd_ref, dgup_ref, vec_ref, h_scr, gup_acc):
        tril, triu, hmask = _gla_consts(nb)
        h_scr[...] = jnp.zeros_like(h_scr)
        gup_acc[...] = jnp.zeros_like(gup_acc)
        gn = gn_ref[...]
        gupb = gup_ref[...]
        m = nb * CHUNK
        ngroup = nchunk // nb

        def group(i, carry):
            dbias, dgn = carry
            g = ngroup - 1 - i
            rows, valid, z, eb, enb, erest, dec, qe, ke, kd = _gla_group_terms(
                g, nb, q_ref, k_ref, gd_ref, gup_ref, gb_ref, tril)
            keb, kdb = ke.astype(BF16), kd.astype(BF16)
            dqe = jnp.zeros((nb, CHUNK, 128), F32)
            dke = jnp.zeros((nb, CHUNK, 128), F32)
            dkd = jnp.zeros((nb, CHUNK, 128), F32)
            ddec = jnp.zeros((nb, 1, 128), F32)
            for h in range(2):
                cols = slice(h * GLA_DV, (h + 1) * GLA_DV)
                o = o_ref[0, rows, cols]
                rh = r_ref[0, rows, cols]
                dy = dy_ref[0, rows, cols]
                rms = lax.rsqrt(jnp.mean(o * o, axis=-1, keepdims=True) + LN_EPS)
                nrm = o * rms
                sig = _sigmoid(rh)
                sw = rh * sig
                dr_ref[0, rows, cols] = (dy * nrm * gn * (sig * (1.0 + rh * (1.0 - sig)))).astype(BF16)
                dgn = dgn + jnp.sum(dy * nrm * sw, axis=0, keepdims=True)
                dn = dy * gn * sw
                do = rms * (dn - nrm * jnp.mean(dn * nrm, axis=-1, keepdims=True))
                dob = do.astype(BF16).reshape(nb, CHUNK, GLA_DV)
                qh = jnp.where(hmask[h], qe, 0.0).astype(BF16)
                vh = v_ref[0, rows, cols].astype(BF16).reshape(nb, CHUNK, GLA_DV)
                ht = h_scr[h]
                hts = [None] * nb
                for n in reversed(range(nb)):
                    hts[n] = ht
                    ht = dec[n] * ht + _tn(dob[n], qh[n])
                h_scr[h] = ht
                htf = jnp.stack(hts)
                htb = htf.astype(BF16)
                st = st_ref[0, h, pl.ds(g * nb, nb)]
                at = jnp.where(triu, _bnt(keb, qh), 0.0).astype(BF16)
                da = jnp.where(tril, _bnt(dob, vh), 0.0).astype(BF16)
                dat = jnp.where(triu, _bnt(vh, dob), 0.0).astype(BF16)
                dqe = dqe + jnp.where(hmask[h], _bnn(da, keb) + _bnn(dob, st.astype(BF16)), 0.0)
                dke = dke + _bnn(dat, qh)
                dv_ref[0, rows, cols] = (_bnn(at, dob) + _bnt(kdb, htb)).reshape(m, GLA_DV).astype(BF16)
                dkd = dkd + jnp.where(hmask[h], _bnn(vh, htb), 0.0)
                ddec = ddec + jnp.where(hmask[h], jnp.sum(htf * st, axis=1, keepdims=True), 0.0)
            dq_ref[0, rows, :] = (dqe * eb * Q_SCALE).reshape(m, 128).astype(BF16)
            dk_ref[0, rows, :] = (dke * enb + dkd * erest).reshape(m, 128).astype(BF16)
            db = dqe * qe - dke * ke - dkd * kd
            dblast = jnp.sum(dkd * kd, axis=1, keepdims=True) + ddec * dec
            lastrow = lax.broadcasted_iota(jnp.int32, (1, CHUNK, 1), 1) == CHUNK - 1
            db = db + jnp.where(lastrow, dblast, 0.0)
            dlg = _bnn(triu.astype(F32), db, precision=lax.Precision.HIGHEST).reshape(m, 128)
            dz = jnp.where(valid, dlg, 0.0) * (1.0 / GLA_TAU) * (1.0 - _sigmoid(z))
            dzb = dz.astype(BF16)
            dgd_ref[0, 0, rows, :] = _nt(dzb, gupb).astype(BF16)
            gup_acc[...] += _tn(gd_ref[0, rows, :].astype(BF16), dzb)
            return dbias + jnp.sum(dz, axis=0, keepdims=True), dgn

        zero = jnp.zeros((1, 128), F32)
        dbias, dgn = lax.fori_loop(0, ngroup, group, (zero, zero))
        dgup_ref[0] = gup_acc[...]
        vec_ref[0] = jnp.zeros((8, 128), F32)
        vec_ref[0, 0:1, :] = dbias
        vec_ref[0, 1:2, :] = dgn

    pair = lambda w, off: pl.BlockSpec((1, tp, w), lambda bi, p: (bi, 0, off // w + p))
    return _fused_call(
        body, fuse, name="gla_bwd",
        out_shape=[jax.ShapeDtypeStruct((bsz, tp, 256), BF16), jax.ShapeDtypeStruct((bsz, tp, 256), BF16),
                   jax.ShapeDtypeStruct((bsz, tp, 512), BF16), jax.ShapeDtypeStruct((bsz, tp, 512), BF16),
                   jax.ShapeDtypeStruct((bsz, 2, tp, 128), BF16), jax.ShapeDtypeStruct((bsz, 128, 256), F32),
                   jax.ShapeDtypeStruct((bsz, 8, 256), F32)],
        grid=(bsz, 2),
        in_specs=[pair(256, 0), pair(128, C_Q), pair(128, C_K), pair(256, C_V), pair(256, C_R),
                  pl.BlockSpec((1, tp, 128), lambda bi, p: (bi, 0, C_GD // 128)),
                  pair(256, 0),
                  pl.BlockSpec((1, 2, nchunk, GLA_DV, 128), lambda bi, p: (bi, p, 0, 0, 0)),
                  pl.BlockSpec((128, 128), lambda bi, p: (0, p)),
                  pl.BlockSpec((1, 128), lambda bi, p: (0, p)),
                  pl.BlockSpec((1, 128), lambda bi, p: (0, 0))],
        out_specs=[pair(128, 0), pair(128, 0), pair(256, 0), pair(256, 0),
                   pl.BlockSpec((1, 1, tp, 128), lambda bi, p: (bi, p, 0, 0)),
                   pl.BlockSpec((1, 128, 128), lambda bi, p: (bi, 0, p)),
                   pl.BlockSpec((1, 8, 128), lambda bi, p: (bi, 0, p))],
        scratch_shapes=[pltpu.VMEM((2, GLA_DV, 128), F32), pltpu.VMEM((128, 128), F32)],
        operands=(dgla, u, u, u, u, u, o_pre, states, gup, gbias, gnorm))


_DU_OFFSETS = (C_VAL, C_GATE, C_Q, C_K, C_V, C_R)
_DU_WIDTHS = (512, 512, 256, 256, 512, 512)


def _du_specs(tm, per, row_map):
    specs = [pl.BlockSpec((tm, w), row_map) for w in _DU_WIDTHS]
    for p in range(2):
        specs.append(pl.BlockSpec((1, 1, tm, 128), lambda *ix, p=p: (row_map(*ix)[0] // per, p, row_map(*ix)[0] % per, 0)))
    return specs


def _du_pieces(refs):
    out = [(off, ref[...]) for off, ref in zip(_DU_OFFSETS, refs[:6])]
    dgd = (refs[6][0, 0].astype(F32) + refs[7][0, 0].astype(F32)).astype(BF16)
    out.append((C_GD, dgd))
    return out


def _in_proj_bwd(pieces, dgd, w_int, dt1, tp, fuse=NO_FUSE):
    r, d = dt1.shape
    tm = _row_tile(tp)
    per = tp // tm

    def body(*refs):
        w_ref, dt_ref, o_ref = refs[8:]
        acc = ALPHA * dt_ref[...]
        for off, val in _du_pieces(refs[:8]):
            acc = acc + _nn(val, w_ref[off:off + val.shape[1], :])
        o_ref[...] = acc

    row = lambda i: (i, 0)
    (ds0,), got = _fused_call(
        body, fuse, name="in_proj_bwd", out_shape=[jax.ShapeDtypeStruct((r, d), F32)], grid=(r // tm,),
        in_specs=_du_specs(tm, per, row) + [pl.BlockSpec((D_IN_PAD, d), lambda i: (0, 0)), pl.BlockSpec((tm, d), row)],
        out_specs=[pl.BlockSpec((tm, d), row)], scratch_shapes=[], operands=(*pieces, dgd, dgd, w_int, dt1))
    return ds0, got


def _grad_w_in(pieces, dgd, s0b, tp, fuse=NO_FUSE):
    r, d = s0b.shape
    tk = _reduce_tile(tp, False)
    per = tp // tk

    def body(*refs):
        s_ref, o_ref = refs[8:]

        @pl.when(pl.program_id(0) == 0)
        def _():
            o_ref[...] = jnp.zeros_like(o_ref)

        s = s_ref[...]
        for off, val in _du_pieces(refs[:8]):
            o_ref[off:off + val.shape[1], :] += _tn(val, s)

    row = lambda k: (k, 0)
    (out,), got = _fused_call(
        body, fuse, name="grad_w_in", out_shape=[jax.ShapeDtypeStruct((D_IN_PAD, d), F32)], grid=(r // tk,),
        in_specs=_du_specs(tk, per, row) + [pl.BlockSpec((tk, d), row)],
        out_specs=[pl.BlockSpec((D_IN_PAD, d), lambda k: (0, 0))], scratch_shapes=[],
        operands=(*pieces, dgd, dgd, s0b))
    return out, got


def _ln_in_bwd(ds0, x, meta, g):
    bsz, s, d = x.shape
    tp = s + HEAD
    nh = 2
    sh = s // nh
    rc = min(256, sh)

    def body(ds_ref, x_ref, meta_ref, g_ref, gx_ref, dm_ref, vec_ref):
        h = pl.program_id(1)
        gg = g_ref[...]

        @pl.when(h == 0)
        def _():
            mh, mr = _ln_stats(meta_ref[...])
            dsm = ds_ref[0, PAD:HEAD, :]
            dm_ref[0] = _ln_bwd(dsm * gg, mh, mr)
            vec_ref[0] = jnp.zeros((8, d), F32)
            vec_ref[0, 0:1, :] = jnp.sum(dsm * mh, axis=0, keepdims=True)
            vec_ref[0, 1:2, :] = jnp.sum(dsm, axis=0, keepdims=True)

        def step(i, carry):
            sg, sb = carry
            dst = pl.ds(pl.multiple_of(i * rc, rc), rc)
            src = pl.ds(pl.multiple_of(HEAD + h * sh + i * rc, 64), rc)
            xh, rstd = _ln_stats(x_ref[0, dst, :])
            dsv = ds_ref[0, src, :]
            gx_ref[0, dst, :] = _ln_bwd(dsv * gg, xh, rstd)
            return sg + jnp.sum(dsv * xh, axis=0, keepdims=True), sb + jnp.sum(dsv, axis=0, keepdims=True)

        zero = jnp.zeros((1, d), F32)
        sg, sb = lax.fori_loop(0, sh // rc, step, (zero, zero))
        vec_ref[0, 0:1, :] += sg
        vec_ref[0, 1:2, :] += sb

    return pl.pallas_call(
        body, name="ln_in_bwd",
        out_shape=[jax.ShapeDtypeStruct((bsz, s, d), F32), jax.ShapeDtypeStruct((bsz, N_META, d), F32),
                   jax.ShapeDtypeStruct((bsz, 8, d), F32)],
        grid=(bsz, nh),
        in_specs=[pl.BlockSpec((1, tp, d), lambda bi, hi: (bi, 0, 0)),
                  pl.BlockSpec((1, sh, d), lambda bi, hi: (bi, hi, 0)),
                  pl.BlockSpec((N_META, d), lambda bi, hi: (0, 0)),
                  pl.BlockSpec((1, d), lambda bi, hi: (0, 0))],
        out_specs=[pl.BlockSpec((1, sh, d), lambda bi, hi: (bi, hi, 0)),
                   pl.BlockSpec((1, N_META, d), lambda bi, hi: (bi, 0, 0)),
                   pl.BlockSpec((1, 8, d), lambda bi, hi: (bi, 0, 0))],
        compiler_params=_params(("parallel", "arbitrary")),
    )(ds0, x, meta, g)


def _rows128(a):
    return a.reshape(-1, 128)


def kernel(x, meta_tokens, ln_in_g, ln_in_b, w_in, conv_w, conv_b, conv_ln_g, conv_ln_b, gate_up, gate_bias, gla_norm_g, w_out, ln1_g, ln1_b, w_ff1, w_ff2, ln2_g, ln2_b, loss_target, m_meta_tokens, m_ln_in_g, m_ln_in_b, m_w_in, m_conv_w, m_conv_b, m_conv_ln_g, m_conv_ln_b, m_gate_up, m_gate_bias, m_gla_norm_g, m_w_out, m_ln1_g, m_ln1_b, m_w_ff1, m_w_ff2, m_ln2_g, m_ln2_b, v_meta_tokens, v_ln_in_g, v_ln_in_b, v_w_in, v_conv_w, v_conv_b, v_conv_ln_g, v_conv_ln_b, v_gate_up, v_gate_bias, v_gla_norm_g, v_w_out, v_ln1_g, v_ln1_b, v_w_ff1, v_w_ff2, v_ln2_g, v_ln2_b):
    bsz, seq, d = x.shape
    tp = seq + HEAD
    r = bsz * tp
    xi, yi, ci = _mesh_pos()
    chip = 2 * xi + yi
    c_arr = jnp.reshape(ci, (1,)).astype(jnp.int32)
    pos_arr = jnp.stack([chip, ci]).astype(jnp.int32)

    sh_in = D_IN // 4
    shard_in = jnp.pad(w_in[0].T.astype(BF16), ((0, D_IN_PAD // 4 - sh_in), (0, 0)))
    shard_w1t, shard_wout, shard_w2 = w_ff1[0].T.astype(BF16), w_out[0].astype(BF16), w_ff2[0].astype(BF16)
    small_w = jnp.concatenate([_rows128(meta_tokens), _rows128(conv_w[0]), _rows128(gate_up[0])], axis=0)
    g_small = _gather_small(small_w)
    n_meta_rows, n_cw_rows = N_META * 256 // 128, CONV_WIDTH * 128 // 128
    meta_full = jnp.concatenate([g_small[j, :n_meta_rows].reshape(N_META, 256) for j in range(4)], axis=1)
    convw_full = jnp.concatenate(
        [g_small[j, n_meta_rows:n_meta_rows + n_cw_rows].reshape(CONV_WIDTH, 128) for j in range(4)], axis=1)
    gup_full = jnp.concatenate(
        [g_small[j, n_meta_rows + n_cw_rows:].reshape(GLA_RANK, 64) for j in range(4)], axis=1)
    convw_p = jnp.pad(convw_full, ((0, 1), (0, 0)))
    gup_p = jnp.pad(gup_full, ((0, 128 - GLA_RANK), (0, 0))).astype(BF16)
    ln_in_g2, ln_in_b2 = ln_in_g.reshape(1, d), ln_in_b.reshape(1, d)

    s0, s0b, ((g_int,),) = _ln_in_fwd(x, meta_full, ln_in_g2, ln_in_b2,
                                      [("gather", [(shard_in, 0, D_IN_PAD // 4, None)])])
    (g_int,) = _place_own([g_int], [shard_in])
    w_int = jnp.pad(g_int[:, :sh_in].reshape(D_IN, d), ((0, D_IN_PAD - D_IN), (0, 0)))
    s0f, s0bf = s0.reshape(r, d), s0b.reshape(r, d)
    u, ((w1_buf,),) = _in_proj(s0bf, w_int, [("gather", [(shard_w1t, 0, 640, None)])])
    (w1_buf,) = _place_own([w1_buf], [shard_w1t])
    u3 = u.reshape(bsz, tp, D_IN_PAD)
    hc, ((w1_buf,),) = _conv_fwd(u3, convw_p, conv_b, [("gather", [(shard_w1t, 640, 384, w1_buf)])])
    hc = hc.reshape(r, D_CONV)
    gla_out, o_pre, states, ((g_wout, w2_buf),) = _gla_fwd(
        u3, gup_p, gate_bias, gla_norm_g, [("gather", [(shard_wout, 0, 256, None), (shard_w2, 0, 256, None)])])
    g_wout, w2_buf = _place_own([g_wout, w2_buf], [shard_wout, shard_w2])
    wout = g_wout.reshape(d, d)
    gla_of = gla_out.reshape(r, 512)
    conv_of, xhat1, rstd1, s1b, ((w2_buf,),) = _out_proj_ln1(
        hc, gla_of, wout, s0f, conv_ln_g, conv_ln_b, ln1_g, ln1_b, [("gather", [(shard_w2, 256, 384, w2_buf)])])
    w1t = w1_buf.reshape(D_FF, d)
    ra, ((w2_buf,),) = _ffn1(s1b, w1t, [("gather", [(shard_w2, 640, 384, w2_buf)])])
    w2 = w2_buf.reshape(D_FF, d)
    dt2, dt2b, acc2 = _ffn2_ln2_loss(ra, w2, xhat1, ln1_g, ln1_b, ln2_g, ln2_b, loss_target, tp)

    def add_pair(g, got, splits=None):
        return _add_pair(g, got, c_arr, splits)

    da = _ffn_bwd_da(dt2b, w2, ra)
    dt1, dt1b, acc1 = _ffn_bwd_ln1(da, w1t, dt2, xhat1, rstd1, ln1_g)
    g_w2, _ = _matmul_tn(ra, dt2b, 1024, square_lhs=True, name="grad_w_ff2")
    big_w2 = g_w2.reshape(4, D_FF // 4, d)
    g_w1t, ((pair_w2,),) = _matmul_tn(da, s1b, 1024, name="grad_w_ff1", fuse=[("pair", [big_w2])])
    big_w1 = g_w1t.reshape(4, D_FF // 4, d)
    dhc, dgla, cacc, ((pair_w1,),) = _out_proj_bwd(dt1b, wout, hc, conv_ln_g, conv_ln_b, [("pair", [big_w1])])
    (part_w1,), (part_w2a, part_w2b) = add_pair(big_w1, pair_w1), add_pair(big_w2, pair_w2, [512, 512])
    big_wout = _grad_w_out(conv_of, gla_of, dt1b).reshape(4, d // 4, d)
    dcv, dcg, dcw, ((chip_w1,), (pair_wout,)) = _conv_bwd(
        dhc.reshape(bsz, tp, D_CONV), u3, convw_p, [("exchange", [part_w1]), ("pair", [big_wout])])
    (part_wout,) = add_pair(big_wout, pair_wout)
    (dq, dk, dv, dr, dgd, dgup, gvec), ((chip_w2a,),) = _gla_bwd(
        dgla.reshape(bsz, tp, 512), u3, o_pre, states, gup_p, gate_bias, gla_norm_g, [("exchange", [part_w2a])])
    pieces = [a.reshape(r, a.shape[-1]) for a in (dcv, dcg, dq, dk, dv, dr)]
    g_wint, ((chip_w2b, chip_wout),) = _grad_w_in(
        pieces, dgd, s0bf, tp, [("exchange", [part_w2b, part_wout])])
    big_win = jnp.stack([g_wint[j * sh_in:(j + 1) * sh_in] for j in range(4)])
    (pair_win,) = _pair_exchange([big_win])
    half_w1 = _add_chips(big_w1, pair_w1, [chip_w1], pos_arr)
    half_w2 = _add_chips(big_w2, pair_w2, [chip_w2a, chip_w2b], pos_arr)
    ds0, ((chip_win,), (sib_w1, sib_w2)) = _in_proj_bwd(
        pieces, dgd, w_int, dt1, tp, [("exchange", add_pair(big_win, pair_win)), ("swap", [half_w1, half_w2])])
    grad_x, dmeta, lvec = _ln_in_bwd(ds0.reshape(bsz, tp, d), x, meta_full, ln_in_g2)

    red, loss = _sum_small(_allgather_small(_pack_small(acc2, acc1, lvec, cacc, gvec, dcw, dgup, dmeta)))

    half_win = _add_chips(big_win, pair_win, [chip_win], pos_arr)
    half_wout = _add_chips(big_wout, pair_wout, [chip_wout], pos_arr)
    sib_win, sib_wout = _pair_swap([half_win, half_wout])
    f_wint, f_wout, f_w1t, f_w2 = [
        jnp.concatenate([jnp.where(ci == 0, own, sib), jnp.where(ci == 0, sib, own)], axis=1)
        for own, sib in ((half_win, sib_win), (half_wout, sib_wout), (half_w1, sib_w1), (half_w2, sib_w2))]

    grads = {"w_in": f_wint.T[None], "w_out": f_wout[None], "w_ff1": f_w1t.T[None], "w_ff2": f_w2[None]}
    weights = dict(meta_tokens=meta_tokens, ln_in_g=ln_in_g, ln_in_b=ln_in_b, w_in=w_in, conv_w=conv_w, conv_b=conv_b,
                   conv_ln_g=conv_ln_g, conv_ln_b=conv_ln_b, gate_up=gate_up, gate_bias=gate_bias,
                   gla_norm_g=gla_norm_g, w_out=w_out, ln1_g=ln1_g, ln1_b=ln1_b, w_ff1=w_ff1, w_ff2=w_ff2,
                   ln2_g=ln2_g, ln2_b=ln2_b)
    moms = dict(meta_tokens=(m_meta_tokens, v_meta_tokens), ln_in_g=(m_ln_in_g, v_ln_in_g),
                ln_in_b=(m_ln_in_b, v_ln_in_b), w_in=(m_w_in, v_w_in), conv_w=(m_conv_w, v_conv_w),
                conv_b=(m_conv_b, v_conv_b), conv_ln_g=(m_conv_ln_g, v_conv_ln_g),
                conv_ln_b=(m_conv_ln_b, v_conv_ln_b), gate_up=(m_gate_up, v_gate_up),
                gate_bias=(m_gate_bias, v_gate_bias), gla_norm_g=(m_gla_norm_g, v_gla_norm_g),
                w_out=(m_w_out, v_w_out), ln1_g=(m_ln1_g, v_ln1_g), ln1_b=(m_ln1_b, v_ln1_b),
                w_ff1=(m_w_ff1, v_w_ff1), w_ff2=(m_w_ff2, v_w_ff2), ln2_g=(m_ln2_g, v_ln2_g),
                ln2_b=(m_ln2_b, v_ln2_b))
    names = list(weights)
    big_names = ("w_in", "w_out", "w_ff1", "w_ff2")
    delta, new_m, new_v = {}, {}, {}
    lin = lambda a: a.reshape(-1, 128)
    unlin = lambda a: a.reshape(sh_in, d).T[None]
    dl, mn, vn = _adamw(lin(w_in[0].T), lin(f_wint), lin(m_w_in[0].T), lin(v_w_in[0].T))
    delta["w_in"], new_m["w_in"], new_v["w_in"] = unlin(dl), unlin(mn), unlin(vn)
    for k in big_names[1:]:
        shp = weights[k].shape
        two = lambda a: a.reshape(shp[-2], shp[-1])
        dl, mn, vn = _adamw(two(weights[k]), two(grads[k]), two(moms[k][0]), two(moms[k][1]))
        delta[k], new_m[k], new_v[k] = dl.reshape(shp), mn.reshape(shp), vn.reshape(shp)
    small_names = [k for k in names if k not in big_names]
    two = lambda a: a.reshape(-1, a.shape[-1])

    def my_cols(k, rows, width):
        r0 = SMALL_AT[k][0]
        return lax.dynamic_slice(red, (r0, chip * width), (rows, width))

    sharded = {"meta_tokens": my_cols("meta_tokens", N_META, 256), "conv_w": my_cols("conv_w", CONV_WIDTH, 128),
               "gate_up": my_cols("gate_up", GLA_RANK, 64)}
    upd = _adamw_small(red, sharded, {k: (two(weights[k]), two(moms[k][0]), two(moms[k][1])) for k in small_names})
    for k in small_names:
        shp = weights[k].shape
        grads[k], delta[k], new_m[k], new_v[k] = [a.reshape(shp) for a in upd[k]]
    loss = loss.reshape(())

    return (loss, grad_x, *[grads[k] for k in names], *[delta[k] for k in names],
            *[new_m[k] for k in names], *[new_v[k] for k in names])
```

```python
import functools

import jax
import jax.numpy as jnp
from jax import lax
from jax.experimental import pallas as pl
from jax.experimental.pallas import tpu as pltpu

F32 = jnp.float32
BF16 = jnp.bfloat16

D_MODEL = 1024
N_META = 16
D_CONV = 512
CONV_WIDTH = 31
GLA_HEADS = 4
GLA_DV = 128
GLA_DK = 64
GLA_RANK = 16
GLA_TAU = 16.0
CHUNK = 64
D_FF = 4096
LN_EPS = 1e-5
ALPHA = 2.0 ** 0.25
D_IN = 2576
D_IN_PAD = 2688
PAD = CHUNK - N_META
HEAD = PAD + N_META
Q_SCALE = GLA_DK ** -0.5
ADAM_LR, ADAM_B1, ADAM_B2, ADAM_EPS, ADAM_WD, ADAM_STEP = 0.001, 0.9, 0.999, 1e-08, 0.01, 10
HALF = D_MODEL // 2
VMEM_LIMIT = 56 * 1024 * 1024
MESH = pl.DeviceIdType.MESH

C_VAL, C_GATE, C_Q, C_K, C_V, C_R, C_GD = 0, 512, 1024, 1280, 1536, 2048, 2560

SMALL_AT = {"loss": (0, 1, 1024), "ln_in_g": (1, 1, 1024), "ln_in_b": (2, 1, 1024), "ln1_g": (3, 1, 1024),
            "ln1_b": (4, 1, 1024), "ln2_g": (5, 1, 1024), "ln2_b": (6, 1, 1024), "conv_b": (7, 1, 512),
            "conv_ln_g": (8, 1, 512), "conv_ln_b": (9, 1, 512), "gate_bias": (10, 1, 256), "gla_norm_g": (11, 1, 128),
            "conv_w": (16, 32, 512), "gate_up": (48, 16, 256), "meta_tokens": (64, 16, 1024)}
SMALL_ROWS = 80


def _params(sem=None, **kw):
    return pltpu.CompilerParams(dimension_semantics=sem, vmem_limit_bytes=VMEM_LIMIT, **kw)


def _row_tile(tp):
    for t in (704, 352, 192, 64):
        if tp % t == 0:
            return t
    raise ValueError(tp)


def _reduce_tile(tp, big):
    for t in ((2112, 1056, 704) if big else (1056, 704)) + (352, 192, 64):
        if tp % t == 0:
            return t
    raise ValueError(tp)


def _sub_rows(tm):
    return [slice(0, tm)]


def _dot(a, b, dims, precision=None):
    return lax.dot_general(a, b, (dims, ((), ())), preferred_element_type=F32, precision=precision)


def _nn(a, b, **kw):
    return _dot(a, b, ((1,), (0,)), **kw)


def _nt(a, b, **kw):
    return _dot(a, b, ((1,), (1,)), **kw)


def _tn(a, b, **kw):
    return _dot(a, b, ((0,), (0,)), **kw)


def _sigmoid(x):
    return 1.0 / (1.0 + jnp.exp(-x))


def _log_sigmoid(z):
    return jnp.minimum(z, 0.0) - jnp.log(1.0 + jnp.exp(-jnp.abs(z)))


def _ln_stats(t):
    mu = jnp.mean(t, axis=-1, keepdims=True)
    d = t - mu
    var = jnp.mean(d * d, axis=-1, keepdims=True)
    rstd = lax.rsqrt(var + LN_EPS)
    return d * rstd, rstd


def _ln_bwd(dxhat, xhat, rstd):
    m1 = jnp.mean(dxhat, axis=-1, keepdims=True)
    m2 = jnp.mean(dxhat * xhat, axis=-1, keepdims=True)
    return rstd * (dxhat - m1 - xhat * m2)


def _mesh_pos():
    return lax.axis_index("x"), lax.axis_index("y"), lax.axis_index("c")


ANY = pl.BlockSpec(memory_space=pl.ANY)


def _gather_sems(n):
    return [pltpu.SemaphoreType.DMA((n, 3))] * 4


def _gather_steps(ins, outs, sems, ranges=None):
    n = len(ins)
    send, recv, fsend, frecv = sems
    ranges = ranges or [(0, ref.shape[0]) for ref in ins]
    x, y, c = _mesh_pos()
    me = 2 * x + y
    sibling = (x, y, 1 - c)
    chips = [(1 - x, y), (x, 1 - y), (1 - x, 1 - y)]
    chip_idx = [2 * px + py for px, py in chips]
    mine = [pl.ds(pl.multiple_of(r0 + c * (nr // 2), 16), nr // 2) for r0, nr in ranges]
    other = [pl.ds(pl.multiple_of(r0 + (1 - c) * (nr // 2), 16), nr // 2) for r0, nr in ranges]
    pairs = [(a, k) for a in range(n) for k in range(3)]

    def ici(a, k, slab):
        return pltpu.make_async_remote_copy(
            src_ref=ins[a].at[mine[a], :], dst_ref=outs[a].at[slab, mine[a], :],
            send_sem=send.at[a, k], recv_sem=recv.at[a, k], device_id=(*chips[k], c), device_id_type=MESH)

    def forward(a, k, rows):
        blk = outs[a].at[chip_idx[k], rows[a], :]
        return pltpu.make_async_remote_copy(
            src_ref=blk, dst_ref=blk, send_sem=fsend.at[a, k], recv_sem=frecv.at[a, k],
            device_id=sibling, device_id_type=MESH)

    def start():
        for a, k in pairs:
            ici(a, k, me).start()

    def relay():
        for a, k in pairs:
            ici(a, k, chip_idx[k]).wait_recv()
            forward(a, k, mine).start()

    def finish():
        for a, k in pairs:
            forward(a, k, other).wait_recv()
        for a, k in pairs:
            ici(a, k, me).wait_send()
            forward(a, k, mine).wait_send()

    return start, relay, finish


def _place_own(gathered, shards):
    chip = 2 * lax.axis_index("x") + lax.axis_index("y")
    return [lax.dynamic_update_slice(g, s[None], (chip, 0, 0)) for g, s in zip(gathered, shards)]


def _gather_small(small):
    def body(small_in, small_out, ssend, srecv):
        x, y, c = _mesh_pos()
        chips = [(1 - x, y), (x, 1 - y), (1 - x, 1 - y)]

        def small_copy(k, slot):
            return pltpu.make_async_remote_copy(
                src_ref=small_in, dst_ref=small_out.at[slot], send_sem=ssend.at[k], recv_sem=srecv.at[k],
                device_id=(*chips[k], c), device_id_type=MESH)

        for k in range(3):
            small_copy(k, 2 * x + y).start()
        for k, (px, py) in enumerate(chips):
            small_copy(k, 2 * px + py).wait_recv()
        for k in range(3):
            small_copy(k, 2 * x + y).wait_send()

    res = pl.pallas_call(
        body, name="gather_small", out_shape=jax.ShapeDtypeStruct((4,) + small.shape, small.dtype),
        in_specs=[ANY], out_specs=ANY,
        scratch_shapes=[pltpu.SemaphoreType.DMA((3,)), pltpu.SemaphoreType.DMA((3,))],
    )(small)
    return _place_own([res], [small])[0]


def _pair_swap_steps(ins, outs, sems):
    send, recv = sems
    x, y, c = _mesh_pos()
    cps = [pltpu.make_async_remote_copy(
        src_ref=ins[a], dst_ref=outs[a], send_sem=send.at[a], recv_sem=recv.at[a],
        device_id=(x, y, 1 - c), device_id_type=MESH) for a in range(len(ins))]

    def start():
        for cp in cps:
            cp.start()

    def finish():
        for cp in cps:
            cp.wait()

    return start, finish


def _pair_exchange_steps(ins, outs, sems):
    send, recv = sems
    x, y, c = _mesh_pos()
    other = pl.ds(pl.multiple_of((1 - c) * HALF, 128), HALF)
    cps = [pltpu.make_async_remote_copy(
        src_ref=ins[a].at[:, :, other], dst_ref=outs[a], send_sem=send.at[a], recv_sem=recv.at[a],
        device_id=(x, y, 1 - c), device_id_type=MESH) for a in range(len(ins))]

    def start():
        for cp in cps:
            cp.start()

    def finish():
        for cp in cps:
            cp.wait()

    return start, finish


def _pair_exchange_shapes(grads):
    return [jax.ShapeDtypeStruct(g.shape[:2] + (HALF,), g.dtype) for g in grads]


def _pair_exchange(grads):
    n = len(grads)

    def body(*refs):
        start, finish = _pair_exchange_steps(refs[:n], refs[n:2 * n], refs[2 * n:])
        start()
        finish()

    return pl.pallas_call(
        body, name="grad_pair_exchange", out_shape=_pair_exchange_shapes(grads),
        in_specs=[ANY] * n, out_specs=[ANY] * n,
        scratch_shapes=[pltpu.SemaphoreType.DMA((n,)), pltpu.SemaphoreType.DMA((n,))],
    )(*grads)


NO_FUSE = ()


def _fuse_plan(kind, items):
    n = len(items)
    if kind == "gather":
        shards = [it[0] for it in items]
        bufs = [it[3] for it in items if it[3] is not None]
        alias, b = {}, 0
        for a, it in enumerate(items):
            if it[3] is not None:
                alias[n + b] = a
                b += 1
        ranges = [(it[1], it[2]) for it in items]
        return (shards + bufs, [jax.ShapeDtypeStruct((4,) + s.shape, s.dtype) for s in shards], alias, _gather_sems(n),
                lambda i, o, s: _gather_steps(i[:n], o, s, ranges))
    if kind == "exchange":
        return list(items), _chip_exchange_shapes(items), {}, _chip_exchange_sems(n), _chip_exchange_steps
    pair_sems = [pltpu.SemaphoreType.DMA((n,)), pltpu.SemaphoreType.DMA((n,))]
    if kind == "swap":
        return (list(items), [jax.ShapeDtypeStruct(h.shape, h.dtype) for h in items], {}, pair_sems, _pair_swap_steps)
    return list(items), _pair_exchange_shapes(items), {}, pair_sems, _pair_exchange_steps


def _fused_call(body, fuse, *, name, grid, in_specs, out_specs, out_shape, scratch_shapes, operands):
    plans = [_fuse_plan(kind, list(items)) for kind, items in fuse if len(items)]
    n_in, n_out, n_s = len(in_specs), len(out_shape), len(scratch_shapes)
    comm = [a for p in plans for a in p[0]]
    shapes = [s for p in plans for s in p[1]]
    nc, no = len(comm), len(shapes)
    aliases, i_at, o_at = {}, n_in, n_out
    for p in plans:
        for i, o in p[2].items():
            aliases[i_at + i] = o_at + o
        i_at, o_at = i_at + len(p[0]), o_at + len(p[1])

    def wrapped(*refs):
        o0 = n_in + nc
        s0 = o0 + n_out + no
        i_at, o_at, sem_at, steps = n_in, o0 + n_out, s0 + n_s, []
        for p in plans:
            steps.append(p[4](refs[i_at:i_at + len(p[0])], refs[o_at:o_at + len(p[1])], refs[sem_at:sem_at + len(p[3])]))
            i_at, o_at, sem_at = i_at + len(p[0]), o_at + len(p[1]), sem_at + len(p[3])
        first, last = _grid_ends(grid)
        for st in steps:
            pl.when(first)(st[0])
        for st in steps:
            for mid in st[1:-1]:
                pl.when(last)(mid)
        body(*refs[:n_in], *refs[o0:o0 + n_out], *refs[s0:s0 + n_s])
        for st in steps:
            pl.when(last)(st[-1])

    res = pl.pallas_call(
        wrapped if plans else body, name=name, grid=grid, in_specs=list(in_specs) + [ANY] * nc,
        out_specs=list(out_specs) + [ANY] * no, out_shape=list(out_shape) + shapes,
        scratch_shapes=list(scratch_shapes) + [s for p in plans for s in p[3]], input_output_aliases=aliases,
        compiler_params=_params(("arbitrary",) * len(grid)))(*operands, *comm)
    outs, got, results = list(res[:n_out]), list(res[n_out:]), []
    for p in plans:
        results.append(got[:len(p[1])])
        got = got[len(p[1]):]
    return outs, results


def _chip_exchange_sems(n):
    return [pltpu.SemaphoreType.DMA((n, 3))] * 2


def _chip_exchange_shapes(parts):
    return [jax.ShapeDtypeStruct((3,) + p.shape[1:], p.dtype) for p in parts]


def _chip_exchange_steps(ins, outs, sems):
    send, recv = sems
    x, y, c = _mesh_pos()
    chips = [(1 - x, y), (x, 1 - y), (1 - x, 1 - y)]
    cps = [pltpu.make_async_remote_copy(
        src_ref=ins[a].at[2 * px + py], dst_ref=outs[a].at[k], send_sem=send.at[a, k], recv_sem=recv.at[a, k],
        device_id=(px, py, c), device_id_type=MESH) for a in range(len(ins)) for k, (px, py) in enumerate(chips)]

    def start():
        for cp in cps:
            cp.start()

    def finish():
        for cp in cps:
            cp.wait()

    return start, finish


def _chip_exchange(parts):
    n = len(parts)

    def body(*refs):
        start, finish = _chip_exchange_steps(refs[:n], refs[n:2 * n], refs[2 * n:])
        start()
        finish()

    return pl.pallas_call(
        body, name="grad_chip_exchange", out_shape=_chip_exchange_shapes(parts),
        in_specs=[ANY] * n, out_specs=[ANY] * n, scratch_shapes=_chip_exchange_sems(n),
    )(*parts)


def _pair_swap(halves):
    n = len(halves)

    def body(*refs):
        start, finish = _pair_swap_steps(refs[:n], refs[n:2 * n], refs[2 * n:])
        start()
        finish()

    return pl.pallas_call(
        body, name="grad_pair_swap",
        out_shape=[jax.ShapeDtypeStruct(h.shape, h.dtype) for h in halves],
        in_specs=[ANY] * n, out_specs=[ANY] * n,
        scratch_shapes=[pltpu.SemaphoreType.DMA((n,)), pltpu.SemaphoreType.DMA((n,))],
    )(*halves)


def _allgather_small(pack):
    m_per, ncol = pack.shape

    def body(x_ref, out_ref, send_sems, recv_sems, local_sem):
        x, y, c = _mesh_pos()
        me, sibling = (x, y, c), (x, y, 1 - c)
        chips = [(1 - x, y), (x, 1 - y), (1 - x, 1 - y)]

        def rows(px, py, pc):
            return out_ref.at[pl.ds(pl.multiple_of((4 * px + 2 * py + pc) * m_per, 8), m_per), :]

        def copy(k, block, to, src=None):
            return pltpu.make_async_remote_copy(
                src_ref=rows(*block) if src is None else src, dst_ref=rows(*block),
                send_sem=send_sems.at[k], recv_sem=recv_sems.at[k], device_id=to, device_id_type=MESH)

        mine = pltpu.make_async_copy(x_ref, rows(*me), local_sem)
        mine.start()
        first = [copy(0, me, sibling, src=x_ref)]
        first += [copy(1 + j, me, (*chip, c), src=x_ref) for j, chip in enumerate(chips)]
        for cp in first:
            cp.start()
        passed = [copy(4 + j, (*chip, c), sibling) for j, chip in enumerate(chips)]
        for j, chip in enumerate(chips):
            copy(1 + j, (*chip, c), me).wait_recv()
            passed[j].start()
        copy(0, sibling, me).wait_recv()
        for j, chip in enumerate(chips):
            copy(4 + j, (*chip, 1 - c), me).wait_recv()
        for cp in first + passed:
            cp.wait_send()
        mine.wait()

    return pl.pallas_call(
        body, name="allgather_small",
        out_shape=jax.ShapeDtypeStruct((8 * m_per, ncol), pack.dtype),
        in_specs=[pl.BlockSpec(memory_space=pltpu.VMEM)],
        out_specs=pl.BlockSpec(memory_space=pltpu.VMEM),
        scratch_shapes=[pltpu.SemaphoreType.DMA((7,)), pltpu.SemaphoreType.DMA((7,)), pltpu.SemaphoreType.DMA],
    )(pack)


def _add_pair(g, got, c_arr, splits=None):
    _, rows, _ = g.shape
    splits = splits or [rows]

    def body(c_ref, g_ref, r_ref, *o_refs):
        at = 0
        for o_ref, n in zip(o_refs, splits):
            o_ref[...] = (g_ref[:, at:at + n, :] + r_ref[:, at:at + n, :]).astype(BF16)
            at += n

    return pl.pallas_call(
        body, name="grad_add_pair", out_shape=[jax.ShapeDtypeStruct((4, n, HALF), BF16) for n in splits],
        grid_spec=pltpu.PrefetchScalarGridSpec(
            num_scalar_prefetch=1, grid=(4,),
            in_specs=[pl.BlockSpec((1, rows, HALF), lambda j, c: (j, 0, c[0])),
                      pl.BlockSpec((1, rows, HALF), lambda j, c: (j, 0, 0))],
            out_specs=[pl.BlockSpec((1, n, HALF), lambda j, c: (j, 0, 0)) for n in splits]),
        compiler_params=_params(("arbitrary",)),
    )(c_arr, g, got)


def _add_chips(g, pair_got, chip_gots, pos_arr):
    _, rows, _ = g.shape

    def body(pos_ref, g_ref, p_ref, *refs):
        o_ref, at = refs[-1], 0
        for r_ref in refs[:-1]:
            n = r_ref.shape[1]
            own = g_ref[0, at:at + n, :] + p_ref[0, at:at + n, :]
            o_ref[at:at + n, :] = ((own + r_ref[0].astype(F32)) + r_ref[1].astype(F32)) + r_ref[2].astype(F32)
            at += n

    return pl.pallas_call(
        body, name="grad_add_chips", out_shape=jax.ShapeDtypeStruct((rows, HALF), F32),
        grid_spec=pltpu.PrefetchScalarGridSpec(
            num_scalar_prefetch=1, grid=(1,),
            in_specs=[pl.BlockSpec((1, rows, HALF), lambda i, p: (p[0], 0, p[1])),
                      pl.BlockSpec((1, rows, HALF), lambda i, p: (p[0], 0, 0))]
            + [pl.BlockSpec(t.shape, lambda i, p: (0, 0, 0)) for t in chip_gots],
            out_specs=pl.BlockSpec((rows, HALF), lambda i, p: (0, 0))),
        compiler_params=_params(("arbitrary",)),
    )(pos_arr, g, pair_got, *chip_gots)


def _pack_small(acc2, acc1, lvec, cacc, gvec, dcw, dgup, dmeta):
    bsz = lvec.shape[0]

    def body(a2_ref, a1_ref, lv_ref, ca_ref, gv_ref, cw_ref, gu_ref, dm_ref, o_ref):
        def put(name, val):
            r0, nr, nl = SMALL_AT[name]
            o_ref[r0:r0 + nr, 0:nl] = val

        over_b = lambda f: functools.reduce(lambda a, b: a + b, [f(b) for b in range(bsz)])
        o_ref[...] = jnp.zeros_like(o_ref)
        put("loss", a2_ref[0:1, :])
        put("ln2_g", a2_ref[1:2, :])
        put("ln2_b", a2_ref[2:3, :])
        put("ln1_g", a1_ref[0:1, :])
        put("ln1_b", a1_ref[1:2, :])
        put("ln_in_g", over_b(lambda b: lv_ref[b, 0:1, :]))
        put("ln_in_b", over_b(lambda b: lv_ref[b, 1:2, :]))
        put("conv_b", ca_ref[0:1, :])
        put("conv_ln_g", ca_ref[1:2, :])
        put("conv_ln_b", ca_ref[2:3, :])
        put("gate_bias", over_b(lambda b: gv_ref[b, 0:1, :]))
        put("gla_norm_g", over_b(lambda b: gv_ref[b, 1:2, 0:128] + gv_ref[b, 1:2, 128:256]))
        put("conv_w", over_b(lambda b: cw_ref[b]))
        put("gate_up", over_b(lambda b: gu_ref[b, 0:GLA_RANK, :]))
        put("meta_tokens", over_b(lambda b: dm_ref[b]))

    return pl.pallas_call(
        body, name="pack_small", out_shape=jax.ShapeDtypeStruct((SMALL_ROWS, D_MODEL), F32),
    )(acc2, acc1, lvec, cacc, gvec, dcw, dgup, dmeta)


def _sum_small(gathered):
    def body(g_ref, o_ref, loss_ref):
        acc = g_ref[0:SMALL_ROWS, :]
        for d in range(1, 8):
            acc = acc + g_ref[d * SMALL_ROWS:(d + 1) * SMALL_ROWS, :]
        o_ref[...] = acc
        loss_ref[...] = jnp.sum(acc[0:1, :], axis=1, keepdims=True)

    return pl.pallas_call(
        body, name="sum_small",
        out_shape=[jax.ShapeDtypeStruct((SMALL_ROWS, D_MODEL), F32), jax.ShapeDtypeStruct((1, 1), F32)],
    )(gathered)


def _adamw_math(w, g, m, v):
    c1 = 1.0 - ADAM_B1 ** ADAM_STEP
    c2 = 1.0 - ADAM_B2 ** ADAM_STEP
    mn = ADAM_B1 * m + (1.0 - ADAM_B1) * g
    vn = ADAM_B2 * v + (1.0 - ADAM_B2) * (g * g)
    return -ADAM_LR * ((mn / c1) / (jnp.sqrt(vn / c2) + ADAM_EPS) + ADAM_WD * w), mn, vn


def _adamw_small(red, sharded_grads, params):
    names = list(params)
    sharded = [k for k in names if k in sharded_grads]
    n, ns = len(names), len(sharded)

    def body(*refs):
        red_ref, sg_refs, p_refs, o_refs = refs[0], refs[1:1 + ns], refs[1 + ns:1 + ns + 3 * n], refs[1 + ns + 3 * n:]
        for i, k in enumerate(names):
            w_ref, m_ref, v_ref = p_refs[3 * i:3 * i + 3]
            if k in sharded:
                g = sg_refs[sharded.index(k)][...]
            else:
                r0, nr, nl = SMALL_AT[k]
                g = red_ref[r0:r0 + nr, 0:nl]
            dl, mn, vn = _adamw_math(w_ref[...], g, m_ref[...], v_ref[...])
            for o_ref, val in zip(o_refs[4 * i:4 * i + 4], (g, dl, mn, vn)):
                o_ref[...] = val

    flat = [a for k in names for a in params[k]]
    res = pl.pallas_call(
        body, name="adamw_small",
        out_shape=[jax.ShapeDtypeStruct(params[k][0].shape, F32) for k in names for _ in range(4)],
    )(red, *[sharded_grads[k] for k in sharded], *flat)
    return {k: tuple(res[4 * i:4 * i + 4]) for i, k in enumerate(names)}


def _adamw(w, g, m, v):
    rows, cols = w.shape
    tr = 256 if rows % 256 == 0 else rows

    def body(w_ref, g_ref, m_ref, v_ref, d_ref, mo_ref, vo_ref):
        d_ref[...], mo_ref[...], vo_ref[...] = _adamw_math(w_ref[...], g_ref[...], m_ref[...], v_ref[...])

    spec = pl.BlockSpec((tr, cols), lambda i: (i, 0))
    return pl.pallas_call(
        body, name="adamw", out_shape=[jax.ShapeDtypeStruct(w.shape, F32)] * 3,
        grid=(rows // tr,), in_specs=[spec] * 4, out_specs=[spec] * 3,
        compiler_params=_params(("parallel",)),
    )(w, g, m, v)


def _ln_in_fwd(x, meta, g, b, fuse=NO_FUSE):
    bsz, s, d = x.shape
    tp = s + HEAD
    nh = 2
    sh = s // nh
    rc = min(256, sh)

    def body(x_ref, meta_ref, g_ref, b_ref, s0_ref, s0b_ref):
        h = pl.program_id(1)
        gg, bb = g_ref[...], b_ref[...]

        @pl.when(h == 0)
        def _():
            s0_ref[0, 0:PAD, :] = jnp.zeros((PAD, d), F32)
            s0b_ref[0, 0:PAD, :] = jnp.zeros((PAD, d), BF16)
            mh, _ = _ln_stats(meta_ref[...])
            mv = mh * gg + bb
            s0_ref[0, PAD:HEAD, :] = mv
            s0b_ref[0, PAD:HEAD, :] = mv.astype(BF16)

        def step(i, carry):
            src = pl.ds(pl.multiple_of(i * rc, rc), rc)
            dst = pl.ds(pl.multiple_of(HEAD + h * sh + i * rc, 64), rc)
            xh, _ = _ln_stats(x_ref[0, src, :])
            val = xh * gg + bb
            s0_ref[0, dst, :] = val
            s0b_ref[0, dst, :] = val.astype(BF16)
            return carry

        lax.fori_loop(0, sh // rc, step, 0)

    full = lambda bi, hi: (bi, 0, 0)
    (s0, s0b), got = _fused_call(
        body, fuse, name="ln_in_fwd",
        out_shape=[jax.ShapeDtypeStruct((bsz, tp, d), F32), jax.ShapeDtypeStruct((bsz, tp, d), BF16)],
        grid=(bsz, nh),
        in_specs=[pl.BlockSpec((1, sh, d), lambda bi, hi: (bi, hi, 0)),
                  pl.BlockSpec((N_META, d), lambda bi, hi: (0, 0)),
                  pl.BlockSpec((1, d), lambda bi, hi: (0, 0)),
                  pl.BlockSpec((1, d), lambda bi, hi: (0, 0))],
        out_specs=[pl.BlockSpec((1, tp, d), full)] * 2, scratch_shapes=[], operands=(x, meta, g, b))
    return s0, s0b, got


def _in_proj(s0b, w_int, fuse=NO_FUSE):
    r, d = s0b.shape
    tm = _row_tile(r)

    def body(a_ref, w_ref, o_ref):
        o_ref[...] = _nt(a_ref[...], w_ref[...])

    (u,), got = _fused_call(
        body, fuse, name="in_proj", out_shape=[jax.ShapeDtypeStruct((r, D_IN_PAD), F32)],
        grid=(r // tm,),
        in_specs=[pl.BlockSpec((tm, d), lambda i: (i, 0)), pl.BlockSpec((D_IN_PAD, d), lambda i: (0, 0))],
        out_specs=[pl.BlockSpec((tm, D_IN_PAD), lambda i: (i, 0))], scratch_shapes=[], operands=(s0b, w_int))
    return u, got


def _conv_fwd(u, conv_w, conv_b, fuse=NO_FUSE):
    bsz, tp, _ = u.shape
    nchunk = tp // CHUNK
    win = CHUNK + 32
    nct = D_CONV // 128

    def body(cv_ref, cg_ref, w_ref, cb_ref, hc_ref, h_scr, win_scr):
        h_scr[0:32, :] = jnp.zeros((32, 128), F32)
        h_scr[32:32 + tp, :] = cv_ref[0] * _sigmoid(cg_ref[0])
        cb = cb_ref[...]

        def step(n, carry):
            r0 = pl.multiple_of(n * CHUNK, CHUNK)
            win_scr[...] = h_scr[pl.ds(r0, win), :]
            acc = jnp.zeros((CHUNK, 128), F32)
            for j in range(CONV_WIDTH):
                acc = acc + w_ref[j:j + 1, :] * win_scr[2 + j:2 + j + CHUNK, :]
            hc_ref[0, pl.ds(r0, CHUNK), :] = acc + cb
            return carry

        lax.fori_loop(0, nchunk, step, 0)

    (hc,), got = _fused_call(
        body, fuse, name="conv_fwd", out_shape=[jax.ShapeDtypeStruct((bsz, tp, D_CONV), F32)],
        grid=(bsz, nct),
        in_specs=[pl.BlockSpec((1, tp, 128), lambda bi, t: (bi, 0, C_VAL // 128 + t)),
                  pl.BlockSpec((1, tp, 128), lambda bi, t: (bi, 0, C_GATE // 128 + t)),
                  pl.BlockSpec((32, 128), lambda bi, t: (0, t)),
                  pl.BlockSpec((1, 128), lambda bi, t: (0, t))],
        out_specs=[pl.BlockSpec((1, tp, 128), lambda bi, t: (bi, 0, t))],
        scratch_shapes=[pltpu.VMEM((tp + 32, 128), F32), pltpu.VMEM((win, 128), F32)],
        operands=(u, u, conv_w, conv_b))
    return hc, got


def _gla_group(nchunk):
    return 11 if nchunk % 11 == 0 else nchunk


def _bdot(a, b, ca, cb, precision=None):
    return lax.dot_general(a, b, (((ca,), (cb,)), ((0,), (0,))), preferred_element_type=F32, precision=precision)


def _bnn(a, b, **kw):
    return _bdot(a, b, 2, 1, **kw)


def _bnt(a, b, **kw):
    return _bdot(a, b, 2, 2, **kw)


def _gla_consts(nb):
    row = lax.broadcasted_iota(jnp.int32, (nb, CHUNK, CHUNK), 1)
    col = lax.broadcasted_iota(jnp.int32, (nb, CHUNK, CHUNK), 2)
    lane = lax.broadcasted_iota(jnp.int32, (1, 1, 128), 2)
    return row >= col, row <= col, [lane < GLA_DK, lane >= GLA_DK]


def _gla_group_terms(g, nb, q_ref, k_ref, gd_ref, gup_ref, gb_ref, tril):
    m = nb * CHUNK
    rows = pl.ds(pl.multiple_of(g * m, CHUNK), m)
    z = _nn(gd_ref[0, rows, :].astype(BF16), gup_ref[...]) + gb_ref[...]
    valid = g * m + lax.broadcasted_iota(jnp.int32, (m, 1), 0) >= PAD
    lg = jnp.where(valid, _log_sigmoid(z) * (1.0 / GLA_TAU), 0.0)
    bcum = _bnn(tril.astype(F32), lg.reshape(nb, CHUNK, 128), precision=lax.Precision.HIGHEST)
    blast = bcum[:, CHUNK - 1:CHUNK, :]
    eb = jnp.exp(bcum)
    enb = jnp.exp(-bcum)
    erest = jnp.exp(blast - bcum)
    q = (q_ref[0, rows, :] * Q_SCALE).reshape(nb, CHUNK, 128)
    k = k_ref[0, rows, :].reshape(nb, CHUNK, 128)
    return rows, valid, z, eb, enb, erest, jnp.exp(blast), q * eb, k * enb, k * erest


def _grid_ends(grid):
    ids = [pl.program_id(i) for i in range(len(grid))]
    first = functools.reduce(jnp.logical_and, [i == 0 for i in ids])
    last = functools.reduce(jnp.logical_and, [i == g - 1 for i, g in zip(ids, grid)])
    return first, last


def _gla_fwd(u, gup, gbias, gnorm, fuse=NO_FUSE):
    bsz, tp, _ = u.shape
    nchunk = tp // CHUNK
    nb = _gla_group(nchunk)

    def body(q_ref, k_ref, v_ref, r_ref, gd_ref, gup_ref, gb_ref, gn_ref, out_ref, o_ref, st_ref, s_scr):
        tril, _, hmask = _gla_consts(nb)
        s_scr[...] = jnp.zeros_like(s_scr)
        gn = gn_ref[...]

        def group(g, carry):
            rows, _, _, _, _, _, dec, qe, ke, kd = _gla_group_terms(g, nb, q_ref, k_ref, gd_ref, gup_ref, gb_ref, tril)
            keb, kdb = ke.astype(BF16), kd.astype(BF16)
            for h in range(2):
                cols = slice(h * GLA_DV, (h + 1) * GLA_DV)
                qh = jnp.where(hmask[h], qe, 0.0).astype(BF16)
                vh = v_ref[0, rows, cols].astype(BF16).reshape(nb, CHUNK, GLA_DV)
                a = jnp.where(tril, _bnt(qh, keb), 0.0).astype(BF16)
                st = s_scr[h]
                sts = []
                for n in range(nb):
                    st_ref[0, h, g * nb + n] = st
                    sts.append(st.astype(BF16))
                    st = dec[n] * st + _tn(vh[n], kdb[n])
                s_scr[h] = st
                o = (_bnn(a, vh) + _bnt(qh, jnp.stack(sts))).reshape(nb * CHUNK, GLA_DV)
                o_ref[0, rows, cols] = o
                rms = lax.rsqrt(jnp.mean(o * o, axis=-1, keepdims=True) + LN_EPS)
                rh = r_ref[0, rows, cols]
                out_ref[0, rows, cols] = (o * rms * gn * (rh * _sigmoid(rh))).astype(BF16)
            return carry

        lax.fori_loop(0, nchunk // nb, group, 0)

    res, got = _fused_call(
        body, fuse, name="gla_fwd",
        out_shape=[jax.ShapeDtypeStruct((bsz, tp, 512), BF16), jax.ShapeDtypeStruct((bsz, tp, 512), F32),
                   jax.ShapeDtypeStruct((bsz, GLA_HEADS, nchunk, GLA_DV, 128), F32)],
        grid=(bsz, 2),
        in_specs=[pl.BlockSpec((1, tp, 128), lambda bi, p: (bi, 0, C_Q // 128 + p)),
                  pl.BlockSpec((1, tp, 128), lambda bi, p: (bi, 0, C_K // 128 + p)),
                  pl.BlockSpec((1, tp, 256), lambda bi, p: (bi, 0, C_V // 256 + p)),
                  pl.BlockSpec((1, tp, 256), lambda bi, p: (bi, 0, C_R // 256 + p)),
                  pl.BlockSpec((1, tp, 128), lambda bi, p: (bi, 0, C_GD // 128)),
                  pl.BlockSpec((128, 128), lambda bi, p: (0, p)),
                  pl.BlockSpec((1, 128), lambda bi, p: (0, p)),
                  pl.BlockSpec((1, 128), lambda bi, p: (0, 0))],
        out_specs=[pl.BlockSpec((1, tp, 256), lambda bi, p: (bi, 0, p)),
                   pl.BlockSpec((1, tp, 256), lambda bi, p: (bi, 0, p)),
                   pl.BlockSpec((1, 2, nchunk, GLA_DV, 128), lambda bi, p: (bi, p, 0, 0, 0))],
        scratch_shapes=[pltpu.VMEM((2, GLA_DV, 128), F32)],
        operands=(u, u, u, u, u, gup, gbias, gnorm))
    return res[0], res[1], res[2], got


def _out_proj_ln1(hc, gla_out, w_out, s0, cg, cb, g1, b1, fuse=NO_FUSE):
    r, d = s0.shape
    tm = _row_tile(r)

    def body(hc_ref, a_ref, w_ref, s0_ref, cg_ref, cb_ref, g_ref, b_ref, co_ref, xh_ref, rstd_ref, s1b_ref):
        for rs in _sub_rows(tm):
            xc, _ = _ln_stats(hc_ref[rs, :])
            nv = xc * cg_ref[...] + cb_ref[...]
            co = (nv * _sigmoid(nv)).astype(BF16)
            co_ref[rs, :] = co
            mix = _nn(co, w_ref[0:D_CONV, :]) + _nn(a_ref[rs, :], w_ref[D_CONV:, :])
            xh, rstd = _ln_stats(ALPHA * s0_ref[rs, :] + mix)
            xh_ref[rs, :] = xh
            rstd_ref[rs, :] = rstd
            s1b_ref[rs, :] = (xh * g_ref[...] + b_ref[...]).astype(BF16)

    row = lambda n: pl.BlockSpec((tm, n), lambda i: (i, 0))
    vec = lambda n: pl.BlockSpec((1, n), lambda i: (0, 0))
    res, got = _fused_call(
        body, fuse, name="out_proj_ln1",
        out_shape=[jax.ShapeDtypeStruct((r, D_CONV), BF16), jax.ShapeDtypeStruct((r, d), F32),
                   jax.ShapeDtypeStruct((r, 1), F32), jax.ShapeDtypeStruct((r, d), BF16)],
        grid=(r // tm,),
        in_specs=[row(D_CONV), row(512), pl.BlockSpec((d, d), lambda i: (0, 0)), row(d),
                  vec(D_CONV), vec(D_CONV), vec(d), vec(d)],
        out_specs=[row(D_CONV), row(d), row(1), row(d)], scratch_shapes=[],
        operands=(hc, gla_out, w_out, s0, cg, cb, g1, b1))
    return res[0], res[1], res[2], res[3], got


def _ffn1(s1b, w1, fuse=NO_FUSE):
    r, d = s1b.shape
    tm = _row_tile(r)
    ns, _, wn = w1.shape

    def body(a_ref, w_ref, o_ref):
        a = a_ref[...]
        for j in range(ns):
            o_ref[:, j * wn:(j + 1) * wn] = jnp.maximum(_nn(a, w_ref[j]), 0.0).astype(BF16)

    (ra,), got = _fused_call(
        body, fuse, name="ffn1", out_shape=[jax.ShapeDtypeStruct((r, D_FF), BF16)], grid=(r // tm,),
        in_specs=[pl.BlockSpec((tm, d), lambda i: (i, 0)), pl.BlockSpec(w1.shape, lambda i: (0, 0, 0))],
        out_specs=[pl.BlockSpec((tm, D_FF), lambda i: (i, 0))], scratch_shapes=[], operands=(s1b, w1))
    return ra, got


def _ffn2_ln2_loss(ra, w2, xhat1, g1, b1, g2, b2, tgt, tp):
    r, d = xhat1.shape
    tm = _row_tile(tp)
    per = tp // tm

    def body(ra_ref, w_ref, xh1_ref, g1_ref, b1_ref, g2_ref, b2_ref, tgt_ref, dt_ref, dtb_ref, acc_ref, t_ref, sem):
        i = pl.program_id(0)
        b, j = i // per, i % per

        @pl.when(i == 0)
        def _():
            acc_ref[...] = jnp.zeros_like(acc_ref)

        head_copy = pltpu.make_async_copy(tgt_ref.at[b, pl.ds(0, tm - HEAD), :], t_ref.at[pl.ds(HEAD, tm - HEAD), :], sem)
        body_copy = pltpu.make_async_copy(
            tgt_ref.at[b, pl.ds(pl.multiple_of(jnp.maximum(j * tm - HEAD, 0), 64), tm), :], t_ref, sem)

        @pl.when(j == 0)
        def _():
            t_ref[0:HEAD, :] = jnp.zeros((HEAD, d), F32)
            head_copy.start()

        pl.when(j > 0)(body_copy.start)

        sums = [jnp.zeros((1, d), F32)] * 3
        for rs in _sub_rows(tm):
            rb = ra_ref[rs, :]
            f = _nn(rb * rb, w_ref[...])
            if rs.start == 0:
                pl.when(j == 0)(head_copy.wait)
                pl.when(j > 0)(body_copy.wait)
            s1 = xh1_ref[rs, :] * g1_ref[...] + b1_ref[...]
            xh2, rstd2 = _ln_stats(ALPHA * s1 + f)
            y = xh2 * g2_ref[...] + b2_ref[...]
            rowid = (i % per) * tm + rs.start + lax.broadcasted_iota(jnp.int32, (rs.stop - rs.start, 1), 0)
            e = jnp.where(rowid >= HEAD, y - t_ref[rs, :], 0.0)
            dy = e * (1.0 / d)
            dt2 = _ln_bwd(dy * g2_ref[...], xh2, rstd2)
            dt_ref[rs, :] = dt2
            dtb_ref[rs, :] = dt2.astype(BF16)
            sums = [sums[0] + (0.5 / d) * jnp.sum(e * e, axis=0, keepdims=True),
                    sums[1] + jnp.sum(dy * xh2, axis=0, keepdims=True), sums[2] + jnp.sum(dy, axis=0, keepdims=True)]
        for k in range(3):
            acc_ref[k:k + 1, :] += sums[k]

    row = lambda n: pl.BlockSpec((tm, n), lambda i: (i, 0))
    vec = pl.BlockSpec((1, d), lambda i: (0, 0))
    return pl.pallas_call(
        body, name="ffn2_ln2_loss",
        out_shape=[jax.ShapeDtypeStruct((r, d), F32), jax.ShapeDtypeStruct((r, d), BF16),
                   jax.ShapeDtypeStruct((8, d), F32)],
        grid=(r // tm,),
        in_specs=[row(D_FF), pl.BlockSpec((D_FF, d), lambda i: (0, 0)), row(d), vec, vec, vec, vec, ANY],
        out_specs=[row(d), row(d), pl.BlockSpec((8, d), lambda i: (0, 0))],
        scratch_shapes=[pltpu.VMEM((tm, d), F32), pltpu.SemaphoreType.DMA],
        compiler_params=_params(("arbitrary",)),
    )(ra, w2, xhat1, g1, b1, g2, b2, tgt)


def _ffn_bwd_da(dt2b, w2, ra):
    r, d = dt2b.shape
    tm = _row_tile(r)

    def body(g_ref, w_ref, ra_ref, o_ref):
        o_ref[...] = (_nt(g_ref[...], w_ref[...]) * (2.0 * ra_ref[...].astype(F32))).astype(BF16)

    return pl.pallas_call(
        body, name="ffn_bwd_da", out_shape=jax.ShapeDtypeStruct((r, D_FF), BF16),
        grid=(r // tm,),
        in_specs=[pl.BlockSpec((tm, d), lambda i: (i, 0)), pl.BlockSpec((D_FF, d), lambda i: (0, 0)),
                  pl.BlockSpec((tm, D_FF), lambda i: (i, 0))],
        out_specs=pl.BlockSpec((tm, D_FF), lambda i: (i, 0)),
        compiler_params=_params(("parallel",)),
    )(dt2b, w2, ra)


def _ffn_bwd_ln1(da, w1, dt2, xhat1, rstd1, g1):
    r, d = dt2.shape
    tm = _row_tile(r)

    def body(da_ref, w_ref, dt2_ref, xh_ref, rstd_ref, g_ref, dt_ref, dtb_ref, acc_ref):
        @pl.when(pl.program_id(0) == 0)
        def _():
            acc_ref[...] = jnp.zeros_like(acc_ref)

        sums = [jnp.zeros((1, d), F32)] * 2
        for rs in _sub_rows(tm):
            ds1 = ALPHA * dt2_ref[rs, :]
            for j in range(w1.shape[0]):
                ds1 = ds1 + _nt(da_ref[rs, j * w1.shape[2]:(j + 1) * w1.shape[2]], w_ref[j])
            xh = xh_ref[rs, :]
            dt1 = _ln_bwd(ds1 * g_ref[...], xh, rstd_ref[rs, :])
            dt_ref[rs, :] = dt1
            dtb_ref[rs, :] = dt1.astype(BF16)
            sums = [sums[0] + jnp.sum(ds1 * xh, axis=0, keepdims=True), sums[1] + jnp.sum(ds1, axis=0, keepdims=True)]
        for k in range(2):
            acc_ref[k:k + 1, :] += sums[k]

    row = lambda n: pl.BlockSpec((tm, n), lambda i: (i, 0))
    return pl.pallas_call(
        body, name="ffn_bwd_ln1",
        out_shape=[jax.ShapeDtypeStruct((r, d), F32), jax.ShapeDtypeStruct((r, d), BF16),
                   jax.ShapeDtypeStruct((8, d), F32)],
        grid=(r // tm,),
        in_specs=[row(D_FF), pl.BlockSpec(w1.shape, lambda i: (0, 0, 0)), row(d), row(d), row(1),
                  pl.BlockSpec((1, d), lambda i: (0, 0))],
        out_specs=[row(d), row(d), pl.BlockSpec((8, d), lambda i: (0, 0))],
        compiler_params=_params(("arbitrary",)),
    )(da, w1, dt2, xhat1, rstd1, g1)


def _matmul_tn(lhs, rhs, bm, square_lhs=False, name="matmul_tn", fuse=NO_FUSE):
    r, m = lhs.shape
    n = rhs.shape[1]
    tk = _reduce_tile(r, True)

    def body(a_ref, b_ref, o_ref):
        @pl.when(pl.program_id(1) == 0)
        def _():
            o_ref[...] = jnp.zeros_like(o_ref)

        a = a_ref[...]
        if square_lhs:
            a = a * a
        o_ref[...] += _tn(a, b_ref[...])

    (out,), got = _fused_call(
        body, fuse, name=name, out_shape=[jax.ShapeDtypeStruct((m, n), F32)], grid=(m // bm, r // tk),
        in_specs=[pl.BlockSpec((tk, bm), lambda i, k: (k, i)), pl.BlockSpec((tk, n), lambda i, k: (k, 0))],
        out_specs=[pl.BlockSpec((bm, n), lambda i, k: (i, 0))], scratch_shapes=[], operands=(lhs, rhs))
    return out, got


def _grad_w_ff1(s1b, da, fuse=NO_FUSE):
    r, d = s1b.shape
    wn = da.shape[1] // 4
    tk = _reduce_tile(r, True)

    def body(a_ref, b_ref, o_ref):
        @pl.when(pl.program_id(1) == 0)
        def _():
            o_ref[...] = jnp.zeros_like(o_ref)

        o_ref[0] += _tn(a_ref[...], b_ref[...])

    (out,), got = _fused_call(
        body, fuse, name="grad_w_ff1", out_shape=[jax.ShapeDtypeStruct((4, d, wn), F32)], grid=(4, r // tk),
        in_specs=[pl.BlockSpec((tk, d), lambda j, k: (k, 0)), pl.BlockSpec((tk, wn), lambda j, k: (k, j))],
        out_specs=[pl.BlockSpec((1, d, wn), lambda j, k: (j, 0, 0))], scratch_shapes=[], operands=(s1b, da))
    return out, got


def _grad_w_out(conv_of, gla_of, dt1b):
    r, n = dt1b.shape
    tk = _reduce_tile(r, True)

    def body(a_ref, b_ref, g_ref, o_ref):
        @pl.when(pl.program_id(1) == 0)
        def _():
            o_ref[...] = jnp.zeros_like(o_ref)

        @pl.when(pl.program_id(0) == 0)
        def _():
            o_ref[...] += _tn(a_ref[...], g_ref[...])

        @pl.when(pl.program_id(0) == 1)
        def _():
            o_ref[...] += _tn(b_ref[...], g_ref[...])

    lhs = pl.BlockSpec((tk, 512), lambda i, k: (k, 0))
    return pl.pallas_call(
        body, name="grad_w_out", out_shape=jax.ShapeDtypeStruct((2 * 512, n), F32), grid=(2, r // tk),
        in_specs=[lhs, lhs, pl.BlockSpec((tk, n), lambda i, k: (k, 0))],
        out_specs=pl.BlockSpec((512, n), lambda i, k: (i, 0)),
        compiler_params=_params(("parallel", "arbitrary")),
    )(conv_of, gla_of, dt1b)


def _out_proj_bwd(dt1b, w_out, hc, cg, cb, fuse=NO_FUSE):
    r, d = dt1b.shape
    tm = _row_tile(r)

    def body(g_ref, w_ref, hc_ref, cg_ref, cb_ref, dhc_ref, dgla_ref, acc_ref):
        @pl.when(pl.program_id(0) == 0)
        def _():
            acc_ref[...] = jnp.zeros_like(acc_ref)

        gg = cg_ref[...]
        sums = [jnp.zeros((1, D_CONV), F32)] * 3
        for rs in _sub_rows(tm):
            dmix = _nt(g_ref[rs, :], w_ref[...])
            dgla_ref[rs, :] = dmix[:, D_CONV:]
            xh, rstd = _ln_stats(hc_ref[rs, :])
            nv = xh * gg + cb_ref[...]
            sig = _sigmoid(nv)
            dn = dmix[:, :D_CONV] * (sig * (1.0 + nv * (1.0 - sig)))
            dhc = _ln_bwd(dn * gg, xh, rstd)
            dhc_ref[rs, :] = dhc
            sums = [sums[0] + jnp.sum(dhc, axis=0, keepdims=True), sums[1] + jnp.sum(dn * xh, axis=0, keepdims=True),
                    sums[2] + jnp.sum(dn, axis=0, keepdims=True)]
        for k in range(3):
            acc_ref[k:k + 1, :] += sums[k]

    row = lambda n: pl.BlockSpec((tm, n), lambda i: (i, 0))
    vec = pl.BlockSpec((1, D_CONV), lambda i: (0, 0))
    res, got = _fused_call(
        body, fuse, name="out_proj_bwd",
        out_shape=[jax.ShapeDtypeStruct((r, D_CONV), F32), jax.ShapeDtypeStruct((r, 512), F32),
                   jax.ShapeDtypeStruct((8, D_CONV), F32)],
        grid=(r // tm,),
        in_specs=[row(d), pl.BlockSpec((d, d), lambda i: (0, 0)), row(D_CONV), vec, vec],
        out_specs=[row(D_CONV), row(512), pl.BlockSpec((8, D_CONV), lambda i: (0, 0))], scratch_shapes=[],
        operands=(dt1b, w_out, hc, cg, cb))
    return res[0], res[1], res[2], got


def _conv_bwd(dhc, u, conv_w, fuse=NO_FUSE):
    bsz, tp, _ = u.shape
    nchunk = tp // CHUNK
    win = CHUNK + 32
    nct = D_CONV // 128

    def body(dhc_ref, cv_ref, cg_ref, w_ref, dv_ref, dg_ref, dw_ref, h_scr, dhc_scr, hwin, dwin, dw_scr):
        h_scr[0:32, :] = jnp.zeros((32, 128), F32)
        h_scr[32:32 + tp, :] = cv_ref[0] * _sigmoid(cg_ref[0])
        dhc_scr[0:tp, :] = dhc_ref[0]
        dhc_scr[tp:tp + 32, :] = jnp.zeros((32, 128), F32)
        dw_scr[...] = jnp.zeros_like(dw_scr)

        def step(n, carry):
            r0 = pl.multiple_of(n * CHUNK, CHUNK)
            rows = pl.ds(r0, CHUNK)
            hwin[...] = h_scr[pl.ds(r0, win), :]
            dwin[...] = dhc_scr[pl.ds(r0, win), :]
            dcur = dwin[0:CHUNK, :]
            acc = jnp.zeros((CHUNK, 128), F32)
            for j in range(CONV_WIDTH):
                acc = acc + w_ref[j:j + 1, :] * dwin[30 - j:30 - j + CHUNK, :]
                prod = dcur * hwin[2 + j:2 + j + CHUNK, :]
                dw_scr[j * 8:(j + 1) * 8, :] += jnp.sum(prod.reshape(CHUNK // 8, 8, 128), axis=0)
            cg = cg_ref[0, rows, :]
            sig = _sigmoid(cg)
            rowid = n * CHUNK + lax.broadcasted_iota(jnp.int32, (CHUNK, 1), 0)
            dh = jnp.where(rowid >= PAD, acc, 0.0)
            dv_ref[0, rows, :] = (dh * sig).astype(BF16)
            dg_ref[0, rows, :] = (dh * cv_ref[0, rows, :] * sig * (1.0 - sig)).astype(BF16)
            return carry

        lax.fori_loop(0, nchunk, step, 0)
        dw_ref[0] = jnp.zeros((32, 128), F32)
        for j in range(CONV_WIDTH):
            dw_ref[0, j:j + 1, :] = jnp.sum(dw_scr[j * 8:(j + 1) * 8, :], axis=0, keepdims=True)

    blk = lambda off: pl.BlockSpec((1, tp, 128), lambda bi, t: (bi, 0, off // 128 + t))
    res, got = _fused_call(
        body, fuse, name="conv_bwd",
        out_shape=[jax.ShapeDtypeStruct((bsz, tp, D_CONV), BF16), jax.ShapeDtypeStruct((bsz, tp, D_CONV), BF16),
                   jax.ShapeDtypeStruct((bsz, 32, D_CONV), F32)],
        grid=(bsz, nct),
        in_specs=[blk(0), blk(C_VAL), blk(C_GATE), pl.BlockSpec((32, 128), lambda bi, t: (0, t))],
        out_specs=[blk(0), blk(0), pl.BlockSpec((1, 32, 128), lambda bi, t: (bi, 0, t))],
        scratch_shapes=[pltpu.VMEM((tp + 32, 128), F32), pltpu.VMEM((tp + 32, 128), F32),
                        pltpu.VMEM((win, 128), F32), pltpu.VMEM((win, 128), F32),
                        pltpu.VMEM((CONV_WIDTH * 8, 128), F32)],
        operands=(dhc, u, u, conv_w))
    return res[0], res[1], res[2], got


def _gla_bwd(dgla, u, o_pre, states, gup, gbias, gnorm, fuse=NO_FUSE):
    bsz, tp, _ = u.shape
    nchunk = tp // CHUNK
    nb = _gla_group(nchunk)

    def body(dy_ref, q_ref, k_ref, v_ref, r_ref, gd_ref, o_ref, st_ref, gup_ref, gb_ref, gn_ref,
             dq_ref, dk_ref, dv_ref, dr_ref, dgd_ref, dgup_ref, vec_ref, h_scr, gup_acc):
        tril, triu, hmask = _gla_consts(nb)
        h_scr[...] = jnp.zeros_like(h_scr)
        gup_acc[...] = jnp.zeros_like(gup_acc)
        gn = gn_ref[...]
        gupb = gup_ref[...]
        m = nb * CHUNK
        ngroup = nchunk // nb

        def group(i, carry):
            dbias, dgn = carry
            g = ngroup - 1 - i
            rows, valid, z, eb, enb, erest, dec, qe, ke, kd = _gla_group_terms(
                g, nb, q_ref, k_ref, gd_ref, gup_ref, gb_ref, tril)
            keb, kdb = ke.astype(BF16), kd.astype(BF16)
            dqe = jnp.zeros((nb, CHUNK, 128), F32)
            dke = jnp.zeros((nb, CHUNK, 128), F32)
            dkd = jnp.zeros((nb, CHUNK, 128), F32)
            ddec = jnp.zeros((nb, 1, 128), F32)
            for h in range(2):
                cols = slice(h * GLA_DV, (h + 1) * GLA_DV)
                o = o_ref[0, rows, cols]
                rh = r_ref[0, rows, cols]
                dy = dy_ref[0, rows, cols]
                rms = lax.rsqrt(jnp.mean(o * o, axis=-1, keepdims=True) + LN_EPS)
                nrm = o * rms
                sig = _sigmoid(rh)
                sw = rh * sig
                dr_ref[0, rows, cols] = (dy * nrm * gn * (sig * (1.0 + rh * (1.0 - sig)))).astype(BF16)
                dgn = dgn + jnp.sum(dy * nrm * sw, axis=0, keepdims=True)
                dn = dy * gn * sw
                do = rms * (dn - nrm * jnp.mean(dn * nrm, axis=-1, keepdims=True))
                dob = do.astype(BF16).reshape(nb, CHUNK, GLA_DV)
                qh = jnp.where(hmask[h], qe, 0.0).astype(BF16)
                vh = v_ref[0, rows, cols].astype(BF16).reshape(nb, CHUNK, GLA_DV)
                ht = h_scr[h]
                hts = [None] * nb
                for n in reversed(range(nb)):
                    hts[n] = ht
                    ht = dec[n] * ht + _tn(dob[n], qh[n])
                h_scr[h] = ht
                htf = jnp.stack(hts)
                htb = htf.astype(BF16)
                st = st_ref[0, h, pl.ds(g * nb, nb)]
                at = jnp.where(triu, _bnt(keb, qh), 0.0).astype(BF16)
                da = jnp.where(tril, _bnt(dob, vh), 0.0).astype(BF16)
                dat = jnp.where(triu, _bnt(vh, dob), 0.0).astype(BF16)
                dqe = dqe + jnp.where(hmask[h], _bnn(da, keb) + _bnn(dob, st.astype(BF16)), 0.0)
                dke = dke + _bnn(dat, qh)
                dv_ref[0, rows, cols] = (_bnn(at, dob) + _bnt(kdb, htb)).reshape(m, GLA_DV).astype(BF16)
                dkd = dkd + jnp.where(hmask[h], _bnn(vh, htb), 0.0)
                ddec = ddec + jnp.where(hmask[h], jnp.sum(htf * st, axis=1, keepdims=True), 0.0)
            dq_ref[0, rows, :] = (dqe * eb * Q_SCALE).reshape(m, 128).astype(BF16)
            dk_ref[0, rows, :] = (dke * enb + dkd * erest).reshape(m, 128).astype(BF16)
            db = dqe * qe - dke * ke - dkd * kd
            dblast = jnp.sum(dkd * kd, axis=1, keepdims=True) + ddec * dec
            lastrow = lax.broadcasted_iota(jnp.int32, (1, CHUNK, 1), 1) == CHUNK - 1
            db = db + jnp.where(lastrow, dblast, 0.0)
            dlg = _bnn(triu.astype(F32), db, precision=lax.Precision.HIGHEST).reshape(m, 128)
            dz = jnp.where(valid, dlg, 0.0) * (1.0 / GLA_TAU) * (1.0 - _sigmoid(z))
            dzb = dz.astype(BF16)
            dgd_ref[0, 0, rows, :] = _nt(dzb, gupb).astype(BF16)
            gup_acc[...] += _tn(gd_ref[0, rows, :].astype(BF16), dzb)
            return dbias + jnp.sum(dz, axis=0, keepdims=True), dgn

        zero = jnp.zeros((1, 128), F32)
        dbias, dgn = lax.fori_loop(0, ngroup, group, (zero, zero))
        dgup_ref[0] = gup_acc[...]
        vec_ref[0] = jnp.zeros((8, 128), F32)
        vec_ref[0, 0:1, :] = dbias
        vec_ref[0, 1:2, :] = dgn

    pair = lambda w, off: pl.BlockSpec((1, tp, w), lambda bi, p: (bi, 0, off // w + p))
    return _fused_call(
        body, fuse, name="gla_bwd",
        out_shape=[jax.ShapeDtypeStruct((bsz, tp, 256), BF16), jax.ShapeDtypeStruct((bsz, tp, 256), BF16),
                   jax.ShapeDtypeStruct((bsz, tp, 512), BF16), jax.ShapeDtypeStruct((bsz, tp, 512), BF16),
                   jax.ShapeDtypeStruct((bsz, 2, tp, 128), BF16), jax.ShapeDtypeStruct((bsz, 128, 256), F32),
                   jax.ShapeDtypeStruct((bsz, 8, 256), F32)],
        grid=(bsz, 2),
        in_specs=[pair(256, 0), pair(128, C_Q), pair(128, C_K), pair(256, C_V), pair(256, C_R),
                  pl.BlockSpec((1, tp, 128), lambda bi, p: (bi, 0, C_GD // 128)),
                  pair(256, 0),
                  pl.BlockSpec((1, 2, nchunk, GLA_DV, 128), lambda bi, p: (bi, p, 0, 0, 0)),
                  pl.BlockSpec((128, 128), lambda bi, p: (0, p)),
                  pl.BlockSpec((1, 128), lambda bi, p: (0, p)),
                  pl.BlockSpec((1, 128), lambda bi, p: (0, 0))],
        out_specs=[pair(128, 0), pair(128, 0), pair(256, 0), pair(256, 0),
                   pl.BlockSpec((1, 1, tp, 128), lambda bi, p: (bi, p, 0, 0)),
                   pl.BlockSpec((1, 128, 128), lambda bi, p: (bi, 0, p)),
                   pl.BlockSpec((1, 8, 128), lambda bi, p: (bi, 0, p))],
        scratch_shapes=[pltpu.VMEM((2, GLA_DV, 128), F32), pltpu.VMEM((128, 128), F32)],
        operands=(dgla, u, u, u, u, u, o_pre, states, gup, gbias, gnorm))


_DU_OFFSETS = (C_VAL, C_GATE, C_Q, C_K, C_V, C_R)
_DU_WIDTHS = (512, 512, 256, 256, 512, 512)


def _du_specs(tm, per, row_map):
    specs = [pl.BlockSpec((tm, w), row_map) for w in _DU_WIDTHS]
    for p in range(2):
        specs.append(pl.BlockSpec((1, 1, tm, 128), lambda *ix, p=p: (row_map(*ix)[0] // per, p, row_map(*ix)[0] % per, 0)))
    return specs


def _du_pieces(refs):
    out = [(off, ref[...]) for off, ref in zip(_DU_OFFSETS, refs[:6])]
    dgd = (refs[6][0, 0].astype(F32) + refs[7][0, 0].astype(F32)).astype(BF16)
    out.append((C_GD, dgd))
    return out


def _in_proj_bwd(pieces, dgd, w_int, dt1, tp, fuse=NO_FUSE):
    r, d = dt1.shape
    tm = _row_tile(tp)
    per = tp // tm

    def body(*refs):
        w_ref, dt_ref, o_ref = refs[8:]
        acc = ALPHA * dt_ref[...]
        for off, val in _du_pieces(refs[:8]):
            acc = acc + _nn(val, w_ref[off:off + val.shape[1], :])
        o_ref[...] = acc

    row = lambda i: (i, 0)
    (ds0,), got = _fused_call(
        body, fuse, name="in_proj_bwd", out_shape=[jax.ShapeDtypeStruct((r, d), F32)], grid=(r // tm,),
        in_specs=_du_specs(tm, per, row) + [pl.BlockSpec((D_IN_PAD, d), lambda i: (0, 0)), pl.BlockSpec((tm, d), row)],
        out_specs=[pl.BlockSpec((tm, d), row)], scratch_shapes=[], operands=(*pieces, dgd, dgd, w_int, dt1))
    return ds0, got


def _grad_w_in(pieces, dgd, s0b, tp, fuse=NO_FUSE):
    r, d = s0b.shape
    tk = _reduce_tile(tp, False)
    per = tp // tk

    def body(*refs):
        s_ref, o_ref = refs[8:]

        @pl.when(pl.program_id(0) == 0)
        def _():
            o_ref[...] = jnp.zeros_like(o_ref)

        s = s_ref[...]
        for off, val in _du_pieces(refs[:8]):
            o_ref[off:off + val.shape[1], :] += _tn(val, s)

    row = lambda k: (k, 0)
    (out,), got = _fused_call(
        body, fuse, name="grad_w_in", out_shape=[jax.ShapeDtypeStruct((D_IN_PAD, d), F32)], grid=(r // tk,),
        in_specs=_du_specs(tk, per, row) + [pl.BlockSpec((tk, d), row)],
        out_specs=[pl.BlockSpec((D_IN_PAD, d), lambda k: (0, 0))], scratch_shapes=[],
        operands=(*pieces, dgd, dgd, s0b))
    return out, got


def _ln_in_bwd(ds0, x, meta, g):
    bsz, s, d = x.shape
    tp = s + HEAD
    nh = 2
    sh = s // nh
    rc = min(256, sh)

    def body(ds_ref, x_ref, meta_ref, g_ref, gx_ref, dm_ref, vec_ref):
        h = pl.program_id(1)
        gg = g_ref[...]

        @pl.when(h == 0)
        def _():
            mh, mr = _ln_stats(meta_ref[...])
            dsm = ds_ref[0, PAD:HEAD, :]
            dm_ref[0] = _ln_bwd(dsm * gg, mh, mr)
            vec_ref[0] = jnp.zeros((8, d), F32)
            vec_ref[0, 0:1, :] = jnp.sum(dsm * mh, axis=0, keepdims=True)
            vec_ref[0, 1:2, :] = jnp.sum(dsm, axis=0, keepdims=True)

        def step(i, carry):
            sg, sb = carry
            dst = pl.ds(pl.multiple_of(i * rc, rc), rc)
            src = pl.ds(pl.multiple_of(HEAD + h * sh + i * rc, 64), rc)
            xh, rstd = _ln_stats(x_ref[0, dst, :])
            dsv = ds_ref[0, src, :]
            gx_ref[0, dst, :] = _ln_bwd(dsv * gg, xh, rstd)
            return sg + jnp.sum(dsv * xh, axis=0, keepdims=True), sb + jnp.sum(dsv, axis=0, keepdims=True)

        zero = jnp.zeros((1, d), F32)
        sg, sb = lax.fori_loop(0, sh // rc, step, (zero, zero))
        vec_ref[0, 0:1, :] += sg
        vec_ref[0, 1:2, :] += sb

    return pl.pallas_call(
        body, name="ln_in_bwd",
        out_shape=[jax.ShapeDtypeStruct((bsz, s, d), F32), jax.ShapeDtypeStruct((bsz, N_META, d), F32),
                   jax.ShapeDtypeStruct((bsz, 8, d), F32)],
        grid=(bsz, nh),
        in_specs=[pl.BlockSpec((1, tp, d), lambda bi, hi: (bi, 0, 0)),
                  pl.BlockSpec((1, sh, d), lambda bi, hi: (bi, hi, 0)),
                  pl.BlockSpec((N_META, d), lambda bi, hi: (0, 0)),
                  pl.BlockSpec((1, d), lambda bi, hi: (0, 0))],
        out_specs=[pl.BlockSpec((1, sh, d), lambda bi, hi: (bi, hi, 0)),
                   pl.BlockSpec((1, N_META, d), lambda bi, hi: (bi, 0, 0)),
                   pl.BlockSpec((1, 8, d), lambda bi, hi: (bi, 0, 0))],
        compiler_params=_params(("parallel", "arbitrary")),
    )(ds0, x, meta, g)


def _rows128(a):
    return a.reshape(-1, 128)


def kernel(x, meta_tokens, ln_in_g, ln_in_b, w_in, conv_w, conv_b, conv_ln_g, conv_ln_b, gate_up, gate_bias, gla_norm_g, w_out, ln1_g, ln1_b, w_ff1, w_ff2, ln2_g, ln2_b, loss_target, m_meta_tokens, m_ln_in_g, m_ln_in_b, m_w_in, m_conv_w, m_conv_b, m_conv_ln_g, m_conv_ln_b, m_gate_up, m_gate_bias, m_gla_norm_g, m_w_out, m_ln1_g, m_ln1_b, m_w_ff1, m_w_ff2, m_ln2_g, m_ln2_b, v_meta_tokens, v_ln_in_g, v_ln_in_b, v_w_in, v_conv_w, v_conv_b, v_conv_ln_g, v_conv_ln_b, v_gate_up, v_gate_bias, v_gla_norm_g, v_w_out, v_ln1_g, v_ln1_b, v_w_ff1, v_w_ff2, v_ln2_g, v_ln2_b):
    bsz, seq, d = x.shape
    tp = seq + HEAD
    r = bsz * tp
    xi, yi, ci = _mesh_pos()
    chip = 2 * xi + yi
    c_arr = jnp.reshape(ci, (1,)).astype(jnp.int32)
    pos_arr = jnp.stack([chip, ci]).astype(jnp.int32)

    sh_in = D_IN // 4
    shard_in = jnp.pad(w_in[0].T.astype(BF16), ((0, D_IN_PAD // 4 - sh_in), (0, 0)))
    shard_w1, shard_wout, shard_w2 = w_ff1[0].astype(BF16), w_out[0].astype(BF16), w_ff2[0].astype(BF16)
    small_w = jnp.concatenate([_rows128(meta_tokens), _rows128(conv_w[0]), _rows128(gate_up[0])], axis=0)
    g_small = _gather_small(small_w)
    n_meta_rows, n_cw_rows = N_META * 256 // 128, CONV_WIDTH * 128 // 128
    meta_full = jnp.concatenate([g_small[j, :n_meta_rows].reshape(N_META, 256) for j in range(4)], axis=1)
    convw_full = jnp.concatenate(
        [g_small[j, n_meta_rows:n_meta_rows + n_cw_rows].reshape(CONV_WIDTH, 128) for j in range(4)], axis=1)
    gup_full = jnp.concatenate(
        [g_small[j, n_meta_rows + n_cw_rows:].reshape(GLA_RANK, 64) for j in range(4)], axis=1)
    convw_p = jnp.pad(convw_full, ((0, 1), (0, 0)))
    gup_p = jnp.pad(gup_full, ((0, 128 - GLA_RANK), (0, 0))).astype(BF16)
    ln_in_g2, ln_in_b2 = ln_in_g.reshape(1, d), ln_in_b.reshape(1, d)

    s0, s0b, ((g_int,),) = _ln_in_fwd(x, meta_full, ln_in_g2, ln_in_b2,
                                      [("gather", [(shard_in, 0, D_IN_PAD // 4, None)])])
    (g_int,) = _place_own([g_int], [shard_in])
    w_int = jnp.pad(g_int[:, :sh_in].reshape(D_IN, d), ((0, D_IN_PAD - D_IN), (0, 0)))
    s0f, s0bf = s0.reshape(r, d), s0b.reshape(r, d)
    u, ((w1_buf,),) = _in_proj(s0bf, w_int, [("gather", [(shard_w1, 0, 640, None)])])
    (w1_buf,) = _place_own([w1_buf], [shard_w1])
    u3 = u.reshape(bsz, tp, D_IN_PAD)
    hc, ((w1_buf,),) = _conv_fwd(u3, convw_p, conv_b, [("gather", [(shard_w1, 640, 384, w1_buf)])])
    hc = hc.reshape(r, D_CONV)
    gla_out, o_pre, states, ((g_wout, w2_buf),) = _gla_fwd(
        u3, gup_p, gate_bias, gla_norm_g, [("gather", [(shard_wout, 0, 256, None), (shard_w2, 0, 256, None)])])
    g_wout, w2_buf = _place_own([g_wout, w2_buf], [shard_wout, shard_w2])
    wout = g_wout.reshape(d, d)
    gla_of = gla_out.reshape(r, 512)
    conv_of, xhat1, rstd1, s1b, ((w2_buf,),) = _out_proj_ln1(
        hc, gla_of, wout, s0f, conv_ln_g, conv_ln_b, ln1_g, ln1_b, [("gather", [(shard_w2, 256, 384, w2_buf)])])
    w1 = w1_buf
    ra, ((w2_buf,),) = _ffn1(s1b, w1, [("gather", [(shard_w2, 640, 384, w2_buf)])])
    w2 = w2_buf.reshape(D_FF, d)
    dt2, dt2b, acc2 = _ffn2_ln2_loss(ra, w2, xhat1, ln1_g, ln1_b, ln2_g, ln2_b, loss_target, tp)

    def add_pair(g, got, splits=None):
        return _add_pair(g, got, c_arr, splits)

    da = _ffn_bwd_da(dt2b, w2, ra)
    dt1, dt1b, acc1 = _ffn_bwd_ln1(da, w1, dt2, xhat1, rstd1, ln1_g)
    g_w2, _ = _matmul_tn(ra, dt2b, 1024, square_lhs=True, name="grad_w_ff2")
    big_w2 = g_w2.reshape(4, D_FF // 4, d)
    big_w1, ((pair_w2,),) = _grad_w_ff1(s1b, da, [("pair", [big_w2])])
    dhc, dgla, cacc, ((pair_w1,),) = _out_proj_bwd(dt1b, wout, hc, conv_ln_g, conv_ln_b, [("pair", [big_w1])])
    (part_w1,), (part_w2a, part_w2b) = add_pair(big_w1, pair_w1), add_pair(big_w2, pair_w2, [512, 512])
    big_wout = _grad_w_out(conv_of, gla_of, dt1b).reshape(4, d // 4, d)
    dcv, dcg, dcw, ((chip_w1,), (pair_wout,)) = _conv_bwd(
        dhc.reshape(bsz, tp, D_CONV), u3, convw_p, [("exchange", [part_w1]), ("pair", [big_wout])])
    (part_wout,) = add_pair(big_wout, pair_wout)
    (dq, dk, dv, dr, dgd, dgup, gvec), ((chip_w2a,),) = _gla_bwd(
        dgla.reshape(bsz, tp, 512), u3, o_pre, states, gup_p, gate_bias, gla_norm_g, [("exchange", [part_w2a])])
    pieces = [a.reshape(r, a.shape[-1]) for a in (dcv, dcg, dq, dk, dv, dr)]
    g_wint, ((chip_w2b, chip_wout),) = _grad_w_in(
        pieces, dgd, s0bf, tp, [("exchange", [part_w2b, part_wout])])
    big_win = jnp.stack([g_wint[j * sh_in:(j + 1) * sh_in] for j in range(4)])
    (pair_win,) = _pair_exchange([big_win])
    half_w1 = _add_chips(big_w1, pair_w1, [chip_w1], pos_arr)
    half_w2 = _add_chips(big_w2, pair_w2, [chip_w2a, chip_w2b], pos_arr)
    ds0, ((chip_win,), (sib_w1, sib_w2)) = _in_proj_bwd(
        pieces, dgd, w_int, dt1, tp, [("exchange", add_pair(big_win, pair_win)), ("swap", [half_w1, half_w2])])
    grad_x, dmeta, lvec = _ln_in_bwd(ds0.reshape(bsz, tp, d), x, meta_full, ln_in_g2)

    red, loss = _sum_small(_allgather_small(_pack_small(acc2, acc1, lvec, cacc, gvec, dcw, dgup, dmeta)))

    half_win = _add_chips(big_win, pair_win, [chip_win], pos_arr)
    half_wout = _add_chips(big_wout, pair_wout, [chip_wout], pos_arr)
    sib_win, sib_wout = _pair_swap([half_win, half_wout])
    f_wint, f_wout, f_w1, f_w2 = [
        jnp.concatenate([jnp.where(ci == 0, own, sib), jnp.where(ci == 0, sib, own)], axis=1)
        for own, sib in ((half_win, sib_win), (half_wout, sib_wout), (half_w1, sib_w1), (half_w2, sib_w2))]

    grads = {"w_in": f_wint.T[None], "w_out": f_wout[None], "w_ff1": f_w1[None], "w_ff2": f_w2[None]}
    weights = dict(meta_tokens=meta_tokens, ln_in_g=ln_in_g, ln_in_b=ln_in_b, w_in=w_in, conv_w=conv_w, conv_b=conv_b,
                   conv_ln_g=conv_ln_g, conv_ln_b=conv_ln_b, gate_up=gate_up, gate_bias=gate_bias,
                   gla_norm_g=gla_norm_g, w_out=w_out, ln1_g=ln1_g, ln1_b=ln1_b, w_ff1=w_ff1, w_ff2=w_ff2,
                   ln2_g=ln2_g, ln2_b=ln2_b)
    moms = dict(meta_tokens=(m_meta_tokens, v_meta_tokens), ln_in_g=(m_ln_in_g, v_ln_in_g),
                ln_in_b=(m_ln_in_b, v_ln_in_b), w_in=(m_w_in, v_w_in), conv_w=(m_conv_w, v_conv_w),
                conv_b=(m_conv_b, v_conv_b), conv_ln_g=(m_conv_ln_g, v_conv_ln_g),
                conv_ln_b=(m_conv_ln_b, v_conv_ln_b), gate_up=(m_gate_up, v_gate_up),
                gate_bias=(m_gate_bias, v_gate_bias), gla_norm_g=(m_gla_norm_g, v_gla_norm_g),
                w_out=(m_w_out, v_w_out), ln1_g=(m_ln1_g, v_ln1_g), ln1_b=(m_ln1_b, v_ln1_b),
                w_ff1=(m_w_ff1, v_w_ff1), w_ff2=(m_w_ff2, v_w_ff2), ln2_g=(m_ln2_g, v_ln2_g),
                ln2_b=(m_ln2_b, v_ln2_b))
    names = list(weights)
    big_names = ("w_in", "w_out", "w_ff1", "w_ff2")
    delta, new_m, new_v = {}, {}, {}
    lin = lambda a: a.reshape(-1, 128)
    unlin = lambda a: a.reshape(sh_in, d).T[None]
    dl, mn, vn = _adamw(lin(w_in[0].T), lin(f_wint), lin(m_w_in[0].T), lin(v_w_in[0].T))
    delta["w_in"], new_m["w_in"], new_v["w_in"] = unlin(dl), unlin(mn), unlin(vn)
    for k in big_names[1:]:
        shp = weights[k].shape
        two = lambda a: a.reshape(shp[-2], shp[-1])
        dl, mn, vn = _adamw(two(weights[k]), two(grads[k]), two(moms[k][0]), two(moms[k][1]))
        delta[k], new_m[k], new_v[k] = dl.reshape(shp), mn.reshape(shp), vn.reshape(shp)
    small_names = [k for k in names if k not in big_names]
    two = lambda a: a.reshape(-1, a.shape[-1])

    def my_cols(k, rows, width):
        r0 = SMALL_AT[k][0]
        return lax.dynamic_slice(red, (r0, chip * width), (rows, width))

    sharded = {"meta_tokens": my_cols("meta_tokens", N_META, 256), "conv_w": my_cols("conv_w", CONV_WIDTH, 128),
               "gate_up": my_cols("gate_up", GLA_RANK, 64)}
    upd = _adamw_small(red, sharded, {k: (two(weights[k]), two(moms[k][0]), two(moms[k][1])) for k in small_names})
    for k in small_names:
        shp = weights[k].shape
        grads[k], delta[k], new_m[k], new_v[k] = [a.reshape(shp) for a in upd[k]]
    loss = loss.reshape(())

    return (loss, grad_x, *[grads[k] for k in names], *[delta[k] for k in names],
            *[new_m[k] for k in names], *[new_v[k] for k in names])
```

```python
import functools

import jax
import jax.numpy as jnp
from jax import lax
from jax.experimental import pallas as pl
from jax.experimental.pallas import tpu as pltpu

F32 = jnp.float32
BF16 = jnp.bfloat16

D_MODEL = 1024
N_META = 16
D_CONV = 512
CONV_WIDTH = 31
GLA_HEADS = 4
GLA_DV = 128
GLA_DK = 64
GLA_RANK = 16
GLA_TAU = 16.0
CHUNK = 64
D_FF = 4096
LN_EPS = 1e-5
ALPHA = 2.0 ** 0.25
D_IN = 2576
D_IN_PAD = 2688
PAD = CHUNK - N_META
HEAD = PAD + N_META
Q_SCALE = GLA_DK ** -0.5
ADAM_LR, ADAM_B1, ADAM_B2, ADAM_EPS, ADAM_WD, ADAM_STEP = 0.001, 0.9, 0.999, 1e-08, 0.01, 10
HALF = D_MODEL // 2
VMEM_LIMIT = 56 * 1024 * 1024
MESH = pl.DeviceIdType.MESH

C_VAL, C_GATE, C_Q, C_K, C_V, C_R, C_GD = 0, 512, 1024, 1280, 1536, 2048, 2560

SMALL_AT = {"loss": (0, 1, 1024), "ln_in_g": (1, 1, 1024), "ln_in_b": (2, 1, 1024), "ln1_g": (3, 1, 1024),
            "ln1_b": (4, 1, 1024), "ln2_g": (5, 1, 1024), "ln2_b": (6, 1, 1024), "conv_b": (7, 1, 512),
            "conv_ln_g": (8, 1, 512), "conv_ln_b": (9, 1, 512), "gate_bias": (10, 1, 256), "gla_norm_g": (11, 1, 128),
            "conv_w": (16, 32, 512), "gate_up": (48, 16, 256), "meta_tokens": (64, 16, 1024)}
SMALL_ROWS = 80


def _params(sem=None, **kw):
    return pltpu.CompilerParams(dimension_semantics=sem, vmem_limit_bytes=VMEM_LIMIT, **kw)


def _row_tile(tp):
    for t in (704, 352, 192, 64):
        if tp % t == 0:
            return t
    raise ValueError(tp)


def _reduce_tile(tp, big):
    for t in ((2112, 1056, 704) if big else (1056, 704)) + (352, 192, 64):
        if tp % t == 0:
            return t
    raise ValueError(tp)


def _sub_rows(tm):
    return [slice(0, tm)]


def _dot(a, b, dims, precision=None):
    return lax.dot_general(a, b, (dims, ((), ())), preferred_element_type=F32, precision=precision)


def _nn(a, b, **kw):
    return _dot(a, b, ((1,), (0,)), **kw)


def _nt(a, b, **kw):
    return _dot(a, b, ((1,), (1,)), **kw)


def _tn(a, b, **kw):
    return _dot(a, b, ((0,), (0,)), **kw)


def _sigmoid(x):
    return 1.0 / (1.0 + jnp.exp(-x))


def _log_sigmoid(z):
    return jnp.minimum(z, 0.0) - jnp.log(1.0 + jnp.exp(-jnp.abs(z)))


def _ln_stats(t):
    mu = jnp.mean(t, axis=-1, keepdims=True)
    d = t - mu
    var = jnp.mean(d * d, axis=-1, keepdims=True)
    rstd = lax.rsqrt(var + LN_EPS)
    return d * rstd, rstd


def _ln_bwd(dxhat, xhat, rstd):
    m1 = jnp.mean(dxhat, axis=-1, keepdims=True)
    m2 = jnp.mean(dxhat * xhat, axis=-1, keepdims=True)
    return rstd * (dxhat - m1 - xhat * m2)


def _mesh_pos():
    return lax.axis_index("x"), lax.axis_index("y"), lax.axis_index("c")


ANY = pl.BlockSpec(memory_space=pl.ANY)


def _gather_sems(n):
    return [pltpu.SemaphoreType.DMA((n, 3))] * 4


def _gather_steps(ins, outs, sems, ranges=None):
    n = len(ins)
    send, recv, fsend, frecv = sems
    ranges = ranges or [(0, ref.shape[0]) for ref in ins]
    x, y, c = _mesh_pos()
    me = 2 * x + y
    sibling = (x, y, 1 - c)
    chips = [(1 - x, y), (x, 1 - y), (1 - x, 1 - y)]
    chip_idx = [2 * px + py for px, py in chips]
    mine = [pl.ds(pl.multiple_of(r0 + c * (nr // 2), 16), nr // 2) for r0, nr in ranges]
    other = [pl.ds(pl.multiple_of(r0 + (1 - c) * (nr // 2), 16), nr // 2) for r0, nr in ranges]
    pairs = [(a, k) for a in range(n) for k in range(3)]

    def ici(a, k, slab):
        return pltpu.make_async_remote_copy(
            src_ref=ins[a].at[mine[a], :], dst_ref=outs[a].at[slab, mine[a], :],
            send_sem=send.at[a, k], recv_sem=recv.at[a, k], device_id=(*chips[k], c), device_id_type=MESH)

    def forward(a, k, rows):
        blk = outs[a].at[chip_idx[k], rows[a], :]
        return pltpu.make_async_remote_copy(
            src_ref=blk, dst_ref=blk, send_sem=fsend.at[a, k], recv_sem=frecv.at[a, k],
            device_id=sibling, device_id_type=MESH)

    def start():
        for a, k in pairs:
            ici(a, k, me).start()

    def relay():
        for a, k in pairs:
            ici(a, k, chip_idx[k]).wait_recv()
            forward(a, k, mine).start()

    def finish():
        for a, k in pairs:
            forward(a, k, other).wait_recv()
        for a, k in pairs:
            ici(a, k, me).wait_send()
            forward(a, k, mine).wait_send()

    return start, relay, finish


def _place_own(gathered, shards):
    chip = 2 * lax.axis_index("x") + lax.axis_index("y")
    return [lax.dynamic_update_slice(g, s[None], (chip, 0, 0)) for g, s in zip(gathered, shards)]


def _gather_small(small):
    def body(small_in, small_out, ssend, srecv):
        x, y, c = _mesh_pos()
        chips = [(1 - x, y), (x, 1 - y), (1 - x, 1 - y)]

        def small_copy(k, slot):
            return pltpu.make_async_remote_copy(
                src_ref=small_in, dst_ref=small_out.at[slot], send_sem=ssend.at[k], recv_sem=srecv.at[k],
                device_id=(*chips[k], c), device_id_type=MESH)

        for k in range(3):
            small_copy(k, 2 * x + y).start()
        for k, (px, py) in enumerate(chips):
            small_copy(k, 2 * px + py).wait_recv()
        for k in range(3):
            small_copy(k, 2 * x + y).wait_send()

    res = pl.pallas_call(
        body, name="gather_small", out_shape=jax.ShapeDtypeStruct((4,) + small.shape, small.dtype),
        in_specs=[ANY], out_specs=ANY,
        scratch_shapes=[pltpu.SemaphoreType.DMA((3,)), pltpu.SemaphoreType.DMA((3,))],
    )(small)
    return _place_own([res], [small])[0]


def _pair_swap_steps(ins, outs, sems):
    send, recv = sems
    x, y, c = _mesh_pos()
    cps = [pltpu.make_async_remote_copy(
        src_ref=ins[a], dst_ref=outs[a], send_sem=send.at[a], recv_sem=recv.at[a],
        device_id=(x, y, 1 - c), device_id_type=MESH) for a in range(len(ins))]

    def start():
        for cp in cps:
            cp.start()

    def finish():
        for cp in cps:
            cp.wait()

    return start, finish


def _pair_exchange_steps(ins, outs, sems):
    send, recv = sems
    x, y, c = _mesh_pos()
    other = pl.ds(pl.multiple_of((1 - c) * HALF, 128), HALF)
    cps = [pltpu.make_async_remote_copy(
        src_ref=ins[a].at[:, :, other], dst_ref=outs[a], send_sem=send.at[a], recv_sem=recv.at[a],
        device_id=(x, y, 1 - c), device_id_type=MESH) for a in range(len(ins))]

    def start():
        for cp in cps:
            cp.start()

    def finish():
        for cp in cps:
            cp.wait()

    return start, finish


def _pair_exchange_shapes(grads):
    return [jax.ShapeDtypeStruct(g.shape[:2] + (HALF,), g.dtype) for g in grads]


def _pair_exchange(grads):
    n = len(grads)

    def body(*refs):
        start, finish = _pair_exchange_steps(refs[:n], refs[n:2 * n], refs[2 * n:])
        start()
        finish()

    return pl.pallas_call(
        body, name="grad_pair_exchange", out_shape=_pair_exchange_shapes(grads),
        in_specs=[ANY] * n, out_specs=[ANY] * n,
        scratch_shapes=[pltpu.SemaphoreType.DMA((n,)), pltpu.SemaphoreType.DMA((n,))],
    )(*grads)


NO_FUSE = ()


def _fuse_plan(kind, items):
    n = len(items)
    if kind == "gather":
        shards = [it[0] for it in items]
        bufs = [it[3] for it in items if it[3] is not None]
        alias, b = {}, 0
        for a, it in enumerate(items):
            if it[3] is not None:
                alias[n + b] = a
                b += 1
        ranges = [(it[1], it[2]) for it in items]
        return (shards + bufs, [jax.ShapeDtypeStruct((4,) + s.shape, s.dtype) for s in shards], alias, _gather_sems(n),
                lambda i, o, s: _gather_steps(i[:n], o, s, ranges))
    if kind == "exchange":
        return list(items), _chip_exchange_shapes(items), {}, _chip_exchange_sems(n), _chip_exchange_steps
    pair_sems = [pltpu.SemaphoreType.DMA((n,)), pltpu.SemaphoreType.DMA((n,))]
    if kind == "swap":
        return (list(items), [jax.ShapeDtypeStruct(h.shape, h.dtype) for h in items], {}, pair_sems, _pair_swap_steps)
    return list(items), _pair_exchange_shapes(items), {}, pair_sems, _pair_exchange_steps


def _fused_call(body, fuse, *, name, grid, in_specs, out_specs, out_shape, scratch_shapes, operands):
    plans = [_fuse_plan(kind, list(items)) for kind, items in fuse if len(items)]
    n_in, n_out, n_s = len(in_specs), len(out_shape), len(scratch_shapes)
    comm = [a for p in plans for a in p[0]]
    shapes = [s for p in plans for s in p[1]]
    nc, no = len(comm), len(shapes)
    aliases, i_at, o_at = {}, n_in, n_out
    for p in plans:
        for i, o in p[2].items():
            aliases[i_at + i] = o_at + o
        i_at, o_at = i_at + len(p[0]), o_at + len(p[1])

    def wrapped(*refs):
        o0 = n_in + nc
        s0 = o0 + n_out + no
        i_at, o_at, sem_at, steps = n_in, o0 + n_out, s0 + n_s, []
        for p in plans:
            steps.append(p[4](refs[i_at:i_at + len(p[0])], refs[o_at:o_at + len(p[1])], refs[sem_at:sem_at + len(p[3])]))
            i_at, o_at, sem_at = i_at + len(p[0]), o_at + len(p[1]), sem_at + len(p[3])
        first, last = _grid_ends(grid)
        for st in steps:
            pl.when(first)(st[0])
        for st in steps:
            for mid in st[1:-1]:
                pl.when(last)(mid)
        body(*refs[:n_in], *refs[o0:o0 + n_out], *refs[s0:s0 + n_s])
        for st in steps:
            pl.when(last)(st[-1])

    res = pl.pallas_call(
        wrapped if plans else body, name=name, grid=grid, in_specs=list(in_specs) + [ANY] * nc,
        out_specs=list(out_specs) + [ANY] * no, out_shape=list(out_shape) + shapes,
        scratch_shapes=list(scratch_shapes) + [s for p in plans for s in p[3]], input_output_aliases=aliases,
        compiler_params=_params(("arbitrary",) * len(grid)))(*operands, *comm)
    outs, got, results = list(res[:n_out]), list(res[n_out:]), []
    for p in plans:
        results.append(got[:len(p[1])])
        got = got[len(p[1]):]
    return outs, results


def _chip_exchange_sems(n):
    return [pltpu.SemaphoreType.DMA((n, 3))] * 2


def _chip_exchange_shapes(parts):
    return [jax.ShapeDtypeStruct((3,) + p.shape[1:], p.dtype) for p in parts]


def _chip_exchange_steps(ins, outs, sems):
    send, recv = sems
    x, y, c = _mesh_pos()
    chips = [(1 - x, y), (x, 1 - y), (1 - x, 1 - y)]
    cps = [pltpu.make_async_remote_copy(
        src_ref=ins[a].at[2 * px + py], dst_ref=outs[a].at[k], send_sem=send.at[a, k], recv_sem=recv.at[a, k],
        device_id=(px, py, c), device_id_type=MESH) for a in range(len(ins)) for k, (px, py) in enumerate(chips)]

    def start():
        for cp in cps:
            cp.start()

    def finish():
        for cp in cps:
            cp.wait()

    return start, finish


def _chip_exchange(parts):
    n = len(parts)

    def body(*refs):
        start, finish = _chip_exchange_steps(refs[:n], refs[n:2 * n], refs[2 * n:])
        start()
        finish()

    return pl.pallas_call(
        body, name="grad_chip_exchange", out_shape=_chip_exchange_shapes(parts),
        in_specs=[ANY] * n, out_specs=[ANY] * n, scratch_shapes=_chip_exchange_sems(n),
    )(*parts)


def _pair_swap(halves):
    n = len(halves)

    def body(*refs):
        start, finish = _pair_swap_steps(refs[:n], refs[n:2 * n], refs[2 * n:])
        start()
        finish()

    return pl.pallas_call(
        body, name="grad_pair_swap",
        out_shape=[jax.ShapeDtypeStruct(h.shape, h.dtype) for h in halves],
        in_specs=[ANY] * n, out_specs=[ANY] * n,
        scratch_shapes=[pltpu.SemaphoreType.DMA((n,)), pltpu.SemaphoreType.DMA((n,))],
    )(*halves)


def _allgather_small(pack):
    m_per, ncol = pack.shape

    def body(x_ref, out_ref, send_sems, recv_sems, local_sem):
        x, y, c = _mesh_pos()
        me, sibling = (x, y, c), (x, y, 1 - c)
        chips = [(1 - x, y), (x, 1 - y), (1 - x, 1 - y)]

        def rows(px, py, pc):
            return out_ref.at[pl.ds(pl.multiple_of((4 * px + 2 * py + pc) * m_per, 8), m_per), :]

        def copy(k, block, to, src=None):
            return pltpu.make_async_remote_copy(
                src_ref=rows(*block) if src is None else src, dst_ref=rows(*block),
                send_sem=send_sems.at[k], recv_sem=recv_sems.at[k], device_id=to, device_id_type=MESH)

        mine = pltpu.make_async_copy(x_ref, rows(*me), local_sem)
        mine.start()
        first = [copy(0, me, sibling, src=x_ref)]
        first += [copy(1 + j, me, (*chip, c), src=x_ref) for j, chip in enumerate(chips)]
        for cp in first:
            cp.start()
        passed = [copy(4 + j, (*chip, c), sibling) for j, chip in enumerate(chips)]
        for j, chip in enumerate(chips):
            copy(1 + j, (*chip, c), me).wait_recv()
            passed[j].start()
        copy(0, sibling, me).wait_recv()
        for j, chip in enumerate(chips):
            copy(4 + j, (*chip, 1 - c), me).wait_recv()
        for cp in first + passed:
            cp.wait_send()
        mine.wait()

    return pl.pallas_call(
        body, name="allgather_small",
        out_shape=jax.ShapeDtypeStruct((8 * m_per, ncol), pack.dtype),
        in_specs=[pl.BlockSpec(memory_space=pltpu.VMEM)],
        out_specs=pl.BlockSpec(memory_space=pltpu.VMEM),
        scratch_shapes=[pltpu.SemaphoreType.DMA((7,)), pltpu.SemaphoreType.DMA((7,)), pltpu.SemaphoreType.DMA],
    )(pack)


def _add_pair(g, got, c_arr, splits=None):
    _, rows, _ = g.shape
    splits = splits or [rows]

    def body(c_ref, g_ref, r_ref, *o_refs):
        at = 0
        for o_ref, n in zip(o_refs, splits):
            o_ref[...] = (g_ref[:, at:at + n, :] + r_ref[:, at:at + n, :]).astype(BF16)
            at += n

    return pl.pallas_call(
        body, name="grad_add_pair", out_shape=[jax.ShapeDtypeStruct((4, n, HALF), BF16) for n in splits],
        grid_spec=pltpu.PrefetchScalarGridSpec(
            num_scalar_prefetch=1, grid=(4,),
            in_specs=[pl.BlockSpec((1, rows, HALF), lambda j, c: (j, 0, c[0])),
                      pl.BlockSpec((1, rows, HALF), lambda j, c: (j, 0, 0))],
            out_specs=[pl.BlockSpec((1, n, HALF), lambda j, c: (j, 0, 0)) for n in splits]),
        compiler_params=_params(("arbitrary",)),
    )(c_arr, g, got)


def _add_chips(g, pair_got, chip_gots, pos_arr):
    _, rows, _ = g.shape

    def body(pos_ref, g_ref, p_ref, *refs):
        o_ref, at = refs[-1], 0
        for r_ref in refs[:-1]:
            n = r_ref.shape[1]
            own = g_ref[0, at:at + n, :] + p_ref[0, at:at + n, :]
            o_ref[at:at + n, :] = ((own + r_ref[0].astype(F32)) + r_ref[1].astype(F32)) + r_ref[2].astype(F32)
            at += n

    return pl.pallas_call(
        body, name="grad_add_chips", out_shape=jax.ShapeDtypeStruct((rows, HALF), F32),
        grid_spec=pltpu.PrefetchScalarGridSpec(
            num_scalar_prefetch=1, grid=(1,),
            in_specs=[pl.BlockSpec((1, rows, HALF), lambda i, p: (p[0], 0, p[1])),
                      pl.BlockSpec((1, rows, HALF), lambda i, p: (p[0], 0, 0))]
            + [pl.BlockSpec(t.shape, lambda i, p: (0, 0, 0)) for t in chip_gots],
            out_specs=pl.BlockSpec((rows, HALF), lambda i, p: (0, 0))),
        compiler_params=_params(("arbitrary",)),
    )(pos_arr, g, pair_got, *chip_gots)


def _pack_small(acc2, acc1, lvec, cacc, gvec, dcw, dgup, dmeta):
    bsz = lvec.shape[0]

    def body(a2_ref, a1_ref, lv_ref, ca_ref, gv_ref, cw_ref, gu_ref, dm_ref, o_ref):
        def put(name, val):
            r0, nr, nl = SMALL_AT[name]
            o_ref[r0:r0 + nr, 0:nl] = val

        over_b = lambda f: functools.reduce(lambda a, b: a + b, [f(b) for b in range(bsz)])
        o_ref[...] = jnp.zeros_like(o_ref)
        put("loss", a2_ref[0:1, :])
        put("ln2_g", a2_ref[1:2, :])
        put("ln2_b", a2_ref[2:3, :])
        put("ln1_g", a1_ref[0:1, :])
        put("ln1_b", a1_ref[1:2, :])
        put("ln_in_g", over_b(lambda b: lv_ref[b, 0:1, :]))
        put("ln_in_b", over_b(lambda b: lv_ref[b, 1:2, :]))
        put("conv_b", ca_ref[0:1, :])
        put("conv_ln_g", ca_ref[1:2, :])
        put("conv_ln_b", ca_ref[2:3, :])
        put("gate_bias", over_b(lambda b: gv_ref[b, 0:1, :]))
        put("gla_norm_g", over_b(lambda b: gv_ref[b, 1:2, 0:128] + gv_ref[b, 1:2, 128:256]))
        put("conv_w", over_b(lambda b: cw_ref[b]))
        put("gate_up", over_b(lambda b: gu_ref[b, 0:GLA_RANK, :]))
        put("meta_tokens", over_b(lambda b: dm_ref[b]))

    return pl.pallas_call(
        body, name="pack_small", out_shape=jax.ShapeDtypeStruct((SMALL_ROWS, D_MODEL), F32),
    )(acc2, acc1, lvec, cacc, gvec, dcw, dgup, dmeta)


def _sum_small(gathered):
    def body(g_ref, o_ref, loss_ref):
        acc = g_ref[0:SMALL_ROWS, :]
        for d in range(1, 8):
            acc = acc + g_ref[d * SMALL_ROWS:(d + 1) * SMALL_ROWS, :]
        o_ref[...] = acc
        loss_ref[...] = jnp.sum(acc[0:1, :], axis=1, keepdims=True)

    return pl.pallas_call(
        body, name="sum_small",
        out_shape=[jax.ShapeDtypeStruct((SMALL_ROWS, D_MODEL), F32), jax.ShapeDtypeStruct((1, 1), F32)],
    )(gathered)


def _adamw_math(w, g, m, v):
    c1 = 1.0 - ADAM_B1 ** ADAM_STEP
    c2 = 1.0 - ADAM_B2 ** ADAM_STEP
    mn = ADAM_B1 * m + (1.0 - ADAM_B1) * g
    vn = ADAM_B2 * v + (1.0 - ADAM_B2) * (g * g)
    return -ADAM_LR * ((mn / c1) / (jnp.sqrt(vn / c2) + ADAM_EPS) + ADAM_WD * w), mn, vn


def _adamw_small(red, sharded_grads, params):
    names = list(params)
    sharded = [k for k in names if k in sharded_grads]
    n, ns = len(names), len(sharded)

    def body(*refs):
        red_ref, sg_refs, p_refs, o_refs = refs[0], refs[1:1 + ns], refs[1 + ns:1 + ns + 3 * n], refs[1 + ns + 3 * n:]
        for i, k in enumerate(names):
            w_ref, m_ref, v_ref = p_refs[3 * i:3 * i + 3]
            if k in sharded:
                g = sg_refs[sharded.index(k)][...]
            else:
                r0, nr, nl = SMALL_AT[k]
                g = red_ref[r0:r0 + nr, 0:nl]
            dl, mn, vn = _adamw_math(w_ref[...], g, m_ref[...], v_ref[...])
            for o_ref, val in zip(o_refs[4 * i:4 * i + 4], (g, dl, mn, vn)):
                o_ref[...] = val

    flat = [a for k in names for a in params[k]]
    res = pl.pallas_call(
        body, name="adamw_small",
        out_shape=[jax.ShapeDtypeStruct(params[k][0].shape, F32) for k in names for _ in range(4)],
    )(red, *[sharded_grads[k] for k in sharded], *flat)
    return {k: tuple(res[4 * i:4 * i + 4]) for i, k in enumerate(names)}


def _adamw_halves(w, own, sib, m, v, c_arr):
    rows, cols = w.shape
    tr = 256 if rows % 256 == 0 else rows

    def body(c_ref, w_ref, own_ref, sib_ref, m_ref, v_ref, g_ref, d_ref, mo_ref, vo_ref):
        first = c_ref[0] == 0
        own, sib = own_ref[...], sib_ref[...]
        g = jnp.concatenate([jnp.where(first, own, sib), jnp.where(first, sib, own)], axis=1)
        g_ref[...] = g
        d_ref[...], mo_ref[...], vo_ref[...] = _adamw_math(w_ref[...], g, m_ref[...], v_ref[...])

    full = pl.BlockSpec((tr, cols), lambda i, c: (i, 0))
    half = pl.BlockSpec((tr, HALF), lambda i, c: (i, 0))
    return pl.pallas_call(
        body, name="adamw_halves", out_shape=[jax.ShapeDtypeStruct(w.shape, F32)] * 4,
        grid_spec=pltpu.PrefetchScalarGridSpec(
            num_scalar_prefetch=1, grid=(rows // tr,), in_specs=[full, half, half, full, full], out_specs=[full] * 4),
        compiler_params=_params(("parallel",)),
    )(c_arr, w, own, sib, m, v)


def _ln_in_fwd(x, meta, g, b, fuse=NO_FUSE):
    bsz, s, d = x.shape
    tp = s + HEAD
    nh = 2
    sh = s // nh
    rc = min(256, sh)

    def body(x_ref, meta_ref, g_ref, b_ref, s0_ref, s0b_ref):
        h = pl.program_id(1)
        gg, bb = g_ref[...], b_ref[...]

        @pl.when(h == 0)
        def _():
            s0_ref[0, 0:PAD, :] = jnp.zeros((PAD, d), F32)
            s0b_ref[0, 0:PAD, :] = jnp.zeros((PAD, d), BF16)
            mh, _ = _ln_stats(meta_ref[...])
            mv = mh * gg + bb
            s0_ref[0, PAD:HEAD, :] = mv
            s0b_ref[0, PAD:HEAD, :] = mv.astype(BF16)

        def step(i, carry):
            src = pl.ds(pl.multiple_of(i * rc, rc), rc)
            dst = pl.ds(pl.multiple_of(HEAD + h * sh + i * rc, 64), rc)
            xh, _ = _ln_stats(x_ref[0, src, :])
            val = xh * gg + bb
            s0_ref[0, dst, :] = val
            s0b_ref[0, dst, :] = val.astype(BF16)
            return carry

        lax.fori_loop(0, sh // rc, step, 0)

    full = lambda bi, hi: (bi, 0, 0)
    (s0, s0b), got = _fused_call(
        body, fuse, name="ln_in_fwd",
        out_shape=[jax.ShapeDtypeStruct((bsz, tp, d), F32), jax.ShapeDtypeStruct((bsz, tp, d), BF16)],
        grid=(bsz, nh),
        in_specs=[pl.BlockSpec((1, sh, d), lambda bi, hi: (bi, hi, 0)),
                  pl.BlockSpec((N_META, d), lambda bi, hi: (0, 0)),
                  pl.BlockSpec((1, d), lambda bi, hi: (0, 0)),
                  pl.BlockSpec((1, d), lambda bi, hi: (0, 0))],
        out_specs=[pl.BlockSpec((1, tp, d), full)] * 2, scratch_shapes=[], operands=(x, meta, g, b))
    return s0, s0b, got


def _in_proj(s0b, w_int, fuse=NO_FUSE):
    r, d = s0b.shape
    tm = _row_tile(r)

    def body(a_ref, w_ref, o_ref):
        o_ref[...] = _nt(a_ref[...], w_ref[...])

    (u,), got = _fused_call(
        body, fuse, name="in_proj", out_shape=[jax.ShapeDtypeStruct((r, D_IN_PAD), F32)],
        grid=(r // tm,),
        in_specs=[pl.BlockSpec((tm, d), lambda i: (i, 0)), pl.BlockSpec((D_IN_PAD, d), lambda i: (0, 0))],
        out_specs=[pl.BlockSpec((tm, D_IN_PAD), lambda i: (i, 0))], scratch_shapes=[], operands=(s0b, w_int))
    return u, got


def _conv_fwd(u, conv_w, conv_b, fuse=NO_FUSE):
    bsz, tp, _ = u.shape
    nchunk = tp // CHUNK
    win = CHUNK + 32
    nct = D_CONV // 128

    def body(cv_ref, cg_ref, w_ref, cb_ref, hc_ref, h_scr, win_scr):
        h_scr[0:32, :] = jnp.zeros((32, 128), F32)
        h_scr[32:32 + tp, :] = cv_ref[0] * _sigmoid(cg_ref[0])
        cb = cb_ref[...]

        def step(n, carry):
            r0 = pl.multiple_of(n * CHUNK, CHUNK)
            win_scr[...] = h_scr[pl.ds(r0, win), :]
            acc = jnp.zeros((CHUNK, 128), F32)
            for j in range(CONV_WIDTH):
                acc = acc + w_ref[j:j + 1, :] * win_scr[2 + j:2 + j + CHUNK, :]
            hc_ref[0, pl.ds(r0, CHUNK), :] = acc + cb
            return carry

        lax.fori_loop(0, nchunk, step, 0)

    (hc,), got = _fused_call(
        body, fuse, name="conv_fwd", out_shape=[jax.ShapeDtypeStruct((bsz, tp, D_CONV), F32)],
        grid=(bsz, nct),
        in_specs=[pl.BlockSpec((1, tp, 128), lambda bi, t: (bi, 0, C_VAL // 128 + t)),
                  pl.BlockSpec((1, tp, 128), lambda bi, t: (bi, 0, C_GATE // 128 + t)),
                  pl.BlockSpec((32, 128), lambda bi, t: (0, t)),
                  pl.BlockSpec((1, 128), lambda bi, t: (0, t))],
        out_specs=[pl.BlockSpec((1, tp, 128), lambda bi, t: (bi, 0, t))],
        scratch_shapes=[pltpu.VMEM((tp + 32, 128), F32), pltpu.VMEM((win, 128), F32)],
        operands=(u, u, conv_w, conv_b))
    return hc, got


def _gla_group(nchunk):
    return 11 if nchunk % 11 == 0 else nchunk


def _bdot(a, b, ca, cb, precision=None):
    return lax.dot_general(a, b, (((ca,), (cb,)), ((0,), (0,))), preferred_element_type=F32, precision=precision)


def _bnn(a, b, **kw):
    return _bdot(a, b, 2, 1, **kw)


def _bnt(a, b, **kw):
    return _bdot(a, b, 2, 2, **kw)


def _gla_consts(nb):
    row = lax.broadcasted_iota(jnp.int32, (nb, CHUNK, CHUNK), 1)
    col = lax.broadcasted_iota(jnp.int32, (nb, CHUNK, CHUNK), 2)
    lane = lax.broadcasted_iota(jnp.int32, (1, 1, 128), 2)
    return row >= col, row <= col, [lane < GLA_DK, lane >= GLA_DK]


def _gla_group_terms(g, nb, q_ref, k_ref, gd_ref, gup_ref, gb_ref, tril):
    m = nb * CHUNK
    rows = pl.ds(pl.multiple_of(g * m, CHUNK), m)
    z = _nn(gd_ref[0, rows, :].astype(BF16), gup_ref[...]) + gb_ref[...]
    valid = g * m + lax.broadcasted_iota(jnp.int32, (m, 1), 0) >= PAD
    lg = jnp.where(valid, _log_sigmoid(z) * (1.0 / GLA_TAU), 0.0)
    bcum = _bnn(tril.astype(F32), lg.reshape(nb, CHUNK, 128), precision=lax.Precision.HIGHEST)
    blast = bcum[:, CHUNK - 1:CHUNK, :]
    eb = jnp.exp(bcum)
    enb = jnp.exp(-bcum)
    erest = jnp.exp(blast - bcum)
    q = (q_ref[0, rows, :] * Q_SCALE).reshape(nb, CHUNK, 128)
    k = k_ref[0, rows, :].reshape(nb, CHUNK, 128)
    return rows, valid, z, eb, enb, erest, jnp.exp(blast), q * eb, k * enb, k * erest


def _grid_ends(grid):
    ids = [pl.program_id(i) for i in range(len(grid))]
    first = functools.reduce(jnp.logical_and, [i == 0 for i in ids])
    last = functools.reduce(jnp.logical_and, [i == g - 1 for i, g in zip(ids, grid)])
    return first, last


def _gla_fwd(u, gup, gbias, gnorm, fuse=NO_FUSE):
    bsz, tp, _ = u.shape
    nchunk = tp // CHUNK
    nb = _gla_group(nchunk)

    def body(q_ref, k_ref, v_ref, r_ref, gd_ref, gup_ref, gb_ref, gn_ref, out_ref, o_ref, st_ref, s_scr):
        tril, _, hmask = _gla_consts(nb)
        s_scr[...] = jnp.zeros_like(s_scr)
        gn = gn_ref[...]

        def group(g, carry):
            rows, _, _, _, _, _, dec, qe, ke, kd = _gla_group_terms(g, nb, q_ref, k_ref, gd_ref, gup_ref, gb_ref, tril)
            keb, kdb = ke.astype(BF16), kd.astype(BF16)
            for h in range(2):
                cols = slice(h * GLA_DV, (h + 1) * GLA_DV)
                qh = jnp.where(hmask[h], qe, 0.0).astype(BF16)
                vh = v_ref[0, rows, cols].astype(BF16).reshape(nb, CHUNK, GLA_DV)
                a = jnp.where(tril, _bnt(qh, keb), 0.0).astype(BF16)
                st = s_scr[h]
                sts = []
                for n in range(nb):
                    st_ref[0, h, g * nb + n] = st
                    sts.append(st.astype(BF16))
                    st = dec[n] * st + _tn(vh[n], kdb[n])
                s_scr[h] = st
                o = (_bnn(a, vh) + _bnt(qh, jnp.stack(sts))).reshape(nb * CHUNK, GLA_DV)
                o_ref[0, rows, cols] = o
                rms = lax.rsqrt(jnp.mean(o * o, axis=-1, keepdims=True) + LN_EPS)
                rh = r_ref[0, rows, cols]
                out_ref[0, rows, cols] = (o * rms * gn * (rh * _sigmoid(rh))).astype(BF16)
            return carry

        lax.fori_loop(0, nchunk // nb, group, 0)

    res, got = _fused_call(
        body, fuse, name="gla_fwd",
        out_shape=[jax.ShapeDtypeStruct((bsz, tp, 512), BF16), jax.ShapeDtypeStruct((bsz, tp, 512), F32),
                   jax.ShapeDtypeStruct((bsz, GLA_HEADS, nchunk, GLA_DV, 128), F32)],
        grid=(bsz, 2),
        in_specs=[pl.BlockSpec((1, tp, 128), lambda bi, p: (bi, 0, C_Q // 128 + p)),
                  pl.BlockSpec((1, tp, 128), lambda bi, p: (bi, 0, C_K // 128 + p)),
                  pl.BlockSpec((1, tp, 256), lambda bi, p: (bi, 0, C_V // 256 + p)),
                  pl.BlockSpec((1, tp, 256), lambda bi, p: (bi, 0, C_R // 256 + p)),
                  pl.BlockSpec((1, tp, 128), lambda bi, p: (bi, 0, C_GD // 128)),
                  pl.BlockSpec((128, 128), lambda bi, p: (0, p)),
                  pl.BlockSpec((1, 128), lambda bi, p: (0, p)),
                  pl.BlockSpec((1, 128), lambda bi, p: (0, 0))],
        out_specs=[pl.BlockSpec((1, tp, 256), lambda bi, p: (bi, 0, p)),
                   pl.BlockSpec((1, tp, 256), lambda bi, p: (bi, 0, p)),
                   pl.BlockSpec((1, 2, nchunk, GLA_DV, 128), lambda bi, p: (bi, p, 0, 0, 0))],
        scratch_shapes=[pltpu.VMEM((2, GLA_DV, 128), F32)],
        operands=(u, u, u, u, u, gup, gbias, gnorm))
    return res[0], res[1], res[2], got


def _out_proj_ln1(hc, gla_out, w_out, s0, cg, cb, g1, b1, fuse=NO_FUSE):
    r, d = s0.shape
    tm = _row_tile(r)

    def body(hc_ref, a_ref, w_ref, s0_ref, cg_ref, cb_ref, g_ref, b_ref, co_ref, xh_ref, rstd_ref, s1b_ref):
        for rs in _sub_rows(tm):
            xc, _ = _ln_stats(hc_ref[rs, :])
            nv = xc * cg_ref[...] + cb_ref[...]
            co = (nv * _sigmoid(nv)).astype(BF16)
            co_ref[rs, :] = co
            mix = _nn(co, w_ref[0:D_CONV, :]) + _nn(a_ref[rs, :], w_ref[D_CONV:, :])
            xh, rstd = _ln_stats(ALPHA * s0_ref[rs, :] + mix)
            xh_ref[rs, :] = xh
            rstd_ref[rs, :] = rstd
            s1b_ref[rs, :] = (xh * g_ref[...] + b_ref[...]).astype(BF16)

    row = lambda n: pl.BlockSpec((tm, n), lambda i: (i, 0))
    vec = lambda n: pl.BlockSpec((1, n), lambda i: (0, 0))
    res, got = _fused_call(
        body, fuse, name="out_proj_ln1",
        out_shape=[jax.ShapeDtypeStruct((r, D_CONV), BF16), jax.ShapeDtypeStruct((r, d), F32),
                   jax.ShapeDtypeStruct((r, 1), F32), jax.ShapeDtypeStruct((r, d), BF16)],
        grid=(r // tm,),
        in_specs=[row(D_CONV), row(512), pl.BlockSpec((d, d), lambda i: (0, 0)), row(d),
                  vec(D_CONV), vec(D_CONV), vec(d), vec(d)],
        out_specs=[row(D_CONV), row(d), row(1), row(d)], scratch_shapes=[],
        operands=(hc, gla_out, w_out, s0, cg, cb, g1, b1))
    return res[0], res[1], res[2], res[3], got


def _ffn1(s1b, w1, fuse=NO_FUSE):
    r, d = s1b.shape
    tm = _row_tile(r)
    ns, _, wn = w1.shape

    def body(a_ref, w_ref, o_ref):
        a = a_ref[...]
        for j in range(ns):
            o_ref[:, j * wn:(j + 1) * wn] = jnp.maximum(_nn(a, w_ref[j]), 0.0).astype(BF16)

    (ra,), got = _fused_call(
        body, fuse, name="ffn1", out_shape=[jax.ShapeDtypeStruct((r, D_FF), BF16)], grid=(r // tm,),
        in_specs=[pl.BlockSpec((tm, d), lambda i: (i, 0)), pl.BlockSpec(w1.shape, lambda i: (0, 0, 0))],
        out_specs=[pl.BlockSpec((tm, D_FF), lambda i: (i, 0))], scratch_shapes=[], operands=(s1b, w1))
    return ra, got


def _ffn2_ln2_loss(ra, w2, xhat1, g1, b1, g2, b2, tgt, tp):
    r, d = xhat1.shape
    tm = _row_tile(tp)
    per = tp // tm

    def body(ra_ref, w_ref, xh1_ref, g1_ref, b1_ref, g2_ref, b2_ref, tgt_ref, dt_ref, dtb_ref, acc_ref, t_ref, sem):
        i = pl.program_id(0)
        b, j = i // per, i % per

        @pl.when(i == 0)
        def _():
            acc_ref[...] = jnp.zeros_like(acc_ref)

        head_copy = pltpu.make_async_copy(tgt_ref.at[b, pl.ds(0, tm - HEAD), :], t_ref.at[pl.ds(HEAD, tm - HEAD), :], sem)
        body_copy = pltpu.make_async_copy(
            tgt_ref.at[b, pl.ds(pl.multiple_of(jnp.maximum(j * tm - HEAD, 0), 64), tm), :], t_ref, sem)

        @pl.when(j == 0)
        def _():
            t_ref[0:HEAD, :] = jnp.zeros((HEAD, d), F32)
            head_copy.start()

        pl.when(j > 0)(body_copy.start)

        sums = [jnp.zeros((1, d), F32)] * 3
        for rs in _sub_rows(tm):
            rb = ra_ref[rs, :]
            f = _nn(rb * rb, w_ref[...])
            if rs.start == 0:
                pl.when(j == 0)(head_copy.wait)
                pl.when(j > 0)(body_copy.wait)
            s1 = xh1_ref[rs, :] * g1_ref[...] + b1_ref[...]
            xh2, rstd2 = _ln_stats(ALPHA * s1 + f)
            y = xh2 * g2_ref[...] + b2_ref[...]
            rowid = (i % per) * tm + rs.start + lax.broadcasted_iota(jnp.int32, (rs.stop - rs.start, 1), 0)
            e = jnp.where(rowid >= HEAD, y - t_ref[rs, :], 0.0)
            dy = e * (1.0 / d)
            dt2 = _ln_bwd(dy * g2_ref[...], xh2, rstd2)
            dt_ref[rs, :] = dt2
            dtb_ref[rs, :] = dt2.astype(BF16)
            sums = [sums[0] + (0.5 / d) * jnp.sum(e * e, axis=0, keepdims=True),
                    sums[1] + jnp.sum(dy * xh2, axis=0, keepdims=True), sums[2] + jnp.sum(dy, axis=0, keepdims=True)]
        for k in range(3):
            acc_ref[k:k + 1, :] += sums[k]

    row = lambda n: pl.BlockSpec((tm, n), lambda i: (i, 0))
    vec = pl.BlockSpec((1, d), lambda i: (0, 0))
    return pl.pallas_call(
        body, name="ffn2_ln2_loss",
        out_shape=[jax.ShapeDtypeStruct((r, d), F32), jax.ShapeDtypeStruct((r, d), BF16),
                   jax.ShapeDtypeStruct((8, d), F32)],
        grid=(r // tm,),
        in_specs=[row(D_FF), pl.BlockSpec((D_FF, d), lambda i: (0, 0)), row(d), vec, vec, vec, vec, ANY],
        out_specs=[row(d), row(d), pl.BlockSpec((8, d), lambda i: (0, 0))],
        scratch_shapes=[pltpu.VMEM((tm, d), F32), pltpu.SemaphoreType.DMA],
        compiler_params=_params(("arbitrary",)),
    )(ra, w2, xhat1, g1, b1, g2, b2, tgt)


def _ffn_bwd_da(dt2b, w2, ra):
    r, d = dt2b.shape
    tm = _row_tile(r)

    def body(g_ref, w_ref, ra_ref, o_ref):
        o_ref[...] = (_nt(g_ref[...], w_ref[...]) * (2.0 * ra_ref[...].astype(F32))).astype(BF16)

    return pl.pallas_call(
        body, name="ffn_bwd_da", out_shape=jax.ShapeDtypeStruct((r, D_FF), BF16),
        grid=(r // tm,),
        in_specs=[pl.BlockSpec((tm, d), lambda i: (i, 0)), pl.BlockSpec((D_FF, d), lambda i: (0, 0)),
                  pl.BlockSpec((tm, D_FF), lambda i: (i, 0))],
        out_specs=pl.BlockSpec((tm, D_FF), lambda i: (i, 0)),
        compiler_params=_params(("parallel",)),
    )(dt2b, w2, ra)


def _ffn_bwd_ln1(da, w1, dt2, xhat1, rstd1, g1):
    r, d = dt2.shape
    tm = _row_tile(r)

    def body(da_ref, w_ref, dt2_ref, xh_ref, rstd_ref, g_ref, dt_ref, dtb_ref, acc_ref):
        @pl.when(pl.program_id(0) == 0)
        def _():
            acc_ref[...] = jnp.zeros_like(acc_ref)

        sums = [jnp.zeros((1, d), F32)] * 2
        for rs in _sub_rows(tm):
            ds1 = ALPHA * dt2_ref[rs, :]
            for j in range(w1.shape[0]):
                ds1 = ds1 + _nt(da_ref[rs, j * w1.shape[2]:(j + 1) * w1.shape[2]], w_ref[j])
            xh = xh_ref[rs, :]
            dt1 = _ln_bwd(ds1 * g_ref[...], xh, rstd_ref[rs, :])
            dt_ref[rs, :] = dt1
            dtb_ref[rs, :] = dt1.astype(BF16)
            sums = [sums[0] + jnp.sum(ds1 * xh, axis=0, keepdims=True), sums[1] + jnp.sum(ds1, axis=0, keepdims=True)]
        for k in range(2):
            acc_ref[k:k + 1, :] += sums[k]

    row = lambda n: pl.BlockSpec((tm, n), lambda i: (i, 0))
    return pl.pallas_call(
        body, name="ffn_bwd_ln1",
        out_shape=[jax.ShapeDtypeStruct((r, d), F32), jax.ShapeDtypeStruct((r, d), BF16),
                   jax.ShapeDtypeStruct((8, d), F32)],
        grid=(r // tm,),
        in_specs=[row(D_FF), pl.BlockSpec(w1.shape, lambda i: (0, 0, 0)), row(d), row(d), row(1),
                  pl.BlockSpec((1, d), lambda i: (0, 0))],
        out_specs=[row(d), row(d), pl.BlockSpec((8, d), lambda i: (0, 0))],
        compiler_params=_params(("arbitrary",)),
    )(da, w1, dt2, xhat1, rstd1, g1)


def _matmul_tn(lhs, rhs, bm, square_lhs=False, name="matmul_tn", fuse=NO_FUSE):
    r, m = lhs.shape
    n = rhs.shape[1]
    tk = _reduce_tile(r, True)

    def body(a_ref, b_ref, o_ref):
        @pl.when(pl.program_id(1) == 0)
        def _():
            o_ref[...] = jnp.zeros_like(o_ref)

        a = a_ref[...]
        if square_lhs:
            a = a * a
        o_ref[...] += _tn(a, b_ref[...])

    (out,), got = _fused_call(
        body, fuse, name=name, out_shape=[jax.ShapeDtypeStruct((m, n), F32)], grid=(m // bm, r // tk),
        in_specs=[pl.BlockSpec((tk, bm), lambda i, k: (k, i)), pl.BlockSpec((tk, n), lambda i, k: (k, 0))],
        out_specs=[pl.BlockSpec((bm, n), lambda i, k: (i, 0))], scratch_shapes=[], operands=(lhs, rhs))
    return out, got


def _grad_w_ff1(s1b, da, fuse=NO_FUSE):
    r, d = s1b.shape
    wn = da.shape[1] // 4
    tk = _reduce_tile(r, True)

    def body(a_ref, b_ref, o_ref):
        @pl.when(pl.program_id(1) == 0)
        def _():
            o_ref[...] = jnp.zeros_like(o_ref)

        o_ref[0] += _tn(a_ref[...], b_ref[...])

    (out,), got = _fused_call(
        body, fuse, name="grad_w_ff1", out_shape=[jax.ShapeDtypeStruct((4, d, wn), F32)], grid=(4, r // tk),
        in_specs=[pl.BlockSpec((tk, d), lambda j, k: (k, 0)), pl.BlockSpec((tk, wn), lambda j, k: (k, j))],
        out_specs=[pl.BlockSpec((1, d, wn), lambda j, k: (j, 0, 0))], scratch_shapes=[], operands=(s1b, da))
    return out, got


def _grad_w_out(conv_of, gla_of, dt1b):
    r, n = dt1b.shape
    tk = _reduce_tile(r, True)

    def body(a_ref, b_ref, g_ref, o_ref):
        @pl.when(pl.program_id(1) == 0)
        def _():
            o_ref[...] = jnp.zeros_like(o_ref)

        @pl.when(pl.program_id(0) == 0)
        def _():
            o_ref[...] += _tn(a_ref[...], g_ref[...])

        @pl.when(pl.program_id(0) == 1)
        def _():
            o_ref[...] += _tn(b_ref[...], g_ref[...])

    lhs = pl.BlockSpec((tk, 512), lambda i, k: (k, 0))
    return pl.pallas_call(
        body, name="grad_w_out", out_shape=jax.ShapeDtypeStruct((2 * 512, n), F32), grid=(2, r // tk),
        in_specs=[lhs, lhs, pl.BlockSpec((tk, n), lambda i, k: (k, 0))],
        out_specs=pl.BlockSpec((512, n), lambda i, k: (i, 0)),
        compiler_params=_params(("parallel", "arbitrary")),
    )(conv_of, gla_of, dt1b)


def _out_proj_bwd(dt1b, w_out, hc, cg, cb, fuse=NO_FUSE):
    r, d = dt1b.shape
    tm = _row_tile(r)

    def body(g_ref, w_ref, hc_ref, cg_ref, cb_ref, dhc_ref, dgla_ref, acc_ref):
        @pl.when(pl.program_id(0) == 0)
        def _():
            acc_ref[...] = jnp.zeros_like(acc_ref)

        gg = cg_ref[...]
        sums = [jnp.zeros((1, D_CONV), F32)] * 3
        for rs in _sub_rows(tm):
            dmix = _nt(g_ref[rs, :], w_ref[...])
            dgla_ref[rs, :] = dmix[:, D_CONV:]
            xh, rstd = _ln_stats(hc_ref[rs, :])
            nv = xh * gg + cb_ref[...]
            sig = _sigmoid(nv)
            dn = dmix[:, :D_CONV] * (sig * (1.0 + nv * (1.0 - sig)))
            dhc = _ln_bwd(dn * gg, xh, rstd)
            dhc_ref[rs, :] = dhc
            sums = [sums[0] + jnp.sum(dhc, axis=0, keepdims=True), sums[1] + jnp.sum(dn * xh, axis=0, keepdims=True),
                    sums[2] + jnp.sum(dn, axis=0, keepdims=True)]
        for k in range(3):
            acc_ref[k:k + 1, :] += sums[k]

    row = lambda n: pl.BlockSpec((tm, n), lambda i: (i, 0))
    vec = pl.BlockSpec((1, D_CONV), lambda i: (0, 0))
    res, got = _fused_call(
        body, fuse, name="out_proj_bwd",
        out_shape=[jax.ShapeDtypeStruct((r, D_CONV), F32), jax.ShapeDtypeStruct((r, 512), F32),
                   jax.ShapeDtypeStruct((8, D_CONV), F32)],
        grid=(r // tm,),
        in_specs=[row(d), pl.BlockSpec((d, d), lambda i: (0, 0)), row(D_CONV), vec, vec],
        out_specs=[row(D_CONV), row(512), pl.BlockSpec((8, D_CONV), lambda i: (0, 0))], scratch_shapes=[],
        operands=(dt1b, w_out, hc, cg, cb))
    return res[0], res[1], res[2], got


def _conv_bwd(dhc, u, conv_w, fuse=NO_FUSE):
    bsz, tp, _ = u.shape
    nchunk = tp // CHUNK
    win = CHUNK + 32
    nct = D_CONV // 128

    def body(dhc_ref, cv_ref, cg_ref, w_ref, dv_ref, dg_ref, dw_ref, h_scr, dhc_scr, hwin, dwin, dw_scr):
        h_scr[0:32, :] = jnp.zeros((32, 128), F32)
        h_scr[32:32 + tp, :] = cv_ref[0] * _sigmoid(cg_ref[0])
        dhc_scr[0:tp, :] = dhc_ref[0]
        dhc_scr[tp:tp + 32, :] = jnp.zeros((32, 128), F32)
        dw_scr[...] = jnp.zeros_like(dw_scr)

        def step(n, carry):
            r0 = pl.multiple_of(n * CHUNK, CHUNK)
            rows = pl.ds(r0, CHUNK)
            hwin[...] = h_scr[pl.ds(r0, win), :]
            dwin[...] = dhc_scr[pl.ds(r0, win), :]
            dcur = dwin[0:CHUNK, :]
            acc = jnp.zeros((CHUNK, 128), F32)
            for j in range(CONV_WIDTH):
                acc = acc + w_ref[j:j + 1, :] * dwin[30 - j:30 - j + CHUNK, :]
                prod = dcur * hwin[2 + j:2 + j + CHUNK, :]
                dw_scr[j * 8:(j + 1) * 8, :] += jnp.sum(prod.reshape(CHUNK // 8, 8, 128), axis=0)
            cg = cg_ref[0, rows, :]
            sig = _sigmoid(cg)
            rowid = n * CHUNK + lax.broadcasted_iota(jnp.int32, (CHUNK, 1), 0)
            dh = jnp.where(rowid >= PAD, acc, 0.0)
            dv_ref[0, rows, :] = (dh * sig).astype(BF16)
            dg_ref[0, rows, :] = (dh * cv_ref[0, rows, :] * sig * (1.0 - sig)).astype(BF16)
            return carry

        lax.fori_loop(0, nchunk, step, 0)
        dw_ref[0] = jnp.zeros((32, 128), F32)
        for j in range(CONV_WIDTH):
            dw_ref[0, j:j + 1, :] = jnp.sum(dw_scr[j * 8:(j + 1) * 8, :], axis=0, keepdims=True)

    blk = lambda off: pl.BlockSpec((1, tp, 128), lambda bi, t: (bi, 0, off // 128 + t))
    res, got = _fused_call(
        body, fuse, name="conv_bwd",
        out_shape=[jax.ShapeDtypeStruct((bsz, tp, D_CONV), BF16), jax.ShapeDtypeStruct((bsz, tp, D_CONV), BF16),
                   jax.ShapeDtypeStruct((bsz, 32, D_CONV), F32)],
        grid=(bsz, nct),
        in_specs=[blk(0), blk(C_VAL), blk(C_GATE), pl.BlockSpec((32, 128), lambda bi, t: (0, t))],
        out_specs=[blk(0), blk(0), pl.BlockSpec((1, 32, 128), lambda bi, t: (bi, 0, t))],
        scratch_shapes=[pltpu.VMEM((tp + 32, 128), F32), pltpu.VMEM((tp + 32, 128), F32),
                        pltpu.VMEM((win, 128), F32), pltpu.VMEM((win, 128), F32),
                        pltpu.VMEM((CONV_WIDTH * 8, 128), F32)],
        operands=(dhc, u, u, conv_w))
    return res[0], res[1], res[2], got


def _gla_bwd(dgla, u, o_pre, states, gup, gbias, gnorm, fuse=NO_FUSE):
    bsz, tp, _ = u.shape
    nchunk = tp // CHUNK
    nb = _gla_group(nchunk)

    def body(dy_ref, q_ref, k_ref, v_ref, r_ref, gd_ref, o_ref, st_ref, gup_ref, gb_ref, gn_ref,
             dq_ref, dk_ref, dv_ref, dr_ref, dgd_ref, dgup_ref, vec_ref, h_scr, gup_acc):
        tril, triu, hmask = _gla_consts(nb)
        h_scr[...] = jnp.zeros_like(h_scr)
        gup_acc[...] = jnp.zeros_like(gup_acc)
        gn = gn_ref[...]
        gupb = gup_ref[...]
        m = nb * CHUNK
        ngroup = nchunk // nb

        def group(i, carry):
            dbias, dgn = carry
            g = ngroup - 1 - i
            rows, valid, z, eb, enb, erest, dec, qe, ke, kd = _gla_group_terms(
                g, nb, q_ref, k_ref, gd_ref, gup_ref, gb_ref, tril)
            keb, kdb = ke.astype(BF16), kd.astype(BF16)
            dqe = jnp.zeros((nb, CHUNK, 128), F32)
            dke = jnp.zeros((nb, CHUNK, 128), F32)
            dkd = jnp.zeros((nb, CHUNK, 128), F32)
            ddec = jnp.zeros((nb, 1, 128), F32)
            for h in range(2):
                cols = slice(h * GLA_DV, (h + 1) * GLA_DV)
                o = o_ref[0, rows, cols]
                rh = r_ref[0, rows, cols]
                dy = dy_ref[0, rows, cols]
                rms = lax.rsqrt(jnp.mean(o * o, axis=-1, keepdims=True) + LN_EPS)
                nrm = o * rms
                sig = _sigmoid(rh)
                sw = rh * sig
                dr_ref[0, rows, cols] = (dy * nrm * gn * (sig * (1.0 + rh * (1.0 - sig)))).astype(BF16)
                dgn = dgn + jnp.sum(dy * nrm * sw, axis=0, keepdims=True)
                dn = dy * gn * sw
                do = rms * (dn - nrm * jnp.mean(dn * nrm, axis=-1, keepdims=True))
                dob = do.astype(BF16).reshape(nb, CHUNK, GLA_DV)
                qh = jnp.where(hmask[h], qe, 0.0).astype(BF16)
                vh = v_ref[0, rows, cols].astype(BF16).reshape(nb, CHUNK, GLA_DV)
                ht = h_scr[h]
                hts = [None] * nb
                for n in reversed(range(nb)):
                    hts[n] = ht
                    ht = dec[n] * ht + _tn(dob[n], qh[n])
                h_scr[h] = ht
                htf = jnp.stack(hts)
                htb = htf.astype(BF16)
                st = st_ref[0, h, pl.ds(g * nb, nb)]
                at = jnp.where(triu, _bnt(keb, qh), 0.0).astype(BF16)
                da = jnp.where(tril, _bnt(dob, vh), 0.0).astype(BF16)
                dat = jnp.where(triu, _bnt(vh, dob), 0.0).astype(BF16)
                dqe = dqe + jnp.where(hmask[h], _bnn(da, keb) + _bnn(dob, st.astype(BF16)), 0.0)
                dke = dke + _bnn(dat, qh)
                dv_ref[0, rows, cols] = (_bnn(at, dob) + _bnt(kdb, htb)).reshape(m, GLA_DV).astype(BF16)
                dkd = dkd + jnp.where(hmask[h], _bnn(vh, htb), 0.0)
                ddec = ddec + jnp.where(hmask[h], jnp.sum(htf * st, axis=1, keepdims=True), 0.0)
            dq_ref[0, rows, :] = (dqe * eb * Q_SCALE).reshape(m, 128).astype(BF16)
            dk_ref[0, rows, :] = (dke * enb + dkd * erest).reshape(m, 128).astype(BF16)
            db = dqe * qe - dke * ke - dkd * kd
            dblast = jnp.sum(dkd * kd, axis=1, keepdims=True) + ddec * dec
            lastrow = lax.broadcasted_iota(jnp.int32, (1, CHUNK, 1), 1) == CHUNK - 1
            db = db + jnp.where(lastrow, dblast, 0.0)
            dlg = _bnn(triu.astype(F32), db, precision=lax.Precision.HIGHEST).reshape(m, 128)
            dz = jnp.where(valid, dlg, 0.0) * (1.0 / GLA_TAU) * (1.0 - _sigmoid(z))
            dzb = dz.astype(BF16)
            dgd_ref[0, 0, rows, :] = _nt(dzb, gupb).astype(BF16)
            gup_acc[...] += _tn(gd_ref[0, rows, :].astype(BF16), dzb)
            return dbias + jnp.sum(dz, axis=0, keepdims=True), dgn

        zero = jnp.zeros((1, 128), F32)
        dbias, dgn = lax.fori_loop(0, ngroup, group, (zero, zero))
        dgup_ref[0] = gup_acc[...]
        vec_ref[0] = jnp.zeros((8, 128), F32)
        vec_ref[0, 0:1, :] = dbias
        vec_ref[0, 1:2, :] = dgn

    pair = lambda w, off: pl.BlockSpec((1, tp, w), lambda bi, p: (bi, 0, off // w + p))
    return _fused_call(
        body, fuse, name="gla_bwd",
        out_shape=[jax.ShapeDtypeStruct((bsz, tp, 256), BF16), jax.ShapeDtypeStruct((bsz, tp, 256), BF16),
                   jax.ShapeDtypeStruct((bsz, tp, 512), BF16), jax.ShapeDtypeStruct((bsz, tp, 512), BF16),
                   jax.ShapeDtypeStruct((bsz, 2, tp, 128), BF16), jax.ShapeDtypeStruct((bsz, 128, 256), F32),
                   jax.ShapeDtypeStruct((bsz, 8, 256), F32)],
        grid=(bsz, 2),
        in_specs=[pair(256, 0), pair(128, C_Q), pair(128, C_K), pair(256, C_V), pair(256, C_R),
                  pl.BlockSpec((1, tp, 128), lambda bi, p: (bi, 0, C_GD // 128)),
                  pair(256, 0),
                  pl.BlockSpec((1, 2, nchunk, GLA_DV, 128), lambda bi, p: (bi, p, 0, 0, 0)),
                  pl.BlockSpec((128, 128), lambda bi, p: (0, p)),
                  pl.BlockSpec((1, 128), lambda bi, p: (0, p)),
                  pl.BlockSpec((1, 128), lambda bi, p: (0, 0))],
        out_specs=[pair(128, 0), pair(128, 0), pair(256, 0), pair(256, 0),
                   pl.BlockSpec((1, 1, tp, 128), lambda bi, p: (bi, p, 0, 0)),
                   pl.BlockSpec((1, 128, 128), lambda bi, p: (bi, 0, p)),
                   pl.BlockSpec((1, 8, 128), lambda bi, p: (bi, 0, p))],
        scratch_shapes=[pltpu.VMEM((2, GLA_DV, 128), F32), pltpu.VMEM((128, 128), F32)],
        operands=(dgla, u, u, u, u, u, o_pre, states, gup, gbias, gnorm))


_DU_OFFSETS = (C_VAL, C_GATE, C_Q, C_K, C_V, C_R)
_DU_WIDTHS = (512, 512, 256, 256, 512, 512)


def _du_specs(tm, per, row_map):
    specs = [pl.BlockSpec((tm, w), row_map) for w in _DU_WIDTHS]
    for p in range(2):
        specs.append(pl.BlockSpec((1, 1, tm, 128), lambda *ix, p=p: (row_map(*ix)[0] // per, p, row_map(*ix)[0] % per, 0)))
    return specs


def _du_pieces(refs):
    out = [(off, ref[...]) for off, ref in zip(_DU_OFFSETS, refs[:6])]
    dgd = (refs[6][0, 0].astype(F32) + refs[7][0, 0].astype(F32)).astype(BF16)
    out.append((C_GD, dgd))
    return out


def _in_proj_bwd(pieces, dgd, w_int, dt1, tp, fuse=NO_FUSE):
    r, d = dt1.shape
    tm = _row_tile(tp)
    per = tp // tm

    def body(*refs):
        w_ref, dt_ref, o_ref = refs[8:]
        acc = ALPHA * dt_ref[...]
        for off, val in _du_pieces(refs[:8]):
            acc = acc + _nn(val, w_ref[off:off + val.shape[1], :])
        o_ref[...] = acc

    row = lambda i: (i, 0)
    (ds0,), got = _fused_call(
        body, fuse, name="in_proj_bwd", out_shape=[jax.ShapeDtypeStruct((r, d), F32)], grid=(r // tm,),
        in_specs=_du_specs(tm, per, row) + [pl.BlockSpec((D_IN_PAD, d), lambda i: (0, 0)), pl.BlockSpec((tm, d), row)],
        out_specs=[pl.BlockSpec((tm, d), row)], scratch_shapes=[], operands=(*pieces, dgd, dgd, w_int, dt1))
    return ds0, got


def _grad_w_in(pieces, dgd, s0b, tp, fuse=NO_FUSE):
    r, d = s0b.shape
    tk = _reduce_tile(tp, False)
    per = tp // tk

    def body(*refs):
        s_ref, o_ref = refs[8:]

        @pl.when(pl.program_id(0) == 0)
        def _():
            o_ref[...] = jnp.zeros_like(o_ref)

        s = s_ref[...]
        for off, val in _du_pieces(refs[:8]):
            o_ref[off:off + val.shape[1], :] += _tn(val, s)

    row = lambda k: (k, 0)
    (out,), got = _fused_call(
        body, fuse, name="grad_w_in", out_shape=[jax.ShapeDtypeStruct((D_IN_PAD, d), F32)], grid=(r // tk,),
        in_specs=_du_specs(tk, per, row) + [pl.BlockSpec((tk, d), row)],
        out_specs=[pl.BlockSpec((D_IN_PAD, d), lambda k: (0, 0))], scratch_shapes=[],
        operands=(*pieces, dgd, dgd, s0b))
    return out, got


def _ln_in_bwd(ds0, x, meta, g):
    bsz, s, d = x.shape
    tp = s + HEAD
    nh = 2
    sh = s // nh
    rc = min(256, sh)

    def body(ds_ref, x_ref, meta_ref, g_ref, gx_ref, dm_ref, vec_ref):
        h = pl.program_id(1)
        gg = g_ref[...]

        @pl.when(h == 0)
        def _():
            mh, mr = _ln_stats(meta_ref[...])
            dsm = ds_ref[0, PAD:HEAD, :]
            dm_ref[0] = _ln_bwd(dsm * gg, mh, mr)
            vec_ref[0] = jnp.zeros((8, d), F32)
            vec_ref[0, 0:1, :] = jnp.sum(dsm * mh, axis=0, keepdims=True)
            vec_ref[0, 1:2, :] = jnp.sum(dsm, axis=0, keepdims=True)

        def step(i, carry):
            sg, sb = carry
            dst = pl.ds(pl.multiple_of(i * rc, rc), rc)
            src = pl.ds(pl.multiple_of(HEAD + h * sh + i * rc, 64), rc)
            xh, rstd = _ln_stats(x_ref[0, dst, :])
            dsv = ds_ref[0, src, :]
            gx_ref[0, dst, :] = _ln_bwd(dsv * gg, xh, rstd)
            return sg + jnp.sum(dsv * xh, axis=0, keepdims=True), sb + jnp.sum(dsv, axis=0, keepdims=True)

        zero = jnp.zeros((1, d), F32)
        sg, sb = lax.fori_loop(0, sh // rc, step, (zero, zero))
        vec_ref[0, 0:1, :] += sg
        vec_ref[0, 1:2, :] += sb

    return pl.pallas_call(
        body, name="ln_in_bwd",
        out_shape=[jax.ShapeDtypeStruct((bsz, s, d), F32), jax.ShapeDtypeStruct((bsz, N_META, d), F32),
                   jax.ShapeDtypeStruct((bsz, 8, d), F32)],
        grid=(bsz, nh),
        in_specs=[pl.BlockSpec((1, tp, d), lambda bi, hi: (bi, 0, 0)),
                  pl.BlockSpec((1, sh, d), lambda bi, hi: (bi, hi, 0)),
                  pl.BlockSpec((N_META, d), lambda bi, hi: (0, 0)),
                  pl.BlockSpec((1, d), lambda bi, hi: (0, 0))],
        out_specs=[pl.BlockSpec((1, sh, d), lambda bi, hi: (bi, hi, 0)),
                   pl.BlockSpec((1, N_META, d), lambda bi, hi: (bi, 0, 0)),
                   pl.BlockSpec((1, 8, d), lambda bi, hi: (bi, 0, 0))],
        compiler_params=_params(("parallel", "arbitrary")),
    )(ds0, x, meta, g)


def _rows128(a):
    return a.reshape(-1, 128)


def kernel(x, meta_tokens, ln_in_g, ln_in_b, w_in, conv_w, conv_b, conv_ln_g, conv_ln_b, gate_up, gate_bias, gla_norm_g, w_out, ln1_g, ln1_b, w_ff1, w_ff2, ln2_g, ln2_b, loss_target, m_meta_tokens, m_ln_in_g, m_ln_in_b, m_w_in, m_conv_w, m_conv_b, m_conv_ln_g, m_conv_ln_b, m_gate_up, m_gate_bias, m_gla_norm_g, m_w_out, m_ln1_g, m_ln1_b, m_w_ff1, m_w_ff2, m_ln2_g, m_ln2_b, v_meta_tokens, v_ln_in_g, v_ln_in_b, v_w_in, v_conv_w, v_conv_b, v_conv_ln_g, v_conv_ln_b, v_gate_up, v_gate_bias, v_gla_norm_g, v_w_out, v_ln1_g, v_ln1_b, v_w_ff1, v_w_ff2, v_ln2_g, v_ln2_b):
    bsz, seq, d = x.shape
    tp = seq + HEAD
    r = bsz * tp
    xi, yi, ci = _mesh_pos()
    chip = 2 * xi + yi
    c_arr = jnp.reshape(ci, (1,)).astype(jnp.int32)
    pos_arr = jnp.stack([chip, ci]).astype(jnp.int32)

    sh_in = D_IN // 4
    shard_in = jnp.pad(w_in[0].T.astype(BF16), ((0, D_IN_PAD // 4 - sh_in), (0, 0)))
    shard_w1, shard_wout, shard_w2 = w_ff1[0].astype(BF16), w_out[0].astype(BF16), w_ff2[0].astype(BF16)
    small_w = jnp.concatenate([_rows128(meta_tokens), _rows128(conv_w[0]), _rows128(gate_up[0])], axis=0)
    g_small = _gather_small(small_w)
    n_meta_rows, n_cw_rows = N_META * 256 // 128, CONV_WIDTH * 128 // 128
    meta_full = jnp.concatenate([g_small[j, :n_meta_rows].reshape(N_META, 256) for j in range(4)], axis=1)
    convw_full = jnp.concatenate(
        [g_small[j, n_meta_rows:n_meta_rows + n_cw_rows].reshape(CONV_WIDTH, 128) for j in range(4)], axis=1)
    gup_full = jnp.concatenate(
        [g_small[j, n_meta_rows + n_cw_rows:].reshape(GLA_RANK, 64) for j in range(4)], axis=1)
    convw_p = jnp.pad(convw_full, ((0, 1), (0, 0)))
    gup_p = jnp.pad(gup_full, ((0, 128 - GLA_RANK), (0, 0))).astype(BF16)
    ln_in_g2, ln_in_b2 = ln_in_g.reshape(1, d), ln_in_b.reshape(1, d)

    s0, s0b, ((g_int,),) = _ln_in_fwd(x, meta_full, ln_in_g2, ln_in_b2,
                                      [("gather", [(shard_in, 0, D_IN_PAD // 4, None)])])
    (g_int,) = _place_own([g_int], [shard_in])
    w_int = jnp.pad(g_int[:, :sh_in].reshape(D_IN, d), ((0, D_IN_PAD - D_IN), (0, 0)))
    s0f, s0bf = s0.reshape(r, d), s0b.reshape(r, d)
    u, ((w1_buf,),) = _in_proj(s0bf, w_int, [("gather", [(shard_w1, 0, 640, None)])])
    (w1_buf,) = _place_own([w1_buf], [shard_w1])
    u3 = u.reshape(bsz, tp, D_IN_PAD)
    hc, ((w1_buf,),) = _conv_fwd(u3, convw_p, conv_b, [("gather", [(shard_w1, 640, 384, w1_buf)])])
    hc = hc.reshape(r, D_CONV)
    gla_out, o_pre, states, ((g_wout, w2_buf),) = _gla_fwd(
        u3, gup_p, gate_bias, gla_norm_g, [("gather", [(shard_wout, 0, 256, None), (shard_w2, 0, 256, None)])])
    g_wout, w2_buf = _place_own([g_wout, w2_buf], [shard_wout, shard_w2])
    wout = g_wout.reshape(d, d)
    gla_of = gla_out.reshape(r, 512)
    conv_of, xhat1, rstd1, s1b, ((w2_buf,),) = _out_proj_ln1(
        hc, gla_of, wout, s0f, conv_ln_g, conv_ln_b, ln1_g, ln1_b, [("gather", [(shard_w2, 256, 384, w2_buf)])])
    w1 = w1_buf
    ra, ((w2_buf,),) = _ffn1(s1b, w1, [("gather", [(shard_w2, 640, 384, w2_buf)])])
    w2 = w2_buf.reshape(D_FF, d)
    dt2, dt2b, acc2 = _ffn2_ln2_loss(ra, w2, xhat1, ln1_g, ln1_b, ln2_g, ln2_b, loss_target, tp)

    def add_pair(g, got, splits=None):
        return _add_pair(g, got, c_arr, splits)

    da = _ffn_bwd_da(dt2b, w2, ra)
    dt1, dt1b, acc1 = _ffn_bwd_ln1(da, w1, dt2, xhat1, rstd1, ln1_g)
    g_w2, _ = _matmul_tn(ra, dt2b, 1024, square_lhs=True, name="grad_w_ff2")
    big_w2 = g_w2.reshape(4, D_FF // 4, d)
    big_w1, ((pair_w2,),) = _grad_w_ff1(s1b, da, [("pair", [big_w2])])
    dhc, dgla, cacc, ((pair_w1,),) = _out_proj_bwd(dt1b, wout, hc, conv_ln_g, conv_ln_b, [("pair", [big_w1])])
    (part_w1,), (part_w2a, part_w2b) = add_pair(big_w1, pair_w1), add_pair(big_w2, pair_w2, [512, 512])
    big_wout = _grad_w_out(conv_of, gla_of, dt1b).reshape(4, d // 4, d)
    dcv, dcg, dcw, ((chip_w1,), (pair_wout,)) = _conv_bwd(
        dhc.reshape(bsz, tp, D_CONV), u3, convw_p, [("exchange", [part_w1]), ("pair", [big_wout])])
    (part_wout,) = add_pair(big_wout, pair_wout)
    (dq, dk, dv, dr, dgd, dgup, gvec), ((chip_w2a,),) = _gla_bwd(
        dgla.reshape(bsz, tp, 512), u3, o_pre, states, gup_p, gate_bias, gla_norm_g, [("exchange", [part_w2a])])
    pieces = [a.reshape(r, a.shape[-1]) for a in (dcv, dcg, dq, dk, dv, dr)]
    g_wint, ((chip_w2b, chip_wout),) = _grad_w_in(
        pieces, dgd, s0bf, tp, [("exchange", [part_w2b, part_wout])])
    big_win = jnp.stack([g_wint[j * sh_in:(j + 1) * sh_in] for j in range(4)])
    (pair_win,) = _pair_exchange([big_win])
    half_w1 = _add_chips(big_w1, pair_w1, [chip_w1], pos_arr)
    half_w2 = _add_chips(big_w2, pair_w2, [chip_w2a, chip_w2b], pos_arr)
    ds0, ((chip_win,), (sib_w1, sib_w2)) = _in_proj_bwd(
        pieces, dgd, w_int, dt1, tp, [("exchange", add_pair(big_win, pair_win)), ("swap", [half_w1, half_w2])])
    grad_x, dmeta, lvec = _ln_in_bwd(ds0.reshape(bsz, tp, d), x, meta_full, ln_in_g2)

    red, loss = _sum_small(_allgather_small(_pack_small(acc2, acc1, lvec, cacc, gvec, dcw, dgup, dmeta)))

    half_win = _add_chips(big_win, pair_win, [chip_win], pos_arr)
    half_wout = _add_chips(big_wout, pair_wout, [chip_wout], pos_arr)
    sib_win, sib_wout = _pair_swap([half_win, half_wout])
    halves = {"w_in": (half_win, sib_win), "w_out": (half_wout, sib_wout), "w_ff1": (half_w1, sib_w1),
              "w_ff2": (half_w2, sib_w2)}

    grads = {}
    weights = dict(meta_tokens=meta_tokens, ln_in_g=ln_in_g, ln_in_b=ln_in_b, w_in=w_in, conv_w=conv_w, conv_b=conv_b,
                   conv_ln_g=conv_ln_g, conv_ln_b=conv_ln_b, gate_up=gate_up, gate_bias=gate_bias,
                   gla_norm_g=gla_norm_g, w_out=w_out, ln1_g=ln1_g, ln1_b=ln1_b, w_ff1=w_ff1, w_ff2=w_ff2,
                   ln2_g=ln2_g, ln2_b=ln2_b)
    moms = dict(meta_tokens=(m_meta_tokens, v_meta_tokens), ln_in_g=(m_ln_in_g, v_ln_in_g),
                ln_in_b=(m_ln_in_b, v_ln_in_b), w_in=(m_w_in, v_w_in), conv_w=(m_conv_w, v_conv_w),
                conv_b=(m_conv_b, v_conv_b), conv_ln_g=(m_conv_ln_g, v_conv_ln_g),
                conv_ln_b=(m_conv_ln_b, v_conv_ln_b), gate_up=(m_gate_up, v_gate_up),
                gate_bias=(m_gate_bias, v_gate_bias), gla_norm_g=(m_gla_norm_g, v_gla_norm_g),
                w_out=(m_w_out, v_w_out), ln1_g=(m_ln1_g, v_ln1_g), ln1_b=(m_ln1_b, v_ln1_b),
                w_ff1=(m_w_ff1, v_w_ff1), w_ff2=(m_w_ff2, v_w_ff2), ln2_g=(m_ln2_g, v_ln2_g),
                ln2_b=(m_ln2_b, v_ln2_b))
    names = list(weights)
    big_names = ("w_in", "w_out", "w_ff1", "w_ff2")
    delta, new_m, new_v = {}, {}, {}
    for k in big_names:
        to2d = (lambda a: a[0].T) if k == "w_in" else (lambda a: a[0])
        back = (lambda a: a.T[None]) if k == "w_in" else (lambda a: a[None])
        res = _adamw_halves(to2d(weights[k]), *halves[k], to2d(moms[k][0]), to2d(moms[k][1]), c_arr)
        grads[k], delta[k], new_m[k], new_v[k] = [back(a) for a in res]
    small_names = [k for k in names if k not in big_names]
    two = lambda a: a.reshape(-1, a.shape[-1])

    def my_cols(k, rows, width):
        r0 = SMALL_AT[k][0]
        return lax.dynamic_slice(red, (r0, chip * width), (rows, width))

    sharded = {"meta_tokens": my_cols("meta_tokens", N_META, 256), "conv_w": my_cols("conv_w", CONV_WIDTH, 128),
               "gate_up": my_cols("gate_up", GLA_RANK, 64)}
    upd = _adamw_small(red, sharded, {k: (two(weights[k]), two(moms[k][0]), two(moms[k][1])) for k in small_names})
    for k in small_names:
        shp = weights[k].shape
        grads[k], delta[k], new_m[k], new_v[k] = [a.reshape(shp) for a in upd[k]]
    loss = loss.reshape(())

    return (loss, grad_x, *[grads[k] for k in names], *[delta[k] for k in names],
            *[new_m[k] for k in names], *[new_v[k] for k in names])
```

```python
import functools

import jax
import jax.numpy as jnp
from jax import lax
from jax.experimental import pallas as pl
from jax.experimental.pallas import tpu as pltpu

F32 = jnp.float32
BF16 = jnp.bfloat16

D_MODEL = 1024
N_META = 16
D_CONV = 512
CONV_WIDTH = 31
GLA_HEADS = 4
GLA_DV = 128
GLA_DK = 64
GLA_RANK = 16
GLA_TAU = 16.0
CHUNK = 64
D_FF = 4096
LN_EPS = 1e-5
ALPHA = 2.0 ** 0.25
D_IN = 2576
D_IN_PAD = 2688
PAD = CHUNK - N_META
HEAD = PAD + N_META
Q_SCALE = GLA_DK ** -0.5
ADAM_LR, ADAM_B1, ADAM_B2, ADAM_EPS, ADAM_WD, ADAM_STEP = 0.001, 0.9, 0.999, 1e-08, 0.01, 10
HALF = D_MODEL // 2
VMEM_LIMIT = 56 * 1024 * 1024
MESH = pl.DeviceIdType.MESH

C_VAL, C_GATE, C_Q, C_K, C_V, C_R, C_GD = 0, 512, 1024, 1280, 1536, 2048, 2560

SMALL_AT = {"loss": (0, 1, 1024), "ln_in_g": (1, 1, 1024), "ln_in_b": (2, 1, 1024), "ln1_g": (3, 1, 1024),
            "ln1_b": (4, 1, 1024), "ln2_g": (5, 1, 1024), "ln2_b": (6, 1, 1024), "conv_b": (7, 1, 512),
            "conv_ln_g": (8, 1, 512), "conv_ln_b": (9, 1, 512), "gate_bias": (10, 1, 256), "gla_norm_g": (11, 1, 128),
            "conv_w": (16, 32, 512), "gate_up": (48, 16, 256), "meta_tokens": (64, 16, 1024)}
SMALL_ROWS = 80


def _params(sem=None, **kw):
    return pltpu.CompilerParams(dimension_semantics=sem, vmem_limit_bytes=VMEM_LIMIT, **kw)


def _row_tile(tp):
    for t in (704, 352, 192, 64):
        if tp % t == 0:
            return t
    raise ValueError(tp)


def _reduce_tile(tp, big):
    for t in ((2112, 1056, 704) if big else (1056, 704)) + (352, 192, 64):
        if tp % t == 0:
            return t
    raise ValueError(tp)


def _sub_rows(tm):
    return [slice(0, tm)]


def _dot(a, b, dims, precision=None):
    return lax.dot_general(a, b, (dims, ((), ())), preferred_element_type=F32, precision=precision)


def _nn(a, b, **kw):
    return _dot(a, b, ((1,), (0,)), **kw)


def _nt(a, b, **kw):
    return _dot(a, b, ((1,), (1,)), **kw)


def _tn(a, b, **kw):
    return _dot(a, b, ((0,), (0,)), **kw)


def _sigmoid(x):
    return 1.0 / (1.0 + jnp.exp(-x))


def _log_sigmoid(z):
    return jnp.minimum(z, 0.0) - jnp.log(1.0 + jnp.exp(-jnp.abs(z)))


def _ln_stats(t):
    mu = jnp.mean(t, axis=-1, keepdims=True)
    d = t - mu
    var = jnp.mean(d * d, axis=-1, keepdims=True)
    rstd = lax.rsqrt(var + LN_EPS)
    return d * rstd, rstd


def _ln_bwd(dxhat, xhat, rstd):
    m1 = jnp.mean(dxhat, axis=-1, keepdims=True)
    m2 = jnp.mean(dxhat * xhat, axis=-1, keepdims=True)
    return rstd * (dxhat - m1 - xhat * m2)


def _mesh_pos():
    return lax.axis_index("x"), lax.axis_index("y"), lax.axis_index("c")


ANY = pl.BlockSpec(memory_space=pl.ANY)


def _gather_sems(n):
    return [pltpu.SemaphoreType.DMA((n, 3))] * 4


def _gather_steps(ins, outs, sems, ranges=None):
    n = len(ins)
    send, recv, fsend, frecv = sems
    ranges = ranges or [(0, ref.shape[0]) for ref in ins]
    x, y, c = _mesh_pos()
    me = 2 * x + y
    sibling = (x, y, 1 - c)
    chips = [(1 - x, y), (x, 1 - y), (1 - x, 1 - y)]
    chip_idx = [2 * px + py for px, py in chips]
    mine = [pl.ds(pl.multiple_of(r0 + c * (nr // 2), 16), nr // 2) for r0, nr in ranges]
    other = [pl.ds(pl.multiple_of(r0 + (1 - c) * (nr // 2), 16), nr // 2) for r0, nr in ranges]
    pairs = [(a, k) for a in range(n) for k in range(3)]

    def ici(a, k, slab):
        return pltpu.make_async_remote_copy(
            src_ref=ins[a].at[mine[a], :], dst_ref=outs[a].at[slab, mine[a], :],
            send_sem=send.at[a, k], recv_sem=recv.at[a, k], device_id=(*chips[k], c), device_id_type=MESH)

    def forward(a, k, rows):
        blk = outs[a].at[chip_idx[k], rows[a], :]
        return pltpu.make_async_remote_copy(
            src_ref=blk, dst_ref=blk, send_sem=fsend.at[a, k], recv_sem=frecv.at[a, k],
            device_id=sibling, device_id_type=MESH)

    def start():
        for a, k in pairs:
            ici(a, k, me).start()

    def relay():
        for a, k in pairs:
            ici(a, k, chip_idx[k]).wait_recv()
            forward(a, k, mine).start()

    def finish():
        for a, k in pairs:
            forward(a, k, other).wait_recv()
        for a, k in pairs:
            ici(a, k, me).wait_send()
            forward(a, k, mine).wait_send()

    return start, relay, finish


def _place_own(gathered, shards):
    chip = 2 * lax.axis_index("x") + lax.axis_index("y")
    return [lax.dynamic_update_slice(g, s[None], (chip, 0, 0)) for g, s in zip(gathered, shards)]


def _gather_small(small):
    def body(small_in, small_out, ssend, srecv):
        x, y, c = _mesh_pos()
        chips = [(1 - x, y), (x, 1 - y), (1 - x, 1 - y)]

        def small_copy(k, slot):
            return pltpu.make_async_remote_copy(
                src_ref=small_in, dst_ref=small_out.at[slot], send_sem=ssend.at[k], recv_sem=srecv.at[k],
                device_id=(*chips[k], c), device_id_type=MESH)

        for k in range(3):
            small_copy(k, 2 * x + y).start()
        for k, (px, py) in enumerate(chips):
            small_copy(k, 2 * px + py).wait_recv()
        for k in range(3):
            small_copy(k, 2 * x + y).wait_send()

    res = pl.pallas_call(
        body, name="gather_small", out_shape=jax.ShapeDtypeStruct((4,) + small.shape, small.dtype),
        in_specs=[ANY], out_specs=ANY,
        scratch_shapes=[pltpu.SemaphoreType.DMA((3,)), pltpu.SemaphoreType.DMA((3,))],
    )(small)
    return _place_own([res], [small])[0]


def _pair_swap_steps(ins, outs, sems):
    send, recv = sems
    x, y, c = _mesh_pos()
    cps = [pltpu.make_async_remote_copy(
        src_ref=ins[a], dst_ref=outs[a], send_sem=send.at[a], recv_sem=recv.at[a],
        device_id=(x, y, 1 - c), device_id_type=MESH) for a in range(len(ins))]

    def start():
        for cp in cps:
            cp.start()

    def finish():
        for cp in cps:
            cp.wait()

    return start, finish


def _pair_exchange_steps(ins, outs, sems):
    send, recv = sems
    x, y, c = _mesh_pos()
    other = pl.ds(pl.multiple_of((1 - c) * HALF, 128), HALF)
    cps = [pltpu.make_async_remote_copy(
        src_ref=ins[a].at[:, :, other], dst_ref=outs[a], send_sem=send.at[a], recv_sem=recv.at[a],
        device_id=(x, y, 1 - c), device_id_type=MESH) for a in range(len(ins))]

    def start():
        for cp in cps:
            cp.start()

    def finish():
        for cp in cps:
            cp.wait()

    return start, finish


def _pair_exchange_shapes(grads):
    return [jax.ShapeDtypeStruct(g.shape[:2] + (HALF,), g.dtype) for g in grads]


def _pair_exchange(grads):
    n = len(grads)

    def body(*refs):
        start, finish = _pair_exchange_steps(refs[:n], refs[n:2 * n], refs[2 * n:])
        start()
        finish()

    return pl.pallas_call(
        body, name="grad_pair_exchange", out_shape=_pair_exchange_shapes(grads),
        in_specs=[ANY] * n, out_specs=[ANY] * n,
        scratch_shapes=[pltpu.SemaphoreType.DMA((n,)), pltpu.SemaphoreType.DMA((n,))],
    )(*grads)


NO_FUSE = ()


def _fuse_plan(kind, items):
    n = len(items)
    if kind == "gather":
        shards = [it[0] for it in items]
        bufs = [it[3] for it in items if it[3] is not None]
        alias, b = {}, 0
        for a, it in enumerate(items):
            if it[3] is not None:
                alias[n + b] = a
                b += 1
        ranges = [(it[1], it[2]) for it in items]
        return (shards + bufs, [jax.ShapeDtypeStruct((4,) + s.shape, s.dtype) for s in shards], alias, _gather_sems(n),
                lambda i, o, s: _gather_steps(i[:n], o, s, ranges))
    if kind == "exchange":
        return list(items), _chip_exchange_shapes(items), {}, _chip_exchange_sems(n), _chip_exchange_steps
    pair_sems = [pltpu.SemaphoreType.DMA((n,)), pltpu.SemaphoreType.DMA((n,))]
    if kind == "swap":
        return (list(items), [jax.ShapeDtypeStruct(h.shape, h.dtype) for h in items], {}, pair_sems, _pair_swap_steps)
    return list(items), _pair_exchange_shapes(items), {}, pair_sems, _pair_exchange_steps


def _fused_call(body, fuse, *, name, grid, in_specs, out_specs, out_shape, scratch_shapes, operands):
    plans = [_fuse_plan(kind, list(items)) for kind, items in fuse if len(items)]
    n_in, n_out, n_s = len(in_specs), len(out_shape), len(scratch_shapes)
    comm = [a for p in plans for a in p[0]]
    shapes = [s for p in plans for s in p[1]]
    nc, no = len(comm), len(shapes)
    aliases, i_at, o_at = {}, n_in, n_out
    for p in plans:
        for i, o in p[2].items():
            aliases[i_at + i] = o_at + o
        i_at, o_at = i_at + len(p[0]), o_at + len(p[1])

    def wrapped(*refs):
        o0 = n_in + nc
        s0 = o0 + n_out + no
        i_at, o_at, sem_at, steps = n_in, o0 + n_out, s0 + n_s, []
        for p in plans:
            steps.append(p[4](refs[i_at:i_at + len(p[0])], refs[o_at:o_at + len(p[1])], refs[sem_at:sem_at + len(p[3])]))
            i_at, o_at, sem_at = i_at + len(p[0]), o_at + len(p[1]), sem_at + len(p[3])
        first, last = _grid_ends(grid)
        for st in steps:
            pl.when(first)(st[0])
        for st in steps:
            for mid in st[1:-1]:
                pl.when(last)(mid)
        body(*refs[:n_in], *refs[o0:o0 + n_out], *refs[s0:s0 + n_s])
        for st in steps:
            pl.when(last)(st[-1])

    res = pl.pallas_call(
        wrapped if plans else body, name=name, grid=grid, in_specs=list(in_specs) + [ANY] * nc,
        out_specs=list(out_specs) + [ANY] * no, out_shape=list(out_shape) + shapes,
        scratch_shapes=list(scratch_shapes) + [s for p in plans for s in p[3]], input_output_aliases=aliases,
        compiler_params=_params(("arbitrary",) * len(grid)))(*operands, *comm)
    outs, got, results = list(res[:n_out]), list(res[n_out:]), []
    for p in plans:
        results.append(got[:len(p[1])])
        got = got[len(p[1]):]
    return outs, results


def _chip_exchange_sems(n):
    return [pltpu.SemaphoreType.DMA((n, 3))] * 2


def _chip_exchange_shapes(parts):
    return [jax.ShapeDtypeStruct((3,) + p.shape[1:], p.dtype) for p in parts]


def _chip_exchange_steps(ins, outs, sems):
    send, recv = sems
    x, y, c = _mesh_pos()
    chips = [(1 - x, y), (x, 1 - y), (1 - x, 1 - y)]
    cps = [pltpu.make_async_remote_copy(
        src_ref=ins[a].at[2 * px + py], dst_ref=outs[a].at[k], send_sem=send.at[a, k], recv_sem=recv.at[a, k],
        device_id=(px, py, c), device_id_type=MESH) for a in range(len(ins)) for k, (px, py) in enumerate(chips)]

    def start():
        for cp in cps:
            cp.start()

    def finish():
        for cp in cps:
            cp.wait()

    return start, finish


def _chip_exchange(parts):
    n = len(parts)

    def body(*refs):
        start, finish = _chip_exchange_steps(refs[:n], refs[n:2 * n], refs[2 * n:])
        start()
        finish()

    return pl.pallas_call(
        body, name="grad_chip_exchange", out_shape=_chip_exchange_shapes(parts),
        in_specs=[ANY] * n, out_specs=[ANY] * n, scratch_shapes=_chip_exchange_sems(n),
    )(*parts)


def _pair_swap(halves):
    n = len(halves)

    def body(*refs):
        start, finish = _pair_swap_steps(refs[:n], refs[n:2 * n], refs[2 * n:])
        start()
        finish()

    return pl.pallas_call(
        body, name="grad_pair_swap",
        out_shape=[jax.ShapeDtypeStruct(h.shape, h.dtype) for h in halves],
        in_specs=[ANY] * n, out_specs=[ANY] * n,
        scratch_shapes=[pltpu.SemaphoreType.DMA((n,)), pltpu.SemaphoreType.DMA((n,))],
    )(*halves)


def _allgather_small(pack):
    m_per, ncol = pack.shape

    def body(x_ref, out_ref, send_sems, recv_sems, local_sem):
        x, y, c = _mesh_pos()
        me, sibling = (x, y, c), (x, y, 1 - c)
        chips = [(1 - x, y), (x, 1 - y), (1 - x, 1 - y)]

        def rows(px, py, pc):
            return out_ref.at[pl.ds(pl.multiple_of((4 * px + 2 * py + pc) * m_per, 8), m_per), :]

        def copy(k, block, to, src=None):
            return pltpu.make_async_remote_copy(
                src_ref=rows(*block) if src is None else src, dst_ref=rows(*block),
                send_sem=send_sems.at[k], recv_sem=recv_sems.at[k], device_id=to, device_id_type=MESH)

        mine = pltpu.make_async_copy(x_ref, rows(*me), local_sem)
        mine.start()
        first = [copy(0, me, sibling, src=x_ref)]
        first += [copy(1 + j, me, (*chip, c), src=x_ref) for j, chip in enumerate(chips)]
        for cp in first:
            cp.start()
        passed = [copy(4 + j, (*chip, c), sibling) for j, chip in enumerate(chips)]
        for j, chip in enumerate(chips):
            copy(1 + j, (*chip, c), me).wait_recv()
            passed[j].start()
        copy(0, sibling, me).wait_recv()
        for j, chip in enumerate(chips):
            copy(4 + j, (*chip, 1 - c), me).wait_recv()
        for cp in first + passed:
            cp.wait_send()
        mine.wait()

    return pl.pallas_call(
        body, name="allgather_small",
        out_shape=jax.ShapeDtypeStruct((8 * m_per, ncol), pack.dtype),
        in_specs=[pl.BlockSpec(memory_space=pltpu.VMEM)],
        out_specs=pl.BlockSpec(memory_space=pltpu.VMEM),
        scratch_shapes=[pltpu.SemaphoreType.DMA((7,)), pltpu.SemaphoreType.DMA((7,)), pltpu.SemaphoreType.DMA],
    )(pack)


def _add_pair(g, got, c_arr, splits=None):
    _, rows, _ = g.shape
    splits = splits or [rows]

    def body(c_ref, g_ref, r_ref, *o_refs):
        at = 0
        for o_ref, n in zip(o_refs, splits):
            o_ref[...] = (g_ref[:, at:at + n, :] + r_ref[:, at:at + n, :]).astype(BF16)
            at += n

    return pl.pallas_call(
        body, name="grad_add_pair", out_shape=[jax.ShapeDtypeStruct((4, n, HALF), BF16) for n in splits],
        grid_spec=pltpu.PrefetchScalarGridSpec(
            num_scalar_prefetch=1, grid=(4,),
            in_specs=[pl.BlockSpec((1, rows, HALF), lambda j, c: (j, 0, c[0])),
                      pl.BlockSpec((1, rows, HALF), lambda j, c: (j, 0, 0))],
            out_specs=[pl.BlockSpec((1, n, HALF), lambda j, c: (j, 0, 0)) for n in splits]),
        compiler_params=_params(("arbitrary",)),
    )(c_arr, g, got)


def _add_chips(g, pair_got, chip_gots, pos_arr):
    _, rows, _ = g.shape

    def body(pos_ref, g_ref, p_ref, *refs):
        o_ref, at = refs[-1], 0
        for r_ref in refs[:-1]:
            n = r_ref.shape[1]
            own = g_ref[0, at:at + n, :] + p_ref[0, at:at + n, :]
            o_ref[at:at + n, :] = ((own + r_ref[0].astype(F32)) + r_ref[1].astype(F32)) + r_ref[2].astype(F32)
            at += n

    return pl.pallas_call(
        body, name="grad_add_chips", out_shape=jax.ShapeDtypeStruct((rows, HALF), F32),
        grid_spec=pltpu.PrefetchScalarGridSpec(
            num_scalar_prefetch=1, grid=(1,),
            in_specs=[pl.BlockSpec((1, rows, HALF), lambda i, p: (p[0], 0, p[1])),
                      pl.BlockSpec((1, rows, HALF), lambda i, p: (p[0], 0, 0))]
            + [pl.BlockSpec(t.shape, lambda i, p: (0, 0, 0)) for t in chip_gots],
            out_specs=pl.BlockSpec((rows, HALF), lambda i, p: (0, 0))),
        compiler_params=_params(("arbitrary",)),
    )(pos_arr, g, pair_got, *chip_gots)


def _add_chips_many(items, fuse=NO_FUSE):
    n = len(items)
    rows = [it[0].shape[1] for it in items]
    n_got = [len(it[2]) for it in items]

    def body(*refs):
        g_refs, p_refs = refs[:n], refs[n:2 * n]
        got_refs = refs[2 * n:2 * n + sum(n_got)]
        o_refs = refs[2 * n + sum(n_got):3 * n + sum(n_got)]
        scr = refs[3 * n + sum(n_got):]
        x, y, c = _mesh_pos()
        chip = 2 * x + y
        mine = pl.ds(pl.multiple_of(c * HALF, 128), HALF)
        copies = []
        for a in range(n):
            copies.append((pltpu.make_async_copy(g_refs[a].at[chip, :, mine], scr[2 * a], scr[2 * n].at[2 * a]),
                           pltpu.make_async_copy(p_refs[a].at[chip], scr[2 * a + 1], scr[2 * n].at[2 * a + 1])))
        for cg, cp in copies:
            cg.start()
            cp.start()
        at_ref = 0
        for a in range(n):
            copies[a][0].wait()
            copies[a][1].wait()
            at = 0
            for r_ref in got_refs[at_ref:at_ref + n_got[a]]:
                k = r_ref.shape[1]
                own = scr[2 * a][at:at + k, :] + scr[2 * a + 1][at:at + k, :]
                o_refs[a][at:at + k, :] = ((own + r_ref[0].astype(F32)) + r_ref[1].astype(F32)) + r_ref[2].astype(F32)
                at += k
            at_ref += n_got[a]

    gots = [t for it in items for t in it[2]]
    outs, got = _fused_call(
        body, fuse, name="grad_add_chips_many", grid=(1,),
        in_specs=[ANY] * (2 * n) + [pl.BlockSpec(t.shape, lambda i: (0, 0, 0)) for t in gots],
        out_specs=[pl.BlockSpec((r, HALF), lambda i: (0, 0)) for r in rows],
        out_shape=[jax.ShapeDtypeStruct((r, HALF), F32) for r in rows],
        scratch_shapes=[pltpu.VMEM((r, HALF), F32) for r in rows for _ in range(2)] + [pltpu.SemaphoreType.DMA((2 * n,))],
        operands=(*[it[0] for it in items], *[it[1] for it in items], *gots))
    return outs, got


def _pack_small(acc2, acc1, lvec, cacc, gvec, dcw, dgup, dmeta):
    bsz = lvec.shape[0]

    def body(a2_ref, a1_ref, lv_ref, ca_ref, gv_ref, cw_ref, gu_ref, dm_ref, o_ref):
        def put(name, val):
            r0, nr, nl = SMALL_AT[name]
            o_ref[r0:r0 + nr, 0:nl] = val

        over_b = lambda f: functools.reduce(lambda a, b: a + b, [f(b) for b in range(bsz)])
        o_ref[...] = jnp.zeros_like(o_ref)
        put("loss", a2_ref[0:1, :])
        put("ln2_g", a2_ref[1:2, :])
        put("ln2_b", a2_ref[2:3, :])
        put("ln1_g", a1_ref[0:1, :])
        put("ln1_b", a1_ref[1:2, :])
        put("ln_in_g", over_b(lambda b: lv_ref[b, 0:1, :]))
        put("ln_in_b", over_b(lambda b: lv_ref[b, 1:2, :]))
        put("conv_b", ca_ref[0:1, :])
        put("conv_ln_g", ca_ref[1:2, :])
        put("conv_ln_b", ca_ref[2:3, :])
        put("gate_bias", over_b(lambda b: gv_ref[b, 0:1, :]))
        put("gla_norm_g", over_b(lambda b: gv_ref[b, 1:2, 0:128] + gv_ref[b, 1:2, 128:256]))
        put("conv_w", over_b(lambda b: cw_ref[b]))
        put("gate_up", over_b(lambda b: gu_ref[b, 0:GLA_RANK, :]))
        put("meta_tokens", over_b(lambda b: dm_ref[b]))

    return pl.pallas_call(
        body, name="pack_small", out_shape=jax.ShapeDtypeStruct((SMALL_ROWS, D_MODEL), F32),
    )(acc2, acc1, lvec, cacc, gvec, dcw, dgup, dmeta)


def _sum_small(gathered):
    def body(g_ref, o_ref, loss_ref):
        acc = g_ref[0:SMALL_ROWS, :]
        for d in range(1, 8):
            acc = acc + g_ref[d * SMALL_ROWS:(d + 1) * SMALL_ROWS, :]
        o_ref[...] = acc
        loss_ref[...] = jnp.sum(acc[0:1, :], axis=1, keepdims=True)

    return pl.pallas_call(
        body, name="sum_small",
        out_shape=[jax.ShapeDtypeStruct((SMALL_ROWS, D_MODEL), F32), jax.ShapeDtypeStruct((1, 1), F32)],
    )(gathered)


def _adamw_math(w, g, m, v):
    c1 = 1.0 - ADAM_B1 ** ADAM_STEP
    c2 = 1.0 - ADAM_B2 ** ADAM_STEP
    mn = ADAM_B1 * m + (1.0 - ADAM_B1) * g
    vn = ADAM_B2 * v + (1.0 - ADAM_B2) * (g * g)
    return -ADAM_LR * ((mn / c1) / (jnp.sqrt(vn / c2) + ADAM_EPS) + ADAM_WD * w), mn, vn


def _adamw_small(red, sharded_grads, params):
    names = list(params)
    sharded = [k for k in names if k in sharded_grads]
    n, ns = len(names), len(sharded)

    def body(*refs):
        red_ref, sg_refs, p_refs, o_refs = refs[0], refs[1:1 + ns], refs[1 + ns:1 + ns + 3 * n], refs[1 + ns + 3 * n:]
        for i, k in enumerate(names):
            w_ref, m_ref, v_ref = p_refs[3 * i:3 * i + 3]
            if k in sharded:
                g = sg_refs[sharded.index(k)][...]
            else:
                r0, nr, nl = SMALL_AT[k]
                g = red_ref[r0:r0 + nr, 0:nl]
            dl, mn, vn = _adamw_math(w_ref[...], g, m_ref[...], v_ref[...])
            for o_ref, val in zip(o_refs[4 * i:4 * i + 4], (g, dl, mn, vn)):
                o_ref[...] = val

    flat = [a for k in names for a in params[k]]
    res = pl.pallas_call(
        body, name="adamw_small",
        out_shape=[jax.ShapeDtypeStruct(params[k][0].shape, F32) for k in names for _ in range(4)],
    )(red, *[sharded_grads[k] for k in sharded], *flat)
    return {k: tuple(res[4 * i:4 * i + 4]) for i, k in enumerate(names)}


def _adamw_halves(w, own, sib, m, v, c_arr):
    rows, cols = w.shape
    tr = 256 if rows % 256 == 0 else rows

    def body(c_ref, w_ref, own_ref, sib_ref, m_ref, v_ref, g_ref, d_ref, mo_ref, vo_ref):
        first = c_ref[0] == 0
        own, sib = own_ref[...], sib_ref[...]
        g = jnp.concatenate([jnp.where(first, own, sib), jnp.where(first, sib, own)], axis=1)
        g_ref[...] = g
        d_ref[...], mo_ref[...], vo_ref[...] = _adamw_math(w_ref[...], g, m_ref[...], v_ref[...])

    full = pl.BlockSpec((tr, cols), lambda i, c: (i, 0))
    half = pl.BlockSpec((tr, HALF), lambda i, c: (i, 0))
    return pl.pallas_call(
        body, name="adamw_halves", out_shape=[jax.ShapeDtypeStruct(w.shape, F32)] * 4,
        grid_spec=pltpu.PrefetchScalarGridSpec(
            num_scalar_prefetch=1, grid=(rows // tr,), in_specs=[full, half, half, full, full], out_specs=[full] * 4),
        compiler_params=_params(("parallel",)),
    )(c_arr, w, own, sib, m, v)


def _ln_in_fwd(x, meta, g, b, fuse=NO_FUSE):
    bsz, s, d = x.shape
    tp = s + HEAD
    nh = 2
    sh = s // nh
    rc = min(256, sh)

    def body(x_ref, meta_ref, g_ref, b_ref, s0_ref, s0b_ref):
        h = pl.program_id(1)
        gg, bb = g_ref[...], b_ref[...]

        @pl.when(h == 0)
        def _():
            s0_ref[0, 0:PAD, :] = jnp.zeros((PAD, d), F32)
            s0b_ref[0, 0:PAD, :] = jnp.zeros((PAD, d), BF16)
            mh, _ = _ln_stats(meta_ref[...])
            mv = mh * gg + bb
            s0_ref[0, PAD:HEAD, :] = mv
            s0b_ref[0, PAD:HEAD, :] = mv.astype(BF16)

        def step(i, carry):
            src = pl.ds(pl.multiple_of(i * rc, rc), rc)
            dst = pl.ds(pl.multiple_of(HEAD + h * sh + i * rc, 64), rc)
            xh, _ = _ln_stats(x_ref[0, src, :])
            val = xh * gg + bb
            s0_ref[0, dst, :] = val
            s0b_ref[0, dst, :] = val.astype(BF16)
            return carry

        lax.fori_loop(0, sh // rc, step, 0)

    full = lambda bi, hi: (bi, 0, 0)
    (s0, s0b), got = _fused_call(
        body, fuse, name="ln_in_fwd",
        out_shape=[jax.ShapeDtypeStruct((bsz, tp, d), F32), jax.ShapeDtypeStruct((bsz, tp, d), BF16)],
        grid=(bsz, nh),
        in_specs=[pl.BlockSpec((1, sh, d), lambda bi, hi: (bi, hi, 0)),
                  pl.BlockSpec((N_META, d), lambda bi, hi: (0, 0)),
                  pl.BlockSpec((1, d), lambda bi, hi: (0, 0)),
                  pl.BlockSpec((1, d), lambda bi, hi: (0, 0))],
        out_specs=[pl.BlockSpec((1, tp, d), full)] * 2, scratch_shapes=[], operands=(x, meta, g, b))
    return s0, s0b, got


def _in_proj(s0b, w_int, fuse=NO_FUSE):
    r, d = s0b.shape
    tm = _row_tile(r)

    def body(a_ref, w_ref, o_ref):
        o_ref[...] = _nt(a_ref[...], w_ref[...])

    (u,), got = _fused_call(
        body, fuse, name="in_proj", out_shape=[jax.ShapeDtypeStruct((r, D_IN_PAD), F32)],
        grid=(r // tm,),
        in_specs=[pl.BlockSpec((tm, d), lambda i: (i, 0)), pl.BlockSpec((D_IN_PAD, d), lambda i: (0, 0))],
        out_specs=[pl.BlockSpec((tm, D_IN_PAD), lambda i: (i, 0))], scratch_shapes=[], operands=(s0b, w_int))
    return u, got


def _conv_fwd(u, conv_w, conv_b, fuse=NO_FUSE):
    bsz, tp, _ = u.shape
    nchunk = tp // CHUNK
    win = CHUNK + 32
    nct = D_CONV // 128

    def body(cv_ref, cg_ref, w_ref, cb_ref, hc_ref, h_scr, win_scr):
        h_scr[0:32, :] = jnp.zeros((32, 128), F32)
        h_scr[32:32 + tp, :] = cv_ref[0] * _sigmoid(cg_ref[0])
        cb = cb_ref[...]

        def step(n, carry):
            r0 = pl.multiple_of(n * CHUNK, CHUNK)
            win_scr[...] = h_scr[pl.ds(r0, win), :]
            acc = jnp.zeros((CHUNK, 128), F32)
            for j in range(CONV_WIDTH):
                acc = acc + w_ref[j:j + 1, :] * win_scr[2 + j:2 + j + CHUNK, :]
            hc_ref[0, pl.ds(r0, CHUNK), :] = acc + cb
            return carry

        lax.fori_loop(0, nchunk, step, 0)

    (hc,), got = _fused_call(
        body, fuse, name="conv_fwd", out_shape=[jax.ShapeDtypeStruct((bsz, tp, D_CONV), F32)],
        grid=(bsz, nct),
        in_specs=[pl.BlockSpec((1, tp, 128), lambda bi, t: (bi, 0, C_VAL // 128 + t)),
                  pl.BlockSpec((1, tp, 128), lambda bi, t: (bi, 0, C_GATE // 128 + t)),
                  pl.BlockSpec((32, 128), lambda bi, t: (0, t)),
                  pl.BlockSpec((1, 128), lambda bi, t: (0, t))],
        out_specs=[pl.BlockSpec((1, tp, 128), lambda bi, t: (bi, 0, t))],
        scratch_shapes=[pltpu.VMEM((tp + 32, 128), F32), pltpu.VMEM((win, 128), F32)],
        operands=(u, u, conv_w, conv_b))
    return hc, got


def _gla_group(nchunk):
    return 11 if nchunk % 11 == 0 else nchunk


def _bdot(a, b, ca, cb, precision=None):
    return lax.dot_general(a, b, (((ca,), (cb,)), ((0,), (0,))), preferred_element_type=F32, precision=precision)


def _bnn(a, b, **kw):
    return _bdot(a, b, 2, 1, **kw)


def _bnt(a, b, **kw):
    return _bdot(a, b, 2, 2, **kw)


def _gla_consts(nb):
    row = lax.broadcasted_iota(jnp.int32, (nb, CHUNK, CHUNK), 1)
    col = lax.broadcasted_iota(jnp.int32, (nb, CHUNK, CHUNK), 2)
    lane = lax.broadcasted_iota(jnp.int32, (1, 1, 128), 2)
    return row >= col, row <= col, [lane < GLA_DK, lane >= GLA_DK]


def _gla_group_terms(g, nb, q_ref, k_ref, gd_ref, gup_ref, gb_ref, tril):
    m = nb * CHUNK
    rows = pl.ds(pl.multiple_of(g * m, CHUNK), m)
    z = _nn(gd_ref[0, rows, :].astype(BF16), gup_ref[...]) + gb_ref[...]
    valid = g * m + lax.broadcasted_iota(jnp.int32, (m, 1), 0) >= PAD
    lg = jnp.where(valid, _log_sigmoid(z) * (1.0 / GLA_TAU), 0.0)
    bcum = _bnn(tril.astype(F32), lg.reshape(nb, CHUNK, 128), precision=lax.Precision.HIGHEST)
    blast = bcum[:, CHUNK - 1:CHUNK, :]
    eb = jnp.exp(bcum)
    enb = jnp.exp(-bcum)
    erest = jnp.exp(blast - bcum)
    q = (q_ref[0, rows, :] * Q_SCALE).reshape(nb, CHUNK, 128)
    k = k_ref[0, rows, :].reshape(nb, CHUNK, 128)
    return rows, valid, z, eb, enb, erest, jnp.exp(blast), q * eb, k * enb, k * erest


def _grid_ends(grid):
    ids = [pl.program_id(i) for i in range(len(grid))]
    first = functools.reduce(jnp.logical_and, [i == 0 for i in ids])
    last = functools.reduce(jnp.logical_and, [i == g - 1 for i, g in zip(ids, grid)])
    return first, last


def _gla_fwd(u, gup, gbias, gnorm, fuse=NO_FUSE):
    bsz, tp, _ = u.shape
    nchunk = tp // CHUNK
    nb = _gla_group(nchunk)

    def body(q_ref, k_ref, v_ref, r_ref, gd_ref, gup_ref, gb_ref, gn_ref, out_ref, o_ref, st_ref, s_scr):
        tril, _, hmask = _gla_consts(nb)
        s_scr[...] = jnp.zeros_like(s_scr)
        gn = gn_ref[...]

        def group(g, carry):
            rows, _, _, _, _, _, dec, qe, ke, kd = _gla_group_terms(g, nb, q_ref, k_ref, gd_ref, gup_ref, gb_ref, tril)
            keb, kdb = ke.astype(BF16), kd.astype(BF16)
            for h in range(2):
                cols = slice(h * GLA_DV, (h + 1) * GLA_DV)
                qh = jnp.where(hmask[h], qe, 0.0).astype(BF16)
                vh = v_ref[0, rows, cols].astype(BF16).reshape(nb, CHUNK, GLA_DV)
                a = jnp.where(tril, _bnt(qh, keb), 0.0).astype(BF16)
                st = s_scr[h]
                sts = []
                for n in range(nb):
                    st_ref[0, h, g * nb + n] = st
                    sts.append(st.astype(BF16))
                    st = dec[n] * st + _tn(vh[n], kdb[n])
                s_scr[h] = st
                o = (_bnn(a, vh) + _bnt(qh, jnp.stack(sts))).reshape(nb * CHUNK, GLA_DV)
                o_ref[0, rows, cols] = o
                rms = lax.rsqrt(jnp.mean(o * o, axis=-1, keepdims=True) + LN_EPS)
                rh = r_ref[0, rows, cols]
                out_ref[0, rows, cols] = (o * rms * gn * (rh * _sigmoid(rh))).astype(BF16)
            return carry

        lax.fori_loop(0, nchunk // nb, group, 0)

    res, got = _fused_call(
        body, fuse, name="gla_fwd",
        out_shape=[jax.ShapeDtypeStruct((bsz, tp, 512), BF16), jax.ShapeDtypeStruct((bsz, tp, 512), F32),
                   jax.ShapeDtypeStruct((bsz, GLA_HEADS, nchunk, GLA_DV, 128), F32)],
        grid=(bsz, 2),
        in_specs=[pl.BlockSpec((1, tp, 128), lambda bi, p: (bi, 0, C_Q // 128 + p)),
                  pl.BlockSpec((1, tp, 128), lambda bi, p: (bi, 0, C_K // 128 + p)),
                  pl.BlockSpec((1, tp, 256), lambda bi, p: (bi, 0, C_V // 256 + p)),
                  pl.BlockSpec((1, tp, 256), lambda bi, p: (bi, 0, C_R // 256 + p)),
                  pl.BlockSpec((1, tp, 128), lambda bi, p: (bi, 0, C_GD // 128)),
                  pl.BlockSpec((128, 128), lambda bi, p: (0, p)),
                  pl.BlockSpec((1, 128), lambda bi, p: (0, p)),
                  pl.BlockSpec((1, 128), lambda bi, p: (0, 0))],
        out_specs=[pl.BlockSpec((1, tp, 256), lambda bi, p: (bi, 0, p)),
                   pl.BlockSpec((1, tp, 256), lambda bi, p: (bi, 0, p)),
                   pl.BlockSpec((1, 2, nchunk, GLA_DV, 128), lambda bi, p: (bi, p, 0, 0, 0))],
        scratch_shapes=[pltpu.VMEM((2, GLA_DV, 128), F32)],
        operands=(u, u, u, u, u, gup, gbias, gnorm))
    return res[0], res[1], res[2], got


def _out_proj_ln1(hc, gla_out, w_out, s0, cg, cb, g1, b1, fuse=NO_FUSE):
    r, d = s0.shape
    tm = _row_tile(r)

    def body(hc_ref, a_ref, w_ref, s0_ref, cg_ref, cb_ref, g_ref, b_ref, co_ref, xh_ref, rstd_ref, s1b_ref):
        for rs in _sub_rows(tm):
            xc, _ = _ln_stats(hc_ref[rs, :])
            nv = xc * cg_ref[...] + cb_ref[...]
            co = (nv * _sigmoid(nv)).astype(BF16)
            co_ref[rs, :] = co
            mix = _nn(co, w_ref[0:D_CONV, :]) + _nn(a_ref[rs, :], w_ref[D_CONV:, :])
            xh, rstd = _ln_stats(ALPHA * s0_ref[rs, :] + mix)
            xh_ref[rs, :] = xh
            rstd_ref[rs, :] = rstd
            s1b_ref[rs, :] = (xh * g_ref[...] + b_ref[...]).astype(BF16)

    row = lambda n: pl.BlockSpec((tm, n), lambda i: (i, 0))
    vec = lambda n: pl.BlockSpec((1, n), lambda i: (0, 0))
    res, got = _fused_call(
        body, fuse, name="out_proj_ln1",
        out_shape=[jax.ShapeDtypeStruct((r, D_CONV), BF16), jax.ShapeDtypeStruct((r, d), F32),
                   jax.ShapeDtypeStruct((r, 1), F32), jax.ShapeDtypeStruct((r, d), BF16)],
        grid=(r // tm,),
        in_specs=[row(D_CONV), row(512), pl.BlockSpec((d, d), lambda i: (0, 0)), row(d),
                  vec(D_CONV), vec(D_CONV), vec(d), vec(d)],
        out_specs=[row(D_CONV), row(d), row(1), row(d)], scratch_shapes=[],
        operands=(hc, gla_out, w_out, s0, cg, cb, g1, b1))
    return res[0], res[1], res[2], res[3], got


def _ffn1(s1b, w1, fuse=NO_FUSE):
    r, d = s1b.shape
    tm = _row_tile(r)
    ns, _, wn = w1.shape

    def body(a_ref, w_ref, o_ref):
        a = a_ref[...]
        for j in range(ns):
            o_ref[:, j * wn:(j + 1) * wn] = jnp.maximum(_nn(a, w_ref[j]), 0.0).astype(BF16)

    (ra,), got = _fused_call(
        body, fuse, name="ffn1", out_shape=[jax.ShapeDtypeStruct((r, D_FF), BF16)], grid=(r // tm,),
        in_specs=[pl.BlockSpec((tm, d), lambda i: (i, 0)), pl.BlockSpec(w1.shape, lambda i: (0, 0, 0))],
        out_specs=[pl.BlockSpec((tm, D_FF), lambda i: (i, 0))], scratch_shapes=[], operands=(s1b, w1))
    return ra, got


def _ffn2_ln2_loss(ra, w2, xhat1, g1, b1, g2, b2, tgt, tp):
    r, d = xhat1.shape
    tm = _row_tile(tp)
    per = tp // tm

    def body(ra_ref, w_ref, xh1_ref, g1_ref, b1_ref, g2_ref, b2_ref, tgt_ref, dt_ref, dtb_ref, acc_ref, t_ref, sem):
        i = pl.program_id(0)
        b, j = i // per, i % per

        @pl.when(i == 0)
        def _():
            acc_ref[...] = jnp.zeros_like(acc_ref)

        head_copy = pltpu.make_async_copy(tgt_ref.at[b, pl.ds(0, tm - HEAD), :], t_ref.at[pl.ds(HEAD, tm - HEAD), :], sem)
        body_copy = pltpu.make_async_copy(
            tgt_ref.at[b, pl.ds(pl.multiple_of(jnp.maximum(j * tm - HEAD, 0), 64), tm), :], t_ref, sem)

        @pl.when(j == 0)
        def _():
            t_ref[0:HEAD, :] = jnp.zeros((HEAD, d), F32)
            head_copy.start()

        pl.when(j > 0)(body_copy.start)

        sums = [jnp.zeros((1, d), F32)] * 3
        for rs in _sub_rows(tm):
            rb = ra_ref[rs, :]
            f = _nn(rb * rb, w_ref[...])
            if rs.start == 0:
                pl.when(j == 0)(head_copy.wait)
                pl.when(j > 0)(body_copy.wait)
            s1 = xh1_ref[rs, :] * g1_ref[...] + b1_ref[...]
            xh2, rstd2 = _ln_stats(ALPHA * s1 + f)
            y = xh2 * g2_ref[...] + b2_ref[...]
            rowid = (i % per) * tm + rs.start + lax.broadcasted_iota(jnp.int32, (rs.stop - rs.start, 1), 0)
            e = jnp.where(rowid >= HEAD, y - t_ref[rs, :], 0.0)
            dy = e * (1.0 / d)
            dt2 = _ln_bwd(dy * g2_ref[...], xh2, rstd2)
            dt_ref[rs, :] = dt2
            dtb_ref[rs, :] = dt2.astype(BF16)
            sums = [sums[0] + (0.5 / d) * jnp.sum(e * e, axis=0, keepdims=True),
                    sums[1] + jnp.sum(dy * xh2, axis=0, keepdims=True), sums[2] + jnp.sum(dy, axis=0, keepdims=True)]
        for k in range(3):
            acc_ref[k:k + 1, :] += sums[k]

    row = lambda n: pl.BlockSpec((tm, n), lambda i: (i, 0))
    vec = pl.BlockSpec((1, d), lambda i: (0, 0))
    return pl.pallas_call(
        body, name="ffn2_ln2_loss",
        out_shape=[jax.ShapeDtypeStruct((r, d), F32), jax.ShapeDtypeStruct((r, d), BF16),
                   jax.ShapeDtypeStruct((8, d), F32)],
        grid=(r // tm,),
        in_specs=[row(D_FF), pl.BlockSpec((D_FF, d), lambda i: (0, 0)), row(d), vec, vec, vec, vec, ANY],
        out_specs=[row(d), row(d), pl.BlockSpec((8, d), lambda i: (0, 0))],
        scratch_shapes=[pltpu.VMEM((tm, d), F32), pltpu.SemaphoreType.DMA],
        compiler_params=_params(("arbitrary",)),
    )(ra, w2, xhat1, g1, b1, g2, b2, tgt)


def _ffn_bwd_da(dt2b, w2, ra):
    r, d = dt2b.shape
    tm = _row_tile(r)

    def body(g_ref, w_ref, ra_ref, o_ref):
        o_ref[...] = (_nt(g_ref[...], w_ref[...]) * (2.0 * ra_ref[...].astype(F32))).astype(BF16)

    return pl.pallas_call(
        body, name="ffn_bwd_da", out_shape=jax.ShapeDtypeStruct((r, D_FF), BF16),
        grid=(r // tm,),
        in_specs=[pl.BlockSpec((tm, d), lambda i: (i, 0)), pl.BlockSpec((D_FF, d), lambda i: (0, 0)),
                  pl.BlockSpec((tm, D_FF), lambda i: (i, 0))],
        out_specs=pl.BlockSpec((tm, D_FF), lambda i: (i, 0)),
        compiler_params=_params(("parallel",)),
    )(dt2b, w2, ra)


def _ffn_bwd_ln1(da, w1, dt2, xhat1, rstd1, g1):
    r, d = dt2.shape
    tm = _row_tile(r)

    def body(da_ref, w_ref, dt2_ref, xh_ref, rstd_ref, g_ref, dt_ref, dtb_ref, acc_ref):
        @pl.when(pl.program_id(0) == 0)
        def _():
            acc_ref[...] = jnp.zeros_like(acc_ref)

        sums = [jnp.zeros((1, d), F32)] * 2
        for rs in _sub_rows(tm):
            ds1 = ALPHA * dt2_ref[rs, :]
            for j in range(w1.shape[0]):
                ds1 = ds1 + _nt(da_ref[rs, j * w1.shape[2]:(j + 1) * w1.shape[2]], w_ref[j])
            xh = xh_ref[rs, :]
            dt1 = _ln_bwd(ds1 * g_ref[...], xh, rstd_ref[rs, :])
            dt_ref[rs, :] = dt1
            dtb_ref[rs, :] = dt1.astype(BF16)
            sums = [sums[0] + jnp.sum(ds1 * xh, axis=0, keepdims=True), sums[1] + jnp.sum(ds1, axis=0, keepdims=True)]
        for k in range(2):
            acc_ref[k:k + 1, :] += sums[k]

    row = lambda n: pl.BlockSpec((tm, n), lambda i: (i, 0))
    return pl.pallas_call(
        body, name="ffn_bwd_ln1",
        out_shape=[jax.ShapeDtypeStruct((r, d), F32), jax.ShapeDtypeStruct((r, d), BF16),
                   jax.ShapeDtypeStruct((8, d), F32)],
        grid=(r // tm,),
        in_specs=[row(D_FF), pl.BlockSpec(w1.shape, lambda i: (0, 0, 0)), row(d), row(d), row(1),
                  pl.BlockSpec((1, d), lambda i: (0, 0))],
        out_specs=[row(d), row(d), pl.BlockSpec((8, d), lambda i: (0, 0))],
        compiler_params=_params(("arbitrary",)),
    )(da, w1, dt2, xhat1, rstd1, g1)


def _matmul_tn(lhs, rhs, bm, square_lhs=False, name="matmul_tn", fuse=NO_FUSE):
    r, m = lhs.shape
    n = rhs.shape[1]
    tk = _reduce_tile(r, True)

    def body(a_ref, b_ref, o_ref):
        @pl.when(pl.program_id(1) == 0)
        def _():
            o_ref[...] = jnp.zeros_like(o_ref)

        a = a_ref[...]
        if square_lhs:
            a = a * a
        o_ref[...] += _tn(a, b_ref[...])

    (out,), got = _fused_call(
        body, fuse, name=name, out_shape=[jax.ShapeDtypeStruct((m, n), F32)], grid=(m // bm, r // tk),
        in_specs=[pl.BlockSpec((tk, bm), lambda i, k: (k, i)), pl.BlockSpec((tk, n), lambda i, k: (k, 0))],
        out_specs=[pl.BlockSpec((bm, n), lambda i, k: (i, 0))], scratch_shapes=[], operands=(lhs, rhs))
    return out, got


def _grad_w_ff1(s1b, da, fuse=NO_FUSE):
    r, d = s1b.shape
    wn = da.shape[1] // 4
    tk = _reduce_tile(r, True)

    def body(a_ref, b_ref, o_ref):
        @pl.when(pl.program_id(1) == 0)
        def _():
            o_ref[...] = jnp.zeros_like(o_ref)

        o_ref[0] += _tn(a_ref[...], b_ref[...])

    (out,), got = _fused_call(
        body, fuse, name="grad_w_ff1", out_shape=[jax.ShapeDtypeStruct((4, d, wn), F32)], grid=(4, r // tk),
        in_specs=[pl.BlockSpec((tk, d), lambda j, k: (k, 0)), pl.BlockSpec((tk, wn), lambda j, k: (k, j))],
        out_specs=[pl.BlockSpec((1, d, wn), lambda j, k: (j, 0, 0))], scratch_shapes=[], operands=(s1b, da))
    return out, got


def _grad_w_out(conv_of, gla_of, dt1b):
    r, n = dt1b.shape
    tk = _reduce_tile(r, True)

    def body(a_ref, b_ref, g_ref, o_ref):
        @pl.when(pl.program_id(1) == 0)
        def _():
            o_ref[...] = jnp.zeros_like(o_ref)

        @pl.when(pl.program_id(0) == 0)
        def _():
            o_ref[...] += _tn(a_ref[...], g_ref[...])

        @pl.when(pl.program_id(0) == 1)
        def _():
            o_ref[...] += _tn(b_ref[...], g_ref[...])

    lhs = pl.BlockSpec((tk, 512), lambda i, k: (k, 0))
    return pl.pallas_call(
        body, name="grad_w_out", out_shape=jax.ShapeDtypeStruct((2 * 512, n), F32), grid=(2, r // tk),
        in_specs=[lhs, lhs, pl.BlockSpec((tk, n), lambda i, k: (k, 0))],
        out_specs=pl.BlockSpec((512, n), lambda i, k: (i, 0)),
        compiler_params=_params(("parallel", "arbitrary")),
    )(conv_of, gla_of, dt1b)


def _out_proj_bwd(dt1b, w_out, hc, cg, cb, fuse=NO_FUSE):
    r, d = dt1b.shape
    tm = _row_tile(r)

    def body(g_ref, w_ref, hc_ref, cg_ref, cb_ref, dhc_ref, dgla_ref, acc_ref):
        @pl.when(pl.program_id(0) == 0)
        def _():
            acc_ref[...] = jnp.zeros_like(acc_ref)

        gg = cg_ref[...]
        sums = [jnp.zeros((1, D_CONV), F32)] * 3
        for rs in _sub_rows(tm):
            dmix = _nt(g_ref[rs, :], w_ref[...])
            dgla_ref[rs, :] = dmix[:, D_CONV:]
            xh, rstd = _ln_stats(hc_ref[rs, :])
            nv = xh * gg + cb_ref[...]
            sig = _sigmoid(nv)
            dn = dmix[:, :D_CONV] * (sig * (1.0 + nv * (1.0 - sig)))
            dhc = _ln_bwd(dn * gg, xh, rstd)
            dhc_ref[rs, :] = dhc
            sums = [sums[0] + jnp.sum(dhc, axis=0, keepdims=True), sums[1] + jnp.sum(dn * xh, axis=0, keepdims=True),
                    sums[2] + jnp.sum(dn, axis=0, keepdims=True)]
        for k in range(3):
            acc_ref[k:k + 1, :] += sums[k]

    row = lambda n: pl.BlockSpec((tm, n), lambda i: (i, 0))
    vec = pl.BlockSpec((1, D_CONV), lambda i: (0, 0))
    res, got = _fused_call(
        body, fuse, name="out_proj_bwd",
        out_shape=[jax.ShapeDtypeStruct((r, D_CONV), F32), jax.ShapeDtypeStruct((r, 512), F32),
                   jax.ShapeDtypeStruct((8, D_CONV), F32)],
        grid=(r // tm,),
        in_specs=[row(d), pl.BlockSpec((d, d), lambda i: (0, 0)), row(D_CONV), vec, vec],
        out_specs=[row(D_CONV), row(512), pl.BlockSpec((8, D_CONV), lambda i: (0, 0))], scratch_shapes=[],
        operands=(dt1b, w_out, hc, cg, cb))
    return res[0], res[1], res[2], got


def _conv_bwd(dhc, u, conv_w, fuse=NO_FUSE):
    bsz, tp, _ = u.shape
    nchunk = tp // CHUNK
    win = CHUNK + 32
    nct = D_CONV // 128

    def body(dhc_ref, cv_ref, cg_ref, w_ref, dv_ref, dg_ref, dw_ref, h_scr, dhc_scr, hwin, dwin, dw_scr):
        h_scr[0:32, :] = jnp.zeros((32, 128), F32)
        h_scr[32:32 + tp, :] = cv_ref[0] * _sigmoid(cg_ref[0])
        dhc_scr[0:tp, :] = dhc_ref[0]
        dhc_scr[tp:tp + 32, :] = jnp.zeros((32, 128), F32)
        dw_scr[...] = jnp.zeros_like(dw_scr)

        def step(n, carry):
            r0 = pl.multiple_of(n * CHUNK, CHUNK)
            rows = pl.ds(r0, CHUNK)
            hwin[...] = h_scr[pl.ds(r0, win), :]
            dwin[...] = dhc_scr[pl.ds(r0, win), :]
            dcur = dwin[0:CHUNK, :]
            acc = jnp.zeros((CHUNK, 128), F32)
            for j in range(CONV_WIDTH):
                acc = acc + w_ref[j:j + 1, :] * dwin[30 - j:30 - j + CHUNK, :]
                prod = dcur * hwin[2 + j:2 + j + CHUNK, :]
                dw_scr[j * 8:(j + 1) * 8, :] += jnp.sum(prod.reshape(CHUNK // 8, 8, 128), axis=0)
            cg = cg_ref[0, rows, :]
            sig = _sigmoid(cg)
            rowid = n * CHUNK + lax.broadcasted_iota(jnp.int32, (CHUNK, 1), 0)
            dh = jnp.where(rowid >= PAD, acc, 0.0)
            dv_ref[0, rows, :] = (dh * sig).astype(BF16)
            dg_ref[0, rows, :] = (dh * cv_ref[0, rows, :] * sig * (1.0 - sig)).astype(BF16)
            return carry

        lax.fori_loop(0, nchunk, step, 0)
        dw_ref[0] = jnp.zeros((32, 128), F32)
        for j in range(CONV_WIDTH):
            dw_ref[0, j:j + 1, :] = jnp.sum(dw_scr[j * 8:(j + 1) * 8, :], axis=0, keepdims=True)

    blk = lambda off: pl.BlockSpec((1, tp, 128), lambda bi, t: (bi, 0, off // 128 + t))
    res, got = _fused_call(
        body, fuse, name="conv_bwd",
        out_shape=[jax.ShapeDtypeStruct((bsz, tp, D_CONV), BF16), jax.ShapeDtypeStruct((bsz, tp, D_CONV), BF16),
                   jax.ShapeDtypeStruct((bsz, 32, D_CONV), F32)],
        grid=(bsz, nct),
        in_specs=[blk(0), blk(C_VAL), blk(C_GATE), pl.BlockSpec((32, 128), lambda bi, t: (0, t))],
        out_specs=[blk(0), blk(0), pl.BlockSpec((1, 32, 128), lambda bi, t: (bi, 0, t))],
        scratch_shapes=[pltpu.VMEM((tp + 32, 128), F32), pltpu.VMEM((tp + 32, 128), F32),
                        pltpu.VMEM((win, 128), F32), pltpu.VMEM((win, 128), F32),
                        pltpu.VMEM((CONV_WIDTH * 8, 128), F32)],
        operands=(dhc, u, u, conv_w))
    return res[0], res[1], res[2], got


def _gla_bwd(dgla, u, o_pre, states, gup, gbias, gnorm, fuse=NO_FUSE):
    bsz, tp, _ = u.shape
    nchunk = tp // CHUNK
    nb = _gla_group(nchunk)

    def body(dy_ref, q_ref, k_ref, v_ref, r_ref, gd_ref, o_ref, st_ref, gup_ref, gb_ref, gn_ref,
             dq_ref, dk_ref, dv_ref, dr_ref, dgd_ref, dgup_ref, vec_ref, h_scr, gup_acc):
        tril, triu, hmask = _gla_consts(nb)
        h_scr[...] = jnp.zeros_like(h_scr)
        gup_acc[...] = jnp.zeros_like(gup_acc)
        gn = gn_ref[...]
        gupb = gup_ref[...]
        m = nb * CHUNK
        ngroup = nchunk // nb

        def group(i, carry):
            dbias, dgn = carry
            g = ngroup - 1 - i
            rows, valid, z, eb, enb, erest, dec, qe, ke, kd = _gla_group_terms(
                g, nb, q_ref, k_ref, gd_ref, gup_ref, gb_ref, tril)
            keb, kdb = ke.astype(BF16), kd.astype(BF16)
            dqe = jnp.zeros((nb, CHUNK, 128), F32)
            dke = jnp.zeros((nb, CHUNK, 128), F32)
            dkd = jnp.zeros((nb, CHUNK, 128), F32)
            ddec = jnp.zeros((nb, 1, 128), F32)
            for h in range(2):
                cols = slice(h * GLA_DV, (h + 1) * GLA_DV)
                o = o_ref[0, rows, cols]
                rh = r_ref[0, rows, cols]
                dy = dy_ref[0, rows, cols]
                rms = lax.rsqrt(jnp.mean(o * o, axis=-1, keepdims=True) + LN_EPS)
                nrm = o * rms
                sig = _sigmoid(rh)
                sw = rh * sig
                dr_ref[0, rows, cols] = (dy * nrm * gn * (sig * (1.0 + rh * (1.0 - sig)))).astype(BF16)
                dgn = dgn + jnp.sum(dy * nrm * sw, axis=0, keepdims=True)
                dn = dy * gn * sw
                do = rms * (dn - nrm * jnp.mean(dn * nrm, axis=-1, keepdims=True))
                dob = do.astype(BF16).reshape(nb, CHUNK, GLA_DV)
                qh = jnp.where(hmask[h], qe, 0.0).astype(BF16)
                vh = v_ref[0, rows, cols].astype(BF16).reshape(nb, CHUNK, GLA_DV)
                ht = h_scr[h]
                hts = [None] * nb
                for n in reversed(range(nb)):
                    hts[n] = ht
                    ht = dec[n] * ht + _tn(dob[n], qh[n])
                h_scr[h] = ht
                htf = jnp.stack(hts)
                htb = htf.astype(BF16)
                st = st_ref[0, h, pl.ds(g * nb, nb)]
                at = jnp.where(triu, _bnt(keb, qh), 0.0).astype(BF16)
                da = jnp.where(tril, _bnt(dob, vh), 0.0).astype(BF16)
                dat = jnp.where(triu, _bnt(vh, dob), 0.0).astype(BF16)
                dqe = dqe + jnp.where(hmask[h], _bnn(da, keb) + _bnn(dob, st.astype(BF16)), 0.0)
                dke = dke + _bnn(dat, qh)
                dv_ref[0, rows, cols] = (_bnn(at, dob) + _bnt(kdb, htb)).reshape(m, GLA_DV).astype(BF16)
                dkd = dkd + jnp.where(hmask[h], _bnn(vh, htb), 0.0)
                ddec = ddec + jnp.where(hmask[h], jnp.sum(htf * st, axis=1, keepdims=True), 0.0)
            dq_ref[0, rows, :] = (dqe * eb * Q_SCALE).reshape(m, 128).astype(BF16)
            dk_ref[0, rows, :] = (dke * enb + dkd * erest).reshape(m, 128).astype(BF16)
            db = dqe * qe - dke * ke - dkd * kd
            dblast = jnp.sum(dkd * kd, axis=1, keepdims=True) + ddec * dec
            lastrow = lax.broadcasted_iota(jnp.int32, (1, CHUNK, 1), 1) == CHUNK - 1
            db = db + jnp.where(lastrow, dblast, 0.0)
            dlg = _bnn(triu.astype(F32), db, precision=lax.Precision.HIGHEST).reshape(m, 128)
            dz = jnp.where(valid, dlg, 0.0) * (1.0 / GLA_TAU) * (1.0 - _sigmoid(z))
            dzb = dz.astype(BF16)
            dgd_ref[0, 0, rows, :] = _nt(dzb, gupb).astype(BF16)
            gup_acc[...] += _tn(gd_ref[0, rows, :].astype(BF16), dzb)
            return dbias + jnp.sum(dz, axis=0, keepdims=True), dgn

        zero = jnp.zeros((1, 128), F32)
        dbias, dgn = lax.fori_loop(0, ngroup, group, (zero, zero))
        dgup_ref[0] = gup_acc[...]
        vec_ref[0] = jnp.zeros((8, 128), F32)
        vec_ref[0, 0:1, :] = dbias
        vec_ref[0, 1:2, :] = dgn

    pair = lambda w, off: pl.BlockSpec((1, tp, w), lambda bi, p: (bi, 0, off // w + p))
    return _fused_call(
        body, fuse, name="gla_bwd",
        out_shape=[jax.ShapeDtypeStruct((bsz, tp, 256), BF16), jax.ShapeDtypeStruct((bsz, tp, 256), BF16),
                   jax.ShapeDtypeStruct((bsz, tp, 512), BF16), jax.ShapeDtypeStruct((bsz, tp, 512), BF16),
                   jax.ShapeDtypeStruct((bsz, 2, tp, 128), BF16), jax.ShapeDtypeStruct((bsz, 128, 256), F32),
                   jax.ShapeDtypeStruct((bsz, 8, 256), F32)],
        grid=(bsz, 2),
        in_specs=[pair(256, 0), pair(128, C_Q), pair(128, C_K), pair(256, C_V), pair(256, C_R),
                  pl.BlockSpec((1, tp, 128), lambda bi, p: (bi, 0, C_GD // 128)),
                  pair(256, 0),
                  pl.BlockSpec((1, 2, nchunk, GLA_DV, 128), lambda bi, p: (bi, p, 0, 0, 0)),
                  pl.BlockSpec((128, 128), lambda bi, p: (0, p)),
                  pl.BlockSpec((1, 128), lambda bi, p: (0, p)),
                  pl.BlockSpec((1, 128), lambda bi, p: (0, 0))],
        out_specs=[pair(128, 0), pair(128, 0), pair(256, 0), pair(256, 0),
                   pl.BlockSpec((1, 1, tp, 128), lambda bi, p: (bi, p, 0, 0)),
                   pl.BlockSpec((1, 128, 128), lambda bi, p: (bi, 0, p)),
                   pl.BlockSpec((1, 8, 128), lambda bi, p: (bi, 0, p))],
        scratch_shapes=[pltpu.VMEM((2, GLA_DV, 128), F32), pltpu.VMEM((128, 128), F32)],
        operands=(dgla, u, u, u, u, u, o_pre, states, gup, gbias, gnorm))


_DU_OFFSETS = (C_VAL, C_GATE, C_Q, C_K, C_V, C_R)
_DU_WIDTHS = (512, 512, 256, 256, 512, 512)


def _du_specs(tm, per, row_map):
    specs = [pl.BlockSpec((tm, w), row_map) for w in _DU_WIDTHS]
    for p in range(2):
        specs.append(pl.BlockSpec((1, 1, tm, 128), lambda *ix, p=p: (row_map(*ix)[0] // per, p, row_map(*ix)[0] % per, 0)))
    return specs


def _du_pieces(refs):
    out = [(off, ref[...]) for off, ref in zip(_DU_OFFSETS, refs[:6])]
    dgd = (refs[6][0, 0].astype(F32) + refs[7][0, 0].astype(F32)).astype(BF16)
    out.append((C_GD, dgd))
    return out


def _in_proj_bwd(pieces, dgd, w_int, dt1, tp, fuse=NO_FUSE):
    r, d = dt1.shape
    tm = _row_tile(tp)
    per = tp // tm

    def body(*refs):
        w_ref, dt_ref, o_ref = refs[8:]
        acc = ALPHA * dt_ref[...]
        for off, val in _du_pieces(refs[:8]):
            acc = acc + _nn(val, w_ref[off:off + val.shape[1], :])
        o_ref[...] = acc

    row = lambda i: (i, 0)
    (ds0,), got = _fused_call(
        body, fuse, name="in_proj_bwd", out_shape=[jax.ShapeDtypeStruct((r, d), F32)], grid=(r // tm,),
        in_specs=_du_specs(tm, per, row) + [pl.BlockSpec((D_IN_PAD, d), lambda i: (0, 0)), pl.BlockSpec((tm, d), row)],
        out_specs=[pl.BlockSpec((tm, d), row)], scratch_shapes=[], operands=(*pieces, dgd, dgd, w_int, dt1))
    return ds0, got


def _grad_w_in(pieces, dgd, s0b, tp, fuse=NO_FUSE):
    r, d = s0b.shape
    tk = _reduce_tile(tp, False)
    per = tp // tk

    def body(*refs):
        s_ref, o_ref = refs[8:]

        @pl.when(pl.program_id(0) == 0)
        def _():
            o_ref[...] = jnp.zeros_like(o_ref)

        s = s_ref[...]
        for off, val in _du_pieces(refs[:8]):
            o_ref[off:off + val.shape[1], :] += _tn(val, s)

    row = lambda k: (k, 0)
    (out,), got = _fused_call(
        body, fuse, name="grad_w_in", out_shape=[jax.ShapeDtypeStruct((D_IN_PAD, d), F32)], grid=(r // tk,),
        in_specs=_du_specs(tk, per, row) + [pl.BlockSpec((tk, d), row)],
        out_specs=[pl.BlockSpec((D_IN_PAD, d), lambda k: (0, 0))], scratch_shapes=[],
        operands=(*pieces, dgd, dgd, s0b))
    return out, got


def _ln_in_bwd(ds0, x, meta, g):
    bsz, s, d = x.shape
    tp = s + HEAD
    nh = 2
    sh = s // nh
    rc = min(256, sh)

    def body(ds_ref, x_ref, meta_ref, g_ref, gx_ref, dm_ref, vec_ref):
        h = pl.program_id(1)
        gg = g_ref[...]

        @pl.when(h == 0)
        def _():
            mh, mr = _ln_stats(meta_ref[...])
            dsm = ds_ref[0, PAD:HEAD, :]
            dm_ref[0] = _ln_bwd(dsm * gg, mh, mr)
            vec_ref[0] = jnp.zeros((8, d), F32)
            vec_ref[0, 0:1, :] = jnp.sum(dsm * mh, axis=0, keepdims=True)
            vec_ref[0, 1:2, :] = jnp.sum(dsm, axis=0, keepdims=True)

        def step(i, carry):
            sg, sb = carry
            dst = pl.ds(pl.multiple_of(i * rc, rc), rc)
            src = pl.ds(pl.multiple_of(HEAD + h * sh + i * rc, 64), rc)
            xh, rstd = _ln_stats(x_ref[0, dst, :])
            dsv = ds_ref[0, src, :]
            gx_ref[0, dst, :] = _ln_bwd(dsv * gg, xh, rstd)
            return sg + jnp.sum(dsv * xh, axis=0, keepdims=True), sb + jnp.sum(dsv, axis=0, keepdims=True)

        zero = jnp.zeros((1, d), F32)
        sg, sb = lax.fori_loop(0, sh // rc, step, (zero, zero))
        vec_ref[0, 0:1, :] += sg
        vec_ref[0, 1:2, :] += sb

    return pl.pallas_call(
        body, name="ln_in_bwd",
        out_shape=[jax.ShapeDtypeStruct((bsz, s, d), F32), jax.ShapeDtypeStruct((bsz, N_META, d), F32),
                   jax.ShapeDtypeStruct((bsz, 8, d), F32)],
        grid=(bsz, nh),
        in_specs=[pl.BlockSpec((1, tp, d), lambda bi, hi: (bi, 0, 0)),
                  pl.BlockSpec((1, sh, d), lambda bi, hi: (bi, hi, 0)),
                  pl.BlockSpec((N_META, d), lambda bi, hi: (0, 0)),
                  pl.BlockSpec((1, d), lambda bi, hi: (0, 0))],
        out_specs=[pl.BlockSpec((1, sh, d), lambda bi, hi: (bi, hi, 0)),
                   pl.BlockSpec((1, N_META, d), lambda bi, hi: (bi, 0, 0)),
                   pl.BlockSpec((1, 8, d), lambda bi, hi: (bi, 0, 0))],
        compiler_params=_params(("parallel", "arbitrary")),
    )(ds0, x, meta, g)


def _rows128(a):
    return a.reshape(-1, 128)


def kernel(x, meta_tokens, ln_in_g, ln_in_b, w_in, conv_w, conv_b, conv_ln_g, conv_ln_b, gate_up, gate_bias, gla_norm_g, w_out, ln1_g, ln1_b, w_ff1, w_ff2, ln2_g, ln2_b, loss_target, m_meta_tokens, m_ln_in_g, m_ln_in_b, m_w_in, m_conv_w, m_conv_b, m_conv_ln_g, m_conv_ln_b, m_gate_up, m_gate_bias, m_gla_norm_g, m_w_out, m_ln1_g, m_ln1_b, m_w_ff1, m_w_ff2, m_ln2_g, m_ln2_b, v_meta_tokens, v_ln_in_g, v_ln_in_b, v_w_in, v_conv_w, v_conv_b, v_conv_ln_g, v_conv_ln_b, v_gate_up, v_gate_bias, v_gla_norm_g, v_w_out, v_ln1_g, v_ln1_b, v_w_ff1, v_w_ff2, v_ln2_g, v_ln2_b):
    bsz, seq, d = x.shape
    tp = seq + HEAD
    r = bsz * tp
    xi, yi, ci = _mesh_pos()
    chip = 2 * xi + yi
    c_arr = jnp.reshape(ci, (1,)).astype(jnp.int32)
    pos_arr = jnp.stack([chip, ci]).astype(jnp.int32)

    sh_in = D_IN // 4
    shard_in = jnp.pad(w_in[0].T.astype(BF16), ((0, D_IN_PAD // 4 - sh_in), (0, 0)))
    shard_w1, shard_wout, shard_w2 = w_ff1[0].astype(BF16), w_out[0].astype(BF16), w_ff2[0].astype(BF16)
    small_w = jnp.concatenate([_rows128(meta_tokens), _rows128(conv_w[0]), _rows128(gate_up[0])], axis=0)
    g_small = _gather_small(small_w)
    n_meta_rows, n_cw_rows = N_META * 256 // 128, CONV_WIDTH * 128 // 128
    meta_full = jnp.concatenate([g_small[j, :n_meta_rows].reshape(N_META, 256) for j in range(4)], axis=1)
    convw_full = jnp.concatenate(
        [g_small[j, n_meta_rows:n_meta_rows + n_cw_rows].reshape(CONV_WIDTH, 128) for j in range(4)], axis=1)
    gup_full = jnp.concatenate(
        [g_small[j, n_meta_rows + n_cw_rows:].reshape(GLA_RANK, 64) for j in range(4)], axis=1)
    convw_p = jnp.pad(convw_full, ((0, 1), (0, 0)))
    gup_p = jnp.pad(gup_full, ((0, 128 - GLA_RANK), (0, 0))).astype(BF16)
    ln_in_g2, ln_in_b2 = ln_in_g.reshape(1, d), ln_in_b.reshape(1, d)

    s0, s0b, ((g_int,),) = _ln_in_fwd(x, meta_full, ln_in_g2, ln_in_b2,
                                      [("gather", [(shard_in, 0, D_IN_PAD // 4, None)])])
    (g_int,) = _place_own([g_int], [shard_in])
    w_int = jnp.pad(g_int[:, :sh_in].reshape(D_IN, d), ((0, D_IN_PAD - D_IN), (0, 0)))
    s0f, s0bf = s0.reshape(r, d), s0b.reshape(r, d)
    u, ((w1_buf,),) = _in_proj(s0bf, w_int, [("gather", [(shard_w1, 0, 640, None)])])
    (w1_buf,) = _place_own([w1_buf], [shard_w1])
    u3 = u.reshape(bsz, tp, D_IN_PAD)
    hc, ((w1_buf,),) = _conv_fwd(u3, convw_p, conv_b, [("gather", [(shard_w1, 640, 384, w1_buf)])])
    hc = hc.reshape(r, D_CONV)
    gla_out, o_pre, states, ((g_wout, w2_buf),) = _gla_fwd(
        u3, gup_p, gate_bias, gla_norm_g, [("gather", [(shard_wout, 0, 256, None), (shard_w2, 0, 256, None)])])
    g_wout, w2_buf = _place_own([g_wout, w2_buf], [shard_wout, shard_w2])
    wout = g_wout.reshape(d, d)
    gla_of = gla_out.reshape(r, 512)
    conv_of, xhat1, rstd1, s1b, ((w2_buf,),) = _out_proj_ln1(
        hc, gla_of, wout, s0f, conv_ln_g, conv_ln_b, ln1_g, ln1_b, [("gather", [(shard_w2, 256, 384, w2_buf)])])
    w1 = w1_buf
    ra, ((w2_buf,),) = _ffn1(s1b, w1, [("gather", [(shard_w2, 640, 384, w2_buf)])])
    w2 = w2_buf.reshape(D_FF, d)
    dt2, dt2b, acc2 = _ffn2_ln2_loss(ra, w2, xhat1, ln1_g, ln1_b, ln2_g, ln2_b, loss_target, tp)

    def add_pair(g, got, splits=None):
        return _add_pair(g, got, c_arr, splits)

    da = _ffn_bwd_da(dt2b, w2, ra)
    dt1, dt1b, acc1 = _ffn_bwd_ln1(da, w1, dt2, xhat1, rstd1, ln1_g)
    g_w2, _ = _matmul_tn(ra, dt2b, 1024, square_lhs=True, name="grad_w_ff2")
    big_w2 = g_w2.reshape(4, D_FF // 4, d)
    big_w1, ((pair_w2,),) = _grad_w_ff1(s1b, da, [("pair", [big_w2])])
    dhc, dgla, cacc, ((pair_w1,),) = _out_proj_bwd(dt1b, wout, hc, conv_ln_g, conv_ln_b, [("pair", [big_w1])])
    (part_w1,), (part_w2a, part_w2b) = add_pair(big_w1, pair_w1), add_pair(big_w2, pair_w2, [512, 512])
    big_wout = _grad_w_out(conv_of, gla_of, dt1b).reshape(4, d // 4, d)
    dcv, dcg, dcw, ((chip_w1,), (pair_wout,)) = _conv_bwd(
        dhc.reshape(bsz, tp, D_CONV), u3, convw_p, [("exchange", [part_w1]), ("pair", [big_wout])])
    (part_wout,) = add_pair(big_wout, pair_wout)
    (dq, dk, dv, dr, dgd, dgup, gvec), ((chip_w2a,),) = _gla_bwd(
        dgla.reshape(bsz, tp, 512), u3, o_pre, states, gup_p, gate_bias, gla_norm_g, [("exchange", [part_w2a])])
    pieces = [a.reshape(r, a.shape[-1]) for a in (dcv, dcg, dq, dk, dv, dr)]
    g_wint, ((chip_w2b, chip_wout),) = _grad_w_in(
        pieces, dgd, s0bf, tp, [("exchange", [part_w2b, part_wout])])
    big_win = jnp.stack([g_wint[j * sh_in:(j + 1) * sh_in] for j in range(4)])
    (half_w1, half_w2), ((pair_win,),) = _add_chips_many(
        [(big_w1, pair_w1, [chip_w1]), (big_w2, pair_w2, [chip_w2a, chip_w2b])], [("pair", [big_win])])
    ds0, ((chip_win,), (sib_w1, sib_w2)) = _in_proj_bwd(
        pieces, dgd, w_int, dt1, tp, [("exchange", add_pair(big_win, pair_win)), ("swap", [half_w1, half_w2])])
    grad_x, dmeta, lvec = _ln_in_bwd(ds0.reshape(bsz, tp, d), x, meta_full, ln_in_g2)

    red, loss = _sum_small(_allgather_small(_pack_small(acc2, acc1, lvec, cacc, gvec, dcw, dgup, dmeta)))

    half_win = _add_chips(big_win, pair_win, [chip_win], pos_arr)
    half_wout = _add_chips(big_wout, pair_wout, [chip_wout], pos_arr)
    sib_win, sib_wout = _pair_swap([half_win, half_wout])
    halves = {"w_in": (half_win, sib_win), "w_out": (half_wout, sib_wout), "w_ff1": (half_w1, sib_w1),
              "w_ff2": (half_w2, sib_w2)}

    grads = {}
    weights = dict(meta_tokens=meta_tokens, ln_in_g=ln_in_g, ln_in_b=ln_in_b, w_in=w_in, conv_w=conv_w, conv_b=conv_b,
                   conv_ln_g=conv_ln_g, conv_ln_b=conv_ln_b, gate_up=gate_up, gate_bias=gate_bias,
                   gla_norm_g=gla_norm_g, w_out=w_out, ln1_g=ln1_g, ln1_b=ln1_b, w_ff1=w_ff1, w_ff2=w_ff2,
                   ln2_g=ln2_g, ln2_b=ln2_b)
    moms = dict(meta_tokens=(m_meta_tokens, v_meta_tokens), ln_in_g=(m_ln_in_g, v_ln_in_g),
                ln_in_b=(m_ln_in_b, v_ln_in_b), w_in=(m_w_in, v_w_in), conv_w=(m_conv_w, v_conv_w),
                conv_b=(m_conv_b, v_conv_b), conv_ln_g=(m_conv_ln_g, v_conv_ln_g),
                conv_ln_b=(m_conv_ln_b, v_conv_ln_b), gate_up=(m_gate_up, v_gate_up),
                gate_bias=(m_gate_bias, v_gate_bias), gla_norm_g=(m_gla_norm_g, v_gla_norm_g),
                w_out=(m_w_out, v_w_out), ln1_g=(m_ln1_g, v_ln1_g), ln1_b=(m_ln1_b, v_ln1_b),
                w_ff1=(m_w_ff1, v_w_ff1), w_ff2=(m_w_ff2, v_w_ff2), ln2_g=(m_ln2_g, v_ln2_g),
                ln2_b=(m_ln2_b, v_ln2_b))
    names = list(weights)
    big_names = ("w_in", "w_out", "w_ff1", "w_ff2")
    delta, new_m, new_v = {}, {}, {}
    for k in big_names:
        to2d = (lambda a: a[0].T) if k == "w_in" else (lambda a: a[0])
        back = (lambda a: a.T[None]) if k == "w_in" else (lambda a: a[None])
        res = _adamw_halves(to2d(weights[k]), *halves[k], to2d(moms[k][0]), to2d(moms[k][1]), c_arr)
        grads[k], delta[k], new_m[k], new_v[k] = [back(a) for a in res]
    small_names = [k for k in names if k not in big_names]
    two = lambda a: a.reshape(-1, a.shape[-1])

    def my_cols(k, rows, width):
        r0 = SMALL_AT[k][0]
        return lax.dynamic_slice(red, (r0, chip * width), (rows, width))

    sharded = {"meta_tokens": my_cols("meta_tokens", N_META, 256), "conv_w": my_cols("conv_w", CONV_WIDTH, 128),
               "gate_up": my_cols("gate_up", GLA_RANK, 64)}
    upd = _adamw_small(red, sharded, {k: (two(weights[k]), two(moms[k][0]), two(moms[k][1])) for k in small_names})
    for k in small_names:
        shp = weights[k].shape
        grads[k], delta[k], new_m[k], new_v[k] = [a.reshape(shp) for a in upd[k]]
    loss = loss.reshape(())

    return (loss, grad_x, *[grads[k] for k in names], *[delta[k] for k in names],
            *[new_m[k] for k in names], *[new_v[k] for k in names])
```

```python
import functools

import jax
import jax.numpy as jnp
from jax import lax
from jax.experimental import pallas as pl
from jax.experimental.pallas import tpu as pltpu

F32 = jnp.float32
BF16 = jnp.bfloat16

D_MODEL = 1024
N_META = 16
D_CONV = 512
CONV_WIDTH = 31
GLA_HEADS = 4
GLA_DV = 128
GLA_DK = 64
GLA_RANK = 16
GLA_TAU = 16.0
CHUNK = 64
D_FF = 4096
LN_EPS = 1e-5
ALPHA = 2.0 ** 0.25
D_IN = 2576
D_IN_PAD = 2688
PAD = CHUNK - N_META
HEAD = PAD + N_META
Q_SCALE = GLA_DK ** -0.5
ADAM_LR, ADAM_B1, ADAM_B2, ADAM_EPS, ADAM_WD, ADAM_STEP = 0.001, 0.9, 0.999, 1e-08, 0.01, 10
HALF = D_MODEL // 2
VMEM_LIMIT = 56 * 1024 * 1024
MESH = pl.DeviceIdType.MESH

C_VAL, C_GATE, C_Q, C_K, C_V, C_R, C_GD = 0, 512, 1024, 1280, 1536, 2048, 2560

SMALL_AT = {"loss": (0, 1, 1024), "ln_in_g": (1, 1, 1024), "ln_in_b": (2, 1, 1024), "ln1_g": (3, 1, 1024),
            "ln1_b": (4, 1, 1024), "ln2_g": (5, 1, 1024), "ln2_b": (6, 1, 1024), "conv_b": (7, 1, 512),
            "conv_ln_g": (8, 1, 512), "conv_ln_b": (9, 1, 512), "gate_bias": (10, 1, 256), "gla_norm_g": (11, 1, 128),
            "conv_w": (16, 32, 512), "gate_up": (48, 16, 256), "meta_tokens": (64, 16, 1024)}
SMALL_ROWS = 80


def _params(sem=None, **kw):
    return pltpu.CompilerParams(dimension_semantics=sem, vmem_limit_bytes=VMEM_LIMIT, **kw)


def _row_tile(tp):
    for t in (704, 352, 192, 64):
        if tp % t == 0:
            return t
    raise ValueError(tp)


def _reduce_tile(tp, big):
    for t in ((2112, 1056, 704) if big else (1056, 704)) + (352, 192, 64):
        if tp % t == 0:
            return t
    raise ValueError(tp)


def _sub_rows(tm):
    return [slice(0, tm)]


def _dot(a, b, dims, precision=None):
    return lax.dot_general(a, b, (dims, ((), ())), preferred_element_type=F32, precision=precision)


def _nn(a, b, **kw):
    return _dot(a, b, ((1,), (0,)), **kw)


def _nt(a, b, **kw):
    return _dot(a, b, ((1,), (1,)), **kw)


def _tn(a, b, **kw):
    return _dot(a, b, ((0,), (0,)), **kw)


def _sigmoid(x):
    return 1.0 / (1.0 + jnp.exp(-x))


def _log_sigmoid(z):
    return jnp.minimum(z, 0.0) - jnp.log(1.0 + jnp.exp(-jnp.abs(z)))


def _ln_stats(t):
    mu = jnp.mean(t, axis=-1, keepdims=True)
    d = t - mu
    var = jnp.mean(d * d, axis=-1, keepdims=True)
    rstd = lax.rsqrt(var + LN_EPS)
    return d * rstd, rstd


def _ln_bwd(dxhat, xhat, rstd):
    m1 = jnp.mean(dxhat, axis=-1, keepdims=True)
    m2 = jnp.mean(dxhat * xhat, axis=-1, keepdims=True)
    return rstd * (dxhat - m1 - xhat * m2)


def _mesh_pos():
    return lax.axis_index("x"), lax.axis_index("y"), lax.axis_index("c")


ANY = pl.BlockSpec(memory_space=pl.ANY)


def _gather_sems(n):
    return [pltpu.SemaphoreType.DMA((n, 3))] * 4


def _gather_steps(ins, outs, sems, ranges=None):
    n = len(ins)
    send, recv, fsend, frecv = sems
    ranges = ranges or [(0, ref.shape[0]) for ref in ins]
    x, y, c = _mesh_pos()
    me = 2 * x + y
    sibling = (x, y, 1 - c)
    chips = [(1 - x, y), (x, 1 - y), (1 - x, 1 - y)]
    chip_idx = [2 * px + py for px, py in chips]
    mine = [pl.ds(pl.multiple_of(r0 + c * (nr // 2), 16), nr // 2) for r0, nr in ranges]
    other = [pl.ds(pl.multiple_of(r0 + (1 - c) * (nr // 2), 16), nr // 2) for r0, nr in ranges]
    pairs = [(a, k) for a in range(n) for k in range(3)]

    def ici(a, k, slab):
        return pltpu.make_async_remote_copy(
            src_ref=ins[a].at[mine[a], :], dst_ref=outs[a].at[slab, mine[a], :],
            send_sem=send.at[a, k], recv_sem=recv.at[a, k], device_id=(*chips[k], c), device_id_type=MESH)

    def forward(a, k, rows):
        blk = outs[a].at[chip_idx[k], rows[a], :]
        return pltpu.make_async_remote_copy(
            src_ref=blk, dst_ref=blk, send_sem=fsend.at[a, k], recv_sem=frecv.at[a, k],
            device_id=sibling, device_id_type=MESH)

    def start():
        for a, k in pairs:
            ici(a, k, me).start()

    def relay():
        for a, k in pairs:
            ici(a, k, chip_idx[k]).wait_recv()
            forward(a, k, mine).start()

    def finish():
        for a, k in pairs:
            forward(a, k, other).wait_recv()
        for a, k in pairs:
            ici(a, k, me).wait_send()
            forward(a, k, mine).wait_send()

    return start, relay, finish


def _place_own(gathered, shards):
    chip = 2 * lax.axis_index("x") + lax.axis_index("y")
    return [lax.dynamic_update_slice(g, s[None], (chip, 0, 0)) for g, s in zip(gathered, shards)]


def _gather_small(small):
    def body(small_in, small_out, ssend, srecv):
        x, y, c = _mesh_pos()
        chips = [(1 - x, y), (x, 1 - y), (1 - x, 1 - y)]

        def small_copy(k, slot):
            return pltpu.make_async_remote_copy(
                src_ref=small_in, dst_ref=small_out.at[slot], send_sem=ssend.at[k], recv_sem=srecv.at[k],
                device_id=(*chips[k], c), device_id_type=MESH)

        for k in range(3):
            small_copy(k, 2 * x + y).start()
        for k, (px, py) in enumerate(chips):
            small_copy(k, 2 * px + py).wait_recv()
        for k in range(3):
            small_copy(k, 2 * x + y).wait_send()

    res = pl.pallas_call(
        body, name="gather_small", out_shape=jax.ShapeDtypeStruct((4,) + small.shape, small.dtype),
        in_specs=[ANY], out_specs=ANY,
        scratch_shapes=[pltpu.SemaphoreType.DMA((3,)), pltpu.SemaphoreType.DMA((3,))],
    )(small)
    return _place_own([res], [small])[0]


def _pair_swap_steps(ins, outs, sems):
    send, recv = sems
    x, y, c = _mesh_pos()
    cps = [pltpu.make_async_remote_copy(
        src_ref=ins[a], dst_ref=outs[a], send_sem=send.at[a], recv_sem=recv.at[a],
        device_id=(x, y, 1 - c), device_id_type=MESH) for a in range(len(ins))]

    def start():
        for cp in cps:
            cp.start()

    def finish():
        for cp in cps:
            cp.wait()

    return start, finish


def _pair_exchange_steps(ins, outs, sems):
    send, recv = sems
    x, y, c = _mesh_pos()
    other = pl.ds(pl.multiple_of((1 - c) * HALF, 128), HALF)
    cps = [pltpu.make_async_remote_copy(
        src_ref=ins[a].at[:, :, other], dst_ref=outs[a], send_sem=send.at[a], recv_sem=recv.at[a],
        device_id=(x, y, 1 - c), device_id_type=MESH) for a in range(len(ins))]

    def start():
        for cp in cps:
            cp.start()

    def finish():
        for cp in cps:
            cp.wait()

    return start, finish


def _pair_exchange_shapes(grads):
    return [jax.ShapeDtypeStruct(g.shape[:2] + (HALF,), g.dtype) for g in grads]


def _pair_exchange(grads):
    n = len(grads)

    def body(*refs):
        start, finish = _pair_exchange_steps(refs[:n], refs[n:2 * n], refs[2 * n:])
        start()
        finish()

    return pl.pallas_call(
        body, name="grad_pair_exchange", out_shape=_pair_exchange_shapes(grads),
        in_specs=[ANY] * n, out_specs=[ANY] * n,
        scratch_shapes=[pltpu.SemaphoreType.DMA((n,)), pltpu.SemaphoreType.DMA((n,))],
    )(*grads)


NO_FUSE = ()


def _fuse_plan(kind, items):
    n = len(items)
    if kind == "gather":
        shards = [it[0] for it in items]
        bufs = [it[3] for it in items if it[3] is not None]
        alias, b = {}, 0
        for a, it in enumerate(items):
            if it[3] is not None:
                alias[n + b] = a
                b += 1
        ranges = [(it[1], it[2]) for it in items]
        return (shards + bufs, [jax.ShapeDtypeStruct((4,) + s.shape, s.dtype) for s in shards], alias, _gather_sems(n),
                lambda i, o, s: _gather_steps(i[:n], o, s, ranges))
    if kind == "exchange":
        return list(items), _chip_exchange_shapes(items), {}, _chip_exchange_sems(n), _chip_exchange_steps
    pair_sems = [pltpu.SemaphoreType.DMA((n,)), pltpu.SemaphoreType.DMA((n,))]
    if kind == "swap":
        return (list(items), [jax.ShapeDtypeStruct(h.shape, h.dtype) for h in items], {}, pair_sems, _pair_swap_steps)
    return list(items), _pair_exchange_shapes(items), {}, pair_sems, _pair_exchange_steps


def _fused_call(body, fuse, *, name, grid, in_specs, out_specs, out_shape, scratch_shapes, operands):
    plans = [_fuse_plan(kind, list(items)) for kind, items in fuse if len(items)]
    n_in, n_out, n_s = len(in_specs), len(out_shape), len(scratch_shapes)
    comm = [a for p in plans for a in p[0]]
    shapes = [s for p in plans for s in p[1]]
    nc, no = len(comm), len(shapes)
    aliases, i_at, o_at = {}, n_in, n_out
    for p in plans:
        for i, o in p[2].items():
            aliases[i_at + i] = o_at + o
        i_at, o_at = i_at + len(p[0]), o_at + len(p[1])

    def wrapped(*refs):
        o0 = n_in + nc
        s0 = o0 + n_out + no
        i_at, o_at, sem_at, steps = n_in, o0 + n_out, s0 + n_s, []
        for p in plans:
            steps.append(p[4](refs[i_at:i_at + len(p[0])], refs[o_at:o_at + len(p[1])], refs[sem_at:sem_at + len(p[3])]))
            i_at, o_at, sem_at = i_at + len(p[0]), o_at + len(p[1]), sem_at + len(p[3])
        first, last = _grid_ends(grid)
        for st in steps:
            pl.when(first)(st[0])
        for st in steps:
            for mid in st[1:-1]:
                pl.when(last)(mid)
        body(*refs[:n_in], *refs[o0:o0 + n_out], *refs[s0:s0 + n_s])
        for st in steps:
            pl.when(last)(st[-1])

    res = pl.pallas_call(
        wrapped if plans else body, name=name, grid=grid, in_specs=list(in_specs) + [ANY] * nc,
        out_specs=list(out_specs) + [ANY] * no, out_shape=list(out_shape) + shapes,
        scratch_shapes=list(scratch_shapes) + [s for p in plans for s in p[3]], input_output_aliases=aliases,
        compiler_params=_params(("arbitrary",) * len(grid)))(*operands, *comm)
    outs, got, results = list(res[:n_out]), list(res[n_out:]), []
    for p in plans:
        results.append(got[:len(p[1])])
        got = got[len(p[1]):]
    return outs, results


def _chip_exchange_sems(n):
    return [pltpu.SemaphoreType.DMA((n, 3))] * 2


def _chip_exchange_shapes(parts):
    return [jax.ShapeDtypeStruct((3,) + p.shape[1:], p.dtype) for p in parts]


def _chip_exchange_steps(ins, outs, sems):
    send, recv = sems
    x, y, c = _mesh_pos()
    chips = [(1 - x, y), (x, 1 - y), (1 - x, 1 - y)]
    cps = [pltpu.make_async_remote_copy(
        src_ref=ins[a].at[2 * px + py], dst_ref=outs[a].at[k], send_sem=send.at[a, k], recv_sem=recv.at[a, k],
        device_id=(px, py, c), device_id_type=MESH) for a in range(len(ins)) for k, (px, py) in enumerate(chips)]

    def start():
        for cp in cps:
            cp.start()

    def finish():
        for cp in cps:
            cp.wait()

    return start, finish


def _chip_exchange(parts):
    n = len(parts)

    def body(*refs):
        start, finish = _chip_exchange_steps(refs[:n], refs[n:2 * n], refs[2 * n:])
        start()
        finish()

    return pl.pallas_call(
        body, name="grad_chip_exchange", out_shape=_chip_exchange_shapes(parts),
        in_specs=[ANY] * n, out_specs=[ANY] * n, scratch_shapes=_chip_exchange_sems(n),
    )(*parts)


def _pair_swap(halves):
    n = len(halves)

    def body(*refs):
        start, finish = _pair_swap_steps(refs[:n], refs[n:2 * n], refs[2 * n:])
        start()
        finish()

    return pl.pallas_call(
        body, name="grad_pair_swap",
        out_shape=[jax.ShapeDtypeStruct(h.shape, h.dtype) for h in halves],
        in_specs=[ANY] * n, out_specs=[ANY] * n,
        scratch_shapes=[pltpu.SemaphoreType.DMA((n,)), pltpu.SemaphoreType.DMA((n,))],
    )(*halves)


def _allgather_small(pack):
    m_per, ncol = pack.shape

    def body(x_ref, out_ref, send_sems, recv_sems, local_sem):
        x, y, c = _mesh_pos()
        me, sibling = (x, y, c), (x, y, 1 - c)
        chips = [(1 - x, y), (x, 1 - y), (1 - x, 1 - y)]

        def rows(px, py, pc):
            return out_ref.at[pl.ds(pl.multiple_of((4 * px + 2 * py + pc) * m_per, 8), m_per), :]

        def copy(k, block, to, src=None):
            return pltpu.make_async_remote_copy(
                src_ref=rows(*block) if src is None else src, dst_ref=rows(*block),
                send_sem=send_sems.at[k], recv_sem=recv_sems.at[k], device_id=to, device_id_type=MESH)

        mine = pltpu.make_async_copy(x_ref, rows(*me), local_sem)
        mine.start()
        first = [copy(0, me, sibling, src=x_ref)]
        first += [copy(1 + j, me, (*chip, c), src=x_ref) for j, chip in enumerate(chips)]
        for cp in first:
            cp.start()
        passed = [copy(4 + j, (*chip, c), sibling) for j, chip in enumerate(chips)]
        for j, chip in enumerate(chips):
            copy(1 + j, (*chip, c), me).wait_recv()
            passed[j].start()
        copy(0, sibling, me).wait_recv()
        for j, chip in enumerate(chips):
            copy(4 + j, (*chip, 1 - c), me).wait_recv()
        for cp in first + passed:
            cp.wait_send()
        mine.wait()

    return pl.pallas_call(
        body, name="allgather_small",
        out_shape=jax.ShapeDtypeStruct((8 * m_per, ncol), pack.dtype),
        in_specs=[pl.BlockSpec(memory_space=pltpu.VMEM)],
        out_specs=pl.BlockSpec(memory_space=pltpu.VMEM),
        scratch_shapes=[pltpu.SemaphoreType.DMA((7,)), pltpu.SemaphoreType.DMA((7,)), pltpu.SemaphoreType.DMA],
    )(pack)


def _add_pair(g, got, c_arr, splits=None):
    _, rows, _ = g.shape
    splits = splits or [rows]

    def body(c_ref, g_ref, r_ref, *o_refs):
        at = 0
        for o_ref, n in zip(o_refs, splits):
            o_ref[...] = (g_ref[:, at:at + n, :] + r_ref[:, at:at + n, :]).astype(BF16)
            at += n

    return pl.pallas_call(
        body, name="grad_add_pair", out_shape=[jax.ShapeDtypeStruct((4, n, HALF), BF16) for n in splits],
        grid_spec=pltpu.PrefetchScalarGridSpec(
            num_scalar_prefetch=1, grid=(4,),
            in_specs=[pl.BlockSpec((1, rows, HALF), lambda j, c: (j, 0, c[0])),
                      pl.BlockSpec((1, rows, HALF), lambda j, c: (j, 0, 0))],
            out_specs=[pl.BlockSpec((1, n, HALF), lambda j, c: (j, 0, 0)) for n in splits]),
        compiler_params=_params(("arbitrary",)),
    )(c_arr, g, got)


def _add_chips(g, pair_got, chip_gots, pos_arr):
    _, rows, _ = g.shape

    def body(pos_ref, g_ref, p_ref, *refs):
        o_ref, at = refs[-1], 0
        for r_ref in refs[:-1]:
            n = r_ref.shape[1]
            own = g_ref[0, at:at + n, :] + p_ref[0, at:at + n, :]
            o_ref[at:at + n, :] = ((own + r_ref[0].astype(F32)) + r_ref[1].astype(F32)) + r_ref[2].astype(F32)
            at += n

    return pl.pallas_call(
        body, name="grad_add_chips", out_shape=jax.ShapeDtypeStruct((rows, HALF), F32),
        grid_spec=pltpu.PrefetchScalarGridSpec(
            num_scalar_prefetch=1, grid=(1,),
            in_specs=[pl.BlockSpec((1, rows, HALF), lambda i, p: (p[0], 0, p[1])),
                      pl.BlockSpec((1, rows, HALF), lambda i, p: (p[0], 0, 0))]
            + [pl.BlockSpec(t.shape, lambda i, p: (0, 0, 0)) for t in chip_gots],
            out_specs=pl.BlockSpec((rows, HALF), lambda i, p: (0, 0))),
        compiler_params=_params(("arbitrary",)),
    )(pos_arr, g, pair_got, *chip_gots)


def _add_chips_many(items, fuse=NO_FUSE):
    n = len(items)
    rows = [it[0].shape[1] for it in items]
    n_got = [len(it[2]) for it in items]

    def body(*refs):
        g_refs, p_refs = refs[:n], refs[n:2 * n]
        got_refs = refs[2 * n:2 * n + sum(n_got)]
        o_refs = refs[2 * n + sum(n_got):3 * n + sum(n_got)]
        scr = refs[3 * n + sum(n_got):]
        x, y, c = _mesh_pos()
        chip = 2 * x + y
        mine = pl.ds(pl.multiple_of(c * HALF, 128), HALF)
        copies = []
        for a in range(n):
            copies.append((pltpu.make_async_copy(g_refs[a].at[chip, :, mine], scr[2 * a], scr[2 * n].at[2 * a]),
                           pltpu.make_async_copy(p_refs[a].at[chip], scr[2 * a + 1], scr[2 * n].at[2 * a + 1])))
        for cg, cp in copies:
            cg.start()
            cp.start()
        at_ref = 0
        for a in range(n):
            copies[a][0].wait()
            copies[a][1].wait()
            at = 0
            for r_ref in got_refs[at_ref:at_ref + n_got[a]]:
                k = r_ref.shape[1]
                own = scr[2 * a][at:at + k, :] + scr[2 * a + 1][at:at + k, :]
                o_refs[a][at:at + k, :] = ((own + r_ref[0].astype(F32)) + r_ref[1].astype(F32)) + r_ref[2].astype(F32)
                at += k
            at_ref += n_got[a]

    gots = [t for it in items for t in it[2]]
    outs, got = _fused_call(
        body, fuse, name="grad_add_chips_many", grid=(1,),
        in_specs=[ANY] * (2 * n) + [pl.BlockSpec(t.shape, lambda i: (0, 0, 0)) for t in gots],
        out_specs=[pl.BlockSpec((r, HALF), lambda i: (0, 0)) for r in rows],
        out_shape=[jax.ShapeDtypeStruct((r, HALF), F32) for r in rows],
        scratch_shapes=[pltpu.VMEM((r, HALF), F32) for r in rows for _ in range(2)] + [pltpu.SemaphoreType.DMA((2 * n,))],
        operands=(*[it[0] for it in items], *[it[1] for it in items], *gots))
    return outs, got


def _pack_small(acc2, acc1, lvec, cacc, gvec, dcw, dgup, dmeta):
    bsz = lvec.shape[0]

    def body(a2_ref, a1_ref, lv_ref, ca_ref, gv_ref, cw_ref, gu_ref, dm_ref, o_ref):
        def put(name, val):
            r0, nr, nl = SMALL_AT[name]
            o_ref[r0:r0 + nr, 0:nl] = val

        over_b = lambda f: functools.reduce(lambda a, b: a + b, [f(b) for b in range(bsz)])
        o_ref[...] = jnp.zeros_like(o_ref)
        put("loss", a2_ref[0:1, :])
        put("ln2_g", a2_ref[1:2, :])
        put("ln2_b", a2_ref[2:3, :])
        put("ln1_g", a1_ref[0:1, :])
        put("ln1_b", a1_ref[1:2, :])
        put("ln_in_g", over_b(lambda b: lv_ref[b, 0:1, :]))
        put("ln_in_b", over_b(lambda b: lv_ref[b, 1:2, :]))
        put("conv_b", ca_ref[0:1, :])
        put("conv_ln_g", ca_ref[1:2, :])
        put("conv_ln_b", ca_ref[2:3, :])
        put("gate_bias", over_b(lambda b: gv_ref[b, 0:1, :]))
        put("gla_norm_g", over_b(lambda b: gv_ref[b, 1:2, 0:128] + gv_ref[b, 1:2, 128:256]))
        put("conv_w", over_b(lambda b: cw_ref[b]))
        put("gate_up", over_b(lambda b: gu_ref[b, 0:GLA_RANK, :]))
        put("meta_tokens", over_b(lambda b: dm_ref[b]))

    return pl.pallas_call(
        body, name="pack_small", out_shape=jax.ShapeDtypeStruct((SMALL_ROWS, D_MODEL), F32),
    )(acc2, acc1, lvec, cacc, gvec, dcw, dgup, dmeta)


def _sum_small(gathered):
    def body(g_ref, o_ref, loss_ref):
        acc = g_ref[0:SMALL_ROWS, :]
        for d in range(1, 8):
            acc = acc + g_ref[d * SMALL_ROWS:(d + 1) * SMALL_ROWS, :]
        o_ref[...] = acc
        loss_ref[...] = jnp.sum(acc[0:1, :], axis=1, keepdims=True)

    return pl.pallas_call(
        body, name="sum_small",
        out_shape=[jax.ShapeDtypeStruct((SMALL_ROWS, D_MODEL), F32), jax.ShapeDtypeStruct((1, 1), F32)],
    )(gathered)


def _adamw_math(w, g, m, v):
    c1 = 1.0 - ADAM_B1 ** ADAM_STEP
    c2 = 1.0 - ADAM_B2 ** ADAM_STEP
    mn = ADAM_B1 * m + (1.0 - ADAM_B1) * g
    vn = ADAM_B2 * v + (1.0 - ADAM_B2) * (g * g)
    return -ADAM_LR * ((mn / c1) / (jnp.sqrt(vn / c2) + ADAM_EPS) + ADAM_WD * w), mn, vn


def _adamw_small(red, sharded_grads, params):
    names = list(params)
    sharded = [k for k in names if k in sharded_grads]
    n, ns = len(names), len(sharded)

    def body(*refs):
        red_ref, sg_refs, p_refs, o_refs = refs[0], refs[1:1 + ns], refs[1 + ns:1 + ns + 3 * n], refs[1 + ns + 3 * n:]
        for i, k in enumerate(names):
            w_ref, m_ref, v_ref = p_refs[3 * i:3 * i + 3]
            if k in sharded:
                g = sg_refs[sharded.index(k)][...]
            else:
                r0, nr, nl = SMALL_AT[k]
                g = red_ref[r0:r0 + nr, 0:nl]
            dl, mn, vn = _adamw_math(w_ref[...], g, m_ref[...], v_ref[...])
            for o_ref, val in zip(o_refs[4 * i:4 * i + 4], (g, dl, mn, vn)):
                o_ref[...] = val

    flat = [a for k in names for a in params[k]]
    res = pl.pallas_call(
        body, name="adamw_small",
        out_shape=[jax.ShapeDtypeStruct(params[k][0].shape, F32) for k in names for _ in range(4)],
    )(red, *[sharded_grads[k] for k in sharded], *flat)
    return {k: tuple(res[4 * i:4 * i + 4]) for i, k in enumerate(names)}


def _adamw_halves(w, own, sib, m, v, c_arr):
    rows, cols = w.shape
    tr = 256 if rows % 256 == 0 else rows

    def body(c_ref, w_ref, own_ref, sib_ref, m_ref, v_ref, g_ref, d_ref, mo_ref, vo_ref):
        first = c_ref[0] == 0
        own, sib = own_ref[...], sib_ref[...]
        g = jnp.concatenate([jnp.where(first, own, sib), jnp.where(first, sib, own)], axis=1)
        g_ref[...] = g
        d_ref[...], mo_ref[...], vo_ref[...] = _adamw_math(w_ref[...], g, m_ref[...], v_ref[...])

    full = pl.BlockSpec((tr, cols), lambda i, c: (i, 0))
    half = pl.BlockSpec((tr, HALF), lambda i, c: (i, 0))
    return pl.pallas_call(
        body, name="adamw_halves", out_shape=[jax.ShapeDtypeStruct(w.shape, F32)] * 4,
        grid_spec=pltpu.PrefetchScalarGridSpec(
            num_scalar_prefetch=1, grid=(rows // tr,), in_specs=[full, half, half, full, full], out_specs=[full] * 4),
        compiler_params=_params(("parallel",)),
    )(c_arr, w, own, sib, m, v)


def _ln_in_fwd(x, meta, g, b, fuse=NO_FUSE):
    bsz, s, d = x.shape
    tp = s + HEAD
    nh = 2
    sh = s // nh
    rc = min(256, sh)

    def body(x_ref, meta_ref, g_ref, b_ref, s0_ref, s0b_ref):
        h = pl.program_id(1)
        gg, bb = g_ref[...], b_ref[...]

        @pl.when(h == 0)
        def _():
            s0_ref[0, 0:PAD, :] = jnp.zeros((PAD, d), F32)
            s0b_ref[0, 0:PAD, :] = jnp.zeros((PAD, d), BF16)
            mh, _ = _ln_stats(meta_ref[...])
            mv = mh * gg + bb
            s0_ref[0, PAD:HEAD, :] = mv
            s0b_ref[0, PAD:HEAD, :] = mv.astype(BF16)

        def step(i, carry):
            src = pl.ds(pl.multiple_of(i * rc, rc), rc)
            dst = pl.ds(pl.multiple_of(HEAD + h * sh + i * rc, 64), rc)
            xh, _ = _ln_stats(x_ref[0, src, :])
            val = xh * gg + bb
            s0_ref[0, dst, :] = val
            s0b_ref[0, dst, :] = val.astype(BF16)
            return carry

        lax.fori_loop(0, sh // rc, step, 0)

    full = lambda bi, hi: (bi, 0, 0)
    (s0, s0b), got = _fused_call(
        body, fuse, name="ln_in_fwd",
        out_shape=[jax.ShapeDtypeStruct((bsz, tp, d), F32), jax.ShapeDtypeStruct((bsz, tp, d), BF16)],
        grid=(bsz, nh),
        in_specs=[pl.BlockSpec((1, sh, d), lambda bi, hi: (bi, hi, 0)),
                  pl.BlockSpec((N_META, d), lambda bi, hi: (0, 0)),
                  pl.BlockSpec((1, d), lambda bi, hi: (0, 0)),
                  pl.BlockSpec((1, d), lambda bi, hi: (0, 0))],
        out_specs=[pl.BlockSpec((1, tp, d), full)] * 2, scratch_shapes=[], operands=(x, meta, g, b))
    return s0, s0b, got


def _in_proj(s0b, w_int, fuse=NO_FUSE):
    r, d = s0b.shape
    tm = _row_tile(r)

    def body(a_ref, w_ref, o_ref):
        o_ref[...] = _nt(a_ref[...], w_ref[...])

    (u,), got = _fused_call(
        body, fuse, name="in_proj", out_shape=[jax.ShapeDtypeStruct((r, D_IN_PAD), F32)],
        grid=(r // tm,),
        in_specs=[pl.BlockSpec((tm, d), lambda i: (i, 0)), pl.BlockSpec((D_IN_PAD, d), lambda i: (0, 0))],
        out_specs=[pl.BlockSpec((tm, D_IN_PAD), lambda i: (i, 0))], scratch_shapes=[], operands=(s0b, w_int))
    return u, got


def _conv_fwd(u, conv_w, conv_b, fuse=NO_FUSE):
    bsz, tp, _ = u.shape
    nchunk = tp // CHUNK
    win = CHUNK + 32
    nct = D_CONV // 128

    def body(cv_ref, cg_ref, w_ref, cb_ref, hc_ref, h_scr, win_scr):
        h_scr[0:32, :] = jnp.zeros((32, 128), F32)
        h_scr[32:32 + tp, :] = cv_ref[0] * _sigmoid(cg_ref[0])
        cb = cb_ref[...]

        def step(n, carry):
            r0 = pl.multiple_of(n * CHUNK, CHUNK)
            win_scr[...] = h_scr[pl.ds(r0, win), :]
            acc = jnp.zeros((CHUNK, 128), F32)
            for j in range(CONV_WIDTH):
                acc = acc + w_ref[j:j + 1, :] * win_scr[2 + j:2 + j + CHUNK, :]
            hc_ref[0, pl.ds(r0, CHUNK), :] = acc + cb
            return carry

        lax.fori_loop(0, nchunk, step, 0)

    (hc,), got = _fused_call(
        body, fuse, name="conv_fwd", out_shape=[jax.ShapeDtypeStruct((bsz, tp, D_CONV), F32)],
        grid=(bsz, nct),
        in_specs=[pl.BlockSpec((1, tp, 128), lambda bi, t: (bi, 0, C_VAL // 128 + t)),
                  pl.BlockSpec((1, tp, 128), lambda bi, t: (bi, 0, C_GATE // 128 + t)),
                  pl.BlockSpec((32, 128), lambda bi, t: (0, t)),
                  pl.BlockSpec((1, 128), lambda bi, t: (0, t))],
        out_specs=[pl.BlockSpec((1, tp, 128), lambda bi, t: (bi, 0, t))],
        scratch_shapes=[pltpu.VMEM((tp + 32, 128), F32), pltpu.VMEM((win, 128), F32)],
        operands=(u, u, conv_w, conv_b))
    return hc, got


def _gla_group(nchunk):
    return 11 if nchunk % 11 == 0 else nchunk


def _bdot(a, b, ca, cb, precision=None):
    return lax.dot_general(a, b, (((ca,), (cb,)), ((0,), (0,))), preferred_element_type=F32, precision=precision)


def _bnn(a, b, **kw):
    return _bdot(a, b, 2, 1, **kw)


def _bnt(a, b, **kw):
    return _bdot(a, b, 2, 2, **kw)


def _gla_consts(nb):
    row = lax.broadcasted_iota(jnp.int32, (nb, CHUNK, CHUNK), 1)
    col = lax.broadcasted_iota(jnp.int32, (nb, CHUNK, CHUNK), 2)
    lane = lax.broadcasted_iota(jnp.int32, (1, 1, 128), 2)
    return row >= col, row <= col, [lane < GLA_DK, lane >= GLA_DK]


def _gla_group_terms(g, nb, q_ref, k_ref, gd_ref, gup_ref, gb_ref, tril):
    m = nb * CHUNK
    rows = pl.ds(pl.multiple_of(g * m, CHUNK), m)
    z = _nn(gd_ref[0, rows, :].astype(BF16), gup_ref[...]) + gb_ref[...]
    valid = g * m + lax.broadcasted_iota(jnp.int32, (m, 1), 0) >= PAD
    lg = jnp.where(valid, _log_sigmoid(z) * (1.0 / GLA_TAU), 0.0)
    bcum = _bnn(tril.astype(F32), lg.reshape(nb, CHUNK, 128), precision=lax.Precision.HIGHEST)
    blast = bcum[:, CHUNK - 1:CHUNK, :]
    eb = jnp.exp(bcum)
    enb = jnp.exp(-bcum)
    erest = jnp.exp(blast - bcum)
    q = (q_ref[0, rows, :] * Q_SCALE).reshape(nb, CHUNK, 128)
    k = k_ref[0, rows, :].reshape(nb, CHUNK, 128)
    return rows, valid, z, eb, enb, erest, jnp.exp(blast), q * eb, k * enb, k * erest


def _grid_ends(grid):
    ids = [pl.program_id(i) for i in range(len(grid))]
    first = functools.reduce(jnp.logical_and, [i == 0 for i in ids])
    last = functools.reduce(jnp.logical_and, [i == g - 1 for i, g in zip(ids, grid)])
    return first, last


def _gla_fwd(u, gup, gbias, gnorm, fuse=NO_FUSE):
    bsz, tp, _ = u.shape
    nchunk = tp // CHUNK
    nb = _gla_group(nchunk)

    def body(q_ref, k_ref, v_ref, r_ref, gd_ref, gup_ref, gb_ref, gn_ref, out_ref, o_ref, st_ref, s_scr):
        tril, _, hmask = _gla_consts(nb)
        s_scr[...] = jnp.zeros_like(s_scr)
        gn = gn_ref[...]

        def group(g, carry):
            rows, _, _, _, _, _, dec, qe, ke, kd = _gla_group_terms(g, nb, q_ref, k_ref, gd_ref, gup_ref, gb_ref, tril)
            keb, kdb = ke.astype(BF16), kd.astype(BF16)
            for h in range(2):
                cols = slice(h * GLA_DV, (h + 1) * GLA_DV)
                qh = jnp.where(hmask[h], qe, 0.0).astype(BF16)
                vh = v_ref[0, rows, cols].astype(BF16).reshape(nb, CHUNK, GLA_DV)
                a = jnp.where(tril, _bnt(qh, keb), 0.0).astype(BF16)
                st = s_scr[h]
                sts = []
                for n in range(nb):
                    st_ref[0, h, g * nb + n] = st
                    sts.append(st.astype(BF16))
                    st = dec[n] * st + _tn(vh[n], kdb[n])
                s_scr[h] = st
                o = (_bnn(a, vh) + _bnt(qh, jnp.stack(sts))).reshape(nb * CHUNK, GLA_DV)
                o_ref[0, rows, cols] = o
                rms = lax.rsqrt(jnp.mean(o * o, axis=-1, keepdims=True) + LN_EPS)
                rh = r_ref[0, rows, cols]
                out_ref[0, rows, cols] = (o * rms * gn * (rh * _sigmoid(rh))).astype(BF16)
            return carry

        lax.fori_loop(0, nchunk // nb, group, 0)

    res, got = _fused_call(
        body, fuse, name="gla_fwd",
        out_shape=[jax.ShapeDtypeStruct((bsz, tp, 512), BF16), jax.ShapeDtypeStruct((bsz, tp, 512), F32),
                   jax.ShapeDtypeStruct((bsz, GLA_HEADS, nchunk, GLA_DV, 128), F32)],
        grid=(bsz, 2),
        in_specs=[pl.BlockSpec((1, tp, 128), lambda bi, p: (bi, 0, C_Q // 128 + p)),
                  pl.BlockSpec((1, tp, 128), lambda bi, p: (bi, 0, C_K // 128 + p)),
                  pl.BlockSpec((1, tp, 256), lambda bi, p: (bi, 0, C_V // 256 + p)),
                  pl.BlockSpec((1, tp, 256), lambda bi, p: (bi, 0, C_R // 256 + p)),
                  pl.BlockSpec((1, tp, 128), lambda bi, p: (bi, 0, C_GD // 128)),
                  pl.BlockSpec((128, 128), lambda bi, p: (0, p)),
                  pl.BlockSpec((1, 128), lambda bi, p: (0, p)),
                  pl.BlockSpec((1, 128), lambda bi, p: (0, 0))],
        out_specs=[pl.BlockSpec((1, tp, 256), lambda bi, p: (bi, 0, p)),
                   pl.BlockSpec((1, tp, 256), lambda bi, p: (bi, 0, p)),
                   pl.BlockSpec((1, 2, nchunk, GLA_DV, 128), lambda bi, p: (bi, p, 0, 0, 0))],
        scratch_shapes=[pltpu.VMEM((2, GLA_DV, 128), F32)],
        operands=(u, u, u, u, u, gup, gbias, gnorm))
    return res[0], res[1], res[2], got


def _out_proj_ln1(hc, gla_out, w_out, s0, cg, cb, g1, b1, fuse=NO_FUSE):
    r, d = s0.shape
    tm = _row_tile(r)

    def body(hc_ref, a_ref, w_ref, s0_ref, cg_ref, cb_ref, g_ref, b_ref, co_ref, xh_ref, rstd_ref, s1b_ref):
        for rs in _sub_rows(tm):
            xc, _ = _ln_stats(hc_ref[rs, :])
            nv = xc * cg_ref[...] + cb_ref[...]
            co = (nv * _sigmoid(nv)).astype(BF16)
            co_ref[rs, :] = co
            mix = _nn(co, w_ref[0:D_CONV, :]) + _nn(a_ref[rs, :], w_ref[D_CONV:, :])
            xh, rstd = _ln_stats(ALPHA * s0_ref[rs, :] + mix)
            xh_ref[rs, :] = xh
            rstd_ref[rs, :] = rstd
            s1b_ref[rs, :] = (xh * g_ref[...] + b_ref[...]).astype(BF16)

    row = lambda n: pl.BlockSpec((tm, n), lambda i: (i, 0))
    vec = lambda n: pl.BlockSpec((1, n), lambda i: (0, 0))
    res, got = _fused_call(
        body, fuse, name="out_proj_ln1",
        out_shape=[jax.ShapeDtypeStruct((r, D_CONV), BF16), jax.ShapeDtypeStruct((r, d), F32),
                   jax.ShapeDtypeStruct((r, 1), F32), jax.ShapeDtypeStruct((r, d), BF16)],
        grid=(r // tm,),
        in_specs=[row(D_CONV), row(512), pl.BlockSpec((d, d), lambda i: (0, 0)), row(d),
                  vec(D_CONV), vec(D_CONV), vec(d), vec(d)],
        out_specs=[row(D_CONV), row(d), row(1), row(d)], scratch_shapes=[],
        operands=(hc, gla_out, w_out, s0, cg, cb, g1, b1))
    return res[0], res[1], res[2], res[3], got


def _ffn1(s1b, w1, fuse=NO_FUSE):
    r, d = s1b.shape
    tm = _row_tile(r)
    ns, _, wn = w1.shape

    def body(a_ref, w_ref, o_ref):
        a = a_ref[...]
        for j in range(ns):
            o_ref[:, j * wn:(j + 1) * wn] = jnp.maximum(_nn(a, w_ref[j]), 0.0).astype(BF16)

    (ra,), got = _fused_call(
        body, fuse, name="ffn1", out_shape=[jax.ShapeDtypeStruct((r, D_FF), BF16)], grid=(r // tm,),
        in_specs=[pl.BlockSpec((tm, d), lambda i: (i, 0)), pl.BlockSpec(w1.shape, lambda i: (0, 0, 0))],
        out_specs=[pl.BlockSpec((tm, D_FF), lambda i: (i, 0))], scratch_shapes=[], operands=(s1b, w1))
    return ra, got


def _ffn2_ln2_loss(ra, w2, xhat1, g1, b1, g2, b2, tgt, tp):
    r, d = xhat1.shape
    tm = _row_tile(tp)
    per = tp // tm

    def body(ra_ref, w_ref, xh1_ref, g1_ref, b1_ref, g2_ref, b2_ref, tgt_ref, dt_ref, dtb_ref, acc_ref, t_ref, sem):
        i = pl.program_id(0)
        b, j = i // per, i % per

        @pl.when(i == 0)
        def _():
            acc_ref[...] = jnp.zeros_like(acc_ref)

        head_copy = pltpu.make_async_copy(tgt_ref.at[b, pl.ds(0, tm - HEAD), :], t_ref.at[pl.ds(HEAD, tm - HEAD), :], sem)
        body_copy = pltpu.make_async_copy(
            tgt_ref.at[b, pl.ds(pl.multiple_of(jnp.maximum(j * tm - HEAD, 0), 64), tm), :], t_ref, sem)

        @pl.when(j == 0)
        def _():
            t_ref[0:HEAD, :] = jnp.zeros((HEAD, d), F32)
            head_copy.start()

        pl.when(j > 0)(body_copy.start)

        sums = [jnp.zeros((1, d), F32)] * 3
        for rs in _sub_rows(tm):
            rb = ra_ref[rs, :]
            f = _nn(rb * rb, w_ref[...])
            if rs.start == 0:
                pl.when(j == 0)(head_copy.wait)
                pl.when(j > 0)(body_copy.wait)
            s1 = xh1_ref[rs, :] * g1_ref[...] + b1_ref[...]
            xh2, rstd2 = _ln_stats(ALPHA * s1 + f)
            y = xh2 * g2_ref[...] + b2_ref[...]
            rowid = (i % per) * tm + rs.start + lax.broadcasted_iota(jnp.int32, (rs.stop - rs.start, 1), 0)
            e = jnp.where(rowid >= HEAD, y - t_ref[rs, :], 0.0)
            dy = e * (1.0 / d)
            dt2 = _ln_bwd(dy * g2_ref[...], xh2, rstd2)
            dt_ref[rs, :] = dt2
            dtb_ref[rs, :] = dt2.astype(BF16)
            sums = [sums[0] + (0.5 / d) * jnp.sum(e * e, axis=0, keepdims=True),
                    sums[1] + jnp.sum(dy * xh2, axis=0, keepdims=True), sums[2] + jnp.sum(dy, axis=0, keepdims=True)]
        for k in range(3):
            acc_ref[k:k + 1, :] += sums[k]

    row = lambda n: pl.BlockSpec((tm, n), lambda i: (i, 0))
    vec = pl.BlockSpec((1, d), lambda i: (0, 0))
    return pl.pallas_call(
        body, name="ffn2_ln2_loss",
        out_shape=[jax.ShapeDtypeStruct((r, d), F32), jax.ShapeDtypeStruct((r, d), BF16),
                   jax.ShapeDtypeStruct((8, d), F32)],
        grid=(r // tm,),
        in_specs=[row(D_FF), pl.BlockSpec((D_FF, d), lambda i: (0, 0)), row(d), vec, vec, vec, vec, ANY],
        out_specs=[row(d), row(d), pl.BlockSpec((8, d), lambda i: (0, 0))],
        scratch_shapes=[pltpu.VMEM((tm, d), F32), pltpu.SemaphoreType.DMA],
        compiler_params=_params(("arbitrary",)),
    )(ra, w2, xhat1, g1, b1, g2, b2, tgt)


def _ffn_bwd_da(dt2b, w2, ra):
    r, d = dt2b.shape
    tm = _row_tile(r)

    def body(g_ref, w_ref, ra_ref, o_ref):
        o_ref[...] = (_nt(g_ref[...], w_ref[...]) * (2.0 * ra_ref[...].astype(F32))).astype(BF16)

    return pl.pallas_call(
        body, name="ffn_bwd_da", out_shape=jax.ShapeDtypeStruct((r, D_FF), BF16),
        grid=(r // tm,),
        in_specs=[pl.BlockSpec((tm, d), lambda i: (i, 0)), pl.BlockSpec((D_FF, d), lambda i: (0, 0)),
                  pl.BlockSpec((tm, D_FF), lambda i: (i, 0))],
        out_specs=pl.BlockSpec((tm, D_FF), lambda i: (i, 0)),
        compiler_params=_params(("parallel",)),
    )(dt2b, w2, ra)


def _ffn_bwd_ln1(da, w1, dt2, xhat1, rstd1, g1):
    r, d = dt2.shape
    tm = _row_tile(r)

    def body(da_ref, w_ref, dt2_ref, xh_ref, rstd_ref, g_ref, dt_ref, dtb_ref, acc_ref):
        @pl.when(pl.program_id(0) == 0)
        def _():
            acc_ref[...] = jnp.zeros_like(acc_ref)

        sums = [jnp.zeros((1, d), F32)] * 2
        for rs in _sub_rows(tm):
            ds1 = ALPHA * dt2_ref[rs, :]
            for j in range(w1.shape[0]):
                ds1 = ds1 + _nt(da_ref[rs, j * w1.shape[2]:(j + 1) * w1.shape[2]], w_ref[j])
            xh = xh_ref[rs, :]
            dt1 = _ln_bwd(ds1 * g_ref[...], xh, rstd_ref[rs, :])
            dt_ref[rs, :] = dt1
            dtb_ref[rs, :] = dt1.astype(BF16)
            sums = [sums[0] + jnp.sum(ds1 * xh, axis=0, keepdims=True), sums[1] + jnp.sum(ds1, axis=0, keepdims=True)]
        for k in range(2):
            acc_ref[k:k + 1, :] += sums[k]

    row = lambda n: pl.BlockSpec((tm, n), lambda i: (i, 0))
    return pl.pallas_call(
        body, name="ffn_bwd_ln1",
        out_shape=[jax.ShapeDtypeStruct((r, d), F32), jax.ShapeDtypeStruct((r, d), BF16),
                   jax.ShapeDtypeStruct((8, d), F32)],
        grid=(r // tm,),
        in_specs=[row(D_FF), pl.BlockSpec(w1.shape, lambda i: (0, 0, 0)), row(d), row(d), row(1),
                  pl.BlockSpec((1, d), lambda i: (0, 0))],
        out_specs=[row(d), row(d), pl.BlockSpec((8, d), lambda i: (0, 0))],
        compiler_params=_params(("arbitrary",)),
    )(da, w1, dt2, xhat1, rstd1, g1)


def _matmul_tn(lhs, rhs, bm, square_lhs=False, name="matmul_tn", fuse=NO_FUSE):
    r, m = lhs.shape
    n = rhs.shape[1]
    tk = _reduce_tile(r, True)

    def body(a_ref, b_ref, o_ref):
        @pl.when(pl.program_id(1) == 0)
        def _():
            o_ref[...] = jnp.zeros_like(o_ref)

        a = a_ref[...]
        if square_lhs:
            a = a * a
        o_ref[...] += _tn(a, b_ref[...])

    (out,), got = _fused_call(
        body, fuse, name=name, out_shape=[jax.ShapeDtypeStruct((m, n), F32)], grid=(m // bm, r // tk),
        in_specs=[pl.BlockSpec((tk, bm), lambda i, k: (k, i)), pl.BlockSpec((tk, n), lambda i, k: (k, 0))],
        out_specs=[pl.BlockSpec((bm, n), lambda i, k: (i, 0))], scratch_shapes=[], operands=(lhs, rhs))
    return out, got


def _grad_w_ff1(s1b, da, fuse=NO_FUSE):
    r, d = s1b.shape
    wn = da.shape[1] // 4
    tk = _reduce_tile(r, True)

    def body(a_ref, b_ref, o_ref):
        @pl.when(pl.program_id(1) == 0)
        def _():
            o_ref[...] = jnp.zeros_like(o_ref)

        o_ref[0] += _tn(a_ref[...], b_ref[...])

    (out,), got = _fused_call(
        body, fuse, name="grad_w_ff1", out_shape=[jax.ShapeDtypeStruct((4, d, wn), F32)], grid=(4, r // tk),
        in_specs=[pl.BlockSpec((tk, d), lambda j, k: (k, 0)), pl.BlockSpec((tk, wn), lambda j, k: (k, j))],
        out_specs=[pl.BlockSpec((1, d, wn), lambda j, k: (j, 0, 0))], scratch_shapes=[], operands=(s1b, da))
    return out, got


def _grad_w_out(conv_of, gla_of, dt1b, fuse=NO_FUSE):
    r, n = dt1b.shape
    tk = _reduce_tile(r, True)

    def body(a_ref, b_ref, g_ref, o_ref):
        @pl.when(pl.program_id(1) == 0)
        def _():
            o_ref[...] = jnp.zeros_like(o_ref)

        @pl.when(pl.program_id(0) == 0)
        def _():
            o_ref[...] += _tn(a_ref[...], g_ref[...])

        @pl.when(pl.program_id(0) == 1)
        def _():
            o_ref[...] += _tn(b_ref[...], g_ref[...])

    lhs = pl.BlockSpec((tk, 512), lambda i, k: (k, 0))
    (out,), got = _fused_call(
        body, fuse, name="grad_w_out", out_shape=[jax.ShapeDtypeStruct((2 * 512, n), F32)], grid=(2, r // tk),
        in_specs=[lhs, lhs, pl.BlockSpec((tk, n), lambda i, k: (k, 0))],
        out_specs=[pl.BlockSpec((512, n), lambda i, k: (i, 0))], scratch_shapes=[],
        operands=(conv_of, gla_of, dt1b))
    return out, got


def _out_proj_bwd(dt1b, w_out, hc, cg, cb, fuse=NO_FUSE):
    r, d = dt1b.shape
    tm = _row_tile(r)

    def body(g_ref, w_ref, hc_ref, cg_ref, cb_ref, dhc_ref, dgla_ref, acc_ref):
        @pl.when(pl.program_id(0) == 0)
        def _():
            acc_ref[...] = jnp.zeros_like(acc_ref)

        gg = cg_ref[...]
        sums = [jnp.zeros((1, D_CONV), F32)] * 3
        for rs in _sub_rows(tm):
            dmix = _nt(g_ref[rs, :], w_ref[...])
            dgla_ref[rs, :] = dmix[:, D_CONV:]
            xh, rstd = _ln_stats(hc_ref[rs, :])
            nv = xh * gg + cb_ref[...]
            sig = _sigmoid(nv)
            dn = dmix[:, :D_CONV] * (sig * (1.0 + nv * (1.0 - sig)))
            dhc = _ln_bwd(dn * gg, xh, rstd)
            dhc_ref[rs, :] = dhc
            sums = [sums[0] + jnp.sum(dhc, axis=0, keepdims=True), sums[1] + jnp.sum(dn * xh, axis=0, keepdims=True),
                    sums[2] + jnp.sum(dn, axis=0, keepdims=True)]
        for k in range(3):
            acc_ref[k:k + 1, :] += sums[k]

    row = lambda n: pl.BlockSpec((tm, n), lambda i: (i, 0))
    vec = pl.BlockSpec((1, D_CONV), lambda i: (0, 0))
    res, got = _fused_call(
        body, fuse, name="out_proj_bwd",
        out_shape=[jax.ShapeDtypeStruct((r, D_CONV), F32), jax.ShapeDtypeStruct((r, 512), F32),
                   jax.ShapeDtypeStruct((8, D_CONV), F32)],
        grid=(r // tm,),
        in_specs=[row(d), pl.BlockSpec((d, d), lambda i: (0, 0)), row(D_CONV), vec, vec],
        out_specs=[row(D_CONV), row(512), pl.BlockSpec((8, D_CONV), lambda i: (0, 0))], scratch_shapes=[],
        operands=(dt1b, w_out, hc, cg, cb))
    return res[0], res[1], res[2], got


def _conv_bwd(dhc, u, conv_w, fuse=NO_FUSE):
    bsz, tp, _ = u.shape
    nchunk = tp // CHUNK
    win = CHUNK + 32
    nct = D_CONV // 128

    def body(dhc_ref, cv_ref, cg_ref, w_ref, dv_ref, dg_ref, dw_ref, h_scr, dhc_scr, hwin, dwin, dw_scr):
        h_scr[0:32, :] = jnp.zeros((32, 128), F32)
        h_scr[32:32 + tp, :] = cv_ref[0] * _sigmoid(cg_ref[0])
        dhc_scr[0:tp, :] = dhc_ref[0]
        dhc_scr[tp:tp + 32, :] = jnp.zeros((32, 128), F32)
        dw_scr[...] = jnp.zeros_like(dw_scr)

        def step(n, carry):
            r0 = pl.multiple_of(n * CHUNK, CHUNK)
            rows = pl.ds(r0, CHUNK)
            hwin[...] = h_scr[pl.ds(r0, win), :]
            dwin[...] = dhc_scr[pl.ds(r0, win), :]
            dcur = dwin[0:CHUNK, :]
            acc = jnp.zeros((CHUNK, 128), F32)
            for j in range(CONV_WIDTH):
                acc = acc + w_ref[j:j + 1, :] * dwin[30 - j:30 - j + CHUNK, :]
                prod = dcur * hwin[2 + j:2 + j + CHUNK, :]
                dw_scr[j * 8:(j + 1) * 8, :] += jnp.sum(prod.reshape(CHUNK // 8, 8, 128), axis=0)
            cg = cg_ref[0, rows, :]
            sig = _sigmoid(cg)
            rowid = n * CHUNK + lax.broadcasted_iota(jnp.int32, (CHUNK, 1), 0)
            dh = jnp.where(rowid >= PAD, acc, 0.0)
            dv_ref[0, rows, :] = (dh * sig).astype(BF16)
            dg_ref[0, rows, :] = (dh * cv_ref[0, rows, :] * sig * (1.0 - sig)).astype(BF16)
            return carry

        lax.fori_loop(0, nchunk, step, 0)
        dw_ref[0] = jnp.zeros((32, 128), F32)
        for j in range(CONV_WIDTH):
            dw_ref[0, j:j + 1, :] = jnp.sum(dw_scr[j * 8:(j + 1) * 8, :], axis=0, keepdims=True)

    blk = lambda off: pl.BlockSpec((1, tp, 128), lambda bi, t: (bi, 0, off // 128 + t))
    res, got = _fused_call(
        body, fuse, name="conv_bwd",
        out_shape=[jax.ShapeDtypeStruct((bsz, tp, D_CONV), BF16), jax.ShapeDtypeStruct((bsz, tp, D_CONV), BF16),
                   jax.ShapeDtypeStruct((bsz, 32, D_CONV), F32)],
        grid=(bsz, nct),
        in_specs=[blk(0), blk(C_VAL), blk(C_GATE), pl.BlockSpec((32, 128), lambda bi, t: (0, t))],
        out_specs=[blk(0), blk(0), pl.BlockSpec((1, 32, 128), lambda bi, t: (bi, 0, t))],
        scratch_shapes=[pltpu.VMEM((tp + 32, 128), F32), pltpu.VMEM((tp + 32, 128), F32),
                        pltpu.VMEM((win, 128), F32), pltpu.VMEM((win, 128), F32),
                        pltpu.VMEM((CONV_WIDTH * 8, 128), F32)],
        operands=(dhc, u, u, conv_w))
    return res[0], res[1], res[2], got


def _gla_bwd(dgla, u, o_pre, states, gup, gbias, gnorm, fuse=NO_FUSE):
    bsz, tp, _ = u.shape
    nchunk = tp // CHUNK
    nb = _gla_group(nchunk)

    def body(dy_ref, q_ref, k_ref, v_ref, r_ref, gd_ref, o_ref, st_ref, gup_ref, gb_ref, gn_ref,
             dq_ref, dk_ref, dv_ref, dr_ref, dgd_ref, dgup_ref, vec_ref, h_scr, gup_acc):
        tril, triu, hmask = _gla_consts(nb)
        h_scr[...] = jnp.zeros_like(h_scr)
        gup_acc[...] = jnp.zeros_like(gup_acc)
        gn = gn_ref[...]
        gupb = gup_ref[...]
        m = nb * CHUNK
        ngroup = nchunk // nb

        def group(i, carry):
            dbias, dgn = carry
            g = ngroup - 1 - i
            rows, valid, z, eb, enb, erest, dec, qe, ke, kd = _gla_group_terms(
                g, nb, q_ref, k_ref, gd_ref, gup_ref, gb_ref, tril)
            keb, kdb = ke.astype(BF16), kd.astype(BF16)
            dqe = jnp.zeros((nb, CHUNK, 128), F32)
            dke = jnp.zeros((nb, CHUNK, 128), F32)
            dkd = jnp.zeros((nb, CHUNK, 128), F32)
            ddec = jnp.zeros((nb, 1, 128), F32)
            for h in range(2):
                cols = slice(h * GLA_DV, (h + 1) * GLA_DV)
                o = o_ref[0, rows, cols]
                rh = r_ref[0, rows, cols]
                dy = dy_ref[0, rows, cols]
                rms = lax.rsqrt(jnp.mean(o * o, axis=-1, keepdims=True) + LN_EPS)
                nrm = o * rms
                sig = _sigmoid(rh)
                sw = rh * sig
                dr_ref[0, rows, cols] = (dy * nrm * gn * (sig * (1.0 + rh * (1.0 - sig)))).astype(BF16)
                dgn = dgn + jnp.sum(dy * nrm * sw, axis=0, keepdims=True)
                dn = dy * gn * sw
                do = rms * (dn - nrm * jnp.mean(dn * nrm, axis=-1, keepdims=True))
                dob = do.astype(BF16).reshape(nb, CHUNK, GLA_DV)
                qh = jnp.where(hmask[h], qe, 0.0).astype(BF16)
                vh = v_ref[0, rows, cols].astype(BF16).reshape(nb, CHUNK, GLA_DV)
                ht = h_scr[h]
                hts = [None] * nb
                for n in reversed(range(nb)):
                    hts[n] = ht
                    ht = dec[n] * ht + _tn(dob[n], qh[n])
                h_scr[h] = ht
                htf = jnp.stack(hts)
                htb = htf.astype(BF16)
                st = st_ref[0, h, pl.ds(g * nb, nb)]
                at = jnp.where(triu, _bnt(keb, qh), 0.0).astype(BF16)
                da = jnp.where(tril, _bnt(dob, vh), 0.0).astype(BF16)
                dat = jnp.where(triu, _bnt(vh, dob), 0.0).astype(BF16)
                dqe = dqe + jnp.where(hmask[h], _bnn(da, keb) + _bnn(dob, st.astype(BF16)), 0.0)
                dke = dke + _bnn(dat, qh)
                dv_ref[0, rows, cols] = (_bnn(at, dob) + _bnt(kdb, htb)).reshape(m, GLA_DV).astype(BF16)
                dkd = dkd + jnp.where(hmask[h], _bnn(vh, htb), 0.0)
                ddec = ddec + jnp.where(hmask[h], jnp.sum(htf * st, axis=1, keepdims=True), 0.0)
            dq_ref[0, rows, :] = (dqe * eb * Q_SCALE).reshape(m, 128).astype(BF16)
            dk_ref[0, rows, :] = (dke * enb + dkd * erest).reshape(m, 128).astype(BF16)
            db = dqe * qe - dke * ke - dkd * kd
            dblast = jnp.sum(dkd * kd, axis=1, keepdims=True) + ddec * dec
            lastrow = lax.broadcasted_iota(jnp.int32, (1, CHUNK, 1), 1) == CHUNK - 1
            db = db + jnp.where(lastrow, dblast, 0.0)
            dlg = _bnn(triu.astype(F32), db, precision=lax.Precision.HIGHEST).reshape(m, 128)
            dz = jnp.where(valid, dlg, 0.0) * (1.0 / GLA_TAU) * (1.0 - _sigmoid(z))
            dzb = dz.astype(BF16)
            dgd_ref[0, 0, rows, :] = _nt(dzb, gupb).astype(BF16)
            gup_acc[...] += _tn(gd_ref[0, rows, :].astype(BF16), dzb)
            return dbias + jnp.sum(dz, axis=0, keepdims=True), dgn

        zero = jnp.zeros((1, 128), F32)
        dbias, dgn = lax.fori_loop(0, ngroup, group, (zero, zero))
        dgup_ref[0] = gup_acc[...]
        vec_ref[0] = jnp.zeros((8, 128), F32)
        vec_ref[0, 0:1, :] = dbias
        vec_ref[0, 1:2, :] = dgn

    pair = lambda w, off: pl.BlockSpec((1, tp, w), lambda bi, p: (bi, 0, off // w + p))
    return _fused_call(
        body, fuse, name="gla_bwd",
        out_shape=[jax.ShapeDtypeStruct((bsz, tp, 256), BF16), jax.ShapeDtypeStruct((bsz, tp, 256), BF16),
                   jax.ShapeDtypeStruct((bsz, tp, 512), BF16), jax.ShapeDtypeStruct((bsz, tp, 512), BF16),
                   jax.ShapeDtypeStruct((bsz, 2, tp, 128), BF16), jax.ShapeDtypeStruct((bsz, 128, 256), F32),
                   jax.ShapeDtypeStruct((bsz, 8, 256), F32)],
        grid=(bsz, 2),
        in_specs=[pair(256, 0), pair(128, C_Q), pair(128, C_K), pair(256, C_V), pair(256, C_R),
                  pl.BlockSpec((1, tp, 128), lambda bi, p: (bi, 0, C_GD // 128)),
                  pair(256, 0),
                  pl.BlockSpec((1, 2, nchunk, GLA_DV, 128), lambda bi, p: (bi, p, 0, 0, 0)),
                  pl.BlockSpec((128, 128), lambda bi, p: (0, p)),
                  pl.BlockSpec((1, 128), lambda bi, p: (0, p)),
                  pl.BlockSpec((1, 128), lambda bi, p: (0, 0))],
        out_specs=[pair(128, 0), pair(128, 0), pair(256, 0), pair(256, 0),
                   pl.BlockSpec((1, 1, tp, 128), lambda bi, p: (bi, p, 0, 0)),
                   pl.BlockSpec((1, 128, 128), lambda bi, p: (bi, 0, p)),
                   pl.BlockSpec((1, 8, 128), lambda bi, p: (bi, 0, p))],
        scratch_shapes=[pltpu.VMEM((2, GLA_DV, 128), F32), pltpu.VMEM((128, 128), F32)],
        operands=(dgla, u, u, u, u, u, o_pre, states, gup, gbias, gnorm))


_DU_OFFSETS = (C_VAL, C_GATE, C_Q, C_K, C_V, C_R)
_DU_WIDTHS = (512, 512, 256, 256, 512, 512)


def _du_specs(tm, per, row_map):
    specs = [pl.BlockSpec((tm, w), row_map) for w in _DU_WIDTHS]
    for p in range(2):
        specs.append(pl.BlockSpec((1, 1, tm, 128), lambda *ix, p=p: (row_map(*ix)[0] // per, p, row_map(*ix)[0] % per, 0)))
    return specs


def _du_pieces(refs):
    out = [(off, ref[...]) for off, ref in zip(_DU_OFFSETS, refs[:6])]
    dgd = (refs[6][0, 0].astype(F32) + refs[7][0, 0].astype(F32)).astype(BF16)
    out.append((C_GD, dgd))
    return out


def _in_proj_bwd(pieces, dgd, w_int, dt1, tp, fuse=NO_FUSE):
    r, d = dt1.shape
    tm = _row_tile(tp)
    per = tp // tm

    def body(*refs):
        w_ref, dt_ref, o_ref = refs[8:]
        acc = ALPHA * dt_ref[...]
        for off, val in _du_pieces(refs[:8]):
            acc = acc + _nn(val, w_ref[off:off + val.shape[1], :])
        o_ref[...] = acc

    row = lambda i: (i, 0)
    (ds0,), got = _fused_call(
        body, fuse, name="in_proj_bwd", out_shape=[jax.ShapeDtypeStruct((r, d), F32)], grid=(r // tm,),
        in_specs=_du_specs(tm, per, row) + [pl.BlockSpec((D_IN_PAD, d), lambda i: (0, 0)), pl.BlockSpec((tm, d), row)],
        out_specs=[pl.BlockSpec((tm, d), row)], scratch_shapes=[], operands=(*pieces, dgd, dgd, w_int, dt1))
    return ds0, got


def _grad_w_in(pieces, dgd, s0b, tp, fuse=NO_FUSE):
    r, d = s0b.shape
    tk = _reduce_tile(tp, False)
    per = tp // tk

    def body(*refs):
        s_ref, o_ref = refs[8:]

        @pl.when(pl.program_id(0) == 0)
        def _():
            o_ref[...] = jnp.zeros_like(o_ref)

        s = s_ref[...]
        for off, val in _du_pieces(refs[:8]):
            o_ref[off:off + val.shape[1], :] += _tn(val, s)

    row = lambda k: (k, 0)
    (out,), got = _fused_call(
        body, fuse, name="grad_w_in", out_shape=[jax.ShapeDtypeStruct((D_IN_PAD, d), F32)], grid=(r // tk,),
        in_specs=_du_specs(tk, per, row) + [pl.BlockSpec((tk, d), row)],
        out_specs=[pl.BlockSpec((D_IN_PAD, d), lambda k: (0, 0))], scratch_shapes=[],
        operands=(*pieces, dgd, dgd, s0b))
    return out, got


def _ln_in_bwd(ds0, x, meta, g):
    bsz, s, d = x.shape
    tp = s + HEAD
    nh = 2
    sh = s // nh
    rc = min(256, sh)

    def body(ds_ref, x_ref, meta_ref, g_ref, gx_ref, dm_ref, vec_ref):
        h = pl.program_id(1)
        gg = g_ref[...]

        @pl.when(h == 0)
        def _():
            mh, mr = _ln_stats(meta_ref[...])
            dsm = ds_ref[0, PAD:HEAD, :]
            dm_ref[0] = _ln_bwd(dsm * gg, mh, mr)
            vec_ref[0] = jnp.zeros((8, d), F32)
            vec_ref[0, 0:1, :] = jnp.sum(dsm * mh, axis=0, keepdims=True)
            vec_ref[0, 1:2, :] = jnp.sum(dsm, axis=0, keepdims=True)

        def step(i, carry):
            sg, sb = carry
            dst = pl.ds(pl.multiple_of(i * rc, rc), rc)
            src = pl.ds(pl.multiple_of(HEAD + h * sh + i * rc, 64), rc)
            xh, rstd = _ln_stats(x_ref[0, dst, :])
            dsv = ds_ref[0, src, :]
            gx_ref[0, dst, :] = _ln_bwd(dsv * gg, xh, rstd)
            return sg + jnp.sum(dsv * xh, axis=0, keepdims=True), sb + jnp.sum(dsv, axis=0, keepdims=True)

        zero = jnp.zeros((1, d), F32)
        sg, sb = lax.fori_loop(0, sh // rc, step, (zero, zero))
        vec_ref[0, 0:1, :] += sg
        vec_ref[0, 1:2, :] += sb

    return pl.pallas_call(
        body, name="ln_in_bwd",
        out_shape=[jax.ShapeDtypeStruct((bsz, s, d), F32), jax.ShapeDtypeStruct((bsz, N_META, d), F32),
                   jax.ShapeDtypeStruct((bsz, 8, d), F32)],
        grid=(bsz, nh),
        in_specs=[pl.BlockSpec((1, tp, d), lambda bi, hi: (bi, 0, 0)),
                  pl.BlockSpec((1, sh, d), lambda bi, hi: (bi, hi, 0)),
                  pl.BlockSpec((N_META, d), lambda bi, hi: (0, 0)),
                  pl.BlockSpec((1, d), lambda bi, hi: (0, 0))],
        out_specs=[pl.BlockSpec((1, sh, d), lambda bi, hi: (bi, hi, 0)),
                   pl.BlockSpec((1, N_META, d), lambda bi, hi: (bi, 0, 0)),
                   pl.BlockSpec((1, 8, d), lambda bi, hi: (bi, 0, 0))],
        compiler_params=_params(("parallel", "arbitrary")),
    )(ds0, x, meta, g)


def _rows128(a):
    return a.reshape(-1, 128)


def kernel(x, meta_tokens, ln_in_g, ln_in_b, w_in, conv_w, conv_b, conv_ln_g, conv_ln_b, gate_up, gate_bias, gla_norm_g, w_out, ln1_g, ln1_b, w_ff1, w_ff2, ln2_g, ln2_b, loss_target, m_meta_tokens, m_ln_in_g, m_ln_in_b, m_w_in, m_conv_w, m_conv_b, m_conv_ln_g, m_conv_ln_b, m_gate_up, m_gate_bias, m_gla_norm_g, m_w_out, m_ln1_g, m_ln1_b, m_w_ff1, m_w_ff2, m_ln2_g, m_ln2_b, v_meta_tokens, v_ln_in_g, v_ln_in_b, v_w_in, v_conv_w, v_conv_b, v_conv_ln_g, v_conv_ln_b, v_gate_up, v_gate_bias, v_gla_norm_g, v_w_out, v_ln1_g, v_ln1_b, v_w_ff1, v_w_ff2, v_ln2_g, v_ln2_b):
    bsz, seq, d = x.shape
    tp = seq + HEAD
    r = bsz * tp
    xi, yi, ci = _mesh_pos()
    chip = 2 * xi + yi
    c_arr = jnp.reshape(ci, (1,)).astype(jnp.int32)
    pos_arr = jnp.stack([chip, ci]).astype(jnp.int32)

    sh_in = D_IN // 4
    shard_in = jnp.pad(w_in[0].T.astype(BF16), ((0, D_IN_PAD // 4 - sh_in), (0, 0)))
    shard_w1, shard_wout, shard_w2 = w_ff1[0].astype(BF16), w_out[0].astype(BF16), w_ff2[0].astype(BF16)
    small_w = jnp.concatenate([_rows128(meta_tokens), _rows128(conv_w[0]), _rows128(gate_up[0])], axis=0)
    g_small = _gather_small(small_w)
    n_meta_rows, n_cw_rows = N_META * 256 // 128, CONV_WIDTH * 128 // 128
    meta_full = jnp.concatenate([g_small[j, :n_meta_rows].reshape(N_META, 256) for j in range(4)], axis=1)
    convw_full = jnp.concatenate(
        [g_small[j, n_meta_rows:n_meta_rows + n_cw_rows].reshape(CONV_WIDTH, 128) for j in range(4)], axis=1)
    gup_full = jnp.concatenate(
        [g_small[j, n_meta_rows + n_cw_rows:].reshape(GLA_RANK, 64) for j in range(4)], axis=1)
    convw_p = jnp.pad(convw_full, ((0, 1), (0, 0)))
    gup_p = jnp.pad(gup_full, ((0, 128 - GLA_RANK), (0, 0))).astype(BF16)
    ln_in_g2, ln_in_b2 = ln_in_g.reshape(1, d), ln_in_b.reshape(1, d)

    s0, s0b, ((g_int,),) = _ln_in_fwd(x, meta_full, ln_in_g2, ln_in_b2,
                                      [("gather", [(shard_in, 0, D_IN_PAD // 4, None)])])
    (g_int,) = _place_own([g_int], [shard_in])
    w_int = jnp.pad(g_int[:, :sh_in].reshape(D_IN, d), ((0, D_IN_PAD - D_IN), (0, 0)))
    s0f, s0bf = s0.reshape(r, d), s0b.reshape(r, d)
    u, ((w1_buf,),) = _in_proj(s0bf, w_int, [("gather", [(shard_w1, 0, 640, None)])])
    (w1_buf,) = _place_own([w1_buf], [shard_w1])
    u3 = u.reshape(bsz, tp, D_IN_PAD)
    hc, ((w1_buf,),) = _conv_fwd(u3, convw_p, conv_b, [("gather", [(shard_w1, 640, 384, w1_buf)])])
    hc = hc.reshape(r, D_CONV)
    gla_out, o_pre, states, ((g_wout,),) = _gla_fwd(
        u3, gup_p, gate_bias, gla_norm_g, [("gather", [(shard_wout, 0, 256, None)])])
    (g_wout,) = _place_own([g_wout], [shard_wout])
    wout = g_wout.reshape(d, d)
    gla_of = gla_out.reshape(r, 512)
    conv_of, xhat1, rstd1, s1b, ((w2_buf,),) = _out_proj_ln1(
        hc, gla_of, wout, s0f, conv_ln_g, conv_ln_b, ln1_g, ln1_b, [("gather", [(shard_w2, 0, 448, None)])])
    (w2_buf,) = _place_own([w2_buf], [shard_w2])
    w1 = w1_buf
    ra, ((w2_buf,),) = _ffn1(s1b, w1, [("gather", [(shard_w2, 448, 576, w2_buf)])])
    w2 = w2_buf.reshape(D_FF, d)
    dt2, dt2b, acc2 = _ffn2_ln2_loss(ra, w2, xhat1, ln1_g, ln1_b, ln2_g, ln2_b, loss_target, tp)

    def add_pair(g, got, splits=None):
        return _add_pair(g, got, c_arr, splits)

    da = _ffn_bwd_da(dt2b, w2, ra)
    dt1, dt1b, acc1 = _ffn_bwd_ln1(da, w1, dt2, xhat1, rstd1, ln1_g)
    g_w2, _ = _matmul_tn(ra, dt2b, 1024, square_lhs=True, name="grad_w_ff2")
    big_w2 = g_w2.reshape(4, D_FF // 4, d)
    big_w1, ((pair_w2,),) = _grad_w_ff1(s1b, da, [("pair", [big_w2])])
    dhc, dgla, cacc, ((pair_w1,),) = _out_proj_bwd(dt1b, wout, hc, conv_ln_g, conv_ln_b, [("pair", [big_w1])])
    parts_w1, parts_w2 = add_pair(big_w1, pair_w1, [384, 640]), add_pair(big_w2, pair_w2, [256, 384, 384])
    g_wout, ((chip_w1a,),) = _grad_w_out(conv_of, gla_of, dt1b, [("exchange", parts_w1[:1])])
    big_wout = g_wout.reshape(4, d // 4, d)
    dcv, dcg, dcw, ((chip_w1b, chip_w2a), (pair_wout,)) = _conv_bwd(
        dhc.reshape(bsz, tp, D_CONV), u3, convw_p, [("exchange", [parts_w1[1], parts_w2[0]]), ("pair", [big_wout])])
    (part_wout,) = add_pair(big_wout, pair_wout)
    (dq, dk, dv, dr, dgd, dgup, gvec), ((chip_w2b,),) = _gla_bwd(
        dgla.reshape(bsz, tp, 512), u3, o_pre, states, gup_p, gate_bias, gla_norm_g, [("exchange", parts_w2[1:2])])
    pieces = [a.reshape(r, a.shape[-1]) for a in (dcv, dcg, dq, dk, dv, dr)]
    g_wint, ((chip_w2c, chip_wout),) = _grad_w_in(
        pieces, dgd, s0bf, tp, [("exchange", [parts_w2[2], part_wout])])
    big_win = jnp.stack([g_wint[j * sh_in:(j + 1) * sh_in] for j in range(4)])
    (half_w1, half_w2), ((pair_win,),) = _add_chips_many(
        [(big_w1, pair_w1, [chip_w1a, chip_w1b]), (big_w2, pair_w2, [chip_w2a, chip_w2b, chip_w2c])],
        [("pair", [big_win])])
    ds0, ((chip_win,), (sib_w1, sib_w2)) = _in_proj_bwd(
        pieces, dgd, w_int, dt1, tp, [("exchange", add_pair(big_win, pair_win)), ("swap", [half_w1, half_w2])])
    grad_x, dmeta, lvec = _ln_in_bwd(ds0.reshape(bsz, tp, d), x, meta_full, ln_in_g2)

    red, loss = _sum_small(_allgather_small(_pack_small(acc2, acc1, lvec, cacc, gvec, dcw, dgup, dmeta)))

    half_win = _add_chips(big_win, pair_win, [chip_win], pos_arr)
    half_wout = _add_chips(big_wout, pair_wout, [chip_wout], pos_arr)
    sib_win, sib_wout = _pair_swap([half_win, half_wout])
    halves = {"w_in": (half_win, sib_win), "w_out": (half_wout, sib_wout), "w_ff1": (half_w1, sib_w1),
              "w_ff2": (half_w2, sib_w2)}

    grads = {}
    weights = dict(meta_tokens=meta_tokens, ln_in_g=ln_in_g, ln_in_b=ln_in_b, w_in=w_in, conv_w=conv_w, conv_b=conv_b,
                   conv_ln_g=conv_ln_g, conv_ln_b=conv_ln_b, gate_up=gate_up, gate_bias=gate_bias,
                   gla_norm_g=gla_norm_g, w_out=w_out, ln1_g=ln1_g, ln1_b=ln1_b, w_ff1=w_ff1, w_ff2=w_ff2,
                   ln2_g=ln2_g, ln2_b=ln2_b)
    moms = dict(meta_tokens=(m_meta_tokens, v_meta_tokens), ln_in_g=(m_ln_in_g, v_ln_in_g),
                ln_in_b=(m_ln_in_b, v_ln_in_b), w_in=(m_w_in, v_w_in), conv_w=(m_conv_w, v_conv_w),
                conv_b=(m_conv_b, v_conv_b), conv_ln_g=(m_conv_ln_g, v_conv_ln_g),
                conv_ln_b=(m_conv_ln_b, v_conv_ln_b), gate_up=(m_gate_up, v_gate_up),
                gate_bias=(m_gate_bias, v_gate_bias), gla_norm_g=(m_gla_norm_g, v_gla_norm_g),
                w_out=(m_w_out, v_w_out), ln1_g=(m_ln1_g, v_ln1_g), ln1_b=(m_ln1_b, v_ln1_b),
                w_ff1=(m_w_ff1, v_w_ff1), w_ff2=(m_w_ff2, v_w_ff2), ln2_g=(m_ln2_g, v_ln2_g),
                ln2_b=(m_ln2_b, v_ln2_b))
    names = list(weights)
    big_names = ("w_in", "w_out", "w_ff1", "w_ff2")
    delta, new_m, new_v = {}, {}, {}
    for k in big_names:
        to2d = (lambda a: a[0].T) if k == "w_in" else (lambda a: a[0])
        back = (lambda a: a.T[None]) if k == "w_in" else (lambda a: a[None])
        res = _adamw_halves(to2d(weights[k]), *halves[k], to2d(moms[k][0]), to2d(moms[k][1]), c_arr)
        grads[k], delta[k], new_m[k], new_v[k] = [back(a) for a in res]
    small_names = [k for k in names if k not in big_names]
    two = lambda a: a.reshape(-1, a.shape[-1])

    def my_cols(k, rows, width):
        r0 = SMALL_AT[k][0]
        return lax.dynamic_slice(red, (r0, chip * width), (rows, width))

    sharded = {"meta_tokens": my_cols("meta_tokens", N_META, 256), "conv_w": my_cols("conv_w", CONV_WIDTH, 128),
               "gate_up": my_cols("gate_up", GLA_RANK, 64)}
    upd = _adamw_small(red, sharded, {k: (two(weights[k]), two(moms[k][0]), two(moms[k][1])) for k in small_names})
    for k in small_names:
        shp = weights[k].shape
        grads[k], delta[k], new_m[k], new_v[k] = [a.reshape(shp) for a in upd[k]]
    loss = loss.reshape(())

    return (loss, grad_x, *[grads[k] for k in names], *[delta[k] for k in names],
            *[new_m[k] for k in names], *[new_v[k] for k in names])
```

```python
import functools

import jax
import jax.numpy as jnp
from jax import lax
from jax.experimental import pallas as pl
from jax.experimental.pallas import tpu as pltpu

F32 = jnp.float32
BF16 = jnp.bfloat16

D_MODEL = 1024
N_META = 16
D_CONV = 512
CONV_WIDTH = 31
GLA_HEADS = 4
GLA_DV = 128
GLA_DK = 64
GLA_RANK = 16
GLA_TAU = 16.0
CHUNK = 64
D_FF = 4096
LN_EPS = 1e-5
ALPHA = 2.0 ** 0.25
D_IN = 2576
D_IN_PAD = 2688
PAD = CHUNK - N_META
HEAD = PAD + N_META
Q_SCALE = GLA_DK ** -0.5
ADAM_LR, ADAM_B1, ADAM_B2, ADAM_EPS, ADAM_WD, ADAM_STEP = 0.001, 0.9, 0.999, 1e-08, 0.01, 10
HALF = D_MODEL // 2
VMEM_LIMIT = 56 * 1024 * 1024
MESH = pl.DeviceIdType.MESH

C_VAL, C_GATE, C_Q, C_K, C_V, C_R, C_GD = 0, 512, 1024, 1280, 1536, 2048, 2560

SMALL_AT = {"loss": (0, 1, 1024), "ln_in_g": (1, 1, 1024), "ln_in_b": (2, 1, 1024), "ln1_g": (3, 1, 1024),
            "ln1_b": (4, 1, 1024), "ln2_g": (5, 1, 1024), "ln2_b": (6, 1, 1024), "conv_b": (7, 1, 512),
            "conv_ln_g": (8, 1, 512), "conv_ln_b": (9, 1, 512), "gate_bias": (10, 1, 256), "gla_norm_g": (11, 1, 128),
            "conv_w": (16, 32, 512), "gate_up": (48, 16, 256), "meta_tokens": (64, 16, 1024)}
SMALL_ROWS = 80


def _params(sem=None, **kw):
    return pltpu.CompilerParams(dimension_semantics=sem, vmem_limit_bytes=VMEM_LIMIT, **kw)


def _row_tile(tp):
    for t in (704, 352, 192, 64):
        if tp % t == 0:
            return t
    raise ValueError(tp)


def _reduce_tile(tp, big):
    for t in ((2112, 1056, 704) if big else (1056, 704)) + (352, 192, 64):
        if tp % t == 0:
            return t
    raise ValueError(tp)


def _sub_rows(tm):
    return [slice(0, tm)]


def _dot(a, b, dims, precision=None):
    return lax.dot_general(a, b, (dims, ((), ())), preferred_element_type=F32, precision=precision)


def _nn(a, b, **kw):
    return _dot(a, b, ((1,), (0,)), **kw)


def _nt(a, b, **kw):
    return _dot(a, b, ((1,), (1,)), **kw)


def _tn(a, b, **kw):
    return _dot(a, b, ((0,), (0,)), **kw)


def _sigmoid(x):
    return 1.0 / (1.0 + jnp.exp(-x))


def _log_sigmoid(z):
    return jnp.minimum(z, 0.0) - jnp.log(1.0 + jnp.exp(-jnp.abs(z)))


def _ln_stats(t):
    mu = jnp.mean(t, axis=-1, keepdims=True)
    d = t - mu
    var = jnp.mean(d * d, axis=-1, keepdims=True)
    rstd = lax.rsqrt(var + LN_EPS)
    return d * rstd, rstd


def _ln_bwd(dxhat, xhat, rstd):
    m1 = jnp.mean(dxhat, axis=-1, keepdims=True)
    m2 = jnp.mean(dxhat * xhat, axis=-1, keepdims=True)
    return rstd * (dxhat - m1 - xhat * m2)


def _mesh_pos():
    return lax.axis_index("x"), lax.axis_index("y"), lax.axis_index("c")


ANY = pl.BlockSpec(memory_space=pl.ANY)


def _gather_sems(n):
    return [pltpu.SemaphoreType.DMA((n, 3))] * 4


def _gather_steps(ins, outs, sems, ranges=None):
    n = len(ins)
    send, recv, fsend, frecv = sems
    ranges = ranges or [(0, ref.shape[0]) for ref in ins]
    x, y, c = _mesh_pos()
    me = 2 * x + y
    sibling = (x, y, 1 - c)
    chips = [(1 - x, y), (x, 1 - y), (1 - x, 1 - y)]
    chip_idx = [2 * px + py for px, py in chips]
    mine = [pl.ds(pl.multiple_of(r0 + c * (nr // 2), 16), nr // 2) for r0, nr in ranges]
    other = [pl.ds(pl.multiple_of(r0 + (1 - c) * (nr // 2), 16), nr // 2) for r0, nr in ranges]
    pairs = [(a, k) for a in range(n) for k in range(3)]

    def ici(a, k, slab):
        return pltpu.make_async_remote_copy(
            src_ref=ins[a].at[mine[a], :], dst_ref=outs[a].at[slab, mine[a], :],
            send_sem=send.at[a, k], recv_sem=recv.at[a, k], device_id=(*chips[k], c), device_id_type=MESH)

    def forward(a, k, rows):
        blk = outs[a].at[chip_idx[k], rows[a], :]
        return pltpu.make_async_remote_copy(
            src_ref=blk, dst_ref=blk, send_sem=fsend.at[a, k], recv_sem=frecv.at[a, k],
            device_id=sibling, device_id_type=MESH)

    def start():
        for a, k in pairs:
            ici(a, k, me).start()

    def relay():
        for a, k in pairs:
            ici(a, k, chip_idx[k]).wait_recv()
            forward(a, k, mine).start()

    def finish():
        for a, k in pairs:
            forward(a, k, other).wait_recv()
        for a, k in pairs:
            ici(a, k, me).wait_send()
            forward(a, k, mine).wait_send()

    return start, relay, finish


def _place_own(gathered, shards):
    chip = 2 * lax.axis_index("x") + lax.axis_index("y")
    return [lax.dynamic_update_slice(g, s[None], (chip, 0, 0)) for g, s in zip(gathered, shards)]


def _gather_small(small):
    def body(small_in, small_out, ssend, srecv):
        x, y, c = _mesh_pos()
        chips = [(1 - x, y), (x, 1 - y), (1 - x, 1 - y)]

        def small_copy(k, slot):
            return pltpu.make_async_remote_copy(
                src_ref=small_in, dst_ref=small_out.at[slot], send_sem=ssend.at[k], recv_sem=srecv.at[k],
                device_id=(*chips[k], c), device_id_type=MESH)

        for k in range(3):
            small_copy(k, 2 * x + y).start()
        for k, (px, py) in enumerate(chips):
            small_copy(k, 2 * px + py).wait_recv()
        for k in range(3):
            small_copy(k, 2 * x + y).wait_send()

    res = pl.pallas_call(
        body, name="gather_small", out_shape=jax.ShapeDtypeStruct((4,) + small.shape, small.dtype),
        in_specs=[ANY], out_specs=ANY,
        scratch_shapes=[pltpu.SemaphoreType.DMA((3,)), pltpu.SemaphoreType.DMA((3,))],
    )(small)
    return _place_own([res], [small])[0]


def _pair_swap_steps(ins, outs, sems):
    send, recv = sems
    x, y, c = _mesh_pos()
    cps = [pltpu.make_async_remote_copy(
        src_ref=ins[a], dst_ref=outs[a], send_sem=send.at[a], recv_sem=recv.at[a],
        device_id=(x, y, 1 - c), device_id_type=MESH) for a in range(len(ins))]

    def start():
        for cp in cps:
            cp.start()

    def finish():
        for cp in cps:
            cp.wait()

    return start, finish


def _pair_exchange_steps(ins, outs, sems):
    send, recv = sems
    x, y, c = _mesh_pos()
    other = pl.ds(pl.multiple_of((1 - c) * HALF, 128), HALF)
    cps = [pltpu.make_async_remote_copy(
        src_ref=ins[a].at[:, :, other], dst_ref=outs[a], send_sem=send.at[a], recv_sem=recv.at[a],
        device_id=(x, y, 1 - c), device_id_type=MESH) for a in range(len(ins))]

    def start():
        for cp in cps:
            cp.start()

    def finish():
        for cp in cps:
            cp.wait()

    return start, finish


def _pair_exchange_shapes(grads):
    return [jax.ShapeDtypeStruct(g.shape[:2] + (HALF,), g.dtype) for g in grads]


def _pair_exchange(grads):
    n = len(grads)

    def body(*refs):
        start, finish = _pair_exchange_steps(refs[:n], refs[n:2 * n], refs[2 * n:])
        start()
        finish()

    return pl.pallas_call(
        body, name="grad_pair_exchange", out_shape=_pair_exchange_shapes(grads),
        in_specs=[ANY] * n, out_specs=[ANY] * n,
        scratch_shapes=[pltpu.SemaphoreType.DMA((n,)), pltpu.SemaphoreType.DMA((n,))],
    )(*grads)


NO_FUSE = ()


def _fuse_plan(kind, items):
    n = len(items)
    if kind == "gather":
        shards = [it[0] for it in items]
        bufs = [it[3] for it in items if it[3] is not None]
        alias, b = {}, 0
        for a, it in enumerate(items):
            if it[3] is not None:
                alias[n + b] = a
                b += 1
        ranges = [(it[1], it[2]) for it in items]
        return (shards + bufs, [jax.ShapeDtypeStruct((4,) + s.shape, s.dtype) for s in shards], alias, _gather_sems(n),
                lambda i, o, s: _gather_steps(i[:n], o, s, ranges))
    if kind == "exchange":
        return list(items), _chip_exchange_shapes(items), {}, _chip_exchange_sems(n), _chip_exchange_steps
    pair_sems = [pltpu.SemaphoreType.DMA((n,)), pltpu.SemaphoreType.DMA((n,))]
    if kind == "swap":
        return (list(items), [jax.ShapeDtypeStruct(h.shape, h.dtype) for h in items], {}, pair_sems, _pair_swap_steps)
    return list(items), _pair_exchange_shapes(items), {}, pair_sems, _pair_exchange_steps


def _fused_call(body, fuse, *, name, grid, in_specs, out_specs, out_shape, scratch_shapes, operands,
                late_relay=False):
    plans = [_fuse_plan(kind, list(items)) for kind, items in fuse if len(items)]
    n_in, n_out, n_s = len(in_specs), len(out_shape), len(scratch_shapes)
    comm = [a for p in plans for a in p[0]]
    shapes = [s for p in plans for s in p[1]]
    nc, no = len(comm), len(shapes)
    aliases, i_at, o_at = {}, n_in, n_out
    for p in plans:
        for i, o in p[2].items():
            aliases[i_at + i] = o_at + o
        i_at, o_at = i_at + len(p[0]), o_at + len(p[1])

    def wrapped(*refs):
        o0 = n_in + nc
        s0 = o0 + n_out + no
        i_at, o_at, sem_at, steps = n_in, o0 + n_out, s0 + n_s, []
        for p in plans:
            steps.append(p[4](refs[i_at:i_at + len(p[0])], refs[o_at:o_at + len(p[1])], refs[sem_at:sem_at + len(p[3])]))
            i_at, o_at, sem_at = i_at + len(p[0]), o_at + len(p[1]), sem_at + len(p[3])
        first, last = _grid_ends(grid)
        for st in steps:
            pl.when(first)(st[0])
        if not late_relay:
            for st in steps:
                for mid in st[1:-1]:
                    pl.when(last)(mid)
        body(*refs[:n_in], *refs[o0:o0 + n_out], *refs[s0:s0 + n_s])
        for st in steps:
            for step in (st[1:] if late_relay else st[-1:]):
                pl.when(last)(step)

    res = pl.pallas_call(
        wrapped if plans else body, name=name, grid=grid, in_specs=list(in_specs) + [ANY] * nc,
        out_specs=list(out_specs) + [ANY] * no, out_shape=list(out_shape) + shapes,
        scratch_shapes=list(scratch_shapes) + [s for p in plans for s in p[3]], input_output_aliases=aliases,
        compiler_params=_params(("arbitrary",) * len(grid)))(*operands, *comm)
    outs, got, results = list(res[:n_out]), list(res[n_out:]), []
    for p in plans:
        results.append(got[:len(p[1])])
        got = got[len(p[1]):]
    return outs, results


def _chip_exchange_sems(n):
    return [pltpu.SemaphoreType.DMA((n, 3))] * 2


def _chip_exchange_shapes(parts):
    return [jax.ShapeDtypeStruct((3,) + p.shape[1:], p.dtype) for p in parts]


def _chip_exchange_steps(ins, outs, sems):
    send, recv = sems
    x, y, c = _mesh_pos()
    chips = [(1 - x, y), (x, 1 - y), (1 - x, 1 - y)]
    cps = [pltpu.make_async_remote_copy(
        src_ref=ins[a].at[2 * px + py], dst_ref=outs[a].at[k], send_sem=send.at[a, k], recv_sem=recv.at[a, k],
        device_id=(px, py, c), device_id_type=MESH) for a in range(len(ins)) for k, (px, py) in enumerate(chips)]

    def start():
        for cp in cps:
            cp.start()

    def finish():
        for cp in cps:
            cp.wait()

    return start, finish


def _chip_exchange(parts):
    n = len(parts)

    def body(*refs):
        start, finish = _chip_exchange_steps(refs[:n], refs[n:2 * n], refs[2 * n:])
        start()
        finish()

    return pl.pallas_call(
        body, name="grad_chip_exchange", out_shape=_chip_exchange_shapes(parts),
        in_specs=[ANY] * n, out_specs=[ANY] * n, scratch_shapes=_chip_exchange_sems(n),
    )(*parts)


def _pair_swap(halves):
    n = len(halves)

    def body(*refs):
        start, finish = _pair_swap_steps(refs[:n], refs[n:2 * n], refs[2 * n:])
        start()
        finish()

    return pl.pallas_call(
        body, name="grad_pair_swap",
        out_shape=[jax.ShapeDtypeStruct(h.shape, h.dtype) for h in halves],
        in_specs=[ANY] * n, out_specs=[ANY] * n,
        scratch_shapes=[pltpu.SemaphoreType.DMA((n,)), pltpu.SemaphoreType.DMA((n,))],
    )(*halves)


def _allgather_small(pack):
    m_per, ncol = pack.shape

    def body(x_ref, out_ref, send_sems, recv_sems, local_sem):
        x, y, c = _mesh_pos()
        me, sibling = (x, y, c), (x, y, 1 - c)
        chips = [(1 - x, y), (x, 1 - y), (1 - x, 1 - y)]

        def rows(px, py, pc):
            return out_ref.at[pl.ds(pl.multiple_of((4 * px + 2 * py + pc) * m_per, 8), m_per), :]

        def copy(k, block, to, src=None):
            return pltpu.make_async_remote_copy(
                src_ref=rows(*block) if src is None else src, dst_ref=rows(*block),
                send_sem=send_sems.at[k], recv_sem=recv_sems.at[k], device_id=to, device_id_type=MESH)

        mine = pltpu.make_async_copy(x_ref, rows(*me), local_sem)
        mine.start()
        first = [copy(0, me, sibling, src=x_ref)]
        first += [copy(1 + j, me, (*chip, c), src=x_ref) for j, chip in enumerate(chips)]
        for cp in first:
            cp.start()
        passed = [copy(4 + j, (*chip, c), sibling) for j, chip in enumerate(chips)]
        for j, chip in enumerate(chips):
            copy(1 + j, (*chip, c), me).wait_recv()
            passed[j].start()
        copy(0, sibling, me).wait_recv()
        for j, chip in enumerate(chips):
            copy(4 + j, (*chip, 1 - c), me).wait_recv()
        for cp in first + passed:
            cp.wait_send()
        mine.wait()

    return pl.pallas_call(
        body, name="allgather_small",
        out_shape=jax.ShapeDtypeStruct((8 * m_per, ncol), pack.dtype),
        in_specs=[pl.BlockSpec(memory_space=pltpu.VMEM)],
        out_specs=pl.BlockSpec(memory_space=pltpu.VMEM),
        scratch_shapes=[pltpu.SemaphoreType.DMA((7,)), pltpu.SemaphoreType.DMA((7,)), pltpu.SemaphoreType.DMA],
    )(pack)


def _add_pair(g, got, c_arr, splits=None):
    _, rows, _ = g.shape
    splits = splits or [rows]

    def body(c_ref, g_ref, r_ref, *o_refs):
        at = 0
        for o_ref, n in zip(o_refs, splits):
            o_ref[...] = (g_ref[:, at:at + n, :] + r_ref[:, at:at + n, :]).astype(BF16)
            at += n

    return pl.pallas_call(
        body, name="grad_add_pair", out_shape=[jax.ShapeDtypeStruct((4, n, HALF), BF16) for n in splits],
        grid_spec=pltpu.PrefetchScalarGridSpec(
            num_scalar_prefetch=1, grid=(4,),
            in_specs=[pl.BlockSpec((1, rows, HALF), lambda j, c: (j, 0, c[0])),
                      pl.BlockSpec((1, rows, HALF), lambda j, c: (j, 0, 0))],
            out_specs=[pl.BlockSpec((1, n, HALF), lambda j, c: (j, 0, 0)) for n in splits]),
        compiler_params=_params(("arbitrary",)),
    )(c_arr, g, got)


def _add_chips(g, pair_got, chip_gots, pos_arr):
    _, rows, _ = g.shape

    def body(pos_ref, g_ref, p_ref, *refs):
        o_ref, at = refs[-1], 0
        for r_ref in refs[:-1]:
            n = r_ref.shape[1]
            own = g_ref[0, at:at + n, :] + p_ref[0, at:at + n, :]
            o_ref[at:at + n, :] = ((own + r_ref[0].astype(F32)) + r_ref[1].astype(F32)) + r_ref[2].astype(F32)
            at += n

    return pl.pallas_call(
        body, name="grad_add_chips", out_shape=jax.ShapeDtypeStruct((rows, HALF), F32),
        grid_spec=pltpu.PrefetchScalarGridSpec(
            num_scalar_prefetch=1, grid=(1,),
            in_specs=[pl.BlockSpec((1, rows, HALF), lambda i, p: (p[0], 0, p[1])),
                      pl.BlockSpec((1, rows, HALF), lambda i, p: (p[0], 0, 0))]
            + [pl.BlockSpec(t.shape, lambda i, p: (0, 0, 0)) for t in chip_gots],
            out_specs=pl.BlockSpec((rows, HALF), lambda i, p: (0, 0))),
        compiler_params=_params(("arbitrary",)),
    )(pos_arr, g, pair_got, *chip_gots)


def _add_chips_many(items, fuse=NO_FUSE):
    n = len(items)
    rows = [it[0].shape[1] for it in items]
    n_got = [len(it[2]) for it in items]

    def body(*refs):
        g_refs, p_refs = refs[:n], refs[n:2 * n]
        got_refs = refs[2 * n:2 * n + sum(n_got)]
        o_refs = refs[2 * n + sum(n_got):3 * n + sum(n_got)]
        scr = refs[3 * n + sum(n_got):]
        x, y, c = _mesh_pos()
        chip = 2 * x + y
        mine = pl.ds(pl.multiple_of(c * HALF, 128), HALF)
        copies = []
        for a in range(n):
            copies.append((pltpu.make_async_copy(g_refs[a].at[chip, :, mine], scr[2 * a], scr[2 * n].at[2 * a]),
                           pltpu.make_async_copy(p_refs[a].at[chip], scr[2 * a + 1], scr[2 * n].at[2 * a + 1])))
        for cg, cp in copies:
            cg.start()
            cp.start()
        at_ref = 0
        for a in range(n):
            copies[a][0].wait()
            copies[a][1].wait()
            at = 0
            for r_ref in got_refs[at_ref:at_ref + n_got[a]]:
                k = r_ref.shape[1]
                own = scr[2 * a][at:at + k, :] + scr[2 * a + 1][at:at + k, :]
                o_refs[a][at:at + k, :] = ((own + r_ref[0].astype(F32)) + r_ref[1].astype(F32)) + r_ref[2].astype(F32)
                at += k
            at_ref += n_got[a]

    gots = [t for it in items for t in it[2]]
    outs, got = _fused_call(
        body, fuse, name="grad_add_chips_many", grid=(1,),
        in_specs=[ANY] * (2 * n) + [pl.BlockSpec(t.shape, lambda i: (0, 0, 0)) for t in gots],
        out_specs=[pl.BlockSpec((r, HALF), lambda i: (0, 0)) for r in rows],
        out_shape=[jax.ShapeDtypeStruct((r, HALF), F32) for r in rows],
        scratch_shapes=[pltpu.VMEM((r, HALF), F32) for r in rows for _ in range(2)] + [pltpu.SemaphoreType.DMA((2 * n,))],
        operands=(*[it[0] for it in items], *[it[1] for it in items], *gots))
    return outs, got


def _pack_small(acc2, acc1, lvec, cacc, gvec, dcw, dgup, dmeta):
    bsz = lvec.shape[0]

    def body(a2_ref, a1_ref, lv_ref, ca_ref, gv_ref, cw_ref, gu_ref, dm_ref, o_ref):
        def put(name, val):
            r0, nr, nl = SMALL_AT[name]
            o_ref[r0:r0 + nr, 0:nl] = val

        over_b = lambda f: functools.reduce(lambda a, b: a + b, [f(b) for b in range(bsz)])
        o_ref[...] = jnp.zeros_like(o_ref)
        put("loss", a2_ref[0:1, :])
        put("ln2_g", a2_ref[1:2, :])
        put("ln2_b", a2_ref[2:3, :])
        put("ln1_g", a1_ref[0:1, :])
        put("ln1_b", a1_ref[1:2, :])
        put("ln_in_g", over_b(lambda b: lv_ref[b, 0:1, :]))
        put("ln_in_b", over_b(lambda b: lv_ref[b, 1:2, :]))
        put("conv_b", ca_ref[0:1, :])
        put("conv_ln_g", ca_ref[1:2, :])
        put("conv_ln_b", ca_ref[2:3, :])
        put("gate_bias", over_b(lambda b: gv_ref[b, 0:1, :]))
        put("gla_norm_g", over_b(lambda b: gv_ref[b, 1:2, 0:128] + gv_ref[b, 1:2, 128:256]))
        put("conv_w", over_b(lambda b: cw_ref[b]))
        put("gate_up", over_b(lambda b: gu_ref[b, 0:GLA_RANK, :]))
        put("meta_tokens", over_b(lambda b: dm_ref[b]))

    return pl.pallas_call(
        body, name="pack_small", out_shape=jax.ShapeDtypeStruct((SMALL_ROWS, D_MODEL), F32),
    )(acc2, acc1, lvec, cacc, gvec, dcw, dgup, dmeta)


def _sum_small(gathered):
    def body(g_ref, o_ref, loss_ref):
        acc = g_ref[0:SMALL_ROWS, :]
        for d in range(1, 8):
            acc = acc + g_ref[d * SMALL_ROWS:(d + 1) * SMALL_ROWS, :]
        o_ref[...] = acc
        loss_ref[...] = jnp.sum(acc[0:1, :], axis=1, keepdims=True)

    return pl.pallas_call(
        body, name="sum_small",
        out_shape=[jax.ShapeDtypeStruct((SMALL_ROWS, D_MODEL), F32), jax.ShapeDtypeStruct((1, 1), F32)],
    )(gathered)


def _adamw_math(w, g, m, v):
    c1 = 1.0 - ADAM_B1 ** ADAM_STEP
    c2 = 1.0 - ADAM_B2 ** ADAM_STEP
    mn = ADAM_B1 * m + (1.0 - ADAM_B1) * g
    vn = ADAM_B2 * v + (1.0 - ADAM_B2) * (g * g)
    return -ADAM_LR * ((mn / c1) / (jnp.sqrt(vn / c2) + ADAM_EPS) + ADAM_WD * w), mn, vn


def _adamw_small(red, sharded_grads, params):
    names = list(params)
    sharded = [k for k in names if k in sharded_grads]
    n, ns = len(names), len(sharded)

    def body(*refs):
        red_ref, sg_refs, p_refs, o_refs = refs[0], refs[1:1 + ns], refs[1 + ns:1 + ns + 3 * n], refs[1 + ns + 3 * n:]
        for i, k in enumerate(names):
            w_ref, m_ref, v_ref = p_refs[3 * i:3 * i + 3]
            if k in sharded:
                g = sg_refs[sharded.index(k)][...]
            else:
                r0, nr, nl = SMALL_AT[k]
                g = red_ref[r0:r0 + nr, 0:nl]
            dl, mn, vn = _adamw_math(w_ref[...], g, m_ref[...], v_ref[...])
            for o_ref, val in zip(o_refs[4 * i:4 * i + 4], (g, dl, mn, vn)):
                o_ref[...] = val

    flat = [a for k in names for a in params[k]]
    res = pl.pallas_call(
        body, name="adamw_small",
        out_shape=[jax.ShapeDtypeStruct(params[k][0].shape, F32) for k in names for _ in range(4)],
    )(red, *[sharded_grads[k] for k in sharded], *flat)
    return {k: tuple(res[4 * i:4 * i + 4]) for i, k in enumerate(names)}


def _adamw_halves(w, own, sib, m, v, c_arr):
    rows, cols = w.shape
    tr = 256 if rows % 256 == 0 else rows

    def body(c_ref, w_ref, own_ref, sib_ref, m_ref, v_ref, g_ref, d_ref, mo_ref, vo_ref):
        first = c_ref[0] == 0
        own, sib = own_ref[...], sib_ref[...]
        g = jnp.concatenate([jnp.where(first, own, sib), jnp.where(first, sib, own)], axis=1)
        g_ref[...] = g
        d_ref[...], mo_ref[...], vo_ref[...] = _adamw_math(w_ref[...], g, m_ref[...], v_ref[...])

    full = pl.BlockSpec((tr, cols), lambda i, c: (i, 0))
    half = pl.BlockSpec((tr, HALF), lambda i, c: (i, 0))
    return pl.pallas_call(
        body, name="adamw_halves", out_shape=[jax.ShapeDtypeStruct(w.shape, F32)] * 4,
        grid_spec=pltpu.PrefetchScalarGridSpec(
            num_scalar_prefetch=1, grid=(rows // tr,), in_specs=[full, half, half, full, full], out_specs=[full] * 4),
        compiler_params=_params(("parallel",)),
    )(c_arr, w, own, sib, m, v)


def _ln_in_fwd(x, meta, g, b, fuse=NO_FUSE):
    bsz, s, d = x.shape
    tp = s + HEAD
    nh = 2
    sh = s // nh
    rc = min(256, sh)

    def body(x_ref, meta_ref, g_ref, b_ref, s0_ref, s0b_ref):
        h = pl.program_id(1)
        gg, bb = g_ref[...], b_ref[...]

        @pl.when(h == 0)
        def _():
            s0_ref[0, 0:PAD, :] = jnp.zeros((PAD, d), F32)
            s0b_ref[0, 0:PAD, :] = jnp.zeros((PAD, d), BF16)
            mh, _ = _ln_stats(meta_ref[...])
            mv = mh * gg + bb
            s0_ref[0, PAD:HEAD, :] = mv
            s0b_ref[0, PAD:HEAD, :] = mv.astype(BF16)

        def step(i, carry):
            src = pl.ds(pl.multiple_of(i * rc, rc), rc)
            dst = pl.ds(pl.multiple_of(HEAD + h * sh + i * rc, 64), rc)
            xh, _ = _ln_stats(x_ref[0, src, :])
            val = xh * gg + bb
            s0_ref[0, dst, :] = val
            s0b_ref[0, dst, :] = val.astype(BF16)
            return carry

        lax.fori_loop(0, sh // rc, step, 0)

    full = lambda bi, hi: (bi, 0, 0)
    (s0, s0b), got = _fused_call(
        body, fuse, name="ln_in_fwd",
        out_shape=[jax.ShapeDtypeStruct((bsz, tp, d), F32), jax.ShapeDtypeStruct((bsz, tp, d), BF16)],
        grid=(bsz, nh),
        in_specs=[pl.BlockSpec((1, sh, d), lambda bi, hi: (bi, hi, 0)),
                  pl.BlockSpec((N_META, d), lambda bi, hi: (0, 0)),
                  pl.BlockSpec((1, d), lambda bi, hi: (0, 0)),
                  pl.BlockSpec((1, d), lambda bi, hi: (0, 0))],
        out_specs=[pl.BlockSpec((1, tp, d), full)] * 2, scratch_shapes=[], operands=(x, meta, g, b),
        late_relay=True)
    return s0, s0b, got


def _in_proj(s0b, w_int, fuse=NO_FUSE):
    r, d = s0b.shape
    tm = _row_tile(r)

    def body(a_ref, w_ref, o_ref):
        o_ref[...] = _nt(a_ref[...], w_ref[...])

    (u,), got = _fused_call(
        body, fuse, name="in_proj", out_shape=[jax.ShapeDtypeStruct((r, D_IN_PAD), F32)],
        grid=(r // tm,),
        in_specs=[pl.BlockSpec((tm, d), lambda i: (i, 0)), pl.BlockSpec((D_IN_PAD, d), lambda i: (0, 0))],
        out_specs=[pl.BlockSpec((tm, D_IN_PAD), lambda i: (i, 0))], scratch_shapes=[], operands=(s0b, w_int))
    return u, got


def _conv_fwd(u, conv_w, conv_b, fuse=NO_FUSE):
    bsz, tp, _ = u.shape
    nchunk = tp // CHUNK
    win = CHUNK + 32
    nct = D_CONV // 128

    def body(cv_ref, cg_ref, w_ref, cb_ref, hc_ref, h_scr, win_scr):
        h_scr[0:32, :] = jnp.zeros((32, 128), F32)
        h_scr[32:32 + tp, :] = cv_ref[0] * _sigmoid(cg_ref[0])
        cb = cb_ref[...]

        def step(n, carry):
            r0 = pl.multiple_of(n * CHUNK, CHUNK)
            win_scr[...] = h_scr[pl.ds(r0, win), :]
            acc = jnp.zeros((CHUNK, 128), F32)
            for j in range(CONV_WIDTH):
                acc = acc + w_ref[j:j + 1, :] * win_scr[2 + j:2 + j + CHUNK, :]
            hc_ref[0, pl.ds(r0, CHUNK), :] = acc + cb
            return carry

        lax.fori_loop(0, nchunk, step, 0)

    (hc,), got = _fused_call(
        body, fuse, name="conv_fwd", out_shape=[jax.ShapeDtypeStruct((bsz, tp, D_CONV), F32)],
        grid=(bsz, nct),
        in_specs=[pl.BlockSpec((1, tp, 128), lambda bi, t: (bi, 0, C_VAL // 128 + t)),
                  pl.BlockSpec((1, tp, 128), lambda bi, t: (bi, 0, C_GATE // 128 + t)),
                  pl.BlockSpec((32, 128), lambda bi, t: (0, t)),
                  pl.BlockSpec((1, 128), lambda bi, t: (0, t))],
        out_specs=[pl.BlockSpec((1, tp, 128), lambda bi, t: (bi, 0, t))],
        scratch_shapes=[pltpu.VMEM((tp + 32, 128), F32), pltpu.VMEM((win, 128), F32)],
        operands=(u, u, conv_w, conv_b))
    return hc, got


def _gla_group(nchunk):
    return 11 if nchunk % 11 == 0 else nchunk


def _bdot(a, b, ca, cb, precision=None):
    return lax.dot_general(a, b, (((ca,), (cb,)), ((0,), (0,))), preferred_element_type=F32, precision=precision)


def _bnn(a, b, **kw):
    return _bdot(a, b, 2, 1, **kw)


def _bnt(a, b, **kw):
    return _bdot(a, b, 2, 2, **kw)


def _gla_consts(nb):
    row = lax.broadcasted_iota(jnp.int32, (nb, CHUNK, CHUNK), 1)
    col = lax.broadcasted_iota(jnp.int32, (nb, CHUNK, CHUNK), 2)
    lane = lax.broadcasted_iota(jnp.int32, (1, 1, 128), 2)
    return row >= col, row <= col, [lane < GLA_DK, lane >= GLA_DK]


def _gla_group_terms(g, nb, q_ref, k_ref, gd_ref, gup_ref, gb_ref, tril):
    m = nb * CHUNK
    rows = pl.ds(pl.multiple_of(g * m, CHUNK), m)
    z = _nn(gd_ref[0, rows, :].astype(BF16), gup_ref[...]) + gb_ref[...]
    valid = g * m + lax.broadcasted_iota(jnp.int32, (m, 1), 0) >= PAD
    lg = jnp.where(valid, _log_sigmoid(z) * (1.0 / GLA_TAU), 0.0)
    bcum = _bnn(tril.astype(F32), lg.reshape(nb, CHUNK, 128), precision=lax.Precision.HIGHEST)
    blast = bcum[:, CHUNK - 1:CHUNK, :]
    eb = jnp.exp(bcum)
    enb = jnp.exp(-bcum)
    erest = jnp.exp(blast - bcum)
    q = (q_ref[0, rows, :] * Q_SCALE).reshape(nb, CHUNK, 128)
    k = k_ref[0, rows, :].reshape(nb, CHUNK, 128)
    return rows, valid, z, eb, enb, erest, jnp.exp(blast), q * eb, k * enb, k * erest


def _grid_ends(grid):
    ids = [pl.program_id(i) for i in range(len(grid))]
    first = functools.reduce(jnp.logical_and, [i == 0 for i in ids])
    last = functools.reduce(jnp.logical_and, [i == g - 1 for i, g in zip(ids, grid)])
    return first, last


def _gla_fwd(u, gup, gbias, gnorm, fuse=NO_FUSE):
    bsz, tp, _ = u.shape
    nchunk = tp // CHUNK
    nb = _gla_group(nchunk)

    def body(q_ref, k_ref, v_ref, r_ref, gd_ref, gup_ref, gb_ref, gn_ref, out_ref, o_ref, st_ref, s_scr):
        tril, _, hmask = _gla_consts(nb)
        s_scr[...] = jnp.zeros_like(s_scr)
        gn = gn_ref[...]

        def group(g, carry):
            rows, _, _, _, _, _, dec, qe, ke, kd = _gla_group_terms(g, nb, q_ref, k_ref, gd_ref, gup_ref, gb_ref, tril)
            keb, kdb = ke.astype(BF16), kd.astype(BF16)
            for h in range(2):
                cols = slice(h * GLA_DV, (h + 1) * GLA_DV)
                qh = jnp.where(hmask[h], qe, 0.0).astype(BF16)
                vh = v_ref[0, rows, cols].astype(BF16).reshape(nb, CHUNK, GLA_DV)
                a = jnp.where(tril, _bnt(qh, keb), 0.0).astype(BF16)
                st = s_scr[h]
                sts = []
                for n in range(nb):
                    st_ref[0, h, g * nb + n] = st
                    sts.append(st.astype(BF16))
                    st = dec[n] * st + _tn(vh[n], kdb[n])
                s_scr[h] = st
                o = (_bnn(a, vh) + _bnt(qh, jnp.stack(sts))).reshape(nb * CHUNK, GLA_DV)
                o_ref[0, rows, cols] = o
                rms = lax.rsqrt(jnp.mean(o * o, axis=-1, keepdims=True) + LN_EPS)
                rh = r_ref[0, rows, cols]
                out_ref[0, rows, cols] = (o * rms * gn * (rh * _sigmoid(rh))).astype(BF16)
            return carry

        lax.fori_loop(0, nchunk // nb, group, 0)

    res, got = _fused_call(
        body, fuse, name="gla_fwd",
        out_shape=[jax.ShapeDtypeStruct((bsz, tp, 512), BF16), jax.ShapeDtypeStruct((bsz, tp, 512), F32),
                   jax.ShapeDtypeStruct((bsz, GLA_HEADS, nchunk, GLA_DV, 128), F32)],
        grid=(bsz, 2),
        in_specs=[pl.BlockSpec((1, tp, 128), lambda bi, p: (bi, 0, C_Q // 128 + p)),
                  pl.BlockSpec((1, tp, 128), lambda bi, p: (bi, 0, C_K // 128 + p)),
                  pl.BlockSpec((1, tp, 256), lambda bi, p: (bi, 0, C_V // 256 + p)),
                  pl.BlockSpec((1, tp, 256), lambda bi, p: (bi, 0, C_R // 256 + p)),
                  pl.BlockSpec((1, tp, 128), lambda bi, p: (bi, 0, C_GD // 128)),
                  pl.BlockSpec((128, 128), lambda bi, p: (0, p)),
                  pl.BlockSpec((1, 128), lambda bi, p: (0, p)),
                  pl.BlockSpec((1, 128), lambda bi, p: (0, 0))],
        out_specs=[pl.BlockSpec((1, tp, 256), lambda bi, p: (bi, 0, p)),
                   pl.BlockSpec((1, tp, 256), lambda bi, p: (bi, 0, p)),
                   pl.BlockSpec((1, 2, nchunk, GLA_DV, 128), lambda bi, p: (bi, p, 0, 0, 0))],
        scratch_shapes=[pltpu.VMEM((2, GLA_DV, 128), F32)],
        operands=(u, u, u, u, u, gup, gbias, gnorm))
    return res[0], res[1], res[2], got


def _out_proj_ln1(hc, gla_out, w_out, s0, cg, cb, g1, b1, fuse=NO_FUSE):
    r, d = s0.shape
    tm = _row_tile(r)

    def body(hc_ref, a_ref, w_ref, s0_ref, cg_ref, cb_ref, g_ref, b_ref, co_ref, xh_ref, rstd_ref, s1b_ref):
        for rs in _sub_rows(tm):
            xc, _ = _ln_stats(hc_ref[rs, :])
            nv = xc * cg_ref[...] + cb_ref[...]
            co = (nv * _sigmoid(nv)).astype(BF16)
            co_ref[rs, :] = co
            mix = _nn(co, w_ref[0:D_CONV, :]) + _nn(a_ref[rs, :], w_ref[D_CONV:, :])
            xh, rstd = _ln_stats(ALPHA * s0_ref[rs, :] + mix)
            xh_ref[rs, :] = xh
            rstd_ref[rs, :] = rstd
            s1b_ref[rs, :] = (xh * g_ref[...] + b_ref[...]).astype(BF16)

    row = lambda n: pl.BlockSpec((tm, n), lambda i: (i, 0))
    vec = lambda n: pl.BlockSpec((1, n), lambda i: (0, 0))
    res, got = _fused_call(
        body, fuse, name="out_proj_ln1",
        out_shape=[jax.ShapeDtypeStruct((r, D_CONV), BF16), jax.ShapeDtypeStruct((r, d), F32),
                   jax.ShapeDtypeStruct((r, 1), F32), jax.ShapeDtypeStruct((r, d), BF16)],
        grid=(r // tm,),
        in_specs=[row(D_CONV), row(512), pl.BlockSpec((d, d), lambda i: (0, 0)), row(d),
                  vec(D_CONV), vec(D_CONV), vec(d), vec(d)],
        out_specs=[row(D_CONV), row(d), row(1), row(d)], scratch_shapes=[],
        operands=(hc, gla_out, w_out, s0, cg, cb, g1, b1))
    return res[0], res[1], res[2], res[3], got


def _ffn1(s1b, w1, fuse=NO_FUSE):
    r, d = s1b.shape
    tm = _row_tile(r)
    ns, _, wn = w1.shape

    def body(a_ref, w_ref, o_ref):
        a = a_ref[...]
        for j in range(ns):
            o_ref[:, j * wn:(j + 1) * wn] = jnp.maximum(_nn(a, w_ref[j]), 0.0).astype(BF16)

    (ra,), got = _fused_call(
        body, fuse, name="ffn1", out_shape=[jax.ShapeDtypeStruct((r, D_FF), BF16)], grid=(r // tm,),
        in_specs=[pl.BlockSpec((tm, d), lambda i: (i, 0)), pl.BlockSpec(w1.shape, lambda i: (0, 0, 0))],
        out_specs=[pl.BlockSpec((tm, D_FF), lambda i: (i, 0))], scratch_shapes=[], operands=(s1b, w1))
    return ra, got


def _ffn2_ln2_loss(ra, w2, xhat1, g1, b1, g2, b2, tgt, tp):
    r, d = xhat1.shape
    tm = _row_tile(tp)
    per = tp // tm

    def body(ra_ref, w_ref, xh1_ref, g1_ref, b1_ref, g2_ref, b2_ref, tgt_ref, dt_ref, dtb_ref, acc_ref, t_ref, sem):
        i = pl.program_id(0)
        b, j = i // per, i % per

        @pl.when(i == 0)
        def _():
            acc_ref[...] = jnp.zeros_like(acc_ref)

        head_copy = pltpu.make_async_copy(tgt_ref.at[b, pl.ds(0, tm - HEAD), :], t_ref.at[pl.ds(HEAD, tm - HEAD), :], sem)
        body_copy = pltpu.make_async_copy(
            tgt_ref.at[b, pl.ds(pl.multiple_of(jnp.maximum(j * tm - HEAD, 0), 64), tm), :], t_ref, sem)

        @pl.when(j == 0)
        def _():
            t_ref[0:HEAD, :] = jnp.zeros((HEAD, d), F32)
            head_copy.start()

        pl.when(j > 0)(body_copy.start)

        sums = [jnp.zeros((1, d), F32)] * 3
        for rs in _sub_rows(tm):
            rb = ra_ref[rs, :]
            f = _nn(rb * rb, w_ref[...])
            if rs.start == 0:
                pl.when(j == 0)(head_copy.wait)
                pl.when(j > 0)(body_copy.wait)
            s1 = xh1_ref[rs, :] * g1_ref[...] + b1_ref[...]
            xh2, rstd2 = _ln_stats(ALPHA * s1 + f)
            y = xh2 * g2_ref[...] + b2_ref[...]
            rowid = (i % per) * tm + rs.start + lax.broadcasted_iota(jnp.int32, (rs.stop - rs.start, 1), 0)
            e = jnp.where(rowid >= HEAD, y - t_ref[rs, :], 0.0)
            dy = e * (1.0 / d)
            dt2 = _ln_bwd(dy * g2_ref[...], xh2, rstd2)
            dt_ref[rs, :] = dt2
            dtb_ref[rs, :] = dt2.astype(BF16)
            sums = [sums[0] + (0.5 / d) * jnp.sum(e * e, axis=0, keepdims=True),
                    sums[1] + jnp.sum(dy * xh2, axis=0, keepdims=True), sums[2] + jnp.sum(dy, axis=0, keepdims=True)]
        for k in range(3):
            acc_ref[k:k + 1, :] += sums[k]

    row = lambda n: pl.BlockSpec((tm, n), lambda i: (i, 0))
    vec = pl.BlockSpec((1, d), lambda i: (0, 0))
    return pl.pallas_call(
        body, name="ffn2_ln2_loss",
        out_shape=[jax.ShapeDtypeStruct((r, d), F32), jax.ShapeDtypeStruct((r, d), BF16),
                   jax.ShapeDtypeStruct((8, d), F32)],
        grid=(r // tm,),
        in_specs=[row(D_FF), pl.BlockSpec((D_FF, d), lambda i: (0, 0)), row(d), vec, vec, vec, vec, ANY],
        out_specs=[row(d), row(d), pl.BlockSpec((8, d), lambda i: (0, 0))],
        scratch_shapes=[pltpu.VMEM((tm, d), F32), pltpu.SemaphoreType.DMA],
        compiler_params=_params(("arbitrary",)),
    )(ra, w2, xhat1, g1, b1, g2, b2, tgt)


def _ffn_bwd_da(dt2b, w2, ra):
    r, d = dt2b.shape
    tm = _row_tile(r)

    def body(g_ref, w_ref, ra_ref, o_ref):
        o_ref[...] = (_nt(g_ref[...], w_ref[...]) * (2.0 * ra_ref[...].astype(F32))).astype(BF16)

    return pl.pallas_call(
        body, name="ffn_bwd_da", out_shape=jax.ShapeDtypeStruct((r, D_FF), BF16),
        grid=(r // tm,),
        in_specs=[pl.BlockSpec((tm, d), lambda i: (i, 0)), pl.BlockSpec((D_FF, d), lambda i: (0, 0)),
                  pl.BlockSpec((tm, D_FF), lambda i: (i, 0))],
        out_specs=pl.BlockSpec((tm, D_FF), lambda i: (i, 0)),
        compiler_params=_params(("parallel",)),
    )(dt2b, w2, ra)


def _ffn_bwd_ln1(da, w1, dt2, xhat1, rstd1, g1):
    r, d = dt2.shape
    tm = _row_tile(r)

    def body(da_ref, w_ref, dt2_ref, xh_ref, rstd_ref, g_ref, dt_ref, dtb_ref, acc_ref):
        @pl.when(pl.program_id(0) == 0)
        def _():
            acc_ref[...] = jnp.zeros_like(acc_ref)

        sums = [jnp.zeros((1, d), F32)] * 2
        for rs in _sub_rows(tm):
            ds1 = ALPHA * dt2_ref[rs, :]
            for j in range(w1.shape[0]):
                ds1 = ds1 + _nt(da_ref[rs, j * w1.shape[2]:(j + 1) * w1.shape[2]], w_ref[j])
            xh = xh_ref[rs, :]
            dt1 = _ln_bwd(ds1 * g_ref[...], xh, rstd_ref[rs, :])
            dt_ref[rs, :] = dt1
            dtb_ref[rs, :] = dt1.astype(BF16)
            sums = [sums[0] + jnp.sum(ds1 * xh, axis=0, keepdims=True), sums[1] + jnp.sum(ds1, axis=0, keepdims=True)]
        for k in range(2):
            acc_ref[k:k + 1, :] += sums[k]

    row = lambda n: pl.BlockSpec((tm, n), lambda i: (i, 0))
    return pl.pallas_call(
        body, name="ffn_bwd_ln1",
        out_shape=[jax.ShapeDtypeStruct((r, d), F32), jax.ShapeDtypeStruct((r, d), BF16),
                   jax.ShapeDtypeStruct((8, d), F32)],
        grid=(r // tm,),
        in_specs=[row(D_FF), pl.BlockSpec(w1.shape, lambda i: (0, 0, 0)), row(d), row(d), row(1),
                  pl.BlockSpec((1, d), lambda i: (0, 0))],
        out_specs=[row(d), row(d), pl.BlockSpec((8, d), lambda i: (0, 0))],
        compiler_params=_params(("arbitrary",)),
    )(da, w1, dt2, xhat1, rstd1, g1)


def _matmul_tn(lhs, rhs, bm, square_lhs=False, name="matmul_tn", fuse=NO_FUSE):
    r, m = lhs.shape
    n = rhs.shape[1]
    tk = _reduce_tile(r, True)

    def body(a_ref, b_ref, o_ref):
        @pl.when(pl.program_id(1) == 0)
        def _():
            o_ref[...] = jnp.zeros_like(o_ref)

        a = a_ref[...]
        if square_lhs:
            a = a * a
        o_ref[...] += _tn(a, b_ref[...])

    (out,), got = _fused_call(
        body, fuse, name=name, out_shape=[jax.ShapeDtypeStruct((m, n), F32)], grid=(m // bm, r // tk),
        in_specs=[pl.BlockSpec((tk, bm), lambda i, k: (k, i)), pl.BlockSpec((tk, n), lambda i, k: (k, 0))],
        out_specs=[pl.BlockSpec((bm, n), lambda i, k: (i, 0))], scratch_shapes=[], operands=(lhs, rhs))
    return out, got


def _grad_w_ff1(s1b, da, fuse=NO_FUSE):
    r, d = s1b.shape
    wn = da.shape[1] // 4
    tk = _reduce_tile(r, True)

    def body(a_ref, b_ref, o_ref):
        @pl.when(pl.program_id(1) == 0)
        def _():
            o_ref[...] = jnp.zeros_like(o_ref)

        o_ref[0] += _tn(a_ref[...], b_ref[...])

    (out,), got = _fused_call(
        body, fuse, name="grad_w_ff1", out_shape=[jax.ShapeDtypeStruct((4, d, wn), F32)], grid=(4, r // tk),
        in_specs=[pl.BlockSpec((tk, d), lambda j, k: (k, 0)), pl.BlockSpec((tk, wn), lambda j, k: (k, j))],
        out_specs=[pl.BlockSpec((1, d, wn), lambda j, k: (j, 0, 0))], scratch_shapes=[], operands=(s1b, da))
    return out, got


def _grad_w_out(conv_of, gla_of, dt1b, fuse=NO_FUSE):
    r, n = dt1b.shape
    tk = _reduce_tile(r, True)

    def body(a_ref, b_ref, g_ref, o_ref):
        @pl.when(pl.program_id(1) == 0)
        def _():
            o_ref[...] = jnp.zeros_like(o_ref)

        @pl.when(pl.program_id(0) == 0)
        def _():
            o_ref[...] += _tn(a_ref[...], g_ref[...])

        @pl.when(pl.program_id(0) == 1)
        def _():
            o_ref[...] += _tn(b_ref[...], g_ref[...])

    lhs = pl.BlockSpec((tk, 512), lambda i, k: (k, 0))
    (out,), got = _fused_call(
        body, fuse, name="grad_w_out", out_shape=[jax.ShapeDtypeStruct((2 * 512, n), F32)], grid=(2, r // tk),
        in_specs=[lhs, lhs, pl.BlockSpec((tk, n), lambda i, k: (k, 0))],
        out_specs=[pl.BlockSpec((512, n), lambda i, k: (i, 0))], scratch_shapes=[],
        operands=(conv_of, gla_of, dt1b))
    return out, got


def _out_proj_bwd(dt1b, w_out, hc, cg, cb, fuse=NO_FUSE):
    r, d = dt1b.shape
    tm = _row_tile(r)

    def body(g_ref, w_ref, hc_ref, cg_ref, cb_ref, dhc_ref, dgla_ref, acc_ref):
        @pl.when(pl.program_id(0) == 0)
        def _():
            acc_ref[...] = jnp.zeros_like(acc_ref)

        gg = cg_ref[...]
        sums = [jnp.zeros((1, D_CONV), F32)] * 3
        for rs in _sub_rows(tm):
            dmix = _nt(g_ref[rs, :], w_ref[...])
            dgla_ref[rs, :] = dmix[:, D_CONV:]
            xh, rstd = _ln_stats(hc_ref[rs, :])
            nv = xh * gg + cb_ref[...]
            sig = _sigmoid(nv)
            dn = dmix[:, :D_CONV] * (sig * (1.0 + nv * (1.0 - sig)))
            dhc = _ln_bwd(dn * gg, xh, rstd)
            dhc_ref[rs, :] = dhc
            sums = [sums[0] + jnp.sum(dhc, axis=0, keepdims=True), sums[1] + jnp.sum(dn * xh, axis=0, keepdims=True),
                    sums[2] + jnp.sum(dn, axis=0, keepdims=True)]
        for k in range(3):
            acc_ref[k:k + 1, :] += sums[k]

    row = lambda n: pl.BlockSpec((tm, n), lambda i: (i, 0))
    vec = pl.BlockSpec((1, D_CONV), lambda i: (0, 0))
    res, got = _fused_call(
        body, fuse, name="out_proj_bwd",
        out_shape=[jax.ShapeDtypeStruct((r, D_CONV), F32), jax.ShapeDtypeStruct((r, 512), F32),
                   jax.ShapeDtypeStruct((8, D_CONV), F32)],
        grid=(r // tm,),
        in_specs=[row(d), pl.BlockSpec((d, d), lambda i: (0, 0)), row(D_CONV), vec, vec],
        out_specs=[row(D_CONV), row(512), pl.BlockSpec((8, D_CONV), lambda i: (0, 0))], scratch_shapes=[],
        operands=(dt1b, w_out, hc, cg, cb))
    return res[0], res[1], res[2], got


def _conv_bwd(dhc, u, conv_w, fuse=NO_FUSE):
    bsz, tp, _ = u.shape
    nchunk = tp // CHUNK
    win = CHUNK + 32
    nct = D_CONV // 128

    def body(dhc_ref, cv_ref, cg_ref, w_ref, dv_ref, dg_ref, dw_ref, h_scr, dhc_scr, hwin, dwin, dw_scr):
        h_scr[0:32, :] = jnp.zeros((32, 128), F32)
        h_scr[32:32 + tp, :] = cv_ref[0] * _sigmoid(cg_ref[0])
        dhc_scr[0:tp, :] = dhc_ref[0]
        dhc_scr[tp:tp + 32, :] = jnp.zeros((32, 128), F32)
        dw_scr[...] = jnp.zeros_like(dw_scr)

        def step(n, carry):
            r0 = pl.multiple_of(n * CHUNK, CHUNK)
            rows = pl.ds(r0, CHUNK)
            hwin[...] = h_scr[pl.ds(r0, win), :]
            dwin[...] = dhc_scr[pl.ds(r0, win), :]
            dcur = dwin[0:CHUNK, :]
            acc = jnp.zeros((CHUNK, 128), F32)
            for j in range(CONV_WIDTH):
                acc = acc + w_ref[j:j + 1, :] * dwin[30 - j:30 - j + CHUNK, :]
                prod = dcur * hwin[2 + j:2 + j + CHUNK, :]
                dw_scr[j * 8:(j + 1) * 8, :] += jnp.sum(prod.reshape(CHUNK // 8, 8, 128), axis=0)
            cg = cg_ref[0, rows, :]
            sig = _sigmoid(cg)
            rowid = n * CHUNK + lax.broadcasted_iota(jnp.int32, (CHUNK, 1), 0)
            dh = jnp.where(rowid >= PAD, acc, 0.0)
            dv_ref[0, rows, :] = (dh * sig).astype(BF16)
            dg_ref[0, rows, :] = (dh * cv_ref[0, rows, :] * sig * (1.0 - sig)).astype(BF16)
            return carry

        lax.fori_loop(0, nchunk, step, 0)
        dw_ref[0] = jnp.zeros((32, 128), F32)
        for j in range(CONV_WIDTH):
            dw_ref[0, j:j + 1, :] = jnp.sum(dw_scr[j * 8:(j + 1) * 8, :], axis=0, keepdims=True)

    blk = lambda off: pl.BlockSpec((1, tp, 128), lambda bi, t: (bi, 0, off // 128 + t))
    res, got = _fused_call(
        body, fuse, name="conv_bwd",
        out_shape=[jax.ShapeDtypeStruct((bsz, tp, D_CONV), BF16), jax.ShapeDtypeStruct((bsz, tp, D_CONV), BF16),
                   jax.ShapeDtypeStruct((bsz, 32, D_CONV), F32)],
        grid=(bsz, nct),
        in_specs=[blk(0), blk(C_VAL), blk(C_GATE), pl.BlockSpec((32, 128), lambda bi, t: (0, t))],
        out_specs=[blk(0), blk(0), pl.BlockSpec((1, 32, 128), lambda bi, t: (bi, 0, t))],
        scratch_shapes=[pltpu.VMEM((tp + 32, 128), F32), pltpu.VMEM((tp + 32, 128), F32),
                        pltpu.VMEM((win, 128), F32), pltpu.VMEM((win, 128), F32),
                        pltpu.VMEM((CONV_WIDTH * 8, 128), F32)],
        operands=(dhc, u, u, conv_w))
    return res[0], res[1], res[2], got


def _gla_bwd(dgla, u, o_pre, states, gup, gbias, gnorm, fuse=NO_FUSE):
    bsz, tp, _ = u.shape
    nchunk = tp // CHUNK
    nb = _gla_group(nchunk)

    def body(dy_ref, q_ref, k_ref, v_ref, r_ref, gd_ref, o_ref, st_ref, gup_ref, gb_ref, gn_ref,
             dq_ref, dk_ref, dv_ref, dr_ref, dgd_ref, dgup_ref, vec_ref, h_scr, gup_acc):
        tril, triu, hmask = _gla_consts(nb)
        h_scr[...] = jnp.zeros_like(h_scr)
        gup_acc[...] = jnp.zeros_like(gup_acc)
        gn = gn_ref[...]
        gupb = gup_ref[...]
        m = nb * CHUNK
        ngroup = nchunk // nb

        def group(i, carry):
            dbias, dgn = carry
            g = ngroup - 1 - i
            rows, valid, z, eb, enb, erest, dec, qe, ke, kd = _gla_group_terms(
                g, nb, q_ref, k_ref, gd_ref, gup_ref, gb_ref, tril)
            keb, kdb = ke.astype(BF16), kd.astype(BF16)
            dqe = jnp.zeros((nb, CHUNK, 128), F32)
            dke = jnp.zeros((nb, CHUNK, 128), F32)
            dkd = jnp.zeros((nb, CHUNK, 128), F32)
            ddec = jnp.zeros((nb, 1, 128), F32)
            for h in range(2):
                cols = slice(h * GLA_DV, (h + 1) * GLA_DV)
                o = o_ref[0, rows, cols]
                rh = r_ref[0, rows, cols]
                dy = dy_ref[0, rows, cols]
                rms = lax.rsqrt(jnp.mean(o * o, axis=-1, keepdims=True) + LN_EPS)
                nrm = o * rms
                sig = _sigmoid(rh)
                sw = rh * sig
                dr_ref[0, rows, cols] = (dy * nrm * gn * (sig * (1.0 + rh * (1.0 - sig)))).astype(BF16)
                dgn = dgn + jnp.sum(dy * nrm * sw, axis=0, keepdims=True)
                dn = dy * gn * sw
                do = rms * (dn - nrm * jnp.mean(dn * nrm, axis=-1, keepdims=True))
                dob = do.astype(BF16).reshape(nb, CHUNK, GLA_DV)
                qh = jnp.where(hmask[h], qe, 0.0).astype(BF16)
                vh = v_ref[0, rows, cols].astype(BF16).reshape(nb, CHUNK, GLA_DV)
                ht = h_scr[h]
                hts = [None] * nb
                for n in reversed(range(nb)):
                    hts[n] = ht
                    ht = dec[n] * ht + _tn(dob[n], qh[n])
                h_scr[h] = ht
                htf = jnp.stack(hts)
                htb = htf.astype(BF16)
                st = st_ref[0, h, pl.ds(g * nb, nb)]
                at = jnp.where(triu, _bnt(keb, qh), 0.0).astype(BF16)
                da = jnp.where(tril, _bnt(dob, vh), 0.0).astype(BF16)
                dat = jnp.where(triu, _bnt(vh, dob), 0.0).astype(BF16)
                dqe = dqe + jnp.where(hmask[h], _bnn(da, keb) + _bnn(dob, st.astype(BF16)), 0.0)
                dke = dke + _bnn(dat, qh)
                dv_ref[0, rows, cols] = (_bnn(at, dob) + _bnt(kdb, htb)).reshape(m, GLA_DV).astype(BF16)
                dkd = dkd + jnp.where(hmask[h], _bnn(vh, htb), 0.0)
                ddec = ddec + jnp.where(hmask[h], jnp.sum(htf * st, axis=1, keepdims=True), 0.0)
            dq_ref[0, rows, :] = (dqe * eb * Q_SCALE).reshape(m, 128).astype(BF16)
            dk_ref[0, rows, :] = (dke * enb + dkd * erest).reshape(m, 128).astype(BF16)
            db = dqe * qe - dke * ke - dkd * kd
            dblast = jnp.sum(dkd * kd, axis=1, keepdims=True) + ddec * dec
            lastrow = lax.broadcasted_iota(jnp.int32, (1, CHUNK, 1), 1) == CHUNK - 1
            db = db + jnp.where(lastrow, dblast, 0.0)
            dlg = _bnn(triu.astype(F32), db, precision=lax.Precision.HIGHEST).reshape(m, 128)
            dz = jnp.where(valid, dlg, 0.0) * (1.0 / GLA_TAU) * (1.0 - _sigmoid(z))
            dzb = dz.astype(BF16)
            dgd_ref[0, 0, rows, :] = _nt(dzb, gupb).astype(BF16)
            gup_acc[...] += _tn(gd_ref[0, rows, :].astype(BF16), dzb)
            return dbias + jnp.sum(dz, axis=0, keepdims=True), dgn

        zero = jnp.zeros((1, 128), F32)
        dbias, dgn = lax.fori_loop(0, ngroup, group, (zero, zero))
        dgup_ref[0] = gup_acc[...]
        vec_ref[0] = jnp.zeros((8, 128), F32)
        vec_ref[0, 0:1, :] = dbias
        vec_ref[0, 1:2, :] = dgn

    pair = lambda w, off: pl.BlockSpec((1, tp, w), lambda bi, p: (bi, 0, off // w + p))
    return _fused_call(
        body, fuse, name="gla_bwd",
        out_shape=[jax.ShapeDtypeStruct((bsz, tp, 256), BF16), jax.ShapeDtypeStruct((bsz, tp, 256), BF16),
                   jax.ShapeDtypeStruct((bsz, tp, 512), BF16), jax.ShapeDtypeStruct((bsz, tp, 512), BF16),
                   jax.ShapeDtypeStruct((bsz, 2, tp, 128), BF16), jax.ShapeDtypeStruct((bsz, 128, 256), F32),
                   jax.ShapeDtypeStruct((bsz, 8, 256), F32)],
        grid=(bsz, 2),
        in_specs=[pair(256, 0), pair(128, C_Q), pair(128, C_K), pair(256, C_V), pair(256, C_R),
                  pl.BlockSpec((1, tp, 128), lambda bi, p: (bi, 0, C_GD // 128)),
                  pair(256, 0),
                  pl.BlockSpec((1, 2, nchunk, GLA_DV, 128), lambda bi, p: (bi, p, 0, 0, 0)),
                  pl.BlockSpec((128, 128), lambda bi, p: (0, p)),
                  pl.BlockSpec((1, 128), lambda bi, p: (0, p)),
                  pl.BlockSpec((1, 128), lambda bi, p: (0, 0))],
        out_specs=[pair(128, 0), pair(128, 0), pair(256, 0), pair(256, 0),
                   pl.BlockSpec((1, 1, tp, 128), lambda bi, p: (bi, p, 0, 0)),
                   pl.BlockSpec((1, 128, 128), lambda bi, p: (bi, 0, p)),
                   pl.BlockSpec((1, 8, 128), lambda bi, p: (bi, 0, p))],
        scratch_shapes=[pltpu.VMEM((2, GLA_DV, 128), F32), pltpu.VMEM((128, 128), F32)],
        operands=(dgla, u, u, u, u, u, o_pre, states, gup, gbias, gnorm))


_DU_OFFSETS = (C_VAL, C_GATE, C_Q, C_K, C_V, C_R)
_DU_WIDTHS = (512, 512, 256, 256, 512, 512)


def _du_specs(tm, per, row_map):
    specs = [pl.BlockSpec((tm, w), row_map) for w in _DU_WIDTHS]
    for p in range(2):
        specs.append(pl.BlockSpec((1, 1, tm, 128), lambda *ix, p=p: (row_map(*ix)[0] // per, p, row_map(*ix)[0] % per, 0)))
    return specs


def _du_pieces(refs):
    out = [(off, ref[...]) for off, ref in zip(_DU_OFFSETS, refs[:6])]
    dgd = (refs[6][0, 0].astype(F32) + refs[7][0, 0].astype(F32)).astype(BF16)
    out.append((C_GD, dgd))
    return out


def _in_proj_bwd(pieces, dgd, w_int, dt1, tp, fuse=NO_FUSE):
    r, d = dt1.shape
    tm = _row_tile(tp)
    per = tp // tm

    def body(*refs):
        w_ref, dt_ref, o_ref = refs[8:]
        acc = ALPHA * dt_ref[...]
        for off, val in _du_pieces(refs[:8]):
            acc = acc + _nn(val, w_ref[off:off + val.shape[1], :])
        o_ref[...] = acc

    row = lambda i: (i, 0)
    (ds0,), got = _fused_call(
        body, fuse, name="in_proj_bwd", out_shape=[jax.ShapeDtypeStruct((r, d), F32)], grid=(r // tm,),
        in_specs=_du_specs(tm, per, row) + [pl.BlockSpec((D_IN_PAD, d), lambda i: (0, 0)), pl.BlockSpec((tm, d), row)],
        out_specs=[pl.BlockSpec((tm, d), row)], scratch_shapes=[], operands=(*pieces, dgd, dgd, w_int, dt1))
    return ds0, got


def _grad_w_in(pieces, dgd, s0b, tp, fuse=NO_FUSE):
    r, d = s0b.shape
    tk = _reduce_tile(tp, False)
    per = tp // tk

    def body(*refs):
        s_ref, o_ref = refs[8:]

        @pl.when(pl.program_id(0) == 0)
        def _():
            o_ref[...] = jnp.zeros_like(o_ref)

        s = s_ref[...]
        for off, val in _du_pieces(refs[:8]):
            o_ref[off:off + val.shape[1], :] += _tn(val, s)

    row = lambda k: (k, 0)
    (out,), got = _fused_call(
        body, fuse, name="grad_w_in", out_shape=[jax.ShapeDtypeStruct((D_IN_PAD, d), F32)], grid=(r // tk,),
        in_specs=_du_specs(tk, per, row) + [pl.BlockSpec((tk, d), row)],
        out_specs=[pl.BlockSpec((D_IN_PAD, d), lambda k: (0, 0))], scratch_shapes=[],
        operands=(*pieces, dgd, dgd, s0b))
    return out, got


def _ln_in_bwd(ds0, x, meta, g):
    bsz, s, d = x.shape
    tp = s + HEAD
    nh = 2
    sh = s // nh
    rc = min(256, sh)

    def body(ds_ref, x_ref, meta_ref, g_ref, gx_ref, dm_ref, vec_ref):
        h = pl.program_id(1)
        gg = g_ref[...]

        @pl.when(h == 0)
        def _():
            mh, mr = _ln_stats(meta_ref[...])
            dsm = ds_ref[0, PAD:HEAD, :]
            dm_ref[0] = _ln_bwd(dsm * gg, mh, mr)
            vec_ref[0] = jnp.zeros((8, d), F32)
            vec_ref[0, 0:1, :] = jnp.sum(dsm * mh, axis=0, keepdims=True)
            vec_ref[0, 1:2, :] = jnp.sum(dsm, axis=0, keepdims=True)

        def step(i, carry):
            sg, sb = carry
            dst = pl.ds(pl.multiple_of(i * rc, rc), rc)
            src = pl.ds(pl.multiple_of(HEAD + h * sh + i * rc, 64), rc)
            xh, rstd = _ln_stats(x_ref[0, dst, :])
            dsv = ds_ref[0, src, :]
            gx_ref[0, dst, :] = _ln_bwd(dsv * gg, xh, rstd)
            return sg + jnp.sum(dsv * xh, axis=0, keepdims=True), sb + jnp.sum(dsv, axis=0, keepdims=True)

        zero = jnp.zeros((1, d), F32)
        sg, sb = lax.fori_loop(0, sh // rc, step, (zero, zero))
        vec_ref[0, 0:1, :] += sg
        vec_ref[0, 1:2, :] += sb

    return pl.pallas_call(
        body, name="ln_in_bwd",
        out_shape=[jax.ShapeDtypeStruct((bsz, s, d), F32), jax.ShapeDtypeStruct((bsz, N_META, d), F32),
                   jax.ShapeDtypeStruct((bsz, 8, d), F32)],
        grid=(bsz, nh),
        in_specs=[pl.BlockSpec((1, tp, d), lambda bi, hi: (bi, 0, 0)),
                  pl.BlockSpec((1, sh, d), lambda bi, hi: (bi, hi, 0)),
                  pl.BlockSpec((N_META, d), lambda bi, hi: (0, 0)),
                  pl.BlockSpec((1, d), lambda bi, hi: (0, 0))],
        out_specs=[pl.BlockSpec((1, sh, d), lambda bi, hi: (bi, hi, 0)),
                   pl.BlockSpec((1, N_META, d), lambda bi, hi: (bi, 0, 0)),
                   pl.BlockSpec((1, 8, d), lambda bi, hi: (bi, 0, 0))],
        compiler_params=_params(("parallel", "arbitrary")),
    )(ds0, x, meta, g)


def _rows128(a):
    return a.reshape(-1, 128)


def kernel(x, meta_tokens, ln_in_g, ln_in_b, w_in, conv_w, conv_b, conv_ln_g, conv_ln_b, gate_up, gate_bias, gla_norm_g, w_out, ln1_g, ln1_b, w_ff1, w_ff2, ln2_g, ln2_b, loss_target, m_meta_tokens, m_ln_in_g, m_ln_in_b, m_w_in, m_conv_w, m_conv_b, m_conv_ln_g, m_conv_ln_b, m_gate_up, m_gate_bias, m_gla_norm_g, m_w_out, m_ln1_g, m_ln1_b, m_w_ff1, m_w_ff2, m_ln2_g, m_ln2_b, v_meta_tokens, v_ln_in_g, v_ln_in_b, v_w_in, v_conv_w, v_conv_b, v_conv_ln_g, v_conv_ln_b, v_gate_up, v_gate_bias, v_gla_norm_g, v_w_out, v_ln1_g, v_ln1_b, v_w_ff1, v_w_ff2, v_ln2_g, v_ln2_b):
    bsz, seq, d = x.shape
    tp = seq + HEAD
    r = bsz * tp
    xi, yi, ci = _mesh_pos()
    chip = 2 * xi + yi
    c_arr = jnp.reshape(ci, (1,)).astype(jnp.int32)
    pos_arr = jnp.stack([chip, ci]).astype(jnp.int32)

    sh_in = D_IN // 4
    shard_in = jnp.pad(w_in[0].T.astype(BF16), ((0, D_IN_PAD // 4 - sh_in), (0, 0)))
    shard_w1, shard_wout, shard_w2 = w_ff1[0].astype(BF16), w_out[0].astype(BF16), w_ff2[0].astype(BF16)
    small_w = jnp.concatenate([_rows128(meta_tokens), _rows128(conv_w[0]), _rows128(gate_up[0])], axis=0)
    g_small = _gather_small(small_w)
    n_meta_rows, n_cw_rows = N_META * 256 // 128, CONV_WIDTH * 128 // 128
    meta_full = jnp.concatenate([g_small[j, :n_meta_rows].reshape(N_META, 256) for j in range(4)], axis=1)
    convw_full = jnp.concatenate(
        [g_small[j, n_meta_rows:n_meta_rows + n_cw_rows].reshape(CONV_WIDTH, 128) for j in range(4)], axis=1)
    gup_full = jnp.concatenate(
        [g_small[j, n_meta_rows + n_cw_rows:].reshape(GLA_RANK, 64) for j in range(4)], axis=1)
    convw_p = jnp.pad(convw_full, ((0, 1), (0, 0)))
    gup_p = jnp.pad(gup_full, ((0, 128 - GLA_RANK), (0, 0))).astype(BF16)
    ln_in_g2, ln_in_b2 = ln_in_g.reshape(1, d), ln_in_b.reshape(1, d)

    s0, s0b, ((g_int,),) = _ln_in_fwd(x, meta_full, ln_in_g2, ln_in_b2,
                                      [("gather", [(shard_in, 0, D_IN_PAD // 4, None)])])
    (g_int,) = _place_own([g_int], [shard_in])
    w_int = jnp.pad(g_int[:, :sh_in].reshape(D_IN, d), ((0, D_IN_PAD - D_IN), (0, 0)))
    s0f, s0bf = s0.reshape(r, d), s0b.reshape(r, d)
    u, ((w1_buf,),) = _in_proj(s0bf, w_int, [("gather", [(shard_w1, 0, 640, None)])])
    (w1_buf,) = _place_own([w1_buf], [shard_w1])
    u3 = u.reshape(bsz, tp, D_IN_PAD)
    hc, ((w1_buf,),) = _conv_fwd(u3, convw_p, conv_b, [("gather", [(shard_w1, 640, 384, w1_buf)])])
    hc = hc.reshape(r, D_CONV)
    gla_out, o_pre, states, ((g_wout,),) = _gla_fwd(
        u3, gup_p, gate_bias, gla_norm_g, [("gather", [(shard_wout, 0, 256, None)])])
    (g_wout,) = _place_own([g_wout], [shard_wout])
    wout = g_wout.reshape(d, d)
    gla_of = gla_out.reshape(r, 512)
    conv_of, xhat1, rstd1, s1b, ((w2_buf,),) = _out_proj_ln1(
        hc, gla_of, wout, s0f, conv_ln_g, conv_ln_b, ln1_g, ln1_b, [("gather", [(shard_w2, 0, 448, None)])])
    (w2_buf,) = _place_own([w2_buf], [shard_w2])
    w1 = w1_buf
    ra, ((w2_buf,),) = _ffn1(s1b, w1, [("gather", [(shard_w2, 448, 576, w2_buf)])])
    w2 = w2_buf.reshape(D_FF, d)
    dt2, dt2b, acc2 = _ffn2_ln2_loss(ra, w2, xhat1, ln1_g, ln1_b, ln2_g, ln2_b, loss_target, tp)

    def add_pair(g, got, splits=None):
        return _add_pair(g, got, c_arr, splits)

    da = _ffn_bwd_da(dt2b, w2, ra)
    dt1, dt1b, acc1 = _ffn_bwd_ln1(da, w1, dt2, xhat1, rstd1, ln1_g)
    g_w2, _ = _matmul_tn(ra, dt2b, 1024, square_lhs=True, name="grad_w_ff2")
    big_w2 = g_w2.reshape(4, D_FF // 4, d)
    big_w1, ((pair_w2,),) = _grad_w_ff1(s1b, da, [("pair", [big_w2])])
    dhc, dgla, cacc, ((pair_w1,),) = _out_proj_bwd(dt1b, wout, hc, conv_ln_g, conv_ln_b, [("pair", [big_w1])])
    (part_w1,), (part_w2a, part_w2b) = add_pair(big_w1, pair_w1), add_pair(big_w2, pair_w2, [512, 512])
    g_wout, _ = _grad_w_out(conv_of, gla_of, dt1b)
    big_wout = g_wout.reshape(4, d // 4, d)
    dcv, dcg, dcw, ((chip_w1,), (pair_wout,)) = _conv_bwd(
        dhc.reshape(bsz, tp, D_CONV), u3, convw_p, [("exchange", [part_w1]), ("pair", [big_wout])])
    (part_wout,) = add_pair(big_wout, pair_wout)
    (dq, dk, dv, dr, dgd, dgup, gvec), ((chip_w2a,),) = _gla_bwd(
        dgla.reshape(bsz, tp, 512), u3, o_pre, states, gup_p, gate_bias, gla_norm_g, [("exchange", [part_w2a])])
    pieces = [a.reshape(r, a.shape[-1]) for a in (dcv, dcg, dq, dk, dv, dr)]
    g_wint, ((chip_w2b, chip_wout),) = _grad_w_in(
        pieces, dgd, s0bf, tp, [("exchange", [part_w2b, part_wout])])
    big_win = jnp.stack([g_wint[j * sh_in:(j + 1) * sh_in] for j in range(4)])
    (half_w1, half_w2), ((pair_win,),) = _add_chips_many(
        [(big_w1, pair_w1, [chip_w1]), (big_w2, pair_w2, [chip_w2a, chip_w2b])], [("pair", [big_win])])
    ds0, ((chip_win,), (sib_w1, sib_w2)) = _in_proj_bwd(
        pieces, dgd, w_int, dt1, tp, [("exchange", add_pair(big_win, pair_win)), ("swap", [half_w1, half_w2])])
    grad_x, dmeta, lvec = _ln_in_bwd(ds0.reshape(bsz, tp, d), x, meta_full, ln_in_g2)

    red, loss = _sum_small(_allgather_small(_pack_small(acc2, acc1, lvec, cacc, gvec, dcw, dgup, dmeta)))

    half_win = _add_chips(big_win, pair_win, [chip_win], pos_arr)
    half_wout = _add_chips(big_wout, pair_wout, [chip_wout], pos_arr)
    sib_win, sib_wout = _pair_swap([half_win, half_wout])
    halves = {"w_in": (half_win, sib_win), "w_out": (half_wout, sib_wout), "w_ff1": (half_w1, sib_w1),
              "w_ff2": (half_w2, sib_w2)}

    grads = {}
    weights = dict(meta_tokens=meta_tokens, ln_in_g=ln_in_g, ln_in_b=ln_in_b, w_in=w_in, conv_w=conv_w, conv_b=conv_b,
                   conv_ln_g=conv_ln_g, conv_ln_b=conv_ln_b, gate_up=gate_up, gate_bias=gate_bias,
                   gla_norm_g=gla_norm_g, w_out=w_out, ln1_g=ln1_g, ln1_b=ln1_b, w_ff1=w_ff1, w_ff2=w_ff2,
                   ln2_g=ln2_g, ln2_b=ln2_b)
    moms = dict(meta_tokens=(m_meta_tokens, v_meta_tokens), ln_in_g=(m_ln_in_g, v_ln_in_g),
                ln_in_b=(m_ln_in_b, v_ln_in_b), w_in=(m_w_in, v_w_in), conv_w=(m_conv_w, v_conv_w),
                conv_b=(m_conv_b, v_conv_b), conv_ln_g=(m_conv_ln_g, v_conv_ln_g),
                conv_ln_b=(m_conv_ln_b, v_conv_ln_b), gate_up=(m_gate_up, v_gate_up),
                gate_bias=(m_gate_bias, v_gate_bias), gla_norm_g=(m_gla_norm_g, v_gla_norm_g),
                w_out=(m_w_out, v_w_out), ln1_g=(m_ln1_g, v_ln1_g), ln1_b=(m_ln1_b, v_ln1_b),
                w_ff1=(m_w_ff1, v_w_ff1), w_ff2=(m_w_ff2, v_w_ff2), ln2_g=(m_ln2_g, v_ln2_g),
                ln2_b=(m_ln2_b, v_ln2_b))
    names = list(weights)
    big_names = ("w_in", "w_out", "w_ff1", "w_ff2")
    delta, new_m, new_v = {}, {}, {}
    for k in big_names:
        to2d = (lambda a: a[0].T) if k == "w_in" else (lambda a: a[0])
        back = (lambda a: a.T[None]) if k == "w_in" else (lambda a: a[None])
        res = _adamw_halves(to2d(weights[k]), *halves[k], to2d(moms[k][0]), to2d(moms[k][1]), c_arr)
        grads[k], delta[k], new_m[k], new_v[k] = [back(a) for a in res]
    small_names = [k for k in names if k not in big_names]
    two = lambda a: a.reshape(-1, a.shape[-1])

    def my_cols(k, rows, width):
        r0 = SMALL_AT[k][0]
        return lax.dynamic_slice(red, (r0, chip * width), (rows, width))

    sharded = {"meta_tokens": my_cols("meta_tokens", N_META, 256), "conv_w": my_cols("conv_w", CONV_WIDTH, 128),
               "gate_up": my_cols("gate_up", GLA_RANK, 64)}
    upd = _adamw_small(red, sharded, {k: (two(weights[k]), two(moms[k][0]), two(moms[k][1])) for k in small_names})
    for k in small_names:
        shp = weights[k].shape
        grads[k], delta[k], new_m[k], new_v[k] = [a.reshape(shp) for a in upd[k]]
    loss = loss.reshape(())

    return (loss, grad_x, *[grads[k] for k in names], *[delta[k] for k in names],
            *[new_m[k] for k in names], *[new_v[k] for k in names])
```

```python
import functools

import jax
import jax.numpy as jnp
from jax import lax
from jax.experimental import pallas as pl
from jax.experimental.pallas import tpu as pltpu

F32 = jnp.float32
BF16 = jnp.bfloat16

D_MODEL = 1024
N_META = 16
D_CONV = 512
CONV_WIDTH = 31
GLA_HEADS = 4
GLA_DV = 128
GLA_DK = 64
GLA_RANK = 16
GLA_TAU = 16.0
CHUNK = 64
D_FF = 4096
LN_EPS = 1e-5
ALPHA = 2.0 ** 0.25
D_IN = 2576
D_IN_PAD = 2688
PAD = CHUNK - N_META
HEAD = PAD + N_META
Q_SCALE = GLA_DK ** -0.5
ADAM_LR, ADAM_B1, ADAM_B2, ADAM_EPS, ADAM_WD, ADAM_STEP = 0.001, 0.9, 0.999, 1e-08, 0.01, 10
HALF = D_MODEL // 2
VMEM_LIMIT = 56 * 1024 * 1024
MESH = pl.DeviceIdType.MESH

C_VAL, C_GATE, C_Q, C_K, C_V, C_R, C_GD = 0, 512, 1024, 1280, 1536, 2048, 2560

SMALL_AT = {"loss": (0, 1, 0, 1024), "ln_in_g": (1, 1, 0, 1024), "ln_in_b": (2, 1, 0, 1024), "ln1_g": (3, 1, 0, 1024),
            "ln1_b": (4, 1, 0, 1024), "ln2_g": (5, 1, 0, 1024), "ln2_b": (6, 1, 0, 1024), "conv_b": (7, 1, 0, 512),
            "conv_ln_g": (7, 1, 512, 512), "conv_ln_b": (8, 1, 0, 512), "gate_bias": (8, 1, 512, 256),
            "gla_norm_g": (8, 1, 768, 128), "meta_tokens": (40, 16, 0, 1024)}
SMALL_CONV_W, SMALL_GATE_UP = 16, 32
SMALL_ROWS = 56


def _params(sem=None, **kw):
    return pltpu.CompilerParams(dimension_semantics=sem, vmem_limit_bytes=VMEM_LIMIT, **kw)


def _row_tile(tp):
    for t in (704, 352, 192, 64):
        if tp % t == 0:
            return t
    raise ValueError(tp)


def _reduce_tile(tp, big):
    for t in ((2112, 1056, 704) if big else (1056, 704)) + (352, 192, 64):
        if tp % t == 0:
            return t
    raise ValueError(tp)


def _sub_rows(tm):
    return [slice(0, tm)]


def _dot(a, b, dims, precision=None):
    return lax.dot_general(a, b, (dims, ((), ())), preferred_element_type=F32, precision=precision)


def _nn(a, b, **kw):
    return _dot(a, b, ((1,), (0,)), **kw)


def _nt(a, b, **kw):
    return _dot(a, b, ((1,), (1,)), **kw)


def _tn(a, b, **kw):
    return _dot(a, b, ((0,), (0,)), **kw)


def _sigmoid(x):
    return 1.0 / (1.0 + jnp.exp(-x))


def _log_sigmoid(z):
    return jnp.minimum(z, 0.0) - jnp.log(1.0 + jnp.exp(-jnp.abs(z)))


def _ln_stats(t):
    mu = jnp.mean(t, axis=-1, keepdims=True)
    d = t - mu
    var = jnp.mean(d * d, axis=-1, keepdims=True)
    rstd = lax.rsqrt(var + LN_EPS)
    return d * rstd, rstd


def _ln_bwd(dxhat, xhat, rstd):
    m1 = jnp.mean(dxhat, axis=-1, keepdims=True)
    m2 = jnp.mean(dxhat * xhat, axis=-1, keepdims=True)
    return rstd * (dxhat - m1 - xhat * m2)


def _mesh_pos():
    return lax.axis_index("x"), lax.axis_index("y"), lax.axis_index("c")


ANY = pl.BlockSpec(memory_space=pl.ANY)


def _gather_sems(n):
    return [pltpu.SemaphoreType.DMA((n, 3))] * 4


def _gather_steps(ins, outs, sems, ranges=None):
    n = len(ins)
    send, recv, fsend, frecv = sems
    ranges = ranges or [(0, ref.shape[0]) for ref in ins]
    x, y, c = _mesh_pos()
    me = 2 * x + y
    sibling = (x, y, 1 - c)
    chips = [(1 - x, y), (x, 1 - y), (1 - x, 1 - y)]
    chip_idx = [2 * px + py for px, py in chips]
    mine = [pl.ds(pl.multiple_of(r0 + c * (nr // 2), 16), nr // 2) for r0, nr in ranges]
    other = [pl.ds(pl.multiple_of(r0 + (1 - c) * (nr // 2), 16), nr // 2) for r0, nr in ranges]
    pairs = [(a, k) for a in range(n) for k in range(3)]

    def ici(a, k, slab):
        return pltpu.make_async_remote_copy(
            src_ref=ins[a].at[mine[a], :], dst_ref=outs[a].at[slab, mine[a], :],
            send_sem=send.at[a, k], recv_sem=recv.at[a, k], device_id=(*chips[k], c), device_id_type=MESH)

    def forward(a, k, rows):
        blk = outs[a].at[chip_idx[k], rows[a], :]
        return pltpu.make_async_remote_copy(
            src_ref=blk, dst_ref=blk, send_sem=fsend.at[a, k], recv_sem=frecv.at[a, k],
            device_id=sibling, device_id_type=MESH)

    def start():
        for a, k in pairs:
            ici(a, k, me).start()

    def relay():
        for a, k in pairs:
            ici(a, k, chip_idx[k]).wait_recv()
            forward(a, k, mine).start()

    def finish():
        for a, k in pairs:
            forward(a, k, other).wait_recv()
        for a, k in pairs:
            ici(a, k, me).wait_send()
            forward(a, k, mine).wait_send()

    return start, relay, finish


def _place_own(gathered, shards):
    chip = 2 * lax.axis_index("x") + lax.axis_index("y")
    return [lax.dynamic_update_slice(g, s[None], (chip, 0, 0)) for g, s in zip(gathered, shards)]


def _gather_small(small):
    def body(small_in, small_out, ssend, srecv):
        x, y, c = _mesh_pos()
        chips = [(1 - x, y), (x, 1 - y), (1 - x, 1 - y)]

        def small_copy(k, slot):
            return pltpu.make_async_remote_copy(
                src_ref=small_in, dst_ref=small_out.at[slot], send_sem=ssend.at[k], recv_sem=srecv.at[k],
                device_id=(*chips[k], c), device_id_type=MESH)

        for k in range(3):
            small_copy(k, 2 * x + y).start()
        for k, (px, py) in enumerate(chips):
            small_copy(k, 2 * px + py).wait_recv()
        for k in range(3):
            small_copy(k, 2 * x + y).wait_send()

    res = pl.pallas_call(
        body, name="gather_small", out_shape=jax.ShapeDtypeStruct((4,) + small.shape, small.dtype),
        in_specs=[ANY], out_specs=ANY,
        scratch_shapes=[pltpu.SemaphoreType.DMA((3,)), pltpu.SemaphoreType.DMA((3,))],
    )(small)
    return _place_own([res], [small])[0]


def _pair_swap_steps(ins, outs, sems):
    send, recv = sems
    x, y, c = _mesh_pos()
    cps = [pltpu.make_async_remote_copy(
        src_ref=ins[a], dst_ref=outs[a], send_sem=send.at[a], recv_sem=recv.at[a],
        device_id=(x, y, 1 - c), device_id_type=MESH) for a in range(len(ins))]

    def start():
        for cp in cps:
            cp.start()

    def finish():
        for cp in cps:
            cp.wait()

    return start, finish


def _pair_exchange_steps(ins, outs, sems):
    send, recv = sems
    x, y, c = _mesh_pos()
    other = pl.ds(pl.multiple_of((1 - c) * HALF, 128), HALF)
    cps = [pltpu.make_async_remote_copy(
        src_ref=ins[a].at[:, :, other], dst_ref=outs[a], send_sem=send.at[a], recv_sem=recv.at[a],
        device_id=(x, y, 1 - c), device_id_type=MESH) for a in range(len(ins))]

    def start():
        for cp in cps:
            cp.start()

    def finish():
        for cp in cps:
            cp.wait()

    return start, finish


def _pair_exchange_shapes(grads):
    return [jax.ShapeDtypeStruct(g.shape[:2] + (HALF,), g.dtype) for g in grads]


def _pair_exchange(grads):
    n = len(grads)

    def body(*refs):
        start, finish = _pair_exchange_steps(refs[:n], refs[n:2 * n], refs[2 * n:])
        start()
        finish()

    return pl.pallas_call(
        body, name="grad_pair_exchange", out_shape=_pair_exchange_shapes(grads),
        in_specs=[ANY] * n, out_specs=[ANY] * n,
        scratch_shapes=[pltpu.SemaphoreType.DMA((n,)), pltpu.SemaphoreType.DMA((n,))],
    )(*grads)


NO_FUSE = ()


def _fuse_plan(kind, items):
    n = len(items)
    if kind == "gather":
        shards = [it[0] for it in items]
        bufs = [it[3] for it in items if it[3] is not None]
        alias, b = {}, 0
        for a, it in enumerate(items):
            if it[3] is not None:
                alias[n + b] = a
                b += 1
        ranges = [(it[1], it[2]) for it in items]
        return (shards + bufs, [jax.ShapeDtypeStruct((4,) + s.shape, s.dtype) for s in shards], alias, _gather_sems(n),
                lambda i, o, s: _gather_steps(i[:n], o, s, ranges))
    if kind == "exchange":
        return list(items), _chip_exchange_shapes(items), {}, _chip_exchange_sems(n), _chip_exchange_steps
    pair_sems = [pltpu.SemaphoreType.DMA((n,)), pltpu.SemaphoreType.DMA((n,))]
    if kind == "swap":
        return (list(items), [jax.ShapeDtypeStruct(h.shape, h.dtype) for h in items], {}, pair_sems, _pair_swap_steps)
    return list(items), _pair_exchange_shapes(items), {}, pair_sems, _pair_exchange_steps


def _fused_call(body, fuse, *, name, grid, in_specs, out_specs, out_shape, scratch_shapes, operands,
                late_relay=False):
    plans = [_fuse_plan(kind, list(items)) for kind, items in fuse if len(items)]
    n_in, n_out, n_s = len(in_specs), len(out_shape), len(scratch_shapes)
    comm = [a for p in plans for a in p[0]]
    shapes = [s for p in plans for s in p[1]]
    nc, no = len(comm), len(shapes)
    aliases, i_at, o_at = {}, n_in, n_out
    for p in plans:
        for i, o in p[2].items():
            aliases[i_at + i] = o_at + o
        i_at, o_at = i_at + len(p[0]), o_at + len(p[1])

    def wrapped(*refs):
        o0 = n_in + nc
        s0 = o0 + n_out + no
        i_at, o_at, sem_at, steps = n_in, o0 + n_out, s0 + n_s, []
        for p in plans:
            steps.append(p[4](refs[i_at:i_at + len(p[0])], refs[o_at:o_at + len(p[1])], refs[sem_at:sem_at + len(p[3])]))
            i_at, o_at, sem_at = i_at + len(p[0]), o_at + len(p[1]), sem_at + len(p[3])
        first, last = _grid_ends(grid)
        for st in steps:
            pl.when(first)(st[0])
        if not late_relay:
            for st in steps:
                for mid in st[1:-1]:
                    pl.when(last)(mid)
        body(*refs[:n_in], *refs[o0:o0 + n_out], *refs[s0:s0 + n_s])
        for st in steps:
            for step in (st[1:] if late_relay else st[-1:]):
                pl.when(last)(step)

    res = pl.pallas_call(
        wrapped if plans else body, name=name, grid=grid, in_specs=list(in_specs) + [ANY] * nc,
        out_specs=list(out_specs) + [ANY] * no, out_shape=list(out_shape) + shapes,
        scratch_shapes=list(scratch_shapes) + [s for p in plans for s in p[3]], input_output_aliases=aliases,
        compiler_params=_params(("arbitrary",) * len(grid)))(*operands, *comm)
    outs, got, results = list(res[:n_out]), list(res[n_out:]), []
    for p in plans:
        results.append(got[:len(p[1])])
        got = got[len(p[1]):]
    return outs, results


def _chip_exchange_sems(n):
    return [pltpu.SemaphoreType.DMA((n, 3))] * 2


def _chip_exchange_shapes(parts):
    return [jax.ShapeDtypeStruct((3,) + p.shape[1:], p.dtype) for p in parts]


def _chip_exchange_steps(ins, outs, sems):
    send, recv = sems
    x, y, c = _mesh_pos()
    chips = [(1 - x, y), (x, 1 - y), (1 - x, 1 - y)]
    cps = [pltpu.make_async_remote_copy(
        src_ref=ins[a].at[2 * px + py], dst_ref=outs[a].at[k], send_sem=send.at[a, k], recv_sem=recv.at[a, k],
        device_id=(px, py, c), device_id_type=MESH) for a in range(len(ins)) for k, (px, py) in enumerate(chips)]

    def start():
        for cp in cps:
            cp.start()

    def finish():
        for cp in cps:
            cp.wait()

    return start, finish


def _chip_exchange(parts):
    n = len(parts)

    def body(*refs):
        start, finish = _chip_exchange_steps(refs[:n], refs[n:2 * n], refs[2 * n:])
        start()
        finish()

    return pl.pallas_call(
        body, name="grad_chip_exchange", out_shape=_chip_exchange_shapes(parts),
        in_specs=[ANY] * n, out_specs=[ANY] * n, scratch_shapes=_chip_exchange_sems(n),
    )(*parts)


def _pair_swap(halves):
    n = len(halves)

    def body(*refs):
        start, finish = _pair_swap_steps(refs[:n], refs[n:2 * n], refs[2 * n:])
        start()
        finish()

    return pl.pallas_call(
        body, name="grad_pair_swap",
        out_shape=[jax.ShapeDtypeStruct(h.shape, h.dtype) for h in halves],
        in_specs=[ANY] * n, out_specs=[ANY] * n,
        scratch_shapes=[pltpu.SemaphoreType.DMA((n,)), pltpu.SemaphoreType.DMA((n,))],
    )(*halves)


def _allgather_small(pack):
    m_per, ncol = pack.shape

    def body(x_ref, out_ref, send_sems, recv_sems, local_sem):
        x, y, c = _mesh_pos()
        me, sibling = (x, y, c), (x, y, 1 - c)
        chips = [(1 - x, y), (x, 1 - y), (1 - x, 1 - y)]

        def rows(px, py, pc):
            return out_ref.at[pl.ds(pl.multiple_of((4 * px + 2 * py + pc) * m_per, 8), m_per), :]

        def copy(k, block, to, src=None):
            return pltpu.make_async_remote_copy(
                src_ref=rows(*block) if src is None else src, dst_ref=rows(*block),
                send_sem=send_sems.at[k], recv_sem=recv_sems.at[k], device_id=to, device_id_type=MESH)

        mine = pltpu.make_async_copy(x_ref, rows(*me), local_sem)
        mine.start()
        first = [copy(0, me, sibling, src=x_ref)]
        first += [copy(1 + j, me, (*chip, c), src=x_ref) for j, chip in enumerate(chips)]
        for cp in first:
            cp.start()
        passed = [copy(4 + j, (*chip, c), sibling) for j, chip in enumerate(chips)]
        for j, chip in enumerate(chips):
            copy(1 + j, (*chip, c), me).wait_recv()
            passed[j].start()
        copy(0, sibling, me).wait_recv()
        for j, chip in enumerate(chips):
            copy(4 + j, (*chip, 1 - c), me).wait_recv()
        for cp in first + passed:
            cp.wait_send()
        mine.wait()

    return pl.pallas_call(
        body, name="allgather_small",
        out_shape=jax.ShapeDtypeStruct((8 * m_per, ncol), pack.dtype),
        in_specs=[pl.BlockSpec(memory_space=pltpu.VMEM)],
        out_specs=pl.BlockSpec(memory_space=pltpu.VMEM),
        scratch_shapes=[pltpu.SemaphoreType.DMA((7,)), pltpu.SemaphoreType.DMA((7,)), pltpu.SemaphoreType.DMA],
    )(pack)


def _add_pair(g, got, c_arr, splits=None):
    _, rows, _ = g.shape
    splits = splits or [rows]

    def body(c_ref, g_ref, r_ref, *o_refs):
        at = 0
        for o_ref, n in zip(o_refs, splits):
            o_ref[...] = (g_ref[:, at:at + n, :] + r_ref[:, at:at + n, :]).astype(BF16)
            at += n

    return pl.pallas_call(
        body, name="grad_add_pair", out_shape=[jax.ShapeDtypeStruct((4, n, HALF), BF16) for n in splits],
        grid_spec=pltpu.PrefetchScalarGridSpec(
            num_scalar_prefetch=1, grid=(4,),
            in_specs=[pl.BlockSpec((1, rows, HALF), lambda j, c: (j, 0, c[0])),
                      pl.BlockSpec((1, rows, HALF), lambda j, c: (j, 0, 0))],
            out_specs=[pl.BlockSpec((1, n, HALF), lambda j, c: (j, 0, 0)) for n in splits]),
        compiler_params=_params(("arbitrary",)),
    )(c_arr, g, got)


def _add_chips(g, pair_got, chip_gots, pos_arr):
    _, rows, _ = g.shape

    def body(pos_ref, g_ref, p_ref, *refs):
        o_ref, at = refs[-1], 0
        for r_ref in refs[:-1]:
            n = r_ref.shape[1]
            own = g_ref[0, at:at + n, :] + p_ref[0, at:at + n, :]
            o_ref[at:at + n, :] = ((own + r_ref[0].astype(F32)) + r_ref[1].astype(F32)) + r_ref[2].astype(F32)
            at += n

    return pl.pallas_call(
        body, name="grad_add_chips", out_shape=jax.ShapeDtypeStruct((rows, HALF), F32),
        grid_spec=pltpu.PrefetchScalarGridSpec(
            num_scalar_prefetch=1, grid=(1,),
            in_specs=[pl.BlockSpec((1, rows, HALF), lambda i, p: (p[0], 0, p[1])),
                      pl.BlockSpec((1, rows, HALF), lambda i, p: (p[0], 0, 0))]
            + [pl.BlockSpec(t.shape, lambda i, p: (0, 0, 0)) for t in chip_gots],
            out_specs=pl.BlockSpec((rows, HALF), lambda i, p: (0, 0))),
        compiler_params=_params(("arbitrary",)),
    )(pos_arr, g, pair_got, *chip_gots)


def _add_chips_many(items, fuse=NO_FUSE):
    n = len(items)
    rows = [it[0].shape[1] for it in items]
    n_got = [len(it[2]) for it in items]

    def body(*refs):
        g_refs, p_refs = refs[:n], refs[n:2 * n]
        got_refs = refs[2 * n:2 * n + sum(n_got)]
        o_refs = refs[2 * n + sum(n_got):3 * n + sum(n_got)]
        scr = refs[3 * n + sum(n_got):]
        x, y, c = _mesh_pos()
        chip = 2 * x + y
        mine = pl.ds(pl.multiple_of(c * HALF, 128), HALF)
        copies = []
        for a in range(n):
            copies.append((pltpu.make_async_copy(g_refs[a].at[chip, :, mine], scr[2 * a], scr[2 * n].at[2 * a]),
                           pltpu.make_async_copy(p_refs[a].at[chip], scr[2 * a + 1], scr[2 * n].at[2 * a + 1])))
        for cg, cp in copies:
            cg.start()
            cp.start()
        at_ref = 0
        for a in range(n):
            copies[a][0].wait()
            copies[a][1].wait()
            at = 0
            for r_ref in got_refs[at_ref:at_ref + n_got[a]]:
                k = r_ref.shape[1]
                own = scr[2 * a][at:at + k, :] + scr[2 * a + 1][at:at + k, :]
                o_refs[a][at:at + k, :] = ((own + r_ref[0].astype(F32)) + r_ref[1].astype(F32)) + r_ref[2].astype(F32)
                at += k
            at_ref += n_got[a]

    gots = [t for it in items for t in it[2]]
    outs, got = _fused_call(
        body, fuse, name="grad_add_chips_many", grid=(1,),
        in_specs=[ANY] * (2 * n) + [pl.BlockSpec(t.shape, lambda i: (0, 0, 0)) for t in gots],
        out_specs=[pl.BlockSpec((r, HALF), lambda i: (0, 0)) for r in rows],
        out_shape=[jax.ShapeDtypeStruct((r, HALF), F32) for r in rows],
        scratch_shapes=[pltpu.VMEM((r, HALF), F32) for r in rows for _ in range(2)] + [pltpu.SemaphoreType.DMA((2 * n,))],
        operands=(*[it[0] for it in items], *[it[1] for it in items], *gots))
    return outs, got


def _pack_small(acc2, acc1, lvec, cacc, gvec, dcw, dgup, dmeta):
    bsz = lvec.shape[0]

    def body(a2_ref, a1_ref, lv_ref, ca_ref, gv_ref, cw_ref, gu_ref, dm_ref, o_ref):
        def put(name, val):
            r0, nr, l0, nl = SMALL_AT[name]
            o_ref[r0:r0 + nr, l0:l0 + nl] = val

        def put_folded(row0, val, per_row):
            w = val.shape[1]
            for i in range(val.shape[0]):
                o_ref[row0 + i // per_row:row0 + i // per_row + 1, (i % per_row) * w:(i % per_row + 1) * w] = val[i:i + 1]

        over_b = lambda f: functools.reduce(lambda a, b: a + b, [f(b) for b in range(bsz)])
        o_ref[...] = jnp.zeros_like(o_ref)
        put("loss", a2_ref[0:1, :])
        put("ln2_g", a2_ref[1:2, :])
        put("ln2_b", a2_ref[2:3, :])
        put("ln1_g", a1_ref[0:1, :])
        put("ln1_b", a1_ref[1:2, :])
        put("ln_in_g", over_b(lambda b: lv_ref[b, 0:1, :]))
        put("ln_in_b", over_b(lambda b: lv_ref[b, 1:2, :]))
        put("conv_b", ca_ref[0:1, :])
        put("conv_ln_g", ca_ref[1:2, :])
        put("conv_ln_b", ca_ref[2:3, :])
        put("gate_bias", over_b(lambda b: gv_ref[b, 0:1, :]))
        put("gla_norm_g", over_b(lambda b: gv_ref[b, 1:2, 0:128] + gv_ref[b, 1:2, 128:256]))
        put_folded(SMALL_CONV_W, over_b(lambda b: cw_ref[b]), D_MODEL // D_CONV)
        put_folded(SMALL_GATE_UP, over_b(lambda b: gu_ref[b, 0:GLA_RANK, :]), D_MODEL // 256)
        put("meta_tokens", over_b(lambda b: dm_ref[b]))

    return pl.pallas_call(
        body, name="pack_small", out_shape=jax.ShapeDtypeStruct((SMALL_ROWS, D_MODEL), F32),
    )(acc2, acc1, lvec, cacc, gvec, dcw, dgup, dmeta)


def _sum_small(gathered):
    def body(g_ref, o_ref, loss_ref):
        acc = g_ref[0:SMALL_ROWS, :]
        for d in range(1, 8):
            acc = acc + g_ref[d * SMALL_ROWS:(d + 1) * SMALL_ROWS, :]
        o_ref[...] = acc
        loss_ref[...] = jnp.sum(acc[0:1, :], axis=1, keepdims=True)

    return pl.pallas_call(
        body, name="sum_small",
        out_shape=[jax.ShapeDtypeStruct((SMALL_ROWS, D_MODEL), F32), jax.ShapeDtypeStruct((1, 1), F32)],
    )(gathered)


def _adamw_math(w, g, m, v):
    c1 = 1.0 - ADAM_B1 ** ADAM_STEP
    c2 = 1.0 - ADAM_B2 ** ADAM_STEP
    mn = ADAM_B1 * m + (1.0 - ADAM_B1) * g
    vn = ADAM_B2 * v + (1.0 - ADAM_B2) * (g * g)
    return -ADAM_LR * ((mn / c1) / (jnp.sqrt(vn / c2) + ADAM_EPS) + ADAM_WD * w), mn, vn


def _adamw_small(red, sharded_grads, params):
    names = list(params)
    sharded = [k for k in names if k in sharded_grads]
    n, ns = len(names), len(sharded)

    def body(*refs):
        red_ref, sg_refs, p_refs, o_refs = refs[0], refs[1:1 + ns], refs[1 + ns:1 + ns + 3 * n], refs[1 + ns + 3 * n:]
        for i, k in enumerate(names):
            w_ref, m_ref, v_ref = p_refs[3 * i:3 * i + 3]
            if k in sharded:
                g = sg_refs[sharded.index(k)][...]
            else:
                r0, nr, l0, nl = SMALL_AT[k]
                g = red_ref[r0:r0 + nr, l0:l0 + nl]
            dl, mn, vn = _adamw_math(w_ref[...], g, m_ref[...], v_ref[...])
            for o_ref, val in zip(o_refs[4 * i:4 * i + 4], (g, dl, mn, vn)):
                o_ref[...] = val

    flat = [a for k in names for a in params[k]]
    res = pl.pallas_call(
        body, name="adamw_small",
        out_shape=[jax.ShapeDtypeStruct(params[k][0].shape, F32) for k in names for _ in range(4)],
    )(red, *[sharded_grads[k] for k in sharded], *flat)
    return {k: tuple(res[4 * i:4 * i + 4]) for i, k in enumerate(names)}


def _adamw_halves(w, own, sib, m, v, c_arr):
    rows, cols = w.shape
    tr = 256 if rows % 256 == 0 else rows

    def body(c_ref, w_ref, own_ref, sib_ref, m_ref, v_ref, g_ref, d_ref, mo_ref, vo_ref):
        first = c_ref[0] == 0
        own, sib = own_ref[...], sib_ref[...]
        g = jnp.concatenate([jnp.where(first, own, sib), jnp.where(first, sib, own)], axis=1)
        g_ref[...] = g
        d_ref[...], mo_ref[...], vo_ref[...] = _adamw_math(w_ref[...], g, m_ref[...], v_ref[...])

    full = pl.BlockSpec((tr, cols), lambda i, c: (i, 0))
    half = pl.BlockSpec((tr, HALF), lambda i, c: (i, 0))
    return pl.pallas_call(
        body, name="adamw_halves", out_shape=[jax.ShapeDtypeStruct(w.shape, F32)] * 4,
        grid_spec=pltpu.PrefetchScalarGridSpec(
            num_scalar_prefetch=1, grid=(rows // tr,), in_specs=[full, half, half, full, full], out_specs=[full] * 4),
        compiler_params=_params(("parallel",)),
    )(c_arr, w, own, sib, m, v)


def _ln_in_fwd(x, meta, g, b, fuse=NO_FUSE):
    bsz, s, d = x.shape
    tp = s + HEAD
    nh = 2
    sh = s // nh
    rc = min(256, sh)

    def body(x_ref, meta_ref, g_ref, b_ref, s0_ref, s0b_ref):
        h = pl.program_id(1)
        gg, bb = g_ref[...], b_ref[...]

        @pl.when(h == 0)
        def _():
            s0_ref[0, 0:PAD, :] = jnp.zeros((PAD, d), F32)
            s0b_ref[0, 0:PAD, :] = jnp.zeros((PAD, d), BF16)
            mh, _ = _ln_stats(meta_ref[...])
            mv = mh * gg + bb
            s0_ref[0, PAD:HEAD, :] = mv
            s0b_ref[0, PAD:HEAD, :] = mv.astype(BF16)

        def step(i, carry):
            src = pl.ds(pl.multiple_of(i * rc, rc), rc)
            dst = pl.ds(pl.multiple_of(HEAD + h * sh + i * rc, 64), rc)
            xh, _ = _ln_stats(x_ref[0, src, :])
            val = xh * gg + bb
            s0_ref[0, dst, :] = val
            s0b_ref[0, dst, :] = val.astype(BF16)
            return carry

        lax.fori_loop(0, sh // rc, step, 0)

    full = lambda bi, hi: (bi, 0, 0)
    (s0, s0b), got = _fused_call(
        body, fuse, name="ln_in_fwd",
        out_shape=[jax.ShapeDtypeStruct((bsz, tp, d), F32), jax.ShapeDtypeStruct((bsz, tp, d), BF16)],
        grid=(bsz, nh),
        in_specs=[pl.BlockSpec((1, sh, d), lambda bi, hi: (bi, hi, 0)),
                  pl.BlockSpec((N_META, d), lambda bi, hi: (0, 0)),
                  pl.BlockSpec((1, d), lambda bi, hi: (0, 0)),
                  pl.BlockSpec((1, d), lambda bi, hi: (0, 0))],
        out_specs=[pl.BlockSpec((1, tp, d), full)] * 2, scratch_shapes=[], operands=(x, meta, g, b),
        late_relay=True)
    return s0, s0b, got


def _in_proj(s0b, w_int, fuse=NO_FUSE):
    r, d = s0b.shape
    tm = _row_tile(r)

    def body(a_ref, w_ref, o_ref):
        o_ref[...] = _nt(a_ref[...], w_ref[...])

    (u,), got = _fused_call(
        body, fuse, name="in_proj", out_shape=[jax.ShapeDtypeStruct((r, D_IN_PAD), F32)],
        grid=(r // tm,),
        in_specs=[pl.BlockSpec((tm, d), lambda i: (i, 0)), pl.BlockSpec((D_IN_PAD, d), lambda i: (0, 0))],
        out_specs=[pl.BlockSpec((tm, D_IN_PAD), lambda i: (i, 0))], scratch_shapes=[], operands=(s0b, w_int))
    return u, got


def _conv_fwd(u, conv_w, conv_b, fuse=NO_FUSE):
    bsz, tp, _ = u.shape
    nchunk = tp // CHUNK
    win = CHUNK + 32
    nct = D_CONV // 128

    def body(cv_ref, cg_ref, w_ref, cb_ref, hc_ref, h_scr, win_scr):
        h_scr[0:32, :] = jnp.zeros((32, 128), F32)
        h_scr[32:32 + tp, :] = cv_ref[0] * _sigmoid(cg_ref[0])
        cb = cb_ref[...]

        def step(n, carry):
            r0 = pl.multiple_of(n * CHUNK, CHUNK)
            win_scr[...] = h_scr[pl.ds(r0, win), :]
            acc = jnp.zeros((CHUNK, 128), F32)
            for j in range(CONV_WIDTH):
                acc = acc + w_ref[j:j + 1, :] * win_scr[2 + j:2 + j + CHUNK, :]
            hc_ref[0, pl.ds(r0, CHUNK), :] = acc + cb
            return carry

        lax.fori_loop(0, nchunk, step, 0)

    (hc,), got = _fused_call(
        body, fuse, name="conv_fwd", out_shape=[jax.ShapeDtypeStruct((bsz, tp, D_CONV), F32)],
        grid=(bsz, nct),
        in_specs=[pl.BlockSpec((1, tp, 128), lambda bi, t: (bi, 0, C_VAL // 128 + t)),
                  pl.BlockSpec((1, tp, 128), lambda bi, t: (bi, 0, C_GATE // 128 + t)),
                  pl.BlockSpec((32, 128), lambda bi, t: (0, t)),
                  pl.BlockSpec((1, 128), lambda bi, t: (0, t))],
        out_specs=[pl.BlockSpec((1, tp, 128), lambda bi, t: (bi, 0, t))],
        scratch_shapes=[pltpu.VMEM((tp + 32, 128), F32), pltpu.VMEM((win, 128), F32)],
        operands=(u, u, conv_w, conv_b))
    return hc, got


def _gla_group(nchunk):
    return 11 if nchunk % 11 == 0 else nchunk


def _bdot(a, b, ca, cb, precision=None):
    return lax.dot_general(a, b, (((ca,), (cb,)), ((0,), (0,))), preferred_element_type=F32, precision=precision)


def _bnn(a, b, **kw):
    return _bdot(a, b, 2, 1, **kw)


def _bnt(a, b, **kw):
    return _bdot(a, b, 2, 2, **kw)


def _gla_consts(nb):
    row = lax.broadcasted_iota(jnp.int32, (nb, CHUNK, CHUNK), 1)
    col = lax.broadcasted_iota(jnp.int32, (nb, CHUNK, CHUNK), 2)
    lane = lax.broadcasted_iota(jnp.int32, (1, 1, 128), 2)
    return row >= col, row <= col, [lane < GLA_DK, lane >= GLA_DK]


def _gla_group_terms(g, nb, q_ref, k_ref, gd_ref, gup_ref, gb_ref, tril):
    m = nb * CHUNK
    rows = pl.ds(pl.multiple_of(g * m, CHUNK), m)
    z = _nn(gd_ref[0, rows, :].astype(BF16), gup_ref[...]) + gb_ref[...]
    valid = g * m + lax.broadcasted_iota(jnp.int32, (m, 1), 0) >= PAD
    lg = jnp.where(valid, _log_sigmoid(z) * (1.0 / GLA_TAU), 0.0)
    bcum = _bnn(tril.astype(F32), lg.reshape(nb, CHUNK, 128), precision=lax.Precision.HIGHEST)
    blast = bcum[:, CHUNK - 1:CHUNK, :]
    eb = jnp.exp(bcum)
    enb = jnp.exp(-bcum)
    erest = jnp.exp(blast - bcum)
    q = (q_ref[0, rows, :] * Q_SCALE).reshape(nb, CHUNK, 128)
    k = k_ref[0, rows, :].reshape(nb, CHUNK, 128)
    return rows, valid, z, eb, enb, erest, jnp.exp(blast), q * eb, k * enb, k * erest


def _grid_ends(grid):
    ids = [pl.program_id(i) for i in range(len(grid))]
    first = functools.reduce(jnp.logical_and, [i == 0 for i in ids])
    last = functools.reduce(jnp.logical_and, [i == g - 1 for i, g in zip(ids, grid)])
    return first, last


def _gla_fwd(u, gup, gbias, gnorm, fuse=NO_FUSE):
    bsz, tp, _ = u.shape
    nchunk = tp // CHUNK
    nb = _gla_group(nchunk)

    def body(q_ref, k_ref, v_ref, r_ref, gd_ref, gup_ref, gb_ref, gn_ref, out_ref, o_ref, st_ref, s_scr):
        tril, _, hmask = _gla_consts(nb)
        s_scr[...] = jnp.zeros_like(s_scr)
        gn = gn_ref[...]

        def group(g, carry):
            rows, _, _, _, _, _, dec, qe, ke, kd = _gla_group_terms(g, nb, q_ref, k_ref, gd_ref, gup_ref, gb_ref, tril)
            keb, kdb = ke.astype(BF16), kd.astype(BF16)
            for h in range(2):
                cols = slice(h * GLA_DV, (h + 1) * GLA_DV)
                qh = jnp.where(hmask[h], qe, 0.0).astype(BF16)
                vh = v_ref[0, rows, cols].astype(BF16).reshape(nb, CHUNK, GLA_DV)
                a = jnp.where(tril, _bnt(qh, keb), 0.0).astype(BF16)
                st = s_scr[h]
                sts = []
                for n in range(nb):
                    st_ref[0, h, g * nb + n] = st
                    sts.append(st.astype(BF16))
                    st = dec[n] * st + _tn(vh[n], kdb[n])
                s_scr[h] = st
                o = (_bnn(a, vh) + _bnt(qh, jnp.stack(sts))).reshape(nb * CHUNK, GLA_DV)
                o_ref[0, rows, cols] = o
                rms = lax.rsqrt(jnp.mean(o * o, axis=-1, keepdims=True) + LN_EPS)
                rh = r_ref[0, rows, cols]
                out_ref[0, rows, cols] = (o * rms * gn * (rh * _sigmoid(rh))).astype(BF16)
            return carry

        lax.fori_loop(0, nchunk // nb, group, 0)

    res, got = _fused_call(
        body, fuse, name="gla_fwd",
        out_shape=[jax.ShapeDtypeStruct((bsz, tp, 512), BF16), jax.ShapeDtypeStruct((bsz, tp, 512), F32),
                   jax.ShapeDtypeStruct((bsz, GLA_HEADS, nchunk, GLA_DV, 128), F32)],
        grid=(bsz, 2),
        in_specs=[pl.BlockSpec((1, tp, 128), lambda bi, p: (bi, 0, C_Q // 128 + p)),
                  pl.BlockSpec((1, tp, 128), lambda bi, p: (bi, 0, C_K // 128 + p)),
                  pl.BlockSpec((1, tp, 256), lambda bi, p: (bi, 0, C_V // 256 + p)),
                  pl.BlockSpec((1, tp, 256), lambda bi, p: (bi, 0, C_R // 256 + p)),
                  pl.BlockSpec((1, tp, 128), lambda bi, p: (bi, 0, C_GD // 128)),
                  pl.BlockSpec((128, 128), lambda bi, p: (0, p)),
                  pl.BlockSpec((1, 128), lambda bi, p: (0, p)),
                  pl.BlockSpec((1, 128), lambda bi, p: (0, 0))],
        out_specs=[pl.BlockSpec((1, tp, 256), lambda bi, p: (bi, 0, p)),
                   pl.BlockSpec((1, tp, 256), lambda bi, p: (bi, 0, p)),
                   pl.BlockSpec((1, 2, nchunk, GLA_DV, 128), lambda bi, p: (bi, p, 0, 0, 0))],
        scratch_shapes=[pltpu.VMEM((2, GLA_DV, 128), F32)],
        operands=(u, u, u, u, u, gup, gbias, gnorm))
    return res[0], res[1], res[2], got


def _out_proj_ln1(hc, gla_out, w_out, s0, cg, cb, g1, b1, fuse=NO_FUSE):
    r, d = s0.shape
    tm = _row_tile(r)

    def body(hc_ref, a_ref, w_ref, s0_ref, cg_ref, cb_ref, g_ref, b_ref, co_ref, xh_ref, rstd_ref, s1b_ref):
        for rs in _sub_rows(tm):
            xc, _ = _ln_stats(hc_ref[rs, :])
            nv = xc * cg_ref[...] + cb_ref[...]
            co = (nv * _sigmoid(nv)).astype(BF16)
            co_ref[rs, :] = co
            mix = _nn(co, w_ref[0:D_CONV, :]) + _nn(a_ref[rs, :], w_ref[D_CONV:, :])
            xh, rstd = _ln_stats(ALPHA * s0_ref[rs, :] + mix)
            xh_ref[rs, :] = xh
            rstd_ref[rs, :] = rstd
            s1b_ref[rs, :] = (xh * g_ref[...] + b_ref[...]).astype(BF16)

    row = lambda n: pl.BlockSpec((tm, n), lambda i: (i, 0))
    vec = lambda n: pl.BlockSpec((1, n), lambda i: (0, 0))
    res, got = _fused_call(
        body, fuse, name="out_proj_ln1",
        out_shape=[jax.ShapeDtypeStruct((r, D_CONV), BF16), jax.ShapeDtypeStruct((r, d), F32),
                   jax.ShapeDtypeStruct((r, 1), F32), jax.ShapeDtypeStruct((r, d), BF16)],
        grid=(r // tm,),
        in_specs=[row(D_CONV), row(512), pl.BlockSpec((d, d), lambda i: (0, 0)), row(d),
                  vec(D_CONV), vec(D_CONV), vec(d), vec(d)],
        out_specs=[row(D_CONV), row(d), row(1), row(d)], scratch_shapes=[],
        operands=(hc, gla_out, w_out, s0, cg, cb, g1, b1))
    return res[0], res[1], res[2], res[3], got


def _ffn1(s1b, w1, fuse=NO_FUSE):
    r, d = s1b.shape
    tm = _row_tile(r)
    ns, _, wn = w1.shape

    def body(a_ref, w_ref, o_ref):
        a = a_ref[...]
        for j in range(ns):
            o_ref[:, j * wn:(j + 1) * wn] = jnp.maximum(_nn(a, w_ref[j]), 0.0).astype(BF16)

    (ra,), got = _fused_call(
        body, fuse, name="ffn1", out_shape=[jax.ShapeDtypeStruct((r, D_FF), BF16)], grid=(r // tm,),
        in_specs=[pl.BlockSpec((tm, d), lambda i: (i, 0)), pl.BlockSpec(w1.shape, lambda i: (0, 0, 0))],
        out_specs=[pl.BlockSpec((tm, D_FF), lambda i: (i, 0))], scratch_shapes=[], operands=(s1b, w1))
    return ra, got


def _ffn2_ln2_loss(ra, w2, xhat1, g1, b1, g2, b2, tgt, tp):
    r, d = xhat1.shape
    tm = _row_tile(tp)
    per = tp // tm

    def body(ra_ref, w_ref, xh1_ref, g1_ref, b1_ref, g2_ref, b2_ref, tgt_ref, dt_ref, dtb_ref, acc_ref, t_ref, sem):
        i = pl.program_id(0)
        b, j = i // per, i % per

        @pl.when(i == 0)
        def _():
            acc_ref[...] = jnp.zeros_like(acc_ref)

        head_copy = pltpu.make_async_copy(tgt_ref.at[b, pl.ds(0, tm - HEAD), :], t_ref.at[pl.ds(HEAD, tm - HEAD), :], sem)
        body_copy = pltpu.make_async_copy(
            tgt_ref.at[b, pl.ds(pl.multiple_of(jnp.maximum(j * tm - HEAD, 0), 64), tm), :], t_ref, sem)

        @pl.when(j == 0)
        def _():
            t_ref[0:HEAD, :] = jnp.zeros((HEAD, d), F32)
            head_copy.start()

        pl.when(j > 0)(body_copy.start)

        sums = [jnp.zeros((1, d), F32)] * 3
        for rs in _sub_rows(tm):
            rb = ra_ref[rs, :]
            f = _nn(rb * rb, w_ref[...])
            if rs.start == 0:
                pl.when(j == 0)(head_copy.wait)
                pl.when(j > 0)(body_copy.wait)
            s1 = xh1_ref[rs, :] * g1_ref[...] + b1_ref[...]
            xh2, rstd2 = _ln_stats(ALPHA * s1 + f)
            y = xh2 * g2_ref[...] + b2_ref[...]
            rowid = (i % per) * tm + rs.start + lax.broadcasted_iota(jnp.int32, (rs.stop - rs.start, 1), 0)
            e = jnp.where(rowid >= HEAD, y - t_ref[rs, :], 0.0)
            dy = e * (1.0 / d)
            dt2 = _ln_bwd(dy * g2_ref[...], xh2, rstd2)
            dt_ref[rs, :] = dt2
            dtb_ref[rs, :] = dt2.astype(BF16)
            sums = [sums[0] + (0.5 / d) * jnp.sum(e * e, axis=0, keepdims=True),
                    sums[1] + jnp.sum(dy * xh2, axis=0, keepdims=True), sums[2] + jnp.sum(dy, axis=0, keepdims=True)]
        for k in range(3):
            acc_ref[k:k + 1, :] += sums[k]

    row = lambda n: pl.BlockSpec((tm, n), lambda i: (i, 0))
    vec = pl.BlockSpec((1, d), lambda i: (0, 0))
    return pl.pallas_call(
        body, name="ffn2_ln2_loss",
        out_shape=[jax.ShapeDtypeStruct((r, d), F32), jax.ShapeDtypeStruct((r, d), BF16),
                   jax.ShapeDtypeStruct((8, d), F32)],
        grid=(r // tm,),
        in_specs=[row(D_FF), pl.BlockSpec((D_FF, d), lambda i: (0, 0)), row(d), vec, vec, vec, vec, ANY],
        out_specs=[row(d), row(d), pl.BlockSpec((8, d), lambda i: (0, 0))],
        scratch_shapes=[pltpu.VMEM((tm, d), F32), pltpu.SemaphoreType.DMA],
        compiler_params=_params(("arbitrary",)),
    )(ra, w2, xhat1, g1, b1, g2, b2, tgt)


def _ffn_bwd_da(dt2b, w2, ra):
    r, d = dt2b.shape
    tm = _row_tile(r)

    def body(g_ref, w_ref, ra_ref, o_ref):
        o_ref[...] = (_nt(g_ref[...], w_ref[...]) * (2.0 * ra_ref[...].astype(F32))).astype(BF16)

    return pl.pallas_call(
        body, name="ffn_bwd_da", out_shape=jax.ShapeDtypeStruct((r, D_FF), BF16),
        grid=(r // tm,),
        in_specs=[pl.BlockSpec((tm, d), lambda i: (i, 0)), pl.BlockSpec((D_FF, d), lambda i: (0, 0)),
                  pl.BlockSpec((tm, D_FF), lambda i: (i, 0))],
        out_specs=pl.BlockSpec((tm, D_FF), lambda i: (i, 0)),
        compiler_params=_params(("parallel",)),
    )(dt2b, w2, ra)


def _ffn_bwd_ln1(da, w1, dt2, xhat1, rstd1, g1):
    r, d = dt2.shape
    tm = _row_tile(r)

    def body(da_ref, w_ref, dt2_ref, xh_ref, rstd_ref, g_ref, dt_ref, dtb_ref, acc_ref):
        @pl.when(pl.program_id(0) == 0)
        def _():
            acc_ref[...] = jnp.zeros_like(acc_ref)

        sums = [jnp.zeros((1, d), F32)] * 2
        for rs in _sub_rows(tm):
            ds1 = ALPHA * dt2_ref[rs, :]
            for j in range(w1.shape[0]):
                ds1 = ds1 + _nt(da_ref[rs, j * w1.shape[2]:(j + 1) * w1.shape[2]], w_ref[j])
            xh = xh_ref[rs, :]
            dt1 = _ln_bwd(ds1 * g_ref[...], xh, rstd_ref[rs, :])
            dt_ref[rs, :] = dt1
            dtb_ref[rs, :] = dt1.astype(BF16)
            sums = [sums[0] + jnp.sum(ds1 * xh, axis=0, keepdims=True), sums[1] + jnp.sum(ds1, axis=0, keepdims=True)]
        for k in range(2):
            acc_ref[k:k + 1, :] += sums[k]

    row = lambda n: pl.BlockSpec((tm, n), lambda i: (i, 0))
    return pl.pallas_call(
        body, name="ffn_bwd_ln1",
        out_shape=[jax.ShapeDtypeStruct((r, d), F32), jax.ShapeDtypeStruct((r, d), BF16),
                   jax.ShapeDtypeStruct((8, d), F32)],
        grid=(r // tm,),
        in_specs=[row(D_FF), pl.BlockSpec(w1.shape, lambda i: (0, 0, 0)), row(d), row(d), row(1),
                  pl.BlockSpec((1, d), lambda i: (0, 0))],
        out_specs=[row(d), row(d), pl.BlockSpec((8, d), lambda i: (0, 0))],
        compiler_params=_params(("arbitrary",)),
    )(da, w1, dt2, xhat1, rstd1, g1)


def _matmul_tn(lhs, rhs, bm, square_lhs=False, name="matmul_tn", fuse=NO_FUSE):
    r, m = lhs.shape
    n = rhs.shape[1]
    tk = _reduce_tile(r, True)

    def body(a_ref, b_ref, o_ref):
        @pl.when(pl.program_id(1) == 0)
        def _():
            o_ref[...] = jnp.zeros_like(o_ref)

        a = a_ref[...]
        if square_lhs:
            a = a * a
        o_ref[...] += _tn(a, b_ref[...])

    (out,), got = _fused_call(
        body, fuse, name=name, out_shape=[jax.ShapeDtypeStruct((m, n), F32)], grid=(m // bm, r // tk),
        in_specs=[pl.BlockSpec((tk, bm), lambda i, k: (k, i)), pl.BlockSpec((tk, n), lambda i, k: (k, 0))],
        out_specs=[pl.BlockSpec((bm, n), lambda i, k: (i, 0))], scratch_shapes=[], operands=(lhs, rhs))
    return out, got


def _grad_w_ff1(s1b, da, fuse=NO_FUSE):
    r, d = s1b.shape
    wn = da.shape[1] // 4
    tk = _reduce_tile(r, True)

    def body(a_ref, b_ref, o_ref):
        @pl.when(pl.program_id(1) == 0)
        def _():
            o_ref[...] = jnp.zeros_like(o_ref)

        o_ref[0] += _tn(a_ref[...], b_ref[...])

    (out,), got = _fused_call(
        body, fuse, name="grad_w_ff1", out_shape=[jax.ShapeDtypeStruct((4, d, wn), F32)], grid=(4, r // tk),
        in_specs=[pl.BlockSpec((tk, d), lambda j, k: (k, 0)), pl.BlockSpec((tk, wn), lambda j, k: (k, j))],
        out_specs=[pl.BlockSpec((1, d, wn), lambda j, k: (j, 0, 0))], scratch_shapes=[], operands=(s1b, da))
    return out, got


def _grad_w_out(conv_of, gla_of, dt1b, fuse=NO_FUSE):
    r, n = dt1b.shape
    tk = _reduce_tile(r, True)

    def body(a_ref, b_ref, g_ref, o_ref):
        @pl.when(pl.program_id(1) == 0)
        def _():
            o_ref[...] = jnp.zeros_like(o_ref)

        @pl.when(pl.program_id(0) == 0)
        def _():
            o_ref[...] += _tn(a_ref[...], g_ref[...])

        @pl.when(pl.program_id(0) == 1)
        def _():
            o_ref[...] += _tn(b_ref[...], g_ref[...])

    lhs = pl.BlockSpec((tk, 512), lambda i, k: (k, 0))
    (out,), got = _fused_call(
        body, fuse, name="grad_w_out", out_shape=[jax.ShapeDtypeStruct((2 * 512, n), F32)], grid=(2, r // tk),
        in_specs=[lhs, lhs, pl.BlockSpec((tk, n), lambda i, k: (k, 0))],
        out_specs=[pl.BlockSpec((512, n), lambda i, k: (i, 0))], scratch_shapes=[],
        operands=(conv_of, gla_of, dt1b))
    return out, got


def _out_proj_bwd(dt1b, w_out, hc, cg, cb, fuse=NO_FUSE):
    r, d = dt1b.shape
    tm = _row_tile(r)

    def body(g_ref, w_ref, hc_ref, cg_ref, cb_ref, dhc_ref, dgla_ref, acc_ref):
        @pl.when(pl.program_id(0) == 0)
        def _():
            acc_ref[...] = jnp.zeros_like(acc_ref)

        gg = cg_ref[...]
        sums = [jnp.zeros((1, D_CONV), F32)] * 3
        for rs in _sub_rows(tm):
            dmix = _nt(g_ref[rs, :], w_ref[...])
            dgla_ref[rs, :] = dmix[:, D_CONV:]
            xh, rstd = _ln_stats(hc_ref[rs, :])
            nv = xh * gg + cb_ref[...]
            sig = _sigmoid(nv)
            dn = dmix[:, :D_CONV] * (sig * (1.0 + nv * (1.0 - sig)))
            dhc = _ln_bwd(dn * gg, xh, rstd)
            dhc_ref[rs, :] = dhc
            sums = [sums[0] + jnp.sum(dhc, axis=0, keepdims=True), sums[1] + jnp.sum(dn * xh, axis=0, keepdims=True),
                    sums[2] + jnp.sum(dn, axis=0, keepdims=True)]
        for k in range(3):
            acc_ref[k:k + 1, :] += sums[k]

    row = lambda n: pl.BlockSpec((tm, n), lambda i: (i, 0))
    vec = pl.BlockSpec((1, D_CONV), lambda i: (0, 0))
    res, got = _fused_call(
        body, fuse, name="out_proj_bwd",
        out_shape=[jax.ShapeDtypeStruct((r, D_CONV), F32), jax.ShapeDtypeStruct((r, 512), F32),
                   jax.ShapeDtypeStruct((8, D_CONV), F32)],
        grid=(r // tm,),
        in_specs=[row(d), pl.BlockSpec((d, d), lambda i: (0, 0)), row(D_CONV), vec, vec],
        out_specs=[row(D_CONV), row(512), pl.BlockSpec((8, D_CONV), lambda i: (0, 0))], scratch_shapes=[],
        operands=(dt1b, w_out, hc, cg, cb))
    return res[0], res[1], res[2], got


def _conv_bwd(dhc, u, conv_w, fuse=NO_FUSE):
    bsz, tp, _ = u.shape
    nchunk = tp // CHUNK
    win = CHUNK + 32
    nct = D_CONV // 128

    def body(dhc_ref, cv_ref, cg_ref, w_ref, dv_ref, dg_ref, dw_ref, h_scr, dhc_scr, hwin, dwin, dw_scr):
        h_scr[0:32, :] = jnp.zeros((32, 128), F32)
        h_scr[32:32 + tp, :] = cv_ref[0] * _sigmoid(cg_ref[0])
        dhc_scr[0:tp, :] = dhc_ref[0]
        dhc_scr[tp:tp + 32, :] = jnp.zeros((32, 128), F32)
        dw_scr[...] = jnp.zeros_like(dw_scr)

        def step(n, carry):
            r0 = pl.multiple_of(n * CHUNK, CHUNK)
            rows = pl.ds(r0, CHUNK)
            hwin[...] = h_scr[pl.ds(r0, win), :]
            dwin[...] = dhc_scr[pl.ds(r0, win), :]
            dcur = dwin[0:CHUNK, :]
            acc = jnp.zeros((CHUNK, 128), F32)
            for j in range(CONV_WIDTH):
                acc = acc + w_ref[j:j + 1, :] * dwin[30 - j:30 - j + CHUNK, :]
                prod = dcur * hwin[2 + j:2 + j + CHUNK, :]
                dw_scr[j * 8:(j + 1) * 8, :] += jnp.sum(prod.reshape(CHUNK // 8, 8, 128), axis=0)
            cg = cg_ref[0, rows, :]
            sig = _sigmoid(cg)
            rowid = n * CHUNK + lax.broadcasted_iota(jnp.int32, (CHUNK, 1), 0)
            dh = jnp.where(rowid >= PAD, acc, 0.0)
            dv_ref[0, rows, :] = (dh * sig).astype(BF16)
            dg_ref[0, rows, :] = (dh * cv_ref[0, rows, :] * sig * (1.0 - sig)).astype(BF16)
            return carry

        lax.fori_loop(0, nchunk, step, 0)
        dw_ref[0] = jnp.zeros((32, 128), F32)
        for j in range(CONV_WIDTH):
            dw_ref[0, j:j + 1, :] = jnp.sum(dw_scr[j * 8:(j + 1) * 8, :], axis=0, keepdims=True)

    blk = lambda off: pl.BlockSpec((1, tp, 128), lambda bi, t: (bi, 0, off // 128 + t))
    res, got = _fused_call(
        body, fuse, name="conv_bwd",
        out_shape=[jax.ShapeDtypeStruct((bsz, tp, D_CONV), BF16), jax.ShapeDtypeStruct((bsz, tp, D_CONV), BF16),
                   jax.ShapeDtypeStruct((bsz, 32, D_CONV), F32)],
        grid=(bsz, nct),
        in_specs=[blk(0), blk(C_VAL), blk(C_GATE), pl.BlockSpec((32, 128), lambda bi, t: (0, t))],
        out_specs=[blk(0), blk(0), pl.BlockSpec((1, 32, 128), lambda bi, t: (bi, 0, t))],
        scratch_shapes=[pltpu.VMEM((tp + 32, 128), F32), pltpu.VMEM((tp + 32, 128), F32),
                        pltpu.VMEM((win, 128), F32), pltpu.VMEM((win, 128), F32),
                        pltpu.VMEM((CONV_WIDTH * 8, 128), F32)],
        operands=(dhc, u, u, conv_w))
    return res[0], res[1], res[2], got


def _gla_bwd(dgla, u, o_pre, states, gup, gbias, gnorm, fuse=NO_FUSE):
    bsz, tp, _ = u.shape
    nchunk = tp // CHUNK
    nb = _gla_group(nchunk)

    def body(dy_ref, q_ref, k_ref, v_ref, r_ref, gd_ref, o_ref, st_ref, gup_ref, gb_ref, gn_ref,
             dq_ref, dk_ref, dv_ref, dr_ref, dgd_ref, dgup_ref, vec_ref, h_scr, gup_acc):
        tril, triu, hmask = _gla_consts(nb)
        h_scr[...] = jnp.zeros_like(h_scr)
        gup_acc[...] = jnp.zeros_like(gup_acc)
        gn = gn_ref[...]
        gupb = gup_ref[...]
        m = nb * CHUNK
        ngroup = nchunk // nb

        def group(i, carry):
            dbias, dgn = carry
            g = ngroup - 1 - i
            rows, valid, z, eb, enb, erest, dec, qe, ke, kd = _gla_group_terms(
                g, nb, q_ref, k_ref, gd_ref, gup_ref, gb_ref, tril)
            keb, kdb = ke.astype(BF16), kd.astype(BF16)
            dqe = jnp.zeros((nb, CHUNK, 128), F32)
            dke = jnp.zeros((nb, CHUNK, 128), F32)
            dkd = jnp.zeros((nb, CHUNK, 128), F32)
            ddec = jnp.zeros((nb, 1, 128), F32)
            for h in range(2):
                cols = slice(h * GLA_DV, (h + 1) * GLA_DV)
                o = o_ref[0, rows, cols]
                rh = r_ref[0, rows, cols]
                dy = dy_ref[0, rows, cols]
                rms = lax.rsqrt(jnp.mean(o * o, axis=-1, keepdims=True) + LN_EPS)
                nrm = o * rms
                sig = _sigmoid(rh)
                sw = rh * sig
                dr_ref[0, rows, cols] = (dy * nrm * gn * (sig * (1.0 + rh * (1.0 - sig)))).astype(BF16)
                dgn = dgn + jnp.sum(dy * nrm * sw, axis=0, keepdims=True)
                dn = dy * gn * sw
                do = rms * (dn - nrm * jnp.mean(dn * nrm, axis=-1, keepdims=True))
                dob = do.astype(BF16).reshape(nb, CHUNK, GLA_DV)
                qh = jnp.where(hmask[h], qe, 0.0).astype(BF16)
                vh = v_ref[0, rows, cols].astype(BF16).reshape(nb, CHUNK, GLA_DV)
                ht = h_scr[h]
                hts = [None] * nb
                for n in reversed(range(nb)):
                    hts[n] = ht
                    ht = dec[n] * ht + _tn(dob[n], qh[n])
                h_scr[h] = ht
                htf = jnp.stack(hts)
                htb = htf.astype(BF16)
                st = st_ref[0, h, pl.ds(g * nb, nb)]
                at = jnp.where(triu, _bnt(keb, qh), 0.0).astype(BF16)
                da = jnp.where(tril, _bnt(dob, vh), 0.0).astype(BF16)
                dat = jnp.where(triu, _bnt(vh, dob), 0.0).astype(BF16)
                dqe = dqe + jnp.where(hmask[h], _bnn(da, keb) + _bnn(dob, st.astype(BF16)), 0.0)
                dke = dke + _bnn(dat, qh)
                dv_ref[0, rows, cols] = (_bnn(at, dob) + _bnt(kdb, htb)).reshape(m, GLA_DV).astype(BF16)
                dkd = dkd + jnp.where(hmask[h], _bnn(vh, htb), 0.0)
                ddec = ddec + jnp.where(hmask[h], jnp.sum(htf * st, axis=1, keepdims=True), 0.0)
            dq_ref[0, rows, :] = (dqe * eb * Q_SCALE).reshape(m, 128).astype(BF16)
            dk_ref[0, rows, :] = (dke * enb + dkd * erest).reshape(m, 128).astype(BF16)
            db = dqe * qe - dke * ke - dkd * kd
            dblast = jnp.sum(dkd * kd, axis=1, keepdims=True) + ddec * dec
            lastrow = lax.broadcasted_iota(jnp.int32, (1, CHUNK, 1), 1) == CHUNK - 1
            db = db + jnp.where(lastrow, dblast, 0.0)
            dlg = _bnn(triu.astype(F32), db, precision=lax.Precision.HIGHEST).reshape(m, 128)
            dz = jnp.where(valid, dlg, 0.0) * (1.0 / GLA_TAU) * (1.0 - _sigmoid(z))
            dzb = dz.astype(BF16)
            dgd_ref[0, 0, rows, :] = _nt(dzb, gupb).astype(BF16)
            gup_acc[...] += _tn(gd_ref[0, rows, :].astype(BF16), dzb)
            return dbias + jnp.sum(dz, axis=0, keepdims=True), dgn

        zero = jnp.zeros((1, 128), F32)
        dbias, dgn = lax.fori_loop(0, ngroup, group, (zero, zero))
        dgup_ref[0] = gup_acc[...]
        vec_ref[0] = jnp.zeros((8, 128), F32)
        vec_ref[0, 0:1, :] = dbias
        vec_ref[0, 1:2, :] = dgn

    pair = lambda w, off: pl.BlockSpec((1, tp, w), lambda bi, p: (bi, 0, off // w + p))
    return _fused_call(
        body, fuse, name="gla_bwd",
        out_shape=[jax.ShapeDtypeStruct((bsz, tp, 256), BF16), jax.ShapeDtypeStruct((bsz, tp, 256), BF16),
                   jax.ShapeDtypeStruct((bsz, tp, 512), BF16), jax.ShapeDtypeStruct((bsz, tp, 512), BF16),
                   jax.ShapeDtypeStruct((bsz, 2, tp, 128), BF16), jax.ShapeDtypeStruct((bsz, 128, 256), F32),
                   jax.ShapeDtypeStruct((bsz, 8, 256), F32)],
        grid=(bsz, 2),
        in_specs=[pair(256, 0), pair(128, C_Q), pair(128, C_K), pair(256, C_V), pair(256, C_R),
                  pl.BlockSpec((1, tp, 128), lambda bi, p: (bi, 0, C_GD // 128)),
                  pair(256, 0),
                  pl.BlockSpec((1, 2, nchunk, GLA_DV, 128), lambda bi, p: (bi, p, 0, 0, 0)),
                  pl.BlockSpec((128, 128), lambda bi, p: (0, p)),
                  pl.BlockSpec((1, 128), lambda bi, p: (0, p)),
                  pl.BlockSpec((1, 128), lambda bi, p: (0, 0))],
        out_specs=[pair(128, 0), pair(128, 0), pair(256, 0), pair(256, 0),
                   pl.BlockSpec((1, 1, tp, 128), lambda bi, p: (bi, p, 0, 0)),
                   pl.BlockSpec((1, 128, 128), lambda bi, p: (bi, 0, p)),
                   pl.BlockSpec((1, 8, 128), lambda bi, p: (bi, 0, p))],
        scratch_shapes=[pltpu.VMEM((2, GLA_DV, 128), F32), pltpu.VMEM((128, 128), F32)],
        operands=(dgla, u, u, u, u, u, o_pre, states, gup, gbias, gnorm))


_DU_OFFSETS = (C_VAL, C_GATE, C_Q, C_K, C_V, C_R)
_DU_WIDTHS = (512, 512, 256, 256, 512, 512)


def _du_specs(tm, per, row_map):
    specs = [pl.BlockSpec((tm, w), row_map) for w in _DU_WIDTHS]
    for p in range(2):
        specs.append(pl.BlockSpec((1, 1, tm, 128), lambda *ix, p=p: (row_map(*ix)[0] // per, p, row_map(*ix)[0] % per, 0)))
    return specs


def _du_pieces(refs):
    out = [(off, ref[...]) for off, ref in zip(_DU_OFFSETS, refs[:6])]
    dgd = (refs[6][0, 0].astype(F32) + refs[7][0, 0].astype(F32)).astype(BF16)
    out.append((C_GD, dgd))
    return out


def _in_proj_bwd(pieces, dgd, w_int, dt1, tp, fuse=NO_FUSE):
    r, d = dt1.shape
    tm = _row_tile(tp)
    per = tp // tm

    def body(*refs):
        w_ref, dt_ref, o_ref = refs[8:]
        acc = ALPHA * dt_ref[...]
        for off, val in _du_pieces(refs[:8]):
            acc = acc + _nn(val, w_ref[off:off + val.shape[1], :])
        o_ref[...] = acc

    row = lambda i: (i, 0)
    (ds0,), got = _fused_call(
        body, fuse, name="in_proj_bwd", out_shape=[jax.ShapeDtypeStruct((r, d), F32)], grid=(r // tm,),
        in_specs=_du_specs(tm, per, row) + [pl.BlockSpec((D_IN_PAD, d), lambda i: (0, 0)), pl.BlockSpec((tm, d), row)],
        out_specs=[pl.BlockSpec((tm, d), row)], scratch_shapes=[], operands=(*pieces, dgd, dgd, w_int, dt1))
    return ds0, got


def _grad_w_in(pieces, dgd, s0b, tp, fuse=NO_FUSE):
    r, d = s0b.shape
    tk = _reduce_tile(tp, False)
    per = tp // tk

    def body(*refs):
        s_ref, o_ref = refs[8:]

        @pl.when(pl.program_id(0) == 0)
        def _():
            o_ref[...] = jnp.zeros_like(o_ref)

        s = s_ref[...]
        for off, val in _du_pieces(refs[:8]):
            o_ref[off:off + val.shape[1], :] += _tn(val, s)

    row = lambda k: (k, 0)
    (out,), got = _fused_call(
        body, fuse, name="grad_w_in", out_shape=[jax.ShapeDtypeStruct((D_IN_PAD, d), F32)], grid=(r // tk,),
        in_specs=_du_specs(tk, per, row) + [pl.BlockSpec((tk, d), row)],
        out_specs=[pl.BlockSpec((D_IN_PAD, d), lambda k: (0, 0))], scratch_shapes=[],
        operands=(*pieces, dgd, dgd, s0b))
    return out, got


def _ln_in_bwd(ds0, x, meta, g):
    bsz, s, d = x.shape
    tp = s + HEAD
    nh = 2
    sh = s // nh
    rc = min(256, sh)

    def body(ds_ref, x_ref, meta_ref, g_ref, gx_ref, dm_ref, vec_ref):
        h = pl.program_id(1)
        gg = g_ref[...]

        @pl.when(h == 0)
        def _():
            mh, mr = _ln_stats(meta_ref[...])
            dsm = ds_ref[0, PAD:HEAD, :]
            dm_ref[0] = _ln_bwd(dsm * gg, mh, mr)
            vec_ref[0] = jnp.zeros((8, d), F32)
            vec_ref[0, 0:1, :] = jnp.sum(dsm * mh, axis=0, keepdims=True)
            vec_ref[0, 1:2, :] = jnp.sum(dsm, axis=0, keepdims=True)

        def step(i, carry):
            sg, sb = carry
            dst = pl.ds(pl.multiple_of(i * rc, rc), rc)
            src = pl.ds(pl.multiple_of(HEAD + h * sh + i * rc, 64), rc)
            xh, rstd = _ln_stats(x_ref[0, dst, :])
            dsv = ds_ref[0, src, :]
            gx_ref[0, dst, :] = _ln_bwd(dsv * gg, xh, rstd)
            return sg + jnp.sum(dsv * xh, axis=0, keepdims=True), sb + jnp.sum(dsv, axis=0, keepdims=True)

        zero = jnp.zeros((1, d), F32)
        sg, sb = lax.fori_loop(0, sh // rc, step, (zero, zero))
        vec_ref[0, 0:1, :] += sg
        vec_ref[0, 1:2, :] += sb

    return pl.pallas_call(
        body, name="ln_in_bwd",
        out_shape=[jax.ShapeDtypeStruct((bsz, s, d), F32), jax.ShapeDtypeStruct((bsz, N_META, d), F32),
                   jax.ShapeDtypeStruct((bsz, 8, d), F32)],
        grid=(bsz, nh),
        in_specs=[pl.BlockSpec((1, tp, d), lambda bi, hi: (bi, 0, 0)),
                  pl.BlockSpec((1, sh, d), lambda bi, hi: (bi, hi, 0)),
                  pl.BlockSpec((N_META, d), lambda bi, hi: (0, 0)),
                  pl.BlockSpec((1, d), lambda bi, hi: (0, 0))],
        out_specs=[pl.BlockSpec((1, sh, d), lambda bi, hi: (bi, hi, 0)),
                   pl.BlockSpec((1, N_META, d), lambda bi, hi: (bi, 0, 0)),
                   pl.BlockSpec((1, 8, d), lambda bi, hi: (bi, 0, 0))],
        compiler_params=_params(("parallel", "arbitrary")),
    )(ds0, x, meta, g)


def _rows128(a):
    return a.reshape(-1, 128)


def kernel(x, meta_tokens, ln_in_g, ln_in_b, w_in, conv_w, conv_b, conv_ln_g, conv_ln_b, gate_up, gate_bias, gla_norm_g, w_out, ln1_g, ln1_b, w_ff1, w_ff2, ln2_g, ln2_b, loss_target, m_meta_tokens, m_ln_in_g, m_ln_in_b, m_w_in, m_conv_w, m_conv_b, m_conv_ln_g, m_conv_ln_b, m_gate_up, m_gate_bias, m_gla_norm_g, m_w_out, m_ln1_g, m_ln1_b, m_w_ff1, m_w_ff2, m_ln2_g, m_ln2_b, v_meta_tokens, v_ln_in_g, v_ln_in_b, v_w_in, v_conv_w, v_conv_b, v_conv_ln_g, v_conv_ln_b, v_gate_up, v_gate_bias, v_gla_norm_g, v_w_out, v_ln1_g, v_ln1_b, v_w_ff1, v_w_ff2, v_ln2_g, v_ln2_b):
    bsz, seq, d = x.shape
    tp = seq + HEAD
    r = bsz * tp
    xi, yi, ci = _mesh_pos()
    chip = 2 * xi + yi
    c_arr = jnp.reshape(ci, (1,)).astype(jnp.int32)
    pos_arr = jnp.stack([chip, ci]).astype(jnp.int32)

    sh_in = D_IN // 4
    shard_in = jnp.pad(w_in[0].T.astype(BF16), ((0, D_IN_PAD // 4 - sh_in), (0, 0)))
    shard_w1, shard_wout, shard_w2 = w_ff1[0].astype(BF16), w_out[0].astype(BF16), w_ff2[0].astype(BF16)
    small_w = jnp.concatenate([_rows128(meta_tokens), _rows128(conv_w[0]), _rows128(gate_up[0])], axis=0)
    g_small = _gather_small(small_w)
    n_meta_rows, n_cw_rows = N_META * 256 // 128, CONV_WIDTH * 128 // 128
    meta_full = jnp.concatenate([g_small[j, :n_meta_rows].reshape(N_META, 256) for j in range(4)], axis=1)
    convw_full = jnp.concatenate(
        [g_small[j, n_meta_rows:n_meta_rows + n_cw_rows].reshape(CONV_WIDTH, 128) for j in range(4)], axis=1)
    gup_full = jnp.concatenate(
        [g_small[j, n_meta_rows + n_cw_rows:].reshape(GLA_RANK, 64) for j in range(4)], axis=1)
    convw_p = jnp.pad(convw_full, ((0, 1), (0, 0)))
    gup_p = jnp.pad(gup_full, ((0, 128 - GLA_RANK), (0, 0))).astype(BF16)
    ln_in_g2, ln_in_b2 = ln_in_g.reshape(1, d), ln_in_b.reshape(1, d)

    s0, s0b, ((g_int,),) = _ln_in_fwd(x, meta_full, ln_in_g2, ln_in_b2,
                                      [("gather", [(shard_in, 0, D_IN_PAD // 4, None)])])
    (g_int,) = _place_own([g_int], [shard_in])
    w_int = jnp.pad(g_int[:, :sh_in].reshape(D_IN, d), ((0, D_IN_PAD - D_IN), (0, 0)))
    s0f, s0bf = s0.reshape(r, d), s0b.reshape(r, d)
    u, ((w1_buf,),) = _in_proj(s0bf, w_int, [("gather", [(shard_w1, 0, 640, None)])])
    (w1_buf,) = _place_own([w1_buf], [shard_w1])
    u3 = u.reshape(bsz, tp, D_IN_PAD)
    hc, ((w1_buf,),) = _conv_fwd(u3, convw_p, conv_b, [("gather", [(shard_w1, 640, 384, w1_buf)])])
    hc = hc.reshape(r, D_CONV)
    gla_out, o_pre, states, ((g_wout,),) = _gla_fwd(
        u3, gup_p, gate_bias, gla_norm_g, [("gather", [(shard_wout, 0, 256, None)])])
    (g_wout,) = _place_own([g_wout], [shard_wout])
    wout = g_wout.reshape(d, d)
    gla_of = gla_out.reshape(r, 512)
    conv_of, xhat1, rstd1, s1b, ((w2_buf,),) = _out_proj_ln1(
        hc, gla_of, wout, s0f, conv_ln_g, conv_ln_b, ln1_g, ln1_b, [("gather", [(shard_w2, 0, 448, None)])])
    (w2_buf,) = _place_own([w2_buf], [shard_w2])
    w1 = w1_buf
    ra, ((w2_buf,),) = _ffn1(s1b, w1, [("gather", [(shard_w2, 448, 576, w2_buf)])])
    w2 = w2_buf.reshape(D_FF, d)
    dt2, dt2b, acc2 = _ffn2_ln2_loss(ra, w2, xhat1, ln1_g, ln1_b, ln2_g, ln2_b, loss_target, tp)

    def add_pair(g, got, splits=None):
        return _add_pair(g, got, c_arr, splits)

    da = _ffn_bwd_da(dt2b, w2, ra)
    dt1, dt1b, acc1 = _ffn_bwd_ln1(da, w1, dt2, xhat1, rstd1, ln1_g)
    g_w2, _ = _matmul_tn(ra, dt2b, 1024, square_lhs=True, name="grad_w_ff2")
    big_w2 = g_w2.reshape(4, D_FF // 4, d)
    big_w1, ((pair_w2,),) = _grad_w_ff1(s1b, da, [("pair", [big_w2])])
    dhc, dgla, cacc, ((pair_w1,),) = _out_proj_bwd(dt1b, wout, hc, conv_ln_g, conv_ln_b, [("pair", [big_w1])])
    (part_w1,), (part_w2a, part_w2b) = add_pair(big_w1, pair_w1), add_pair(big_w2, pair_w2, [512, 512])
    g_wout, _ = _grad_w_out(conv_of, gla_of, dt1b)
    big_wout = g_wout.reshape(4, d // 4, d)
    dcv, dcg, dcw, ((chip_w1,), (pair_wout,)) = _conv_bwd(
        dhc.reshape(bsz, tp, D_CONV), u3, convw_p, [("exchange", [part_w1]), ("pair", [big_wout])])
    (part_wout,) = add_pair(big_wout, pair_wout)
    (dq, dk, dv, dr, dgd, dgup, gvec), ((chip_w2a,),) = _gla_bwd(
        dgla.reshape(bsz, tp, 512), u3, o_pre, states, gup_p, gate_bias, gla_norm_g, [("exchange", [part_w2a])])
    pieces = [a.reshape(r, a.shape[-1]) for a in (dcv, dcg, dq, dk, dv, dr)]
    g_wint, ((chip_w2b, chip_wout),) = _grad_w_in(
        pieces, dgd, s0bf, tp, [("exchange", [part_w2b, part_wout])])
    big_win = jnp.stack([g_wint[j * sh_in:(j + 1) * sh_in] for j in range(4)])
    (half_w1, half_w2), ((pair_win,),) = _add_chips_many(
        [(big_w1, pair_w1, [chip_w1]), (big_w2, pair_w2, [chip_w2a, chip_w2b])], [("pair", [big_win])])
    ds0, ((chip_win,), (sib_w1, sib_w2)) = _in_proj_bwd(
        pieces, dgd, w_int, dt1, tp, [("exchange", add_pair(big_win, pair_win)), ("swap", [half_w1, half_w2])])
    grad_x, dmeta, lvec = _ln_in_bwd(ds0.reshape(bsz, tp, d), x, meta_full, ln_in_g2)

    red, loss = _sum_small(_allgather_small(_pack_small(acc2, acc1, lvec, cacc, gvec, dcw, dgup, dmeta)))

    half_win = _add_chips(big_win, pair_win, [chip_win], pos_arr)
    half_wout = _add_chips(big_wout, pair_wout, [chip_wout], pos_arr)
    sib_win, sib_wout = _pair_swap([half_win, half_wout])
    halves = {"w_in": (half_win, sib_win), "w_out": (half_wout, sib_wout), "w_ff1": (half_w1, sib_w1),
              "w_ff2": (half_w2, sib_w2)}

    grads = {}
    weights = dict(meta_tokens=meta_tokens, ln_in_g=ln_in_g, ln_in_b=ln_in_b, w_in=w_in, conv_w=conv_w, conv_b=conv_b,
                   conv_ln_g=conv_ln_g, conv_ln_b=conv_ln_b, gate_up=gate_up, gate_bias=gate_bias,
                   gla_norm_g=gla_norm_g, w_out=w_out, ln1_g=ln1_g, ln1_b=ln1_b, w_ff1=w_ff1, w_ff2=w_ff2,
                   ln2_g=ln2_g, ln2_b=ln2_b)
    moms = dict(meta_tokens=(m_meta_tokens, v_meta_tokens), ln_in_g=(m_ln_in_g, v_ln_in_g),
                ln_in_b=(m_ln_in_b, v_ln_in_b), w_in=(m_w_in, v_w_in), conv_w=(m_conv_w, v_conv_w),
                conv_b=(m_conv_b, v_conv_b), conv_ln_g=(m_conv_ln_g, v_conv_ln_g),
                conv_ln_b=(m_conv_ln_b, v_conv_ln_b), gate_up=(m_gate_up, v_gate_up),
                gate_bias=(m_gate_bias, v_gate_bias), gla_norm_g=(m_gla_norm_g, v_gla_norm_g),
                w_out=(m_w_out, v_w_out), ln1_g=(m_ln1_g, v_ln1_g), ln1_b=(m_ln1_b, v_ln1_b),
                w_ff1=(m_w_ff1, v_w_ff1), w_ff2=(m_w_ff2, v_w_ff2), ln2_g=(m_ln2_g, v_ln2_g),
                ln2_b=(m_ln2_b, v_ln2_b))
    names = list(weights)
    big_names = ("w_in", "w_out", "w_ff1", "w_ff2")
    delta, new_m, new_v = {}, {}, {}
    for k in big_names:
        to2d = (lambda a: a[0].T) if k == "w_in" else (lambda a: a[0])
        back = (lambda a: a.T[None]) if k == "w_in" else (lambda a: a[None])
        res = _adamw_halves(to2d(weights[k]), *halves[k], to2d(moms[k][0]), to2d(moms[k][1]), c_arr)
        grads[k], delta[k], new_m[k], new_v[k] = [back(a) for a in res]
    small_names = [k for k in names if k not in big_names]
    two = lambda a: a.reshape(-1, a.shape[-1])

    def my_cols(full, rows, width):
        return lax.dynamic_slice(full, (0, chip * width), (rows, width))

    sharded = {
        "meta_tokens": my_cols(red[SMALL_AT["meta_tokens"][0]:SMALL_AT["meta_tokens"][0] + N_META], N_META, 256),
        "conv_w": my_cols(red[SMALL_CONV_W:SMALL_CONV_W + 16].reshape(32, D_CONV), CONV_WIDTH, 128),
        "gate_up": my_cols(red[SMALL_GATE_UP:SMALL_GATE_UP + 4].reshape(GLA_RANK, 256), GLA_RANK, 64)}
    upd = _adamw_small(red, sharded, {k: (two(weights[k]), two(moms[k][0]), two(moms[k][1])) for k in small_names})
    for k in small_names:
        shp = weights[k].shape
        grads[k], delta[k], new_m[k], new_v[k] = [a.reshape(shp) for a in upd[k]]
    loss = loss.reshape(())

    return (loss, grad_x, *[grads[k] for k in names], *[delta[k] for k in names],
            *[new_m[k] for k in names], *[new_v[k] for k in names])
```

```python
import functools

import jax
import jax.numpy as jnp
from jax import lax
from jax.experimental import pallas as pl
from jax.experimental.pallas import tpu as pltpu

F32 = jnp.float32
BF16 = jnp.bfloat16

D_MODEL = 1024
N_META = 16
D_CONV = 512
CONV_WIDTH = 31
GLA_HEADS = 4
GLA_DV = 128
GLA_DK = 64
GLA_RANK = 16
GLA_TAU = 16.0
CHUNK = 64
D_FF = 4096
LN_EPS = 1e-5
ALPHA = 2.0 ** 0.25
D_IN = 2576
D_IN_PAD = 2688
PAD = CHUNK - N_META
HEAD = PAD + N_META
Q_SCALE = GLA_DK ** -0.5
ADAM_LR, ADAM_B1, ADAM_B2, ADAM_EPS, ADAM_WD, ADAM_STEP = 0.001, 0.9, 0.999, 1e-08, 0.01, 10
HALF = D_MODEL // 2
VMEM_LIMIT = 56 * 1024 * 1024
MESH = pl.DeviceIdType.MESH

C_VAL, C_GATE, C_Q, C_K, C_V, C_R, C_GD = 0, 512, 1024, 1280, 1536, 2048, 2560

SMALL_AT = {"loss": (0, 1, 0, 1024), "ln_in_g": (1, 1, 0, 1024), "ln_in_b": (2, 1, 0, 1024), "ln1_g": (3, 1, 0, 1024),
            "ln1_b": (4, 1, 0, 1024), "ln2_g": (5, 1, 0, 1024), "ln2_b": (6, 1, 0, 1024), "conv_b": (7, 1, 0, 512),
            "conv_ln_g": (7, 1, 512, 512), "conv_ln_b": (8, 1, 0, 512), "gate_bias": (8, 1, 512, 256),
            "gla_norm_g": (8, 1, 768, 128), "meta_tokens": (40, 16, 0, 1024)}
SMALL_CONV_W, SMALL_GATE_UP = 16, 32
SMALL_ROWS = 56


def _params(sem=None, **kw):
    return pltpu.CompilerParams(dimension_semantics=sem, vmem_limit_bytes=VMEM_LIMIT, **kw)


def _row_tile(tp):
    for t in (704, 352, 192, 64):
        if tp % t == 0:
            return t
    raise ValueError(tp)


def _reduce_tile(tp, big):
    for t in ((2112, 1056, 704) if big else (1056, 704)) + (352, 192, 64):
        if tp % t == 0:
            return t
    raise ValueError(tp)


def _sub_rows(tm):
    return [slice(0, tm)]


def _dot(a, b, dims, precision=None):
    return lax.dot_general(a, b, (dims, ((), ())), preferred_element_type=F32, precision=precision)


def _nn(a, b, **kw):
    return _dot(a, b, ((1,), (0,)), **kw)


def _nt(a, b, **kw):
    return _dot(a, b, ((1,), (1,)), **kw)


def _tn(a, b, **kw):
    return _dot(a, b, ((0,), (0,)), **kw)


def _sigmoid(x):
    return 1.0 / (1.0 + jnp.exp(-x))


def _log_sigmoid(z):
    return jnp.minimum(z, 0.0) - jnp.log(1.0 + jnp.exp(-jnp.abs(z)))


def _ln_stats(t):
    mu = jnp.mean(t, axis=-1, keepdims=True)
    d = t - mu
    var = jnp.mean(d * d, axis=-1, keepdims=True)
    rstd = lax.rsqrt(var + LN_EPS)
    return d * rstd, rstd


def _ln_bwd(dxhat, xhat, rstd):
    m1 = jnp.mean(dxhat, axis=-1, keepdims=True)
    m2 = jnp.mean(dxhat * xhat, axis=-1, keepdims=True)
    return rstd * (dxhat - m1 - xhat * m2)


def _mesh_pos():
    return lax.axis_index("x"), lax.axis_index("y"), lax.axis_index("c")


ANY = pl.BlockSpec(memory_space=pl.ANY)


def _gather_sems(n):
    return [pltpu.SemaphoreType.DMA((n, 3))] * 4


def _gather_steps(ins, outs, sems, ranges=None):
    n = len(ins)
    send, recv, fsend, frecv = sems
    ranges = ranges or [(0, ref.shape[0]) for ref in ins]
    x, y, c = _mesh_pos()
    me = 2 * x + y
    sibling = (x, y, 1 - c)
    chips = [(1 - x, y), (x, 1 - y), (1 - x, 1 - y)]
    chip_idx = [2 * px + py for px, py in chips]
    mine = [pl.ds(pl.multiple_of(r0 + c * (nr // 2), 16), nr // 2) for r0, nr in ranges]
    other = [pl.ds(pl.multiple_of(r0 + (1 - c) * (nr // 2), 16), nr // 2) for r0, nr in ranges]
    pairs = [(a, k) for a in range(n) for k in range(3)]

    def ici(a, k, slab):
        return pltpu.make_async_remote_copy(
            src_ref=ins[a].at[mine[a], :], dst_ref=outs[a].at[slab, mine[a], :],
            send_sem=send.at[a, k], recv_sem=recv.at[a, k], device_id=(*chips[k], c), device_id_type=MESH)

    def forward(a, k, rows):
        blk = outs[a].at[chip_idx[k], rows[a], :]
        return pltpu.make_async_remote_copy(
            src_ref=blk, dst_ref=blk, send_sem=fsend.at[a, k], recv_sem=frecv.at[a, k],
            device_id=sibling, device_id_type=MESH)

    def start():
        for a, k in pairs:
            ici(a, k, me).start()

    def relay():
        for a, k in pairs:
            ici(a, k, chip_idx[k]).wait_recv()
            forward(a, k, mine).start()

    def finish():
        for a, k in pairs:
            forward(a, k, other).wait_recv()
        for a, k in pairs:
            ici(a, k, me).wait_send()
            forward(a, k, mine).wait_send()

    return start, relay, finish


def _place_own(gathered, shards):
    chip = 2 * lax.axis_index("x") + lax.axis_index("y")
    return [lax.dynamic_update_slice(g, s[None], (chip, 0, 0)) for g, s in zip(gathered, shards)]


def _gather_small(small):
    def body(small_in, small_out, ssend, srecv):
        x, y, c = _mesh_pos()
        chips = [(1 - x, y), (x, 1 - y), (1 - x, 1 - y)]

        def small_copy(k, slot):
            return pltpu.make_async_remote_copy(
                src_ref=small_in, dst_ref=small_out.at[slot], send_sem=ssend.at[k], recv_sem=srecv.at[k],
                device_id=(*chips[k], c), device_id_type=MESH)

        for k in range(3):
            small_copy(k, 2 * x + y).start()
        for k, (px, py) in enumerate(chips):
            small_copy(k, 2 * px + py).wait_recv()
        for k in range(3):
            small_copy(k, 2 * x + y).wait_send()

    res = pl.pallas_call(
        body, name="gather_small", out_shape=jax.ShapeDtypeStruct((4,) + small.shape, small.dtype),
        in_specs=[ANY], out_specs=ANY,
        scratch_shapes=[pltpu.SemaphoreType.DMA((3,)), pltpu.SemaphoreType.DMA((3,))],
    )(small)
    return _place_own([res], [small])[0]


def _pair_swap_steps(ins, outs, sems):
    send, recv = sems
    x, y, c = _mesh_pos()
    cps = [pltpu.make_async_remote_copy(
        src_ref=ins[a], dst_ref=outs[a], send_sem=send.at[a], recv_sem=recv.at[a],
        device_id=(x, y, 1 - c), device_id_type=MESH) for a in range(len(ins))]

    def start():
        for cp in cps:
            cp.start()

    def finish():
        for cp in cps:
            cp.wait()

    return start, finish


def _pair_exchange_steps(ins, outs, sems):
    send, recv = sems
    x, y, c = _mesh_pos()
    other = pl.ds(pl.multiple_of((1 - c) * HALF, 128), HALF)
    cps = [pltpu.make_async_remote_copy(
        src_ref=ins[a].at[:, :, other], dst_ref=outs[a], send_sem=send.at[a], recv_sem=recv.at[a],
        device_id=(x, y, 1 - c), device_id_type=MESH) for a in range(len(ins))]

    def start():
        for cp in cps:
            cp.start()

    def finish():
        for cp in cps:
            cp.wait()

    return start, finish


def _pair_exchange_shapes(grads):
    return [jax.ShapeDtypeStruct(g.shape[:2] + (HALF,), g.dtype) for g in grads]


def _pair_exchange(grads):
    n = len(grads)

    def body(*refs):
        start, finish = _pair_exchange_steps(refs[:n], refs[n:2 * n], refs[2 * n:])
        start()
        finish()

    return pl.pallas_call(
        body, name="grad_pair_exchange", out_shape=_pair_exchange_shapes(grads),
        in_specs=[ANY] * n, out_specs=[ANY] * n,
        scratch_shapes=[pltpu.SemaphoreType.DMA((n,)), pltpu.SemaphoreType.DMA((n,))],
    )(*grads)


NO_FUSE = ()


def _fuse_plan(kind, items):
    n = len(items)
    if kind == "gather":
        shards = [it[0] for it in items]
        bufs = [it[3] for it in items if it[3] is not None]
        alias, b = {}, 0
        for a, it in enumerate(items):
            if it[3] is not None:
                alias[n + b] = a
                b += 1
        ranges = [(it[1], it[2]) for it in items]
        return (shards + bufs, [jax.ShapeDtypeStruct((4,) + s.shape, s.dtype) for s in shards], alias, _gather_sems(n),
                lambda i, o, s: _gather_steps(i[:n], o, s, ranges))
    if kind == "exchange":
        return list(items), _chip_exchange_shapes(items), {}, _chip_exchange_sems(n), _chip_exchange_steps
    pair_sems = [pltpu.SemaphoreType.DMA((n,)), pltpu.SemaphoreType.DMA((n,))]
    if kind == "swap":
        return (list(items), [jax.ShapeDtypeStruct(h.shape, h.dtype) for h in items], {}, pair_sems, _pair_swap_steps)
    return list(items), _pair_exchange_shapes(items), {}, pair_sems, _pair_exchange_steps


def _fused_call(body, fuse, *, name, grid, in_specs, out_specs, out_shape, scratch_shapes, operands,
                late_relay=False):
    plans = [_fuse_plan(kind, list(items)) for kind, items in fuse if len(items)]
    n_in, n_out, n_s = len(in_specs), len(out_shape), len(scratch_shapes)
    comm = [a for p in plans for a in p[0]]
    shapes = [s for p in plans for s in p[1]]
    nc, no = len(comm), len(shapes)
    aliases, i_at, o_at = {}, n_in, n_out
    for p in plans:
        for i, o in p[2].items():
            aliases[i_at + i] = o_at + o
        i_at, o_at = i_at + len(p[0]), o_at + len(p[1])

    def wrapped(*refs):
        o0 = n_in + nc
        s0 = o0 + n_out + no
        i_at, o_at, sem_at, steps = n_in, o0 + n_out, s0 + n_s, []
        for p in plans:
            steps.append(p[4](refs[i_at:i_at + len(p[0])], refs[o_at:o_at + len(p[1])], refs[sem_at:sem_at + len(p[3])]))
            i_at, o_at, sem_at = i_at + len(p[0]), o_at + len(p[1]), sem_at + len(p[3])
        first, last = _grid_ends(grid)
        for st in steps:
            pl.when(first)(st[0])
        if not late_relay:
            for st in steps:
                for mid in st[1:-1]:
                    pl.when(last)(mid)
        body(*refs[:n_in], *refs[o0:o0 + n_out], *refs[s0:s0 + n_s])
        for st in steps:
            for step in (st[1:] if late_relay else st[-1:]):
                pl.when(last)(step)

    res = pl.pallas_call(
        wrapped if plans else body, name=name, grid=grid, in_specs=list(in_specs) + [ANY] * nc,
        out_specs=list(out_specs) + [ANY] * no, out_shape=list(out_shape) + shapes,
        scratch_shapes=list(scratch_shapes) + [s for p in plans for s in p[3]], input_output_aliases=aliases,
        compiler_params=_params(("arbitrary",) * len(grid)))(*operands, *comm)
    outs, got, results = list(res[:n_out]), list(res[n_out:]), []
    for p in plans:
        results.append(got[:len(p[1])])
        got = got[len(p[1]):]
    return outs, results


def _chip_exchange_sems(n):
    return [pltpu.SemaphoreType.DMA((n, 3))] * 2


def _chip_exchange_shapes(parts):
    return [jax.ShapeDtypeStruct((3,) + p.shape[1:], p.dtype) for p in parts]


def _chip_exchange_steps(ins, outs, sems):
    send, recv = sems
    x, y, c = _mesh_pos()
    chips = [(1 - x, y), (x, 1 - y), (1 - x, 1 - y)]
    cps = [pltpu.make_async_remote_copy(
        src_ref=ins[a].at[2 * px + py], dst_ref=outs[a].at[k], send_sem=send.at[a, k], recv_sem=recv.at[a, k],
        device_id=(px, py, c), device_id_type=MESH) for a in range(len(ins)) for k, (px, py) in enumerate(chips)]

    def start():
        for cp in cps:
            cp.start()

    def finish():
        for cp in cps:
            cp.wait()

    return start, finish


def _chip_exchange(parts):
    n = len(parts)

    def body(*refs):
        start, finish = _chip_exchange_steps(refs[:n], refs[n:2 * n], refs[2 * n:])
        start()
        finish()

    return pl.pallas_call(
        body, name="grad_chip_exchange", out_shape=_chip_exchange_shapes(parts),
        in_specs=[ANY] * n, out_specs=[ANY] * n, scratch_shapes=_chip_exchange_sems(n),
    )(*parts)


def _pair_swap(halves):
    n = len(halves)

    def body(*refs):
        start, finish = _pair_swap_steps(refs[:n], refs[n:2 * n], refs[2 * n:])
        start()
        finish()

    return pl.pallas_call(
        body, name="grad_pair_swap",
        out_shape=[jax.ShapeDtypeStruct(h.shape, h.dtype) for h in halves],
        in_specs=[ANY] * n, out_specs=[ANY] * n,
        scratch_shapes=[pltpu.SemaphoreType.DMA((n,)), pltpu.SemaphoreType.DMA((n,))],
    )(*halves)


def _allgather_small(pack):
    m_per, ncol = pack.shape

    def body(x_ref, out_ref, send_sems, recv_sems, local_sem):
        x, y, c = _mesh_pos()
        me, sibling = (x, y, c), (x, y, 1 - c)
        chips = [(1 - x, y), (x, 1 - y), (1 - x, 1 - y)]

        def rows(px, py, pc):
            return out_ref.at[pl.ds(pl.multiple_of((4 * px + 2 * py + pc) * m_per, 8), m_per), :]

        def copy(k, block, to, src=None):
            return pltpu.make_async_remote_copy(
                src_ref=rows(*block) if src is None else src, dst_ref=rows(*block),
                send_sem=send_sems.at[k], recv_sem=recv_sems.at[k], device_id=to, device_id_type=MESH)

        mine = pltpu.make_async_copy(x_ref, rows(*me), local_sem)
        mine.start()
        first = [copy(0, me, sibling, src=x_ref)]
        first += [copy(1 + j, me, (*chip, c), src=x_ref) for j, chip in enumerate(chips)]
        for cp in first:
            cp.start()
        passed = [copy(4 + j, (*chip, c), sibling) for j, chip in enumerate(chips)]
        for j, chip in enumerate(chips):
            copy(1 + j, (*chip, c), me).wait_recv()
            passed[j].start()
        copy(0, sibling, me).wait_recv()
        for j, chip in enumerate(chips):
            copy(4 + j, (*chip, 1 - c), me).wait_recv()
        for cp in first + passed:
            cp.wait_send()
        mine.wait()

    return pl.pallas_call(
        body, name="allgather_small",
        out_shape=jax.ShapeDtypeStruct((8 * m_per, ncol), pack.dtype),
        in_specs=[pl.BlockSpec(memory_space=pltpu.VMEM)],
        out_specs=pl.BlockSpec(memory_space=pltpu.VMEM),
        scratch_shapes=[pltpu.SemaphoreType.DMA((7,)), pltpu.SemaphoreType.DMA((7,)), pltpu.SemaphoreType.DMA],
    )(pack)


def _add_pair(g, got, c_arr, splits=None):
    _, rows, _ = g.shape
    splits = splits or [rows]

    def body(c_ref, g_ref, r_ref, *o_refs):
        at = 0
        for o_ref, n in zip(o_refs, splits):
            o_ref[...] = (g_ref[:, at:at + n, :] + r_ref[:, at:at + n, :]).astype(BF16)
            at += n

    return pl.pallas_call(
        body, name="grad_add_pair", out_shape=[jax.ShapeDtypeStruct((4, n, HALF), BF16) for n in splits],
        grid_spec=pltpu.PrefetchScalarGridSpec(
            num_scalar_prefetch=1, grid=(4,),
            in_specs=[pl.BlockSpec((1, rows, HALF), lambda j, c: (j, 0, c[0])),
                      pl.BlockSpec((1, rows, HALF), lambda j, c: (j, 0, 0))],
            out_specs=[pl.BlockSpec((1, n, HALF), lambda j, c: (j, 0, 0)) for n in splits]),
        compiler_params=_params(("arbitrary",)),
    )(c_arr, g, got)


def _add_chips(g, pair_got, chip_gots, pos_arr):
    _, rows, _ = g.shape

    def body(pos_ref, g_ref, p_ref, *refs):
        o_ref, at = refs[-1], 0
        for r_ref in refs[:-1]:
            n = r_ref.shape[1]
            own = g_ref[0, at:at + n, :] + p_ref[0, at:at + n, :]
            o_ref[at:at + n, :] = ((own + r_ref[0].astype(F32)) + r_ref[1].astype(F32)) + r_ref[2].astype(F32)
            at += n

    return pl.pallas_call(
        body, name="grad_add_chips", out_shape=jax.ShapeDtypeStruct((rows, HALF), F32),
        grid_spec=pltpu.PrefetchScalarGridSpec(
            num_scalar_prefetch=1, grid=(1,),
            in_specs=[pl.BlockSpec((1, rows, HALF), lambda i, p: (p[0], 0, p[1])),
                      pl.BlockSpec((1, rows, HALF), lambda i, p: (p[0], 0, 0))]
            + [pl.BlockSpec(t.shape, lambda i, p: (0, 0, 0)) for t in chip_gots],
            out_specs=pl.BlockSpec((rows, HALF), lambda i, p: (0, 0))),
        compiler_params=_params(("arbitrary",)),
    )(pos_arr, g, pair_got, *chip_gots)


def _add_chips_many(items, fuse=NO_FUSE):
    n = len(items)
    rows = [it[0].shape[1] for it in items]
    n_got = [len(it[2]) for it in items]

    def body(*refs):
        g_refs, p_refs = refs[:n], refs[n:2 * n]
        got_refs = refs[2 * n:2 * n + sum(n_got)]
        o_refs = refs[2 * n + sum(n_got):3 * n + sum(n_got)]
        scr = refs[3 * n + sum(n_got):]
        x, y, c = _mesh_pos()
        chip = 2 * x + y
        mine = pl.ds(pl.multiple_of(c * HALF, 128), HALF)
        copies = []
        for a in range(n):
            copies.append((pltpu.make_async_copy(g_refs[a].at[chip, :, mine], scr[2 * a], scr[2 * n].at[2 * a]),
                           pltpu.make_async_copy(p_refs[a].at[chip], scr[2 * a + 1], scr[2 * n].at[2 * a + 1])))
        for cg, cp in copies:
            cg.start()
            cp.start()
        at_ref = 0
        for a in range(n):
            copies[a][0].wait()
            copies[a][1].wait()
            at = 0
            for r_ref in got_refs[at_ref:at_ref + n_got[a]]:
                k = r_ref.shape[1]
                own = scr[2 * a][at:at + k, :] + scr[2 * a + 1][at:at + k, :]
                o_refs[a][at:at + k, :] = ((own + r_ref[0].astype(F32)) + r_ref[1].astype(F32)) + r_ref[2].astype(F32)
                at += k
            at_ref += n_got[a]

    gots = [t for it in items for t in it[2]]
    outs, got = _fused_call(
        body, fuse, name="grad_add_chips_many", grid=(1,),
        in_specs=[ANY] * (2 * n) + [pl.BlockSpec(t.shape, lambda i: (0, 0, 0)) for t in gots],
        out_specs=[pl.BlockSpec((r, HALF), lambda i: (0, 0)) for r in rows],
        out_shape=[jax.ShapeDtypeStruct((r, HALF), F32) for r in rows],
        scratch_shapes=[pltpu.VMEM((r, HALF), F32) for r in rows for _ in range(2)] + [pltpu.SemaphoreType.DMA((2 * n,))],
        operands=(*[it[0] for it in items], *[it[1] for it in items], *gots))
    return outs, got


def _pack_small(acc2, acc1, lvec, cacc, gvec, dcw, dgup, dmeta):
    bsz = lvec.shape[0]

    def body(a2_ref, a1_ref, lv_ref, ca_ref, gv_ref, cw_ref, gu_ref, dm_ref, o_ref):
        def put(name, val):
            r0, nr, l0, nl = SMALL_AT[name]
            o_ref[r0:r0 + nr, l0:l0 + nl] = val

        def put_folded(row0, val, per_row):
            w = val.shape[1]
            for i in range(val.shape[0]):
                o_ref[row0 + i // per_row:row0 + i // per_row + 1, (i % per_row) * w:(i % per_row + 1) * w] = val[i:i + 1]

        over_b = lambda f: functools.reduce(lambda a, b: a + b, [f(b) for b in range(bsz)])
        o_ref[...] = jnp.zeros_like(o_ref)
        put("loss", a2_ref[0:1, :])
        put("ln2_g", a2_ref[1:2, :])
        put("ln2_b", a2_ref[2:3, :])
        put("ln1_g", a1_ref[0:1, :])
        put("ln1_b", a1_ref[1:2, :])
        put("ln_in_g", over_b(lambda b: lv_ref[b, 0:1, :]))
        put("ln_in_b", over_b(lambda b: lv_ref[b, 1:2, :]))
        put("conv_b", ca_ref[0:1, :])
        put("conv_ln_g", ca_ref[1:2, :])
        put("conv_ln_b", ca_ref[2:3, :])
        put("gate_bias", over_b(lambda b: gv_ref[b, 0:1, :]))
        put("gla_norm_g", over_b(lambda b: gv_ref[b, 1:2, 0:128] + gv_ref[b, 1:2, 128:256]))
        put_folded(SMALL_CONV_W, over_b(lambda b: cw_ref[b]), D_MODEL // D_CONV)
        put_folded(SMALL_GATE_UP, over_b(lambda b: gu_ref[b, 0:GLA_RANK, :]), D_MODEL // 256)
        put("meta_tokens", over_b(lambda b: dm_ref[b]))

    return pl.pallas_call(
        body, name="pack_small", out_shape=jax.ShapeDtypeStruct((SMALL_ROWS, D_MODEL), F32),
    )(acc2, acc1, lvec, cacc, gvec, dcw, dgup, dmeta)


def _sum_small(gathered):
    def body(g_ref, o_ref, loss_ref):
        acc = g_ref[0:SMALL_ROWS, :]
        for d in range(1, 8):
            acc = acc + g_ref[d * SMALL_ROWS:(d + 1) * SMALL_ROWS, :]
        o_ref[...] = acc
        loss_ref[...] = jnp.sum(acc[0:1, :], axis=1, keepdims=True)

    return pl.pallas_call(
        body, name="sum_small",
        out_shape=[jax.ShapeDtypeStruct((SMALL_ROWS, D_MODEL), F32), jax.ShapeDtypeStruct((1, 1), F32)],
    )(gathered)


def _adamw_math(w, g, m, v):
    c1 = 1.0 - ADAM_B1 ** ADAM_STEP
    c2 = 1.0 - ADAM_B2 ** ADAM_STEP
    mn = ADAM_B1 * m + (1.0 - ADAM_B1) * g
    vn = ADAM_B2 * v + (1.0 - ADAM_B2) * (g * g)
    return -ADAM_LR * ((mn / c1) / (jnp.sqrt(vn / c2) + ADAM_EPS) + ADAM_WD * w), mn, vn


def _adamw_small(red, sharded_grads, params):
    names = list(params)
    sharded = [k for k in names if k in sharded_grads]
    n, ns = len(names), len(sharded)

    def body(*refs):
        red_ref, sg_refs, p_refs, o_refs = refs[0], refs[1:1 + ns], refs[1 + ns:1 + ns + 3 * n], refs[1 + ns + 3 * n:]
        for i, k in enumerate(names):
            w_ref, m_ref, v_ref = p_refs[3 * i:3 * i + 3]
            if k in sharded:
                g = sg_refs[sharded.index(k)][...]
            else:
                r0, nr, l0, nl = SMALL_AT[k]
                g = red_ref[r0:r0 + nr, l0:l0 + nl]
            dl, mn, vn = _adamw_math(w_ref[...], g, m_ref[...], v_ref[...])
            for o_ref, val in zip(o_refs[4 * i:4 * i + 4], (g, dl, mn, vn)):
                o_ref[...] = val

    flat = [a for k in names for a in params[k]]
    res = pl.pallas_call(
        body, name="adamw_small",
        out_shape=[jax.ShapeDtypeStruct(params[k][0].shape, F32) for k in names for _ in range(4)],
    )(red, *[sharded_grads[k] for k in sharded], *flat)
    return {k: tuple(res[4 * i:4 * i + 4]) for i, k in enumerate(names)}


def _adamw_halves(w, own, sib, m, v, c_arr):
    rows, cols = w.shape
    tr = 256 if rows % 256 == 0 else rows

    def body(c_ref, w_ref, own_ref, sib_ref, m_ref, v_ref, g_ref, d_ref, mo_ref, vo_ref):
        first = c_ref[0] == 0
        own, sib = own_ref[...], sib_ref[...]
        g = jnp.concatenate([jnp.where(first, own, sib), jnp.where(first, sib, own)], axis=1)
        g_ref[...] = g
        d_ref[...], mo_ref[...], vo_ref[...] = _adamw_math(w_ref[...], g, m_ref[...], v_ref[...])

    full = pl.BlockSpec((tr, cols), lambda i, c: (i, 0))
    half = pl.BlockSpec((tr, HALF), lambda i, c: (i, 0))
    return pl.pallas_call(
        body, name="adamw_halves", out_shape=[jax.ShapeDtypeStruct(w.shape, F32)] * 4,
        grid_spec=pltpu.PrefetchScalarGridSpec(
            num_scalar_prefetch=1, grid=(rows // tr,), in_specs=[full, half, half, full, full], out_specs=[full] * 4),
        compiler_params=_params(("parallel",)),
    )(c_arr, w, own, sib, m, v)


def _ln_in_fwd(x, meta, g, b, fuse=NO_FUSE):
    bsz, s, d = x.shape
    tp = s + HEAD
    nh = 2
    sh = s // nh
    rc = min(256, sh)

    def body(x_ref, meta_ref, g_ref, b_ref, s0_ref, s0b_ref):
        h = pl.program_id(1)
        gg, bb = g_ref[...], b_ref[...]

        @pl.when(h == 0)
        def _():
            s0_ref[0, 0:PAD, :] = jnp.zeros((PAD, d), F32)
            s0b_ref[0, 0:PAD, :] = jnp.zeros((PAD, d), BF16)
            mh, _ = _ln_stats(meta_ref[...])
            mv = mh * gg + bb
            s0_ref[0, PAD:HEAD, :] = mv
            s0b_ref[0, PAD:HEAD, :] = mv.astype(BF16)

        def step(i, carry):
            src = pl.ds(pl.multiple_of(i * rc, rc), rc)
            dst = pl.ds(pl.multiple_of(HEAD + h * sh + i * rc, 64), rc)
            xh, _ = _ln_stats(x_ref[0, src, :])
            val = xh * gg + bb
            s0_ref[0, dst, :] = val
            s0b_ref[0, dst, :] = val.astype(BF16)
            return carry

        lax.fori_loop(0, sh // rc, step, 0)

    full = lambda bi, hi: (bi, 0, 0)
    (s0, s0b), got = _fused_call(
        body, fuse, name="ln_in_fwd",
        out_shape=[jax.ShapeDtypeStruct((bsz, tp, d), F32), jax.ShapeDtypeStruct((bsz, tp, d), BF16)],
        grid=(bsz, nh),
        in_specs=[pl.BlockSpec((1, sh, d), lambda bi, hi: (bi, hi, 0)),
                  pl.BlockSpec((N_META, d), lambda bi, hi: (0, 0)),
                  pl.BlockSpec((1, d), lambda bi, hi: (0, 0)),
                  pl.BlockSpec((1, d), lambda bi, hi: (0, 0))],
        out_specs=[pl.BlockSpec((1, tp, d), full)] * 2, scratch_shapes=[], operands=(x, meta, g, b),
        late_relay=True)
    return s0, s0b, got


def _in_proj(s0b, w_int, fuse=NO_FUSE):
    r, d = s0b.shape
    tm = _row_tile(r)

    def body(a_ref, w_ref, o_ref):
        o_ref[...] = _nt(a_ref[...], w_ref[...])

    (u,), got = _fused_call(
        body, fuse, name="in_proj", out_shape=[jax.ShapeDtypeStruct((r, D_IN_PAD), F32)],
        grid=(r // tm,),
        in_specs=[pl.BlockSpec((tm, d), lambda i: (i, 0)), pl.BlockSpec((D_IN_PAD, d), lambda i: (0, 0))],
        out_specs=[pl.BlockSpec((tm, D_IN_PAD), lambda i: (i, 0))], scratch_shapes=[], operands=(s0b, w_int))
    return u, got


def _conv_fwd(u, conv_w, conv_b, fuse=NO_FUSE):
    bsz, tp, _ = u.shape
    nchunk = tp // CHUNK
    win = CHUNK + 32
    nct = D_CONV // 128

    def body(cv_ref, cg_ref, w_ref, cb_ref, hc_ref, h_scr, win_scr):
        h_scr[0:32, :] = jnp.zeros((32, 128), F32)
        h_scr[32:32 + tp, :] = cv_ref[0] * _sigmoid(cg_ref[0])
        cb = cb_ref[...]

        def step(n, carry):
            r0 = pl.multiple_of(n * CHUNK, CHUNK)
            win_scr[...] = h_scr[pl.ds(r0, win), :]
            acc = jnp.zeros((CHUNK, 128), F32)
            for j in range(CONV_WIDTH):
                acc = acc + w_ref[j:j + 1, :] * win_scr[2 + j:2 + j + CHUNK, :]
            hc_ref[0, pl.ds(r0, CHUNK), :] = acc + cb
            return carry

        lax.fori_loop(0, nchunk, step, 0)

    (hc,), got = _fused_call(
        body, fuse, name="conv_fwd", out_shape=[jax.ShapeDtypeStruct((bsz, tp, D_CONV), F32)],
        grid=(bsz, nct),
        in_specs=[pl.BlockSpec((1, tp, 128), lambda bi, t: (bi, 0, C_VAL // 128 + t)),
                  pl.BlockSpec((1, tp, 128), lambda bi, t: (bi, 0, C_GATE // 128 + t)),
                  pl.BlockSpec((32, 128), lambda bi, t: (0, t)),
                  pl.BlockSpec((1, 128), lambda bi, t: (0, t))],
        out_specs=[pl.BlockSpec((1, tp, 128), lambda bi, t: (bi, 0, t))],
        scratch_shapes=[pltpu.VMEM((tp + 32, 128), F32), pltpu.VMEM((win, 128), F32)],
        operands=(u, u, conv_w, conv_b))
    return hc, got


def _gla_group(nchunk):
    return 11 if nchunk % 11 == 0 else nchunk


def _bdot(a, b, ca, cb, precision=None):
    return lax.dot_general(a, b, (((ca,), (cb,)), ((0,), (0,))), preferred_element_type=F32, precision=precision)


def _bnn(a, b, **kw):
    return _bdot(a, b, 2, 1, **kw)


def _bnt(a, b, **kw):
    return _bdot(a, b, 2, 2, **kw)


def _gla_consts(nb):
    row = lax.broadcasted_iota(jnp.int32, (nb, CHUNK, CHUNK), 1)
    col = lax.broadcasted_iota(jnp.int32, (nb, CHUNK, CHUNK), 2)
    lane = lax.broadcasted_iota(jnp.int32, (1, 1, 128), 2)
    return row >= col, row <= col, [lane < GLA_DK, lane >= GLA_DK]


def _gla_group_terms(g, nb, q_ref, k_ref, gd_ref, gup_ref, gb_ref, tril):
    m = nb * CHUNK
    rows = pl.ds(pl.multiple_of(g * m, CHUNK), m)
    z = _nn(gd_ref[0, rows, :].astype(BF16), gup_ref[...]) + gb_ref[...]
    valid = g * m + lax.broadcasted_iota(jnp.int32, (m, 1), 0) >= PAD
    lg = jnp.where(valid, _log_sigmoid(z) * (1.0 / GLA_TAU), 0.0)
    bcum = _bnn(tril.astype(F32), lg.reshape(nb, CHUNK, 128), precision=lax.Precision.HIGHEST)
    blast = bcum[:, CHUNK - 1:CHUNK, :]
    eb = jnp.exp(bcum)
    enb = jnp.exp(-bcum)
    erest = jnp.exp(blast - bcum)
    q = (q_ref[0, rows, :] * Q_SCALE).reshape(nb, CHUNK, 128)
    k = k_ref[0, rows, :].reshape(nb, CHUNK, 128)
    return rows, valid, z, eb, enb, erest, jnp.exp(blast), q * eb, k * enb, k * erest


def _grid_ends(grid):
    ids = [pl.program_id(i) for i in range(len(grid))]
    first = functools.reduce(jnp.logical_and, [i == 0 for i in ids])
    last = functools.reduce(jnp.logical_and, [i == g - 1 for i, g in zip(ids, grid)])
    return first, last


def _gla_fwd(u, gup, gbias, gnorm, fuse=NO_FUSE):
    bsz, tp, _ = u.shape
    nchunk = tp // CHUNK
    nb = _gla_group(nchunk)

    def body(q_ref, k_ref, v_ref, r_ref, gd_ref, gup_ref, gb_ref, gn_ref, out_ref, o_ref, st_ref, s_scr):
        tril, _, hmask = _gla_consts(nb)
        s_scr[...] = jnp.zeros_like(s_scr)
        gn = gn_ref[...]

        def group(g, carry):
            rows, _, _, _, _, _, dec, qe, ke, kd = _gla_group_terms(g, nb, q_ref, k_ref, gd_ref, gup_ref, gb_ref, tril)
            keb, kdb = ke.astype(BF16), kd.astype(BF16)
            for h in range(2):
                cols = slice(h * GLA_DV, (h + 1) * GLA_DV)
                qh = jnp.where(hmask[h], qe, 0.0).astype(BF16)
                vh = v_ref[0, rows, cols].astype(BF16).reshape(nb, CHUNK, GLA_DV)
                a = jnp.where(tril, _bnt(qh, keb), 0.0).astype(BF16)
                st = s_scr[h]
                sts = []
                for n in range(nb):
                    st_ref[0, h, g * nb + n] = st
                    sts.append(st.astype(BF16))
                    st = dec[n] * st + _tn(vh[n], kdb[n])
                s_scr[h] = st
                o = (_bnn(a, vh) + _bnt(qh, jnp.stack(sts))).reshape(nb * CHUNK, GLA_DV)
                o_ref[0, rows, cols] = o
                rms = lax.rsqrt(jnp.mean(o * o, axis=-1, keepdims=True) + LN_EPS)
                rh = r_ref[0, rows, cols]
                out_ref[0, rows, cols] = (o * rms * gn * (rh * _sigmoid(rh))).astype(BF16)
            return carry

        lax.fori_loop(0, nchunk // nb, group, 0)

    res, got = _fused_call(
        body, fuse, name="gla_fwd",
        out_shape=[jax.ShapeDtypeStruct((bsz, tp, 512), BF16), jax.ShapeDtypeStruct((bsz, tp, 512), F32),
                   jax.ShapeDtypeStruct((bsz, GLA_HEADS, nchunk, GLA_DV, 128), F32)],
        grid=(bsz, 2),
        in_specs=[pl.BlockSpec((1, tp, 128), lambda bi, p: (bi, 0, C_Q // 128 + p)),
                  pl.BlockSpec((1, tp, 128), lambda bi, p: (bi, 0, C_K // 128 + p)),
                  pl.BlockSpec((1, tp, 256), lambda bi, p: (bi, 0, C_V // 256 + p)),
                  pl.BlockSpec((1, tp, 256), lambda bi, p: (bi, 0, C_R // 256 + p)),
                  pl.BlockSpec((1, tp, 128), lambda bi, p: (bi, 0, C_GD // 128)),
                  pl.BlockSpec((128, 128), lambda bi, p: (0, p)),
                  pl.BlockSpec((1, 128), lambda bi, p: (0, p)),
                  pl.BlockSpec((1, 128), lambda bi, p: (0, 0))],
        out_specs=[pl.BlockSpec((1, tp, 256), lambda bi, p: (bi, 0, p)),
                   pl.BlockSpec((1, tp, 256), lambda bi, p: (bi, 0, p)),
                   pl.BlockSpec((1, 2, nchunk, GLA_DV, 128), lambda bi, p: (bi, p, 0, 0, 0))],
        scratch_shapes=[pltpu.VMEM((2, GLA_DV, 128), F32)],
        operands=(u, u, u, u, u, gup, gbias, gnorm))
    return res[0], res[1], res[2], got


def _out_proj_ln1(hc, gla_out, w_out, s0, cg, cb, g1, b1, fuse=NO_FUSE):
    r, d = s0.shape
    tm = _row_tile(r)

    def body(hc_ref, a_ref, w_ref, s0_ref, cg_ref, cb_ref, g_ref, b_ref, co_ref, xh_ref, rstd_ref, s1b_ref):
        for rs in _sub_rows(tm):
            xc, _ = _ln_stats(hc_ref[rs, :])
            nv = xc * cg_ref[...] + cb_ref[...]
            co = (nv * _sigmoid(nv)).astype(BF16)
            co_ref[rs, :] = co
            mix = _nn(co, w_ref[0:D_CONV, :]) + _nn(a_ref[rs, :], w_ref[D_CONV:, :])
            xh, rstd = _ln_stats(ALPHA * s0_ref[rs, :] + mix)
            xh_ref[rs, :] = xh
            rstd_ref[rs, :] = rstd
            s1b_ref[rs, :] = (xh * g_ref[...] + b_ref[...]).astype(BF16)

    row = lambda n: pl.BlockSpec((tm, n), lambda i: (i, 0))
    vec = lambda n: pl.BlockSpec((1, n), lambda i: (0, 0))
    res, got = _fused_call(
        body, fuse, name="out_proj_ln1",
        out_shape=[jax.ShapeDtypeStruct((r, D_CONV), BF16), jax.ShapeDtypeStruct((r, d), F32),
                   jax.ShapeDtypeStruct((r, 1), F32), jax.ShapeDtypeStruct((r, d), BF16)],
        grid=(r // tm,),
        in_specs=[row(D_CONV), row(512), pl.BlockSpec((d, d), lambda i: (0, 0)), row(d),
                  vec(D_CONV), vec(D_CONV), vec(d), vec(d)],
        out_specs=[row(D_CONV), row(d), row(1), row(d)], scratch_shapes=[],
        operands=(hc, gla_out, w_out, s0, cg, cb, g1, b1))
    return res[0], res[1], res[2], res[3], got


def _ffn1(s1b, w1, fuse=NO_FUSE):
    r, d = s1b.shape
    tm = _row_tile(r)
    ns, _, wn = w1.shape

    def body(a_ref, w_ref, o_ref, sq_ref):
        a = a_ref[...]
        for j in range(ns):
            ra = jnp.maximum(_nn(a, w_ref[j]), 0.0).astype(BF16)
            o_ref[:, j * wn:(j + 1) * wn] = ra
            sq_ref[:, j * wn:(j + 1) * wn] = ra * ra

    (ra, h2), got = _fused_call(
        body, fuse, name="ffn1", out_shape=[jax.ShapeDtypeStruct((r, D_FF), BF16)] * 2, grid=(r // tm,),
        in_specs=[pl.BlockSpec((tm, d), lambda i: (i, 0)), pl.BlockSpec(w1.shape, lambda i: (0, 0, 0))],
        out_specs=[pl.BlockSpec((tm, D_FF), lambda i: (i, 0))] * 2, scratch_shapes=[], operands=(s1b, w1))
    return ra, h2, got


def _ffn2_ln2_loss(ra, w2, xhat1, g1, b1, g2, b2, tgt, tp):
    r, d = xhat1.shape
    tm = _row_tile(tp)
    per = tp // tm

    def body(ra_ref, w_ref, xh1_ref, g1_ref, b1_ref, g2_ref, b2_ref, tgt_ref, dt_ref, dtb_ref, acc_ref, t_ref, sem):
        i = pl.program_id(0)
        b, j = i // per, i % per

        @pl.when(i == 0)
        def _():
            acc_ref[...] = jnp.zeros_like(acc_ref)

        head_copy = pltpu.make_async_copy(tgt_ref.at[b, pl.ds(0, tm - HEAD), :], t_ref.at[pl.ds(HEAD, tm - HEAD), :], sem)
        body_copy = pltpu.make_async_copy(
            tgt_ref.at[b, pl.ds(pl.multiple_of(jnp.maximum(j * tm - HEAD, 0), 64), tm), :], t_ref, sem)

        @pl.when(j == 0)
        def _():
            t_ref[0:HEAD, :] = jnp.zeros((HEAD, d), F32)
            head_copy.start()

        pl.when(j > 0)(body_copy.start)

        sums = [jnp.zeros((1, d), F32)] * 3
        for rs in _sub_rows(tm):
            f = _nn(ra_ref[rs, :], w_ref[...])
            if rs.start == 0:
                pl.when(j == 0)(head_copy.wait)
                pl.when(j > 0)(body_copy.wait)
            s1 = xh1_ref[rs, :] * g1_ref[...] + b1_ref[...]
            xh2, rstd2 = _ln_stats(ALPHA * s1 + f)
            y = xh2 * g2_ref[...] + b2_ref[...]
            rowid = (i % per) * tm + rs.start + lax.broadcasted_iota(jnp.int32, (rs.stop - rs.start, 1), 0)
            e = jnp.where(rowid >= HEAD, y - t_ref[rs, :], 0.0)
            dy = e * (1.0 / d)
            dt2 = _ln_bwd(dy * g2_ref[...], xh2, rstd2)
            dt_ref[rs, :] = dt2
            dtb_ref[rs, :] = dt2.astype(BF16)
            sums = [sums[0] + (0.5 / d) * jnp.sum(e * e, axis=0, keepdims=True),
                    sums[1] + jnp.sum(dy * xh2, axis=0, keepdims=True), sums[2] + jnp.sum(dy, axis=0, keepdims=True)]
        for k in range(3):
            acc_ref[k:k + 1, :] += sums[k]

    row = lambda n: pl.BlockSpec((tm, n), lambda i: (i, 0))
    vec = pl.BlockSpec((1, d), lambda i: (0, 0))
    return pl.pallas_call(
        body, name="ffn2_ln2_loss",
        out_shape=[jax.ShapeDtypeStruct((r, d), F32), jax.ShapeDtypeStruct((r, d), BF16),
                   jax.ShapeDtypeStruct((8, d), F32)],
        grid=(r // tm,),
        in_specs=[row(D_FF), pl.BlockSpec((D_FF, d), lambda i: (0, 0)), row(d), vec, vec, vec, vec, ANY],
        out_specs=[row(d), row(d), pl.BlockSpec((8, d), lambda i: (0, 0))],
        scratch_shapes=[pltpu.VMEM((tm, d), F32), pltpu.SemaphoreType.DMA],
        compiler_params=_params(("arbitrary",)),
    )(ra, w2, xhat1, g1, b1, g2, b2, tgt)


def _ffn_bwd_da(dt2b, w2, ra):
    r, d = dt2b.shape
    tm = _row_tile(r)

    def body(g_ref, w_ref, ra_ref, o_ref):
        o_ref[...] = (_nt(g_ref[...], w_ref[...]) * (2.0 * ra_ref[...].astype(F32))).astype(BF16)

    return pl.pallas_call(
        body, name="ffn_bwd_da", out_shape=jax.ShapeDtypeStruct((r, D_FF), BF16),
        grid=(r // tm,),
        in_specs=[pl.BlockSpec((tm, d), lambda i: (i, 0)), pl.BlockSpec((D_FF, d), lambda i: (0, 0)),
                  pl.BlockSpec((tm, D_FF), lambda i: (i, 0))],
        out_specs=pl.BlockSpec((tm, D_FF), lambda i: (i, 0)),
        compiler_params=_params(("parallel",)),
    )(dt2b, w2, ra)


def _ffn_bwd_ln1(da, w1, dt2, xhat1, rstd1, g1):
    r, d = dt2.shape
    tm = _row_tile(r)

    def body(da_ref, w_ref, dt2_ref, xh_ref, rstd_ref, g_ref, dt_ref, dtb_ref, acc_ref):
        @pl.when(pl.program_id(0) == 0)
        def _():
            acc_ref[...] = jnp.zeros_like(acc_ref)

        sums = [jnp.zeros((1, d), F32)] * 2
        for rs in _sub_rows(tm):
            ds1 = ALPHA * dt2_ref[rs, :]
            for j in range(w1.shape[0]):
                ds1 = ds1 + _nt(da_ref[rs, j * w1.shape[2]:(j + 1) * w1.shape[2]], w_ref[j])
            xh = xh_ref[rs, :]
            dt1 = _ln_bwd(ds1 * g_ref[...], xh, rstd_ref[rs, :])
            dt_ref[rs, :] = dt1
            dtb_ref[rs, :] = dt1.astype(BF16)
            sums = [sums[0] + jnp.sum(ds1 * xh, axis=0, keepdims=True), sums[1] + jnp.sum(ds1, axis=0, keepdims=True)]
        for k in range(2):
            acc_ref[k:k + 1, :] += sums[k]

    row = lambda n: pl.BlockSpec((tm, n), lambda i: (i, 0))
    return pl.pallas_call(
        body, name="ffn_bwd_ln1",
        out_shape=[jax.ShapeDtypeStruct((r, d), F32), jax.ShapeDtypeStruct((r, d), BF16),
                   jax.ShapeDtypeStruct((8, d), F32)],
        grid=(r // tm,),
        in_specs=[row(D_FF), pl.BlockSpec(w1.shape, lambda i: (0, 0, 0)), row(d), row(d), row(1),
                  pl.BlockSpec((1, d), lambda i: (0, 0))],
        out_specs=[row(d), row(d), pl.BlockSpec((8, d), lambda i: (0, 0))],
        compiler_params=_params(("arbitrary",)),
    )(da, w1, dt2, xhat1, rstd1, g1)


def _matmul_tn(lhs, rhs, bm, name="matmul_tn", fuse=NO_FUSE):
    r, m = lhs.shape
    n = rhs.shape[1]
    tk = _reduce_tile(r, True)

    def body(a_ref, b_ref, o_ref):
        @pl.when(pl.program_id(1) == 0)
        def _():
            o_ref[...] = jnp.zeros_like(o_ref)

        o_ref[...] += _tn(a_ref[...], b_ref[...])

    (out,), got = _fused_call(
        body, fuse, name=name, out_shape=[jax.ShapeDtypeStruct((m, n), F32)], grid=(m // bm, r // tk),
        in_specs=[pl.BlockSpec((tk, bm), lambda i, k: (k, i)), pl.BlockSpec((tk, n), lambda i, k: (k, 0))],
        out_specs=[pl.BlockSpec((bm, n), lambda i, k: (i, 0))], scratch_shapes=[], operands=(lhs, rhs))
    return out, got


def _grad_w_ff1(s1b, da, fuse=NO_FUSE):
    r, d = s1b.shape
    wn = da.shape[1] // 4
    tk = _reduce_tile(r, True)

    def body(a_ref, b_ref, o_ref):
        @pl.when(pl.program_id(1) == 0)
        def _():
            o_ref[...] = jnp.zeros_like(o_ref)

        o_ref[0] += _tn(a_ref[...], b_ref[...])

    (out,), got = _fused_call(
        body, fuse, name="grad_w_ff1", out_shape=[jax.ShapeDtypeStruct((4, d, wn), F32)], grid=(4, r // tk),
        in_specs=[pl.BlockSpec((tk, d), lambda j, k: (k, 0)), pl.BlockSpec((tk, wn), lambda j, k: (k, j))],
        out_specs=[pl.BlockSpec((1, d, wn), lambda j, k: (j, 0, 0))], scratch_shapes=[], operands=(s1b, da))
    return out, got


def _grad_w_out(conv_of, gla_of, dt1b, fuse=NO_FUSE):
    r, n = dt1b.shape
    tk = _reduce_tile(r, True)

    def body(a_ref, b_ref, g_ref, o_ref):
        @pl.when(pl.program_id(1) == 0)
        def _():
            o_ref[...] = jnp.zeros_like(o_ref)

        @pl.when(pl.program_id(0) == 0)
        def _():
            o_ref[...] += _tn(a_ref[...], g_ref[...])

        @pl.when(pl.program_id(0) == 1)
        def _():
            o_ref[...] += _tn(b_ref[...], g_ref[...])

    lhs = pl.BlockSpec((tk, 512), lambda i, k: (k, 0))
    (out,), got = _fused_call(
        body, fuse, name="grad_w_out", out_shape=[jax.ShapeDtypeStruct((2 * 512, n), F32)], grid=(2, r // tk),
        in_specs=[lhs, lhs, pl.BlockSpec((tk, n), lambda i, k: (k, 0))],
        out_specs=[pl.BlockSpec((512, n), lambda i, k: (i, 0))], scratch_shapes=[],
        operands=(conv_of, gla_of, dt1b))
    return out, got


def _out_proj_bwd(dt1b, w_out, hc, cg, cb, fuse=NO_FUSE):
    r, d = dt1b.shape
    tm = _row_tile(r)

    def body(g_ref, w_ref, hc_ref, cg_ref, cb_ref, dhc_ref, dgla_ref, acc_ref):
        @pl.when(pl.program_id(0) == 0)
        def _():
            acc_ref[...] = jnp.zeros_like(acc_ref)

        gg = cg_ref[...]
        sums = [jnp.zeros((1, D_CONV), F32)] * 3
        for rs in _sub_rows(tm):
            dmix = _nt(g_ref[rs, :], w_ref[...])
            dgla_ref[rs, :] = dmix[:, D_CONV:]
            xh, rstd = _ln_stats(hc_ref[rs, :])
            nv = xh * gg + cb_ref[...]
            sig = _sigmoid(nv)
            dn = dmix[:, :D_CONV] * (sig * (1.0 + nv * (1.0 - sig)))
            dhc = _ln_bwd(dn * gg, xh, rstd)
            dhc_ref[rs, :] = dhc
            sums = [sums[0] + jnp.sum(dhc, axis=0, keepdims=True), sums[1] + jnp.sum(dn * xh, axis=0, keepdims=True),
                    sums[2] + jnp.sum(dn, axis=0, keepdims=True)]
        for k in range(3):
            acc_ref[k:k + 1, :] += sums[k]

    row = lambda n: pl.BlockSpec((tm, n), lambda i: (i, 0))
    vec = pl.BlockSpec((1, D_CONV), lambda i: (0, 0))
    res, got = _fused_call(
        body, fuse, name="out_proj_bwd",
        out_shape=[jax.ShapeDtypeStruct((r, D_CONV), F32), jax.ShapeDtypeStruct((r, 512), F32),
                   jax.ShapeDtypeStruct((8, D_CONV), F32)],
        grid=(r // tm,),
        in_specs=[row(d), pl.BlockSpec((d, d), lambda i: (0, 0)), row(D_CONV), vec, vec],
        out_specs=[row(D_CONV), row(512), pl.BlockSpec((8, D_CONV), lambda i: (0, 0))], scratch_shapes=[],
        operands=(dt1b, w_out, hc, cg, cb))
    return res[0], res[1], res[2], got


def _conv_bwd(dhc, u, conv_w, fuse=NO_FUSE):
    bsz, tp, _ = u.shape
    nchunk = tp // CHUNK
    win = CHUNK + 32
    nct = D_CONV // 128

    def body(dhc_ref, cv_ref, cg_ref, w_ref, dv_ref, dg_ref, dw_ref, h_scr, dhc_scr, hwin, dwin, dw_scr):
        h_scr[0:32, :] = jnp.zeros((32, 128), F32)
        h_scr[32:32 + tp, :] = cv_ref[0] * _sigmoid(cg_ref[0])
        dhc_scr[0:tp, :] = dhc_ref[0]
        dhc_scr[tp:tp + 32, :] = jnp.zeros((32, 128), F32)
        dw_scr[...] = jnp.zeros_like(dw_scr)

        def step(n, carry):
            r0 = pl.multiple_of(n * CHUNK, CHUNK)
            rows = pl.ds(r0, CHUNK)
            hwin[...] = h_scr[pl.ds(r0, win), :]
            dwin[...] = dhc_scr[pl.ds(r0, win), :]
            dcur = dwin[0:CHUNK, :]
            acc = jnp.zeros((CHUNK, 128), F32)
            for j in range(CONV_WIDTH):
                acc = acc + w_ref[j:j + 1, :] * dwin[30 - j:30 - j + CHUNK, :]
                prod = dcur * hwin[2 + j:2 + j + CHUNK, :]
                dw_scr[j * 8:(j + 1) * 8, :] += jnp.sum(prod.reshape(CHUNK // 8, 8, 128), axis=0)
            cg = cg_ref[0, rows, :]
            sig = _sigmoid(cg)
            rowid = n * CHUNK + lax.broadcasted_iota(jnp.int32, (CHUNK, 1), 0)
            dh = jnp.where(rowid >= PAD, acc, 0.0)
            dv_ref[0, rows, :] = (dh * sig).astype(BF16)
            dg_ref[0, rows, :] = (dh * cv_ref[0, rows, :] * sig * (1.0 - sig)).astype(BF16)
            return carry

        lax.fori_loop(0, nchunk, step, 0)
        dw_ref[0] = jnp.zeros((32, 128), F32)
        for j in range(CONV_WIDTH):
            dw_ref[0, j:j + 1, :] = jnp.sum(dw_scr[j * 8:(j + 1) * 8, :], axis=0, keepdims=True)

    blk = lambda off: pl.BlockSpec((1, tp, 128), lambda bi, t: (bi, 0, off // 128 + t))
    res, got = _fused_call(
        body, fuse, name="conv_bwd",
        out_shape=[jax.ShapeDtypeStruct((bsz, tp, D_CONV), BF16), jax.ShapeDtypeStruct((bsz, tp, D_CONV), BF16),
                   jax.ShapeDtypeStruct((bsz, 32, D_CONV), F32)],
        grid=(bsz, nct),
        in_specs=[blk(0), blk(C_VAL), blk(C_GATE), pl.BlockSpec((32, 128), lambda bi, t: (0, t))],
        out_specs=[blk(0), blk(0), pl.BlockSpec((1, 32, 128), lambda bi, t: (bi, 0, t))],
        scratch_shapes=[pltpu.VMEM((tp + 32, 128), F32), pltpu.VMEM((tp + 32, 128), F32),
                        pltpu.VMEM((win, 128), F32), pltpu.VMEM((win, 128), F32),
                        pltpu.VMEM((CONV_WIDTH * 8, 128), F32)],
        operands=(dhc, u, u, conv_w))
    return res[0], res[1], res[2], got


def _gla_bwd(dgla, u, o_pre, states, gup, gbias, gnorm, fuse=NO_FUSE):
    bsz, tp, _ = u.shape
    nchunk = tp // CHUNK
    nb = _gla_group(nchunk)

    def body(dy_ref, q_ref, k_ref, v_ref, r_ref, gd_ref, o_ref, st_ref, gup_ref, gb_ref, gn_ref,
             dq_ref, dk_ref, dv_ref, dr_ref, dgd_ref, dgup_ref, vec_ref, h_scr, gup_acc):
        tril, triu, hmask = _gla_consts(nb)
        h_scr[...] = jnp.zeros_like(h_scr)
        gup_acc[...] = jnp.zeros_like(gup_acc)
        gn = gn_ref[...]
        gupb = gup_ref[...]
        m = nb * CHUNK
        ngroup = nchunk // nb

        def group(i, carry):
            dbias, dgn = carry
            g = ngroup - 1 - i
            rows, valid, z, eb, enb, erest, dec, qe, ke, kd = _gla_group_terms(
                g, nb, q_ref, k_ref, gd_ref, gup_ref, gb_ref, tril)
            keb, kdb = ke.astype(BF16), kd.astype(BF16)
            dqe = jnp.zeros((nb, CHUNK, 128), F32)
            dke = jnp.zeros((nb, CHUNK, 128), F32)
            dkd = jnp.zeros((nb, CHUNK, 128), F32)
            ddec = jnp.zeros((nb, 1, 128), F32)
            for h in range(2):
                cols = slice(h * GLA_DV, (h + 1) * GLA_DV)
                o = o_ref[0, rows, cols]
                rh = r_ref[0, rows, cols]
                dy = dy_ref[0, rows, cols]
                rms = lax.rsqrt(jnp.mean(o * o, axis=-1, keepdims=True) + LN_EPS)
                nrm = o * rms
                sig = _sigmoid(rh)
                sw = rh * sig
                dr_ref[0, rows, cols] = (dy * nrm * gn * (sig * (1.0 + rh * (1.0 - sig)))).astype(BF16)
                dgn = dgn + jnp.sum(dy * nrm * sw, axis=0, keepdims=True)
                dn = dy * gn * sw
                do = rms * (dn - nrm * jnp.mean(dn * nrm, axis=-1, keepdims=True))
                dob = do.astype(BF16).reshape(nb, CHUNK, GLA_DV)
                qh = jnp.where(hmask[h], qe, 0.0).astype(BF16)
                vh = v_ref[0, rows, cols].astype(BF16).reshape(nb, CHUNK, GLA_DV)
                ht = h_scr[h]
                hts = [None] * nb
                for n in reversed(range(nb)):
                    hts[n] = ht
                    ht = dec[n] * ht + _tn(dob[n], qh[n])
                h_scr[h] = ht
                htf = jnp.stack(hts)
                htb = htf.astype(BF16)
                st = st_ref[0, h, pl.ds(g * nb, nb)]
                at = jnp.where(triu, _bnt(keb, qh), 0.0).astype(BF16)
                da = jnp.where(tril, _bnt(dob, vh), 0.0).astype(BF16)
                dat = jnp.where(triu, _bnt(vh, dob), 0.0).astype(BF16)
                dqe = dqe + jnp.where(hmask[h], _bnn(da, keb) + _bnn(dob, st.astype(BF16)), 0.0)
                dke = dke + _bnn(dat, qh)
                dv_ref[0, rows, cols] = (_bnn(at, dob) + _bnt(kdb, htb)).reshape(m, GLA_DV).astype(BF16)
                dkd = dkd + jnp.where(hmask[h], _bnn(vh, htb), 0.0)
                ddec = ddec + jnp.where(hmask[h], jnp.sum(htf * st, axis=1, keepdims=True), 0.0)
            dq_ref[0, rows, :] = (dqe * eb * Q_SCALE).reshape(m, 128).astype(BF16)
            dk_ref[0, rows, :] = (dke * enb + dkd * erest).reshape(m, 128).astype(BF16)
            db = dqe * qe - dke * ke - dkd * kd
            dblast = jnp.sum(dkd * kd, axis=1, keepdims=True) + ddec * dec
            lastrow = lax.broadcasted_iota(jnp.int32, (1, CHUNK, 1), 1) == CHUNK - 1
            db = db + jnp.where(lastrow, dblast, 0.0)
            dlg = _bnn(triu.astype(F32), db, precision=lax.Precision.HIGHEST).reshape(m, 128)
            dz = jnp.where(valid, dlg, 0.0) * (1.0 / GLA_TAU) * (1.0 - _sigmoid(z))
            dzb = dz.astype(BF16)
            dgd_ref[0, 0, rows, :] = _nt(dzb, gupb).astype(BF16)
            gup_acc[...] += _tn(gd_ref[0, rows, :].astype(BF16), dzb)
            return dbias + jnp.sum(dz, axis=0, keepdims=True), dgn

        zero = jnp.zeros((1, 128), F32)
        dbias, dgn = lax.fori_loop(0, ngroup, group, (zero, zero))
        dgup_ref[0] = gup_acc[...]
        vec_ref[0] = jnp.zeros((8, 128), F32)
        vec_ref[0, 0:1, :] = dbias
        vec_ref[0, 1:2, :] = dgn

    pair = lambda w, off: pl.BlockSpec((1, tp, w), lambda bi, p: (bi, 0, off // w + p))
    return _fused_call(
        body, fuse, name="gla_bwd",
        out_shape=[jax.ShapeDtypeStruct((bsz, tp, 256), BF16), jax.ShapeDtypeStruct((bsz, tp, 256), BF16),
                   jax.ShapeDtypeStruct((bsz, tp, 512), BF16), jax.ShapeDtypeStruct((bsz, tp, 512), BF16),
                   jax.ShapeDtypeStruct((bsz, 2, tp, 128), BF16), jax.ShapeDtypeStruct((bsz, 128, 256), F32),
                   jax.ShapeDtypeStruct((bsz, 8, 256), F32)],
        grid=(bsz, 2),
        in_specs=[pair(256, 0), pair(128, C_Q), pair(128, C_K), pair(256, C_V), pair(256, C_R),
                  pl.BlockSpec((1, tp, 128), lambda bi, p: (bi, 0, C_GD // 128)),
                  pair(256, 0),
                  pl.BlockSpec((1, 2, nchunk, GLA_DV, 128), lambda bi, p: (bi, p, 0, 0, 0)),
                  pl.BlockSpec((128, 128), lambda bi, p: (0, p)),
                  pl.BlockSpec((1, 128), lambda bi, p: (0, p)),
                  pl.BlockSpec((1, 128), lambda bi, p: (0, 0))],
        out_specs=[pair(128, 0), pair(128, 0), pair(256, 0), pair(256, 0),
                   pl.BlockSpec((1, 1, tp, 128), lambda bi, p: (bi, p, 0, 0)),
                   pl.BlockSpec((1, 128, 128), lambda bi, p: (bi, 0, p)),
                   pl.BlockSpec((1, 8, 128), lambda bi, p: (bi, 0, p))],
        scratch_shapes=[pltpu.VMEM((2, GLA_DV, 128), F32), pltpu.VMEM((128, 128), F32)],
        operands=(dgla, u, u, u, u, u, o_pre, states, gup, gbias, gnorm))


_DU_OFFSETS = (C_VAL, C_GATE, C_Q, C_K, C_V, C_R)
_DU_WIDTHS = (512, 512, 256, 256, 512, 512)


def _du_specs(tm, per, row_map):
    specs = [pl.BlockSpec((tm, w), row_map) for w in _DU_WIDTHS]
    for p in range(2):
        specs.append(pl.BlockSpec((1, 1, tm, 128), lambda *ix, p=p: (row_map(*ix)[0] // per, p, row_map(*ix)[0] % per, 0)))
    return specs


def _du_pieces(refs):
    out = [(off, ref[...]) for off, ref in zip(_DU_OFFSETS, refs[:6])]
    dgd = (refs[6][0, 0].astype(F32) + refs[7][0, 0].astype(F32)).astype(BF16)
    out.append((C_GD, dgd))
    return out


def _in_proj_bwd(pieces, dgd, w_int, dt1, tp, fuse=NO_FUSE):
    r, d = dt1.shape
    tm = _row_tile(tp)
    per = tp // tm

    def body(*refs):
        w_ref, dt_ref, o_ref = refs[8:]
        acc = ALPHA * dt_ref[...]
        for off, val in _du_pieces(refs[:8]):
            acc = acc + _nn(val, w_ref[off:off + val.shape[1], :])
        o_ref[...] = acc

    row = lambda i: (i, 0)
    (ds0,), got = _fused_call(
        body, fuse, name="in_proj_bwd", out_shape=[jax.ShapeDtypeStruct((r, d), F32)], grid=(r // tm,),
        in_specs=_du_specs(tm, per, row) + [pl.BlockSpec((D_IN_PAD, d), lambda i: (0, 0)), pl.BlockSpec((tm, d), row)],
        out_specs=[pl.BlockSpec((tm, d), row)], scratch_shapes=[], operands=(*pieces, dgd, dgd, w_int, dt1))
    return ds0, got


def _grad_w_in(pieces, dgd, s0b, tp, fuse=NO_FUSE):
    r, d = s0b.shape
    tk = _reduce_tile(tp, False)
    per = tp // tk

    def body(*refs):
        s_ref, o_ref = refs[8:]

        @pl.when(pl.program_id(0) == 0)
        def _():
            o_ref[...] = jnp.zeros_like(o_ref)

        s = s_ref[...]
        for off, val in _du_pieces(refs[:8]):
            o_ref[off:off + val.shape[1], :] += _tn(val, s)

    row = lambda k: (k, 0)
    (out,), got = _fused_call(
        body, fuse, name="grad_w_in", out_shape=[jax.ShapeDtypeStruct((D_IN_PAD, d), F32)], grid=(r // tk,),
        in_specs=_du_specs(tk, per, row) + [pl.BlockSpec((tk, d), row)],
        out_specs=[pl.BlockSpec((D_IN_PAD, d), lambda k: (0, 0))], scratch_shapes=[],
        operands=(*pieces, dgd, dgd, s0b))
    return out, got


def _ln_in_bwd(ds0, x, meta, g):
    bsz, s, d = x.shape
    tp = s + HEAD
    nh = 2
    sh = s // nh
    rc = min(256, sh)

    def body(ds_ref, x_ref, meta_ref, g_ref, gx_ref, dm_ref, vec_ref):
        h = pl.program_id(1)
        gg = g_ref[...]

        @pl.when(h == 0)
        def _():
            mh, mr = _ln_stats(meta_ref[...])
            dsm = ds_ref[0, PAD:HEAD, :]
            dm_ref[0] = _ln_bwd(dsm * gg, mh, mr)
            vec_ref[0] = jnp.zeros((8, d), F32)
            vec_ref[0, 0:1, :] = jnp.sum(dsm * mh, axis=0, keepdims=True)
            vec_ref[0, 1:2, :] = jnp.sum(dsm, axis=0, keepdims=True)

        def step(i, carry):
            sg, sb = carry
            dst = pl.ds(pl.multiple_of(i * rc, rc), rc)
            src = pl.ds(pl.multiple_of(HEAD + h * sh + i * rc, 64), rc)
            xh, rstd = _ln_stats(x_ref[0, dst, :])
            dsv = ds_ref[0, src, :]
            gx_ref[0, dst, :] = _ln_bwd(dsv * gg, xh, rstd)
            return sg + jnp.sum(dsv * xh, axis=0, keepdims=True), sb + jnp.sum(dsv, axis=0, keepdims=True)

        zero = jnp.zeros((1, d), F32)
        sg, sb = lax.fori_loop(0, sh // rc, step, (zero, zero))
        vec_ref[0, 0:1, :] += sg
        vec_ref[0, 1:2, :] += sb

    return pl.pallas_call(
        body, name="ln_in_bwd",
        out_shape=[jax.ShapeDtypeStruct((bsz, s, d), F32), jax.ShapeDtypeStruct((bsz, N_META, d), F32),
                   jax.ShapeDtypeStruct((bsz, 8, d), F32)],
        grid=(bsz, nh),
        in_specs=[pl.BlockSpec((1, tp, d), lambda bi, hi: (bi, 0, 0)),
                  pl.BlockSpec((1, sh, d), lambda bi, hi: (bi, hi, 0)),
                  pl.BlockSpec((N_META, d), lambda bi, hi: (0, 0)),
                  pl.BlockSpec((1, d), lambda bi, hi: (0, 0))],
        out_specs=[pl.BlockSpec((1, sh, d), lambda bi, hi: (bi, hi, 0)),
                   pl.BlockSpec((1, N_META, d), lambda bi, hi: (bi, 0, 0)),
                   pl.BlockSpec((1, 8, d), lambda bi, hi: (bi, 0, 0))],
        compiler_params=_params(("parallel", "arbitrary")),
    )(ds0, x, meta, g)


def _rows128(a):
    return a.reshape(-1, 128)


def kernel(x, meta_tokens, ln_in_g, ln_in_b, w_in, conv_w, conv_b, conv_ln_g, conv_ln_b, gate_up, gate_bias, gla_norm_g, w_out, ln1_g, ln1_b, w_ff1, w_ff2, ln2_g, ln2_b, loss_target, m_meta_tokens, m_ln_in_g, m_ln_in_b, m_w_in, m_conv_w, m_conv_b, m_conv_ln_g, m_conv_ln_b, m_gate_up, m_gate_bias, m_gla_norm_g, m_w_out, m_ln1_g, m_ln1_b, m_w_ff1, m_w_ff2, m_ln2_g, m_ln2_b, v_meta_tokens, v_ln_in_g, v_ln_in_b, v_w_in, v_conv_w, v_conv_b, v_conv_ln_g, v_conv_ln_b, v_gate_up, v_gate_bias, v_gla_norm_g, v_w_out, v_ln1_g, v_ln1_b, v_w_ff1, v_w_ff2, v_ln2_g, v_ln2_b):
    bsz, seq, d = x.shape
    tp = seq + HEAD
    r = bsz * tp
    xi, yi, ci = _mesh_pos()
    chip = 2 * xi + yi
    c_arr = jnp.reshape(ci, (1,)).astype(jnp.int32)
    pos_arr = jnp.stack([chip, ci]).astype(jnp.int32)

    sh_in = D_IN // 4
    shard_in = jnp.pad(w_in[0].T.astype(BF16), ((0, D_IN_PAD // 4 - sh_in), (0, 0)))
    shard_w1, shard_wout, shard_w2 = w_ff1[0].astype(BF16), w_out[0].astype(BF16), w_ff2[0].astype(BF16)
    small_w = jnp.concatenate([_rows128(meta_tokens), _rows128(conv_w[0]), _rows128(gate_up[0])], axis=0)
    g_small = _gather_small(small_w)
    n_meta_rows, n_cw_rows = N_META * 256 // 128, CONV_WIDTH * 128 // 128
    meta_full = jnp.concatenate([g_small[j, :n_meta_rows].reshape(N_META, 256) for j in range(4)], axis=1)
    convw_full = jnp.concatenate(
        [g_small[j, n_meta_rows:n_meta_rows + n_cw_rows].reshape(CONV_WIDTH, 128) for j in range(4)], axis=1)
    gup_full = jnp.concatenate(
        [g_small[j, n_meta_rows + n_cw_rows:].reshape(GLA_RANK, 64) for j in range(4)], axis=1)
    convw_p = jnp.pad(convw_full, ((0, 1), (0, 0)))
    gup_p = jnp.pad(gup_full, ((0, 128 - GLA_RANK), (0, 0))).astype(BF16)
    ln_in_g2, ln_in_b2 = ln_in_g.reshape(1, d), ln_in_b.reshape(1, d)

    s0, s0b, ((g_int,),) = _ln_in_fwd(x, meta_full, ln_in_g2, ln_in_b2,
                                      [("gather", [(shard_in, 0, D_IN_PAD // 4, None)])])
    (g_int,) = _place_own([g_int], [shard_in])
    w_int = jnp.pad(g_int[:, :sh_in].reshape(D_IN, d), ((0, D_IN_PAD - D_IN), (0, 0)))
    s0f, s0bf = s0.reshape(r, d), s0b.reshape(r, d)
    u, ((w1_buf,),) = _in_proj(s0bf, w_int, [("gather", [(shard_w1, 0, 640, None)])])
    (w1_buf,) = _place_own([w1_buf], [shard_w1])
    u3 = u.reshape(bsz, tp, D_IN_PAD)
    hc, ((w1_buf,),) = _conv_fwd(u3, convw_p, conv_b, [("gather", [(shard_w1, 640, 384, w1_buf)])])
    hc = hc.reshape(r, D_CONV)
    gla_out, o_pre, states, ((g_wout,),) = _gla_fwd(
        u3, gup_p, gate_bias, gla_norm_g, [("gather", [(shard_wout, 0, 256, None)])])
    (g_wout,) = _place_own([g_wout], [shard_wout])
    wout = g_wout.reshape(d, d)
    gla_of = gla_out.reshape(r, 512)
    conv_of, xhat1, rstd1, s1b, ((w2_buf,),) = _out_proj_ln1(
        hc, gla_of, wout, s0f, conv_ln_g, conv_ln_b, ln1_g, ln1_b, [("gather", [(shard_w2, 0, 448, None)])])
    (w2_buf,) = _place_own([w2_buf], [shard_w2])
    w1 = w1_buf
    ra, h2, ((w2_buf,),) = _ffn1(s1b, w1, [("gather", [(shard_w2, 448, 576, w2_buf)])])
    w2 = w2_buf.reshape(D_FF, d)
    dt2, dt2b, acc2 = _ffn2_ln2_loss(h2, w2, xhat1, ln1_g, ln1_b, ln2_g, ln2_b, loss_target, tp)

    def add_pair(g, got, splits=None):
        return _add_pair(g, got, c_arr, splits)

    da = _ffn_bwd_da(dt2b, w2, ra)
    dt1, dt1b, acc1 = _ffn_bwd_ln1(da, w1, dt2, xhat1, rstd1, ln1_g)
    g_w2, _ = _matmul_tn(h2, dt2b, 1024, name="grad_w_ff2")
    big_w2 = g_w2.reshape(4, D_FF // 4, d)
    big_w1, ((pair_w2,),) = _grad_w_ff1(s1b, da, [("pair", [big_w2])])
    dhc, dgla, cacc, ((pair_w1,),) = _out_proj_bwd(dt1b, wout, hc, conv_ln_g, conv_ln_b, [("pair", [big_w1])])
    (part_w1,), (part_w2a, part_w2b) = add_pair(big_w1, pair_w1), add_pair(big_w2, pair_w2, [512, 512])
    g_wout, _ = _grad_w_out(conv_of, gla_of, dt1b)
    big_wout = g_wout.reshape(4, d // 4, d)
    dcv, dcg, dcw, ((chip_w1,), (pair_wout,)) = _conv_bwd(
        dhc.reshape(bsz, tp, D_CONV), u3, convw_p, [("exchange", [part_w1]), ("pair", [big_wout])])
    (part_wout,) = add_pair(big_wout, pair_wout)
    (dq, dk, dv, dr, dgd, dgup, gvec), ((chip_w2a,),) = _gla_bwd(
        dgla.reshape(bsz, tp, 512), u3, o_pre, states, gup_p, gate_bias, gla_norm_g, [("exchange", [part_w2a])])
    pieces = [a.reshape(r, a.shape[-1]) for a in (dcv, dcg, dq, dk, dv, dr)]
    g_wint, ((chip_w2b, chip_wout),) = _grad_w_in(
        pieces, dgd, s0bf, tp, [("exchange", [part_w2b, part_wout])])
    big_win = jnp.stack([g_wint[j * sh_in:(j + 1) * sh_in] for j in range(4)])
    (half_w1, half_w2), ((pair_win,),) = _add_chips_many(
        [(big_w1, pair_w1, [chip_w1]), (big_w2, pair_w2, [chip_w2a, chip_w2b])], [("pair", [big_win])])
    ds0, ((chip_win,), (sib_w1, sib_w2)) = _in_proj_bwd(
        pieces, dgd, w_int, dt1, tp, [("exchange", add_pair(big_win, pair_win)), ("swap", [half_w1, half_w2])])
    grad_x, dmeta, lvec = _ln_in_bwd(ds0.reshape(bsz, tp, d), x, meta_full, ln_in_g2)

    red, loss = _sum_small(_allgather_small(_pack_small(acc2, acc1, lvec, cacc, gvec, dcw, dgup, dmeta)))

    half_win = _add_chips(big_win, pair_win, [chip_win], pos_arr)
    half_wout = _add_chips(big_wout, pair_wout, [chip_wout], pos_arr)
    sib_win, sib_wout = _pair_swap([half_win, half_wout])
    halves = {"w_in": (half_win, sib_win), "w_out": (half_wout, sib_wout), "w_ff1": (half_w1, sib_w1),
              "w_ff2": (half_w2, sib_w2)}

    grads = {}
    weights = dict(meta_tokens=meta_tokens, ln_in_g=ln_in_g, ln_in_b=ln_in_b, w_in=w_in, conv_w=conv_w, conv_b=conv_b,
                   conv_ln_g=conv_ln_g, conv_ln_b=conv_ln_b, gate_up=gate_up, gate_bias=gate_bias,
                   gla_norm_g=gla_norm_g, w_out=w_out, ln1_g=ln1_g, ln1_b=ln1_b, w_ff1=w_ff1, w_ff2=w_ff2,
                   ln2_g=ln2_g, ln2_b=ln2_b)
    moms = dict(meta_tokens=(m_meta_tokens, v_meta_tokens), ln_in_g=(m_ln_in_g, v_ln_in_g),
                ln_in_b=(m_ln_in_b, v_ln_in_b), w_in=(m_w_in, v_w_in), conv_w=(m_conv_w, v_conv_w),
                conv_b=(m_conv_b, v_conv_b), conv_ln_g=(m_conv_ln_g, v_conv_ln_g),
                conv_ln_b=(m_conv_ln_b, v_conv_ln_b), gate_up=(m_gate_up, v_gate_up),
                gate_bias=(m_gate_bias, v_gate_bias), gla_norm_g=(m_gla_norm_g, v_gla_norm_g),
                w_out=(m_w_out, v_w_out), ln1_g=(m_ln1_g, v_ln1_g), ln1_b=(m_ln1_b, v_ln1_b),
                w_ff1=(m_w_ff1, v_w_ff1), w_ff2=(m_w_ff2, v_w_ff2), ln2_g=(m_ln2_g, v_ln2_g),
                ln2_b=(m_ln2_b, v_ln2_b))
    names = list(weights)
    big_names = ("w_in", "w_out", "w_ff1", "w_ff2")
    delta, new_m, new_v = {}, {}, {}
    for k in big_names:
        to2d = (lambda a: a[0].T) if k == "w_in" else (lambda a: a[0])
        back = (lambda a: a.T[None]) if k == "w_in" else (lambda a: a[None])
        res = _adamw_halves(to2d(weights[k]), *halves[k], to2d(moms[k][0]), to2d(moms[k][1]), c_arr)
        grads[k], delta[k], new_m[k], new_v[k] = [back(a) for a in res]
    small_names = [k for k in names if k not in big_names]
    two = lambda a: a.reshape(-1, a.shape[-1])

    def my_cols(full, rows, width):
        return lax.dynamic_slice(full, (0, chip * width), (rows, width))

    sharded = {
        "meta_tokens": my_cols(red[SMALL_AT["meta_tokens"][0]:SMALL_AT["meta_tokens"][0] + N_META], N_META, 256),
        "conv_w": my_cols(red[SMALL_CONV_W:SMALL_CONV_W + 16].reshape(32, D_CONV), CONV_WIDTH, 128),
        "gate_up": my_cols(red[SMALL_GATE_UP:SMALL_GATE_UP + 4].reshape(GLA_RANK, 256), GLA_RANK, 64)}
    upd = _adamw_small(red, sharded, {k: (two(weights[k]), two(moms[k][0]), two(moms[k][1])) for k in small_names})
    for k in small_names:
        shp = weights[k].shape
        grads[k], delta[k], new_m[k], new_v[k] = [a.reshape(shp) for a in upd[k]]
    loss = loss.reshape(())

    return (loss, grad_x, *[grads[k] for k in names], *[delta[k] for k in names],
            *[new_m[k] for k in names], *[new_v[k] for k in names])
```

```python
import functools

import jax
import jax.numpy as jnp
from jax import lax
from jax.experimental import pallas as pl
from jax.experimental.pallas import tpu as pltpu

F32 = jnp.float32
BF16 = jnp.bfloat16

D_MODEL = 1024
N_META = 16
D_CONV = 512
CONV_WIDTH = 31
GLA_HEADS = 4
GLA_DV = 128
GLA_DK = 64
GLA_RANK = 16
GLA_TAU = 16.0
CHUNK = 64
D_FF = 4096
LN_EPS = 1e-5
ALPHA = 2.0 ** 0.25
D_IN = 2576
D_IN_PAD = 2688
PAD = CHUNK - N_META
HEAD = PAD + N_META
Q_SCALE = GLA_DK ** -0.5
ADAM_LR, ADAM_B1, ADAM_B2, ADAM_EPS, ADAM_WD, ADAM_STEP = 0.001, 0.9, 0.999, 1e-08, 0.01, 10
HALF = D_MODEL // 2
VMEM_LIMIT = 56 * 1024 * 1024
MESH = pl.DeviceIdType.MESH

C_VAL, C_GATE, C_Q, C_K, C_V, C_R, C_GD = 0, 512, 1024, 1280, 1536, 2048, 2560

SMALL_AT = {"loss": (0, 1, 0, 1024), "ln_in_g": (1, 1, 0, 1024), "ln_in_b": (2, 1, 0, 1024), "ln1_g": (3, 1, 0, 1024),
            "ln1_b": (4, 1, 0, 1024), "ln2_g": (5, 1, 0, 1024), "ln2_b": (6, 1, 0, 1024), "conv_b": (7, 1, 0, 512),
            "conv_ln_g": (7, 1, 512, 512), "conv_ln_b": (8, 1, 0, 512), "gate_bias": (8, 1, 512, 256),
            "gla_norm_g": (8, 1, 768, 128), "meta_tokens": (40, 16, 0, 1024)}
SMALL_CONV_W, SMALL_GATE_UP = 16, 32
SMALL_ROWS = 56


def _params(sem=None, **kw):
    return pltpu.CompilerParams(dimension_semantics=sem, vmem_limit_bytes=VMEM_LIMIT, **kw)


def _row_tile(tp):
    for t in (704, 352, 192, 64):
        if tp % t == 0:
            return t
    raise ValueError(tp)


def _reduce_tile(tp, big):
    for t in ((2112, 1056, 704) if big else (1056, 704)) + (352, 192, 64):
        if tp % t == 0:
            return t
    raise ValueError(tp)


def _sub_rows(tm):
    return [slice(0, tm)]


def _dot(a, b, dims, precision=None):
    return lax.dot_general(a, b, (dims, ((), ())), preferred_element_type=F32, precision=precision)


def _nn(a, b, **kw):
    return _dot(a, b, ((1,), (0,)), **kw)


def _nt(a, b, **kw):
    return _dot(a, b, ((1,), (1,)), **kw)


def _tn(a, b, **kw):
    return _dot(a, b, ((0,), (0,)), **kw)


def _sigmoid(x):
    return 1.0 / (1.0 + jnp.exp(-x))


def _log_sigmoid(z):
    return jnp.minimum(z, 0.0) - jnp.log(1.0 + jnp.exp(-jnp.abs(z)))


def _ln_stats(t):
    mu = jnp.mean(t, axis=-1, keepdims=True)
    d = t - mu
    var = jnp.mean(d * d, axis=-1, keepdims=True)
    rstd = lax.rsqrt(var + LN_EPS)
    return d * rstd, rstd


def _ln_bwd(dxhat, xhat, rstd):
    m1 = jnp.mean(dxhat, axis=-1, keepdims=True)
    m2 = jnp.mean(dxhat * xhat, axis=-1, keepdims=True)
    return rstd * (dxhat - m1 - xhat * m2)


def _mesh_pos():
    return lax.axis_index("x"), lax.axis_index("y"), lax.axis_index("c")


ANY = pl.BlockSpec(memory_space=pl.ANY)


def _gather_sems(n):
    return [pltpu.SemaphoreType.DMA((n, 3))] * 4


def _gather_steps(ins, outs, sems, ranges=None):
    n = len(ins)
    send, recv, fsend, frecv = sems
    ranges = ranges or [(0, ref.shape[0]) for ref in ins]
    x, y, c = _mesh_pos()
    me = 2 * x + y
    sibling = (x, y, 1 - c)
    chips = [(1 - x, y), (x, 1 - y), (1 - x, 1 - y)]
    chip_idx = [2 * px + py for px, py in chips]
    mine = [pl.ds(pl.multiple_of(r0 + c * (nr // 2), 16), nr // 2) for r0, nr in ranges]
    other = [pl.ds(pl.multiple_of(r0 + (1 - c) * (nr // 2), 16), nr // 2) for r0, nr in ranges]
    pairs = [(a, k) for a in range(n) for k in range(3)]

    def ici(a, k, slab):
        return pltpu.make_async_remote_copy(
            src_ref=ins[a].at[mine[a], :], dst_ref=outs[a].at[slab, mine[a], :],
            send_sem=send.at[a, k], recv_sem=recv.at[a, k], device_id=(*chips[k], c), device_id_type=MESH)

    def forward(a, k, rows):
        blk = outs[a].at[chip_idx[k], rows[a], :]
        return pltpu.make_async_remote_copy(
            src_ref=blk, dst_ref=blk, send_sem=fsend.at[a, k], recv_sem=frecv.at[a, k],
            device_id=sibling, device_id_type=MESH)

    def start():
        for a, k in pairs:
            ici(a, k, me).start()

    def relay():
        for a, k in pairs:
            ici(a, k, chip_idx[k]).wait_recv()
            forward(a, k, mine).start()

    def finish():
        for a, k in pairs:
            forward(a, k, other).wait_recv()
        for a, k in pairs:
            ici(a, k, me).wait_send()
            forward(a, k, mine).wait_send()

    return start, relay, finish


def _place_own(gathered, shards):
    chip = 2 * lax.axis_index("x") + lax.axis_index("y")
    return [lax.dynamic_update_slice(g, s[None], (chip, 0, 0)) for g, s in zip(gathered, shards)]


def _gather_small_steps(ins, outs, sems):
    send, recv = sems
    x, y, c = _mesh_pos()
    chips = [(1 - x, y), (x, 1 - y), (1 - x, 1 - y)]

    def copy(a, k, slot):
        return pltpu.make_async_remote_copy(
            src_ref=ins[a], dst_ref=outs[a].at[slot], send_sem=send.at[a, k], recv_sem=recv.at[a, k],
            device_id=(*chips[k], c), device_id_type=MESH)

    def start():
        for a in range(len(ins)):
            for k in range(3):
                copy(a, k, 2 * x + y).start()

    def finish():
        for a in range(len(ins)):
            for k, (px, py) in enumerate(chips):
                copy(a, k, 2 * px + py).wait_recv()
            for k in range(3):
                copy(a, k, 2 * x + y).wait_send()

    return start, finish


def _pair_swap_steps(ins, outs, sems):
    send, recv = sems
    x, y, c = _mesh_pos()
    cps = [pltpu.make_async_remote_copy(
        src_ref=ins[a], dst_ref=outs[a], send_sem=send.at[a], recv_sem=recv.at[a],
        device_id=(x, y, 1 - c), device_id_type=MESH) for a in range(len(ins))]

    def start():
        for cp in cps:
            cp.start()

    def finish():
        for cp in cps:
            cp.wait()

    return start, finish


def _pair_exchange_steps(ins, outs, sems):
    send, recv = sems
    x, y, c = _mesh_pos()
    other = pl.ds(pl.multiple_of((1 - c) * HALF, 128), HALF)
    cps = [pltpu.make_async_remote_copy(
        src_ref=ins[a].at[:, :, other], dst_ref=outs[a], send_sem=send.at[a], recv_sem=recv.at[a],
        device_id=(x, y, 1 - c), device_id_type=MESH) for a in range(len(ins))]

    def start():
        for cp in cps:
            cp.start()

    def finish():
        for cp in cps:
            cp.wait()

    return start, finish


def _pair_exchange_shapes(grads):
    return [jax.ShapeDtypeStruct(g.shape[:2] + (HALF,), g.dtype) for g in grads]


def _pair_exchange(grads):
    n = len(grads)

    def body(*refs):
        start, finish = _pair_exchange_steps(refs[:n], refs[n:2 * n], refs[2 * n:])
        start()
        finish()

    return pl.pallas_call(
        body, name="grad_pair_exchange", out_shape=_pair_exchange_shapes(grads),
        in_specs=[ANY] * n, out_specs=[ANY] * n,
        scratch_shapes=[pltpu.SemaphoreType.DMA((n,)), pltpu.SemaphoreType.DMA((n,))],
    )(*grads)


NO_FUSE = ()


def _fuse_plan(kind, items):
    n = len(items)
    if kind == "gather":
        shards = [it[0] for it in items]
        bufs = [it[3] for it in items if it[3] is not None]
        alias, b = {}, 0
        for a, it in enumerate(items):
            if it[3] is not None:
                alias[n + b] = a
                b += 1
        ranges = [(it[1], it[2]) for it in items]
        return (shards + bufs, [jax.ShapeDtypeStruct((4,) + s.shape, s.dtype) for s in shards], alias, _gather_sems(n),
                lambda i, o, s: _gather_steps(i[:n], o, s, ranges))
    if kind == "exchange":
        return list(items), _chip_exchange_shapes(items), {}, _chip_exchange_sems(n), _chip_exchange_steps
    if kind == "small":
        return (list(items), [jax.ShapeDtypeStruct((4,) + s.shape, s.dtype) for s in items], {},
                _chip_exchange_sems(n), _gather_small_steps)
    pair_sems = [pltpu.SemaphoreType.DMA((n,)), pltpu.SemaphoreType.DMA((n,))]
    if kind == "swap":
        return (list(items), [jax.ShapeDtypeStruct(h.shape, h.dtype) for h in items], {}, pair_sems, _pair_swap_steps)
    return list(items), _pair_exchange_shapes(items), {}, pair_sems, _pair_exchange_steps


def _fused_call(body, fuse, *, name, grid, in_specs, out_specs, out_shape, scratch_shapes, operands,
                late_relay=False):
    plans = [_fuse_plan(kind, list(items)) for kind, items in fuse if len(items)]
    n_in, n_out, n_s = len(in_specs), len(out_shape), len(scratch_shapes)
    comm = [a for p in plans for a in p[0]]
    shapes = [s for p in plans for s in p[1]]
    nc, no = len(comm), len(shapes)
    aliases, i_at, o_at = {}, n_in, n_out
    for p in plans:
        for i, o in p[2].items():
            aliases[i_at + i] = o_at + o
        i_at, o_at = i_at + len(p[0]), o_at + len(p[1])

    def wrapped(*refs):
        o0 = n_in + nc
        s0 = o0 + n_out + no
        i_at, o_at, sem_at, steps = n_in, o0 + n_out, s0 + n_s, []
        for p in plans:
            steps.append(p[4](refs[i_at:i_at + len(p[0])], refs[o_at:o_at + len(p[1])], refs[sem_at:sem_at + len(p[3])]))
            i_at, o_at, sem_at = i_at + len(p[0]), o_at + len(p[1]), sem_at + len(p[3])
        first, last = _grid_ends(grid)
        for st in steps:
            pl.when(first)(st[0])
        if not late_relay:
            for st in steps:
                for mid in st[1:-1]:
                    pl.when(last)(mid)
        body(*refs[:n_in], *refs[o0:o0 + n_out], *refs[s0:s0 + n_s])
        for st in steps:
            for step in (st[1:] if late_relay else st[-1:]):
                pl.when(last)(step)

    res = pl.pallas_call(
        wrapped if plans else body, name=name, grid=grid, in_specs=list(in_specs) + [ANY] * nc,
        out_specs=list(out_specs) + [ANY] * no, out_shape=list(out_shape) + shapes,
        scratch_shapes=list(scratch_shapes) + [s for p in plans for s in p[3]], input_output_aliases=aliases,
        compiler_params=_params(("arbitrary",) * len(grid)))(*operands, *comm)
    outs, got, results = list(res[:n_out]), list(res[n_out:]), []
    for p in plans:
        results.append(got[:len(p[1])])
        got = got[len(p[1]):]
    return outs, results


def _chip_exchange_sems(n):
    return [pltpu.SemaphoreType.DMA((n, 3))] * 2


def _chip_exchange_shapes(parts):
    return [jax.ShapeDtypeStruct((3,) + p.shape[1:], p.dtype) for p in parts]


def _chip_exchange_steps(ins, outs, sems):
    send, recv = sems
    x, y, c = _mesh_pos()
    chips = [(1 - x, y), (x, 1 - y), (1 - x, 1 - y)]
    cps = [pltpu.make_async_remote_copy(
        src_ref=ins[a].at[2 * px + py], dst_ref=outs[a].at[k], send_sem=send.at[a, k], recv_sem=recv.at[a, k],
        device_id=(px, py, c), device_id_type=MESH) for a in range(len(ins)) for k, (px, py) in enumerate(chips)]

    def start():
        for cp in cps:
            cp.start()

    def finish():
        for cp in cps:
            cp.wait()

    return start, finish


def _chip_exchange(parts):
    n = len(parts)

    def body(*refs):
        start, finish = _chip_exchange_steps(refs[:n], refs[n:2 * n], refs[2 * n:])
        start()
        finish()

    return pl.pallas_call(
        body, name="grad_chip_exchange", out_shape=_chip_exchange_shapes(parts),
        in_specs=[ANY] * n, out_specs=[ANY] * n, scratch_shapes=_chip_exchange_sems(n),
    )(*parts)


def _pair_swap(halves):
    n = len(halves)

    def body(*refs):
        start, finish = _pair_swap_steps(refs[:n], refs[n:2 * n], refs[2 * n:])
        start()
        finish()

    return pl.pallas_call(
        body, name="grad_pair_swap",
        out_shape=[jax.ShapeDtypeStruct(h.shape, h.dtype) for h in halves],
        in_specs=[ANY] * n, out_specs=[ANY] * n,
        scratch_shapes=[pltpu.SemaphoreType.DMA((n,)), pltpu.SemaphoreType.DMA((n,))],
    )(*halves)


def _allgather_small(pack):
    m_per, ncol = pack.shape

    def body(x_ref, out_ref, send_sems, recv_sems, local_sem):
        x, y, c = _mesh_pos()
        me, sibling = (x, y, c), (x, y, 1 - c)
        chips = [(1 - x, y), (x, 1 - y), (1 - x, 1 - y)]

        def rows(px, py, pc):
            return out_ref.at[pl.ds(pl.multiple_of((4 * px + 2 * py + pc) * m_per, 8), m_per), :]

        def copy(k, block, to, src=None):
            return pltpu.make_async_remote_copy(
                src_ref=rows(*block) if src is None else src, dst_ref=rows(*block),
                send_sem=send_sems.at[k], recv_sem=recv_sems.at[k], device_id=to, device_id_type=MESH)

        mine = pltpu.make_async_copy(x_ref, rows(*me), local_sem)
        mine.start()
        first = [copy(0, me, sibling, src=x_ref)]
        first += [copy(1 + j, me, (*chip, c), src=x_ref) for j, chip in enumerate(chips)]
        for cp in first:
            cp.start()
        passed = [copy(4 + j, (*chip, c), sibling) for j, chip in enumerate(chips)]
        for j, chip in enumerate(chips):
            copy(1 + j, (*chip, c), me).wait_recv()
            passed[j].start()
        copy(0, sibling, me).wait_recv()
        for j, chip in enumerate(chips):
            copy(4 + j, (*chip, 1 - c), me).wait_recv()
        for cp in first + passed:
            cp.wait_send()
        mine.wait()

    return pl.pallas_call(
        body, name="allgather_small",
        out_shape=jax.ShapeDtypeStruct((8 * m_per, ncol), pack.dtype),
        in_specs=[pl.BlockSpec(memory_space=pltpu.VMEM)],
        out_specs=pl.BlockSpec(memory_space=pltpu.VMEM),
        scratch_shapes=[pltpu.SemaphoreType.DMA((7,)), pltpu.SemaphoreType.DMA((7,)), pltpu.SemaphoreType.DMA],
    )(pack)


def _add_pair(g, got, c_arr, splits=None):
    _, rows, _ = g.shape
    splits = splits or [rows]

    def body(c_ref, g_ref, r_ref, *o_refs):
        at = 0
        for o_ref, n in zip(o_refs, splits):
            o_ref[...] = (g_ref[:, at:at + n, :] + r_ref[:, at:at + n, :]).astype(BF16)
            at += n

    return pl.pallas_call(
        body, name="grad_add_pair", out_shape=[jax.ShapeDtypeStruct((4, n, HALF), BF16) for n in splits],
        grid_spec=pltpu.PrefetchScalarGridSpec(
            num_scalar_prefetch=1, grid=(4,),
            in_specs=[pl.BlockSpec((1, rows, HALF), lambda j, c: (j, 0, c[0])),
                      pl.BlockSpec((1, rows, HALF), lambda j, c: (j, 0, 0))],
            out_specs=[pl.BlockSpec((1, n, HALF), lambda j, c: (j, 0, 0)) for n in splits]),
        compiler_params=_params(("arbitrary",)),
    )(c_arr, g, got)


def _add_chips(g, pair_got, chip_gots, pos_arr):
    _, rows, _ = g.shape

    def body(pos_ref, g_ref, p_ref, *refs):
        o_ref, at = refs[-1], 0
        for r_ref in refs[:-1]:
            n = r_ref.shape[1]
            own = g_ref[0, at:at + n, :] + p_ref[0, at:at + n, :]
            o_ref[at:at + n, :] = ((own + r_ref[0].astype(F32)) + r_ref[1].astype(F32)) + r_ref[2].astype(F32)
            at += n

    return pl.pallas_call(
        body, name="grad_add_chips", out_shape=jax.ShapeDtypeStruct((rows, HALF), F32),
        grid_spec=pltpu.PrefetchScalarGridSpec(
            num_scalar_prefetch=1, grid=(1,),
            in_specs=[pl.BlockSpec((1, rows, HALF), lambda i, p: (p[0], 0, p[1])),
                      pl.BlockSpec((1, rows, HALF), lambda i, p: (p[0], 0, 0))]
            + [pl.BlockSpec(t.shape, lambda i, p: (0, 0, 0)) for t in chip_gots],
            out_specs=pl.BlockSpec((rows, HALF), lambda i, p: (0, 0))),
        compiler_params=_params(("arbitrary",)),
    )(pos_arr, g, pair_got, *chip_gots)


def _add_chips_many(items, fuse=NO_FUSE):
    n = len(items)
    rows = [it[0].shape[1] for it in items]
    n_got = [len(it[2]) for it in items]

    def body(*refs):
        g_refs, p_refs = refs[:n], refs[n:2 * n]
        got_refs = refs[2 * n:2 * n + sum(n_got)]
        o_refs = refs[2 * n + sum(n_got):3 * n + sum(n_got)]
        scr = refs[3 * n + sum(n_got):]
        x, y, c = _mesh_pos()
        chip = 2 * x + y
        mine = pl.ds(pl.multiple_of(c * HALF, 128), HALF)
        copies = []
        for a in range(n):
            copies.append((pltpu.make_async_copy(g_refs[a].at[chip, :, mine], scr[2 * a], scr[2 * n].at[2 * a]),
                           pltpu.make_async_copy(p_refs[a].at[chip], scr[2 * a + 1], scr[2 * n].at[2 * a + 1])))
        for cg, cp in copies:
            cg.start()
            cp.start()
        at_ref = 0
        for a in range(n):
            copies[a][0].wait()
            copies[a][1].wait()
            at = 0
            for r_ref in got_refs[at_ref:at_ref + n_got[a]]:
                k = r_ref.shape[1]
                own = scr[2 * a][at:at + k, :] + scr[2 * a + 1][at:at + k, :]
                o_refs[a][at:at + k, :] = ((own + r_ref[0].astype(F32)) + r_ref[1].astype(F32)) + r_ref[2].astype(F32)
                at += k
            at_ref += n_got[a]

    gots = [t for it in items for t in it[2]]
    outs, got = _fused_call(
        body, fuse, name="grad_add_chips_many", grid=(1,),
        in_specs=[ANY] * (2 * n) + [pl.BlockSpec(t.shape, lambda i: (0, 0, 0)) for t in gots],
        out_specs=[pl.BlockSpec((r, HALF), lambda i: (0, 0)) for r in rows],
        out_shape=[jax.ShapeDtypeStruct((r, HALF), F32) for r in rows],
        scratch_shapes=[pltpu.VMEM((r, HALF), F32) for r in rows for _ in range(2)] + [pltpu.SemaphoreType.DMA((2 * n,))],
        operands=(*[it[0] for it in items], *[it[1] for it in items], *gots))
    return outs, got


def _pack_small(acc2, acc1, lvec, cacc, gvec, dcw, dgup, dmeta):
    bsz = lvec.shape[0]

    def body(a2_ref, a1_ref, lv_ref, ca_ref, gv_ref, cw_ref, gu_ref, dm_ref, o_ref):
        def put(name, val):
            r0, nr, l0, nl = SMALL_AT[name]
            o_ref[r0:r0 + nr, l0:l0 + nl] = val

        def put_folded(row0, val, per_row):
            w = val.shape[1]
            for i in range(val.shape[0]):
                o_ref[row0 + i // per_row:row0 + i // per_row + 1, (i % per_row) * w:(i % per_row + 1) * w] = val[i:i + 1]

        over_b = lambda f: functools.reduce(lambda a, b: a + b, [f(b) for b in range(bsz)])
        o_ref[...] = jnp.zeros_like(o_ref)
        put("loss", a2_ref[0:1, :])
        put("ln2_g", a2_ref[1:2, :])
        put("ln2_b", a2_ref[2:3, :])
        put("ln1_g", a1_ref[0:1, :])
        put("ln1_b", a1_ref[1:2, :])
        put("ln_in_g", over_b(lambda b: lv_ref[b, 0:1, :]))
        put("ln_in_b", over_b(lambda b: lv_ref[b, 1:2, :]))
        put("conv_b", ca_ref[0:1, :])
        put("conv_ln_g", ca_ref[1:2, :])
        put("conv_ln_b", ca_ref[2:3, :])
        put("gate_bias", over_b(lambda b: gv_ref[b, 0:1, :]))
        put("gla_norm_g", over_b(lambda b: gv_ref[b, 1:2, 0:128] + gv_ref[b, 1:2, 128:256]))
        put_folded(SMALL_CONV_W, over_b(lambda b: cw_ref[b]), D_MODEL // D_CONV)
        put_folded(SMALL_GATE_UP, over_b(lambda b: gu_ref[b, 0:GLA_RANK, :]), D_MODEL // 256)
        put("meta_tokens", over_b(lambda b: dm_ref[b]))

    return pl.pallas_call(
        body, name="pack_small", out_shape=jax.ShapeDtypeStruct((SMALL_ROWS, D_MODEL), F32),
    )(acc2, acc1, lvec, cacc, gvec, dcw, dgup, dmeta)


def _sum_small(gathered):
    def body(g_ref, o_ref, loss_ref):
        acc = g_ref[0:SMALL_ROWS, :]
        for d in range(1, 8):
            acc = acc + g_ref[d * SMALL_ROWS:(d + 1) * SMALL_ROWS, :]
        o_ref[...] = acc
        loss_ref[...] = jnp.sum(acc[0:1, :], axis=1, keepdims=True)

    return pl.pallas_call(
        body, name="sum_small",
        out_shape=[jax.ShapeDtypeStruct((SMALL_ROWS, D_MODEL), F32), jax.ShapeDtypeStruct((1, 1), F32)],
    )(gathered)


def _adamw_math(w, g, m, v):
    c1 = 1.0 - ADAM_B1 ** ADAM_STEP
    c2 = 1.0 - ADAM_B2 ** ADAM_STEP
    mn = ADAM_B1 * m + (1.0 - ADAM_B1) * g
    vn = ADAM_B2 * v + (1.0 - ADAM_B2) * (g * g)
    return -ADAM_LR * ((mn / c1) / (jnp.sqrt(vn / c2) + ADAM_EPS) + ADAM_WD * w), mn, vn


def _adamw_small(red, sharded_grads, params):
    names = list(params)
    sharded = [k for k in names if k in sharded_grads]
    n, ns = len(names), len(sharded)

    def body(*refs):
        red_ref, sg_refs, p_refs, o_refs = refs[0], refs[1:1 + ns], refs[1 + ns:1 + ns + 3 * n], refs[1 + ns + 3 * n:]
        for i, k in enumerate(names):
            w_ref, m_ref, v_ref = p_refs[3 * i:3 * i + 3]
            if k in sharded:
                g = sg_refs[sharded.index(k)][...]
            else:
                r0, nr, l0, nl = SMALL_AT[k]
                g = red_ref[r0:r0 + nr, l0:l0 + nl]
            dl, mn, vn = _adamw_math(w_ref[...], g, m_ref[...], v_ref[...])
            for o_ref, val in zip(o_refs[4 * i:4 * i + 4], (g, dl, mn, vn)):
                o_ref[...] = val

    flat = [a for k in names for a in params[k]]
    res = pl.pallas_call(
        body, name="adamw_small",
        out_shape=[jax.ShapeDtypeStruct(params[k][0].shape, F32) for k in names for _ in range(4)],
    )(red, *[sharded_grads[k] for k in sharded], *flat)
    return {k: tuple(res[4 * i:4 * i + 4]) for i, k in enumerate(names)}


def _adamw_halves(w, own, sib, m, v, c_arr):
    rows, cols = w.shape
    tr = 128 if rows % 128 == 0 else rows

    def body(c_ref, w_ref, own_ref, sib_ref, m_ref, v_ref, g_ref, d_ref, mo_ref, vo_ref):
        first = c_ref[0] == 0
        own, sib = own_ref[...], sib_ref[...]
        g = jnp.concatenate([jnp.where(first, own, sib), jnp.where(first, sib, own)], axis=1)
        g_ref[...] = g
        d_ref[...], mo_ref[...], vo_ref[...] = _adamw_math(w_ref[...], g, m_ref[...], v_ref[...])

    full = pl.BlockSpec((tr, cols), lambda i, c: (i, 0))
    half = pl.BlockSpec((tr, HALF), lambda i, c: (i, 0))
    return pl.pallas_call(
        body, name="adamw_halves", out_shape=[jax.ShapeDtypeStruct(w.shape, F32)] * 4,
        grid_spec=pltpu.PrefetchScalarGridSpec(
            num_scalar_prefetch=1, grid=(rows // tr,), in_specs=[full, half, half, full, full], out_specs=[full] * 4),
        compiler_params=_params(("parallel",)),
    )(c_arr, w, own, sib, m, v)


def _ln_in_fwd(x, g, b, fuse=NO_FUSE):
    bsz, s, d = x.shape
    tp = s + HEAD
    nh = 2
    sh = s // nh
    rc = min(256, sh)

    def body(x_ref, g_ref, b_ref, s0_ref, s0b_ref):
        h = pl.program_id(1)
        gg, bb = g_ref[...], b_ref[...]

        def step(i, carry):
            src = pl.ds(pl.multiple_of(i * rc, rc), rc)
            dst = pl.ds(pl.multiple_of(HEAD + h * sh + i * rc, 64), rc)
            xh, _ = _ln_stats(x_ref[0, src, :])
            val = xh * gg + bb
            s0_ref[0, dst, :] = val
            s0b_ref[0, dst, :] = val.astype(BF16)
            return carry

        lax.fori_loop(0, sh // rc, step, 0)

    full = lambda bi, hi: (bi, 0, 0)
    (s0, s0b), got = _fused_call(
        body, fuse, name="ln_in_fwd",
        out_shape=[jax.ShapeDtypeStruct((bsz, tp, d), F32), jax.ShapeDtypeStruct((bsz, tp, d), BF16)],
        grid=(bsz, nh),
        in_specs=[pl.BlockSpec((1, sh, d), lambda bi, hi: (bi, hi, 0)),
                  pl.BlockSpec((1, d), lambda bi, hi: (0, 0)),
                  pl.BlockSpec((1, d), lambda bi, hi: (0, 0))],
        out_specs=[pl.BlockSpec((1, tp, d), full)] * 2, scratch_shapes=[], operands=(x, g, b),
        late_relay=True)
    return s0, s0b, got


def _ln_meta(s0, s0b, meta, g, b):
    bsz, tp, d = s0.shape

    def body(s0_in, s0b_in, meta_ref, g_ref, b_ref, s0_ref, s0b_ref):
        mh, _ = _ln_stats(meta_ref[...])
        mv = mh * g_ref[...] + b_ref[...]
        s0_ref[0, 0:PAD, :] = jnp.zeros((PAD, d), F32)
        s0b_ref[0, 0:PAD, :] = jnp.zeros((PAD, d), BF16)
        s0_ref[0, PAD:HEAD, :] = mv
        s0b_ref[0, PAD:HEAD, :] = mv.astype(BF16)

    head = pl.BlockSpec((1, HEAD, d), lambda bi: (bi, 0, 0))
    vec = pl.BlockSpec((1, d), lambda bi: (0, 0))
    return pl.pallas_call(
        body, name="ln_meta", out_shape=[jax.ShapeDtypeStruct(s0.shape, F32), jax.ShapeDtypeStruct(s0b.shape, BF16)],
        grid=(bsz,), in_specs=[head, head, pl.BlockSpec((N_META, d), lambda bi: (0, 0)), vec, vec],
        out_specs=[head, head], input_output_aliases={0: 0, 1: 1},
        compiler_params=_params(("parallel",)),
    )(s0, s0b, meta, g, b)


def _in_proj(s0b, w_int, fuse=NO_FUSE):
    r, d = s0b.shape
    tm = _row_tile(r)

    def body(a_ref, w_ref, o_ref):
        o_ref[...] = _nt(a_ref[...], w_ref[...])

    (u,), got = _fused_call(
        body, fuse, name="in_proj", out_shape=[jax.ShapeDtypeStruct((r, D_IN_PAD), F32)],
        grid=(r // tm,),
        in_specs=[pl.BlockSpec((tm, d), lambda i: (i, 0)), pl.BlockSpec((D_IN_PAD, d), lambda i: (0, 0))],
        out_specs=[pl.BlockSpec((tm, D_IN_PAD), lambda i: (i, 0))], scratch_shapes=[], operands=(s0b, w_int))
    return u, got


def _conv_fwd(u, conv_w, conv_b, fuse=NO_FUSE):
    bsz, tp, _ = u.shape
    nchunk = tp // CHUNK
    win = CHUNK + 32
    nct = D_CONV // 128

    def body(cv_ref, cg_ref, w_ref, cb_ref, hc_ref, h_scr, win_scr):
        h_scr[0:32, :] = jnp.zeros((32, 128), F32)
        h_scr[32:32 + tp, :] = cv_ref[0] * _sigmoid(cg_ref[0])
        cb = cb_ref[...]

        def step(n, carry):
            r0 = pl.multiple_of(n * CHUNK, CHUNK)
            win_scr[...] = h_scr[pl.ds(r0, win), :]
            acc = jnp.zeros((CHUNK, 128), F32)
            for j in range(CONV_WIDTH):
                acc = acc + w_ref[j:j + 1, :] * win_scr[2 + j:2 + j + CHUNK, :]
            hc_ref[0, pl.ds(r0, CHUNK), :] = acc + cb
            return carry

        lax.fori_loop(0, nchunk, step, 0)

    (hc,), got = _fused_call(
        body, fuse, name="conv_fwd", out_shape=[jax.ShapeDtypeStruct((bsz, tp, D_CONV), F32)],
        grid=(bsz, nct),
        in_specs=[pl.BlockSpec((1, tp, 128), lambda bi, t: (bi, 0, C_VAL // 128 + t)),
                  pl.BlockSpec((1, tp, 128), lambda bi, t: (bi, 0, C_GATE // 128 + t)),
                  pl.BlockSpec((32, 128), lambda bi, t: (0, t)),
                  pl.BlockSpec((1, 128), lambda bi, t: (0, t))],
        out_specs=[pl.BlockSpec((1, tp, 128), lambda bi, t: (bi, 0, t))],
        scratch_shapes=[pltpu.VMEM((tp + 32, 128), F32), pltpu.VMEM((win, 128), F32)],
        operands=(u, u, conv_w, conv_b))
    return hc, got


def _gla_group(nchunk):
    return 11 if nchunk % 11 == 0 else nchunk


def _bdot(a, b, ca, cb, precision=None):
    return lax.dot_general(a, b, (((ca,), (cb,)), ((0,), (0,))), preferred_element_type=F32, precision=precision)


def _bnn(a, b, **kw):
    return _bdot(a, b, 2, 1, **kw)


def _bnt(a, b, **kw):
    return _bdot(a, b, 2, 2, **kw)


def _gla_consts(nb):
    row = lax.broadcasted_iota(jnp.int32, (nb, CHUNK, CHUNK), 1)
    col = lax.broadcasted_iota(jnp.int32, (nb, CHUNK, CHUNK), 2)
    lane = lax.broadcasted_iota(jnp.int32, (1, 1, 128), 2)
    return row >= col, row <= col, [lane < GLA_DK, lane >= GLA_DK]


def _gla_group_terms(g, nb, q_ref, k_ref, gd_ref, gup_ref, gb_ref, tril):
    m = nb * CHUNK
    rows = pl.ds(pl.multiple_of(g * m, CHUNK), m)
    z = _nn(gd_ref[0, rows, :].astype(BF16), gup_ref[...]) + gb_ref[...]
    valid = g * m + lax.broadcasted_iota(jnp.int32, (m, 1), 0) >= PAD
    lg = jnp.where(valid, _log_sigmoid(z) * (1.0 / GLA_TAU), 0.0)
    bcum = _bnn(tril.astype(F32), lg.reshape(nb, CHUNK, 128), precision=lax.Precision.HIGHEST)
    blast = bcum[:, CHUNK - 1:CHUNK, :]
    eb = jnp.exp(bcum)
    enb = jnp.exp(-bcum)
    erest = jnp.exp(blast - bcum)
    q = (q_ref[0, rows, :] * Q_SCALE).reshape(nb, CHUNK, 128)
    k = k_ref[0, rows, :].reshape(nb, CHUNK, 128)
    return rows, valid, z, eb, enb, erest, jnp.exp(blast), q * eb, k * enb, k * erest


def _grid_ends(grid):
    ids = [pl.program_id(i) for i in range(len(grid))]
    first = functools.reduce(jnp.logical_and, [i == 0 for i in ids])
    last = functools.reduce(jnp.logical_and, [i == g - 1 for i, g in zip(ids, grid)])
    return first, last


def _gla_fwd(u, gup, gbias, gnorm, fuse=NO_FUSE):
    bsz, tp, _ = u.shape
    nchunk = tp // CHUNK
    nb = _gla_group(nchunk)

    def body(q_ref, k_ref, v_ref, r_ref, gd_ref, gup_ref, gb_ref, gn_ref, out_ref, o_ref, st_ref, s_scr):
        tril, _, hmask = _gla_consts(nb)
        s_scr[...] = jnp.zeros_like(s_scr)
        gn = gn_ref[...]

        def group(g, carry):
            rows, _, _, _, _, _, dec, qe, ke, kd = _gla_group_terms(g, nb, q_ref, k_ref, gd_ref, gup_ref, gb_ref, tril)
            keb, kdb = ke.astype(BF16), kd.astype(BF16)
            for h in range(2):
                cols = slice(h * GLA_DV, (h + 1) * GLA_DV)
                qh = jnp.where(hmask[h], qe, 0.0).astype(BF16)
                vh = v_ref[0, rows, cols].astype(BF16).reshape(nb, CHUNK, GLA_DV)
                a = jnp.where(tril, _bnt(qh, keb), 0.0).astype(BF16)
                st = s_scr[h]
                sts = []
                for n in range(nb):
                    st_ref[0, h, g * nb + n] = st
                    sts.append(st.astype(BF16))
                    st = dec[n] * st + _tn(vh[n], kdb[n])
                s_scr[h] = st
                o = (_bnn(a, vh) + _bnt(qh, jnp.stack(sts))).reshape(nb * CHUNK, GLA_DV)
                o_ref[0, rows, cols] = o
                rms = lax.rsqrt(jnp.mean(o * o, axis=-1, keepdims=True) + LN_EPS)
                rh = r_ref[0, rows, cols]
                out_ref[0, rows, cols] = (o * rms * gn * (rh * _sigmoid(rh))).astype(BF16)
            return carry

        lax.fori_loop(0, nchunk // nb, group, 0)

    res, got = _fused_call(
        body, fuse, name="gla_fwd",
        out_shape=[jax.ShapeDtypeStruct((bsz, tp, 512), BF16), jax.ShapeDtypeStruct((bsz, tp, 512), F32),
                   jax.ShapeDtypeStruct((bsz, GLA_HEADS, nchunk, GLA_DV, 128), F32)],
        grid=(bsz, 2),
        in_specs=[pl.BlockSpec((1, tp, 128), lambda bi, p: (bi, 0, C_Q // 128 + p)),
                  pl.BlockSpec((1, tp, 128), lambda bi, p: (bi, 0, C_K // 128 + p)),
                  pl.BlockSpec((1, tp, 256), lambda bi, p: (bi, 0, C_V // 256 + p)),
                  pl.BlockSpec((1, tp, 256), lambda bi, p: (bi, 0, C_R // 256 + p)),
                  pl.BlockSpec((1, tp, 128), lambda bi, p: (bi, 0, C_GD // 128)),
                  pl.BlockSpec((128, 128), lambda bi, p: (0, p)),
                  pl.BlockSpec((1, 128), lambda bi, p: (0, p)),
                  pl.BlockSpec((1, 128), lambda bi, p: (0, 0))],
        out_specs=[pl.BlockSpec((1, tp, 256), lambda bi, p: (bi, 0, p)),
                   pl.BlockSpec((1, tp, 256), lambda bi, p: (bi, 0, p)),
                   pl.BlockSpec((1, 2, nchunk, GLA_DV, 128), lambda bi, p: (bi, p, 0, 0, 0))],
        scratch_shapes=[pltpu.VMEM((2, GLA_DV, 128), F32)],
        operands=(u, u, u, u, u, gup, gbias, gnorm))
    return res[0], res[1], res[2], got


def _out_proj_ln1(hc, gla_out, w_out, s0, cg, cb, g1, b1, fuse=NO_FUSE):
    r, d = s0.shape
    tm = _row_tile(r)

    def body(hc_ref, a_ref, w_ref, s0_ref, cg_ref, cb_ref, g_ref, b_ref, co_ref, xh_ref, rstd_ref, s1b_ref):
        for rs in _sub_rows(tm):
            xc, _ = _ln_stats(hc_ref[rs, :])
            nv = xc * cg_ref[...] + cb_ref[...]
            co = (nv * _sigmoid(nv)).astype(BF16)
            co_ref[rs, :] = co
            mix = _nn(co, w_ref[0:D_CONV, :]) + _nn(a_ref[rs, :], w_ref[D_CONV:, :])
            xh, rstd = _ln_stats(ALPHA * s0_ref[rs, :] + mix)
            xh_ref[rs, :] = xh
            rstd_ref[rs, :] = rstd
            s1b_ref[rs, :] = (xh * g_ref[...] + b_ref[...]).astype(BF16)

    row = lambda n: pl.BlockSpec((tm, n), lambda i: (i, 0))
    vec = lambda n: pl.BlockSpec((1, n), lambda i: (0, 0))
    res, got = _fused_call(
        body, fuse, name="out_proj_ln1",
        out_shape=[jax.ShapeDtypeStruct((r, D_CONV), BF16), jax.ShapeDtypeStruct((r, d), F32),
                   jax.ShapeDtypeStruct((r, 1), F32), jax.ShapeDtypeStruct((r, d), BF16)],
        grid=(r // tm,),
        in_specs=[row(D_CONV), row(512), pl.BlockSpec((d, d), lambda i: (0, 0)), row(d),
                  vec(D_CONV), vec(D_CONV), vec(d), vec(d)],
        out_specs=[row(D_CONV), row(d), row(1), row(d)], scratch_shapes=[],
        operands=(hc, gla_out, w_out, s0, cg, cb, g1, b1))
    return res[0], res[1], res[2], res[3], got


def _ffn1(s1b, w1, fuse=NO_FUSE):
    r, d = s1b.shape
    tm = _row_tile(r)
    ns, _, wn = w1.shape

    def body(a_ref, w_ref, o_ref):
        a = a_ref[...]
        for j in range(ns):
            o_ref[:, j * wn:(j + 1) * wn] = jnp.maximum(_nn(a, w_ref[j]), 0.0).astype(BF16)

    (ra,), got = _fused_call(
        body, fuse, name="ffn1", out_shape=[jax.ShapeDtypeStruct((r, D_FF), BF16)], grid=(r // tm,),
        in_specs=[pl.BlockSpec((tm, d), lambda i: (i, 0)), pl.BlockSpec(w1.shape, lambda i: (0, 0, 0))],
        out_specs=[pl.BlockSpec((tm, D_FF), lambda i: (i, 0))], scratch_shapes=[], operands=(s1b, w1))
    return ra, got


def _ffn2_ln2_loss(ra, w2, xhat1, g1, b1, g2, b2, tgt, tp):
    r, d = xhat1.shape
    tm = _row_tile(tp)
    per = tp // tm

    def body(ra_ref, w_ref, xh1_ref, g1_ref, b1_ref, g2_ref, b2_ref, tgt_ref, dt_ref, dtb_ref, acc_ref, t_ref, sem):
        i = pl.program_id(0)
        b, j = i // per, i % per

        @pl.when(i == 0)
        def _():
            acc_ref[...] = jnp.zeros_like(acc_ref)

        head_copy = pltpu.make_async_copy(tgt_ref.at[b, pl.ds(0, tm - HEAD), :], t_ref.at[pl.ds(HEAD, tm - HEAD), :], sem)
        body_copy = pltpu.make_async_copy(
            tgt_ref.at[b, pl.ds(pl.multiple_of(jnp.maximum(j * tm - HEAD, 0), 64), tm), :], t_ref, sem)

        @pl.when(j == 0)
        def _():
            t_ref[0:HEAD, :] = jnp.zeros((HEAD, d), F32)
            head_copy.start()

        pl.when(j > 0)(body_copy.start)

        sums = [jnp.zeros((1, d), F32)] * 3
        for rs in _sub_rows(tm):
            rb = ra_ref[rs, :]
            f = _nn(rb * rb, w_ref[...])
            if rs.start == 0:
                pl.when(j == 0)(head_copy.wait)
                pl.when(j > 0)(body_copy.wait)
            s1 = xh1_ref[rs, :] * g1_ref[...] + b1_ref[...]
            xh2, rstd2 = _ln_stats(ALPHA * s1 + f)
            y = xh2 * g2_ref[...] + b2_ref[...]
            rowid = (i % per) * tm + rs.start + lax.broadcasted_iota(jnp.int32, (rs.stop - rs.start, 1), 0)
            e = jnp.where(rowid >= HEAD, y - t_ref[rs, :], 0.0)
            dy = e * (1.0 / d)
            dt2 = _ln_bwd(dy * g2_ref[...], xh2, rstd2)
            dt_ref[rs, :] = dt2
            dtb_ref[rs, :] = dt2.astype(BF16)
            sums = [sums[0] + (0.5 / d) * jnp.sum(e * e, axis=0, keepdims=True),
                    sums[1] + jnp.sum(dy * xh2, axis=0, keepdims=True), sums[2] + jnp.sum(dy, axis=0, keepdims=True)]
        for k in range(3):
            acc_ref[k:k + 1, :] += sums[k]

    row = lambda n: pl.BlockSpec((tm, n), lambda i: (i, 0))
    vec = pl.BlockSpec((1, d), lambda i: (0, 0))
    return pl.pallas_call(
        body, name="ffn2_ln2_loss",
        out_shape=[jax.ShapeDtypeStruct((r, d), F32), jax.ShapeDtypeStruct((r, d), BF16),
                   jax.ShapeDtypeStruct((8, d), F32)],
        grid=(r // tm,),
        in_specs=[row(D_FF), pl.BlockSpec((D_FF, d), lambda i: (0, 0)), row(d), vec, vec, vec, vec, ANY],
        out_specs=[row(d), row(d), pl.BlockSpec((8, d), lambda i: (0, 0))],
        scratch_shapes=[pltpu.VMEM((tm, d), F32), pltpu.SemaphoreType.DMA],
        compiler_params=_params(("arbitrary",)),
    )(ra, w2, xhat1, g1, b1, g2, b2, tgt)


def _ffn_bwd_da(dt2b, w2, ra):
    r, d = dt2b.shape
    tm = _row_tile(r)

    def body(g_ref, w_ref, ra_ref, o_ref):
        o_ref[...] = (_nt(g_ref[...], w_ref[...]) * (2.0 * ra_ref[...].astype(F32))).astype(BF16)

    return pl.pallas_call(
        body, name="ffn_bwd_da", out_shape=jax.ShapeDtypeStruct((r, D_FF), BF16),
        grid=(r // tm,),
        in_specs=[pl.BlockSpec((tm, d), lambda i: (i, 0)), pl.BlockSpec((D_FF, d), lambda i: (0, 0)),
                  pl.BlockSpec((tm, D_FF), lambda i: (i, 0))],
        out_specs=pl.BlockSpec((tm, D_FF), lambda i: (i, 0)),
        compiler_params=_params(("parallel",)),
    )(dt2b, w2, ra)


def _ffn_bwd_ln1(da, w1, dt2, xhat1, rstd1, g1):
    r, d = dt2.shape
    tm = _row_tile(r)

    def body(da_ref, w_ref, dt2_ref, xh_ref, rstd_ref, g_ref, dt_ref, dtb_ref, acc_ref):
        @pl.when(pl.program_id(0) == 0)
        def _():
            acc_ref[...] = jnp.zeros_like(acc_ref)

        sums = [jnp.zeros((1, d), F32)] * 2
        for rs in _sub_rows(tm):
            ds1 = ALPHA * dt2_ref[rs, :]
            for j in range(w1.shape[0]):
                ds1 = ds1 + _nt(da_ref[rs, j * w1.shape[2]:(j + 1) * w1.shape[2]], w_ref[j])
            xh = xh_ref[rs, :]
            dt1 = _ln_bwd(ds1 * g_ref[...], xh, rstd_ref[rs, :])
            dt_ref[rs, :] = dt1
            dtb_ref[rs, :] = dt1.astype(BF16)
            sums = [sums[0] + jnp.sum(ds1 * xh, axis=0, keepdims=True), sums[1] + jnp.sum(ds1, axis=0, keepdims=True)]
        for k in range(2):
            acc_ref[k:k + 1, :] += sums[k]

    row = lambda n: pl.BlockSpec((tm, n), lambda i: (i, 0))
    return pl.pallas_call(
        body, name="ffn_bwd_ln1",
        out_shape=[jax.ShapeDtypeStruct((r, d), F32), jax.ShapeDtypeStruct((r, d), BF16),
                   jax.ShapeDtypeStruct((8, d), F32)],
        grid=(r // tm,),
        in_specs=[row(D_FF), pl.BlockSpec(w1.shape, lambda i: (0, 0, 0)), row(d), row(d), row(1),
                  pl.BlockSpec((1, d), lambda i: (0, 0))],
        out_specs=[row(d), row(d), pl.BlockSpec((8, d), lambda i: (0, 0))],
        compiler_params=_params(("arbitrary",)),
    )(da, w1, dt2, xhat1, rstd1, g1)


def _matmul_tn(lhs, rhs, bm, square_lhs=False, name="matmul_tn", fuse=NO_FUSE):
    r, m = lhs.shape
    n = rhs.shape[1]
    tk = _reduce_tile(r, True)

    def body(a_ref, b_ref, o_ref):
        @pl.when(pl.program_id(1) == 0)
        def _():
            o_ref[...] = jnp.zeros_like(o_ref)

        a = a_ref[...]
        if square_lhs:
            a = a * a
        o_ref[...] += _tn(a, b_ref[...])

    (out,), got = _fused_call(
        body, fuse, name=name, out_shape=[jax.ShapeDtypeStruct((m, n), F32)], grid=(m // bm, r // tk),
        in_specs=[pl.BlockSpec((tk, bm), lambda i, k: (k, i)), pl.BlockSpec((tk, n), lambda i, k: (k, 0))],
        out_specs=[pl.BlockSpec((bm, n), lambda i, k: (i, 0))], scratch_shapes=[], operands=(lhs, rhs))
    return out, got


def _grad_w_ff1(s1b, da, fuse=NO_FUSE):
    r, d = s1b.shape
    wn = da.shape[1] // 4
    tk = _reduce_tile(r, True)

    def body(a_ref, b_ref, o_ref):
        @pl.when(pl.program_id(1) == 0)
        def _():
            o_ref[...] = jnp.zeros_like(o_ref)

        o_ref[0] += _tn(a_ref[...], b_ref[...])

    (out,), got = _fused_call(
        body, fuse, name="grad_w_ff1", out_shape=[jax.ShapeDtypeStruct((4, d, wn), F32)], grid=(4, r // tk),
        in_specs=[pl.BlockSpec((tk, d), lambda j, k: (k, 0)), pl.BlockSpec((tk, wn), lambda j, k: (k, j))],
        out_specs=[pl.BlockSpec((1, d, wn), lambda j, k: (j, 0, 0))], scratch_shapes=[], operands=(s1b, da))
    return out, got


def _grad_w_out(conv_of, gla_of, dt1b, fuse=NO_FUSE):
    r, n = dt1b.shape
    tk = _reduce_tile(r, True)

    def body(a_ref, b_ref, g_ref, o_ref):
        @pl.when(pl.program_id(1) == 0)
        def _():
            o_ref[...] = jnp.zeros_like(o_ref)

        @pl.when(pl.program_id(0) == 0)
        def _():
            o_ref[...] += _tn(a_ref[...], g_ref[...])

        @pl.when(pl.program_id(0) == 1)
        def _():
            o_ref[...] += _tn(b_ref[...], g_ref[...])

    lhs = pl.BlockSpec((tk, 512), lambda i, k: (k, 0))
    (out,), got = _fused_call(
        body, fuse, name="grad_w_out", out_shape=[jax.ShapeDtypeStruct((2 * 512, n), F32)], grid=(2, r // tk),
        in_specs=[lhs, lhs, pl.BlockSpec((tk, n), lambda i, k: (k, 0))],
        out_specs=[pl.BlockSpec((512, n), lambda i, k: (i, 0))], scratch_shapes=[],
        operands=(conv_of, gla_of, dt1b))
    return out, got


def _out_proj_bwd(dt1b, w_out, hc, cg, cb, fuse=NO_FUSE):
    r, d = dt1b.shape
    tm = _row_tile(r)

    def body(g_ref, w_ref, hc_ref, cg_ref, cb_ref, dhc_ref, dgla_ref, acc_ref):
        @pl.when(pl.program_id(0) == 0)
        def _():
            acc_ref[...] = jnp.zeros_like(acc_ref)

        gg = cg_ref[...]
        sums = [jnp.zeros((1, D_CONV), F32)] * 3
        for rs in _sub_rows(tm):
            dmix = _nt(g_ref[rs, :], w_ref[...])
            dgla_ref[rs, :] = dmix[:, D_CONV:]
            xh, rstd = _ln_stats(hc_ref[rs, :])
            nv = xh * gg + cb_ref[...]
            sig = _sigmoid(nv)
            dn = dmix[:, :D_CONV] * (sig * (1.0 + nv * (1.0 - sig)))
            dhc = _ln_bwd(dn * gg, xh, rstd)
            dhc_ref[rs, :] = dhc
            sums = [sums[0] + jnp.sum(dhc, axis=0, keepdims=True), sums[1] + jnp.sum(dn * xh, axis=0, keepdims=True),
                    sums[2] + jnp.sum(dn, axis=0, keepdims=True)]
        for k in range(3):
            acc_ref[k:k + 1, :] += sums[k]

    row = lambda n: pl.BlockSpec((tm, n), lambda i: (i, 0))
    vec = pl.BlockSpec((1, D_CONV), lambda i: (0, 0))
    res, got = _fused_call(
        body, fuse, name="out_proj_bwd",
        out_shape=[jax.ShapeDtypeStruct((r, D_CONV), F32), jax.ShapeDtypeStruct((r, 512), F32),
                   jax.ShapeDtypeStruct((8, D_CONV), F32)],
        grid=(r // tm,),
        in_specs=[row(d), pl.BlockSpec((d, d), lambda i: (0, 0)), row(D_CONV), vec, vec],
        out_specs=[row(D_CONV), row(512), pl.BlockSpec((8, D_CONV), lambda i: (0, 0))], scratch_shapes=[],
        operands=(dt1b, w_out, hc, cg, cb))
    return res[0], res[1], res[2], got


def _conv_bwd(dhc, u, conv_w, fuse=NO_FUSE):
    bsz, tp, _ = u.shape
    nchunk = tp // CHUNK
    win = CHUNK + 32
    nct = D_CONV // 128

    def body(dhc_ref, cv_ref, cg_ref, w_ref, dv_ref, dg_ref, dw_ref, h_scr, dhc_scr, hwin, dwin, dw_scr):
        h_scr[0:32, :] = jnp.zeros((32, 128), F32)
        h_scr[32:32 + tp, :] = cv_ref[0] * _sigmoid(cg_ref[0])
        dhc_scr[0:tp, :] = dhc_ref[0]
        dhc_scr[tp:tp + 32, :] = jnp.zeros((32, 128), F32)
        dw_scr[...] = jnp.zeros_like(dw_scr)

        def step(n, carry):
            r0 = pl.multiple_of(n * CHUNK, CHUNK)
            rows = pl.ds(r0, CHUNK)
            hwin[...] = h_scr[pl.ds(r0, win), :]
            dwin[...] = dhc_scr[pl.ds(r0, win), :]
            dcur = dwin[0:CHUNK, :]
            acc = jnp.zeros((CHUNK, 128), F32)
            for j in range(CONV_WIDTH):
                acc = acc + w_ref[j:j + 1, :] * dwin[30 - j:30 - j + CHUNK, :]
                prod = dcur * hwin[2 + j:2 + j + CHUNK, :]
                dw_scr[j * 8:(j + 1) * 8, :] += jnp.sum(prod.reshape(CHUNK // 8, 8, 128), axis=0)
            cg = cg_ref[0, rows, :]
            sig = _sigmoid(cg)
            rowid = n * CHUNK + lax.broadcasted_iota(jnp.int32, (CHUNK, 1), 0)
            dh = jnp.where(rowid >= PAD, acc, 0.0)
            dv_ref[0, rows, :] = (dh * sig).astype(BF16)
            dg_ref[0, rows, :] = (dh * cv_ref[0, rows, :] * sig * (1.0 - sig)).astype(BF16)
            return carry

        lax.fori_loop(0, nchunk, step, 0)
        dw_ref[0] = jnp.zeros((32, 128), F32)
        for j in range(CONV_WIDTH):
            dw_ref[0, j:j + 1, :] = jnp.sum(dw_scr[j * 8:(j + 1) * 8, :], axis=0, keepdims=True)

    blk = lambda off: pl.BlockSpec((1, tp, 128), lambda bi, t: (bi, 0, off // 128 + t))
    res, got = _fused_call(
        body, fuse, name="conv_bwd",
        out_shape=[jax.ShapeDtypeStruct((bsz, tp, D_CONV), BF16), jax.ShapeDtypeStruct((bsz, tp, D_CONV), BF16),
                   jax.ShapeDtypeStruct((bsz, 32, D_CONV), F32)],
        grid=(bsz, nct),
        in_specs=[blk(0), blk(C_VAL), blk(C_GATE), pl.BlockSpec((32, 128), lambda bi, t: (0, t))],
        out_specs=[blk(0), blk(0), pl.BlockSpec((1, 32, 128), lambda bi, t: (bi, 0, t))],
        scratch_shapes=[pltpu.VMEM((tp + 32, 128), F32), pltpu.VMEM((tp + 32, 128), F32),
                        pltpu.VMEM((win, 128), F32), pltpu.VMEM((win, 128), F32),
                        pltpu.VMEM((CONV_WIDTH * 8, 128), F32)],
        operands=(dhc, u, u, conv_w))
    return res[0], res[1], res[2], got


def _gla_bwd(dgla, u, o_pre, states, gup, gbias, gnorm, fuse=NO_FUSE):
    bsz, tp, _ = u.shape
    nchunk = tp // CHUNK
    nb = _gla_group(nchunk)

    def body(dy_ref, q_ref, k_ref, v_ref, r_ref, gd_ref, o_ref, st_ref, gup_ref, gb_ref, gn_ref,
             dq_ref, dk_ref, dv_ref, dr_ref, dgd_ref, dgup_ref, vec_ref, h_scr, gup_acc):
        tril, triu, hmask = _gla_consts(nb)
        h_scr[...] = jnp.zeros_like(h_scr)
        gup_acc[...] = jnp.zeros_like(gup_acc)
        gn = gn_ref[...]
        gupb = gup_ref[...]
        m = nb * CHUNK
        ngroup = nchunk // nb

        def group(i, carry):
            dbias, dgn = carry
            g = ngroup - 1 - i
            rows, valid, z, eb, enb, erest, dec, qe, ke, kd = _gla_group_terms(
                g, nb, q_ref, k_ref, gd_ref, gup_ref, gb_ref, tril)
            keb, kdb = ke.astype(BF16), kd.astype(BF16)
            dqe = jnp.zeros((nb, CHUNK, 128), F32)
            dke = jnp.zeros((nb, CHUNK, 128), F32)
            dkd = jnp.zeros((nb, CHUNK, 128), F32)
            ddec = jnp.zeros((nb, 1, 128), F32)
            for h in range(2):
                cols = slice(h * GLA_DV, (h + 1) * GLA_DV)
                o = o_ref[0, rows, cols]
                rh = r_ref[0, rows, cols]
                dy = dy_ref[0, rows, cols]
                rms = lax.rsqrt(jnp.mean(o * o, axis=-1, keepdims=True) + LN_EPS)
                nrm = o * rms
                sig = _sigmoid(rh)
                sw = rh * sig
                dr_ref[0, rows, cols] = (dy * nrm * gn * (sig * (1.0 + rh * (1.0 - sig)))).astype(BF16)
                dgn = dgn + jnp.sum(dy * nrm * sw, axis=0, keepdims=True)
                dn = dy * gn * sw
                do = rms * (dn - nrm * jnp.mean(dn * nrm, axis=-1, keepdims=True))
                dob = do.astype(BF16).reshape(nb, CHUNK, GLA_DV)
                qh = jnp.where(hmask[h], qe, 0.0).astype(BF16)
                vh = v_ref[0, rows, cols].astype(BF16).reshape(nb, CHUNK, GLA_DV)
                ht = h_scr[h]
                hts = [None] * nb
                for n in reversed(range(nb)):
                    hts[n] = ht
                    ht = dec[n] * ht + _tn(dob[n], qh[n])
                h_scr[h] = ht
                htf = jnp.stack(hts)
                htb = htf.astype(BF16)
                st = st_ref[0, h, pl.ds(g * nb, nb)]
                at = jnp.where(triu, _bnt(keb, qh), 0.0).astype(BF16)
                da = jnp.where(tril, _bnt(dob, vh), 0.0).astype(BF16)
                dat = jnp.where(triu, _bnt(vh, dob), 0.0).astype(BF16)
                dqe = dqe + jnp.where(hmask[h], _bnn(da, keb) + _bnn(dob, st.astype(BF16)), 0.0)
                dke = dke + _bnn(dat, qh)
                dv_ref[0, rows, cols] = (_bnn(at, dob) + _bnt(kdb, htb)).reshape(m, GLA_DV).astype(BF16)
                dkd = dkd + jnp.where(hmask[h], _bnn(vh, htb), 0.0)
                ddec = ddec + jnp.where(hmask[h], jnp.sum(htf * st, axis=1, keepdims=True), 0.0)
            dq_ref[0, rows, :] = (dqe * eb * Q_SCALE).reshape(m, 128).astype(BF16)
            dk_ref[0, rows, :] = (dke * enb + dkd * erest).reshape(m, 128).astype(BF16)
            db = dqe * qe - dke * ke - dkd * kd
            dblast = jnp.sum(dkd * kd, axis=1, keepdims=True) + ddec * dec
            lastrow = lax.broadcasted_iota(jnp.int32, (1, CHUNK, 1), 1) == CHUNK - 1
            db = db + jnp.where(lastrow, dblast, 0.0)
            dlg = _bnn(triu.astype(F32), db, precision=lax.Precision.HIGHEST).reshape(m, 128)
            dz = jnp.where(valid, dlg, 0.0) * (1.0 / GLA_TAU) * (1.0 - _sigmoid(z))
            dzb = dz.astype(BF16)
            dgd_ref[0, 0, rows, :] = _nt(dzb, gupb).astype(BF16)
            gup_acc[...] += _tn(gd_ref[0, rows, :].astype(BF16), dzb)
            return dbias + jnp.sum(dz, axis=0, keepdims=True), dgn

        zero = jnp.zeros((1, 128), F32)
        dbias, dgn = lax.fori_loop(0, ngroup, group, (zero, zero))
        dgup_ref[0] = gup_acc[...]
        vec_ref[0] = jnp.zeros((8, 128), F32)
        vec_ref[0, 0:1, :] = dbias
        vec_ref[0, 1:2, :] = dgn

    pair = lambda w, off: pl.BlockSpec((1, tp, w), lambda bi, p: (bi, 0, off // w + p))
    return _fused_call(
        body, fuse, name="gla_bwd",
        out_shape=[jax.ShapeDtypeStruct((bsz, tp, 256), BF16), jax.ShapeDtypeStruct((bsz, tp, 256), BF16),
                   jax.ShapeDtypeStruct((bsz, tp, 512), BF16), jax.ShapeDtypeStruct((bsz, tp, 512), BF16),
                   jax.ShapeDtypeStruct((bsz, 2, tp, 128), BF16), jax.ShapeDtypeStruct((bsz, 128, 256), F32),
                   jax.ShapeDtypeStruct((bsz, 8, 256), F32)],
        grid=(bsz, 2),
        in_specs=[pair(256, 0), pair(128, C_Q), pair(128, C_K), pair(256, C_V), pair(256, C_R),
                  pl.BlockSpec((1, tp, 128), lambda bi, p: (bi, 0, C_GD // 128)),
                  pair(256, 0),
                  pl.BlockSpec((1, 2, nchunk, GLA_DV, 128), lambda bi, p: (bi, p, 0, 0, 0)),
                  pl.BlockSpec((128, 128), lambda bi, p: (0, p)),
                  pl.BlockSpec((1, 128), lambda bi, p: (0, p)),
                  pl.BlockSpec((1, 128), lambda bi, p: (0, 0))],
        out_specs=[pair(128, 0), pair(128, 0), pair(256, 0), pair(256, 0),
                   pl.BlockSpec((1, 1, tp, 128), lambda bi, p: (bi, p, 0, 0)),
                   pl.BlockSpec((1, 128, 128), lambda bi, p: (bi, 0, p)),
                   pl.BlockSpec((1, 8, 128), lambda bi, p: (bi, 0, p))],
        scratch_shapes=[pltpu.VMEM((2, GLA_DV, 128), F32), pltpu.VMEM((128, 128), F32)],
        operands=(dgla, u, u, u, u, u, o_pre, states, gup, gbias, gnorm))


_DU_OFFSETS = (C_VAL, C_GATE, C_Q, C_K, C_V, C_R)
_DU_WIDTHS = (512, 512, 256, 256, 512, 512)


def _du_specs(tm, per, row_map):
    specs = [pl.BlockSpec((tm, w), row_map) for w in _DU_WIDTHS]
    for p in range(2):
        specs.append(pl.BlockSpec((1, 1, tm, 128), lambda *ix, p=p: (row_map(*ix)[0] // per, p, row_map(*ix)[0] % per, 0)))
    return specs


def _du_pieces(refs):
    out = [(off, ref[...]) for off, ref in zip(_DU_OFFSETS, refs[:6])]
    dgd = (refs[6][0, 0].astype(F32) + refs[7][0, 0].astype(F32)).astype(BF16)
    out.append((C_GD, dgd))
    return out


def _in_proj_bwd(pieces, dgd, w_int, dt1, tp, fuse=NO_FUSE):
    r, d = dt1.shape
    tm = _row_tile(tp)
    per = tp // tm

    def body(*refs):
        w_ref, dt_ref, o_ref = refs[8:]
        acc = ALPHA * dt_ref[...]
        for off, val in _du_pieces(refs[:8]):
            acc = acc + _nn(val, w_ref[off:off + val.shape[1], :])
        o_ref[...] = acc

    row = lambda i: (i, 0)
    (ds0,), got = _fused_call(
        body, fuse, name="in_proj_bwd", out_shape=[jax.ShapeDtypeStruct((r, d), F32)], grid=(r // tm,),
        in_specs=_du_specs(tm, per, row) + [pl.BlockSpec((D_IN_PAD, d), lambda i: (0, 0)), pl.BlockSpec((tm, d), row)],
        out_specs=[pl.BlockSpec((tm, d), row)], scratch_shapes=[], operands=(*pieces, dgd, dgd, w_int, dt1))
    return ds0, got


def _grad_w_in(pieces, dgd, s0b, tp, fuse=NO_FUSE):
    r, d = s0b.shape
    tk = _reduce_tile(tp, False)
    per = tp // tk

    def body(*refs):
        s_ref, o_ref = refs[8:]

        @pl.when(pl.program_id(0) == 0)
        def _():
            o_ref[...] = jnp.zeros_like(o_ref)

        s = s_ref[...]
        for off, val in _du_pieces(refs[:8]):
            o_ref[off:off + val.shape[1], :] += _tn(val, s)

    row = lambda k: (k, 0)
    (out,), got = _fused_call(
        body, fuse, name="grad_w_in", out_shape=[jax.ShapeDtypeStruct((D_IN_PAD, d), F32)], grid=(r // tk,),
        in_specs=_du_specs(tk, per, row) + [pl.BlockSpec((tk, d), row)],
        out_specs=[pl.BlockSpec((D_IN_PAD, d), lambda k: (0, 0))], scratch_shapes=[],
        operands=(*pieces, dgd, dgd, s0b))
    return out, got


def _ln_in_bwd(ds0, x, meta, g):
    bsz, s, d = x.shape
    tp = s + HEAD
    nh = 2
    sh = s // nh
    rc = min(256, sh)

    def body(ds_ref, x_ref, meta_ref, g_ref, gx_ref, dm_ref, vec_ref):
        h = pl.program_id(1)
        gg = g_ref[...]

        @pl.when(h == 0)
        def _():
            mh, mr = _ln_stats(meta_ref[...])
            dsm = ds_ref[0, PAD:HEAD, :]
            dm_ref[0] = _ln_bwd(dsm * gg, mh, mr)
            vec_ref[0] = jnp.zeros((8, d), F32)
            vec_ref[0, 0:1, :] = jnp.sum(dsm * mh, axis=0, keepdims=True)
            vec_ref[0, 1:2, :] = jnp.sum(dsm, axis=0, keepdims=True)

        def step(i, carry):
            sg, sb = carry
            dst = pl.ds(pl.multiple_of(i * rc, rc), rc)
            src = pl.ds(pl.multiple_of(HEAD + h * sh + i * rc, 64), rc)
            xh, rstd = _ln_stats(x_ref[0, dst, :])
            dsv = ds_ref[0, src, :]
            gx_ref[0, dst, :] = _ln_bwd(dsv * gg, xh, rstd)
            return sg + jnp.sum(dsv * xh, axis=0, keepdims=True), sb + jnp.sum(dsv, axis=0, keepdims=True)

        zero = jnp.zeros((1, d), F32)
        sg, sb = lax.fori_loop(0, sh // rc, step, (zero, zero))
        vec_ref[0, 0:1, :] += sg
        vec_ref[0, 1:2, :] += sb

    return pl.pallas_call(
        body, name="ln_in_bwd",
        out_shape=[jax.ShapeDtypeStruct((bsz, s, d), F32), jax.ShapeDtypeStruct((bsz, N_META, d), F32),
                   jax.ShapeDtypeStruct((bsz, 8, d), F32)],
        grid=(bsz, nh),
        in_specs=[pl.BlockSpec((1, tp, d), lambda bi, hi: (bi, 0, 0)),
                  pl.BlockSpec((1, sh, d), lambda bi, hi: (bi, hi, 0)),
                  pl.BlockSpec((N_META, d), lambda bi, hi: (0, 0)),
                  pl.BlockSpec((1, d), lambda bi, hi: (0, 0))],
        out_specs=[pl.BlockSpec((1, sh, d), lambda bi, hi: (bi, hi, 0)),
                   pl.BlockSpec((1, N_META, d), lambda bi, hi: (bi, 0, 0)),
                   pl.BlockSpec((1, 8, d), lambda bi, hi: (bi, 0, 0))],
        compiler_params=_params(("parallel", "arbitrary")),
    )(ds0, x, meta, g)


def _rows128(a):
    return a.reshape(-1, 128)


def kernel(x, meta_tokens, ln_in_g, ln_in_b, w_in, conv_w, conv_b, conv_ln_g, conv_ln_b, gate_up, gate_bias, gla_norm_g, w_out, ln1_g, ln1_b, w_ff1, w_ff2, ln2_g, ln2_b, loss_target, m_meta_tokens, m_ln_in_g, m_ln_in_b, m_w_in, m_conv_w, m_conv_b, m_conv_ln_g, m_conv_ln_b, m_gate_up, m_gate_bias, m_gla_norm_g, m_w_out, m_ln1_g, m_ln1_b, m_w_ff1, m_w_ff2, m_ln2_g, m_ln2_b, v_meta_tokens, v_ln_in_g, v_ln_in_b, v_w_in, v_conv_w, v_conv_b, v_conv_ln_g, v_conv_ln_b, v_gate_up, v_gate_bias, v_gla_norm_g, v_w_out, v_ln1_g, v_ln1_b, v_w_ff1, v_w_ff2, v_ln2_g, v_ln2_b):
    bsz, seq, d = x.shape
    tp = seq + HEAD
    r = bsz * tp
    xi, yi, ci = _mesh_pos()
    chip = 2 * xi + yi
    c_arr = jnp.reshape(ci, (1,)).astype(jnp.int32)
    pos_arr = jnp.stack([chip, ci]).astype(jnp.int32)

    sh_in = D_IN // 4
    shard_in = jnp.pad(w_in[0].T.astype(BF16), ((0, D_IN_PAD // 4 - sh_in), (0, 0)))
    shard_w1, shard_wout, shard_w2 = w_ff1[0].astype(BF16), w_out[0].astype(BF16), w_ff2[0].astype(BF16)
    small_w = jnp.concatenate([_rows128(meta_tokens), _rows128(conv_w[0]), _rows128(gate_up[0])], axis=0)
    ln_in_g2, ln_in_b2 = ln_in_g.reshape(1, d), ln_in_b.reshape(1, d)

    s0, s0b, ((g_int,), (g_small,)) = _ln_in_fwd(
        x, ln_in_g2, ln_in_b2, [("gather", [(shard_in, 0, D_IN_PAD // 4, None)]), ("small", [small_w])])
    (g_int,) = _place_own([g_int], [shard_in])
    (g_small,) = _place_own([g_small], [small_w])
    n_meta_rows, n_cw_rows = N_META * 256 // 128, CONV_WIDTH * 128 // 128
    meta_full = jnp.concatenate([g_small[j, :n_meta_rows].reshape(N_META, 256) for j in range(4)], axis=1)
    convw_full = jnp.concatenate(
        [g_small[j, n_meta_rows:n_meta_rows + n_cw_rows].reshape(CONV_WIDTH, 128) for j in range(4)], axis=1)
    gup_full = jnp.concatenate(
        [g_small[j, n_meta_rows + n_cw_rows:].reshape(GLA_RANK, 64) for j in range(4)], axis=1)
    convw_p = jnp.pad(convw_full, ((0, 1), (0, 0)))
    gup_p = jnp.pad(gup_full, ((0, 128 - GLA_RANK), (0, 0))).astype(BF16)
    s0, s0b = _ln_meta(s0, s0b, meta_full, ln_in_g2, ln_in_b2)
    w_int = jnp.pad(g_int[:, :sh_in].reshape(D_IN, d), ((0, D_IN_PAD - D_IN), (0, 0)))
    s0f, s0bf = s0.reshape(r, d), s0b.reshape(r, d)
    u, ((w1_buf,),) = _in_proj(s0bf, w_int, [("gather", [(shard_w1, 0, 640, None)])])
    (w1_buf,) = _place_own([w1_buf], [shard_w1])
    u3 = u.reshape(bsz, tp, D_IN_PAD)
    hc, ((w1_buf,),) = _conv_fwd(u3, convw_p, conv_b, [("gather", [(shard_w1, 640, 384, w1_buf)])])
    hc = hc.reshape(r, D_CONV)
    gla_out, o_pre, states, ((g_wout,),) = _gla_fwd(
        u3, gup_p, gate_bias, gla_norm_g, [("gather", [(shard_wout, 0, 256, None)])])
    (g_wout,) = _place_own([g_wout], [shard_wout])
    wout = g_wout.reshape(d, d)
    gla_of = gla_out.reshape(r, 512)
    conv_of, xhat1, rstd1, s1b, ((w2_buf,),) = _out_proj_ln1(
        hc, gla_of, wout, s0f, conv_ln_g, conv_ln_b, ln1_g, ln1_b, [("gather", [(shard_w2, 0, 448, None)])])
    (w2_buf,) = _place_own([w2_buf], [shard_w2])
    w1 = w1_buf
    ra, ((w2_buf,),) = _ffn1(s1b, w1, [("gather", [(shard_w2, 448, 576, w2_buf)])])
    w2 = w2_buf.reshape(D_FF, d)
    dt2, dt2b, acc2 = _ffn2_ln2_loss(ra, w2, xhat1, ln1_g, ln1_b, ln2_g, ln2_b, loss_target, tp)

    def add_pair(g, got, splits=None):
        return _add_pair(g, got, c_arr, splits)

    da = _ffn_bwd_da(dt2b, w2, ra)
    dt1, dt1b, acc1 = _ffn_bwd_ln1(da, w1, dt2, xhat1, rstd1, ln1_g)
    g_w2, _ = _matmul_tn(ra, dt2b, 1024, square_lhs=True, name="grad_w_ff2")
    big_w2 = g_w2.reshape(4, D_FF // 4, d)
    big_w1, ((pair_w2,),) = _grad_w_ff1(s1b, da, [("pair", [big_w2])])
    dhc, dgla, cacc, ((pair_w1,),) = _out_proj_bwd(dt1b, wout, hc, conv_ln_g, conv_ln_b, [("pair", [big_w1])])
    (part_w1,), (part_w2a, part_w2b) = add_pair(big_w1, pair_w1), add_pair(big_w2, pair_w2, [512, 512])
    g_wout, _ = _grad_w_out(conv_of, gla_of, dt1b)
    big_wout = g_wout.reshape(4, d // 4, d)
    dcv, dcg, dcw, ((chip_w1,), (pair_wout,)) = _conv_bwd(
        dhc.reshape(bsz, tp, D_CONV), u3, convw_p, [("exchange", [part_w1]), ("pair", [big_wout])])
    (part_wout,) = add_pair(big_wout, pair_wout)
    (dq, dk, dv, dr, dgd, dgup, gvec), ((chip_w2a,),) = _gla_bwd(
        dgla.reshape(bsz, tp, 512), u3, o_pre, states, gup_p, gate_bias, gla_norm_g, [("exchange", [part_w2a])])
    pieces = [a.reshape(r, a.shape[-1]) for a in (dcv, dcg, dq, dk, dv, dr)]
    g_wint, ((chip_w2b, chip_wout),) = _grad_w_in(
        pieces, dgd, s0bf, tp, [("exchange", [part_w2b, part_wout])])
    big_win = jnp.stack([g_wint[j * sh_in:(j + 1) * sh_in] for j in range(4)])
    (half_w1, half_w2), ((pair_win,),) = _add_chips_many(
        [(big_w1, pair_w1, [chip_w1]), (big_w2, pair_w2, [chip_w2a, chip_w2b])], [("pair", [big_win])])
    ds0, ((chip_win,), (sib_w1, sib_w2)) = _in_proj_bwd(
        pieces, dgd, w_int, dt1, tp, [("exchange", add_pair(big_win, pair_win)), ("swap", [half_w1, half_w2])])
    grad_x, dmeta, lvec = _ln_in_bwd(ds0.reshape(bsz, tp, d), x, meta_full, ln_in_g2)

    red, loss = _sum_small(_allgather_small(_pack_small(acc2, acc1, lvec, cacc, gvec, dcw, dgup, dmeta)))

    half_win = _add_chips(big_win, pair_win, [chip_win], pos_arr)
    half_wout = _add_chips(big_wout, pair_wout, [chip_wout], pos_arr)
    sib_win, sib_wout = _pair_swap([half_win, half_wout])
    halves = {"w_in": (half_win, sib_win), "w_out": (half_wout, sib_wout), "w_ff1": (half_w1, sib_w1),
              "w_ff2": (half_w2, sib_w2)}

    grads = {}
    weights = dict(meta_tokens=meta_tokens, ln_in_g=ln_in_g, ln_in_b=ln_in_b, w_in=w_in, conv_w=conv_w, conv_b=conv_b,
                   conv_ln_g=conv_ln_g, conv_ln_b=conv_ln_b, gate_up=gate_up, gate_bias=gate_bias,
                   gla_norm_g=gla_norm_g, w_out=w_out, ln1_g=ln1_g, ln1_b=ln1_b, w_ff1=w_ff1, w_ff2=w_ff2,
                   ln2_g=ln2_g, ln2_b=ln2_b)
    moms = dict(meta_tokens=(m_meta_tokens, v_meta_tokens), ln_in_g=(m_ln_in_g, v_ln_in_g),
                ln_in_b=(m_ln_in_b, v_ln_in_b), w_in=(m_w_in, v_w_in), conv_w=(m_conv_w, v_conv_w),
                conv_b=(m_conv_b, v_conv_b), conv_ln_g=(m_conv_ln_g, v_conv_ln_g),
                conv_ln_b=(m_conv_ln_b, v_conv_ln_b), gate_up=(m_gate_up, v_gate_up),
                gate_bias=(m_gate_bias, v_gate_bias), gla_norm_g=(m_gla_norm_g, v_gla_norm_g),
                w_out=(m_w_out, v_w_out), ln1_g=(m_ln1_g, v_ln1_g), ln1_b=(m_ln1_b, v_ln1_b),
                w_ff1=(m_w_ff1, v_w_ff1), w_ff2=(m_w_ff2, v_w_ff2), ln2_g=(m_ln2_g, v_ln2_g),
                ln2_b=(m_ln2_b, v_ln2_b))
    names = list(weights)
    big_names = ("w_in", "w_out", "w_ff1", "w_ff2")
    delta, new_m, new_v = {}, {}, {}
    for k in big_names:
        to2d = (lambda a: a[0].T) if k == "w_in" else (lambda a: a[0])
        back = (lambda a: a.T[None]) if k == "w_in" else (lambda a: a[None])
        res = _adamw_halves(to2d(weights[k]), *halves[k], to2d(moms[k][0]), to2d(moms[k][1]), c_arr)
        grads[k], delta[k], new_m[k], new_v[k] = [back(a) for a in res]
    small_names = [k for k in names if k not in big_names]
    two = lambda a: a.reshape(-1, a.shape[-1])

    def my_cols(full, rows, width):
        return lax.dynamic_slice(full, (0, chip * width), (rows, width))

    sharded = {
        "meta_tokens": my_cols(red[SMALL_AT["meta_tokens"][0]:SMALL_AT["meta_tokens"][0] + N_META], N_META, 256),
        "conv_w": my_cols(red[SMALL_CONV_W:SMALL_CONV_W + 16].reshape(32, D_CONV), CONV_WIDTH, 128),
        "gate_up": my_cols(red[SMALL_GATE_UP:SMALL_GATE_UP + 4].reshape(GLA_RANK, 256), GLA_RANK, 64)}
    upd = _adamw_small(red, sharded, {k: (two(weights[k]), two(moms[k][0]), two(moms[k][1])) for k in small_names})
    for k in small_names:
        shp = weights[k].shape
        grads[k], delta[k], new_m[k], new_v[k] = [a.reshape(shp) for a in upd[k]]
    loss = loss.reshape(())

    return (loss, grad_x, *[grads[k] for k in names], *[delta[k] for k in names],
            *[new_m[k] for k in names], *[new_v[k] for k in names])
```

```python
import functools

import jax
import jax.numpy as jnp
from jax import lax
from jax.experimental import pallas as pl
from jax.experimental.pallas import tpu as pltpu

F32 = jnp.float32
BF16 = jnp.bfloat16

D_MODEL = 1024
N_META = 16
D_CONV = 512
CONV_WIDTH = 31
GLA_HEADS = 4
GLA_DV = 128
GLA_DK = 64
GLA_RANK = 16
GLA_TAU = 16.0
CHUNK = 64
D_FF = 4096
LN_EPS = 1e-5
ALPHA = 2.0 ** 0.25
D_IN = 2576
D_IN_PAD = 2688
PAD = CHUNK - N_META
HEAD = PAD + N_META
Q_SCALE = GLA_DK ** -0.5
ADAM_LR, ADAM_B1, ADAM_B2, ADAM_EPS, ADAM_WD, ADAM_STEP = 0.001, 0.9, 0.999, 1e-08, 0.01, 10
HALF = D_MODEL // 2
VMEM_LIMIT = 56 * 1024 * 1024
MESH = pl.DeviceIdType.MESH

C_VAL, C_GATE, C_Q, C_K, C_V, C_R, C_GD = 0, 512, 1024, 1280, 1536, 2048, 2560

SMALL_AT = {"loss": (0, 1, 0, 1024), "ln1_g": (1, 1, 0, 1024), "ln1_b": (2, 1, 0, 1024), "ln2_g": (3, 1, 0, 1024),
            "ln2_b": (4, 1, 0, 1024), "conv_b": (5, 1, 0, 512), "conv_ln_g": (5, 1, 512, 512),
            "conv_ln_b": (6, 1, 0, 512), "gate_bias": (6, 1, 512, 256), "gla_norm_g": (6, 1, 768, 128),
            "meta_tokens": (32, 16, 0, 1024), "ln_in_g": (48, 1, 0, 1024), "ln_in_b": (49, 1, 0, 1024)}
SMALL_CONV_W, SMALL_GATE_UP = 8, 24
SMALL_EARLY_ROWS, SMALL_ROWS = 32, 56


def _params(sem=None, **kw):
    return pltpu.CompilerParams(dimension_semantics=sem, vmem_limit_bytes=VMEM_LIMIT, **kw)


def _row_tile(tp):
    for t in (704, 352, 192, 64):
        if tp % t == 0:
            return t
    raise ValueError(tp)


def _reduce_tile(tp, big):
    for t in ((2112, 1056, 704) if big else (1056, 704)) + (352, 192, 64):
        if tp % t == 0:
            return t
    raise ValueError(tp)


def _sub_rows(tm):
    return [slice(0, tm)]


def _dot(a, b, dims, precision=None):
    return lax.dot_general(a, b, (dims, ((), ())), preferred_element_type=F32, precision=precision)


def _nn(a, b, **kw):
    return _dot(a, b, ((1,), (0,)), **kw)


def _nt(a, b, **kw):
    return _dot(a, b, ((1,), (1,)), **kw)


def _tn(a, b, **kw):
    return _dot(a, b, ((0,), (0,)), **kw)


def _sigmoid(x):
    return 1.0 / (1.0 + jnp.exp(-x))


def _log_sigmoid(z):
    return jnp.minimum(z, 0.0) - jnp.log(1.0 + jnp.exp(-jnp.abs(z)))


def _ln_stats(t):
    mu = jnp.mean(t, axis=-1, keepdims=True)
    d = t - mu
    var = jnp.mean(d * d, axis=-1, keepdims=True)
    rstd = lax.rsqrt(var + LN_EPS)
    return d * rstd, rstd


def _ln_bwd(dxhat, xhat, rstd):
    m1 = jnp.mean(dxhat, axis=-1, keepdims=True)
    m2 = jnp.mean(dxhat * xhat, axis=-1, keepdims=True)
    return rstd * (dxhat - m1 - xhat * m2)


def _mesh_pos():
    return lax.axis_index("x"), lax.axis_index("y"), lax.axis_index("c")


ANY = pl.BlockSpec(memory_space=pl.ANY)


def _gather_sems(n):
    return [pltpu.SemaphoreType.DMA((n, 3))] * 4


def _gather_steps(ins, outs, sems, ranges=None):
    n = len(ins)
    send, recv, fsend, frecv = sems
    ranges = ranges or [(0, ref.shape[0]) for ref in ins]
    x, y, c = _mesh_pos()
    me = 2 * x + y
    sibling = (x, y, 1 - c)
    chips = [(1 - x, y), (x, 1 - y), (1 - x, 1 - y)]
    chip_idx = [2 * px + py for px, py in chips]
    mine = [pl.ds(pl.multiple_of(r0 + c * (nr // 2), 16), nr // 2) for r0, nr in ranges]
    other = [pl.ds(pl.multiple_of(r0 + (1 - c) * (nr // 2), 16), nr // 2) for r0, nr in ranges]
    pairs = [(a, k) for a in range(n) for k in range(3)]

    def ici(a, k, slab):
        return pltpu.make_async_remote_copy(
            src_ref=ins[a].at[mine[a], :], dst_ref=outs[a].at[slab, mine[a], :],
            send_sem=send.at[a, k], recv_sem=recv.at[a, k], device_id=(*chips[k], c), device_id_type=MESH)

    def forward(a, k, rows):
        blk = outs[a].at[chip_idx[k], rows[a], :]
        return pltpu.make_async_remote_copy(
            src_ref=blk, dst_ref=blk, send_sem=fsend.at[a, k], recv_sem=frecv.at[a, k],
            device_id=sibling, device_id_type=MESH)

    def start():
        for a, k in pairs:
            ici(a, k, me).start()

    def relay():
        for a, k in pairs:
            ici(a, k, chip_idx[k]).wait_recv()
            forward(a, k, mine).start()

    def finish():
        for a, k in pairs:
            forward(a, k, other).wait_recv()
        for a, k in pairs:
            ici(a, k, me).wait_send()
            forward(a, k, mine).wait_send()

    return start, relay, finish


def _place_own(gathered, shards):
    chip = 2 * lax.axis_index("x") + lax.axis_index("y")
    return [lax.dynamic_update_slice(g, s[None], (chip, 0, 0)) for g, s in zip(gathered, shards)]


def _gather_small_steps(ins, outs, sems):
    send, recv = sems
    x, y, c = _mesh_pos()
    chips = [(1 - x, y), (x, 1 - y), (1 - x, 1 - y)]

    def copy(a, k, slot):
        return pltpu.make_async_remote_copy(
            src_ref=ins[a], dst_ref=outs[a].at[slot], send_sem=send.at[a, k], recv_sem=recv.at[a, k],
            device_id=(*chips[k], c), device_id_type=MESH)

    def start():
        for a in range(len(ins)):
            for k in range(3):
                copy(a, k, 2 * x + y).start()

    def finish():
        for a in range(len(ins)):
            for k, (px, py) in enumerate(chips):
                copy(a, k, 2 * px + py).wait_recv()
            for k in range(3):
                copy(a, k, 2 * x + y).wait_send()

    return start, finish


def _pair_swap_steps(ins, outs, sems):
    send, recv = sems
    x, y, c = _mesh_pos()
    cps = [pltpu.make_async_remote_copy(
        src_ref=ins[a], dst_ref=outs[a], send_sem=send.at[a], recv_sem=recv.at[a],
        device_id=(x, y, 1 - c), device_id_type=MESH) for a in range(len(ins))]

    def start():
        for cp in cps:
            cp.start()

    def finish():
        for cp in cps:
            cp.wait()

    return start, finish


def _pair_exchange_steps(ins, outs, sems):
    send, recv = sems
    x, y, c = _mesh_pos()
    other = pl.ds(pl.multiple_of((1 - c) * HALF, 128), HALF)
    cps = [pltpu.make_async_remote_copy(
        src_ref=ins[a].at[:, :, other], dst_ref=outs[a], send_sem=send.at[a], recv_sem=recv.at[a],
        device_id=(x, y, 1 - c), device_id_type=MESH) for a in range(len(ins))]

    def start():
        for cp in cps:
            cp.start()

    def finish():
        for cp in cps:
            cp.wait()

    return start, finish


def _pair_exchange_shapes(grads):
    return [jax.ShapeDtypeStruct(g.shape[:2] + (HALF,), g.dtype) for g in grads]


def _pair_exchange(grads):
    n = len(grads)

    def body(*refs):
        start, finish = _pair_exchange_steps(refs[:n], refs[n:2 * n], refs[2 * n:])
        start()
        finish()

    return pl.pallas_call(
        body, name="grad_pair_exchange", out_shape=_pair_exchange_shapes(grads),
        in_specs=[ANY] * n, out_specs=[ANY] * n,
        scratch_shapes=[pltpu.SemaphoreType.DMA((n,)), pltpu.SemaphoreType.DMA((n,))],
    )(*grads)


NO_FUSE = ()


def _fuse_plan(kind, items):
    n = len(items)
    if kind == "gather":
        shards = [it[0] for it in items]
        bufs = [it[3] for it in items if it[3] is not None]
        alias, b = {}, 0
        for a, it in enumerate(items):
            if it[3] is not None:
                alias[n + b] = a
                b += 1
        ranges = [(it[1], it[2]) for it in items]
        return (shards + bufs, [jax.ShapeDtypeStruct((4,) + s.shape, s.dtype) for s in shards], alias, _gather_sems(n),
                lambda i, o, s: _gather_steps(i[:n], o, s, ranges))
    if kind == "exchange":
        return list(items), _chip_exchange_shapes(items), {}, _chip_exchange_sems(n), _chip_exchange_steps
    if kind == "small":
        return (list(items), [jax.ShapeDtypeStruct((4,) + s.shape, s.dtype) for s in items], {},
                _chip_exchange_sems(n), _gather_small_steps)
    if kind == "all":
        (pack,) = items
        return ([pack], [jax.ShapeDtypeStruct((8 * pack.shape[0], pack.shape[1]), pack.dtype)], {},
                [pltpu.SemaphoreType.DMA((7,)), pltpu.SemaphoreType.DMA((7,)), pltpu.SemaphoreType.DMA((1,))],
                _allgather_all_steps)
    pair_sems = [pltpu.SemaphoreType.DMA((n,)), pltpu.SemaphoreType.DMA((n,))]
    if kind == "swap":
        return (list(items), [jax.ShapeDtypeStruct(h.shape, h.dtype) for h in items], {}, pair_sems, _pair_swap_steps)
    return list(items), _pair_exchange_shapes(items), {}, pair_sems, _pair_exchange_steps


def _fused_call(body, fuse, *, name, grid, in_specs, out_specs, out_shape, scratch_shapes, operands,
                late_relay=False):
    plans = [_fuse_plan(kind, list(items)) for kind, items in fuse if len(items)]
    n_in, n_out, n_s = len(in_specs), len(out_shape), len(scratch_shapes)
    comm = [a for p in plans for a in p[0]]
    shapes = [s for p in plans for s in p[1]]
    nc, no = len(comm), len(shapes)
    aliases, i_at, o_at = {}, n_in, n_out
    for p in plans:
        for i, o in p[2].items():
            aliases[i_at + i] = o_at + o
        i_at, o_at = i_at + len(p[0]), o_at + len(p[1])

    def wrapped(*refs):
        o0 = n_in + nc
        s0 = o0 + n_out + no
        i_at, o_at, sem_at, steps = n_in, o0 + n_out, s0 + n_s, []
        for p in plans:
            steps.append(p[4](refs[i_at:i_at + len(p[0])], refs[o_at:o_at + len(p[1])], refs[sem_at:sem_at + len(p[3])]))
            i_at, o_at, sem_at = i_at + len(p[0]), o_at + len(p[1]), sem_at + len(p[3])
        first, last = _grid_ends(grid)
        for st in steps:
            pl.when(first)(st[0])
        if not late_relay:
            for st in steps:
                for mid in st[1:-1]:
                    pl.when(last)(mid)
        body(*refs[:n_in], *refs[o0:o0 + n_out], *refs[s0:s0 + n_s])
        for st in steps:
            for step in (st[1:] if late_relay else st[-1:]):
                pl.when(last)(step)

    res = pl.pallas_call(
        wrapped if plans else body, name=name, grid=grid, in_specs=list(in_specs) + [ANY] * nc,
        out_specs=list(out_specs) + [ANY] * no, out_shape=list(out_shape) + shapes,
        scratch_shapes=list(scratch_shapes) + [s for p in plans for s in p[3]], input_output_aliases=aliases,
        compiler_params=_params(("arbitrary",) * len(grid)))(*operands, *comm)
    outs, got, results = list(res[:n_out]), list(res[n_out:]), []
    for p in plans:
        results.append(got[:len(p[1])])
        got = got[len(p[1]):]
    return outs, results


def _chip_exchange_sems(n):
    return [pltpu.SemaphoreType.DMA((n, 3))] * 2


def _chip_exchange_shapes(parts):
    return [jax.ShapeDtypeStruct((3,) + p.shape[1:], p.dtype) for p in parts]


def _chip_exchange_steps(ins, outs, sems):
    send, recv = sems
    x, y, c = _mesh_pos()
    chips = [(1 - x, y), (x, 1 - y), (1 - x, 1 - y)]
    cps = [pltpu.make_async_remote_copy(
        src_ref=ins[a].at[2 * px + py], dst_ref=outs[a].at[k], send_sem=send.at[a, k], recv_sem=recv.at[a, k],
        device_id=(px, py, c), device_id_type=MESH) for a in range(len(ins)) for k, (px, py) in enumerate(chips)]

    def start():
        for cp in cps:
            cp.start()

    def finish():
        for cp in cps:
            cp.wait()

    return start, finish


def _chip_exchange(parts):
    n = len(parts)

    def body(*refs):
        start, finish = _chip_exchange_steps(refs[:n], refs[n:2 * n], refs[2 * n:])
        start()
        finish()

    return pl.pallas_call(
        body, name="grad_chip_exchange", out_shape=_chip_exchange_shapes(parts),
        in_specs=[ANY] * n, out_specs=[ANY] * n, scratch_shapes=_chip_exchange_sems(n),
    )(*parts)


def _pair_swap(halves):
    n = len(halves)

    def body(*refs):
        start, finish = _pair_swap_steps(refs[:n], refs[n:2 * n], refs[2 * n:])
        start()
        finish()

    return pl.pallas_call(
        body, name="grad_pair_swap",
        out_shape=[jax.ShapeDtypeStruct(h.shape, h.dtype) for h in halves],
        in_specs=[ANY] * n, out_specs=[ANY] * n,
        scratch_shapes=[pltpu.SemaphoreType.DMA((n,)), pltpu.SemaphoreType.DMA((n,))],
    )(*halves)


def _allgather_all_steps(ins, outs, sems):
    send_sems, recv_sems, local_sem = sems
    x_ref, out_ref = ins[0], outs[0]
    m_per = x_ref.shape[0]
    x, y, c = _mesh_pos()
    me, sibling = (x, y, c), (x, y, 1 - c)
    chips = [(1 - x, y), (x, 1 - y), (1 - x, 1 - y)]

    def rows(px, py, pc):
        return out_ref.at[pl.ds(pl.multiple_of((4 * px + 2 * py + pc) * m_per, 8), m_per), :]

    def copy(k, block, to, src=None):
        return pltpu.make_async_remote_copy(
            src_ref=rows(*block) if src is None else src, dst_ref=rows(*block),
            send_sem=send_sems.at[k], recv_sem=recv_sems.at[k], device_id=to, device_id_type=MESH)

    mine = pltpu.make_async_copy(x_ref, rows(*me), local_sem.at[0])
    first = [copy(0, me, sibling, src=x_ref)]
    first += [copy(1 + j, me, (*chip, c), src=x_ref) for j, chip in enumerate(chips)]
    passed = [copy(4 + j, (*chip, c), sibling) for j, chip in enumerate(chips)]

    def start():
        mine.start()
        for cp in first:
            cp.start()

    def relay():
        for j, chip in enumerate(chips):
            copy(1 + j, (*chip, c), me).wait_recv()
            passed[j].start()

    def finish():
        copy(0, sibling, me).wait_recv()
        for j, chip in enumerate(chips):
            copy(4 + j, (*chip, 1 - c), me).wait_recv()
        for cp in first + passed:
            cp.wait_send()
        mine.wait()

    return start, relay, finish


def _add_pair(g, got, c_arr, splits=None):
    _, rows, _ = g.shape
    splits = splits or [rows]

    def body(c_ref, g_ref, r_ref, *o_refs):
        at = 0
        for o_ref, n in zip(o_refs, splits):
            o_ref[...] = (g_ref[:, at:at + n, :] + r_ref[:, at:at + n, :]).astype(BF16)
            at += n

    return pl.pallas_call(
        body, name="grad_add_pair", out_shape=[jax.ShapeDtypeStruct((4, n, HALF), BF16) for n in splits],
        grid_spec=pltpu.PrefetchScalarGridSpec(
            num_scalar_prefetch=1, grid=(4,),
            in_specs=[pl.BlockSpec((1, rows, HALF), lambda j, c: (j, 0, c[0])),
                      pl.BlockSpec((1, rows, HALF), lambda j, c: (j, 0, 0))],
            out_specs=[pl.BlockSpec((1, n, HALF), lambda j, c: (j, 0, 0)) for n in splits]),
        compiler_params=_params(("arbitrary",)),
    )(c_arr, g, got)


def _add_chips(g, pair_got, chip_gots, pos_arr):
    _, rows, _ = g.shape

    def body(pos_ref, g_ref, p_ref, *refs):
        o_ref, at = refs[-1], 0
        for r_ref in refs[:-1]:
            n = r_ref.shape[1]
            own = g_ref[0, at:at + n, :] + p_ref[0, at:at + n, :]
            o_ref[at:at + n, :] = ((own + r_ref[0].astype(F32)) + r_ref[1].astype(F32)) + r_ref[2].astype(F32)
            at += n

    return pl.pallas_call(
        body, name="grad_add_chips", out_shape=jax.ShapeDtypeStruct((rows, HALF), F32),
        grid_spec=pltpu.PrefetchScalarGridSpec(
            num_scalar_prefetch=1, grid=(1,),
            in_specs=[pl.BlockSpec((1, rows, HALF), lambda i, p: (p[0], 0, p[1])),
                      pl.BlockSpec((1, rows, HALF), lambda i, p: (p[0], 0, 0))]
            + [pl.BlockSpec(t.shape, lambda i, p: (0, 0, 0)) for t in chip_gots],
            out_specs=pl.BlockSpec((rows, HALF), lambda i, p: (0, 0))),
        compiler_params=_params(("arbitrary",)),
    )(pos_arr, g, pair_got, *chip_gots)


def _add_chips_many(items, fuse=NO_FUSE):
    n = len(items)
    rows = [it[0].shape[1] for it in items]
    n_got = [len(it[2]) for it in items]

    def body(*refs):
        g_refs, p_refs = refs[:n], refs[n:2 * n]
        got_refs = refs[2 * n:2 * n + sum(n_got)]
        o_refs = refs[2 * n + sum(n_got):3 * n + sum(n_got)]
        scr = refs[3 * n + sum(n_got):]
        x, y, c = _mesh_pos()
        chip = 2 * x + y
        mine = pl.ds(pl.multiple_of(c * HALF, 128), HALF)
        copies = []
        for a in range(n):
            copies.append((pltpu.make_async_copy(g_refs[a].at[chip, :, mine], scr[2 * a], scr[2 * n].at[2 * a]),
                           pltpu.make_async_copy(p_refs[a].at[chip], scr[2 * a + 1], scr[2 * n].at[2 * a + 1])))
        for cg, cp in copies:
            cg.start()
            cp.start()
        at_ref = 0
        for a in range(n):
            copies[a][0].wait()
            copies[a][1].wait()
            at = 0
            for r_ref in got_refs[at_ref:at_ref + n_got[a]]:
                k = r_ref.shape[1]
                own = scr[2 * a][at:at + k, :] + scr[2 * a + 1][at:at + k, :]
                o_refs[a][at:at + k, :] = ((own + r_ref[0].astype(F32)) + r_ref[1].astype(F32)) + r_ref[2].astype(F32)
                at += k
            at_ref += n_got[a]

    gots = [t for it in items for t in it[2]]
    outs, got = _fused_call(
        body, fuse, name="grad_add_chips_many", grid=(1,),
        in_specs=[ANY] * (2 * n) + [pl.BlockSpec(t.shape, lambda i: (0, 0, 0)) for t in gots],
        out_specs=[pl.BlockSpec((r, HALF), lambda i: (0, 0)) for r in rows],
        out_shape=[jax.ShapeDtypeStruct((r, HALF), F32) for r in rows],
        scratch_shapes=[pltpu.VMEM((r, HALF), F32) for r in rows for _ in range(2)] + [pltpu.SemaphoreType.DMA((2 * n,))],
        operands=(*[it[0] for it in items], *[it[1] for it in items], *gots))
    return outs, got


def _put_small(o_ref, name, val, row0=0):
    r0, nr, l0, nl = SMALL_AT[name]
    o_ref[r0 - row0:r0 - row0 + nr, l0:l0 + nl] = val


def _sum_examples(f, bsz):
    return functools.reduce(lambda a, b: a + b, [f(b) for b in range(bsz)])


def _pack_small_early(acc2, acc1, cacc, gvec, dcw, dgup):
    bsz = gvec.shape[0]

    def body(a2_ref, a1_ref, ca_ref, gv_ref, cw_ref, gu_ref, o_ref):
        put = functools.partial(_put_small, o_ref)

        def put_folded(row0, val, per_row):
            w = val.shape[1]
            for i in range(val.shape[0]):
                o_ref[row0 + i // per_row:row0 + i // per_row + 1, (i % per_row) * w:(i % per_row + 1) * w] = val[i:i + 1]

        over_b = lambda f: _sum_examples(f, bsz)
        o_ref[...] = jnp.zeros_like(o_ref)
        put("loss", a2_ref[0:1, :])
        put("ln2_g", a2_ref[1:2, :])
        put("ln2_b", a2_ref[2:3, :])
        put("ln1_g", a1_ref[0:1, :])
        put("ln1_b", a1_ref[1:2, :])
        put("conv_b", ca_ref[0:1, :])
        put("conv_ln_g", ca_ref[1:2, :])
        put("conv_ln_b", ca_ref[2:3, :])
        put("gate_bias", over_b(lambda b: gv_ref[b, 0:1, :]))
        put("gla_norm_g", over_b(lambda b: gv_ref[b, 1:2, 0:128] + gv_ref[b, 1:2, 128:256]))
        put_folded(SMALL_CONV_W, over_b(lambda b: cw_ref[b]), D_MODEL // D_CONV)
        put_folded(SMALL_GATE_UP, over_b(lambda b: gu_ref[b, 0:GLA_RANK, :]), D_MODEL // 256)

    return pl.pallas_call(
        body, name="pack_small_early", out_shape=jax.ShapeDtypeStruct((SMALL_EARLY_ROWS, D_MODEL), F32),
    )(acc2, acc1, cacc, gvec, dcw, dgup)


def _pack_small_late(lvec, dmeta):
    bsz = lvec.shape[0]

    def body(lv_ref, dm_ref, o_ref):
        put = functools.partial(_put_small, o_ref, row0=SMALL_EARLY_ROWS)
        o_ref[...] = jnp.zeros_like(o_ref)
        put("meta_tokens", _sum_examples(lambda b: dm_ref[b], bsz))
        put("ln_in_g", _sum_examples(lambda b: lv_ref[b, 0:1, :], bsz))
        put("ln_in_b", _sum_examples(lambda b: lv_ref[b, 1:2, :], bsz))

    return pl.pallas_call(
        body, name="pack_small_late",
        out_shape=jax.ShapeDtypeStruct((SMALL_ROWS - SMALL_EARLY_ROWS, D_MODEL), F32),
    )(lvec, dmeta)


def _sum_small(early, late):
    def body(e_ref, l_ref, o_ref, loss_ref):
        for g_ref, r0, n in ((e_ref, 0, SMALL_EARLY_ROWS), (l_ref, SMALL_EARLY_ROWS, SMALL_ROWS - SMALL_EARLY_ROWS)):
            acc = g_ref[0:n, :]
            for d in range(1, 8):
                acc = acc + g_ref[d * n:(d + 1) * n, :]
            o_ref[r0:r0 + n, :] = acc
        loss_ref[...] = jnp.sum(o_ref[0:1, :], axis=1, keepdims=True)

    return pl.pallas_call(
        body, name="sum_small",
        out_shape=[jax.ShapeDtypeStruct((SMALL_ROWS, D_MODEL), F32), jax.ShapeDtypeStruct((1, 1), F32)],
    )(early, late)


def _adamw_math(w, g, m, v):
    c1 = 1.0 - ADAM_B1 ** ADAM_STEP
    c2 = 1.0 - ADAM_B2 ** ADAM_STEP
    mn = ADAM_B1 * m + (1.0 - ADAM_B1) * g
    vn = ADAM_B2 * v + (1.0 - ADAM_B2) * (g * g)
    return -ADAM_LR * ((mn / c1) / (jnp.sqrt(vn / c2) + ADAM_EPS) + ADAM_WD * w), mn, vn


def _adamw_small(red, sharded_grads, params):
    names = list(params)
    sharded = [k for k in names if k in sharded_grads]
    n, ns = len(names), len(sharded)

    def body(*refs):
        red_ref, sg_refs, p_refs, o_refs = refs[0], refs[1:1 + ns], refs[1 + ns:1 + ns + 3 * n], refs[1 + ns + 3 * n:]
        for i, k in enumerate(names):
            w_ref, m_ref, v_ref = p_refs[3 * i:3 * i + 3]
            if k in sharded:
                g = sg_refs[sharded.index(k)][...]
            else:
                r0, nr, l0, nl = SMALL_AT[k]
                g = red_ref[r0:r0 + nr, l0:l0 + nl]
            dl, mn, vn = _adamw_math(w_ref[...], g, m_ref[...], v_ref[...])
            for o_ref, val in zip(o_refs[4 * i:4 * i + 4], (g, dl, mn, vn)):
                o_ref[...] = val

    flat = [a for k in names for a in params[k]]
    res = pl.pallas_call(
        body, name="adamw_small",
        out_shape=[jax.ShapeDtypeStruct(params[k][0].shape, F32) for k in names for _ in range(4)],
    )(red, *[sharded_grads[k] for k in sharded], *flat)
    return {k: tuple(res[4 * i:4 * i + 4]) for i, k in enumerate(names)}


def _adamw_halves(w, own, sib, m, v, fuse=NO_FUSE):
    rows, cols = w.shape
    tile = 128
    by_rows = rows % tile == 0
    per_half = HALF // tile

    def body(w_ref, own_ref, sib_ref, m_ref, v_ref, g_ref, d_ref, mo_ref, vo_ref):
        c = _mesh_pos()[2]
        own, sib = own_ref[...], sib_ref[...]
        if by_rows:
            g = jnp.concatenate([jnp.where(c == 0, own, sib), jnp.where(c == 0, sib, own)], axis=1)
        else:
            g = jnp.where(pl.program_id(0) // per_half == c, own, sib)
        g_ref[...] = g
        d_ref[...], mo_ref[...], vo_ref[...] = _adamw_math(w_ref[...], g, m_ref[...], v_ref[...])

    if by_rows:
        grid = (rows // tile,)
        full = pl.BlockSpec((tile, cols), lambda i: (i, 0))
        half = pl.BlockSpec((tile, HALF), lambda i: (i, 0))
    else:
        grid = (cols // tile,)
        full = pl.BlockSpec((rows, tile), lambda j: (0, j))
        half = pl.BlockSpec((rows, tile), lambda j: (0, j % per_half))
    return _fused_call(
        body, fuse, name="adamw_halves", grid=grid, in_specs=[full, half, half, full, full],
        out_specs=[full] * 4, out_shape=[jax.ShapeDtypeStruct(w.shape, F32)] * 4, scratch_shapes=[],
        operands=(w, own, sib, m, v))


def _ln_in_fwd(x, g, b, fuse=NO_FUSE):
    bsz, s, d = x.shape
    tp = s + HEAD
    nh = 2
    sh = s // nh
    rc = min(256, sh)

    def body(x_ref, g_ref, b_ref, s0_ref, s0b_ref):
        h = pl.program_id(1)
        gg, bb = g_ref[...], b_ref[...]

        def step(i, carry):
            src = pl.ds(pl.multiple_of(i * rc, rc), rc)
            dst = pl.ds(pl.multiple_of(HEAD + h * sh + i * rc, 64), rc)
            xh, _ = _ln_stats(x_ref[0, src, :])
            val = xh * gg + bb
            s0_ref[0, dst, :] = val
            s0b_ref[0, dst, :] = val.astype(BF16)
            return carry

        lax.fori_loop(0, sh // rc, step, 0)

    full = lambda bi, hi: (bi, 0, 0)
    (s0, s0b), got = _fused_call(
        body, fuse, name="ln_in_fwd",
        out_shape=[jax.ShapeDtypeStruct((bsz, tp, d), F32), jax.ShapeDtypeStruct((bsz, tp, d), BF16)],
        grid=(bsz, nh),
        in_specs=[pl.BlockSpec((1, sh, d), lambda bi, hi: (bi, hi, 0)),
                  pl.BlockSpec((1, d), lambda bi, hi: (0, 0)),
                  pl.BlockSpec((1, d), lambda bi, hi: (0, 0))],
        out_specs=[pl.BlockSpec((1, tp, d), full)] * 2, scratch_shapes=[], operands=(x, g, b),
        late_relay=True)
    return s0, s0b, got


def _ln_meta(s0, s0b, meta, g, b):
    bsz, tp, d = s0.shape

    def body(s0_in, s0b_in, meta_ref, g_ref, b_ref, s0_ref, s0b_ref):
        mh, _ = _ln_stats(meta_ref[...])
        mv = mh * g_ref[...] + b_ref[...]
        s0_ref[0, 0:PAD, :] = jnp.zeros((PAD, d), F32)
        s0b_ref[0, 0:PAD, :] = jnp.zeros((PAD, d), BF16)
        s0_ref[0, PAD:HEAD, :] = mv
        s0b_ref[0, PAD:HEAD, :] = mv.astype(BF16)

    head = pl.BlockSpec((1, HEAD, d), lambda bi: (bi, 0, 0))
    vec = pl.BlockSpec((1, d), lambda bi: (0, 0))
    return pl.pallas_call(
        body, name="ln_meta", out_shape=[jax.ShapeDtypeStruct(s0.shape, F32), jax.ShapeDtypeStruct(s0b.shape, BF16)],
        grid=(bsz,), in_specs=[head, head, pl.BlockSpec((N_META, d), lambda bi: (0, 0)), vec, vec],
        out_specs=[head, head], input_output_aliases={0: 0, 1: 1},
        compiler_params=_params(("parallel",)),
    )(s0, s0b, meta, g, b)


def _in_proj(s0b, w_int, fuse=NO_FUSE):
    r, d = s0b.shape
    tm = _row_tile(r)

    def body(a_ref, w_ref, o_ref):
        o_ref[...] = _nt(a_ref[...], w_ref[...])

    (u,), got = _fused_call(
        body, fuse, name="in_proj", out_shape=[jax.ShapeDtypeStruct((r, D_IN_PAD), F32)],
        grid=(r // tm,),
        in_specs=[pl.BlockSpec((tm, d), lambda i: (i, 0)), pl.BlockSpec((D_IN_PAD, d), lambda i: (0, 0))],
        out_specs=[pl.BlockSpec((tm, D_IN_PAD), lambda i: (i, 0))], scratch_shapes=[], operands=(s0b, w_int))
    return u, got


def _conv_fwd(u, conv_w, conv_b, fuse=NO_FUSE):
    bsz, tp, _ = u.shape
    nchunk = tp // CHUNK
    win = CHUNK + 32
    nct = D_CONV // 128

    def body(cv_ref, cg_ref, w_ref, cb_ref, hc_ref, h_scr, win_scr):
        h_scr[0:32, :] = jnp.zeros((32, 128), F32)
        h_scr[32:32 + tp, :] = cv_ref[0] * _sigmoid(cg_ref[0])
        cb = cb_ref[...]

        def step(n, carry):
            r0 = pl.multiple_of(n * CHUNK, CHUNK)
            win_scr[...] = h_scr[pl.ds(r0, win), :]
            acc = jnp.zeros((CHUNK, 128), F32)
            for j in range(CONV_WIDTH):
                acc = acc + w_ref[j:j + 1, :] * win_scr[2 + j:2 + j + CHUNK, :]
            hc_ref[0, pl.ds(r0, CHUNK), :] = acc + cb
            return carry

        lax.fori_loop(0, nchunk, step, 0)

    (hc,), got = _fused_call(
        body, fuse, name="conv_fwd", out_shape=[jax.ShapeDtypeStruct((bsz, tp, D_CONV), F32)],
        grid=(bsz, nct),
        in_specs=[pl.BlockSpec((1, tp, 128), lambda bi, t: (bi, 0, C_VAL // 128 + t)),
                  pl.BlockSpec((1, tp, 128), lambda bi, t: (bi, 0, C_GATE // 128 + t)),
                  pl.BlockSpec((32, 128), lambda bi, t: (0, t)),
                  pl.BlockSpec((1, 128), lambda bi, t: (0, t))],
        out_specs=[pl.BlockSpec((1, tp, 128), lambda bi, t: (bi, 0, t))],
        scratch_shapes=[pltpu.VMEM((tp + 32, 128), F32), pltpu.VMEM((win, 128), F32)],
        operands=(u, u, conv_w, conv_b))
    return hc, got


def _gla_group(nchunk):
    return 11 if nchunk % 11 == 0 else nchunk


def _bdot(a, b, ca, cb, precision=None):
    return lax.dot_general(a, b, (((ca,), (cb,)), ((0,), (0,))), preferred_element_type=F32, precision=precision)


def _bnn(a, b, **kw):
    return _bdot(a, b, 2, 1, **kw)


def _bnt(a, b, **kw):
    return _bdot(a, b, 2, 2, **kw)


def _gla_consts(nb):
    row = lax.broadcasted_iota(jnp.int32, (nb, CHUNK, CHUNK), 1)
    col = lax.broadcasted_iota(jnp.int32, (nb, CHUNK, CHUNK), 2)
    lane = lax.broadcasted_iota(jnp.int32, (1, 1, 128), 2)
    return row >= col, row <= col, [lane < GLA_DK, lane >= GLA_DK]


def _gla_group_terms(g, nb, q_ref, k_ref, gd_ref, gup_ref, gb_ref, tril):
    m = nb * CHUNK
    rows = pl.ds(pl.multiple_of(g * m, CHUNK), m)
    z = _nn(gd_ref[0, rows, :].astype(BF16), gup_ref[...]) + gb_ref[...]
    valid = g * m + lax.broadcasted_iota(jnp.int32, (m, 1), 0) >= PAD
    lg = jnp.where(valid, _log_sigmoid(z) * (1.0 / GLA_TAU), 0.0)
    bcum = _bnn(tril.astype(F32), lg.reshape(nb, CHUNK, 128), precision=lax.Precision.HIGHEST)
    blast = bcum[:, CHUNK - 1:CHUNK, :]
    eb = jnp.exp(bcum)
    enb = jnp.exp(-bcum)
    erest = jnp.exp(blast - bcum)
    q = (q_ref[0, rows, :] * Q_SCALE).reshape(nb, CHUNK, 128)
    k = k_ref[0, rows, :].reshape(nb, CHUNK, 128)
    return rows, valid, z, eb, enb, erest, jnp.exp(blast), q * eb, k * enb, k * erest


def _grid_ends(grid):
    ids = [pl.program_id(i) for i in range(len(grid))]
    first = functools.reduce(jnp.logical_and, [i == 0 for i in ids])
    last = functools.reduce(jnp.logical_and, [i == g - 1 for i, g in zip(ids, grid)])
    return first, last


def _gla_fwd(u, gup, gbias, gnorm, fuse=NO_FUSE):
    bsz, tp, _ = u.shape
    nchunk = tp // CHUNK
    nb = _gla_group(nchunk)

    def body(q_ref, k_ref, v_ref, r_ref, gd_ref, gup_ref, gb_ref, gn_ref, out_ref, o_ref, st_ref, s_scr):
        tril, _, hmask = _gla_consts(nb)
        s_scr[...] = jnp.zeros_like(s_scr)
        gn = gn_ref[...]

        def group(g, carry):
            rows, _, _, _, _, _, dec, qe, ke, kd = _gla_group_terms(g, nb, q_ref, k_ref, gd_ref, gup_ref, gb_ref, tril)
            keb, kdb = ke.astype(BF16), kd.astype(BF16)
            for h in range(2):
                cols = slice(h * GLA_DV, (h + 1) * GLA_DV)
                qh = jnp.where(hmask[h], qe, 0.0).astype(BF16)
                vh = v_ref[0, rows, cols].astype(BF16).reshape(nb, CHUNK, GLA_DV)
                a = jnp.where(tril, _bnt(qh, keb), 0.0).astype(BF16)
                st = s_scr[h]
                sts = []
                for n in range(nb):
                    st_ref[0, h, g * nb + n] = st
                    sts.append(st.astype(BF16))
                    st = dec[n] * st + _tn(vh[n], kdb[n])
                s_scr[h] = st
                o = (_bnn(a, vh) + _bnt(qh, jnp.stack(sts))).reshape(nb * CHUNK, GLA_DV)
                o_ref[0, rows, cols] = o
                rms = lax.rsqrt(jnp.mean(o * o, axis=-1, keepdims=True) + LN_EPS)
                rh = r_ref[0, rows, cols]
                out_ref[0, rows, cols] = (o * rms * gn * (rh * _sigmoid(rh))).astype(BF16)
            return carry

        lax.fori_loop(0, nchunk // nb, group, 0)

    res, got = _fused_call(
        body, fuse, name="gla_fwd",
        out_shape=[jax.ShapeDtypeStruct((bsz, tp, 512), BF16), jax.ShapeDtypeStruct((bsz, tp, 512), F32),
                   jax.ShapeDtypeStruct((bsz, GLA_HEADS, nchunk, GLA_DV, 128), F32)],
        grid=(bsz, 2),
        in_specs=[pl.BlockSpec((1, tp, 128), lambda bi, p: (bi, 0, C_Q // 128 + p)),
                  pl.BlockSpec((1, tp, 128), lambda bi, p: (bi, 0, C_K // 128 + p)),
                  pl.BlockSpec((1, tp, 256), lambda bi, p: (bi, 0, C_V // 256 + p)),
                  pl.BlockSpec((1, tp, 256), lambda bi, p: (bi, 0, C_R // 256 + p)),
                  pl.BlockSpec((1, tp, 128), lambda bi, p: (bi, 0, C_GD // 128)),
                  pl.BlockSpec((128, 128), lambda bi, p: (0, p)),
                  pl.BlockSpec((1, 128), lambda bi, p: (0, p)),
                  pl.BlockSpec((1, 128), lambda bi, p: (0, 0))],
        out_specs=[pl.BlockSpec((1, tp, 256), lambda bi, p: (bi, 0, p)),
                   pl.BlockSpec((1, tp, 256), lambda bi, p: (bi, 0, p)),
                   pl.BlockSpec((1, 2, nchunk, GLA_DV, 128), lambda bi, p: (bi, p, 0, 0, 0))],
        scratch_shapes=[pltpu.VMEM((2, GLA_DV, 128), F32)],
        operands=(u, u, u, u, u, gup, gbias, gnorm))
    return res[0], res[1], res[2], got


def _out_proj_ln1(hc, gla_out, w_out, s0, cg, cb, g1, b1, fuse=NO_FUSE):
    r, d = s0.shape
    tm = _row_tile(r)

    def body(hc_ref, a_ref, w_ref, s0_ref, cg_ref, cb_ref, g_ref, b_ref, co_ref, xh_ref, rstd_ref, s1b_ref):
        for rs in _sub_rows(tm):
            xc, _ = _ln_stats(hc_ref[rs, :])
            nv = xc * cg_ref[...] + cb_ref[...]
            co = (nv * _sigmoid(nv)).astype(BF16)
            co_ref[rs, :] = co
            mix = _nn(co, w_ref[0:D_CONV, :]) + _nn(a_ref[rs, :], w_ref[D_CONV:, :])
            xh, rstd = _ln_stats(ALPHA * s0_ref[rs, :] + mix)
            xh_ref[rs, :] = xh
            rstd_ref[rs, :] = rstd
            s1b_ref[rs, :] = (xh * g_ref[...] + b_ref[...]).astype(BF16)

    row = lambda n: pl.BlockSpec((tm, n), lambda i: (i, 0))
    vec = lambda n: pl.BlockSpec((1, n), lambda i: (0, 0))
    res, got = _fused_call(
        body, fuse, name="out_proj_ln1",
        out_shape=[jax.ShapeDtypeStruct((r, D_CONV), BF16), jax.ShapeDtypeStruct((r, d), F32),
                   jax.ShapeDtypeStruct((r, 1), F32), jax.ShapeDtypeStruct((r, d), BF16)],
        grid=(r // tm,),
        in_specs=[row(D_CONV), row(512), pl.BlockSpec((d, d), lambda i: (0, 0)), row(d),
                  vec(D_CONV), vec(D_CONV), vec(d), vec(d)],
        out_specs=[row(D_CONV), row(d), row(1), row(d)], scratch_shapes=[],
        operands=(hc, gla_out, w_out, s0, cg, cb, g1, b1))
    return res[0], res[1], res[2], res[3], got


def _ffn1(s1b, w1, fuse=NO_FUSE):
    r, d = s1b.shape
    tm = _row_tile(r)
    ns, _, wn = w1.shape

    def body(a_ref, w_ref, o_ref):
        a = a_ref[...]
        for j in range(ns):
            o_ref[:, j * wn:(j + 1) * wn] = jnp.maximum(_nn(a, w_ref[j]), 0.0).astype(BF16)

    (ra,), got = _fused_call(
        body, fuse, name="ffn1", out_shape=[jax.ShapeDtypeStruct((r, D_FF), BF16)], grid=(r // tm,),
        in_specs=[pl.BlockSpec((tm, d), lambda i: (i, 0)), pl.BlockSpec(w1.shape, lambda i: (0, 0, 0))],
        out_specs=[pl.BlockSpec((tm, D_FF), lambda i: (i, 0))], scratch_shapes=[], operands=(s1b, w1))
    return ra, got


def _ffn2_ln2_loss(ra, w2, xhat1, g1, b1, g2, b2, tgt, tp):
    r, d = xhat1.shape
    tm = _row_tile(tp)
    per = tp // tm

    def body(ra_ref, w_ref, xh1_ref, g1_ref, b1_ref, g2_ref, b2_ref, tgt_ref, dt_ref, dtb_ref, acc_ref, t_ref, sem):
        i = pl.program_id(0)
        b, j = i // per, i % per

        @pl.when(i == 0)
        def _():
            acc_ref[...] = jnp.zeros_like(acc_ref)

        head_copy = pltpu.make_async_copy(tgt_ref.at[b, pl.ds(0, tm - HEAD), :], t_ref.at[pl.ds(HEAD, tm - HEAD), :], sem)
        body_copy = pltpu.make_async_copy(
            tgt_ref.at[b, pl.ds(pl.multiple_of(jnp.maximum(j * tm - HEAD, 0), 64), tm), :], t_ref, sem)

        @pl.when(j == 0)
        def _():
            t_ref[0:HEAD, :] = jnp.zeros((HEAD, d), F32)
            head_copy.start()

        pl.when(j > 0)(body_copy.start)

        sums = [jnp.zeros((1, d), F32)] * 3
        for rs in _sub_rows(tm):
            rb = ra_ref[rs, :]
            f = _nn(rb * rb, w_ref[...])
            if rs.start == 0:
                pl.when(j == 0)(head_copy.wait)
                pl.when(j > 0)(body_copy.wait)
            s1 = xh1_ref[rs, :] * g1_ref[...] + b1_ref[...]
            xh2, rstd2 = _ln_stats(ALPHA * s1 + f)
            y = xh2 * g2_ref[...] + b2_ref[...]
            rowid = (i % per) * tm + rs.start + lax.broadcasted_iota(jnp.int32, (rs.stop - rs.start, 1), 0)
            e = jnp.where(rowid >= HEAD, y - t_ref[rs, :], 0.0)
            dy = e * (1.0 / d)
            dt2 = _ln_bwd(dy * g2_ref[...], xh2, rstd2)
            dt_ref[rs, :] = dt2
            dtb_ref[rs, :] = dt2.astype(BF16)
            sums = [sums[0] + (0.5 / d) * jnp.sum(e * e, axis=0, keepdims=True),
                    sums[1] + jnp.sum(dy * xh2, axis=0, keepdims=True), sums[2] + jnp.sum(dy, axis=0, keepdims=True)]
        for k in range(3):
            acc_ref[k:k + 1, :] += sums[k]

    row = lambda n: pl.BlockSpec((tm, n), lambda i: (i, 0))
    vec = pl.BlockSpec((1, d), lambda i: (0, 0))
    return pl.pallas_call(
        body, name="ffn2_ln2_loss",
        out_shape=[jax.ShapeDtypeStruct((r, d), F32), jax.ShapeDtypeStruct((r, d), BF16),
                   jax.ShapeDtypeStruct((8, d), F32)],
        grid=(r // tm,),
        in_specs=[row(D_FF), pl.BlockSpec((D_FF, d), lambda i: (0, 0)), row(d), vec, vec, vec, vec, ANY],
        out_specs=[row(d), row(d), pl.BlockSpec((8, d), lambda i: (0, 0))],
        scratch_shapes=[pltpu.VMEM((tm, d), F32), pltpu.SemaphoreType.DMA],
        compiler_params=_params(("arbitrary",)),
    )(ra, w2, xhat1, g1, b1, g2, b2, tgt)


def _ffn_bwd_da(dt2b, w2, ra):
    r, d = dt2b.shape
    tm = _row_tile(r)

    def body(g_ref, w_ref, ra_ref, o_ref):
        o_ref[...] = (_nt(g_ref[...], w_ref[...]) * (2.0 * ra_ref[...].astype(F32))).astype(BF16)

    return pl.pallas_call(
        body, name="ffn_bwd_da", out_shape=jax.ShapeDtypeStruct((r, D_FF), BF16),
        grid=(r // tm,),
        in_specs=[pl.BlockSpec((tm, d), lambda i: (i, 0)), pl.BlockSpec((D_FF, d), lambda i: (0, 0)),
                  pl.BlockSpec((tm, D_FF), lambda i: (i, 0))],
        out_specs=pl.BlockSpec((tm, D_FF), lambda i: (i, 0)),
        compiler_params=_params(("parallel",)),
    )(dt2b, w2, ra)


def _ffn_bwd_ln1(da, w1, dt2, xhat1, rstd1, g1):
    r, d = dt2.shape
    tm = _row_tile(r)

    def body(da_ref, w_ref, dt2_ref, xh_ref, rstd_ref, g_ref, dt_ref, dtb_ref, acc_ref):
        @pl.when(pl.program_id(0) == 0)
        def _():
            acc_ref[...] = jnp.zeros_like(acc_ref)

        sums = [jnp.zeros((1, d), F32)] * 2
        for rs in _sub_rows(tm):
            ds1 = ALPHA * dt2_ref[rs, :]
            for j in range(w1.shape[0]):
                ds1 = ds1 + _nt(da_ref[rs, j * w1.shape[2]:(j + 1) * w1.shape[2]], w_ref[j])
            xh = xh_ref[rs, :]
            dt1 = _ln_bwd(ds1 * g_ref[...], xh, rstd_ref[rs, :])
            dt_ref[rs, :] = dt1
            dtb_ref[rs, :] = dt1.astype(BF16)
            sums = [sums[0] + jnp.sum(ds1 * xh, axis=0, keepdims=True), sums[1] + jnp.sum(ds1, axis=0, keepdims=True)]
        for k in range(2):
            acc_ref[k:k + 1, :] += sums[k]

    row = lambda n: pl.BlockSpec((tm, n), lambda i: (i, 0))
    return pl.pallas_call(
        body, name="ffn_bwd_ln1",
        out_shape=[jax.ShapeDtypeStruct((r, d), F32), jax.ShapeDtypeStruct((r, d), BF16),
                   jax.ShapeDtypeStruct((8, d), F32)],
        grid=(r // tm,),
        in_specs=[row(D_FF), pl.BlockSpec(w1.shape, lambda i: (0, 0, 0)), row(d), row(d), row(1),
                  pl.BlockSpec((1, d), lambda i: (0, 0))],
        out_specs=[row(d), row(d), pl.BlockSpec((8, d), lambda i: (0, 0))],
        compiler_params=_params(("arbitrary",)),
    )(da, w1, dt2, xhat1, rstd1, g1)


def _matmul_tn(lhs, rhs, bm, square_lhs=False, name="matmul_tn", fuse=NO_FUSE):
    r, m = lhs.shape
    n = rhs.shape[1]
    tk = _reduce_tile(r, True)

    def body(a_ref, b_ref, o_ref):
        @pl.when(pl.program_id(1) == 0)
        def _():
            o_ref[...] = jnp.zeros_like(o_ref)

        a = a_ref[...]
        if square_lhs:
            a = a * a
        o_ref[...] += _tn(a, b_ref[...])

    (out,), got = _fused_call(
        body, fuse, name=name, out_shape=[jax.ShapeDtypeStruct((m, n), F32)], grid=(m // bm, r // tk),
        in_specs=[pl.BlockSpec((tk, bm), lambda i, k: (k, i)), pl.BlockSpec((tk, n), lambda i, k: (k, 0))],
        out_specs=[pl.BlockSpec((bm, n), lambda i, k: (i, 0))], scratch_shapes=[], operands=(lhs, rhs))
    return out, got


def _grad_w_ff1(s1b, da, fuse=NO_FUSE):
    r, d = s1b.shape
    wn = da.shape[1] // 4
    tk = _reduce_tile(r, True)

    def body(a_ref, b_ref, o_ref):
        @pl.when(pl.program_id(1) == 0)
        def _():
            o_ref[...] = jnp.zeros_like(o_ref)

        o_ref[0] += _tn(a_ref[...], b_ref[...])

    (out,), got = _fused_call(
        body, fuse, name="grad_w_ff1", out_shape=[jax.ShapeDtypeStruct((4, d, wn), F32)], grid=(4, r // tk),
        in_specs=[pl.BlockSpec((tk, d), lambda j, k: (k, 0)), pl.BlockSpec((tk, wn), lambda j, k: (k, j))],
        out_specs=[pl.BlockSpec((1, d, wn), lambda j, k: (j, 0, 0))], scratch_shapes=[], operands=(s1b, da))
    return out, got


def _grad_w_out(conv_of, gla_of, dt1b, fuse=NO_FUSE):
    r, n = dt1b.shape
    tk = _reduce_tile(r, True)

    def body(a_ref, b_ref, g_ref, o_ref):
        @pl.when(pl.program_id(1) == 0)
        def _():
            o_ref[...] = jnp.zeros_like(o_ref)

        @pl.when(pl.program_id(0) == 0)
        def _():
            o_ref[...] += _tn(a_ref[...], g_ref[...])

        @pl.when(pl.program_id(0) == 1)
        def _():
            o_ref[...] += _tn(b_ref[...], g_ref[...])

    lhs = pl.BlockSpec((tk, 512), lambda i, k: (k, 0))
    (out,), got = _fused_call(
        body, fuse, name="grad_w_out", out_shape=[jax.ShapeDtypeStruct((2 * 512, n), F32)], grid=(2, r // tk),
        in_specs=[lhs, lhs, pl.BlockSpec((tk, n), lambda i, k: (k, 0))],
        out_specs=[pl.BlockSpec((512, n), lambda i, k: (i, 0))], scratch_shapes=[],
        operands=(conv_of, gla_of, dt1b))
    return out, got


def _out_proj_bwd(dt1b, w_out, hc, cg, cb, fuse=NO_FUSE):
    r, d = dt1b.shape
    tm = _row_tile(r)

    def body(g_ref, w_ref, hc_ref, cg_ref, cb_ref, dhc_ref, dgla_ref, acc_ref):
        @pl.when(pl.program_id(0) == 0)
        def _():
            acc_ref[...] = jnp.zeros_like(acc_ref)

        gg = cg_ref[...]
        sums = [jnp.zeros((1, D_CONV), F32)] * 3
        for rs in _sub_rows(tm):
            dmix = _nt(g_ref[rs, :], w_ref[...])
            dgla_ref[rs, :] = dmix[:, D_CONV:]
            xh, rstd = _ln_stats(hc_ref[rs, :])
            nv = xh * gg + cb_ref[...]
            sig = _sigmoid(nv)
            dn = dmix[:, :D_CONV] * (sig * (1.0 + nv * (1.0 - sig)))
            dhc = _ln_bwd(dn * gg, xh, rstd)
            dhc_ref[rs, :] = dhc
            sums = [sums[0] + jnp.sum(dhc, axis=0, keepdims=True), sums[1] + jnp.sum(dn * xh, axis=0, keepdims=True),
                    sums[2] + jnp.sum(dn, axis=0, keepdims=True)]
        for k in range(3):
            acc_ref[k:k + 1, :] += sums[k]

    row = lambda n: pl.BlockSpec((tm, n), lambda i: (i, 0))
    vec = pl.BlockSpec((1, D_CONV), lambda i: (0, 0))
    res, got = _fused_call(
        body, fuse, name="out_proj_bwd",
        out_shape=[jax.ShapeDtypeStruct((r, D_CONV), F32), jax.ShapeDtypeStruct((r, 512), F32),
                   jax.ShapeDtypeStruct((8, D_CONV), F32)],
        grid=(r // tm,),
        in_specs=[row(d), pl.BlockSpec((d, d), lambda i: (0, 0)), row(D_CONV), vec, vec],
        out_specs=[row(D_CONV), row(512), pl.BlockSpec((8, D_CONV), lambda i: (0, 0))], scratch_shapes=[],
        operands=(dt1b, w_out, hc, cg, cb))
    return res[0], res[1], res[2], got


def _conv_bwd(dhc, u, conv_w, fuse=NO_FUSE):
    bsz, tp, _ = u.shape
    nchunk = tp // CHUNK
    win = CHUNK + 32
    nct = D_CONV // 128

    def body(dhc_ref, cv_ref, cg_ref, w_ref, dv_ref, dg_ref, dw_ref, h_scr, dhc_scr, hwin, dwin, dw_scr):
        h_scr[0:32, :] = jnp.zeros((32, 128), F32)
        h_scr[32:32 + tp, :] = cv_ref[0] * _sigmoid(cg_ref[0])
        dhc_scr[0:tp, :] = dhc_ref[0]
        dhc_scr[tp:tp + 32, :] = jnp.zeros((32, 128), F32)
        dw_scr[...] = jnp.zeros_like(dw_scr)

        def step(n, carry):
            r0 = pl.multiple_of(n * CHUNK, CHUNK)
            rows = pl.ds(r0, CHUNK)
            hwin[...] = h_scr[pl.ds(r0, win), :]
            dwin[...] = dhc_scr[pl.ds(r0, win), :]
            dcur = dwin[0:CHUNK, :]
            acc = jnp.zeros((CHUNK, 128), F32)
            for j in range(CONV_WIDTH):
                acc = acc + w_ref[j:j + 1, :] * dwin[30 - j:30 - j + CHUNK, :]
                prod = dcur * hwin[2 + j:2 + j + CHUNK, :]
                dw_scr[j * 8:(j + 1) * 8, :] += jnp.sum(prod.reshape(CHUNK // 8, 8, 128), axis=0)
            cg = cg_ref[0, rows, :]
            sig = _sigmoid(cg)
            rowid = n * CHUNK + lax.broadcasted_iota(jnp.int32, (CHUNK, 1), 0)
            dh = jnp.where(rowid >= PAD, acc, 0.0)
            dv_ref[0, rows, :] = (dh * sig).astype(BF16)
            dg_ref[0, rows, :] = (dh * cv_ref[0, rows, :] * sig * (1.0 - sig)).astype(BF16)
            return carry

        lax.fori_loop(0, nchunk, step, 0)
        dw_ref[0] = jnp.zeros((32, 128), F32)
        for j in range(CONV_WIDTH):
            dw_ref[0, j:j + 1, :] = jnp.sum(dw_scr[j * 8:(j + 1) * 8, :], axis=0, keepdims=True)

    blk = lambda off: pl.BlockSpec((1, tp, 128), lambda bi, t: (bi, 0, off // 128 + t))
    res, got = _fused_call(
        body, fuse, name="conv_bwd",
        out_shape=[jax.ShapeDtypeStruct((bsz, tp, D_CONV), BF16), jax.ShapeDtypeStruct((bsz, tp, D_CONV), BF16),
                   jax.ShapeDtypeStruct((bsz, 32, D_CONV), F32)],
        grid=(bsz, nct),
        in_specs=[blk(0), blk(C_VAL), blk(C_GATE), pl.BlockSpec((32, 128), lambda bi, t: (0, t))],
        out_specs=[blk(0), blk(0), pl.BlockSpec((1, 32, 128), lambda bi, t: (bi, 0, t))],
        scratch_shapes=[pltpu.VMEM((tp + 32, 128), F32), pltpu.VMEM((tp + 32, 128), F32),
                        pltpu.VMEM((win, 128), F32), pltpu.VMEM((win, 128), F32),
                        pltpu.VMEM((CONV_WIDTH * 8, 128), F32)],
        operands=(dhc, u, u, conv_w))
    return res[0], res[1], res[2], got


def _gla_bwd(dgla, u, o_pre, states, gup, gbias, gnorm, fuse=NO_FUSE):
    bsz, tp, _ = u.shape
    nchunk = tp // CHUNK
    nb = _gla_group(nchunk)

    def body(dy_ref, q_ref, k_ref, v_ref, r_ref, gd_ref, o_ref, st_ref, gup_ref, gb_ref, gn_ref,
             dq_ref, dk_ref, dv_ref, dr_ref, dgd_ref, dgup_ref, vec_ref, h_scr, gup_acc):
        tril, triu, hmask = _gla_consts(nb)
        h_scr[...] = jnp.zeros_like(h_scr)
        gup_acc[...] = jnp.zeros_like(gup_acc)
        gn = gn_ref[...]
        gupb = gup_ref[...]
        m = nb * CHUNK
        ngroup = nchunk // nb

        def group(i, carry):
            dbias, dgn = carry
            g = ngroup - 1 - i
            rows, valid, z, eb, enb, erest, dec, qe, ke, kd = _gla_group_terms(
                g, nb, q_ref, k_ref, gd_ref, gup_ref, gb_ref, tril)
            keb, kdb = ke.astype(BF16), kd.astype(BF16)
            dqe = jnp.zeros((nb, CHUNK, 128), F32)
            dke = jnp.zeros((nb, CHUNK, 128), F32)
            dkd = jnp.zeros((nb, CHUNK, 128), F32)
            ddec = jnp.zeros((nb, 1, 128), F32)
            for h in range(2):
                cols = slice(h * GLA_DV, (h + 1) * GLA_DV)
                o = o_ref[0, rows, cols]
                rh = r_ref[0, rows, cols]
                dy = dy_ref[0, rows, cols]
                rms = lax.rsqrt(jnp.mean(o * o, axis=-1, keepdims=True) + LN_EPS)
                nrm = o * rms
                sig = _sigmoid(rh)
                sw = rh * sig
                dr_ref[0, rows, cols] = (dy * nrm * gn * (sig * (1.0 + rh * (1.0 - sig)))).astype(BF16)
                dgn = dgn + jnp.sum(dy * nrm * sw, axis=0, keepdims=True)
                dn = dy * gn * sw
                do = rms * (dn - nrm * jnp.mean(dn * nrm, axis=-1, keepdims=True))
                dob = do.astype(BF16).reshape(nb, CHUNK, GLA_DV)
                qh = jnp.where(hmask[h], qe, 0.0).astype(BF16)
                vh = v_ref[0, rows, cols].astype(BF16).reshape(nb, CHUNK, GLA_DV)
                ht = h_scr[h]
                hts = [None] * nb
                for n in reversed(range(nb)):
                    hts[n] = ht
                    ht = dec[n] * ht + _tn(dob[n], qh[n])
                h_scr[h] = ht
                htf = jnp.stack(hts)
                htb = htf.astype(BF16)
                st = st_ref[0, h, pl.ds(g * nb, nb)]
                at = jnp.where(triu, _bnt(keb, qh), 0.0).astype(BF16)
                da = jnp.where(tril, _bnt(dob, vh), 0.0).astype(BF16)
                dat = jnp.where(triu, _bnt(vh, dob), 0.0).astype(BF16)
                dqe = dqe + jnp.where(hmask[h], _bnn(da, keb) + _bnn(dob, st.astype(BF16)), 0.0)
                dke = dke + _bnn(dat, qh)
                dv_ref[0, rows, cols] = (_bnn(at, dob) + _bnt(kdb, htb)).reshape(m, GLA_DV).astype(BF16)
                dkd = dkd + jnp.where(hmask[h], _bnn(vh, htb), 0.0)
                ddec = ddec + jnp.where(hmask[h], jnp.sum(htf * st, axis=1, keepdims=True), 0.0)
            dq_ref[0, rows, :] = (dqe * eb * Q_SCALE).reshape(m, 128).astype(BF16)
            dk_ref[0, rows, :] = (dke * enb + dkd * erest).reshape(m, 128).astype(BF16)
            db = dqe * qe - dke * ke - dkd * kd
            dblast = jnp.sum(dkd * kd, axis=1, keepdims=True) + ddec * dec
            lastrow = lax.broadcasted_iota(jnp.int32, (1, CHUNK, 1), 1) == CHUNK - 1
            db = db + jnp.where(lastrow, dblast, 0.0)
            dlg = _bnn(triu.astype(F32), db, precision=lax.Precision.HIGHEST).reshape(m, 128)
            dz = jnp.where(valid, dlg, 0.0) * (1.0 / GLA_TAU) * (1.0 - _sigmoid(z))
            dzb = dz.astype(BF16)
            dgd_ref[0, 0, rows, :] = _nt(dzb, gupb).astype(BF16)
            gup_acc[...] += _tn(gd_ref[0, rows, :].astype(BF16), dzb)
            return dbias + jnp.sum(dz, axis=0, keepdims=True), dgn

        zero = jnp.zeros((1, 128), F32)
        dbias, dgn = lax.fori_loop(0, ngroup, group, (zero, zero))
        dgup_ref[0] = gup_acc[...]
        vec_ref[0] = jnp.zeros((8, 128), F32)
        vec_ref[0, 0:1, :] = dbias
        vec_ref[0, 1:2, :] = dgn

    pair = lambda w, off: pl.BlockSpec((1, tp, w), lambda bi, p: (bi, 0, off // w + p))
    return _fused_call(
        body, fuse, name="gla_bwd",
        out_shape=[jax.ShapeDtypeStruct((bsz, tp, 256), BF16), jax.ShapeDtypeStruct((bsz, tp, 256), BF16),
                   jax.ShapeDtypeStruct((bsz, tp, 512), BF16), jax.ShapeDtypeStruct((bsz, tp, 512), BF16),
                   jax.ShapeDtypeStruct((bsz, 2, tp, 128), BF16), jax.ShapeDtypeStruct((bsz, 128, 256), F32),
                   jax.ShapeDtypeStruct((bsz, 8, 256), F32)],
        grid=(bsz, 2),
        in_specs=[pair(256, 0), pair(128, C_Q), pair(128, C_K), pair(256, C_V), pair(256, C_R),
                  pl.BlockSpec((1, tp, 128), lambda bi, p: (bi, 0, C_GD // 128)),
                  pair(256, 0),
                  pl.BlockSpec((1, 2, nchunk, GLA_DV, 128), lambda bi, p: (bi, p, 0, 0, 0)),
                  pl.BlockSpec((128, 128), lambda bi, p: (0, p)),
                  pl.BlockSpec((1, 128), lambda bi, p: (0, p)),
                  pl.BlockSpec((1, 128), lambda bi, p: (0, 0))],
        out_specs=[pair(128, 0), pair(128, 0), pair(256, 0), pair(256, 0),
                   pl.BlockSpec((1, 1, tp, 128), lambda bi, p: (bi, p, 0, 0)),
                   pl.BlockSpec((1, 128, 128), lambda bi, p: (bi, 0, p)),
                   pl.BlockSpec((1, 8, 128), lambda bi, p: (bi, 0, p))],
        scratch_shapes=[pltpu.VMEM((2, GLA_DV, 128), F32), pltpu.VMEM((128, 128), F32)],
        operands=(dgla, u, u, u, u, u, o_pre, states, gup, gbias, gnorm))


_DU_OFFSETS = (C_VAL, C_GATE, C_Q, C_K, C_V, C_R)
_DU_WIDTHS = (512, 512, 256, 256, 512, 512)


def _du_specs(tm, per, row_map):
    specs = [pl.BlockSpec((tm, w), row_map) for w in _DU_WIDTHS]
    for p in range(2):
        specs.append(pl.BlockSpec((1, 1, tm, 128), lambda *ix, p=p: (row_map(*ix)[0] // per, p, row_map(*ix)[0] % per, 0)))
    return specs


def _du_pieces(refs):
    out = [(off, ref[...]) for off, ref in zip(_DU_OFFSETS, refs[:6])]
    dgd = (refs[6][0, 0].astype(F32) + refs[7][0, 0].astype(F32)).astype(BF16)
    out.append((C_GD, dgd))
    return out


def _in_proj_bwd(pieces, dgd, w_int, dt1, tp, fuse=NO_FUSE):
    r, d = dt1.shape
    tm = _row_tile(tp)
    per = tp // tm

    def body(*refs):
        w_ref, dt_ref, o_ref = refs[8:]
        acc = ALPHA * dt_ref[...]
        for off, val in _du_pieces(refs[:8]):
            acc = acc + _nn(val, w_ref[off:off + val.shape[1], :])
        o_ref[...] = acc

    row = lambda i: (i, 0)
    (ds0,), got = _fused_call(
        body, fuse, name="in_proj_bwd", out_shape=[jax.ShapeDtypeStruct((r, d), F32)], grid=(r // tm,),
        in_specs=_du_specs(tm, per, row) + [pl.BlockSpec((D_IN_PAD, d), lambda i: (0, 0)), pl.BlockSpec((tm, d), row)],
        out_specs=[pl.BlockSpec((tm, d), row)], scratch_shapes=[], operands=(*pieces, dgd, dgd, w_int, dt1))
    return ds0, got


def _grad_w_in(pieces, dgd, s0b, tp, fuse=NO_FUSE):
    r, d = s0b.shape
    tk = _reduce_tile(tp, False)
    per = tp // tk

    def body(*refs):
        s_ref, o_ref = refs[8:]

        @pl.when(pl.program_id(0) == 0)
        def _():
            o_ref[...] = jnp.zeros_like(o_ref)

        s = s_ref[...]
        for off, val in _du_pieces(refs[:8]):
            o_ref[off:off + val.shape[1], :] += _tn(val, s)

    row = lambda k: (k, 0)
    (out,), got = _fused_call(
        body, fuse, name="grad_w_in", out_shape=[jax.ShapeDtypeStruct((D_IN_PAD, d), F32)], grid=(r // tk,),
        in_specs=_du_specs(tk, per, row) + [pl.BlockSpec((tk, d), row)],
        out_specs=[pl.BlockSpec((D_IN_PAD, d), lambda k: (0, 0))], scratch_shapes=[],
        operands=(*pieces, dgd, dgd, s0b))
    return out, got


def _ln_in_bwd(ds0, x, meta, g):
    bsz, s, d = x.shape
    tp = s + HEAD
    nh = 2
    sh = s // nh
    rc = min(256, sh)

    def body(ds_ref, x_ref, meta_ref, g_ref, gx_ref, dm_ref, vec_ref):
        h = pl.program_id(1)
        gg = g_ref[...]

        @pl.when(h == 0)
        def _():
            mh, mr = _ln_stats(meta_ref[...])
            dsm = ds_ref[0, PAD:HEAD, :]
            dm_ref[0] = _ln_bwd(dsm * gg, mh, mr)
            vec_ref[0] = jnp.zeros((8, d), F32)
            vec_ref[0, 0:1, :] = jnp.sum(dsm * mh, axis=0, keepdims=True)
            vec_ref[0, 1:2, :] = jnp.sum(dsm, axis=0, keepdims=True)

        def step(i, carry):
            sg, sb = carry
            dst = pl.ds(pl.multiple_of(i * rc, rc), rc)
            src = pl.ds(pl.multiple_of(HEAD + h * sh + i * rc, 64), rc)
            xh, rstd = _ln_stats(x_ref[0, dst, :])
            dsv = ds_ref[0, src, :]
            gx_ref[0, dst, :] = _ln_bwd(dsv * gg, xh, rstd)
            return sg + jnp.sum(dsv * xh, axis=0, keepdims=True), sb + jnp.sum(dsv, axis=0, keepdims=True)

        zero = jnp.zeros((1, d), F32)
        sg, sb = lax.fori_loop(0, sh // rc, step, (zero, zero))
        vec_ref[0, 0:1, :] += sg
        vec_ref[0, 1:2, :] += sb

    return pl.pallas_call(
        body, name="ln_in_bwd",
        out_shape=[jax.ShapeDtypeStruct((bsz, s, d), F32), jax.ShapeDtypeStruct((bsz, N_META, d), F32),
                   jax.ShapeDtypeStruct((bsz, 8, d), F32)],
        grid=(bsz, nh),
        in_specs=[pl.BlockSpec((1, tp, d), lambda bi, hi: (bi, 0, 0)),
                  pl.BlockSpec((1, sh, d), lambda bi, hi: (bi, hi, 0)),
                  pl.BlockSpec((N_META, d), lambda bi, hi: (0, 0)),
                  pl.BlockSpec((1, d), lambda bi, hi: (0, 0))],
        out_specs=[pl.BlockSpec((1, sh, d), lambda bi, hi: (bi, hi, 0)),
                   pl.BlockSpec((1, N_META, d), lambda bi, hi: (bi, 0, 0)),
                   pl.BlockSpec((1, 8, d), lambda bi, hi: (bi, 0, 0))],
        compiler_params=_params(("parallel", "arbitrary")),
    )(ds0, x, meta, g)


def _rows128(a):
    return a.reshape(-1, 128)


def kernel(x, meta_tokens, ln_in_g, ln_in_b, w_in, conv_w, conv_b, conv_ln_g, conv_ln_b, gate_up, gate_bias, gla_norm_g, w_out, ln1_g, ln1_b, w_ff1, w_ff2, ln2_g, ln2_b, loss_target, m_meta_tokens, m_ln_in_g, m_ln_in_b, m_w_in, m_conv_w, m_conv_b, m_conv_ln_g, m_conv_ln_b, m_gate_up, m_gate_bias, m_gla_norm_g, m_w_out, m_ln1_g, m_ln1_b, m_w_ff1, m_w_ff2, m_ln2_g, m_ln2_b, v_meta_tokens, v_ln_in_g, v_ln_in_b, v_w_in, v_conv_w, v_conv_b, v_conv_ln_g, v_conv_ln_b, v_gate_up, v_gate_bias, v_gla_norm_g, v_w_out, v_ln1_g, v_ln1_b, v_w_ff1, v_w_ff2, v_ln2_g, v_ln2_b):
    bsz, seq, d = x.shape
    tp = seq + HEAD
    r = bsz * tp
    xi, yi, ci = _mesh_pos()
    chip = 2 * xi + yi
    c_arr = jnp.reshape(ci, (1,)).astype(jnp.int32)
    pos_arr = jnp.stack([chip, ci]).astype(jnp.int32)

    sh_in = D_IN // 4
    shard_in = jnp.pad(w_in[0].T.astype(BF16), ((0, D_IN_PAD // 4 - sh_in), (0, 0)))
    shard_w1, shard_wout, shard_w2 = w_ff1[0].astype(BF16), w_out[0].astype(BF16), w_ff2[0].astype(BF16)
    small_w = jnp.concatenate([_rows128(meta_tokens), _rows128(conv_w[0]), _rows128(gate_up[0])], axis=0)
    ln_in_g2, ln_in_b2 = ln_in_g.reshape(1, d), ln_in_b.reshape(1, d)

    s0, s0b, ((g_int,), (g_small,)) = _ln_in_fwd(
        x, ln_in_g2, ln_in_b2, [("gather", [(shard_in, 0, D_IN_PAD // 4, None)]), ("small", [small_w])])
    (g_int,) = _place_own([g_int], [shard_in])
    (g_small,) = _place_own([g_small], [small_w])
    n_meta_rows, n_cw_rows = N_META * 256 // 128, CONV_WIDTH * 128 // 128
    meta_full = jnp.concatenate([g_small[j, :n_meta_rows].reshape(N_META, 256) for j in range(4)], axis=1)
    convw_full = jnp.concatenate(
        [g_small[j, n_meta_rows:n_meta_rows + n_cw_rows].reshape(CONV_WIDTH, 128) for j in range(4)], axis=1)
    gup_full = jnp.concatenate(
        [g_small[j, n_meta_rows + n_cw_rows:].reshape(GLA_RANK, 64) for j in range(4)], axis=1)
    convw_p = jnp.pad(convw_full, ((0, 1), (0, 0)))
    gup_p = jnp.pad(gup_full, ((0, 128 - GLA_RANK), (0, 0))).astype(BF16)
    s0, s0b = _ln_meta(s0, s0b, meta_full, ln_in_g2, ln_in_b2)
    w_int = jnp.pad(g_int[:, :sh_in].reshape(D_IN, d), ((0, D_IN_PAD - D_IN), (0, 0)))
    s0f, s0bf = s0.reshape(r, d), s0b.reshape(r, d)
    u, ((w1_buf,),) = _in_proj(s0bf, w_int, [("gather", [(shard_w1, 0, 640, None)])])
    (w1_buf,) = _place_own([w1_buf], [shard_w1])
    u3 = u.reshape(bsz, tp, D_IN_PAD)
    hc, ((w1_buf,),) = _conv_fwd(u3, convw_p, conv_b, [("gather", [(shard_w1, 640, 384, w1_buf)])])
    hc = hc.reshape(r, D_CONV)
    gla_out, o_pre, states, ((g_wout,),) = _gla_fwd(
        u3, gup_p, gate_bias, gla_norm_g, [("gather", [(shard_wout, 0, 256, None)])])
    (g_wout,) = _place_own([g_wout], [shard_wout])
    wout = g_wout.reshape(d, d)
    gla_of = gla_out.reshape(r, 512)
    conv_of, xhat1, rstd1, s1b, ((w2_buf,),) = _out_proj_ln1(
        hc, gla_of, wout, s0f, conv_ln_g, conv_ln_b, ln1_g, ln1_b, [("gather", [(shard_w2, 0, 448, None)])])
    (w2_buf,) = _place_own([w2_buf], [shard_w2])
    w1 = w1_buf
    ra, ((w2_buf,),) = _ffn1(s1b, w1, [("gather", [(shard_w2, 448, 576, w2_buf)])])
    w2 = w2_buf.reshape(D_FF, d)
    dt2, dt2b, acc2 = _ffn2_ln2_loss(ra, w2, xhat1, ln1_g, ln1_b, ln2_g, ln2_b, loss_target, tp)

    def add_pair(g, got, splits=None):
        return _add_pair(g, got, c_arr, splits)

    da = _ffn_bwd_da(dt2b, w2, ra)
    dt1, dt1b, acc1 = _ffn_bwd_ln1(da, w1, dt2, xhat1, rstd1, ln1_g)
    g_w2, _ = _matmul_tn(ra, dt2b, 1024, square_lhs=True, name="grad_w_ff2")
    big_w2 = g_w2.reshape(4, D_FF // 4, d)
    big_w1, ((pair_w2,),) = _grad_w_ff1(s1b, da, [("pair", [big_w2])])
    dhc, dgla, cacc, ((pair_w1,),) = _out_proj_bwd(dt1b, wout, hc, conv_ln_g, conv_ln_b, [("pair", [big_w1])])
    (part_w1,), (part_w2a, part_w2b) = add_pair(big_w1, pair_w1), add_pair(big_w2, pair_w2, [512, 512])
    g_wout, _ = _grad_w_out(conv_of, gla_of, dt1b)
    big_wout = g_wout.reshape(4, d // 4, d)
    dcv, dcg, dcw, ((chip_w1,), (pair_wout,)) = _conv_bwd(
        dhc.reshape(bsz, tp, D_CONV), u3, convw_p, [("exchange", [part_w1]), ("pair", [big_wout])])
    (part_wout,) = add_pair(big_wout, pair_wout)
    (dq, dk, dv, dr, dgd, dgup, gvec), ((chip_w2a,),) = _gla_bwd(
        dgla.reshape(bsz, tp, 512), u3, o_pre, states, gup_p, gate_bias, gla_norm_g, [("exchange", [part_w2a])])
    pieces = [a.reshape(r, a.shape[-1]) for a in (dcv, dcg, dq, dk, dv, dr)]
    g_wint, ((chip_w2b, chip_wout),) = _grad_w_in(
        pieces, dgd, s0bf, tp, [("exchange", [part_w2b, part_wout])])
    big_win = jnp.stack([g_wint[j * sh_in:(j + 1) * sh_in] for j in range(4)])
    (half_w1, half_w2), ((pair_win,),) = _add_chips_many(
        [(big_w1, pair_w1, [chip_w1]), (big_w2, pair_w2, [chip_w2a, chip_w2b])], [("pair", [big_win])])
    ds0, ((chip_win,), (sib_w1, sib_w2), (small_early,)) = _in_proj_bwd(
        pieces, dgd, w_int, dt1, tp,
        [("exchange", add_pair(big_win, pair_win)), ("swap", [half_w1, half_w2]),
         ("all", [_pack_small_early(acc2, acc1, cacc, gvec, dcw, dgup)])])
    grad_x, dmeta, lvec = _ln_in_bwd(ds0.reshape(bsz, tp, d), x, meta_full, ln_in_g2)

    half_win = _add_chips(big_win, pair_win, [chip_win], pos_arr)
    half_wout = _add_chips(big_wout, pair_wout, [chip_wout], pos_arr)
    sib_win, sib_wout = _pair_swap([half_win, half_wout])
    grads = {}
    weights = dict(meta_tokens=meta_tokens, ln_in_g=ln_in_g, ln_in_b=ln_in_b, w_in=w_in, conv_w=conv_w, conv_b=conv_b,
                   conv_ln_g=conv_ln_g, conv_ln_b=conv_ln_b, gate_up=gate_up, gate_bias=gate_bias,
                   gla_norm_g=gla_norm_g, w_out=w_out, ln1_g=ln1_g, ln1_b=ln1_b, w_ff1=w_ff1, w_ff2=w_ff2,
                   ln2_g=ln2_g, ln2_b=ln2_b)
    moms = dict(meta_tokens=(m_meta_tokens, v_meta_tokens), ln_in_g=(m_ln_in_g, v_ln_in_g),
                ln_in_b=(m_ln_in_b, v_ln_in_b), w_in=(m_w_in, v_w_in), conv_w=(m_conv_w, v_conv_w),
                conv_b=(m_conv_b, v_conv_b), conv_ln_g=(m_conv_ln_g, v_conv_ln_g),
                conv_ln_b=(m_conv_ln_b, v_conv_ln_b), gate_up=(m_gate_up, v_gate_up),
                gate_bias=(m_gate_bias, v_gate_bias), gla_norm_g=(m_gla_norm_g, v_gla_norm_g),
                w_out=(m_w_out, v_w_out), ln1_g=(m_ln1_g, v_ln1_g), ln1_b=(m_ln1_b, v_ln1_b),
                w_ff1=(m_w_ff1, v_w_ff1), w_ff2=(m_w_ff2, v_w_ff2), ln2_g=(m_ln2_g, v_ln2_g),
                ln2_b=(m_ln2_b, v_ln2_b))
    names = list(weights)
    big_names = ("w_in", "w_out", "w_ff1", "w_ff2")
    delta, new_m, new_v = {}, {}, {}
    def adamw_big(k, own, sib, fuse=NO_FUSE):
        to2d = (lambda a: a[0].T) if k == "w_in" else (lambda a: a[0])
        back = (lambda a: a.T[None]) if k == "w_in" else (lambda a: a[None])
        res, got = _adamw_halves(to2d(weights[k]), own, sib, to2d(moms[k][0]), to2d(moms[k][1]), fuse)
        grads[k], delta[k], new_m[k], new_v[k] = [back(a) for a in res]
        return got

    ((small_late,),) = adamw_big("w_in", half_win, sib_win, [("all", [_pack_small_late(lvec, dmeta)])])
    adamw_big("w_ff1", half_w1, sib_w1)
    adamw_big("w_ff2", half_w2, sib_w2)
    adamw_big("w_out", half_wout, sib_wout)
    red, loss = _sum_small(small_early, small_late)
    small_names = [k for k in names if k not in big_names]
    two = lambda a: a.reshape(-1, a.shape[-1])

    def my_cols(full, rows, width):
        return lax.dynamic_slice(full, (0, chip * width), (rows, width))

    sharded = {
        "meta_tokens": my_cols(red[SMALL_AT["meta_tokens"][0]:SMALL_AT["meta_tokens"][0] + N_META], N_META, 256),
        "conv_w": my_cols(red[SMALL_CONV_W:SMALL_CONV_W + 16].reshape(32, D_CONV), CONV_WIDTH, 128),
        "gate_up": my_cols(red[SMALL_GATE_UP:SMALL_GATE_UP + 4].reshape(GLA_RANK, 256), GLA_RANK, 64)}
    upd = _adamw_small(red, sharded, {k: (two(weights[k]), two(moms[k][0]), two(moms[k][1])) for k in small_names})
    for k in small_names:
        shp = weights[k].shape
        grads[k], delta[k], new_m[k], new_v[k] = [a.reshape(shp) for a in upd[k]]
    loss = loss.reshape(())

    return (loss, grad_x, *[grads[k] for k in names], *[delta[k] for k in names],
            *[new_m[k] for k in names], *[new_v[k] for k in names])
```

```python
import functools

import jax
import jax.numpy as jnp
from jax import lax
from jax.experimental import pallas as pl
from jax.experimental.pallas import tpu as pltpu

F32 = jnp.float32
BF16 = jnp.bfloat16

D_MODEL = 1024
N_META = 16
D_CONV = 512
CONV_WIDTH = 31
GLA_HEADS = 4
GLA_DV = 128
GLA_DK = 64
GLA_RANK = 16
GLA_TAU = 16.0
CHUNK = 64
D_FF = 4096
LN_EPS = 1e-5
ALPHA = 2.0 ** 0.25
D_IN = 2576
D_IN_PAD = 2688
PAD = CHUNK - N_META
HEAD = PAD + N_META
Q_SCALE = GLA_DK ** -0.5
ADAM_LR, ADAM_B1, ADAM_B2, ADAM_EPS, ADAM_WD, ADAM_STEP = 0.001, 0.9, 0.999, 1e-08, 0.01, 10
HALF = D_MODEL // 2
VMEM_LIMIT = 56 * 1024 * 1024
MESH = pl.DeviceIdType.MESH

C_VAL, C_GATE, C_Q, C_K, C_V, C_R, C_GD = 0, 512, 1024, 1280, 1536, 2048, 2560

SMALL_AT = {"loss": (0, 1, 0, 1024), "ln1_g": (1, 1, 0, 1024), "ln1_b": (2, 1, 0, 1024), "ln2_g": (3, 1, 0, 1024),
            "ln2_b": (4, 1, 0, 1024), "conv_b": (5, 1, 0, 512), "conv_ln_g": (5, 1, 512, 512),
            "conv_ln_b": (6, 1, 0, 512), "gate_bias": (6, 1, 512, 256), "gla_norm_g": (6, 1, 768, 128),
            "meta_tokens": (32, 16, 0, 1024), "ln_in_g": (48, 1, 0, 1024), "ln_in_b": (49, 1, 0, 1024)}
SMALL_CONV_W, SMALL_GATE_UP = 8, 24
SMALL_EARLY_ROWS, SMALL_ROWS = 32, 56


def _params(sem=None, **kw):
    return pltpu.CompilerParams(dimension_semantics=sem, vmem_limit_bytes=VMEM_LIMIT, **kw)


def _row_tile(tp):
    for t in (704, 352, 192, 64):
        if tp % t == 0:
            return t
    raise ValueError(tp)


def _reduce_tile(tp, big):
    for t in ((2112, 1056, 704) if big else (1056, 704)) + (352, 192, 64):
        if tp % t == 0:
            return t
    raise ValueError(tp)


def _sub_rows(tm):
    return [slice(0, tm)]


def _dot(a, b, dims, precision=None):
    return lax.dot_general(a, b, (dims, ((), ())), preferred_element_type=F32, precision=precision)


def _nn(a, b, **kw):
    return _dot(a, b, ((1,), (0,)), **kw)


def _nt(a, b, **kw):
    return _dot(a, b, ((1,), (1,)), **kw)


def _tn(a, b, **kw):
    return _dot(a, b, ((0,), (0,)), **kw)


def _sigmoid(x):
    return 1.0 / (1.0 + jnp.exp(-x))


def _log_sigmoid(z):
    return jnp.minimum(z, 0.0) - jnp.log(1.0 + jnp.exp(-jnp.abs(z)))


def _ln_stats(t):
    mu = jnp.mean(t, axis=-1, keepdims=True)
    d = t - mu
    var = jnp.mean(d * d, axis=-1, keepdims=True)
    rstd = lax.rsqrt(var + LN_EPS)
    return d * rstd, rstd


def _ln_bwd(dxhat, xhat, rstd):
    m1 = jnp.mean(dxhat, axis=-1, keepdims=True)
    m2 = jnp.mean(dxhat * xhat, axis=-1, keepdims=True)
    return rstd * (dxhat - m1 - xhat * m2)


def _mesh_pos():
    return lax.axis_index("x"), lax.axis_index("y"), lax.axis_index("c")


ANY = pl.BlockSpec(memory_space=pl.ANY)


def _gather_sems(n):
    return [pltpu.SemaphoreType.DMA((n, 3))] * 4


def _gather_steps(ins, outs, sems, ranges=None):
    n = len(ins)
    send, recv, fsend, frecv = sems
    ranges = ranges or [(0, ref.shape[0]) for ref in ins]
    x, y, c = _mesh_pos()
    me = 2 * x + y
    sibling = (x, y, 1 - c)
    chips = [(1 - x, y), (x, 1 - y), (1 - x, 1 - y)]
    chip_idx = [2 * px + py for px, py in chips]
    mine = [pl.ds(pl.multiple_of(r0 + c * (nr // 2), 16), nr // 2) for r0, nr in ranges]
    other = [pl.ds(pl.multiple_of(r0 + (1 - c) * (nr // 2), 16), nr // 2) for r0, nr in ranges]
    pairs = [(a, k) for a in range(n) for k in range(3)]

    def ici(a, k, slab):
        return pltpu.make_async_remote_copy(
            src_ref=ins[a].at[mine[a], :], dst_ref=outs[a].at[slab, mine[a], :],
            send_sem=send.at[a, k], recv_sem=recv.at[a, k], device_id=(*chips[k], c), device_id_type=MESH)

    def forward(a, k, rows):
        blk = outs[a].at[chip_idx[k], rows[a], :]
        return pltpu.make_async_remote_copy(
            src_ref=blk, dst_ref=blk, send_sem=fsend.at[a, k], recv_sem=frecv.at[a, k],
            device_id=sibling, device_id_type=MESH)

    def start():
        for a, k in pairs:
            ici(a, k, me).start()

    def relay():
        for a, k in pairs:
            ici(a, k, chip_idx[k]).wait_recv()
            forward(a, k, mine).start()

    def finish():
        for a, k in pairs:
            forward(a, k, other).wait_recv()
        for a, k in pairs:
            ici(a, k, me).wait_send()
            forward(a, k, mine).wait_send()

    return start, relay, finish


def _place_own(gathered, shards):
    chip = 2 * lax.axis_index("x") + lax.axis_index("y")
    return [lax.dynamic_update_slice(g, s[None], (chip, 0, 0)) for g, s in zip(gathered, shards)]


def _gather_small_steps(ins, outs, sems):
    send, recv = sems
    x, y, c = _mesh_pos()
    chips = [(1 - x, y), (x, 1 - y), (1 - x, 1 - y)]

    def copy(a, k, slot):
        return pltpu.make_async_remote_copy(
            src_ref=ins[a], dst_ref=outs[a].at[slot], send_sem=send.at[a, k], recv_sem=recv.at[a, k],
            device_id=(*chips[k], c), device_id_type=MESH)

    def start():
        for a in range(len(ins)):
            for k in range(3):
                copy(a, k, 2 * x + y).start()

    def finish():
        for a in range(len(ins)):
            for k, (px, py) in enumerate(chips):
                copy(a, k, 2 * px + py).wait_recv()
            for k in range(3):
                copy(a, k, 2 * x + y).wait_send()

    return start, finish


def _pair_swap_steps(ins, outs, sems):
    send, recv = sems
    x, y, c = _mesh_pos()
    cps = [pltpu.make_async_remote_copy(
        src_ref=ins[a], dst_ref=outs[a], send_sem=send.at[a], recv_sem=recv.at[a],
        device_id=(x, y, 1 - c), device_id_type=MESH) for a in range(len(ins))]

    def start():
        for cp in cps:
            cp.start()

    def finish():
        for cp in cps:
            cp.wait()

    return start, finish


def _pair_exchange_steps(ins, outs, sems):
    send, recv = sems
    x, y, c = _mesh_pos()
    other = pl.ds(pl.multiple_of((1 - c) * HALF, 128), HALF)
    cps = [pltpu.make_async_remote_copy(
        src_ref=ins[a].at[:, :, other], dst_ref=outs[a], send_sem=send.at[a], recv_sem=recv.at[a],
        device_id=(x, y, 1 - c), device_id_type=MESH) for a in range(len(ins))]

    def start():
        for cp in cps:
            cp.start()

    def finish():
        for cp in cps:
            cp.wait()

    return start, finish


def _pair_exchange_shapes(grads):
    return [jax.ShapeDtypeStruct(g.shape[:2] + (HALF,), g.dtype) for g in grads]


def _pair_exchange(grads):
    n = len(grads)

    def body(*refs):
        start, finish = _pair_exchange_steps(refs[:n], refs[n:2 * n], refs[2 * n:])
        start()
        finish()

    return pl.pallas_call(
        body, name="grad_pair_exchange", out_shape=_pair_exchange_shapes(grads),
        in_specs=[ANY] * n, out_specs=[ANY] * n,
        scratch_shapes=[pltpu.SemaphoreType.DMA((n,)), pltpu.SemaphoreType.DMA((n,))],
    )(*grads)


NO_FUSE = ()


def _fuse_plan(kind, items):
    n = len(items)
    if kind == "gather":
        shards = [it[0] for it in items]
        bufs = [it[3] for it in items if it[3] is not None]
        alias, b = {}, 0
        for a, it in enumerate(items):
            if it[3] is not None:
                alias[n + b] = a
                b += 1
        ranges = [(it[1], it[2]) for it in items]
        return (shards + bufs, [jax.ShapeDtypeStruct((4,) + s.shape, s.dtype) for s in shards], alias, _gather_sems(n),
                lambda i, o, s: _gather_steps(i[:n], o, s, ranges))
    if kind == "exchange":
        return list(items), _chip_exchange_shapes(items), {}, _chip_exchange_sems(n), _chip_exchange_steps
    if kind == "small":
        return (list(items), [jax.ShapeDtypeStruct((4,) + s.shape, s.dtype) for s in items], {},
                _chip_exchange_sems(n), _gather_small_steps)
    if kind == "all":
        (pack,) = items
        return ([pack], [jax.ShapeDtypeStruct((8 * pack.shape[0], pack.shape[1]), pack.dtype)], {},
                [pltpu.SemaphoreType.DMA((7,)), pltpu.SemaphoreType.DMA((7,)), pltpu.SemaphoreType.DMA((1,))],
                _allgather_all_steps)
    pair_sems = [pltpu.SemaphoreType.DMA((n,)), pltpu.SemaphoreType.DMA((n,))]
    if kind == "swap":
        return (list(items), [jax.ShapeDtypeStruct(h.shape, h.dtype) for h in items], {}, pair_sems, _pair_swap_steps)
    return list(items), _pair_exchange_shapes(items), {}, pair_sems, _pair_exchange_steps


def _fused_call(body, fuse, *, name, grid, in_specs, out_specs, out_shape, scratch_shapes, operands,
                late_relay=False):
    plans = [_fuse_plan(kind, list(items)) for kind, items in fuse if len(items)]
    n_in, n_out, n_s = len(in_specs), len(out_shape), len(scratch_shapes)
    comm = [a for p in plans for a in p[0]]
    shapes = [s for p in plans for s in p[1]]
    nc, no = len(comm), len(shapes)
    aliases, i_at, o_at = {}, n_in, n_out
    for p in plans:
        for i, o in p[2].items():
            aliases[i_at + i] = o_at + o
        i_at, o_at = i_at + len(p[0]), o_at + len(p[1])

    def wrapped(*refs):
        o0 = n_in + nc
        s0 = o0 + n_out + no
        i_at, o_at, sem_at, steps = n_in, o0 + n_out, s0 + n_s, []
        for p in plans:
            steps.append(p[4](refs[i_at:i_at + len(p[0])], refs[o_at:o_at + len(p[1])], refs[sem_at:sem_at + len(p[3])]))
            i_at, o_at, sem_at = i_at + len(p[0]), o_at + len(p[1]), sem_at + len(p[3])
        first, last = _grid_ends(grid)
        for st in steps:
            pl.when(first)(st[0])
        if not late_relay:
            for st in steps:
                for mid in st[1:-1]:
                    pl.when(last)(mid)
        body(*refs[:n_in], *refs[o0:o0 + n_out], *refs[s0:s0 + n_s])
        for st in steps:
            for step in (st[1:] if late_relay else st[-1:]):
                pl.when(last)(step)

    res = pl.pallas_call(
        wrapped if plans else body, name=name, grid=grid, in_specs=list(in_specs) + [ANY] * nc,
        out_specs=list(out_specs) + [ANY] * no, out_shape=list(out_shape) + shapes,
        scratch_shapes=list(scratch_shapes) + [s for p in plans for s in p[3]], input_output_aliases=aliases,
        compiler_params=_params(("arbitrary",) * len(grid)))(*operands, *comm)
    outs, got, results = list(res[:n_out]), list(res[n_out:]), []
    for p in plans:
        results.append(got[:len(p[1])])
        got = got[len(p[1]):]
    return outs, results


def _chip_exchange_sems(n):
    return [pltpu.SemaphoreType.DMA((n, 3))] * 2


def _chip_exchange_shapes(parts):
    return [jax.ShapeDtypeStruct((3,) + p.shape[1:], p.dtype) for p in parts]


def _chip_exchange_steps(ins, outs, sems):
    send, recv = sems
    x, y, c = _mesh_pos()
    chips = [(1 - x, y), (x, 1 - y), (1 - x, 1 - y)]
    cps = [pltpu.make_async_remote_copy(
        src_ref=ins[a].at[2 * px + py], dst_ref=outs[a].at[k], send_sem=send.at[a, k], recv_sem=recv.at[a, k],
        device_id=(px, py, c), device_id_type=MESH) for a in range(len(ins)) for k, (px, py) in enumerate(chips)]

    def start():
        for cp in cps:
            cp.start()

    def finish():
        for cp in cps:
            cp.wait()

    return start, finish


def _chip_exchange(parts):
    n = len(parts)

    def body(*refs):
        start, finish = _chip_exchange_steps(refs[:n], refs[n:2 * n], refs[2 * n:])
        start()
        finish()

    return pl.pallas_call(
        body, name="grad_chip_exchange", out_shape=_chip_exchange_shapes(parts),
        in_specs=[ANY] * n, out_specs=[ANY] * n, scratch_shapes=_chip_exchange_sems(n),
    )(*parts)


def _pair_swap(halves):
    n = len(halves)

    def body(*refs):
        start, finish = _pair_swap_steps(refs[:n], refs[n:2 * n], refs[2 * n:])
        start()
        finish()

    return pl.pallas_call(
        body, name="grad_pair_swap",
        out_shape=[jax.ShapeDtypeStruct(h.shape, h.dtype) for h in halves],
        in_specs=[ANY] * n, out_specs=[ANY] * n,
        scratch_shapes=[pltpu.SemaphoreType.DMA((n,)), pltpu.SemaphoreType.DMA((n,))],
    )(*halves)


def _allgather_all_steps(ins, outs, sems):
    send_sems, recv_sems, local_sem = sems
    x_ref, out_ref = ins[0], outs[0]
    m_per = x_ref.shape[0]
    x, y, c = _mesh_pos()
    me, sibling = (x, y, c), (x, y, 1 - c)
    chips = [(1 - x, y), (x, 1 - y), (1 - x, 1 - y)]

    def rows(px, py, pc):
        return out_ref.at[pl.ds(pl.multiple_of((4 * px + 2 * py + pc) * m_per, 8), m_per), :]

    def copy(k, block, to, src=None):
        return pltpu.make_async_remote_copy(
            src_ref=rows(*block) if src is None else src, dst_ref=rows(*block),
            send_sem=send_sems.at[k], recv_sem=recv_sems.at[k], device_id=to, device_id_type=MESH)

    mine = pltpu.make_async_copy(x_ref, rows(*me), local_sem.at[0])
    first = [copy(0, me, sibling, src=x_ref)]
    first += [copy(1 + j, me, (*chip, c), src=x_ref) for j, chip in enumerate(chips)]
    passed = [copy(4 + j, (*chip, c), sibling) for j, chip in enumerate(chips)]

    def start():
        mine.start()
        for cp in first:
            cp.start()

    def relay():
        for j, chip in enumerate(chips):
            copy(1 + j, (*chip, c), me).wait_recv()
            passed[j].start()

    def finish():
        copy(0, sibling, me).wait_recv()
        for j, chip in enumerate(chips):
            copy(4 + j, (*chip, 1 - c), me).wait_recv()
        for cp in first + passed:
            cp.wait_send()
        mine.wait()

    return start, relay, finish


def _add_pair(g, got, c_arr, splits=None):
    _, rows, _ = g.shape
    splits = splits or [rows]

    def body(c_ref, g_ref, r_ref, *o_refs):
        at = 0
        for o_ref, n in zip(o_refs, splits):
            o_ref[...] = (g_ref[:, at:at + n, :] + r_ref[:, at:at + n, :]).astype(BF16)
            at += n

    return pl.pallas_call(
        body, name="grad_add_pair", out_shape=[jax.ShapeDtypeStruct((4, n, HALF), BF16) for n in splits],
        grid_spec=pltpu.PrefetchScalarGridSpec(
            num_scalar_prefetch=1, grid=(4,),
            in_specs=[pl.BlockSpec((1, rows, HALF), lambda j, c: (j, 0, c[0])),
                      pl.BlockSpec((1, rows, HALF), lambda j, c: (j, 0, 0))],
            out_specs=[pl.BlockSpec((1, n, HALF), lambda j, c: (j, 0, 0)) for n in splits]),
        compiler_params=_params(("arbitrary",)),
    )(c_arr, g, got)


def _add_chips(g, pair_got, chip_gots, pos_arr):
    _, rows, _ = g.shape

    def body(pos_ref, g_ref, p_ref, *refs):
        o_ref, at = refs[-1], 0
        for r_ref in refs[:-1]:
            n = r_ref.shape[1]
            own = g_ref[0, at:at + n, :] + p_ref[0, at:at + n, :]
            o_ref[at:at + n, :] = ((own + r_ref[0].astype(F32)) + r_ref[1].astype(F32)) + r_ref[2].astype(F32)
            at += n

    return pl.pallas_call(
        body, name="grad_add_chips", out_shape=jax.ShapeDtypeStruct((rows, HALF), F32),
        grid_spec=pltpu.PrefetchScalarGridSpec(
            num_scalar_prefetch=1, grid=(1,),
            in_specs=[pl.BlockSpec((1, rows, HALF), lambda i, p: (p[0], 0, p[1])),
                      pl.BlockSpec((1, rows, HALF), lambda i, p: (p[0], 0, 0))]
            + [pl.BlockSpec(t.shape, lambda i, p: (0, 0, 0)) for t in chip_gots],
            out_specs=pl.BlockSpec((rows, HALF), lambda i, p: (0, 0))),
        compiler_params=_params(("arbitrary",)),
    )(pos_arr, g, pair_got, *chip_gots)


def _add_chips_many(items, fuse=NO_FUSE):
    n = len(items)
    rows = [it[0].shape[1] for it in items]
    n_got = [len(it[2]) for it in items]

    def body(*refs):
        g_refs, p_refs = refs[:n], refs[n:2 * n]
        got_refs = refs[2 * n:2 * n + sum(n_got)]
        o_refs = refs[2 * n + sum(n_got):3 * n + sum(n_got)]
        scr = refs[3 * n + sum(n_got):]
        x, y, c = _mesh_pos()
        chip = 2 * x + y
        mine = pl.ds(pl.multiple_of(c * HALF, 128), HALF)
        copies = []
        for a in range(n):
            copies.append((pltpu.make_async_copy(g_refs[a].at[chip, :, mine], scr[2 * a], scr[2 * n].at[2 * a]),
                           pltpu.make_async_copy(p_refs[a].at[chip], scr[2 * a + 1], scr[2 * n].at[2 * a + 1])))
        for cg, cp in copies:
            cg.start()
            cp.start()
        at_ref = 0
        for a in range(n):
            copies[a][0].wait()
            copies[a][1].wait()
            at = 0
            for r_ref in got_refs[at_ref:at_ref + n_got[a]]:
                k = r_ref.shape[1]
                own = scr[2 * a][at:at + k, :] + scr[2 * a + 1][at:at + k, :]
                o_refs[a][at:at + k, :] = ((own + r_ref[0].astype(F32)) + r_ref[1].astype(F32)) + r_ref[2].astype(F32)
                at += k
            at_ref += n_got[a]

    gots = [t for it in items for t in it[2]]
    outs, got = _fused_call(
        body, fuse, name="grad_add_chips_many", grid=(1,),
        in_specs=[ANY] * (2 * n) + [pl.BlockSpec(t.shape, lambda i: (0, 0, 0)) for t in gots],
        out_specs=[pl.BlockSpec((r, HALF), lambda i: (0, 0)) for r in rows],
        out_shape=[jax.ShapeDtypeStruct((r, HALF), F32) for r in rows],
        scratch_shapes=[pltpu.VMEM((r, HALF), F32) for r in rows for _ in range(2)] + [pltpu.SemaphoreType.DMA((2 * n,))],
        operands=(*[it[0] for it in items], *[it[1] for it in items], *gots))
    return outs, got


def _put_small(o_ref, name, val, row0=0):
    r0, nr, l0, nl = SMALL_AT[name]
    o_ref[r0 - row0:r0 - row0 + nr, l0:l0 + nl] = val


def _sum_examples(f, bsz):
    return functools.reduce(lambda a, b: a + b, [f(b) for b in range(bsz)])


def _pack_small_early(acc2, acc1, cacc, gvec, dcw, dgup):
    bsz = gvec.shape[0]

    def body(a2_ref, a1_ref, ca_ref, gv_ref, cw_ref, gu_ref, o_ref):
        put = functools.partial(_put_small, o_ref)

        def put_folded(row0, val, per_row):
            w = val.shape[1]
            for i in range(val.shape[0]):
                o_ref[row0 + i // per_row:row0 + i // per_row + 1, (i % per_row) * w:(i % per_row + 1) * w] = val[i:i + 1]

        over_b = lambda f: _sum_examples(f, bsz)
        o_ref[...] = jnp.zeros_like(o_ref)
        put("loss", a2_ref[0:1, :])
        put("ln2_g", a2_ref[1:2, :])
        put("ln2_b", a2_ref[2:3, :])
        put("ln1_g", a1_ref[0:1, :])
        put("ln1_b", a1_ref[1:2, :])
        put("conv_b", ca_ref[0:1, :])
        put("conv_ln_g", ca_ref[1:2, :])
        put("conv_ln_b", ca_ref[2:3, :])
        put("gate_bias", over_b(lambda b: gv_ref[b, 0:1, :]))
        put("gla_norm_g", over_b(lambda b: gv_ref[b, 1:2, 0:128] + gv_ref[b, 1:2, 128:256]))
        put_folded(SMALL_CONV_W, over_b(lambda b: cw_ref[b]), D_MODEL // D_CONV)
        put_folded(SMALL_GATE_UP, over_b(lambda b: gu_ref[b, 0:GLA_RANK, :]), D_MODEL // 256)

    return pl.pallas_call(
        body, name="pack_small_early", out_shape=jax.ShapeDtypeStruct((SMALL_EARLY_ROWS, D_MODEL), F32),
    )(acc2, acc1, cacc, gvec, dcw, dgup)


def _pack_small_late(lvec, dmeta):
    bsz = lvec.shape[0]

    def body(lv_ref, dm_ref, o_ref):
        put = functools.partial(_put_small, o_ref, row0=SMALL_EARLY_ROWS)
        o_ref[...] = jnp.zeros_like(o_ref)
        put("meta_tokens", _sum_examples(lambda b: dm_ref[b], bsz))
        put("ln_in_g", _sum_examples(lambda b: lv_ref[b, 0:1, :], bsz))
        put("ln_in_b", _sum_examples(lambda b: lv_ref[b, 1:2, :], bsz))

    return pl.pallas_call(
        body, name="pack_small_late",
        out_shape=jax.ShapeDtypeStruct((SMALL_ROWS - SMALL_EARLY_ROWS, D_MODEL), F32),
    )(lvec, dmeta)


def _sum_small(early, late):
    def body(e_ref, l_ref, o_ref, loss_ref):
        for g_ref, r0, n in ((e_ref, 0, SMALL_EARLY_ROWS), (l_ref, SMALL_EARLY_ROWS, SMALL_ROWS - SMALL_EARLY_ROWS)):
            acc = g_ref[0:n, :]
            for d in range(1, 8):
                acc = acc + g_ref[d * n:(d + 1) * n, :]
            o_ref[r0:r0 + n, :] = acc
        loss_ref[...] = jnp.sum(o_ref[0:1, :], axis=1, keepdims=True)

    return pl.pallas_call(
        body, name="sum_small",
        out_shape=[jax.ShapeDtypeStruct((SMALL_ROWS, D_MODEL), F32), jax.ShapeDtypeStruct((1, 1), F32)],
    )(early, late)


def _adamw_math(w, g, m, v):
    c1 = 1.0 - ADAM_B1 ** ADAM_STEP
    c2 = 1.0 - ADAM_B2 ** ADAM_STEP
    mn = ADAM_B1 * m + (1.0 - ADAM_B1) * g
    vn = ADAM_B2 * v + (1.0 - ADAM_B2) * (g * g)
    return -ADAM_LR * ((mn / c1) / (jnp.sqrt(vn / c2) + ADAM_EPS) + ADAM_WD * w), mn, vn


def _adamw_small(red, sharded_grads, params):
    names = list(params)
    sharded = [k for k in names if k in sharded_grads]
    n, ns = len(names), len(sharded)

    def body(*refs):
        red_ref, sg_refs, p_refs, o_refs = refs[0], refs[1:1 + ns], refs[1 + ns:1 + ns + 3 * n], refs[1 + ns + 3 * n:]
        for i, k in enumerate(names):
            w_ref, m_ref, v_ref = p_refs[3 * i:3 * i + 3]
            if k in sharded:
                g = sg_refs[sharded.index(k)][...]
            else:
                r0, nr, l0, nl = SMALL_AT[k]
                g = red_ref[r0:r0 + nr, l0:l0 + nl]
            dl, mn, vn = _adamw_math(w_ref[...], g, m_ref[...], v_ref[...])
            for o_ref, val in zip(o_refs[4 * i:4 * i + 4], (g, dl, mn, vn)):
                o_ref[...] = val

    flat = [a for k in names for a in params[k]]
    res = pl.pallas_call(
        body, name="adamw_small",
        out_shape=[jax.ShapeDtypeStruct(params[k][0].shape, F32) for k in names for _ in range(4)],
    )(red, *[sharded_grads[k] for k in sharded], *flat)
    return {k: tuple(res[4 * i:4 * i + 4]) for i, k in enumerate(names)}


def _adamw_halves(w, own, sib, m, v, fuse=NO_FUSE):
    rows, cols = w.shape
    tile = 128
    by_rows = rows % tile == 0
    per_half = HALF // tile

    def body(w_ref, own_ref, sib_ref, m_ref, v_ref, g_ref, d_ref, mo_ref, vo_ref):
        c = _mesh_pos()[2]
        own, sib = own_ref[...], sib_ref[...]
        if by_rows:
            g = jnp.concatenate([jnp.where(c == 0, own, sib), jnp.where(c == 0, sib, own)], axis=1)
        else:
            g = jnp.where(pl.program_id(0) // per_half == c, own, sib)
        g_ref[...] = g
        d_ref[...], mo_ref[...], vo_ref[...] = _adamw_math(w_ref[...], g, m_ref[...], v_ref[...])

    if by_rows:
        grid = (rows // tile,)
        full = pl.BlockSpec((tile, cols), lambda i: (i, 0))
        half = pl.BlockSpec((tile, HALF), lambda i: (i, 0))
    else:
        grid = (cols // tile,)
        full = pl.BlockSpec((rows, tile), lambda j: (0, j))
        half = pl.BlockSpec((rows, tile), lambda j: (0, j % per_half))
    return _fused_call(
        body, fuse, name="adamw_halves", grid=grid, in_specs=[full, half, half, full, full],
        out_specs=[full] * 4, out_shape=[jax.ShapeDtypeStruct(w.shape, F32)] * 4, scratch_shapes=[],
        operands=(w, own, sib, m, v))


def _ln_in_fwd(x, g, b, fuse=NO_FUSE):
    bsz, s, d = x.shape
    tp = s + HEAD
    nh = 2
    sh = s // nh
    rc = min(256, sh)

    def body(x_ref, g_ref, b_ref, s0_ref, s0b_ref):
        h = pl.program_id(1)
        gg, bb = g_ref[...], b_ref[...]

        def step(i, carry):
            src = pl.ds(pl.multiple_of(i * rc, rc), rc)
            dst = pl.ds(pl.multiple_of(HEAD + h * sh + i * rc, 64), rc)
            xh, _ = _ln_stats(x_ref[0, src, :])
            val = xh * gg + bb
            s0_ref[0, dst, :] = val
            s0b_ref[0, dst, :] = val.astype(BF16)
            return carry

        lax.fori_loop(0, sh // rc, step, 0)

    full = lambda bi, hi: (bi, 0, 0)
    (s0, s0b), got = _fused_call(
        body, fuse, name="ln_in_fwd",
        out_shape=[jax.ShapeDtypeStruct((bsz, tp, d), F32), jax.ShapeDtypeStruct((bsz, tp, d), BF16)],
        grid=(bsz, nh),
        in_specs=[pl.BlockSpec((1, sh, d), lambda bi, hi: (bi, hi, 0)),
                  pl.BlockSpec((1, d), lambda bi, hi: (0, 0)),
                  pl.BlockSpec((1, d), lambda bi, hi: (0, 0))],
        out_specs=[pl.BlockSpec((1, tp, d), full)] * 2, scratch_shapes=[], operands=(x, g, b),
        late_relay=True)
    return s0, s0b, got


def _ln_meta(s0, s0b, meta, g, b):
    bsz, tp, d = s0.shape

    def body(s0_in, s0b_in, meta_ref, g_ref, b_ref, s0_ref, s0b_ref):
        mh, _ = _ln_stats(meta_ref[...])
        mv = mh * g_ref[...] + b_ref[...]
        s0_ref[0, 0:PAD, :] = jnp.zeros((PAD, d), F32)
        s0b_ref[0, 0:PAD, :] = jnp.zeros((PAD, d), BF16)
        s0_ref[0, PAD:HEAD, :] = mv
        s0b_ref[0, PAD:HEAD, :] = mv.astype(BF16)

    head = pl.BlockSpec((1, HEAD, d), lambda bi: (bi, 0, 0))
    vec = pl.BlockSpec((1, d), lambda bi: (0, 0))
    return pl.pallas_call(
        body, name="ln_meta", out_shape=[jax.ShapeDtypeStruct(s0.shape, F32), jax.ShapeDtypeStruct(s0b.shape, BF16)],
        grid=(bsz,), in_specs=[head, head, pl.BlockSpec((N_META, d), lambda bi: (0, 0)), vec, vec],
        out_specs=[head, head], input_output_aliases={0: 0, 1: 1},
        compiler_params=_params(("parallel",)),
    )(s0, s0b, meta, g, b)


def _in_proj(s0b, w_int, fuse=NO_FUSE):
    r, d = s0b.shape
    tm = _row_tile(r)

    def body(a_ref, w_ref, o_ref):
        o_ref[...] = _nt(a_ref[...], w_ref[...])

    (u,), got = _fused_call(
        body, fuse, name="in_proj", out_shape=[jax.ShapeDtypeStruct((r, D_IN_PAD), F32)],
        grid=(r // tm,),
        in_specs=[pl.BlockSpec((tm, d), lambda i: (i, 0)), pl.BlockSpec((D_IN_PAD, d), lambda i: (0, 0))],
        out_specs=[pl.BlockSpec((tm, D_IN_PAD), lambda i: (i, 0))], scratch_shapes=[], operands=(s0b, w_int))
    return u, got


def _conv_fwd(u, conv_w, conv_b, fuse=NO_FUSE):
    bsz, tp, _ = u.shape
    nchunk = tp // CHUNK
    win = CHUNK + 32
    nct = D_CONV // 128

    def body(cv_ref, cg_ref, w_ref, cb_ref, hc_ref, h_scr, win_scr):
        h_scr[0:32, :] = jnp.zeros((32, 128), F32)
        h_scr[32:32 + tp, :] = cv_ref[0] * _sigmoid(cg_ref[0])
        cb = cb_ref[...]

        def step(n, carry):
            r0 = pl.multiple_of(n * CHUNK, CHUNK)
            win_scr[...] = h_scr[pl.ds(r0, win), :]
            acc = jnp.zeros((CHUNK, 128), F32)
            for j in range(CONV_WIDTH):
                acc = acc + w_ref[j:j + 1, :] * win_scr[2 + j:2 + j + CHUNK, :]
            hc_ref[0, pl.ds(r0, CHUNK), :] = acc + cb
            return carry

        lax.fori_loop(0, nchunk, step, 0)

    (hc,), got = _fused_call(
        body, fuse, name="conv_fwd", out_shape=[jax.ShapeDtypeStruct((bsz, tp, D_CONV), F32)],
        grid=(bsz, nct),
        in_specs=[pl.BlockSpec((1, tp, 128), lambda bi, t: (bi, 0, C_VAL // 128 + t)),
                  pl.BlockSpec((1, tp, 128), lambda bi, t: (bi, 0, C_GATE // 128 + t)),
                  pl.BlockSpec((32, 128), lambda bi, t: (0, t)),
                  pl.BlockSpec((1, 128), lambda bi, t: (0, t))],
        out_specs=[pl.BlockSpec((1, tp, 128), lambda bi, t: (bi, 0, t))],
        scratch_shapes=[pltpu.VMEM((tp + 32, 128), F32), pltpu.VMEM((win, 128), F32)],
        operands=(u, u, conv_w, conv_b))
    return hc, got


def _gla_group(nchunk):
    return 11 if nchunk % 11 == 0 else nchunk


def _bdot(a, b, ca, cb, precision=None):
    return lax.dot_general(a, b, (((ca,), (cb,)), ((0,), (0,))), preferred_element_type=F32, precision=precision)


def _bnn(a, b, **kw):
    return _bdot(a, b, 2, 1, **kw)


def _bnt(a, b, **kw):
    return _bdot(a, b, 2, 2, **kw)


def _gla_consts(nb):
    row = lax.broadcasted_iota(jnp.int32, (nb, CHUNK, CHUNK), 1)
    col = lax.broadcasted_iota(jnp.int32, (nb, CHUNK, CHUNK), 2)
    lane = lax.broadcasted_iota(jnp.int32, (1, 1, 128), 2)
    return row >= col, row <= col, [lane < GLA_DK, lane >= GLA_DK]


def _gla_group_terms(g, nb, q_ref, k_ref, gd_ref, gup_ref, gb_ref, tril):
    m = nb * CHUNK
    rows = pl.ds(pl.multiple_of(g * m, CHUNK), m)
    z = _nn(gd_ref[0, rows, :].astype(BF16), gup_ref[...]) + gb_ref[...]
    valid = g * m + lax.broadcasted_iota(jnp.int32, (m, 1), 0) >= PAD
    lg = jnp.where(valid, _log_sigmoid(z) * (1.0 / GLA_TAU), 0.0)
    bcum = _bnn(tril.astype(F32), lg.reshape(nb, CHUNK, 128), precision=lax.Precision.HIGHEST)
    blast = bcum[:, CHUNK - 1:CHUNK, :]
    eb = jnp.exp(bcum)
    enb = jnp.exp(-bcum)
    erest = jnp.exp(blast - bcum)
    q = (q_ref[0, rows, :] * Q_SCALE).reshape(nb, CHUNK, 128)
    k = k_ref[0, rows, :].reshape(nb, CHUNK, 128)
    return rows, valid, z, eb, enb, erest, jnp.exp(blast), q * eb, k * enb, k * erest


def _grid_ends(grid):
    ids = [pl.program_id(i) for i in range(len(grid))]
    first = functools.reduce(jnp.logical_and, [i == 0 for i in ids])
    last = functools.reduce(jnp.logical_and, [i == g - 1 for i, g in zip(ids, grid)])
    return first, last


def _gla_fwd(u, gup, gbias, gnorm, fuse=NO_FUSE):
    bsz, tp, _ = u.shape
    nchunk = tp // CHUNK
    nb = _gla_group(nchunk)

    def body(q_ref, k_ref, v_ref, r_ref, gd_ref, gup_ref, gb_ref, gn_ref, out_ref, o_ref, st_ref, s_scr):
        tril, _, hmask = _gla_consts(nb)
        s_scr[...] = jnp.zeros_like(s_scr)
        gn = gn_ref[...]

        def group(g, carry):
            rows, _, _, _, _, _, dec, qe, ke, kd = _gla_group_terms(g, nb, q_ref, k_ref, gd_ref, gup_ref, gb_ref, tril)
            keb, kdb = ke.astype(BF16), kd.astype(BF16)
            for h in range(2):
                cols = slice(h * GLA_DV, (h + 1) * GLA_DV)
                qh = jnp.where(hmask[h], qe, 0.0).astype(BF16)
                vh = v_ref[0, rows, cols].astype(BF16).reshape(nb, CHUNK, GLA_DV)
                a = jnp.where(tril, _bnt(qh, keb), 0.0).astype(BF16)
                st = s_scr[h]
                sts = []
                for n in range(nb):
                    st_ref[0, h, g * nb + n] = st
                    sts.append(st.astype(BF16))
                    st = dec[n] * st + _tn(vh[n], kdb[n])
                s_scr[h] = st
                o = (_bnn(a, vh) + _bnt(qh, jnp.stack(sts))).reshape(nb * CHUNK, GLA_DV)
                o_ref[0, rows, cols] = o
                rms = lax.rsqrt(jnp.mean(o * o, axis=-1, keepdims=True) + LN_EPS)
                rh = r_ref[0, rows, cols]
                out_ref[0, rows, cols] = (o * rms * gn * (rh * _sigmoid(rh))).astype(BF16)
            return carry

        lax.fori_loop(0, nchunk // nb, group, 0)

    res, got = _fused_call(
        body, fuse, name="gla_fwd",
        out_shape=[jax.ShapeDtypeStruct((bsz, tp, 512), BF16), jax.ShapeDtypeStruct((bsz, tp, 512), F32),
                   jax.ShapeDtypeStruct((bsz, GLA_HEADS, nchunk, GLA_DV, 128), F32)],
        grid=(bsz, 2),
        in_specs=[pl.BlockSpec((1, tp, 128), lambda bi, p: (bi, 0, C_Q // 128 + p)),
                  pl.BlockSpec((1, tp, 128), lambda bi, p: (bi, 0, C_K // 128 + p)),
                  pl.BlockSpec((1, tp, 256), lambda bi, p: (bi, 0, C_V // 256 + p)),
                  pl.BlockSpec((1, tp, 256), lambda bi, p: (bi, 0, C_R // 256 + p)),
                  pl.BlockSpec((1, tp, 128), lambda bi, p: (bi, 0, C_GD // 128)),
                  pl.BlockSpec((128, 128), lambda bi, p: (0, p)),
                  pl.BlockSpec((1, 128), lambda bi, p: (0, p)),
                  pl.BlockSpec((1, 128), lambda bi, p: (0, 0))],
        out_specs=[pl.BlockSpec((1, tp, 256), lambda bi, p: (bi, 0, p)),
                   pl.BlockSpec((1, tp, 256), lambda bi, p: (bi, 0, p)),
                   pl.BlockSpec((1, 2, nchunk, GLA_DV, 128), lambda bi, p: (bi, p, 0, 0, 0))],
        scratch_shapes=[pltpu.VMEM((2, GLA_DV, 128), F32)],
        operands=(u, u, u, u, u, gup, gbias, gnorm))
    return res[0], res[1], res[2], got


def _out_proj_ln1(hc, gla_out, w_out, s0, cg, cb, g1, b1, fuse=NO_FUSE):
    r, d = s0.shape
    tm = _row_tile(r)

    def body(hc_ref, a_ref, w_ref, s0_ref, cg_ref, cb_ref, g_ref, b_ref, co_ref, xh_ref, rstd_ref, s1b_ref):
        for rs in _sub_rows(tm):
            xc, _ = _ln_stats(hc_ref[rs, :])
            nv = xc * cg_ref[...] + cb_ref[...]
            co = (nv * _sigmoid(nv)).astype(BF16)
            co_ref[rs, :] = co
            mix = _nn(co, w_ref[0:D_CONV, :]) + _nn(a_ref[rs, :], w_ref[D_CONV:, :])
            xh, rstd = _ln_stats(ALPHA * s0_ref[rs, :] + mix)
            xh_ref[rs, :] = xh
            rstd_ref[rs, :] = rstd
            s1b_ref[rs, :] = (xh * g_ref[...] + b_ref[...]).astype(BF16)

    row = lambda n: pl.BlockSpec((tm, n), lambda i: (i, 0))
    vec = lambda n: pl.BlockSpec((1, n), lambda i: (0, 0))
    res, got = _fused_call(
        body, fuse, name="out_proj_ln1",
        out_shape=[jax.ShapeDtypeStruct((r, D_CONV), BF16), jax.ShapeDtypeStruct((r, d), F32),
                   jax.ShapeDtypeStruct((r, 1), F32), jax.ShapeDtypeStruct((r, d), BF16)],
        grid=(r // tm,),
        in_specs=[row(D_CONV), row(512), pl.BlockSpec((d, d), lambda i: (0, 0)), row(d),
                  vec(D_CONV), vec(D_CONV), vec(d), vec(d)],
        out_specs=[row(D_CONV), row(d), row(1), row(d)], scratch_shapes=[],
        operands=(hc, gla_out, w_out, s0, cg, cb, g1, b1))
    return res[0], res[1], res[2], res[3], got


def _ffn1(s1b, w1, fuse=NO_FUSE):
    r, d = s1b.shape
    tm = _row_tile(r)
    ns, _, wn = w1.shape

    def body(a_ref, w_ref, o_ref):
        a = a_ref[...]
        for j in range(ns):
            o_ref[:, j * wn:(j + 1) * wn] = jnp.maximum(_nn(a, w_ref[j]), 0.0).astype(BF16)

    (ra,), got = _fused_call(
        body, fuse, name="ffn1", out_shape=[jax.ShapeDtypeStruct((r, D_FF), BF16)], grid=(r // tm,),
        in_specs=[pl.BlockSpec((tm, d), lambda i: (i, 0)), pl.BlockSpec(w1.shape, lambda i: (0, 0, 0))],
        out_specs=[pl.BlockSpec((tm, D_FF), lambda i: (i, 0))], scratch_shapes=[], operands=(s1b, w1))
    return ra, got


def _ffn2_ln2_loss(ra, w2, xhat1, g1, b1, g2, b2, tgt, tp):
    r, d = xhat1.shape
    tm = _row_tile(tp)
    per = tp // tm

    def body(ra_ref, w_ref, xh1_ref, g1_ref, b1_ref, g2_ref, b2_ref, tgt_ref, dt_ref, dtb_ref, acc_ref, t_ref, sem):
        i = pl.program_id(0)
        b, j = i // per, i % per

        @pl.when(i == 0)
        def _():
            acc_ref[...] = jnp.zeros_like(acc_ref)

        head_copy = pltpu.make_async_copy(tgt_ref.at[b, pl.ds(0, tm - HEAD), :], t_ref.at[pl.ds(HEAD, tm - HEAD), :], sem)
        body_copy = pltpu.make_async_copy(
            tgt_ref.at[b, pl.ds(pl.multiple_of(jnp.maximum(j * tm - HEAD, 0), 64), tm), :], t_ref, sem)

        @pl.when(j == 0)
        def _():
            t_ref[0:HEAD, :] = jnp.zeros((HEAD, d), F32)
            head_copy.start()

        pl.when(j > 0)(body_copy.start)

        sums = [jnp.zeros((1, d), F32)] * 3
        for rs in _sub_rows(tm):
            rb = ra_ref[rs, :]
            f = _nn(rb * rb, w_ref[...])
            if rs.start == 0:
                pl.when(j == 0)(head_copy.wait)
                pl.when(j > 0)(body_copy.wait)
            s1 = xh1_ref[rs, :] * g1_ref[...] + b1_ref[...]
            xh2, rstd2 = _ln_stats(ALPHA * s1 + f)
            y = xh2 * g2_ref[...] + b2_ref[...]
            rowid = (i % per) * tm + rs.start + lax.broadcasted_iota(jnp.int32, (rs.stop - rs.start, 1), 0)
            e = jnp.where(rowid >= HEAD, y - t_ref[rs, :], 0.0)
            dy = e * (1.0 / d)
            dt2 = _ln_bwd(dy * g2_ref[...], xh2, rstd2)
            dt_ref[rs, :] = dt2
            dtb_ref[rs, :] = dt2.astype(BF16)
            sums = [sums[0] + (0.5 / d) * jnp.sum(e * e, axis=0, keepdims=True),
                    sums[1] + jnp.sum(dy * xh2, axis=0, keepdims=True), sums[2] + jnp.sum(dy, axis=0, keepdims=True)]
        for k in range(3):
            acc_ref[k:k + 1, :] += sums[k]

    row = lambda n: pl.BlockSpec((tm, n), lambda i: (i, 0))
    vec = pl.BlockSpec((1, d), lambda i: (0, 0))
    return pl.pallas_call(
        body, name="ffn2_ln2_loss",
        out_shape=[jax.ShapeDtypeStruct((r, d), F32), jax.ShapeDtypeStruct((r, d), BF16),
                   jax.ShapeDtypeStruct((8, d), F32)],
        grid=(r // tm,),
        in_specs=[row(D_FF), pl.BlockSpec((D_FF, d), lambda i: (0, 0)), row(d), vec, vec, vec, vec, ANY],
        out_specs=[row(d), row(d), pl.BlockSpec((8, d), lambda i: (0, 0))],
        scratch_shapes=[pltpu.VMEM((tm, d), F32), pltpu.SemaphoreType.DMA],
        compiler_params=_params(("arbitrary",)),
    )(ra, w2, xhat1, g1, b1, g2, b2, tgt)


def _ffn_bwd_da(dt2b, w2, ra):
    r, d = dt2b.shape
    tm = _row_tile(r)

    def body(g_ref, w_ref, ra_ref, o_ref):
        o_ref[...] = (_nt(g_ref[...], w_ref[...]) * (2.0 * ra_ref[...].astype(F32))).astype(BF16)

    return pl.pallas_call(
        body, name="ffn_bwd_da", out_shape=jax.ShapeDtypeStruct((r, D_FF), BF16),
        grid=(r // tm,),
        in_specs=[pl.BlockSpec((tm, d), lambda i: (i, 0)), pl.BlockSpec((D_FF, d), lambda i: (0, 0)),
                  pl.BlockSpec((tm, D_FF), lambda i: (i, 0))],
        out_specs=pl.BlockSpec((tm, D_FF), lambda i: (i, 0)),
        compiler_params=_params(("parallel",)),
    )(dt2b, w2, ra)


def _ffn_bwd_ln1(da, w1, dt2, xhat1, rstd1, g1):
    r, d = dt2.shape
    tm = _row_tile(r)

    def body(da_ref, w_ref, dt2_ref, xh_ref, rstd_ref, g_ref, dt_ref, dtb_ref, acc_ref):
        @pl.when(pl.program_id(0) == 0)
        def _():
            acc_ref[...] = jnp.zeros_like(acc_ref)

        sums = [jnp.zeros((1, d), F32)] * 2
        for rs in _sub_rows(tm):
            ds1 = ALPHA * dt2_ref[rs, :]
            for j in range(w1.shape[0]):
                ds1 = ds1 + _nt(da_ref[rs, j * w1.shape[2]:(j + 1) * w1.shape[2]], w_ref[j])
            xh = xh_ref[rs, :]
            dt1 = _ln_bwd(ds1 * g_ref[...], xh, rstd_ref[rs, :])
            dt_ref[rs, :] = dt1
            dtb_ref[rs, :] = dt1.astype(BF16)
            sums = [sums[0] + jnp.sum(ds1 * xh, axis=0, keepdims=True), sums[1] + jnp.sum(ds1, axis=0, keepdims=True)]
        for k in range(2):
            acc_ref[k:k + 1, :] += sums[k]

    row = lambda n: pl.BlockSpec((tm, n), lambda i: (i, 0))
    return pl.pallas_call(
        body, name="ffn_bwd_ln1",
        out_shape=[jax.ShapeDtypeStruct((r, d), F32), jax.ShapeDtypeStruct((r, d), BF16),
                   jax.ShapeDtypeStruct((8, d), F32)],
        grid=(r // tm,),
        in_specs=[row(D_FF), pl.BlockSpec(w1.shape, lambda i: (0, 0, 0)), row(d), row(d), row(1),
                  pl.BlockSpec((1, d), lambda i: (0, 0))],
        out_specs=[row(d), row(d), pl.BlockSpec((8, d), lambda i: (0, 0))],
        compiler_params=_params(("arbitrary",)),
    )(da, w1, dt2, xhat1, rstd1, g1)


def _matmul_tn(lhs, rhs, bm, square_lhs=False, name="matmul_tn", fuse=NO_FUSE):
    r, m = lhs.shape
    n = rhs.shape[1]
    tk = _reduce_tile(r, True)

    def body(a_ref, b_ref, o_ref):
        @pl.when(pl.program_id(1) == 0)
        def _():
            o_ref[...] = jnp.zeros_like(o_ref)

        a = a_ref[...]
        if square_lhs:
            a = a * a
        o_ref[...] += _tn(a, b_ref[...])

    (out,), got = _fused_call(
        body, fuse, name=name, out_shape=[jax.ShapeDtypeStruct((m, n), F32)], grid=(m // bm, r // tk),
        in_specs=[pl.BlockSpec((tk, bm), lambda i, k: (k, i)), pl.BlockSpec((tk, n), lambda i, k: (k, 0))],
        out_specs=[pl.BlockSpec((bm, n), lambda i, k: (i, 0))], scratch_shapes=[], operands=(lhs, rhs))
    return out, got


def _grad_w_ff1(s1b, da, fuse=NO_FUSE):
    r, d = s1b.shape
    wn = da.shape[1] // 4
    tk = _reduce_tile(r, True)

    def body(a_ref, b_ref, o_ref):
        @pl.when(pl.program_id(1) == 0)
        def _():
            o_ref[...] = jnp.zeros_like(o_ref)

        o_ref[0] += _tn(a_ref[...], b_ref[...])

    (out,), got = _fused_call(
        body, fuse, name="grad_w_ff1", out_shape=[jax.ShapeDtypeStruct((4, d, wn), F32)], grid=(4, r // tk),
        in_specs=[pl.BlockSpec((tk, d), lambda j, k: (k, 0)), pl.BlockSpec((tk, wn), lambda j, k: (k, j))],
        out_specs=[pl.BlockSpec((1, d, wn), lambda j, k: (j, 0, 0))], scratch_shapes=[], operands=(s1b, da))
    return out, got


def _grad_w_out(conv_of, gla_of, dt1b, fuse=NO_FUSE):
    r, n = dt1b.shape
    tk = _reduce_tile(r, True)

    def body(a_ref, b_ref, g_ref, o_ref):
        @pl.when(pl.program_id(1) == 0)
        def _():
            o_ref[...] = jnp.zeros_like(o_ref)

        @pl.when(pl.program_id(0) == 0)
        def _():
            o_ref[...] += _tn(a_ref[...], g_ref[...])

        @pl.when(pl.program_id(0) == 1)
        def _():
            o_ref[...] += _tn(b_ref[...], g_ref[...])

    lhs = pl.BlockSpec((tk, 512), lambda i, k: (k, 0))
    (out,), got = _fused_call(
        body, fuse, name="grad_w_out", out_shape=[jax.ShapeDtypeStruct((2 * 512, n), F32)], grid=(2, r // tk),
        in_specs=[lhs, lhs, pl.BlockSpec((tk, n), lambda i, k: (k, 0))],
        out_specs=[pl.BlockSpec((512, n), lambda i, k: (i, 0))], scratch_shapes=[],
        operands=(conv_of, gla_of, dt1b))
    return out, got


def _out_proj_bwd(dt1b, w_out, hc, cg, cb, fuse=NO_FUSE):
    r, d = dt1b.shape
    tm = _row_tile(r)

    def body(g_ref, w_ref, hc_ref, cg_ref, cb_ref, dhc_ref, dgla_ref, acc_ref):
        @pl.when(pl.program_id(0) == 0)
        def _():
            acc_ref[...] = jnp.zeros_like(acc_ref)

        gg = cg_ref[...]
        sums = [jnp.zeros((1, D_CONV), F32)] * 3
        for rs in _sub_rows(tm):
            dmix = _nt(g_ref[rs, :], w_ref[...])
            dgla_ref[rs, :] = dmix[:, D_CONV:]
            xh, rstd = _ln_stats(hc_ref[rs, :])
            nv = xh * gg + cb_ref[...]
            sig = _sigmoid(nv)
            dn = dmix[:, :D_CONV] * (sig * (1.0 + nv * (1.0 - sig)))
            dhc = _ln_bwd(dn * gg, xh, rstd)
            dhc_ref[rs, :] = dhc
            sums = [sums[0] + jnp.sum(dhc, axis=0, keepdims=True), sums[1] + jnp.sum(dn * xh, axis=0, keepdims=True),
                    sums[2] + jnp.sum(dn, axis=0, keepdims=True)]
        for k in range(3):
            acc_ref[k:k + 1, :] += sums[k]

    row = lambda n: pl.BlockSpec((tm, n), lambda i: (i, 0))
    vec = pl.BlockSpec((1, D_CONV), lambda i: (0, 0))
    res, got = _fused_call(
        body, fuse, name="out_proj_bwd",
        out_shape=[jax.ShapeDtypeStruct((r, D_CONV), F32), jax.ShapeDtypeStruct((r, 512), F32),
                   jax.ShapeDtypeStruct((8, D_CONV), F32)],
        grid=(r // tm,),
        in_specs=[row(d), pl.BlockSpec((d, d), lambda i: (0, 0)), row(D_CONV), vec, vec],
        out_specs=[row(D_CONV), row(512), pl.BlockSpec((8, D_CONV), lambda i: (0, 0))], scratch_shapes=[],
        operands=(dt1b, w_out, hc, cg, cb))
    return res[0], res[1], res[2], got


def _conv_bwd(dhc, u, conv_w, fuse=NO_FUSE):
    bsz, tp, _ = u.shape
    nchunk = tp // CHUNK
    win = CHUNK + 32
    nct = D_CONV // 128

    def body(dhc_ref, cv_ref, cg_ref, w_ref, dv_ref, dg_ref, dw_ref, h_scr, dhc_scr, hwin, dwin, dw_scr):
        h_scr[0:32, :] = jnp.zeros((32, 128), F32)
        h_scr[32:32 + tp, :] = cv_ref[0] * _sigmoid(cg_ref[0])
        dhc_scr[0:tp, :] = dhc_ref[0]
        dhc_scr[tp:tp + 32, :] = jnp.zeros((32, 128), F32)
        dw_scr[...] = jnp.zeros_like(dw_scr)

        def step(n, carry):
            r0 = pl.multiple_of(n * CHUNK, CHUNK)
            rows = pl.ds(r0, CHUNK)
            hwin[...] = h_scr[pl.ds(r0, win), :]
            dwin[...] = dhc_scr[pl.ds(r0, win), :]
            dcur = dwin[0:CHUNK, :]
            acc = jnp.zeros((CHUNK, 128), F32)
            for j in range(CONV_WIDTH):
                acc = acc + w_ref[j:j + 1, :] * dwin[30 - j:30 - j + CHUNK, :]
                prod = dcur * hwin[2 + j:2 + j + CHUNK, :]
                dw_scr[j * 8:(j + 1) * 8, :] += jnp.sum(prod.reshape(CHUNK // 8, 8, 128), axis=0)
            cg = cg_ref[0, rows, :]
            sig = _sigmoid(cg)
            rowid = n * CHUNK + lax.broadcasted_iota(jnp.int32, (CHUNK, 1), 0)
            dh = jnp.where(rowid >= PAD, acc, 0.0)
            dv_ref[0, rows, :] = (dh * sig).astype(BF16)
            dg_ref[0, rows, :] = (dh * cv_ref[0, rows, :] * sig * (1.0 - sig)).astype(BF16)
            return carry

        lax.fori_loop(0, nchunk, step, 0)
        dw_ref[0] = jnp.zeros((32, 128), F32)
        for j in range(CONV_WIDTH):
            dw_ref[0, j:j + 1, :] = jnp.sum(dw_scr[j * 8:(j + 1) * 8, :], axis=0, keepdims=True)

    blk = lambda off: pl.BlockSpec((1, tp, 128), lambda bi, t: (bi, 0, off // 128 + t))
    res, got = _fused_call(
        body, fuse, name="conv_bwd",
        out_shape=[jax.ShapeDtypeStruct((bsz, tp, D_CONV), BF16), jax.ShapeDtypeStruct((bsz, tp, D_CONV), BF16),
                   jax.ShapeDtypeStruct((bsz, 32, D_CONV), F32)],
        grid=(bsz, nct),
        in_specs=[blk(0), blk(C_VAL), blk(C_GATE), pl.BlockSpec((32, 128), lambda bi, t: (0, t))],
        out_specs=[blk(0), blk(0), pl.BlockSpec((1, 32, 128), lambda bi, t: (bi, 0, t))],
        scratch_shapes=[pltpu.VMEM((tp + 32, 128), F32), pltpu.VMEM((tp + 32, 128), F32),
                        pltpu.VMEM((win, 128), F32), pltpu.VMEM((win, 128), F32),
                        pltpu.VMEM((CONV_WIDTH * 8, 128), F32)],
        operands=(dhc, u, u, conv_w))
    return res[0], res[1], res[2], got


def _gla_bwd(dgla, u, o_pre, states, gup, gbias, gnorm, fuse=NO_FUSE):
    bsz, tp, _ = u.shape
    nchunk = tp // CHUNK
    nb = _gla_group(nchunk)

    def body(dy_ref, q_ref, k_ref, v_ref, r_ref, gd_ref, o_ref, st_ref, gup_ref, gb_ref, gn_ref,
             dq_ref, dk_ref, dv_ref, dr_ref, dgd_ref, dgup_ref, vec_ref, h_scr, gup_acc):
        tril, triu, hmask = _gla_consts(nb)
        h_scr[...] = jnp.zeros_like(h_scr)
        gup_acc[...] = jnp.zeros_like(gup_acc)
        gn = gn_ref[...]
        gupb = gup_ref[...]
        m = nb * CHUNK
        ngroup = nchunk // nb

        def group(i, carry):
            dbias, dgn = carry
            g = ngroup - 1 - i
            rows, valid, z, eb, enb, erest, dec, qe, ke, kd = _gla_group_terms(
                g, nb, q_ref, k_ref, gd_ref, gup_ref, gb_ref, tril)
            keb, kdb = ke.astype(BF16), kd.astype(BF16)
            dqe = jnp.zeros((nb, CHUNK, 128), F32)
            dke = jnp.zeros((nb, CHUNK, 128), F32)
            dkd = jnp.zeros((nb, CHUNK, 128), F32)
            ddec = jnp.zeros((nb, 1, 128), F32)
            for h in range(2):
                cols = slice(h * GLA_DV, (h + 1) * GLA_DV)
                o = o_ref[0, rows, cols]
                rh = r_ref[0, rows, cols]
                dy = dy_ref[0, rows, cols]
                rms = lax.rsqrt(jnp.mean(o * o, axis=-1, keepdims=True) + LN_EPS)
                nrm = o * rms
                sig = _sigmoid(rh)
                sw = rh * sig
                dr_ref[0, rows, cols] = (dy * nrm * gn * (sig * (1.0 + rh * (1.0 - sig)))).astype(BF16)
                dgn = dgn + jnp.sum(dy * nrm * sw, axis=0, keepdims=True)
                dn = dy * gn * sw
                do = rms * (dn - nrm * jnp.mean(dn * nrm, axis=-1, keepdims=True))
                dob = do.astype(BF16).reshape(nb, CHUNK, GLA_DV)
                qh = jnp.where(hmask[h], qe, 0.0).astype(BF16)
                vh = v_ref[0, rows, cols].astype(BF16).reshape(nb, CHUNK, GLA_DV)
                ht = h_scr[h]
                hts = [None] * nb
                for n in reversed(range(nb)):
                    hts[n] = ht
                    ht = dec[n] * ht + _tn(dob[n], qh[n])
                h_scr[h] = ht
                htf = jnp.stack(hts)
                htb = htf.astype(BF16)
                st = st_ref[0, h, pl.ds(g * nb, nb)]
                at = jnp.where(triu, _bnt(keb, qh), 0.0).astype(BF16)
                da = jnp.where(tril, _bnt(dob, vh), 0.0).astype(BF16)
                dat = jnp.where(triu, _bnt(vh, dob), 0.0).astype(BF16)
                dqe = dqe + jnp.where(hmask[h], _bnn(da, keb) + _bnn(dob, st.astype(BF16)), 0.0)
                dke = dke + _bnn(dat, qh)
                dv_ref[0, rows, cols] = (_bnn(at, dob) + _bnt(kdb, htb)).reshape(m, GLA_DV).astype(BF16)
                dkd = dkd + jnp.where(hmask[h], _bnn(vh, htb), 0.0)
                ddec = ddec + jnp.where(hmask[h], jnp.sum(htf * st, axis=1, keepdims=True), 0.0)
            dq_ref[0, rows, :] = (dqe * eb * Q_SCALE).reshape(m, 128).astype(BF16)
            dk_ref[0, rows, :] = (dke * enb + dkd * erest).reshape(m, 128).astype(BF16)
            db = dqe * qe - dke * ke - dkd * kd
            dblast = jnp.sum(dkd * kd, axis=1, keepdims=True) + ddec * dec
            lastrow = lax.broadcasted_iota(jnp.int32, (1, CHUNK, 1), 1) == CHUNK - 1
            db = db + jnp.where(lastrow, dblast, 0.0)
            dlg = _bnn(triu.astype(F32), db, precision=lax.Precision.HIGHEST).reshape(m, 128)
            dz = jnp.where(valid, dlg, 0.0) * (1.0 / GLA_TAU) * (1.0 - _sigmoid(z))
            dzb = dz.astype(BF16)
            dgd_ref[0, 0, rows, :] = _nt(dzb, gupb).astype(BF16)
            gup_acc[...] += _tn(gd_ref[0, rows, :].astype(BF16), dzb)
            return dbias + jnp.sum(dz, axis=0, keepdims=True), dgn

        zero = jnp.zeros((1, 128), F32)
        dbias, dgn = lax.fori_loop(0, ngroup, group, (zero, zero))
        dgup_ref[0] = gup_acc[...]
        vec_ref[0] = jnp.zeros((8, 128), F32)
        vec_ref[0, 0:1, :] = dbias
        vec_ref[0, 1:2, :] = dgn

    pair = lambda w, off: pl.BlockSpec((1, tp, w), lambda bi, p: (bi, 0, off // w + p))
    return _fused_call(
        body, fuse, name="gla_bwd",
        out_shape=[jax.ShapeDtypeStruct((bsz, tp, 256), BF16), jax.ShapeDtypeStruct((bsz, tp, 256), BF16),
                   jax.ShapeDtypeStruct((bsz, tp, 512), BF16), jax.ShapeDtypeStruct((bsz, tp, 512), BF16),
                   jax.ShapeDtypeStruct((bsz, 2, tp, 128), BF16), jax.ShapeDtypeStruct((bsz, 128, 256), F32),
                   jax.ShapeDtypeStruct((bsz, 8, 256), F32)],
        grid=(bsz, 2),
        in_specs=[pair(256, 0), pair(128, C_Q), pair(128, C_K), pair(256, C_V), pair(256, C_R),
                  pl.BlockSpec((1, tp, 128), lambda bi, p: (bi, 0, C_GD // 128)),
                  pair(256, 0),
                  pl.BlockSpec((1, 2, nchunk, GLA_DV, 128), lambda bi, p: (bi, p, 0, 0, 0)),
                  pl.BlockSpec((128, 128), lambda bi, p: (0, p)),
                  pl.BlockSpec((1, 128), lambda bi, p: (0, p)),
                  pl.BlockSpec((1, 128), lambda bi, p: (0, 0))],
        out_specs=[pair(128, 0), pair(128, 0), pair(256, 0), pair(256, 0),
                   pl.BlockSpec((1, 1, tp, 128), lambda bi, p: (bi, p, 0, 0)),
                   pl.BlockSpec((1, 128, 128), lambda bi, p: (bi, 0, p)),
                   pl.BlockSpec((1, 8, 128), lambda bi, p: (bi, 0, p))],
        scratch_shapes=[pltpu.VMEM((2, GLA_DV, 128), F32), pltpu.VMEM((128, 128), F32)],
        operands=(dgla, u, u, u, u, u, o_pre, states, gup, gbias, gnorm))


_DU_OFFSETS = (C_VAL, C_GATE, C_Q, C_K, C_V, C_R)
_DU_WIDTHS = (512, 512, 256, 256, 512, 512)


def _du_specs(tm, per, row_map):
    specs = [pl.BlockSpec((tm, w), row_map) for w in _DU_WIDTHS]
    for p in range(2):
        specs.append(pl.BlockSpec((1, 1, tm, 128), lambda *ix, p=p: (row_map(*ix)[0] // per, p, row_map(*ix)[0] % per, 0)))
    return specs


def _du_pieces(refs):
    out = [(off, ref[...]) for off, ref in zip(_DU_OFFSETS, refs[:6])]
    dgd = (refs[6][0, 0].astype(F32) + refs[7][0, 0].astype(F32)).astype(BF16)
    out.append((C_GD, dgd))
    return out


def _in_proj_bwd(pieces, dgd, w_int, dt1, tp, fuse=NO_FUSE):
    r, d = dt1.shape
    tm = _row_tile(tp)
    per = tp // tm

    def body(*refs):
        w_ref, dt_ref, o_ref = refs[8:]
        acc = ALPHA * dt_ref[...]
        for off, val in _du_pieces(refs[:8]):
            acc = acc + _nn(val, w_ref[off:off + val.shape[1], :])
        o_ref[...] = acc

    row = lambda i: (i, 0)
    (ds0,), got = _fused_call(
        body, fuse, name="in_proj_bwd", out_shape=[jax.ShapeDtypeStruct((r, d), F32)], grid=(r // tm,),
        in_specs=_du_specs(tm, per, row) + [pl.BlockSpec((D_IN_PAD, d), lambda i: (0, 0)), pl.BlockSpec((tm, d), row)],
        out_specs=[pl.BlockSpec((tm, d), row)], scratch_shapes=[], operands=(*pieces, dgd, dgd, w_int, dt1))
    return ds0, got


def _grad_w_in(pieces, dgd, s0b, tp, fuse=NO_FUSE):
    r, d = s0b.shape
    tk = _reduce_tile(tp, False)
    per = tp // tk

    def body(*refs):
        s_ref, o_ref = refs[8:]

        @pl.when(pl.program_id(0) == 0)
        def _():
            o_ref[...] = jnp.zeros_like(o_ref)

        s = s_ref[...]
        for off, val in _du_pieces(refs[:8]):
            o_ref[off:off + val.shape[1], :] += _tn(val, s)

    row = lambda k: (k, 0)
    (out,), got = _fused_call(
        body, fuse, name="grad_w_in", out_shape=[jax.ShapeDtypeStruct((D_IN_PAD, d), F32)], grid=(r // tk,),
        in_specs=_du_specs(tk, per, row) + [pl.BlockSpec((tk, d), row)],
        out_specs=[pl.BlockSpec((D_IN_PAD, d), lambda k: (0, 0))], scratch_shapes=[],
        operands=(*pieces, dgd, dgd, s0b))
    return out, got


def _ln_in_bwd(ds0, x, meta, g):
    bsz, s, d = x.shape
    tp = s + HEAD
    nh = 2
    sh = s // nh
    rc = min(256, sh)

    def body(ds_ref, x_ref, meta_ref, g_ref, gx_ref, dm_ref, vec_ref):
        h = pl.program_id(1)
        gg = g_ref[...]

        @pl.when(h == 0)
        def _():
            mh, mr = _ln_stats(meta_ref[...])
            dsm = ds_ref[0, PAD:HEAD, :]
            dm_ref[0] = _ln_bwd(dsm * gg, mh, mr)
            vec_ref[0] = jnp.zeros((8, d), F32)
            vec_ref[0, 0:1, :] = jnp.sum(dsm * mh, axis=0, keepdims=True)
            vec_ref[0, 1:2, :] = jnp.sum(dsm, axis=0, keepdims=True)

        def step(i, carry):
            sg, sb = carry
            dst = pl.ds(pl.multiple_of(i * rc, rc), rc)
            src = pl.ds(pl.multiple_of(HEAD + h * sh + i * rc, 64), rc)
            xh, rstd = _ln_stats(x_ref[0, dst, :])
            dsv = ds_ref[0, src, :]
            gx_ref[0, dst, :] = _ln_bwd(dsv * gg, xh, rstd)
            return sg + jnp.sum(dsv * xh, axis=0, keepdims=True), sb + jnp.sum(dsv, axis=0, keepdims=True)

        zero = jnp.zeros((1, d), F32)
        sg, sb = lax.fori_loop(0, sh // rc, step, (zero, zero))
        vec_ref[0, 0:1, :] += sg
        vec_ref[0, 1:2, :] += sb

    return pl.pallas_call(
        body, name="ln_in_bwd",
        out_shape=[jax.ShapeDtypeStruct((bsz, s, d), F32), jax.ShapeDtypeStruct((bsz, N_META, d), F32),
                   jax.ShapeDtypeStruct((bsz, 8, d), F32)],
        grid=(bsz, nh),
        in_specs=[pl.BlockSpec((1, tp, d), lambda bi, hi: (bi, 0, 0)),
                  pl.BlockSpec((1, sh, d), lambda bi, hi: (bi, hi, 0)),
                  pl.BlockSpec((N_META, d), lambda bi, hi: (0, 0)),
                  pl.BlockSpec((1, d), lambda bi, hi: (0, 0))],
        out_specs=[pl.BlockSpec((1, sh, d), lambda bi, hi: (bi, hi, 0)),
                   pl.BlockSpec((1, N_META, d), lambda bi, hi: (bi, 0, 0)),
                   pl.BlockSpec((1, 8, d), lambda bi, hi: (bi, 0, 0))],
        compiler_params=_params(("parallel", "arbitrary")),
    )(ds0, x, meta, g)


def _rows128(a):
    return a.reshape(-1, 128)


def kernel(x, meta_tokens, ln_in_g, ln_in_b, w_in, conv_w, conv_b, conv_ln_g, conv_ln_b, gate_up, gate_bias, gla_norm_g, w_out, ln1_g, ln1_b, w_ff1, w_ff2, ln2_g, ln2_b, loss_target, m_meta_tokens, m_ln_in_g, m_ln_in_b, m_w_in, m_conv_w, m_conv_b, m_conv_ln_g, m_conv_ln_b, m_gate_up, m_gate_bias, m_gla_norm_g, m_w_out, m_ln1_g, m_ln1_b, m_w_ff1, m_w_ff2, m_ln2_g, m_ln2_b, v_meta_tokens, v_ln_in_g, v_ln_in_b, v_w_in, v_conv_w, v_conv_b, v_conv_ln_g, v_conv_ln_b, v_gate_up, v_gate_bias, v_gla_norm_g, v_w_out, v_ln1_g, v_ln1_b, v_w_ff1, v_w_ff2, v_ln2_g, v_ln2_b):
    bsz, seq, d = x.shape
    tp = seq + HEAD
    r = bsz * tp
    xi, yi, ci = _mesh_pos()
    chip = 2 * xi + yi
    c_arr = jnp.reshape(ci, (1,)).astype(jnp.int32)
    pos_arr = jnp.stack([chip, ci]).astype(jnp.int32)

    sh_in = D_IN // 4
    shard_in = jnp.pad(w_in[0].T.astype(BF16), ((0, D_IN_PAD // 4 - sh_in), (0, 0)))
    shard_w1, shard_wout, shard_w2 = w_ff1[0].astype(BF16), w_out[0].astype(BF16), w_ff2[0].astype(BF16)
    small_w = jnp.concatenate([_rows128(meta_tokens), _rows128(conv_w[0]), _rows128(gate_up[0])], axis=0)
    ln_in_g2, ln_in_b2 = ln_in_g.reshape(1, d), ln_in_b.reshape(1, d)

    s0, s0b, ((g_int,), (g_small,)) = _ln_in_fwd(
        x, ln_in_g2, ln_in_b2, [("gather", [(shard_in, 0, D_IN_PAD // 4, None)]), ("small", [small_w])])
    (g_int,) = _place_own([g_int], [shard_in])
    (g_small,) = _place_own([g_small], [small_w])
    n_meta_rows, n_cw_rows = N_META * 256 // 128, CONV_WIDTH * 128 // 128
    meta_full = jnp.concatenate([g_small[j, :n_meta_rows].reshape(N_META, 256) for j in range(4)], axis=1)
    convw_full = jnp.concatenate(
        [g_small[j, n_meta_rows:n_meta_rows + n_cw_rows].reshape(CONV_WIDTH, 128) for j in range(4)], axis=1)
    gup_full = jnp.concatenate(
        [g_small[j, n_meta_rows + n_cw_rows:].reshape(GLA_RANK, 64) for j in range(4)], axis=1)
    convw_p = jnp.pad(convw_full, ((0, 1), (0, 0)))
    gup_p = jnp.pad(gup_full, ((0, 128 - GLA_RANK), (0, 0))).astype(BF16)
    s0, s0b = _ln_meta(s0, s0b, meta_full, ln_in_g2, ln_in_b2)
    w_int = jnp.pad(g_int[:, :sh_in].reshape(D_IN, d), ((0, D_IN_PAD - D_IN), (0, 0)))
    s0f, s0bf = s0.reshape(r, d), s0b.reshape(r, d)
    u, ((w1_buf,),) = _in_proj(s0bf, w_int, [("gather", [(shard_w1, 0, 640, None)])])
    (w1_buf,) = _place_own([w1_buf], [shard_w1])
    u3 = u.reshape(bsz, tp, D_IN_PAD)
    hc, ((w1_buf,),) = _conv_fwd(u3, convw_p, conv_b, [("gather", [(shard_w1, 640, 384, w1_buf)])])
    hc = hc.reshape(r, D_CONV)
    hc, u3_after_conv = lax.optimization_barrier((hc, u3))
    gla_out, o_pre, states, ((g_wout,),) = _gla_fwd(
        u3_after_conv, gup_p, gate_bias, gla_norm_g, [("gather", [(shard_wout, 0, 256, None)])])
    (g_wout,) = _place_own([g_wout], [shard_wout])
    wout = g_wout.reshape(d, d)
    gla_of = gla_out.reshape(r, 512)
    conv_of, xhat1, rstd1, s1b, ((w2_buf,),) = _out_proj_ln1(
        hc, gla_of, wout, s0f, conv_ln_g, conv_ln_b, ln1_g, ln1_b, [("gather", [(shard_w2, 0, 448, None)])])
    (w2_buf,) = _place_own([w2_buf], [shard_w2])
    w1 = w1_buf
    ra, ((w2_buf,),) = _ffn1(s1b, w1, [("gather", [(shard_w2, 448, 576, w2_buf)])])
    w2 = w2_buf.reshape(D_FF, d)
    dt2, dt2b, acc2 = _ffn2_ln2_loss(ra, w2, xhat1, ln1_g, ln1_b, ln2_g, ln2_b, loss_target, tp)

    def add_pair(g, got, splits=None):
        return _add_pair(g, got, c_arr, splits)

    da = _ffn_bwd_da(dt2b, w2, ra)
    dt1, dt1b, acc1 = _ffn_bwd_ln1(da, w1, dt2, xhat1, rstd1, ln1_g)
    g_w2, _ = _matmul_tn(ra, dt2b, 1024, square_lhs=True, name="grad_w_ff2")
    big_w2 = g_w2.reshape(4, D_FF // 4, d)
    big_w1, ((pair_w2,),) = _grad_w_ff1(s1b, da, [("pair", [big_w2])])
    dhc, dgla, cacc, ((pair_w1,),) = _out_proj_bwd(dt1b, wout, hc, conv_ln_g, conv_ln_b, [("pair", [big_w1])])
    (part_w1,), (part_w2a, part_w2b) = add_pair(big_w1, pair_w1), add_pair(big_w2, pair_w2, [512, 512])
    g_wout, _ = _grad_w_out(conv_of, gla_of, dt1b)
    big_wout = g_wout.reshape(4, d // 4, d)
    dcv, dcg, dcw, ((chip_w1,), (pair_wout,)) = _conv_bwd(
        dhc.reshape(bsz, tp, D_CONV), u3, convw_p, [("exchange", [part_w1]), ("pair", [big_wout])])
    (part_wout,) = add_pair(big_wout, pair_wout)
    (dq, dk, dv, dr, dgd, dgup, gvec), ((chip_w2a,),) = _gla_bwd(
        dgla.reshape(bsz, tp, 512), u3, o_pre, states, gup_p, gate_bias, gla_norm_g, [("exchange", [part_w2a])])
    pieces = [a.reshape(r, a.shape[-1]) for a in (dcv, dcg, dq, dk, dv, dr)]
    g_wint, ((chip_w2b, chip_wout),) = _grad_w_in(
        pieces, dgd, s0bf, tp, [("exchange", [part_w2b, part_wout])])
    big_win = jnp.stack([g_wint[j * sh_in:(j + 1) * sh_in] for j in range(4)])
    (half_w1, half_w2), ((pair_win,),) = _add_chips_many(
        [(big_w1, pair_w1, [chip_w1]), (big_w2, pair_w2, [chip_w2a, chip_w2b])], [("pair", [big_win])])
    ds0, ((small_early,), (chip_win,), (sib_w1, sib_w2)) = _in_proj_bwd(
        pieces, dgd, w_int, dt1, tp,
        [("all", [_pack_small_early(acc2, acc1, cacc, gvec, dcw, dgup)]),
         ("exchange", add_pair(big_win, pair_win)), ("swap", [half_w1, half_w2])])
    grad_x, dmeta, lvec = _ln_in_bwd(ds0.reshape(bsz, tp, d), x, meta_full, ln_in_g2)

    half_win = _add_chips(big_win, pair_win, [chip_win], pos_arr)
    half_wout = _add_chips(big_wout, pair_wout, [chip_wout], pos_arr)
    sib_win, sib_wout = _pair_swap([half_win, half_wout])
    grads = {}
    weights = dict(meta_tokens=meta_tokens, ln_in_g=ln_in_g, ln_in_b=ln_in_b, w_in=w_in, conv_w=conv_w, conv_b=conv_b,
                   conv_ln_g=conv_ln_g, conv_ln_b=conv_ln_b, gate_up=gate_up, gate_bias=gate_bias,
                   gla_norm_g=gla_norm_g, w_out=w_out, ln1_g=ln1_g, ln1_b=ln1_b, w_ff1=w_ff1, w_ff2=w_ff2,
                   ln2_g=ln2_g, ln2_b=ln2_b)
    moms = dict(meta_tokens=(m_meta_tokens, v_meta_tokens), ln_in_g=(m_ln_in_g, v_ln_in_g),
                ln_in_b=(m_ln_in_b, v_ln_in_b), w_in=(m_w_in, v_w_in), conv_w=(m_conv_w, v_conv_w),
                conv_b=(m_conv_b, v_conv_b), conv_ln_g=(m_conv_ln_g, v_conv_ln_g),
                conv_ln_b=(m_conv_ln_b, v_conv_ln_b), gate_up=(m_gate_up, v_gate_up),
                gate_bias=(m_gate_bias, v_gate_bias), gla_norm_g=(m_gla_norm_g, v_gla_norm_g),
                w_out=(m_w_out, v_w_out), ln1_g=(m_ln1_g, v_ln1_g), ln1_b=(m_ln1_b, v_ln1_b),
                w_ff1=(m_w_ff1, v_w_ff1), w_ff2=(m_w_ff2, v_w_ff2), ln2_g=(m_ln2_g, v_ln2_g),
                ln2_b=(m_ln2_b, v_ln2_b))
    names = list(weights)
    big_names = ("w_in", "w_out", "w_ff1", "w_ff2")
    delta, new_m, new_v = {}, {}, {}
    def adamw_big(k, own, sib, fuse=NO_FUSE):
        to2d = (lambda a: a[0].T) if k == "w_in" else (lambda a: a[0])
        back = (lambda a: a.T[None]) if k == "w_in" else (lambda a: a[None])
        res, got = _adamw_halves(to2d(weights[k]), own, sib, to2d(moms[k][0]), to2d(moms[k][1]), fuse)
        grads[k], delta[k], new_m[k], new_v[k] = [back(a) for a in res]
        return got

    ((small_late,),) = adamw_big("w_in", half_win, sib_win, [("all", [_pack_small_late(lvec, dmeta)])])
    adamw_big("w_ff1", half_w1, sib_w1)
    adamw_big("w_ff2", half_w2, sib_w2)
    adamw_big("w_out", half_wout, sib_wout)
    red, loss = _sum_small(small_early, small_late)
    small_names = [k for k in names if k not in big_names]
    two = lambda a: a.reshape(-1, a.shape[-1])

    def my_cols(full, rows, width):
        return lax.dynamic_slice(full, (0, chip * width), (rows, width))

    sharded = {
        "meta_tokens": my_cols(red[SMALL_AT["meta_tokens"][0]:SMALL_AT["meta_tokens"][0] + N_META], N_META, 256),
        "conv_w": my_cols(red[SMALL_CONV_W:SMALL_CONV_W + 16].reshape(32, D_CONV), CONV_WIDTH, 128),
        "gate_up": my_cols(red[SMALL_GATE_UP:SMALL_GATE_UP + 4].reshape(GLA_RANK, 256), GLA_RANK, 64)}
    upd = _adamw_small(red, sharded, {k: (two(weights[k]), two(moms[k][0]), two(moms[k][1])) for k in small_names})
    for k in small_names:
        shp = weights[k].shape
        grads[k], delta[k], new_m[k], new_v[k] = [a.reshape(shp) for a in upd[k]]
    loss = loss.reshape(())

    return (loss, grad_x, *[grads[k] for k in names], *[delta[k] for k in names],
            *[new_m[k] for k in names], *[new_v[k] for k in names])
```

```python
import functools

import jax
import jax.numpy as jnp
from jax import lax
from jax.experimental import pallas as pl
from jax.experimental.pallas import tpu as pltpu

F32 = jnp.float32
BF16 = jnp.bfloat16

D_MODEL = 1024
N_META = 16
D_CONV = 512
CONV_WIDTH = 31
GLA_HEADS = 4
GLA_DV = 128
GLA_DK = 64
GLA_RANK = 16
GLA_TAU = 16.0
CHUNK = 64
D_FF = 4096
LN_EPS = 1e-5
ALPHA = 2.0 ** 0.25
D_IN = 2576
D_IN_PAD = 2688
PAD = CHUNK - N_META
HEAD = PAD + N_META
Q_SCALE = GLA_DK ** -0.5
ADAM_LR, ADAM_B1, ADAM_B2, ADAM_EPS, ADAM_WD, ADAM_STEP = 0.001, 0.9, 0.999, 1e-08, 0.01, 10
HALF = D_MODEL // 2
VMEM_LIMIT = 56 * 1024 * 1024
MESH = pl.DeviceIdType.MESH

C_VAL, C_GATE, C_Q, C_K, C_V, C_R, C_GD = 0, 512, 1024, 1280, 1536, 2048, 2560

SMALL_AT = {"loss": (0, 1, 0, 1024), "ln_in_g": (1, 1, 0, 1024), "ln_in_b": (2, 1, 0, 1024), "ln1_g": (3, 1, 0, 1024),
            "ln1_b": (4, 1, 0, 1024), "ln2_g": (5, 1, 0, 1024), "ln2_b": (6, 1, 0, 1024), "conv_b": (7, 1, 0, 512),
            "conv_ln_g": (7, 1, 512, 512), "conv_ln_b": (8, 1, 0, 512), "gate_bias": (8, 1, 512, 256),
            "gla_norm_g": (8, 1, 768, 128), "meta_tokens": (40, 16, 0, 1024)}
SMALL_CONV_W, SMALL_GATE_UP = 16, 32
SMALL_ROWS = 56


def _params(sem=None, **kw):
    return pltpu.CompilerParams(dimension_semantics=sem, vmem_limit_bytes=VMEM_LIMIT, **kw)


def _row_tile(tp):
    for t in (704, 352, 192, 64):
        if tp % t == 0:
            return t
    raise ValueError(tp)


def _reduce_tile(tp, big):
    for t in ((2112, 1056, 704) if big else (1056, 704)) + (352, 192, 64):
        if tp % t == 0:
            return t
    raise ValueError(tp)


def _sub_rows(tm):
    return [slice(0, tm)]


def _dot(a, b, dims, precision=None):
    return lax.dot_general(a, b, (dims, ((), ())), preferred_element_type=F32, precision=precision)


def _nn(a, b, **kw):
    return _dot(a, b, ((1,), (0,)), **kw)


def _nt(a, b, **kw):
    return _dot(a, b, ((1,), (1,)), **kw)


def _tn(a, b, **kw):
    return _dot(a, b, ((0,), (0,)), **kw)


def _sigmoid(x):
    return 1.0 / (1.0 + jnp.exp(-x))


def _log_sigmoid(z):
    return jnp.minimum(z, 0.0) - jnp.log(1.0 + jnp.exp(-jnp.abs(z)))


def _ln_stats(t):
    mu = jnp.mean(t, axis=-1, keepdims=True)
    d = t - mu
    var = jnp.mean(d * d, axis=-1, keepdims=True)
    rstd = lax.rsqrt(var + LN_EPS)
    return d * rstd, rstd


def _ln_bwd(dxhat, xhat, rstd):
    m1 = jnp.mean(dxhat, axis=-1, keepdims=True)
    m2 = jnp.mean(dxhat * xhat, axis=-1, keepdims=True)
    return rstd * (dxhat - m1 - xhat * m2)


def _mesh_pos():
    return lax.axis_index("x"), lax.axis_index("y"), lax.axis_index("c")


ANY = pl.BlockSpec(memory_space=pl.ANY)


def _gather_sems(n):
    return [pltpu.SemaphoreType.DMA((n, 3))] * 4


def _gather_steps(ins, outs, sems, ranges=None):
    n = len(ins)
    send, recv, fsend, frecv = sems
    ranges = ranges or [(0, ref.shape[0]) for ref in ins]
    x, y, c = _mesh_pos()
    me = 2 * x + y
    sibling = (x, y, 1 - c)
    chips = [(1 - x, y), (x, 1 - y), (1 - x, 1 - y)]
    chip_idx = [2 * px + py for px, py in chips]
    mine = [pl.ds(pl.multiple_of(r0 + c * (nr // 2), 16), nr // 2) for r0, nr in ranges]
    other = [pl.ds(pl.multiple_of(r0 + (1 - c) * (nr // 2), 16), nr // 2) for r0, nr in ranges]
    pairs = [(a, k) for a in range(n) for k in range(3)]

    def ici(a, k, slab):
        return pltpu.make_async_remote_copy(
            src_ref=ins[a].at[mine[a], :], dst_ref=outs[a].at[slab, mine[a], :],
            send_sem=send.at[a, k], recv_sem=recv.at[a, k], device_id=(*chips[k], c), device_id_type=MESH)

    def forward(a, k, rows):
        blk = outs[a].at[chip_idx[k], rows[a], :]
        return pltpu.make_async_remote_copy(
            src_ref=blk, dst_ref=blk, send_sem=fsend.at[a, k], recv_sem=frecv.at[a, k],
            device_id=sibling, device_id_type=MESH)

    def start():
        for a, k in pairs:
            ici(a, k, me).start()

    def relay():
        for a, k in pairs:
            ici(a, k, chip_idx[k]).wait_recv()
            forward(a, k, mine).start()

    def finish():
        for a, k in pairs:
            forward(a, k, other).wait_recv()
        for a, k in pairs:
            ici(a, k, me).wait_send()
            forward(a, k, mine).wait_send()

    return start, relay, finish


def _place_own(gathered, shards):
    chip = 2 * lax.axis_index("x") + lax.axis_index("y")
    return [lax.dynamic_update_slice(g, s[None], (chip, 0, 0)) for g, s in zip(gathered, shards)]


def _gather_small_steps(ins, outs, sems):
    send, recv = sems
    x, y, c = _mesh_pos()
    chips = [(1 - x, y), (x, 1 - y), (1 - x, 1 - y)]

    def copy(a, k, slot):
        return pltpu.make_async_remote_copy(
            src_ref=ins[a], dst_ref=outs[a].at[slot], send_sem=send.at[a, k], recv_sem=recv.at[a, k],
            device_id=(*chips[k], c), device_id_type=MESH)

    def start():
        for a in range(len(ins)):
            for k in range(3):
                copy(a, k, 2 * x + y).start()

    def finish():
        for a in range(len(ins)):
            for k, (px, py) in enumerate(chips):
                copy(a, k, 2 * px + py).wait_recv()
            for k in range(3):
                copy(a, k, 2 * x + y).wait_send()

    return start, finish


def _pair_swap_steps(ins, outs, sems):
    send, recv = sems
    x, y, c = _mesh_pos()
    cps = [pltpu.make_async_remote_copy(
        src_ref=ins[a], dst_ref=outs[a], send_sem=send.at[a], recv_sem=recv.at[a],
        device_id=(x, y, 1 - c), device_id_type=MESH) for a in range(len(ins))]

    def start():
        for cp in cps:
            cp.start()

    def finish():
        for cp in cps:
            cp.wait()

    return start, finish


def _pair_exchange_steps(ins, outs, sems):
    send, recv = sems
    x, y, c = _mesh_pos()
    other = pl.ds(pl.multiple_of((1 - c) * HALF, 128), HALF)
    cps = [pltpu.make_async_remote_copy(
        src_ref=ins[a].at[:, :, other], dst_ref=outs[a], send_sem=send.at[a], recv_sem=recv.at[a],
        device_id=(x, y, 1 - c), device_id_type=MESH) for a in range(len(ins))]

    def start():
        for cp in cps:
            cp.start()

    def finish():
        for cp in cps:
            cp.wait()

    return start, finish


def _pair_exchange_shapes(grads):
    return [jax.ShapeDtypeStruct(g.shape[:2] + (HALF,), g.dtype) for g in grads]


def _pair_exchange(grads):
    n = len(grads)

    def body(*refs):
        start, finish = _pair_exchange_steps(refs[:n], refs[n:2 * n], refs[2 * n:])
        start()
        finish()

    return pl.pallas_call(
        body, name="grad_pair_exchange", out_shape=_pair_exchange_shapes(grads),
        in_specs=[ANY] * n, out_specs=[ANY] * n,
        scratch_shapes=[pltpu.SemaphoreType.DMA((n,)), pltpu.SemaphoreType.DMA((n,))],
    )(*grads)


NO_FUSE = ()


def _fuse_plan(kind, items):
    n = len(items)
    if kind == "gather":
        shards = [it[0] for it in items]
        bufs = [it[3] for it in items if it[3] is not None]
        alias, b = {}, 0
        for a, it in enumerate(items):
            if it[3] is not None:
                alias[n + b] = a
                b += 1
        ranges = [(it[1], it[2]) for it in items]
        return (shards + bufs, [jax.ShapeDtypeStruct((4,) + s.shape, s.dtype) for s in shards], alias, _gather_sems(n),
                lambda i, o, s: _gather_steps(i[:n], o, s, ranges))
    if kind == "exchange":
        return list(items), _chip_exchange_shapes(items), {}, _chip_exchange_sems(n), _chip_exchange_steps
    if kind == "small":
        return (list(items), [jax.ShapeDtypeStruct((4,) + s.shape, s.dtype) for s in items], {},
                _chip_exchange_sems(n), _gather_small_steps)
    if kind == "all":
        (pack,) = items
        return ([pack], [jax.ShapeDtypeStruct((8 * pack.shape[0], pack.shape[1]), pack.dtype)], {},
                [pltpu.SemaphoreType.DMA((7,)), pltpu.SemaphoreType.DMA((7,)), pltpu.SemaphoreType.DMA((1,))],
                _allgather_all_steps)
    pair_sems = [pltpu.SemaphoreType.DMA((n,)), pltpu.SemaphoreType.DMA((n,))]
    if kind == "swap":
        return (list(items), [jax.ShapeDtypeStruct(h.shape, h.dtype) for h in items], {}, pair_sems, _pair_swap_steps)
    return list(items), _pair_exchange_shapes(items), {}, pair_sems, _pair_exchange_steps


def _fused_call(body, fuse, *, name, grid, in_specs, out_specs, out_shape, scratch_shapes, operands,
                late_relay=False):
    plans = [_fuse_plan(kind, list(items)) for kind, items in fuse if len(items)]
    n_in, n_out, n_s = len(in_specs), len(out_shape), len(scratch_shapes)
    comm = [a for p in plans for a in p[0]]
    shapes = [s for p in plans for s in p[1]]
    nc, no = len(comm), len(shapes)
    aliases, i_at, o_at = {}, n_in, n_out
    for p in plans:
        for i, o in p[2].items():
            aliases[i_at + i] = o_at + o
        i_at, o_at = i_at + len(p[0]), o_at + len(p[1])

    def wrapped(*refs):
        o0 = n_in + nc
        s0 = o0 + n_out + no
        i_at, o_at, sem_at, steps = n_in, o0 + n_out, s0 + n_s, []
        for p in plans:
            steps.append(p[4](refs[i_at:i_at + len(p[0])], refs[o_at:o_at + len(p[1])], refs[sem_at:sem_at + len(p[3])]))
            i_at, o_at, sem_at = i_at + len(p[0]), o_at + len(p[1]), sem_at + len(p[3])
        first, last = _grid_ends(grid)
        for st in steps:
            pl.when(first)(st[0])
        if not late_relay:
            for st in steps:
                for mid in st[1:-1]:
                    pl.when(last)(mid)
        body(*refs[:n_in], *refs[o0:o0 + n_out], *refs[s0:s0 + n_s])
        for st in steps:
            for step in (st[1:] if late_relay else st[-1:]):
                pl.when(last)(step)

    res = pl.pallas_call(
        wrapped if plans else body, name=name, grid=grid, in_specs=list(in_specs) + [ANY] * nc,
        out_specs=list(out_specs) + [ANY] * no, out_shape=list(out_shape) + shapes,
        scratch_shapes=list(scratch_shapes) + [s for p in plans for s in p[3]], input_output_aliases=aliases,
        compiler_params=_params(("arbitrary",) * len(grid)))(*operands, *comm)
    outs, got, results = list(res[:n_out]), list(res[n_out:]), []
    for p in plans:
        results.append(got[:len(p[1])])
        got = got[len(p[1]):]
    return outs, results


def _chip_exchange_sems(n):
    return [pltpu.SemaphoreType.DMA((n, 3))] * 2


def _chip_exchange_shapes(parts):
    return [jax.ShapeDtypeStruct((3,) + p.shape[1:], p.dtype) for p in parts]


def _chip_exchange_steps(ins, outs, sems):
    send, recv = sems
    x, y, c = _mesh_pos()
    chips = [(1 - x, y), (x, 1 - y), (1 - x, 1 - y)]
    cps = [pltpu.make_async_remote_copy(
        src_ref=ins[a].at[2 * px + py], dst_ref=outs[a].at[k], send_sem=send.at[a, k], recv_sem=recv.at[a, k],
        device_id=(px, py, c), device_id_type=MESH) for a in range(len(ins)) for k, (px, py) in enumerate(chips)]

    def start():
        for cp in cps:
            cp.start()

    def finish():
        for cp in cps:
            cp.wait()

    return start, finish


def _chip_exchange(parts):
    n = len(parts)

    def body(*refs):
        start, finish = _chip_exchange_steps(refs[:n], refs[n:2 * n], refs[2 * n:])
        start()
        finish()

    return pl.pallas_call(
        body, name="grad_chip_exchange", out_shape=_chip_exchange_shapes(parts),
        in_specs=[ANY] * n, out_specs=[ANY] * n, scratch_shapes=_chip_exchange_sems(n),
    )(*parts)


def _pair_swap(halves):
    n = len(halves)

    def body(*refs):
        start, finish = _pair_swap_steps(refs[:n], refs[n:2 * n], refs[2 * n:])
        start()
        finish()

    return pl.pallas_call(
        body, name="grad_pair_swap",
        out_shape=[jax.ShapeDtypeStruct(h.shape, h.dtype) for h in halves],
        in_specs=[ANY] * n, out_specs=[ANY] * n,
        scratch_shapes=[pltpu.SemaphoreType.DMA((n,)), pltpu.SemaphoreType.DMA((n,))],
    )(*halves)


def _allgather_all_steps(ins, outs, sems):
    send_sems, recv_sems, local_sem = sems
    x_ref, out_ref = ins[0], outs[0]
    m_per = x_ref.shape[0]
    x, y, c = _mesh_pos()
    me, sibling = (x, y, c), (x, y, 1 - c)
    chips = [(1 - x, y), (x, 1 - y), (1 - x, 1 - y)]

    def rows(px, py, pc):
        return out_ref.at[pl.ds(pl.multiple_of((4 * px + 2 * py + pc) * m_per, 8), m_per), :]

    def copy(k, block, to, src=None):
        return pltpu.make_async_remote_copy(
            src_ref=rows(*block) if src is None else src, dst_ref=rows(*block),
            send_sem=send_sems.at[k], recv_sem=recv_sems.at[k], device_id=to, device_id_type=MESH)

    mine = pltpu.make_async_copy(x_ref, rows(*me), local_sem.at[0])
    first = [copy(0, me, sibling, src=x_ref)]
    first += [copy(1 + j, me, (*chip, c), src=x_ref) for j, chip in enumerate(chips)]
    passed = [copy(4 + j, (*chip, c), sibling) for j, chip in enumerate(chips)]

    def start():
        mine.start()
        for cp in first:
            cp.start()

    def relay():
        for j, chip in enumerate(chips):
            copy(1 + j, (*chip, c), me).wait_recv()
            passed[j].start()

    def finish():
        copy(0, sibling, me).wait_recv()
        for j, chip in enumerate(chips):
            copy(4 + j, (*chip, 1 - c), me).wait_recv()
        for cp in first + passed:
            cp.wait_send()
        mine.wait()

    return start, relay, finish


def _add_pair(g, got, c_arr, splits=None):
    _, rows, _ = g.shape
    splits = splits or [rows]

    def body(c_ref, g_ref, r_ref, *o_refs):
        at = 0
        for o_ref, n in zip(o_refs, splits):
            o_ref[...] = (g_ref[:, at:at + n, :] + r_ref[:, at:at + n, :]).astype(BF16)
            at += n

    return pl.pallas_call(
        body, name="grad_add_pair", out_shape=[jax.ShapeDtypeStruct((4, n, HALF), BF16) for n in splits],
        grid_spec=pltpu.PrefetchScalarGridSpec(
            num_scalar_prefetch=1, grid=(4,),
            in_specs=[pl.BlockSpec((1, rows, HALF), lambda j, c: (j, 0, c[0])),
                      pl.BlockSpec((1, rows, HALF), lambda j, c: (j, 0, 0))],
            out_specs=[pl.BlockSpec((1, n, HALF), lambda j, c: (j, 0, 0)) for n in splits]),
        compiler_params=_params(("arbitrary",)),
    )(c_arr, g, got)


def _add_chips(g, pair_got, chip_gots, pos_arr):
    _, rows, _ = g.shape

    def body(pos_ref, g_ref, p_ref, *refs):
        o_ref, at = refs[-1], 0
        for r_ref in refs[:-1]:
            n = r_ref.shape[1]
            own = g_ref[0, at:at + n, :] + p_ref[0, at:at + n, :]
            o_ref[at:at + n, :] = ((own + r_ref[0].astype(F32)) + r_ref[1].astype(F32)) + r_ref[2].astype(F32)
            at += n

    return pl.pallas_call(
        body, name="grad_add_chips", out_shape=jax.ShapeDtypeStruct((rows, HALF), F32),
        grid_spec=pltpu.PrefetchScalarGridSpec(
            num_scalar_prefetch=1, grid=(1,),
            in_specs=[pl.BlockSpec((1, rows, HALF), lambda i, p: (p[0], 0, p[1])),
                      pl.BlockSpec((1, rows, HALF), lambda i, p: (p[0], 0, 0))]
            + [pl.BlockSpec(t.shape, lambda i, p: (0, 0, 0)) for t in chip_gots],
            out_specs=pl.BlockSpec((rows, HALF), lambda i, p: (0, 0))),
        compiler_params=_params(("arbitrary",)),
    )(pos_arr, g, pair_got, *chip_gots)


def _add_chips_many(items, fuse=NO_FUSE):
    n = len(items)
    rows = [it[0].shape[1] for it in items]
    n_got = [len(it[2]) for it in items]

    def body(*refs):
        g_refs, p_refs = refs[:n], refs[n:2 * n]
        got_refs = refs[2 * n:2 * n + sum(n_got)]
        o_refs = refs[2 * n + sum(n_got):3 * n + sum(n_got)]
        scr = refs[3 * n + sum(n_got):]
        x, y, c = _mesh_pos()
        chip = 2 * x + y
        mine = pl.ds(pl.multiple_of(c * HALF, 128), HALF)
        copies = []
        for a in range(n):
            copies.append((pltpu.make_async_copy(g_refs[a].at[chip, :, mine], scr[2 * a], scr[2 * n].at[2 * a]),
                           pltpu.make_async_copy(p_refs[a].at[chip], scr[2 * a + 1], scr[2 * n].at[2 * a + 1])))
        for cg, cp in copies:
            cg.start()
            cp.start()
        at_ref = 0
        for a in range(n):
            copies[a][0].wait()
            copies[a][1].wait()
            at = 0
            for r_ref in got_refs[at_ref:at_ref + n_got[a]]:
                k = r_ref.shape[1]
                own = scr[2 * a][at:at + k, :] + scr[2 * a + 1][at:at + k, :]
                o_refs[a][at:at + k, :] = ((own + r_ref[0].astype(F32)) + r_ref[1].astype(F32)) + r_ref[2].astype(F32)
                at += k
            at_ref += n_got[a]

    gots = [t for it in items for t in it[2]]
    outs, got = _fused_call(
        body, fuse, name="grad_add_chips_many", grid=(1,),
        in_specs=[ANY] * (2 * n) + [pl.BlockSpec(t.shape, lambda i: (0, 0, 0)) for t in gots],
        out_specs=[pl.BlockSpec((r, HALF), lambda i: (0, 0)) for r in rows],
        out_shape=[jax.ShapeDtypeStruct((r, HALF), F32) for r in rows],
        scratch_shapes=[pltpu.VMEM((r, HALF), F32) for r in rows for _ in range(2)] + [pltpu.SemaphoreType.DMA((2 * n,))],
        operands=(*[it[0] for it in items], *[it[1] for it in items], *gots))
    return outs, got


def _pack_small(acc2, acc1, lvec, cacc, gvec, dcw, dgup, dmeta):
    bsz = lvec.shape[0]

    def body(a2_ref, a1_ref, lv_ref, ca_ref, gv_ref, cw_ref, gu_ref, dm_ref, o_ref):
        def put(name, val):
            r0, nr, l0, nl = SMALL_AT[name]
            o_ref[r0:r0 + nr, l0:l0 + nl] = val

        def put_folded(row0, val, per_row):
            w = val.shape[1]
            for i in range(val.shape[0]):
                o_ref[row0 + i // per_row:row0 + i // per_row + 1, (i % per_row) * w:(i % per_row + 1) * w] = val[i:i + 1]

        over_b = lambda f: functools.reduce(lambda a, b: a + b, [f(b) for b in range(bsz)])
        o_ref[...] = jnp.zeros_like(o_ref)
        put("loss", a2_ref[0:1, :])
        put("ln2_g", a2_ref[1:2, :])
        put("ln2_b", a2_ref[2:3, :])
        put("ln1_g", a1_ref[0:1, :])
        put("ln1_b", a1_ref[1:2, :])
        put("ln_in_g", over_b(lambda b: lv_ref[b, 0:1, :]))
        put("ln_in_b", over_b(lambda b: lv_ref[b, 1:2, :]))
        put("conv_b", ca_ref[0:1, :])
        put("conv_ln_g", ca_ref[1:2, :])
        put("conv_ln_b", ca_ref[2:3, :])
        put("gate_bias", over_b(lambda b: gv_ref[b, 0:1, :]))
        put("gla_norm_g", over_b(lambda b: gv_ref[b, 1:2, 0:128] + gv_ref[b, 1:2, 128:256]))
        put_folded(SMALL_CONV_W, over_b(lambda b: cw_ref[b]), D_MODEL // D_CONV)
        put_folded(SMALL_GATE_UP, over_b(lambda b: gu_ref[b, 0:GLA_RANK, :]), D_MODEL // 256)
        put("meta_tokens", over_b(lambda b: dm_ref[b]))

    return pl.pallas_call(
        body, name="pack_small", out_shape=jax.ShapeDtypeStruct((SMALL_ROWS, D_MODEL), F32),
    )(acc2, acc1, lvec, cacc, gvec, dcw, dgup, dmeta)


def _sum_small(gathered):
    def body(g_ref, o_ref, loss_ref):
        acc = g_ref[0:SMALL_ROWS, :]
        for d in range(1, 8):
            acc = acc + g_ref[d * SMALL_ROWS:(d + 1) * SMALL_ROWS, :]
        o_ref[...] = acc
        loss_ref[...] = jnp.sum(acc[0:1, :], axis=1, keepdims=True)

    return pl.pallas_call(
        body, name="sum_small",
        out_shape=[jax.ShapeDtypeStruct((SMALL_ROWS, D_MODEL), F32), jax.ShapeDtypeStruct((1, 1), F32)],
    )(gathered)


def _adamw_math(w, g, m, v):
    c1 = 1.0 - ADAM_B1 ** ADAM_STEP
    c2 = 1.0 - ADAM_B2 ** ADAM_STEP
    mn = ADAM_B1 * m + (1.0 - ADAM_B1) * g
    vn = ADAM_B2 * v + (1.0 - ADAM_B2) * (g * g)
    return -ADAM_LR * ((mn / c1) / (jnp.sqrt(vn / c2) + ADAM_EPS) + ADAM_WD * w), mn, vn


def _adamw_small(red, sharded_grads, params):
    names = list(params)
    sharded = [k for k in names if k in sharded_grads]
    n, ns = len(names), len(sharded)

    def body(*refs):
        red_ref, sg_refs, p_refs, o_refs = refs[0], refs[1:1 + ns], refs[1 + ns:1 + ns + 3 * n], refs[1 + ns + 3 * n:]
        for i, k in enumerate(names):
            w_ref, m_ref, v_ref = p_refs[3 * i:3 * i + 3]
            if k in sharded:
                g = sg_refs[sharded.index(k)][...]
            else:
                r0, nr, l0, nl = SMALL_AT[k]
                g = red_ref[r0:r0 + nr, l0:l0 + nl]
            dl, mn, vn = _adamw_math(w_ref[...], g, m_ref[...], v_ref[...])
            for o_ref, val in zip(o_refs[4 * i:4 * i + 4], (g, dl, mn, vn)):
                o_ref[...] = val

    flat = [a for k in names for a in params[k]]
    res = pl.pallas_call(
        body, name="adamw_small",
        out_shape=[jax.ShapeDtypeStruct(params[k][0].shape, F32) for k in names for _ in range(4)],
    )(red, *[sharded_grads[k] for k in sharded], *flat)
    return {k: tuple(res[4 * i:4 * i + 4]) for i, k in enumerate(names)}


def _adamw_halves(w, own, sib, m, v, fuse=NO_FUSE):
    rows, cols = w.shape
    tile = 128
    by_rows = rows % tile == 0
    per_half = HALF // tile

    def body(w_ref, own_ref, sib_ref, m_ref, v_ref, g_ref, d_ref, mo_ref, vo_ref):
        c = _mesh_pos()[2]
        own, sib = own_ref[...], sib_ref[...]
        if by_rows:
            g = jnp.concatenate([jnp.where(c == 0, own, sib), jnp.where(c == 0, sib, own)], axis=1)
        else:
            g = jnp.where(pl.program_id(0) // per_half == c, own, sib)
        g_ref[...] = g
        d_ref[...], mo_ref[...], vo_ref[...] = _adamw_math(w_ref[...], g, m_ref[...], v_ref[...])

    if by_rows:
        grid = (rows // tile,)
        full = pl.BlockSpec((tile, cols), lambda i: (i, 0))
        half = pl.BlockSpec((tile, HALF), lambda i: (i, 0))
    else:
        grid = (cols // tile,)
        full = pl.BlockSpec((rows, tile), lambda j: (0, j))
        half = pl.BlockSpec((rows, tile), lambda j: (0, j % per_half))
    return _fused_call(
        body, fuse, name="adamw_halves", grid=grid, in_specs=[full, half, half, full, full],
        out_specs=[full] * 4, out_shape=[jax.ShapeDtypeStruct(w.shape, F32)] * 4, scratch_shapes=[],
        operands=(w, own, sib, m, v))


def _ln_in_fwd(x, g, b, fuse=NO_FUSE):
    bsz, s, d = x.shape
    tp = s + HEAD
    nh = 2
    sh = s // nh
    rc = min(256, sh)

    def body(x_ref, g_ref, b_ref, s0_ref, s0b_ref):
        h = pl.program_id(1)
        gg, bb = g_ref[...], b_ref[...]

        def step(i, carry):
            src = pl.ds(pl.multiple_of(i * rc, rc), rc)
            dst = pl.ds(pl.multiple_of(HEAD + h * sh + i * rc, 64), rc)
            xh, _ = _ln_stats(x_ref[0, src, :])
            val = xh * gg + bb
            s0_ref[0, dst, :] = val
            s0b_ref[0, dst, :] = val.astype(BF16)
            return carry

        lax.fori_loop(0, sh // rc, step, 0)

    full = lambda bi, hi: (bi, 0, 0)
    (s0, s0b), got = _fused_call(
        body, fuse, name="ln_in_fwd",
        out_shape=[jax.ShapeDtypeStruct((bsz, tp, d), F32), jax.ShapeDtypeStruct((bsz, tp, d), BF16)],
        grid=(bsz, nh),
        in_specs=[pl.BlockSpec((1, sh, d), lambda bi, hi: (bi, hi, 0)),
                  pl.BlockSpec((1, d), lambda bi, hi: (0, 0)),
                  pl.BlockSpec((1, d), lambda bi, hi: (0, 0))],
        out_specs=[pl.BlockSpec((1, tp, d), full)] * 2, scratch_shapes=[], operands=(x, g, b),
        late_relay=True)
    return s0, s0b, got


def _ln_meta(s0, s0b, meta, g, b):
    bsz, tp, d = s0.shape

    def body(s0_in, s0b_in, meta_ref, g_ref, b_ref, s0_ref, s0b_ref):
        mh, _ = _ln_stats(meta_ref[...])
        mv = mh * g_ref[...] + b_ref[...]
        s0_ref[0, 0:PAD, :] = jnp.zeros((PAD, d), F32)
        s0b_ref[0, 0:PAD, :] = jnp.zeros((PAD, d), BF16)
        s0_ref[0, PAD:HEAD, :] = mv
        s0b_ref[0, PAD:HEAD, :] = mv.astype(BF16)

    head = pl.BlockSpec((1, HEAD, d), lambda bi: (bi, 0, 0))
    vec = pl.BlockSpec((1, d), lambda bi: (0, 0))
    return pl.pallas_call(
        body, name="ln_meta", out_shape=[jax.ShapeDtypeStruct(s0.shape, F32), jax.ShapeDtypeStruct(s0b.shape, BF16)],
        grid=(bsz,), in_specs=[head, head, pl.BlockSpec((N_META, d), lambda bi: (0, 0)), vec, vec],
        out_specs=[head, head], input_output_aliases={0: 0, 1: 1},
        compiler_params=_params(("parallel",)),
    )(s0, s0b, meta, g, b)


def _in_proj(s0b, w_int, fuse=NO_FUSE):
    r, d = s0b.shape
    tm = _row_tile(r)

    def body(a_ref, w_ref, o_ref):
        o_ref[...] = _nt(a_ref[...], w_ref[...])

    (u,), got = _fused_call(
        body, fuse, name="in_proj", out_shape=[jax.ShapeDtypeStruct((r, D_IN_PAD), F32)],
        grid=(r // tm,),
        in_specs=[pl.BlockSpec((tm, d), lambda i: (i, 0)), pl.BlockSpec((D_IN_PAD, d), lambda i: (0, 0))],
        out_specs=[pl.BlockSpec((tm, D_IN_PAD), lambda i: (i, 0))], scratch_shapes=[], operands=(s0b, w_int))
    return u, got


def _conv_fwd(u, conv_w, conv_b, fuse=NO_FUSE):
    bsz, tp, _ = u.shape
    nchunk = tp // CHUNK
    win = CHUNK + 32
    nct = D_CONV // 128

    def body(cv_ref, cg_ref, w_ref, cb_ref, hc_ref, h_scr, win_scr):
        h_scr[0:32, :] = jnp.zeros((32, 128), F32)
        h_scr[32:32 + tp, :] = cv_ref[0] * _sigmoid(cg_ref[0])
        cb = cb_ref[...]

        def step(n, carry):
            r0 = pl.multiple_of(n * CHUNK, CHUNK)
            win_scr[...] = h_scr[pl.ds(r0, win), :]
            acc = jnp.zeros((CHUNK, 128), F32)
            for j in range(CONV_WIDTH):
                acc = acc + w_ref[j:j + 1, :] * win_scr[2 + j:2 + j + CHUNK, :]
            hc_ref[0, pl.ds(r0, CHUNK), :] = acc + cb
            return carry

        lax.fori_loop(0, nchunk, step, 0)

    (hc,), got = _fused_call(
        body, fuse, name="conv_fwd", out_shape=[jax.ShapeDtypeStruct((bsz, tp, D_CONV), F32)],
        grid=(bsz, nct),
        in_specs=[pl.BlockSpec((1, tp, 128), lambda bi, t: (bi, 0, C_VAL // 128 + t)),
                  pl.BlockSpec((1, tp, 128), lambda bi, t: (bi, 0, C_GATE // 128 + t)),
                  pl.BlockSpec((32, 128), lambda bi, t: (0, t)),
                  pl.BlockSpec((1, 128), lambda bi, t: (0, t))],
        out_specs=[pl.BlockSpec((1, tp, 128), lambda bi, t: (bi, 0, t))],
        scratch_shapes=[pltpu.VMEM((tp + 32, 128), F32), pltpu.VMEM((win, 128), F32)],
        operands=(u, u, conv_w, conv_b))
    return hc, got


def _gla_group(nchunk):
    return 11 if nchunk % 11 == 0 else nchunk


def _bdot(a, b, ca, cb, precision=None):
    return lax.dot_general(a, b, (((ca,), (cb,)), ((0,), (0,))), preferred_element_type=F32, precision=precision)


def _bnn(a, b, **kw):
    return _bdot(a, b, 2, 1, **kw)


def _bnt(a, b, **kw):
    return _bdot(a, b, 2, 2, **kw)


def _gla_consts(nb):
    row = lax.broadcasted_iota(jnp.int32, (nb, CHUNK, CHUNK), 1)
    col = lax.broadcasted_iota(jnp.int32, (nb, CHUNK, CHUNK), 2)
    lane = lax.broadcasted_iota(jnp.int32, (1, 1, 128), 2)
    return row >= col, row <= col, [lane < GLA_DK, lane >= GLA_DK]


def _gla_group_terms(g, nb, q_ref, k_ref, gd_ref, gup_ref, gb_ref, tril):
    m = nb * CHUNK
    rows = pl.ds(pl.multiple_of(g * m, CHUNK), m)
    z = _nn(gd_ref[0, rows, :].astype(BF16), gup_ref[...]) + gb_ref[...]
    valid = g * m + lax.broadcasted_iota(jnp.int32, (m, 1), 0) >= PAD
    lg = jnp.where(valid, _log_sigmoid(z) * (1.0 / GLA_TAU), 0.0)
    bcum = _bnn(tril.astype(F32), lg.reshape(nb, CHUNK, 128), precision=lax.Precision.HIGHEST)
    blast = bcum[:, CHUNK - 1:CHUNK, :]
    eb = jnp.exp(bcum)
    enb = jnp.exp(-bcum)
    erest = jnp.exp(blast - bcum)
    q = (q_ref[0, rows, :] * Q_SCALE).reshape(nb, CHUNK, 128)
    k = k_ref[0, rows, :].reshape(nb, CHUNK, 128)
    return rows, valid, z, eb, enb, erest, jnp.exp(blast), q * eb, k * enb, k * erest


def _grid_ends(grid):
    ids = [pl.program_id(i) for i in range(len(grid))]
    first = functools.reduce(jnp.logical_and, [i == 0 for i in ids])
    last = functools.reduce(jnp.logical_and, [i == g - 1 for i, g in zip(ids, grid)])
    return first, last


def _gla_fwd(u, gup, gbias, gnorm, fuse=NO_FUSE):
    bsz, tp, _ = u.shape
    nchunk = tp // CHUNK
    nb = _gla_group(nchunk)

    def body(q_ref, k_ref, v_ref, r_ref, gd_ref, gup_ref, gb_ref, gn_ref, out_ref, o_ref, st_ref, s_scr):
        tril, _, hmask = _gla_consts(nb)
        s_scr[...] = jnp.zeros_like(s_scr)
        gn = gn_ref[...]

        def group(g, carry):
            rows, _, _, _, _, _, dec, qe, ke, kd = _gla_group_terms(g, nb, q_ref, k_ref, gd_ref, gup_ref, gb_ref, tril)
            keb, kdb = ke.astype(BF16), kd.astype(BF16)
            for h in range(2):
                cols = slice(h * GLA_DV, (h + 1) * GLA_DV)
                qh = jnp.where(hmask[h], qe, 0.0).astype(BF16)
                vh = v_ref[0, rows, cols].astype(BF16).reshape(nb, CHUNK, GLA_DV)
                a = jnp.where(tril, _bnt(qh, keb), 0.0).astype(BF16)
                st = s_scr[h]
                sts = []
                for n in range(nb):
                    st_ref[0, h, g * nb + n] = st
                    sts.append(st.astype(BF16))
                    st = dec[n] * st + _tn(vh[n], kdb[n])
                s_scr[h] = st
                o = (_bnn(a, vh) + _bnt(qh, jnp.stack(sts))).reshape(nb * CHUNK, GLA_DV)
                o_ref[0, rows, cols] = o
                rms = lax.rsqrt(jnp.mean(o * o, axis=-1, keepdims=True) + LN_EPS)
                rh = r_ref[0, rows, cols]
                out_ref[0, rows, cols] = (o * rms * gn * (rh * _sigmoid(rh))).astype(BF16)
            return carry

        lax.fori_loop(0, nchunk // nb, group, 0)

    res, got = _fused_call(
        body, fuse, name="gla_fwd",
        out_shape=[jax.ShapeDtypeStruct((bsz, tp, 512), BF16), jax.ShapeDtypeStruct((bsz, tp, 512), F32),
                   jax.ShapeDtypeStruct((bsz, GLA_HEADS, nchunk, GLA_DV, 128), F32)],
        grid=(bsz, 2),
        in_specs=[pl.BlockSpec((1, tp, 128), lambda bi, p: (bi, 0, C_Q // 128 + p)),
                  pl.BlockSpec((1, tp, 128), lambda bi, p: (bi, 0, C_K // 128 + p)),
                  pl.BlockSpec((1, tp, 256), lambda bi, p: (bi, 0, C_V // 256 + p)),
                  pl.BlockSpec((1, tp, 256), lambda bi, p: (bi, 0, C_R // 256 + p)),
                  pl.BlockSpec((1, tp, 128), lambda bi, p: (bi, 0, C_GD // 128)),
                  pl.BlockSpec((128, 128), lambda bi, p: (0, p)),
                  pl.BlockSpec((1, 128), lambda bi, p: (0, p)),
                  pl.BlockSpec((1, 128), lambda bi, p: (0, 0))],
        out_specs=[pl.BlockSpec((1, tp, 256), lambda bi, p: (bi, 0, p)),
                   pl.BlockSpec((1, tp, 256), lambda bi, p: (bi, 0, p)),
                   pl.BlockSpec((1, 2, nchunk, GLA_DV, 128), lambda bi, p: (bi, p, 0, 0, 0))],
        scratch_shapes=[pltpu.VMEM((2, GLA_DV, 128), F32)],
        operands=(u, u, u, u, u, gup, gbias, gnorm))
    return res[0], res[1], res[2], got


def _out_proj_ln1(hc, gla_out, w_out, s0, cg, cb, g1, b1, fuse=NO_FUSE):
    r, d = s0.shape
    tm = _row_tile(r)

    def body(hc_ref, a_ref, w_ref, s0_ref, cg_ref, cb_ref, g_ref, b_ref, co_ref, xh_ref, rstd_ref, s1b_ref):
        for rs in _sub_rows(tm):
            xc, _ = _ln_stats(hc_ref[rs, :])
            nv = xc * cg_ref[...] + cb_ref[...]
            co = (nv * _sigmoid(nv)).astype(BF16)
            co_ref[rs, :] = co
            mix = _nn(co, w_ref[0:D_CONV, :]) + _nn(a_ref[rs, :], w_ref[D_CONV:, :])
            xh, rstd = _ln_stats(ALPHA * s0_ref[rs, :] + mix)
            xh_ref[rs, :] = xh
            rstd_ref[rs, :] = rstd
            s1b_ref[rs, :] = (xh * g_ref[...] + b_ref[...]).astype(BF16)

    row = lambda n: pl.BlockSpec((tm, n), lambda i: (i, 0))
    vec = lambda n: pl.BlockSpec((1, n), lambda i: (0, 0))
    res, got = _fused_call(
        body, fuse, name="out_proj_ln1",
        out_shape=[jax.ShapeDtypeStruct((r, D_CONV), BF16), jax.ShapeDtypeStruct((r, d), F32),
                   jax.ShapeDtypeStruct((r, 1), F32), jax.ShapeDtypeStruct((r, d), BF16)],
        grid=(r // tm,),
        in_specs=[row(D_CONV), row(512), pl.BlockSpec((d, d), lambda i: (0, 0)), row(d),
                  vec(D_CONV), vec(D_CONV), vec(d), vec(d)],
        out_specs=[row(D_CONV), row(d), row(1), row(d)], scratch_shapes=[],
        operands=(hc, gla_out, w_out, s0, cg, cb, g1, b1))
    return res[0], res[1], res[2], res[3], got


def _ffn1(s1b, w1, fuse=NO_FUSE):
    r, d = s1b.shape
    tm = _row_tile(r)
    ns, _, wn = w1.shape

    def body(a_ref, w_ref, o_ref):
        a = a_ref[...]
        for j in range(ns):
            o_ref[:, j * wn:(j + 1) * wn] = jnp.maximum(_nn(a, w_ref[j]), 0.0).astype(BF16)

    (ra,), got = _fused_call(
        body, fuse, name="ffn1", out_shape=[jax.ShapeDtypeStruct((r, D_FF), BF16)], grid=(r // tm,),
        in_specs=[pl.BlockSpec((tm, d), lambda i: (i, 0)), pl.BlockSpec(w1.shape, lambda i: (0, 0, 0))],
        out_specs=[pl.BlockSpec((tm, D_FF), lambda i: (i, 0))], scratch_shapes=[], operands=(s1b, w1))
    return ra, got


def _ffn2_ln2_loss(ra, w2, xhat1, g1, b1, g2, b2, tgt, tp):
    r, d = xhat1.shape
    tm = _row_tile(tp)
    per = tp // tm

    def body(ra_ref, w_ref, xh1_ref, g1_ref, b1_ref, g2_ref, b2_ref, tgt_ref, dt_ref, dtb_ref, acc_ref, t_ref, sem):
        i = pl.program_id(0)
        b, j = i // per, i % per

        @pl.when(i == 0)
        def _():
            acc_ref[...] = jnp.zeros_like(acc_ref)

        head_copy = pltpu.make_async_copy(tgt_ref.at[b, pl.ds(0, tm - HEAD), :], t_ref.at[pl.ds(HEAD, tm - HEAD), :], sem)
        body_copy = pltpu.make_async_copy(
            tgt_ref.at[b, pl.ds(pl.multiple_of(jnp.maximum(j * tm - HEAD, 0), 64), tm), :], t_ref, sem)

        @pl.when(j == 0)
        def _():
            t_ref[0:HEAD, :] = jnp.zeros((HEAD, d), F32)
            head_copy.start()

        pl.when(j > 0)(body_copy.start)

        sums = [jnp.zeros((1, d), F32)] * 3
        for rs in _sub_rows(tm):
            rb = ra_ref[rs, :]
            f = _nn(rb * rb, w_ref[...])
            if rs.start == 0:
                pl.when(j == 0)(head_copy.wait)
                pl.when(j > 0)(body_copy.wait)
            s1 = xh1_ref[rs, :] * g1_ref[...] + b1_ref[...]
            xh2, rstd2 = _ln_stats(ALPHA * s1 + f)
            y = xh2 * g2_ref[...] + b2_ref[...]
            rowid = (i % per) * tm + rs.start + lax.broadcasted_iota(jnp.int32, (rs.stop - rs.start, 1), 0)
            e = jnp.where(rowid >= HEAD, y - t_ref[rs, :], 0.0)
            dy = e * (1.0 / d)
            dt2 = _ln_bwd(dy * g2_ref[...], xh2, rstd2)
            dt_ref[rs, :] = dt2
            dtb_ref[rs, :] = dt2.astype(BF16)
            sums = [sums[0] + (0.5 / d) * jnp.sum(e * e, axis=0, keepdims=True),
                    sums[1] + jnp.sum(dy * xh2, axis=0, keepdims=True), sums[2] + jnp.sum(dy, axis=0, keepdims=True)]
        for k in range(3):
            acc_ref[k:k + 1, :] += sums[k]

    row = lambda n: pl.BlockSpec((tm, n), lambda i: (i, 0))
    vec = pl.BlockSpec((1, d), lambda i: (0, 0))
    return pl.pallas_call(
        body, name="ffn2_ln2_loss",
        out_shape=[jax.ShapeDtypeStruct((r, d), F32), jax.ShapeDtypeStruct((r, d), BF16),
                   jax.ShapeDtypeStruct((8, d), F32)],
        grid=(r // tm,),
        in_specs=[row(D_FF), pl.BlockSpec((D_FF, d), lambda i: (0, 0)), row(d), vec, vec, vec, vec, ANY],
        out_specs=[row(d), row(d), pl.BlockSpec((8, d), lambda i: (0, 0))],
        scratch_shapes=[pltpu.VMEM((tm, d), F32), pltpu.SemaphoreType.DMA],
        compiler_params=_params(("arbitrary",)),
    )(ra, w2, xhat1, g1, b1, g2, b2, tgt)


def _ffn_bwd_da(dt2b, w2, ra):
    r, d = dt2b.shape
    tm = _row_tile(r)

    def body(g_ref, w_ref, ra_ref, o_ref):
        o_ref[...] = (_nt(g_ref[...], w_ref[...]) * (2.0 * ra_ref[...].astype(F32))).astype(BF16)

    return pl.pallas_call(
        body, name="ffn_bwd_da", out_shape=jax.ShapeDtypeStruct((r, D_FF), BF16),
        grid=(r // tm,),
        in_specs=[pl.BlockSpec((tm, d), lambda i: (i, 0)), pl.BlockSpec((D_FF, d), lambda i: (0, 0)),
                  pl.BlockSpec((tm, D_FF), lambda i: (i, 0))],
        out_specs=pl.BlockSpec((tm, D_FF), lambda i: (i, 0)),
        compiler_params=_params(("parallel",)),
    )(dt2b, w2, ra)


def _ffn_bwd_ln1(da, w1, dt2, xhat1, rstd1, g1):
    r, d = dt2.shape
    tm = _row_tile(r)

    def body(da_ref, w_ref, dt2_ref, xh_ref, rstd_ref, g_ref, dt_ref, dtb_ref, acc_ref):
        @pl.when(pl.program_id(0) == 0)
        def _():
            acc_ref[...] = jnp.zeros_like(acc_ref)

        sums = [jnp.zeros((1, d), F32)] * 2
        for rs in _sub_rows(tm):
            ds1 = ALPHA * dt2_ref[rs, :]
            for j in range(w1.shape[0]):
                ds1 = ds1 + _nt(da_ref[rs, j * w1.shape[2]:(j + 1) * w1.shape[2]], w_ref[j])
            xh = xh_ref[rs, :]
            dt1 = _ln_bwd(ds1 * g_ref[...], xh, rstd_ref[rs, :])
            dt_ref[rs, :] = dt1
            dtb_ref[rs, :] = dt1.astype(BF16)
            sums = [sums[0] + jnp.sum(ds1 * xh, axis=0, keepdims=True), sums[1] + jnp.sum(ds1, axis=0, keepdims=True)]
        for k in range(2):
            acc_ref[k:k + 1, :] += sums[k]

    row = lambda n: pl.BlockSpec((tm, n), lambda i: (i, 0))
    return pl.pallas_call(
        body, name="ffn_bwd_ln1",
        out_shape=[jax.ShapeDtypeStruct((r, d), F32), jax.ShapeDtypeStruct((r, d), BF16),
                   jax.ShapeDtypeStruct((8, d), F32)],
        grid=(r // tm,),
        in_specs=[row(D_FF), pl.BlockSpec(w1.shape, lambda i: (0, 0, 0)), row(d), row(d), row(1),
                  pl.BlockSpec((1, d), lambda i: (0, 0))],
        out_specs=[row(d), row(d), pl.BlockSpec((8, d), lambda i: (0, 0))],
        compiler_params=_params(("arbitrary",)),
    )(da, w1, dt2, xhat1, rstd1, g1)


def _matmul_tn(lhs, rhs, bm, square_lhs=False, name="matmul_tn", fuse=NO_FUSE):
    r, m = lhs.shape
    n = rhs.shape[1]
    tk = _reduce_tile(r, True)

    def body(a_ref, b_ref, o_ref):
        @pl.when(pl.program_id(1) == 0)
        def _():
            o_ref[...] = jnp.zeros_like(o_ref)

        a = a_ref[...]
        if square_lhs:
            a = a * a
        o_ref[...] += _tn(a, b_ref[...])

    (out,), got = _fused_call(
        body, fuse, name=name, out_shape=[jax.ShapeDtypeStruct((m, n), F32)], grid=(m // bm, r // tk),
        in_specs=[pl.BlockSpec((tk, bm), lambda i, k: (k, i)), pl.BlockSpec((tk, n), lambda i, k: (k, 0))],
        out_specs=[pl.BlockSpec((bm, n), lambda i, k: (i, 0))], scratch_shapes=[], operands=(lhs, rhs))
    return out, got


def _grad_w_ff1(s1b, da, fuse=NO_FUSE):
    r, d = s1b.shape
    wn = da.shape[1] // 4
    tk = _reduce_tile(r, True)

    def body(a_ref, b_ref, o_ref):
        @pl.when(pl.program_id(1) == 0)
        def _():
            o_ref[...] = jnp.zeros_like(o_ref)

        o_ref[0] += _tn(a_ref[...], b_ref[...])

    (out,), got = _fused_call(
        body, fuse, name="grad_w_ff1", out_shape=[jax.ShapeDtypeStruct((4, d, wn), F32)], grid=(4, r // tk),
        in_specs=[pl.BlockSpec((tk, d), lambda j, k: (k, 0)), pl.BlockSpec((tk, wn), lambda j, k: (k, j))],
        out_specs=[pl.BlockSpec((1, d, wn), lambda j, k: (j, 0, 0))], scratch_shapes=[], operands=(s1b, da))
    return out, got


def _grad_w_out(conv_of, gla_of, dt1b, fuse=NO_FUSE):
    r, n = dt1b.shape
    tk = _reduce_tile(r, True)

    def body(a_ref, b_ref, g_ref, o_ref):
        @pl.when(pl.program_id(1) == 0)
        def _():
            o_ref[...] = jnp.zeros_like(o_ref)

        @pl.when(pl.program_id(0) == 0)
        def _():
            o_ref[...] += _tn(a_ref[...], g_ref[...])

        @pl.when(pl.program_id(0) == 1)
        def _():
            o_ref[...] += _tn(b_ref[...], g_ref[...])

    lhs = pl.BlockSpec((tk, 512), lambda i, k: (k, 0))
    (out,), got = _fused_call(
        body, fuse, name="grad_w_out", out_shape=[jax.ShapeDtypeStruct((2 * 512, n), F32)], grid=(2, r // tk),
        in_specs=[lhs, lhs, pl.BlockSpec((tk, n), lambda i, k: (k, 0))],
        out_specs=[pl.BlockSpec((512, n), lambda i, k: (i, 0))], scratch_shapes=[],
        operands=(conv_of, gla_of, dt1b))
    return out, got


def _out_proj_bwd(dt1b, w_out, hc, cg, cb, fuse=NO_FUSE):
    r, d = dt1b.shape
    tm = _row_tile(r)

    def body(g_ref, w_ref, hc_ref, cg_ref, cb_ref, dhc_ref, dgla_ref, acc_ref):
        @pl.when(pl.program_id(0) == 0)
        def _():
            acc_ref[...] = jnp.zeros_like(acc_ref)

        gg = cg_ref[...]
        sums = [jnp.zeros((1, D_CONV), F32)] * 3
        for rs in _sub_rows(tm):
            dmix = _nt(g_ref[rs, :], w_ref[...])
            dgla_ref[rs, :] = dmix[:, D_CONV:]
            xh, rstd = _ln_stats(hc_ref[rs, :])
            nv = xh * gg + cb_ref[...]
            sig = _sigmoid(nv)
            dn = dmix[:, :D_CONV] * (sig * (1.0 + nv * (1.0 - sig)))
            dhc = _ln_bwd(dn * gg, xh, rstd)
            dhc_ref[rs, :] = dhc
            sums = [sums[0] + jnp.sum(dhc, axis=0, keepdims=True), sums[1] + jnp.sum(dn * xh, axis=0, keepdims=True),
                    sums[2] + jnp.sum(dn, axis=0, keepdims=True)]
        for k in range(3):
            acc_ref[k:k + 1, :] += sums[k]

    row = lambda n: pl.BlockSpec((tm, n), lambda i: (i, 0))
    vec = pl.BlockSpec((1, D_CONV), lambda i: (0, 0))
    res, got = _fused_call(
        body, fuse, name="out_proj_bwd",
        out_shape=[jax.ShapeDtypeStruct((r, D_CONV), F32), jax.ShapeDtypeStruct((r, 512), F32),
                   jax.ShapeDtypeStruct((8, D_CONV), F32)],
        grid=(r // tm,),
        in_specs=[row(d), pl.BlockSpec((d, d), lambda i: (0, 0)), row(D_CONV), vec, vec],
        out_specs=[row(D_CONV), row(512), pl.BlockSpec((8, D_CONV), lambda i: (0, 0))], scratch_shapes=[],
        operands=(dt1b, w_out, hc, cg, cb))
    return res[0], res[1], res[2], got


def _conv_bwd(dhc, u, conv_w, fuse=NO_FUSE):
    bsz, tp, _ = u.shape
    nchunk = tp // CHUNK
    win = CHUNK + 32
    nct = D_CONV // 128

    def body(dhc_ref, cv_ref, cg_ref, w_ref, dv_ref, dg_ref, dw_ref, h_scr, dhc_scr, hwin, dwin, dw_scr):
        h_scr[0:32, :] = jnp.zeros((32, 128), F32)
        h_scr[32:32 + tp, :] = cv_ref[0] * _sigmoid(cg_ref[0])
        dhc_scr[0:tp, :] = dhc_ref[0]
        dhc_scr[tp:tp + 32, :] = jnp.zeros((32, 128), F32)
        dw_scr[...] = jnp.zeros_like(dw_scr)

        def step(n, carry):
            r0 = pl.multiple_of(n * CHUNK, CHUNK)
            rows = pl.ds(r0, CHUNK)
            hwin[...] = h_scr[pl.ds(r0, win), :]
            dwin[...] = dhc_scr[pl.ds(r0, win), :]
            dcur = dwin[0:CHUNK, :]
            acc = jnp.zeros((CHUNK, 128), F32)
            for j in range(CONV_WIDTH):
                acc = acc + w_ref[j:j + 1, :] * dwin[30 - j:30 - j + CHUNK, :]
                prod = dcur * hwin[2 + j:2 + j + CHUNK, :]
                dw_scr[j * 8:(j + 1) * 8, :] += jnp.sum(prod.reshape(CHUNK // 8, 8, 128), axis=0)
            cg = cg_ref[0, rows, :]
            sig = _sigmoid(cg)
            rowid = n * CHUNK + lax.broadcasted_iota(jnp.int32, (CHUNK, 1), 0)
            dh = jnp.where(rowid >= PAD, acc, 0.0)
            dv_ref[0, rows, :] = (dh * sig).astype(BF16)
            dg_ref[0, rows, :] = (dh * cv_ref[0, rows, :] * sig * (1.0 - sig)).astype(BF16)
            return carry

        lax.fori_loop(0, nchunk, step, 0)
        dw_ref[0] = jnp.zeros((32, 128), F32)
        for j in range(CONV_WIDTH):
            dw_ref[0, j:j + 1, :] = jnp.sum(dw_scr[j * 8:(j + 1) * 8, :], axis=0, keepdims=True)

    blk = lambda off: pl.BlockSpec((1, tp, 128), lambda bi, t: (bi, 0, off // 128 + t))
    res, got = _fused_call(
        body, fuse, name="conv_bwd",
        out_shape=[jax.ShapeDtypeStruct((bsz, tp, D_CONV), BF16), jax.ShapeDtypeStruct((bsz, tp, D_CONV), BF16),
                   jax.ShapeDtypeStruct((bsz, 32, D_CONV), F32)],
        grid=(bsz, nct),
        in_specs=[blk(0), blk(C_VAL), blk(C_GATE), pl.BlockSpec((32, 128), lambda bi, t: (0, t))],
        out_specs=[blk(0), blk(0), pl.BlockSpec((1, 32, 128), lambda bi, t: (bi, 0, t))],
        scratch_shapes=[pltpu.VMEM((tp + 32, 128), F32), pltpu.VMEM((tp + 32, 128), F32),
                        pltpu.VMEM((win, 128), F32), pltpu.VMEM((win, 128), F32),
                        pltpu.VMEM((CONV_WIDTH * 8, 128), F32)],
        operands=(dhc, u, u, conv_w))
    return res[0], res[1], res[2], got


def _gla_bwd(dgla, u, o_pre, states, gup, gbias, gnorm, fuse=NO_FUSE):
    bsz, tp, _ = u.shape
    nchunk = tp // CHUNK
    nb = _gla_group(nchunk)

    def body(dy_ref, q_ref, k_ref, v_ref, r_ref, gd_ref, o_ref, st_ref, gup_ref, gb_ref, gn_ref,
             dq_ref, dk_ref, dv_ref, dr_ref, dgd_ref, dgup_ref, vec_ref, h_scr, gup_acc):
        tril, triu, hmask = _gla_consts(nb)
        h_scr[...] = jnp.zeros_like(h_scr)
        gup_acc[...] = jnp.zeros_like(gup_acc)
        gn = gn_ref[...]
        gupb = gup_ref[...]
        m = nb * CHUNK
        ngroup = nchunk // nb

        def group(i, carry):
            dbias, dgn = carry
            g = ngroup - 1 - i
            rows, valid, z, eb, enb, erest, dec, qe, ke, kd = _gla_group_terms(
                g, nb, q_ref, k_ref, gd_ref, gup_ref, gb_ref, tril)
            keb, kdb = ke.astype(BF16), kd.astype(BF16)
            dqe = jnp.zeros((nb, CHUNK, 128), F32)
            dke = jnp.zeros((nb, CHUNK, 128), F32)
            dkd = jnp.zeros((nb, CHUNK, 128), F32)
            ddec = jnp.zeros((nb, 1, 128), F32)
            for h in range(2):
                cols = slice(h * GLA_DV, (h + 1) * GLA_DV)
                o = o_ref[0, rows, cols]
                rh = r_ref[0, rows, cols]
                dy = dy_ref[0, rows, cols]
                rms = lax.rsqrt(jnp.mean(o * o, axis=-1, keepdims=True) + LN_EPS)
                nrm = o * rms
                sig = _sigmoid(rh)
                sw = rh * sig
                dr_ref[0, rows, cols] = (dy * nrm * gn * (sig * (1.0 + rh * (1.0 - sig)))).astype(BF16)
                dgn = dgn + jnp.sum(dy * nrm * sw, axis=0, keepdims=True)
                dn = dy * gn * sw
                do = rms * (dn - nrm * jnp.mean(dn * nrm, axis=-1, keepdims=True))
                dob = do.astype(BF16).reshape(nb, CHUNK, GLA_DV)
                qh = jnp.where(hmask[h], qe, 0.0).astype(BF16)
                vh = v_ref[0, rows, cols].astype(BF16).reshape(nb, CHUNK, GLA_DV)
                ht = h_scr[h]
                hts = [None] * nb
                for n in reversed(range(nb)):
                    hts[n] = ht
                    ht = dec[n] * ht + _tn(dob[n], qh[n])
                h_scr[h] = ht
                htf = jnp.stack(hts)
                htb = htf.astype(BF16)
                st = st_ref[0, h, pl.ds(g * nb, nb)]
                at = jnp.where(triu, _bnt(keb, qh), 0.0).astype(BF16)
                da = jnp.where(tril, _bnt(dob, vh), 0.0).astype(BF16)
                dat = jnp.where(triu, _bnt(vh, dob), 0.0).astype(BF16)
                dqe = dqe + jnp.where(hmask[h], _bnn(da, keb) + _bnn(dob, st.astype(BF16)), 0.0)
                dke = dke + _bnn(dat, qh)
                dv_ref[0, rows, cols] = (_bnn(at, dob) + _bnt(kdb, htb)).reshape(m, GLA_DV).astype(BF16)
                dkd = dkd + jnp.where(hmask[h], _bnn(vh, htb), 0.0)
                ddec = ddec + jnp.where(hmask[h], jnp.sum(htf * st, axis=1, keepdims=True), 0.0)
            dq_ref[0, rows, :] = (dqe * eb * Q_SCALE).reshape(m, 128).astype(BF16)
            dk_ref[0, rows, :] = (dke * enb + dkd * erest).reshape(m, 128).astype(BF16)
            db = dqe * qe - dke * ke - dkd * kd
            dblast = jnp.sum(dkd * kd, axis=1, keepdims=True) + ddec * dec
            lastrow = lax.broadcasted_iota(jnp.int32, (1, CHUNK, 1), 1) == CHUNK - 1
            db = db + jnp.where(lastrow, dblast, 0.0)
            dlg = _bnn(triu.astype(F32), db, precision=lax.Precision.HIGHEST).reshape(m, 128)
            dz = jnp.where(valid, dlg, 0.0) * (1.0 / GLA_TAU) * (1.0 - _sigmoid(z))
            dzb = dz.astype(BF16)
            dgd_ref[0, 0, rows, :] = _nt(dzb, gupb).astype(BF16)
            gup_acc[...] += _tn(gd_ref[0, rows, :].astype(BF16), dzb)
            return dbias + jnp.sum(dz, axis=0, keepdims=True), dgn

        zero = jnp.zeros((1, 128), F32)
        dbias, dgn = lax.fori_loop(0, ngroup, group, (zero, zero))
        dgup_ref[0] = gup_acc[...]
        vec_ref[0] = jnp.zeros((8, 128), F32)
        vec_ref[0, 0:1, :] = dbias
        vec_ref[0, 1:2, :] = dgn

    pair = lambda w, off: pl.BlockSpec((1, tp, w), lambda bi, p: (bi, 0, off // w + p))
    return _fused_call(
        body, fuse, name="gla_bwd",
        out_shape=[jax.ShapeDtypeStruct((bsz, tp, 256), BF16), jax.ShapeDtypeStruct((bsz, tp, 256), BF16),
                   jax.ShapeDtypeStruct((bsz, tp, 512), BF16), jax.ShapeDtypeStruct((bsz, tp, 512), BF16),
                   jax.ShapeDtypeStruct((bsz, 2, tp, 128), BF16), jax.ShapeDtypeStruct((bsz, 128, 256), F32),
                   jax.ShapeDtypeStruct((bsz, 8, 256), F32)],
        grid=(bsz, 2),
        in_specs=[pair(256, 0), pair(128, C_Q), pair(128, C_K), pair(256, C_V), pair(256, C_R),
                  pl.BlockSpec((1, tp, 128), lambda bi, p: (bi, 0, C_GD // 128)),
                  pair(256, 0),
                  pl.BlockSpec((1, 2, nchunk, GLA_DV, 128), lambda bi, p: (bi, p, 0, 0, 0)),
                  pl.BlockSpec((128, 128), lambda bi, p: (0, p)),
                  pl.BlockSpec((1, 128), lambda bi, p: (0, p)),
                  pl.BlockSpec((1, 128), lambda bi, p: (0, 0))],
        out_specs=[pair(128, 0), pair(128, 0), pair(256, 0), pair(256, 0),
                   pl.BlockSpec((1, 1, tp, 128), lambda bi, p: (bi, p, 0, 0)),
                   pl.BlockSpec((1, 128, 128), lambda bi, p: (bi, 0, p)),
                   pl.BlockSpec((1, 8, 128), lambda bi, p: (bi, 0, p))],
        scratch_shapes=[pltpu.VMEM((2, GLA_DV, 128), F32), pltpu.VMEM((128, 128), F32)],
        operands=(dgla, u, u, u, u, u, o_pre, states, gup, gbias, gnorm))


_DU_OFFSETS = (C_VAL, C_GATE, C_Q, C_K, C_V, C_R)
_DU_WIDTHS = (512, 512, 256, 256, 512, 512)


def _du_specs(tm, per, row_map):
    specs = [pl.BlockSpec((tm, w), row_map) for w in _DU_WIDTHS]
    for p in range(2):
        specs.append(pl.BlockSpec((1, 1, tm, 128), lambda *ix, p=p: (row_map(*ix)[0] // per, p, row_map(*ix)[0] % per, 0)))
    return specs


def _du_pieces(refs):
    out = [(off, ref[...]) for off, ref in zip(_DU_OFFSETS, refs[:6])]
    dgd = (refs[6][0, 0].astype(F32) + refs[7][0, 0].astype(F32)).astype(BF16)
    out.append((C_GD, dgd))
    return out


def _in_proj_bwd(pieces, dgd, w_int, dt1, tp, fuse=NO_FUSE):
    r, d = dt1.shape
    tm = _row_tile(tp)
    per = tp // tm

    def body(*refs):
        w_ref, dt_ref, o_ref = refs[8:]
        acc = ALPHA * dt_ref[...]
        for off, val in _du_pieces(refs[:8]):
            acc = acc + _nn(val, w_ref[off:off + val.shape[1], :])
        o_ref[...] = acc

    row = lambda i: (i, 0)
    (ds0,), got = _fused_call(
        body, fuse, name="in_proj_bwd", out_shape=[jax.ShapeDtypeStruct((r, d), F32)], grid=(r // tm,),
        in_specs=_du_specs(tm, per, row) + [pl.BlockSpec((D_IN_PAD, d), lambda i: (0, 0)), pl.BlockSpec((tm, d), row)],
        out_specs=[pl.BlockSpec((tm, d), row)], scratch_shapes=[], operands=(*pieces, dgd, dgd, w_int, dt1))
    return ds0, got


def _grad_w_in(pieces, dgd, s0b, tp, fuse=NO_FUSE):
    r, d = s0b.shape
    tk = _reduce_tile(tp, False)
    per = tp // tk

    def body(*refs):
        s_ref, o_ref = refs[8:]

        @pl.when(pl.program_id(0) == 0)
        def _():
            o_ref[...] = jnp.zeros_like(o_ref)

        s = s_ref[...]
        for off, val in _du_pieces(refs[:8]):
            o_ref[off:off + val.shape[1], :] += _tn(val, s)

    row = lambda k: (k, 0)
    (out,), got = _fused_call(
        body, fuse, name="grad_w_in", out_shape=[jax.ShapeDtypeStruct((D_IN_PAD, d), F32)], grid=(r // tk,),
        in_specs=_du_specs(tk, per, row) + [pl.BlockSpec((tk, d), row)],
        out_specs=[pl.BlockSpec((D_IN_PAD, d), lambda k: (0, 0))], scratch_shapes=[],
        operands=(*pieces, dgd, dgd, s0b))
    return out, got


def _ln_in_bwd(ds0, x, meta, g):
    bsz, s, d = x.shape
    tp = s + HEAD
    nh = 2
    sh = s // nh
    rc = min(256, sh)

    def body(ds_ref, x_ref, meta_ref, g_ref, gx_ref, dm_ref, vec_ref):
        h = pl.program_id(1)
        gg = g_ref[...]

        @pl.when(h == 0)
        def _():
            mh, mr = _ln_stats(meta_ref[...])
            dsm = ds_ref[0, PAD:HEAD, :]
            dm_ref[0] = _ln_bwd(dsm * gg, mh, mr)
            vec_ref[0] = jnp.zeros((8, d), F32)
            vec_ref[0, 0:1, :] = jnp.sum(dsm * mh, axis=0, keepdims=True)
            vec_ref[0, 1:2, :] = jnp.sum(dsm, axis=0, keepdims=True)

        def step(i, carry):
            sg, sb = carry
            dst = pl.ds(pl.multiple_of(i * rc, rc), rc)
            src = pl.ds(pl.multiple_of(HEAD + h * sh + i * rc, 64), rc)
            xh, rstd = _ln_stats(x_ref[0, dst, :])
            dsv = ds_ref[0, src, :]
            gx_ref[0, dst, :] = _ln_bwd(dsv * gg, xh, rstd)
            return sg + jnp.sum(dsv * xh, axis=0, keepdims=True), sb + jnp.sum(dsv, axis=0, keepdims=True)

        zero = jnp.zeros((1, d), F32)
        sg, sb = lax.fori_loop(0, sh // rc, step, (zero, zero))
        vec_ref[0, 0:1, :] += sg
        vec_ref[0, 1:2, :] += sb

    return pl.pallas_call(
        body, name="ln_in_bwd",
        out_shape=[jax.ShapeDtypeStruct((bsz, s, d), F32), jax.ShapeDtypeStruct((bsz, N_META, d), F32),
                   jax.ShapeDtypeStruct((bsz, 8, d), F32)],
        grid=(bsz, nh),
        in_specs=[pl.BlockSpec((1, tp, d), lambda bi, hi: (bi, 0, 0)),
                  pl.BlockSpec((1, sh, d), lambda bi, hi: (bi, hi, 0)),
                  pl.BlockSpec((N_META, d), lambda bi, hi: (0, 0)),
                  pl.BlockSpec((1, d), lambda bi, hi: (0, 0))],
        out_specs=[pl.BlockSpec((1, sh, d), lambda bi, hi: (bi, hi, 0)),
                   pl.BlockSpec((1, N_META, d), lambda bi, hi: (bi, 0, 0)),
                   pl.BlockSpec((1, 8, d), lambda bi, hi: (bi, 0, 0))],
        compiler_params=_params(("parallel", "arbitrary")),
    )(ds0, x, meta, g)


def _rows128(a):
    return a.reshape(-1, 128)


def kernel(x, meta_tokens, ln_in_g, ln_in_b, w_in, conv_w, conv_b, conv_ln_g, conv_ln_b, gate_up, gate_bias, gla_norm_g, w_out, ln1_g, ln1_b, w_ff1, w_ff2, ln2_g, ln2_b, loss_target, m_meta_tokens, m_ln_in_g, m_ln_in_b, m_w_in, m_conv_w, m_conv_b, m_conv_ln_g, m_conv_ln_b, m_gate_up, m_gate_bias, m_gla_norm_g, m_w_out, m_ln1_g, m_ln1_b, m_w_ff1, m_w_ff2, m_ln2_g, m_ln2_b, v_meta_tokens, v_ln_in_g, v_ln_in_b, v_w_in, v_conv_w, v_conv_b, v_conv_ln_g, v_conv_ln_b, v_gate_up, v_gate_bias, v_gla_norm_g, v_w_out, v_ln1_g, v_ln1_b, v_w_ff1, v_w_ff2, v_ln2_g, v_ln2_b):
    bsz, seq, d = x.shape
    tp = seq + HEAD
    r = bsz * tp
    xi, yi, ci = _mesh_pos()
    chip = 2 * xi + yi
    c_arr = jnp.reshape(ci, (1,)).astype(jnp.int32)
    pos_arr = jnp.stack([chip, ci]).astype(jnp.int32)

    sh_in = D_IN // 4
    shard_in = jnp.pad(w_in[0].T.astype(BF16), ((0, D_IN_PAD // 4 - sh_in), (0, 0)))
    shard_w1, shard_wout, shard_w2 = w_ff1[0].astype(BF16), w_out[0].astype(BF16), w_ff2[0].astype(BF16)
    small_w = jnp.concatenate([_rows128(meta_tokens), _rows128(conv_w[0]), _rows128(gate_up[0])], axis=0)
    ln_in_g2, ln_in_b2 = ln_in_g.reshape(1, d), ln_in_b.reshape(1, d)

    s0, s0b, ((g_int,), (g_small,)) = _ln_in_fwd(
        x, ln_in_g2, ln_in_b2, [("gather", [(shard_in, 0, D_IN_PAD // 4, None)]), ("small", [small_w])])
    (g_int,) = _place_own([g_int], [shard_in])
    (g_small,) = _place_own([g_small], [small_w])
    n_meta_rows, n_cw_rows = N_META * 256 // 128, CONV_WIDTH * 128 // 128
    meta_full = jnp.concatenate([g_small[j, :n_meta_rows].reshape(N_META, 256) for j in range(4)], axis=1)
    convw_full = jnp.concatenate(
        [g_small[j, n_meta_rows:n_meta_rows + n_cw_rows].reshape(CONV_WIDTH, 128) for j in range(4)], axis=1)
    gup_full = jnp.concatenate(
        [g_small[j, n_meta_rows + n_cw_rows:].reshape(GLA_RANK, 64) for j in range(4)], axis=1)
    convw_p = jnp.pad(convw_full, ((0, 1), (0, 0)))
    gup_p = jnp.pad(gup_full, ((0, 128 - GLA_RANK), (0, 0))).astype(BF16)
    s0, s0b = _ln_meta(s0, s0b, meta_full, ln_in_g2, ln_in_b2)
    w_int = jnp.pad(g_int[:, :sh_in].reshape(D_IN, d), ((0, D_IN_PAD - D_IN), (0, 0)))
    s0f, s0bf = s0.reshape(r, d), s0b.reshape(r, d)
    u, ((w1_buf,),) = _in_proj(s0bf, w_int, [("gather", [(shard_w1, 0, 640, None)])])
    (w1_buf,) = _place_own([w1_buf], [shard_w1])
    u3 = u.reshape(bsz, tp, D_IN_PAD)
    hc, ((w1_buf,),) = _conv_fwd(u3, convw_p, conv_b, [("gather", [(shard_w1, 640, 384, w1_buf)])])
    hc = hc.reshape(r, D_CONV)
    hc, u3_after_conv = lax.optimization_barrier((hc, u3))
    gla_out, o_pre, states, ((g_wout,),) = _gla_fwd(
        u3_after_conv, gup_p, gate_bias, gla_norm_g, [("gather", [(shard_wout, 0, 256, None)])])
    (g_wout,) = _place_own([g_wout], [shard_wout])
    wout = g_wout.reshape(d, d)
    gla_of = gla_out.reshape(r, 512)
    conv_of, xhat1, rstd1, s1b, ((w2_buf,),) = _out_proj_ln1(
        hc, gla_of, wout, s0f, conv_ln_g, conv_ln_b, ln1_g, ln1_b, [("gather", [(shard_w2, 0, 448, None)])])
    (w2_buf,) = _place_own([w2_buf], [shard_w2])
    w1 = w1_buf
    ra, ((w2_buf,),) = _ffn1(s1b, w1, [("gather", [(shard_w2, 448, 576, w2_buf)])])
    w2 = w2_buf.reshape(D_FF, d)
    dt2, dt2b, acc2 = _ffn2_ln2_loss(ra, w2, xhat1, ln1_g, ln1_b, ln2_g, ln2_b, loss_target, tp)

    def add_pair(g, got, splits=None):
        return _add_pair(g, got, c_arr, splits)

    da = _ffn_bwd_da(dt2b, w2, ra)
    dt1, dt1b, acc1 = _ffn_bwd_ln1(da, w1, dt2, xhat1, rstd1, ln1_g)
    g_w2, _ = _matmul_tn(ra, dt2b, 1024, square_lhs=True, name="grad_w_ff2")
    big_w2 = g_w2.reshape(4, D_FF // 4, d)
    big_w1, ((pair_w2,),) = _grad_w_ff1(s1b, da, [("pair", [big_w2])])
    dhc, dgla, cacc, ((pair_w1,),) = _out_proj_bwd(dt1b, wout, hc, conv_ln_g, conv_ln_b, [("pair", [big_w1])])
    (part_w1,), (part_w2a, part_w2b) = add_pair(big_w1, pair_w1), add_pair(big_w2, pair_w2, [512, 512])
    g_wout, _ = _grad_w_out(conv_of, gla_of, dt1b)
    big_wout = g_wout.reshape(4, d // 4, d)
    dcv, dcg, dcw, ((chip_w1,), (pair_wout,)) = _conv_bwd(
        dhc.reshape(bsz, tp, D_CONV), u3, convw_p, [("exchange", [part_w1]), ("pair", [big_wout])])
    (part_wout,) = add_pair(big_wout, pair_wout)
    (dq, dk, dv, dr, dgd, dgup, gvec), ((chip_w2a,),) = _gla_bwd(
        dgla.reshape(bsz, tp, 512), u3, o_pre, states, gup_p, gate_bias, gla_norm_g, [("exchange", [part_w2a])])
    pieces = [a.reshape(r, a.shape[-1]) for a in (dcv, dcg, dq, dk, dv, dr)]
    g_wint, ((chip_w2b, chip_wout),) = _grad_w_in(
        pieces, dgd, s0bf, tp, [("exchange", [part_w2b, part_wout])])
    big_win = jnp.stack([g_wint[j * sh_in:(j + 1) * sh_in] for j in range(4)])
    (half_w1, half_w2), ((pair_win,),) = _add_chips_many(
        [(big_w1, pair_w1, [chip_w1]), (big_w2, pair_w2, [chip_w2a, chip_w2b])], [("pair", [big_win])])
    ds0, ((chip_win,), (sib_w1, sib_w2)) = _in_proj_bwd(
        pieces, dgd, w_int, dt1, tp, [("exchange", add_pair(big_win, pair_win)), ("swap", [half_w1, half_w2])])
    grad_x, dmeta, lvec = _ln_in_bwd(ds0.reshape(bsz, tp, d), x, meta_full, ln_in_g2)

    half_win = _add_chips(big_win, pair_win, [chip_win], pos_arr)
    half_wout = _add_chips(big_wout, pair_wout, [chip_wout], pos_arr)
    sib_win, sib_wout = _pair_swap([half_win, half_wout])
    grads = {}
    weights = dict(meta_tokens=meta_tokens, ln_in_g=ln_in_g, ln_in_b=ln_in_b, w_in=w_in, conv_w=conv_w, conv_b=conv_b,
                   conv_ln_g=conv_ln_g, conv_ln_b=conv_ln_b, gate_up=gate_up, gate_bias=gate_bias,
                   gla_norm_g=gla_norm_g, w_out=w_out, ln1_g=ln1_g, ln1_b=ln1_b, w_ff1=w_ff1, w_ff2=w_ff2,
                   ln2_g=ln2_g, ln2_b=ln2_b)
    moms = dict(meta_tokens=(m_meta_tokens, v_meta_tokens), ln_in_g=(m_ln_in_g, v_ln_in_g),
                ln_in_b=(m_ln_in_b, v_ln_in_b), w_in=(m_w_in, v_w_in), conv_w=(m_conv_w, v_conv_w),
                conv_b=(m_conv_b, v_conv_b), conv_ln_g=(m_conv_ln_g, v_conv_ln_g),
                conv_ln_b=(m_conv_ln_b, v_conv_ln_b), gate_up=(m_gate_up, v_gate_up),
                gate_bias=(m_gate_bias, v_gate_bias), gla_norm_g=(m_gla_norm_g, v_gla_norm_g),
                w_out=(m_w_out, v_w_out), ln1_g=(m_ln1_g, v_ln1_g), ln1_b=(m_ln1_b, v_ln1_b),
                w_ff1=(m_w_ff1, v_w_ff1), w_ff2=(m_w_ff2, v_w_ff2), ln2_g=(m_ln2_g, v_ln2_g),
                ln2_b=(m_ln2_b, v_ln2_b))
    names = list(weights)
    big_names = ("w_in", "w_out", "w_ff1", "w_ff2")
    delta, new_m, new_v = {}, {}, {}
    def adamw_big(k, own, sib, fuse=NO_FUSE):
        to2d = (lambda a: a[0].T) if k == "w_in" else (lambda a: a[0])
        back = (lambda a: a.T[None]) if k == "w_in" else (lambda a: a[None])
        res, got = _adamw_halves(to2d(weights[k]), own, sib, to2d(moms[k][0]), to2d(moms[k][1]), fuse)
        grads[k], delta[k], new_m[k], new_v[k] = [back(a) for a in res]
        return got

    small_pack = _pack_small(acc2, acc1, lvec, cacc, gvec, dcw, dgup, dmeta)
    ((small_all,),) = adamw_big("w_in", half_win, sib_win, [("all", [small_pack])])
    adamw_big("w_ff1", half_w1, sib_w1)
    adamw_big("w_ff2", half_w2, sib_w2)
    adamw_big("w_out", half_wout, sib_wout)
    red, loss = _sum_small(small_all)
    small_names = [k for k in names if k not in big_names]
    two = lambda a: a.reshape(-1, a.shape[-1])

    def my_cols(full, rows, width):
        return lax.dynamic_slice(full, (0, chip * width), (rows, width))

    sharded = {
        "meta_tokens": my_cols(red[SMALL_AT["meta_tokens"][0]:SMALL_AT["meta_tokens"][0] + N_META], N_META, 256),
        "conv_w": my_cols(red[SMALL_CONV_W:SMALL_CONV_W + 16].reshape(32, D_CONV), CONV_WIDTH, 128),
        "gate_up": my_cols(red[SMALL_GATE_UP:SMALL_GATE_UP + 4].reshape(GLA_RANK, 256), GLA_RANK, 64)}
    upd = _adamw_small(red, sharded, {k: (two(weights[k]), two(moms[k][0]), two(moms[k][1])) for k in small_names})
    for k in small_names:
        shp = weights[k].shape
        grads[k], delta[k], new_m[k], new_v[k] = [a.reshape(shp) for a in upd[k]]
    loss = loss.reshape(())

    return (loss, grad_x, *[grads[k] for k in names], *[delta[k] for k in names],
            *[new_m[k] for k in names], *[new_v[k] for k in names])
```

```python
import functools

import jax
import jax.numpy as jnp
from jax import lax
from jax.experimental import pallas as pl
from jax.experimental.pallas import tpu as pltpu

F32 = jnp.float32
BF16 = jnp.bfloat16

D_MODEL = 1024
N_META = 16
D_CONV = 512
CONV_WIDTH = 31
GLA_HEADS = 4
GLA_DV = 128
GLA_DK = 64
GLA_RANK = 16
GLA_TAU = 16.0
CHUNK = 64
D_FF = 4096
LN_EPS = 1e-5
ALPHA = 2.0 ** 0.25
D_IN = 2576
D_IN_PAD = 2688
PAD = CHUNK - N_META
HEAD = PAD + N_META
Q_SCALE = GLA_DK ** -0.5
ADAM_LR, ADAM_B1, ADAM_B2, ADAM_EPS, ADAM_WD, ADAM_STEP = 0.001, 0.9, 0.999, 1e-08, 0.01, 10
HALF = D_MODEL // 2
VMEM_LIMIT = 56 * 1024 * 1024
MESH = pl.DeviceIdType.MESH

C_VAL, C_GATE, C_Q, C_K, C_V, C_R, C_GD = 0, 512, 1024, 1280, 1536, 2048, 2560

SMALL_AT = {"loss": (0, 1, 0, 1024), "ln_in_g": (1, 1, 0, 1024), "ln_in_b": (2, 1, 0, 1024), "ln1_g": (3, 1, 0, 1024),
            "ln1_b": (4, 1, 0, 1024), "ln2_g": (5, 1, 0, 1024), "ln2_b": (6, 1, 0, 1024), "conv_b": (7, 1, 0, 512),
            "conv_ln_g": (7, 1, 512, 512), "conv_ln_b": (8, 1, 0, 512), "gate_bias": (8, 1, 512, 256),
            "gla_norm_g": (8, 1, 768, 128), "meta_tokens": (40, 16, 0, 1024)}
SMALL_CONV_W, SMALL_GATE_UP = 16, 32
SMALL_ROWS = 56


def _params(sem=None, **kw):
    return pltpu.CompilerParams(dimension_semantics=sem, vmem_limit_bytes=VMEM_LIMIT, **kw)


def _row_tile(tp):
    for t in (704, 352, 192, 64):
        if tp % t == 0:
            return t
    raise ValueError(tp)


def _reduce_tile(tp, big):
    for t in ((2112, 1056, 704) if big else (1056, 704)) + (352, 192, 64):
        if tp % t == 0:
            return t
    raise ValueError(tp)


def _sub_rows(tm):
    return [slice(0, tm)]


def _dot(a, b, dims, precision=None):
    return lax.dot_general(a, b, (dims, ((), ())), preferred_element_type=F32, precision=precision)


def _nn(a, b, **kw):
    return _dot(a, b, ((1,), (0,)), **kw)


def _nt(a, b, **kw):
    return _dot(a, b, ((1,), (1,)), **kw)


def _tn(a, b, **kw):
    return _dot(a, b, ((0,), (0,)), **kw)


def _sigmoid(x):
    return 1.0 / (1.0 + jnp.exp(-x))


def _log_sigmoid(z):
    return jnp.minimum(z, 0.0) - jnp.log(1.0 + jnp.exp(-jnp.abs(z)))


def _ln_stats(t):
    mu = jnp.mean(t, axis=-1, keepdims=True)
    d = t - mu
    var = jnp.mean(d * d, axis=-1, keepdims=True)
    rstd = lax.rsqrt(var + LN_EPS)
    return d * rstd, rstd


def _ln_bwd(dxhat, xhat, rstd):
    m1 = jnp.mean(dxhat, axis=-1, keepdims=True)
    m2 = jnp.mean(dxhat * xhat, axis=-1, keepdims=True)
    return rstd * (dxhat - m1 - xhat * m2)


def _mesh_pos():
    return lax.axis_index("x"), lax.axis_index("y"), lax.axis_index("c")


ANY = pl.BlockSpec(memory_space=pl.ANY)


def _gather_sems(n):
    return [pltpu.SemaphoreType.DMA((n, 3))] * 4


def _gather_steps(ins, outs, sems, ranges=None):
    n = len(ins)
    send, recv, fsend, frecv = sems
    ranges = ranges or [(0, ref.shape[0]) for ref in ins]
    x, y, c = _mesh_pos()
    me = 2 * x + y
    sibling = (x, y, 1 - c)
    chips = [(1 - x, y), (x, 1 - y), (1 - x, 1 - y)]
    chip_idx = [2 * px + py for px, py in chips]
    mine = [pl.ds(pl.multiple_of(r0 + c * (nr // 2), 16), nr // 2) for r0, nr in ranges]
    other = [pl.ds(pl.multiple_of(r0 + (1 - c) * (nr // 2), 16), nr // 2) for r0, nr in ranges]
    pairs = [(a, k) for a in range(n) for k in range(3)]

    def ici(a, k, slab):
        return pltpu.make_async_remote_copy(
            src_ref=ins[a].at[mine[a], :], dst_ref=outs[a].at[slab, mine[a], :],
            send_sem=send.at[a, k], recv_sem=recv.at[a, k], device_id=(*chips[k], c), device_id_type=MESH)

    def forward(a, k, rows):
        blk = outs[a].at[chip_idx[k], rows[a], :]
        return pltpu.make_async_remote_copy(
            src_ref=blk, dst_ref=blk, send_sem=fsend.at[a, k], recv_sem=frecv.at[a, k],
            device_id=sibling, device_id_type=MESH)

    def start():
        for a, k in pairs:
            ici(a, k, me).start()

    def relay():
        for a, k in pairs:
            ici(a, k, chip_idx[k]).wait_recv()
            forward(a, k, mine).start()

    def finish():
        for a, k in pairs:
            forward(a, k, other).wait_recv()
        for a, k in pairs:
            ici(a, k, me).wait_send()
            forward(a, k, mine).wait_send()

    return start, relay, finish


def _place_own(gathered, shards):
    chip = 2 * lax.axis_index("x") + lax.axis_index("y")
    return [lax.dynamic_update_slice(g, s[None], (chip, 0, 0)) for g, s in zip(gathered, shards)]


def _gather_small_steps(ins, outs, sems):
    send, recv = sems
    x, y, c = _mesh_pos()
    chips = [(1 - x, y), (x, 1 - y), (1 - x, 1 - y)]

    def copy(a, k, slot):
        return pltpu.make_async_remote_copy(
            src_ref=ins[a], dst_ref=outs[a].at[slot], send_sem=send.at[a, k], recv_sem=recv.at[a, k],
            device_id=(*chips[k], c), device_id_type=MESH)

    def start():
        for a in range(len(ins)):
            for k in range(3):
                copy(a, k, 2 * x + y).start()

    def finish():
        for a in range(len(ins)):
            for k, (px, py) in enumerate(chips):
                copy(a, k, 2 * px + py).wait_recv()
            for k in range(3):
                copy(a, k, 2 * x + y).wait_send()

    return start, finish


def _pair_swap_steps(ins, outs, sems):
    send, recv = sems
    x, y, c = _mesh_pos()
    cps = [pltpu.make_async_remote_copy(
        src_ref=ins[a], dst_ref=outs[a], send_sem=send.at[a], recv_sem=recv.at[a],
        device_id=(x, y, 1 - c), device_id_type=MESH) for a in range(len(ins))]

    def start():
        for cp in cps:
            cp.start()

    def finish():
        for cp in cps:
            cp.wait()

    return start, finish


def _pair_exchange_steps(ins, outs, sems):
    send, recv = sems
    x, y, c = _mesh_pos()
    other = pl.ds(pl.multiple_of((1 - c) * HALF, 128), HALF)
    cps = [pltpu.make_async_remote_copy(
        src_ref=ins[a].at[:, :, other], dst_ref=outs[a], send_sem=send.at[a], recv_sem=recv.at[a],
        device_id=(x, y, 1 - c), device_id_type=MESH) for a in range(len(ins))]

    def start():
        for cp in cps:
            cp.start()

    def finish():
        for cp in cps:
            cp.wait()

    return start, finish


def _pair_exchange_shapes(grads):
    return [jax.ShapeDtypeStruct(g.shape[:2] + (HALF,), g.dtype) for g in grads]


def _pair_exchange(grads):
    n = len(grads)

    def body(*refs):
        start, finish = _pair_exchange_steps(refs[:n], refs[n:2 * n], refs[2 * n:])
        start()
        finish()

    return pl.pallas_call(
        body, name="grad_pair_exchange", out_shape=_pair_exchange_shapes(grads),
        in_specs=[ANY] * n, out_specs=[ANY] * n,
        scratch_shapes=[pltpu.SemaphoreType.DMA((n,)), pltpu.SemaphoreType.DMA((n,))],
    )(*grads)


NO_FUSE = ()


def _fuse_plan(kind, items):
    n = len(items)
    if kind == "gather":
        shards = [it[0] for it in items]
        bufs = [it[3] for it in items if it[3] is not None]
        alias, b = {}, 0
        for a, it in enumerate(items):
            if it[3] is not None:
                alias[n + b] = a
                b += 1
        ranges = [(it[1], it[2]) for it in items]
        return (shards + bufs, [jax.ShapeDtypeStruct((4,) + s.shape, s.dtype) for s in shards], alias, _gather_sems(n),
                lambda i, o, s: _gather_steps(i[:n], o, s, ranges))
    if kind == "exchange":
        return list(items), _chip_exchange_shapes(items), {}, _chip_exchange_sems(n), _chip_exchange_steps
    if kind == "small":
        return (list(items), [jax.ShapeDtypeStruct((4,) + s.shape, s.dtype) for s in items], {},
                _chip_exchange_sems(n), _gather_small_steps)
    if kind == "all":
        (pack,) = items
        return ([pack], [jax.ShapeDtypeStruct((8 * pack.shape[0], pack.shape[1]), pack.dtype)], {},
                [pltpu.SemaphoreType.DMA((7,)), pltpu.SemaphoreType.DMA((7,)), pltpu.SemaphoreType.DMA((1,))],
                _allgather_all_steps)
    pair_sems = [pltpu.SemaphoreType.DMA((n,)), pltpu.SemaphoreType.DMA((n,))]
    if kind == "swap":
        return (list(items), [jax.ShapeDtypeStruct(h.shape, h.dtype) for h in items], {}, pair_sems, _pair_swap_steps)
    return list(items), _pair_exchange_shapes(items), {}, pair_sems, _pair_exchange_steps


def _fused_call(body, fuse, *, name, grid, in_specs, out_specs, out_shape, scratch_shapes, operands,
                late_relay=False):
    plans = [_fuse_plan(kind, list(items)) for kind, items in fuse if len(items)]
    n_in, n_out, n_s = len(in_specs), len(out_shape), len(scratch_shapes)
    comm = [a for p in plans for a in p[0]]
    shapes = [s for p in plans for s in p[1]]
    nc, no = len(comm), len(shapes)
    aliases, i_at, o_at = {}, n_in, n_out
    for p in plans:
        for i, o in p[2].items():
            aliases[i_at + i] = o_at + o
        i_at, o_at = i_at + len(p[0]), o_at + len(p[1])

    def wrapped(*refs):
        o0 = n_in + nc
        s0 = o0 + n_out + no
        i_at, o_at, sem_at, steps = n_in, o0 + n_out, s0 + n_s, []
        for p in plans:
            steps.append(p[4](refs[i_at:i_at + len(p[0])], refs[o_at:o_at + len(p[1])], refs[sem_at:sem_at + len(p[3])]))
            i_at, o_at, sem_at = i_at + len(p[0]), o_at + len(p[1]), sem_at + len(p[3])
        first, last = _grid_ends(grid)
        for st in steps:
            pl.when(first)(st[0])
        if not late_relay:
            for st in steps:
                for mid in st[1:-1]:
                    pl.when(last)(mid)
        body(*refs[:n_in], *refs[o0:o0 + n_out], *refs[s0:s0 + n_s])
        for st in steps:
            for step in (st[1:] if late_relay else st[-1:]):
                pl.when(last)(step)

    res = pl.pallas_call(
        wrapped if plans else body, name=name, grid=grid, in_specs=list(in_specs) + [ANY] * nc,
        out_specs=list(out_specs) + [ANY] * no, out_shape=list(out_shape) + shapes,
        scratch_shapes=list(scratch_shapes) + [s for p in plans for s in p[3]], input_output_aliases=aliases,
        compiler_params=_params(("arbitrary",) * len(grid)))(*operands, *comm)
    outs, got, results = list(res[:n_out]), list(res[n_out:]), []
    for p in plans:
        results.append(got[:len(p[1])])
        got = got[len(p[1]):]
    return outs, results


def _chip_exchange_sems(n):
    return [pltpu.SemaphoreType.DMA((n, 3))] * 2


def _chip_exchange_shapes(parts):
    return [jax.ShapeDtypeStruct((3,) + p.shape[1:], p.dtype) for p in parts]


def _chip_exchange_steps(ins, outs, sems):
    send, recv = sems
    x, y, c = _mesh_pos()
    chips = [(1 - x, y), (x, 1 - y), (1 - x, 1 - y)]
    cps = [pltpu.make_async_remote_copy(
        src_ref=ins[a].at[2 * px + py], dst_ref=outs[a].at[k], send_sem=send.at[a, k], recv_sem=recv.at[a, k],
        device_id=(px, py, c), device_id_type=MESH) for a in range(len(ins)) for k, (px, py) in enumerate(chips)]

    def start():
        for cp in cps:
            cp.start()

    def finish():
        for cp in cps:
            cp.wait()

    return start, finish


def _chip_exchange(parts):
    n = len(parts)

    def body(*refs):
        start, finish = _chip_exchange_steps(refs[:n], refs[n:2 * n], refs[2 * n:])
        start()
        finish()

    return pl.pallas_call(
        body, name="grad_chip_exchange", out_shape=_chip_exchange_shapes(parts),
        in_specs=[ANY] * n, out_specs=[ANY] * n, scratch_shapes=_chip_exchange_sems(n),
    )(*parts)


def _pair_swap(halves):
    n = len(halves)

    def body(*refs):
        start, finish = _pair_swap_steps(refs[:n], refs[n:2 * n], refs[2 * n:])
        start()
        finish()

    return pl.pallas_call(
        body, name="grad_pair_swap",
        out_shape=[jax.ShapeDtypeStruct(h.shape, h.dtype) for h in halves],
        in_specs=[ANY] * n, out_specs=[ANY] * n,
        scratch_shapes=[pltpu.SemaphoreType.DMA((n,)), pltpu.SemaphoreType.DMA((n,))],
    )(*halves)


def _allgather_all_steps(ins, outs, sems):
    send_sems, recv_sems, local_sem = sems
    x_ref, out_ref = ins[0], outs[0]
    m_per = x_ref.shape[0]
    x, y, c = _mesh_pos()
    me, sibling = (x, y, c), (x, y, 1 - c)
    chips = [(1 - x, y), (x, 1 - y), (1 - x, 1 - y)]

    def rows(px, py, pc):
        return out_ref.at[pl.ds(pl.multiple_of((4 * px + 2 * py + pc) * m_per, 8), m_per), :]

    def copy(k, block, to, src=None):
        return pltpu.make_async_remote_copy(
            src_ref=rows(*block) if src is None else src, dst_ref=rows(*block),
            send_sem=send_sems.at[k], recv_sem=recv_sems.at[k], device_id=to, device_id_type=MESH)

    mine = pltpu.make_async_copy(x_ref, rows(*me), local_sem.at[0])
    first = [copy(0, me, sibling, src=x_ref)]
    first += [copy(1 + j, me, (*chip, c), src=x_ref) for j, chip in enumerate(chips)]
    passed = [copy(4 + j, (*chip, c), sibling) for j, chip in enumerate(chips)]

    def start():
        mine.start()
        for cp in first:
            cp.start()

    def relay():
        for j, chip in enumerate(chips):
            copy(1 + j, (*chip, c), me).wait_recv()
            passed[j].start()

    def finish():
        copy(0, sibling, me).wait_recv()
        for j, chip in enumerate(chips):
            copy(4 + j, (*chip, 1 - c), me).wait_recv()
        for cp in first + passed:
            cp.wait_send()
        mine.wait()

    return start, relay, finish


def _add_pair(g, got, c_arr, splits=None):
    _, rows, _ = g.shape
    splits = splits or [rows]

    def body(c_ref, g_ref, r_ref, *o_refs):
        at = 0
        for o_ref, n in zip(o_refs, splits):
            o_ref[...] = (g_ref[:, at:at + n, :] + r_ref[:, at:at + n, :]).astype(BF16)
            at += n

    return pl.pallas_call(
        body, name="grad_add_pair", out_shape=[jax.ShapeDtypeStruct((4, n, HALF), BF16) for n in splits],
        grid_spec=pltpu.PrefetchScalarGridSpec(
            num_scalar_prefetch=1, grid=(4,),
            in_specs=[pl.BlockSpec((1, rows, HALF), lambda j, c: (j, 0, c[0])),
                      pl.BlockSpec((1, rows, HALF), lambda j, c: (j, 0, 0))],
            out_specs=[pl.BlockSpec((1, n, HALF), lambda j, c: (j, 0, 0)) for n in splits]),
        compiler_params=_params(("arbitrary",)),
    )(c_arr, g, got)


def _add_chips(g, pair_got, chip_gots, pos_arr):
    _, rows, _ = g.shape

    def body(pos_ref, g_ref, p_ref, *refs):
        o_ref, at = refs[-1], 0
        for r_ref in refs[:-1]:
            n = r_ref.shape[1]
            own = g_ref[0, at:at + n, :] + p_ref[0, at:at + n, :]
            o_ref[at:at + n, :] = ((own + r_ref[0].astype(F32)) + r_ref[1].astype(F32)) + r_ref[2].astype(F32)
            at += n

    return pl.pallas_call(
        body, name="grad_add_chips", out_shape=jax.ShapeDtypeStruct((rows, HALF), F32),
        grid_spec=pltpu.PrefetchScalarGridSpec(
            num_scalar_prefetch=1, grid=(1,),
            in_specs=[pl.BlockSpec((1, rows, HALF), lambda i, p: (p[0], 0, p[1])),
                      pl.BlockSpec((1, rows, HALF), lambda i, p: (p[0], 0, 0))]
            + [pl.BlockSpec(t.shape, lambda i, p: (0, 0, 0)) for t in chip_gots],
            out_specs=pl.BlockSpec((rows, HALF), lambda i, p: (0, 0))),
        compiler_params=_params(("arbitrary",)),
    )(pos_arr, g, pair_got, *chip_gots)


def _add_chips_many(items, fuse=NO_FUSE):
    n = len(items)
    rows = [it[0].shape[1] for it in items]
    n_got = [len(it[2]) for it in items]

    def body(*refs):
        g_refs, p_refs = refs[:n], refs[n:2 * n]
        got_refs = refs[2 * n:2 * n + sum(n_got)]
        o_refs = refs[2 * n + sum(n_got):3 * n + sum(n_got)]
        scr = refs[3 * n + sum(n_got):]
        x, y, c = _mesh_pos()
        chip = 2 * x + y
        mine = pl.ds(pl.multiple_of(c * HALF, 128), HALF)
        copies = []
        for a in range(n):
            copies.append((pltpu.make_async_copy(g_refs[a].at[chip, :, mine], scr[2 * a], scr[2 * n].at[2 * a]),
                           pltpu.make_async_copy(p_refs[a].at[chip], scr[2 * a + 1], scr[2 * n].at[2 * a + 1])))
        for cg, cp in copies:
            cg.start()
            cp.start()
        at_ref = 0
        for a in range(n):
            copies[a][0].wait()
            copies[a][1].wait()
            at = 0
            for r_ref in got_refs[at_ref:at_ref + n_got[a]]:
                k = r_ref.shape[1]
                own = scr[2 * a][at:at + k, :] + scr[2 * a + 1][at:at + k, :]
                o_refs[a][at:at + k, :] = ((own + r_ref[0].astype(F32)) + r_ref[1].astype(F32)) + r_ref[2].astype(F32)
                at += k
            at_ref += n_got[a]

    gots = [t for it in items for t in it[2]]
    outs, got = _fused_call(
        body, fuse, name="grad_add_chips_many", grid=(1,),
        in_specs=[ANY] * (2 * n) + [pl.BlockSpec(t.shape, lambda i: (0, 0, 0)) for t in gots],
        out_specs=[pl.BlockSpec((r, HALF), lambda i: (0, 0)) for r in rows],
        out_shape=[jax.ShapeDtypeStruct((r, HALF), F32) for r in rows],
        scratch_shapes=[pltpu.VMEM((r, HALF), F32) for r in rows for _ in range(2)] + [pltpu.SemaphoreType.DMA((2 * n,))],
        operands=(*[it[0] for it in items], *[it[1] for it in items], *gots))
    return outs, got


def _pack_small(acc2, acc1, lvec, cacc, gvec, dcw, dgup, dmeta):
    bsz = lvec.shape[0]

    def body(a2_ref, a1_ref, lv_ref, ca_ref, gv_ref, cw_ref, gu_ref, dm_ref, o_ref):
        def put(name, val):
            r0, nr, l0, nl = SMALL_AT[name]
            o_ref[r0:r0 + nr, l0:l0 + nl] = val

        def put_folded(row0, val, per_row):
            w = val.shape[1]
            for i in range(val.shape[0]):
                o_ref[row0 + i // per_row:row0 + i // per_row + 1, (i % per_row) * w:(i % per_row + 1) * w] = val[i:i + 1]

        over_b = lambda f: functools.reduce(lambda a, b: a + b, [f(b) for b in range(bsz)])
        o_ref[...] = jnp.zeros_like(o_ref)
        put("loss", a2_ref[0:1, :])
        put("ln2_g", a2_ref[1:2, :])
        put("ln2_b", a2_ref[2:3, :])
        put("ln1_g", a1_ref[0:1, :])
        put("ln1_b", a1_ref[1:2, :])
        put("ln_in_g", over_b(lambda b: lv_ref[b, 0:1, :]))
        put("ln_in_b", over_b(lambda b: lv_ref[b, 1:2, :]))
        put("conv_b", ca_ref[0:1, :])
        put("conv_ln_g", ca_ref[1:2, :])
        put("conv_ln_b", ca_ref[2:3, :])
        put("gate_bias", over_b(lambda b: gv_ref[b, 0:1, :]))
        put("gla_norm_g", over_b(lambda b: gv_ref[b, 1:2, 0:128] + gv_ref[b, 1:2, 128:256]))
        put_folded(SMALL_CONV_W, over_b(lambda b: cw_ref[b]), D_MODEL // D_CONV)
        put_folded(SMALL_GATE_UP, over_b(lambda b: gu_ref[b, 0:GLA_RANK, :]), D_MODEL // 256)
        put("meta_tokens", over_b(lambda b: dm_ref[b]))

    return pl.pallas_call(
        body, name="pack_small", out_shape=jax.ShapeDtypeStruct((SMALL_ROWS, D_MODEL), F32),
    )(acc2, acc1, lvec, cacc, gvec, dcw, dgup, dmeta)


def _sum_small(gathered):
    def body(g_ref, o_ref, loss_ref):
        acc = g_ref[0:SMALL_ROWS, :]
        for d in range(1, 8):
            acc = acc + g_ref[d * SMALL_ROWS:(d + 1) * SMALL_ROWS, :]
        o_ref[...] = acc
        loss_ref[...] = jnp.sum(acc[0:1, :], axis=1, keepdims=True)

    return pl.pallas_call(
        body, name="sum_small",
        out_shape=[jax.ShapeDtypeStruct((SMALL_ROWS, D_MODEL), F32), jax.ShapeDtypeStruct((1, 1), F32)],
    )(gathered)


def _adamw_math(w, g, m, v):
    c1 = 1.0 - ADAM_B1 ** ADAM_STEP
    c2 = 1.0 - ADAM_B2 ** ADAM_STEP
    mn = ADAM_B1 * m + (1.0 - ADAM_B1) * g
    vn = ADAM_B2 * v + (1.0 - ADAM_B2) * (g * g)
    return -ADAM_LR * ((mn / c1) / (jnp.sqrt(vn / c2) + ADAM_EPS) + ADAM_WD * w), mn, vn


def _adamw_small(red, sharded_grads, params):
    names = list(params)
    sharded = [k for k in names if k in sharded_grads]
    n, ns = len(names), len(sharded)

    def body(*refs):
        red_ref, sg_refs, p_refs, o_refs = refs[0], refs[1:1 + ns], refs[1 + ns:1 + ns + 3 * n], refs[1 + ns + 3 * n:]
        for i, k in enumerate(names):
            w_ref, m_ref, v_ref = p_refs[3 * i:3 * i + 3]
            if k in sharded:
                g = sg_refs[sharded.index(k)][...]
            else:
                r0, nr, l0, nl = SMALL_AT[k]
                g = red_ref[r0:r0 + nr, l0:l0 + nl]
            dl, mn, vn = _adamw_math(w_ref[...], g, m_ref[...], v_ref[...])
            for o_ref, val in zip(o_refs[4 * i:4 * i + 4], (g, dl, mn, vn)):
                o_ref[...] = val

    flat = [a for k in names for a in params[k]]
    res = pl.pallas_call(
        body, name="adamw_small",
        out_shape=[jax.ShapeDtypeStruct(params[k][0].shape, F32) for k in names for _ in range(4)],
    )(red, *[sharded_grads[k] for k in sharded], *flat)
    return {k: tuple(res[4 * i:4 * i + 4]) for i, k in enumerate(names)}


def _adamw_halves(w, own, sib, m, v, fuse=NO_FUSE):
    rows, cols = w.shape
    tile = 128
    by_rows = rows % tile == 0
    per_half = HALF // tile

    def body(w_ref, own_ref, sib_ref, m_ref, v_ref, g_ref, d_ref, mo_ref, vo_ref):
        c = _mesh_pos()[2]
        own, sib = own_ref[...], sib_ref[...]
        if by_rows:
            g = jnp.concatenate([jnp.where(c == 0, own, sib), jnp.where(c == 0, sib, own)], axis=1)
        else:
            g = jnp.where(pl.program_id(0) // per_half == c, own, sib)
        g_ref[...] = g
        d_ref[...], mo_ref[...], vo_ref[...] = _adamw_math(w_ref[...], g, m_ref[...], v_ref[...])

    if by_rows:
        grid = (rows // tile,)
        full = pl.BlockSpec((tile, cols), lambda i: (i, 0))
        half = pl.BlockSpec((tile, HALF), lambda i: (i, 0))
    else:
        grid = (cols // tile,)
        full = pl.BlockSpec((rows, tile), lambda j: (0, j))
        half = pl.BlockSpec((rows, tile), lambda j: (0, j % per_half))
    return _fused_call(
        body, fuse, name="adamw_halves", grid=grid, in_specs=[full, half, half, full, full],
        out_specs=[full] * 4, out_shape=[jax.ShapeDtypeStruct(w.shape, F32)] * 4, scratch_shapes=[],
        operands=(w, own, sib, m, v))


def _ln_in_fwd(x, g, b, fuse=NO_FUSE):
    bsz, s, d = x.shape
    tp = s + HEAD
    nh = 2
    sh = s // nh
    rc = min(256, sh)

    def body(x_ref, g_ref, b_ref, s0_ref, s0b_ref):
        h = pl.program_id(1)
        gg, bb = g_ref[...], b_ref[...]

        def step(i, carry):
            src = pl.ds(pl.multiple_of(i * rc, rc), rc)
            dst = pl.ds(pl.multiple_of(HEAD + h * sh + i * rc, 64), rc)
            xh, _ = _ln_stats(x_ref[0, src, :])
            val = xh * gg + bb
            s0_ref[0, dst, :] = val
            s0b_ref[0, dst, :] = val.astype(BF16)
            return carry

        lax.fori_loop(0, sh // rc, step, 0)

    full = lambda bi, hi: (bi, 0, 0)
    (s0, s0b), got = _fused_call(
        body, fuse, name="ln_in_fwd",
        out_shape=[jax.ShapeDtypeStruct((bsz, tp, d), F32), jax.ShapeDtypeStruct((bsz, tp, d), BF16)],
        grid=(bsz, nh),
        in_specs=[pl.BlockSpec((1, sh, d), lambda bi, hi: (bi, hi, 0)),
                  pl.BlockSpec((1, d), lambda bi, hi: (0, 0)),
                  pl.BlockSpec((1, d), lambda bi, hi: (0, 0))],
        out_specs=[pl.BlockSpec((1, tp, d), full)] * 2, scratch_shapes=[], operands=(x, g, b),
        late_relay=True)
    return s0, s0b, got


def _ln_meta(s0, s0b, meta, g, b):
    bsz, tp, d = s0.shape

    def body(s0_in, s0b_in, meta_ref, g_ref, b_ref, s0_ref, s0b_ref):
        mh, _ = _ln_stats(meta_ref[...])
        mv = mh * g_ref[...] + b_ref[...]
        s0_ref[0, 0:PAD, :] = jnp.zeros((PAD, d), F32)
        s0b_ref[0, 0:PAD, :] = jnp.zeros((PAD, d), BF16)
        s0_ref[0, PAD:HEAD, :] = mv
        s0b_ref[0, PAD:HEAD, :] = mv.astype(BF16)

    head = pl.BlockSpec((1, HEAD, d), lambda bi: (bi, 0, 0))
    vec = pl.BlockSpec((1, d), lambda bi: (0, 0))
    return pl.pallas_call(
        body, name="ln_meta", out_shape=[jax.ShapeDtypeStruct(s0.shape, F32), jax.ShapeDtypeStruct(s0b.shape, BF16)],
        grid=(bsz,), in_specs=[head, head, pl.BlockSpec((N_META, d), lambda bi: (0, 0)), vec, vec],
        out_specs=[head, head], input_output_aliases={0: 0, 1: 1},
        compiler_params=_params(("parallel",)),
    )(s0, s0b, meta, g, b)


def _in_proj(s0b, w_int, fuse=NO_FUSE):
    r, d = s0b.shape
    tm = _row_tile(r)

    def body(a_ref, w_ref, o_ref):
        o_ref[...] = _nt(a_ref[...], w_ref[...])

    (u,), got = _fused_call(
        body, fuse, name="in_proj", out_shape=[jax.ShapeDtypeStruct((r, D_IN_PAD), F32)],
        grid=(r // tm,),
        in_specs=[pl.BlockSpec((tm, d), lambda i: (i, 0)), pl.BlockSpec((D_IN_PAD, d), lambda i: (0, 0))],
        out_specs=[pl.BlockSpec((tm, D_IN_PAD), lambda i: (i, 0))], scratch_shapes=[], operands=(s0b, w_int))
    return u, got


def _conv_fwd(u, conv_w, conv_b, fuse=NO_FUSE):
    bsz, tp, _ = u.shape
    nchunk = tp // CHUNK
    win = CHUNK + 32
    nct = D_CONV // 128

    def body(cv_ref, cg_ref, w_ref, cb_ref, hc_ref, h_scr, win_scr):
        h_scr[0:32, :] = jnp.zeros((32, 128), F32)
        h_scr[32:32 + tp, :] = cv_ref[0] * _sigmoid(cg_ref[0])
        cb = cb_ref[...]

        def step(n, carry):
            r0 = pl.multiple_of(n * CHUNK, CHUNK)
            win_scr[...] = h_scr[pl.ds(r0, win), :]
            acc = jnp.zeros((CHUNK, 128), F32)
            for j in range(CONV_WIDTH):
                acc = acc + w_ref[j:j + 1, :] * win_scr[2 + j:2 + j + CHUNK, :]
            hc_ref[0, pl.ds(r0, CHUNK), :] = acc + cb
            return carry

        lax.fori_loop(0, nchunk, step, 0)

    (hc,), got = _fused_call(
        body, fuse, name="conv_fwd", out_shape=[jax.ShapeDtypeStruct((bsz, tp, D_CONV), F32)],
        grid=(bsz, nct),
        in_specs=[pl.BlockSpec((1, tp, 128), lambda bi, t: (bi, 0, C_VAL // 128 + t)),
                  pl.BlockSpec((1, tp, 128), lambda bi, t: (bi, 0, C_GATE // 128 + t)),
                  pl.BlockSpec((32, 128), lambda bi, t: (0, t)),
                  pl.BlockSpec((1, 128), lambda bi, t: (0, t))],
        out_specs=[pl.BlockSpec((1, tp, 128), lambda bi, t: (bi, 0, t))],
        scratch_shapes=[pltpu.VMEM((tp + 32, 128), F32), pltpu.VMEM((win, 128), F32)],
        operands=(u, u, conv_w, conv_b))
    return hc, got


def _gla_group(nchunk):
    return 11 if nchunk % 11 == 0 else nchunk


def _bdot(a, b, ca, cb, precision=None):
    return lax.dot_general(a, b, (((ca,), (cb,)), ((0,), (0,))), preferred_element_type=F32, precision=precision)


def _bnn(a, b, **kw):
    return _bdot(a, b, 2, 1, **kw)


def _bnt(a, b, **kw):
    return _bdot(a, b, 2, 2, **kw)


def _gla_consts(nb):
    row = lax.broadcasted_iota(jnp.int32, (nb, CHUNK, CHUNK), 1)
    col = lax.broadcasted_iota(jnp.int32, (nb, CHUNK, CHUNK), 2)
    lane = lax.broadcasted_iota(jnp.int32, (1, 1, 128), 2)
    return row >= col, row <= col, [lane < GLA_DK, lane >= GLA_DK]


def _gla_group_terms(g, nb, q_ref, k_ref, gd_ref, gup_ref, gb_ref, tril):
    m = nb * CHUNK
    rows = pl.ds(pl.multiple_of(g * m, CHUNK), m)
    z = _nn(gd_ref[0, rows, :].astype(BF16), gup_ref[...]) + gb_ref[...]
    valid = g * m + lax.broadcasted_iota(jnp.int32, (m, 1), 0) >= PAD
    lg = jnp.where(valid, _log_sigmoid(z) * (1.0 / GLA_TAU), 0.0)
    bcum = _bnn(tril.astype(F32), lg.reshape(nb, CHUNK, 128), precision=lax.Precision.HIGHEST)
    blast = bcum[:, CHUNK - 1:CHUNK, :]
    eb = jnp.exp(bcum)
    enb = jnp.exp(-bcum)
    erest = jnp.exp(blast - bcum)
    q = (q_ref[0, rows, :] * Q_SCALE).reshape(nb, CHUNK, 128)
    k = k_ref[0, rows, :].reshape(nb, CHUNK, 128)
    return rows, valid, z, eb, enb, erest, jnp.exp(blast), q * eb, k * enb, k * erest


def _grid_ends(grid):
    ids = [pl.program_id(i) for i in range(len(grid))]
    first = functools.reduce(jnp.logical_and, [i == 0 for i in ids])
    last = functools.reduce(jnp.logical_and, [i == g - 1 for i, g in zip(ids, grid)])
    return first, last


def _gla_fwd(u, gup, gbias, gnorm, fuse=NO_FUSE):
    bsz, tp, _ = u.shape
    nchunk = tp // CHUNK
    nb = _gla_group(nchunk)

    def body(q_ref, k_ref, v_ref, r_ref, gd_ref, gup_ref, gb_ref, gn_ref, out_ref, o_ref, st_ref, s_scr):
        tril, _, hmask = _gla_consts(nb)
        s_scr[...] = jnp.zeros_like(s_scr)
        gn = gn_ref[...]

        def group(g, carry):
            rows, _, _, _, _, _, dec, qe, ke, kd = _gla_group_terms(g, nb, q_ref, k_ref, gd_ref, gup_ref, gb_ref, tril)
            keb, kdb = ke.astype(BF16), kd.astype(BF16)
            for h in range(2):
                cols = slice(h * GLA_DV, (h + 1) * GLA_DV)
                qh = jnp.where(hmask[h], qe, 0.0).astype(BF16)
                vh = v_ref[0, rows, cols].astype(BF16).reshape(nb, CHUNK, GLA_DV)
                a = jnp.where(tril, _bnt(qh, keb), 0.0).astype(BF16)
                st = s_scr[h]
                sts = []
                for n in range(nb):
                    st_ref[0, h, g * nb + n] = st
                    sts.append(st.astype(BF16))
                    st = dec[n] * st + _tn(vh[n], kdb[n])
                s_scr[h] = st
                o = (_bnn(a, vh) + _bnt(qh, jnp.stack(sts))).reshape(nb * CHUNK, GLA_DV)
                o_ref[0, rows, cols] = o
                rms = lax.rsqrt(jnp.mean(o * o, axis=-1, keepdims=True) + LN_EPS)
                rh = r_ref[0, rows, cols]
                out_ref[0, rows, cols] = (o * rms * gn * (rh * _sigmoid(rh))).astype(BF16)
            return carry

        lax.fori_loop(0, nchunk // nb, group, 0)

    res, got = _fused_call(
        body, fuse, name="gla_fwd",
        out_shape=[jax.ShapeDtypeStruct((bsz, tp, 512), BF16), jax.ShapeDtypeStruct((bsz, tp, 512), F32),
                   jax.ShapeDtypeStruct((bsz, GLA_HEADS, nchunk, GLA_DV, 128), F32)],
        grid=(bsz, 2),
        in_specs=[pl.BlockSpec((1, tp, 128), lambda bi, p: (bi, 0, C_Q // 128 + p)),
                  pl.BlockSpec((1, tp, 128), lambda bi, p: (bi, 0, C_K // 128 + p)),
                  pl.BlockSpec((1, tp, 256), lambda bi, p: (bi, 0, C_V // 256 + p)),
                  pl.BlockSpec((1, tp, 256), lambda bi, p: (bi, 0, C_R // 256 + p)),
                  pl.BlockSpec((1, tp, 128), lambda bi, p: (bi, 0, C_GD // 128)),
                  pl.BlockSpec((128, 128), lambda bi, p: (0, p)),
                  pl.BlockSpec((1, 128), lambda bi, p: (0, p)),
                  pl.BlockSpec((1, 128), lambda bi, p: (0, 0))],
        out_specs=[pl.BlockSpec((1, tp, 256), lambda bi, p: (bi, 0, p)),
                   pl.BlockSpec((1, tp, 256), lambda bi, p: (bi, 0, p)),
                   pl.BlockSpec((1, 2, nchunk, GLA_DV, 128), lambda bi, p: (bi, p, 0, 0, 0))],
        scratch_shapes=[pltpu.VMEM((2, GLA_DV, 128), F32)],
        operands=(u, u, u, u, u, gup, gbias, gnorm))
    return res[0], res[1], res[2], got


def _out_proj_ln1(hc, gla_out, w_out, s0, cg, cb, g1, b1, fuse=NO_FUSE):
    r, d = s0.shape
    tm = _row_tile(r)

    def body(hc_ref, a_ref, w_ref, s0_ref, cg_ref, cb_ref, g_ref, b_ref, co_ref, xh_ref, rstd_ref, s1b_ref):
        for rs in _sub_rows(tm):
            xc, _ = _ln_stats(hc_ref[rs, :])
            nv = xc * cg_ref[...] + cb_ref[...]
            co = (nv * _sigmoid(nv)).astype(BF16)
            co_ref[rs, :] = co
            mix = _nn(co, w_ref[0:D_CONV, :]) + _nn(a_ref[rs, :], w_ref[D_CONV:, :])
            xh, rstd = _ln_stats(ALPHA * s0_ref[rs, :] + mix)
            xh_ref[rs, :] = xh
            rstd_ref[rs, :] = rstd
            s1b_ref[rs, :] = (xh * g_ref[...] + b_ref[...]).astype(BF16)

    row = lambda n: pl.BlockSpec((tm, n), lambda i: (i, 0))
    vec = lambda n: pl.BlockSpec((1, n), lambda i: (0, 0))
    res, got = _fused_call(
        body, fuse, name="out_proj_ln1",
        out_shape=[jax.ShapeDtypeStruct((r, D_CONV), BF16), jax.ShapeDtypeStruct((r, d), F32),
                   jax.ShapeDtypeStruct((r, 1), F32), jax.ShapeDtypeStruct((r, d), BF16)],
        grid=(r // tm,),
        in_specs=[row(D_CONV), row(512), pl.BlockSpec((d, d), lambda i: (0, 0)), row(d),
                  vec(D_CONV), vec(D_CONV), vec(d), vec(d)],
        out_specs=[row(D_CONV), row(d), row(1), row(d)], scratch_shapes=[],
        operands=(hc, gla_out, w_out, s0, cg, cb, g1, b1))
    return res[0], res[1], res[2], res[3], got


def _ffn1(s1b, w1, fuse=NO_FUSE):
    r, d = s1b.shape
    tm = _row_tile(r)
    ns, _, wn = w1.shape

    def body(a_ref, w_ref, o_ref):
        a = a_ref[...]
        for j in range(ns):
            o_ref[:, j * wn:(j + 1) * wn] = jnp.maximum(_nn(a, w_ref[j]), 0.0).astype(BF16)

    (ra,), got = _fused_call(
        body, fuse, name="ffn1", out_shape=[jax.ShapeDtypeStruct((r, D_FF), BF16)], grid=(r // tm,),
        in_specs=[pl.BlockSpec((tm, d), lambda i: (i, 0)), pl.BlockSpec(w1.shape, lambda i: (0, 0, 0))],
        out_specs=[pl.BlockSpec((tm, D_FF), lambda i: (i, 0))], scratch_shapes=[], operands=(s1b, w1))
    return ra, got


def _ffn2_ln2_loss(ra, w2, xhat1, g1, b1, g2, b2, tgt, tp):
    r, d = xhat1.shape
    tm = _row_tile(tp)
    per = tp // tm

    def body(ra_ref, w_ref, xh1_ref, g1_ref, b1_ref, g2_ref, b2_ref, tgt_ref, dt_ref, dtb_ref, acc_ref, t_ref, sem):
        i = pl.program_id(0)
        b, j = i // per, i % per

        @pl.when(i == 0)
        def _():
            acc_ref[...] = jnp.zeros_like(acc_ref)

        head_copy = pltpu.make_async_copy(tgt_ref.at[b, pl.ds(0, tm - HEAD), :], t_ref.at[pl.ds(HEAD, tm - HEAD), :], sem)
        body_copy = pltpu.make_async_copy(
            tgt_ref.at[b, pl.ds(pl.multiple_of(jnp.maximum(j * tm - HEAD, 0), 64), tm), :], t_ref, sem)

        @pl.when(j == 0)
        def _():
            t_ref[0:HEAD, :] = jnp.zeros((HEAD, d), F32)
            head_copy.start()

        pl.when(j > 0)(body_copy.start)

        sums = [jnp.zeros((1, d), F32)] * 3
        for rs in _sub_rows(tm):
            rb = ra_ref[rs, :]
            f = _nn(rb * rb, w_ref[...])
            if rs.start == 0:
                pl.when(j == 0)(head_copy.wait)
                pl.when(j > 0)(body_copy.wait)
            s1 = xh1_ref[rs, :] * g1_ref[...] + b1_ref[...]
            xh2, rstd2 = _ln_stats(ALPHA * s1 + f)
            y = xh2 * g2_ref[...] + b2_ref[...]
            rowid = (i % per) * tm + rs.start + lax.broadcasted_iota(jnp.int32, (rs.stop - rs.start, 1), 0)
            e = jnp.where(rowid >= HEAD, y - t_ref[rs, :], 0.0)
            dy = e * (1.0 / d)
            dt2 = _ln_bwd(dy * g2_ref[...], xh2, rstd2)
            dt_ref[rs, :] = dt2
            dtb_ref[rs, :] = dt2.astype(BF16)
            sums = [sums[0] + (0.5 / d) * jnp.sum(e * e, axis=0, keepdims=True),
                    sums[1] + jnp.sum(dy * xh2, axis=0, keepdims=True), sums[2] + jnp.sum(dy, axis=0, keepdims=True)]
        for k in range(3):
            acc_ref[k:k + 1, :] += sums[k]

    row = lambda n: pl.BlockSpec((tm, n), lambda i: (i, 0))
    vec = pl.BlockSpec((1, d), lambda i: (0, 0))
    return pl.pallas_call(
        body, name="ffn2_ln2_loss",
        out_shape=[jax.ShapeDtypeStruct((r, d), F32), jax.ShapeDtypeStruct((r, d), BF16),
                   jax.ShapeDtypeStruct((8, d), F32)],
        grid=(r // tm,),
        in_specs=[row(D_FF), pl.BlockSpec((D_FF, d), lambda i: (0, 0)), row(d), vec, vec, vec, vec, ANY],
        out_specs=[row(d), row(d), pl.BlockSpec((8, d), lambda i: (0, 0))],
        scratch_shapes=[pltpu.VMEM((tm, d), F32), pltpu.SemaphoreType.DMA],
        compiler_params=_params(("arbitrary",)),
    )(ra, w2, xhat1, g1, b1, g2, b2, tgt)


def _ffn_bwd_da(dt2b, w2, ra):
    r, d = dt2b.shape
    tm = _row_tile(r)

    def body(g_ref, w_ref, ra_ref, o_ref):
        o_ref[...] = (_nt(g_ref[...], w_ref[...]) * (2.0 * ra_ref[...].astype(F32))).astype(BF16)

    return pl.pallas_call(
        body, name="ffn_bwd_da", out_shape=jax.ShapeDtypeStruct((r, D_FF), BF16),
        grid=(r // tm,),
        in_specs=[pl.BlockSpec((tm, d), lambda i: (i, 0)), pl.BlockSpec((D_FF, d), lambda i: (0, 0)),
                  pl.BlockSpec((tm, D_FF), lambda i: (i, 0))],
        out_specs=pl.BlockSpec((tm, D_FF), lambda i: (i, 0)),
        compiler_params=_params(("parallel",)),
    )(dt2b, w2, ra)


def _ffn_bwd_ln1(da, w1, dt2, xhat1, rstd1, g1):
    r, d = dt2.shape
    tm = _row_tile(r)

    def body(da_ref, w_ref, dt2_ref, xh_ref, rstd_ref, g_ref, dt_ref, dtb_ref, acc_ref):
        @pl.when(pl.program_id(0) == 0)
        def _():
            acc_ref[...] = jnp.zeros_like(acc_ref)

        sums = [jnp.zeros((1, d), F32)] * 2
        for rs in _sub_rows(tm):
            ds1 = ALPHA * dt2_ref[rs, :]
            for j in range(w1.shape[0]):
                ds1 = ds1 + _nt(da_ref[rs, j * w1.shape[2]:(j + 1) * w1.shape[2]], w_ref[j])
            xh = xh_ref[rs, :]
            dt1 = _ln_bwd(ds1 * g_ref[...], xh, rstd_ref[rs, :])
            dt_ref[rs, :] = dt1
            dtb_ref[rs, :] = dt1.astype(BF16)
            sums = [sums[0] + jnp.sum(ds1 * xh, axis=0, keepdims=True), sums[1] + jnp.sum(ds1, axis=0, keepdims=True)]
        for k in range(2):
            acc_ref[k:k + 1, :] += sums[k]

    row = lambda n: pl.BlockSpec((tm, n), lambda i: (i, 0))
    return pl.pallas_call(
        body, name="ffn_bwd_ln1",
        out_shape=[jax.ShapeDtypeStruct((r, d), F32), jax.ShapeDtypeStruct((r, d), BF16),
                   jax.ShapeDtypeStruct((8, d), F32)],
        grid=(r // tm,),
        in_specs=[row(D_FF), pl.BlockSpec(w1.shape, lambda i: (0, 0, 0)), row(d), row(d), row(1),
                  pl.BlockSpec((1, d), lambda i: (0, 0))],
        out_specs=[row(d), row(d), pl.BlockSpec((8, d), lambda i: (0, 0))],
        compiler_params=_params(("arbitrary",)),
    )(da, w1, dt2, xhat1, rstd1, g1)


def _matmul_tn(lhs, rhs, bm, square_lhs=False, name="matmul_tn", fuse=NO_FUSE):
    r, m = lhs.shape
    n = rhs.shape[1]
    tk = _reduce_tile(r, True)

    def body(a_ref, b_ref, o_ref):
        @pl.when(pl.program_id(1) == 0)
        def _():
            o_ref[...] = jnp.zeros_like(o_ref)

        a = a_ref[...]
        if square_lhs:
            a = a * a
        o_ref[...] += _tn(a, b_ref[...])

    (out,), got = _fused_call(
        body, fuse, name=name, out_shape=[jax.ShapeDtypeStruct((m, n), F32)], grid=(m // bm, r // tk),
        in_specs=[pl.BlockSpec((tk, bm), lambda i, k: (k, i)), pl.BlockSpec((tk, n), lambda i, k: (k, 0))],
        out_specs=[pl.BlockSpec((bm, n), lambda i, k: (i, 0))], scratch_shapes=[], operands=(lhs, rhs))
    return out, got


def _grad_w_ff1(s1b, da, fuse=NO_FUSE):
    r, d = s1b.shape
    wn = da.shape[1] // 4
    tk = _reduce_tile(r, True)

    def body(a_ref, b_ref, o_ref):
        @pl.when(pl.program_id(1) == 0)
        def _():
            o_ref[...] = jnp.zeros_like(o_ref)

        o_ref[0] += _tn(a_ref[...], b_ref[...])

    (out,), got = _fused_call(
        body, fuse, name="grad_w_ff1", out_shape=[jax.ShapeDtypeStruct((4, d, wn), F32)], grid=(4, r // tk),
        in_specs=[pl.BlockSpec((tk, d), lambda j, k: (k, 0)), pl.BlockSpec((tk, wn), lambda j, k: (k, j))],
        out_specs=[pl.BlockSpec((1, d, wn), lambda j, k: (j, 0, 0))], scratch_shapes=[], operands=(s1b, da))
    return out, got


def _grad_w_out(conv_of, gla_of, dt1b, fuse=NO_FUSE):
    r, n = dt1b.shape
    tk = _reduce_tile(r, True)

    def body(a_ref, b_ref, g_ref, o_ref):
        @pl.when(pl.program_id(1) == 0)
        def _():
            o_ref[...] = jnp.zeros_like(o_ref)

        @pl.when(pl.program_id(0) == 0)
        def _():
            o_ref[...] += _tn(a_ref[...], g_ref[...])

        @pl.when(pl.program_id(0) == 1)
        def _():
            o_ref[...] += _tn(b_ref[...], g_ref[...])

    lhs = pl.BlockSpec((tk, 512), lambda i, k: (k, 0))
    (out,), got = _fused_call(
        body, fuse, name="grad_w_out", out_shape=[jax.ShapeDtypeStruct((2 * 512, n), F32)], grid=(2, r // tk),
        in_specs=[lhs, lhs, pl.BlockSpec((tk, n), lambda i, k: (k, 0))],
        out_specs=[pl.BlockSpec((512, n), lambda i, k: (i, 0))], scratch_shapes=[],
        operands=(conv_of, gla_of, dt1b))
    return out, got


def _out_proj_bwd(dt1b, w_out, hc, cg, cb, fuse=NO_FUSE):
    r, d = dt1b.shape
    tm = _row_tile(r)

    def body(g_ref, w_ref, hc_ref, cg_ref, cb_ref, dhc_ref, dgla_ref, acc_ref):
        @pl.when(pl.program_id(0) == 0)
        def _():
            acc_ref[...] = jnp.zeros_like(acc_ref)

        gg = cg_ref[...]
        sums = [jnp.zeros((1, D_CONV), F32)] * 3
        for rs in _sub_rows(tm):
            dmix = _nt(g_ref[rs, :], w_ref[...])
            dgla_ref[rs, :] = dmix[:, D_CONV:]
            xh, rstd = _ln_stats(hc_ref[rs, :])
            nv = xh * gg + cb_ref[...]
            sig = _sigmoid(nv)
            dn = dmix[:, :D_CONV] * (sig * (1.0 + nv * (1.0 - sig)))
            dhc = _ln_bwd(dn * gg, xh, rstd)
            dhc_ref[rs, :] = dhc
            sums = [sums[0] + jnp.sum(dhc, axis=0, keepdims=True), sums[1] + jnp.sum(dn * xh, axis=0, keepdims=True),
                    sums[2] + jnp.sum(dn, axis=0, keepdims=True)]
        for k in range(3):
            acc_ref[k:k + 1, :] += sums[k]

    row = lambda n: pl.BlockSpec((tm, n), lambda i: (i, 0))
    vec = pl.BlockSpec((1, D_CONV), lambda i: (0, 0))
    res, got = _fused_call(
        body, fuse, name="out_proj_bwd",
        out_shape=[jax.ShapeDtypeStruct((r, D_CONV), F32), jax.ShapeDtypeStruct((r, 512), F32),
                   jax.ShapeDtypeStruct((8, D_CONV), F32)],
        grid=(r // tm,),
        in_specs=[row(d), pl.BlockSpec((d, d), lambda i: (0, 0)), row(D_CONV), vec, vec],
        out_specs=[row(D_CONV), row(512), pl.BlockSpec((8, D_CONV), lambda i: (0, 0))], scratch_shapes=[],
        operands=(dt1b, w_out, hc, cg, cb))
    return res[0], res[1], res[2], got


def _conv_bwd(dhc, u, conv_w, fuse=NO_FUSE):
    bsz, tp, _ = u.shape
    nchunk = tp // CHUNK
    win = CHUNK + 32
    nct = D_CONV // 128

    def body(dhc_ref, cv_ref, cg_ref, w_ref, dv_ref, dg_ref, dw_ref, h_scr, dhc_scr, hwin, dwin, dw_scr):
        h_scr[0:32, :] = jnp.zeros((32, 128), F32)
        h_scr[32:32 + tp, :] = cv_ref[0] * _sigmoid(cg_ref[0])
        dhc_scr[0:tp, :] = dhc_ref[0]
        dhc_scr[tp:tp + 32, :] = jnp.zeros((32, 128), F32)
        dw_scr[...] = jnp.zeros_like(dw_scr)

        def step(n, carry):
            r0 = pl.multiple_of(n * CHUNK, CHUNK)
            rows = pl.ds(r0, CHUNK)
            hwin[...] = h_scr[pl.ds(r0, win), :]
            dwin[...] = dhc_scr[pl.ds(r0, win), :]
            dcur = dwin[0:CHUNK, :]
            acc = jnp.zeros((CHUNK, 128), F32)
            for j in range(CONV_WIDTH):
                acc = acc + w_ref[j:j + 1, :] * dwin[30 - j:30 - j + CHUNK, :]
                prod = dcur * hwin[2 + j:2 + j + CHUNK, :]
                dw_scr[j * 8:(j + 1) * 8, :] += jnp.sum(prod.reshape(CHUNK // 8, 8, 128), axis=0)
            cg = cg_ref[0, rows, :]
            sig = _sigmoid(cg)
            rowid = n * CHUNK + lax.broadcasted_iota(jnp.int32, (CHUNK, 1), 0)
            dh = jnp.where(rowid >= PAD, acc, 0.0)
            dv_ref[0, rows, :] = (dh * sig).astype(BF16)
            dg_ref[0, rows, :] = (dh * cv_ref[0, rows, :] * sig * (1.0 - sig)).astype(BF16)
            return carry

        lax.fori_loop(0, nchunk, step, 0)
        dw_ref[0] = jnp.zeros((32, 128), F32)
        for j in range(CONV_WIDTH):
            dw_ref[0, j:j + 1, :] = jnp.sum(dw_scr[j * 8:(j + 1) * 8, :], axis=0, keepdims=True)

    blk = lambda off: pl.BlockSpec((1, tp, 128), lambda bi, t: (bi, 0, off // 128 + t))
    res, got = _fused_call(
        body, fuse, name="conv_bwd",
        out_shape=[jax.ShapeDtypeStruct((bsz, tp, D_CONV), BF16), jax.ShapeDtypeStruct((bsz, tp, D_CONV), BF16),
                   jax.ShapeDtypeStruct((bsz, 32, D_CONV), F32)],
        grid=(bsz, nct),
        in_specs=[blk(0), blk(C_VAL), blk(C_GATE), pl.BlockSpec((32, 128), lambda bi, t: (0, t))],
        out_specs=[blk(0), blk(0), pl.BlockSpec((1, 32, 128), lambda bi, t: (bi, 0, t))],
        scratch_shapes=[pltpu.VMEM((tp + 32, 128), F32), pltpu.VMEM((tp + 32, 128), F32),
                        pltpu.VMEM((win, 128), F32), pltpu.VMEM((win, 128), F32),
                        pltpu.VMEM((CONV_WIDTH * 8, 128), F32)],
        operands=(dhc, u, u, conv_w))
    return res[0], res[1], res[2], got


def _gla_bwd(dgla, u, o_pre, states, gup, gbias, gnorm, fuse=NO_FUSE):
    bsz, tp, _ = u.shape
    nchunk = tp // CHUNK
    nb = _gla_group(nchunk)

    def body(dy_ref, q_ref, k_ref, v_ref, r_ref, gd_ref, o_ref, st_ref, gup_ref, gb_ref, gn_ref,
             dq_ref, dk_ref, dv_ref, dr_ref, dgd_ref, dgup_ref, vec_ref, h_scr, gup_acc):
        tril, triu, hmask = _gla_consts(nb)
        h_scr[...] = jnp.zeros_like(h_scr)
        gup_acc[...] = jnp.zeros_like(gup_acc)
        gn = gn_ref[...]
        gupb = gup_ref[...]
        m = nb * CHUNK
        ngroup = nchunk // nb

        def group(i, carry):
            dbias, dgn = carry
            g = ngroup - 1 - i
            rows, valid, z, eb, enb, erest, dec, qe, ke, kd = _gla_group_terms(
                g, nb, q_ref, k_ref, gd_ref, gup_ref, gb_ref, tril)
            keb, kdb = ke.astype(BF16), kd.astype(BF16)
            dqe = jnp.zeros((nb, CHUNK, 128), F32)
            dke = jnp.zeros((nb, CHUNK, 128), F32)
            dkd = jnp.zeros((nb, CHUNK, 128), F32)
            ddec = jnp.zeros((nb, 1, 128), F32)
            for h in range(2):
                cols = slice(h * GLA_DV, (h + 1) * GLA_DV)
                o = o_ref[0, rows, cols]
                rh = r_ref[0, rows, cols]
                dy = dy_ref[0, rows, cols]
                rms = lax.rsqrt(jnp.mean(o * o, axis=-1, keepdims=True) + LN_EPS)
                nrm = o * rms
                sig = _sigmoid(rh)
                sw = rh * sig
                dr_ref[0, rows, cols] = (dy * nrm * gn * (sig * (1.0 + rh * (1.0 - sig)))).astype(BF16)
                dgn = dgn + jnp.sum(dy * nrm * sw, axis=0, keepdims=True)
                dn = dy * gn * sw
                do = rms * (dn - nrm * jnp.mean(dn * nrm, axis=-1, keepdims=True))
                dob = do.astype(BF16).reshape(nb, CHUNK, GLA_DV)
                qh = jnp.where(hmask[h], qe, 0.0).astype(BF16)
                vh = v_ref[0, rows, cols].astype(BF16).reshape(nb, CHUNK, GLA_DV)
                ht = h_scr[h]
                hts = [None] * nb
                for n in reversed(range(nb)):
                    hts[n] = ht
                    ht = dec[n] * ht + _tn(dob[n], qh[n])
                h_scr[h] = ht
                htf = jnp.stack(hts)
                htb = htf.astype(BF16)
                st = st_ref[0, h, pl.ds(g * nb, nb)]
                at = jnp.where(triu, _bnt(keb, qh), 0.0).astype(BF16)
                da = jnp.where(tril, _bnt(dob, vh), 0.0).astype(BF16)
                dat = jnp.where(triu, _bnt(vh, dob), 0.0).astype(BF16)
                dqe = dqe + jnp.where(hmask[h], _bnn(da, keb) + _bnn(dob, st.astype(BF16)), 0.0)
                dke = dke + _bnn(dat, qh)
                dv_ref[0, rows, cols] = (_bnn(at, dob) + _bnt(kdb, htb)).reshape(m, GLA_DV).astype(BF16)
                dkd = dkd + jnp.where(hmask[h], _bnn(vh, htb), 0.0)
                ddec = ddec + jnp.where(hmask[h], jnp.sum(htf * st, axis=1, keepdims=True), 0.0)
            dq_ref[0, rows, :] = (dqe * eb * Q_SCALE).reshape(m, 128).astype(BF16)
            dk_ref[0, rows, :] = (dke * enb + dkd * erest).reshape(m, 128).astype(BF16)
            db = dqe * qe - dke * ke - dkd * kd
            dblast = jnp.sum(dkd * kd, axis=1, keepdims=True) + ddec * dec
            lastrow = lax.broadcasted_iota(jnp.int32, (1, CHUNK, 1), 1) == CHUNK - 1
            db = db + jnp.where(lastrow, dblast, 0.0)
            dlg = _bnn(triu.astype(F32), db, precision=lax.Precision.HIGHEST).reshape(m, 128)
            dz = jnp.where(valid, dlg, 0.0) * (1.0 / GLA_TAU) * (1.0 - _sigmoid(z))
            dzb = dz.astype(BF16)
            dgd_ref[0, 0, rows, :] = _nt(dzb, gupb).astype(BF16)
            gup_acc[...] += _tn(gd_ref[0, rows, :].astype(BF16), dzb)
            return dbias + jnp.sum(dz, axis=0, keepdims=True), dgn

        zero = jnp.zeros((1, 128), F32)
        dbias, dgn = lax.fori_loop(0, ngroup, group, (zero, zero))
        dgup_ref[0] = gup_acc[...]
        vec_ref[0] = jnp.zeros((8, 128), F32)
        vec_ref[0, 0:1, :] = dbias
        vec_ref[0, 1:2, :] = dgn

    pair = lambda w, off: pl.BlockSpec((1, tp, w), lambda bi, p: (bi, 0, off // w + p))
    return _fused_call(
        body, fuse, name="gla_bwd",
        out_shape=[jax.ShapeDtypeStruct((bsz, tp, 256), BF16), jax.ShapeDtypeStruct((bsz, tp, 256), BF16),
                   jax.ShapeDtypeStruct((bsz, tp, 512), BF16), jax.ShapeDtypeStruct((bsz, tp, 512), BF16),
                   jax.ShapeDtypeStruct((bsz, 2, tp, 128), BF16), jax.ShapeDtypeStruct((bsz, 128, 256), F32),
                   jax.ShapeDtypeStruct((bsz, 8, 256), F32)],
        grid=(bsz, 2),
        in_specs=[pair(256, 0), pair(128, C_Q), pair(128, C_K), pair(256, C_V), pair(256, C_R),
                  pl.BlockSpec((1, tp, 128), lambda bi, p: (bi, 0, C_GD // 128)),
                  pair(256, 0),
                  pl.BlockSpec((1, 2, nchunk, GLA_DV, 128), lambda bi, p: (bi, p, 0, 0, 0)),
                  pl.BlockSpec((128, 128), lambda bi, p: (0, p)),
                  pl.BlockSpec((1, 128), lambda bi, p: (0, p)),
                  pl.BlockSpec((1, 128), lambda bi, p: (0, 0))],
        out_specs=[pair(128, 0), pair(128, 0), pair(256, 0), pair(256, 0),
                   pl.BlockSpec((1, 1, tp, 128), lambda bi, p: (bi, p, 0, 0)),
                   pl.BlockSpec((1, 128, 128), lambda bi, p: (bi, 0, p)),
                   pl.BlockSpec((1, 8, 128), lambda bi, p: (bi, 0, p))],
        scratch_shapes=[pltpu.VMEM((2, GLA_DV, 128), F32), pltpu.VMEM((128, 128), F32)],
        operands=(dgla, u, u, u, u, u, o_pre, states, gup, gbias, gnorm))


_DU_OFFSETS = (C_VAL, C_GATE, C_Q, C_K, C_V, C_R)
_DU_WIDTHS = (512, 512, 256, 256, 512, 512)


def _du_specs(tm, per, row_map):
    specs = [pl.BlockSpec((tm, w), row_map) for w in _DU_WIDTHS]
    for p in range(2):
        specs.append(pl.BlockSpec((1, 1, tm, 128), lambda *ix, p=p: (row_map(*ix)[0] // per, p, row_map(*ix)[0] % per, 0)))
    return specs


def _du_pieces(refs):
    out = [(off, ref[...]) for off, ref in zip(_DU_OFFSETS, refs[:6])]
    dgd = (refs[6][0, 0].astype(F32) + refs[7][0, 0].astype(F32)).astype(BF16)
    out.append((C_GD, dgd))
    return out


def _in_proj_bwd(pieces, dgd, w_int, dt1, tp, fuse=NO_FUSE):
    r, d = dt1.shape
    tm = _row_tile(tp)
    per = tp // tm

    def body(*refs):
        w_ref, dt_ref, o_ref = refs[8:]
        acc = ALPHA * dt_ref[...]
        for off, val in _du_pieces(refs[:8]):
            acc = acc + _nn(val, w_ref[off:off + val.shape[1], :])
        o_ref[...] = acc

    row = lambda i: (i, 0)
    (ds0,), got = _fused_call(
        body, fuse, name="in_proj_bwd", out_shape=[jax.ShapeDtypeStruct((r, d), F32)], grid=(r // tm,),
        in_specs=_du_specs(tm, per, row) + [pl.BlockSpec((D_IN_PAD, d), lambda i: (0, 0)), pl.BlockSpec((tm, d), row)],
        out_specs=[pl.BlockSpec((tm, d), row)], scratch_shapes=[], operands=(*pieces, dgd, dgd, w_int, dt1))
    return ds0, got


def _grad_w_in(pieces, dgd, s0b, tp, fuse=NO_FUSE):
    r, d = s0b.shape
    tk = _reduce_tile(tp, False)
    per = tp // tk

    def body(*refs):
        s_ref, o_ref = refs[8:]

        @pl.when(pl.program_id(0) == 0)
        def _():
            o_ref[...] = jnp.zeros_like(o_ref)

        s = s_ref[...]
        for off, val in _du_pieces(refs[:8]):
            o_ref[off:off + val.shape[1], :] += _tn(val, s)

    row = lambda k: (k, 0)
    (out,), got = _fused_call(
        body, fuse, name="grad_w_in", out_shape=[jax.ShapeDtypeStruct((D_IN_PAD, d), F32)], grid=(r // tk,),
        in_specs=_du_specs(tk, per, row) + [pl.BlockSpec((tk, d), row)],
        out_specs=[pl.BlockSpec((D_IN_PAD, d), lambda k: (0, 0))], scratch_shapes=[],
        operands=(*pieces, dgd, dgd, s0b))
    return out, got


def _ln_in_bwd(ds0, x, meta, g):
    bsz, s, d = x.shape
    tp = s + HEAD
    nh = 2
    sh = s // nh
    rc = min(256, sh)

    def body(ds_ref, x_ref, meta_ref, g_ref, gx_ref, dm_ref, vec_ref):
        h = pl.program_id(1)
        gg = g_ref[...]

        @pl.when(h == 0)
        def _():
            mh, mr = _ln_stats(meta_ref[...])
            dsm = ds_ref[0, PAD:HEAD, :]
            dm_ref[0] = _ln_bwd(dsm * gg, mh, mr)
            vec_ref[0] = jnp.zeros((8, d), F32)
            vec_ref[0, 0:1, :] = jnp.sum(dsm * mh, axis=0, keepdims=True)
            vec_ref[0, 1:2, :] = jnp.sum(dsm, axis=0, keepdims=True)

        def step(i, carry):
            sg, sb = carry
            dst = pl.ds(pl.multiple_of(i * rc, rc), rc)
            src = pl.ds(pl.multiple_of(HEAD + h * sh + i * rc, 64), rc)
            xh, rstd = _ln_stats(x_ref[0, dst, :])
            dsv = ds_ref[0, src, :]
            gx_ref[0, dst, :] = _ln_bwd(dsv * gg, xh, rstd)
            return sg + jnp.sum(dsv * xh, axis=0, keepdims=True), sb + jnp.sum(dsv, axis=0, keepdims=True)

        zero = jnp.zeros((1, d), F32)
        sg, sb = lax.fori_loop(0, sh // rc, step, (zero, zero))
        vec_ref[0, 0:1, :] += sg
        vec_ref[0, 1:2, :] += sb

    return pl.pallas_call(
        body, name="ln_in_bwd",
        out_shape=[jax.ShapeDtypeStruct((bsz, s, d), F32), jax.ShapeDtypeStruct((bsz, N_META, d), F32),
                   jax.ShapeDtypeStruct((bsz, 8, d), F32)],
        grid=(bsz, nh),
        in_specs=[pl.BlockSpec((1, tp, d), lambda bi, hi: (bi, 0, 0)),
                  pl.BlockSpec((1, sh, d), lambda bi, hi: (bi, hi, 0)),
                  pl.BlockSpec((N_META, d), lambda bi, hi: (0, 0)),
                  pl.BlockSpec((1, d), lambda bi, hi: (0, 0))],
        out_specs=[pl.BlockSpec((1, sh, d), lambda bi, hi: (bi, hi, 0)),
                   pl.BlockSpec((1, N_META, d), lambda bi, hi: (bi, 0, 0)),
                   pl.BlockSpec((1, 8, d), lambda bi, hi: (bi, 0, 0))],
        compiler_params=_params(("parallel", "arbitrary")),
    )(ds0, x, meta, g)


def _rows128(a):
    return a.reshape(-1, 128)


def kernel(x, meta_tokens, ln_in_g, ln_in_b, w_in, conv_w, conv_b, conv_ln_g, conv_ln_b, gate_up, gate_bias, gla_norm_g, w_out, ln1_g, ln1_b, w_ff1, w_ff2, ln2_g, ln2_b, loss_target, m_meta_tokens, m_ln_in_g, m_ln_in_b, m_w_in, m_conv_w, m_conv_b, m_conv_ln_g, m_conv_ln_b, m_gate_up, m_gate_bias, m_gla_norm_g, m_w_out, m_ln1_g, m_ln1_b, m_w_ff1, m_w_ff2, m_ln2_g, m_ln2_b, v_meta_tokens, v_ln_in_g, v_ln_in_b, v_w_in, v_conv_w, v_conv_b, v_conv_ln_g, v_conv_ln_b, v_gate_up, v_gate_bias, v_gla_norm_g, v_w_out, v_ln1_g, v_ln1_b, v_w_ff1, v_w_ff2, v_ln2_g, v_ln2_b):
    bsz, seq, d = x.shape
    tp = seq + HEAD
    r = bsz * tp
    xi, yi, ci = _mesh_pos()
    chip = 2 * xi + yi
    c_arr = jnp.reshape(ci, (1,)).astype(jnp.int32)
    pos_arr = jnp.stack([chip, ci]).astype(jnp.int32)

    sh_in = D_IN // 4
    shard_in = jnp.pad(w_in[0].T.astype(BF16), ((0, D_IN_PAD // 4 - sh_in), (0, 0)))
    shard_w1, shard_wout, shard_w2 = w_ff1[0].astype(BF16), w_out[0].astype(BF16), w_ff2[0].astype(BF16)
    small_w = jnp.concatenate([_rows128(meta_tokens), _rows128(conv_w[0]), _rows128(gate_up[0])], axis=0)
    ln_in_g2, ln_in_b2 = ln_in_g.reshape(1, d), ln_in_b.reshape(1, d)

    s0, s0b, ((g_int,), (g_small,)) = _ln_in_fwd(
        x, ln_in_g2, ln_in_b2, [("gather", [(shard_in, 0, D_IN_PAD // 4, None)]), ("small", [small_w])])
    (g_int,) = _place_own([g_int], [shard_in])
    (g_small,) = _place_own([g_small], [small_w])
    n_meta_rows, n_cw_rows = N_META * 256 // 128, CONV_WIDTH * 128 // 128
    meta_full = jnp.concatenate([g_small[j, :n_meta_rows].reshape(N_META, 256) for j in range(4)], axis=1)
    convw_full = jnp.concatenate(
        [g_small[j, n_meta_rows:n_meta_rows + n_cw_rows].reshape(CONV_WIDTH, 128) for j in range(4)], axis=1)
    gup_full = jnp.concatenate(
        [g_small[j, n_meta_rows + n_cw_rows:].reshape(GLA_RANK, 64) for j in range(4)], axis=1)
    convw_p = jnp.pad(convw_full, ((0, 1), (0, 0)))
    gup_p = jnp.pad(gup_full, ((0, 128 - GLA_RANK), (0, 0))).astype(BF16)
    s0, s0b = _ln_meta(s0, s0b, meta_full, ln_in_g2, ln_in_b2)
    w_int = jnp.pad(g_int[:, :sh_in].reshape(D_IN, d), ((0, D_IN_PAD - D_IN), (0, 0)))
    s0f, s0bf = s0.reshape(r, d), s0b.reshape(r, d)
    u, ((w1_buf,),) = _in_proj(s0bf, w_int, [("gather", [(shard_w1, 0, 640, None)])])
    (w1_buf,) = _place_own([w1_buf], [shard_w1])
    u3 = u.reshape(bsz, tp, D_IN_PAD)
    hc, ((w1_buf,),) = _conv_fwd(u3, convw_p, conv_b, [("gather", [(shard_w1, 640, 384, w1_buf)])])
    hc = hc.reshape(r, D_CONV)
    hc, u3_after_conv = lax.optimization_barrier((hc, u3))
    gla_out, o_pre, states, ((g_wout,),) = _gla_fwd(
        u3_after_conv, gup_p, gate_bias, gla_norm_g, [("gather", [(shard_wout, 0, 256, None)])])
    (g_wout,) = _place_own([g_wout], [shard_wout])
    wout = g_wout.reshape(d, d)
    gla_of = gla_out.reshape(r, 512)
    conv_of, xhat1, rstd1, s1b, ((w2_buf,),) = _out_proj_ln1(
        hc, gla_of, wout, s0f, conv_ln_g, conv_ln_b, ln1_g, ln1_b, [("gather", [(shard_w2, 0, 256, None)])])
    (w2_buf,) = _place_own([w2_buf], [shard_w2])
    w1 = w1_buf
    ra, ((w2_buf,),) = _ffn1(s1b, w1, [("gather", [(shard_w2, 256, 768, w2_buf)])])
    w2 = w2_buf.reshape(D_FF, d)
    dt2, dt2b, acc2 = _ffn2_ln2_loss(ra, w2, xhat1, ln1_g, ln1_b, ln2_g, ln2_b, loss_target, tp)

    def add_pair(g, got, splits=None):
        return _add_pair(g, got, c_arr, splits)

    da = _ffn_bwd_da(dt2b, w2, ra)
    dt1, dt1b, acc1 = _ffn_bwd_ln1(da, w1, dt2, xhat1, rstd1, ln1_g)
    g_w2, _ = _matmul_tn(ra, dt2b, 1024, square_lhs=True, name="grad_w_ff2")
    big_w2 = g_w2.reshape(4, D_FF // 4, d)
    big_w1, ((pair_w2,),) = _grad_w_ff1(s1b, da, [("pair", [big_w2])])
    dhc, dgla, cacc, ((pair_w1,),) = _out_proj_bwd(dt1b, wout, hc, conv_ln_g, conv_ln_b, [("pair", [big_w1])])
    (part_w1,), (part_w2a, part_w2b) = add_pair(big_w1, pair_w1), add_pair(big_w2, pair_w2, [512, 512])
    g_wout, _ = _grad_w_out(conv_of, gla_of, dt1b)
    big_wout = g_wout.reshape(4, d // 4, d)
    dcv, dcg, dcw, ((chip_w1,), (pair_wout,)) = _conv_bwd(
        dhc.reshape(bsz, tp, D_CONV), u3, convw_p, [("exchange", [part_w1]), ("pair", [big_wout])])
    (part_wout,) = add_pair(big_wout, pair_wout)
    (dq, dk, dv, dr, dgd, dgup, gvec), ((chip_w2a,),) = _gla_bwd(
        dgla.reshape(bsz, tp, 512), u3, o_pre, states, gup_p, gate_bias, gla_norm_g, [("exchange", [part_w2a])])
    pieces = [a.reshape(r, a.shape[-1]) for a in (dcv, dcg, dq, dk, dv, dr)]
    g_wint, ((chip_w2b, chip_wout),) = _grad_w_in(
        pieces, dgd, s0bf, tp, [("exchange", [part_w2b, part_wout])])
    big_win = jnp.stack([g_wint[j * sh_in:(j + 1) * sh_in] for j in range(4)])
    (half_w1, half_w2), ((pair_win,),) = _add_chips_many(
        [(big_w1, pair_w1, [chip_w1]), (big_w2, pair_w2, [chip_w2a, chip_w2b])], [("pair", [big_win])])
    ds0, ((chip_win,), (sib_w1, sib_w2)) = _in_proj_bwd(
        pieces, dgd, w_int, dt1, tp, [("exchange", add_pair(big_win, pair_win)), ("swap", [half_w1, half_w2])])
    grad_x, dmeta, lvec = _ln_in_bwd(ds0.reshape(bsz, tp, d), x, meta_full, ln_in_g2)

    half_win = _add_chips(big_win, pair_win, [chip_win], pos_arr)
    half_wout = _add_chips(big_wout, pair_wout, [chip_wout], pos_arr)
    sib_win, sib_wout = _pair_swap([half_win, half_wout])
    grads = {}
    weights = dict(meta_tokens=meta_tokens, ln_in_g=ln_in_g, ln_in_b=ln_in_b, w_in=w_in, conv_w=conv_w, conv_b=conv_b,
                   conv_ln_g=conv_ln_g, conv_ln_b=conv_ln_b, gate_up=gate_up, gate_bias=gate_bias,
                   gla_norm_g=gla_norm_g, w_out=w_out, ln1_g=ln1_g, ln1_b=ln1_b, w_ff1=w_ff1, w_ff2=w_ff2,
                   ln2_g=ln2_g, ln2_b=ln2_b)
    moms = dict(meta_tokens=(m_meta_tokens, v_meta_tokens), ln_in_g=(m_ln_in_g, v_ln_in_g),
                ln_in_b=(m_ln_in_b, v_ln_in_b), w_in=(m_w_in, v_w_in), conv_w=(m_conv_w, v_conv_w),
                conv_b=(m_conv_b, v_conv_b), conv_ln_g=(m_conv_ln_g, v_conv_ln_g),
                conv_ln_b=(m_conv_ln_b, v_conv_ln_b), gate_up=(m_gate_up, v_gate_up),
                gate_bias=(m_gate_bias, v_gate_bias), gla_norm_g=(m_gla_norm_g, v_gla_norm_g),
                w_out=(m_w_out, v_w_out), ln1_g=(m_ln1_g, v_ln1_g), ln1_b=(m_ln1_b, v_ln1_b),
                w_ff1=(m_w_ff1, v_w_ff1), w_ff2=(m_w_ff2, v_w_ff2), ln2_g=(m_ln2_g, v_ln2_g),
                ln2_b=(m_ln2_b, v_ln2_b))
    names = list(weights)
    big_names = ("w_in", "w_out", "w_ff1", "w_ff2")
    delta, new_m, new_v = {}, {}, {}
    def adamw_big(k, own, sib, fuse=NO_FUSE):
        to2d = (lambda a: a[0].T) if k == "w_in" else (lambda a: a[0])
        back = (lambda a: a.T[None]) if k == "w_in" else (lambda a: a[None])
        res, got = _adamw_halves(to2d(weights[k]), own, sib, to2d(moms[k][0]), to2d(moms[k][1]), fuse)
        grads[k], delta[k], new_m[k], new_v[k] = [back(a) for a in res]
        return got

    small_pack = _pack_small(acc2, acc1, lvec, cacc, gvec, dcw, dgup, dmeta)
    ((small_all,),) = adamw_big("w_in", half_win, sib_win, [("all", [small_pack])])
    adamw_big("w_ff1", half_w1, sib_w1)
    adamw_big("w_ff2", half_w2, sib_w2)
    adamw_big("w_out", half_wout, sib_wout)
    red, loss = _sum_small(small_all)
    small_names = [k for k in names if k not in big_names]
    two = lambda a: a.reshape(-1, a.shape[-1])

    def my_cols(full, rows, width):
        return lax.dynamic_slice(full, (0, chip * width), (rows, width))

    sharded = {
        "meta_tokens": my_cols(red[SMALL_AT["meta_tokens"][0]:SMALL_AT["meta_tokens"][0] + N_META], N_META, 256),
        "conv_w": my_cols(red[SMALL_CONV_W:SMALL_CONV_W + 16].reshape(32, D_CONV), CONV_WIDTH, 128),
        "gate_up": my_cols(red[SMALL_GATE_UP:SMALL_GATE_UP + 4].reshape(GLA_RANK, 256), GLA_RANK, 64)}
    upd = _adamw_small(red, sharded, {k: (two(weights[k]), two(moms[k][0]), two(moms[k][1])) for k in small_names})
    for k in small_names:
        shp = weights[k].shape
        grads[k], delta[k], new_m[k], new_v[k] = [a.reshape(shp) for a in upd[k]]
    loss = loss.reshape(())

    return (loss, grad_x, *[grads[k] for k in names], *[delta[k] for k in names],
            *[new_m[k] for k in names], *[new_v[k] for k in names])
```

```python
import functools

import jax
import jax.numpy as jnp
from jax import lax
from jax.experimental import pallas as pl
from jax.experimental.pallas import tpu as pltpu

F32 = jnp.float32
BF16 = jnp.bfloat16

D_MODEL = 1024
N_META = 16
D_CONV = 512
CONV_WIDTH = 31
GLA_HEADS = 4
GLA_DV = 128
GLA_DK = 64
GLA_RANK = 16
GLA_TAU = 16.0
CHUNK = 64
D_FF = 4096
LN_EPS = 1e-5
ALPHA = 2.0 ** 0.25
D_IN = 2576
D_IN_PAD = 2688
PAD = CHUNK - N_META
HEAD = PAD + N_META
Q_SCALE = GLA_DK ** -0.5
ADAM_LR, ADAM_B1, ADAM_B2, ADAM_EPS, ADAM_WD, ADAM_STEP = 0.001, 0.9, 0.999, 1e-08, 0.01, 10
HALF = D_MODEL // 2
VMEM_LIMIT = 56 * 1024 * 1024
MESH = pl.DeviceIdType.MESH

C_VAL, C_GATE, C_Q, C_K, C_V, C_R, C_GD = 0, 512, 1024, 1280, 1536, 2048, 2560

SMALL_AT = {"loss": (0, 1, 0, 1024), "ln_in_g": (1, 1, 0, 1024), "ln_in_b": (2, 1, 0, 1024), "ln1_g": (3, 1, 0, 1024),
            "ln1_b": (4, 1, 0, 1024), "ln2_g": (5, 1, 0, 1024), "ln2_b": (6, 1, 0, 1024), "conv_b": (7, 1, 0, 512),
            "conv_ln_g": (7, 1, 512, 512), "conv_ln_b": (8, 1, 0, 512), "gate_bias": (8, 1, 512, 256),
            "gla_norm_g": (8, 1, 768, 128), "meta_tokens": (40, 16, 0, 1024)}
SMALL_CONV_W, SMALL_GATE_UP = 16, 32
SMALL_ROWS = 56


def _params(sem=None, **kw):
    return pltpu.CompilerParams(dimension_semantics=sem, vmem_limit_bytes=VMEM_LIMIT, **kw)


def _row_tile(tp):
    for t in (704, 352, 192, 64):
        if tp % t == 0:
            return t
    raise ValueError(tp)


def _reduce_tile(tp, big):
    for t in ((2112, 1056, 704) if big else (1056, 704)) + (352, 192, 64):
        if tp % t == 0:
            return t
    raise ValueError(tp)


def _sub_rows(tm):
    return [slice(0, tm)]


def _dot(a, b, dims, precision=None):
    return lax.dot_general(a, b, (dims, ((), ())), preferred_element_type=F32, precision=precision)


def _nn(a, b, **kw):
    return _dot(a, b, ((1,), (0,)), **kw)


def _nt(a, b, **kw):
    return _dot(a, b, ((1,), (1,)), **kw)


def _tn(a, b, **kw):
    return _dot(a, b, ((0,), (0,)), **kw)


def _sigmoid(x):
    return 1.0 / (1.0 + jnp.exp(-x))


def _log_sigmoid(z):
    return jnp.minimum(z, 0.0) - jnp.log(1.0 + jnp.exp(-jnp.abs(z)))


def _ln_stats(t):
    mu = jnp.mean(t, axis=-1, keepdims=True)
    d = t - mu
    var = jnp.mean(d * d, axis=-1, keepdims=True)
    rstd = lax.rsqrt(var + LN_EPS)
    return d * rstd, rstd


def _ln_bwd(dxhat, xhat, rstd):
    m1 = jnp.mean(dxhat, axis=-1, keepdims=True)
    m2 = jnp.mean(dxhat * xhat, axis=-1, keepdims=True)
    return rstd * (dxhat - m1 - xhat * m2)


def _mesh_pos():
    return lax.axis_index("x"), lax.axis_index("y"), lax.axis_index("c")


ANY = pl.BlockSpec(memory_space=pl.ANY)


def _gather_sems(n):
    return [pltpu.SemaphoreType.DMA((n, 3))] * 4


def _gather_steps(ins, outs, sems, ranges=None):
    n = len(ins)
    send, recv, fsend, frecv = sems
    ranges = ranges or [(0, ref.shape[0]) for ref in ins]
    x, y, c = _mesh_pos()
    me = 2 * x + y
    sibling = (x, y, 1 - c)
    chips = [(1 - x, y), (x, 1 - y), (1 - x, 1 - y)]
    chip_idx = [2 * px + py for px, py in chips]
    mine = [pl.ds(pl.multiple_of(r0 + c * (nr // 2), 16), nr // 2) for r0, nr in ranges]
    other = [pl.ds(pl.multiple_of(r0 + (1 - c) * (nr // 2), 16), nr // 2) for r0, nr in ranges]
    pairs = [(a, k) for a in range(n) for k in range(3)]

    def ici(a, k, slab):
        return pltpu.make_async_remote_copy(
            src_ref=ins[a].at[mine[a], :], dst_ref=outs[a].at[slab, mine[a], :],
            send_sem=send.at[a, k], recv_sem=recv.at[a, k], device_id=(*chips[k], c), device_id_type=MESH)

    def forward(a, k, rows):
        blk = outs[a].at[chip_idx[k], rows[a], :]
        return pltpu.make_async_remote_copy(
            src_ref=blk, dst_ref=blk, send_sem=fsend.at[a, k], recv_sem=frecv.at[a, k],
            device_id=sibling, device_id_type=MESH)

    def start():
        for a, k in pairs:
            ici(a, k, me).start()

    def relay():
        for a, k in pairs:
            ici(a, k, chip_idx[k]).wait_recv()
            forward(a, k, mine).start()

    def finish():
        for a, k in pairs:
            forward(a, k, other).wait_recv()
        for a, k in pairs:
            ici(a, k, me).wait_send()
            forward(a, k, mine).wait_send()

    return start, relay, finish


def _place_own(gathered, shards):
    chip = 2 * lax.axis_index("x") + lax.axis_index("y")
    return [lax.dynamic_update_slice(g, s[None], (chip, 0, 0)) for g, s in zip(gathered, shards)]


def _gather_small_steps(ins, outs, sems):
    send, recv = sems
    x, y, c = _mesh_pos()
    chips = [(1 - x, y), (x, 1 - y), (1 - x, 1 - y)]

    def copy(a, k, slot):
        return pltpu.make_async_remote_copy(
            src_ref=ins[a], dst_ref=outs[a].at[slot], send_sem=send.at[a, k], recv_sem=recv.at[a, k],
            device_id=(*chips[k], c), device_id_type=MESH)

    def start():
        for a in range(len(ins)):
            for k in range(3):
                copy(a, k, 2 * x + y).start()

    def finish():
        for a in range(len(ins)):
            for k, (px, py) in enumerate(chips):
                copy(a, k, 2 * px + py).wait_recv()
            for k in range(3):
                copy(a, k, 2 * x + y).wait_send()

    return start, finish


def _pair_swap_steps(ins, outs, sems):
    send, recv = sems
    x, y, c = _mesh_pos()
    cps = [pltpu.make_async_remote_copy(
        src_ref=ins[a], dst_ref=outs[a], send_sem=send.at[a], recv_sem=recv.at[a],
        device_id=(x, y, 1 - c), device_id_type=MESH) for a in range(len(ins))]

    def start():
        for cp in cps:
            cp.start()

    def finish():
        for cp in cps:
            cp.wait()

    return start, finish


def _pair_exchange_steps(ins, outs, sems):
    send, recv = sems
    x, y, c = _mesh_pos()
    other = pl.ds(pl.multiple_of((1 - c) * HALF, 128), HALF)
    cps = [pltpu.make_async_remote_copy(
        src_ref=ins[a].at[:, :, other], dst_ref=outs[a], send_sem=send.at[a], recv_sem=recv.at[a],
        device_id=(x, y, 1 - c), device_id_type=MESH) for a in range(len(ins))]

    def start():
        for cp in cps:
            cp.start()

    def finish():
        for cp in cps:
            cp.wait()

    return start, finish


def _pair_exchange_shapes(grads):
    return [jax.ShapeDtypeStruct(g.shape[:2] + (HALF,), g.dtype) for g in grads]


def _pair_exchange(grads):
    n = len(grads)

    def body(*refs):
        start, finish = _pair_exchange_steps(refs[:n], refs[n:2 * n], refs[2 * n:])
        start()
        finish()

    return pl.pallas_call(
        body, name="grad_pair_exchange", out_shape=_pair_exchange_shapes(grads),
        in_specs=[ANY] * n, out_specs=[ANY] * n,
        scratch_shapes=[pltpu.SemaphoreType.DMA((n,)), pltpu.SemaphoreType.DMA((n,))],
    )(*grads)


NO_FUSE = ()


def _fuse_plan(kind, items):
    n = len(items)
    if kind == "gather":
        shards = [it[0] for it in items]
        bufs = [it[3] for it in items if it[3] is not None]
        alias, b = {}, 0
        for a, it in enumerate(items):
            if it[3] is not None:
                alias[n + b] = a
                b += 1
        ranges = [(it[1], it[2]) for it in items]
        return (shards + bufs, [jax.ShapeDtypeStruct((4,) + s.shape, s.dtype) for s in shards], alias, _gather_sems(n),
                lambda i, o, s: _gather_steps(i[:n], o, s, ranges))
    if kind == "exchange":
        return list(items), _chip_exchange_shapes(items), {}, _chip_exchange_sems(n), _chip_exchange_steps
    if kind == "small":
        return (list(items), [jax.ShapeDtypeStruct((4,) + s.shape, s.dtype) for s in items], {},
                _chip_exchange_sems(n), _gather_small_steps)
    if kind == "all":
        (pack,) = items
        return ([pack], [jax.ShapeDtypeStruct((8 * pack.shape[0], pack.shape[1]), pack.dtype)], {},
                [pltpu.SemaphoreType.DMA((7,)), pltpu.SemaphoreType.DMA((7,)), pltpu.SemaphoreType.DMA((1,))],
                _allgather_all_steps)
    pair_sems = [pltpu.SemaphoreType.DMA((n,)), pltpu.SemaphoreType.DMA((n,))]
    if kind == "swap":
        return (list(items), [jax.ShapeDtypeStruct(h.shape, h.dtype) for h in items], {}, pair_sems, _pair_swap_steps)
    return list(items), _pair_exchange_shapes(items), {}, pair_sems, _pair_exchange_steps


def _fused_call(body, fuse, *, name, grid, in_specs, out_specs, out_shape, scratch_shapes, operands,
                late_relay=False):
    plans = [_fuse_plan(kind, list(items)) for kind, items in fuse if len(items)]
    n_in, n_out, n_s = len(in_specs), len(out_shape), len(scratch_shapes)
    comm = [a for p in plans for a in p[0]]
    shapes = [s for p in plans for s in p[1]]
    nc, no = len(comm), len(shapes)
    aliases, i_at, o_at = {}, n_in, n_out
    for p in plans:
        for i, o in p[2].items():
            aliases[i_at + i] = o_at + o
        i_at, o_at = i_at + len(p[0]), o_at + len(p[1])

    def wrapped(*refs):
        o0 = n_in + nc
        s0 = o0 + n_out + no
        i_at, o_at, sem_at, steps = n_in, o0 + n_out, s0 + n_s, []
        for p in plans:
            steps.append(p[4](refs[i_at:i_at + len(p[0])], refs[o_at:o_at + len(p[1])], refs[sem_at:sem_at + len(p[3])]))
            i_at, o_at, sem_at = i_at + len(p[0]), o_at + len(p[1]), sem_at + len(p[3])
        first, last = _grid_ends(grid)
        for st in steps:
            pl.when(first)(st[0])
        if not late_relay:
            for st in steps:
                for mid in st[1:-1]:
                    pl.when(last)(mid)
        body(*refs[:n_in], *refs[o0:o0 + n_out], *refs[s0:s0 + n_s])
        for st in steps:
            for step in (st[1:] if late_relay else st[-1:]):
                pl.when(last)(step)

    res = pl.pallas_call(
        wrapped if plans else body, name=name, grid=grid, in_specs=list(in_specs) + [ANY] * nc,
        out_specs=list(out_specs) + [ANY] * no, out_shape=list(out_shape) + shapes,
        scratch_shapes=list(scratch_shapes) + [s for p in plans for s in p[3]], input_output_aliases=aliases,
        compiler_params=_params(("arbitrary",) * len(grid)))(*operands, *comm)
    outs, got, results = list(res[:n_out]), list(res[n_out:]), []
    for p in plans:
        results.append(got[:len(p[1])])
        got = got[len(p[1]):]
    return outs, results


def _chip_exchange_sems(n):
    return [pltpu.SemaphoreType.DMA((n, 3))] * 2


def _chip_exchange_shapes(parts):
    return [jax.ShapeDtypeStruct((3,) + p.shape[1:], p.dtype) for p in parts]


def _chip_exchange_steps(ins, outs, sems):
    send, recv = sems
    x, y, c = _mesh_pos()
    chips = [(1 - x, y), (x, 1 - y), (1 - x, 1 - y)]
    cps = [pltpu.make_async_remote_copy(
        src_ref=ins[a].at[2 * px + py], dst_ref=outs[a].at[k], send_sem=send.at[a, k], recv_sem=recv.at[a, k],
        device_id=(px, py, c), device_id_type=MESH) for a in range(len(ins)) for k, (px, py) in enumerate(chips)]

    def start():
        for cp in cps:
            cp.start()

    def finish():
        for cp in cps:
            cp.wait()

    return start, finish


def _chip_exchange(parts):
    n = len(parts)

    def body(*refs):
        start, finish = _chip_exchange_steps(refs[:n], refs[n:2 * n], refs[2 * n:])
        start()
        finish()

    return pl.pallas_call(
        body, name="grad_chip_exchange", out_shape=_chip_exchange_shapes(parts),
        in_specs=[ANY] * n, out_specs=[ANY] * n, scratch_shapes=_chip_exchange_sems(n),
    )(*parts)


def _pair_swap(halves):
    n = len(halves)

    def body(*refs):
        start, finish = _pair_swap_steps(refs[:n], refs[n:2 * n], refs[2 * n:])
        start()
        finish()

    return pl.pallas_call(
        body, name="grad_pair_swap",
        out_shape=[jax.ShapeDtypeStruct(h.shape, h.dtype) for h in halves],
        in_specs=[ANY] * n, out_specs=[ANY] * n,
        scratch_shapes=[pltpu.SemaphoreType.DMA((n,)), pltpu.SemaphoreType.DMA((n,))],
    )(*halves)


def _allgather_all_steps(ins, outs, sems):
    send_sems, recv_sems, local_sem = sems
    x_ref, out_ref = ins[0], outs[0]
    m_per = x_ref.shape[0]
    x, y, c = _mesh_pos()
    me, sibling = (x, y, c), (x, y, 1 - c)
    chips = [(1 - x, y), (x, 1 - y), (1 - x, 1 - y)]

    def rows(px, py, pc):
        return out_ref.at[pl.ds(pl.multiple_of((4 * px + 2 * py + pc) * m_per, 8), m_per), :]

    def copy(k, block, to, src=None):
        return pltpu.make_async_remote_copy(
            src_ref=rows(*block) if src is None else src, dst_ref=rows(*block),
            send_sem=send_sems.at[k], recv_sem=recv_sems.at[k], device_id=to, device_id_type=MESH)

    mine = pltpu.make_async_copy(x_ref, rows(*me), local_sem.at[0])
    first = [copy(0, me, sibling, src=x_ref)]
    first += [copy(1 + j, me, (*chip, c), src=x_ref) for j, chip in enumerate(chips)]
    passed = [copy(4 + j, (*chip, c), sibling) for j, chip in enumerate(chips)]

    def start():
        mine.start()
        for cp in first:
            cp.start()

    def relay():
        for j, chip in enumerate(chips):
            copy(1 + j, (*chip, c), me).wait_recv()
            passed[j].start()

    def finish():
        copy(0, sibling, me).wait_recv()
        for j, chip in enumerate(chips):
            copy(4 + j, (*chip, 1 - c), me).wait_recv()
        for cp in first + passed:
            cp.wait_send()
        mine.wait()

    return start, relay, finish


def _add_pair(g, got, c_arr, splits=None):
    _, rows, _ = g.shape
    splits = splits or [rows]

    def body(c_ref, g_ref, r_ref, *o_refs):
        at = 0
        for o_ref, n in zip(o_refs, splits):
            o_ref[...] = (g_ref[:, at:at + n, :] + r_ref[:, at:at + n, :]).astype(BF16)
            at += n

    return pl.pallas_call(
        body, name="grad_add_pair", out_shape=[jax.ShapeDtypeStruct((4, n, HALF), BF16) for n in splits],
        grid_spec=pltpu.PrefetchScalarGridSpec(
            num_scalar_prefetch=1, grid=(4,),
            in_specs=[pl.BlockSpec((1, rows, HALF), lambda j, c: (j, 0, c[0])),
                      pl.BlockSpec((1, rows, HALF), lambda j, c: (j, 0, 0))],
            out_specs=[pl.BlockSpec((1, n, HALF), lambda j, c: (j, 0, 0)) for n in splits]),
        compiler_params=_params(("arbitrary",)),
    )(c_arr, g, got)


def _add_chips(g, pair_got, chip_gots, pos_arr):
    _, rows, _ = g.shape

    def body(pos_ref, g_ref, p_ref, *refs):
        o_ref, at = refs[-1], 0
        for r_ref in refs[:-1]:
            n = r_ref.shape[1]
            own = g_ref[0, at:at + n, :] + p_ref[0, at:at + n, :]
            o_ref[at:at + n, :] = ((own + r_ref[0].astype(F32)) + r_ref[1].astype(F32)) + r_ref[2].astype(F32)
            at += n

    return pl.pallas_call(
        body, name="grad_add_chips", out_shape=jax.ShapeDtypeStruct((rows, HALF), F32),
        grid_spec=pltpu.PrefetchScalarGridSpec(
            num_scalar_prefetch=1, grid=(1,),
            in_specs=[pl.BlockSpec((1, rows, HALF), lambda i, p: (p[0], 0, p[1])),
                      pl.BlockSpec((1, rows, HALF), lambda i, p: (p[0], 0, 0))]
            + [pl.BlockSpec(t.shape, lambda i, p: (0, 0, 0)) for t in chip_gots],
            out_specs=pl.BlockSpec((rows, HALF), lambda i, p: (0, 0))),
        compiler_params=_params(("arbitrary",)),
    )(pos_arr, g, pair_got, *chip_gots)


def _add_chips_many(items, fuse=NO_FUSE):
    n = len(items)
    rows = [it[0].shape[1] for it in items]
    n_got = [len(it[2]) for it in items]

    def body(*refs):
        g_refs, p_refs = refs[:n], refs[n:2 * n]
        got_refs = refs[2 * n:2 * n + sum(n_got)]
        o_refs = refs[2 * n + sum(n_got):3 * n + sum(n_got)]
        scr = refs[3 * n + sum(n_got):]
        x, y, c = _mesh_pos()
        chip = 2 * x + y
        mine = pl.ds(pl.multiple_of(c * HALF, 128), HALF)
        copies = []
        for a in range(n):
            copies.append((pltpu.make_async_copy(g_refs[a].at[chip, :, mine], scr[2 * a], scr[2 * n].at[2 * a]),
                           pltpu.make_async_copy(p_refs[a].at[chip], scr[2 * a + 1], scr[2 * n].at[2 * a + 1])))
        for cg, cp in copies:
            cg.start()
            cp.start()
        at_ref = 0
        for a in range(n):
            copies[a][0].wait()
            copies[a][1].wait()
            at = 0
            for r_ref in got_refs[at_ref:at_ref + n_got[a]]:
                k = r_ref.shape[1]
                own = scr[2 * a][at:at + k, :] + scr[2 * a + 1][at:at + k, :]
                o_refs[a][at:at + k, :] = ((own + r_ref[0].astype(F32)) + r_ref[1].astype(F32)) + r_ref[2].astype(F32)
                at += k
            at_ref += n_got[a]

    gots = [t for it in items for t in it[2]]
    outs, got = _fused_call(
        body, fuse, name="grad_add_chips_many", grid=(1,),
        in_specs=[ANY] * (2 * n) + [pl.BlockSpec(t.shape, lambda i: (0, 0, 0)) for t in gots],
        out_specs=[pl.BlockSpec((r, HALF), lambda i: (0, 0)) for r in rows],
        out_shape=[jax.ShapeDtypeStruct((r, HALF), F32) for r in rows],
        scratch_shapes=[pltpu.VMEM((r, HALF), F32) for r in rows for _ in range(2)] + [pltpu.SemaphoreType.DMA((2 * n,))],
        operands=(*[it[0] for it in items], *[it[1] for it in items], *gots))
    return outs, got


def _pack_small(acc2, acc1, lvec, cacc, gvec, dcw, dgup, dmeta):
    bsz = lvec.shape[0]

    def body(a2_ref, a1_ref, lv_ref, ca_ref, gv_ref, cw_ref, gu_ref, dm_ref, o_ref):
        def put(name, val):
            r0, nr, l0, nl = SMALL_AT[name]
            o_ref[r0:r0 + nr, l0:l0 + nl] = val

        def put_folded(row0, val, per_row):
            w = val.shape[1]
            for i in range(val.shape[0]):
                o_ref[row0 + i // per_row:row0 + i // per_row + 1, (i % per_row) * w:(i % per_row + 1) * w] = val[i:i + 1]

        over_b = lambda f: functools.reduce(lambda a, b: a + b, [f(b) for b in range(bsz)])
        o_ref[...] = jnp.zeros_like(o_ref)
        put("loss", a2_ref[0:1, :])
        put("ln2_g", a2_ref[1:2, :])
        put("ln2_b", a2_ref[2:3, :])
        put("ln1_g", a1_ref[0:1, :])
        put("ln1_b", a1_ref[1:2, :])
        put("ln_in_g", over_b(lambda b: lv_ref[b, 0:1, :]))
        put("ln_in_b", over_b(lambda b: lv_ref[b, 1:2, :]))
        put("conv_b", ca_ref[0:1, :])
        put("conv_ln_g", ca_ref[1:2, :])
        put("conv_ln_b", ca_ref[2:3, :])
        put("gate_bias", over_b(lambda b: gv_ref[b, 0:1, :]))
        put("gla_norm_g", over_b(lambda b: gv_ref[b, 1:2, 0:128] + gv_ref[b, 1:2, 128:256]))
        put_folded(SMALL_CONV_W, over_b(lambda b: cw_ref[b]), D_MODEL // D_CONV)
        put_folded(SMALL_GATE_UP, over_b(lambda b: gu_ref[b, 0:GLA_RANK, :]), D_MODEL // 256)
        put("meta_tokens", over_b(lambda b: dm_ref[b]))

    return pl.pallas_call(
        body, name="pack_small", out_shape=jax.ShapeDtypeStruct((SMALL_ROWS, D_MODEL), F32),
    )(acc2, acc1, lvec, cacc, gvec, dcw, dgup, dmeta)


def _sum_small(gathered):
    def body(g_ref, o_ref, loss_ref):
        acc = g_ref[0:SMALL_ROWS, :]
        for d in range(1, 8):
            acc = acc + g_ref[d * SMALL_ROWS:(d + 1) * SMALL_ROWS, :]
        o_ref[...] = acc
        loss_ref[...] = jnp.sum(acc[0:1, :], axis=1, keepdims=True)

    return pl.pallas_call(
        body, name="sum_small",
        out_shape=[jax.ShapeDtypeStruct((SMALL_ROWS, D_MODEL), F32), jax.ShapeDtypeStruct((1, 1), F32)],
    )(gathered)


def _adamw_math(w, g, m, v):
    c1 = 1.0 - ADAM_B1 ** ADAM_STEP
    c2 = 1.0 - ADAM_B2 ** ADAM_STEP
    mn = ADAM_B1 * m + (1.0 - ADAM_B1) * g
    vn = ADAM_B2 * v + (1.0 - ADAM_B2) * (g * g)
    return -ADAM_LR * ((mn / c1) / (jnp.sqrt(vn / c2) + ADAM_EPS) + ADAM_WD * w), mn, vn


def _adamw_small(red, sharded_grads, params):
    names = list(params)
    sharded = [k for k in names if k in sharded_grads]
    n, ns = len(names), len(sharded)

    def body(*refs):
        red_ref, sg_refs, p_refs, o_refs = refs[0], refs[1:1 + ns], refs[1 + ns:1 + ns + 3 * n], refs[1 + ns + 3 * n:]
        for i, k in enumerate(names):
            w_ref, m_ref, v_ref = p_refs[3 * i:3 * i + 3]
            if k in sharded:
                g = sg_refs[sharded.index(k)][...]
            else:
                r0, nr, l0, nl = SMALL_AT[k]
                g = red_ref[r0:r0 + nr, l0:l0 + nl]
            dl, mn, vn = _adamw_math(w_ref[...], g, m_ref[...], v_ref[...])
            for o_ref, val in zip(o_refs[4 * i:4 * i + 4], (g, dl, mn, vn)):
                o_ref[...] = val

    flat = [a for k in names for a in params[k]]
    res = pl.pallas_call(
        body, name="adamw_small",
        out_shape=[jax.ShapeDtypeStruct(params[k][0].shape, F32) for k in names for _ in range(4)],
    )(red, *[sharded_grads[k] for k in sharded], *flat)
    return {k: tuple(res[4 * i:4 * i + 4]) for i, k in enumerate(names)}


def _adamw_halves(w, own, sib, m, v, fuse=NO_FUSE):
    rows, cols = w.shape
    tile = 128
    by_rows = rows % tile == 0
    per_half = HALF // tile

    def body(w_ref, own_ref, sib_ref, m_ref, v_ref, g_ref, d_ref, mo_ref, vo_ref):
        c = _mesh_pos()[2]
        own, sib = own_ref[...], sib_ref[...]
        if by_rows:
            g = jnp.concatenate([jnp.where(c == 0, own, sib), jnp.where(c == 0, sib, own)], axis=1)
        else:
            g = jnp.where(pl.program_id(0) // per_half == c, own, sib)
        g_ref[...] = g
        d_ref[...], mo_ref[...], vo_ref[...] = _adamw_math(w_ref[...], g, m_ref[...], v_ref[...])

    if by_rows:
        grid = (rows // tile,)
        full = pl.BlockSpec((tile, cols), lambda i: (i, 0))
        half = pl.BlockSpec((tile, HALF), lambda i: (i, 0))
    else:
        grid = (cols // tile,)
        full = pl.BlockSpec((rows, tile), lambda j: (0, j))
        half = pl.BlockSpec((rows, tile), lambda j: (0, j % per_half))
    return _fused_call(
        body, fuse, name="adamw_halves", grid=grid, in_specs=[full, half, half, full, full],
        out_specs=[full] * 4, out_shape=[jax.ShapeDtypeStruct(w.shape, F32)] * 4, scratch_shapes=[],
        operands=(w, own, sib, m, v))


def _ln_in_fwd(x, g, b, fuse=NO_FUSE):
    bsz, s, d = x.shape
    tp = s + HEAD
    nh = 2
    sh = s // nh
    rc = min(256, sh)

    def body(x_ref, g_ref, b_ref, s0_ref, s0b_ref):
        h = pl.program_id(1)
        gg, bb = g_ref[...], b_ref[...]

        def step(i, carry):
            src = pl.ds(pl.multiple_of(i * rc, rc), rc)
            dst = pl.ds(pl.multiple_of(HEAD + h * sh + i * rc, 64), rc)
            xh, _ = _ln_stats(x_ref[0, src, :])
            val = xh * gg + bb
            s0_ref[0, dst, :] = val
            s0b_ref[0, dst, :] = val.astype(BF16)
            return carry

        lax.fori_loop(0, sh // rc, step, 0)

    full = lambda bi, hi: (bi, 0, 0)
    (s0, s0b), got = _fused_call(
        body, fuse, name="ln_in_fwd",
        out_shape=[jax.ShapeDtypeStruct((bsz, tp, d), F32), jax.ShapeDtypeStruct((bsz, tp, d), BF16)],
        grid=(bsz, nh),
        in_specs=[pl.BlockSpec((1, sh, d), lambda bi, hi: (bi, hi, 0)),
                  pl.BlockSpec((1, d), lambda bi, hi: (0, 0)),
                  pl.BlockSpec((1, d), lambda bi, hi: (0, 0))],
        out_specs=[pl.BlockSpec((1, tp, d), full)] * 2, scratch_shapes=[], operands=(x, g, b),
        late_relay=True)
    return s0, s0b, got


def _ln_meta(s0, s0b, meta, g, b):
    bsz, tp, d = s0.shape

    def body(s0_in, s0b_in, meta_ref, g_ref, b_ref, s0_ref, s0b_ref):
        mh, _ = _ln_stats(meta_ref[...])
        mv = mh * g_ref[...] + b_ref[...]
        s0_ref[0, 0:PAD, :] = jnp.zeros((PAD, d), F32)
        s0b_ref[0, 0:PAD, :] = jnp.zeros((PAD, d), BF16)
        s0_ref[0, PAD:HEAD, :] = mv
        s0b_ref[0, PAD:HEAD, :] = mv.astype(BF16)

    head = pl.BlockSpec((1, HEAD, d), lambda bi: (bi, 0, 0))
    vec = pl.BlockSpec((1, d), lambda bi: (0, 0))
    return pl.pallas_call(
        body, name="ln_meta", out_shape=[jax.ShapeDtypeStruct(s0.shape, F32), jax.ShapeDtypeStruct(s0b.shape, BF16)],
        grid=(bsz,), in_specs=[head, head, pl.BlockSpec((N_META, d), lambda bi: (0, 0)), vec, vec],
        out_specs=[head, head], input_output_aliases={0: 0, 1: 1},
        compiler_params=_params(("parallel",)),
    )(s0, s0b, meta, g, b)


def _in_proj(s0b, w_int, fuse=NO_FUSE):
    r, d = s0b.shape
    tm = _row_tile(r)

    def body(a_ref, w_ref, o_ref):
        o_ref[...] = _nt(a_ref[...], w_ref[...])

    (u,), got = _fused_call(
        body, fuse, name="in_proj", out_shape=[jax.ShapeDtypeStruct((r, D_IN_PAD), F32)],
        grid=(r // tm,),
        in_specs=[pl.BlockSpec((tm, d), lambda i: (i, 0)), pl.BlockSpec((D_IN_PAD, d), lambda i: (0, 0))],
        out_specs=[pl.BlockSpec((tm, D_IN_PAD), lambda i: (i, 0))], scratch_shapes=[], operands=(s0b, w_int))
    return u, got


def _conv_fwd(u, conv_w, conv_b, fuse=NO_FUSE):
    bsz, tp, _ = u.shape
    nchunk = tp // CHUNK
    win = CHUNK + 32
    nct = D_CONV // 128

    def body(cv_ref, cg_ref, w_ref, cb_ref, hc_ref, h_scr, win_scr):
        h_scr[0:32, :] = jnp.zeros((32, 128), F32)
        h_scr[32:32 + tp, :] = cv_ref[0] * _sigmoid(cg_ref[0])
        cb = cb_ref[...]

        def step(n, carry):
            r0 = pl.multiple_of(n * CHUNK, CHUNK)
            win_scr[...] = h_scr[pl.ds(r0, win), :]
            acc = jnp.zeros((CHUNK, 128), F32)
            for j in range(CONV_WIDTH):
                acc = acc + w_ref[j:j + 1, :] * win_scr[2 + j:2 + j + CHUNK, :]
            hc_ref[0, pl.ds(r0, CHUNK), :] = acc + cb
            return carry

        lax.fori_loop(0, nchunk, step, 0)

    (hc,), got = _fused_call(
        body, fuse, name="conv_fwd", out_shape=[jax.ShapeDtypeStruct((bsz, tp, D_CONV), F32)],
        grid=(bsz, nct),
        in_specs=[pl.BlockSpec((1, tp, 128), lambda bi, t: (bi, 0, C_VAL // 128 + t)),
                  pl.BlockSpec((1, tp, 128), lambda bi, t: (bi, 0, C_GATE // 128 + t)),
                  pl.BlockSpec((32, 128), lambda bi, t: (0, t)),
                  pl.BlockSpec((1, 128), lambda bi, t: (0, t))],
        out_specs=[pl.BlockSpec((1, tp, 128), lambda bi, t: (bi, 0, t))],
        scratch_shapes=[pltpu.VMEM((tp + 32, 128), F32), pltpu.VMEM((win, 128), F32)],
        operands=(u, u, conv_w, conv_b))
    return hc, got


def _gla_group(nchunk):
    return 11 if nchunk % 11 == 0 else nchunk


def _bdot(a, b, ca, cb, precision=None):
    return lax.dot_general(a, b, (((ca,), (cb,)), ((0,), (0,))), preferred_element_type=F32, precision=precision)


def _bnn(a, b, **kw):
    return _bdot(a, b, 2, 1, **kw)


def _bnt(a, b, **kw):
    return _bdot(a, b, 2, 2, **kw)


def _gla_consts(nb):
    row = lax.broadcasted_iota(jnp.int32, (nb, CHUNK, CHUNK), 1)
    col = lax.broadcasted_iota(jnp.int32, (nb, CHUNK, CHUNK), 2)
    lane = lax.broadcasted_iota(jnp.int32, (1, 1, 128), 2)
    return row >= col, row <= col, [lane < GLA_DK, lane >= GLA_DK]


def _gla_group_terms(g, nb, q_ref, k_ref, gd_ref, gup_ref, gb_ref, tril):
    m = nb * CHUNK
    rows = pl.ds(pl.multiple_of(g * m, CHUNK), m)
    z = _nn(gd_ref[0, rows, :].astype(BF16), gup_ref[...]) + gb_ref[...]
    valid = g * m + lax.broadcasted_iota(jnp.int32, (m, 1), 0) >= PAD
    lg = jnp.where(valid, _log_sigmoid(z) * (1.0 / GLA_TAU), 0.0)
    bcum = _bnn(tril.astype(F32), lg.reshape(nb, CHUNK, 128), precision=lax.Precision.HIGHEST)
    blast = bcum[:, CHUNK - 1:CHUNK, :]
    eb = jnp.exp(bcum)
    enb = jnp.exp(-bcum)
    erest = jnp.exp(blast - bcum)
    q = (q_ref[0, rows, :] * Q_SCALE).reshape(nb, CHUNK, 128)
    k = k_ref[0, rows, :].reshape(nb, CHUNK, 128)
    return rows, valid, z, eb, enb, erest, jnp.exp(blast), q * eb, k * enb, k * erest


def _grid_ends(grid):
    ids = [pl.program_id(i) for i in range(len(grid))]
    first = functools.reduce(jnp.logical_and, [i == 0 for i in ids])
    last = functools.reduce(jnp.logical_and, [i == g - 1 for i, g in zip(ids, grid)])
    return first, last


def _gla_fwd(u, gup, gbias, gnorm, fuse=NO_FUSE):
    bsz, tp, _ = u.shape
    nchunk = tp // CHUNK
    nb = _gla_group(nchunk)

    def body(q_ref, k_ref, v_ref, r_ref, gd_ref, gup_ref, gb_ref, gn_ref, out_ref, o_ref, st_ref, s_scr):
        tril, _, hmask = _gla_consts(nb)
        s_scr[...] = jnp.zeros_like(s_scr)
        gn = gn_ref[...]

        def group(g, carry):
            rows, _, _, _, _, _, dec, qe, ke, kd = _gla_group_terms(g, nb, q_ref, k_ref, gd_ref, gup_ref, gb_ref, tril)
            keb, kdb = ke.astype(BF16), kd.astype(BF16)
            for h in range(2):
                cols = slice(h * GLA_DV, (h + 1) * GLA_DV)
                qh = jnp.where(hmask[h], qe, 0.0).astype(BF16)
                vh = v_ref[0, rows, cols].astype(BF16).reshape(nb, CHUNK, GLA_DV)
                a = jnp.where(tril, _bnt(qh, keb), 0.0).astype(BF16)
                st = s_scr[h]
                sts = []
                for n in range(nb):
                    st_ref[0, h, g * nb + n] = st
                    sts.append(st.astype(BF16))
                    st = dec[n] * st + _tn(vh[n], kdb[n])
                s_scr[h] = st
                o = (_bnn(a, vh) + _bnt(qh, jnp.stack(sts))).reshape(nb * CHUNK, GLA_DV)
                o_ref[0, rows, cols] = o
                rms = lax.rsqrt(jnp.mean(o * o, axis=-1, keepdims=True) + LN_EPS)
                rh = r_ref[0, rows, cols]
                out_ref[0, rows, cols] = (o * rms * gn * (rh * _sigmoid(rh))).astype(BF16)
            return carry

        lax.fori_loop(0, nchunk // nb, group, 0)

    res, got = _fused_call(
        body, fuse, name="gla_fwd",
        out_shape=[jax.ShapeDtypeStruct((bsz, tp, 512), BF16), jax.ShapeDtypeStruct((bsz, tp, 512), F32),
                   jax.ShapeDtypeStruct((bsz, GLA_HEADS, nchunk, GLA_DV, 128), F32)],
        grid=(bsz, 2),
        in_specs=[pl.BlockSpec((1, tp, 128), lambda bi, p: (bi, 0, C_Q // 128 + p)),
                  pl.BlockSpec((1, tp, 128), lambda bi, p: (bi, 0, C_K // 128 + p)),
                  pl.BlockSpec((1, tp, 256), lambda bi, p: (bi, 0, C_V // 256 + p)),
                  pl.BlockSpec((1, tp, 256), lambda bi, p: (bi, 0, C_R // 256 + p)),
                  pl.BlockSpec((1, tp, 128), lambda bi, p: (bi, 0, C_GD // 128)),
                  pl.BlockSpec((128, 128), lambda bi, p: (0, p)),
                  pl.BlockSpec((1, 128), lambda bi, p: (0, p)),
                  pl.BlockSpec((1, 128), lambda bi, p: (0, 0))],
        out_specs=[pl.BlockSpec((1, tp, 256), lambda bi, p: (bi, 0, p)),
                   pl.BlockSpec((1, tp, 256), lambda bi, p: (bi, 0, p)),
                   pl.BlockSpec((1, 2, nchunk, GLA_DV, 128), lambda bi, p: (bi, p, 0, 0, 0))],
        scratch_shapes=[pltpu.VMEM((2, GLA_DV, 128), F32)],
        operands=(u, u, u, u, u, gup, gbias, gnorm))
    return res[0], res[1], res[2], got


def _out_proj_ln1(hc, gla_out, w_out, s0, cg, cb, g1, b1, fuse=NO_FUSE):
    r, d = s0.shape
    tm = _row_tile(r)

    def body(hc_ref, a_ref, w_ref, s0_ref, cg_ref, cb_ref, g_ref, b_ref, co_ref, xh_ref, rstd_ref, s1b_ref):
        for rs in _sub_rows(tm):
            xc, _ = _ln_stats(hc_ref[rs, :])
            nv = xc * cg_ref[...] + cb_ref[...]
            co = (nv * _sigmoid(nv)).astype(BF16)
            co_ref[rs, :] = co
            mix = _nn(co, w_ref[0:D_CONV, :]) + _nn(a_ref[rs, :], w_ref[D_CONV:, :])
            xh, rstd = _ln_stats(ALPHA * s0_ref[rs, :] + mix)
            xh_ref[rs, :] = xh
            rstd_ref[rs, :] = rstd
            s1b_ref[rs, :] = (xh * g_ref[...] + b_ref[...]).astype(BF16)

    row = lambda n: pl.BlockSpec((tm, n), lambda i: (i, 0))
    vec = lambda n: pl.BlockSpec((1, n), lambda i: (0, 0))
    res, got = _fused_call(
        body, fuse, name="out_proj_ln1",
        out_shape=[jax.ShapeDtypeStruct((r, D_CONV), BF16), jax.ShapeDtypeStruct((r, d), F32),
                   jax.ShapeDtypeStruct((r, 1), F32), jax.ShapeDtypeStruct((r, d), BF16)],
        grid=(r // tm,),
        in_specs=[row(D_CONV), row(512), pl.BlockSpec((d, d), lambda i: (0, 0)), row(d),
                  vec(D_CONV), vec(D_CONV), vec(d), vec(d)],
        out_specs=[row(D_CONV), row(d), row(1), row(d)], scratch_shapes=[],
        operands=(hc, gla_out, w_out, s0, cg, cb, g1, b1))
    return res[0], res[1], res[2], res[3], got


def _ffn1(s1b, w1, fuse=NO_FUSE):
    r, d = s1b.shape
    tm = _row_tile(r)
    ns, _, wn = w1.shape

    def body(a_ref, w_ref, o_ref):
        a = a_ref[...]
        for j in range(ns):
            o_ref[:, j * wn:(j + 1) * wn] = jnp.maximum(_nn(a, w_ref[j]), 0.0).astype(BF16)

    (ra,), got = _fused_call(
        body, fuse, name="ffn1", out_shape=[jax.ShapeDtypeStruct((r, D_FF), BF16)], grid=(r // tm,),
        in_specs=[pl.BlockSpec((tm, d), lambda i: (i, 0)), pl.BlockSpec(w1.shape, lambda i: (0, 0, 0))],
        out_specs=[pl.BlockSpec((tm, D_FF), lambda i: (i, 0))], scratch_shapes=[], operands=(s1b, w1))
    return ra, got


def _ffn2_ln2_loss(ra, w2, xhat1, g1, b1, g2, b2, tgt, tp):
    r, d = xhat1.shape
    tm = _row_tile(tp)
    per = tp // tm

    def body(ra_ref, w_ref, xh1_ref, g1_ref, b1_ref, g2_ref, b2_ref, tgt_ref, dt_ref, dtb_ref, acc_ref, t_ref, sem):
        i = pl.program_id(0)
        b, j = i // per, i % per

        @pl.when(i == 0)
        def _():
            acc_ref[...] = jnp.zeros_like(acc_ref)

        head_copy = pltpu.make_async_copy(tgt_ref.at[b, pl.ds(0, tm - HEAD), :], t_ref.at[pl.ds(HEAD, tm - HEAD), :], sem)
        body_copy = pltpu.make_async_copy(
            tgt_ref.at[b, pl.ds(pl.multiple_of(jnp.maximum(j * tm - HEAD, 0), 64), tm), :], t_ref, sem)

        @pl.when(j == 0)
        def _():
            t_ref[0:HEAD, :] = jnp.zeros((HEAD, d), F32)
            head_copy.start()

        pl.when(j > 0)(body_copy.start)

        sums = [jnp.zeros((1, d), F32)] * 3
        for rs in _sub_rows(tm):
            rb = ra_ref[rs, :]
            f = _nn(rb * rb, w_ref[...])
            if rs.start == 0:
                pl.when(j == 0)(head_copy.wait)
                pl.when(j > 0)(body_copy.wait)
            s1 = xh1_ref[rs, :] * g1_ref[...] + b1_ref[...]
            xh2, rstd2 = _ln_stats(ALPHA * s1 + f)
            y = xh2 * g2_ref[...] + b2_ref[...]
            rowid = (i % per) * tm + rs.start + lax.broadcasted_iota(jnp.int32, (rs.stop - rs.start, 1), 0)
            e = jnp.where(rowid >= HEAD, y - t_ref[rs, :], 0.0)
            dy = e * (1.0 / d)
            dt2 = _ln_bwd(dy * g2_ref[...], xh2, rstd2)
            dt_ref[rs, :] = dt2
            dtb_ref[rs, :] = dt2.astype(BF16)
            sums = [sums[0] + (0.5 / d) * jnp.sum(e * e, axis=0, keepdims=True),
                    sums[1] + jnp.sum(dy * xh2, axis=0, keepdims=True), sums[2] + jnp.sum(dy, axis=0, keepdims=True)]
        for k in range(3):
            acc_ref[k:k + 1, :] += sums[k]

    row = lambda n: pl.BlockSpec((tm, n), lambda i: (i, 0))
    vec = pl.BlockSpec((1, d), lambda i: (0, 0))
    return pl.pallas_call(
        body, name="ffn2_ln2_loss",
        out_shape=[jax.ShapeDtypeStruct((r, d), F32), jax.ShapeDtypeStruct((r, d), BF16),
                   jax.ShapeDtypeStruct((8, d), F32)],
        grid=(r // tm,),
        in_specs=[row(D_FF), pl.BlockSpec((D_FF, d), lambda i: (0, 0)), row(d), vec, vec, vec, vec, ANY],
        out_specs=[row(d), row(d), pl.BlockSpec((8, d), lambda i: (0, 0))],
        scratch_shapes=[pltpu.VMEM((tm, d), F32), pltpu.SemaphoreType.DMA],
        compiler_params=_params(("arbitrary",)),
    )(ra, w2, xhat1, g1, b1, g2, b2, tgt)


def _ffn_bwd_da(dt2b, w2, ra):
    r, d = dt2b.shape
    tm = _row_tile(r)

    def body(g_ref, w_ref, ra_ref, o_ref):
        o_ref[...] = (_nt(g_ref[...], w_ref[...]) * (2.0 * ra_ref[...].astype(F32))).astype(BF16)

    return pl.pallas_call(
        body, name="ffn_bwd_da", out_shape=jax.ShapeDtypeStruct((r, D_FF), BF16),
        grid=(r // tm,),
        in_specs=[pl.BlockSpec((tm, d), lambda i: (i, 0)), pl.BlockSpec((D_FF, d), lambda i: (0, 0)),
                  pl.BlockSpec((tm, D_FF), lambda i: (i, 0))],
        out_specs=pl.BlockSpec((tm, D_FF), lambda i: (i, 0)),
        compiler_params=_params(("parallel",)),
    )(dt2b, w2, ra)


def _ffn_bwd_ln1(da, w1, dt2, xhat1, rstd1, g1):
    r, d = dt2.shape
    tm = _row_tile(r)

    def body(da_ref, w_ref, dt2_ref, xh_ref, rstd_ref, g_ref, dt_ref, dtb_ref, acc_ref):
        @pl.when(pl.program_id(0) == 0)
        def _():
            acc_ref[...] = jnp.zeros_like(acc_ref)

        sums = [jnp.zeros((1, d), F32)] * 2
        for rs in _sub_rows(tm):
            ds1 = ALPHA * dt2_ref[rs, :]
            for j in range(w1.shape[0]):
                ds1 = ds1 + _nt(da_ref[rs, j * w1.shape[2]:(j + 1) * w1.shape[2]], w_ref[j])
            xh = xh_ref[rs, :]
            dt1 = _ln_bwd(ds1 * g_ref[...], xh, rstd_ref[rs, :])
            dt_ref[rs, :] = dt1
            dtb_ref[rs, :] = dt1.astype(BF16)
            sums = [sums[0] + jnp.sum(ds1 * xh, axis=0, keepdims=True), sums[1] + jnp.sum(ds1, axis=0, keepdims=True)]
        for k in range(2):
            acc_ref[k:k + 1, :] += sums[k]

    row = lambda n: pl.BlockSpec((tm, n), lambda i: (i, 0))
    return pl.pallas_call(
        body, name="ffn_bwd_ln1",
        out_shape=[jax.ShapeDtypeStruct((r, d), F32), jax.ShapeDtypeStruct((r, d), BF16),
                   jax.ShapeDtypeStruct((8, d), F32)],
        grid=(r // tm,),
        in_specs=[row(D_FF), pl.BlockSpec(w1.shape, lambda i: (0, 0, 0)), row(d), row(d), row(1),
                  pl.BlockSpec((1, d), lambda i: (0, 0))],
        out_specs=[row(d), row(d), pl.BlockSpec((8, d), lambda i: (0, 0))],
        compiler_params=_params(("arbitrary",)),
    )(da, w1, dt2, xhat1, rstd1, g1)


def _matmul_tn(lhs, rhs, bm, square_lhs=False, name="matmul_tn", fuse=NO_FUSE):
    r, m = lhs.shape
    n = rhs.shape[1]
    tk = _reduce_tile(r, True)

    def body(a_ref, b_ref, o_ref):
        @pl.when(pl.program_id(1) == 0)
        def _():
            o_ref[...] = jnp.zeros_like(o_ref)

        a = a_ref[...]
        if square_lhs:
            a = a * a
        o_ref[...] += _tn(a, b_ref[...])

    (out,), got = _fused_call(
        body, fuse, name=name, out_shape=[jax.ShapeDtypeStruct((m, n), F32)], grid=(m // bm, r // tk),
        in_specs=[pl.BlockSpec((tk, bm), lambda i, k: (k, i)), pl.BlockSpec((tk, n), lambda i, k: (k, 0))],
        out_specs=[pl.BlockSpec((bm, n), lambda i, k: (i, 0))], scratch_shapes=[], operands=(lhs, rhs))
    return out, got


def _grad_w_ff1(s1b, da, fuse=NO_FUSE):
    r, d = s1b.shape
    wn = da.shape[1] // 4
    tk = _reduce_tile(r, True)

    def body(a_ref, b_ref, o_ref):
        @pl.when(pl.program_id(1) == 0)
        def _():
            o_ref[...] = jnp.zeros_like(o_ref)

        o_ref[0] += _tn(a_ref[...], b_ref[...])

    (out,), got = _fused_call(
        body, fuse, name="grad_w_ff1", out_shape=[jax.ShapeDtypeStruct((4, d, wn), F32)], grid=(4, r // tk),
        in_specs=[pl.BlockSpec((tk, d), lambda j, k: (k, 0)), pl.BlockSpec((tk, wn), lambda j, k: (k, j))],
        out_specs=[pl.BlockSpec((1, d, wn), lambda j, k: (j, 0, 0))], scratch_shapes=[], operands=(s1b, da))
    return out, got


def _grad_w_out(conv_of, gla_of, dt1b, fuse=NO_FUSE):
    r, n = dt1b.shape
    tk = _reduce_tile(r, True)

    def body(a_ref, b_ref, g_ref, o_ref):
        @pl.when(pl.program_id(1) == 0)
        def _():
            o_ref[...] = jnp.zeros_like(o_ref)

        @pl.when(pl.program_id(0) == 0)
        def _():
            o_ref[...] += _tn(a_ref[...], g_ref[...])

        @pl.when(pl.program_id(0) == 1)
        def _():
            o_ref[...] += _tn(b_ref[...], g_ref[...])

    lhs = pl.BlockSpec((tk, 512), lambda i, k: (k, 0))
    (out,), got = _fused_call(
        body, fuse, name="grad_w_out", out_shape=[jax.ShapeDtypeStruct((2 * 512, n), F32)], grid=(2, r // tk),
        in_specs=[lhs, lhs, pl.BlockSpec((tk, n), lambda i, k: (k, 0))],
        out_specs=[pl.BlockSpec((512, n), lambda i, k: (i, 0))], scratch_shapes=[],
        operands=(conv_of, gla_of, dt1b))
    return out, got


def _out_proj_bwd(dt1b, w_out, hc, cg, cb, fuse=NO_FUSE):
    r, d = dt1b.shape
    tm = _row_tile(r)

    def body(g_ref, w_ref, hc_ref, cg_ref, cb_ref, dhc_ref, dgla_ref, acc_ref):
        @pl.when(pl.program_id(0) == 0)
        def _():
            acc_ref[...] = jnp.zeros_like(acc_ref)

        gg = cg_ref[...]
        sums = [jnp.zeros((1, D_CONV), F32)] * 3
        for rs in _sub_rows(tm):
            dmix = _nt(g_ref[rs, :], w_ref[...])
            dgla_ref[rs, :] = dmix[:, D_CONV:]
            xh, rstd = _ln_stats(hc_ref[rs, :])
            nv = xh * gg + cb_ref[...]
            sig = _sigmoid(nv)
            dn = dmix[:, :D_CONV] * (sig * (1.0 + nv * (1.0 - sig)))
            dhc = _ln_bwd(dn * gg, xh, rstd)
            dhc_ref[rs, :] = dhc
            sums = [sums[0] + jnp.sum(dhc, axis=0, keepdims=True), sums[1] + jnp.sum(dn * xh, axis=0, keepdims=True),
                    sums[2] + jnp.sum(dn, axis=0, keepdims=True)]
        for k in range(3):
            acc_ref[k:k + 1, :] += sums[k]

    row = lambda n: pl.BlockSpec((tm, n), lambda i: (i, 0))
    vec = pl.BlockSpec((1, D_CONV), lambda i: (0, 0))
    res, got = _fused_call(
        body, fuse, name="out_proj_bwd",
        out_shape=[jax.ShapeDtypeStruct((r, D_CONV), F32), jax.ShapeDtypeStruct((r, 512), F32),
                   jax.ShapeDtypeStruct((8, D_CONV), F32)],
        grid=(r // tm,),
        in_specs=[row(d), pl.BlockSpec((d, d), lambda i: (0, 0)), row(D_CONV), vec, vec],
        out_specs=[row(D_CONV), row(512), pl.BlockSpec((8, D_CONV), lambda i: (0, 0))], scratch_shapes=[],
        operands=(dt1b, w_out, hc, cg, cb))
    return res[0], res[1], res[2], got


def _conv_bwd(dhc, u, conv_w, fuse=NO_FUSE):
    bsz, tp, _ = u.shape
    nchunk = tp // CHUNK
    win = CHUNK + 32
    nct = D_CONV // 128

    def body(dhc_ref, cv_ref, cg_ref, w_ref, dv_ref, dg_ref, dw_ref, h_scr, dhc_scr, hwin, dwin, dw_scr):
        h_scr[0:32, :] = jnp.zeros((32, 128), F32)
        h_scr[32:32 + tp, :] = cv_ref[0] * _sigmoid(cg_ref[0])
        dhc_scr[0:tp, :] = dhc_ref[0]
        dhc_scr[tp:tp + 32, :] = jnp.zeros((32, 128), F32)
        dw_scr[...] = jnp.zeros_like(dw_scr)

        def step(n, carry):
            r0 = pl.multiple_of(n * CHUNK, CHUNK)
            rows = pl.ds(r0, CHUNK)
            hwin[...] = h_scr[pl.ds(r0, win), :]
            dwin[...] = dhc_scr[pl.ds(r0, win), :]
            dcur = dwin[0:CHUNK, :]
            acc = jnp.zeros((CHUNK, 128), F32)
            for j in range(CONV_WIDTH):
                acc = acc + w_ref[j:j + 1, :] * dwin[30 - j:30 - j + CHUNK, :]
                prod = dcur * hwin[2 + j:2 + j + CHUNK, :]
                dw_scr[j * 8:(j + 1) * 8, :] += jnp.sum(prod.reshape(CHUNK // 8, 8, 128), axis=0)
            cg = cg_ref[0, rows, :]
            sig = _sigmoid(cg)
            rowid = n * CHUNK + lax.broadcasted_iota(jnp.int32, (CHUNK, 1), 0)
            dh = jnp.where(rowid >= PAD, acc, 0.0)
            dv_ref[0, rows, :] = (dh * sig).astype(BF16)
            dg_ref[0, rows, :] = (dh * cv_ref[0, rows, :] * sig * (1.0 - sig)).astype(BF16)
            return carry

        lax.fori_loop(0, nchunk, step, 0)
        dw_ref[0] = jnp.zeros((32, 128), F32)
        for j in range(CONV_WIDTH):
            dw_ref[0, j:j + 1, :] = jnp.sum(dw_scr[j * 8:(j + 1) * 8, :], axis=0, keepdims=True)

    blk = lambda off: pl.BlockSpec((1, tp, 128), lambda bi, t: (bi, 0, off // 128 + t))
    res, got = _fused_call(
        body, fuse, name="conv_bwd",
        out_shape=[jax.ShapeDtypeStruct((bsz, tp, D_CONV), BF16), jax.ShapeDtypeStruct((bsz, tp, D_CONV), BF16),
                   jax.ShapeDtypeStruct((bsz, 32, D_CONV), F32)],
        grid=(bsz, nct),
        in_specs=[blk(0), blk(C_VAL), blk(C_GATE), pl.BlockSpec((32, 128), lambda bi, t: (0, t))],
        out_specs=[blk(0), blk(0), pl.BlockSpec((1, 32, 128), lambda bi, t: (bi, 0, t))],
        scratch_shapes=[pltpu.VMEM((tp + 32, 128), F32), pltpu.VMEM((tp + 32, 128), F32),
                        pltpu.VMEM((win, 128), F32), pltpu.VMEM((win, 128), F32),
                        pltpu.VMEM((CONV_WIDTH * 8, 128), F32)],
        operands=(dhc, u, u, conv_w))
    return res[0], res[1], res[2], got


def _gla_bwd(dgla, u, o_pre, states, gup, gbias, gnorm, fuse=NO_FUSE):
    bsz, tp, _ = u.shape
    nchunk = tp // CHUNK
    nb = _gla_group(nchunk)

    def body(dy_ref, q_ref, k_ref, v_ref, r_ref, gd_ref, o_ref, st_ref, gup_ref, gb_ref, gn_ref,
             dq_ref, dk_ref, dv_ref, dr_ref, dgd_ref, dgup_ref, vec_ref, h_scr, gup_acc):
        tril, triu, hmask = _gla_consts(nb)
        h_scr[...] = jnp.zeros_like(h_scr)
        gup_acc[...] = jnp.zeros_like(gup_acc)
        gn = gn_ref[...]
        gupb = gup_ref[...]
        m = nb * CHUNK
        ngroup = nchunk // nb

        def group(i, carry):
            dbias, dgn = carry
            g = ngroup - 1 - i
            rows, valid, z, eb, enb, erest, dec, qe, ke, kd = _gla_group_terms(
                g, nb, q_ref, k_ref, gd_ref, gup_ref, gb_ref, tril)
            keb, kdb = ke.astype(BF16), kd.astype(BF16)
            dqe = jnp.zeros((nb, CHUNK, 128), F32)
            dke = jnp.zeros((nb, CHUNK, 128), F32)
            dkd = jnp.zeros((nb, CHUNK, 128), F32)
            ddec = jnp.zeros((nb, 1, 128), F32)
            for h in range(2):
                cols = slice(h * GLA_DV, (h + 1) * GLA_DV)
                o = o_ref[0, rows, cols]
                rh = r_ref[0, rows, cols]
                dy = dy_ref[0, rows, cols]
                rms = lax.rsqrt(jnp.mean(o * o, axis=-1, keepdims=True) + LN_EPS)
                nrm = o * rms
                sig = _sigmoid(rh)
                sw = rh * sig
                dr_ref[0, rows, cols] = (dy * nrm * gn * (sig * (1.0 + rh * (1.0 - sig)))).astype(BF16)
                dgn = dgn + jnp.sum(dy * nrm * sw, axis=0, keepdims=True)
                dn = dy * gn * sw
                do = rms * (dn - nrm * jnp.mean(dn * nrm, axis=-1, keepdims=True))
                dob = do.astype(BF16).reshape(nb, CHUNK, GLA_DV)
                qh = jnp.where(hmask[h], qe, 0.0).astype(BF16)
                vh = v_ref[0, rows, cols].astype(BF16).reshape(nb, CHUNK, GLA_DV)
                ht = h_scr[h]
                hts = [None] * nb
                for n in reversed(range(nb)):
                    hts[n] = ht
                    ht = dec[n] * ht + _tn(dob[n], qh[n])
                h_scr[h] = ht
                htf = jnp.stack(hts)
                htb = htf.astype(BF16)
                st = st_ref[0, h, pl.ds(g * nb, nb)]
                at = jnp.where(triu, _bnt(keb, qh), 0.0).astype(BF16)
                da = jnp.where(tril, _bnt(dob, vh), 0.0).astype(BF16)
                dat = jnp.where(triu, _bnt(vh, dob), 0.0).astype(BF16)
                dqe = dqe + jnp.where(hmask[h], _bnn(da, keb) + _bnn(dob, st.astype(BF16)), 0.0)
                dke = dke + _bnn(dat, qh)
                dv_ref[0, rows, cols] = (_bnn(at, dob) + _bnt(kdb, htb)).reshape(m, GLA_DV).astype(BF16)
                dkd = dkd + jnp.where(hmask[h], _bnn(vh, htb), 0.0)
                ddec = ddec + jnp.where(hmask[h], jnp.sum(htf * st, axis=1, keepdims=True), 0.0)
            dq_ref[0, rows, :] = (dqe * eb * Q_SCALE).reshape(m, 128).astype(BF16)
            dk_ref[0, rows, :] = (dke * enb + dkd * erest).reshape(m, 128).astype(BF16)
            db = dqe * qe - dke * ke - dkd * kd
            dblast = jnp.sum(dkd * kd, axis=1, keepdims=True) + ddec * dec
            lastrow = lax.broadcasted_iota(jnp.int32, (1, CHUNK, 1), 1) == CHUNK - 1
            db = db + jnp.where(lastrow, dblast, 0.0)
            dlg = _bnn(triu.astype(F32), db, precision=lax.Precision.HIGHEST).reshape(m, 128)
            dz = jnp.where(valid, dlg, 0.0) * (1.0 / GLA_TAU) * (1.0 - _sigmoid(z))
            dzb = dz.astype(BF16)
            dgd_ref[0, 0, rows, :] = _nt(dzb, gupb).astype(BF16)
            gup_acc[...] += _tn(gd_ref[0, rows, :].astype(BF16), dzb)
            return dbias + jnp.sum(dz, axis=0, keepdims=True), dgn

        zero = jnp.zeros((1, 128), F32)
        dbias, dgn = lax.fori_loop(0, ngroup, group, (zero, zero))
        dgup_ref[0] = gup_acc[...]
        vec_ref[0] = jnp.zeros((8, 128), F32)
        vec_ref[0, 0:1, :] = dbias
        vec_ref[0, 1:2, :] = dgn

    pair = lambda w, off: pl.BlockSpec((1, tp, w), lambda bi, p: (bi, 0, off // w + p))
    return _fused_call(
        body, fuse, name="gla_bwd",
        out_shape=[jax.ShapeDtypeStruct((bsz, tp, 256), BF16), jax.ShapeDtypeStruct((bsz, tp, 256), BF16),
                   jax.ShapeDtypeStruct((bsz, tp, 512), BF16), jax.ShapeDtypeStruct((bsz, tp, 512), BF16),
                   jax.ShapeDtypeStruct((bsz, 2, tp, 128), BF16), jax.ShapeDtypeStruct((bsz, 128, 256), F32),
                   jax.ShapeDtypeStruct((bsz, 8, 256), F32)],
        grid=(bsz, 2),
        in_specs=[pair(256, 0), pair(128, C_Q), pair(128, C_K), pair(256, C_V), pair(256, C_R),
                  pl.BlockSpec((1, tp, 128), lambda bi, p: (bi, 0, C_GD // 128)),
                  pair(256, 0),
                  pl.BlockSpec((1, 2, nchunk, GLA_DV, 128), lambda bi, p: (bi, p, 0, 0, 0)),
                  pl.BlockSpec((128, 128), lambda bi, p: (0, p)),
                  pl.BlockSpec((1, 128), lambda bi, p: (0, p)),
                  pl.BlockSpec((1, 128), lambda bi, p: (0, 0))],
        out_specs=[pair(128, 0), pair(128, 0), pair(256, 0), pair(256, 0),
                   pl.BlockSpec((1, 1, tp, 128), lambda bi, p: (bi, p, 0, 0)),
                   pl.BlockSpec((1, 128, 128), lambda bi, p: (bi, 0, p)),
                   pl.BlockSpec((1, 8, 128), lambda bi, p: (bi, 0, p))],
        scratch_shapes=[pltpu.VMEM((2, GLA_DV, 128), F32), pltpu.VMEM((128, 128), F32)],
        operands=(dgla, u, u, u, u, u, o_pre, states, gup, gbias, gnorm))


_DU_OFFSETS = (C_VAL, C_GATE, C_Q, C_K, C_V, C_R)
_DU_WIDTHS = (512, 512, 256, 256, 512, 512)


def _du_specs(tm, per, row_map):
    specs = [pl.BlockSpec((tm, w), row_map) for w in _DU_WIDTHS]
    for p in range(2):
        specs.append(pl.BlockSpec((1, 1, tm, 128), lambda *ix, p=p: (row_map(*ix)[0] // per, p, row_map(*ix)[0] % per, 0)))
    return specs


def _du_pieces(refs):
    out = [(off, ref[...]) for off, ref in zip(_DU_OFFSETS, refs[:6])]
    dgd = (refs[6][0, 0].astype(F32) + refs[7][0, 0].astype(F32)).astype(BF16)
    out.append((C_GD, dgd))
    return out


def _in_proj_bwd(pieces, dgd, w_int, dt1, tp, fuse=NO_FUSE):
    r, d = dt1.shape
    tm = _row_tile(tp)
    per = tp // tm

    def body(*refs):
        w_ref, dt_ref, o_ref = refs[8:]
        acc = ALPHA * dt_ref[...]
        for off, val in _du_pieces(refs[:8]):
            acc = acc + _nn(val, w_ref[off:off + val.shape[1], :])
        o_ref[...] = acc

    row = lambda i: (i, 0)
    (ds0,), got = _fused_call(
        body, fuse, name="in_proj_bwd", out_shape=[jax.ShapeDtypeStruct((r, d), F32)], grid=(r // tm,),
        in_specs=_du_specs(tm, per, row) + [pl.BlockSpec((D_IN_PAD, d), lambda i: (0, 0)), pl.BlockSpec((tm, d), row)],
        out_specs=[pl.BlockSpec((tm, d), row)], scratch_shapes=[], operands=(*pieces, dgd, dgd, w_int, dt1))
    return ds0, got


def _grad_w_in(pieces, dgd, s0b, tp, fuse=NO_FUSE):
    r, d = s0b.shape
    tk = _reduce_tile(tp, False)
    per = tp // tk

    def body(*refs):
        s_ref, o_ref = refs[8:]

        @pl.when(pl.program_id(0) == 0)
        def _():
            o_ref[...] = jnp.zeros_like(o_ref)

        s = s_ref[...]
        for off, val in _du_pieces(refs[:8]):
            o_ref[off:off + val.shape[1], :] += _tn(val, s)

    row = lambda k: (k, 0)
    (out,), got = _fused_call(
        body, fuse, name="grad_w_in", out_shape=[jax.ShapeDtypeStruct((D_IN_PAD, d), F32)], grid=(r // tk,),
        in_specs=_du_specs(tk, per, row) + [pl.BlockSpec((tk, d), row)],
        out_specs=[pl.BlockSpec((D_IN_PAD, d), lambda k: (0, 0))], scratch_shapes=[],
        operands=(*pieces, dgd, dgd, s0b))
    return out, got


def _ln_in_bwd(ds0, x, meta, g):
    bsz, s, d = x.shape
    tp = s + HEAD
    nh = 2
    sh = s // nh
    rc = min(256, sh)

    def body(ds_ref, x_ref, meta_ref, g_ref, gx_ref, dm_ref, vec_ref):
        h = pl.program_id(1)
        gg = g_ref[...]

        @pl.when(h == 0)
        def _():
            mh, mr = _ln_stats(meta_ref[...])
            dsm = ds_ref[0, PAD:HEAD, :]
            dm_ref[0] = _ln_bwd(dsm * gg, mh, mr)
            vec_ref[0] = jnp.zeros((8, d), F32)
            vec_ref[0, 0:1, :] = jnp.sum(dsm * mh, axis=0, keepdims=True)
            vec_ref[0, 1:2, :] = jnp.sum(dsm, axis=0, keepdims=True)

        def step(i, carry):
            sg, sb = carry
            dst = pl.ds(pl.multiple_of(i * rc, rc), rc)
            src = pl.ds(pl.multiple_of(HEAD + h * sh + i * rc, 64), rc)
            xh, rstd = _ln_stats(x_ref[0, dst, :])
            dsv = ds_ref[0, src, :]
            gx_ref[0, dst, :] = _ln_bwd(dsv * gg, xh, rstd)
            return sg + jnp.sum(dsv * xh, axis=0, keepdims=True), sb + jnp.sum(dsv, axis=0, keepdims=True)

        zero = jnp.zeros((1, d), F32)
        sg, sb = lax.fori_loop(0, sh // rc, step, (zero, zero))
        vec_ref[0, 0:1, :] += sg
        vec_ref[0, 1:2, :] += sb

    return pl.pallas_call(
        body, name="ln_in_bwd",
        out_shape=[jax.ShapeDtypeStruct((bsz, s, d), F32), jax.ShapeDtypeStruct((bsz, N_META, d), F32),
                   jax.ShapeDtypeStruct((bsz, 8, d), F32)],
        grid=(bsz, nh),
        in_specs=[pl.BlockSpec((1, tp, d), lambda bi, hi: (bi, 0, 0)),
                  pl.BlockSpec((1, sh, d), lambda bi, hi: (bi, hi, 0)),
                  pl.BlockSpec((N_META, d), lambda bi, hi: (0, 0)),
                  pl.BlockSpec((1, d), lambda bi, hi: (0, 0))],
        out_specs=[pl.BlockSpec((1, sh, d), lambda bi, hi: (bi, hi, 0)),
                   pl.BlockSpec((1, N_META, d), lambda bi, hi: (bi, 0, 0)),
                   pl.BlockSpec((1, 8, d), lambda bi, hi: (bi, 0, 0))],
        compiler_params=_params(("parallel", "arbitrary")),
    )(ds0, x, meta, g)


def _rows128(a):
    return a.reshape(-1, 128)


def kernel(x, meta_tokens, ln_in_g, ln_in_b, w_in, conv_w, conv_b, conv_ln_g, conv_ln_b, gate_up, gate_bias, gla_norm_g, w_out, ln1_g, ln1_b, w_ff1, w_ff2, ln2_g, ln2_b, loss_target, m_meta_tokens, m_ln_in_g, m_ln_in_b, m_w_in, m_conv_w, m_conv_b, m_conv_ln_g, m_conv_ln_b, m_gate_up, m_gate_bias, m_gla_norm_g, m_w_out, m_ln1_g, m_ln1_b, m_w_ff1, m_w_ff2, m_ln2_g, m_ln2_b, v_meta_tokens, v_ln_in_g, v_ln_in_b, v_w_in, v_conv_w, v_conv_b, v_conv_ln_g, v_conv_ln_b, v_gate_up, v_gate_bias, v_gla_norm_g, v_w_out, v_ln1_g, v_ln1_b, v_w_ff1, v_w_ff2, v_ln2_g, v_ln2_b):
    bsz, seq, d = x.shape
    tp = seq + HEAD
    r = bsz * tp
    xi, yi, ci = _mesh_pos()
    chip = 2 * xi + yi
    c_arr = jnp.reshape(ci, (1,)).astype(jnp.int32)
    pos_arr = jnp.stack([chip, ci]).astype(jnp.int32)

    sh_in = D_IN // 4
    shard_in = jnp.pad(w_in[0].T.astype(BF16), ((0, D_IN_PAD // 4 - sh_in), (0, 0)))
    shard_w1, shard_wout, shard_w2 = w_ff1[0].astype(BF16), w_out[0].astype(BF16), w_ff2[0].astype(BF16)
    small_w = jnp.concatenate([_rows128(meta_tokens), _rows128(conv_w[0]), _rows128(gate_up[0])], axis=0)
    ln_in_g2, ln_in_b2 = ln_in_g.reshape(1, d), ln_in_b.reshape(1, d)

    s0, s0b, ((g_int,), (g_small,)) = _ln_in_fwd(
        x, ln_in_g2, ln_in_b2, [("gather", [(shard_in, 0, D_IN_PAD // 4, None)]), ("small", [small_w])])
    (g_int,) = _place_own([g_int], [shard_in])
    (g_small,) = _place_own([g_small], [small_w])
    n_meta_rows, n_cw_rows = N_META * 256 // 128, CONV_WIDTH * 128 // 128
    meta_full = jnp.concatenate([g_small[j, :n_meta_rows].reshape(N_META, 256) for j in range(4)], axis=1)
    convw_full = jnp.concatenate(
        [g_small[j, n_meta_rows:n_meta_rows + n_cw_rows].reshape(CONV_WIDTH, 128) for j in range(4)], axis=1)
    gup_full = jnp.concatenate(
        [g_small[j, n_meta_rows + n_cw_rows:].reshape(GLA_RANK, 64) for j in range(4)], axis=1)
    convw_p = jnp.pad(convw_full, ((0, 1), (0, 0)))
    gup_p = jnp.pad(gup_full, ((0, 128 - GLA_RANK), (0, 0))).astype(BF16)
    s0, s0b = _ln_meta(s0, s0b, meta_full, ln_in_g2, ln_in_b2)
    w_int = jnp.pad(g_int[:, :sh_in].reshape(D_IN, d), ((0, D_IN_PAD - D_IN), (0, 0)))
    s0f, s0bf = s0.reshape(r, d), s0b.reshape(r, d)
    u, ((w1_buf,),) = _in_proj(s0bf, w_int, [("gather", [(shard_w1, 0, 640, None)])])
    (w1_buf,) = _place_own([w1_buf], [shard_w1])
    u3 = u.reshape(bsz, tp, D_IN_PAD)
    hc, ((w1_buf,),) = _conv_fwd(u3, convw_p, conv_b, [("gather", [(shard_w1, 640, 384, w1_buf)])])
    hc = hc.reshape(r, D_CONV)
    hc, u3_after_conv = lax.optimization_barrier((hc, u3))
    gla_out, o_pre, states, ((g_wout,),) = _gla_fwd(
        u3_after_conv, gup_p, gate_bias, gla_norm_g, [("gather", [(shard_wout, 0, 256, None)])])
    (g_wout,) = _place_own([g_wout], [shard_wout])
    wout = g_wout.reshape(d, d)
    gla_of = gla_out.reshape(r, 512)
    conv_of, xhat1, rstd1, s1b, ((w2_buf,),) = _out_proj_ln1(
        hc, gla_of, wout, s0f, conv_ln_g, conv_ln_b, ln1_g, ln1_b, [("gather", [(shard_w2, 0, 448, None)])])
    (w2_buf,) = _place_own([w2_buf], [shard_w2])
    w1 = w1_buf
    ra, ((w2_buf,),) = _ffn1(s1b, w1, [("gather", [(shard_w2, 448, 576, w2_buf)])])
    w2 = w2_buf.reshape(D_FF, d)
    dt2, dt2b, acc2 = _ffn2_ln2_loss(ra, w2, xhat1, ln1_g, ln1_b, ln2_g, ln2_b, loss_target, tp)

    def add_pair(g, got, splits=None):
        return _add_pair(g, got, c_arr, splits)

    da = _ffn_bwd_da(dt2b, w2, ra)
    dt1, dt1b, acc1 = _ffn_bwd_ln1(da, w1, dt2, xhat1, rstd1, ln1_g)
    g_w2, _ = _matmul_tn(ra, dt2b, 1024, square_lhs=True, name="grad_w_ff2")
    big_w2 = g_w2.reshape(4, D_FF // 4, d)
    big_w1, ((pair_w2,),) = _grad_w_ff1(s1b, da, [("pair", [big_w2])])
    dhc, dgla, cacc, ((pair_w1,),) = _out_proj_bwd(dt1b, wout, hc, conv_ln_g, conv_ln_b, [("pair", [big_w1])])
    (part_w1,), (part_w2a, part_w2b) = add_pair(big_w1, pair_w1), add_pair(big_w2, pair_w2, [512, 512])
    g_wout, _ = _grad_w_out(conv_of, gla_of, dt1b)
    big_wout = g_wout.reshape(4, d // 4, d)
    dcv, dcg, dcw, ((chip_w1,), (pair_wout,)) = _conv_bwd(
        dhc.reshape(bsz, tp, D_CONV), u3, convw_p, [("exchange", [part_w1]), ("pair", [big_wout])])
    (part_wout,) = add_pair(big_wout, pair_wout)
    (dq, dk, dv, dr, dgd, dgup, gvec), ((chip_w2a,),) = _gla_bwd(
        dgla.reshape(bsz, tp, 512), u3, o_pre, states, gup_p, gate_bias, gla_norm_g, [("exchange", [part_w2a])])
    pieces = [a.reshape(r, a.shape[-1]) for a in (dcv, dcg, dq, dk, dv, dr)]
    g_wint, ((chip_w2b, chip_wout),) = _grad_w_in(
        pieces, dgd, s0bf, tp, [("exchange", [part_w2b, part_wout])])
    big_win = jnp.stack([g_wint[j * sh_in:(j + 1) * sh_in] for j in range(4)])
    (half_w1, half_w2), ((pair_win,),) = _add_chips_many(
        [(big_w1, pair_w1, [chip_w1]), (big_w2, pair_w2, [chip_w2a, chip_w2b])], [("pair", [big_win])])
    ds0, ((chip_win,), (sib_w1, sib_w2)) = _in_proj_bwd(
        pieces, dgd, w_int, dt1, tp, [("exchange", add_pair(big_win, pair_win)), ("swap", [half_w1, half_w2])])
    grad_x, dmeta, lvec = _ln_in_bwd(ds0.reshape(bsz, tp, d), x, meta_full, ln_in_g2)

    half_win = _add_chips(big_win, pair_win, [chip_win], pos_arr)
    half_wout = _add_chips(big_wout, pair_wout, [chip_wout], pos_arr)
    sib_win, sib_wout = _pair_swap([half_win, half_wout])
    grads = {}
    weights = dict(meta_tokens=meta_tokens, ln_in_g=ln_in_g, ln_in_b=ln_in_b, w_in=w_in, conv_w=conv_w, conv_b=conv_b,
                   conv_ln_g=conv_ln_g, conv_ln_b=conv_ln_b, gate_up=gate_up, gate_bias=gate_bias,
                   gla_norm_g=gla_norm_g, w_out=w_out, ln1_g=ln1_g, ln1_b=ln1_b, w_ff1=w_ff1, w_ff2=w_ff2,
                   ln2_g=ln2_g, ln2_b=ln2_b)
    moms = dict(meta_tokens=(m_meta_tokens, v_meta_tokens), ln_in_g=(m_ln_in_g, v_ln_in_g),
                ln_in_b=(m_ln_in_b, v_ln_in_b), w_in=(m_w_in, v_w_in), conv_w=(m_conv_w, v_conv_w),
                conv_b=(m_conv_b, v_conv_b), conv_ln_g=(m_conv_ln_g, v_conv_ln_g),
                conv_ln_b=(m_conv_ln_b, v_conv_ln_b), gate_up=(m_gate_up, v_gate_up),
                gate_bias=(m_gate_bias, v_gate_bias), gla_norm_g=(m_gla_norm_g, v_gla_norm_g),
                w_out=(m_w_out, v_w_out), ln1_g=(m_ln1_g, v_ln1_g), ln1_b=(m_ln1_b, v_ln1_b),
                w_ff1=(m_w_ff1, v_w_ff1), w_ff2=(m_w_ff2, v_w_ff2), ln2_g=(m_ln2_g, v_ln2_g),
                ln2_b=(m_ln2_b, v_ln2_b))
    names = list(weights)
    big_names = ("w_in", "w_out", "w_ff1", "w_ff2")
    delta, new_m, new_v = {}, {}, {}
    def adamw_big(k, own, sib, fuse=NO_FUSE):
        to2d = (lambda a: a[0].T) if k == "w_in" else (lambda a: a[0])
        back = (lambda a: a.T[None]) if k == "w_in" else (lambda a: a[None])
        res, got = _adamw_halves(to2d(weights[k]), own, sib, to2d(moms[k][0]), to2d(moms[k][1]), fuse)
        grads[k], delta[k], new_m[k], new_v[k] = [back(a) for a in res]
        return got

    small_pack = _pack_small(acc2, acc1, lvec, cacc, gvec, dcw, dgup, dmeta)
    ((small_all,),) = adamw_big("w_in", half_win, sib_win, [("all", [small_pack])])
    small_all, half_w1, sib_w1, half_w2, sib_w2 = lax.optimization_barrier(
        (small_all, half_w1, sib_w1, half_w2, sib_w2))
    adamw_big("w_ff1", half_w1, sib_w1)
    adamw_big("w_ff2", half_w2, sib_w2)
    adamw_big("w_out", half_wout, sib_wout)
    red, loss = _sum_small(small_all)
    small_names = [k for k in names if k not in big_names]
    two = lambda a: a.reshape(-1, a.shape[-1])

    def my_cols(full, rows, width):
        return lax.dynamic_slice(full, (0, chip * width), (rows, width))

    sharded = {
        "meta_tokens": my_cols(red[SMALL_AT["meta_tokens"][0]:SMALL_AT["meta_tokens"][0] + N_META], N_META, 256),
        "conv_w": my_cols(red[SMALL_CONV_W:SMALL_CONV_W + 16].reshape(32, D_CONV), CONV_WIDTH, 128),
        "gate_up": my_cols(red[SMALL_GATE_UP:SMALL_GATE_UP + 4].reshape(GLA_RANK, 256), GLA_RANK, 64)}
    upd = _adamw_small(red, sharded, {k: (two(weights[k]), two(moms[k][0]), two(moms[k][1])) for k in small_names})
    for k in small_names:
        shp = weights[k].shape
        grads[k], delta[k], new_m[k], new_v[k] = [a.reshape(shp) for a in upd[k]]
    loss = loss.reshape(())

    return (loss, grad_x, *[grads[k] for k in names], *[delta[k] for k in names],
            *[new_m[k] for k in names], *[new_v[k] for k in names])
```

```python
import functools

import jax
import jax.numpy as jnp
from jax import lax
from jax.experimental import pallas as pl
from jax.experimental.pallas import tpu as pltpu

F32 = jnp.float32
BF16 = jnp.bfloat16

D_MODEL = 1024
N_META = 16
D_CONV = 512
CONV_WIDTH = 31
GLA_HEADS = 4
GLA_DV = 128
GLA_DK = 64
GLA_RANK = 16
GLA_TAU = 16.0
CHUNK = 64
D_FF = 4096
LN_EPS = 1e-5
ALPHA = 2.0 ** 0.25
D_IN = 2576
D_IN_PAD = 2688
PAD = CHUNK - N_META
HEAD = PAD + N_META
Q_SCALE = GLA_DK ** -0.5
ADAM_LR, ADAM_B1, ADAM_B2, ADAM_EPS, ADAM_WD, ADAM_STEP = 0.001, 0.9, 0.999, 1e-08, 0.01, 10
HALF = D_MODEL // 2
VMEM_LIMIT = 56 * 1024 * 1024
MESH = pl.DeviceIdType.MESH

C_VAL, C_GATE, C_Q, C_K, C_V, C_R, C_GD = 0, 512, 1024, 1280, 1536, 2048, 2560

SMALL_AT = {"loss": (0, 1, 0, 1024), "ln_in_g": (1, 1, 0, 1024), "ln_in_b": (2, 1, 0, 1024), "ln1_g": (3, 1, 0, 1024),
            "ln1_b": (4, 1, 0, 1024), "ln2_g": (5, 1, 0, 1024), "ln2_b": (6, 1, 0, 1024), "conv_b": (7, 1, 0, 512),
            "conv_ln_g": (7, 1, 512, 512), "conv_ln_b": (8, 1, 0, 512), "gate_bias": (8, 1, 512, 256),
            "gla_norm_g": (8, 1, 768, 128), "meta_tokens": (40, 16, 0, 1024)}
SMALL_CONV_W, SMALL_GATE_UP = 16, 32
SMALL_ROWS = 56


def _params(sem=None, **kw):
    return pltpu.CompilerParams(dimension_semantics=sem, vmem_limit_bytes=VMEM_LIMIT, **kw)


def _row_tile(tp):
    for t in (704, 352, 192, 64):
        if tp % t == 0:
            return t
    raise ValueError(tp)


def _reduce_tile(tp, big):
    for t in ((2112, 1056, 704) if big else (1056, 704)) + (352, 192, 64):
        if tp % t == 0:
            return t
    raise ValueError(tp)


def _sub_rows(tm):
    return [slice(0, tm)]


def _dot(a, b, dims, precision=None):
    return lax.dot_general(a, b, (dims, ((), ())), preferred_element_type=F32, precision=precision)


def _nn(a, b, **kw):
    return _dot(a, b, ((1,), (0,)), **kw)


def _nt(a, b, **kw):
    return _dot(a, b, ((1,), (1,)), **kw)


def _tn(a, b, **kw):
    return _dot(a, b, ((0,), (0,)), **kw)


def _sigmoid(x):
    return 1.0 / (1.0 + jnp.exp(-x))


def _log_sigmoid(z):
    return jnp.minimum(z, 0.0) - jnp.log(1.0 + jnp.exp(-jnp.abs(z)))


def _ln_stats(t):
    mu = jnp.mean(t, axis=-1, keepdims=True)
    d = t - mu
    var = jnp.mean(d * d, axis=-1, keepdims=True)
    rstd = lax.rsqrt(var + LN_EPS)
    return d * rstd, rstd


def _ln_bwd(dxhat, xhat, rstd):
    m1 = jnp.mean(dxhat, axis=-1, keepdims=True)
    m2 = jnp.mean(dxhat * xhat, axis=-1, keepdims=True)
    return rstd * (dxhat - m1 - xhat * m2)


def _mesh_pos():
    return lax.axis_index("x"), lax.axis_index("y"), lax.axis_index("c")


ANY = pl.BlockSpec(memory_space=pl.ANY)


def _gather_sems(n):
    return [pltpu.SemaphoreType.DMA((n, 3))] * 4


def _gather_steps(ins, outs, sems, ranges=None):
    n = len(ins)
    send, recv, fsend, frecv = sems
    ranges = ranges or [(0, ref.shape[0]) for ref in ins]
    x, y, c = _mesh_pos()
    me = 2 * x + y
    sibling = (x, y, 1 - c)
    chips = [(1 - x, y), (x, 1 - y), (1 - x, 1 - y)]
    chip_idx = [2 * px + py for px, py in chips]
    mine = [pl.ds(pl.multiple_of(r0 + c * (nr // 2), 16), nr // 2) for r0, nr in ranges]
    other = [pl.ds(pl.multiple_of(r0 + (1 - c) * (nr // 2), 16), nr // 2) for r0, nr in ranges]
    pairs = [(a, k) for a in range(n) for k in range(3)]

    def ici(a, k, slab):
        return pltpu.make_async_remote_copy(
            src_ref=ins[a].at[mine[a], :], dst_ref=outs[a].at[slab, mine[a], :],
            send_sem=send.at[a, k], recv_sem=recv.at[a, k], device_id=(*chips[k], c), device_id_type=MESH)

    def forward(a, k, rows):
        blk = outs[a].at[chip_idx[k], rows[a], :]
        return pltpu.make_async_remote_copy(
            src_ref=blk, dst_ref=blk, send_sem=fsend.at[a, k], recv_sem=frecv.at[a, k],
            device_id=sibling, device_id_type=MESH)

    def start():
        for a, k in pairs:
            ici(a, k, me).start()

    def relay():
        for a, k in pairs:
            ici(a, k, chip_idx[k]).wait_recv()
            forward(a, k, mine).start()

    def finish():
        for a, k in pairs:
            forward(a, k, other).wait_recv()
        for a, k in pairs:
            ici(a, k, me).wait_send()
            forward(a, k, mine).wait_send()

    return start, relay, finish


def _place_own(gathered, shards):
    chip = 2 * lax.axis_index("x") + lax.axis_index("y")
    return [lax.dynamic_update_slice(g, s[None], (chip, 0, 0)) for g, s in zip(gathered, shards)]


def _gather_small_steps(ins, outs, sems):
    send, recv = sems
    x, y, c = _mesh_pos()
    chips = [(1 - x, y), (x, 1 - y), (1 - x, 1 - y)]

    def copy(a, k, slot):
        return pltpu.make_async_remote_copy(
            src_ref=ins[a], dst_ref=outs[a].at[slot], send_sem=send.at[a, k], recv_sem=recv.at[a, k],
            device_id=(*chips[k], c), device_id_type=MESH)

    def start():
        for a in range(len(ins)):
            for k in range(3):
                copy(a, k, 2 * x + y).start()

    def finish():
        for a in range(len(ins)):
            for k, (px, py) in enumerate(chips):
                copy(a, k, 2 * px + py).wait_recv()
            for k in range(3):
                copy(a, k, 2 * x + y).wait_send()

    return start, finish


def _pair_swap_steps(ins, outs, sems):
    send, recv = sems
    x, y, c = _mesh_pos()
    cps = [pltpu.make_async_remote_copy(
        src_ref=ins[a], dst_ref=outs[a], send_sem=send.at[a], recv_sem=recv.at[a],
        device_id=(x, y, 1 - c), device_id_type=MESH) for a in range(len(ins))]

    def start():
        for cp in cps:
            cp.start()

    def finish():
        for cp in cps:
            cp.wait()

    return start, finish


def _pair_exchange_steps(ins, outs, sems):
    send, recv = sems
    x, y, c = _mesh_pos()
    other = pl.ds(pl.multiple_of((1 - c) * HALF, 128), HALF)
    cps = [pltpu.make_async_remote_copy(
        src_ref=ins[a].at[:, :, other], dst_ref=outs[a], send_sem=send.at[a], recv_sem=recv.at[a],
        device_id=(x, y, 1 - c), device_id_type=MESH) for a in range(len(ins))]

    def start():
        for cp in cps:
            cp.start()

    def finish():
        for cp in cps:
            cp.wait()

    return start, finish


def _pair_exchange_shapes(grads):
    return [jax.ShapeDtypeStruct(g.shape[:2] + (HALF,), g.dtype) for g in grads]


NO_FUSE = ()


def _fuse_plan(kind, items):
    n = len(items)
    if kind == "gather":
        shards = [it[0] for it in items]
        bufs = [it[3] for it in items if it[3] is not None]
        alias, b = {}, 0
        for a, it in enumerate(items):
            if it[3] is not None:
                alias[n + b] = a
                b += 1
        ranges = [(it[1], it[2]) for it in items]
        return (shards + bufs, [jax.ShapeDtypeStruct((4,) + s.shape, s.dtype) for s in shards], alias, _gather_sems(n),
                lambda i, o, s: _gather_steps(i[:n], o, s, ranges))
    if kind == "exchange":
        return list(items), _chip_exchange_shapes(items), {}, _chip_exchange_sems(n), _chip_exchange_steps
    if kind == "small":
        return (list(items), [jax.ShapeDtypeStruct((4,) + s.shape, s.dtype) for s in items], {},
                _chip_exchange_sems(n), _gather_small_steps)
    if kind == "all":
        (pack,) = items
        return ([pack], [jax.ShapeDtypeStruct((8 * pack.shape[0], pack.shape[1]), pack.dtype)], {},
                [pltpu.SemaphoreType.DMA((7,)), pltpu.SemaphoreType.DMA((7,)), pltpu.SemaphoreType.DMA((1,))],
                _allgather_all_steps)
    pair_sems = [pltpu.SemaphoreType.DMA((n,)), pltpu.SemaphoreType.DMA((n,))]
    if kind == "swap":
        return (list(items), [jax.ShapeDtypeStruct(h.shape, h.dtype) for h in items], {}, pair_sems, _pair_swap_steps)
    return list(items), _pair_exchange_shapes(items), {}, pair_sems, _pair_exchange_steps


def _fused_call(body, fuse, *, name, grid, in_specs, out_specs, out_shape, scratch_shapes, operands,
                late_relay=False):
    plans = [_fuse_plan(kind, list(items)) for kind, items in fuse if len(items)]
    n_in, n_out, n_s = len(in_specs), len(out_shape), len(scratch_shapes)
    comm = [a for p in plans for a in p[0]]
    shapes = [s for p in plans for s in p[1]]
    nc, no = len(comm), len(shapes)
    aliases, i_at, o_at = {}, n_in, n_out
    for p in plans:
        for i, o in p[2].items():
            aliases[i_at + i] = o_at + o
        i_at, o_at = i_at + len(p[0]), o_at + len(p[1])

    def wrapped(*refs):
        o0 = n_in + nc
        s0 = o0 + n_out + no
        i_at, o_at, sem_at, steps = n_in, o0 + n_out, s0 + n_s, []
        for p in plans:
            steps.append(p[4](refs[i_at:i_at + len(p[0])], refs[o_at:o_at + len(p[1])], refs[sem_at:sem_at + len(p[3])]))
            i_at, o_at, sem_at = i_at + len(p[0]), o_at + len(p[1]), sem_at + len(p[3])
        first, last = _grid_ends(grid)
        for st in steps:
            pl.when(first)(st[0])
        if not late_relay:
            for st in steps:
                for mid in st[1:-1]:
                    pl.when(last)(mid)
        body(*refs[:n_in], *refs[o0:o0 + n_out], *refs[s0:s0 + n_s])
        for st in steps:
            for step in (st[1:] if late_relay else st[-1:]):
                pl.when(last)(step)

    res = pl.pallas_call(
        wrapped if plans else body, name=name, grid=grid, in_specs=list(in_specs) + [ANY] * nc,
        out_specs=list(out_specs) + [ANY] * no, out_shape=list(out_shape) + shapes,
        scratch_shapes=list(scratch_shapes) + [s for p in plans for s in p[3]], input_output_aliases=aliases,
        compiler_params=_params(("arbitrary",) * len(grid)))(*operands, *comm)
    outs, got, results = list(res[:n_out]), list(res[n_out:]), []
    for p in plans:
        results.append(got[:len(p[1])])
        got = got[len(p[1]):]
    return outs, results


def _chip_exchange_sems(n):
    return [pltpu.SemaphoreType.DMA((n, 3))] * 2


def _chip_exchange_shapes(parts):
    return [jax.ShapeDtypeStruct((3,) + p.shape[1:], p.dtype) for p in parts]


def _chip_exchange_steps(ins, outs, sems):
    send, recv = sems
    x, y, c = _mesh_pos()
    chips = [(1 - x, y), (x, 1 - y), (1 - x, 1 - y)]
    cps = [pltpu.make_async_remote_copy(
        src_ref=ins[a].at[2 * px + py], dst_ref=outs[a].at[k], send_sem=send.at[a, k], recv_sem=recv.at[a, k],
        device_id=(px, py, c), device_id_type=MESH) for a in range(len(ins)) for k, (px, py) in enumerate(chips)]

    def start():
        for cp in cps:
            cp.start()

    def finish():
        for cp in cps:
            cp.wait()

    return start, finish


def _pair_swap(halves):
    n = len(halves)

    def body(*refs):
        start, finish = _pair_swap_steps(refs[:n], refs[n:2 * n], refs[2 * n:])
        start()
        finish()

    return pl.pallas_call(
        body, name="grad_pair_swap",
        out_shape=[jax.ShapeDtypeStruct(h.shape, h.dtype) for h in halves],
        in_specs=[ANY] * n, out_specs=[ANY] * n,
        scratch_shapes=[pltpu.SemaphoreType.DMA((n,)), pltpu.SemaphoreType.DMA((n,))],
    )(*halves)


def _allgather_all_steps(ins, outs, sems):
    send_sems, recv_sems, local_sem = sems
    x_ref, out_ref = ins[0], outs[0]
    m_per = x_ref.shape[0]
    x, y, c = _mesh_pos()
    me, sibling = (x, y, c), (x, y, 1 - c)
    chips = [(1 - x, y), (x, 1 - y), (1 - x, 1 - y)]

    def rows(px, py, pc):
        return out_ref.at[pl.ds(pl.multiple_of((4 * px + 2 * py + pc) * m_per, 8), m_per), :]

    def copy(k, block, to, src=None):
        return pltpu.make_async_remote_copy(
            src_ref=rows(*block) if src is None else src, dst_ref=rows(*block),
            send_sem=send_sems.at[k], recv_sem=recv_sems.at[k], device_id=to, device_id_type=MESH)

    mine = pltpu.make_async_copy(x_ref, rows(*me), local_sem.at[0])
    first = [copy(0, me, sibling, src=x_ref)]
    first += [copy(1 + j, me, (*chip, c), src=x_ref) for j, chip in enumerate(chips)]
    passed = [copy(4 + j, (*chip, c), sibling) for j, chip in enumerate(chips)]

    def start():
        mine.start()
        for cp in first:
            cp.start()

    def relay():
        for j, chip in enumerate(chips):
            copy(1 + j, (*chip, c), me).wait_recv()
            passed[j].start()

    def finish():
        copy(0, sibling, me).wait_recv()
        for j, chip in enumerate(chips):
            copy(4 + j, (*chip, 1 - c), me).wait_recv()
        for cp in first + passed:
            cp.wait_send()
        mine.wait()

    return start, relay, finish


def _add_pair(g, got, c_arr, splits=None):
    _, rows, _ = g.shape
    splits = splits or [rows]

    def body(c_ref, g_ref, r_ref, *o_refs):
        at = 0
        for o_ref, n in zip(o_refs, splits):
            o_ref[...] = (g_ref[:, at:at + n, :] + r_ref[:, at:at + n, :]).astype(BF16)
            at += n

    return pl.pallas_call(
        body, name="grad_add_pair", out_shape=[jax.ShapeDtypeStruct((4, n, HALF), BF16) for n in splits],
        grid_spec=pltpu.PrefetchScalarGridSpec(
            num_scalar_prefetch=1, grid=(4,),
            in_specs=[pl.BlockSpec((1, rows, HALF), lambda j, c: (j, 0, c[0])),
                      pl.BlockSpec((1, rows, HALF), lambda j, c: (j, 0, 0))],
            out_specs=[pl.BlockSpec((1, n, HALF), lambda j, c: (j, 0, 0)) for n in splits]),
        compiler_params=_params(("arbitrary",)),
    )(c_arr, g, got)


def _add_chips_many(items, fuse=NO_FUSE, late_relay=False):
    n = len(items)
    rows = [it[0].shape[1] for it in items]
    n_got = [len(it[2]) for it in items]

    def body(*refs):
        g_refs, p_refs = refs[:n], refs[n:2 * n]
        got_refs = refs[2 * n:2 * n + sum(n_got)]
        o_refs = refs[2 * n + sum(n_got):3 * n + sum(n_got)]
        scr = refs[3 * n + sum(n_got):]
        x, y, c = _mesh_pos()
        chip = 2 * x + y
        mine = pl.ds(pl.multiple_of(c * HALF, 128), HALF)
        copies = []
        for a in range(n):
            copies.append((pltpu.make_async_copy(g_refs[a].at[chip, :, mine], scr[2 * a], scr[2 * n].at[2 * a]),
                           pltpu.make_async_copy(p_refs[a].at[chip], scr[2 * a + 1], scr[2 * n].at[2 * a + 1])))
        for cg, cp in copies:
            cg.start()
            cp.start()
        at_ref = 0
        for a in range(n):
            copies[a][0].wait()
            copies[a][1].wait()
            at = 0
            for r_ref in got_refs[at_ref:at_ref + n_got[a]]:
                k = r_ref.shape[1]
                own = scr[2 * a][at:at + k, :] + scr[2 * a + 1][at:at + k, :]
                o_refs[a][at:at + k, :] = ((own + r_ref[0].astype(F32)) + r_ref[1].astype(F32)) + r_ref[2].astype(F32)
                at += k
            at_ref += n_got[a]

    gots = [t for it in items for t in it[2]]
    outs, got = _fused_call(
        body, fuse, name="grad_add_chips_many", grid=(1,),
        in_specs=[ANY] * (2 * n) + [pl.BlockSpec(t.shape, lambda i: (0, 0, 0)) for t in gots],
        out_specs=[pl.BlockSpec((r, HALF), lambda i: (0, 0)) for r in rows],
        out_shape=[jax.ShapeDtypeStruct((r, HALF), F32) for r in rows],
        scratch_shapes=[pltpu.VMEM((r, HALF), F32) for r in rows for _ in range(2)] + [pltpu.SemaphoreType.DMA((2 * n,))],
        operands=(*[it[0] for it in items], *[it[1] for it in items], *gots), late_relay=late_relay)
    return outs, got


def _pack_small(acc2, acc1, lvec, cacc, gvec, dcw, dgup, dmeta):
    bsz = lvec.shape[0]

    def body(a2_ref, a1_ref, lv_ref, ca_ref, gv_ref, cw_ref, gu_ref, dm_ref, o_ref):
        def put(name, val):
            r0, nr, l0, nl = SMALL_AT[name]
            o_ref[r0:r0 + nr, l0:l0 + nl] = val

        def put_folded(row0, val, per_row):
            w = val.shape[1]
            for i in range(val.shape[0]):
                o_ref[row0 + i // per_row:row0 + i // per_row + 1, (i % per_row) * w:(i % per_row + 1) * w] = val[i:i + 1]

        over_b = lambda f: functools.reduce(lambda a, b: a + b, [f(b) for b in range(bsz)])
        o_ref[...] = jnp.zeros_like(o_ref)
        put("loss", a2_ref[0:1, :])
        put("ln2_g", a2_ref[1:2, :])
        put("ln2_b", a2_ref[2:3, :])
        put("ln1_g", a1_ref[0:1, :])
        put("ln1_b", a1_ref[1:2, :])
        put("ln_in_g", over_b(lambda b: lv_ref[b, 0:1, :]))
        put("ln_in_b", over_b(lambda b: lv_ref[b, 1:2, :]))
        put("conv_b", ca_ref[0:1, :])
        put("conv_ln_g", ca_ref[1:2, :])
        put("conv_ln_b", ca_ref[2:3, :])
        put("gate_bias", over_b(lambda b: gv_ref[b, 0:1, :]))
        put("gla_norm_g", over_b(lambda b: gv_ref[b, 1:2, 0:128] + gv_ref[b, 1:2, 128:256]))
        put_folded(SMALL_CONV_W, over_b(lambda b: cw_ref[b]), D_MODEL // D_CONV)
        put_folded(SMALL_GATE_UP, over_b(lambda b: gu_ref[b, 0:GLA_RANK, :]), D_MODEL // 256)
        put("meta_tokens", over_b(lambda b: dm_ref[b]))

    return pl.pallas_call(
        body, name="pack_small", out_shape=jax.ShapeDtypeStruct((SMALL_ROWS, D_MODEL), F32),
    )(acc2, acc1, lvec, cacc, gvec, dcw, dgup, dmeta)


def _sum_small(gathered):
    def body(g_ref, o_ref, loss_ref):
        acc = g_ref[0:SMALL_ROWS, :]
        for d in range(1, 8):
            acc = acc + g_ref[d * SMALL_ROWS:(d + 1) * SMALL_ROWS, :]
        o_ref[...] = acc
        loss_ref[...] = jnp.sum(acc[0:1, :], axis=1, keepdims=True)

    return pl.pallas_call(
        body, name="sum_small",
        out_shape=[jax.ShapeDtypeStruct((SMALL_ROWS, D_MODEL), F32), jax.ShapeDtypeStruct((1, 1), F32)],
    )(gathered)


def _adamw_math(w, g, m, v):
    c1 = 1.0 - ADAM_B1 ** ADAM_STEP
    c2 = 1.0 - ADAM_B2 ** ADAM_STEP
    mn = ADAM_B1 * m + (1.0 - ADAM_B1) * g
    vn = ADAM_B2 * v + (1.0 - ADAM_B2) * (g * g)
    return -ADAM_LR * ((mn / c1) / (jnp.sqrt(vn / c2) + ADAM_EPS) + ADAM_WD * w), mn, vn


def _adamw_small(red, sharded_grads, params):
    names = list(params)
    sharded = [k for k in names if k in sharded_grads]
    n, ns = len(names), len(sharded)

    def body(*refs):
        red_ref, sg_refs, p_refs, o_refs = refs[0], refs[1:1 + ns], refs[1 + ns:1 + ns + 3 * n], refs[1 + ns + 3 * n:]
        for i, k in enumerate(names):
            w_ref, m_ref, v_ref = p_refs[3 * i:3 * i + 3]
            if k in sharded:
                g = sg_refs[sharded.index(k)][...]
            else:
                r0, nr, l0, nl = SMALL_AT[k]
                g = red_ref[r0:r0 + nr, l0:l0 + nl]
            dl, mn, vn = _adamw_math(w_ref[...], g, m_ref[...], v_ref[...])
            for o_ref, val in zip(o_refs[4 * i:4 * i + 4], (g, dl, mn, vn)):
                o_ref[...] = val

    flat = [a for k in names for a in params[k]]
    res = pl.pallas_call(
        body, name="adamw_small",
        out_shape=[jax.ShapeDtypeStruct(params[k][0].shape, F32) for k in names for _ in range(4)],
    )(red, *[sharded_grads[k] for k in sharded], *flat)
    return {k: tuple(res[4 * i:4 * i + 4]) for i, k in enumerate(names)}


def _adamw_halves(w, own, sib, m, v, c_arr):
    rows, cols = w.shape
    tr = 128 if rows % 128 == 0 else rows

    def body(c_ref, w_ref, own_ref, sib_ref, m_ref, v_ref, g_ref, d_ref, mo_ref, vo_ref):
        first = c_ref[0] == 0
        own, sib = own_ref[...], sib_ref[...]
        g = jnp.concatenate([jnp.where(first, own, sib), jnp.where(first, sib, own)], axis=1)
        g_ref[...] = g
        d_ref[...], mo_ref[...], vo_ref[...] = _adamw_math(w_ref[...], g, m_ref[...], v_ref[...])

    full = pl.BlockSpec((tr, cols), lambda i, c: (i, 0))
    half = pl.BlockSpec((tr, HALF), lambda i, c: (i, 0))
    return pl.pallas_call(
        body, name="adamw_halves", out_shape=[jax.ShapeDtypeStruct(w.shape, F32)] * 4,
        grid_spec=pltpu.PrefetchScalarGridSpec(
            num_scalar_prefetch=1, grid=(rows // tr,), in_specs=[full, half, half, full, full], out_specs=[full] * 4),
        compiler_params=_params(("parallel",)),
    )(c_arr, w, own, sib, m, v)


def _ln_in_fwd(x, g, b, fuse=NO_FUSE):
    bsz, s, d = x.shape
    tp = s + HEAD
    nh = 2
    sh = s // nh
    rc = min(256, sh)

    def body(x_ref, g_ref, b_ref, s0_ref, s0b_ref):
        h = pl.program_id(1)
        gg, bb = g_ref[...], b_ref[...]

        def step(i, carry):
            src = pl.ds(pl.multiple_of(i * rc, rc), rc)
            dst = pl.ds(pl.multiple_of(HEAD + h * sh + i * rc, 64), rc)
            xh, _ = _ln_stats(x_ref[0, src, :])
            val = xh * gg + bb
            s0_ref[0, dst, :] = val
            s0b_ref[0, dst, :] = val.astype(BF16)
            return carry

        lax.fori_loop(0, sh // rc, step, 0)

    full = lambda bi, hi: (bi, 0, 0)
    (s0, s0b), got = _fused_call(
        body, fuse, name="ln_in_fwd",
        out_shape=[jax.ShapeDtypeStruct((bsz, tp, d), F32), jax.ShapeDtypeStruct((bsz, tp, d), BF16)],
        grid=(bsz, nh),
        in_specs=[pl.BlockSpec((1, sh, d), lambda bi, hi: (bi, hi, 0)),
                  pl.BlockSpec((1, d), lambda bi, hi: (0, 0)),
                  pl.BlockSpec((1, d), lambda bi, hi: (0, 0))],
        out_specs=[pl.BlockSpec((1, tp, d), full)] * 2, scratch_shapes=[], operands=(x, g, b),
        late_relay=True)
    return s0, s0b, got


def _ln_meta(s0, s0b, meta, g, b):
    bsz, tp, d = s0.shape

    def body(s0_in, s0b_in, meta_ref, g_ref, b_ref, s0_ref, s0b_ref):
        mh, _ = _ln_stats(meta_ref[...])
        mv = mh * g_ref[...] + b_ref[...]
        s0_ref[0, 0:PAD, :] = jnp.zeros((PAD, d), F32)
        s0b_ref[0, 0:PAD, :] = jnp.zeros((PAD, d), BF16)
        s0_ref[0, PAD:HEAD, :] = mv
        s0b_ref[0, PAD:HEAD, :] = mv.astype(BF16)

    head = pl.BlockSpec((1, HEAD, d), lambda bi: (bi, 0, 0))
    vec = pl.BlockSpec((1, d), lambda bi: (0, 0))
    return pl.pallas_call(
        body, name="ln_meta", out_shape=[jax.ShapeDtypeStruct(s0.shape, F32), jax.ShapeDtypeStruct(s0b.shape, BF16)],
        grid=(bsz,), in_specs=[head, head, pl.BlockSpec((N_META, d), lambda bi: (0, 0)), vec, vec],
        out_specs=[head, head], input_output_aliases={0: 0, 1: 1},
        compiler_params=_params(("parallel",)),
    )(s0, s0b, meta, g, b)


def _in_proj(s0b, w_int, fuse=NO_FUSE):
    r, d = s0b.shape
    tm = _row_tile(r)

    def body(a_ref, w_ref, o_ref):
        o_ref[...] = _nt(a_ref[...], w_ref[...])

    (u,), got = _fused_call(
        body, fuse, name="in_proj", out_shape=[jax.ShapeDtypeStruct((r, D_IN_PAD), F32)],
        grid=(r // tm,),
        in_specs=[pl.BlockSpec((tm, d), lambda i: (i, 0)), pl.BlockSpec((D_IN_PAD, d), lambda i: (0, 0))],
        out_specs=[pl.BlockSpec((tm, D_IN_PAD), lambda i: (i, 0))], scratch_shapes=[], operands=(s0b, w_int))
    return u, got


def _conv_fwd(u, conv_w, conv_b, fuse=NO_FUSE):
    bsz, tp, _ = u.shape
    nchunk = tp // CHUNK
    win = CHUNK + 32
    nct = D_CONV // 128

    def body(cv_ref, cg_ref, w_ref, cb_ref, hc_ref, h_scr, win_scr):
        h_scr[0:32, :] = jnp.zeros((32, 128), F32)
        h_scr[32:32 + tp, :] = cv_ref[0] * _sigmoid(cg_ref[0])
        cb = cb_ref[...]

        def step(n, carry):
            r0 = pl.multiple_of(n * CHUNK, CHUNK)
            win_scr[...] = h_scr[pl.ds(r0, win), :]
            acc = jnp.zeros((CHUNK, 128), F32)
            for j in range(CONV_WIDTH):
                acc = acc + w_ref[j:j + 1, :] * win_scr[2 + j:2 + j + CHUNK, :]
            hc_ref[0, pl.ds(r0, CHUNK), :] = acc + cb
            return carry

        lax.fori_loop(0, nchunk, step, 0)

    (hc,), got = _fused_call(
        body, fuse, name="conv_fwd", out_shape=[jax.ShapeDtypeStruct((bsz, tp, D_CONV), F32)],
        grid=(bsz, nct),
        in_specs=[pl.BlockSpec((1, tp, 128), lambda bi, t: (bi, 0, C_VAL // 128 + t)),
                  pl.BlockSpec((1, tp, 128), lambda bi, t: (bi, 0, C_GATE // 128 + t)),
                  pl.BlockSpec((32, 128), lambda bi, t: (0, t)),
                  pl.BlockSpec((1, 128), lambda bi, t: (0, t))],
        out_specs=[pl.BlockSpec((1, tp, 128), lambda bi, t: (bi, 0, t))],
        scratch_shapes=[pltpu.VMEM((tp + 32, 128), F32), pltpu.VMEM((win, 128), F32)],
        operands=(u, u, conv_w, conv_b))
    return hc, got


def _gla_group(nchunk):
    return 11 if nchunk % 11 == 0 else nchunk


def _bdot(a, b, ca, cb, precision=None):
    return lax.dot_general(a, b, (((ca,), (cb,)), ((0,), (0,))), preferred_element_type=F32, precision=precision)


def _bnn(a, b, **kw):
    return _bdot(a, b, 2, 1, **kw)


def _bnt(a, b, **kw):
    return _bdot(a, b, 2, 2, **kw)


def _gla_consts(nb):
    row = lax.broadcasted_iota(jnp.int32, (nb, CHUNK, CHUNK), 1)
    col = lax.broadcasted_iota(jnp.int32, (nb, CHUNK, CHUNK), 2)
    lane = lax.broadcasted_iota(jnp.int32, (1, 1, 128), 2)
    return row >= col, row <= col, [lane < GLA_DK, lane >= GLA_DK]


def _gla_group_terms(g, nb, q_ref, k_ref, gd_ref, gup_ref, gb_ref, tril):
    m = nb * CHUNK
    rows = pl.ds(pl.multiple_of(g * m, CHUNK), m)
    z = _nn(gd_ref[0, rows, :].astype(BF16), gup_ref[...]) + gb_ref[...]
    valid = g * m + lax.broadcasted_iota(jnp.int32, (m, 1), 0) >= PAD
    lg = jnp.where(valid, _log_sigmoid(z) * (1.0 / GLA_TAU), 0.0)
    bcum = _bnn(tril.astype(F32), lg.reshape(nb, CHUNK, 128), precision=lax.Precision.HIGHEST)
    blast = bcum[:, CHUNK - 1:CHUNK, :]
    eb = jnp.exp(bcum)
    enb = jnp.exp(-bcum)
    erest = jnp.exp(blast - bcum)
    q = (q_ref[0, rows, :] * Q_SCALE).reshape(nb, CHUNK, 128)
    k = k_ref[0, rows, :].reshape(nb, CHUNK, 128)
    return rows, valid, z, eb, enb, erest, jnp.exp(blast), q * eb, k * enb, k * erest


def _grid_ends(grid):
    ids = [pl.program_id(i) for i in range(len(grid))]
    first = functools.reduce(jnp.logical_and, [i == 0 for i in ids])
    last = functools.reduce(jnp.logical_and, [i == g - 1 for i, g in zip(ids, grid)])
    return first, last


def _gla_fwd(u, gup, gbias, gnorm, fuse=NO_FUSE):
    bsz, tp, _ = u.shape
    nchunk = tp // CHUNK
    nb = _gla_group(nchunk)

    def body(q_ref, k_ref, v_ref, r_ref, gd_ref, gup_ref, gb_ref, gn_ref, out_ref, o_ref, st_ref, s_scr):
        tril, _, hmask = _gla_consts(nb)
        s_scr[...] = jnp.zeros_like(s_scr)
        gn = gn_ref[...]

        def group(g, carry):
            rows, _, _, _, _, _, dec, qe, ke, kd = _gla_group_terms(g, nb, q_ref, k_ref, gd_ref, gup_ref, gb_ref, tril)
            keb, kdb = ke.astype(BF16), kd.astype(BF16)
            for h in range(2):
                cols = slice(h * GLA_DV, (h + 1) * GLA_DV)
                qh = jnp.where(hmask[h], qe, 0.0).astype(BF16)
                vh = v_ref[0, rows, cols].astype(BF16).reshape(nb, CHUNK, GLA_DV)
                a = jnp.where(tril, _bnt(qh, keb), 0.0).astype(BF16)
                st = s_scr[h]
                sts = []
                for n in range(nb):
                    st_ref[0, h, g * nb + n] = st
                    sts.append(st.astype(BF16))
                    st = dec[n] * st + _tn(vh[n], kdb[n])
                s_scr[h] = st
                o = (_bnn(a, vh) + _bnt(qh, jnp.stack(sts))).reshape(nb * CHUNK, GLA_DV)
                o_ref[0, rows, cols] = o
                rms = lax.rsqrt(jnp.mean(o * o, axis=-1, keepdims=True) + LN_EPS)
                rh = r_ref[0, rows, cols]
                out_ref[0, rows, cols] = (o * rms * gn * (rh * _sigmoid(rh))).astype(BF16)
            return carry

        lax.fori_loop(0, nchunk // nb, group, 0)

    res, got = _fused_call(
        body, fuse, name="gla_fwd",
        out_shape=[jax.ShapeDtypeStruct((bsz, tp, 512), BF16), jax.ShapeDtypeStruct((bsz, tp, 512), F32),
                   jax.ShapeDtypeStruct((bsz, GLA_HEADS, nchunk, GLA_DV, 128), F32)],
        grid=(bsz, 2),
        in_specs=[pl.BlockSpec((1, tp, 128), lambda bi, p: (bi, 0, C_Q // 128 + p)),
                  pl.BlockSpec((1, tp, 128), lambda bi, p: (bi, 0, C_K // 128 + p)),
                  pl.BlockSpec((1, tp, 256), lambda bi, p: (bi, 0, C_V // 256 + p)),
                  pl.BlockSpec((1, tp, 256), lambda bi, p: (bi, 0, C_R // 256 + p)),
                  pl.BlockSpec((1, tp, 128), lambda bi, p: (bi, 0, C_GD // 128)),
                  pl.BlockSpec((128, 128), lambda bi, p: (0, p)),
                  pl.BlockSpec((1, 128), lambda bi, p: (0, p)),
                  pl.BlockSpec((1, 128), lambda bi, p: (0, 0))],
        out_specs=[pl.BlockSpec((1, tp, 256), lambda bi, p: (bi, 0, p)),
                   pl.BlockSpec((1, tp, 256), lambda bi, p: (bi, 0, p)),
                   pl.BlockSpec((1, 2, nchunk, GLA_DV, 128), lambda bi, p: (bi, p, 0, 0, 0))],
        scratch_shapes=[pltpu.VMEM((2, GLA_DV, 128), F32)],
        operands=(u, u, u, u, u, gup, gbias, gnorm))
    return res[0], res[1], res[2], got


def _out_proj_ln1(hc, gla_out, w_out, s0, cg, cb, g1, b1, fuse=NO_FUSE):
    r, d = s0.shape
    tm = _row_tile(r)

    def body(hc_ref, a_ref, w_ref, s0_ref, cg_ref, cb_ref, g_ref, b_ref, co_ref, xh_ref, rstd_ref, s1b_ref):
        for rs in _sub_rows(tm):
            xc, _ = _ln_stats(hc_ref[rs, :])
            nv = xc * cg_ref[...] + cb_ref[...]
            co = (nv * _sigmoid(nv)).astype(BF16)
            co_ref[rs, :] = co
            mix = _nn(co, w_ref[0:D_CONV, :]) + _nn(a_ref[rs, :], w_ref[D_CONV:, :])
            xh, rstd = _ln_stats(ALPHA * s0_ref[rs, :] + mix)
            xh_ref[rs, :] = xh
            rstd_ref[rs, :] = rstd
            s1b_ref[rs, :] = (xh * g_ref[...] + b_ref[...]).astype(BF16)

    row = lambda n: pl.BlockSpec((tm, n), lambda i: (i, 0))
    vec = lambda n: pl.BlockSpec((1, n), lambda i: (0, 0))
    res, got = _fused_call(
        body, fuse, name="out_proj_ln1",
        out_shape=[jax.ShapeDtypeStruct((r, D_CONV), BF16), jax.ShapeDtypeStruct((r, d), F32),
                   jax.ShapeDtypeStruct((r, 1), F32), jax.ShapeDtypeStruct((r, d), BF16)],
        grid=(r // tm,),
        in_specs=[row(D_CONV), row(512), pl.BlockSpec((d, d), lambda i: (0, 0)), row(d),
                  vec(D_CONV), vec(D_CONV), vec(d), vec(d)],
        out_specs=[row(D_CONV), row(d), row(1), row(d)], scratch_shapes=[],
        operands=(hc, gla_out, w_out, s0, cg, cb, g1, b1))
    return res[0], res[1], res[2], res[3], got


def _ffn1(s1b, w1, fuse=NO_FUSE):
    r, d = s1b.shape
    tm = _row_tile(r)
    ns, _, wn = w1.shape

    def body(a_ref, w_ref, o_ref):
        a = a_ref[...]
        for j in range(ns):
            o_ref[:, j * wn:(j + 1) * wn] = jnp.maximum(_nn(a, w_ref[j]), 0.0).astype(BF16)

    (ra,), got = _fused_call(
        body, fuse, name="ffn1", out_shape=[jax.ShapeDtypeStruct((r, D_FF), BF16)], grid=(r // tm,),
        in_specs=[pl.BlockSpec((tm, d), lambda i: (i, 0)), pl.BlockSpec(w1.shape, lambda i: (0, 0, 0))],
        out_specs=[pl.BlockSpec((tm, D_FF), lambda i: (i, 0))], scratch_shapes=[], operands=(s1b, w1))
    return ra, got


def _ffn2_ln2_loss(ra, w2, xhat1, g1, b1, g2, b2, tgt, tp):
    r, d = xhat1.shape
    tm = _row_tile(tp)
    per = tp // tm

    def body(ra_ref, w_ref, xh1_ref, g1_ref, b1_ref, g2_ref, b2_ref, tgt_ref, dt_ref, dtb_ref, acc_ref, t_ref, sem):
        i = pl.program_id(0)
        b, j = i // per, i % per

        @pl.when(i == 0)
        def _():
            acc_ref[...] = jnp.zeros_like(acc_ref)

        head_copy = pltpu.make_async_copy(tgt_ref.at[b, pl.ds(0, tm - HEAD), :], t_ref.at[pl.ds(HEAD, tm - HEAD), :], sem)
        body_copy = pltpu.make_async_copy(
            tgt_ref.at[b, pl.ds(pl.multiple_of(jnp.maximum(j * tm - HEAD, 0), 64), tm), :], t_ref, sem)

        @pl.when(j == 0)
        def _():
            t_ref[0:HEAD, :] = jnp.zeros((HEAD, d), F32)
            head_copy.start()

        pl.when(j > 0)(body_copy.start)

        sums = [jnp.zeros((1, d), F32)] * 3
        for rs in _sub_rows(tm):
            rb = ra_ref[rs, :]
            f = _nn(rb * rb, w_ref[...])
            if rs.start == 0:
                pl.when(j == 0)(head_copy.wait)
                pl.when(j > 0)(body_copy.wait)
            s1 = xh1_ref[rs, :] * g1_ref[...] + b1_ref[...]
            xh2, rstd2 = _ln_stats(ALPHA * s1 + f)
            y = xh2 * g2_ref[...] + b2_ref[...]
            rowid = (i % per) * tm + rs.start + lax.broadcasted_iota(jnp.int32, (rs.stop - rs.start, 1), 0)
            e = jnp.where(rowid >= HEAD, y - t_ref[rs, :], 0.0)
            dy = e * (1.0 / d)
            dt2 = _ln_bwd(dy * g2_ref[...], xh2, rstd2)
            dt_ref[rs, :] = dt2
            dtb_ref[rs, :] = dt2.astype(BF16)
            sums = [sums[0] + (0.5 / d) * jnp.sum(e * e, axis=0, keepdims=True),
                    sums[1] + jnp.sum(dy * xh2, axis=0, keepdims=True), sums[2] + jnp.sum(dy, axis=0, keepdims=True)]
        for k in range(3):
            acc_ref[k:k + 1, :] += sums[k]

    row = lambda n: pl.BlockSpec((tm, n), lambda i: (i, 0))
    vec = pl.BlockSpec((1, d), lambda i: (0, 0))
    return pl.pallas_call(
        body, name="ffn2_ln2_loss",
        out_shape=[jax.ShapeDtypeStruct((r, d), F32), jax.ShapeDtypeStruct((r, d), BF16),
                   jax.ShapeDtypeStruct((8, d), F32)],
        grid=(r // tm,),
        in_specs=[row(D_FF), pl.BlockSpec((D_FF, d), lambda i: (0, 0)), row(d), vec, vec, vec, vec, ANY],
        out_specs=[row(d), row(d), pl.BlockSpec((8, d), lambda i: (0, 0))],
        scratch_shapes=[pltpu.VMEM((tm, d), F32), pltpu.SemaphoreType.DMA],
        compiler_params=_params(("arbitrary",)),
    )(ra, w2, xhat1, g1, b1, g2, b2, tgt)


def _ffn_bwd_da(dt2b, w2, ra):
    r, d = dt2b.shape
    tm = _row_tile(r)

    def body(g_ref, w_ref, ra_ref, o_ref):
        o_ref[...] = (_nt(g_ref[...], w_ref[...]) * (2.0 * ra_ref[...].astype(F32))).astype(BF16)

    return pl.pallas_call(
        body, name="ffn_bwd_da", out_shape=jax.ShapeDtypeStruct((r, D_FF), BF16),
        grid=(r // tm,),
        in_specs=[pl.BlockSpec((tm, d), lambda i: (i, 0)), pl.BlockSpec((D_FF, d), lambda i: (0, 0)),
                  pl.BlockSpec((tm, D_FF), lambda i: (i, 0))],
        out_specs=pl.BlockSpec((tm, D_FF), lambda i: (i, 0)),
        compiler_params=_params(("parallel",)),
    )(dt2b, w2, ra)


def _ffn_bwd_ln1(da, w1, dt2, xhat1, rstd1, g1):
    r, d = dt2.shape
    tm = _row_tile(r)

    def body(da_ref, w_ref, dt2_ref, xh_ref, rstd_ref, g_ref, dt_ref, dtb_ref, acc_ref):
        @pl.when(pl.program_id(0) == 0)
        def _():
            acc_ref[...] = jnp.zeros_like(acc_ref)

        sums = [jnp.zeros((1, d), F32)] * 2
        for rs in _sub_rows(tm):
            ds1 = ALPHA * dt2_ref[rs, :]
            for j in range(w1.shape[0]):
                ds1 = ds1 + _nt(da_ref[rs, j * w1.shape[2]:(j + 1) * w1.shape[2]], w_ref[j])
            xh = xh_ref[rs, :]
            dt1 = _ln_bwd(ds1 * g_ref[...], xh, rstd_ref[rs, :])
            dt_ref[rs, :] = dt1
            dtb_ref[rs, :] = dt1.astype(BF16)
            sums = [sums[0] + jnp.sum(ds1 * xh, axis=0, keepdims=True), sums[1] + jnp.sum(ds1, axis=0, keepdims=True)]
        for k in range(2):
            acc_ref[k:k + 1, :] += sums[k]

    row = lambda n: pl.BlockSpec((tm, n), lambda i: (i, 0))
    return pl.pallas_call(
        body, name="ffn_bwd_ln1",
        out_shape=[jax.ShapeDtypeStruct((r, d), F32), jax.ShapeDtypeStruct((r, d), BF16),
                   jax.ShapeDtypeStruct((8, d), F32)],
        grid=(r // tm,),
        in_specs=[row(D_FF), pl.BlockSpec(w1.shape, lambda i: (0, 0, 0)), row(d), row(d), row(1),
                  pl.BlockSpec((1, d), lambda i: (0, 0))],
        out_specs=[row(d), row(d), pl.BlockSpec((8, d), lambda i: (0, 0))],
        compiler_params=_params(("arbitrary",)),
    )(da, w1, dt2, xhat1, rstd1, g1)


def _matmul_tn(lhs, rhs, bm, square_lhs=False, name="matmul_tn", fuse=NO_FUSE):
    r, m = lhs.shape
    n = rhs.shape[1]
    tk = _reduce_tile(r, True)

    def body(a_ref, b_ref, o_ref):
        @pl.when(pl.program_id(1) == 0)
        def _():
            o_ref[...] = jnp.zeros_like(o_ref)

        a = a_ref[...]
        if square_lhs:
            a = a * a
        o_ref[...] += _tn(a, b_ref[...])

    (out,), got = _fused_call(
        body, fuse, name=name, out_shape=[jax.ShapeDtypeStruct((m, n), F32)], grid=(m // bm, r // tk),
        in_specs=[pl.BlockSpec((tk, bm), lambda i, k: (k, i)), pl.BlockSpec((tk, n), lambda i, k: (k, 0))],
        out_specs=[pl.BlockSpec((bm, n), lambda i, k: (i, 0))], scratch_shapes=[], operands=(lhs, rhs))
    return out, got


def _grad_w_ff1(s1b, da, fuse=NO_FUSE):
    r, d = s1b.shape
    wn = da.shape[1] // 4
    tk = _reduce_tile(r, True)

    def body(a_ref, b_ref, o_ref):
        @pl.when(pl.program_id(1) == 0)
        def _():
            o_ref[...] = jnp.zeros_like(o_ref)

        o_ref[0] += _tn(a_ref[...], b_ref[...])

    (out,), got = _fused_call(
        body, fuse, name="grad_w_ff1", out_shape=[jax.ShapeDtypeStruct((4, d, wn), F32)], grid=(4, r // tk),
        in_specs=[pl.BlockSpec((tk, d), lambda j, k: (k, 0)), pl.BlockSpec((tk, wn), lambda j, k: (k, j))],
        out_specs=[pl.BlockSpec((1, d, wn), lambda j, k: (j, 0, 0))], scratch_shapes=[], operands=(s1b, da))
    return out, got


def _grad_w_out(conv_of, gla_of, dt1b, fuse=NO_FUSE):
    r, n = dt1b.shape
    tk = _reduce_tile(r, True)

    def body(a_ref, b_ref, g_ref, o_ref):
        @pl.when(pl.program_id(1) == 0)
        def _():
            o_ref[...] = jnp.zeros_like(o_ref)

        @pl.when(pl.program_id(0) == 0)
        def _():
            o_ref[...] += _tn(a_ref[...], g_ref[...])

        @pl.when(pl.program_id(0) == 1)
        def _():
            o_ref[...] += _tn(b_ref[...], g_ref[...])

    lhs = pl.BlockSpec((tk, 512), lambda i, k: (k, 0))
    (out,), got = _fused_call(
        body, fuse, name="grad_w_out", out_shape=[jax.ShapeDtypeStruct((2 * 512, n), F32)], grid=(2, r // tk),
        in_specs=[lhs, lhs, pl.BlockSpec((tk, n), lambda i, k: (k, 0))],
        out_specs=[pl.BlockSpec((512, n), lambda i, k: (i, 0))], scratch_shapes=[],
        operands=(conv_of, gla_of, dt1b))
    return out, got


def _out_proj_bwd(dt1b, w_out, hc, cg, cb, fuse=NO_FUSE):
    r, d = dt1b.shape
    tm = _row_tile(r)

    def body(g_ref, w_ref, hc_ref, cg_ref, cb_ref, dhc_ref, dgla_ref, acc_ref):
        @pl.when(pl.program_id(0) == 0)
        def _():
            acc_ref[...] = jnp.zeros_like(acc_ref)

        gg = cg_ref[...]
        sums = [jnp.zeros((1, D_CONV), F32)] * 3
        for rs in _sub_rows(tm):
            dmix = _nt(g_ref[rs, :], w_ref[...])
            dgla_ref[rs, :] = dmix[:, D_CONV:]
            xh, rstd = _ln_stats(hc_ref[rs, :])
            nv = xh * gg + cb_ref[...]
            sig = _sigmoid(nv)
            dn = dmix[:, :D_CONV] * (sig * (1.0 + nv * (1.0 - sig)))
            dhc = _ln_bwd(dn * gg, xh, rstd)
            dhc_ref[rs, :] = dhc
            sums = [sums[0] + jnp.sum(dhc, axis=0, keepdims=True), sums[1] + jnp.sum(dn * xh, axis=0, keepdims=True),
                    sums[2] + jnp.sum(dn, axis=0, keepdims=True)]
        for k in range(3):
            acc_ref[k:k + 1, :] += sums[k]

    row = lambda n: pl.BlockSpec((tm, n), lambda i: (i, 0))
    vec = pl.BlockSpec((1, D_CONV), lambda i: (0, 0))
    res, got = _fused_call(
        body, fuse, name="out_proj_bwd",
        out_shape=[jax.ShapeDtypeStruct((r, D_CONV), F32), jax.ShapeDtypeStruct((r, 512), F32),
                   jax.ShapeDtypeStruct((8, D_CONV), F32)],
        grid=(r // tm,),
        in_specs=[row(d), pl.BlockSpec((d, d), lambda i: (0, 0)), row(D_CONV), vec, vec],
        out_specs=[row(D_CONV), row(512), pl.BlockSpec((8, D_CONV), lambda i: (0, 0))], scratch_shapes=[],
        operands=(dt1b, w_out, hc, cg, cb))
    return res[0], res[1], res[2], got


def _conv_bwd(dhc, u, conv_w, fuse=NO_FUSE):
    bsz, tp, _ = u.shape
    nchunk = tp // CHUNK
    win = CHUNK + 32
    nct = D_CONV // 128

    def body(dhc_ref, cv_ref, cg_ref, w_ref, dv_ref, dg_ref, dw_ref, h_scr, dhc_scr, hwin, dwin, dw_scr):
        h_scr[0:32, :] = jnp.zeros((32, 128), F32)
        h_scr[32:32 + tp, :] = cv_ref[0] * _sigmoid(cg_ref[0])
        dhc_scr[0:tp, :] = dhc_ref[0]
        dhc_scr[tp:tp + 32, :] = jnp.zeros((32, 128), F32)
        dw_scr[...] = jnp.zeros_like(dw_scr)

        def step(n, carry):
            r0 = pl.multiple_of(n * CHUNK, CHUNK)
            rows = pl.ds(r0, CHUNK)
            hwin[...] = h_scr[pl.ds(r0, win), :]
            dwin[...] = dhc_scr[pl.ds(r0, win), :]
            dcur = dwin[0:CHUNK, :]
            acc = jnp.zeros((CHUNK, 128), F32)
            for j in range(CONV_WIDTH):
                acc = acc + w_ref[j:j + 1, :] * dwin[30 - j:30 - j + CHUNK, :]
                prod = dcur * hwin[2 + j:2 + j + CHUNK, :]
                dw_scr[j * 8:(j + 1) * 8, :] += jnp.sum(prod.reshape(CHUNK // 8, 8, 128), axis=0)
            cg = cg_ref[0, rows, :]
            sig = _sigmoid(cg)
            rowid = n * CHUNK + lax.broadcasted_iota(jnp.int32, (CHUNK, 1), 0)
            dh = jnp.where(rowid >= PAD, acc, 0.0)
            dv_ref[0, rows, :] = (dh * sig).astype(BF16)
            dg_ref[0, rows, :] = (dh * cv_ref[0, rows, :] * sig * (1.0 - sig)).astype(BF16)
            return carry

        lax.fori_loop(0, nchunk, step, 0)
        dw_ref[0] = jnp.zeros((32, 128), F32)
        for j in range(CONV_WIDTH):
            dw_ref[0, j:j + 1, :] = jnp.sum(dw_scr[j * 8:(j + 1) * 8, :], axis=0, keepdims=True)

    blk = lambda off: pl.BlockSpec((1, tp, 128), lambda bi, t: (bi, 0, off // 128 + t))
    res, got = _fused_call(
        body, fuse, name="conv_bwd",
        out_shape=[jax.ShapeDtypeStruct((bsz, tp, D_CONV), BF16), jax.ShapeDtypeStruct((bsz, tp, D_CONV), BF16),
                   jax.ShapeDtypeStruct((bsz, 32, D_CONV), F32)],
        grid=(bsz, nct),
        in_specs=[blk(0), blk(C_VAL), blk(C_GATE), pl.BlockSpec((32, 128), lambda bi, t: (0, t))],
        out_specs=[blk(0), blk(0), pl.BlockSpec((1, 32, 128), lambda bi, t: (bi, 0, t))],
        scratch_shapes=[pltpu.VMEM((tp + 32, 128), F32), pltpu.VMEM((tp + 32, 128), F32),
                        pltpu.VMEM((win, 128), F32), pltpu.VMEM((win, 128), F32),
                        pltpu.VMEM((CONV_WIDTH * 8, 128), F32)],
        operands=(dhc, u, u, conv_w))
    return res[0], res[1], res[2], got


def _gla_bwd(dgla, u, o_pre, states, gup, gbias, gnorm, fuse=NO_FUSE):
    bsz, tp, _ = u.shape
    nchunk = tp // CHUNK
    nb = _gla_group(nchunk)

    def body(dy_ref, q_ref, k_ref, v_ref, r_ref, gd_ref, o_ref, st_ref, gup_ref, gb_ref, gn_ref,
             dq_ref, dk_ref, dv_ref, dr_ref, dgd_ref, dgup_ref, vec_ref, h_scr, gup_acc):
        tril, triu, hmask = _gla_consts(nb)
        h_scr[...] = jnp.zeros_like(h_scr)
        gup_acc[...] = jnp.zeros_like(gup_acc)
        gn = gn_ref[...]
        gupb = gup_ref[...]
        m = nb * CHUNK
        ngroup = nchunk // nb

        def group(i, carry):
            dbias, dgn = carry
            g = ngroup - 1 - i
            rows, valid, z, eb, enb, erest, dec, qe, ke, kd = _gla_group_terms(
                g, nb, q_ref, k_ref, gd_ref, gup_ref, gb_ref, tril)
            keb, kdb = ke.astype(BF16), kd.astype(BF16)
            dqe = jnp.zeros((nb, CHUNK, 128), F32)
            dke = jnp.zeros((nb, CHUNK, 128), F32)
            dkd = jnp.zeros((nb, CHUNK, 128), F32)
            ddec = jnp.zeros((nb, 1, 128), F32)
            for h in range(2):
                cols = slice(h * GLA_DV, (h + 1) * GLA_DV)
                o = o_ref[0, rows, cols]
                rh = r_ref[0, rows, cols]
                dy = dy_ref[0, rows, cols]
                rms = lax.rsqrt(jnp.mean(o * o, axis=-1, keepdims=True) + LN_EPS)
                nrm = o * rms
                sig = _sigmoid(rh)
                sw = rh * sig
                dr_ref[0, rows, cols] = (dy * nrm * gn * (sig * (1.0 + rh * (1.0 - sig)))).astype(BF16)
                dgn = dgn + jnp.sum(dy * nrm * sw, axis=0, keepdims=True)
                dn = dy * gn * sw
                do = rms * (dn - nrm * jnp.mean(dn * nrm, axis=-1, keepdims=True))
                dob = do.astype(BF16).reshape(nb, CHUNK, GLA_DV)
                qh = jnp.where(hmask[h], qe, 0.0).astype(BF16)
                vh = v_ref[0, rows, cols].astype(BF16).reshape(nb, CHUNK, GLA_DV)
                ht = h_scr[h]
                hts = [None] * nb
                for n in reversed(range(nb)):
                    hts[n] = ht
                    ht = dec[n] * ht + _tn(dob[n], qh[n])
                h_scr[h] = ht
                htf = jnp.stack(hts)
                htb = htf.astype(BF16)
                st = st_ref[0, h, pl.ds(g * nb, nb)]
                at = jnp.where(triu, _bnt(keb, qh), 0.0).astype(BF16)
                da = jnp.where(tril, _bnt(dob, vh), 0.0).astype(BF16)
                dat = jnp.where(triu, _bnt(vh, dob), 0.0).astype(BF16)
                dqe = dqe + jnp.where(hmask[h], _bnn(da, keb) + _bnn(dob, st.astype(BF16)), 0.0)
                dke = dke + _bnn(dat, qh)
                dv_ref[0, rows, cols] = (_bnn(at, dob) + _bnt(kdb, htb)).reshape(m, GLA_DV).astype(BF16)
                dkd = dkd + jnp.where(hmask[h], _bnn(vh, htb), 0.0)
                ddec = ddec + jnp.where(hmask[h], jnp.sum(htf * st, axis=1, keepdims=True), 0.0)
            dq_ref[0, rows, :] = (dqe * eb * Q_SCALE).reshape(m, 128).astype(BF16)
            dk_ref[0, rows, :] = (dke * enb + dkd * erest).reshape(m, 128).astype(BF16)
            db = dqe * qe - dke * ke - dkd * kd
            dblast = jnp.sum(dkd * kd, axis=1, keepdims=True) + ddec * dec
            lastrow = lax.broadcasted_iota(jnp.int32, (1, CHUNK, 1), 1) == CHUNK - 1
            db = db + jnp.where(lastrow, dblast, 0.0)
            dlg = _bnn(triu.astype(F32), db, precision=lax.Precision.HIGHEST).reshape(m, 128)
            dz = jnp.where(valid, dlg, 0.0) * (1.0 / GLA_TAU) * (1.0 - _sigmoid(z))
            dzb = dz.astype(BF16)
            dgd_ref[0, 0, rows, :] = _nt(dzb, gupb).astype(BF16)
            gup_acc[...] += _tn(gd_ref[0, rows, :].astype(BF16), dzb)
            return dbias + jnp.sum(dz, axis=0, keepdims=True), dgn

        zero = jnp.zeros((1, 128), F32)
        dbias, dgn = lax.fori_loop(0, ngroup, group, (zero, zero))
        dgup_ref[0] = gup_acc[...]
        vec_ref[0] = jnp.zeros((8, 128), F32)
        vec_ref[0, 0:1, :] = dbias
        vec_ref[0, 1:2, :] = dgn

    pair = lambda w, off: pl.BlockSpec((1, tp, w), lambda bi, p: (bi, 0, off // w + p))
    return _fused_call(
        body, fuse, name="gla_bwd",
        out_shape=[jax.ShapeDtypeStruct((bsz, tp, 256), BF16), jax.ShapeDtypeStruct((bsz, tp, 256), BF16),
                   jax.ShapeDtypeStruct((bsz, tp, 512), BF16), jax.ShapeDtypeStruct((bsz, tp, 512), BF16),
                   jax.ShapeDtypeStruct((bsz, 2, tp, 128), BF16), jax.ShapeDtypeStruct((bsz, 128, 256), F32),
                   jax.ShapeDtypeStruct((bsz, 8, 256), F32)],
        grid=(bsz, 2),
        in_specs=[pair(256, 0), pair(128, C_Q), pair(128, C_K), pair(256, C_V), pair(256, C_R),
                  pl.BlockSpec((1, tp, 128), lambda bi, p: (bi, 0, C_GD // 128)),
                  pair(256, 0),
                  pl.BlockSpec((1, 2, nchunk, GLA_DV, 128), lambda bi, p: (bi, p, 0, 0, 0)),
                  pl.BlockSpec((128, 128), lambda bi, p: (0, p)),
                  pl.BlockSpec((1, 128), lambda bi, p: (0, p)),
                  pl.BlockSpec((1, 128), lambda bi, p: (0, 0))],
        out_specs=[pair(128, 0), pair(128, 0), pair(256, 0), pair(256, 0),
                   pl.BlockSpec((1, 1, tp, 128), lambda bi, p: (bi, p, 0, 0)),
                   pl.BlockSpec((1, 128, 128), lambda bi, p: (bi, 0, p)),
                   pl.BlockSpec((1, 8, 128), lambda bi, p: (bi, 0, p))],
        scratch_shapes=[pltpu.VMEM((2, GLA_DV, 128), F32), pltpu.VMEM((128, 128), F32)],
        operands=(dgla, u, u, u, u, u, o_pre, states, gup, gbias, gnorm))


_DU_OFFSETS = (C_VAL, C_GATE, C_Q, C_K, C_V, C_R)
_DU_WIDTHS = (512, 512, 256, 256, 512, 512)


def _du_specs(tm, per, row_map):
    specs = [pl.BlockSpec((tm, w), row_map) for w in _DU_WIDTHS]
    for p in range(2):
        specs.append(pl.BlockSpec((1, 1, tm, 128), lambda *ix, p=p: (row_map(*ix)[0] // per, p, row_map(*ix)[0] % per, 0)))
    return specs


def _du_pieces(refs):
    out = [(off, ref[...]) for off, ref in zip(_DU_OFFSETS, refs[:6])]
    dgd = (refs[6][0, 0].astype(F32) + refs[7][0, 0].astype(F32)).astype(BF16)
    out.append((C_GD, dgd))
    return out


def _in_proj_bwd(pieces, dgd, w_int, dt1, tp, fuse=NO_FUSE):
    r, d = dt1.shape
    tm = _row_tile(tp)
    per = tp // tm

    def body(*refs):
        w_ref, dt_ref, o_ref = refs[8:]
        acc = ALPHA * dt_ref[...]
        for off, val in _du_pieces(refs[:8]):
            acc = acc + _nn(val, w_ref[off:off + val.shape[1], :])
        o_ref[...] = acc

    row = lambda i: (i, 0)
    (ds0,), got = _fused_call(
        body, fuse, name="in_proj_bwd", out_shape=[jax.ShapeDtypeStruct((r, d), F32)], grid=(r // tm,),
        in_specs=_du_specs(tm, per, row) + [pl.BlockSpec((D_IN_PAD, d), lambda i: (0, 0)), pl.BlockSpec((tm, d), row)],
        out_specs=[pl.BlockSpec((tm, d), row)], scratch_shapes=[], operands=(*pieces, dgd, dgd, w_int, dt1))
    return ds0, got


def _grad_w_in(pieces, dgd, s0b, tp, fuse=NO_FUSE):
    r, d = s0b.shape
    tk = _reduce_tile(tp, False)
    per = tp // tk

    def body(*refs):
        s_ref, o_ref = refs[8:]

        @pl.when(pl.program_id(0) == 0)
        def _():
            o_ref[...] = jnp.zeros_like(o_ref)

        s = s_ref[...]
        for off, val in _du_pieces(refs[:8]):
            o_ref[off:off + val.shape[1], :] += _tn(val, s)

    row = lambda k: (k, 0)
    (out,), got = _fused_call(
        body, fuse, name="grad_w_in", out_shape=[jax.ShapeDtypeStruct((D_IN_PAD, d), F32)], grid=(r // tk,),
        in_specs=_du_specs(tk, per, row) + [pl.BlockSpec((tk, d), row)],
        out_specs=[pl.BlockSpec((D_IN_PAD, d), lambda k: (0, 0))], scratch_shapes=[],
        operands=(*pieces, dgd, dgd, s0b))
    return out, got


def _ln_in_bwd(ds0, x, meta, g):
    bsz, s, d = x.shape
    tp = s + HEAD
    nh = 2
    sh = s // nh
    rc = min(256, sh)

    def body(ds_ref, x_ref, meta_ref, g_ref, gx_ref, dm_ref, vec_ref):
        h = pl.program_id(1)
        gg = g_ref[...]

        @pl.when(h == 0)
        def _():
            mh, mr = _ln_stats(meta_ref[...])
            dsm = ds_ref[0, PAD:HEAD, :]
            dm_ref[0] = _ln_bwd(dsm * gg, mh, mr)
            vec_ref[0] = jnp.zeros((8, d), F32)
            vec_ref[0, 0:1, :] = jnp.sum(dsm * mh, axis=0, keepdims=True)
            vec_ref[0, 1:2, :] = jnp.sum(dsm, axis=0, keepdims=True)

        def step(i, carry):
            sg, sb = carry
            dst = pl.ds(pl.multiple_of(i * rc, rc), rc)
            src = pl.ds(pl.multiple_of(HEAD + h * sh + i * rc, 64), rc)
            xh, rstd = _ln_stats(x_ref[0, dst, :])
            dsv = ds_ref[0, src, :]
            gx_ref[0, dst, :] = _ln_bwd(dsv * gg, xh, rstd)
            return sg + jnp.sum(dsv * xh, axis=0, keepdims=True), sb + jnp.sum(dsv, axis=0, keepdims=True)

        zero = jnp.zeros((1, d), F32)
        sg, sb = lax.fori_loop(0, sh // rc, step, (zero, zero))
        vec_ref[0, 0:1, :] += sg
        vec_ref[0, 1:2, :] += sb

    return pl.pallas_call(
        body, name="ln_in_bwd",
        out_shape=[jax.ShapeDtypeStruct((bsz, s, d), F32), jax.ShapeDtypeStruct((bsz, N_META, d), F32),
                   jax.ShapeDtypeStruct((bsz, 8, d), F32)],
        grid=(bsz, nh),
        in_specs=[pl.BlockSpec((1, tp, d), lambda bi, hi: (bi, 0, 0)),
                  pl.BlockSpec((1, sh, d), lambda bi, hi: (bi, hi, 0)),
                  pl.BlockSpec((N_META, d), lambda bi, hi: (0, 0)),
                  pl.BlockSpec((1, d), lambda bi, hi: (0, 0))],
        out_specs=[pl.BlockSpec((1, sh, d), lambda bi, hi: (bi, hi, 0)),
                   pl.BlockSpec((1, N_META, d), lambda bi, hi: (bi, 0, 0)),
                   pl.BlockSpec((1, 8, d), lambda bi, hi: (bi, 0, 0))],
        compiler_params=_params(("parallel", "arbitrary")),
    )(ds0, x, meta, g)


def _rows128(a):
    return a.reshape(-1, 128)


def kernel(x, meta_tokens, ln_in_g, ln_in_b, w_in, conv_w, conv_b, conv_ln_g, conv_ln_b, gate_up, gate_bias, gla_norm_g, w_out, ln1_g, ln1_b, w_ff1, w_ff2, ln2_g, ln2_b, loss_target, m_meta_tokens, m_ln_in_g, m_ln_in_b, m_w_in, m_conv_w, m_conv_b, m_conv_ln_g, m_conv_ln_b, m_gate_up, m_gate_bias, m_gla_norm_g, m_w_out, m_ln1_g, m_ln1_b, m_w_ff1, m_w_ff2, m_ln2_g, m_ln2_b, v_meta_tokens, v_ln_in_g, v_ln_in_b, v_w_in, v_conv_w, v_conv_b, v_conv_ln_g, v_conv_ln_b, v_gate_up, v_gate_bias, v_gla_norm_g, v_w_out, v_ln1_g, v_ln1_b, v_w_ff1, v_w_ff2, v_ln2_g, v_ln2_b):
    bsz, seq, d = x.shape
    tp = seq + HEAD
    r = bsz * tp
    xi, yi, ci = _mesh_pos()
    chip = 2 * xi + yi
    c_arr = jnp.reshape(ci, (1,)).astype(jnp.int32)

    sh_in = D_IN // 4
    shard_in = jnp.pad(w_in[0].T.astype(BF16), ((0, D_IN_PAD // 4 - sh_in), (0, 0)))
    shard_w1, shard_wout, shard_w2 = w_ff1[0].astype(BF16), w_out[0].astype(BF16), w_ff2[0].astype(BF16)
    small_w = jnp.concatenate([_rows128(meta_tokens), _rows128(conv_w[0]), _rows128(gate_up[0])], axis=0)
    ln_in_g2, ln_in_b2 = ln_in_g.reshape(1, d), ln_in_b.reshape(1, d)

    s0, s0b, ((g_int,), (g_small,)) = _ln_in_fwd(
        x, ln_in_g2, ln_in_b2, [("gather", [(shard_in, 0, D_IN_PAD // 4, None)]), ("small", [small_w])])
    (g_int,) = _place_own([g_int], [shard_in])
    (g_small,) = _place_own([g_small], [small_w])
    n_meta_rows, n_cw_rows = N_META * 256 // 128, CONV_WIDTH * 128 // 128
    meta_full = jnp.concatenate([g_small[j, :n_meta_rows].reshape(N_META, 256) for j in range(4)], axis=1)
    convw_full = jnp.concatenate(
        [g_small[j, n_meta_rows:n_meta_rows + n_cw_rows].reshape(CONV_WIDTH, 128) for j in range(4)], axis=1)
    gup_full = jnp.concatenate(
        [g_small[j, n_meta_rows + n_cw_rows:].reshape(GLA_RANK, 64) for j in range(4)], axis=1)
    convw_p = jnp.pad(convw_full, ((0, 1), (0, 0)))
    gup_p = jnp.pad(gup_full, ((0, 128 - GLA_RANK), (0, 0))).astype(BF16)
    s0, s0b = _ln_meta(s0, s0b, meta_full, ln_in_g2, ln_in_b2)
    w_int = jnp.pad(g_int[:, :sh_in].reshape(D_IN, d), ((0, D_IN_PAD - D_IN), (0, 0)))
    s0f, s0bf = s0.reshape(r, d), s0b.reshape(r, d)
    u, ((w1_buf,),) = _in_proj(s0bf, w_int, [("gather", [(shard_w1, 0, 640, None)])])
    (w1_buf,) = _place_own([w1_buf], [shard_w1])
    u3 = u.reshape(bsz, tp, D_IN_PAD)
    hc, ((w1_buf,),) = _conv_fwd(u3, convw_p, conv_b, [("gather", [(shard_w1, 640, 384, w1_buf)])])
    hc = hc.reshape(r, D_CONV)
    gla_out, o_pre, states, ((g_wout,),) = _gla_fwd(
        u3, gup_p, gate_bias, gla_norm_g, [("gather", [(shard_wout, 0, 256, None)])])
    (g_wout,) = _place_own([g_wout], [shard_wout])
    wout = g_wout.reshape(d, d)
    gla_of = gla_out.reshape(r, 512)
    conv_of, xhat1, rstd1, s1b, ((w2_buf,),) = _out_proj_ln1(
        hc, gla_of, wout, s0f, conv_ln_g, conv_ln_b, ln1_g, ln1_b, [("gather", [(shard_w2, 0, 448, None)])])
    (w2_buf,) = _place_own([w2_buf], [shard_w2])
    w1 = w1_buf
    ra, ((w2_buf,),) = _ffn1(s1b, w1, [("gather", [(shard_w2, 448, 576, w2_buf)])])
    w2 = w2_buf.reshape(D_FF, d)
    dt2, dt2b, acc2 = _ffn2_ln2_loss(ra, w2, xhat1, ln1_g, ln1_b, ln2_g, ln2_b, loss_target, tp)

    def add_pair(g, got, splits=None):
        return _add_pair(g, got, c_arr, splits)

    da = _ffn_bwd_da(dt2b, w2, ra)
    dt1, dt1b, acc1 = _ffn_bwd_ln1(da, w1, dt2, xhat1, rstd1, ln1_g)
    g_w2, _ = _matmul_tn(ra, dt2b, 1024, square_lhs=True, name="grad_w_ff2")
    big_w2 = g_w2.reshape(4, D_FF // 4, d)
    big_w1, ((pair_w2,),) = _grad_w_ff1(s1b, da, [("pair", [big_w2])])
    dhc, dgla, cacc, ((pair_w1,),) = _out_proj_bwd(dt1b, wout, hc, conv_ln_g, conv_ln_b, [("pair", [big_w1])])
    (part_w1,), (part_w2a, part_w2b) = add_pair(big_w1, pair_w1), add_pair(big_w2, pair_w2, [512, 512])
    g_wout, _ = _grad_w_out(conv_of, gla_of, dt1b)
    big_wout = g_wout.reshape(4, d // 4, d)
    dcv, dcg, dcw, ((chip_w1,), (pair_wout,)) = _conv_bwd(
        dhc.reshape(bsz, tp, D_CONV), u3, convw_p, [("exchange", [part_w1]), ("pair", [big_wout])])
    (part_wout,) = add_pair(big_wout, pair_wout)
    (dq, dk, dv, dr, dgd, dgup, gvec), ((chip_w2a,),) = _gla_bwd(
        dgla.reshape(bsz, tp, 512), u3, o_pre, states, gup_p, gate_bias, gla_norm_g, [("exchange", [part_w2a])])
    pieces = [a.reshape(r, a.shape[-1]) for a in (dcv, dcg, dq, dk, dv, dr)]
    g_wint, ((chip_w2b, chip_wout),) = _grad_w_in(
        pieces, dgd, s0bf, tp, [("exchange", [part_w2b, part_wout])])
    big_win = jnp.stack([g_wint[j * sh_in:(j + 1) * sh_in] for j in range(4)])
    (half_w1, half_w2), ((pair_win,),) = _add_chips_many(
        [(big_w1, pair_w1, [chip_w1]), (big_w2, pair_w2, [chip_w2a, chip_w2b])], [("pair", [big_win])])
    ds0, ((chip_win,), (sib_w1, sib_w2)) = _in_proj_bwd(
        pieces, dgd, w_int, dt1, tp, [("exchange", add_pair(big_win, pair_win)), ("swap", [half_w1, half_w2])])
    grad_x, dmeta, lvec = _ln_in_bwd(ds0.reshape(bsz, tp, d), x, meta_full, ln_in_g2)

    small_pack = _pack_small(acc2, acc1, lvec, cacc, gvec, dcw, dgup, dmeta)
    (half_win, half_wout), ((small_all,),) = _add_chips_many(
        [(big_win, pair_win, [chip_win]), (big_wout, pair_wout, [chip_wout])], [("all", [small_pack])],
        late_relay=True)
    red, loss = _sum_small(small_all)
    sib_win, sib_wout = _pair_swap([half_win, half_wout])
    halves = {"w_in": (half_win, sib_win), "w_out": (half_wout, sib_wout), "w_ff1": (half_w1, sib_w1),
              "w_ff2": (half_w2, sib_w2)}

    grads = {}
    weights = dict(meta_tokens=meta_tokens, ln_in_g=ln_in_g, ln_in_b=ln_in_b, w_in=w_in, conv_w=conv_w, conv_b=conv_b,
                   conv_ln_g=conv_ln_g, conv_ln_b=conv_ln_b, gate_up=gate_up, gate_bias=gate_bias,
                   gla_norm_g=gla_norm_g, w_out=w_out, ln1_g=ln1_g, ln1_b=ln1_b, w_ff1=w_ff1, w_ff2=w_ff2,
                   ln2_g=ln2_g, ln2_b=ln2_b)
    moms = dict(meta_tokens=(m_meta_tokens, v_meta_tokens), ln_in_g=(m_ln_in_g, v_ln_in_g),
                ln_in_b=(m_ln_in_b, v_ln_in_b), w_in=(m_w_in, v_w_in), conv_w=(m_conv_w, v_conv_w),
                conv_b=(m_conv_b, v_conv_b), conv_ln_g=(m_conv_ln_g, v_conv_ln_g),
                conv_ln_b=(m_conv_ln_b, v_conv_ln_b), gate_up=(m_gate_up, v_gate_up),
                gate_bias=(m_gate_bias, v_gate_bias), gla_norm_g=(m_gla_norm_g, v_gla_norm_g),
                w_out=(m_w_out, v_w_out), ln1_g=(m_ln1_g, v_ln1_g), ln1_b=(m_ln1_b, v_ln1_b),
                w_ff1=(m_w_ff1, v_w_ff1), w_ff2=(m_w_ff2, v_w_ff2), ln2_g=(m_ln2_g, v_ln2_g),
                ln2_b=(m_ln2_b, v_ln2_b))
    names = list(weights)
    big_names = ("w_in", "w_out", "w_ff1", "w_ff2")
    delta, new_m, new_v = {}, {}, {}
    for k in big_names:
        to2d = (lambda a: a[0].T) if k == "w_in" else (lambda a: a[0])
        back = (lambda a: a.T[None]) if k == "w_in" else (lambda a: a[None])
        res = _adamw_halves(to2d(weights[k]), *halves[k], to2d(moms[k][0]), to2d(moms[k][1]), c_arr)
        grads[k], delta[k], new_m[k], new_v[k] = [back(a) for a in res]
    small_names = [k for k in names if k not in big_names]
    two = lambda a: a.reshape(-1, a.shape[-1])

    def my_cols(full, rows, width):
        return lax.dynamic_slice(full, (0, chip * width), (rows, width))

    sharded = {
        "meta_tokens": my_cols(red[SMALL_AT["meta_tokens"][0]:SMALL_AT["meta_tokens"][0] + N_META], N_META, 256),
        "conv_w": my_cols(red[SMALL_CONV_W:SMALL_CONV_W + 16].reshape(32, D_CONV), CONV_WIDTH, 128),
        "gate_up": my_cols(red[SMALL_GATE_UP:SMALL_GATE_UP + 4].reshape(GLA_RANK, 256), GLA_RANK, 64)}
    upd = _adamw_small(red, sharded, {k: (two(weights[k]), two(moms[k][0]), two(moms[k][1])) for k in small_names})
    for k in small_names:
        shp = weights[k].shape
        grads[k], delta[k], new_m[k], new_v[k] = [a.reshape(shp) for a in upd[k]]
    loss = loss.reshape(())

    return (loss, grad_x, *[grads[k] for k in names], *[delta[k] for k in names],
            *[new_m[k] for k in names], *[new_v[k] for k in names])
```

```python
import functools

import jax
import jax.numpy as jnp
from jax import lax
from jax.experimental import pallas as pl
from jax.experimental.pallas import tpu as pltpu

F32 = jnp.float32
BF16 = jnp.bfloat16

D_MODEL = 1024
N_META = 16
D_CONV = 512
CONV_WIDTH = 31
GLA_HEADS = 4
GLA_DV = 128
GLA_DK = 64
GLA_RANK = 16
GLA_TAU = 16.0
CHUNK = 64
D_FF = 4096
LN_EPS = 1e-5
ALPHA = 2.0 ** 0.25
D_IN = 2576
D_IN_PAD = 2688
PAD = CHUNK - N_META
HEAD = PAD + N_META
Q_SCALE = GLA_DK ** -0.5
ADAM_LR, ADAM_B1, ADAM_B2, ADAM_EPS, ADAM_WD, ADAM_STEP = 0.001, 0.9, 0.999, 1e-08, 0.01, 10
HALF = D_MODEL // 2
VMEM_LIMIT = 56 * 1024 * 1024
MESH = pl.DeviceIdType.MESH

C_VAL, C_GATE, C_Q, C_K, C_V, C_R, C_GD = 0, 512, 1024, 1280, 1536, 2048, 2560

SMALL_AT = {"loss": (0, 1, 0, 1024), "ln_in_g": (1, 1, 0, 1024), "ln_in_b": (2, 1, 0, 1024), "ln1_g": (3, 1, 0, 1024),
            "ln1_b": (4, 1, 0, 1024), "ln2_g": (5, 1, 0, 1024), "ln2_b": (6, 1, 0, 1024), "conv_b": (7, 1, 0, 512),
            "conv_ln_g": (7, 1, 512, 512), "conv_ln_b": (8, 1, 0, 512), "gate_bias": (8, 1, 512, 256),
            "gla_norm_g": (8, 1, 768, 128), "meta_tokens": (40, 16, 0, 1024)}
SMALL_CONV_W, SMALL_GATE_UP = 16, 32
SMALL_ROWS = 56


def _params(sem=None, **kw):
    return pltpu.CompilerParams(dimension_semantics=sem, vmem_limit_bytes=VMEM_LIMIT, **kw)


def _row_tile(tp):
    for t in (704, 352, 192, 64):
        if tp % t == 0:
            return t
    raise ValueError(tp)


def _reduce_tile(tp, big):
    for t in ((2112, 1056, 704) if big else (1056, 704)) + (352, 192, 64):
        if tp % t == 0:
            return t
    raise ValueError(tp)


def _sub_rows(tm):
    return [slice(0, tm)]


def _dot(a, b, dims, precision=None):
    return lax.dot_general(a, b, (dims, ((), ())), preferred_element_type=F32, precision=precision)


def _nn(a, b, **kw):
    return _dot(a, b, ((1,), (0,)), **kw)


def _nt(a, b, **kw):
    return _dot(a, b, ((1,), (1,)), **kw)


def _tn(a, b, **kw):
    return _dot(a, b, ((0,), (0,)), **kw)


def _sigmoid(x):
    return 1.0 / (1.0 + jnp.exp(-x))


def _log_sigmoid(z):
    return jnp.minimum(z, 0.0) - jnp.log(1.0 + jnp.exp(-jnp.abs(z)))


def _ln_stats(t):
    mu = jnp.mean(t, axis=-1, keepdims=True)
    d = t - mu
    var = jnp.mean(d * d, axis=-1, keepdims=True)
    rstd = lax.rsqrt(var + LN_EPS)
    return d * rstd, rstd


def _ln_bwd(dxhat, xhat, rstd):
    m1 = jnp.mean(dxhat, axis=-1, keepdims=True)
    m2 = jnp.mean(dxhat * xhat, axis=-1, keepdims=True)
    return rstd * (dxhat - m1 - xhat * m2)


def _mesh_pos():
    return lax.axis_index("x"), lax.axis_index("y"), lax.axis_index("c")


ANY = pl.BlockSpec(memory_space=pl.ANY)


def _gather_sems(n):
    return [pltpu.SemaphoreType.DMA((n, 3))] * 4


def _gather_steps(ins, outs, sems, ranges=None):
    n = len(ins)
    send, recv, fsend, frecv = sems
    ranges = ranges or [(0, ref.shape[0]) for ref in ins]
    x, y, c = _mesh_pos()
    me = 2 * x + y
    sibling = (x, y, 1 - c)
    chips = [(1 - x, y), (x, 1 - y), (1 - x, 1 - y)]
    chip_idx = [2 * px + py for px, py in chips]
    mine = [pl.ds(pl.multiple_of(r0 + c * (nr // 2), 16), nr // 2) for r0, nr in ranges]
    other = [pl.ds(pl.multiple_of(r0 + (1 - c) * (nr // 2), 16), nr // 2) for r0, nr in ranges]
    pairs = [(a, k) for a in range(n) for k in range(3)]

    def ici(a, k, slab):
        return pltpu.make_async_remote_copy(
            src_ref=ins[a].at[mine[a], :], dst_ref=outs[a].at[slab, mine[a], :],
            send_sem=send.at[a, k], recv_sem=recv.at[a, k], device_id=(*chips[k], c), device_id_type=MESH)

    def forward(a, k, rows):
        blk = outs[a].at[chip_idx[k], rows[a], :]
        return pltpu.make_async_remote_copy(
            src_ref=blk, dst_ref=blk, send_sem=fsend.at[a, k], recv_sem=frecv.at[a, k],
            device_id=sibling, device_id_type=MESH)

    def start():
        for a, k in pairs:
            ici(a, k, me).start()

    def relay():
        for a, k in pairs:
            ici(a, k, chip_idx[k]).wait_recv()
            forward(a, k, mine).start()

    def finish():
        for a, k in pairs:
            forward(a, k, other).wait_recv()
        for a, k in pairs:
            ici(a, k, me).wait_send()
            forward(a, k, mine).wait_send()

    return start, relay, finish


def _place_own(gathered, shards):
    chip = 2 * lax.axis_index("x") + lax.axis_index("y")
    return [lax.dynamic_update_slice(g, s[None], (chip, 0, 0)) for g, s in zip(gathered, shards)]


def _gather_small_steps(ins, outs, sems):
    send, recv = sems
    x, y, c = _mesh_pos()
    chips = [(1 - x, y), (x, 1 - y), (1 - x, 1 - y)]

    def copy(a, k, slot):
        return pltpu.make_async_remote_copy(
            src_ref=ins[a], dst_ref=outs[a].at[slot], send_sem=send.at[a, k], recv_sem=recv.at[a, k],
            device_id=(*chips[k], c), device_id_type=MESH)

    def start():
        for a in range(len(ins)):
            for k in range(3):
                copy(a, k, 2 * x + y).start()

    def finish():
        for a in range(len(ins)):
            for k, (px, py) in enumerate(chips):
                copy(a, k, 2 * px + py).wait_recv()
            for k in range(3):
                copy(a, k, 2 * x + y).wait_send()

    return start, finish


def _pair_swap_steps(ins, outs, sems):
    send, recv = sems
    x, y, c = _mesh_pos()
    cps = [pltpu.make_async_remote_copy(
        src_ref=ins[a], dst_ref=outs[a], send_sem=send.at[a], recv_sem=recv.at[a],
        device_id=(x, y, 1 - c), device_id_type=MESH) for a in range(len(ins))]

    def start():
        for cp in cps:
            cp.start()

    def finish():
        for cp in cps:
            cp.wait()

    return start, finish


def _pair_exchange_steps(ins, outs, sems):
    send, recv = sems
    x, y, c = _mesh_pos()
    other = pl.ds(pl.multiple_of((1 - c) * HALF, 128), HALF)
    cps = [pltpu.make_async_remote_copy(
        src_ref=ins[a].at[:, :, other], dst_ref=outs[a], send_sem=send.at[a], recv_sem=recv.at[a],
        device_id=(x, y, 1 - c), device_id_type=MESH) for a in range(len(ins))]

    def start():
        for cp in cps:
            cp.start()

    def finish():
        for cp in cps:
            cp.wait()

    return start, finish


def _pair_exchange_shapes(grads):
    return [jax.ShapeDtypeStruct(g.shape[:2] + (HALF,), g.dtype) for g in grads]


NO_FUSE = ()


def _fuse_plan(kind, items):
    n = len(items)
    if kind == "gather":
        shards = [it[0] for it in items]
        bufs = [it[3] for it in items if it[3] is not None]
        alias, b = {}, 0
        for a, it in enumerate(items):
            if it[3] is not None:
                alias[n + b] = a
                b += 1
        ranges = [(it[1], it[2]) for it in items]
        return (shards + bufs, [jax.ShapeDtypeStruct((4,) + s.shape, s.dtype) for s in shards], alias, _gather_sems(n),
                lambda i, o, s: _gather_steps(i[:n], o, s, ranges))
    if kind == "exchange":
        return list(items), _chip_exchange_shapes(items), {}, _chip_exchange_sems(n), _chip_exchange_steps
    if kind == "small":
        return (list(items), [jax.ShapeDtypeStruct((4,) + s.shape, s.dtype) for s in items], {},
                _chip_exchange_sems(n), _gather_small_steps)
    if kind == "all":
        (pack,) = items
        return ([pack], [jax.ShapeDtypeStruct((8 * pack.shape[0], pack.shape[1]), pack.dtype)], {},
                [pltpu.SemaphoreType.DMA((7,)), pltpu.SemaphoreType.DMA((7,)), pltpu.SemaphoreType.DMA((1,))],
                _allgather_all_steps)
    pair_sems = [pltpu.SemaphoreType.DMA((n,)), pltpu.SemaphoreType.DMA((n,))]
    if kind == "swap":
        return (list(items), [jax.ShapeDtypeStruct(h.shape, h.dtype) for h in items], {}, pair_sems, _pair_swap_steps)
    return list(items), _pair_exchange_shapes(items), {}, pair_sems, _pair_exchange_steps


def _fused_call(body, fuse, *, name, grid, in_specs, out_specs, out_shape, scratch_shapes, operands,
                late_relay=False):
    plans = [_fuse_plan(kind, list(items)) for kind, items in fuse if len(items)]
    n_in, n_out, n_s = len(in_specs), len(out_shape), len(scratch_shapes)
    comm = [a for p in plans for a in p[0]]
    shapes = [s for p in plans for s in p[1]]
    nc, no = len(comm), len(shapes)
    aliases, i_at, o_at = {}, n_in, n_out
    for p in plans:
        for i, o in p[2].items():
            aliases[i_at + i] = o_at + o
        i_at, o_at = i_at + len(p[0]), o_at + len(p[1])

    def wrapped(*refs):
        o0 = n_in + nc
        s0 = o0 + n_out + no
        i_at, o_at, sem_at, steps = n_in, o0 + n_out, s0 + n_s, []
        for p in plans:
            steps.append(p[4](refs[i_at:i_at + len(p[0])], refs[o_at:o_at + len(p[1])], refs[sem_at:sem_at + len(p[3])]))
            i_at, o_at, sem_at = i_at + len(p[0]), o_at + len(p[1]), sem_at + len(p[3])
        first, last = _grid_ends(grid)
        for st in steps:
            pl.when(first)(st[0])
        if not late_relay:
            for st in steps:
                for mid in st[1:-1]:
                    pl.when(last)(mid)
        body(*refs[:n_in], *refs[o0:o0 + n_out], *refs[s0:s0 + n_s])
        for st in steps:
            for step in (st[1:] if late_relay else st[-1:]):
                pl.when(last)(step)

    res = pl.pallas_call(
        wrapped if plans else body, name=name, grid=grid, in_specs=list(in_specs) + [ANY] * nc,
        out_specs=list(out_specs) + [ANY] * no, out_shape=list(out_shape) + shapes,
        scratch_shapes=list(scratch_shapes) + [s for p in plans for s in p[3]], input_output_aliases=aliases,
        compiler_params=_params(("arbitrary",) * len(grid)))(*operands, *comm)
    outs, got, results = list(res[:n_out]), list(res[n_out:]), []
    for p in plans:
        results.append(got[:len(p[1])])
        got = got[len(p[1]):]
    return outs, results


def _chip_exchange_sems(n):
    return [pltpu.SemaphoreType.DMA((n, 3))] * 2


def _chip_exchange_shapes(parts):
    return [jax.ShapeDtypeStruct((3,) + p.shape[1:], p.dtype) for p in parts]


def _chip_exchange_steps(ins, outs, sems):
    send, recv = sems
    x, y, c = _mesh_pos()
    chips = [(1 - x, y), (x, 1 - y), (1 - x, 1 - y)]
    cps = [pltpu.make_async_remote_copy(
        src_ref=ins[a].at[2 * px + py], dst_ref=outs[a].at[k], send_sem=send.at[a, k], recv_sem=recv.at[a, k],
        device_id=(px, py, c), device_id_type=MESH) for a in range(len(ins)) for k, (px, py) in enumerate(chips)]

    def start():
        for cp in cps:
            cp.start()

    def finish():
        for cp in cps:
            cp.wait()

    return start, finish


def _pair_swap(halves):
    n = len(halves)

    def body(*refs):
        start, finish = _pair_swap_steps(refs[:n], refs[n:2 * n], refs[2 * n:])
        start()
        finish()

    return pl.pallas_call(
        body, name="grad_pair_swap",
        out_shape=[jax.ShapeDtypeStruct(h.shape, h.dtype) for h in halves],
        in_specs=[ANY] * n, out_specs=[ANY] * n,
        scratch_shapes=[pltpu.SemaphoreType.DMA((n,)), pltpu.SemaphoreType.DMA((n,))],
    )(*halves)


def _allgather_all_steps(ins, outs, sems):
    send_sems, recv_sems, local_sem = sems
    x_ref, out_ref = ins[0], outs[0]
    m_per = x_ref.shape[0]
    x, y, c = _mesh_pos()
    me, sibling = (x, y, c), (x, y, 1 - c)
    chips = [(1 - x, y), (x, 1 - y), (1 - x, 1 - y)]

    def rows(px, py, pc):
        return out_ref.at[pl.ds(pl.multiple_of((4 * px + 2 * py + pc) * m_per, 8), m_per), :]

    def copy(k, block, to, src=None):
        return pltpu.make_async_remote_copy(
            src_ref=rows(*block) if src is None else src, dst_ref=rows(*block),
            send_sem=send_sems.at[k], recv_sem=recv_sems.at[k], device_id=to, device_id_type=MESH)

    mine = pltpu.make_async_copy(x_ref, rows(*me), local_sem.at[0])
    first = [copy(0, me, sibling, src=x_ref)]
    first += [copy(1 + j, me, (*chip, c), src=x_ref) for j, chip in enumerate(chips)]
    passed = [copy(4 + j, (*chip, c), sibling) for j, chip in enumerate(chips)]

    def start():
        mine.start()
        for cp in first:
            cp.start()

    def relay():
        for j, chip in enumerate(chips):
            copy(1 + j, (*chip, c), me).wait_recv()
            passed[j].start()

    def finish():
        copy(0, sibling, me).wait_recv()
        for j, chip in enumerate(chips):
            copy(4 + j, (*chip, 1 - c), me).wait_recv()
        for cp in first + passed:
            cp.wait_send()
        mine.wait()

    return start, relay, finish


def _add_pair(g, got, c_arr, splits=None):
    _, rows, _ = g.shape
    splits = splits or [rows]

    def body(c_ref, g_ref, r_ref, *o_refs):
        at = 0
        for o_ref, n in zip(o_refs, splits):
            o_ref[...] = (g_ref[:, at:at + n, :] + r_ref[:, at:at + n, :]).astype(BF16)
            at += n

    return pl.pallas_call(
        body, name="grad_add_pair", out_shape=[jax.ShapeDtypeStruct((4, n, HALF), BF16) for n in splits],
        grid_spec=pltpu.PrefetchScalarGridSpec(
            num_scalar_prefetch=1, grid=(4,),
            in_specs=[pl.BlockSpec((1, rows, HALF), lambda j, c: (j, 0, c[0])),
                      pl.BlockSpec((1, rows, HALF), lambda j, c: (j, 0, 0))],
            out_specs=[pl.BlockSpec((1, n, HALF), lambda j, c: (j, 0, 0)) for n in splits]),
        compiler_params=_params(("arbitrary",)),
    )(c_arr, g, got)


def _add_chips_many(items, fuse=NO_FUSE, late_relay=False):
    n = len(items)
    rows = [it[0].shape[1] for it in items]
    n_got = [len(it[2]) for it in items]

    def body(*refs):
        g_refs, p_refs = refs[:n], refs[n:2 * n]
        got_refs = refs[2 * n:2 * n + sum(n_got)]
        o_refs = refs[2 * n + sum(n_got):3 * n + sum(n_got)]
        scr = refs[3 * n + sum(n_got):]
        x, y, c = _mesh_pos()
        chip = 2 * x + y
        mine = pl.ds(pl.multiple_of(c * HALF, 128), HALF)
        copies = []
        for a in range(n):
            copies.append((pltpu.make_async_copy(g_refs[a].at[chip, :, mine], scr[2 * a], scr[2 * n].at[2 * a]),
                           pltpu.make_async_copy(p_refs[a].at[chip], scr[2 * a + 1], scr[2 * n].at[2 * a + 1])))
        for cg, cp in copies:
            cg.start()
            cp.start()
        at_ref = 0
        for a in range(n):
            copies[a][0].wait()
            copies[a][1].wait()
            at = 0
            for r_ref in got_refs[at_ref:at_ref + n_got[a]]:
                k = r_ref.shape[1]
                own = scr[2 * a][at:at + k, :] + scr[2 * a + 1][at:at + k, :]
                o_refs[a][at:at + k, :] = ((own + r_ref[0].astype(F32)) + r_ref[1].astype(F32)) + r_ref[2].astype(F32)
                at += k
            at_ref += n_got[a]

    gots = [t for it in items for t in it[2]]
    outs, got = _fused_call(
        body, fuse, name="grad_add_chips_many", grid=(1,),
        in_specs=[ANY] * (2 * n) + [pl.BlockSpec(t.shape, lambda i: (0, 0, 0)) for t in gots],
        out_specs=[pl.BlockSpec((r, HALF), lambda i: (0, 0)) for r in rows],
        out_shape=[jax.ShapeDtypeStruct((r, HALF), F32) for r in rows],
        scratch_shapes=[pltpu.VMEM((r, HALF), F32) for r in rows for _ in range(2)] + [pltpu.SemaphoreType.DMA((2 * n,))],
        operands=(*[it[0] for it in items], *[it[1] for it in items], *gots), late_relay=late_relay)
    return outs, got


def _pack_small(acc2, acc1, lvec, cacc, gvec, dcw, dgup, dmeta):
    bsz = lvec.shape[0]

    def body(a2_ref, a1_ref, lv_ref, ca_ref, gv_ref, cw_ref, gu_ref, dm_ref, o_ref):
        def put(name, val):
            r0, nr, l0, nl = SMALL_AT[name]
            o_ref[r0:r0 + nr, l0:l0 + nl] = val

        def put_folded(row0, val, per_row):
            w = val.shape[1]
            for i in range(val.shape[0]):
                o_ref[row0 + i // per_row:row0 + i // per_row + 1, (i % per_row) * w:(i % per_row + 1) * w] = val[i:i + 1]

        over_b = lambda f: functools.reduce(lambda a, b: a + b, [f(b) for b in range(bsz)])
        o_ref[...] = jnp.zeros_like(o_ref)
        put("loss", a2_ref[0:1, :])
        put("ln2_g", a2_ref[1:2, :])
        put("ln2_b", a2_ref[2:3, :])
        put("ln1_g", a1_ref[0:1, :])
        put("ln1_b", a1_ref[1:2, :])
        put("ln_in_g", over_b(lambda b: lv_ref[b, 0:1, :]))
        put("ln_in_b", over_b(lambda b: lv_ref[b, 1:2, :]))
        put("conv_b", ca_ref[0:1, :])
        put("conv_ln_g", ca_ref[1:2, :])
        put("conv_ln_b", ca_ref[2:3, :])
        put("gate_bias", over_b(lambda b: gv_ref[b, 0:1, :]))
        put("gla_norm_g", over_b(lambda b: gv_ref[b, 1:2, 0:128] + gv_ref[b, 1:2, 128:256]))
        put_folded(SMALL_CONV_W, over_b(lambda b: cw_ref[b]), D_MODEL // D_CONV)
        put_folded(SMALL_GATE_UP, over_b(lambda b: gu_ref[b, 0:GLA_RANK, :]), D_MODEL // 256)
        put("meta_tokens", over_b(lambda b: dm_ref[b]))

    return pl.pallas_call(
        body, name="pack_small", out_shape=jax.ShapeDtypeStruct((SMALL_ROWS, D_MODEL), F32),
    )(acc2, acc1, lvec, cacc, gvec, dcw, dgup, dmeta)


def _sum_small(gathered):
    def body(g_ref, o_ref, loss_ref):
        acc = g_ref[0:SMALL_ROWS, :]
        for d in range(1, 8):
            acc = acc + g_ref[d * SMALL_ROWS:(d + 1) * SMALL_ROWS, :]
        o_ref[...] = acc
        loss_ref[...] = jnp.sum(acc[0:1, :], axis=1, keepdims=True)

    return pl.pallas_call(
        body, name="sum_small",
        out_shape=[jax.ShapeDtypeStruct((SMALL_ROWS, D_MODEL), F32), jax.ShapeDtypeStruct((1, 1), F32)],
    )(gathered)


def _adamw_math(w, g, m, v):
    c1 = 1.0 - ADAM_B1 ** ADAM_STEP
    c2 = 1.0 - ADAM_B2 ** ADAM_STEP
    mn = ADAM_B1 * m + (1.0 - ADAM_B1) * g
    vn = ADAM_B2 * v + (1.0 - ADAM_B2) * (g * g)
    return -ADAM_LR * ((mn / c1) / (jnp.sqrt(vn / c2) + ADAM_EPS) + ADAM_WD * w), mn, vn


def _adamw_small(red, sharded_grads, params):
    names = list(params)
    sharded = [k for k in names if k in sharded_grads]
    n, ns = len(names), len(sharded)

    def body(*refs):
        red_ref, sg_refs, p_refs, o_refs = refs[0], refs[1:1 + ns], refs[1 + ns:1 + ns + 3 * n], refs[1 + ns + 3 * n:]
        for i, k in enumerate(names):
            w_ref, m_ref, v_ref = p_refs[3 * i:3 * i + 3]
            if k in sharded:
                g = sg_refs[sharded.index(k)][...]
            else:
                r0, nr, l0, nl = SMALL_AT[k]
                g = red_ref[r0:r0 + nr, l0:l0 + nl]
            dl, mn, vn = _adamw_math(w_ref[...], g, m_ref[...], v_ref[...])
            for o_ref, val in zip(o_refs[4 * i:4 * i + 4], (g, dl, mn, vn)):
                o_ref[...] = val

    flat = [a for k in names for a in params[k]]
    res = pl.pallas_call(
        body, name="adamw_small",
        out_shape=[jax.ShapeDtypeStruct(params[k][0].shape, F32) for k in names for _ in range(4)],
    )(red, *[sharded_grads[k] for k in sharded], *flat)
    return {k: tuple(res[4 * i:4 * i + 4]) for i, k in enumerate(names)}


def _adamw_halves(w, own, sib, m, v, c_arr):
    rows, cols = w.shape
    tr = 128 if rows % 128 == 0 else rows

    def body(c_ref, w_ref, own_ref, sib_ref, m_ref, v_ref, g_ref, d_ref, mo_ref, vo_ref):
        first = c_ref[0] == 0
        own, sib = own_ref[...], sib_ref[...]
        g = jnp.concatenate([jnp.where(first, own, sib), jnp.where(first, sib, own)], axis=1)
        g_ref[...] = g
        d_ref[...], mo_ref[...], vo_ref[...] = _adamw_math(w_ref[...], g, m_ref[...], v_ref[...])

    full = pl.BlockSpec((tr, cols), lambda i, c: (i, 0))
    half = pl.BlockSpec((tr, HALF), lambda i, c: (i, 0))
    return pl.pallas_call(
        body, name="adamw_halves", out_shape=[jax.ShapeDtypeStruct(w.shape, F32)] * 4,
        grid_spec=pltpu.PrefetchScalarGridSpec(
            num_scalar_prefetch=1, grid=(rows // tr,), in_specs=[full, half, half, full, full], out_specs=[full] * 4),
        compiler_params=_params(("parallel",)),
    )(c_arr, w, own, sib, m, v)


def _ln_in_fwd(x, g, b, fuse=NO_FUSE):
    bsz, s, d = x.shape
    tp = s + HEAD
    nh = 2
    sh = s // nh
    rc = min(256, sh)

    def body(x_ref, g_ref, b_ref, s0_ref, s0b_ref):
        h = pl.program_id(1)
        gg, bb = g_ref[...], b_ref[...]

        def step(i, carry):
            src = pl.ds(pl.multiple_of(i * rc, rc), rc)
            dst = pl.ds(pl.multiple_of(HEAD + h * sh + i * rc, 64), rc)
            xh, _ = _ln_stats(x_ref[0, src, :])
            val = xh * gg + bb
            s0_ref[0, dst, :] = val
            s0b_ref[0, dst, :] = val.astype(BF16)
            return carry

        lax.fori_loop(0, sh // rc, step, 0)

    full = lambda bi, hi: (bi, 0, 0)
    (s0, s0b), got = _fused_call(
        body, fuse, name="ln_in_fwd",
        out_shape=[jax.ShapeDtypeStruct((bsz, tp, d), F32), jax.ShapeDtypeStruct((bsz, tp, d), BF16)],
        grid=(bsz, nh),
        in_specs=[pl.BlockSpec((1, sh, d), lambda bi, hi: (bi, hi, 0)),
                  pl.BlockSpec((1, d), lambda bi, hi: (0, 0)),
                  pl.BlockSpec((1, d), lambda bi, hi: (0, 0))],
        out_specs=[pl.BlockSpec((1, tp, d), full)] * 2, scratch_shapes=[], operands=(x, g, b),
        late_relay=True)
    return s0, s0b, got


def _ln_meta(s0, s0b, meta, g, b):
    bsz, tp, d = s0.shape

    def body(s0_in, s0b_in, meta_ref, g_ref, b_ref, s0_ref, s0b_ref):
        mh, _ = _ln_stats(meta_ref[...])
        mv = mh * g_ref[...] + b_ref[...]
        s0_ref[0, 0:PAD, :] = jnp.zeros((PAD, d), F32)
        s0b_ref[0, 0:PAD, :] = jnp.zeros((PAD, d), BF16)
        s0_ref[0, PAD:HEAD, :] = mv
        s0b_ref[0, PAD:HEAD, :] = mv.astype(BF16)

    head = pl.BlockSpec((1, HEAD, d), lambda bi: (bi, 0, 0))
    vec = pl.BlockSpec((1, d), lambda bi: (0, 0))
    return pl.pallas_call(
        body, name="ln_meta", out_shape=[jax.ShapeDtypeStruct(s0.shape, F32), jax.ShapeDtypeStruct(s0b.shape, BF16)],
        grid=(bsz,), in_specs=[head, head, pl.BlockSpec((N_META, d), lambda bi: (0, 0)), vec, vec],
        out_specs=[head, head], input_output_aliases={0: 0, 1: 1},
        compiler_params=_params(("parallel",)),
    )(s0, s0b, meta, g, b)


def _in_proj(s0b, w_int, fuse=NO_FUSE):
    r, d = s0b.shape
    tm = _row_tile(r)

    def body(a_ref, w_ref, o_ref):
        o_ref[...] = _nt(a_ref[...], w_ref[...])

    (u,), got = _fused_call(
        body, fuse, name="in_proj", out_shape=[jax.ShapeDtypeStruct((r, D_IN_PAD), F32)],
        grid=(r // tm,),
        in_specs=[pl.BlockSpec((tm, d), lambda i: (i, 0)), pl.BlockSpec((D_IN_PAD, d), lambda i: (0, 0))],
        out_specs=[pl.BlockSpec((tm, D_IN_PAD), lambda i: (i, 0))], scratch_shapes=[], operands=(s0b, w_int))
    return u, got


def _conv_fwd(u, conv_w, conv_b, fuse=NO_FUSE):
    bsz, tp, _ = u.shape
    nchunk = tp // CHUNK
    win = CHUNK + 32
    nct = D_CONV // 128

    def body(cv_ref, cg_ref, w_ref, cb_ref, hc_ref, h_scr, win_scr):
        h_scr[0:32, :] = jnp.zeros((32, 128), F32)
        h_scr[32:32 + tp, :] = cv_ref[0] * _sigmoid(cg_ref[0])
        cb = cb_ref[...]

        def step(n, carry):
            r0 = pl.multiple_of(n * CHUNK, CHUNK)
            win_scr[...] = h_scr[pl.ds(r0, win), :]
            acc = jnp.zeros((CHUNK, 128), F32)
            for j in range(CONV_WIDTH):
                acc = acc + w_ref[j:j + 1, :] * win_scr[2 + j:2 + j + CHUNK, :]
            hc_ref[0, pl.ds(r0, CHUNK), :] = acc + cb
            return carry

        lax.fori_loop(0, nchunk, step, 0)

    (hc,), got = _fused_call(
        body, fuse, name="conv_fwd", out_shape=[jax.ShapeDtypeStruct((bsz, tp, D_CONV), F32)],
        grid=(bsz, nct),
        in_specs=[pl.BlockSpec((1, tp, 128), lambda bi, t: (bi, 0, C_VAL // 128 + t)),
                  pl.BlockSpec((1, tp, 128), lambda bi, t: (bi, 0, C_GATE // 128 + t)),
                  pl.BlockSpec((32, 128), lambda bi, t: (0, t)),
                  pl.BlockSpec((1, 128), lambda bi, t: (0, t))],
        out_specs=[pl.BlockSpec((1, tp, 128), lambda bi, t: (bi, 0, t))],
        scratch_shapes=[pltpu.VMEM((tp + 32, 128), F32), pltpu.VMEM((win, 128), F32)],
        operands=(u, u, conv_w, conv_b))
    return hc, got


def _gla_group(nchunk):
    return 11 if nchunk % 11 == 0 else nchunk


def _bdot(a, b, ca, cb, precision=None):
    return lax.dot_general(a, b, (((ca,), (cb,)), ((0,), (0,))), preferred_element_type=F32, precision=precision)


def _bnn(a, b, **kw):
    return _bdot(a, b, 2, 1, **kw)


def _bnt(a, b, **kw):
    return _bdot(a, b, 2, 2, **kw)


def _gla_consts(nb):
    row = lax.broadcasted_iota(jnp.int32, (nb, CHUNK, CHUNK), 1)
    col = lax.broadcasted_iota(jnp.int32, (nb, CHUNK, CHUNK), 2)
    lane = lax.broadcasted_iota(jnp.int32, (1, 1, 128), 2)
    return row >= col, row <= col, [lane < GLA_DK, lane >= GLA_DK]


def _gla_group_terms(g, nb, q_ref, k_ref, gd_ref, gup_ref, gb_ref, tril):
    m = nb * CHUNK
    rows = pl.ds(pl.multiple_of(g * m, CHUNK), m)
    z = _nn(gd_ref[0, rows, :].astype(BF16), gup_ref[...]) + gb_ref[...]
    valid = g * m + lax.broadcasted_iota(jnp.int32, (m, 1), 0) >= PAD
    lg = jnp.where(valid, _log_sigmoid(z) * (1.0 / GLA_TAU), 0.0)
    bcum = _bnn(tril.astype(F32), lg.reshape(nb, CHUNK, 128), precision=lax.Precision.HIGHEST)
    blast = bcum[:, CHUNK - 1:CHUNK, :]
    eb = jnp.exp(bcum)
    enb = jnp.exp(-bcum)
    erest = jnp.exp(blast - bcum)
    q = (q_ref[0, rows, :] * Q_SCALE).reshape(nb, CHUNK, 128)
    k = k_ref[0, rows, :].reshape(nb, CHUNK, 128)
    return rows, valid, z, eb, enb, erest, jnp.exp(blast), q * eb, k * enb, k * erest


def _grid_ends(grid):
    ids = [pl.program_id(i) for i in range(len(grid))]
    first = functools.reduce(jnp.logical_and, [i == 0 for i in ids])
    last = functools.reduce(jnp.logical_and, [i == g - 1 for i, g in zip(ids, grid)])
    return first, last


def _gla_fwd(u, gup, gbias, gnorm, fuse=NO_FUSE):
    bsz, tp, _ = u.shape
    nchunk = tp // CHUNK
    nb = _gla_group(nchunk)

    def body(q_ref, k_ref, v_ref, r_ref, gd_ref, gup_ref, gb_ref, gn_ref, out_ref, o_ref, st_ref, s_scr):
        tril, _, hmask = _gla_consts(nb)
        s_scr[...] = jnp.zeros_like(s_scr)
        gn = gn_ref[...]

        def group(g, carry):
            rows, _, _, _, _, _, dec, qe, ke, kd = _gla_group_terms(g, nb, q_ref, k_ref, gd_ref, gup_ref, gb_ref, tril)
            keb, kdb = ke.astype(BF16), kd.astype(BF16)
            for h in range(2):
                cols = slice(h * GLA_DV, (h + 1) * GLA_DV)
                qh = jnp.where(hmask[h], qe, 0.0).astype(BF16)
                vh = v_ref[0, rows, cols].astype(BF16).reshape(nb, CHUNK, GLA_DV)
                a = jnp.where(tril, _bnt(qh, keb), 0.0).astype(BF16)
                st = s_scr[h]
                sts = []
                for n in range(nb):
                    st_ref[0, h, g * nb + n] = st
                    sts.append(st.astype(BF16))
                    st = dec[n] * st + _tn(vh[n], kdb[n])
                s_scr[h] = st
                o = (_bnn(a, vh) + _bnt(qh, jnp.stack(sts))).reshape(nb * CHUNK, GLA_DV)
                o_ref[0, rows, cols] = o
                rms = lax.rsqrt(jnp.mean(o * o, axis=-1, keepdims=True) + LN_EPS)
                rh = r_ref[0, rows, cols]
                out_ref[0, rows, cols] = (o * rms * gn * (rh * _sigmoid(rh))).astype(BF16)
            return carry

        lax.fori_loop(0, nchunk // nb, group, 0)

    res, got = _fused_call(
        body, fuse, name="gla_fwd",
        out_shape=[jax.ShapeDtypeStruct((bsz, tp, 512), BF16), jax.ShapeDtypeStruct((bsz, tp, 512), F32),
                   jax.ShapeDtypeStruct((bsz, GLA_HEADS, nchunk, GLA_DV, 128), F32)],
        grid=(bsz, 2),
        in_specs=[pl.BlockSpec((1, tp, 128), lambda bi, p: (bi, 0, C_Q // 128 + p)),
                  pl.BlockSpec((1, tp, 128), lambda bi, p: (bi, 0, C_K // 128 + p)),
                  pl.BlockSpec((1, tp, 256), lambda bi, p: (bi, 0, C_V // 256 + p)),
                  pl.BlockSpec((1, tp, 256), lambda bi, p: (bi, 0, C_R // 256 + p)),
                  pl.BlockSpec((1, tp, 128), lambda bi, p: (bi, 0, C_GD // 128)),
                  pl.BlockSpec((128, 128), lambda bi, p: (0, p)),
                  pl.BlockSpec((1, 128), lambda bi, p: (0, p)),
                  pl.BlockSpec((1, 128), lambda bi, p: (0, 0))],
        out_specs=[pl.BlockSpec((1, tp, 256), lambda bi, p: (bi, 0, p)),
                   pl.BlockSpec((1, tp, 256), lambda bi, p: (bi, 0, p)),
                   pl.BlockSpec((1, 2, nchunk, GLA_DV, 128), lambda bi, p: (bi, p, 0, 0, 0))],
        scratch_shapes=[pltpu.VMEM((2, GLA_DV, 128), F32)],
        operands=(u, u, u, u, u, gup, gbias, gnorm))
    return res[0], res[1], res[2], got


def _out_proj_ln1(hc, gla_out, w_out, s0, cg, cb, g1, b1, fuse=NO_FUSE):
    r, d = s0.shape
    tm = _row_tile(r)

    def body(hc_ref, a_ref, w_ref, s0_ref, cg_ref, cb_ref, g_ref, b_ref, co_ref, xh_ref, rstd_ref, s1b_ref):
        for rs in _sub_rows(tm):
            xc, _ = _ln_stats(hc_ref[rs, :])
            nv = xc * cg_ref[...] + cb_ref[...]
            co = (nv * _sigmoid(nv)).astype(BF16)
            co_ref[rs, :] = co
            mix = _nn(co, w_ref[0:D_CONV, :]) + _nn(a_ref[rs, :], w_ref[D_CONV:, :])
            xh, rstd = _ln_stats(ALPHA * s0_ref[rs, :] + mix)
            xh_ref[rs, :] = xh
            rstd_ref[rs, :] = rstd
            s1b_ref[rs, :] = (xh * g_ref[...] + b_ref[...]).astype(BF16)

    row = lambda n: pl.BlockSpec((tm, n), lambda i: (i, 0))
    vec = lambda n: pl.BlockSpec((1, n), lambda i: (0, 0))
    res, got = _fused_call(
        body, fuse, name="out_proj_ln1",
        out_shape=[jax.ShapeDtypeStruct((r, D_CONV), BF16), jax.ShapeDtypeStruct((r, d), F32),
                   jax.ShapeDtypeStruct((r, 1), F32), jax.ShapeDtypeStruct((r, d), BF16)],
        grid=(r // tm,),
        in_specs=[row(D_CONV), row(512), pl.BlockSpec((d, d), lambda i: (0, 0)), row(d),
                  vec(D_CONV), vec(D_CONV), vec(d), vec(d)],
        out_specs=[row(D_CONV), row(d), row(1), row(d)], scratch_shapes=[],
        operands=(hc, gla_out, w_out, s0, cg, cb, g1, b1))
    return res[0], res[1], res[2], res[3], got


def _ffn1(s1b, w1, fuse=NO_FUSE):
    r, d = s1b.shape
    tm = _row_tile(r)
    ns, _, wn = w1.shape

    def body(a_ref, w_ref, o_ref):
        a = a_ref[...]
        for j in range(ns):
            o_ref[:, j * wn:(j + 1) * wn] = jnp.maximum(_nn(a, w_ref[j]), 0.0).astype(BF16)

    (ra,), got = _fused_call(
        body, fuse, name="ffn1", out_shape=[jax.ShapeDtypeStruct((r, D_FF), BF16)], grid=(r // tm,),
        in_specs=[pl.BlockSpec((tm, d), lambda i: (i, 0)), pl.BlockSpec(w1.shape, lambda i: (0, 0, 0))],
        out_specs=[pl.BlockSpec((tm, D_FF), lambda i: (i, 0))], scratch_shapes=[], operands=(s1b, w1))
    return ra, got


def _ffn2_ln2_loss(ra, w2, xhat1, g1, b1, g2, b2, tgt, tp):
    r, d = xhat1.shape
    tm = _row_tile(tp)
    per = tp // tm

    def body(ra_ref, w_ref, xh1_ref, g1_ref, b1_ref, g2_ref, b2_ref, tgt_ref, dt_ref, dtb_ref, acc_ref, t_ref, sem):
        i = pl.program_id(0)
        b, j = i // per, i % per

        @pl.when(i == 0)
        def _():
            acc_ref[...] = jnp.zeros_like(acc_ref)

        head_copy = pltpu.make_async_copy(tgt_ref.at[b, pl.ds(0, tm - HEAD), :], t_ref.at[pl.ds(HEAD, tm - HEAD), :], sem)
        body_copy = pltpu.make_async_copy(
            tgt_ref.at[b, pl.ds(pl.multiple_of(jnp.maximum(j * tm - HEAD, 0), 64), tm), :], t_ref, sem)

        @pl.when(j == 0)
        def _():
            t_ref[0:HEAD, :] = jnp.zeros((HEAD, d), F32)
            head_copy.start()

        pl.when(j > 0)(body_copy.start)

        sums = [jnp.zeros((1, d), F32)] * 3
        for rs in _sub_rows(tm):
            rb = ra_ref[rs, :]
            f = _nn(rb * rb, w_ref[...])
            if rs.start == 0:
                pl.when(j == 0)(head_copy.wait)
                pl.when(j > 0)(body_copy.wait)
            s1 = xh1_ref[rs, :] * g1_ref[...] + b1_ref[...]
            xh2, rstd2 = _ln_stats(ALPHA * s1 + f)
            y = xh2 * g2_ref[...] + b2_ref[...]
            rowid = (i % per) * tm + rs.start + lax.broadcasted_iota(jnp.int32, (rs.stop - rs.start, 1), 0)
            e = jnp.where(rowid >= HEAD, y - t_ref[rs, :], 0.0)
            dy = e * (1.0 / d)
            dt2 = _ln_bwd(dy * g2_ref[...], xh2, rstd2)
            dt_ref[rs, :] = dt2
            dtb_ref[rs, :] = dt2.astype(BF16)
            sums = [sums[0] + (0.5 / d) * jnp.sum(e * e, axis=0, keepdims=True),
                    sums[1] + jnp.sum(dy * xh2, axis=0, keepdims=True), sums[2] + jnp.sum(dy, axis=0, keepdims=True)]
        for k in range(3):
            acc_ref[k:k + 1, :] += sums[k]

    row = lambda n: pl.BlockSpec((tm, n), lambda i: (i, 0))
    vec = pl.BlockSpec((1, d), lambda i: (0, 0))
    return pl.pallas_call(
        body, name="ffn2_ln2_loss",
        out_shape=[jax.ShapeDtypeStruct((r, d), F32), jax.ShapeDtypeStruct((r, d), BF16),
                   jax.ShapeDtypeStruct((8, d), F32)],
        grid=(r // tm,),
        in_specs=[row(D_FF), pl.BlockSpec((D_FF, d), lambda i: (0, 0)), row(d), vec, vec, vec, vec, ANY],
        out_specs=[row(d), row(d), pl.BlockSpec((8, d), lambda i: (0, 0))],
        scratch_shapes=[pltpu.VMEM((tm, d), F32), pltpu.SemaphoreType.DMA],
        compiler_params=_params(("arbitrary",)),
    )(ra, w2, xhat1, g1, b1, g2, b2, tgt)


def _ffn_bwd_da(dt2b, w2, ra):
    r, d = dt2b.shape
    tm = _row_tile(r)

    def body(g_ref, w_ref, ra_ref, o_ref):
        o_ref[...] = (_nt(g_ref[...], w_ref[...]) * (2.0 * ra_ref[...].astype(F32))).astype(BF16)

    return pl.pallas_call(
        body, name="ffn_bwd_da", out_shape=jax.ShapeDtypeStruct((r, D_FF), BF16),
        grid=(r // tm,),
        in_specs=[pl.BlockSpec((tm, d), lambda i: (i, 0)), pl.BlockSpec((D_FF, d), lambda i: (0, 0)),
                  pl.BlockSpec((tm, D_FF), lambda i: (i, 0))],
        out_specs=pl.BlockSpec((tm, D_FF), lambda i: (i, 0)),
        compiler_params=_params(("parallel",)),
    )(dt2b, w2, ra)


def _ffn_bwd_ln1(da, w1, dt2, xhat1, rstd1, g1):
    r, d = dt2.shape
    tm = _row_tile(r)

    def body(da_ref, w_ref, dt2_ref, xh_ref, rstd_ref, g_ref, dt_ref, dtb_ref, acc_ref):
        @pl.when(pl.program_id(0) == 0)
        def _():
            acc_ref[...] = jnp.zeros_like(acc_ref)

        sums = [jnp.zeros((1, d), F32)] * 2
        for rs in _sub_rows(tm):
            ds1 = ALPHA * dt2_ref[rs, :]
            for j in range(w1.shape[0]):
                ds1 = ds1 + _nt(da_ref[rs, j * w1.shape[2]:(j + 1) * w1.shape[2]], w_ref[j])
            xh = xh_ref[rs, :]
            dt1 = _ln_bwd(ds1 * g_ref[...], xh, rstd_ref[rs, :])
            dt_ref[rs, :] = dt1
            dtb_ref[rs, :] = dt1.astype(BF16)
            sums = [sums[0] + jnp.sum(ds1 * xh, axis=0, keepdims=True), sums[1] + jnp.sum(ds1, axis=0, keepdims=True)]
        for k in range(2):
            acc_ref[k:k + 1, :] += sums[k]

    row = lambda n: pl.BlockSpec((tm, n), lambda i: (i, 0))
    return pl.pallas_call(
        body, name="ffn_bwd_ln1",
        out_shape=[jax.ShapeDtypeStruct((r, d), F32), jax.ShapeDtypeStruct((r, d), BF16),
                   jax.ShapeDtypeStruct((8, d), F32)],
        grid=(r // tm,),
        in_specs=[row(D_FF), pl.BlockSpec(w1.shape, lambda i: (0, 0, 0)), row(d), row(d), row(1),
                  pl.BlockSpec((1, d), lambda i: (0, 0))],
        out_specs=[row(d), row(d), pl.BlockSpec((8, d), lambda i: (0, 0))],
        compiler_params=_params(("arbitrary",)),
    )(da, w1, dt2, xhat1, rstd1, g1)


def _matmul_tn(lhs, rhs, bm, square_lhs=False, name="matmul_tn", fuse=NO_FUSE):
    r, m = lhs.shape
    n = rhs.shape[1]
    tk = _reduce_tile(r, True)

    def body(a_ref, b_ref, o_ref):
        @pl.when(pl.program_id(1) == 0)
        def _():
            o_ref[...] = jnp.zeros_like(o_ref)

        a = a_ref[...]
        if square_lhs:
            a = a * a
        o_ref[...] += _tn(a, b_ref[...])

    (out,), got = _fused_call(
        body, fuse, name=name, out_shape=[jax.ShapeDtypeStruct((m, n), F32)], grid=(m // bm, r // tk),
        in_specs=[pl.BlockSpec((tk, bm), lambda i, k: (k, i)), pl.BlockSpec((tk, n), lambda i, k: (k, 0))],
        out_specs=[pl.BlockSpec((bm, n), lambda i, k: (i, 0))], scratch_shapes=[], operands=(lhs, rhs))
    return out, got


def _grad_w_ff1(s1b, da, fuse=NO_FUSE):
    r, d = s1b.shape
    wn = da.shape[1] // 4
    tk = _reduce_tile(r, True)

    def body(a_ref, b_ref, o_ref):
        @pl.when(pl.program_id(1) == 0)
        def _():
            o_ref[...] = jnp.zeros_like(o_ref)

        o_ref[0] += _tn(a_ref[...], b_ref[...])

    (out,), got = _fused_call(
        body, fuse, name="grad_w_ff1", out_shape=[jax.ShapeDtypeStruct((4, d, wn), F32)], grid=(4, r // tk),
        in_specs=[pl.BlockSpec((tk, d), lambda j, k: (k, 0)), pl.BlockSpec((tk, wn), lambda j, k: (k, j))],
        out_specs=[pl.BlockSpec((1, d, wn), lambda j, k: (j, 0, 0))], scratch_shapes=[], operands=(s1b, da))
    return out, got


def _grad_w_out(conv_of, gla_of, dt1b, fuse=NO_FUSE):
    r, n = dt1b.shape
    tk = _reduce_tile(r, True)

    def body(a_ref, b_ref, g_ref, o_ref):
        @pl.when(pl.program_id(1) == 0)
        def _():
            o_ref[...] = jnp.zeros_like(o_ref)

        @pl.when(pl.program_id(0) == 0)
        def _():
            o_ref[...] += _tn(a_ref[...], g_ref[...])

        @pl.when(pl.program_id(0) == 1)
        def _():
            o_ref[...] += _tn(b_ref[...], g_ref[...])

    lhs = pl.BlockSpec((tk, 512), lambda i, k: (k, 0))
    (out,), got = _fused_call(
        body, fuse, name="grad_w_out", out_shape=[jax.ShapeDtypeStruct((2 * 512, n), F32)], grid=(2, r // tk),
        in_specs=[lhs, lhs, pl.BlockSpec((tk, n), lambda i, k: (k, 0))],
        out_specs=[pl.BlockSpec((512, n), lambda i, k: (i, 0))], scratch_shapes=[],
        operands=(conv_of, gla_of, dt1b))
    return out, got


def _out_proj_bwd(dt1b, w_out, hc, cg, cb, fuse=NO_FUSE):
    r, d = dt1b.shape
    tm = _row_tile(r)

    def body(g_ref, w_ref, hc_ref, cg_ref, cb_ref, dhc_ref, dgla_ref, acc_ref):
        @pl.when(pl.program_id(0) == 0)
        def _():
            acc_ref[...] = jnp.zeros_like(acc_ref)

        gg = cg_ref[...]
        sums = [jnp.zeros((1, D_CONV), F32)] * 3
        for rs in _sub_rows(tm):
            dmix = _nt(g_ref[rs, :], w_ref[...])
            dgla_ref[rs, :] = dmix[:, D_CONV:]
            xh, rstd = _ln_stats(hc_ref[rs, :])
            nv = xh * gg + cb_ref[...]
            sig = _sigmoid(nv)
            dn = dmix[:, :D_CONV] * (sig * (1.0 + nv * (1.0 - sig)))
            dhc = _ln_bwd(dn * gg, xh, rstd)
            dhc_ref[rs, :] = dhc
            sums = [sums[0] + jnp.sum(dhc, axis=0, keepdims=True), sums[1] + jnp.sum(dn * xh, axis=0, keepdims=True),
                    sums[2] + jnp.sum(dn, axis=0, keepdims=True)]
        for k in range(3):
            acc_ref[k:k + 1, :] += sums[k]

    row = lambda n: pl.BlockSpec((tm, n), lambda i: (i, 0))
    vec = pl.BlockSpec((1, D_CONV), lambda i: (0, 0))
    res, got = _fused_call(
        body, fuse, name="out_proj_bwd",
        out_shape=[jax.ShapeDtypeStruct((r, D_CONV), F32), jax.ShapeDtypeStruct((r, 512), F32),
                   jax.ShapeDtypeStruct((8, D_CONV), F32)],
        grid=(r // tm,),
        in_specs=[row(d), pl.BlockSpec((d, d), lambda i: (0, 0)), row(D_CONV), vec, vec],
        out_specs=[row(D_CONV), row(512), pl.BlockSpec((8, D_CONV), lambda i: (0, 0))], scratch_shapes=[],
        operands=(dt1b, w_out, hc, cg, cb))
    return res[0], res[1], res[2], got


def _conv_bwd(dhc, u, conv_w, fuse=NO_FUSE):
    bsz, tp, _ = u.shape
    nchunk = tp // CHUNK
    win = CHUNK + 32
    nct = D_CONV // 128

    def body(dhc_ref, cv_ref, cg_ref, w_ref, dv_ref, dg_ref, dw_ref, h_scr, dhc_scr, hwin, dwin, dw_scr):
        h_scr[0:32, :] = jnp.zeros((32, 128), F32)
        h_scr[32:32 + tp, :] = cv_ref[0] * _sigmoid(cg_ref[0])
        dhc_scr[0:tp, :] = dhc_ref[0]
        dhc_scr[tp:tp + 32, :] = jnp.zeros((32, 128), F32)
        dw_scr[...] = jnp.zeros_like(dw_scr)

        def step(n, carry):
            r0 = pl.multiple_of(n * CHUNK, CHUNK)
            rows = pl.ds(r0, CHUNK)
            hwin[...] = h_scr[pl.ds(r0, win), :]
            dwin[...] = dhc_scr[pl.ds(r0, win), :]
            dcur = dwin[0:CHUNK, :]
            acc = jnp.zeros((CHUNK, 128), F32)
            for j in range(CONV_WIDTH):
                acc = acc + w_ref[j:j + 1, :] * dwin[30 - j:30 - j + CHUNK, :]
                prod = dcur * hwin[2 + j:2 + j + CHUNK, :]
                dw_scr[j * 8:(j + 1) * 8, :] += jnp.sum(prod.reshape(CHUNK // 8, 8, 128), axis=0)
            cg = cg_ref[0, rows, :]
            sig = _sigmoid(cg)
            rowid = n * CHUNK + lax.broadcasted_iota(jnp.int32, (CHUNK, 1), 0)
            dh = jnp.where(rowid >= PAD, acc, 0.0)
            dv_ref[0, rows, :] = (dh * sig).astype(BF16)
            dg_ref[0, rows, :] = (dh * cv_ref[0, rows, :] * sig * (1.0 - sig)).astype(BF16)
            return carry

        lax.fori_loop(0, nchunk, step, 0)
        dw_ref[0] = jnp.zeros((32, 128), F32)
        for j in range(CONV_WIDTH):
            dw_ref[0, j:j + 1, :] = jnp.sum(dw_scr[j * 8:(j + 1) * 8, :], axis=0, keepdims=True)

    blk = lambda off: pl.BlockSpec((1, tp, 128), lambda bi, t: (bi, 0, off // 128 + t))
    res, got = _fused_call(
        body, fuse, name="conv_bwd",
        out_shape=[jax.ShapeDtypeStruct((bsz, tp, D_CONV), BF16), jax.ShapeDtypeStruct((bsz, tp, D_CONV), BF16),
                   jax.ShapeDtypeStruct((bsz, 32, D_CONV), F32)],
        grid=(bsz, nct),
        in_specs=[blk(0), blk(C_VAL), blk(C_GATE), pl.BlockSpec((32, 128), lambda bi, t: (0, t))],
        out_specs=[blk(0), blk(0), pl.BlockSpec((1, 32, 128), lambda bi, t: (bi, 0, t))],
        scratch_shapes=[pltpu.VMEM((tp + 32, 128), F32), pltpu.VMEM((tp + 32, 128), F32),
                        pltpu.VMEM((win, 128), F32), pltpu.VMEM((win, 128), F32),
                        pltpu.VMEM((CONV_WIDTH * 8, 128), F32)],
        operands=(dhc, u, u, conv_w))
    return res[0], res[1], res[2], got


def _gla_bwd(dgla, u, o_pre, states, gup, gbias, gnorm, fuse=NO_FUSE):
    bsz, tp, _ = u.shape
    nchunk = tp // CHUNK
    nb = _gla_group(nchunk)

    def body(dy_ref, q_ref, k_ref, v_ref, r_ref, gd_ref, o_ref, st_ref, gup_ref, gb_ref, gn_ref,
             dq_ref, dk_ref, dv_ref, dr_ref, dgd_ref, dgup_ref, vec_ref, h_scr, gup_acc):
        tril, triu, hmask = _gla_consts(nb)
        h_scr[...] = jnp.zeros_like(h_scr)
        gup_acc[...] = jnp.zeros_like(gup_acc)
        gn = gn_ref[...]
        gupb = gup_ref[...]
        m = nb * CHUNK
        ngroup = nchunk // nb

        def group(i, carry):
            dbias, dgn = carry
            g = ngroup - 1 - i
            rows, valid, z, eb, enb, erest, dec, qe, ke, kd = _gla_group_terms(
                g, nb, q_ref, k_ref, gd_ref, gup_ref, gb_ref, tril)
            keb, kdb = ke.astype(BF16), kd.astype(BF16)
            dqe = jnp.zeros((nb, CHUNK, 128), F32)
            dke = jnp.zeros((nb, CHUNK, 128), F32)
            dkd = jnp.zeros((nb, CHUNK, 128), F32)
            ddec = jnp.zeros((nb, 1, 128), F32)
            for h in range(2):
                cols = slice(h * GLA_DV, (h + 1) * GLA_DV)
                o = o_ref[0, rows, cols]
                rh = r_ref[0, rows, cols]
                dy = dy_ref[0, rows, cols]
                rms = lax.rsqrt(jnp.mean(o * o, axis=-1, keepdims=True) + LN_EPS)
                nrm = o * rms
                sig = _sigmoid(rh)
                sw = rh * sig
                dr_ref[0, rows, cols] = (dy * nrm * gn * (sig * (1.0 + rh * (1.0 - sig)))).astype(BF16)
                dgn = dgn + jnp.sum(dy * nrm * sw, axis=0, keepdims=True)
                dn = dy * gn * sw
                do = rms * (dn - nrm * jnp.mean(dn * nrm, axis=-1, keepdims=True))
                dob = do.astype(BF16).reshape(nb, CHUNK, GLA_DV)
                qh = jnp.where(hmask[h], qe, 0.0).astype(BF16)
                vh = v_ref[0, rows, cols].astype(BF16).reshape(nb, CHUNK, GLA_DV)
                ht = h_scr[h]
                hts = [None] * nb
                for n in reversed(range(nb)):
                    hts[n] = ht
                    ht = dec[n] * ht + _tn(dob[n], qh[n])
                h_scr[h] = ht
                htf = jnp.stack(hts)
                htb = htf.astype(BF16)
                st = st_ref[0, h, pl.ds(g * nb, nb)]
                at = jnp.where(triu, _bnt(keb, qh), 0.0).astype(BF16)
                da = jnp.where(tril, _bnt(dob, vh), 0.0).astype(BF16)
                dat = jnp.where(triu, _bnt(vh, dob), 0.0).astype(BF16)
                dqe = dqe + jnp.where(hmask[h], _bnn(da, keb) + _bnn(dob, st.astype(BF16)), 0.0)
                dke = dke + _bnn(dat, qh)
                dv_ref[0, rows, cols] = (_bnn(at, dob) + _bnt(kdb, htb)).reshape(m, GLA_DV).astype(BF16)
                dkd = dkd + jnp.where(hmask[h], _bnn(vh, htb), 0.0)
                ddec = ddec + jnp.where(hmask[h], jnp.sum(htf * st, axis=1, keepdims=True), 0.0)
            dq_ref[0, rows, :] = (dqe * eb * Q_SCALE).reshape(m, 128).astype(BF16)
            dk_ref[0, rows, :] = (dke * enb + dkd * erest).reshape(m, 128).astype(BF16)
            db = dqe * qe - dke * ke - dkd * kd
            dblast = jnp.sum(dkd * kd, axis=1, keepdims=True) + ddec * dec
            lastrow = lax.broadcasted_iota(jnp.int32, (1, CHUNK, 1), 1) == CHUNK - 1
            db = db + jnp.where(lastrow, dblast, 0.0)
            dlg = _bnn(triu.astype(F32), db, precision=lax.Precision.HIGHEST).reshape(m, 128)
            dz = jnp.where(valid, dlg, 0.0) * (1.0 / GLA_TAU) * (1.0 - _sigmoid(z))
            dzb = dz.astype(BF16)
            dgd_ref[0, 0, rows, :] = _nt(dzb, gupb).astype(BF16)
            gup_acc[...] += _tn(gd_ref[0, rows, :].astype(BF16), dzb)
            return dbias + jnp.sum(dz, axis=0, keepdims=True), dgn

        zero = jnp.zeros((1, 128), F32)
        dbias, dgn = lax.fori_loop(0, ngroup, group, (zero, zero))
        dgup_ref[0] = gup_acc[...]
        vec_ref[0] = jnp.zeros((8, 128), F32)
        vec_ref[0, 0:1, :] = dbias
        vec_ref[0, 1:2, :] = dgn

    pair = lambda w, off: pl.BlockSpec((1, tp, w), lambda bi, p: (bi, 0, off // w + p))
    return _fused_call(
        body, fuse, name="gla_bwd",
        out_shape=[jax.ShapeDtypeStruct((bsz, tp, 256), BF16), jax.ShapeDtypeStruct((bsz, tp, 256), BF16),
                   jax.ShapeDtypeStruct((bsz, tp, 512), BF16), jax.ShapeDtypeStruct((bsz, tp, 512), BF16),
                   jax.ShapeDtypeStruct((bsz, 2, tp, 128), BF16), jax.ShapeDtypeStruct((bsz, 128, 256), F32),
                   jax.ShapeDtypeStruct((bsz, 8, 256), F32)],
        grid=(bsz, 2),
        in_specs=[pair(256, 0), pair(128, C_Q), pair(128, C_K), pair(256, C_V), pair(256, C_R),
                  pl.BlockSpec((1, tp, 128), lambda bi, p: (bi, 0, C_GD // 128)),
                  pair(256, 0),
                  pl.BlockSpec((1, 2, nchunk, GLA_DV, 128), lambda bi, p: (bi, p, 0, 0, 0)),
                  pl.BlockSpec((128, 128), lambda bi, p: (0, p)),
                  pl.BlockSpec((1, 128), lambda bi, p: (0, p)),
                  pl.BlockSpec((1, 128), lambda bi, p: (0, 0))],
        out_specs=[pair(128, 0), pair(128, 0), pair(256, 0), pair(256, 0),
                   pl.BlockSpec((1, 1, tp, 128), lambda bi, p: (bi, p, 0, 0)),
                   pl.BlockSpec((1, 128, 128), lambda bi, p: (bi, 0, p)),
                   pl.BlockSpec((1, 8, 128), lambda bi, p: (bi, 0, p))],
        scratch_shapes=[pltpu.VMEM((2, GLA_DV, 128), F32), pltpu.VMEM((128, 128), F32)],
        operands=(dgla, u, u, u, u, u, o_pre, states, gup, gbias, gnorm))


_DU_OFFSETS = (C_VAL, C_GATE, C_Q, C_K, C_V, C_R)
_DU_WIDTHS = (512, 512, 256, 256, 512, 512)


def _du_specs(tm, per, row_map):
    specs = [pl.BlockSpec((tm, w), row_map) for w in _DU_WIDTHS]
    for p in range(2):
        specs.append(pl.BlockSpec((1, 1, tm, 128), lambda *ix, p=p: (row_map(*ix)[0] // per, p, row_map(*ix)[0] % per, 0)))
    return specs


def _du_pieces(refs):
    out = [(off, ref[...]) for off, ref in zip(_DU_OFFSETS, refs[:6])]
    dgd = (refs[6][0, 0].astype(F32) + refs[7][0, 0].astype(F32)).astype(BF16)
    out.append((C_GD, dgd))
    return out


def _in_proj_bwd(pieces, dgd, w_int, dt1, tp, fuse=NO_FUSE):
    r, d = dt1.shape
    tm = _row_tile(tp)
    per = tp // tm

    def body(*refs):
        w_ref, dt_ref, o_ref = refs[8:]
        acc = ALPHA * dt_ref[...]
        for off, val in _du_pieces(refs[:8]):
            acc = acc + _nn(val, w_ref[off:off + val.shape[1], :])
        o_ref[...] = acc

    row = lambda i: (i, 0)
    (ds0,), got = _fused_call(
        body, fuse, name="in_proj_bwd", out_shape=[jax.ShapeDtypeStruct((r, d), F32)], grid=(r // tm,),
        in_specs=_du_specs(tm, per, row) + [pl.BlockSpec((D_IN_PAD, d), lambda i: (0, 0)), pl.BlockSpec((tm, d), row)],
        out_specs=[pl.BlockSpec((tm, d), row)], scratch_shapes=[], operands=(*pieces, dgd, dgd, w_int, dt1))
    return ds0, got


def _grad_w_in(pieces, dgd, s0b, tp, fuse=NO_FUSE):
    r, d = s0b.shape
    tk = _reduce_tile(tp, False)
    per = tp // tk

    def body(*refs):
        s_ref, o_ref = refs[8:]

        @pl.when(pl.program_id(0) == 0)
        def _():
            o_ref[...] = jnp.zeros_like(o_ref)

        s = s_ref[...]
        for off, val in _du_pieces(refs[:8]):
            o_ref[off:off + val.shape[1], :] += _tn(val, s)

    row = lambda k: (k, 0)
    (out,), got = _fused_call(
        body, fuse, name="grad_w_in", out_shape=[jax.ShapeDtypeStruct((D_IN_PAD, d), F32)], grid=(r // tk,),
        in_specs=_du_specs(tk, per, row) + [pl.BlockSpec((tk, d), row)],
        out_specs=[pl.BlockSpec((D_IN_PAD, d), lambda k: (0, 0))], scratch_shapes=[],
        operands=(*pieces, dgd, dgd, s0b))
    return out, got


def _ln_in_bwd(ds0, x, meta, g):
    bsz, s, d = x.shape
    tp = s + HEAD
    nh = 2
    sh = s // nh
    rc = min(256, sh)

    def body(ds_ref, x_ref, meta_ref, g_ref, gx_ref, dm_ref, vec_ref):
        h = pl.program_id(1)
        gg = g_ref[...]

        @pl.when(h == 0)
        def _():
            mh, mr = _ln_stats(meta_ref[...])
            dsm = ds_ref[0, PAD:HEAD, :]
            dm_ref[0] = _ln_bwd(dsm * gg, mh, mr)
            vec_ref[0] = jnp.zeros((8, d), F32)
            vec_ref[0, 0:1, :] = jnp.sum(dsm * mh, axis=0, keepdims=True)
            vec_ref[0, 1:2, :] = jnp.sum(dsm, axis=0, keepdims=True)

        def step(i, carry):
            sg, sb = carry
            dst = pl.ds(pl.multiple_of(i * rc, rc), rc)
            src = pl.ds(pl.multiple_of(HEAD + h * sh + i * rc, 64), rc)
            xh, rstd = _ln_stats(x_ref[0, dst, :])
            dsv = ds_ref[0, src, :]
            gx_ref[0, dst, :] = _ln_bwd(dsv * gg, xh, rstd)
            return sg + jnp.sum(dsv * xh, axis=0, keepdims=True), sb + jnp.sum(dsv, axis=0, keepdims=True)

        zero = jnp.zeros((1, d), F32)
        sg, sb = lax.fori_loop(0, sh // rc, step, (zero, zero))
        vec_ref[0, 0:1, :] += sg
        vec_ref[0, 1:2, :] += sb

    return pl.pallas_call(
        body, name="ln_in_bwd",
        out_shape=[jax.ShapeDtypeStruct((bsz, s, d), F32), jax.ShapeDtypeStruct((bsz, N_META, d), F32),
                   jax.ShapeDtypeStruct((bsz, 8, d), F32)],
        grid=(bsz, nh),
        in_specs=[pl.BlockSpec((1, tp, d), lambda bi, hi: (bi, 0, 0)),
                  pl.BlockSpec((1, sh, d), lambda bi, hi: (bi, hi, 0)),
                  pl.BlockSpec((N_META, d), lambda bi, hi: (0, 0)),
                  pl.BlockSpec((1, d), lambda bi, hi: (0, 0))],
        out_specs=[pl.BlockSpec((1, sh, d), lambda bi, hi: (bi, hi, 0)),
                   pl.BlockSpec((1, N_META, d), lambda bi, hi: (bi, 0, 0)),
                   pl.BlockSpec((1, 8, d), lambda bi, hi: (bi, 0, 0))],
        compiler_params=_params(("parallel", "arbitrary")),
    )(ds0, x, meta, g)


def _rows128(a):
    return a.reshape(-1, 128)


def kernel(x, meta_tokens, ln_in_g, ln_in_b, w_in, conv_w, conv_b, conv_ln_g, conv_ln_b, gate_up, gate_bias, gla_norm_g, w_out, ln1_g, ln1_b, w_ff1, w_ff2, ln2_g, ln2_b, loss_target, m_meta_tokens, m_ln_in_g, m_ln_in_b, m_w_in, m_conv_w, m_conv_b, m_conv_ln_g, m_conv_ln_b, m_gate_up, m_gate_bias, m_gla_norm_g, m_w_out, m_ln1_g, m_ln1_b, m_w_ff1, m_w_ff2, m_ln2_g, m_ln2_b, v_meta_tokens, v_ln_in_g, v_ln_in_b, v_w_in, v_conv_w, v_conv_b, v_conv_ln_g, v_conv_ln_b, v_gate_up, v_gate_bias, v_gla_norm_g, v_w_out, v_ln1_g, v_ln1_b, v_w_ff1, v_w_ff2, v_ln2_g, v_ln2_b):
    bsz, seq, d = x.shape
    tp = seq + HEAD
    r = bsz * tp
    xi, yi, ci = _mesh_pos()
    chip = 2 * xi + yi
    c_arr = jnp.reshape(ci, (1,)).astype(jnp.int32)

    sh_in = D_IN // 4
    shard_in = jnp.pad(w_in[0].T.astype(BF16), ((0, D_IN_PAD // 4 - sh_in), (0, 0)))
    shard_w1, shard_wout, shard_w2 = w_ff1[0].astype(BF16), w_out[0].astype(BF16), w_ff2[0].astype(BF16)
    small_w = jnp.concatenate([_rows128(meta_tokens), _rows128(conv_w[0]), _rows128(gate_up[0])], axis=0)
    ln_in_g2, ln_in_b2 = ln_in_g.reshape(1, d), ln_in_b.reshape(1, d)

    s0, s0b, ((g_int,), (g_small,)) = _ln_in_fwd(
        x, ln_in_g2, ln_in_b2, [("gather", [(shard_in, 0, D_IN_PAD // 4, None)]), ("small", [small_w])])
    (g_int,) = _place_own([g_int], [shard_in])
    (g_small,) = _place_own([g_small], [small_w])
    n_meta_rows, n_cw_rows = N_META * 256 // 128, CONV_WIDTH * 128 // 128
    meta_full = jnp.concatenate([g_small[j, :n_meta_rows].reshape(N_META, 256) for j in range(4)], axis=1)
    convw_full = jnp.concatenate(
        [g_small[j, n_meta_rows:n_meta_rows + n_cw_rows].reshape(CONV_WIDTH, 128) for j in range(4)], axis=1)
    gup_full = jnp.concatenate(
        [g_small[j, n_meta_rows + n_cw_rows:].reshape(GLA_RANK, 64) for j in range(4)], axis=1)
    convw_p = jnp.pad(convw_full, ((0, 1), (0, 0)))
    gup_p = jnp.pad(gup_full, ((0, 128 - GLA_RANK), (0, 0))).astype(BF16)
    s0, s0b = _ln_meta(s0, s0b, meta_full, ln_in_g2, ln_in_b2)
    w_int = jnp.pad(g_int[:, :sh_in].reshape(D_IN, d), ((0, D_IN_PAD - D_IN), (0, 0)))
    s0f, s0bf = s0.reshape(r, d), s0b.reshape(r, d)
    u, ((w1_buf,),) = _in_proj(s0bf, w_int, [("gather", [(shard_w1, 0, 640, None)])])
    (w1_buf,) = _place_own([w1_buf], [shard_w1])
    u3 = u.reshape(bsz, tp, D_IN_PAD)
    hc, ((w1_buf,),) = _conv_fwd(u3, convw_p, conv_b, [("gather", [(shard_w1, 640, 384, w1_buf)])])
    hc = hc.reshape(r, D_CONV)
    gla_out, o_pre, states, ((g_wout, w2_buf),) = _gla_fwd(
        u3, gup_p, gate_bias, gla_norm_g, [("gather", [(shard_wout, 0, 256, None), (shard_w2, 0, 128, None)])])
    g_wout, w2_buf = _place_own([g_wout, w2_buf], [shard_wout, shard_w2])
    wout = g_wout.reshape(d, d)
    gla_of = gla_out.reshape(r, 512)
    conv_of, xhat1, rstd1, s1b, ((w2_buf,),) = _out_proj_ln1(
        hc, gla_of, wout, s0f, conv_ln_g, conv_ln_b, ln1_g, ln1_b, [("gather", [(shard_w2, 128, 384, w2_buf)])])
    w1 = w1_buf
    ra, ((w2_buf,),) = _ffn1(s1b, w1, [("gather", [(shard_w2, 512, 512, w2_buf)])])
    w2 = w2_buf.reshape(D_FF, d)
    dt2, dt2b, acc2 = _ffn2_ln2_loss(ra, w2, xhat1, ln1_g, ln1_b, ln2_g, ln2_b, loss_target, tp)

    def add_pair(g, got, splits=None):
        return _add_pair(g, got, c_arr, splits)

    da = _ffn_bwd_da(dt2b, w2, ra)
    dt1, dt1b, acc1 = _ffn_bwd_ln1(da, w1, dt2, xhat1, rstd1, ln1_g)
    g_w2, _ = _matmul_tn(ra, dt2b, 1024, square_lhs=True, name="grad_w_ff2")
    big_w2 = g_w2.reshape(4, D_FF // 4, d)
    big_w1, ((pair_w2,),) = _grad_w_ff1(s1b, da, [("pair", [big_w2])])
    dhc, dgla, cacc, ((pair_w1,),) = _out_proj_bwd(dt1b, wout, hc, conv_ln_g, conv_ln_b, [("pair", [big_w1])])
    (part_w1,), (part_w2a, part_w2b) = add_pair(big_w1, pair_w1), add_pair(big_w2, pair_w2, [512, 512])
    g_wout, _ = _grad_w_out(conv_of, gla_of, dt1b)
    big_wout = g_wout.reshape(4, d // 4, d)
    dcv, dcg, dcw, ((chip_w1,), (pair_wout,)) = _conv_bwd(
        dhc.reshape(bsz, tp, D_CONV), u3, convw_p, [("exchange", [part_w1]), ("pair", [big_wout])])
    (part_wout,) = add_pair(big_wout, pair_wout)
    (dq, dk, dv, dr, dgd, dgup, gvec), ((chip_w2a,),) = _gla_bwd(
        dgla.reshape(bsz, tp, 512), u3, o_pre, states, gup_p, gate_bias, gla_norm_g, [("exchange", [part_w2a])])
    pieces = [a.reshape(r, a.shape[-1]) for a in (dcv, dcg, dq, dk, dv, dr)]
    g_wint, ((chip_w2b, chip_wout),) = _grad_w_in(
        pieces, dgd, s0bf, tp, [("exchange", [part_w2b, part_wout])])
    big_win = jnp.stack([g_wint[j * sh_in:(j + 1) * sh_in] for j in range(4)])
    (half_w1, half_w2), ((pair_win,),) = _add_chips_many(
        [(big_w1, pair_w1, [chip_w1]), (big_w2, pair_w2, [chip_w2a, chip_w2b])], [("pair", [big_win])])
    ds0, ((chip_win,), (sib_w1, sib_w2)) = _in_proj_bwd(
        pieces, dgd, w_int, dt1, tp, [("exchange", add_pair(big_win, pair_win)), ("swap", [half_w1, half_w2])])
    grad_x, dmeta, lvec = _ln_in_bwd(ds0.reshape(bsz, tp, d), x, meta_full, ln_in_g2)

    small_pack = _pack_small(acc2, acc1, lvec, cacc, gvec, dcw, dgup, dmeta)
    (half_win, half_wout), ((small_all,),) = _add_chips_many(
        [(big_win, pair_win, [chip_win]), (big_wout, pair_wout, [chip_wout])], [("all", [small_pack])],
        late_relay=True)
    red, loss = _sum_small(small_all)
    sib_win, sib_wout = _pair_swap([half_win, half_wout])
    halves = {"w_in": (half_win, sib_win), "w_out": (half_wout, sib_wout), "w_ff1": (half_w1, sib_w1),
              "w_ff2": (half_w2, sib_w2)}

    grads = {}
    weights = dict(meta_tokens=meta_tokens, ln_in_g=ln_in_g, ln_in_b=ln_in_b, w_in=w_in, conv_w=conv_w, conv_b=conv_b,
                   conv_ln_g=conv_ln_g, conv_ln_b=conv_ln_b, gate_up=gate_up, gate_bias=gate_bias,
                   gla_norm_g=gla_norm_g, w_out=w_out, ln1_g=ln1_g, ln1_b=ln1_b, w_ff1=w_ff1, w_ff2=w_ff2,
                   ln2_g=ln2_g, ln2_b=ln2_b)
    moms = dict(meta_tokens=(m_meta_tokens, v_meta_tokens), ln_in_g=(m_ln_in_g, v_ln_in_g),
                ln_in_b=(m_ln_in_b, v_ln_in_b), w_in=(m_w_in, v_w_in), conv_w=(m_conv_w, v_conv_w),
                conv_b=(m_conv_b, v_conv_b), conv_ln_g=(m_conv_ln_g, v_conv_ln_g),
                conv_ln_b=(m_conv_ln_b, v_conv_ln_b), gate_up=(m_gate_up, v_gate_up),
                gate_bias=(m_gate_bias, v_gate_bias), gla_norm_g=(m_gla_norm_g, v_gla_norm_g),
                w_out=(m_w_out, v_w_out), ln1_g=(m_ln1_g, v_ln1_g), ln1_b=(m_ln1_b, v_ln1_b),
                w_ff1=(m_w_ff1, v_w_ff1), w_ff2=(m_w_ff2, v_w_ff2), ln2_g=(m_ln2_g, v_ln2_g),
                ln2_b=(m_ln2_b, v_ln2_b))
    names = list(weights)
    big_names = ("w_in", "w_out", "w_ff1", "w_ff2")
    delta, new_m, new_v = {}, {}, {}
    for k in big_names:
        to2d = (lambda a: a[0].T) if k == "w_in" else (lambda a: a[0])
        back = (lambda a: a.T[None]) if k == "w_in" else (lambda a: a[None])
        res = _adamw_halves(to2d(weights[k]), *halves[k], to2d(moms[k][0]), to2d(moms[k][1]), c_arr)
        grads[k], delta[k], new_m[k], new_v[k] = [back(a) for a in res]
    small_names = [k for k in names if k not in big_names]
    two = lambda a: a.reshape(-1, a.shape[-1])

    def my_cols(full, rows, width):
        return lax.dynamic_slice(full, (0, chip * width), (rows, width))

    sharded = {
        "meta_tokens": my_cols(red[SMALL_AT["meta_tokens"][0]:SMALL_AT["meta_tokens"][0] + N_META], N_META, 256),
        "conv_w": my_cols(red[SMALL_CONV_W:SMALL_CONV_W + 16].reshape(32, D_CONV), CONV_WIDTH, 128),
        "gate_up": my_cols(red[SMALL_GATE_UP:SMALL_GATE_UP + 4].reshape(GLA_RANK, 256), GLA_RANK, 64)}
    upd = _adamw_small(red, sharded, {k: (two(weights[k]), two(moms[k][0]), two(moms[k][1])) for k in small_names})
    for k in small_names:
        shp = weights[k].shape
        grads[k], delta[k], new_m[k], new_v[k] = [a.reshape(shp) for a in upd[k]]
    loss = loss.reshape(())

    return (loss, grad_x, *[grads[k] for k in names], *[delta[k] for k in names],
            *[new_m[k] for k in names], *[new_v[k] for k in names])
```

```python
import functools

import jax
import jax.numpy as jnp
from jax import lax
from jax.experimental import pallas as pl
from jax.experimental.pallas import tpu as pltpu

F32 = jnp.float32
BF16 = jnp.bfloat16

D_MODEL = 1024
N_META = 16
D_CONV = 512
CONV_WIDTH = 31
GLA_HEADS = 4
GLA_DV = 128
GLA_DK = 64
GLA_RANK = 16
GLA_TAU = 16.0
CHUNK = 64
D_FF = 4096
LN_EPS = 1e-5
ALPHA = 2.0 ** 0.25
D_IN = 2576
D_IN_PAD = 2688
PAD = CHUNK - N_META
HEAD = PAD + N_META
Q_SCALE = GLA_DK ** -0.5
ADAM_LR, ADAM_B1, ADAM_B2, ADAM_EPS, ADAM_WD, ADAM_STEP = 0.001, 0.9, 0.999, 1e-08, 0.01, 10
HALF = D_MODEL // 2
VMEM_LIMIT = 56 * 1024 * 1024
MESH = pl.DeviceIdType.MESH

C_VAL, C_GATE, C_Q, C_K, C_V, C_R, C_GD = 0, 512, 1024, 1280, 1536, 2048, 2560

SMALL_AT = {"loss": (0, 1, 0, 1024), "ln_in_g": (1, 1, 0, 1024), "ln_in_b": (2, 1, 0, 1024), "ln1_g": (3, 1, 0, 1024),
            "ln1_b": (4, 1, 0, 1024), "ln2_g": (5, 1, 0, 1024), "ln2_b": (6, 1, 0, 1024), "conv_b": (7, 1, 0, 512),
            "conv_ln_g": (7, 1, 512, 512), "conv_ln_b": (8, 1, 0, 512), "gate_bias": (8, 1, 512, 256),
            "gla_norm_g": (8, 1, 768, 128), "meta_tokens": (40, 16, 0, 1024)}
SMALL_CONV_W, SMALL_GATE_UP = 16, 32
SMALL_ROWS = 56


def _params(sem=None, **kw):
    return pltpu.CompilerParams(dimension_semantics=sem, vmem_limit_bytes=VMEM_LIMIT, **kw)


def _row_tile(tp):
    for t in (704, 352, 192, 64):
        if tp % t == 0:
            return t
    raise ValueError(tp)


def _reduce_tile(tp, big):
    for t in ((2112, 1056, 704) if big else (1056, 704)) + (352, 192, 64):
        if tp % t == 0:
            return t
    raise ValueError(tp)


def _sub_rows(tm):
    return [slice(0, tm)]


def _dot(a, b, dims, precision=None):
    return lax.dot_general(a, b, (dims, ((), ())), preferred_element_type=F32, precision=precision)


def _nn(a, b, **kw):
    return _dot(a, b, ((1,), (0,)), **kw)


def _nt(a, b, **kw):
    return _dot(a, b, ((1,), (1,)), **kw)


def _tn(a, b, **kw):
    return _dot(a, b, ((0,), (0,)), **kw)


def _sigmoid(x):
    return 1.0 / (1.0 + jnp.exp(-x))


def _log_sigmoid(z):
    return jnp.minimum(z, 0.0) - jnp.log(1.0 + jnp.exp(-jnp.abs(z)))


def _ln_stats(t):
    mu = jnp.mean(t, axis=-1, keepdims=True)
    d = t - mu
    var = jnp.mean(d * d, axis=-1, keepdims=True)
    rstd = lax.rsqrt(var + LN_EPS)
    return d * rstd, rstd


def _ln_bwd(dxhat, xhat, rstd):
    m1 = jnp.mean(dxhat, axis=-1, keepdims=True)
    m2 = jnp.mean(dxhat * xhat, axis=-1, keepdims=True)
    return rstd * (dxhat - m1 - xhat * m2)


def _mesh_pos():
    return lax.axis_index("x"), lax.axis_index("y"), lax.axis_index("c")


ANY = pl.BlockSpec(memory_space=pl.ANY)


def _gather_sems(n):
    return [pltpu.SemaphoreType.DMA((n, 3))] * 4


def _gather_steps(ins, outs, sems, ranges=None):
    n = len(ins)
    send, recv, fsend, frecv = sems
    ranges = ranges or [(0, ref.shape[0]) for ref in ins]
    x, y, c = _mesh_pos()
    me = 2 * x + y
    sibling = (x, y, 1 - c)
    chips = [(1 - x, y), (x, 1 - y), (1 - x, 1 - y)]
    chip_idx = [2 * px + py for px, py in chips]
    mine = [pl.ds(pl.multiple_of(r0 + c * (nr // 2), 16), nr // 2) for r0, nr in ranges]
    other = [pl.ds(pl.multiple_of(r0 + (1 - c) * (nr // 2), 16), nr // 2) for r0, nr in ranges]
    pairs = [(a, k) for a in range(n) for k in range(3)]

    def ici(a, k, slab):
        return pltpu.make_async_remote_copy(
            src_ref=ins[a].at[mine[a], :], dst_ref=outs[a].at[slab, mine[a], :],
            send_sem=send.at[a, k], recv_sem=recv.at[a, k], device_id=(*chips[k], c), device_id_type=MESH)

    def forward(a, k, rows):
        blk = outs[a].at[chip_idx[k], rows[a], :]
        return pltpu.make_async_remote_copy(
            src_ref=blk, dst_ref=blk, send_sem=fsend.at[a, k], recv_sem=frecv.at[a, k],
            device_id=sibling, device_id_type=MESH)

    def start():
        for a, k in pairs:
            ici(a, k, me).start()

    def relay():
        for a, k in pairs:
            ici(a, k, chip_idx[k]).wait_recv()
            forward(a, k, mine).start()

    def finish():
        for a, k in pairs:
            forward(a, k, other).wait_recv()
        for a, k in pairs:
            ici(a, k, me).wait_send()
            forward(a, k, mine).wait_send()

    return start, relay, finish


def _place_own(gathered, shards):
    chip = 2 * lax.axis_index("x") + lax.axis_index("y")
    return [lax.dynamic_update_slice(g, s[None], (chip, 0, 0)) for g, s in zip(gathered, shards)]


def _gather_small_steps(ins, outs, sems):
    send, recv = sems
    x, y, c = _mesh_pos()
    chips = [(1 - x, y), (x, 1 - y), (1 - x, 1 - y)]

    def copy(a, k, slot):
        return pltpu.make_async_remote_copy(
            src_ref=ins[a], dst_ref=outs[a].at[slot], send_sem=send.at[a, k], recv_sem=recv.at[a, k],
            device_id=(*chips[k], c), device_id_type=MESH)

    def start():
        for a in range(len(ins)):
            for k in range(3):
                copy(a, k, 2 * x + y).start()

    def finish():
        for a in range(len(ins)):
            for k, (px, py) in enumerate(chips):
                copy(a, k, 2 * px + py).wait_recv()
            for k in range(3):
                copy(a, k, 2 * x + y).wait_send()

    return start, finish


def _pair_swap_steps(ins, outs, sems):
    send, recv = sems
    x, y, c = _mesh_pos()
    cps = [pltpu.make_async_remote_copy(
        src_ref=ins[a], dst_ref=outs[a], send_sem=send.at[a], recv_sem=recv.at[a],
        device_id=(x, y, 1 - c), device_id_type=MESH) for a in range(len(ins))]

    def start():
        for cp in cps:
            cp.start()

    def finish():
        for cp in cps:
            cp.wait()

    return start, finish


def _pair_exchange_steps(ins, outs, sems):
    send, recv = sems
    x, y, c = _mesh_pos()
    other = pl.ds(pl.multiple_of((1 - c) * HALF, 128), HALF)
    cps = [pltpu.make_async_remote_copy(
        src_ref=ins[a].at[:, :, other], dst_ref=outs[a], send_sem=send.at[a], recv_sem=recv.at[a],
        device_id=(x, y, 1 - c), device_id_type=MESH) for a in range(len(ins))]

    def start():
        for cp in cps:
            cp.start()

    def finish():
        for cp in cps:
            cp.wait()

    return start, finish


def _pair_exchange_shapes(grads):
    return [jax.ShapeDtypeStruct(g.shape[:2] + (HALF,), g.dtype) for g in grads]


NO_FUSE = ()


def _fuse_plan(kind, items):
    n = len(items)
    if kind == "gather":
        shards = [it[0] for it in items]
        bufs = [it[3] for it in items if it[3] is not None]
        alias, b = {}, 0
        for a, it in enumerate(items):
            if it[3] is not None:
                alias[n + b] = a
                b += 1
        ranges = [(it[1], it[2]) for it in items]
        return (shards + bufs, [jax.ShapeDtypeStruct((4,) + s.shape, s.dtype) for s in shards], alias, _gather_sems(n),
                lambda i, o, s: _gather_steps(i[:n], o, s, ranges))
    if kind == "exchange":
        return list(items), _chip_exchange_shapes(items), {}, _chip_exchange_sems(n), _chip_exchange_steps
    if kind == "small":
        return (list(items), [jax.ShapeDtypeStruct((4,) + s.shape, s.dtype) for s in items], {},
                _chip_exchange_sems(n), _gather_small_steps)
    if kind == "all":
        (pack,) = items
        return ([pack], [jax.ShapeDtypeStruct((8 * pack.shape[0], pack.shape[1]), pack.dtype)], {},
                [pltpu.SemaphoreType.DMA((7,)), pltpu.SemaphoreType.DMA((7,)), pltpu.SemaphoreType.DMA((1,))],
                _allgather_all_steps)
    pair_sems = [pltpu.SemaphoreType.DMA((n,)), pltpu.SemaphoreType.DMA((n,))]
    if kind == "swap":
        return (list(items), [jax.ShapeDtypeStruct(h.shape, h.dtype) for h in items], {}, pair_sems, _pair_swap_steps)
    return list(items), _pair_exchange_shapes(items), {}, pair_sems, _pair_exchange_steps


def _fused_call(body, fuse, *, name, grid, in_specs, out_specs, out_shape, scratch_shapes, operands,
                late_relay=False):
    plans = [_fuse_plan(kind, list(items)) for kind, items in fuse if len(items)]
    n_in, n_out, n_s = len(in_specs), len(out_shape), len(scratch_shapes)
    comm = [a for p in plans for a in p[0]]
    shapes = [s for p in plans for s in p[1]]
    nc, no = len(comm), len(shapes)
    aliases, i_at, o_at = {}, n_in, n_out
    for p in plans:
        for i, o in p[2].items():
            aliases[i_at + i] = o_at + o
        i_at, o_at = i_at + len(p[0]), o_at + len(p[1])

    def wrapped(*refs):
        o0 = n_in + nc
        s0 = o0 + n_out + no
        i_at, o_at, sem_at, steps = n_in, o0 + n_out, s0 + n_s, []
        for p in plans:
            steps.append(p[4](refs[i_at:i_at + len(p[0])], refs[o_at:o_at + len(p[1])], refs[sem_at:sem_at + len(p[3])]))
            i_at, o_at, sem_at = i_at + len(p[0]), o_at + len(p[1]), sem_at + len(p[3])
        first, last = _grid_ends(grid)
        for st in steps:
            pl.when(first)(st[0])
        if not late_relay:
            for st in steps:
                for mid in st[1:-1]:
                    pl.when(last)(mid)
        body(*refs[:n_in], *refs[o0:o0 + n_out], *refs[s0:s0 + n_s])
        for st in steps:
            for step in (st[1:] if late_relay else st[-1:]):
                pl.when(last)(step)

    res = pl.pallas_call(
        wrapped if plans else body, name=name, grid=grid, in_specs=list(in_specs) + [ANY] * nc,
        out_specs=list(out_specs) + [ANY] * no, out_shape=list(out_shape) + shapes,
        scratch_shapes=list(scratch_shapes) + [s for p in plans for s in p[3]], input_output_aliases=aliases,
        compiler_params=_params(("arbitrary",) * len(grid)))(*operands, *comm)
    outs, got, results = list(res[:n_out]), list(res[n_out:]), []
    for p in plans:
        results.append(got[:len(p[1])])
        got = got[len(p[1]):]
    return outs, results


def _chip_exchange_sems(n):
    return [pltpu.SemaphoreType.DMA((n, 3))] * 2


def _chip_exchange_shapes(parts):
    return [jax.ShapeDtypeStruct((3,) + p.shape[1:], p.dtype) for p in parts]


def _chip_exchange_steps(ins, outs, sems):
    send, recv = sems
    x, y, c = _mesh_pos()
    chips = [(1 - x, y), (x, 1 - y), (1 - x, 1 - y)]
    cps = [pltpu.make_async_remote_copy(
        src_ref=ins[a].at[2 * px + py], dst_ref=outs[a].at[k], send_sem=send.at[a, k], recv_sem=recv.at[a, k],
        device_id=(px, py, c), device_id_type=MESH) for a in range(len(ins)) for k, (px, py) in enumerate(chips)]

    def start():
        for cp in cps:
            cp.start()

    def finish():
        for cp in cps:
            cp.wait()

    return start, finish


def _pair_swap(halves):
    n = len(halves)

    def body(*refs):
        start, finish = _pair_swap_steps(refs[:n], refs[n:2 * n], refs[2 * n:])
        start()
        finish()

    return pl.pallas_call(
        body, name="grad_pair_swap",
        out_shape=[jax.ShapeDtypeStruct(h.shape, h.dtype) for h in halves],
        in_specs=[ANY] * n, out_specs=[ANY] * n,
        scratch_shapes=[pltpu.SemaphoreType.DMA((n,)), pltpu.SemaphoreType.DMA((n,))],
    )(*halves)


def _allgather_all_steps(ins, outs, sems):
    send_sems, recv_sems, local_sem = sems
    x_ref, out_ref = ins[0], outs[0]
    m_per = x_ref.shape[0]
    x, y, c = _mesh_pos()
    me, sibling = (x, y, c), (x, y, 1 - c)
    chips = [(1 - x, y), (x, 1 - y), (1 - x, 1 - y)]

    def rows(px, py, pc):
        return out_ref.at[pl.ds(pl.multiple_of((4 * px + 2 * py + pc) * m_per, 8), m_per), :]

    def copy(k, block, to, src=None):
        return pltpu.make_async_remote_copy(
            src_ref=rows(*block) if src is None else src, dst_ref=rows(*block),
            send_sem=send_sems.at[k], recv_sem=recv_sems.at[k], device_id=to, device_id_type=MESH)

    mine = pltpu.make_async_copy(x_ref, rows(*me), local_sem.at[0])
    first = [copy(0, me, sibling, src=x_ref)]
    first += [copy(1 + j, me, (*chip, c), src=x_ref) for j, chip in enumerate(chips)]
    passed = [copy(4 + j, (*chip, c), sibling) for j, chip in enumerate(chips)]

    def start():
        for cp in first:
            cp.start()
        mine.start(priority=1)

    def relay():
        for j, chip in enumerate(chips):
            copy(1 + j, (*chip, c), me).wait_recv()
            passed[j].start()

    def finish():
        copy(0, sibling, me).wait_recv()
        for j, chip in enumerate(chips):
            copy(4 + j, (*chip, 1 - c), me).wait_recv()
        for cp in first + passed:
            cp.wait_send()
        mine.wait()

    return start, relay, finish


def _add_pair(g, got, c_arr, splits=None):
    _, rows, _ = g.shape
    splits = splits or [rows]

    def body(c_ref, g_ref, r_ref, *o_refs):
        at = 0
        for o_ref, n in zip(o_refs, splits):
            o_ref[...] = (g_ref[:, at:at + n, :] + r_ref[:, at:at + n, :]).astype(BF16)
            at += n

    return pl.pallas_call(
        body, name="grad_add_pair", out_shape=[jax.ShapeDtypeStruct((4, n, HALF), BF16) for n in splits],
        grid_spec=pltpu.PrefetchScalarGridSpec(
            num_scalar_prefetch=1, grid=(4,),
            in_specs=[pl.BlockSpec((1, rows, HALF), lambda j, c: (j, 0, c[0])),
                      pl.BlockSpec((1, rows, HALF), lambda j, c: (j, 0, 0))],
            out_specs=[pl.BlockSpec((1, n, HALF), lambda j, c: (j, 0, 0)) for n in splits]),
        compiler_params=_params(("arbitrary",)),
    )(c_arr, g, got)


def _add_chips_many(items, fuse=NO_FUSE, late_relay=False):
    n = len(items)
    rows = [it[0].shape[1] for it in items]
    n_got = [len(it[2]) for it in items]

    def body(*refs):
        g_refs, p_refs = refs[:n], refs[n:2 * n]
        got_refs = refs[2 * n:2 * n + sum(n_got)]
        o_refs = refs[2 * n + sum(n_got):3 * n + sum(n_got)]
        scr = refs[3 * n + sum(n_got):]
        x, y, c = _mesh_pos()
        chip = 2 * x + y
        mine = pl.ds(pl.multiple_of(c * HALF, 128), HALF)
        copies = []
        for a in range(n):
            copies.append((pltpu.make_async_copy(g_refs[a].at[chip, :, mine], scr[2 * a], scr[2 * n].at[2 * a]),
                           pltpu.make_async_copy(p_refs[a].at[chip], scr[2 * a + 1], scr[2 * n].at[2 * a + 1])))
        for cg, cp in copies:
            cg.start()
            cp.start()
        at_ref = 0
        for a in range(n):
            copies[a][0].wait()
            copies[a][1].wait()
            at = 0
            for r_ref in got_refs[at_ref:at_ref + n_got[a]]:
                k = r_ref.shape[1]
                own = scr[2 * a][at:at + k, :] + scr[2 * a + 1][at:at + k, :]
                o_refs[a][at:at + k, :] = ((own + r_ref[0].astype(F32)) + r_ref[1].astype(F32)) + r_ref[2].astype(F32)
                at += k
            at_ref += n_got[a]

    gots = [t for it in items for t in it[2]]
    outs, got = _fused_call(
        body, fuse, name="grad_add_chips_many", grid=(1,),
        in_specs=[ANY] * (2 * n) + [pl.BlockSpec(t.shape, lambda i: (0, 0, 0)) for t in gots],
        out_specs=[pl.BlockSpec((r, HALF), lambda i: (0, 0)) for r in rows],
        out_shape=[jax.ShapeDtypeStruct((r, HALF), F32) for r in rows],
        scratch_shapes=[pltpu.VMEM((r, HALF), F32) for r in rows for _ in range(2)] + [pltpu.SemaphoreType.DMA((2 * n,))],
        operands=(*[it[0] for it in items], *[it[1] for it in items], *gots), late_relay=late_relay)
    return outs, got


def _pack_small(acc2, acc1, lvec, cacc, gvec, dcw, dgup, dmeta):
    bsz = lvec.shape[0]

    def body(a2_ref, a1_ref, lv_ref, ca_ref, gv_ref, cw_ref, gu_ref, dm_ref, o_ref):
        def put(name, val):
            r0, nr, l0, nl = SMALL_AT[name]
            o_ref[r0:r0 + nr, l0:l0 + nl] = val

        def put_folded(row0, val, per_row):
            w = val.shape[1]
            for i in range(val.shape[0]):
                o_ref[row0 + i // per_row:row0 + i // per_row + 1, (i % per_row) * w:(i % per_row + 1) * w] = val[i:i + 1]

        over_b = lambda f: functools.reduce(lambda a, b: a + b, [f(b) for b in range(bsz)])
        o_ref[...] = jnp.zeros_like(o_ref)
        put("loss", a2_ref[0:1, :])
        put("ln2_g", a2_ref[1:2, :])
        put("ln2_b", a2_ref[2:3, :])
        put("ln1_g", a1_ref[0:1, :])
        put("ln1_b", a1_ref[1:2, :])
        put("ln_in_g", over_b(lambda b: lv_ref[b, 0:1, :]))
        put("ln_in_b", over_b(lambda b: lv_ref[b, 1:2, :]))
        put("conv_b", ca_ref[0:1, :])
        put("conv_ln_g", ca_ref[1:2, :])
        put("conv_ln_b", ca_ref[2:3, :])
        put("gate_bias", over_b(lambda b: gv_ref[b, 0:1, :]))
        put("gla_norm_g", over_b(lambda b: gv_ref[b, 1:2, 0:128] + gv_ref[b, 1:2, 128:256]))
        put_folded(SMALL_CONV_W, over_b(lambda b: cw_ref[b]), D_MODEL // D_CONV)
        put_folded(SMALL_GATE_UP, over_b(lambda b: gu_ref[b, 0:GLA_RANK, :]), D_MODEL // 256)
        put("meta_tokens", over_b(lambda b: dm_ref[b]))

    return pl.pallas_call(
        body, name="pack_small", out_shape=jax.ShapeDtypeStruct((SMALL_ROWS, D_MODEL), F32),
    )(acc2, acc1, lvec, cacc, gvec, dcw, dgup, dmeta)


def _sum_small(gathered):
    def body(g_ref, o_ref, loss_ref):
        acc = g_ref[0:SMALL_ROWS, :]
        for d in range(1, 8):
            acc = acc + g_ref[d * SMALL_ROWS:(d + 1) * SMALL_ROWS, :]
        o_ref[...] = acc
        loss_ref[...] = jnp.sum(acc[0:1, :], axis=1, keepdims=True)

    return pl.pallas_call(
        body, name="sum_small",
        out_shape=[jax.ShapeDtypeStruct((SMALL_ROWS, D_MODEL), F32), jax.ShapeDtypeStruct((1, 1), F32)],
    )(gathered)


def _adamw_math(w, g, m, v):
    c1 = 1.0 - ADAM_B1 ** ADAM_STEP
    c2 = 1.0 - ADAM_B2 ** ADAM_STEP
    mn = ADAM_B1 * m + (1.0 - ADAM_B1) * g
    vn = ADAM_B2 * v + (1.0 - ADAM_B2) * (g * g)
    return -ADAM_LR * ((mn / c1) / (jnp.sqrt(vn / c2) + ADAM_EPS) + ADAM_WD * w), mn, vn


def _adamw_small(red, sharded_grads, params):
    names = list(params)
    sharded = [k for k in names if k in sharded_grads]
    n, ns = len(names), len(sharded)

    def body(*refs):
        red_ref, sg_refs, p_refs, o_refs = refs[0], refs[1:1 + ns], refs[1 + ns:1 + ns + 3 * n], refs[1 + ns + 3 * n:]
        for i, k in enumerate(names):
            w_ref, m_ref, v_ref = p_refs[3 * i:3 * i + 3]
            if k in sharded:
                g = sg_refs[sharded.index(k)][...]
            else:
                r0, nr, l0, nl = SMALL_AT[k]
                g = red_ref[r0:r0 + nr, l0:l0 + nl]
            dl, mn, vn = _adamw_math(w_ref[...], g, m_ref[...], v_ref[...])
            for o_ref, val in zip(o_refs[4 * i:4 * i + 4], (g, dl, mn, vn)):
                o_ref[...] = val

    flat = [a for k in names for a in params[k]]
    res = pl.pallas_call(
        body, name="adamw_small",
        out_shape=[jax.ShapeDtypeStruct(params[k][0].shape, F32) for k in names for _ in range(4)],
    )(red, *[sharded_grads[k] for k in sharded], *flat)
    return {k: tuple(res[4 * i:4 * i + 4]) for i, k in enumerate(names)}


def _adamw_halves(w, own, sib, m, v, c_arr):
    rows, cols = w.shape
    tr = 128 if rows % 128 == 0 else rows

    def body(c_ref, w_ref, own_ref, sib_ref, m_ref, v_ref, g_ref, d_ref, mo_ref, vo_ref):
        first = c_ref[0] == 0
        own, sib = own_ref[...], sib_ref[...]
        g = jnp.concatenate([jnp.where(first, own, sib), jnp.where(first, sib, own)], axis=1)
        g_ref[...] = g
        d_ref[...], mo_ref[...], vo_ref[...] = _adamw_math(w_ref[...], g, m_ref[...], v_ref[...])

    full = pl.BlockSpec((tr, cols), lambda i, c: (i, 0))
    half = pl.BlockSpec((tr, HALF), lambda i, c: (i, 0))
    return pl.pallas_call(
        body, name="adamw_halves", out_shape=[jax.ShapeDtypeStruct(w.shape, F32)] * 4,
        grid_spec=pltpu.PrefetchScalarGridSpec(
            num_scalar_prefetch=1, grid=(rows // tr,), in_specs=[full, half, half, full, full], out_specs=[full] * 4),
        compiler_params=_params(("parallel",)),
    )(c_arr, w, own, sib, m, v)


def _ln_in_fwd(x, g, b, fuse=NO_FUSE):
    bsz, s, d = x.shape
    tp = s + HEAD
    nh = 2
    sh = s // nh
    rc = min(256, sh)

    def body(x_ref, g_ref, b_ref, s0_ref, s0b_ref):
        h = pl.program_id(1)
        gg, bb = g_ref[...], b_ref[...]

        def step(i, carry):
            src = pl.ds(pl.multiple_of(i * rc, rc), rc)
            dst = pl.ds(pl.multiple_of(HEAD + h * sh + i * rc, 64), rc)
            xh, _ = _ln_stats(x_ref[0, src, :])
            val = xh * gg + bb
            s0_ref[0, dst, :] = val
            s0b_ref[0, dst, :] = val.astype(BF16)
            return carry

        lax.fori_loop(0, sh // rc, step, 0)

    full = lambda bi, hi: (bi, 0, 0)
    (s0, s0b), got = _fused_call(
        body, fuse, name="ln_in_fwd",
        out_shape=[jax.ShapeDtypeStruct((bsz, tp, d), F32), jax.ShapeDtypeStruct((bsz, tp, d), BF16)],
        grid=(bsz, nh),
        in_specs=[pl.BlockSpec((1, sh, d), lambda bi, hi: (bi, hi, 0)),
                  pl.BlockSpec((1, d), lambda bi, hi: (0, 0)),
                  pl.BlockSpec((1, d), lambda bi, hi: (0, 0))],
        out_specs=[pl.BlockSpec((1, tp, d), full)] * 2, scratch_shapes=[], operands=(x, g, b),
        late_relay=True)
    return s0, s0b, got


def _ln_meta(s0, s0b, meta, g, b):
    bsz, tp, d = s0.shape

    def body(s0_in, s0b_in, meta_ref, g_ref, b_ref, s0_ref, s0b_ref):
        mh, _ = _ln_stats(meta_ref[...])
        mv = mh * g_ref[...] + b_ref[...]
        s0_ref[0, 0:PAD, :] = jnp.zeros((PAD, d), F32)
        s0b_ref[0, 0:PAD, :] = jnp.zeros((PAD, d), BF16)
        s0_ref[0, PAD:HEAD, :] = mv
        s0b_ref[0, PAD:HEAD, :] = mv.astype(BF16)

    head = pl.BlockSpec((1, HEAD, d), lambda bi: (bi, 0, 0))
    vec = pl.BlockSpec((1, d), lambda bi: (0, 0))
    return pl.pallas_call(
        body, name="ln_meta", out_shape=[jax.ShapeDtypeStruct(s0.shape, F32), jax.ShapeDtypeStruct(s0b.shape, BF16)],
        grid=(bsz,), in_specs=[head, head, pl.BlockSpec((N_META, d), lambda bi: (0, 0)), vec, vec],
        out_specs=[head, head], input_output_aliases={0: 0, 1: 1},
        compiler_params=_params(("parallel",)),
    )(s0, s0b, meta, g, b)


def _in_proj(s0b, w_int, fuse=NO_FUSE):
    r, d = s0b.shape
    tm = _row_tile(r)

    def body(a_ref, w_ref, o_ref):
        o_ref[...] = _nt(a_ref[...], w_ref[...])

    (u,), got = _fused_call(
        body, fuse, name="in_proj", out_shape=[jax.ShapeDtypeStruct((r, D_IN_PAD), F32)],
        grid=(r // tm,),
        in_specs=[pl.BlockSpec((tm, d), lambda i: (i, 0)), pl.BlockSpec((D_IN_PAD, d), lambda i: (0, 0))],
        out_specs=[pl.BlockSpec((tm, D_IN_PAD), lambda i: (i, 0))], scratch_shapes=[], operands=(s0b, w_int))
    return u, got


def _conv_fwd(u, conv_w, conv_b, fuse=NO_FUSE):
    bsz, tp, _ = u.shape
    nchunk = tp // CHUNK
    win = CHUNK + 32
    nct = D_CONV // 128

    def body(cv_ref, cg_ref, w_ref, cb_ref, hc_ref, h_scr, win_scr):
        h_scr[0:32, :] = jnp.zeros((32, 128), F32)
        h_scr[32:32 + tp, :] = cv_ref[0] * _sigmoid(cg_ref[0])
        cb = cb_ref[...]

        def step(n, carry):
            r0 = pl.multiple_of(n * CHUNK, CHUNK)
            win_scr[...] = h_scr[pl.ds(r0, win), :]
            acc = jnp.zeros((CHUNK, 128), F32)
            for j in range(CONV_WIDTH):
                acc = acc + w_ref[j:j + 1, :] * win_scr[2 + j:2 + j + CHUNK, :]
            hc_ref[0, pl.ds(r0, CHUNK), :] = acc + cb
            return carry

        lax.fori_loop(0, nchunk, step, 0)

    (hc,), got = _fused_call(
        body, fuse, name="conv_fwd", out_shape=[jax.ShapeDtypeStruct((bsz, tp, D_CONV), F32)],
        grid=(bsz, nct),
        in_specs=[pl.BlockSpec((1, tp, 128), lambda bi, t: (bi, 0, C_VAL // 128 + t)),
                  pl.BlockSpec((1, tp, 128), lambda bi, t: (bi, 0, C_GATE // 128 + t)),
                  pl.BlockSpec((32, 128), lambda bi, t: (0, t)),
                  pl.BlockSpec((1, 128), lambda bi, t: (0, t))],
        out_specs=[pl.BlockSpec((1, tp, 128), lambda bi, t: (bi, 0, t))],
        scratch_shapes=[pltpu.VMEM((tp + 32, 128), F32), pltpu.VMEM((win, 128), F32)],
        operands=(u, u, conv_w, conv_b))
    return hc, got


def _gla_group(nchunk):
    return 11 if nchunk % 11 == 0 else nchunk


def _bdot(a, b, ca, cb, precision=None):
    return lax.dot_general(a, b, (((ca,), (cb,)), ((0,), (0,))), preferred_element_type=F32, precision=precision)


def _bnn(a, b, **kw):
    return _bdot(a, b, 2, 1, **kw)


def _bnt(a, b, **kw):
    return _bdot(a, b, 2, 2, **kw)


def _gla_consts(nb):
    row = lax.broadcasted_iota(jnp.int32, (nb, CHUNK, CHUNK), 1)
    col = lax.broadcasted_iota(jnp.int32, (nb, CHUNK, CHUNK), 2)
    lane = lax.broadcasted_iota(jnp.int32, (1, 1, 128), 2)
    return row >= col, row <= col, [lane < GLA_DK, lane >= GLA_DK]


def _gla_group_terms(g, nb, q_ref, k_ref, gd_ref, gup_ref, gb_ref, tril):
    m = nb * CHUNK
    rows = pl.ds(pl.multiple_of(g * m, CHUNK), m)
    z = _nn(gd_ref[0, rows, :].astype(BF16), gup_ref[...]) + gb_ref[...]
    valid = g * m + lax.broadcasted_iota(jnp.int32, (m, 1), 0) >= PAD
    lg = jnp.where(valid, _log_sigmoid(z) * (1.0 / GLA_TAU), 0.0)
    bcum = _bnn(tril.astype(F32), lg.reshape(nb, CHUNK, 128), precision=lax.Precision.HIGHEST)
    blast = bcum[:, CHUNK - 1:CHUNK, :]
    eb = jnp.exp(bcum)
    enb = jnp.exp(-bcum)
    erest = jnp.exp(blast - bcum)
    q = (q_ref[0, rows, :] * Q_SCALE).reshape(nb, CHUNK, 128)
    k = k_ref[0, rows, :].reshape(nb, CHUNK, 128)
    return rows, valid, z, eb, enb, erest, jnp.exp(blast), q * eb, k * enb, k * erest


def _grid_ends(grid):
    ids = [pl.program_id(i) for i in range(len(grid))]
    first = functools.reduce(jnp.logical_and, [i == 0 for i in ids])
    last = functools.reduce(jnp.logical_and, [i == g - 1 for i, g in zip(ids, grid)])
    return first, last


def _gla_fwd(u, gup, gbias, gnorm, fuse=NO_FUSE):
    bsz, tp, _ = u.shape
    nchunk = tp // CHUNK
    nb = _gla_group(nchunk)

    def body(q_ref, k_ref, v_ref, r_ref, gd_ref, gup_ref, gb_ref, gn_ref, out_ref, o_ref, st_ref, s_scr):
        tril, _, hmask = _gla_consts(nb)
        s_scr[...] = jnp.zeros_like(s_scr)
        gn = gn_ref[...]

        def group(g, carry):
            rows, _, _, _, _, _, dec, qe, ke, kd = _gla_group_terms(g, nb, q_ref, k_ref, gd_ref, gup_ref, gb_ref, tril)
            keb, kdb = ke.astype(BF16), kd.astype(BF16)
            for h in range(2):
                cols = slice(h * GLA_DV, (h + 1) * GLA_DV)
                qh = jnp.where(hmask[h], qe, 0.0).astype(BF16)
                vh = v_ref[0, rows, cols].astype(BF16).reshape(nb, CHUNK, GLA_DV)
                a = jnp.where(tril, _bnt(qh, keb), 0.0).astype(BF16)
                st = s_scr[h]
                sts = []
                for n in range(nb):
                    st_ref[0, h, g * nb + n] = st
                    sts.append(st.astype(BF16))
                    st = dec[n] * st + _tn(vh[n], kdb[n])
                s_scr[h] = st
                o = (_bnn(a, vh) + _bnt(qh, jnp.stack(sts))).reshape(nb * CHUNK, GLA_DV)
                o_ref[0, rows, cols] = o
                rms = lax.rsqrt(jnp.mean(o * o, axis=-1, keepdims=True) + LN_EPS)
                rh = r_ref[0, rows, cols]
                out_ref[0, rows, cols] = (o * rms * gn * (rh * _sigmoid(rh))).astype(BF16)
            return carry

        lax.fori_loop(0, nchunk // nb, group, 0)

    res, got = _fused_call(
        body, fuse, name="gla_fwd",
        out_shape=[jax.ShapeDtypeStruct((bsz, tp, 512), BF16), jax.ShapeDtypeStruct((bsz, tp, 512), F32),
                   jax.ShapeDtypeStruct((bsz, GLA_HEADS, nchunk, GLA_DV, 128), F32)],
        grid=(bsz, 2),
        in_specs=[pl.BlockSpec((1, tp, 128), lambda bi, p: (bi, 0, C_Q // 128 + p)),
                  pl.BlockSpec((1, tp, 128), lambda bi, p: (bi, 0, C_K // 128 + p)),
                  pl.BlockSpec((1, tp, 256), lambda bi, p: (bi, 0, C_V // 256 + p)),
                  pl.BlockSpec((1, tp, 256), lambda bi, p: (bi, 0, C_R // 256 + p)),
                  pl.BlockSpec((1, tp, 128), lambda bi, p: (bi, 0, C_GD // 128)),
                  pl.BlockSpec((128, 128), lambda bi, p: (0, p)),
                  pl.BlockSpec((1, 128), lambda bi, p: (0, p)),
                  pl.BlockSpec((1, 128), lambda bi, p: (0, 0))],
        out_specs=[pl.BlockSpec((1, tp, 256), lambda bi, p: (bi, 0, p)),
                   pl.BlockSpec((1, tp, 256), lambda bi, p: (bi, 0, p)),
                   pl.BlockSpec((1, 2, nchunk, GLA_DV, 128), lambda bi, p: (bi, p, 0, 0, 0))],
        scratch_shapes=[pltpu.VMEM((2, GLA_DV, 128), F32)],
        operands=(u, u, u, u, u, gup, gbias, gnorm))
    return res[0], res[1], res[2], got


def _out_proj_ln1(hc, gla_out, w_out, s0, cg, cb, g1, b1, fuse=NO_FUSE):
    r, d = s0.shape
    tm = _row_tile(r)

    def body(hc_ref, a_ref, w_ref, s0_ref, cg_ref, cb_ref, g_ref, b_ref, co_ref, xh_ref, rstd_ref, s1b_ref):
        for rs in _sub_rows(tm):
            xc, _ = _ln_stats(hc_ref[rs, :])
            nv = xc * cg_ref[...] + cb_ref[...]
            co = (nv * _sigmoid(nv)).astype(BF16)
            co_ref[rs, :] = co
            mix = _nn(co, w_ref[0:D_CONV, :]) + _nn(a_ref[rs, :], w_ref[D_CONV:, :])
            xh, rstd = _ln_stats(ALPHA * s0_ref[rs, :] + mix)
            xh_ref[rs, :] = xh
            rstd_ref[rs, :] = rstd
            s1b_ref[rs, :] = (xh * g_ref[...] + b_ref[...]).astype(BF16)

    row = lambda n: pl.BlockSpec((tm, n), lambda i: (i, 0))
    vec = lambda n: pl.BlockSpec((1, n), lambda i: (0, 0))
    res, got = _fused_call(
        body, fuse, name="out_proj_ln1",
        out_shape=[jax.ShapeDtypeStruct((r, D_CONV), BF16), jax.ShapeDtypeStruct((r, d), F32),
                   jax.ShapeDtypeStruct((r, 1), F32), jax.ShapeDtypeStruct((r, d), BF16)],
        grid=(r // tm,),
        in_specs=[row(D_CONV), row(512), pl.BlockSpec((d, d), lambda i: (0, 0)), row(d),
                  vec(D_CONV), vec(D_CONV), vec(d), vec(d)],
        out_specs=[row(D_CONV), row(d), row(1), row(d)], scratch_shapes=[],
        operands=(hc, gla_out, w_out, s0, cg, cb, g1, b1))
    return res[0], res[1], res[2], res[3], got


def _ffn1(s1b, w1, fuse=NO_FUSE):
    r, d = s1b.shape
    tm = _row_tile(r)
    ns, _, wn = w1.shape

    def body(a_ref, w_ref, o_ref):
        a = a_ref[...]
        for j in range(ns):
            o_ref[:, j * wn:(j + 1) * wn] = jnp.maximum(_nn(a, w_ref[j]), 0.0).astype(BF16)

    (ra,), got = _fused_call(
        body, fuse, name="ffn1", out_shape=[jax.ShapeDtypeStruct((r, D_FF), BF16)], grid=(r // tm,),
        in_specs=[pl.BlockSpec((tm, d), lambda i: (i, 0)), pl.BlockSpec(w1.shape, lambda i: (0, 0, 0))],
        out_specs=[pl.BlockSpec((tm, D_FF), lambda i: (i, 0))], scratch_shapes=[], operands=(s1b, w1))
    return ra, got


def _ffn2_ln2_loss(ra, w2, xhat1, g1, b1, g2, b2, tgt, tp):
    r, d = xhat1.shape
    tm = _row_tile(tp)
    per = tp // tm

    def body(ra_ref, w_ref, xh1_ref, g1_ref, b1_ref, g2_ref, b2_ref, tgt_ref, dt_ref, dtb_ref, acc_ref, t_ref, sem):
        i = pl.program_id(0)
        b, j = i // per, i % per

        @pl.when(i == 0)
        def _():
            acc_ref[...] = jnp.zeros_like(acc_ref)

        head_copy = pltpu.make_async_copy(tgt_ref.at[b, pl.ds(0, tm - HEAD), :], t_ref.at[pl.ds(HEAD, tm - HEAD), :], sem)
        body_copy = pltpu.make_async_copy(
            tgt_ref.at[b, pl.ds(pl.multiple_of(jnp.maximum(j * tm - HEAD, 0), 64), tm), :], t_ref, sem)

        @pl.when(j == 0)
        def _():
            t_ref[0:HEAD, :] = jnp.zeros((HEAD, d), F32)
            head_copy.start()

        pl.when(j > 0)(body_copy.start)

        sums = [jnp.zeros((1, d), F32)] * 3
        for rs in _sub_rows(tm):
            rb = ra_ref[rs, :]
            f = _nn(rb * rb, w_ref[...])
            if rs.start == 0:
                pl.when(j == 0)(head_copy.wait)
                pl.when(j > 0)(body_copy.wait)
            s1 = xh1_ref[rs, :] * g1_ref[...] + b1_ref[...]
            xh2, rstd2 = _ln_stats(ALPHA * s1 + f)
            y = xh2 * g2_ref[...] + b2_ref[...]
            rowid = (i % per) * tm + rs.start + lax.broadcasted_iota(jnp.int32, (rs.stop - rs.start, 1), 0)
            e = jnp.where(rowid >= HEAD, y - t_ref[rs, :], 0.0)
            dy = e * (1.0 / d)
            dt2 = _ln_bwd(dy * g2_ref[...], xh2, rstd2)
            dt_ref[rs, :] = dt2
            dtb_ref[rs, :] = dt2.astype(BF16)
            sums = [sums[0] + (0.5 / d) * jnp.sum(e * e, axis=0, keepdims=True),
                    sums[1] + jnp.sum(dy * xh2, axis=0, keepdims=True), sums[2] + jnp.sum(dy, axis=0, keepdims=True)]
        for k in range(3):
            acc_ref[k:k + 1, :] += sums[k]

    row = lambda n: pl.BlockSpec((tm, n), lambda i: (i, 0))
    vec = pl.BlockSpec((1, d), lambda i: (0, 0))
    return pl.pallas_call(
        body, name="ffn2_ln2_loss",
        out_shape=[jax.ShapeDtypeStruct((r, d), F32), jax.ShapeDtypeStruct((r, d), BF16),
                   jax.ShapeDtypeStruct((8, d), F32)],
        grid=(r // tm,),
        in_specs=[row(D_FF), pl.BlockSpec((D_FF, d), lambda i: (0, 0)), row(d), vec, vec, vec, vec, ANY],
        out_specs=[row(d), row(d), pl.BlockSpec((8, d), lambda i: (0, 0))],
        scratch_shapes=[pltpu.VMEM((tm, d), F32), pltpu.SemaphoreType.DMA],
        compiler_params=_params(("arbitrary",)),
    )(ra, w2, xhat1, g1, b1, g2, b2, tgt)


def _ffn_bwd_da(dt2b, w2, ra):
    r, d = dt2b.shape
    tm = _row_tile(r)

    def body(g_ref, w_ref, ra_ref, o_ref):
        o_ref[...] = (_nt(g_ref[...], w_ref[...]) * (2.0 * ra_ref[...].astype(F32))).astype(BF16)

    return pl.pallas_call(
        body, name="ffn_bwd_da", out_shape=jax.ShapeDtypeStruct((r, D_FF), BF16),
        grid=(r // tm,),
        in_specs=[pl.BlockSpec((tm, d), lambda i: (i, 0)), pl.BlockSpec((D_FF, d), lambda i: (0, 0)),
                  pl.BlockSpec((tm, D_FF), lambda i: (i, 0))],
        out_specs=pl.BlockSpec((tm, D_FF), lambda i: (i, 0)),
        compiler_params=_params(("parallel",)),
    )(dt2b, w2, ra)


def _ffn_bwd_ln1(da, w1, dt2, xhat1, rstd1, g1):
    r, d = dt2.shape
    tm = _row_tile(r)

    def body(da_ref, w_ref, dt2_ref, xh_ref, rstd_ref, g_ref, dt_ref, dtb_ref, acc_ref):
        @pl.when(pl.program_id(0) == 0)
        def _():
            acc_ref[...] = jnp.zeros_like(acc_ref)

        sums = [jnp.zeros((1, d), F32)] * 2
        for rs in _sub_rows(tm):
            ds1 = ALPHA * dt2_ref[rs, :]
            for j in range(w1.shape[0]):
                ds1 = ds1 + _nt(da_ref[rs, j * w1.shape[2]:(j + 1) * w1.shape[2]], w_ref[j])
            xh = xh_ref[rs, :]
            dt1 = _ln_bwd(ds1 * g_ref[...], xh, rstd_ref[rs, :])
            dt_ref[rs, :] = dt1
            dtb_ref[rs, :] = dt1.astype(BF16)
            sums = [sums[0] + jnp.sum(ds1 * xh, axis=0, keepdims=True), sums[1] + jnp.sum(ds1, axis=0, keepdims=True)]
        for k in range(2):
            acc_ref[k:k + 1, :] += sums[k]

    row = lambda n: pl.BlockSpec((tm, n), lambda i: (i, 0))
    return pl.pallas_call(
        body, name="ffn_bwd_ln1",
        out_shape=[jax.ShapeDtypeStruct((r, d), F32), jax.ShapeDtypeStruct((r, d), BF16),
                   jax.ShapeDtypeStruct((8, d), F32)],
        grid=(r // tm,),
        in_specs=[row(D_FF), pl.BlockSpec(w1.shape, lambda i: (0, 0, 0)), row(d), row(d), row(1),
                  pl.BlockSpec((1, d), lambda i: (0, 0))],
        out_specs=[row(d), row(d), pl.BlockSpec((8, d), lambda i: (0, 0))],
        compiler_params=_params(("arbitrary",)),
    )(da, w1, dt2, xhat1, rstd1, g1)


def _matmul_tn(lhs, rhs, bm, square_lhs=False, name="matmul_tn", fuse=NO_FUSE):
    r, m = lhs.shape
    n = rhs.shape[1]
    tk = _reduce_tile(r, True)

    def body(a_ref, b_ref, o_ref):
        @pl.when(pl.program_id(1) == 0)
        def _():
            o_ref[...] = jnp.zeros_like(o_ref)

        a = a_ref[...]
        if square_lhs:
            a = a * a
        o_ref[...] += _tn(a, b_ref[...])

    (out,), got = _fused_call(
        body, fuse, name=name, out_shape=[jax.ShapeDtypeStruct((m, n), F32)], grid=(m // bm, r // tk),
        in_specs=[pl.BlockSpec((tk, bm), lambda i, k: (k, i)), pl.BlockSpec((tk, n), lambda i, k: (k, 0))],
        out_specs=[pl.BlockSpec((bm, n), lambda i, k: (i, 0))], scratch_shapes=[], operands=(lhs, rhs))
    return out, got


def _grad_w_ff1(s1b, da, fuse=NO_FUSE):
    r, d = s1b.shape
    wn = da.shape[1] // 4
    tk = _reduce_tile(r, True)

    def body(a_ref, b_ref, o_ref):
        @pl.when(pl.program_id(1) == 0)
        def _():
            o_ref[...] = jnp.zeros_like(o_ref)

        o_ref[0] += _tn(a_ref[...], b_ref[...])

    (out,), got = _fused_call(
        body, fuse, name="grad_w_ff1", out_shape=[jax.ShapeDtypeStruct((4, d, wn), F32)], grid=(4, r // tk),
        in_specs=[pl.BlockSpec((tk, d), lambda j, k: (k, 0)), pl.BlockSpec((tk, wn), lambda j, k: (k, j))],
        out_specs=[pl.BlockSpec((1, d, wn), lambda j, k: (j, 0, 0))], scratch_shapes=[], operands=(s1b, da))
    return out, got


def _grad_w_out(conv_of, gla_of, dt1b, fuse=NO_FUSE):
    r, n = dt1b.shape
    tk = _reduce_tile(r, True)

    def body(a_ref, b_ref, g_ref, o_ref):
        @pl.when(pl.program_id(1) == 0)
        def _():
            o_ref[...] = jnp.zeros_like(o_ref)

        @pl.when(pl.program_id(0) == 0)
        def _():
            o_ref[...] += _tn(a_ref[...], g_ref[...])

        @pl.when(pl.program_id(0) == 1)
        def _():
            o_ref[...] += _tn(b_ref[...], g_ref[...])

    lhs = pl.BlockSpec((tk, 512), lambda i, k: (k, 0))
    (out,), got = _fused_call(
        body, fuse, name="grad_w_out", out_shape=[jax.ShapeDtypeStruct((2 * 512, n), F32)], grid=(2, r // tk),
        in_specs=[lhs, lhs, pl.BlockSpec((tk, n), lambda i, k: (k, 0))],
        out_specs=[pl.BlockSpec((512, n), lambda i, k: (i, 0))], scratch_shapes=[],
        operands=(conv_of, gla_of, dt1b))
    return out, got


def _out_proj_bwd(dt1b, w_out, hc, cg, cb, fuse=NO_FUSE):
    r, d = dt1b.shape
    tm = _row_tile(r)

    def body(g_ref, w_ref, hc_ref, cg_ref, cb_ref, dhc_ref, dgla_ref, acc_ref):
        @pl.when(pl.program_id(0) == 0)
        def _():
            acc_ref[...] = jnp.zeros_like(acc_ref)

        gg = cg_ref[...]
        sums = [jnp.zeros((1, D_CONV), F32)] * 3
        for rs in _sub_rows(tm):
            dmix = _nt(g_ref[rs, :], w_ref[...])
            dgla_ref[rs, :] = dmix[:, D_CONV:]
            xh, rstd = _ln_stats(hc_ref[rs, :])
            nv = xh * gg + cb_ref[...]
            sig = _sigmoid(nv)
            dn = dmix[:, :D_CONV] * (sig * (1.0 + nv * (1.0 - sig)))
            dhc = _ln_bwd(dn * gg, xh, rstd)
            dhc_ref[rs, :] = dhc
            sums = [sums[0] + jnp.sum(dhc, axis=0, keepdims=True), sums[1] + jnp.sum(dn * xh, axis=0, keepdims=True),
                    sums[2] + jnp.sum(dn, axis=0, keepdims=True)]
        for k in range(3):
            acc_ref[k:k + 1, :] += sums[k]

    row = lambda n: pl.BlockSpec((tm, n), lambda i: (i, 0))
    vec = pl.BlockSpec((1, D_CONV), lambda i: (0, 0))
    res, got = _fused_call(
        body, fuse, name="out_proj_bwd",
        out_shape=[jax.ShapeDtypeStruct((r, D_CONV), F32), jax.ShapeDtypeStruct((r, 512), F32),
                   jax.ShapeDtypeStruct((8, D_CONV), F32)],
        grid=(r // tm,),
        in_specs=[row(d), pl.BlockSpec((d, d), lambda i: (0, 0)), row(D_CONV), vec, vec],
        out_specs=[row(D_CONV), row(512), pl.BlockSpec((8, D_CONV), lambda i: (0, 0))], scratch_shapes=[],
        operands=(dt1b, w_out, hc, cg, cb))
    return res[0], res[1], res[2], got


def _conv_bwd(dhc, u, conv_w, fuse=NO_FUSE):
    bsz, tp, _ = u.shape
    nchunk = tp // CHUNK
    win = CHUNK + 32
    nct = D_CONV // 128

    def body(dhc_ref, cv_ref, cg_ref, w_ref, dv_ref, dg_ref, dw_ref, h_scr, dhc_scr, hwin, dwin, dw_scr):
        h_scr[0:32, :] = jnp.zeros((32, 128), F32)
        h_scr[32:32 + tp, :] = cv_ref[0] * _sigmoid(cg_ref[0])
        dhc_scr[0:tp, :] = dhc_ref[0]
        dhc_scr[tp:tp + 32, :] = jnp.zeros((32, 128), F32)
        dw_scr[...] = jnp.zeros_like(dw_scr)

        def step(n, carry):
            r0 = pl.multiple_of(n * CHUNK, CHUNK)
            rows = pl.ds(r0, CHUNK)
            hwin[...] = h_scr[pl.ds(r0, win), :]
            dwin[...] = dhc_scr[pl.ds(r0, win), :]
            dcur = dwin[0:CHUNK, :]
            acc = jnp.zeros((CHUNK, 128), F32)
            for j in range(CONV_WIDTH):
                acc = acc + w_ref[j:j + 1, :] * dwin[30 - j:30 - j + CHUNK, :]
                prod = dcur * hwin[2 + j:2 + j + CHUNK, :]
                dw_scr[j * 8:(j + 1) * 8, :] += jnp.sum(prod.reshape(CHUNK // 8, 8, 128), axis=0)
            cg = cg_ref[0, rows, :]
            sig = _sigmoid(cg)
            rowid = n * CHUNK + lax.broadcasted_iota(jnp.int32, (CHUNK, 1), 0)
            dh = jnp.where(rowid >= PAD, acc, 0.0)
            dv_ref[0, rows, :] = (dh * sig).astype(BF16)
            dg_ref[0, rows, :] = (dh * cv_ref[0, rows, :] * sig * (1.0 - sig)).astype(BF16)
            return carry

        lax.fori_loop(0, nchunk, step, 0)
        dw_ref[0] = jnp.zeros((32, 128), F32)
        for j in range(CONV_WIDTH):
            dw_ref[0, j:j + 1, :] = jnp.sum(dw_scr[j * 8:(j + 1) * 8, :], axis=0, keepdims=True)

    blk = lambda off: pl.BlockSpec((1, tp, 128), lambda bi, t: (bi, 0, off // 128 + t))
    res, got = _fused_call(
        body, fuse, name="conv_bwd",
        out_shape=[jax.ShapeDtypeStruct((bsz, tp, D_CONV), BF16), jax.ShapeDtypeStruct((bsz, tp, D_CONV), BF16),
                   jax.ShapeDtypeStruct((bsz, 32, D_CONV), F32)],
        grid=(bsz, nct),
        in_specs=[blk(0), blk(C_VAL), blk(C_GATE), pl.BlockSpec((32, 128), lambda bi, t: (0, t))],
        out_specs=[blk(0), blk(0), pl.BlockSpec((1, 32, 128), lambda bi, t: (bi, 0, t))],
        scratch_shapes=[pltpu.VMEM((tp + 32, 128), F32), pltpu.VMEM((tp + 32, 128), F32),
                        pltpu.VMEM((win, 128), F32), pltpu.VMEM((win, 128), F32),
                        pltpu.VMEM((CONV_WIDTH * 8, 128), F32)],
        operands=(dhc, u, u, conv_w))
    return res[0], res[1], res[2], got


def _gla_bwd(dgla, u, o_pre, states, gup, gbias, gnorm, fuse=NO_FUSE):
    bsz, tp, _ = u.shape
    nchunk = tp // CHUNK
    nb = _gla_group(nchunk)

    def body(dy_ref, q_ref, k_ref, v_ref, r_ref, gd_ref, o_ref, st_ref, gup_ref, gb_ref, gn_ref,
             dq_ref, dk_ref, dv_ref, dr_ref, dgd_ref, dgup_ref, vec_ref, h_scr, gup_acc):
        tril, triu, hmask = _gla_consts(nb)
        h_scr[...] = jnp.zeros_like(h_scr)
        gup_acc[...] = jnp.zeros_like(gup_acc)
        gn = gn_ref[...]
        gupb = gup_ref[...]
        m = nb * CHUNK
        ngroup = nchunk // nb

        def group(i, carry):
            dbias, dgn = carry
            g = ngroup - 1 - i
            rows, valid, z, eb, enb, erest, dec, qe, ke, kd = _gla_group_terms(
                g, nb, q_ref, k_ref, gd_ref, gup_ref, gb_ref, tril)
            keb, kdb = ke.astype(BF16), kd.astype(BF16)
            dqe = jnp.zeros((nb, CHUNK, 128), F32)
            dke = jnp.zeros((nb, CHUNK, 128), F32)
            dkd = jnp.zeros((nb, CHUNK, 128), F32)
            ddec = jnp.zeros((nb, 1, 128), F32)
            for h in range(2):
                cols = slice(h * GLA_DV, (h + 1) * GLA_DV)
                o = o_ref[0, rows, cols]
                rh = r_ref[0, rows, cols]
                dy = dy_ref[0, rows, cols]
                rms = lax.rsqrt(jnp.mean(o * o, axis=-1, keepdims=True) + LN_EPS)
                nrm = o * rms
                sig = _sigmoid(rh)
                sw = rh * sig
                dr_ref[0, rows, cols] = (dy * nrm * gn * (sig * (1.0 + rh * (1.0 - sig)))).astype(BF16)
                dgn = dgn + jnp.sum(dy * nrm * sw, axis=0, keepdims=True)
                dn = dy * gn * sw
                do = rms * (dn - nrm * jnp.mean(dn * nrm, axis=-1, keepdims=True))
                dob = do.astype(BF16).reshape(nb, CHUNK, GLA_DV)
                qh = jnp.where(hmask[h], qe, 0.0).astype(BF16)
                vh = v_ref[0, rows, cols].astype(BF16).reshape(nb, CHUNK, GLA_DV)
                ht = h_scr[h]
                hts = [None] * nb
                for n in reversed(range(nb)):
                    hts[n] = ht
                    ht = dec[n] * ht + _tn(dob[n], qh[n])
                h_scr[h] = ht
                htf = jnp.stack(hts)
                htb = htf.astype(BF16)
                st = st_ref[0, h, pl.ds(g * nb, nb)]
                at = jnp.where(triu, _bnt(keb, qh), 0.0).astype(BF16)
                da = jnp.where(tril, _bnt(dob, vh), 0.0).astype(BF16)
                dat = jnp.where(triu, _bnt(vh, dob), 0.0).astype(BF16)
                dqe = dqe + jnp.where(hmask[h], _bnn(da, keb) + _bnn(dob, st.astype(BF16)), 0.0)
                dke = dke + _bnn(dat, qh)
                dv_ref[0, rows, cols] = (_bnn(at, dob) + _bnt(kdb, htb)).reshape(m, GLA_DV).astype(BF16)
                dkd = dkd + jnp.where(hmask[h], _bnn(vh, htb), 0.0)
                ddec = ddec + jnp.where(hmask[h], jnp.sum(htf * st, axis=1, keepdims=True), 0.0)
            dq_ref[0, rows, :] = (dqe * eb * Q_SCALE).reshape(m, 128).astype(BF16)
            dk_ref[0, rows, :] = (dke * enb + dkd * erest).reshape(m, 128).astype(BF16)
            db = dqe * qe - dke * ke - dkd * kd
            dblast = jnp.sum(dkd * kd, axis=1, keepdims=True) + ddec * dec
            lastrow = lax.broadcasted_iota(jnp.int32, (1, CHUNK, 1), 1) == CHUNK - 1
            db = db + jnp.where(lastrow, dblast, 0.0)
            dlg = _bnn(triu.astype(F32), db, precision=lax.Precision.HIGHEST).reshape(m, 128)
            dz = jnp.where(valid, dlg, 0.0) * (1.0 / GLA_TAU) * (1.0 - _sigmoid(z))
            dzb = dz.astype(BF16)
            dgd_ref[0, 0, rows, :] = _nt(dzb, gupb).astype(BF16)
            gup_acc[...] += _tn(gd_ref[0, rows, :].astype(BF16), dzb)
            return dbias + jnp.sum(dz, axis=0, keepdims=True), dgn

        zero = jnp.zeros((1, 128), F32)
        dbias, dgn = lax.fori_loop(0, ngroup, group, (zero, zero))
        dgup_ref[0] = gup_acc[...]
        vec_ref[0] = jnp.zeros((8, 128), F32)
        vec_ref[0, 0:1, :] = dbias
        vec_ref[0, 1:2, :] = dgn

    pair = lambda w, off: pl.BlockSpec((1, tp, w), lambda bi, p: (bi, 0, off // w + p))
    return _fused_call(
        body, fuse, name="gla_bwd",
        out_shape=[jax.ShapeDtypeStruct((bsz, tp, 256), BF16), jax.ShapeDtypeStruct((bsz, tp, 256), BF16),
                   jax.ShapeDtypeStruct((bsz, tp, 512), BF16), jax.ShapeDtypeStruct((bsz, tp, 512), BF16),
                   jax.ShapeDtypeStruct((bsz, 2, tp, 128), BF16), jax.ShapeDtypeStruct((bsz, 128, 256), F32),
                   jax.ShapeDtypeStruct((bsz, 8, 256), F32)],
        grid=(bsz, 2),
        in_specs=[pair(256, 0), pair(128, C_Q), pair(128, C_K), pair(256, C_V), pair(256, C_R),
                  pl.BlockSpec((1, tp, 128), lambda bi, p: (bi, 0, C_GD // 128)),
                  pair(256, 0),
                  pl.BlockSpec((1, 2, nchunk, GLA_DV, 128), lambda bi, p: (bi, p, 0, 0, 0)),
                  pl.BlockSpec((128, 128), lambda bi, p: (0, p)),
                  pl.BlockSpec((1, 128), lambda bi, p: (0, p)),
                  pl.BlockSpec((1, 128), lambda bi, p: (0, 0))],
        out_specs=[pair(128, 0), pair(128, 0), pair(256, 0), pair(256, 0),
                   pl.BlockSpec((1, 1, tp, 128), lambda bi, p: (bi, p, 0, 0)),
                   pl.BlockSpec((1, 128, 128), lambda bi, p: (bi, 0, p)),
                   pl.BlockSpec((1, 8, 128), lambda bi, p: (bi, 0, p))],
        scratch_shapes=[pltpu.VMEM((2, GLA_DV, 128), F32), pltpu.VMEM((128, 128), F32)],
        operands=(dgla, u, u, u, u, u, o_pre, states, gup, gbias, gnorm))


_DU_OFFSETS = (C_VAL, C_GATE, C_Q, C_K, C_V, C_R)
_DU_WIDTHS = (512, 512, 256, 256, 512, 512)


def _du_specs(tm, per, row_map):
    specs = [pl.BlockSpec((tm, w), row_map) for w in _DU_WIDTHS]
    for p in range(2):
        specs.append(pl.BlockSpec((1, 1, tm, 128), lambda *ix, p=p: (row_map(*ix)[0] // per, p, row_map(*ix)[0] % per, 0)))
    return specs


def _du_pieces(refs):
    out = [(off, ref[...]) for off, ref in zip(_DU_OFFSETS, refs[:6])]
    dgd = (refs[6][0, 0].astype(F32) + refs[7][0, 0].astype(F32)).astype(BF16)
    out.append((C_GD, dgd))
    return out


def _in_proj_bwd(pieces, dgd, w_int, dt1, tp, fuse=NO_FUSE):
    r, d = dt1.shape
    tm = _row_tile(tp)
    per = tp // tm

    def body(*refs):
        w_ref, dt_ref, o_ref = refs[8:]
        acc = ALPHA * dt_ref[...]
        for off, val in _du_pieces(refs[:8]):
            acc = acc + _nn(val, w_ref[off:off + val.shape[1], :])
        o_ref[...] = acc

    row = lambda i: (i, 0)
    (ds0,), got = _fused_call(
        body, fuse, name="in_proj_bwd", out_shape=[jax.ShapeDtypeStruct((r, d), F32)], grid=(r // tm,),
        in_specs=_du_specs(tm, per, row) + [pl.BlockSpec((D_IN_PAD, d), lambda i: (0, 0)), pl.BlockSpec((tm, d), row)],
        out_specs=[pl.BlockSpec((tm, d), row)], scratch_shapes=[], operands=(*pieces, dgd, dgd, w_int, dt1))
    return ds0, got


def _grad_w_in(pieces, dgd, s0b, tp, fuse=NO_FUSE):
    r, d = s0b.shape
    tk = _reduce_tile(tp, False)
    per = tp // tk

    def body(*refs):
        s_ref, o_ref = refs[8:]

        @pl.when(pl.program_id(0) == 0)
        def _():
            o_ref[...] = jnp.zeros_like(o_ref)

        s = s_ref[...]
        for off, val in _du_pieces(refs[:8]):
            o_ref[off:off + val.shape[1], :] += _tn(val, s)

    row = lambda k: (k, 0)
    (out,), got = _fused_call(
        body, fuse, name="grad_w_in", out_shape=[jax.ShapeDtypeStruct((D_IN_PAD, d), F32)], grid=(r // tk,),
        in_specs=_du_specs(tk, per, row) + [pl.BlockSpec((tk, d), row)],
        out_specs=[pl.BlockSpec((D_IN_PAD, d), lambda k: (0, 0))], scratch_shapes=[],
        operands=(*pieces, dgd, dgd, s0b))
    return out, got


def _ln_in_bwd(ds0, x, meta, g):
    bsz, s, d = x.shape
    tp = s + HEAD
    nh = 2
    sh = s // nh
    rc = min(256, sh)

    def body(ds_ref, x_ref, meta_ref, g_ref, gx_ref, dm_ref, vec_ref):
        h = pl.program_id(1)
        gg = g_ref[...]

        @pl.when(h == 0)
        def _():
            mh, mr = _ln_stats(meta_ref[...])
            dsm = ds_ref[0, PAD:HEAD, :]
            dm_ref[0] = _ln_bwd(dsm * gg, mh, mr)
            vec_ref[0] = jnp.zeros((8, d), F32)
            vec_ref[0, 0:1, :] = jnp.sum(dsm * mh, axis=0, keepdims=True)
            vec_ref[0, 1:2, :] = jnp.sum(dsm, axis=0, keepdims=True)

        def step(i, carry):
            sg, sb = carry
            dst = pl.ds(pl.multiple_of(i * rc, rc), rc)
            src = pl.ds(pl.multiple_of(HEAD + h * sh + i * rc, 64), rc)
            xh, rstd = _ln_stats(x_ref[0, dst, :])
            dsv = ds_ref[0, src, :]
            gx_ref[0, dst, :] = _ln_bwd(dsv * gg, xh, rstd)
            return sg + jnp.sum(dsv * xh, axis=0, keepdims=True), sb + jnp.sum(dsv, axis=0, keepdims=True)

        zero = jnp.zeros((1, d), F32)
        sg, sb = lax.fori_loop(0, sh // rc, step, (zero, zero))
        vec_ref[0, 0:1, :] += sg
        vec_ref[0, 1:2, :] += sb

    return pl.pallas_call(
        body, name="ln_in_bwd",
        out_shape=[jax.ShapeDtypeStruct((bsz, s, d), F32), jax.ShapeDtypeStruct((bsz, N_META, d), F32),
                   jax.ShapeDtypeStruct((bsz, 8, d), F32)],
        grid=(bsz, nh),
        in_specs=[pl.BlockSpec((1, tp, d), lambda bi, hi: (bi, 0, 0)),
                  pl.BlockSpec((1, sh, d), lambda bi, hi: (bi, hi, 0)),
                  pl.BlockSpec((N_META, d), lambda bi, hi: (0, 0)),
                  pl.BlockSpec((1, d), lambda bi, hi: (0, 0))],
        out_specs=[pl.BlockSpec((1, sh, d), lambda bi, hi: (bi, hi, 0)),
                   pl.BlockSpec((1, N_META, d), lambda bi, hi: (bi, 0, 0)),
                   pl.BlockSpec((1, 8, d), lambda bi, hi: (bi, 0, 0))],
        compiler_params=_params(("parallel", "arbitrary")),
    )(ds0, x, meta, g)


def _rows128(a):
    return a.reshape(-1, 128)


def kernel(x, meta_tokens, ln_in_g, ln_in_b, w_in, conv_w, conv_b, conv_ln_g, conv_ln_b, gate_up, gate_bias, gla_norm_g, w_out, ln1_g, ln1_b, w_ff1, w_ff2, ln2_g, ln2_b, loss_target, m_meta_tokens, m_ln_in_g, m_ln_in_b, m_w_in, m_conv_w, m_conv_b, m_conv_ln_g, m_conv_ln_b, m_gate_up, m_gate_bias, m_gla_norm_g, m_w_out, m_ln1_g, m_ln1_b, m_w_ff1, m_w_ff2, m_ln2_g, m_ln2_b, v_meta_tokens, v_ln_in_g, v_ln_in_b, v_w_in, v_conv_w, v_conv_b, v_conv_ln_g, v_conv_ln_b, v_gate_up, v_gate_bias, v_gla_norm_g, v_w_out, v_ln1_g, v_ln1_b, v_w_ff1, v_w_ff2, v_ln2_g, v_ln2_b):
    bsz, seq, d = x.shape
    tp = seq + HEAD
    r = bsz * tp
    xi, yi, ci = _mesh_pos()
    chip = 2 * xi + yi
    c_arr = jnp.reshape(ci, (1,)).astype(jnp.int32)

    sh_in = D_IN // 4
    shard_in = jnp.pad(w_in[0].T.astype(BF16), ((0, D_IN_PAD // 4 - sh_in), (0, 0)))
    shard_w1, shard_wout, shard_w2 = w_ff1[0].astype(BF16), w_out[0].astype(BF16), w_ff2[0].astype(BF16)
    small_w = jnp.concatenate([_rows128(meta_tokens), _rows128(conv_w[0]), _rows128(gate_up[0])], axis=0)
    ln_in_g2, ln_in_b2 = ln_in_g.reshape(1, d), ln_in_b.reshape(1, d)

    s0, s0b, ((g_int,), (g_small,)) = _ln_in_fwd(
        x, ln_in_g2, ln_in_b2, [("gather", [(shard_in, 0, D_IN_PAD // 4, None)]), ("small", [small_w])])
    (g_int,) = _place_own([g_int], [shard_in])
    (g_small,) = _place_own([g_small], [small_w])
    n_meta_rows, n_cw_rows = N_META * 256 // 128, CONV_WIDTH * 128 // 128
    meta_full = jnp.concatenate([g_small[j, :n_meta_rows].reshape(N_META, 256) for j in range(4)], axis=1)
    convw_full = jnp.concatenate(
        [g_small[j, n_meta_rows:n_meta_rows + n_cw_rows].reshape(CONV_WIDTH, 128) for j in range(4)], axis=1)
    gup_full = jnp.concatenate(
        [g_small[j, n_meta_rows + n_cw_rows:].reshape(GLA_RANK, 64) for j in range(4)], axis=1)
    convw_p = jnp.pad(convw_full, ((0, 1), (0, 0)))
    gup_p = jnp.pad(gup_full, ((0, 128 - GLA_RANK), (0, 0))).astype(BF16)
    s0, s0b = _ln_meta(s0, s0b, meta_full, ln_in_g2, ln_in_b2)
    w_int = jnp.pad(g_int[:, :sh_in].reshape(D_IN, d), ((0, D_IN_PAD - D_IN), (0, 0)))
    s0f, s0bf = s0.reshape(r, d), s0b.reshape(r, d)
    u, ((w1_buf,),) = _in_proj(s0bf, w_int, [("gather", [(shard_w1, 0, 640, None)])])
    (w1_buf,) = _place_own([w1_buf], [shard_w1])
    u3 = u.reshape(bsz, tp, D_IN_PAD)
    hc, ((w1_buf,),) = _conv_fwd(u3, convw_p, conv_b, [("gather", [(shard_w1, 640, 384, w1_buf)])])
    hc = hc.reshape(r, D_CONV)
    gla_out, o_pre, states, ((g_wout,),) = _gla_fwd(
        u3, gup_p, gate_bias, gla_norm_g, [("gather", [(shard_wout, 0, 256, None)])])
    (g_wout,) = _place_own([g_wout], [shard_wout])
    wout = g_wout.reshape(d, d)
    gla_of = gla_out.reshape(r, 512)
    conv_of, xhat1, rstd1, s1b, ((w2_buf,),) = _out_proj_ln1(
        hc, gla_of, wout, s0f, conv_ln_g, conv_ln_b, ln1_g, ln1_b, [("gather", [(shard_w2, 0, 448, None)])])
    (w2_buf,) = _place_own([w2_buf], [shard_w2])
    w1 = w1_buf
    ra, ((w2_buf,),) = _ffn1(s1b, w1, [("gather", [(shard_w2, 448, 576, w2_buf)])])
    w2 = w2_buf.reshape(D_FF, d)
    dt2, dt2b, acc2 = _ffn2_ln2_loss(ra, w2, xhat1, ln1_g, ln1_b, ln2_g, ln2_b, loss_target, tp)

    def add_pair(g, got, splits=None):
        return _add_pair(g, got, c_arr, splits)

    da = _ffn_bwd_da(dt2b, w2, ra)
    dt1, dt1b, acc1 = _ffn_bwd_ln1(da, w1, dt2, xhat1, rstd1, ln1_g)
    g_w2, _ = _matmul_tn(ra, dt2b, 1024, square_lhs=True, name="grad_w_ff2")
    big_w2 = g_w2.reshape(4, D_FF // 4, d)
    big_w1, ((pair_w2,),) = _grad_w_ff1(s1b, da, [("pair", [big_w2])])
    dhc, dgla, cacc, ((pair_w1,),) = _out_proj_bwd(dt1b, wout, hc, conv_ln_g, conv_ln_b, [("pair", [big_w1])])
    (part_w1,), (part_w2a, part_w2b) = add_pair(big_w1, pair_w1), add_pair(big_w2, pair_w2, [512, 512])
    g_wout, _ = _grad_w_out(conv_of, gla_of, dt1b)
    big_wout = g_wout.reshape(4, d // 4, d)
    dcv, dcg, dcw, ((chip_w1,), (pair_wout,)) = _conv_bwd(
        dhc.reshape(bsz, tp, D_CONV), u3, convw_p, [("exchange", [part_w1]), ("pair", [big_wout])])
    (part_wout,) = add_pair(big_wout, pair_wout)
    (dq, dk, dv, dr, dgd, dgup, gvec), ((chip_w2a,),) = _gla_bwd(
        dgla.reshape(bsz, tp, 512), u3, o_pre, states, gup_p, gate_bias, gla_norm_g, [("exchange", [part_w2a])])
    pieces = [a.reshape(r, a.shape[-1]) for a in (dcv, dcg, dq, dk, dv, dr)]
    g_wint, ((chip_w2b, chip_wout),) = _grad_w_in(
        pieces, dgd, s0bf, tp, [("exchange", [part_w2b, part_wout])])
    big_win = jnp.stack([g_wint[j * sh_in:(j + 1) * sh_in] for j in range(4)])
    (half_w1, half_w2), ((pair_win,),) = _add_chips_many(
        [(big_w1, pair_w1, [chip_w1]), (big_w2, pair_w2, [chip_w2a, chip_w2b])], [("pair", [big_win])])
    ds0, ((chip_win,), (sib_w1, sib_w2)) = _in_proj_bwd(
        pieces, dgd, w_int, dt1, tp, [("exchange", add_pair(big_win, pair_win)), ("swap", [half_w1, half_w2])])
    grad_x, dmeta, lvec = _ln_in_bwd(ds0.reshape(bsz, tp, d), x, meta_full, ln_in_g2)

    small_pack = _pack_small(acc2, acc1, lvec, cacc, gvec, dcw, dgup, dmeta)
    (half_win, half_wout), ((small_all,),) = _add_chips_many(
        [(big_win, pair_win, [chip_win]), (big_wout, pair_wout, [chip_wout])], [("all", [small_pack])],
        late_relay=True)
    red, loss = _sum_small(small_all)
    sib_win, sib_wout = _pair_swap([half_win, half_wout])
    halves = {"w_in": (half_win, sib_win), "w_out": (half_wout, sib_wout), "w_ff1": (half_w1, sib_w1),
              "w_ff2": (half_w2, sib_w2)}

    grads = {}
    weights = dict(meta_tokens=meta_tokens, ln_in_g=ln_in_g, ln_in_b=ln_in_b, w_in=w_in, conv_w=conv_w, conv_b=conv_b,
                   conv_ln_g=conv_ln_g, conv_ln_b=conv_ln_b, gate_up=gate_up, gate_bias=gate_bias,
                   gla_norm_g=gla_norm_g, w_out=w_out, ln1_g=ln1_g, ln1_b=ln1_b, w_ff1=w_ff1, w_ff2=w_ff2,
                   ln2_g=ln2_g, ln2_b=ln2_b)
    moms = dict(meta_tokens=(m_meta_tokens, v_meta_tokens), ln_in_g=(m_ln_in_g, v_ln_in_g),
                ln_in_b=(m_ln_in_b, v_ln_in_b), w_in=(m_w_in, v_w_in), conv_w=(m_conv_w, v_conv_w),
                conv_b=(m_conv_b, v_conv_b), conv_ln_g=(m_conv_ln_g, v_conv_ln_g),
                conv_ln_b=(m_conv_ln_b, v_conv_ln_b), gate_up=(m_gate_up, v_gate_up),
                gate_bias=(m_gate_bias, v_gate_bias), gla_norm_g=(m_gla_norm_g, v_gla_norm_g),
                w_out=(m_w_out, v_w_out), ln1_g=(m_ln1_g, v_ln1_g), ln1_b=(m_ln1_b, v_ln1_b),
                w_ff1=(m_w_ff1, v_w_ff1), w_ff2=(m_w_ff2, v_w_ff2), ln2_g=(m_ln2_g, v_ln2_g),
                ln2_b=(m_ln2_b, v_ln2_b))
    names = list(weights)
    big_names = ("w_in", "w_out", "w_ff1", "w_ff2")
    delta, new_m, new_v = {}, {}, {}
    for k in big_names:
        to2d = (lambda a: a[0].T) if k == "w_in" else (lambda a: a[0])
        back = (lambda a: a.T[None]) if k == "w_in" else (lambda a: a[None])
        res = _adamw_halves(to2d(weights[k]), *halves[k], to2d(moms[k][0]), to2d(moms[k][1]), c_arr)
        grads[k], delta[k], new_m[k], new_v[k] = [back(a) for a in res]
    small_names = [k for k in names if k not in big_names]
    two = lambda a: a.reshape(-1, a.shape[-1])

    def my_cols(full, rows, width):
        return lax.dynamic_slice(full, (0, chip * width), (rows, width))

    sharded = {
        "meta_tokens": my_cols(red[SMALL_AT["meta_tokens"][0]:SMALL_AT["meta_tokens"][0] + N_META], N_META, 256),
        "conv_w": my_cols(red[SMALL_CONV_W:SMALL_CONV_W + 16].reshape(32, D_CONV), CONV_WIDTH, 128),
        "gate_up": my_cols(red[SMALL_GATE_UP:SMALL_GATE_UP + 4].reshape(GLA_RANK, 256), GLA_RANK, 64)}
    upd = _adamw_small(red, sharded, {k: (two(weights[k]), two(moms[k][0]), two(moms[k][1])) for k in small_names})
    for k in small_names:
        shp = weights[k].shape
        grads[k], delta[k], new_m[k], new_v[k] = [a.reshape(shp) for a in upd[k]]
    loss = loss.reshape(())

    return (loss, grad_x, *[grads[k] for k in names], *[delta[k] for k in names],
            *[new_m[k] for k in names], *[new_v[k] for k in names])
```
